```python
import jax, jax.numpy as jnp
from jax import lax
import numpy as np

D_MODEL = 1024
BATCH = 8
SEQ = 8192
DEPTH = 2

N_META = 16
CHUNK = 64
SUB = 16
RET_HEADS = 4
RET_DK = 128
RET_DV = 128
GLA_HEADS = 4
GLA_DK = 64
GLA_DV = 128
GLA_GATE_RANK = 16
GLA_TAU = 16.0
D_FF = 2816
CONV_W = 3
ROPE_BASE = 10000.0
EPS = 1e-6

RET_QK = RET_HEADS * RET_DK
RET_V = RET_HEADS * RET_DV
GLA_QK = GLA_HEADS * GLA_DK
GLA_V = GLA_HEADS * GLA_DV
D_MIX = RET_V + GLA_V
IN_SPLITS = (RET_QK, RET_QK, RET_V, RET_V, GLA_QK, GLA_QK, GLA_V, GLA_V, GLA_GATE_RANK)
IN_WIDTH = 2 * RET_QK + 2 * RET_V + 2 * GLA_QK + 2 * GLA_V + GLA_GATE_RANK

kernel_name = "hybrid_retention_gla_convffn"


def _rmsnorm(x, w):
    xf = x.astype(jnp.float32)
    y = xf * lax.rsqrt(jnp.mean(xf * xf, axis=-1, keepdims=True) + EPS)
    return (y * w.astype(jnp.float32)).astype(x.dtype)


def _rope(t, pos):
    half = t.shape[-1] // 2
    inv = ROPE_BASE ** (-jnp.arange(half, dtype=jnp.float32) / half)
    ang = pos[:, None] * inv[None, :]
    c = jnp.cos(ang)[None, :, None, :]
    s = jnp.sin(ang)[None, :, None, :]
    t = t.astype(jnp.float32)
    t1, t2 = t[..., :half], t[..., half:]
    return jnp.concatenate([t1 * c - t2 * s, t1 * s + t2 * c], axis=-1)


def _to_chunks(t):
    t = jnp.pad(t.astype(jnp.float32), ((0, 0), (CHUNK - N_META, 0), (0, 0), (0, 0)))
    b, lp, h, d = t.shape
    return t.reshape(b, lp // CHUNK, CHUNK, h, d).transpose(0, 3, 1, 2, 4)


def _from_chunks(o):
    b, h, n, c, d = o.shape
    o = o.transpose(0, 2, 3, 1, 4).reshape(b, n * c, h, d)
    return o[:, CHUNK - N_META:]


def _scan_states(decay, kv):
    def step(state, inp):
        dec_n, kv_n = inp
        return dec_n * state + kv_n, state
    init = jnp.zeros(kv.shape[:2] + kv.shape[3:], kv.dtype)
    _, prev = lax.scan(step, init, (jnp.moveaxis(decay, 2, 0), jnp.moveaxis(kv, 2, 0)))
    return jnp.moveaxis(prev, 0, 2)


def _retention(q, k, v):
    b, h, n, c, _ = q.shape
    log_g = jnp.log(1.0 - 2.0 ** (-5.0 - jnp.arange(h, dtype=jnp.float32)))
    idx = jnp.arange(c, dtype=jnp.float32)
    diff = idx[:, None] - idx[None, :]
    dmat = jnp.where(diff >= 0, jnp.exp(log_g[:, None, None] * jnp.maximum(diff, 0.0)), 0.0)
    k = k * (RET_DK ** -0.5)
    scores = jnp.einsum('bhncd,bhnsd->bhncs', q, k) * dmat[None, :, None]
    o_intra = jnp.einsum('bhncs,bhnsv->bhncv', scores, v)
    zeta = jnp.exp(log_g[:, None] * (c - 1.0 - idx)[None, :])
    kv = jnp.einsum('bhncd,hc,bhncv->bhndv', k, zeta, v)
    chunk_decay = jnp.broadcast_to(jnp.exp(log_g * c)[None, :, None, None, None], (b, h, n, 1, 1))
    prev = _scan_states(chunk_decay, kv)
    xi = jnp.exp(log_g[:, None] * (idx + 1.0)[None, :])
    o_inter = jnp.einsum('bhncd,bhndv->bhncv', q, prev) * xi[None, :, None, :, None]
    return o_intra + o_inter


def _gla(q, k, v, log_a):
    b, h, n, c, dk = q.shape
    dv = v.shape[-1]
    ns = c // SUB
    q = q * (GLA_DK ** -0.5)
    cum = jnp.cumsum(log_a, axis=3)
    last = cum[:, :, :, -1:, :]
    kv = jnp.einsum('bhncd,bhncv->bhndv', k * jnp.exp(last - cum), v)
    prev = _scan_states(jnp.exp(last[:, :, :, 0, :])[..., None], kv)
    o_inter = jnp.einsum('bhncd,bhndv->bhncv', q * jnp.exp(cum), prev)
    qs = q.reshape(b, h, n, ns, SUB, dk)
    ks = k.reshape(b, h, n, ns, SUB, dk)
    vs = v.reshape(b, h, n, ns, SUB, dv)
    cs = cum.reshape(b, h, n, ns, SUB, dk)
    ref = jnp.concatenate([jnp.zeros((b, h, n, 1, dk), cum.dtype), cum[:, :, :, SUB - 1:c - 1:SUB, :]], axis=3)
    q_hat = qs * jnp.exp(cs - ref[:, :, :, :, None, :])
    k_hat = k[:, :, :, None] * jnp.exp(jnp.minimum(ref[:, :, :, :, None, :] - cum[:, :, :, None], 0.0))
    off = jnp.einsum('bhnasd,bhnacd->bhnasc', q_hat, k_hat)
    off_mask = jnp.arange(c)[None, :] < (jnp.arange(ns) * SUB)[:, None]
    off = jnp.where(off_mask[:, None, :], off, 0.0)
    o_off = jnp.einsum('bhnasc,bhncv->bhnasv', off, v)
    causal = jnp.tril(jnp.ones((SUB, SUB), dtype=bool))
    ddiff = cs[..., :, None, :] - cs[..., None, :, :]
    dec = jnp.exp(jnp.where(causal[:, :, None], ddiff, -jnp.inf))
    diag = jnp.einsum('bhnasd,bhnatd,bhnastd->bhnast', qs, ks, dec)
    o_diag = jnp.einsum('bhnast,bhnatv->bhnasv', diag, vs)
    return o_inter + (o_off + o_diag).reshape(b, h, n, c, dv)


def _head_group_norm(o, w):
    mu = jnp.mean(o, axis=-1, keepdims=True)
    var = jnp.mean(jnp.square(o - mu), axis=-1, keepdims=True)
    y = (o - mu) * lax.rsqrt(var + EPS)
    return y.reshape(o.shape[0], o.shape[1], -1) * w.astype(jnp.float32)


def _head_rms_norm(o, w):
    y = o * lax.rsqrt(jnp.mean(o * o, axis=-1, keepdims=True) + EPS)
    return y.reshape(o.shape[0], o.shape[1], -1) * w.astype(jnp.float32)


def _mixer(h, pos, w_in, gla_gate_w2, gla_gate_b, ret_norm_w, gla_norm_w, w_out):
    bsz, length, _ = h.shape
    proj = h @ w_in
    offs = np.cumsum(np.array(IN_SPLITS))[:-1].tolist()
    rq, rk, rv, rg, gq, gk, gv, gr, ga = jnp.split(proj, offs, axis=-1)
    rq = _rope(rq.reshape(bsz, length, RET_HEADS, RET_DK), pos)
    rk = _rope(rk.reshape(bsz, length, RET_HEADS, RET_DK), pos)
    rv = rv.reshape(bsz, length, RET_HEADS, RET_DV)
    o_ret = _from_chunks(_retention(_to_chunks(rq), _to_chunks(rk), _to_chunks(rv)))
    o_ret = _head_group_norm(o_ret, ret_norm_w) * jax.nn.silu(rg.astype(jnp.float32))
    z = (ga @ gla_gate_w2 + gla_gate_b).astype(jnp.float32)
    log_a = (jax.nn.log_sigmoid(z) / GLA_TAU).reshape(bsz, length, GLA_HEADS, GLA_DK)
    gq = gq.reshape(bsz, length, GLA_HEADS, GLA_DK)
    gk = gk.reshape(bsz, length, GLA_HEADS, GLA_DK)
    gv = gv.reshape(bsz, length, GLA_HEADS, GLA_DV)
    o_gla = _from_chunks(_gla(_to_chunks(gq), _to_chunks(gk), _to_chunks(gv), _to_chunks(log_a)))
    o_gla = _head_rms_norm(o_gla, gla_norm_w) * jax.nn.silu(gr.astype(jnp.float32))
    merged = jnp.concatenate([o_ret, o_gla], axis=-1).astype(h.dtype)
    return merged @ w_out


def _conv_ffn(h, ffn_up, ffn_conv_w, ffn_conv_b, ffn_down):
    length = h.shape[1]
    u = h @ ffn_up
    up = jnp.pad(u, ((0, 0), (CONV_W - 1, 0), (0, 0)))
    conv = ffn_conv_b + sum(up[:, i:i + length] * ffn_conv_w[i] for i in range(CONV_W))
    a, g = jnp.split(conv, 2, axis=-1)
    return (jax.nn.gelu(a, approximate=True) * g) @ ffn_down


def _fwd_setup_inputs(seed: int = 0) -> dict:
    key = jax.random.key(seed)
    ks = jax.random.split(key, 18)
    nrm = lambda k, shape, s: jax.random.normal(k, shape, jnp.float32) * s
    gain = lambda k, shape: 1.0 + 0.02 * jax.random.normal(k, shape, jnp.float32)
    return {
        "x": nrm(ks[0], (BATCH, SEQ, D_MODEL), 1.0),
        "meta_tokens": nrm(ks[1], (N_META, D_MODEL), 1.0),
        "pre_mix_norm": gain(ks[2], (DEPTH, D_MODEL)),
        "w_in": nrm(ks[3], (DEPTH, D_MODEL, IN_WIDTH), D_MODEL ** -0.5),
        "gla_gate_w2": nrm(ks[4], (DEPTH, GLA_GATE_RANK, GLA_QK), GLA_GATE_RANK ** -0.5),
        "gla_gate_b": nrm(ks[5], (DEPTH, GLA_QK), 0.1),
        "ret_norm_w": gain(ks[6], (DEPTH, RET_V)),
        "gla_norm_w": gain(ks[7], (DEPTH, GLA_V)),
        "w_out": nrm(ks[8], (DEPTH, D_MIX, D_MODEL), D_MIX ** -0.5),
        "post_mix_norm": gain(ks[9], (DEPTH, D_MODEL)),
        "pre_ffn_norm": gain(ks[10], (DEPTH, D_MODEL)),
        "ffn_up": nrm(ks[11], (DEPTH, D_MODEL, 2 * D_FF), D_MODEL ** -0.5),
        "ffn_conv_w": nrm(ks[12], (DEPTH, CONV_W, 2 * D_FF), CONV_W ** -0.5),
        "ffn_conv_b": nrm(ks[13], (DEPTH, 2 * D_FF), 0.02),
        "ffn_down": nrm(ks[14], (DEPTH, D_FF, D_MODEL), D_FF ** -0.5),
        "post_ffn_norm": gain(ks[15], (DEPTH, D_MODEL)),
    }


def _fwd_reference(x, meta_tokens, pre_mix_norm, w_in, gla_gate_w2, gla_gate_b, ret_norm_w, gla_norm_w,
              w_out, post_mix_norm, pre_ffn_norm, ffn_up, ffn_conv_w, ffn_conv_b, ffn_down, post_ffn_norm):
    bsz = x.shape[0]
    meta = jnp.broadcast_to(meta_tokens.astype(x.dtype)[None], (bsz, N_META, x.shape[-1]))
    h = jnp.concatenate([meta, x], axis=1)
    pos = jnp.arange(h.shape[1], dtype=jnp.float32)
    for i in range(DEPTH):
        m = _mixer(_rmsnorm(h, pre_mix_norm[i]), pos, w_in[i], gla_gate_w2[i], gla_gate_b[i],
                   ret_norm_w[i], gla_norm_w[i], w_out[i])
        h = h + _rmsnorm(m, post_mix_norm[i])
        f = _conv_ffn(_rmsnorm(h, pre_ffn_norm[i]), ffn_up[i], ffn_conv_w[i], ffn_conv_b[i], ffn_down[i])
        h = h + _rmsnorm(f, post_ffn_norm[i])
    return h[:, N_META:]


import jax as _jax
import jax.numpy as _jnp

TWIN_FORMAT = 'train_step'
FWD_PARAMS = ['x', 'meta_tokens', 'pre_mix_norm', 'w_in', 'gla_gate_w2', 'gla_gate_b', 'ret_norm_w', 'gla_norm_w', 'w_out', 'post_mix_norm', 'pre_ffn_norm', 'ffn_up', 'ffn_conv_w', 'ffn_conv_b', 'ffn_down', 'post_ffn_norm']
TWIN_WEIGHTS = ['meta_tokens', 'pre_mix_norm', 'w_in', 'gla_gate_w2', 'gla_gate_b', 'ret_norm_w', 'gla_norm_w', 'w_out', 'post_mix_norm', 'pre_ffn_norm', 'ffn_up', 'ffn_conv_w', 'ffn_conv_b', 'ffn_down', 'post_ffn_norm']
TWIN_DIFF_INPUT = 'x'
TWIN_INPUTS = ['x', 'meta_tokens', 'pre_mix_norm', 'w_in', 'gla_gate_w2', 'gla_gate_b', 'ret_norm_w', 'gla_norm_w', 'w_out', 'post_mix_norm', 'pre_ffn_norm', 'ffn_up', 'ffn_conv_w', 'ffn_conv_b', 'ffn_down', 'post_ffn_norm', 'loss_target', 'm_meta_tokens', 'm_pre_mix_norm', 'm_w_in', 'm_gla_gate_w2', 'm_gla_gate_b', 'm_ret_norm_w', 'm_gla_norm_w', 'm_w_out', 'm_post_mix_norm', 'm_pre_ffn_norm', 'm_ffn_up', 'm_ffn_conv_w', 'm_ffn_conv_b', 'm_ffn_down', 'm_post_ffn_norm', 'v_meta_tokens', 'v_pre_mix_norm', 'v_w_in', 'v_gla_gate_w2', 'v_gla_gate_b', 'v_ret_norm_w', 'v_gla_norm_w', 'v_w_out', 'v_post_mix_norm', 'v_pre_ffn_norm', 'v_ffn_up', 'v_ffn_conv_w', 'v_ffn_conv_b', 'v_ffn_down', 'v_post_ffn_norm']
TWIN_OUTPUTS = ['loss', 'grad_x', 'grad_meta_tokens', 'grad_pre_mix_norm', 'grad_w_in', 'grad_gla_gate_w2', 'grad_gla_gate_b', 'grad_ret_norm_w', 'grad_gla_norm_w', 'grad_w_out', 'grad_post_mix_norm', 'grad_pre_ffn_norm', 'grad_ffn_up', 'grad_ffn_conv_w', 'grad_ffn_conv_b', 'grad_ffn_down', 'grad_post_ffn_norm', 'delta_meta_tokens', 'delta_pre_mix_norm', 'delta_w_in', 'delta_gla_gate_w2', 'delta_gla_gate_b', 'delta_ret_norm_w', 'delta_gla_norm_w', 'delta_w_out', 'delta_post_mix_norm', 'delta_pre_ffn_norm', 'delta_ffn_up', 'delta_ffn_conv_w', 'delta_ffn_conv_b', 'delta_ffn_down', 'delta_post_ffn_norm', 'new_m_meta_tokens', 'new_m_pre_mix_norm', 'new_m_w_in', 'new_m_gla_gate_w2', 'new_m_gla_gate_b', 'new_m_ret_norm_w', 'new_m_gla_norm_w', 'new_m_w_out', 'new_m_post_mix_norm', 'new_m_pre_ffn_norm', 'new_m_ffn_up', 'new_m_ffn_conv_w', 'new_m_ffn_conv_b', 'new_m_ffn_down', 'new_m_post_ffn_norm', 'new_v_meta_tokens', 'new_v_pre_mix_norm', 'new_v_w_in', 'new_v_gla_gate_w2', 'new_v_gla_gate_b', 'new_v_ret_norm_w', 'new_v_gla_norm_w', 'new_v_w_out', 'new_v_post_mix_norm', 'new_v_pre_ffn_norm', 'new_v_ffn_up', 'new_v_ffn_conv_w', 'new_v_ffn_conv_b', 'new_v_ffn_down', 'new_v_post_ffn_norm']
TWIN_LEAF_KINDS = {'loss': 'loss', 'grad_x': 'grad_x', 'grad_meta_tokens': 'grad_w', 'grad_pre_mix_norm': 'grad_w', 'grad_w_in': 'grad_w', 'grad_gla_gate_w2': 'grad_w', 'grad_gla_gate_b': 'grad_w', 'grad_ret_norm_w': 'grad_w', 'grad_gla_norm_w': 'grad_w', 'grad_w_out': 'grad_w', 'grad_post_mix_norm': 'grad_w', 'grad_pre_ffn_norm': 'grad_w', 'grad_ffn_up': 'grad_w', 'grad_ffn_conv_w': 'grad_w', 'grad_ffn_conv_b': 'grad_w', 'grad_ffn_down': 'grad_w', 'grad_post_ffn_norm': 'grad_w', 'delta_meta_tokens': 'delta_w', 'delta_pre_mix_norm': 'delta_w', 'delta_w_in': 'delta_w', 'delta_gla_gate_w2': 'delta_w', 'delta_gla_gate_b': 'delta_w', 'delta_ret_norm_w': 'delta_w', 'delta_gla_norm_w': 'delta_w', 'delta_w_out': 'delta_w', 'delta_post_mix_norm': 'delta_w', 'delta_pre_ffn_norm': 'delta_w', 'delta_ffn_up': 'delta_w', 'delta_ffn_conv_w': 'delta_w', 'delta_ffn_conv_b': 'delta_w', 'delta_ffn_down': 'delta_w', 'delta_post_ffn_norm': 'delta_w', 'new_m_meta_tokens': 'new_m', 'new_m_pre_mix_norm': 'new_m', 'new_m_w_in': 'new_m', 'new_m_gla_gate_w2': 'new_m', 'new_m_gla_gate_b': 'new_m', 'new_m_ret_norm_w': 'new_m', 'new_m_gla_norm_w': 'new_m', 'new_m_w_out': 'new_m', 'new_m_post_mix_norm': 'new_m', 'new_m_pre_ffn_norm': 'new_m', 'new_m_ffn_up': 'new_m', 'new_m_ffn_conv_w': 'new_m', 'new_m_ffn_conv_b': 'new_m', 'new_m_ffn_down': 'new_m', 'new_m_post_ffn_norm': 'new_m', 'new_v_meta_tokens': 'new_v', 'new_v_pre_mix_norm': 'new_v', 'new_v_w_in': 'new_v', 'new_v_gla_gate_w2': 'new_v', 'new_v_gla_gate_b': 'new_v', 'new_v_ret_norm_w': 'new_v', 'new_v_gla_norm_w': 'new_v', 'new_v_w_out': 'new_v', 'new_v_post_mix_norm': 'new_v', 'new_v_pre_ffn_norm': 'new_v', 'new_v_ffn_up': 'new_v', 'new_v_ffn_conv_w': 'new_v', 'new_v_ffn_conv_b': 'new_v', 'new_v_ffn_down': 'new_v', 'new_v_post_ffn_norm': 'new_v'}


def _forward(args):
    return _fwd_reference(*[args[k] for k in FWD_PARAMS])


def _output_shape():
    def fwd():
        inp = _fwd_setup_inputs(0)
        return _fwd_reference(*[inp[k] for k in FWD_PARAMS])
    out = _jax.eval_shape(fwd)
    return out.shape, out.dtype

N_MICROBATCH = 1
ADAM_LR = 0.001
ADAM_B1 = 0.9
ADAM_B2 = 0.999
ADAM_EPS = 1e-08
ADAM_WD = 0.01
ADAM_STEP = 10
PER_EXAMPLE_BATCH_AXIS = {'x': 0, 'loss_target': 0}
SHARED_INPUTS = []
_WEIGHT_DTYPES = {'meta_tokens': _jnp.float32, 'pre_mix_norm': _jnp.float32, 'w_in': _jnp.float32, 'gla_gate_w2': _jnp.float32, 'gla_gate_b': _jnp.float32, 'ret_norm_w': _jnp.float32, 'gla_norm_w': _jnp.float32, 'w_out': _jnp.float32, 'post_mix_norm': _jnp.float32, 'pre_ffn_norm': _jnp.float32, 'ffn_up': _jnp.float32, 'ffn_conv_w': _jnp.float32, 'ffn_conv_b': _jnp.float32, 'ffn_down': _jnp.float32, 'post_ffn_norm': _jnp.float32}
MOMENT_SCALE = {'meta_tokens': 1.902319e-01, 'pre_mix_norm': 2.439576e+00, 'w_in': 1.293016e+00, 'gla_gate_w2': 1.911666e-01, 'gla_gate_b': 8.133843e-01, 'ret_norm_w': 1.348706e+00, 'gla_norm_w': 1.206587e+00, 'w_out': 1.221002e+00, 'post_mix_norm': 6.378869e+01, 'pre_ffn_norm': 1.438800e+00, 'ffn_up': 5.670778e-01, 'ffn_conv_w': 5.948071e-01, 'ffn_conv_b': 7.998069e-01, 'ffn_down': 1.029502e+00, 'post_ffn_norm': 6.387481e+01}


def _to_microbatches(a, axis):
    t = _jnp.moveaxis(a, axis, 0)
    t = t.reshape((N_MICROBATCH, t.shape[0] // N_MICROBATCH) + t.shape[1:])
    return _jnp.moveaxis(t, 1, axis + 1)


def setup_inputs(seed: int = 0) -> dict:
    inp = _fwd_setup_inputs(seed)
    key = _jax.random.fold_in(_jax.random.key(seed), 7919)
    shape, _ = _output_shape()
    out = dict(inp)
    out["loss_target"] = _jax.random.normal(_jax.random.fold_in(key, 0), shape, _jnp.float32)
    for i, name in enumerate(TWIN_WEIGHTS):
        w = inp[name].astype(_jnp.float32)
        if MOMENT_SCALE is None:
            s = _jnp.sqrt(_jnp.mean(_jnp.square(w)) + 1e-30)
        else:
            s = MOMENT_SCALE[name]
        km, kv = _jax.random.split(_jax.random.fold_in(key, i + 1))
        out[name] = w
        out["m_" + name] = s * _jax.random.normal(km, w.shape, _jnp.float32)
        out["v_" + name] = (s * s) * _jax.random.uniform(kv, w.shape, _jnp.float32, 0.5, 1.5)
    if N_MICROBATCH > 1:
        for name, axis in PER_EXAMPLE_BATCH_AXIS.items():
            out[name] = _to_microbatches(out[name], axis)
    return {'x': out['x'], 'meta_tokens': out['meta_tokens'], 'pre_mix_norm': out['pre_mix_norm'], 'w_in': out['w_in'], 'gla_gate_w2': out['gla_gate_w2'], 'gla_gate_b': out['gla_gate_b'], 'ret_norm_w': out['ret_norm_w'], 'gla_norm_w': out['gla_norm_w'], 'w_out': out['w_out'], 'post_mix_norm': out['post_mix_norm'], 'pre_ffn_norm': out['pre_ffn_norm'], 'ffn_up': out['ffn_up'], 'ffn_conv_w': out['ffn_conv_w'], 'ffn_conv_b': out['ffn_conv_b'], 'ffn_down': out['ffn_down'], 'post_ffn_norm': out['post_ffn_norm'], 'loss_target': out['loss_target'], 'm_meta_tokens': out['m_meta_tokens'], 'm_pre_mix_norm': out['m_pre_mix_norm'], 'm_w_in': out['m_w_in'], 'm_gla_gate_w2': out['m_gla_gate_w2'], 'm_gla_gate_b': out['m_gla_gate_b'], 'm_ret_norm_w': out['m_ret_norm_w'], 'm_gla_norm_w': out['m_gla_norm_w'], 'm_w_out': out['m_w_out'], 'm_post_mix_norm': out['m_post_mix_norm'], 'm_pre_ffn_norm': out['m_pre_ffn_norm'], 'm_ffn_up': out['m_ffn_up'], 'm_ffn_conv_w': out['m_ffn_conv_w'], 'm_ffn_conv_b': out['m_ffn_conv_b'], 'm_ffn_down': out['m_ffn_down'], 'm_post_ffn_norm': out['m_post_ffn_norm'], 'v_meta_tokens': out['v_meta_tokens'], 'v_pre_mix_norm': out['v_pre_mix_norm'], 'v_w_in': out['v_w_in'], 'v_gla_gate_w2': out['v_gla_gate_w2'], 'v_gla_gate_b': out['v_gla_gate_b'], 'v_ret_norm_w': out['v_ret_norm_w'], 'v_gla_norm_w': out['v_gla_norm_w'], 'v_w_out': out['v_w_out'], 'v_post_mix_norm': out['v_post_mix_norm'], 'v_pre_ffn_norm': out['v_pre_ffn_norm'], 'v_ffn_up': out['v_ffn_up'], 'v_ffn_conv_w': out['v_ffn_conv_w'], 'v_ffn_conv_b': out['v_ffn_conv_b'], 'v_ffn_down': out['v_ffn_down'], 'v_post_ffn_norm': out['v_post_ffn_norm']}


def _loss(weights, diff, rest, loss_target):
    with _jax.named_scope("forward"):
        args = {**rest, TWIN_DIFF_INPUT: diff, **{k: w.astype(_WEIGHT_DTYPES[k]) for k, w in weights.items()}}
        y = _forward(args)
    with _jax.named_scope("loss_head"):
        err = _jnp.square(y.astype(_jnp.float32) - loss_target)
        return 0.5 * _jnp.sum(_jnp.mean(err, axis=-1)) if err.ndim else 0.5 * err


def _adamw(w, g, m, v):
    m = ADAM_B1 * m + (1.0 - ADAM_B1) * g
    v = ADAM_B2 * v + (1.0 - ADAM_B2) * _jnp.square(g)
    m_hat = m / (1.0 - ADAM_B1 ** ADAM_STEP)
    v_hat = v / (1.0 - ADAM_B2 ** ADAM_STEP)
    delta = -ADAM_LR * (m_hat / (_jnp.sqrt(v_hat) + ADAM_EPS) + ADAM_WD * w)
    return delta, m, v


def reference(x, meta_tokens, pre_mix_norm, w_in, gla_gate_w2, gla_gate_b, ret_norm_w, gla_norm_w, w_out, post_mix_norm, pre_ffn_norm, ffn_up, ffn_conv_w, ffn_conv_b, ffn_down, post_ffn_norm, loss_target, m_meta_tokens, m_pre_mix_norm, m_w_in, m_gla_gate_w2, m_gla_gate_b, m_ret_norm_w, m_gla_norm_w, m_w_out, m_post_mix_norm, m_pre_ffn_norm, m_ffn_up, m_ffn_conv_w, m_ffn_conv_b, m_ffn_down, m_post_ffn_norm, v_meta_tokens, v_pre_mix_norm, v_w_in, v_gla_gate_w2, v_gla_gate_b, v_ret_norm_w, v_gla_norm_w, v_w_out, v_post_mix_norm, v_pre_ffn_norm, v_ffn_up, v_ffn_conv_w, v_ffn_conv_b, v_ffn_down, v_post_ffn_norm):
    given = dict(x=x, meta_tokens=meta_tokens, pre_mix_norm=pre_mix_norm, w_in=w_in, gla_gate_w2=gla_gate_w2, gla_gate_b=gla_gate_b, ret_norm_w=ret_norm_w, gla_norm_w=gla_norm_w, w_out=w_out, post_mix_norm=post_mix_norm, pre_ffn_norm=pre_ffn_norm, ffn_up=ffn_up, ffn_conv_w=ffn_conv_w, ffn_conv_b=ffn_conv_b, ffn_down=ffn_down, post_ffn_norm=post_ffn_norm, loss_target=loss_target, m_meta_tokens=m_meta_tokens, m_pre_mix_norm=m_pre_mix_norm, m_w_in=m_w_in, m_gla_gate_w2=m_gla_gate_w2, m_gla_gate_b=m_gla_gate_b, m_ret_norm_w=m_ret_norm_w, m_gla_norm_w=m_gla_norm_w, m_w_out=m_w_out, m_post_mix_norm=m_post_mix_norm, m_pre_ffn_norm=m_pre_ffn_norm, m_ffn_up=m_ffn_up, m_ffn_conv_w=m_ffn_conv_w, m_ffn_conv_b=m_ffn_conv_b, m_ffn_down=m_ffn_down, m_post_ffn_norm=m_post_ffn_norm, v_meta_tokens=v_meta_tokens, v_pre_mix_norm=v_pre_mix_norm, v_w_in=v_w_in, v_gla_gate_w2=v_gla_gate_w2, v_gla_gate_b=v_gla_gate_b, v_ret_norm_w=v_ret_norm_w, v_gla_norm_w=v_gla_norm_w, v_w_out=v_w_out, v_post_mix_norm=v_post_mix_norm, v_pre_ffn_norm=v_pre_ffn_norm, v_ffn_up=v_ffn_up, v_ffn_conv_w=v_ffn_conv_w, v_ffn_conv_b=v_ffn_conv_b, v_ffn_down=v_ffn_down, v_post_ffn_norm=v_post_ffn_norm)
    weights = {n: given[n] for n in TWIN_WEIGHTS}
    shared = {n: given[n] for n in SHARED_INPUTS}
    per_example = {n: given[n] for n in ['x']}
    grad_fn = _jax.value_and_grad(_loss, argnums=(0, 1))

    def one_microbatch(ex, loss_target):
        ex = dict(ex)
        diff = ex.pop(TWIN_DIFF_INPUT)
        return grad_fn(weights, diff, {**shared, **ex}, loss_target)

    if N_MICROBATCH == 1:
        loss, (grad_w, grad_x) = one_microbatch(per_example, given["loss_target"])
    else:
        def body(carry, xs):
            loss_sum, grad_sum = carry
            l_k, (gw_k, gx_k) = one_microbatch(xs[0], xs[1])
            with _jax.named_scope("update"):
                return (loss_sum + l_k, _jax.tree.map(_jnp.add, grad_sum, gw_k)), gx_k

        init = (_jnp.zeros((), _jnp.float32), _jax.tree.map(_jnp.zeros_like, weights))
        (loss, grad_w), grad_x = _jax.lax.scan(body, init, (per_example, given["loss_target"]))
    with _jax.named_scope("update"):
        delta_w, new_m, new_v = {}, {}, {}
        for n in TWIN_WEIGHTS:
            delta_w[n], new_m[n], new_v[n] = _adamw(weights[n], grad_w[n], given["m_" + n], given["v_" + n])
    return (loss, grad_x, *[grad_w[n] for n in TWIN_WEIGHTS], *[delta_w[n] for n in TWIN_WEIGHTS],
            *[new_m[n] for n in TWIN_WEIGHTS], *[new_v[n] for n in TWIN_WEIGHTS])
```

```python
import functools
import math

import numpy as np
import jax
import jax.numpy as jnp
from jax import lax
from jax.experimental import pallas as pl
from jax.experimental.pallas import tpu as pltpu

F32 = jnp.float32
BF16 = jnp.bfloat16

D_MODEL = 1024
DEPTH = 2
N_META = 16
EPS = 1e-6
RET_HEADS = 4
RET_DK = 128
GLA_HEADS = 4
GLA_DK = 64
GLA_DV = 128
GLA_QK = GLA_HEADS * GLA_DK
GLA_V = GLA_HEADS * GLA_DV
GLA_RANK = 16
GLA_TAU = 16.0
D_FF = 2816
ROPE_BASE = 10000.0
IN_WIDTH = 3600
IN_PAD = 3840
C_RQ, C_RK, C_RV, C_RG, C_GQ, C_GK, C_GV, C_GR, C_GA = 0, 512, 1024, 1536, 2048, 2304, 2560, 3072, 3584

FRONT = 64
BACK = 64
PADF = FRONT - N_META
RET_CHUNK = 128
GLA_CHUNK = 64
GLA_SUB = 16
BLK = 640

ADAM_LR, ADAM_B1, ADAM_B2, ADAM_EPS, ADAM_WD, ADAM_STEP = 0.001, 0.9, 0.999, 1e-08, 0.01, 10

VMEM_LIMIT = 56 * 2 ** 20
MESH = pl.DeviceIdType.MESH


def _cp(*sem):
    return pltpu.CompilerParams(dimension_semantics=sem, vmem_limit_bytes=VMEM_LIMIT)


def _tile(n, cands):
    for t in cands:
        if n % t == 0:
            return t
    raise ValueError(f"no tile for {n} in {cands}")


def _row_tile(n):
    return _tile(n, (640, 512, 320, 256, 128, 64))


def _mm(a, b, *, nt=False, add=None, out_dtype=F32, tn=None, name):
    m, k = a.shape
    n = b.shape[0] if nt else b.shape[1]
    tm = _tile(m, (320, 256, 128, 64))
    tn = n if tn is None else tn
    dn = (((1,), (1,)), ((), ())) if nt else (((1,), (0,)), ((), ()))

    def body(*refs):
        if add is None:
            a_ref, b_ref, o_ref = refs
        else:
            a_ref, b_ref, c_ref, o_ref = refs
        r = lax.dot_general(a_ref[...].astype(BF16), b_ref[...].astype(BF16), dn, preferred_element_type=F32)
        if add is not None:
            r = r + c_ref[...]
        o_ref[...] = r.astype(o_ref.dtype)

    b_spec = pl.BlockSpec((tn, k), lambda j, i: (j, 0)) if nt else pl.BlockSpec((k, tn), lambda j, i: (0, j))
    in_specs = [pl.BlockSpec((tm, k), lambda j, i: (i, 0)), b_spec]
    args = [a, b]
    if add is not None:
        in_specs.append(pl.BlockSpec((tm, tn), lambda j, i: (i, j)))
        args.append(add)
    return pl.pallas_call(
        body, out_shape=jax.ShapeDtypeStruct((m, n), out_dtype), grid=(n // tn, m // tm),
        in_specs=in_specs, out_specs=pl.BlockSpec((tm, tn), lambda j, i: (i, j)),
        compiler_params=_cp("parallel", "parallel"), name=name)(*args)


def _mm_tn(a, b, *, tn=None, name):
    m, k = a.shape
    n = b.shape[1]
    tm = _tile(m, (320, 256, 128, 64))
    tn = n if tn is None else tn

    def body(a_ref, b_ref, o_ref):
        @pl.when(pl.program_id(1) == 0)
        def _():
            o_ref[...] = jnp.zeros_like(o_ref)
        o_ref[...] += lax.dot_general(a_ref[...].astype(BF16), b_ref[...].astype(BF16),
                                      (((0,), (0,)), ((), ())), preferred_element_type=F32)

    return pl.pallas_call(
        body, out_shape=jax.ShapeDtypeStruct((k, n), F32), grid=(n // tn, m // tm),
        in_specs=[pl.BlockSpec((tm, k), lambda j, i: (i, 0)), pl.BlockSpec((tm, tn), lambda j, i: (i, j))],
        out_specs=pl.BlockSpec((k, tn), lambda j, i: (0, j)),
        compiler_params=_cp("parallel", "arbitrary"), name=name)(a, b)


def _rms(x, w):
    r = lax.rsqrt(jnp.mean(x * x, axis=-1, keepdims=True) + EPS)
    return x * r * w


def _rms_bwd(x, w, dy):
    r = lax.rsqrt(jnp.mean(x * x, axis=-1, keepdims=True) + EPS)
    xh = x * r
    dxh = dy * w
    dx = r * (dxh - xh * jnp.mean(dxh * xh, axis=-1, keepdims=True))
    return dx, jnp.sum(dy * xh, axis=0, keepdims=True)


def _resid_norm(h, t, w_post, w_next, *, name):
    lp, d = h.shape
    tm = _row_tile(lp)
    has_t = t is not None

    def body(*refs):
        if has_t:
            h_ref, t_ref, wp_ref, wn_ref, ho_ref, hn_ref = refs
            hv = h_ref[...] + _rms(t_ref[...], wp_ref[...])
            ho_ref[...] = hv
        else:
            h_ref, wn_ref, hn_ref = refs
            hv = h_ref[...]
        hn_ref[...] = _rms(hv, wn_ref[...]).astype(BF16)

    row = pl.BlockSpec((tm, d), lambda i: (i, 0))
    vec = pl.BlockSpec((1, d), lambda i: (0, 0))
    if has_t:
        return pl.pallas_call(
            body, out_shape=(jax.ShapeDtypeStruct((lp, d), F32), jax.ShapeDtypeStruct((lp, d), BF16)),
            grid=(lp // tm,), in_specs=[row, row, vec, vec], out_specs=(row, row),
            compiler_params=_cp("parallel"), name=name)(h, t, w_post, w_next)
    return h, pl.pallas_call(
        body, out_shape=jax.ShapeDtypeStruct((lp, d), BF16), grid=(lp // tm,), in_specs=[row, vec],
        out_specs=row, compiler_params=_cp("parallel"), name=name)(h, w_next)


def _resid_norm_bwd(dh_out, dhn, h_new, t, w_next, w_post, *, name):
    lp, d = h_new.shape if h_new is not None else t.shape
    tm = _row_tile(lp)
    has_n = dhn is not None
    has_t = t is not None

    def body(*refs):
        refs = list(refs)
        dho_ref = refs.pop(0)
        if has_n:
            dhn_ref, hn_ref, wn_ref = refs.pop(0), refs.pop(0), refs.pop(0)
        if has_t:
            t_ref, wp_ref = refs.pop(0), refs.pop(0)
        dh_ref = refs.pop(0) if has_n else None
        dt_ref = refs.pop(0) if has_t else None
        dwn_ref = refs.pop(0) if has_n else None
        dwp_ref = refs.pop(0) if has_t else None
        first = pl.program_id(0) == 0
        dh = dho_ref[...]
        if has_n:
            dx, dwn = _rms_bwd(hn_ref[...], wn_ref[...], dhn_ref[...])
            dh = dh + dx
            dh_ref[...] = dh

            @pl.when(first)
            def _():
                dwn_ref[...] = jnp.zeros_like(dwn_ref)
            dwn_ref[...] += dwn
        if has_t:
            dt, dwp = _rms_bwd(t_ref[...], wp_ref[...], dh)
            dt_ref[...] = dt.astype(BF16)

            @pl.when(first)
            def _():
                dwp_ref[...] = jnp.zeros_like(dwp_ref)
            dwp_ref[...] += dwp

    row = pl.BlockSpec((tm, d), lambda i: (i, 0))
    vec = pl.BlockSpec((1, d), lambda i: (0, 0))
    args, in_specs, out_shape, out_specs = [dh_out], [row], [], []
    if has_n:
        args += [dhn, h_new, w_next]
        in_specs += [row, row, vec]
    if has_t:
        args += [t, w_post]
        in_specs += [row, vec]
    if has_n:
        out_shape.append(jax.ShapeDtypeStruct((lp, d), F32)); out_specs.append(row)
    if has_t:
        out_shape.append(jax.ShapeDtypeStruct((lp, d), BF16)); out_specs.append(row)
    if has_n:
        out_shape.append(jax.ShapeDtypeStruct((1, d), F32)); out_specs.append(vec)
    if has_t:
        out_shape.append(jax.ShapeDtypeStruct((1, d), F32)); out_specs.append(vec)
    outs = list(pl.pallas_call(body, out_shape=tuple(out_shape), grid=(lp // tm,), in_specs=in_specs,
                               out_specs=tuple(out_specs), compiler_params=_cp("arbitrary"), name=name)(*args))
    dh = outs.pop(0) if has_n else dh_out
    dt = outs.pop(0) if has_t else None
    dwn = outs.pop(0) if has_n else None
    dwp = outs.pop(0) if has_t else None
    return dh, dt, dwn, dwp


def _loss_head(h, f, w_post, target, *, name):
    lp, d = h.shape
    tm = _row_tile(lp)

    def body(h_ref, f_ref, w_ref, t_ref, loss_ref, dy_ref):
        i = pl.program_id(0)
        y = h_ref[...] + _rms(f_ref[...], w_ref[...])
        rows = i * tm + lax.broadcasted_iota(jnp.int32, (tm, 1), 0)
        tok = (rows >= FRONT) & (rows < lp - BACK)
        err = jnp.where(tok, y - t_ref[...], 0.0)
        dy_ref[...] = err * (1.0 / d)

        @pl.when(i == 0)
        def _():
            loss_ref[...] = jnp.zeros_like(loss_ref)
        part = jnp.sum(jnp.sum(err * err, axis=1, keepdims=True), axis=0, keepdims=True) * (0.5 / d)
        loss_ref[...] += jnp.broadcast_to(part, loss_ref.shape)

    row = pl.BlockSpec((tm, d), lambda i: (i, 0))
    loss, dy = pl.pallas_call(
        body, out_shape=(jax.ShapeDtypeStruct((8, 128), F32), jax.ShapeDtypeStruct((lp, d), F32)),
        grid=(lp // tm,), in_specs=[row, row, pl.BlockSpec((1, d), lambda i: (0, 0)), row],
        out_specs=(pl.BlockSpec((8, 128), lambda i: (0, 0)), row),
        compiler_params=_cp("arbitrary"), name=name)(h, f, w_post, target)
    return loss[0, 0], dy


_GELU_C = math.sqrt(2.0 / math.pi)


def _gelu(a):
    return 0.5 * a * (1.0 + jnp.tanh(_GELU_C * (a + 0.044715 * a * a * a)))


def _gelu_grad(a):
    t = jnp.tanh(_GELU_C * (a + 0.044715 * a * a * a))
    return 0.5 * (1.0 + t) + 0.5 * a * (1.0 - t * t) * _GELU_C * (1.0 + 3.0 * 0.044715 * a * a)


def _conv3(x, n, w, b):
    tot = x.shape[0]
    x1 = pltpu.roll(x, 1, 0)
    x2 = pltpu.roll(x, 2, 0)
    return (b + x[8:8 + n] * w[2:3] + x1[8:8 + n] * w[1:2] + x2[8:8 + n] * w[0:1]), x1[8:8 + n], x2[8:8 + n]


def _conv_act(ua, ug, wa, wg, ba, bg, *, name):
    lp, n = ua.shape
    tm = _row_tile(lp)
    tc = _tile(n, (256, 128))
    nb8 = tm // 8

    def body(ua_ref, uap_ref, ug_ref, ugp_ref, wa_ref, wg_ref, ba_ref, bg_ref, o_ref):
        i = pl.program_id(0)
        grow = i * tm - 8 + lax.broadcasted_iota(jnp.int32, (tm + 8, 1), 0)
        xa = jnp.where(grow >= 0, jnp.concatenate([uap_ref[...], ua_ref[...]], axis=0), 0.0)
        xg = jnp.where(grow >= 0, jnp.concatenate([ugp_ref[...], ug_ref[...]], axis=0), 0.0)
        ca, _, _ = _conv3(xa, tm, wa_ref[...], ba_ref[...])
        cg, _, _ = _conv3(xg, tm, wg_ref[...], bg_ref[...])
        rows = grow[8:]
        ok = (rows >= PADF) & (rows < lp - BACK)
        o_ref[...] = jnp.where(ok, _gelu(ca) * cg, 0.0).astype(BF16)

    cur = pl.BlockSpec((tm, tc), lambda i, j: (i, j))
    prev = pl.BlockSpec((8, tc), lambda i, j: (jnp.maximum(i * nb8 - 1, 0), j))
    w3 = pl.BlockSpec((3, tc), lambda i, j: (0, j))
    b1 = pl.BlockSpec((1, tc), lambda i, j: (0, j))
    return pl.pallas_call(
        body, out_shape=jax.ShapeDtypeStruct((lp, n), BF16), grid=(lp // tm, n // tc),
        in_specs=[cur, prev, cur, prev, w3, w3, b1, b1], out_specs=cur,
        compiler_params=_cp("parallel", "parallel"), name=name)(ua, ua, ug, ug, wa, wg, ba, bg)


def _conv_act_bwd(ua, ug, dact, wa, wg, ba, bg, *, name):
    lp, n = ua.shape
    tm = _row_tile(lp)
    tc = _tile(n, (256, 128))
    nb8 = tm // 8
    last8 = lp // 8 - 1
    ext = tm + 8

    def body(ua_ref, uap_ref, uan_ref, ug_ref, ugp_ref, ugn_ref, da_ref, dan_ref, wa_ref, wg_ref, ba_ref, bg_ref,
             dua_ref, dug_ref, dwa_ref, dwg_ref, dba_ref, dbg_ref):
        i = pl.program_id(1)
        grow = i * tm - 8 + lax.broadcasted_iota(jnp.int32, (tm + 16, 1), 0)
        inside = (grow >= 0) & (grow < lp)
        xa = jnp.where(inside, jnp.concatenate([uap_ref[...], ua_ref[...], uan_ref[...]], axis=0), 0.0)
        xg = jnp.where(inside, jnp.concatenate([ugp_ref[...], ug_ref[...], ugn_ref[...]], axis=0), 0.0)
        wa, wg = wa_ref[...], wg_ref[...]
        ca, xa1, xa2 = _conv3(xa, ext, wa, ba_ref[...])
        cg, xg1, xg2 = _conv3(xg, ext, wg, bg_ref[...])
        rows = grow[8:]
        ok = (rows >= PADF) & (rows < lp - BACK)
        dact_e = jnp.where(ok, jnp.concatenate([da_ref[...], dan_ref[...]], axis=0), 0.0)
        dca = dact_e * cg * _gelu_grad(ca)
        dcg = dact_e * _gelu(ca)

        def back(dc, w):
            return (dc[:tm] * w[2:3] + pltpu.roll(dc, ext - 1, 0)[:tm] * w[1:2]
                    + pltpu.roll(dc, ext - 2, 0)[:tm] * w[0:1])

        dua_ref[...] = back(dca, wa).astype(BF16)
        dug_ref[...] = back(dcg, wg).astype(BF16)

        @pl.when(i == 0)
        def _():
            dwa_ref[...] = jnp.zeros_like(dwa_ref)
            dwg_ref[...] = jnp.zeros_like(dwg_ref)
            dba_ref[...] = jnp.zeros_like(dba_ref)
            dbg_ref[...] = jnp.zeros_like(dbg_ref)

        def wsum(dw_ref, db_ref, dc, x, x1, x2):
            d = dc[:tm]
            s = lambda v: jnp.sum(v, axis=0, keepdims=True)
            dw_ref[0:1, :] += s(d * x2[:tm])
            dw_ref[1:2, :] += s(d * x1[:tm])
            dw_ref[2:3, :] += s(d * x[8:8 + tm])
            db_ref[...] += s(d)

        wsum(dwa_ref, dba_ref, dca, xa, xa1, xa2)
        wsum(dwg_ref, dbg_ref, dcg, xg, xg1, xg2)

    cur = pl.BlockSpec((tm, tc), lambda j, i: (i, j))
    prev = pl.BlockSpec((8, tc), lambda j, i: (jnp.maximum(i * nb8 - 1, 0), j))
    nxt = pl.BlockSpec((8, tc), lambda j, i: (jnp.minimum((i + 1) * nb8, last8), j))
    w3 = pl.BlockSpec((3, tc), lambda j, i: (0, j))
    b1 = pl.BlockSpec((1, tc), lambda j, i: (0, j))
    return pl.pallas_call(
        body,
        out_shape=(jax.ShapeDtypeStruct((lp, n), BF16), jax.ShapeDtypeStruct((lp, n), BF16),
                   jax.ShapeDtypeStruct((3, n), F32), jax.ShapeDtypeStruct((3, n), F32),
                   jax.ShapeDtypeStruct((1, n), F32), jax.ShapeDtypeStruct((1, n), F32)),
        grid=(n // tc, lp // tm),
        in_specs=[cur, prev, nxt, cur, prev, nxt, cur, nxt, w3, w3, b1, b1],
        out_specs=(cur, cur, w3, w3, b1, b1),
        compiler_params=_cp("parallel", "arbitrary"), name=name)(ua, ua, ua, ug, ug, ug, dact, dact, wa, wg, ba, bg)


def _sigmoid(x):
    return 1.0 / (1.0 + jnp.exp(-x))


def _merge(o_ret, o_gla, proj, w_ret, w_gla, *, name):
    lp = o_ret.shape[0]
    tm = _row_tile(lp)

    def body(or_ref, og_ref, rg_ref, gr_ref, wr_ref, wg_ref, m_ref):
        oret, ogla = or_ref[...], og_ref[...]
        yr, yg = [], []
        for h in range(4):
            hs = slice(128 * h, 128 * h + 128)
            o = oret[:, hs]
            xc = o - jnp.mean(o, axis=-1, keepdims=True)
            yr.append(xc * lax.rsqrt(jnp.mean(xc * xc, axis=-1, keepdims=True) + EPS))
            o = ogla[:, hs]
            yg.append(o * lax.rsqrt(jnp.mean(o * o, axis=-1, keepdims=True) + EPS))
        rg, gr = rg_ref[...], gr_ref[...]
        m_ref[:, 0:512] = (jnp.concatenate(yr, axis=1) * wr_ref[...] * (rg * _sigmoid(rg))).astype(BF16)
        m_ref[:, 512:1024] = (jnp.concatenate(yg, axis=1) * wg_ref[...] * (gr * _sigmoid(gr))).astype(BF16)

    row = pl.BlockSpec((tm, 512), lambda i: (i, 0))
    vec = pl.BlockSpec((1, 512), lambda i: (0, 0))
    return pl.pallas_call(
        body, out_shape=jax.ShapeDtypeStruct((lp, 1024), BF16), grid=(lp // tm,),
        in_specs=[row, row, pl.BlockSpec((tm, 512), lambda i: (i, C_RG // 512)),
                  pl.BlockSpec((tm, 512), lambda i: (i, C_GR // 512)), vec, vec],
        out_specs=pl.BlockSpec((tm, 1024), lambda i: (i, 0)),
        compiler_params=_cp("parallel"), name=name)(o_ret, o_gla, proj, proj, w_ret, w_gla)


def _merge_bwd(dm, o_ret, o_gla, proj, w_ret, w_gla, *, name):
    lp = o_ret.shape[0]
    tm = _row_tile(lp)

    def body(dm_ref, or_ref, og_ref, rg_ref, gr_ref, wr_ref, wg_ref, dor_ref, dog_ref, drg_ref, dgr_ref, dwr_ref, dwg_ref):
        @pl.when(pl.program_id(0) == 0)
        def _():
            dwr_ref[...] = jnp.zeros_like(dwr_ref)
            dwg_ref[...] = jnp.zeros_like(dwg_ref)

        def group(d, o_all, gate, w, center):
            sg = _sigmoid(gate)
            s = gate * sg
            ds = sg * (1.0 + gate * (1.0 - sg))
            xh, rr = [], []
            for h in range(4):
                o = o_all[:, 128 * h:128 * h + 128]
                if center:
                    o = o - jnp.mean(o, axis=-1, keepdims=True)
                r = lax.rsqrt(jnp.mean(o * o, axis=-1, keepdims=True) + EPS)
                xh.append(o * r)
                rr.append(r)
            xh_all = jnp.concatenate(xh, axis=1)
            dgate = d * xh_all * w * ds
            dw = jnp.sum(d * xh_all * s, axis=0, keepdims=True)
            dxh_all = d * w * s
            do = []
            for h in range(4):
                dxh = dxh_all[:, 128 * h:128 * h + 128]
                t = dxh - xh[h] * jnp.mean(dxh * xh[h], axis=-1, keepdims=True)
                if center:
                    t = t - jnp.mean(dxh, axis=-1, keepdims=True)
                do.append(rr[h] * t)
            return jnp.concatenate(do, axis=1), dgate, dw

        dmv = dm_ref[...]
        do, dg, dw = group(dmv[:, 0:512], or_ref[...], rg_ref[...], wr_ref[...], True)
        dor_ref[...] = do
        drg_ref[...] = dg.astype(BF16)
        dwr_ref[...] += dw
        do, dg, dw = group(dmv[:, 512:1024], og_ref[...], gr_ref[...], wg_ref[...], False)
        dog_ref[...] = do
        dgr_ref[...] = dg.astype(BF16)
        dwg_ref[...] += dw

    row = pl.BlockSpec((tm, 512), lambda i: (i, 0))
    vec = pl.BlockSpec((1, 512), lambda i: (0, 0))
    return pl.pallas_call(
        body,
        out_shape=(jax.ShapeDtypeStruct((lp, 512), F32), jax.ShapeDtypeStruct((lp, 512), F32),
                   jax.ShapeDtypeStruct((lp, 512), BF16), jax.ShapeDtypeStruct((lp, 512), BF16),
                   jax.ShapeDtypeStruct((1, 512), F32), jax.ShapeDtypeStruct((1, 512), F32)),
        grid=(lp // tm,),
        in_specs=[pl.BlockSpec((tm, 1024), lambda i: (i, 0)), row, row,
                  pl.BlockSpec((tm, 512), lambda i: (i, C_RG // 512)),
                  pl.BlockSpec((tm, 512), lambda i: (i, C_GR // 512)), vec, vec],
        out_specs=(row, row, row, row, vec, vec),
        compiler_params=_cp("arbitrary"), name=name)(dm, o_ret, o_gla, proj, proj, w_ret, w_gla)


def _dot(a, b):
    return lax.dot_general(a, b, (((1,), (0,)), ((), ())), preferred_element_type=F32)


def _dot_nt(a, b):
    return lax.dot_general(a, b, (((1,), (1,)), ((), ())), preferred_element_type=F32)


def _dot_tn(a, b):
    return lax.dot_general(a, b, (((0,), (0,)), ((), ())), preferred_element_type=F32)


def _ret_tables(lp):
    cr = RET_CHUNK
    pos = jnp.arange(lp, dtype=F32) - float(PADF)
    half = RET_DK // 2
    inv = ROPE_BASE ** (-jnp.arange(half, dtype=F32) / half)
    ang = pos[:, None] * inv[None, :]
    c, s = jnp.cos(ang), jnp.sin(ang)
    rope_c = jnp.concatenate([c, c], axis=1)
    rope_s = jnp.concatenate([-s, s], axis=1)
    log_g = np.log(1.0 - 2.0 ** (-5.0 - np.arange(RET_HEADS, dtype=np.float64)))
    idx = np.arange(cr, dtype=np.float64)
    diff = idx[:, None] - idx[None, :]
    dmat = np.where(diff >= 0, np.exp(log_g[:, None, None] * np.maximum(diff, 0.0)), 0.0)
    zeta = np.exp(log_g[:, None] * (cr - 1.0 - idx)[None, :])
    xi = np.exp(log_g[:, None] * (idx + 1.0)[None, :])
    gc = np.exp(log_g * cr)
    f = lambda a: jnp.asarray(a.astype(np.float32))
    return (rope_c, rope_s, f(dmat), f(np.broadcast_to(zeta[:, :, None], (RET_HEADS, cr, 128))),
            f(np.broadcast_to(xi[:, :, None], (RET_HEADS, cr, 128))),
            f(np.broadcast_to(gc[:, None, None], (RET_HEADS, 8, 128))))


def _rope(t, c, s):
    return t * c + pltpu.roll(t, 64, 1) * s


def _rope_t(d, c, s):
    return d * c + pltpu.roll(d * s, 64, 1)


def _ret_specs(fwd_blocks, nblk, rev):
    ix = (lambda i: nblk - 1 - i) if rev else (lambda i: i)
    cr = RET_CHUNK
    col = lambda base: pl.BlockSpec((BLK, 128), lambda h, i: (ix(i), base // 128 + h))
    tab = pl.BlockSpec((BLK, 128), lambda h, i: (ix(i), 0))
    sq = pl.BlockSpec((1, cr, cr), lambda h, i: (h, 0, 0))
    hv = pl.BlockSpec((1, cr, 128), lambda h, i: (h, 0, 0))
    g8 = pl.BlockSpec((1, 8, 128), lambda h, i: (h, 0, 0))
    st = pl.BlockSpec((1, BLK // cr, 128, 128), lambda h, i: (h, ix(i), 0, 0))
    out = pl.BlockSpec((BLK, 128), lambda h, i: (ix(i), h))
    return col, tab, sq, hv, g8, st, out


def _retention(proj, tables, *, name):
    lp = proj.shape[0]
    nblk, cr = lp // BLK, RET_CHUNK
    scale = RET_DK ** -0.5

    def body(q_ref, k_ref, v_ref, c_ref, s_ref, d_ref, z_ref, x_ref, g_ref, o_ref, st_ref, state):
        @pl.when(pl.program_id(1) == 0)
        def _():
            state[...] = jnp.zeros_like(state)
        dmat, zeta, xi, gc = d_ref[0], z_ref[0], x_ref[0], g_ref[0][0:1, :]

        def chunk(ci, carry):
            sl = pl.ds(pl.multiple_of(ci * cr, cr), cr)
            c, s = c_ref[sl, :], s_ref[sl, :]
            q = _rope(q_ref[sl, :], c, s)
            k = _rope(k_ref[sl, :], c, s) * scale
            qb, kb, vb = q.astype(BF16), k.astype(BF16), v_ref[sl, :].astype(BF16)
            st = state[...]
            st_ref[0, ci] = st
            sc = _dot_nt(qb, kb) * dmat
            o_ref[sl, :] = _dot(sc.astype(BF16), vb) + _dot(qb, st.astype(BF16)) * xi
            state[...] = st * gc + _dot_tn((k * zeta).astype(BF16), vb)
            return carry

        lax.fori_loop(0, BLK // cr, chunk, 0)

    col, tab, sq, hv, g8, st, out = _ret_specs(True, nblk, False)
    return pl.pallas_call(
        body,
        out_shape=(jax.ShapeDtypeStruct((lp, 512), F32), jax.ShapeDtypeStruct((4, lp // cr, 128, 128), F32)),
        grid=(4, nblk), in_specs=[col(C_RQ), col(C_RK), col(C_RV), tab, tab, sq, hv, hv, g8],
        out_specs=(out, st), scratch_shapes=[pltpu.VMEM((128, 128), F32)],
        compiler_params=_cp("parallel", "arbitrary"), name=name)(proj, proj, proj, *tables)


def _retention_bwd(proj, do, states, tables, *, name):
    lp = proj.shape[0]
    nblk, cr = lp // BLK, RET_CHUNK
    nch = BLK // cr
    scale = RET_DK ** -0.5

    def body(q_ref, k_ref, v_ref, do_ref, st_ref, c_ref, s_ref, d_ref, z_ref, x_ref, g_ref, dq_ref, dk_ref, dv_ref, dstate):
        @pl.when(pl.program_id(1) == 0)
        def _():
            dstate[...] = jnp.zeros_like(dstate)
        dmat, zeta, xi, gc = d_ref[0], z_ref[0], x_ref[0], g_ref[0][0:1, :]

        def chunk(cc, carry):
            ci = nch - 1 - cc
            sl = pl.ds(pl.multiple_of(ci * cr, cr), cr)
            c, s = c_ref[sl, :], s_ref[sl, :]
            q = _rope(q_ref[sl, :], c, s)
            k = _rope(k_ref[sl, :], c, s) * scale
            qb, kb, vb = q.astype(BF16), k.astype(BF16), v_ref[sl, :].astype(BF16)
            kzb = (k * zeta).astype(BF16)
            dov = do_ref[sl, :]
            dob, doxb = dov.astype(BF16), (dov * xi).astype(BF16)
            stb = st_ref[0, ci].astype(BF16)
            dsn = dstate[...]
            dsnb = dsn.astype(BF16)
            scb = (_dot_nt(qb, kb) * dmat).astype(BF16)
            dscb = (_dot_nt(dob, vb) * dmat).astype(BF16)
            dq = _dot(dscb, kb) + _dot_nt(doxb, stb)
            dk = _dot_tn(dscb, qb) + _dot_nt(vb, dsnb) * zeta
            dv = _dot_tn(scb, dob) + _dot(kzb, dsnb)
            dstate[...] = dsn * gc + _dot_tn(qb, doxb)
            dq_ref[sl, :] = _rope_t(dq, c, s).astype(BF16)
            dk_ref[sl, :] = _rope_t(dk * scale, c, s).astype(BF16)
            dv_ref[sl, :] = dv.astype(BF16)
            return carry

        lax.fori_loop(0, nch, chunk, 0)

    col, tab, sq, hv, g8, st, out = _ret_specs(False, nblk, True)
    o3 = jax.ShapeDtypeStruct((lp, 512), BF16)
    return pl.pallas_call(
        body, out_shape=(o3, o3, o3), grid=(4, nblk),
        in_specs=[col(C_RQ), col(C_RK), col(C_RV), out, st, tab, tab, sq, hv, hv, g8],
        out_specs=(out, out, out), scratch_shapes=[pltpu.VMEM((128, 128), F32)],
        compiler_params=_cp("parallel", "arbitrary"), name=name)(proj, proj, proj, do, states, *tables)


def _gla_tables():
    c = GLA_CHUNK
    tri = np.tril(np.ones((c, c), np.float32))
    ones_qv = np.kron(np.eye(GLA_HEADS, dtype=np.float32), np.ones((GLA_DK, GLA_DV), np.float32))
    return (jnp.asarray(tri, BF16), jnp.asarray(tri.T.copy(), BF16), jnp.asarray(ones_qv, BF16),
            jnp.asarray(ones_qv.T.copy(), BF16))


def _split3(x):
    hi = x.astype(BF16)
    r1 = x - hi.astype(F32)
    mid = r1.astype(BF16)
    lo = (r1 - mid.astype(F32)).astype(BF16)
    return hi, mid, lo


def _tri_sum(tri, x):
    hi, mid, lo = _split3(x)
    return _dot(tri, hi) + _dot(tri, mid) + _dot(tri, lo)


def _head_masks(width, per):
    lane = lax.broadcasted_iota(jnp.int32, (1, width), 1)
    return [((lane >= per * h) & (lane < per * (h + 1))).astype(F32) for h in range(GLA_HEADS)]


def _stack_heads(x, masks):
    return jnp.concatenate([x * m for m in masks], axis=0)


def _gla_gate(ga, w2, b, ok, tri):
    z = _dot(ga.astype(BF16), w2) + b
    la = (jnp.minimum(z, 0.0) - jnp.log(1.0 + jnp.exp(-jnp.abs(z)))) * (1.0 / GLA_TAU)
    la = jnp.where(ok, la, 0.0)
    return z, _tri_sum(tri, la)


def _gla_rows(i_blk, ci, lp):
    c = GLA_CHUNK
    rows = i_blk * BLK + ci * c + lax.broadcasted_iota(jnp.int32, (c, 1), 0)
    return (rows >= PADF) & (rows < lp - BACK)


def _gla_off_parts(a, qs, k, g, hm_q):
    s = GLA_SUB
    ra = g[s * a - 1:s * a, :]
    ga_ = g[s * a:s * a + s, :]
    eq = jnp.exp(ga_ - ra)
    ek = jnp.exp(jnp.minimum(ra - g, 0.0))
    qh = qs[s * a:s * a + s, :] * eq
    kh = k * ek
    qst = _stack_heads(qh, hm_q).astype(BF16)
    col = lax.broadcasted_iota(jnp.int32, (GLA_HEADS * s, GLA_CHUNK), 1)
    pmask = col < s * a
    p = jnp.where(pmask, _dot_nt(qst, kh.astype(BF16)), 0.0)
    return eq, ek, qh, kh, qst, pmask, p


def _lag_mask(j):
    r = lax.broadcasted_iota(jnp.int32, (GLA_CHUNK, 1), 0)
    return (jnp.bitwise_and(r, GLA_SUB - 1) >= j).astype(F32)


def _roll_rows(x, j):
    return x if j == 0 else pltpu.roll(x, j, 0)


def _gla(proj, w2p, b, tables, *, name):
    lp = proj.shape[0]
    nblk, c, s = lp // BLK, GLA_CHUNK, GLA_SUB
    nch = BLK // c
    na = c // s

    def body(q_ref, k_ref, v_ref, a_ref, w_ref, b_ref, tri_ref, ones_ref, o_ref, st_ref, state):
        i_blk = pl.program_id(0)

        @pl.when(i_blk == 0)
        def _():
            state[...] = jnp.zeros_like(state)
        hm_q = _head_masks(GLA_QK, GLA_DK)
        tri, ones_qv, w2, bias = tri_ref[...], ones_ref[...], w_ref[...], b_ref[...]

        def chunk(ci, carry):
            sl = pl.ds(pl.multiple_of(ci * c, c), c)
            ok = _gla_rows(i_blk, ci, lp)
            k, v = k_ref[sl, :], v_ref[sl, :]
            vb = v.astype(BF16)
            qs = q_ref[sl, :] * (GLA_DK ** -0.5)
            _, g = _gla_gate(a_ref[sl, :], w2, bias, ok, tri)
            last = g[c - 1:c, :]
            st = state[...]
            st_ref[ci] = st
            qst = _stack_heads(qs * jnp.exp(g), hm_q).astype(BF16)
            oi = _dot_nt(qst, st.astype(BF16))
            o = jnp.concatenate([oi[c * h:c * h + c, :] for h in range(GLA_HEADS)], axis=1)
            ke = k * jnp.exp(last - g)
            f = _dot_tn(vb, ke.astype(BF16))
            upd = f[0:GLA_DV, :] * hm_q[0]
            for h in range(1, GLA_HEADS):
                upd = upd + f[GLA_DV * h:GLA_DV * (h + 1), :] * hm_q[h]
            state[...] = st * jnp.exp(last) + upd
            off = [jnp.zeros((s, GLA_V), F32)]
            for a in range(1, na):
                p = _gla_off_parts(a, qs, k, g, hm_q)[-1]
                ob = _dot(p.astype(BF16), vb)
                off.append(jnp.concatenate(
                    [ob[s * h:s * h + s, GLA_DV * h:GLA_DV * (h + 1)] for h in range(GLA_HEADS)], axis=1))
            o = o + jnp.concatenate(off, axis=0)
            for j in range(s):
                ej = jnp.exp(jnp.minimum(g - _roll_rows(g, j), 0.0))
                wj = qs * _roll_rows(k, j) * ej * _lag_mask(j)
                o = o + _dot(wj.astype(BF16), ones_qv) * _roll_rows(v, j)
            o_ref[sl, :] = o
            return carry

        lax.fori_loop(0, nch, chunk, 0)

    tri, _, ones_qv, _ = tables
    full = lambda arr: pl.BlockSpec(arr.shape, lambda i: (0,) * arr.ndim)
    return pl.pallas_call(
        body,
        out_shape=(jax.ShapeDtypeStruct((lp, GLA_V), F32), jax.ShapeDtypeStruct((lp // c, GLA_DV, GLA_QK), F32)),
        grid=(nblk,),
        in_specs=[pl.BlockSpec((BLK, GLA_QK), lambda i: (i, C_GQ // GLA_QK)),
                  pl.BlockSpec((BLK, GLA_QK), lambda i: (i, C_GK // GLA_QK)),
                  pl.BlockSpec((BLK, GLA_V), lambda i: (i, C_GV // GLA_V)),
                  pl.BlockSpec((BLK, 128), lambda i: (i, C_GA // 128)),
                  full(w2p), full(b), full(tri), full(ones_qv)],
        out_specs=(pl.BlockSpec((BLK, GLA_V), lambda i: (i, 0)),
                   pl.BlockSpec((nch, GLA_DV, GLA_QK), lambda i: (i, 0, 0))),
        scratch_shapes=[pltpu.VMEM((GLA_DV, GLA_QK), F32)],
        compiler_params=_cp("arbitrary"), name=name)(proj, proj, proj, proj, w2p, b, tri, ones_qv)


def _gla_bwd(proj, do, states, w2p, b, tables, *, name):
    lp = proj.shape[0]
    nblk, c, s = lp // BLK, GLA_CHUNK, GLA_SUB
    nch = BLK // c
    na = c // s

    def body(q_ref, k_ref, v_ref, a_ref, do_ref, st_ref, w_ref, b_ref, tri_ref, trit_ref, ones_ref, onest_ref,
             dq_ref, dk_ref, dv_ref, da_ref, dw_ref, db_ref, dstate, dqs_s, dk_s, dg_s, dv_s):
        i_blk = nblk - 1 - pl.program_id(0)

        @pl.when(pl.program_id(0) == 0)
        def _():
            dstate[...] = jnp.zeros_like(dstate)
            dw_ref[...] = jnp.zeros_like(dw_ref)
            db_ref[...] = jnp.zeros_like(db_ref)
        hm_q = _head_masks(GLA_QK, GLA_DK)
        hm_v = _head_masks(GLA_V, GLA_DV)
        tri, trit, ones_qv, ones_vq = tri_ref[...], trit_ref[...], ones_ref[...], onest_ref[...]
        w2, bias = w_ref[...], b_ref[...]
        rsum = lambda x: jnp.sum(x, axis=0, keepdims=True)

        def chunk(cc, carry):
            ci = nch - 1 - cc
            sl = pl.ds(pl.multiple_of(ci * c, c), c)
            ok = _gla_rows(i_blk, ci, lp)
            k, v, ga = k_ref[sl, :], v_ref[sl, :], a_ref[sl, :]
            vb = v.astype(BF16)
            qs = q_ref[sl, :] * (GLA_DK ** -0.5)
            z, g = _gla_gate(ga, w2, bias, ok, tri)
            last = g[c - 1:c, :]
            elast = jnp.exp(last)
            eg = jnp.exp(g)
            ekl = jnp.exp(last - g)
            qe, ke = qs * eg, k * ekl
            dov = do_ref[sl, :]
            st = st_ref[ci]
            dsn = dstate[...]
            qst = _stack_heads(qe, hm_q).astype(BF16)
            dost = jnp.concatenate([dov[:, GLA_DV * h:GLA_DV * (h + 1)] for h in range(GLA_HEADS)], axis=0).astype(BF16)
            dqe_st = _dot(dost, st.astype(BF16))
            dqe = dqe_st[0:c, :] * hm_q[0]
            for h in range(1, GLA_HEADS):
                dqe = dqe + dqe_st[c * h:c * h + c, :] * hm_q[h]
            dstate[...] = _dot_tn(dost, qst) + dsn * elast
            dlast = rsum(dsn * st) * elast
            df = _stack_heads(dsn, hm_q).astype(BF16)
            dv_s[...] = _dot_nt(ke.astype(BF16), df)
            dke = _dot(vb, df)
            xk = dke * ke
            dqs_s[...] = dqe * eg
            dk_s[...] = dke * ekl
            dg_s[...] = dqe * qe - xk
            dlast = dlast + rsum(xk)
            for a in range(1, na):
                eq, ek, qh, kh, qsa, pmask, p = _gla_off_parts(a, qs, k, g, hm_q)
                rows = slice(s * a, s * a + s)
                dofull = _stack_heads(dov[rows, :], hm_v).astype(BF16)
                dp = jnp.where(pmask, _dot_nt(dofull, vb), 0.0).astype(BF16)
                dv_s[...] += _dot_tn(p.astype(BF16), dofull)
                dq_st = _dot(dp, kh.astype(BF16))
                dqh = dq_st[0:s, :] * hm_q[0]
                for h in range(1, GLA_HEADS):
                    dqh = dqh + dq_st[s * h:s * h + s, :] * hm_q[h]
                dkh = _dot_tn(dp, qsa)
                xq = dqh * qh
                xkh = dkh * kh
                dqs_s[rows, :] += dqh * eq
                dg_s[rows, :] += xq
                dk_s[...] += dkh * ek
                dg_s[...] -= xkh
                dg_s[s * a - 1:s * a, :] += rsum(xkh) - rsum(xq)
            for j in range(s):
                back = (lambda x: x) if j == 0 else (lambda x: pltpu.roll(x, c - j, 0))
                kj, vj = _roll_rows(k, j), _roll_rows(v, j)
                ej = jnp.exp(jnp.minimum(g - _roll_rows(g, j), 0.0))
                mj = _lag_mask(j)
                wj = qs * kj * ej * mj
                bj = _dot(wj.astype(BF16), ones_qv)
                dv_s[...] += back(bj * dov)
                dw = _dot((dov * vj).astype(BF16), ones_vq) * mj
                dqs_s[...] += dw * kj * ej
                dk_s[...] += back(dw * qs * ej)
                x = dw * wj
                dg_s[...] += x - back(x)
            dg_s[c - 1:c, :] += dlast
            dla = jnp.where(ok, _tri_sum(trit, dg_s[...]), 0.0)
            dz = dla * (1.0 / GLA_TAU) / (1.0 + jnp.exp(z))
            dzb = dz.astype(BF16)
            dq_ref[sl, :] = (dqs_s[...] * (GLA_DK ** -0.5)).astype(BF16)
            dk_ref[sl, :] = dk_s[...].astype(BF16)
            dv_ref[sl, :] = dv_s[...].astype(BF16)
            da_ref[sl, :] = _dot_nt(dzb, w2).astype(BF16)
            dw_ref[...] += _dot_tn(ga.astype(BF16), dzb)
            db_ref[...] += rsum(dz)
            return carry

        lax.fori_loop(0, nch, chunk, 0)

    tri, trit, ones_qv, ones_vq = tables
    full = lambda arr: pl.BlockSpec(arr.shape, lambda i: (0,) * arr.ndim)
    rev = lambda i: nblk - 1 - i
    qk = jax.ShapeDtypeStruct((lp, GLA_QK), BF16)
    return pl.pallas_call(
        body,
        out_shape=(qk, qk, jax.ShapeDtypeStruct((lp, GLA_V), BF16), jax.ShapeDtypeStruct((lp, 128), BF16),
                   jax.ShapeDtypeStruct((128, GLA_QK), F32), jax.ShapeDtypeStruct((1, GLA_QK), F32)),
        grid=(nblk,),
        in_specs=[pl.BlockSpec((BLK, GLA_QK), lambda i: (rev(i), C_GQ // GLA_QK)),
                  pl.BlockSpec((BLK, GLA_QK), lambda i: (rev(i), C_GK // GLA_QK)),
                  pl.BlockSpec((BLK, GLA_V), lambda i: (rev(i), C_GV // GLA_V)),
                  pl.BlockSpec((BLK, 128), lambda i: (rev(i), C_GA // 128)),
                  pl.BlockSpec((BLK, GLA_V), lambda i: (rev(i), 0)),
                  pl.BlockSpec((nch, GLA_DV, GLA_QK), lambda i: (rev(i), 0, 0)),
                  full(w2p), full(b), full(tri), full(trit), full(ones_qv), full(ones_vq)],
        out_specs=(pl.BlockSpec((BLK, GLA_QK), lambda i: (rev(i), 0)),
                   pl.BlockSpec((BLK, GLA_QK), lambda i: (rev(i), 0)),
                   pl.BlockSpec((BLK, GLA_V), lambda i: (rev(i), 0)),
                   pl.BlockSpec((BLK, 128), lambda i: (rev(i), 0)),
                   pl.BlockSpec((128, GLA_QK), lambda i: (0, 0)),
                   pl.BlockSpec((1, GLA_QK), lambda i: (0, 0))),
        scratch_shapes=[pltpu.VMEM((GLA_DV, GLA_QK), F32), pltpu.VMEM((c, GLA_QK), F32),
                        pltpu.VMEM((c, GLA_QK), F32), pltpu.VMEM((c, GLA_QK), F32), pltpu.VMEM((c, GLA_V), F32)],
        compiler_params=_cp("arbitrary"), name=name)(proj, proj, proj, proj, do, states, w2p, b, tri, trit, ones_qv, ones_vq)


def _as2d(a):
    return a.reshape(-1, a.shape[-1])


def _ew_tile(r):
    return _tile(r, (512, 256, 128, 64, 32, 16, 8))


def _add2(a, b, *, name):
    a2, b2 = _as2d(a), _as2d(b)
    r, n = a2.shape
    tm = _ew_tile(r)

    def body(a_ref, b_ref, o_ref):
        o_ref[...] = a_ref[...] + b_ref[...]

    blk = pl.BlockSpec((tm, n), lambda i: (i, 0))
    return pl.pallas_call(body, out_shape=jax.ShapeDtypeStruct((r, n), F32), grid=(r // tm,), in_specs=[blk, blk],
                          out_specs=blk, compiler_params=_cp("parallel"), name=name)(a2, b2).reshape(a.shape)


def _sum_slots(q, *, name):
    shape = q.shape[1:]
    q3 = q.reshape(4, -1, shape[-1])
    r, n = q3.shape[1:]
    tm = _ew_tile(r)

    def body(q_ref, o_ref):
        o_ref[...] = ((q_ref[3] + q_ref[0]) + q_ref[1]) + q_ref[2]

    return pl.pallas_call(
        body, out_shape=jax.ShapeDtypeStruct((r, n), F32), grid=(r // tm,),
        in_specs=[pl.BlockSpec((4, tm, n), lambda i: (0, i, 0))], out_specs=pl.BlockSpec((tm, n), lambda i: (i, 0)),
        compiler_params=_cp("parallel"), name=name)(q3).reshape(shape)


def _adamw(w, g, m, v, *, name):
    shape = w.shape
    w2, g2, m2, v2 = _as2d(w), _as2d(g), _as2d(m), _as2d(v)
    r, n = w2.shape
    tm = _ew_tile(r)
    c1 = 1.0 - ADAM_B1 ** ADAM_STEP
    c2 = 1.0 - ADAM_B2 ** ADAM_STEP

    def body(w_ref, g_ref, m_ref, v_ref, d_ref, mo_ref, vo_ref):
        gv = g_ref[...]
        mn = ADAM_B1 * m_ref[...] + (1.0 - ADAM_B1) * gv
        vn = ADAM_B2 * v_ref[...] + (1.0 - ADAM_B2) * (gv * gv)
        mo_ref[...] = mn
        vo_ref[...] = vn
        d_ref[...] = -ADAM_LR * ((mn / c1) / (jnp.sqrt(vn / c2) + ADAM_EPS) + ADAM_WD * w_ref[...])

    blk = pl.BlockSpec((tm, n), lambda i: (i, 0))
    o = jax.ShapeDtypeStruct((r, n), F32)
    d, mo, vo = pl.pallas_call(body, out_shape=(o, o, o), grid=(r // tm,), in_specs=[blk] * 4, out_specs=(blk,) * 3,
                               compiler_params=_cp("parallel"), name=name)(w2, g2, m2, v2)
    return d.reshape(shape), mo.reshape(shape), vo.reshape(shape)


ANY = pl.BlockSpec(memory_space=pl.ANY)


def _place():
    return lax.axis_index("x"), lax.axis_index("y"), lax.axis_index("c")


def _other_chips(x, y):
    return [(1 - x, y), (x, 1 - y), (1 - x, 1 - y)]


def _remote(src, dst, ssem, rsem, dev):
    return pltpu.make_async_remote_copy(src_ref=src, dst_ref=dst, send_sem=ssem, recv_sem=rsem, device_id=dev,
                                        device_id_type=MESH)


def _allgather_chips(arrs, *, name):
    n = len(arrs)

    def body(*refs):
        ins, outs = refs[:n], refs[n:2 * n]
        s1, r1, s2, r2, lsem = refs[2 * n:]
        x, y, c = _place()
        q = 2 * x + y
        chips = _other_chips(x, y)
        qs = [2 * cx + cy for cx, cy in chips]
        sib = (x, y, 1 - c)
        local, first, passed = [], [], []
        for k in range(n):
            for half in range(2):
                local.append(pltpu.make_async_copy(ins[k].at[half], outs[k].at[half, q], lsem.at[k, half]))
            for j, chip in enumerate(chips):
                first.append(_remote(ins[k].at[c], outs[k].at[c, q], s1.at[k, j], r1.at[k, j], (*chip, c)))
        for cp in local + first:
            cp.start()
        for k in range(n):
            for j, chip in enumerate(chips):
                land = outs[k].at[c, qs[j]]
                _remote(land, land, s1.at[k, j], r1.at[k, j], (*chip, c)).wait_recv()
                fw = _remote(land, land, s2.at[k, j], r2.at[k, j], sib)
                fw.start()
                passed.append(fw)
        for k in range(n):
            for j in range(3):
                land = outs[k].at[1 - c, qs[j]]
                _remote(land, land, s2.at[k, j], r2.at[k, j], sib).wait_recv()
        for cp in first + passed:
            cp.wait_send()
        for cp in local:
            cp.wait()

    sem = pltpu.SemaphoreType.DMA
    return pl.pallas_call(
        body, out_shape=tuple(jax.ShapeDtypeStruct((2, 4) + a.shape[1:], a.dtype) for a in arrs),
        in_specs=[ANY] * n, out_specs=(ANY,) * n,
        scratch_shapes=[sem((n, 3)), sem((n, 3)), sem((n, 3)), sem((n, 3)), sem((n, 2))], name=name)(*arrs)


def _pair_exchange(arrs, *, name):
    n = len(arrs)

    def body(*refs):
        ins, outs = refs[:n], refs[n:2 * n]
        ssem, rsem = refs[2 * n:]
        x, y, c = _place()
        cps = [_remote(ins[k].at[1 - c], outs[k], ssem.at[k], rsem.at[k], (x, y, 1 - c)) for k in range(n)]
        for cp in cps:
            cp.start()
        for cp in cps:
            cp.wait()

    sem = pltpu.SemaphoreType.DMA
    return pl.pallas_call(
        body, out_shape=tuple(jax.ShapeDtypeStruct(a.shape[1:], a.dtype) for a in arrs),
        in_specs=[ANY] * n, out_specs=(ANY,) * n, scratch_shapes=[sem((n,)), sem((n,))], name=name)(*arrs)


def _chip_exchange(arrs, *, name):
    n = len(arrs)

    def body(*refs):
        ins, outs = refs[:n], refs[n:2 * n]
        ssem, rsem, lsem = refs[2 * n:]
        x, y, c = _place()
        q = 2 * x + y
        chips = _other_chips(x, y)
        cps, local = [], []
        for k in range(n):
            local.append(pltpu.make_async_copy(ins[k].at[q], outs[k].at[3], lsem.at[k]))
            for j, (cx, cy) in enumerate(chips):
                cps.append(_remote(ins[k].at[2 * cx + cy], outs[k].at[j], ssem.at[k, j], rsem.at[k, j], (cx, cy, c)))
        for cp in local + cps:
            cp.start()
        for cp in cps:
            cp.wait()
        for cp in local:
            cp.wait()

    sem = pltpu.SemaphoreType.DMA
    return pl.pallas_call(
        body, out_shape=tuple(jax.ShapeDtypeStruct(a.shape, a.dtype) for a in arrs),
        in_specs=[ANY] * n, out_specs=(ANY,) * n, scratch_shapes=[sem((n, 3)), sem((n, 3)), sem((n,))],
        name=name)(*arrs)


def _pair_share(arrs, *, name):
    n = len(arrs)

    def body(*refs):
        ins, outs = refs[:n], refs[n:2 * n]
        ssem, rsem, lsem = refs[2 * n:]
        x, y, c = _place()
        local = [pltpu.make_async_copy(ins[k], outs[k].at[c], lsem.at[k]) for k in range(n)]
        for cp in local:
            cp.start()
        for k in range(n):
            _remote(ins[k], outs[k].at[c], ssem.at[k], rsem.at[k], (x, y, 1 - c)).start()
        for k in range(n):
            _remote(ins[k], outs[k].at[1 - c], ssem.at[k], rsem.at[k], (x, y, 1 - c)).wait()
        for cp in local:
            cp.wait()

    sem = pltpu.SemaphoreType.DMA
    return pl.pallas_call(
        body, out_shape=tuple(jax.ShapeDtypeStruct((2,) + a.shape, a.dtype) for a in arrs),
        in_specs=[ANY] * n, out_specs=(ANY,) * n, scratch_shapes=[sem((n,)), sem((n,)), sem((n,))],
        name=name)(*arrs)


def _allreduce_small(slab, *, name):
    r, n = slab.shape

    def body(x_ref, o_ref, buf, ssem, rsem):
        x, y, c = _place()
        me = 4 * x + 2 * y + c
        buf[me] = x_ref[...]
        cps = []
        for rel in range(1, 8):
            bx, by, bc = (rel >> 2) & 1, (rel >> 1) & 1, rel & 1
            px, py, pc = (x + bx) % 2, (y + by) % 2, (c + bc) % 2
            cps.append((_remote(x_ref, buf.at[me], ssem.at[rel - 1], rsem.at[rel - 1], (px, py, pc)),
                        4 * px + 2 * py + pc, (px, py, pc)))
        for cp, _, _ in cps:
            cp.start()
        for rel, (cp, peer, dev) in enumerate(cps):
            cp.wait_send()
            _remote(x_ref, buf.at[peer], ssem.at[rel], rsem.at[rel], dev).wait_recv()
        acc = buf[0]
        for k in range(1, 8):
            acc = acc + buf[k]
        o_ref[...] = acc

    vm = pl.BlockSpec(memory_space=pltpu.VMEM)
    sem = pltpu.SemaphoreType.DMA
    return pl.pallas_call(
        body, out_shape=jax.ShapeDtypeStruct((r, n), F32), in_specs=[vm], out_specs=vm,
        scratch_shapes=[pltpu.VMEM((8, r, n), F32), sem((7,)), sem((7,))], name=name)(slab)


def _slab(arrs, row_mult):
    flat = jnp.concatenate([a.reshape(-1) for a in arrs])
    unit = 128 * row_mult
    total = -(-flat.size // unit) * unit
    return jnp.pad(flat, (0, total - flat.size)).reshape(-1, 128)


def _unslab(slab, shapes):
    flat = slab.reshape(-1)
    out, off = [], 0
    for s in shapes:
        size = int(np.prod(s))
        out.append(flat[off:off + size].reshape(s))
        off += size
    return out


def _cols_from_chips(a):
    return jnp.transpose(a, (1, 0, 2)).reshape(a.shape[1], -1)


def _cols_to_chips(a, parts):
    r = a.shape[0]
    return jnp.transpose(a.reshape(r, parts, -1), (1, 0, 2))


def _gather_params(w_in, w_out, ffn_up, ffn_down, meta_tokens, gla_gate_w2, ffn_conv_w):
    d = D_MODEL
    sh_shapes = [meta_tokens.shape, gla_gate_w2.shape, ffn_conv_w.shape]
    sh_slab = _slab([meta_tokens, gla_gate_w2, ffn_conv_w], 16)
    g_in, g_out, g_up, g_down, g_sh = _allgather_chips(
        [w_in.astype(BF16), w_out.astype(BF16), ffn_up.astype(BF16), ffn_down.astype(BF16),
         sh_slab.reshape(2, -1, 128)], name="gather_weights")
    sh_all = jnp.transpose(g_sh, (1, 0, 2, 3)).reshape(4, -1, 128)
    parts = [_unslab(sh_all[k], sh_shapes) for k in range(4)]
    meta_full = jnp.concatenate([p[0] for p in parts], axis=-1)
    w2_full = jnp.concatenate([p[1] for p in parts], axis=-1)
    cw_full = jnp.concatenate([p[2] for p in parts], axis=-1)

    win = [jnp.pad(_cols_from_chips(g_in[l]), ((0, 0), (0, IN_PAD - IN_WIDTH))) for l in range(DEPTH)]
    wout = [g_out[l].reshape(d, d) for l in range(DEPTH)]
    up_a = [_cols_from_chips(g_up[l, 0:2]) for l in range(DEPTH)]
    up_g = [_cols_from_chips(g_up[l, 2:4]) for l in range(DEPTH)]
    down = [g_down[l].reshape(D_FF, d) for l in range(DEPTH)]
    w2p = [jnp.pad(w2_full[l], ((0, 128 - GLA_RANK), (0, 0))).astype(BF16) for l in range(DEPTH)]
    return meta_full, win, wout, up_a, up_g, down, w2p, cw_full


def _local_step(x_rows, target_rows, meta_full, win, wout, up_a, up_g, down, w2p, cw_full, pre_mix_norm, gla_gate_b,
                ret_norm_w, gla_norm_w, post_mix_norm, pre_ffn_norm, ffn_conv_b, post_ffn_norm):
    d = D_MODEL
    lp = x_rows.shape[0] + FRONT + BACK
    row = lambda a, l: a[l][None, :]
    rtab = _ret_tables(lp)
    gtab = _gla_tables()
    h0 = jnp.concatenate([jnp.zeros((PADF, d), F32), meta_full, x_rows, jnp.zeros((BACK, d), F32)], axis=0)
    target = jnp.pad(target_rows, ((FRONT, BACK), (0, 0)))

    saved = []
    h = h0
    _, hn = _resid_norm(h0, None, None, row(pre_mix_norm, 0), name="norm_in")
    loss_local = dy = None
    for l in range(DEPTH):
        s = {"h_in": h, "hn": hn}
        s["proj"] = _mm(hn, win[l], name="proj")
        s["o_ret"], s["st_ret"] = _retention(s["proj"], rtab, name="retention")
        s["o_gla"], s["st_gla"] = _gla(s["proj"], w2p[l], row(gla_gate_b, l), gtab, name="gla")
        s["merged"] = _merge(s["o_ret"], s["o_gla"], s["proj"], row(ret_norm_w, l), row(gla_norm_w, l), name="merge")
        s["m"] = _mm(s["merged"], wout[l], name="mix_out")
        s["h_mid"], s["hn2"] = _resid_norm(h, s["m"], row(post_mix_norm, l), row(pre_ffn_norm, l), name="resid_mix")
        s["ua"] = _mm(s["hn2"], up_a[l], name="ffn_up_a")
        s["ug"] = _mm(s["hn2"], up_g[l], name="ffn_up_g")
        cw_a, cw_g = cw_full[l][:, :D_FF], cw_full[l][:, D_FF:]
        cb_a, cb_g = ffn_conv_b[l][None, :D_FF], ffn_conv_b[l][None, D_FF:]
        s["conv"] = (cw_a, cw_g, cb_a, cb_g)
        s["act"] = _conv_act(s["ua"], s["ug"], cw_a, cw_g, cb_a, cb_g, name="conv_act")
        s["f"] = _mm(s["act"], down[l], name="ffn_down")
        if l + 1 < DEPTH:
            h, hn = _resid_norm(s["h_mid"], s["f"], row(post_ffn_norm, l), row(pre_mix_norm, l + 1), name="resid_ffn")
        else:
            loss_local, dy = _loss_head(s["h_mid"], s["f"], row(post_ffn_norm, l), target, name="loss_head")
        saved.append(s)

    g = {k: [None] * DEPTH for k in ("pre_mix", "w_in", "w2", "gb", "ret_n", "gla_n", "w_out", "post_mix", "pre_ffn",
                                     "up_a", "up_g", "cw", "cb", "down", "post_ffn")}
    dh_out, dhn_next = dy, None
    for l in reversed(range(DEPTH)):
        s = saved[l]
        cw_a, cw_g, cb_a, cb_g = s["conv"]
        if l + 1 < DEPTH:
            dh, df, g["pre_mix"][l + 1], g["post_ffn"][l] = _resid_norm_bwd(
                dh_out, dhn_next, saved[l + 1]["h_in"], s["f"], row(pre_mix_norm, l + 1), row(post_ffn_norm, l),
                name="resid_ffn_bwd")
        else:
            dh, df, _, g["post_ffn"][l] = _resid_norm_bwd(dh_out, None, None, s["f"], None, row(post_ffn_norm, l),
                                                          name="loss_head_bwd")
        dact = _mm(df, down[l], nt=True, name="ffn_down_dx")
        g["down"][l] = _mm_tn(s["act"], df, tn=512, name="ffn_down_dw")
        du_a, du_g, dcw_a, dcw_g, dcb_a, dcb_g = _conv_act_bwd(s["ua"], s["ug"], dact, cw_a, cw_g, cb_a, cb_g,
                                                               name="conv_act_bwd")
        g["cw"][l] = jnp.concatenate([dcw_a, dcw_g], axis=1)
        g["cb"][l] = jnp.concatenate([dcb_a, dcb_g], axis=1)[0]
        g["up_a"][l] = _mm_tn(s["hn2"], du_a, tn=1408, name="ffn_up_a_dw")
        g["up_g"][l] = _mm_tn(s["hn2"], du_g, tn=1408, name="ffn_up_g_dw")
        dhn2 = _mm(du_a, up_a[l], nt=True, name="ffn_up_a_dx")
        dhn2 = _mm(du_g, up_g[l], nt=True, add=dhn2, name="ffn_up_g_dx")
        dh, dm, g["pre_ffn"][l], g["post_mix"][l] = _resid_norm_bwd(
            dh, dhn2, s["h_mid"], s["m"], row(pre_ffn_norm, l), row(post_mix_norm, l), name="resid_mix_bwd")
        g["w_out"][l] = _mm_tn(s["merged"], dm, name="mix_out_dw")
        dmerged = _mm(dm, wout[l], nt=True, name="mix_out_dx")
        do_ret, do_gla, drg, dgr, g["ret_n"][l], g["gla_n"][l] = _merge_bwd(
            dmerged, s["o_ret"], s["o_gla"], s["proj"], row(ret_norm_w, l), row(gla_norm_w, l), name="merge_bwd")
        drq, drk, drv = _retention_bwd(s["proj"], do_ret, s["st_ret"], rtab, name="retention_bwd")
        dgq, dgk, dgv, dga, dw2, dgb = _gla_bwd(s["proj"], do_gla, s["st_gla"], w2p[l], row(gla_gate_b, l), gtab,
                                                name="gla_bwd")
        g["w2"][l], g["gb"][l] = dw2[:GLA_RANK], dgb[0]
        dproj = jnp.concatenate([drq, drk, drv, drg, dgq, dgk, dgv, dgr, dga,
                                 jnp.zeros((lp, IN_PAD - C_GA - 128), BF16)], axis=1)
        g["w_in"][l] = _mm_tn(s["hn"], dproj, tn=1280, name="proj_dw")
        dhn_next = _mm(dproj, win[l], nt=True, name="proj_dx")
        dh_out = dh
    dh0, _, g["pre_mix"][0], _ = _resid_norm_bwd(dh_out, dhn_next, h0, None, row(pre_mix_norm, 0), None,
                                                 name="norm_in_bwd")
    return loss_local, dh0, g


def kernel(x, meta_tokens, pre_mix_norm, w_in, gla_gate_w2, gla_gate_b, ret_norm_w, gla_norm_w, w_out, post_mix_norm, pre_ffn_norm, ffn_up, ffn_conv_w, ffn_conv_b, ffn_down, post_ffn_norm, loss_target, m_meta_tokens, m_pre_mix_norm, m_w_in, m_gla_gate_w2, m_gla_gate_b, m_ret_norm_w, m_gla_norm_w, m_w_out, m_post_mix_norm, m_pre_ffn_norm, m_ffn_up, m_ffn_conv_w, m_ffn_conv_b, m_ffn_down, m_post_ffn_norm, v_meta_tokens, v_pre_mix_norm, v_w_in, v_gla_gate_w2, v_gla_gate_b, v_ret_norm_w, v_gla_norm_w, v_w_out, v_post_mix_norm, v_pre_ffn_norm, v_ffn_up, v_ffn_conv_w, v_ffn_conv_b, v_ffn_down, v_post_ffn_norm):
    xi, yi, ci = _place()
    chip = 2 * xi + yi
    seq = x.shape[1]
    d = D_MODEL
    meta_full, win, wout, up_a, up_g, down, w2p, cw_full = _gather_params(
        w_in, w_out, ffn_up, ffn_down, meta_tokens, gla_gate_w2, ffn_conv_w)
    loss_local, dh0, g = _local_step(x[0], loss_target[0], meta_full, win, wout, up_a, up_g, down, w2p, cw_full,
                                     pre_mix_norm, gla_gate_b, ret_norm_w, gla_norm_w, post_mix_norm, pre_ffn_norm,
                                     ffn_conv_b, post_ffn_norm)
    grad_x = dh0[FRONT:FRONT + seq][None]

    big = [
        jnp.stack([_cols_to_chips(g["w_in"][l][:, :IN_WIDTH], 4) for l in range(DEPTH)]),
        jnp.stack([g["w_out"][l].reshape(4, d // 4, d) for l in range(DEPTH)]),
        jnp.stack([jnp.concatenate([_cols_to_chips(g["up_a"][l], 2), _cols_to_chips(g["up_g"][l], 2)], axis=0)
                   for l in range(DEPTH)]),
        jnp.stack([g["down"][l].reshape(4, D_FF // 4, d) for l in range(DEPTH)]),
    ]
    from_sib = _pair_exchange(big, name="grads_pair_exchange")
    mine = [lax.dynamic_index_in_dim(a, ci, 0, keepdims=False) for a in big]
    names = ("w_in", "w_out", "ffn_up", "ffn_down")
    chip_sums = [_add2(a, b, name=f"pair_sum_{nm}") for a, b, nm in zip(mine, from_sib, names)]
    slots = _chip_exchange(chip_sums, name="grads_chip_exchange")
    layer_grads = [_sum_slots(q, name=f"chip_sum_{nm}") for q, nm in zip(slots, names)]
    g_w_in, g_w_out, g_ffn_up, g_ffn_down = _pair_share(layer_grads, name="grads_pair_share")

    small_full = [dh0[PADF:FRONT], jnp.stack(g["pre_mix"])[:, 0], jnp.stack(g["w2"]), jnp.stack(g["gb"]),
                  jnp.stack(g["ret_n"])[:, 0], jnp.stack(g["gla_n"])[:, 0], jnp.stack(g["post_mix"])[:, 0],
                  jnp.stack(g["pre_ffn"])[:, 0], jnp.stack(g["cw"]), jnp.stack(g["cb"]),
                  jnp.stack(g["post_ffn"])[:, 0]]
    small_sum = _unslab(_allreduce_small(_slab(small_full, 8), name="small_allreduce"), [a.shape for a in small_full])
    (g_meta, g_pre_mix, g_w2, g_gb, g_ret_n, g_gla_n, g_post_mix, g_pre_ffn, g_cw, g_cb, g_post_ffn) = small_sum
    g_meta = lax.dynamic_slice_in_dim(g_meta, chip * 256, 256, axis=1)
    g_w2 = lax.dynamic_slice_in_dim(g_w2, chip * 64, 64, axis=2)
    g_cw = lax.dynamic_slice_in_dim(g_cw, chip * 1408, 1408, axis=2)

    grads = [g_meta, g_pre_mix, g_w_in, g_w2, g_gb, g_ret_n, g_gla_n, g_w_out, g_post_mix, g_pre_ffn, g_ffn_up,
             g_cw, g_cb, g_ffn_down, g_post_ffn]
    ws = [meta_tokens, pre_mix_norm, w_in, gla_gate_w2, gla_gate_b, ret_norm_w, gla_norm_w, w_out, post_mix_norm,
          pre_ffn_norm, ffn_up, ffn_conv_w, ffn_conv_b, ffn_down, post_ffn_norm]
    ms = [m_meta_tokens, m_pre_mix_norm, m_w_in, m_gla_gate_w2, m_gla_gate_b, m_ret_norm_w, m_gla_norm_w, m_w_out,
          m_post_mix_norm, m_pre_ffn_norm, m_ffn_up, m_ffn_conv_w, m_ffn_conv_b, m_ffn_down, m_post_ffn_norm]
    vs = [v_meta_tokens, v_pre_mix_norm, v_w_in, v_gla_gate_w2, v_gla_gate_b, v_ret_norm_w, v_gla_norm_w, v_w_out,
          v_post_mix_norm, v_pre_ffn_norm, v_ffn_up, v_ffn_conv_w, v_ffn_conv_b, v_ffn_down, v_post_ffn_norm]
    big_idx = (2, 7, 10, 13)
    deltas, new_m, new_v = [None] * 15, [None] * 15, [None] * 15
    for i, nm in zip(big_idx, names):
        deltas[i], new_m[i], new_v[i] = _adamw(ws[i], grads[i], ms[i], vs[i], name=f"adamw_{nm}")
    small_idx = [i for i in range(15) if i not in big_idx]
    shapes = [ws[i].shape for i in small_idx]
    sd, sm, sv = _adamw(_slab([ws[i] for i in small_idx], 8), _slab([grads[i] for i in small_idx], 8),
                        _slab([ms[i] for i in small_idx], 8), _slab([vs[i] for i in small_idx], 8), name="adamw_small")
    for i, a, b, c_ in zip(small_idx, _unslab(sd, shapes), _unslab(sm, shapes), _unslab(sv, shapes)):
        deltas[i], new_m[i], new_v[i] = a, b, c_

    loss = lax.psum(loss_local, ("x", "y", "c"))
    return (loss, grad_x, *grads, *deltas, *new_m, *new_v)
```

```python
import functools
import math

import numpy as np
import jax
import jax.numpy as jnp
from jax import lax
from jax.experimental import pallas as pl
from jax.experimental.pallas import tpu as pltpu

F32 = jnp.float32
BF16 = jnp.bfloat16

D_MODEL = 1024
DEPTH = 2
N_META = 16
EPS = 1e-6
RET_HEADS = 4
RET_DK = 128
GLA_HEADS = 4
GLA_DK = 64
GLA_DV = 128
GLA_QK = GLA_HEADS * GLA_DK
GLA_V = GLA_HEADS * GLA_DV
GLA_RANK = 16
GLA_TAU = 16.0
D_FF = 2816
ROPE_BASE = 10000.0
IN_WIDTH = 3600
IN_PAD = 3840
C_RQ, C_RK, C_RV, C_RG, C_GQ, C_GK, C_GV, C_GR, C_GA = 0, 512, 1024, 1536, 2048, 2304, 2560, 3072, 3584

FRONT = 64
BACK = 64
PADF = FRONT - N_META
RET_CHUNK = 128
GLA_CHUNK = 64
GLA_SUB = 16
BLK = 640

ADAM_LR, ADAM_B1, ADAM_B2, ADAM_EPS, ADAM_WD, ADAM_STEP = 0.001, 0.9, 0.999, 1e-08, 0.01, 10

VMEM_LIMIT = 56 * 2 ** 20
MESH = pl.DeviceIdType.MESH


def _cp(*sem):
    return pltpu.CompilerParams(dimension_semantics=sem, vmem_limit_bytes=VMEM_LIMIT)


def _tile(n, cands):
    for t in cands:
        if n % t == 0:
            return t
    raise ValueError(f"no tile for {n} in {cands}")


def _row_tile(n):
    return _tile(n, (640, 512, 320, 256, 128, 64))


def _mm(a, b, *, nt=False, add=None, out_dtype=F32, tn=None, name):
    m, k = a.shape
    n = b.shape[0] if nt else b.shape[1]
    tm = _tile(m, (320, 256, 128, 64))
    tn = n if tn is None else tn
    dn = (((1,), (1,)), ((), ())) if nt else (((1,), (0,)), ((), ()))

    def body(*refs):
        if add is None:
            a_ref, b_ref, o_ref = refs
        else:
            a_ref, b_ref, c_ref, o_ref = refs
        r = lax.dot_general(a_ref[...].astype(BF16), b_ref[...].astype(BF16), dn, preferred_element_type=F32)
        if add is not None:
            r = r + c_ref[...]
        o_ref[...] = r.astype(o_ref.dtype)

    b_spec = pl.BlockSpec((tn, k), lambda j, i: (j, 0)) if nt else pl.BlockSpec((k, tn), lambda j, i: (0, j))
    in_specs = [pl.BlockSpec((tm, k), lambda j, i: (i, 0)), b_spec]
    args = [a, b]
    if add is not None:
        in_specs.append(pl.BlockSpec((tm, tn), lambda j, i: (i, j)))
        args.append(add)
    return pl.pallas_call(
        body, out_shape=jax.ShapeDtypeStruct((m, n), out_dtype), grid=(n // tn, m // tm),
        in_specs=in_specs, out_specs=pl.BlockSpec((tm, tn), lambda j, i: (i, j)),
        compiler_params=_cp("parallel", "parallel"), name=name)(*args)


def _mm_tn(a, b, *, tn=None, name):
    m, k = a.shape
    n = b.shape[1]
    tm = _tile(m, (1664, 640, 320, 256, 128, 64))
    tn = n if tn is None else tn

    def body(a_ref, b_ref, o_ref):
        @pl.when(pl.program_id(1) == 0)
        def _():
            o_ref[...] = jnp.zeros_like(o_ref)
        o_ref[...] += lax.dot_general(a_ref[...].astype(BF16), b_ref[...].astype(BF16),
                                      (((0,), (0,)), ((), ())), preferred_element_type=F32)

    return pl.pallas_call(
        body, out_shape=jax.ShapeDtypeStruct((k, n), F32), grid=(n // tn, m // tm),
        in_specs=[pl.BlockSpec((tm, k), lambda j, i: (i, 0)), pl.BlockSpec((tm, tn), lambda j, i: (i, j))],
        out_specs=pl.BlockSpec((k, tn), lambda j, i: (0, j)),
        compiler_params=_cp("parallel", "arbitrary"), name=name)(a, b)


def _rms(x, w):
    r = lax.rsqrt(jnp.mean(x * x, axis=-1, keepdims=True) + EPS)
    return x * r * w


def _rms_bwd(x, w, dy):
    r = lax.rsqrt(jnp.mean(x * x, axis=-1, keepdims=True) + EPS)
    xh = x * r
    dxh = dy * w
    dx = r * (dxh - xh * jnp.mean(dxh * xh, axis=-1, keepdims=True))
    return dx, jnp.sum(dy * xh, axis=0, keepdims=True)


def _resid_norm(h, t, w_post, w_next, *, name):
    lp, d = h.shape
    tm = _row_tile(lp)
    has_t = t is not None

    def body(*refs):
        if has_t:
            h_ref, t_ref, wp_ref, wn_ref, ho_ref, hn_ref = refs
            hv = h_ref[...] + _rms(t_ref[...], wp_ref[...])
            ho_ref[...] = hv
        else:
            h_ref, wn_ref, hn_ref = refs
            hv = h_ref[...]
        hn_ref[...] = _rms(hv, wn_ref[...]).astype(BF16)

    row = pl.BlockSpec((tm, d), lambda i: (i, 0))
    vec = pl.BlockSpec((1, d), lambda i: (0, 0))
    if has_t:
        return pl.pallas_call(
            body, out_shape=(jax.ShapeDtypeStruct((lp, d), F32), jax.ShapeDtypeStruct((lp, d), BF16)),
            grid=(lp // tm,), in_specs=[row, row, vec, vec], out_specs=(row, row),
            compiler_params=_cp("parallel"), name=name)(h, t, w_post, w_next)
    return h, pl.pallas_call(
        body, out_shape=jax.ShapeDtypeStruct((lp, d), BF16), grid=(lp // tm,), in_specs=[row, vec],
        out_specs=row, compiler_params=_cp("parallel"), name=name)(h, w_next)


def _resid_norm_bwd(dh_out, dhn, h_new, t, w_next, w_post, *, name):
    lp, d = h_new.shape if h_new is not None else t.shape
    tm = _row_tile(lp)
    has_n = dhn is not None
    has_t = t is not None

    def body(*refs):
        refs = list(refs)
        dho_ref = refs.pop(0)
        if has_n:
            dhn_ref, hn_ref, wn_ref = refs.pop(0), refs.pop(0), refs.pop(0)
        if has_t:
            t_ref, wp_ref = refs.pop(0), refs.pop(0)
        dh_ref = refs.pop(0) if has_n else None
        dt_ref = refs.pop(0) if has_t else None
        dwn_ref = refs.pop(0) if has_n else None
        dwp_ref = refs.pop(0) if has_t else None
        first = pl.program_id(0) == 0
        dh = dho_ref[...]
        if has_n:
            dx, dwn = _rms_bwd(hn_ref[...], wn_ref[...], dhn_ref[...])
            dh = dh + dx
            dh_ref[...] = dh

            @pl.when(first)
            def _():
                dwn_ref[...] = jnp.zeros_like(dwn_ref)
            dwn_ref[...] += dwn
        if has_t:
            dt, dwp = _rms_bwd(t_ref[...], wp_ref[...], dh)
            dt_ref[...] = dt.astype(BF16)

            @pl.when(first)
            def _():
                dwp_ref[...] = jnp.zeros_like(dwp_ref)
            dwp_ref[...] += dwp

    row = pl.BlockSpec((tm, d), lambda i: (i, 0))
    vec = pl.BlockSpec((1, d), lambda i: (0, 0))
    args, in_specs, out_shape, out_specs = [dh_out], [row], [], []
    if has_n:
        args += [dhn, h_new, w_next]
        in_specs += [row, row, vec]
    if has_t:
        args += [t, w_post]
        in_specs += [row, vec]
    if has_n:
        out_shape.append(jax.ShapeDtypeStruct((lp, d), F32)); out_specs.append(row)
    if has_t:
        out_shape.append(jax.ShapeDtypeStruct((lp, d), BF16)); out_specs.append(row)
    if has_n:
        out_shape.append(jax.ShapeDtypeStruct((1, d), F32)); out_specs.append(vec)
    if has_t:
        out_shape.append(jax.ShapeDtypeStruct((1, d), F32)); out_specs.append(vec)
    outs = list(pl.pallas_call(body, out_shape=tuple(out_shape), grid=(lp // tm,), in_specs=in_specs,
                               out_specs=tuple(out_specs), compiler_params=_cp("arbitrary"), name=name)(*args))
    dh = outs.pop(0) if has_n else dh_out
    dt = outs.pop(0) if has_t else None
    dwn = outs.pop(0) if has_n else None
    dwp = outs.pop(0) if has_t else None
    return dh, dt, dwn, dwp


def _loss_head(h, f, w_post, target, *, name):
    lp, d = h.shape
    tm = _row_tile(lp)

    def body(h_ref, f_ref, w_ref, t_ref, loss_ref, dy_ref):
        i = pl.program_id(0)
        y = h_ref[...] + _rms(f_ref[...], w_ref[...])
        rows = i * tm + lax.broadcasted_iota(jnp.int32, (tm, 1), 0)
        tok = (rows >= FRONT) & (rows < lp - BACK)
        err = jnp.where(tok, y - t_ref[...], 0.0)
        dy_ref[...] = err * (1.0 / d)

        @pl.when(i == 0)
        def _():
            loss_ref[...] = jnp.zeros_like(loss_ref)
        part = jnp.sum(jnp.sum(err * err, axis=1, keepdims=True), axis=0, keepdims=True) * (0.5 / d)
        loss_ref[...] += jnp.broadcast_to(part, loss_ref.shape)

    row = pl.BlockSpec((tm, d), lambda i: (i, 0))
    loss, dy = pl.pallas_call(
        body, out_shape=(jax.ShapeDtypeStruct((8, 128), F32), jax.ShapeDtypeStruct((lp, d), F32)),
        grid=(lp // tm,), in_specs=[row, row, pl.BlockSpec((1, d), lambda i: (0, 0)), row],
        out_specs=(pl.BlockSpec((8, 128), lambda i: (0, 0)), row),
        compiler_params=_cp("arbitrary"), name=name)(h, f, w_post, target)
    return loss[0, 0], dy


_GELU_C = math.sqrt(2.0 / math.pi)


def _gelu_and_grad(a):
    a2 = a * a
    t = jnp.tanh(a * (_GELU_C + (_GELU_C * 0.044715) * a2))
    ha = 0.5 * a
    h1 = 0.5 + 0.5 * t
    return a * h1, h1 + ha * (1.0 - t * t) * (_GELU_C + (3.0 * _GELU_C * 0.044715) * a2)


def _gelu(a):
    t = jnp.tanh(a * (_GELU_C + (_GELU_C * 0.044715) * (a * a)))
    return a * (0.5 + 0.5 * t)


def _conv3(x, n, w, b):
    tot = x.shape[0]
    x1 = pltpu.roll(x, 1, 0)
    x2 = pltpu.roll(x, 2, 0)
    return (b + x[8:8 + n] * w[2:3] + x1[8:8 + n] * w[1:2] + x2[8:8 + n] * w[0:1]), x1[8:8 + n], x2[8:8 + n]


def _conv_act(ua, ug, wa, wg, ba, bg, *, name):
    lp, n = ua.shape
    tm = _row_tile(lp)
    tc = _tile(n, (256, 128))
    nb8 = tm // 8

    def body(ua_ref, uap_ref, ug_ref, ugp_ref, wa_ref, wg_ref, ba_ref, bg_ref, o_ref):
        i = pl.program_id(0)
        grow = i * tm - 8 + lax.broadcasted_iota(jnp.int32, (tm + 8, 1), 0)
        xa = jnp.where(grow >= 0, jnp.concatenate([uap_ref[...], ua_ref[...]], axis=0), 0.0)
        xg = jnp.where(grow >= 0, jnp.concatenate([ugp_ref[...], ug_ref[...]], axis=0), 0.0)
        ca, _, _ = _conv3(xa, tm, wa_ref[...], ba_ref[...])
        cg, _, _ = _conv3(xg, tm, wg_ref[...], bg_ref[...])
        rows = grow[8:]
        ok = (rows >= PADF) & (rows < lp - BACK)
        o_ref[...] = jnp.where(ok, _gelu(ca) * cg, 0.0).astype(BF16)

    cur = pl.BlockSpec((tm, tc), lambda i, j: (i, j))
    prev = pl.BlockSpec((8, tc), lambda i, j: (jnp.maximum(i * nb8 - 1, 0), j))
    w3 = pl.BlockSpec((3, tc), lambda i, j: (0, j))
    b1 = pl.BlockSpec((1, tc), lambda i, j: (0, j))
    return pl.pallas_call(
        body, out_shape=jax.ShapeDtypeStruct((lp, n), BF16), grid=(lp // tm, n // tc),
        in_specs=[cur, prev, cur, prev, w3, w3, b1, b1], out_specs=cur,
        compiler_params=_cp("parallel", "parallel"), name=name)(ua, ua, ug, ug, wa, wg, ba, bg)


def _conv_act_bwd(ua, ug, dact, wa, wg, ba, bg, *, name):
    lp, n = ua.shape
    tm = _row_tile(lp)
    tc = _tile(n, (256, 128))
    nb8 = tm // 8
    last8 = lp // 8 - 1
    ext = tm + 8

    def body(ua_ref, uap_ref, uan_ref, ug_ref, ugp_ref, ugn_ref, da_ref, dan_ref, wa_ref, wg_ref, ba_ref, bg_ref,
             dua_ref, dug_ref, dwa_ref, dwg_ref, dba_ref, dbg_ref):
        i = pl.program_id(1)
        grow = i * tm - 8 + lax.broadcasted_iota(jnp.int32, (tm + 16, 1), 0)
        inside = (grow >= 0) & (grow < lp)
        xa = jnp.where(inside, jnp.concatenate([uap_ref[...], ua_ref[...], uan_ref[...]], axis=0), 0.0)
        xg = jnp.where(inside, jnp.concatenate([ugp_ref[...], ug_ref[...], ugn_ref[...]], axis=0), 0.0)
        wa, wg = wa_ref[...], wg_ref[...]
        ca, xa1, xa2 = _conv3(xa, ext, wa, ba_ref[...])
        cg, xg1, xg2 = _conv3(xg, ext, wg, bg_ref[...])
        rows = grow[8:]
        ok = (rows >= PADF) & (rows < lp - BACK)
        dact_e = jnp.where(ok, jnp.concatenate([da_ref[...], dan_ref[...]], axis=0), 0.0)
        gel, gel_d = _gelu_and_grad(ca)
        dca = dact_e * cg * gel_d
        dcg = dact_e * gel

        def back(dc, w):
            return (dc[:tm] * w[2:3] + pltpu.roll(dc, ext - 1, 0)[:tm] * w[1:2]
                    + pltpu.roll(dc, ext - 2, 0)[:tm] * w[0:1])

        dua_ref[...] = back(dca, wa).astype(BF16)
        dug_ref[...] = back(dcg, wg).astype(BF16)

        @pl.when(i == 0)
        def _():
            dwa_ref[...] = jnp.zeros_like(dwa_ref)
            dwg_ref[...] = jnp.zeros_like(dwg_ref)
            dba_ref[...] = jnp.zeros_like(dba_ref)
            dbg_ref[...] = jnp.zeros_like(dbg_ref)

        def wsum(dw_ref, db_ref, dc, x, x1, x2):
            d = dc[:tm]
            s = lambda v: jnp.sum(v, axis=0, keepdims=True)
            dw_ref[0:1, :] += s(d * x2[:tm])
            dw_ref[1:2, :] += s(d * x1[:tm])
            dw_ref[2:3, :] += s(d * x[8:8 + tm])
            db_ref[...] += s(d)

        wsum(dwa_ref, dba_ref, dca, xa, xa1, xa2)
        wsum(dwg_ref, dbg_ref, dcg, xg, xg1, xg2)

    cur = pl.BlockSpec((tm, tc), lambda j, i: (i, j))
    prev = pl.BlockSpec((8, tc), lambda j, i: (jnp.maximum(i * nb8 - 1, 0), j))
    nxt = pl.BlockSpec((8, tc), lambda j, i: (jnp.minimum((i + 1) * nb8, last8), j))
    w3 = pl.BlockSpec((3, tc), lambda j, i: (0, j))
    b1 = pl.BlockSpec((1, tc), lambda j, i: (0, j))
    return pl.pallas_call(
        body,
        out_shape=(jax.ShapeDtypeStruct((lp, n), BF16), jax.ShapeDtypeStruct((lp, n), BF16),
                   jax.ShapeDtypeStruct((3, n), F32), jax.ShapeDtypeStruct((3, n), F32),
                   jax.ShapeDtypeStruct((1, n), F32), jax.ShapeDtypeStruct((1, n), F32)),
        grid=(n // tc, lp // tm),
        in_specs=[cur, prev, nxt, cur, prev, nxt, cur, nxt, w3, w3, b1, b1],
        out_specs=(cur, cur, w3, w3, b1, b1),
        compiler_params=_cp("parallel", "arbitrary"), name=name)(ua, ua, ua, ug, ug, ug, dact, dact, wa, wg, ba, bg)


def _sigmoid(x):
    return 1.0 / (1.0 + jnp.exp(-x))


def _merge(o_ret, o_gla, proj, w_ret, w_gla, *, name):
    lp = o_ret.shape[0]
    tm = _row_tile(lp)

    def body(or_ref, og_ref, rg_ref, gr_ref, wr_ref, wg_ref, m_ref):
        oret, ogla = or_ref[...], og_ref[...]
        yr, yg = [], []
        for h in range(4):
            hs = slice(128 * h, 128 * h + 128)
            o = oret[:, hs]
            xc = o - jnp.mean(o, axis=-1, keepdims=True)
            yr.append(xc * lax.rsqrt(jnp.mean(xc * xc, axis=-1, keepdims=True) + EPS))
            o = ogla[:, hs]
            yg.append(o * lax.rsqrt(jnp.mean(o * o, axis=-1, keepdims=True) + EPS))
        rg, gr = rg_ref[...], gr_ref[...]
        m_ref[:, 0:512] = (jnp.concatenate(yr, axis=1) * wr_ref[...] * (rg * _sigmoid(rg))).astype(BF16)
        m_ref[:, 512:1024] = (jnp.concatenate(yg, axis=1) * wg_ref[...] * (gr * _sigmoid(gr))).astype(BF16)

    row = pl.BlockSpec((tm, 512), lambda i: (i, 0))
    vec = pl.BlockSpec((1, 512), lambda i: (0, 0))
    return pl.pallas_call(
        body, out_shape=jax.ShapeDtypeStruct((lp, 1024), BF16), grid=(lp // tm,),
        in_specs=[row, row, pl.BlockSpec((tm, 512), lambda i: (i, C_RG // 512)),
                  pl.BlockSpec((tm, 512), lambda i: (i, C_GR // 512)), vec, vec],
        out_specs=pl.BlockSpec((tm, 1024), lambda i: (i, 0)),
        compiler_params=_cp("parallel"), name=name)(o_ret, o_gla, proj, proj, w_ret, w_gla)


def _merge_bwd(dm, o_ret, o_gla, proj, w_ret, w_gla, *, name):
    lp = o_ret.shape[0]
    tm = _row_tile(lp)

    def body(dm_ref, or_ref, og_ref, rg_ref, gr_ref, wr_ref, wg_ref, dor_ref, dog_ref, drg_ref, dgr_ref, dwr_ref, dwg_ref):
        @pl.when(pl.program_id(0) == 0)
        def _():
            dwr_ref[...] = jnp.zeros_like(dwr_ref)
            dwg_ref[...] = jnp.zeros_like(dwg_ref)

        def group(d, o_all, gate, w, center):
            sg = _sigmoid(gate)
            s = gate * sg
            ds = sg * (1.0 + gate * (1.0 - sg))
            xh, rr = [], []
            for h in range(4):
                o = o_all[:, 128 * h:128 * h + 128]
                if center:
                    o = o - jnp.mean(o, axis=-1, keepdims=True)
                r = lax.rsqrt(jnp.mean(o * o, axis=-1, keepdims=True) + EPS)
                xh.append(o * r)
                rr.append(r)
            xh_all = jnp.concatenate(xh, axis=1)
            dgate = d * xh_all * w * ds
            dw = jnp.sum(d * xh_all * s, axis=0, keepdims=True)
            dxh_all = d * w * s
            do = []
            for h in range(4):
                dxh = dxh_all[:, 128 * h:128 * h + 128]
                t = dxh - xh[h] * jnp.mean(dxh * xh[h], axis=-1, keepdims=True)
                if center:
                    t = t - jnp.mean(dxh, axis=-1, keepdims=True)
                do.append(rr[h] * t)
            return jnp.concatenate(do, axis=1), dgate, dw

        dmv = dm_ref[...]
        do, dg, dw = group(dmv[:, 0:512], or_ref[...], rg_ref[...], wr_ref[...], True)
        dor_ref[...] = do
        drg_ref[...] = dg.astype(BF16)
        dwr_ref[...] += dw
        do, dg, dw = group(dmv[:, 512:1024], og_ref[...], gr_ref[...], wg_ref[...], False)
        dog_ref[...] = do
        dgr_ref[...] = dg.astype(BF16)
        dwg_ref[...] += dw

    row = pl.BlockSpec((tm, 512), lambda i: (i, 0))
    vec = pl.BlockSpec((1, 512), lambda i: (0, 0))
    return pl.pallas_call(
        body,
        out_shape=(jax.ShapeDtypeStruct((lp, 512), F32), jax.ShapeDtypeStruct((lp, 512), F32),
                   jax.ShapeDtypeStruct((lp, 512), BF16), jax.ShapeDtypeStruct((lp, 512), BF16),
                   jax.ShapeDtypeStruct((1, 512), F32), jax.ShapeDtypeStruct((1, 512), F32)),
        grid=(lp // tm,),
        in_specs=[pl.BlockSpec((tm, 1024), lambda i: (i, 0)), row, row,
                  pl.BlockSpec((tm, 512), lambda i: (i, C_RG // 512)),
                  pl.BlockSpec((tm, 512), lambda i: (i, C_GR // 512)), vec, vec],
        out_specs=(row, row, row, row, vec, vec),
        compiler_params=_cp("arbitrary"), name=name)(dm, o_ret, o_gla, proj, proj, w_ret, w_gla)


def _dot(a, b):
    return lax.dot_general(a, b, (((1,), (0,)), ((), ())), preferred_element_type=F32)


def _dot_nt(a, b):
    return lax.dot_general(a, b, (((1,), (1,)), ((), ())), preferred_element_type=F32)


def _dot_tn(a, b):
    return lax.dot_general(a, b, (((0,), (0,)), ((), ())), preferred_element_type=F32)


def _ret_tables(lp):
    cr = RET_CHUNK
    pos = jnp.arange(lp, dtype=F32) - float(PADF)
    half = RET_DK // 2
    inv = ROPE_BASE ** (-jnp.arange(half, dtype=F32) / half)
    ang = pos[:, None] * inv[None, :]
    c, s = jnp.cos(ang), jnp.sin(ang)
    rope_c = jnp.concatenate([c, c], axis=1)
    rope_s = jnp.concatenate([-s, s], axis=1)
    log_g = np.log(1.0 - 2.0 ** (-5.0 - np.arange(RET_HEADS, dtype=np.float64)))
    idx = np.arange(cr, dtype=np.float64)
    diff = idx[:, None] - idx[None, :]
    dmat = np.where(diff >= 0, np.exp(log_g[:, None, None] * np.maximum(diff, 0.0)), 0.0)
    zeta = np.exp(log_g[:, None] * (cr - 1.0 - idx)[None, :])
    xi = np.exp(log_g[:, None] * (idx + 1.0)[None, :])
    gc = np.exp(log_g * cr)
    f = lambda a: jnp.asarray(a.astype(np.float32))
    return (rope_c, rope_s, f(dmat), f(np.broadcast_to(zeta[:, :, None], (RET_HEADS, cr, 128))),
            f(np.broadcast_to(xi[:, :, None], (RET_HEADS, cr, 128))),
            f(np.broadcast_to(gc[:, None, None], (RET_HEADS, 8, 128))))


def _rope(t, c, s):
    return t * c + pltpu.roll(t, 64, 1) * s


def _rope_t(d, c, s):
    return d * c + pltpu.roll(d * s, 64, 1)


def _ret_specs(fwd_blocks, nblk, rev):
    ix = (lambda i: nblk - 1 - i) if rev else (lambda i: i)
    cr = RET_CHUNK
    col = lambda base: pl.BlockSpec((BLK, 128), lambda h, i: (ix(i), base // 128 + h))
    tab = pl.BlockSpec((BLK, 128), lambda h, i: (ix(i), 0))
    sq = pl.BlockSpec((1, cr, cr), lambda h, i: (h, 0, 0))
    hv = pl.BlockSpec((1, cr, 128), lambda h, i: (h, 0, 0))
    g8 = pl.BlockSpec((1, 8, 128), lambda h, i: (h, 0, 0))
    st = pl.BlockSpec((1, BLK // cr, 128, 128), lambda h, i: (h, ix(i), 0, 0))
    out = pl.BlockSpec((BLK, 128), lambda h, i: (ix(i), h))
    return col, tab, sq, hv, g8, st, out


def _retention(proj, tables, *, name):
    lp = proj.shape[0]
    nblk, cr = lp // BLK, RET_CHUNK
    scale = RET_DK ** -0.5

    def body(q_ref, k_ref, v_ref, c_ref, s_ref, d_ref, z_ref, x_ref, g_ref, o_ref, st_ref, state):
        @pl.when(pl.program_id(1) == 0)
        def _():
            state[...] = jnp.zeros_like(state)
        dmat, zeta, xi, gc = d_ref[0], z_ref[0], x_ref[0], g_ref[0][0:1, :]

        def chunk(ci, carry):
            sl = pl.ds(pl.multiple_of(ci * cr, cr), cr)
            c, s = c_ref[sl, :], s_ref[sl, :]
            q = _rope(q_ref[sl, :], c, s)
            k = _rope(k_ref[sl, :], c, s) * scale
            qb, kb, vb = q.astype(BF16), k.astype(BF16), v_ref[sl, :].astype(BF16)
            st = state[...]
            st_ref[0, ci] = st
            sc = _dot_nt(qb, kb) * dmat
            o_ref[sl, :] = _dot(sc.astype(BF16), vb) + _dot(qb, st.astype(BF16)) * xi
            state[...] = st * gc + _dot_tn((k * zeta).astype(BF16), vb)
            return carry

        lax.fori_loop(0, BLK // cr, chunk, 0)

    col, tab, sq, hv, g8, st, out = _ret_specs(True, nblk, False)
    return pl.pallas_call(
        body,
        out_shape=(jax.ShapeDtypeStruct((lp, 512), F32), jax.ShapeDtypeStruct((4, lp // cr, 128, 128), F32)),
        grid=(4, nblk), in_specs=[col(C_RQ), col(C_RK), col(C_RV), tab, tab, sq, hv, hv, g8],
        out_specs=(out, st), scratch_shapes=[pltpu.VMEM((128, 128), F32)],
        compiler_params=_cp("parallel", "arbitrary"), name=name)(proj, proj, proj, *tables)


def _retention_bwd(proj, do, states, tables, *, name):
    lp = proj.shape[0]
    nblk, cr = lp // BLK, RET_CHUNK
    nch = BLK // cr
    scale = RET_DK ** -0.5

    def body(q_ref, k_ref, v_ref, do_ref, st_ref, c_ref, s_ref, d_ref, z_ref, x_ref, g_ref, dq_ref, dk_ref, dv_ref, dstate):
        @pl.when(pl.program_id(1) == 0)
        def _():
            dstate[...] = jnp.zeros_like(dstate)
        dmat, zeta, xi, gc = d_ref[0], z_ref[0], x_ref[0], g_ref[0][0:1, :]

        def chunk(cc, carry):
            ci = nch - 1 - cc
            sl = pl.ds(pl.multiple_of(ci * cr, cr), cr)
            c, s = c_ref[sl, :], s_ref[sl, :]
            q = _rope(q_ref[sl, :], c, s)
            k = _rope(k_ref[sl, :], c, s) * scale
            qb, kb, vb = q.astype(BF16), k.astype(BF16), v_ref[sl, :].astype(BF16)
            kzb = (k * zeta).astype(BF16)
            dov = do_ref[sl, :]
            dob, doxb = dov.astype(BF16), (dov * xi).astype(BF16)
            stb = st_ref[0, ci].astype(BF16)
            dsn = dstate[...]
            dsnb = dsn.astype(BF16)
            scb = (_dot_nt(qb, kb) * dmat).astype(BF16)
            dscb = (_dot_nt(dob, vb) * dmat).astype(BF16)
            dq = _dot(dscb, kb) + _dot_nt(doxb, stb)
            dk = _dot_tn(dscb, qb) + _dot_nt(vb, dsnb) * zeta
            dv = _dot_tn(scb, dob) + _dot(kzb, dsnb)
            dstate[...] = dsn * gc + _dot_tn(qb, doxb)
            dq_ref[sl, :] = _rope_t(dq, c, s).astype(BF16)
            dk_ref[sl, :] = _rope_t(dk * scale, c, s).astype(BF16)
            dv_ref[sl, :] = dv.astype(BF16)
            return carry

        lax.fori_loop(0, nch, chunk, 0)

    col, tab, sq, hv, g8, st, out = _ret_specs(False, nblk, True)
    o3 = jax.ShapeDtypeStruct((lp, 512), BF16)
    return pl.pallas_call(
        body, out_shape=(o3, o3, o3), grid=(4, nblk),
        in_specs=[col(C_RQ), col(C_RK), col(C_RV), out, st, tab, tab, sq, hv, hv, g8],
        out_specs=(out, out, out), scratch_shapes=[pltpu.VMEM((128, 128), F32)],
        compiler_params=_cp("parallel", "arbitrary"), name=name)(proj, proj, proj, do, states, *tables)


def _gla_tables():
    c = GLA_CHUNK
    tri = np.tril(np.ones((c, c), np.float32))
    ones_qv = np.kron(np.eye(GLA_HEADS, dtype=np.float32), np.ones((GLA_DK, GLA_DV), np.float32))
    return (jnp.asarray(tri, BF16), jnp.asarray(tri.T.copy(), BF16), jnp.asarray(ones_qv, BF16),
            jnp.asarray(ones_qv.T.copy(), BF16))


def _split3(x):
    hi = x.astype(BF16)
    r1 = x - hi.astype(F32)
    mid = r1.astype(BF16)
    lo = (r1 - mid.astype(F32)).astype(BF16)
    return hi, mid, lo


def _tri_sum(tri, x):
    hi, mid, lo = _split3(x)
    return _dot(tri, hi) + _dot(tri, mid) + _dot(tri, lo)


def _head_masks(width, per):
    lane = lax.broadcasted_iota(jnp.int32, (1, width), 1)
    return [((lane >= per * h) & (lane < per * (h + 1))).astype(F32) for h in range(GLA_HEADS)]


def _stack_heads(x, masks):
    return jnp.concatenate([x * m for m in masks], axis=0)


def _gla_gate(ga, w2, b, ok, tri):
    z = _dot(ga.astype(BF16), w2) + b
    la = (jnp.minimum(z, 0.0) - jnp.log(1.0 + jnp.exp(-jnp.abs(z)))) * (1.0 / GLA_TAU)
    la = jnp.where(ok, la, 0.0)
    return z, _tri_sum(tri, la)


def _gla_rows(i_blk, ci, lp):
    c = GLA_CHUNK
    rows = i_blk * BLK + ci * c + lax.broadcasted_iota(jnp.int32, (c, 1), 0)
    return (rows >= PADF) & (rows < lp - BACK)


def _gla_off_parts(a, qs, k, g, hm_q):
    s = GLA_SUB
    ra = g[s * a - 1:s * a, :]
    ga_ = g[s * a:s * a + s, :]
    eq = jnp.exp(ga_ - ra)
    ek = jnp.exp(jnp.minimum(ra - g, 0.0))
    qh = qs[s * a:s * a + s, :] * eq
    kh = k * ek
    qst = _stack_heads(qh, hm_q).astype(BF16)
    col = lax.broadcasted_iota(jnp.int32, (GLA_HEADS * s, GLA_CHUNK), 1)
    pmask = col < s * a
    p = jnp.where(pmask, _dot_nt(qst, kh.astype(BF16)), 0.0)
    return eq, ek, qh, kh, qst, pmask, p


def _lag_mask(j):
    r = lax.broadcasted_iota(jnp.int32, (GLA_CHUNK, 1), 0)
    return (jnp.bitwise_and(r, GLA_SUB - 1) >= j).astype(F32)


def _roll_rows(x, j):
    return x if j == 0 else pltpu.roll(x, j, 0)


def _gla(proj, w2p, b, tables, *, name):
    lp = proj.shape[0]
    nblk, c, s = lp // BLK, GLA_CHUNK, GLA_SUB
    nch = BLK // c
    na = c // s

    def body(q_ref, k_ref, v_ref, a_ref, w_ref, b_ref, tri_ref, ones_ref, o_ref, st_ref, state):
        i_blk = pl.program_id(0)

        @pl.when(i_blk == 0)
        def _():
            state[...] = jnp.zeros_like(state)
        hm_q = _head_masks(GLA_QK, GLA_DK)
        tri, ones_qv, w2, bias = tri_ref[...], ones_ref[...], w_ref[...], b_ref[...]

        def chunk(ci, carry):
            sl = pl.ds(pl.multiple_of(ci * c, c), c)
            ok = _gla_rows(i_blk, ci, lp)
            k, v = k_ref[sl, :], v_ref[sl, :]
            vb = v.astype(BF16)
            qs = q_ref[sl, :] * (GLA_DK ** -0.5)
            _, g = _gla_gate(a_ref[sl, :], w2, bias, ok, tri)
            last = g[c - 1:c, :]
            st = state[...]
            st_ref[ci] = st
            qst = _stack_heads(qs * jnp.exp(g), hm_q).astype(BF16)
            oi = _dot_nt(qst, st.astype(BF16))
            o = jnp.concatenate([oi[c * h:c * h + c, :] for h in range(GLA_HEADS)], axis=1)
            ke = k * jnp.exp(last - g)
            f = _dot_tn(vb, ke.astype(BF16))
            upd = f[0:GLA_DV, :] * hm_q[0]
            for h in range(1, GLA_HEADS):
                upd = upd + f[GLA_DV * h:GLA_DV * (h + 1), :] * hm_q[h]
            state[...] = st * jnp.exp(last) + upd
            off = [jnp.zeros((s, GLA_V), F32)]
            for a in range(1, na):
                p = _gla_off_parts(a, qs, k, g, hm_q)[-1]
                ob = _dot(p.astype(BF16), vb)
                off.append(jnp.concatenate(
                    [ob[s * h:s * h + s, GLA_DV * h:GLA_DV * (h + 1)] for h in range(GLA_HEADS)], axis=1))
            o = o + jnp.concatenate(off, axis=0)
            ws = []
            for j in range(s):
                ej = jnp.exp(jnp.minimum(g - _roll_rows(g, j), 0.0))
                ws.append((qs * _roll_rows(k, j) * ej * _lag_mask(j)).astype(BF16))
            ball = _dot(jnp.concatenate(ws, axis=0), ones_qv)
            for j in range(s):
                o = o + ball[c * j:c * j + c, :] * _roll_rows(v, j)
            o_ref[sl, :] = o
            return carry

        lax.fori_loop(0, nch, chunk, 0)

    tri, _, ones_qv, _ = tables
    full = lambda arr: pl.BlockSpec(arr.shape, lambda i: (0,) * arr.ndim)
    return pl.pallas_call(
        body,
        out_shape=(jax.ShapeDtypeStruct((lp, GLA_V), F32), jax.ShapeDtypeStruct((lp // c, GLA_DV, GLA_QK), F32)),
        grid=(nblk,),
        in_specs=[pl.BlockSpec((BLK, GLA_QK), lambda i: (i, C_GQ // GLA_QK)),
                  pl.BlockSpec((BLK, GLA_QK), lambda i: (i, C_GK // GLA_QK)),
                  pl.BlockSpec((BLK, GLA_V), lambda i: (i, C_GV // GLA_V)),
                  pl.BlockSpec((BLK, 128), lambda i: (i, C_GA // 128)),
                  full(w2p), full(b), full(tri), full(ones_qv)],
        out_specs=(pl.BlockSpec((BLK, GLA_V), lambda i: (i, 0)),
                   pl.BlockSpec((nch, GLA_DV, GLA_QK), lambda i: (i, 0, 0))),
        scratch_shapes=[pltpu.VMEM((GLA_DV, GLA_QK), F32)],
        compiler_params=_cp("arbitrary"), name=name)(proj, proj, proj, proj, w2p, b, tri, ones_qv)


def _gla_bwd(proj, do, states, w2p, b, tables, *, name):
    lp = proj.shape[0]
    nblk, c, s = lp // BLK, GLA_CHUNK, GLA_SUB
    nch = BLK // c
    na = c // s

    def body(q_ref, k_ref, v_ref, a_ref, do_ref, st_ref, w_ref, b_ref, tri_ref, trit_ref, ones_ref, onest_ref,
             dq_ref, dk_ref, dv_ref, da_ref, dw_ref, db_ref, dstate, dqs_s, dk_s, dg_s, dv_s):
        i_blk = nblk - 1 - pl.program_id(0)

        @pl.when(pl.program_id(0) == 0)
        def _():
            dstate[...] = jnp.zeros_like(dstate)
            dw_ref[...] = jnp.zeros_like(dw_ref)
            db_ref[...] = jnp.zeros_like(db_ref)
        hm_q = _head_masks(GLA_QK, GLA_DK)
        hm_v = _head_masks(GLA_V, GLA_DV)
        tri, trit, ones_qv, ones_vq = tri_ref[...], trit_ref[...], ones_ref[...], onest_ref[...]
        w2, bias = w_ref[...], b_ref[...]
        rsum = lambda x: jnp.sum(x, axis=0, keepdims=True)

        def chunk(cc, carry):
            ci = nch - 1 - cc
            sl = pl.ds(pl.multiple_of(ci * c, c), c)
            ok = _gla_rows(i_blk, ci, lp)
            k, v, ga = k_ref[sl, :], v_ref[sl, :], a_ref[sl, :]
            vb = v.astype(BF16)
            qs = q_ref[sl, :] * (GLA_DK ** -0.5)
            z, g = _gla_gate(ga, w2, bias, ok, tri)
            last = g[c - 1:c, :]
            elast = jnp.exp(last)
            eg = jnp.exp(g)
            ekl = jnp.exp(last - g)
            qe, ke = qs * eg, k * ekl
            dov = do_ref[sl, :]
            st = st_ref[ci]
            dsn = dstate[...]
            qst = _stack_heads(qe, hm_q).astype(BF16)
            dost = jnp.concatenate([dov[:, GLA_DV * h:GLA_DV * (h + 1)] for h in range(GLA_HEADS)], axis=0).astype(BF16)
            dqe_st = _dot(dost, st.astype(BF16))
            dqe = dqe_st[0:c, :] * hm_q[0]
            for h in range(1, GLA_HEADS):
                dqe = dqe + dqe_st[c * h:c * h + c, :] * hm_q[h]
            dstate[...] = _dot_tn(dost, qst) + dsn * elast
            dlast = rsum(dsn * st) * elast
            df = _stack_heads(dsn, hm_q).astype(BF16)
            dv_s[...] = _dot_nt(ke.astype(BF16), df)
            dke = _dot(vb, df)
            xk = dke * ke
            dqs_s[...] = dqe * eg
            dk_s[...] = dke * ekl
            dg_s[...] = dqe * qe - xk
            dlast = dlast + rsum(xk)
            for a in range(1, na):
                eq, ek, qh, kh, qsa, pmask, p = _gla_off_parts(a, qs, k, g, hm_q)
                rows = slice(s * a, s * a + s)
                dofull = _stack_heads(dov[rows, :], hm_v).astype(BF16)
                dp = jnp.where(pmask, _dot_nt(dofull, vb), 0.0).astype(BF16)
                dv_s[...] += _dot_tn(p.astype(BF16), dofull)
                dq_st = _dot(dp, kh.astype(BF16))
                dqh = dq_st[0:s, :] * hm_q[0]
                for h in range(1, GLA_HEADS):
                    dqh = dqh + dq_st[s * h:s * h + s, :] * hm_q[h]
                dkh = _dot_tn(dp, qsa)
                xq = dqh * qh
                xkh = dkh * kh
                dqs_s[rows, :] += dqh * eq
                dg_s[rows, :] += xq
                dk_s[...] += dkh * ek
                dg_s[...] -= xkh
                dg_s[s * a - 1:s * a, :] += rsum(xkh) - rsum(xq)
            kes, qes, ws, dbs = [], [], [], []
            for j in range(s):
                em = jnp.exp(jnp.minimum(g - _roll_rows(g, j), 0.0)) * _lag_mask(j)
                kes.append(_roll_rows(k, j) * em)
                qes.append(qs * em)
                ws.append((qs * kes[j]).astype(BF16))
                dbs.append((dov * _roll_rows(v, j)).astype(BF16))
            ball = _dot(jnp.concatenate(ws, axis=0), ones_qv)
            dwall = _dot(jnp.concatenate(dbs, axis=0), ones_vq)
            for j in range(s):
                back = (lambda x: x) if j == 0 else (lambda x, j=j: pltpu.roll(x, c - j, 0))
                dw = dwall[c * j:c * j + c, :]
                dv_s[...] += back(ball[c * j:c * j + c, :] * dov)
                dqs_s[...] += dw * kes[j]
                dk_s[...] += back(dw * qes[j])
                x = dw * qs * kes[j]
                dg_s[...] += x - back(x)
            dg_s[c - 1:c, :] += dlast
            dla = jnp.where(ok, _tri_sum(trit, dg_s[...]), 0.0)
            dz = dla * (1.0 / GLA_TAU) / (1.0 + jnp.exp(z))
            dzb = dz.astype(BF16)
            dq_ref[sl, :] = (dqs_s[...] * (GLA_DK ** -0.5)).astype(BF16)
            dk_ref[sl, :] = dk_s[...].astype(BF16)
            dv_ref[sl, :] = dv_s[...].astype(BF16)
            da_ref[sl, :] = _dot_nt(dzb, w2).astype(BF16)
            dw_ref[...] += _dot_tn(ga.astype(BF16), dzb)
            db_ref[...] += rsum(dz)
            return carry

        lax.fori_loop(0, nch, chunk, 0)

    tri, trit, ones_qv, ones_vq = tables
    full = lambda arr: pl.BlockSpec(arr.shape, lambda i: (0,) * arr.ndim)
    rev = lambda i: nblk - 1 - i
    qk = jax.ShapeDtypeStruct((lp, GLA_QK), BF16)
    return pl.pallas_call(
        body,
        out_shape=(qk, qk, jax.ShapeDtypeStruct((lp, GLA_V), BF16), jax.ShapeDtypeStruct((lp, 128), BF16),
                   jax.ShapeDtypeStruct((128, GLA_QK), F32), jax.ShapeDtypeStruct((1, GLA_QK), F32)),
        grid=(nblk,),
        in_specs=[pl.BlockSpec((BLK, GLA_QK), lambda i: (rev(i), C_GQ // GLA_QK)),
                  pl.BlockSpec((BLK, GLA_QK), lambda i: (rev(i), C_GK // GLA_QK)),
                  pl.BlockSpec((BLK, GLA_V), lambda i: (rev(i), C_GV // GLA_V)),
                  pl.BlockSpec((BLK, 128), lambda i: (rev(i), C_GA // 128)),
                  pl.BlockSpec((BLK, GLA_V), lambda i: (rev(i), 0)),
                  pl.BlockSpec((nch, GLA_DV, GLA_QK), lambda i: (rev(i), 0, 0)),
                  full(w2p), full(b), full(tri), full(trit), full(ones_qv), full(ones_vq)],
        out_specs=(pl.BlockSpec((BLK, GLA_QK), lambda i: (rev(i), 0)),
                   pl.BlockSpec((BLK, GLA_QK), lambda i: (rev(i), 0)),
                   pl.BlockSpec((BLK, GLA_V), lambda i: (rev(i), 0)),
                   pl.BlockSpec((BLK, 128), lambda i: (rev(i), 0)),
                   pl.BlockSpec((128, GLA_QK), lambda i: (0, 0)),
                   pl.BlockSpec((1, GLA_QK), lambda i: (0, 0))),
        scratch_shapes=[pltpu.VMEM((GLA_DV, GLA_QK), F32), pltpu.VMEM((c, GLA_QK), F32),
                        pltpu.VMEM((c, GLA_QK), F32), pltpu.VMEM((c, GLA_QK), F32), pltpu.VMEM((c, GLA_V), F32)],
        compiler_params=_cp("arbitrary"), name=name)(proj, proj, proj, proj, do, states, w2p, b, tri, trit, ones_qv, ones_vq)


def _as2d(a):
    return a.reshape(-1, a.shape[-1])


def _ew_tile(r):
    return _tile(r, (512, 256, 128, 64, 32, 16, 8))


def _add2(a, b, *, out_dtype, name):
    a2, b2 = _as2d(a), _as2d(b)
    r, n = a2.shape
    tm = _ew_tile(r)

    def body(a_ref, b_ref, o_ref):
        o_ref[...] = (a_ref[...] + b_ref[...]).astype(o_ref.dtype)

    blk = pl.BlockSpec((tm, n), lambda i: (i, 0))
    return pl.pallas_call(body, out_shape=jax.ShapeDtypeStruct((r, n), out_dtype), grid=(r // tm,), in_specs=[blk, blk],
                          out_specs=blk, compiler_params=_cp("parallel"), name=name)(a2, b2).reshape(a.shape)


def _sum_slots(q, *, name):
    shape = q.shape[1:]
    q3 = q.reshape(4, -1, shape[-1])
    r, n = q3.shape[1:]
    tm = _ew_tile(r)

    def body(q_ref, o_ref):
        f = lambda i: q_ref[i].astype(F32)
        o_ref[...] = ((f(3) + f(0)) + f(1)) + f(2)

    return pl.pallas_call(
        body, out_shape=jax.ShapeDtypeStruct((r, n), F32), grid=(r // tm,),
        in_specs=[pl.BlockSpec((4, tm, n), lambda i: (0, i, 0))], out_specs=pl.BlockSpec((tm, n), lambda i: (i, 0)),
        compiler_params=_cp("parallel"), name=name)(q3).reshape(shape)


def _adamw(w, g, m, v, *, name):
    shape = w.shape
    w2, g2, m2, v2 = _as2d(w), _as2d(g), _as2d(m), _as2d(v)
    r, n = w2.shape
    tm = _ew_tile(r)
    c1 = 1.0 - ADAM_B1 ** ADAM_STEP
    c2 = 1.0 - ADAM_B2 ** ADAM_STEP

    def body(w_ref, g_ref, m_ref, v_ref, d_ref, mo_ref, vo_ref):
        gv = g_ref[...]
        mn = ADAM_B1 * m_ref[...] + (1.0 - ADAM_B1) * gv
        vn = ADAM_B2 * v_ref[...] + (1.0 - ADAM_B2) * (gv * gv)
        mo_ref[...] = mn
        vo_ref[...] = vn
        d_ref[...] = -ADAM_LR * ((mn / c1) / (jnp.sqrt(vn / c2) + ADAM_EPS) + ADAM_WD * w_ref[...])

    blk = pl.BlockSpec((tm, n), lambda i: (i, 0))
    o = jax.ShapeDtypeStruct((r, n), F32)
    d, mo, vo = pl.pallas_call(body, out_shape=(o, o, o), grid=(r // tm,), in_specs=[blk] * 4, out_specs=(blk,) * 3,
                               compiler_params=_cp("parallel"), name=name)(w2, g2, m2, v2)
    return d.reshape(shape), mo.reshape(shape), vo.reshape(shape)


ANY = pl.BlockSpec(memory_space=pl.ANY)


def _place():
    return lax.axis_index("x"), lax.axis_index("y"), lax.axis_index("c")


def _other_chips(x, y):
    return [(1 - x, y), (x, 1 - y), (1 - x, 1 - y)]


def _remote(src, dst, ssem, rsem, dev):
    return pltpu.make_async_remote_copy(src_ref=src, dst_ref=dst, send_sem=ssem, recv_sem=rsem, device_id=dev,
                                        device_id_type=MESH)


def _allgather_chips(arrs, *, name):
    n = len(arrs)

    def body(*refs):
        ins, outs = refs[:n], refs[n:2 * n]
        s1, r1, s2, r2, lsem = refs[2 * n:]
        x, y, c = _place()
        chips = _other_chips(x, y)
        sib = (x, y, 1 - c)

        def run(q, c_):
            qs = [q ^ 2, q ^ 1, q ^ 3]
            local, first, passed = [], [], []
            for k in range(n):
                for half in range(2):
                    local.append(pltpu.make_async_copy(ins[k].at[half], outs[k].at[half, q], lsem.at[k, half]))
                for j, chip in enumerate(chips):
                    first.append(_remote(ins[k].at[c_], outs[k].at[c_, q], s1.at[k, j], r1.at[k, j], (*chip, c)))
            for cp in local + first:
                cp.start()
            for k in range(n):
                for j, chip in enumerate(chips):
                    land = outs[k].at[c_, qs[j]]
                    _remote(land, land, s1.at[k, j], r1.at[k, j], (*chip, c)).wait_recv()
                    fw = _remote(land, land, s2.at[k, j], r2.at[k, j], sib)
                    fw.start()
                    passed.append(fw)
            for k in range(n):
                for j in range(3):
                    land = outs[k].at[1 - c_, qs[j]]
                    _remote(land, land, s2.at[k, j], r2.at[k, j], sib).wait_recv()
            for cp in first + passed:
                cp.wait_send()
            for cp in local:
                cp.wait()

        for q in range(4):
            for c_ in range(2):
                pl.when((2 * x + y == q) & (c == c_))(functools.partial(run, q, c_))

    sem = pltpu.SemaphoreType.DMA
    return pl.pallas_call(
        body, out_shape=tuple(jax.ShapeDtypeStruct((2, 4) + a.shape[1:], a.dtype) for a in arrs),
        in_specs=[ANY] * n, out_specs=(ANY,) * n,
        scratch_shapes=[sem((n, 3)), sem((n, 3)), sem((n, 3)), sem((n, 3)), sem((n, 2))], name=name)(*arrs)


def _pair_exchange(arrs, *, name):
    n = len(arrs)

    def body(*refs):
        ins, outs = refs[:n], refs[n:2 * n]
        ssem, rsem = refs[2 * n:]
        x, y, c = _place()
        cps = [_remote(ins[k].at[1 - c], outs[k], ssem.at[k], rsem.at[k], (x, y, 1 - c)) for k in range(n)]
        for cp in cps:
            cp.start()
        for cp in cps:
            cp.wait()

    sem = pltpu.SemaphoreType.DMA
    return pl.pallas_call(
        body, out_shape=tuple(jax.ShapeDtypeStruct(a.shape[1:], a.dtype) for a in arrs),
        in_specs=[ANY] * n, out_specs=(ANY,) * n, scratch_shapes=[sem((n,)), sem((n,))], name=name)(*arrs)


def _chip_exchange(arrs, *, name):
    n = len(arrs)

    def body(*refs):
        ins, outs = refs[:n], refs[n:2 * n]
        ssem, rsem, lsem = refs[2 * n:]
        x, y, c = _place()
        q = 2 * x + y
        chips = _other_chips(x, y)
        cps, local = [], []
        for k in range(n):
            local.append(pltpu.make_async_copy(ins[k].at[q], outs[k].at[3], lsem.at[k]))
            for j, (cx, cy) in enumerate(chips):
                cps.append(_remote(ins[k].at[2 * cx + cy], outs[k].at[j], ssem.at[k, j], rsem.at[k, j], (cx, cy, c)))
        for cp in local + cps:
            cp.start()
        for cp in cps:
            cp.wait()
        for cp in local:
            cp.wait()

    sem = pltpu.SemaphoreType.DMA
    return pl.pallas_call(
        body, out_shape=tuple(jax.ShapeDtypeStruct(a.shape, a.dtype) for a in arrs),
        in_specs=[ANY] * n, out_specs=(ANY,) * n, scratch_shapes=[sem((n, 3)), sem((n, 3)), sem((n,))],
        name=name)(*arrs)


def _pair_share(arrs, *, name):
    n = len(arrs)

    def body(*refs):
        ins, outs = refs[:n], refs[n:2 * n]
        ssem, rsem, lsem = refs[2 * n:]
        x, y, c = _place()

        def run(c_):
            local = [pltpu.make_async_copy(ins[k], outs[k].at[c_], lsem.at[k]) for k in range(n)]
            for cp in local:
                cp.start()
            for k in range(n):
                _remote(ins[k], outs[k].at[c_], ssem.at[k], rsem.at[k], (x, y, 1 - c)).start()
            for k in range(n):
                _remote(ins[k], outs[k].at[1 - c_], ssem.at[k], rsem.at[k], (x, y, 1 - c)).wait()
            for cp in local:
                cp.wait()

        for c_ in range(2):
            pl.when(c == c_)(functools.partial(run, c_))

    sem = pltpu.SemaphoreType.DMA
    return pl.pallas_call(
        body, out_shape=tuple(jax.ShapeDtypeStruct((2,) + a.shape, a.dtype) for a in arrs),
        in_specs=[ANY] * n, out_specs=(ANY,) * n, scratch_shapes=[sem((n,)), sem((n,)), sem((n,))],
        name=name)(*arrs)


def _allreduce_small(slab, *, name):
    r, n = slab.shape

    def body(x_ref, o_ref, buf, ssem, rsem):
        x, y, c = _place()
        me = 4 * x + 2 * y + c
        buf[me] = x_ref[...]
        cps = []
        for rel in range(1, 8):
            bx, by, bc = (rel >> 2) & 1, (rel >> 1) & 1, rel & 1
            px, py, pc = (x + bx) % 2, (y + by) % 2, (c + bc) % 2
            cps.append((_remote(x_ref, buf.at[me], ssem.at[rel - 1], rsem.at[rel - 1], (px, py, pc)),
                        4 * px + 2 * py + pc, (px, py, pc)))
        for cp, _, _ in cps:
            cp.start()
        for rel, (cp, peer, dev) in enumerate(cps):
            cp.wait_send()
            _remote(x_ref, buf.at[peer], ssem.at[rel], rsem.at[rel], dev).wait_recv()
        acc = buf[0]
        for k in range(1, 8):
            acc = acc + buf[k]
        o_ref[...] = acc

    vm = pl.BlockSpec(memory_space=pltpu.VMEM)
    sem = pltpu.SemaphoreType.DMA
    return pl.pallas_call(
        body, out_shape=jax.ShapeDtypeStruct((r, n), F32), in_specs=[vm], out_specs=vm,
        scratch_shapes=[pltpu.VMEM((8, r, n), F32), sem((7,)), sem((7,))], name=name)(slab)


def _slab(arrs, row_mult):
    flat = jnp.concatenate([a.reshape(-1) for a in arrs])
    unit = 128 * row_mult
    total = -(-flat.size // unit) * unit
    return jnp.pad(flat, (0, total - flat.size)).reshape(-1, 128)


def _unslab(slab, shapes):
    flat = slab.reshape(-1)
    out, off = [], 0
    for s in shapes:
        size = int(np.prod(s))
        out.append(flat[off:off + size].reshape(s))
        off += size
    return out


def _cols_from_chips(a):
    return jnp.transpose(a, (1, 0, 2)).reshape(a.shape[1], -1)


def _cols_to_chips(a, parts):
    r = a.shape[0]
    return jnp.transpose(a.reshape(r, parts, -1), (1, 0, 2))


def _gather_params(w_in, w_out, ffn_up, ffn_down, meta_tokens, gla_gate_w2, ffn_conv_w):
    d = D_MODEL
    sh_shapes = [meta_tokens.shape, gla_gate_w2.shape, ffn_conv_w.shape]
    sh_slab = _slab([meta_tokens, gla_gate_w2, ffn_conv_w], 16)
    g_in, g_out, g_up, g_down, g_sh = _allgather_chips(
        [w_in.astype(BF16), w_out.astype(BF16), ffn_up.astype(BF16), ffn_down.astype(BF16),
         sh_slab.reshape(2, -1, 128)], name="gather_weights")
    sh_all = jnp.transpose(g_sh, (1, 0, 2, 3)).reshape(4, -1, 128)
    parts = [_unslab(sh_all[k], sh_shapes) for k in range(4)]
    meta_full = jnp.concatenate([p[0] for p in parts], axis=-1)
    w2_full = jnp.concatenate([p[1] for p in parts], axis=-1)
    cw_full = jnp.concatenate([p[2] for p in parts], axis=-1)

    win = [jnp.pad(_cols_from_chips(g_in[l]), ((0, 0), (0, IN_PAD - IN_WIDTH))) for l in range(DEPTH)]
    wout = [g_out[l].reshape(d, d) for l in range(DEPTH)]
    up_a = [_cols_from_chips(g_up[l, 0:2]) for l in range(DEPTH)]
    up_g = [_cols_from_chips(g_up[l, 2:4]) for l in range(DEPTH)]
    down = [g_down[l].reshape(D_FF, d) for l in range(DEPTH)]
    w2p = [jnp.pad(w2_full[l], ((0, 128 - GLA_RANK), (0, 0))).astype(BF16) for l in range(DEPTH)]
    return meta_full, win, wout, up_a, up_g, down, w2p, cw_full


def _local_step(x_rows, target_rows, meta_full, win, wout, up_a, up_g, down, w2p, cw_full, pre_mix_norm, gla_gate_b,
                ret_norm_w, gla_norm_w, post_mix_norm, pre_ffn_norm, ffn_conv_b, post_ffn_norm):
    d = D_MODEL
    lp = x_rows.shape[0] + FRONT + BACK
    row = lambda a, l: a[l][None, :]
    rtab = _ret_tables(lp)
    gtab = _gla_tables()
    h0 = jnp.concatenate([jnp.zeros((PADF, d), F32), meta_full, x_rows, jnp.zeros((BACK, d), F32)], axis=0)
    target = jnp.pad(target_rows, ((FRONT, BACK), (0, 0)))

    saved = []
    h = h0
    _, hn = _resid_norm(h0, None, None, row(pre_mix_norm, 0), name="norm_in")
    loss_local = dy = None
    for l in range(DEPTH):
        s = {"h_in": h, "hn": hn}
        s["proj"] = _mm(hn, win[l], name="proj")
        s["o_ret"], s["st_ret"] = _retention(s["proj"], rtab, name="retention")
        s["o_gla"], s["st_gla"] = _gla(s["proj"], w2p[l], row(gla_gate_b, l), gtab, name="gla")
        s["merged"] = _merge(s["o_ret"], s["o_gla"], s["proj"], row(ret_norm_w, l), row(gla_norm_w, l), name="merge")
        s["m"] = _mm(s["merged"], wout[l], name="mix_out")
        s["h_mid"], s["hn2"] = _resid_norm(h, s["m"], row(post_mix_norm, l), row(pre_ffn_norm, l), name="resid_mix")
        s["ua"] = _mm(s["hn2"], up_a[l], name="ffn_up_a")
        s["ug"] = _mm(s["hn2"], up_g[l], name="ffn_up_g")
        cw_a, cw_g = cw_full[l][:, :D_FF], cw_full[l][:, D_FF:]
        cb_a, cb_g = ffn_conv_b[l][None, :D_FF], ffn_conv_b[l][None, D_FF:]
        s["conv"] = (cw_a, cw_g, cb_a, cb_g)
        s["act"] = _conv_act(s["ua"], s["ug"], cw_a, cw_g, cb_a, cb_g, name="conv_act")
        s["f"] = _mm(s["act"], down[l], name="ffn_down")
        if l + 1 < DEPTH:
            h, hn = _resid_norm(s["h_mid"], s["f"], row(post_ffn_norm, l), row(pre_mix_norm, l + 1), name="resid_ffn")
        else:
            loss_local, dy = _loss_head(s["h_mid"], s["f"], row(post_ffn_norm, l), target, name="loss_head")
        saved.append(s)

    g = {k: [None] * DEPTH for k in ("pre_mix", "w_in", "w2", "gb", "ret_n", "gla_n", "w_out", "post_mix", "pre_ffn",
                                     "up_a", "up_g", "cw", "cb", "down", "post_ffn")}
    dh_out, dhn_next = dy, None
    for l in reversed(range(DEPTH)):
        s = saved[l]
        cw_a, cw_g, cb_a, cb_g = s["conv"]
        if l + 1 < DEPTH:
            dh, df, g["pre_mix"][l + 1], g["post_ffn"][l] = _resid_norm_bwd(
                dh_out, dhn_next, saved[l + 1]["h_in"], s["f"], row(pre_mix_norm, l + 1), row(post_ffn_norm, l),
                name="resid_ffn_bwd")
        else:
            dh, df, _, g["post_ffn"][l] = _resid_norm_bwd(dh_out, None, None, s["f"], None, row(post_ffn_norm, l),
                                                          name="loss_head_bwd")
        dact = _mm(df, down[l], nt=True, name="ffn_down_dx")
        g["down"][l] = _mm_tn(s["act"], df, tn=512, name="ffn_down_dw")
        du_a, du_g, dcw_a, dcw_g, dcb_a, dcb_g = _conv_act_bwd(s["ua"], s["ug"], dact, cw_a, cw_g, cb_a, cb_g,
                                                               name="conv_act_bwd")
        g["cw"][l] = jnp.concatenate([dcw_a, dcw_g], axis=1)
        g["cb"][l] = jnp.concatenate([dcb_a, dcb_g], axis=1)[0]
        g["up_a"][l] = _mm_tn(s["hn2"], du_a, tn=1408, name="ffn_up_a_dw")
        g["up_g"][l] = _mm_tn(s["hn2"], du_g, tn=1408, name="ffn_up_g_dw")
        dhn2 = _mm(du_a, up_a[l], nt=True, name="ffn_up_a_dx")
        dhn2 = _mm(du_g, up_g[l], nt=True, add=dhn2, name="ffn_up_g_dx")
        dh, dm, g["pre_ffn"][l], g["post_mix"][l] = _resid_norm_bwd(
            dh, dhn2, s["h_mid"], s["m"], row(pre_ffn_norm, l), row(post_mix_norm, l), name="resid_mix_bwd")
        g["w_out"][l] = _mm_tn(s["merged"], dm, name="mix_out_dw")
        dmerged = _mm(dm, wout[l], nt=True, name="mix_out_dx")
        do_ret, do_gla, drg, dgr, g["ret_n"][l], g["gla_n"][l] = _merge_bwd(
            dmerged, s["o_ret"], s["o_gla"], s["proj"], row(ret_norm_w, l), row(gla_norm_w, l), name="merge_bwd")
        drq, drk, drv = _retention_bwd(s["proj"], do_ret, s["st_ret"], rtab, name="retention_bwd")
        dgq, dgk, dgv, dga, dw2, dgb = _gla_bwd(s["proj"], do_gla, s["st_gla"], w2p[l], row(gla_gate_b, l), gtab,
                                                name="gla_bwd")
        g["w2"][l], g["gb"][l] = dw2[:GLA_RANK], dgb[0]
        dproj = jnp.concatenate([drq, drk, drv, drg, dgq, dgk, dgv, dgr, dga,
                                 jnp.zeros((lp, IN_PAD - C_GA - 128), BF16)], axis=1)
        g["w_in"][l] = _mm_tn(s["hn"], dproj, tn=1280, name="proj_dw")
        dhn_next = _mm(dproj, win[l], nt=True, name="proj_dx")
        dh_out = dh
    dh0, _, g["pre_mix"][0], _ = _resid_norm_bwd(dh_out, dhn_next, h0, None, row(pre_mix_norm, 0), None,
                                                 name="norm_in_bwd")
    return loss_local, dh0, g


def kernel(x, meta_tokens, pre_mix_norm, w_in, gla_gate_w2, gla_gate_b, ret_norm_w, gla_norm_w, w_out, post_mix_norm, pre_ffn_norm, ffn_up, ffn_conv_w, ffn_conv_b, ffn_down, post_ffn_norm, loss_target, m_meta_tokens, m_pre_mix_norm, m_w_in, m_gla_gate_w2, m_gla_gate_b, m_ret_norm_w, m_gla_norm_w, m_w_out, m_post_mix_norm, m_pre_ffn_norm, m_ffn_up, m_ffn_conv_w, m_ffn_conv_b, m_ffn_down, m_post_ffn_norm, v_meta_tokens, v_pre_mix_norm, v_w_in, v_gla_gate_w2, v_gla_gate_b, v_ret_norm_w, v_gla_norm_w, v_w_out, v_post_mix_norm, v_pre_ffn_norm, v_ffn_up, v_ffn_conv_w, v_ffn_conv_b, v_ffn_down, v_post_ffn_norm):
    xi, yi, ci = _place()
    chip = 2 * xi + yi
    seq = x.shape[1]
    d = D_MODEL
    meta_full, win, wout, up_a, up_g, down, w2p, cw_full = _gather_params(
        w_in, w_out, ffn_up, ffn_down, meta_tokens, gla_gate_w2, ffn_conv_w)
    loss_local, dh0, g = _local_step(x[0], loss_target[0], meta_full, win, wout, up_a, up_g, down, w2p, cw_full,
                                     pre_mix_norm, gla_gate_b, ret_norm_w, gla_norm_w, post_mix_norm, pre_ffn_norm,
                                     ffn_conv_b, post_ffn_norm)
    grad_x = dh0[FRONT:FRONT + seq][None]

    big = [
        jnp.stack([_cols_to_chips(g["w_in"][l][:, :IN_WIDTH], 4) for l in range(DEPTH)]),
        jnp.stack([g["w_out"][l].reshape(4, d // 4, d) for l in range(DEPTH)]),
        jnp.stack([jnp.concatenate([_cols_to_chips(g["up_a"][l], 2), _cols_to_chips(g["up_g"][l], 2)], axis=0)
                   for l in range(DEPTH)]),
        jnp.stack([g["down"][l].reshape(4, D_FF // 4, d) for l in range(DEPTH)]),
    ]
    from_sib = _pair_exchange(big, name="grads_pair_exchange")
    mine = [lax.dynamic_index_in_dim(a, ci, 0, keepdims=False) for a in big]
    names = ("w_in", "w_out", "ffn_up", "ffn_down")
    chip_sums = [_add2(a, b, out_dtype=BF16, name=f"pair_sum_{nm}") for a, b, nm in zip(mine, from_sib, names)]
    slots = _chip_exchange(chip_sums, name="grads_chip_exchange")
    layer_grads = [_sum_slots(q, name=f"chip_sum_{nm}") for q, nm in zip(slots, names)]
    g_w_in, g_w_out, g_ffn_up, g_ffn_down = _pair_share(layer_grads, name="grads_pair_share")

    small_full = [dh0[PADF:FRONT], jnp.stack(g["pre_mix"])[:, 0], jnp.stack(g["w2"]), jnp.stack(g["gb"]),
                  jnp.stack(g["ret_n"])[:, 0], jnp.stack(g["gla_n"])[:, 0], jnp.stack(g["post_mix"])[:, 0],
                  jnp.stack(g["pre_ffn"])[:, 0], jnp.stack(g["cw"]), jnp.stack(g["cb"]),
                  jnp.stack(g["post_ffn"])[:, 0]]
    small_sum = _unslab(_allreduce_small(_slab(small_full, 8), name="small_allreduce"), [a.shape for a in small_full])
    (g_meta, g_pre_mix, g_w2, g_gb, g_ret_n, g_gla_n, g_post_mix, g_pre_ffn, g_cw, g_cb, g_post_ffn) = small_sum
    g_meta = lax.dynamic_slice_in_dim(g_meta, chip * 256, 256, axis=1)
    g_w2 = lax.dynamic_slice_in_dim(g_w2, chip * 64, 64, axis=2)
    g_cw = lax.dynamic_slice_in_dim(g_cw, chip * 1408, 1408, axis=2)

    grads = [g_meta, g_pre_mix, g_w_in, g_w2, g_gb, g_ret_n, g_gla_n, g_w_out, g_post_mix, g_pre_ffn, g_ffn_up,
             g_cw, g_cb, g_ffn_down, g_post_ffn]
    ws = [meta_tokens, pre_mix_norm, w_in, gla_gate_w2, gla_gate_b, ret_norm_w, gla_norm_w, w_out, post_mix_norm,
          pre_ffn_norm, ffn_up, ffn_conv_w, ffn_conv_b, ffn_down, post_ffn_norm]
    ms = [m_meta_tokens, m_pre_mix_norm, m_w_in, m_gla_gate_w2, m_gla_gate_b, m_ret_norm_w, m_gla_norm_w, m_w_out,
          m_post_mix_norm, m_pre_ffn_norm, m_ffn_up, m_ffn_conv_w, m_ffn_conv_b, m_ffn_down, m_post_ffn_norm]
    vs = [v_meta_tokens, v_pre_mix_norm, v_w_in, v_gla_gate_w2, v_gla_gate_b, v_ret_norm_w, v_gla_norm_w, v_w_out,
          v_post_mix_norm, v_pre_ffn_norm, v_ffn_up, v_ffn_conv_w, v_ffn_conv_b, v_ffn_down, v_post_ffn_norm]
    big_idx = (2, 7, 10, 13)
    deltas, new_m, new_v = [None] * 15, [None] * 15, [None] * 15
    for i, nm in zip(big_idx, names):
        deltas[i], new_m[i], new_v[i] = _adamw(ws[i], grads[i], ms[i], vs[i], name=f"adamw_{nm}")
    small_idx = [i for i in range(15) if i not in big_idx]
    shapes = [ws[i].shape for i in small_idx]
    sd, sm, sv = _adamw(_slab([ws[i] for i in small_idx], 8), _slab([grads[i] for i in small_idx], 8),
                        _slab([ms[i] for i in small_idx], 8), _slab([vs[i] for i in small_idx], 8), name="adamw_small")
    for i, a, b, c_ in zip(small_idx, _unslab(sd, shapes), _unslab(sm, shapes), _unslab(sv, shapes)):
        deltas[i], new_m[i], new_v[i] = a, b, c_

    loss = lax.psum(loss_local, ("x", "y", "c"))
    return (loss, grad_x, *grads, *deltas, *new_m, *new_v)
```

```python
import functools
import math

import numpy as np
import jax
import jax.numpy as jnp
from jax import lax
from jax.experimental import pallas as pl
from jax.experimental.pallas import tpu as pltpu

F32 = jnp.float32
BF16 = jnp.bfloat16

D_MODEL = 1024
DEPTH = 2
N_META = 16
EPS = 1e-6
RET_HEADS = 4
RET_DK = 128
GLA_HEADS = 4
GLA_DK = 64
GLA_DV = 128
GLA_QK = GLA_HEADS * GLA_DK
GLA_V = GLA_HEADS * GLA_DV
GLA_RANK = 16
GLA_TAU = 16.0
D_FF = 2816
ROPE_BASE = 10000.0
IN_WIDTH = 3600
IN_PAD = 3840
C_RQ, C_RK, C_RV, C_RG, C_GQ, C_GK, C_GV, C_GR, C_GA = 0, 512, 1024, 1536, 2048, 2304, 2560, 3072, 3584

FRONT = 64
BACK = 64
PADF = FRONT - N_META
RET_CHUNK = 128
GLA_CHUNK = 64
GLA_SUB = 16
BLK = 640

ADAM_LR, ADAM_B1, ADAM_B2, ADAM_EPS, ADAM_WD, ADAM_STEP = 0.001, 0.9, 0.999, 1e-08, 0.01, 10

VMEM_LIMIT = 56 * 2 ** 20
MESH = pl.DeviceIdType.MESH


def _cp(*sem):
    return pltpu.CompilerParams(dimension_semantics=sem, vmem_limit_bytes=VMEM_LIMIT)


def _tile(n, cands):
    for t in cands:
        if n % t == 0:
            return t
    raise ValueError(f"no tile for {n} in {cands}")


def _row_tile(n):
    return _tile(n, (640, 512, 320, 256, 128, 64))


def _mm(a, b, *, nt=False, add=None, out_dtype=F32, tn=None, name):
    m, k = a.shape
    n = b.shape[0] if nt else b.shape[1]
    tm = _tile(m, (320, 256, 128, 64))
    tn = n if tn is None else tn
    dn = (((1,), (1,)), ((), ())) if nt else (((1,), (0,)), ((), ()))

    def body(*refs):
        if add is None:
            a_ref, b_ref, o_ref = refs
        else:
            a_ref, b_ref, c_ref, o_ref = refs
        r = lax.dot_general(a_ref[...].astype(BF16), b_ref[...].astype(BF16), dn, preferred_element_type=F32)
        if add is not None:
            r = r + c_ref[...]
        o_ref[...] = r.astype(o_ref.dtype)

    b_spec = pl.BlockSpec((tn, k), lambda j, i: (j, 0)) if nt else pl.BlockSpec((k, tn), lambda j, i: (0, j))
    in_specs = [pl.BlockSpec((tm, k), lambda j, i: (i, 0)), b_spec]
    args = [a, b]
    if add is not None:
        in_specs.append(pl.BlockSpec((tm, tn), lambda j, i: (i, j)))
        args.append(add)
    return pl.pallas_call(
        body, out_shape=jax.ShapeDtypeStruct((m, n), out_dtype), grid=(n // tn, m // tm),
        in_specs=in_specs, out_specs=pl.BlockSpec((tm, tn), lambda j, i: (i, j)),
        compiler_params=_cp("parallel", "parallel"), name=name)(*args)


def _mm_tn(a, b, *, tn=None, name):
    m, k = a.shape
    n = b.shape[1]
    tm = _tile(m, (1664, 640, 320, 256, 128, 64))
    tn = n if tn is None else tn

    def body(a_ref, b_ref, o_ref):
        @pl.when(pl.program_id(1) == 0)
        def _():
            o_ref[...] = jnp.zeros_like(o_ref)
        o_ref[...] += lax.dot_general(a_ref[...].astype(BF16), b_ref[...].astype(BF16),
                                      (((0,), (0,)), ((), ())), preferred_element_type=F32)

    return pl.pallas_call(
        body, out_shape=jax.ShapeDtypeStruct((k, n), F32), grid=(n // tn, m // tm),
        in_specs=[pl.BlockSpec((tm, k), lambda j, i: (i, 0)), pl.BlockSpec((tm, tn), lambda j, i: (i, j))],
        out_specs=pl.BlockSpec((k, tn), lambda j, i: (0, j)),
        compiler_params=_cp("parallel", "arbitrary"), name=name)(a, b)


def _rms(x, w):
    r = lax.rsqrt(jnp.mean(x * x, axis=-1, keepdims=True) + EPS)
    return x * r * w


def _rms_bwd(x, w, dy):
    r = lax.rsqrt(jnp.mean(x * x, axis=-1, keepdims=True) + EPS)
    xh = x * r
    dxh = dy * w
    dx = r * (dxh - xh * jnp.mean(dxh * xh, axis=-1, keepdims=True))
    return dx, jnp.sum(dy * xh, axis=0, keepdims=True)


def _resid_norm(h, t, w_post, w_next, *, name):
    lp, d = h.shape
    tm = _row_tile(lp)
    has_t = t is not None

    def body(*refs):
        if has_t:
            h_ref, t_ref, wp_ref, wn_ref, ho_ref, hn_ref = refs
            hv = h_ref[...] + _rms(t_ref[...], wp_ref[...])
            ho_ref[...] = hv
        else:
            h_ref, wn_ref, hn_ref = refs
            hv = h_ref[...]
        hn_ref[...] = _rms(hv, wn_ref[...]).astype(BF16)

    row = pl.BlockSpec((tm, d), lambda i: (i, 0))
    vec = pl.BlockSpec((1, d), lambda i: (0, 0))
    if has_t:
        return pl.pallas_call(
            body, out_shape=(jax.ShapeDtypeStruct((lp, d), F32), jax.ShapeDtypeStruct((lp, d), BF16)),
            grid=(lp // tm,), in_specs=[row, row, vec, vec], out_specs=(row, row),
            compiler_params=_cp("parallel"), name=name)(h, t, w_post, w_next)
    return h, pl.pallas_call(
        body, out_shape=jax.ShapeDtypeStruct((lp, d), BF16), grid=(lp // tm,), in_specs=[row, vec],
        out_specs=row, compiler_params=_cp("parallel"), name=name)(h, w_next)


def _resid_norm_bwd(dh_out, dhn, h_new, t, w_next, w_post, *, name):
    lp, d = h_new.shape if h_new is not None else t.shape
    tm = _row_tile(lp)
    has_n = dhn is not None
    has_t = t is not None

    def body(*refs):
        refs = list(refs)
        dho_ref = refs.pop(0)
        if has_n:
            dhn_ref, hn_ref, wn_ref = refs.pop(0), refs.pop(0), refs.pop(0)
        if has_t:
            t_ref, wp_ref = refs.pop(0), refs.pop(0)
        dh_ref = refs.pop(0) if has_n else None
        dt_ref = refs.pop(0) if has_t else None
        dwn_ref = refs.pop(0) if has_n else None
        dwp_ref = refs.pop(0) if has_t else None
        first = pl.program_id(0) == 0
        dh = dho_ref[...]
        if has_n:
            dx, dwn = _rms_bwd(hn_ref[...], wn_ref[...], dhn_ref[...])
            dh = dh + dx
            dh_ref[...] = dh

            @pl.when(first)
            def _():
                dwn_ref[...] = jnp.zeros_like(dwn_ref)
            dwn_ref[...] += dwn
        if has_t:
            dt, dwp = _rms_bwd(t_ref[...], wp_ref[...], dh)
            dt_ref[...] = dt.astype(BF16)

            @pl.when(first)
            def _():
                dwp_ref[...] = jnp.zeros_like(dwp_ref)
            dwp_ref[...] += dwp

    row = pl.BlockSpec((tm, d), lambda i: (i, 0))
    vec = pl.BlockSpec((1, d), lambda i: (0, 0))
    args, in_specs, out_shape, out_specs = [dh_out], [row], [], []
    if has_n:
        args += [dhn, h_new, w_next]
        in_specs += [row, row, vec]
    if has_t:
        args += [t, w_post]
        in_specs += [row, vec]
    if has_n:
        out_shape.append(jax.ShapeDtypeStruct((lp, d), F32)); out_specs.append(row)
    if has_t:
        out_shape.append(jax.ShapeDtypeStruct((lp, d), BF16)); out_specs.append(row)
    if has_n:
        out_shape.append(jax.ShapeDtypeStruct((1, d), F32)); out_specs.append(vec)
    if has_t:
        out_shape.append(jax.ShapeDtypeStruct((1, d), F32)); out_specs.append(vec)
    outs = list(pl.pallas_call(body, out_shape=tuple(out_shape), grid=(lp // tm,), in_specs=in_specs,
                               out_specs=tuple(out_specs), compiler_params=_cp("arbitrary"), name=name)(*args))
    dh = outs.pop(0) if has_n else dh_out
    dt = outs.pop(0) if has_t else None
    dwn = outs.pop(0) if has_n else None
    dwp = outs.pop(0) if has_t else None
    return dh, dt, dwn, dwp


def _loss_head(h, f, w_post, target, *, name):
    lp, d = h.shape
    tm = _row_tile(lp)

    def body(h_ref, f_ref, w_ref, t_ref, loss_ref, dy_ref):
        i = pl.program_id(0)
        y = h_ref[...] + _rms(f_ref[...], w_ref[...])
        rows = i * tm + lax.broadcasted_iota(jnp.int32, (tm, 1), 0)
        tok = (rows >= FRONT) & (rows < lp - BACK)
        err = jnp.where(tok, y - t_ref[...], 0.0)
        dy_ref[...] = err * (1.0 / d)

        @pl.when(i == 0)
        def _():
            loss_ref[...] = jnp.zeros_like(loss_ref)
        part = jnp.sum(jnp.sum(err * err, axis=1, keepdims=True), axis=0, keepdims=True) * (0.5 / d)
        loss_ref[...] += jnp.broadcast_to(part, loss_ref.shape)

    row = pl.BlockSpec((tm, d), lambda i: (i, 0))
    loss, dy = pl.pallas_call(
        body, out_shape=(jax.ShapeDtypeStruct((8, 128), F32), jax.ShapeDtypeStruct((lp, d), F32)),
        grid=(lp // tm,), in_specs=[row, row, pl.BlockSpec((1, d), lambda i: (0, 0)), row],
        out_specs=(pl.BlockSpec((8, 128), lambda i: (0, 0)), row),
        compiler_params=_cp("arbitrary"), name=name)(h, f, w_post, target)
    return loss[0, 0], dy


_GELU_C = math.sqrt(2.0 / math.pi)


def _gelu_and_grad(a):
    a2 = a * a
    t = jnp.tanh(a * (_GELU_C + (_GELU_C * 0.044715) * a2))
    ha = 0.5 * a
    h1 = 0.5 + 0.5 * t
    return a * h1, h1 + ha * (1.0 - t * t) * (_GELU_C + (3.0 * _GELU_C * 0.044715) * a2)


def _gelu(a):
    t = jnp.tanh(a * (_GELU_C + (_GELU_C * 0.044715) * (a * a)))
    return a * (0.5 + 0.5 * t)


def _conv3(parts, n, w, b):
    xx = jnp.concatenate(parts, axis=0)
    x, x1, x2 = xx[8:8 + n], pltpu.roll(xx, 1, 0)[8:8 + n], pltpu.roll(xx, 2, 0)[8:8 + n]
    return b + x * w[2:3] + x1 * w[1:2] + x2 * w[0:1], x, x1, x2


def _conv_act(ua, ug, wa, wg, ba, bg, *, name):
    lp, n = ua.shape
    tm = _row_tile(lp)
    tc = _tile(n, (256, 128))
    nb8 = tm // 8

    def body(ua_ref, uap_ref, ug_ref, ugp_ref, wa_ref, wg_ref, ba_ref, bg_ref, o_ref):
        i = pl.program_id(0)
        ca = _conv3([uap_ref[...], ua_ref[...]], tm, wa_ref[...], ba_ref[...])[0]
        cg = _conv3([ugp_ref[...], ug_ref[...]], tm, wg_ref[...], bg_ref[...])[0]
        rows = i * tm + lax.broadcasted_iota(jnp.int32, (tm, 1), 0)
        ok = (rows >= PADF) & (rows < lp - BACK)
        o_ref[...] = jnp.where(ok, _gelu(ca) * cg, 0.0).astype(BF16)

    cur = pl.BlockSpec((tm, tc), lambda i, j: (i, j))
    prev = pl.BlockSpec((8, tc), lambda i, j: (jnp.maximum(i * nb8 - 1, 0), j))
    w3 = pl.BlockSpec((3, tc), lambda i, j: (0, j))
    b1 = pl.BlockSpec((1, tc), lambda i, j: (0, j))
    return pl.pallas_call(
        body, out_shape=jax.ShapeDtypeStruct((lp, n), BF16), grid=(lp // tm, n // tc),
        in_specs=[cur, prev, cur, prev, w3, w3, b1, b1], out_specs=cur,
        compiler_params=_cp("parallel", "parallel"), name=name)(ua, ua, ug, ug, wa, wg, ba, bg)


def _conv_act_bwd(ua, ug, dact, wa, wg, ba, bg, *, name):
    lp, n = ua.shape
    tm = _row_tile(lp)
    tc = _tile(n, (256, 128))
    nb8 = tm // 8
    last8 = lp // 8 - 1
    ext = tm + 8

    def body(ua_ref, uap_ref, uan_ref, ug_ref, ugp_ref, ugn_ref, da_ref, dan_ref, wa_ref, wg_ref, ba_ref, bg_ref,
             dua_ref, dug_ref, dwa_ref, dwg_ref, dba_ref, dbg_ref):
        i = pl.program_id(1)
        wa, wg = wa_ref[...], wg_ref[...]
        ca, xa, xa1, xa2 = _conv3([uap_ref[...], ua_ref[...], uan_ref[...]], ext, wa, ba_ref[...])
        cg, xg, xg1, xg2 = _conv3([ugp_ref[...], ug_ref[...], ugn_ref[...]], ext, wg, bg_ref[...])
        rows = i * tm + lax.broadcasted_iota(jnp.int32, (ext, 1), 0)
        ok = (rows >= PADF) & (rows < lp - BACK)
        dact_e = jnp.where(ok, jnp.concatenate([da_ref[...], dan_ref[...]], axis=0), 0.0)
        gel, gel_d = _gelu_and_grad(ca)
        dca = dact_e * cg * gel_d
        dcg = dact_e * gel

        def back(dc, w):
            return (dc[:tm] * w[2:3] + pltpu.roll(dc, ext - 1, 0)[:tm] * w[1:2]
                    + pltpu.roll(dc, ext - 2, 0)[:tm] * w[0:1])

        dua_ref[...] = back(dca, wa).astype(BF16)
        dug_ref[...] = back(dcg, wg).astype(BF16)

        @pl.when(i == 0)
        def _():
            dwa_ref[...] = jnp.zeros_like(dwa_ref)
            dwg_ref[...] = jnp.zeros_like(dwg_ref)
            dba_ref[...] = jnp.zeros_like(dba_ref)
            dbg_ref[...] = jnp.zeros_like(dbg_ref)

        def wsum(dw_ref, db_ref, dc, x, x1, x2):
            d = dc[:tm]
            s = lambda v: jnp.sum(v, axis=0, keepdims=True)
            dw_ref[0:1, :] += s(d * x2[:tm])
            dw_ref[1:2, :] += s(d * x1[:tm])
            dw_ref[2:3, :] += s(d * x[:tm])
            db_ref[...] += s(d)

        wsum(dwa_ref, dba_ref, dca, xa, xa1, xa2)
        wsum(dwg_ref, dbg_ref, dcg, xg, xg1, xg2)

    cur = pl.BlockSpec((tm, tc), lambda j, i: (i, j))
    prev = pl.BlockSpec((8, tc), lambda j, i: (jnp.maximum(i * nb8 - 1, 0), j))
    nxt = pl.BlockSpec((8, tc), lambda j, i: (jnp.minimum((i + 1) * nb8, last8), j))
    w3 = pl.BlockSpec((3, tc), lambda j, i: (0, j))
    b1 = pl.BlockSpec((1, tc), lambda j, i: (0, j))
    return pl.pallas_call(
        body,
        out_shape=(jax.ShapeDtypeStruct((lp, n), BF16), jax.ShapeDtypeStruct((lp, n), BF16),
                   jax.ShapeDtypeStruct((3, n), F32), jax.ShapeDtypeStruct((3, n), F32),
                   jax.ShapeDtypeStruct((1, n), F32), jax.ShapeDtypeStruct((1, n), F32)),
        grid=(n // tc, lp // tm),
        in_specs=[cur, prev, nxt, cur, prev, nxt, cur, nxt, w3, w3, b1, b1],
        out_specs=(cur, cur, w3, w3, b1, b1),
        compiler_params=_cp("parallel", "arbitrary"), name=name)(ua, ua, ua, ug, ug, ug, dact, dact, wa, wg, ba, bg)


def _sigmoid(x):
    return 1.0 / (1.0 + jnp.exp(-x))


def _merge(o_ret, o_gla, proj, w_ret, w_gla, *, name):
    lp = o_ret.shape[0]
    tm = _row_tile(lp)

    def body(or_ref, og_ref, rg_ref, gr_ref, wr_ref, wg_ref, m_ref):
        oret, ogla = or_ref[...], og_ref[...]
        yr, yg = [], []
        for h in range(4):
            hs = slice(128 * h, 128 * h + 128)
            o = oret[:, hs]
            xc = o - jnp.mean(o, axis=-1, keepdims=True)
            yr.append(xc * lax.rsqrt(jnp.mean(xc * xc, axis=-1, keepdims=True) + EPS))
            o = ogla[:, hs]
            yg.append(o * lax.rsqrt(jnp.mean(o * o, axis=-1, keepdims=True) + EPS))
        rg, gr = rg_ref[...], gr_ref[...]
        m_ref[:, 0:512] = (jnp.concatenate(yr, axis=1) * wr_ref[...] * (rg * _sigmoid(rg))).astype(BF16)
        m_ref[:, 512:1024] = (jnp.concatenate(yg, axis=1) * wg_ref[...] * (gr * _sigmoid(gr))).astype(BF16)

    row = pl.BlockSpec((tm, 512), lambda i: (i, 0))
    vec = pl.BlockSpec((1, 512), lambda i: (0, 0))
    return pl.pallas_call(
        body, out_shape=jax.ShapeDtypeStruct((lp, 1024), BF16), grid=(lp // tm,),
        in_specs=[row, row, pl.BlockSpec((tm, 512), lambda i: (i, C_RG // 512)),
                  pl.BlockSpec((tm, 512), lambda i: (i, C_GR // 512)), vec, vec],
        out_specs=pl.BlockSpec((tm, 1024), lambda i: (i, 0)),
        compiler_params=_cp("parallel"), name=name)(o_ret, o_gla, proj, proj, w_ret, w_gla)


def _merge_bwd(dm, o_ret, o_gla, proj, w_ret, w_gla, *, name):
    lp = o_ret.shape[0]
    tm = _row_tile(lp)

    def body(dm_ref, or_ref, og_ref, rg_ref, gr_ref, wr_ref, wg_ref, dor_ref, dog_ref, drg_ref, dgr_ref, dwr_ref, dwg_ref):
        @pl.when(pl.program_id(0) == 0)
        def _():
            dwr_ref[...] = jnp.zeros_like(dwr_ref)
            dwg_ref[...] = jnp.zeros_like(dwg_ref)

        def group(d, o_all, gate, w, center):
            sg = _sigmoid(gate)
            s = gate * sg
            ds = sg * (1.0 + gate * (1.0 - sg))
            xh, rr = [], []
            for h in range(4):
                o = o_all[:, 128 * h:128 * h + 128]
                if center:
                    o = o - jnp.mean(o, axis=-1, keepdims=True)
                r = lax.rsqrt(jnp.mean(o * o, axis=-1, keepdims=True) + EPS)
                xh.append(o * r)
                rr.append(r)
            xh_all = jnp.concatenate(xh, axis=1)
            dgate = d * xh_all * w * ds
            dw = jnp.sum(d * xh_all * s, axis=0, keepdims=True)
            dxh_all = d * w * s
            do = []
            for h in range(4):
                dxh = dxh_all[:, 128 * h:128 * h + 128]
                t = dxh - xh[h] * jnp.mean(dxh * xh[h], axis=-1, keepdims=True)
                if center:
                    t = t - jnp.mean(dxh, axis=-1, keepdims=True)
                do.append(rr[h] * t)
            return jnp.concatenate(do, axis=1), dgate, dw

        dmv = dm_ref[...]
        do, dg, dw = group(dmv[:, 0:512], or_ref[...], rg_ref[...], wr_ref[...], True)
        dor_ref[...] = do
        drg_ref[...] = dg.astype(BF16)
        dwr_ref[...] += dw
        do, dg, dw = group(dmv[:, 512:1024], og_ref[...], gr_ref[...], wg_ref[...], False)
        dog_ref[...] = do
        dgr_ref[...] = dg.astype(BF16)
        dwg_ref[...] += dw

    row = pl.BlockSpec((tm, 512), lambda i: (i, 0))
    vec = pl.BlockSpec((1, 512), lambda i: (0, 0))
    return pl.pallas_call(
        body,
        out_shape=(jax.ShapeDtypeStruct((lp, 512), F32), jax.ShapeDtypeStruct((lp, 512), F32),
                   jax.ShapeDtypeStruct((lp, 512), BF16), jax.ShapeDtypeStruct((lp, 512), BF16),
                   jax.ShapeDtypeStruct((1, 512), F32), jax.ShapeDtypeStruct((1, 512), F32)),
        grid=(lp // tm,),
        in_specs=[pl.BlockSpec((tm, 1024), lambda i: (i, 0)), row, row,
                  pl.BlockSpec((tm, 512), lambda i: (i, C_RG // 512)),
                  pl.BlockSpec((tm, 512), lambda i: (i, C_GR // 512)), vec, vec],
        out_specs=(row, row, row, row, vec, vec),
        compiler_params=_cp("arbitrary"), name=name)(dm, o_ret, o_gla, proj, proj, w_ret, w_gla)


def _dot(a, b):
    return lax.dot_general(a, b, (((1,), (0,)), ((), ())), preferred_element_type=F32)


def _dot_nt(a, b):
    return lax.dot_general(a, b, (((1,), (1,)), ((), ())), preferred_element_type=F32)


def _dot_tn(a, b):
    return lax.dot_general(a, b, (((0,), (0,)), ((), ())), preferred_element_type=F32)


def _ret_tables(lp):
    cr = RET_CHUNK
    pos = jnp.arange(lp, dtype=F32) - float(PADF)
    half = RET_DK // 2
    inv = ROPE_BASE ** (-jnp.arange(half, dtype=F32) / half)
    ang = pos[:, None] * inv[None, :]
    c, s = jnp.cos(ang), jnp.sin(ang)
    rope_c = jnp.concatenate([c, c], axis=1)
    rope_s = jnp.concatenate([-s, s], axis=1)
    log_g = np.log(1.0 - 2.0 ** (-5.0 - np.arange(RET_HEADS, dtype=np.float64)))
    idx = np.arange(cr, dtype=np.float64)
    diff = idx[:, None] - idx[None, :]
    dmat = np.where(diff >= 0, np.exp(log_g[:, None, None] * np.maximum(diff, 0.0)), 0.0)
    zeta = np.exp(log_g[:, None] * (cr - 1.0 - idx)[None, :])
    xi = np.exp(log_g[:, None] * (idx + 1.0)[None, :])
    gc = np.exp(log_g * cr)
    f = lambda a: jnp.asarray(a.astype(np.float32))
    return (rope_c, rope_s, f(dmat), f(np.broadcast_to(zeta[:, :, None], (RET_HEADS, cr, 128))),
            f(np.broadcast_to(xi[:, :, None], (RET_HEADS, cr, 128))),
            f(np.broadcast_to(gc[:, None, None], (RET_HEADS, 8, 128))))


def _rope(t, c, s):
    return t * c + pltpu.roll(t, 64, 1) * s


def _rope_t(d, c, s):
    return d * c + pltpu.roll(d * s, 64, 1)


def _ret_specs(nblk, rev):
    ix = (lambda i: nblk - 1 - i) if rev else (lambda i: i)
    cr = RET_CHUNK
    col = lambda base: pl.BlockSpec((BLK, 512), lambda i: (ix(i), base // 512))
    tab = pl.BlockSpec((BLK, 128), lambda i: (ix(i), 0))
    sq = pl.BlockSpec((RET_HEADS, cr, cr), lambda i: (0, 0, 0))
    hv = pl.BlockSpec((RET_HEADS, cr, 128), lambda i: (0, 0, 0))
    g8 = pl.BlockSpec((RET_HEADS, 8, 128), lambda i: (0, 0, 0))
    st = pl.BlockSpec((RET_HEADS, BLK // cr, 128, 128), lambda i: (0, ix(i), 0, 0))
    out = pl.BlockSpec((BLK, 512), lambda i: (ix(i), 0))
    return col, tab, sq, hv, g8, st, out


def _retention(proj, tables, *, name):
    lp = proj.shape[0]
    nblk, cr = lp // BLK, RET_CHUNK
    scale = RET_DK ** -0.5

    def body(q_ref, k_ref, v_ref, c_ref, s_ref, d_ref, z_ref, x_ref, g_ref, o_ref, st_ref, state):
        @pl.when(pl.program_id(0) == 0)
        def _():
            state[...] = jnp.zeros_like(state)

        def chunk(ci, carry):
            sl = pl.ds(pl.multiple_of(ci * cr, cr), cr)
            c, s = c_ref[sl, :], s_ref[sl, :]
            for h in range(RET_HEADS):
                hs = slice(128 * h, 128 * h + 128)
                q = _rope(q_ref[sl, hs], c, s)
                k = _rope(k_ref[sl, hs], c, s) * scale
                qb, kb, vb = q.astype(BF16), k.astype(BF16), v_ref[sl, hs].astype(BF16)
                st = state[h]
                st_ref[h, ci] = st
                sc = _dot_nt(qb, kb) * d_ref[h]
                o_ref[sl, hs] = _dot(sc.astype(BF16), vb) + _dot(qb, st.astype(BF16)) * x_ref[h]
                state[h] = st * g_ref[h][0:1, :] + _dot_tn((k * z_ref[h]).astype(BF16), vb)
            return carry

        lax.fori_loop(0, BLK // cr, chunk, 0)

    col, tab, sq, hv, g8, st, out = _ret_specs(nblk, False)
    return pl.pallas_call(
        body,
        out_shape=(jax.ShapeDtypeStruct((lp, 512), F32), jax.ShapeDtypeStruct((4, lp // cr, 128, 128), F32)),
        grid=(nblk,), in_specs=[col(C_RQ), col(C_RK), col(C_RV), tab, tab, sq, hv, hv, g8],
        out_specs=(out, st), scratch_shapes=[pltpu.VMEM((RET_HEADS, 128, 128), F32)],
        compiler_params=_cp("arbitrary"), name=name)(proj, proj, proj, *tables)


def _retention_bwd(proj, do, states, tables, *, name):
    lp = proj.shape[0]
    nblk, cr = lp // BLK, RET_CHUNK
    nch = BLK // cr
    scale = RET_DK ** -0.5

    def body(q_ref, k_ref, v_ref, do_ref, st_ref, c_ref, s_ref, d_ref, z_ref, x_ref, g_ref, dq_ref, dk_ref, dv_ref, dstate):
        @pl.when(pl.program_id(0) == 0)
        def _():
            dstate[...] = jnp.zeros_like(dstate)

        def chunk(cc, carry):
            ci = nch - 1 - cc
            sl = pl.ds(pl.multiple_of(ci * cr, cr), cr)
            c, s = c_ref[sl, :], s_ref[sl, :]
            for h in range(RET_HEADS):
                hs = slice(128 * h, 128 * h + 128)
                dmat, zeta, xi = d_ref[h], z_ref[h], x_ref[h]
                q = _rope(q_ref[sl, hs], c, s)
                k = _rope(k_ref[sl, hs], c, s) * scale
                qb, kb, vb = q.astype(BF16), k.astype(BF16), v_ref[sl, hs].astype(BF16)
                kzb = (k * zeta).astype(BF16)
                dov = do_ref[sl, hs]
                dob, doxb = dov.astype(BF16), (dov * xi).astype(BF16)
                stb = st_ref[h, ci].astype(BF16)
                dsn = dstate[h]
                dsnb = dsn.astype(BF16)
                scb = (_dot_nt(qb, kb) * dmat).astype(BF16)
                dscb = (_dot_nt(dob, vb) * dmat).astype(BF16)
                dq = _dot(dscb, kb) + _dot_nt(doxb, stb)
                dk = _dot_tn(dscb, qb) + _dot_nt(vb, dsnb) * zeta
                dv = _dot_tn(scb, dob) + _dot(kzb, dsnb)
                dstate[h] = dsn * g_ref[h][0:1, :] + _dot_tn(qb, doxb)
                dq_ref[sl, hs] = _rope_t(dq, c, s).astype(BF16)
                dk_ref[sl, hs] = _rope_t(dk * scale, c, s).astype(BF16)
                dv_ref[sl, hs] = dv.astype(BF16)
            return carry

        lax.fori_loop(0, nch, chunk, 0)

    col, tab, sq, hv, g8, st, out = _ret_specs(nblk, True)
    o3 = jax.ShapeDtypeStruct((lp, 512), BF16)
    return pl.pallas_call(
        body, out_shape=(o3, o3, o3), grid=(nblk,),
        in_specs=[col(C_RQ), col(C_RK), col(C_RV), out, st, tab, tab, sq, hv, hv, g8],
        out_specs=(out, out, out), scratch_shapes=[pltpu.VMEM((RET_HEADS, 128, 128), F32)],
        compiler_params=_cp("arbitrary"), name=name)(proj, proj, proj, do, states, *tables)


def _gla_tables():
    c = GLA_CHUNK
    tri = np.tril(np.ones((c, c), np.float32))
    ones_qv = np.kron(np.eye(GLA_HEADS, dtype=np.float32), np.ones((GLA_DK, GLA_DV), np.float32))
    return (jnp.asarray(tri, BF16), jnp.asarray(tri.T.copy(), BF16), jnp.asarray(ones_qv, BF16),
            jnp.asarray(ones_qv.T.copy(), BF16))


def _split3(x):
    hi = x.astype(BF16)
    r1 = x - hi.astype(F32)
    mid = r1.astype(BF16)
    lo = (r1 - mid.astype(F32)).astype(BF16)
    return hi, mid, lo


def _tri_sum(tri, x):
    hi, mid, lo = _split3(x)
    return _dot(tri, hi) + _dot(tri, mid) + _dot(tri, lo)


def _head_masks(width, per):
    lane = lax.broadcasted_iota(jnp.int32, (1, width), 1)
    return [((lane >= per * h) & (lane < per * (h + 1))).astype(F32) for h in range(GLA_HEADS)]


def _stack_heads(x, masks):
    return jnp.concatenate([x * m for m in masks], axis=0)


def _gla_gate(ga, w2, b, ok, tri):
    z = _dot(ga.astype(BF16), w2) + b
    la = (jnp.minimum(z, 0.0) - jnp.log(1.0 + jnp.exp(-jnp.abs(z)))) * (1.0 / GLA_TAU)
    la = jnp.where(ok, la, 0.0)
    return z, _tri_sum(tri, la)


def _gla_rows(i_blk, ci, lp):
    c = GLA_CHUNK
    rows = i_blk * BLK + ci * c + lax.broadcasted_iota(jnp.int32, (c, 1), 0)
    return (rows >= PADF) & (rows < lp - BACK)


def _gla_off_parts(a, qs, k, g, hm_q):
    s = GLA_SUB
    ra = g[s * a - 1:s * a, :]
    ga_ = g[s * a:s * a + s, :]
    eq = jnp.exp(ga_ - ra)
    ek = jnp.exp(jnp.minimum(ra - g, 0.0))
    qh = qs[s * a:s * a + s, :] * eq
    kh = k * ek
    qst = _stack_heads(qh, hm_q).astype(BF16)
    col = lax.broadcasted_iota(jnp.int32, (GLA_HEADS * s, GLA_CHUNK), 1)
    pmask = col < s * a
    p = jnp.where(pmask, _dot_nt(qst, kh.astype(BF16)), 0.0)
    return eq, ek, qh, kh, qst, pmask, p


def _lag_mask(j):
    r = lax.broadcasted_iota(jnp.int32, (GLA_CHUNK, 1), 0)
    return (jnp.bitwise_and(r, GLA_SUB - 1) >= j).astype(F32)


def _roll_rows(x, j):
    return x if j == 0 else pltpu.roll(x, j, 0)


def _gla(proj, w2p, b, tables, *, name):
    lp = proj.shape[0]
    nblk, c, s = lp // BLK, GLA_CHUNK, GLA_SUB
    nch = BLK // c
    na = c // s

    def body(q_ref, k_ref, v_ref, a_ref, w_ref, b_ref, tri_ref, ones_ref, o_ref, st_ref, state):
        i_blk = pl.program_id(0)

        @pl.when(i_blk == 0)
        def _():
            state[...] = jnp.zeros_like(state)
        hm_q = _head_masks(GLA_QK, GLA_DK)
        tri, ones_qv, w2, bias = tri_ref[...], ones_ref[...], w_ref[...], b_ref[...]

        def chunk(ci, carry):
            sl = pl.ds(pl.multiple_of(ci * c, c), c)
            ok = _gla_rows(i_blk, ci, lp)
            k, v = k_ref[sl, :], v_ref[sl, :]
            vb = v.astype(BF16)
            qs = q_ref[sl, :] * (GLA_DK ** -0.5)
            _, g = _gla_gate(a_ref[sl, :], w2, bias, ok, tri)
            last = g[c - 1:c, :]
            st = state[...]
            st_ref[ci] = st
            qst = _stack_heads(qs * jnp.exp(g), hm_q).astype(BF16)
            oi = _dot_nt(qst, st.astype(BF16))
            o = jnp.concatenate([oi[c * h:c * h + c, :] for h in range(GLA_HEADS)], axis=1)
            ke = k * jnp.exp(last - g)
            f = _dot_tn(vb, ke.astype(BF16))
            upd = f[0:GLA_DV, :] * hm_q[0]
            for h in range(1, GLA_HEADS):
                upd = upd + f[GLA_DV * h:GLA_DV * (h + 1), :] * hm_q[h]
            state[...] = st * jnp.exp(last) + upd
            off = [jnp.zeros((s, GLA_V), F32)]
            for a in range(1, na):
                p = _gla_off_parts(a, qs, k, g, hm_q)[-1]
                ob = _dot(p.astype(BF16), vb)
                off.append(jnp.concatenate(
                    [ob[s * h:s * h + s, GLA_DV * h:GLA_DV * (h + 1)] for h in range(GLA_HEADS)], axis=1))
            o = o + jnp.concatenate(off, axis=0)
            ws = []
            for j in range(s):
                ej = jnp.exp(jnp.minimum(g - _roll_rows(g, j), 0.0))
                ws.append((qs * _roll_rows(k, j) * ej * _lag_mask(j)).astype(BF16))
            ball = _dot(jnp.concatenate(ws, axis=0), ones_qv)
            for j in range(s):
                o = o + ball[c * j:c * j + c, :] * _roll_rows(v, j)
            o_ref[sl, :] = o
            return carry

        lax.fori_loop(0, nch, chunk, 0)

    tri, _, ones_qv, _ = tables
    full = lambda arr: pl.BlockSpec(arr.shape, lambda i: (0,) * arr.ndim)
    return pl.pallas_call(
        body,
        out_shape=(jax.ShapeDtypeStruct((lp, GLA_V), F32), jax.ShapeDtypeStruct((lp // c, GLA_DV, GLA_QK), F32)),
        grid=(nblk,),
        in_specs=[pl.BlockSpec((BLK, GLA_QK), lambda i: (i, C_GQ // GLA_QK)),
                  pl.BlockSpec((BLK, GLA_QK), lambda i: (i, C_GK // GLA_QK)),
                  pl.BlockSpec((BLK, GLA_V), lambda i: (i, C_GV // GLA_V)),
                  pl.BlockSpec((BLK, 128), lambda i: (i, C_GA // 128)),
                  full(w2p), full(b), full(tri), full(ones_qv)],
        out_specs=(pl.BlockSpec((BLK, GLA_V), lambda i: (i, 0)),
                   pl.BlockSpec((nch, GLA_DV, GLA_QK), lambda i: (i, 0, 0))),
        scratch_shapes=[pltpu.VMEM((GLA_DV, GLA_QK), F32)],
        compiler_params=_cp("arbitrary"), name=name)(proj, proj, proj, proj, w2p, b, tri, ones_qv)


def _gla_bwd(proj, do, states, w2p, b, tables, *, name):
    lp = proj.shape[0]
    nblk, c, s = lp // BLK, GLA_CHUNK, GLA_SUB
    nch = BLK // c
    na = c // s

    def body(q_ref, k_ref, v_ref, a_ref, do_ref, st_ref, w_ref, b_ref, tri_ref, trit_ref, ones_ref, onest_ref,
             dq_ref, dk_ref, dv_ref, da_ref, dw_ref, db_ref, dstate, dqs_s, dk_s, dg_s, dv_s):
        i_blk = nblk - 1 - pl.program_id(0)

        @pl.when(pl.program_id(0) == 0)
        def _():
            dstate[...] = jnp.zeros_like(dstate)
            dw_ref[...] = jnp.zeros_like(dw_ref)
            db_ref[...] = jnp.zeros_like(db_ref)
        hm_q = _head_masks(GLA_QK, GLA_DK)
        hm_v = _head_masks(GLA_V, GLA_DV)
        tri, trit, ones_qv, ones_vq = tri_ref[...], trit_ref[...], ones_ref[...], onest_ref[...]
        w2, bias = w_ref[...], b_ref[...]
        rsum = lambda x: jnp.sum(x, axis=0, keepdims=True)

        def chunk(cc, carry):
            ci = nch - 1 - cc
            sl = pl.ds(pl.multiple_of(ci * c, c), c)
            ok = _gla_rows(i_blk, ci, lp)
            k, v, ga = k_ref[sl, :], v_ref[sl, :], a_ref[sl, :]
            vb = v.astype(BF16)
            qs = q_ref[sl, :] * (GLA_DK ** -0.5)
            z, g = _gla_gate(ga, w2, bias, ok, tri)
            last = g[c - 1:c, :]
            elast = jnp.exp(last)
            eg = jnp.exp(g)
            ekl = jnp.exp(last - g)
            qe, ke = qs * eg, k * ekl
            dov = do_ref[sl, :]
            st = st_ref[ci]
            dsn = dstate[...]
            qst = _stack_heads(qe, hm_q).astype(BF16)
            dost = jnp.concatenate([dov[:, GLA_DV * h:GLA_DV * (h + 1)] for h in range(GLA_HEADS)], axis=0).astype(BF16)
            dqe_st = _dot(dost, st.astype(BF16))
            dqe = dqe_st[0:c, :] * hm_q[0]
            for h in range(1, GLA_HEADS):
                dqe = dqe + dqe_st[c * h:c * h + c, :] * hm_q[h]
            dstate[...] = _dot_tn(dost, qst) + dsn * elast
            dlast = rsum(dsn * st) * elast
            df = _stack_heads(dsn, hm_q).astype(BF16)
            dv_s[...] = _dot_nt(ke.astype(BF16), df)
            dke = _dot(vb, df)
            xk = dke * ke
            dqs_s[...] = dqe * eg
            dk_s[...] = dke * ekl
            dg_s[...] = dqe * qe - xk
            dlast = dlast + rsum(xk)
            for a in range(1, na):
                eq, ek, qh, kh, qsa, pmask, p = _gla_off_parts(a, qs, k, g, hm_q)
                rows = slice(s * a, s * a + s)
                dofull = _stack_heads(dov[rows, :], hm_v).astype(BF16)
                dp = jnp.where(pmask, _dot_nt(dofull, vb), 0.0).astype(BF16)
                dv_s[...] += _dot_tn(p.astype(BF16), dofull)
                dq_st = _dot(dp, kh.astype(BF16))
                dqh = dq_st[0:s, :] * hm_q[0]
                for h in range(1, GLA_HEADS):
                    dqh = dqh + dq_st[s * h:s * h + s, :] * hm_q[h]
                dkh = _dot_tn(dp, qsa)
                xq = dqh * qh
                xkh = dkh * kh
                dqs_s[rows, :] += dqh * eq
                dg_s[rows, :] += xq
                dk_s[...] += dkh * ek
                dg_s[...] -= xkh
                dg_s[s * a - 1:s * a, :] += rsum(xkh) - rsum(xq)
            kes, qes, ws, dbs = [], [], [], []
            for j in range(s):
                em = jnp.exp(jnp.minimum(g - _roll_rows(g, j), 0.0)) * _lag_mask(j)
                kes.append(_roll_rows(k, j) * em)
                qes.append(qs * em)
                ws.append((qs * kes[j]).astype(BF16))
                dbs.append((dov * _roll_rows(v, j)).astype(BF16))
            ball = _dot(jnp.concatenate(ws, axis=0), ones_qv)
            dwall = _dot(jnp.concatenate(dbs, axis=0), ones_vq)
            for j in range(s):
                back = (lambda x: x) if j == 0 else (lambda x, j=j: pltpu.roll(x, c - j, 0))
                dw = dwall[c * j:c * j + c, :]
                dv_s[...] += back(ball[c * j:c * j + c, :] * dov)
                dqs_s[...] += dw * kes[j]
                dk_s[...] += back(dw * qes[j])
                x = dw * qs * kes[j]
                dg_s[...] += x - back(x)
            dg_s[c - 1:c, :] += dlast
            dla = jnp.where(ok, _tri_sum(trit, dg_s[...]), 0.0)
            dz = dla * (1.0 / GLA_TAU) / (1.0 + jnp.exp(z))
            dzb = dz.astype(BF16)
            dq_ref[sl, :] = (dqs_s[...] * (GLA_DK ** -0.5)).astype(BF16)
            dk_ref[sl, :] = dk_s[...].astype(BF16)
            dv_ref[sl, :] = dv_s[...].astype(BF16)
            da_ref[sl, :] = _dot_nt(dzb, w2).astype(BF16)
            dw_ref[...] += _dot_tn(ga.astype(BF16), dzb)
            db_ref[...] += rsum(dz)
            return carry

        lax.fori_loop(0, nch, chunk, 0)

    tri, trit, ones_qv, ones_vq = tables
    full = lambda arr: pl.BlockSpec(arr.shape, lambda i: (0,) * arr.ndim)
    rev = lambda i: nblk - 1 - i
    qk = jax.ShapeDtypeStruct((lp, GLA_QK), BF16)
    return pl.pallas_call(
        body,
        out_shape=(qk, qk, jax.ShapeDtypeStruct((lp, GLA_V), BF16), jax.ShapeDtypeStruct((lp, 128), BF16),
                   jax.ShapeDtypeStruct((128, GLA_QK), F32), jax.ShapeDtypeStruct((1, GLA_QK), F32)),
        grid=(nblk,),
        in_specs=[pl.BlockSpec((BLK, GLA_QK), lambda i: (rev(i), C_GQ // GLA_QK)),
                  pl.BlockSpec((BLK, GLA_QK), lambda i: (rev(i), C_GK // GLA_QK)),
                  pl.BlockSpec((BLK, GLA_V), lambda i: (rev(i), C_GV // GLA_V)),
                  pl.BlockSpec((BLK, 128), lambda i: (rev(i), C_GA // 128)),
                  pl.BlockSpec((BLK, GLA_V), lambda i: (rev(i), 0)),
                  pl.BlockSpec((nch, GLA_DV, GLA_QK), lambda i: (rev(i), 0, 0)),
                  full(w2p), full(b), full(tri), full(trit), full(ones_qv), full(ones_vq)],
        out_specs=(pl.BlockSpec((BLK, GLA_QK), lambda i: (rev(i), 0)),
                   pl.BlockSpec((BLK, GLA_QK), lambda i: (rev(i), 0)),
                   pl.BlockSpec((BLK, GLA_V), lambda i: (rev(i), 0)),
                   pl.BlockSpec((BLK, 128), lambda i: (rev(i), 0)),
                   pl.BlockSpec((128, GLA_QK), lambda i: (0, 0)),
                   pl.BlockSpec((1, GLA_QK), lambda i: (0, 0))),
        scratch_shapes=[pltpu.VMEM((GLA_DV, GLA_QK), F32), pltpu.VMEM((c, GLA_QK), F32),
                        pltpu.VMEM((c, GLA_QK), F32), pltpu.VMEM((c, GLA_QK), F32), pltpu.VMEM((c, GLA_V), F32)],
        compiler_params=_cp("arbitrary"), name=name)(proj, proj, proj, proj, do, states, w2p, b, tri, trit, ones_qv, ones_vq)


def _as2d(a):
    return a.reshape(-1, a.shape[-1])


def _ew_tile(r):
    return _tile(r, (512, 256, 128, 64, 32, 16, 8))


def _add2(a, b, *, out_dtype, name):
    a2, b2 = _as2d(a), _as2d(b)
    r, n = a2.shape
    tm = _ew_tile(r)

    def body(a_ref, b_ref, o_ref):
        o_ref[...] = (a_ref[...] + b_ref[...]).astype(o_ref.dtype)

    blk = pl.BlockSpec((tm, n), lambda i: (i, 0))
    return pl.pallas_call(body, out_shape=jax.ShapeDtypeStruct((r, n), out_dtype), grid=(r // tm,), in_specs=[blk, blk],
                          out_specs=blk, compiler_params=_cp("parallel"), name=name)(a2, b2).reshape(a.shape)


def _sum_slots(own, q, *, name):
    shape = own.shape
    q3 = q.reshape(3, -1, shape[-1])
    own2 = _as2d(own)
    r, n = own2.shape
    tm = _ew_tile(r)

    def body(own_ref, q_ref, o_ref):
        f = lambda i: q_ref[i].astype(F32)
        o_ref[...] = ((own_ref[...].astype(F32) + f(0)) + f(1)) + f(2)

    blk = pl.BlockSpec((tm, n), lambda i: (i, 0))
    return pl.pallas_call(
        body, out_shape=jax.ShapeDtypeStruct((r, n), F32), grid=(r // tm,),
        in_specs=[blk, pl.BlockSpec((3, tm, n), lambda i: (0, i, 0))], out_specs=blk,
        compiler_params=_cp("parallel"), name=name)(own2, q3).reshape(shape)


def _adamw(w, g, m, v, *, name):
    shape = w.shape
    w2, g2, m2, v2 = _as2d(w), _as2d(g), _as2d(m), _as2d(v)
    r, n = w2.shape
    tm = _ew_tile(r)
    c1 = 1.0 - ADAM_B1 ** ADAM_STEP
    c2 = 1.0 - ADAM_B2 ** ADAM_STEP

    def body(w_ref, g_ref, m_ref, v_ref, d_ref, mo_ref, vo_ref):
        gv = g_ref[...]
        mn = ADAM_B1 * m_ref[...] + (1.0 - ADAM_B1) * gv
        vn = ADAM_B2 * v_ref[...] + (1.0 - ADAM_B2) * (gv * gv)
        mo_ref[...] = mn
        vo_ref[...] = vn
        d_ref[...] = -ADAM_LR * ((mn / c1) / (jnp.sqrt(vn / c2) + ADAM_EPS) + ADAM_WD * w_ref[...])

    blk = pl.BlockSpec((tm, n), lambda i: (i, 0))
    o = jax.ShapeDtypeStruct((r, n), F32)
    d, mo, vo = pl.pallas_call(body, out_shape=(o, o, o), grid=(r // tm,), in_specs=[blk] * 4, out_specs=(blk,) * 3,
                               compiler_params=_cp("parallel"), name=name)(w2, g2, m2, v2)
    return d.reshape(shape), mo.reshape(shape), vo.reshape(shape)


ANY = pl.BlockSpec(memory_space=pl.ANY)


def _place():
    return lax.axis_index("x"), lax.axis_index("y"), lax.axis_index("c")


def _other_chips(x, y):
    return [(1 - x, y), (x, 1 - y), (1 - x, 1 - y)]


def _remote(src, dst, ssem, rsem, dev):
    return pltpu.make_async_remote_copy(src_ref=src, dst_ref=dst, send_sem=ssem, recv_sem=rsem, device_id=dev,
                                        device_id_type=MESH)


def _allgather_chips(arrs, *, name):
    n = len(arrs)

    def body(*refs):
        ins, outs = refs[:n], refs[n:2 * n]
        s1, r1, s2, r2 = refs[2 * n:]
        x, y, c = _place()
        q = 2 * x + y
        chips = _other_chips(x, y)
        qs = [2 * cx + cy for cx, cy in chips]
        sib = (x, y, 1 - c)
        first, passed = [], []
        for k in range(n):
            for j, chip in enumerate(chips):
                first.append(_remote(ins[k].at[c], outs[k].at[c, q], s1.at[k, j], r1.at[k, j], (*chip, c)))
        for cp in first:
            cp.start()
        for k in range(n):
            for j, chip in enumerate(chips):
                land = outs[k].at[c, qs[j]]
                _remote(land, land, s1.at[k, j], r1.at[k, j], (*chip, c)).wait_recv()
                fw = _remote(land, land, s2.at[k, j], r2.at[k, j], sib)
                fw.start()
                passed.append(fw)
        for k in range(n):
            for j in range(3):
                land = outs[k].at[1 - c, qs[j]]
                _remote(land, land, s2.at[k, j], r2.at[k, j], sib).wait_recv()
        for cp in first + passed:
            cp.wait_send()

    sem = pltpu.SemaphoreType.DMA
    outs = pl.pallas_call(
        body, out_shape=tuple(jax.ShapeDtypeStruct((2, 4) + a.shape[1:], a.dtype) for a in arrs),
        in_specs=[ANY] * n, out_specs=(ANY,) * n,
        scratch_shapes=[sem((n, 3)), sem((n, 3)), sem((n, 3)), sem((n, 3))], name=name)(*arrs)
    chip = 2 * lax.axis_index("x") + lax.axis_index("y")
    return [lax.dynamic_update_slice_in_dim(o, a[:, None], chip, axis=1) for o, a in zip(outs, arrs)]


def _pair_exchange(arrs, *, name):
    n = len(arrs)

    def body(*refs):
        ins, outs = refs[:n], refs[n:2 * n]
        ssem, rsem = refs[2 * n:]
        x, y, c = _place()
        cps = [_remote(ins[k].at[1 - c], outs[k], ssem.at[k], rsem.at[k], (x, y, 1 - c)) for k in range(n)]
        for cp in cps:
            cp.start()
        for cp in cps:
            cp.wait()

    sem = pltpu.SemaphoreType.DMA
    return pl.pallas_call(
        body, out_shape=tuple(jax.ShapeDtypeStruct(a.shape[1:], a.dtype) for a in arrs),
        in_specs=[ANY] * n, out_specs=(ANY,) * n, scratch_shapes=[sem((n,)), sem((n,))], name=name)(*arrs)


def _chip_exchange(arrs, *, name):
    n = len(arrs)

    def body(*refs):
        ins, outs = refs[:n], refs[n:2 * n]
        ssem, rsem = refs[2 * n:]
        x, y, c = _place()
        cps = []
        for k in range(n):
            for j, (cx, cy) in enumerate(_other_chips(x, y)):
                cps.append(_remote(ins[k].at[2 * cx + cy], outs[k].at[j], ssem.at[k, j], rsem.at[k, j], (cx, cy, c)))
        for cp in cps:
            cp.start()
        for cp in cps:
            cp.wait()

    sem = pltpu.SemaphoreType.DMA
    return pl.pallas_call(
        body, out_shape=tuple(jax.ShapeDtypeStruct((3,) + a.shape[1:], a.dtype) for a in arrs),
        in_specs=[ANY] * n, out_specs=(ANY,) * n, scratch_shapes=[sem((n, 3)), sem((n, 3))], name=name)(*arrs)


def _pair_swap(arrs, *, name):
    n = len(arrs)

    def body(*refs):
        ins, outs = refs[:n], refs[n:2 * n]
        ssem, rsem = refs[2 * n:]
        x, y, c = _place()
        cps = [_remote(ins[k], outs[k], ssem.at[k], rsem.at[k], (x, y, 1 - c)) for k in range(n)]
        for cp in cps:
            cp.start()
        for cp in cps:
            cp.wait()

    sem = pltpu.SemaphoreType.DMA
    return pl.pallas_call(
        body, out_shape=tuple(jax.ShapeDtypeStruct(a.shape, a.dtype) for a in arrs),
        in_specs=[ANY] * n, out_specs=(ANY,) * n, scratch_shapes=[sem((n,)), sem((n,))], name=name)(*arrs)


def _allreduce_small(slab, *, name):
    r, n = slab.shape

    def body(x_ref, o_ref, buf, ssem, rsem):
        x, y, c = _place()
        me = 4 * x + 2 * y + c
        buf[me] = x_ref[...]
        cps = []
        for rel in range(1, 8):
            bx, by, bc = (rel >> 2) & 1, (rel >> 1) & 1, rel & 1
            px, py, pc = (x + bx) % 2, (y + by) % 2, (c + bc) % 2
            cps.append((_remote(x_ref, buf.at[me], ssem.at[rel - 1], rsem.at[rel - 1], (px, py, pc)),
                        4 * px + 2 * py + pc, (px, py, pc)))
        for cp, _, _ in cps:
            cp.start()
        for rel, (cp, peer, dev) in enumerate(cps):
            cp.wait_send()
            _remote(x_ref, buf.at[peer], ssem.at[rel], rsem.at[rel], dev).wait_recv()
        acc = buf[0]
        for k in range(1, 8):
            acc = acc + buf[k]
        o_ref[...] = acc

    vm = pl.BlockSpec(memory_space=pltpu.VMEM)
    sem = pltpu.SemaphoreType.DMA
    return pl.pallas_call(
        body, out_shape=jax.ShapeDtypeStruct((r, n), F32), in_specs=[vm], out_specs=vm,
        scratch_shapes=[pltpu.VMEM((8, r, n), F32), sem((7,)), sem((7,))], name=name)(slab)


def _slab(arrs, row_mult):
    flat = jnp.concatenate([a.reshape(-1) for a in arrs])
    unit = 128 * row_mult
    total = -(-flat.size // unit) * unit
    return jnp.pad(flat, (0, total - flat.size)).reshape(-1, 128)


def _unslab(slab, shapes):
    flat = slab.reshape(-1)
    out, off = [], 0
    for s in shapes:
        size = int(np.prod(s))
        out.append(flat[off:off + size].reshape(s))
        off += size
    return out


def _cols_from_chips(a):
    return jnp.transpose(a, (1, 0, 2)).reshape(a.shape[1], -1)


def _cols_to_chips(a, parts):
    r = a.shape[0]
    return jnp.transpose(a.reshape(r, parts, -1), (1, 0, 2))


def _gather_params(w_in, w_out, ffn_up, ffn_down, meta_tokens, gla_gate_w2, ffn_conv_w):
    d = D_MODEL
    sh_shapes = [meta_tokens.shape, gla_gate_w2.shape, ffn_conv_w.shape]
    sh_slab = _slab([meta_tokens, gla_gate_w2, ffn_conv_w], 16)
    g_in, g_out, g_up, g_down, g_sh = _allgather_chips(
        [w_in.astype(BF16), w_out.astype(BF16), ffn_up.astype(BF16), ffn_down.astype(BF16),
         sh_slab.reshape(2, -1, 128)], name="gather_weights")
    sh_all = jnp.transpose(g_sh, (1, 0, 2, 3)).reshape(4, -1, 128)
    parts = [_unslab(sh_all[k], sh_shapes) for k in range(4)]
    meta_full = jnp.concatenate([p[0] for p in parts], axis=-1)
    w2_full = jnp.concatenate([p[1] for p in parts], axis=-1)
    cw_full = jnp.concatenate([p[2] for p in parts], axis=-1)

    win = [jnp.pad(_cols_from_chips(g_in[l]), ((0, 0), (0, IN_PAD - IN_WIDTH))) for l in range(DEPTH)]
    wout = [g_out[l].reshape(d, d) for l in range(DEPTH)]
    up_a = [_cols_from_chips(g_up[l, 0:2]) for l in range(DEPTH)]
    up_g = [_cols_from_chips(g_up[l, 2:4]) for l in range(DEPTH)]
    down = [g_down[l].reshape(D_FF, d) for l in range(DEPTH)]
    w2p = [jnp.pad(w2_full[l], ((0, 128 - GLA_RANK), (0, 0))).astype(BF16) for l in range(DEPTH)]
    return meta_full, win, wout, up_a, up_g, down, w2p, cw_full


def _local_step(x_rows, target_rows, meta_full, win, wout, up_a, up_g, down, w2p, cw_full, pre_mix_norm, gla_gate_b,
                ret_norm_w, gla_norm_w, post_mix_norm, pre_ffn_norm, ffn_conv_b, post_ffn_norm):
    d = D_MODEL
    lp = x_rows.shape[0] + FRONT + BACK
    row = lambda a, l: a[l][None, :]
    rtab = _ret_tables(lp)
    gtab = _gla_tables()
    h0 = jnp.concatenate([jnp.zeros((PADF, d), F32), meta_full, x_rows, jnp.zeros((BACK, d), F32)], axis=0)
    target = jnp.pad(target_rows, ((FRONT, BACK), (0, 0)))

    saved = []
    h = h0
    _, hn = _resid_norm(h0, None, None, row(pre_mix_norm, 0), name="norm_in")
    loss_local = dy = None
    for l in range(DEPTH):
        s = {"h_in": h, "hn": hn}
        s["proj"] = _mm(hn, win[l], name="proj")
        s["o_ret"], s["st_ret"] = _retention(s["proj"], rtab, name="retention")
        s["o_gla"], s["st_gla"] = _gla(s["proj"], w2p[l], row(gla_gate_b, l), gtab, name="gla")
        s["merged"] = _merge(s["o_ret"], s["o_gla"], s["proj"], row(ret_norm_w, l), row(gla_norm_w, l), name="merge")
        s["m"] = _mm(s["merged"], wout[l], name="mix_out")
        s["h_mid"], s["hn2"] = _resid_norm(h, s["m"], row(post_mix_norm, l), row(pre_ffn_norm, l), name="resid_mix")
        s["ua"] = _mm(s["hn2"], up_a[l], name="ffn_up_a")
        s["ug"] = _mm(s["hn2"], up_g[l], name="ffn_up_g")
        cw_a, cw_g = cw_full[l][:, :D_FF], cw_full[l][:, D_FF:]
        cb_a, cb_g = ffn_conv_b[l][None, :D_FF], ffn_conv_b[l][None, D_FF:]
        s["conv"] = (cw_a, cw_g, cb_a, cb_g)
        s["act"] = _conv_act(s["ua"], s["ug"], cw_a, cw_g, cb_a, cb_g, name="conv_act")
        s["f"] = _mm(s["act"], down[l], name="ffn_down")
        if l + 1 < DEPTH:
            h, hn = _resid_norm(s["h_mid"], s["f"], row(post_ffn_norm, l), row(pre_mix_norm, l + 1), name="resid_ffn")
        else:
            loss_local, dy = _loss_head(s["h_mid"], s["f"], row(post_ffn_norm, l), target, name="loss_head")
        saved.append(s)

    g = {k: [None] * DEPTH for k in ("pre_mix", "w_in", "w2", "gb", "ret_n", "gla_n", "w_out", "post_mix", "pre_ffn",
                                     "up_a", "up_g", "cw", "cb", "down", "post_ffn")}
    dh_out, dhn_next = dy, None
    for l in reversed(range(DEPTH)):
        s = saved[l]
        cw_a, cw_g, cb_a, cb_g = s["conv"]
        if l + 1 < DEPTH:
            dh, df, g["pre_mix"][l + 1], g["post_ffn"][l] = _resid_norm_bwd(
                dh_out, dhn_next, saved[l + 1]["h_in"], s["f"], row(pre_mix_norm, l + 1), row(post_ffn_norm, l),
                name="resid_ffn_bwd")
        else:
            dh, df, _, g["post_ffn"][l] = _resid_norm_bwd(dh_out, None, None, s["f"], None, row(post_ffn_norm, l),
                                                          name="loss_head_bwd")
        dact = _mm(df, down[l], nt=True, name="ffn_down_dx")
        g["down"][l] = _mm_tn(s["act"], df, tn=512, name="ffn_down_dw")
        du_a, du_g, dcw_a, dcw_g, dcb_a, dcb_g = _conv_act_bwd(s["ua"], s["ug"], dact, cw_a, cw_g, cb_a, cb_g,
                                                               name="conv_act_bwd")
        g["cw"][l] = jnp.concatenate([dcw_a, dcw_g], axis=1)
        g["cb"][l] = jnp.concatenate([dcb_a, dcb_g], axis=1)[0]
        g["up_a"][l] = _mm_tn(s["hn2"], du_a, tn=1408, name="ffn_up_a_dw")
        g["up_g"][l] = _mm_tn(s["hn2"], du_g, tn=1408, name="ffn_up_g_dw")
        dhn2 = _mm(du_a, up_a[l], nt=True, name="ffn_up_a_dx")
        dhn2 = _mm(du_g, up_g[l], nt=True, add=dhn2, name="ffn_up_g_dx")
        dh, dm, g["pre_ffn"][l], g["post_mix"][l] = _resid_norm_bwd(
            dh, dhn2, s["h_mid"], s["m"], row(pre_ffn_norm, l), row(post_mix_norm, l), name="resid_mix_bwd")
        g["w_out"][l] = _mm_tn(s["merged"], dm, name="mix_out_dw")
        dmerged = _mm(dm, wout[l], nt=True, name="mix_out_dx")
        do_ret, do_gla, drg, dgr, g["ret_n"][l], g["gla_n"][l] = _merge_bwd(
            dmerged, s["o_ret"], s["o_gla"], s["proj"], row(ret_norm_w, l), row(gla_norm_w, l), name="merge_bwd")
        drq, drk, drv = _retention_bwd(s["proj"], do_ret, s["st_ret"], rtab, name="retention_bwd")
        dgq, dgk, dgv, dga, dw2, dgb = _gla_bwd(s["proj"], do_gla, s["st_gla"], w2p[l], row(gla_gate_b, l), gtab,
                                                name="gla_bwd")
        g["w2"][l], g["gb"][l] = dw2[:GLA_RANK], dgb[0]
        dproj = jnp.concatenate([drq, drk, drv, drg, dgq, dgk, dgv, dgr, dga,
                                 jnp.zeros((lp, IN_PAD - C_GA - 128), BF16)], axis=1)
        g["w_in"][l] = _mm_tn(s["hn"], dproj, tn=1280, name="proj_dw")
        dhn_next = _mm(dproj, win[l], nt=True, name="proj_dx")
        dh_out = dh
    dh0, _, g["pre_mix"][0], _ = _resid_norm_bwd(dh_out, dhn_next, h0, None, row(pre_mix_norm, 0), None,
                                                 name="norm_in_bwd")
    return loss_local, dh0, g


def kernel(x, meta_tokens, pre_mix_norm, w_in, gla_gate_w2, gla_gate_b, ret_norm_w, gla_norm_w, w_out, post_mix_norm, pre_ffn_norm, ffn_up, ffn_conv_w, ffn_conv_b, ffn_down, post_ffn_norm, loss_target, m_meta_tokens, m_pre_mix_norm, m_w_in, m_gla_gate_w2, m_gla_gate_b, m_ret_norm_w, m_gla_norm_w, m_w_out, m_post_mix_norm, m_pre_ffn_norm, m_ffn_up, m_ffn_conv_w, m_ffn_conv_b, m_ffn_down, m_post_ffn_norm, v_meta_tokens, v_pre_mix_norm, v_w_in, v_gla_gate_w2, v_gla_gate_b, v_ret_norm_w, v_gla_norm_w, v_w_out, v_post_mix_norm, v_pre_ffn_norm, v_ffn_up, v_ffn_conv_w, v_ffn_conv_b, v_ffn_down, v_post_ffn_norm):
    xi, yi, ci = _place()
    chip = 2 * xi + yi
    seq = x.shape[1]
    d = D_MODEL
    meta_full, win, wout, up_a, up_g, down, w2p, cw_full = _gather_params(
        w_in, w_out, ffn_up, ffn_down, meta_tokens, gla_gate_w2, ffn_conv_w)
    loss_local, dh0, g = _local_step(x[0], loss_target[0], meta_full, win, wout, up_a, up_g, down, w2p, cw_full,
                                     pre_mix_norm, gla_gate_b, ret_norm_w, gla_norm_w, post_mix_norm, pre_ffn_norm,
                                     ffn_conv_b, post_ffn_norm)
    grad_x = dh0[FRONT:FRONT + seq][None]

    big = [
        jnp.stack([_cols_to_chips(g["w_in"][l][:, :IN_WIDTH], 4) for l in range(DEPTH)]),
        jnp.stack([g["w_out"][l].reshape(4, d // 4, d) for l in range(DEPTH)]),
        jnp.stack([jnp.concatenate([_cols_to_chips(g["up_a"][l], 2), _cols_to_chips(g["up_g"][l], 2)], axis=0)
                   for l in range(DEPTH)]),
        jnp.stack([g["down"][l].reshape(4, D_FF // 4, d) for l in range(DEPTH)]),
    ]
    from_sib = _pair_exchange(big, name="grads_pair_exchange")
    mine = [lax.dynamic_index_in_dim(a, ci, 0, keepdims=False) for a in big]
    names = ("w_in", "w_out", "ffn_up", "ffn_down")
    chip_sums = [_add2(a, b, out_dtype=BF16, name=f"pair_sum_{nm}") for a, b, nm in zip(mine, from_sib, names)]
    slots = _chip_exchange(chip_sums, name="grads_chip_exchange")
    own = [lax.dynamic_index_in_dim(a, chip, 0, keepdims=False) for a in chip_sums]
    layer_grads = [_sum_slots(o, q, name=f"chip_sum_{nm}") for o, q, nm in zip(own, slots, names)]
    from_sib = _pair_swap(layer_grads, name="grads_pair_swap")
    g_w_in, g_w_out, g_ffn_up, g_ffn_down = [
        jnp.where(ci == 0, jnp.stack([a, b]), jnp.stack([b, a])) for a, b in zip(layer_grads, from_sib)]

    small_full = [dh0[PADF:FRONT], jnp.stack(g["pre_mix"])[:, 0], jnp.stack(g["w2"]), jnp.stack(g["gb"]),
                  jnp.stack(g["ret_n"])[:, 0], jnp.stack(g["gla_n"])[:, 0], jnp.stack(g["post_mix"])[:, 0],
                  jnp.stack(g["pre_ffn"])[:, 0], jnp.stack(g["cw"]), jnp.stack(g["cb"]),
                  jnp.stack(g["post_ffn"])[:, 0]]
    small_sum = _unslab(_allreduce_small(_slab(small_full, 8), name="small_allreduce"), [a.shape for a in small_full])
    (g_meta, g_pre_mix, g_w2, g_gb, g_ret_n, g_gla_n, g_post_mix, g_pre_ffn, g_cw, g_cb, g_post_ffn) = small_sum
    g_meta = lax.dynamic_slice_in_dim(g_meta, chip * 256, 256, axis=1)
    g_w2 = lax.dynamic_slice_in_dim(g_w2, chip * 64, 64, axis=2)
    g_cw = lax.dynamic_slice_in_dim(g_cw, chip * 1408, 1408, axis=2)

    grads = [g_meta, g_pre_mix, g_w_in, g_w2, g_gb, g_ret_n, g_gla_n, g_w_out, g_post_mix, g_pre_ffn, g_ffn_up,
             g_cw, g_cb, g_ffn_down, g_post_ffn]
    ws = [meta_tokens, pre_mix_norm, w_in, gla_gate_w2, gla_gate_b, ret_norm_w, gla_norm_w, w_out, post_mix_norm,
          pre_ffn_norm, ffn_up, ffn_conv_w, ffn_conv_b, ffn_down, post_ffn_norm]
    ms = [m_meta_tokens, m_pre_mix_norm, m_w_in, m_gla_gate_w2, m_gla_gate_b, m_ret_norm_w, m_gla_norm_w, m_w_out,
          m_post_mix_norm, m_pre_ffn_norm, m_ffn_up, m_ffn_conv_w, m_ffn_conv_b, m_ffn_down, m_post_ffn_norm]
    vs = [v_meta_tokens, v_pre_mix_norm, v_w_in, v_gla_gate_w2, v_gla_gate_b, v_ret_norm_w, v_gla_norm_w, v_w_out,
          v_post_mix_norm, v_pre_ffn_norm, v_ffn_up, v_ffn_conv_w, v_ffn_conv_b, v_ffn_down, v_post_ffn_norm]
    big_idx = (2, 7, 10, 13)
    deltas, new_m, new_v = [None] * 15, [None] * 15, [None] * 15
    for i, nm in zip(big_idx, names):
        deltas[i], new_m[i], new_v[i] = _adamw(ws[i], grads[i], ms[i], vs[i], name=f"adamw_{nm}")
    small_idx = [i for i in range(15) if i not in big_idx]
    shapes = [ws[i].shape for i in small_idx]
    sd, sm, sv = _adamw(_slab([ws[i] for i in small_idx], 8), _slab([grads[i] for i in small_idx], 8),
                        _slab([ms[i] for i in small_idx], 8), _slab([vs[i] for i in small_idx], 8), name="adamw_small")
    for i, a, b, c_ in zip(small_idx, _unslab(sd, shapes), _unslab(sm, shapes), _unslab(sv, shapes)):
        deltas[i], new_m[i], new_v[i] = a, b, c_

    loss = lax.psum(loss_local, ("x", "y", "c"))
    return (loss, grad_x, *grads, *deltas, *new_m, *new_v)
```

```python
import functools
import math

import numpy as np
import jax
import jax.numpy as jnp
from jax import lax
from jax.experimental import pallas as pl
from jax.experimental.pallas import tpu as pltpu

F32 = jnp.float32
BF16 = jnp.bfloat16

D_MODEL = 1024
DEPTH = 2
N_META = 16
EPS = 1e-6
RET_HEADS = 4
RET_DK = 128
GLA_HEADS = 4
GLA_DK = 64
GLA_DV = 128
GLA_QK = GLA_HEADS * GLA_DK
GLA_V = GLA_HEADS * GLA_DV
GLA_RANK = 16
GLA_TAU = 16.0
D_FF = 2816
ROPE_BASE = 10000.0
IN_WIDTH = 3600
IN_PAD = 3840
C_RQ, C_RK, C_RV, C_RG, C_GQ, C_GK, C_GV, C_GR, C_GA = 0, 512, 1024, 1536, 2048, 2304, 2560, 3072, 3584

FRONT = 64
BACK = 64
PADF = FRONT - N_META
RET_CHUNK = 128
GLA_CHUNK = 64
GLA_SUB = 16
BLK = 640

ADAM_LR, ADAM_B1, ADAM_B2, ADAM_EPS, ADAM_WD, ADAM_STEP = 0.001, 0.9, 0.999, 1e-08, 0.01, 10

VMEM_LIMIT = 56 * 2 ** 20
MESH = pl.DeviceIdType.MESH


def _cp(*sem):
    return pltpu.CompilerParams(dimension_semantics=sem, vmem_limit_bytes=VMEM_LIMIT)


def _tile(n, cands):
    for t in cands:
        if n % t == 0:
            return t
    raise ValueError(f"no tile for {n} in {cands}")


def _row_tile(n):
    return _tile(n, (640, 512, 320, 256, 128, 64))


def _mm(a, b, *, nt=False, add=None, out_dtype=F32, tn=None, name, carry=None):
    m, k = a.shape
    n = b.shape[0] if nt else b.shape[1]
    tm = _tile(m, (640, 320, 256, 128, 64))
    tn = n if tn is None else tn
    dn = (((1,), (1,)), ((), ())) if nt else (((1,), (0,)), ((), ()))
    nj, ni = n // tn, m // tm
    n_in = 2 + (add is not None)
    c_arrs, c_mode = carry if carry is not None else ((), None)
    nc = len(c_arrs)

    def body(*refs):
        a_ref, b_ref = refs[:2]
        c_ref = refs[2] if add is not None else None
        o_ref = refs[n_in + nc]
        if nc:
            c_ins, c_outs = refs[n_in:n_in + nc], refs[n_in + nc + 1:n_in + 2 * nc + 1]
            ssem, rsem = refs[n_in + 2 * nc + 1:]
            j, i = pl.program_id(0), pl.program_id(1)

            @pl.when((j == 0) & (i == 0))
            def _():
                for cp in _chip_copies(c_ins, c_outs, ssem, rsem, c_mode):
                    cp.start()
        r = lax.dot_general(a_ref[...].astype(BF16), b_ref[...].astype(BF16), dn, preferred_element_type=F32)
        if add is not None:
            r = r + c_ref[...]
        o_ref[...] = r.astype(o_ref.dtype)
        if nc:
            @pl.when((j == nj - 1) & (i == ni - 1))
            def _():
                for cp in _chip_copies(c_ins, c_outs, ssem, rsem, c_mode):
                    cp.wait()

    b_spec = pl.BlockSpec((tn, k), lambda j, i: (j, 0)) if nt else pl.BlockSpec((k, tn), lambda j, i: (0, j))
    in_specs = [pl.BlockSpec((tm, k), lambda j, i: (i, 0)), b_spec]
    args = [a, b]
    if add is not None:
        in_specs.append(pl.BlockSpec((tm, tn), lambda j, i: (i, j)))
        args.append(add)
    out_shape = jax.ShapeDtypeStruct((m, n), out_dtype)
    out_spec = pl.BlockSpec((tm, tn), lambda j, i: (i, j))
    if not nc:
        return pl.pallas_call(
            body, out_shape=out_shape, grid=(nj, ni), in_specs=in_specs, out_specs=out_spec,
            compiler_params=_cp("parallel", "parallel"), name=name)(*args)
    sem = pltpu.SemaphoreType.DMA
    outs = pl.pallas_call(
        body, out_shape=(out_shape,) + tuple(_landing_shape(x) for x in c_arrs), grid=(nj, ni),
        in_specs=in_specs + [ANY] * nc, out_specs=(out_spec,) + (ANY,) * nc,
        scratch_shapes=[sem((nc, 3)), sem((nc, 3))],
        compiler_params=_cp("arbitrary", "arbitrary"), name=name)(*args, *c_arrs)
    return outs[0], list(outs[1:])


def _mm_tn(a, b, *, tn=None, name):
    m, k = a.shape
    n = b.shape[1]
    tm = _tile(m, (1664, 640, 320, 256, 128, 64))
    tn = n if tn is None else tn

    def body(a_ref, b_ref, o_ref):
        @pl.when(pl.program_id(1) == 0)
        def _():
            o_ref[...] = jnp.zeros_like(o_ref)
        o_ref[...] += lax.dot_general(a_ref[...].astype(BF16), b_ref[...].astype(BF16),
                                      (((0,), (0,)), ((), ())), preferred_element_type=F32)

    return pl.pallas_call(
        body, out_shape=jax.ShapeDtypeStruct((k, n), F32), grid=(n // tn, m // tm),
        in_specs=[pl.BlockSpec((tm, k), lambda j, i: (i, 0)), pl.BlockSpec((tm, tn), lambda j, i: (i, j))],
        out_specs=pl.BlockSpec((k, tn), lambda j, i: (0, j)),
        compiler_params=_cp("parallel", "arbitrary"), name=name)(a, b)


def _rms(x, w):
    r = lax.rsqrt(jnp.mean(x * x, axis=-1, keepdims=True) + EPS)
    return x * r * w


def _rms_bwd(x, w, dy):
    r = lax.rsqrt(jnp.mean(x * x, axis=-1, keepdims=True) + EPS)
    xh = x * r
    dxh = dy * w
    dx = r * (dxh - xh * jnp.mean(dxh * xh, axis=-1, keepdims=True))
    return dx, jnp.sum(dy * xh, axis=0, keepdims=True)


def _resid_norm(h, t, w_post, w_next, *, name):
    lp, d = h.shape
    tm = _row_tile(lp)
    has_t = t is not None

    def body(*refs):
        if has_t:
            h_ref, t_ref, wp_ref, wn_ref, ho_ref, hn_ref = refs
            hv = h_ref[...] + _rms(t_ref[...], wp_ref[...])
            ho_ref[...] = hv
        else:
            h_ref, wn_ref, hn_ref = refs
            hv = h_ref[...]
        hn_ref[...] = _rms(hv, wn_ref[...]).astype(BF16)

    row = pl.BlockSpec((tm, d), lambda i: (i, 0))
    vec = pl.BlockSpec((1, d), lambda i: (0, 0))
    if has_t:
        return pl.pallas_call(
            body, out_shape=(jax.ShapeDtypeStruct((lp, d), F32), jax.ShapeDtypeStruct((lp, d), BF16)),
            grid=(lp // tm,), in_specs=[row, row, vec, vec], out_specs=(row, row),
            compiler_params=_cp("parallel"), name=name)(h, t, w_post, w_next)
    return h, pl.pallas_call(
        body, out_shape=jax.ShapeDtypeStruct((lp, d), BF16), grid=(lp // tm,), in_specs=[row, vec],
        out_specs=row, compiler_params=_cp("parallel"), name=name)(h, w_next)


def _resid_norm_bwd(dh_out, dhn, h_new, t, w_next, w_post, *, name):
    lp, d = h_new.shape if h_new is not None else t.shape
    tm = _row_tile(lp)
    has_n = dhn is not None
    has_t = t is not None

    def body(*refs):
        refs = list(refs)
        dho_ref = refs.pop(0)
        if has_n:
            dhn_ref, hn_ref, wn_ref = refs.pop(0), refs.pop(0), refs.pop(0)
        if has_t:
            t_ref, wp_ref = refs.pop(0), refs.pop(0)
        dh_ref = refs.pop(0) if has_n else None
        dt_ref = refs.pop(0) if has_t else None
        dwn_ref = refs.pop(0) if has_n else None
        dwp_ref = refs.pop(0) if has_t else None
        first = pl.program_id(0) == 0
        dh = dho_ref[...]
        if has_n:
            dx, dwn = _rms_bwd(hn_ref[...], wn_ref[...], dhn_ref[...])
            dh = dh + dx
            dh_ref[...] = dh

            @pl.when(first)
            def _():
                dwn_ref[...] = jnp.zeros_like(dwn_ref)
            dwn_ref[...] += dwn
        if has_t:
            dt, dwp = _rms_bwd(t_ref[...], wp_ref[...], dh)
            dt_ref[...] = dt.astype(BF16)

            @pl.when(first)
            def _():
                dwp_ref[...] = jnp.zeros_like(dwp_ref)
            dwp_ref[...] += dwp

    row = pl.BlockSpec((tm, d), lambda i: (i, 0))
    vec = pl.BlockSpec((1, d), lambda i: (0, 0))
    args, in_specs, out_shape, out_specs = [dh_out], [row], [], []
    if has_n:
        args += [dhn, h_new, w_next]
        in_specs += [row, row, vec]
    if has_t:
        args += [t, w_post]
        in_specs += [row, vec]
    if has_n:
        out_shape.append(jax.ShapeDtypeStruct((lp, d), F32)); out_specs.append(row)
    if has_t:
        out_shape.append(jax.ShapeDtypeStruct((lp, d), BF16)); out_specs.append(row)
    if has_n:
        out_shape.append(jax.ShapeDtypeStruct((1, d), F32)); out_specs.append(vec)
    if has_t:
        out_shape.append(jax.ShapeDtypeStruct((1, d), F32)); out_specs.append(vec)
    outs = list(pl.pallas_call(body, out_shape=tuple(out_shape), grid=(lp // tm,), in_specs=in_specs,
                               out_specs=tuple(out_specs), compiler_params=_cp("arbitrary"), name=name)(*args))
    dh = outs.pop(0) if has_n else dh_out
    dt = outs.pop(0) if has_t else None
    dwn = outs.pop(0) if has_n else None
    dwp = outs.pop(0) if has_t else None
    return dh, dt, dwn, dwp


def _loss_head(h, f, w_post, target, *, name):
    lp, d = h.shape
    tm = _row_tile(lp)

    def body(h_ref, f_ref, w_ref, t_ref, loss_ref, dy_ref):
        i = pl.program_id(0)
        y = h_ref[...] + _rms(f_ref[...], w_ref[...])
        rows = i * tm + lax.broadcasted_iota(jnp.int32, (tm, 1), 0)
        tok = (rows >= FRONT) & (rows < lp - BACK)
        err = jnp.where(tok, y - t_ref[...], 0.0)
        dy_ref[...] = err * (1.0 / d)

        @pl.when(i == 0)
        def _():
            loss_ref[...] = jnp.zeros_like(loss_ref)
        part = jnp.sum(jnp.sum(err * err, axis=1, keepdims=True), axis=0, keepdims=True) * (0.5 / d)
        loss_ref[...] += jnp.broadcast_to(part, loss_ref.shape)

    row = pl.BlockSpec((tm, d), lambda i: (i, 0))
    loss, dy = pl.pallas_call(
        body, out_shape=(jax.ShapeDtypeStruct((8, 128), F32), jax.ShapeDtypeStruct((lp, d), F32)),
        grid=(lp // tm,), in_specs=[row, row, pl.BlockSpec((1, d), lambda i: (0, 0)), row],
        out_specs=(pl.BlockSpec((8, 128), lambda i: (0, 0)), row),
        compiler_params=_cp("arbitrary"), name=name)(h, f, w_post, target)
    return loss[0, 0], dy


_GELU_C = math.sqrt(2.0 / math.pi)


def _gelu_and_grad(a):
    a2 = a * a
    t = jnp.tanh(a * (_GELU_C + (_GELU_C * 0.044715) * a2))
    ha = 0.5 * a
    h1 = 0.5 + 0.5 * t
    return a * h1, h1 + ha * (1.0 - t * t) * (_GELU_C + (3.0 * _GELU_C * 0.044715) * a2)


def _gelu(a):
    t = jnp.tanh(a * (_GELU_C + (_GELU_C * 0.044715) * (a * a)))
    return a * (0.5 + 0.5 * t)


def _conv3(parts, n, w, b):
    xx = jnp.concatenate(parts, axis=0)
    return b + xx[8:8 + n] * w[2:3] + pltpu.roll(xx, 1, 0)[8:8 + n] * w[1:2] + pltpu.roll(xx, 2, 0)[8:8 + n] * w[0:1]


def _conv_act(ua, ug, wa, wg, ba, bg, *, name):
    lp, n = ua.shape
    tm = _row_tile(lp)
    tc = _tile(n, (256, 128))
    nb8 = tm // 8

    def body(ua_ref, uap_ref, ug_ref, ugp_ref, wa_ref, wg_ref, ba_ref, bg_ref, o_ref, ca_ref, cg_ref):
        i = pl.program_id(0)
        ca = _conv3([uap_ref[...], ua_ref[...]], tm, wa_ref[...], ba_ref[...])
        cg = _conv3([ugp_ref[...], ug_ref[...]], tm, wg_ref[...], bg_ref[...])
        ca_ref[...] = ca
        cg_ref[...] = cg
        rows = i * tm + lax.broadcasted_iota(jnp.int32, (tm, 1), 0)
        ok = (rows >= PADF) & (rows < lp - BACK)
        o_ref[...] = jnp.where(ok, _gelu(ca) * cg, 0.0).astype(BF16)

    cur = pl.BlockSpec((tm, tc), lambda i, j: (i, j))
    prev = pl.BlockSpec((8, tc), lambda i, j: (jnp.maximum(i * nb8 - 1, 0), j))
    w3 = pl.BlockSpec((3, tc), lambda i, j: (0, j))
    b1 = pl.BlockSpec((1, tc), lambda i, j: (0, j))
    keep = jax.ShapeDtypeStruct((lp, n), F32)
    return pl.pallas_call(
        body, out_shape=(jax.ShapeDtypeStruct((lp, n), BF16), keep, keep), grid=(lp // tm, n // tc),
        in_specs=[cur, prev, cur, prev, w3, w3, b1, b1], out_specs=(cur, cur, cur),
        compiler_params=_cp("parallel", "parallel"), name=name)(ua, ua, ug, ug, wa, wg, ba, bg)


def _conv_act_bwd(ua, ug, ca, cg, dact, wa, wg, *, name):
    lp, n = ua.shape
    tm = _row_tile(lp)
    tc = _tile(n, (256, 128))
    nb8 = tm // 8
    last8 = lp // 8 - 1
    ext = tm + 8

    def body(ua_ref, ug_ref, ca_ref, can_ref, cg_ref, cgn_ref, da_ref, dan_ref, wa_ref, wg_ref,
             dua_ref, dug_ref, dwa_ref, dwg_ref, dba_ref, dbg_ref):
        i = pl.program_id(1)
        ext_rows = lambda c_ref, n_ref: jnp.concatenate([c_ref[...], n_ref[...]], axis=0)
        rows = i * tm + lax.broadcasted_iota(jnp.int32, (ext, 1), 0)
        ok = (rows >= PADF) & (rows < lp - BACK)
        dact_e = jnp.where(ok, ext_rows(da_ref, dan_ref), 0.0)
        gel, gel_d = _gelu_and_grad(ext_rows(ca_ref, can_ref))
        dca = dact_e * ext_rows(cg_ref, cgn_ref) * gel_d
        dcg = dact_e * gel

        @pl.when(i == 0)
        def _():
            dwa_ref[...] = jnp.zeros_like(dwa_ref)
            dwg_ref[...] = jnp.zeros_like(dwg_ref)
            dba_ref[...] = jnp.zeros_like(dba_ref)
            dbg_ref[...] = jnp.zeros_like(dbg_ref)

        def back(dc, w, x, du_ref, dw_ref, db_ref):
            d0, d1, d2 = dc[:tm], pltpu.roll(dc, ext - 1, 0)[:tm], pltpu.roll(dc, ext - 2, 0)[:tm]
            du_ref[...] = (d0 * w[2:3] + d1 * w[1:2] + d2 * w[0:1]).astype(BF16)
            s = lambda v: jnp.sum(v, axis=0, keepdims=True)
            dw_ref[0:1, :] += s(d2 * x)
            dw_ref[1:2, :] += s(d1 * x)
            dw_ref[2:3, :] += s(d0 * x)
            db_ref[...] += s(d0)

        back(dca, wa_ref[...], ua_ref[...], dua_ref, dwa_ref, dba_ref)
        back(dcg, wg_ref[...], ug_ref[...], dug_ref, dwg_ref, dbg_ref)

    cur = pl.BlockSpec((tm, tc), lambda j, i: (i, j))
    nxt = pl.BlockSpec((8, tc), lambda j, i: (jnp.minimum((i + 1) * nb8, last8), j))
    w3 = pl.BlockSpec((3, tc), lambda j, i: (0, j))
    b1 = pl.BlockSpec((1, tc), lambda j, i: (0, j))
    return pl.pallas_call(
        body,
        out_shape=(jax.ShapeDtypeStruct((lp, n), BF16), jax.ShapeDtypeStruct((lp, n), BF16),
                   jax.ShapeDtypeStruct((3, n), F32), jax.ShapeDtypeStruct((3, n), F32),
                   jax.ShapeDtypeStruct((1, n), F32), jax.ShapeDtypeStruct((1, n), F32)),
        grid=(n // tc, lp // tm),
        in_specs=[cur, cur, cur, nxt, cur, nxt, cur, nxt, w3, w3],
        out_specs=(cur, cur, w3, w3, b1, b1),
        compiler_params=_cp("parallel", "arbitrary"), name=name)(ua, ug, ca, ca, cg, cg, dact, dact, wa, wg)


def _sigmoid(x):
    return 1.0 / (1.0 + jnp.exp(-x))


def _merge(o_ret, o_gla, proj, w_ret, w_gla, *, name):
    lp = o_ret.shape[0]
    tm = _row_tile(lp)

    def body(or_ref, og_ref, rg_ref, gr_ref, wr_ref, wg_ref, m_ref):
        oret, ogla = or_ref[...], og_ref[...]
        yr, yg = [], []
        for h in range(4):
            hs = slice(128 * h, 128 * h + 128)
            o = oret[:, hs]
            xc = o - jnp.mean(o, axis=-1, keepdims=True)
            yr.append(xc * lax.rsqrt(jnp.mean(xc * xc, axis=-1, keepdims=True) + EPS))
            o = ogla[:, hs]
            yg.append(o * lax.rsqrt(jnp.mean(o * o, axis=-1, keepdims=True) + EPS))
        rg, gr = rg_ref[...], gr_ref[...]
        m_ref[:, 0:512] = (jnp.concatenate(yr, axis=1) * wr_ref[...] * (rg * _sigmoid(rg))).astype(BF16)
        m_ref[:, 512:1024] = (jnp.concatenate(yg, axis=1) * wg_ref[...] * (gr * _sigmoid(gr))).astype(BF16)

    row = pl.BlockSpec((tm, 512), lambda i: (i, 0))
    vec = pl.BlockSpec((1, 512), lambda i: (0, 0))
    return pl.pallas_call(
        body, out_shape=jax.ShapeDtypeStruct((lp, 1024), BF16), grid=(lp // tm,),
        in_specs=[row, row, pl.BlockSpec((tm, 512), lambda i: (i, C_RG // 512)),
                  pl.BlockSpec((tm, 512), lambda i: (i, C_GR // 512)), vec, vec],
        out_specs=pl.BlockSpec((tm, 1024), lambda i: (i, 0)),
        compiler_params=_cp("parallel"), name=name)(o_ret, o_gla, proj, proj, w_ret, w_gla)


def _merge_bwd(dm, o_ret, o_gla, proj, w_ret, w_gla, *, name):
    lp = o_ret.shape[0]
    tm = _row_tile(lp)

    def body(dm_ref, or_ref, og_ref, rg_ref, gr_ref, wr_ref, wg_ref, dor_ref, dog_ref, drg_ref, dgr_ref, dwr_ref, dwg_ref):
        @pl.when(pl.program_id(0) == 0)
        def _():
            dwr_ref[...] = jnp.zeros_like(dwr_ref)
            dwg_ref[...] = jnp.zeros_like(dwg_ref)

        def group(d, o_all, gate, w, center):
            sg = _sigmoid(gate)
            s = gate * sg
            ds = sg * (1.0 + gate * (1.0 - sg))
            xh, rr = [], []
            for h in range(4):
                o = o_all[:, 128 * h:128 * h + 128]
                if center:
                    o = o - jnp.mean(o, axis=-1, keepdims=True)
                r = lax.rsqrt(jnp.mean(o * o, axis=-1, keepdims=True) + EPS)
                xh.append(o * r)
                rr.append(r)
            xh_all = jnp.concatenate(xh, axis=1)
            dgate = d * xh_all * w * ds
            dw = jnp.sum(d * xh_all * s, axis=0, keepdims=True)
            dxh_all = d * w * s
            do = []
            for h in range(4):
                dxh = dxh_all[:, 128 * h:128 * h + 128]
                t = dxh - xh[h] * jnp.mean(dxh * xh[h], axis=-1, keepdims=True)
                if center:
                    t = t - jnp.mean(dxh, axis=-1, keepdims=True)
                do.append(rr[h] * t)
            return jnp.concatenate(do, axis=1), dgate, dw

        dmv = dm_ref[...]
        do, dg, dw = group(dmv[:, 0:512], or_ref[...], rg_ref[...], wr_ref[...], True)
        dor_ref[...] = do
        drg_ref[...] = dg.astype(BF16)
        dwr_ref[...] += dw
        do, dg, dw = group(dmv[:, 512:1024], og_ref[...], gr_ref[...], wg_ref[...], False)
        dog_ref[...] = do
        dgr_ref[...] = dg.astype(BF16)
        dwg_ref[...] += dw

    row = pl.BlockSpec((tm, 512), lambda i: (i, 0))
    vec = pl.BlockSpec((1, 512), lambda i: (0, 0))
    return pl.pallas_call(
        body,
        out_shape=(jax.ShapeDtypeStruct((lp, 512), F32), jax.ShapeDtypeStruct((lp, 512), F32),
                   jax.ShapeDtypeStruct((lp, 512), BF16), jax.ShapeDtypeStruct((lp, 512), BF16),
                   jax.ShapeDtypeStruct((1, 512), F32), jax.ShapeDtypeStruct((1, 512), F32)),
        grid=(lp // tm,),
        in_specs=[pl.BlockSpec((tm, 1024), lambda i: (i, 0)), row, row,
                  pl.BlockSpec((tm, 512), lambda i: (i, C_RG // 512)),
                  pl.BlockSpec((tm, 512), lambda i: (i, C_GR // 512)), vec, vec],
        out_specs=(row, row, row, row, vec, vec),
        compiler_params=_cp("arbitrary"), name=name)(dm, o_ret, o_gla, proj, proj, w_ret, w_gla)


def _dot(a, b):
    return lax.dot_general(a, b, (((1,), (0,)), ((), ())), preferred_element_type=F32)


def _dot_nt(a, b):
    return lax.dot_general(a, b, (((1,), (1,)), ((), ())), preferred_element_type=F32)


def _dot_tn(a, b):
    return lax.dot_general(a, b, (((0,), (0,)), ((), ())), preferred_element_type=F32)


def _ret_tables(lp):
    cr = RET_CHUNK
    pos = jnp.arange(lp, dtype=F32) - float(PADF)
    half = RET_DK // 2
    inv = ROPE_BASE ** (-jnp.arange(half, dtype=F32) / half)
    ang = pos[:, None] * inv[None, :]
    c, s = jnp.cos(ang), jnp.sin(ang)
    rope_c = jnp.concatenate([c, c], axis=1)
    rope_s = jnp.concatenate([-s, s], axis=1)
    log_g = np.log(1.0 - 2.0 ** (-5.0 - np.arange(RET_HEADS, dtype=np.float64)))
    idx = np.arange(cr, dtype=np.float64)
    diff = idx[:, None] - idx[None, :]
    dmat = np.where(diff >= 0, np.exp(log_g[:, None, None] * np.maximum(diff, 0.0)), 0.0)
    zeta = np.exp(log_g[:, None] * (cr - 1.0 - idx)[None, :])
    xi = np.exp(log_g[:, None] * (idx + 1.0)[None, :])
    gc = np.exp(log_g * cr)
    f = lambda a: jnp.asarray(a.astype(np.float32))
    return (rope_c, rope_s, f(dmat), f(np.broadcast_to(zeta[:, :, None], (RET_HEADS, cr, 128))),
            f(np.broadcast_to(xi[:, :, None], (RET_HEADS, cr, 128))),
            f(np.broadcast_to(gc[:, None, None], (RET_HEADS, 8, 128))))


def _rope(t, c, s):
    return t * c + pltpu.roll(t, 64, 1) * s


def _rope_t(d, c, s):
    return d * c + pltpu.roll(d * s, 64, 1)


def _ret_specs(nblk, rev):
    ix = (lambda i: nblk - 1 - i) if rev else (lambda i: i)
    cr = RET_CHUNK
    col = lambda base: pl.BlockSpec((BLK, 512), lambda i: (ix(i), base // 512))
    tab = pl.BlockSpec((BLK, 128), lambda i: (ix(i), 0))
    sq = pl.BlockSpec((RET_HEADS, cr, cr), lambda i: (0, 0, 0))
    hv = pl.BlockSpec((RET_HEADS, cr, 128), lambda i: (0, 0, 0))
    g8 = pl.BlockSpec((RET_HEADS, 8, 128), lambda i: (0, 0, 0))
    st = pl.BlockSpec((RET_HEADS, BLK // cr, 128, 128), lambda i: (0, ix(i), 0, 0))
    out = pl.BlockSpec((BLK, 512), lambda i: (ix(i), 0))
    return col, tab, sq, hv, g8, st, out


def _retention(proj, tables, *, name):
    lp = proj.shape[0]
    nblk, cr = lp // BLK, RET_CHUNK
    scale = RET_DK ** -0.5

    def body(q_ref, k_ref, v_ref, c_ref, s_ref, d_ref, z_ref, x_ref, g_ref, o_ref, st_ref, state):
        @pl.when(pl.program_id(0) == 0)
        def _():
            state[...] = jnp.zeros_like(state)

        def chunk(ci, carry):
            sl = pl.ds(pl.multiple_of(ci * cr, cr), cr)
            c, s = c_ref[sl, :], s_ref[sl, :]
            for h in range(RET_HEADS):
                hs = slice(128 * h, 128 * h + 128)
                q = _rope(q_ref[sl, hs], c, s)
                k = _rope(k_ref[sl, hs], c, s) * scale
                qb, kb, vb = q.astype(BF16), k.astype(BF16), v_ref[sl, hs].astype(BF16)
                st = state[h]
                st_ref[h, ci] = st
                sc = _dot_nt(qb, kb) * d_ref[h]
                o_ref[sl, hs] = _dot(sc.astype(BF16), vb) + _dot(qb, st.astype(BF16)) * x_ref[h]
                state[h] = st * g_ref[h][0:1, :] + _dot_tn((k * z_ref[h]).astype(BF16), vb)
            return carry

        lax.fori_loop(0, BLK // cr, chunk, 0)

    col, tab, sq, hv, g8, st, out = _ret_specs(nblk, False)
    return pl.pallas_call(
        body,
        out_shape=(jax.ShapeDtypeStruct((lp, 512), F32), jax.ShapeDtypeStruct((4, lp // cr, 128, 128), F32)),
        grid=(nblk,), in_specs=[col(C_RQ), col(C_RK), col(C_RV), tab, tab, sq, hv, hv, g8],
        out_specs=(out, st), scratch_shapes=[pltpu.VMEM((RET_HEADS, 128, 128), F32)],
        compiler_params=_cp("arbitrary"), name=name)(proj, proj, proj, *tables)


def _retention_bwd(proj, do, states, tables, *, name):
    lp = proj.shape[0]
    nblk, cr = lp // BLK, RET_CHUNK
    nch = BLK // cr
    scale = RET_DK ** -0.5

    def body(q_ref, k_ref, v_ref, do_ref, st_ref, c_ref, s_ref, d_ref, z_ref, x_ref, g_ref, dq_ref, dk_ref, dv_ref, dstate):
        @pl.when(pl.program_id(0) == 0)
        def _():
            dstate[...] = jnp.zeros_like(dstate)

        def chunk(cc, carry):
            ci = nch - 1 - cc
            sl = pl.ds(pl.multiple_of(ci * cr, cr), cr)
            c, s = c_ref[sl, :], s_ref[sl, :]
            for h in range(RET_HEADS):
                hs = slice(128 * h, 128 * h + 128)
                dmat, zeta, xi = d_ref[h], z_ref[h], x_ref[h]
                q = _rope(q_ref[sl, hs], c, s)
                k = _rope(k_ref[sl, hs], c, s) * scale
                qb, kb, vb = q.astype(BF16), k.astype(BF16), v_ref[sl, hs].astype(BF16)
                kzb = (k * zeta).astype(BF16)
                dov = do_ref[sl, hs]
                dob, doxb = dov.astype(BF16), (dov * xi).astype(BF16)
                stb = st_ref[h, ci].astype(BF16)
                dsn = dstate[h]
                dsnb = dsn.astype(BF16)
                scb = (_dot_nt(qb, kb) * dmat).astype(BF16)
                dscb = (_dot_nt(dob, vb) * dmat).astype(BF16)
                dq = _dot(dscb, kb) + _dot_nt(doxb, stb)
                dk = _dot_tn(dscb, qb) + _dot_nt(vb, dsnb) * zeta
                dv = _dot_tn(scb, dob) + _dot(kzb, dsnb)
                dstate[h] = dsn * g_ref[h][0:1, :] + _dot_tn(qb, doxb)
                dq_ref[sl, hs] = _rope_t(dq, c, s).astype(BF16)
                dk_ref[sl, hs] = _rope_t(dk * scale, c, s).astype(BF16)
                dv_ref[sl, hs] = dv.astype(BF16)
            return carry

        lax.fori_loop(0, nch, chunk, 0)

    col, tab, sq, hv, g8, st, out = _ret_specs(nblk, True)
    o3 = jax.ShapeDtypeStruct((lp, 512), BF16)
    return pl.pallas_call(
        body, out_shape=(o3, o3, o3), grid=(nblk,),
        in_specs=[col(C_RQ), col(C_RK), col(C_RV), out, st, tab, tab, sq, hv, hv, g8],
        out_specs=(out, out, out), scratch_shapes=[pltpu.VMEM((RET_HEADS, 128, 128), F32)],
        compiler_params=_cp("arbitrary"), name=name)(proj, proj, proj, do, states, *tables)


def _gla_tables():
    c = GLA_CHUNK
    tri = np.tril(np.ones((c, c), np.float32))
    ones_qv = np.kron(np.eye(GLA_HEADS, dtype=np.float32), np.ones((GLA_DK, GLA_DV), np.float32))
    return (jnp.asarray(tri, BF16), jnp.asarray(tri.T.copy(), BF16), jnp.asarray(ones_qv, BF16),
            jnp.asarray(ones_qv.T.copy(), BF16))


def _split3(x):
    hi = x.astype(BF16)
    r1 = x - hi.astype(F32)
    mid = r1.astype(BF16)
    lo = (r1 - mid.astype(F32)).astype(BF16)
    return hi, mid, lo


def _tri_sum(tri, x):
    hi, mid, lo = _split3(x)
    return _dot(tri, hi) + _dot(tri, mid) + _dot(tri, lo)


def _head_masks(width, per):
    lane = lax.broadcasted_iota(jnp.int32, (1, width), 1)
    return [((lane >= per * h) & (lane < per * (h + 1))).astype(F32) for h in range(GLA_HEADS)]


def _stack_heads(x, masks):
    return jnp.concatenate([x * m for m in masks], axis=0)


def _gla_gate(ga, w2, b, ok, tri):
    z = _dot(ga.astype(BF16), w2) + b
    la = (jnp.minimum(z, 0.0) - jnp.log(1.0 + jnp.exp(-jnp.abs(z)))) * (1.0 / GLA_TAU)
    la = jnp.where(ok, la, 0.0)
    return z, _tri_sum(tri, la)


def _gla_rows(i_blk, ci, lp):
    c = GLA_CHUNK
    rows = i_blk * BLK + ci * c + lax.broadcasted_iota(jnp.int32, (c, 1), 0)
    return (rows >= PADF) & (rows < lp - BACK)


def _gla_off_parts(a, qs, k, g, hm_q):
    s = GLA_SUB
    ra = g[s * a - 1:s * a, :]
    ga_ = g[s * a:s * a + s, :]
    eq = jnp.exp(ga_ - ra)
    ek = jnp.exp(jnp.minimum(ra - g, 0.0))
    qh = qs[s * a:s * a + s, :] * eq
    kh = k * ek
    qst = _stack_heads(qh, hm_q).astype(BF16)
    col = lax.broadcasted_iota(jnp.int32, (GLA_HEADS * s, GLA_CHUNK), 1)
    pmask = col < s * a
    p = jnp.where(pmask, _dot_nt(qst, kh.astype(BF16)), 0.0)
    return eq, ek, qh, kh, qst, pmask, p


def _lag_mask(j):
    r = lax.broadcasted_iota(jnp.int32, (GLA_CHUNK, 1), 0)
    return (jnp.bitwise_and(r, GLA_SUB - 1) >= j).astype(F32)


def _roll_rows(x, j):
    return x if j == 0 else pltpu.roll(x, j, 0)


def _gla(proj, w2p, b, tables, *, name):
    lp = proj.shape[0]
    nblk, c, s = lp // BLK, GLA_CHUNK, GLA_SUB
    nch = BLK // c
    na = c // s

    def body(q_ref, k_ref, v_ref, a_ref, w_ref, b_ref, tri_ref, ones_ref, o_ref, st_ref, state):
        i_blk = pl.program_id(0)

        @pl.when(i_blk == 0)
        def _():
            state[...] = jnp.zeros_like(state)
        hm_q = _head_masks(GLA_QK, GLA_DK)
        tri, ones_qv, w2, bias = tri_ref[...], ones_ref[...], w_ref[...], b_ref[...]

        def chunk(ci, carry):
            sl = pl.ds(pl.multiple_of(ci * c, c), c)
            ok = _gla_rows(i_blk, ci, lp)
            k, v = k_ref[sl, :], v_ref[sl, :]
            vb = v.astype(BF16)
            qs = q_ref[sl, :] * (GLA_DK ** -0.5)
            _, g = _gla_gate(a_ref[sl, :], w2, bias, ok, tri)
            last = g[c - 1:c, :]
            st = state[...]
            st_ref[ci] = st
            qst = _stack_heads(qs * jnp.exp(g), hm_q).astype(BF16)
            oi = _dot_nt(qst, st.astype(BF16))
            o = jnp.concatenate([oi[c * h:c * h + c, :] for h in range(GLA_HEADS)], axis=1)
            ke = k * jnp.exp(last - g)
            f = _dot_tn(vb, ke.astype(BF16))
            upd = f[0:GLA_DV, :] * hm_q[0]
            for h in range(1, GLA_HEADS):
                upd = upd + f[GLA_DV * h:GLA_DV * (h + 1), :] * hm_q[h]
            state[...] = st * jnp.exp(last) + upd
            off = [jnp.zeros((s, GLA_V), F32)]
            for a in range(1, na):
                p = _gla_off_parts(a, qs, k, g, hm_q)[-1]
                ob = _dot(p.astype(BF16), vb)
                off.append(jnp.concatenate(
                    [ob[s * h:s * h + s, GLA_DV * h:GLA_DV * (h + 1)] for h in range(GLA_HEADS)], axis=1))
            o = o + jnp.concatenate(off, axis=0)
            ws = []
            for j in range(s):
                ej = jnp.exp(jnp.minimum(g - _roll_rows(g, j), 0.0))
                ws.append((qs * _roll_rows(k, j) * ej * _lag_mask(j)).astype(BF16))
            ball = _dot(jnp.concatenate(ws, axis=0), ones_qv)
            for j in range(s):
                o = o + ball[c * j:c * j + c, :] * _roll_rows(v, j)
            o_ref[sl, :] = o
            return carry

        lax.fori_loop(0, nch, chunk, 0)

    tri, _, ones_qv, _ = tables
    full = lambda arr: pl.BlockSpec(arr.shape, lambda i: (0,) * arr.ndim)
    return pl.pallas_call(
        body,
        out_shape=(jax.ShapeDtypeStruct((lp, GLA_V), F32), jax.ShapeDtypeStruct((lp // c, GLA_DV, GLA_QK), F32)),
        grid=(nblk,),
        in_specs=[pl.BlockSpec((BLK, GLA_QK), lambda i: (i, C_GQ // GLA_QK)),
                  pl.BlockSpec((BLK, GLA_QK), lambda i: (i, C_GK // GLA_QK)),
                  pl.BlockSpec((BLK, GLA_V), lambda i: (i, C_GV // GLA_V)),
                  pl.BlockSpec((BLK, 128), lambda i: (i, C_GA // 128)),
                  full(w2p), full(b), full(tri), full(ones_qv)],
        out_specs=(pl.BlockSpec((BLK, GLA_V), lambda i: (i, 0)),
                   pl.BlockSpec((nch, GLA_DV, GLA_QK), lambda i: (i, 0, 0))),
        scratch_shapes=[pltpu.VMEM((GLA_DV, GLA_QK), F32)],
        compiler_params=_cp("arbitrary"), name=name)(proj, proj, proj, proj, w2p, b, tri, ones_qv)


def _gla_bwd(proj, do, states, w2p, b, tables, *, name):
    lp = proj.shape[0]
    nblk, c, s = lp // BLK, GLA_CHUNK, GLA_SUB
    nch = BLK // c
    na = c // s

    def body(q_ref, k_ref, v_ref, a_ref, do_ref, st_ref, w_ref, b_ref, tri_ref, trit_ref, ones_ref, onest_ref,
             dq_ref, dk_ref, dv_ref, da_ref, dw_ref, db_ref, dstate, dqs_s, dk_s, dg_s, dv_s):
        i_blk = nblk - 1 - pl.program_id(0)

        @pl.when(pl.program_id(0) == 0)
        def _():
            dstate[...] = jnp.zeros_like(dstate)
            dw_ref[...] = jnp.zeros_like(dw_ref)
            db_ref[...] = jnp.zeros_like(db_ref)
        hm_q = _head_masks(GLA_QK, GLA_DK)
        hm_v = _head_masks(GLA_V, GLA_DV)
        tri, trit, ones_qv, ones_vq = tri_ref[...], trit_ref[...], ones_ref[...], onest_ref[...]
        w2, bias = w_ref[...], b_ref[...]
        rsum = lambda x: jnp.sum(x, axis=0, keepdims=True)

        def chunk(cc, carry):
            ci = nch - 1 - cc
            sl = pl.ds(pl.multiple_of(ci * c, c), c)
            ok = _gla_rows(i_blk, ci, lp)
            k, v, ga = k_ref[sl, :], v_ref[sl, :], a_ref[sl, :]
            vb = v.astype(BF16)
            qs = q_ref[sl, :] * (GLA_DK ** -0.5)
            z, g = _gla_gate(ga, w2, bias, ok, tri)
            last = g[c - 1:c, :]
            elast = jnp.exp(last)
            eg = jnp.exp(g)
            ekl = jnp.exp(last - g)
            qe, ke = qs * eg, k * ekl
            dov = do_ref[sl, :]
            st = st_ref[ci]
            dsn = dstate[...]
            qst = _stack_heads(qe, hm_q).astype(BF16)
            dost = jnp.concatenate([dov[:, GLA_DV * h:GLA_DV * (h + 1)] for h in range(GLA_HEADS)], axis=0).astype(BF16)
            dqe_st = _dot(dost, st.astype(BF16))
            dqe = dqe_st[0:c, :] * hm_q[0]
            for h in range(1, GLA_HEADS):
                dqe = dqe + dqe_st[c * h:c * h + c, :] * hm_q[h]
            dstate[...] = _dot_tn(dost, qst) + dsn * elast
            dlast = rsum(dsn * st) * elast
            df = _stack_heads(dsn, hm_q).astype(BF16)
            dv_s[...] = _dot_nt(ke.astype(BF16), df)
            dke = _dot(vb, df)
            xk = dke * ke
            dqs_s[...] = dqe * eg
            dk_s[...] = dke * ekl
            dg_s[...] = dqe * qe - xk
            dlast = dlast + rsum(xk)
            for a in range(1, na):
                eq, ek, qh, kh, qsa, pmask, p = _gla_off_parts(a, qs, k, g, hm_q)
                rows = slice(s * a, s * a + s)
                dofull = _stack_heads(dov[rows, :], hm_v).astype(BF16)
                dp = jnp.where(pmask, _dot_nt(dofull, vb), 0.0).astype(BF16)
                dv_s[...] += _dot_tn(p.astype(BF16), dofull)
                dq_st = _dot(dp, kh.astype(BF16))
                dqh = dq_st[0:s, :] * hm_q[0]
                for h in range(1, GLA_HEADS):
                    dqh = dqh + dq_st[s * h:s * h + s, :] * hm_q[h]
                dkh = _dot_tn(dp, qsa)
                xq = dqh * qh
                xkh = dkh * kh
                dqs_s[rows, :] += dqh * eq
                dg_s[rows, :] += xq
                dk_s[...] += dkh * ek
                dg_s[...] -= xkh
                dg_s[s * a - 1:s * a, :] += rsum(xkh) - rsum(xq)
            kes, qes, ws, dbs = [], [], [], []
            for j in range(s):
                em = jnp.exp(jnp.minimum(g - _roll_rows(g, j), 0.0)) * _lag_mask(j)
                kes.append(_roll_rows(k, j) * em)
                qes.append(qs * em)
                ws.append((qs * kes[j]).astype(BF16))
                dbs.append((dov * _roll_rows(v, j)).astype(BF16))
            ball = _dot(jnp.concatenate(ws, axis=0), ones_qv)
            dwall = _dot(jnp.concatenate(dbs, axis=0), ones_vq)
            for j in range(s):
                back = (lambda x: x) if j == 0 else (lambda x, j=j: pltpu.roll(x, c - j, 0))
                dw = dwall[c * j:c * j + c, :]
                dv_s[...] += back(ball[c * j:c * j + c, :] * dov)
                dqs_s[...] += dw * kes[j]
                dk_s[...] += back(dw * qes[j])
                x = dw * qs * kes[j]
                dg_s[...] += x - back(x)
            dg_s[c - 1:c, :] += dlast
            dla = jnp.where(ok, _tri_sum(trit, dg_s[...]), 0.0)
            dz = dla * (1.0 / GLA_TAU) / (1.0 + jnp.exp(z))
            dzb = dz.astype(BF16)
            dq_ref[sl, :] = (dqs_s[...] * (GLA_DK ** -0.5)).astype(BF16)
            dk_ref[sl, :] = dk_s[...].astype(BF16)
            dv_ref[sl, :] = dv_s[...].astype(BF16)
            da_ref[sl, :] = _dot_nt(dzb, w2).astype(BF16)
            dw_ref[...] += _dot_tn(ga.astype(BF16), dzb)
            db_ref[...] += rsum(dz)
            return carry

        lax.fori_loop(0, nch, chunk, 0)

    tri, trit, ones_qv, ones_vq = tables
    full = lambda arr: pl.BlockSpec(arr.shape, lambda i: (0,) * arr.ndim)
    rev = lambda i: nblk - 1 - i
    qk = jax.ShapeDtypeStruct((lp, GLA_QK), BF16)
    return pl.pallas_call(
        body,
        out_shape=(qk, qk, jax.ShapeDtypeStruct((lp, GLA_V), BF16), jax.ShapeDtypeStruct((lp, 128), BF16),
                   jax.ShapeDtypeStruct((128, GLA_QK), F32), jax.ShapeDtypeStruct((1, GLA_QK), F32)),
        grid=(nblk,),
        in_specs=[pl.BlockSpec((BLK, GLA_QK), lambda i: (rev(i), C_GQ // GLA_QK)),
                  pl.BlockSpec((BLK, GLA_QK), lambda i: (rev(i), C_GK // GLA_QK)),
                  pl.BlockSpec((BLK, GLA_V), lambda i: (rev(i), C_GV // GLA_V)),
                  pl.BlockSpec((BLK, 128), lambda i: (rev(i), C_GA // 128)),
                  pl.BlockSpec((BLK, GLA_V), lambda i: (rev(i), 0)),
                  pl.BlockSpec((nch, GLA_DV, GLA_QK), lambda i: (rev(i), 0, 0)),
                  full(w2p), full(b), full(tri), full(trit), full(ones_qv), full(ones_vq)],
        out_specs=(pl.BlockSpec((BLK, GLA_QK), lambda i: (rev(i), 0)),
                   pl.BlockSpec((BLK, GLA_QK), lambda i: (rev(i), 0)),
                   pl.BlockSpec((BLK, GLA_V), lambda i: (rev(i), 0)),
                   pl.BlockSpec((BLK, 128), lambda i: (rev(i), 0)),
                   pl.BlockSpec((128, GLA_QK), lambda i: (0, 0)),
                   pl.BlockSpec((1, GLA_QK), lambda i: (0, 0))),
        scratch_shapes=[pltpu.VMEM((GLA_DV, GLA_QK), F32), pltpu.VMEM((c, GLA_QK), F32),
                        pltpu.VMEM((c, GLA_QK), F32), pltpu.VMEM((c, GLA_QK), F32), pltpu.VMEM((c, GLA_V), F32)],
        compiler_params=_cp("arbitrary"), name=name)(proj, proj, proj, proj, do, states, w2p, b, tri, trit, ones_qv, ones_vq)


def _as2d(a):
    return a.reshape(-1, a.shape[-1])


def _ew_tile(r):
    return _tile(r, (512, 256, 128, 64, 32, 16, 8))


def _add2(a, b, *, out_dtype, name):
    a2, b2 = _as2d(a), _as2d(b)
    r, n = a2.shape
    tm = _ew_tile(r)

    def body(a_ref, b_ref, o_ref):
        o_ref[...] = (a_ref[...] + b_ref[...]).astype(o_ref.dtype)

    blk = pl.BlockSpec((tm, n), lambda i: (i, 0))
    return pl.pallas_call(body, out_shape=jax.ShapeDtypeStruct((r, n), out_dtype), grid=(r // tm,), in_specs=[blk, blk],
                          out_specs=blk, compiler_params=_cp("parallel"), name=name)(a2, b2).reshape(a.shape)


def _sum_slots(own, q, *, name):
    shape = own.shape
    q3 = q.reshape(3, -1, shape[-1])
    own2 = _as2d(own)
    r, n = own2.shape
    tm = _ew_tile(r)

    def body(own_ref, q_ref, o_ref):
        f = lambda i: q_ref[i].astype(F32)
        o_ref[...] = ((own_ref[...].astype(F32) + f(0)) + f(1)) + f(2)

    blk = pl.BlockSpec((tm, n), lambda i: (i, 0))
    return pl.pallas_call(
        body, out_shape=jax.ShapeDtypeStruct((r, n), F32), grid=(r // tm,),
        in_specs=[blk, pl.BlockSpec((3, tm, n), lambda i: (0, i, 0))], out_specs=blk,
        compiler_params=_cp("parallel"), name=name)(own2, q3).reshape(shape)


def _adamw(w, g, m, v, *, name):
    shape = w.shape
    w2, g2, m2, v2 = _as2d(w), _as2d(g), _as2d(m), _as2d(v)
    r, n = w2.shape
    tm = _ew_tile(r)
    c1 = 1.0 - ADAM_B1 ** ADAM_STEP
    c2 = 1.0 - ADAM_B2 ** ADAM_STEP

    def body(w_ref, g_ref, m_ref, v_ref, d_ref, mo_ref, vo_ref):
        gv = g_ref[...]
        mn = ADAM_B1 * m_ref[...] + (1.0 - ADAM_B1) * gv
        vn = ADAM_B2 * v_ref[...] + (1.0 - ADAM_B2) * (gv * gv)
        mo_ref[...] = mn
        vo_ref[...] = vn
        d_ref[...] = -ADAM_LR * ((mn / c1) / (jnp.sqrt(vn / c2) + ADAM_EPS) + ADAM_WD * w_ref[...])

    blk = pl.BlockSpec((tm, n), lambda i: (i, 0))
    o = jax.ShapeDtypeStruct((r, n), F32)
    d, mo, vo = pl.pallas_call(body, out_shape=(o, o, o), grid=(r // tm,), in_specs=[blk] * 4, out_specs=(blk,) * 3,
                               compiler_params=_cp("parallel"), name=name)(w2, g2, m2, v2)
    return d.reshape(shape), mo.reshape(shape), vo.reshape(shape)


ANY = pl.BlockSpec(memory_space=pl.ANY)


def _place():
    return lax.axis_index("x"), lax.axis_index("y"), lax.axis_index("c")


def _other_chips(x, y):
    return [(1 - x, y), (x, 1 - y), (1 - x, 1 - y)]


def _remote(src, dst, ssem, rsem, dev):
    return pltpu.make_async_remote_copy(src_ref=src, dst_ref=dst, send_sem=ssem, recv_sem=rsem, device_id=dev,
                                        device_id_type=MESH)


def _allgather_chips(arrs, *, name):
    n = len(arrs)

    def body(*refs):
        ins, outs = refs[:n], refs[n:2 * n]
        s1, r1, s2, r2 = refs[2 * n:]
        x, y, c = _place()
        q = 2 * x + y
        chips = _other_chips(x, y)
        qs = [2 * cx + cy for cx, cy in chips]
        sib = (x, y, 1 - c)
        first, passed = [], []
        for k in range(n):
            for j, chip in enumerate(chips):
                first.append(_remote(ins[k].at[c], outs[k].at[c, q], s1.at[k, j], r1.at[k, j], (*chip, c)))
        for cp in first:
            cp.start()
        for k in range(n):
            for j, chip in enumerate(chips):
                land = outs[k].at[c, qs[j]]
                _remote(land, land, s1.at[k, j], r1.at[k, j], (*chip, c)).wait_recv()
                fw = _remote(land, land, s2.at[k, j], r2.at[k, j], sib)
                fw.start()
                passed.append(fw)
        for k in range(n):
            for j in range(3):
                land = outs[k].at[1 - c, qs[j]]
                _remote(land, land, s2.at[k, j], r2.at[k, j], sib).wait_recv()
        for cp in first + passed:
            cp.wait_send()

    sem = pltpu.SemaphoreType.DMA
    outs = pl.pallas_call(
        body, out_shape=tuple(jax.ShapeDtypeStruct((2, 4) + a.shape[1:], a.dtype) for a in arrs),
        in_specs=[ANY] * n, out_specs=(ANY,) * n,
        scratch_shapes=[sem((n, 3)), sem((n, 3)), sem((n, 3)), sem((n, 3))], name=name)(*arrs)
    chip = 2 * lax.axis_index("x") + lax.axis_index("y")
    return [lax.dynamic_update_slice_in_dim(o, a[:, None], chip, axis=1) for o, a in zip(outs, arrs)]


def _pair_exchange(arrs, *, name):
    n = len(arrs)

    def body(*refs):
        ins, outs = refs[:n], refs[n:2 * n]
        ssem, rsem = refs[2 * n:]
        x, y, c = _place()
        cps = [_remote(ins[k].at[1 - c], outs[k], ssem.at[k], rsem.at[k], (x, y, 1 - c)) for k in range(n)]
        for cp in cps:
            cp.start()
        for cp in cps:
            cp.wait()

    sem = pltpu.SemaphoreType.DMA
    return pl.pallas_call(
        body, out_shape=tuple(jax.ShapeDtypeStruct(a.shape[1:], a.dtype) for a in arrs),
        in_specs=[ANY] * n, out_specs=(ANY,) * n, scratch_shapes=[sem((n,)), sem((n,))], name=name)(*arrs)


def _chip_copies(ins, outs, ssem, rsem, mode):
    x, y, c = _place()
    cps = []
    for k in range(len(ins)):
        for j, (cx, cy) in enumerate(_other_chips(x, y)):
            src = ins[k].at[2 * cx + cy] if mode == "scatter" else ins[k].at[c]
            cps.append(_remote(src, outs[k].at[j], ssem.at[k, j], rsem.at[k, j], (cx, cy, c)))
    return cps


def _landing_shape(a):
    return jax.ShapeDtypeStruct((3,) + a.shape[1:], a.dtype)


def _chip_exchange(arrs, mode, *, name):
    n = len(arrs)

    def body(*refs):
        ins, outs = refs[:n], refs[n:2 * n]
        ssem, rsem = refs[2 * n:]
        cps = _chip_copies(ins, outs, ssem, rsem, mode)
        for cp in cps:
            cp.start()
        for cp in cps:
            cp.wait()

    sem = pltpu.SemaphoreType.DMA
    return list(pl.pallas_call(
        body, out_shape=tuple(_landing_shape(a) for a in arrs),
        in_specs=[ANY] * n, out_specs=(ANY,) * n, scratch_shapes=[sem((n, 3)), sem((n, 3))], name=name)(*arrs))


def _pair_swap(arrs, *, name):
    n = len(arrs)

    def body(*refs):
        ins, outs = refs[:n], refs[n:2 * n]
        ssem, rsem = refs[2 * n:]
        x, y, c = _place()
        cps = [_remote(ins[k], outs[k], ssem.at[k], rsem.at[k], (x, y, 1 - c)) for k in range(n)]
        for cp in cps:
            cp.start()
        for cp in cps:
            cp.wait()

    sem = pltpu.SemaphoreType.DMA
    return pl.pallas_call(
        body, out_shape=tuple(jax.ShapeDtypeStruct(a.shape, a.dtype) for a in arrs),
        in_specs=[ANY] * n, out_specs=(ANY,) * n, scratch_shapes=[sem((n,)), sem((n,))], name=name)(*arrs)


def _allreduce_small(slab, *, name):
    r, n = slab.shape

    def body(x_ref, o_ref, buf, ssem, rsem):
        x, y, c = _place()
        me = 4 * x + 2 * y + c
        buf[me] = x_ref[...]
        cps = []
        for rel in range(1, 8):
            bx, by, bc = (rel >> 2) & 1, (rel >> 1) & 1, rel & 1
            px, py, pc = (x + bx) % 2, (y + by) % 2, (c + bc) % 2
            cps.append((_remote(x_ref, buf.at[me], ssem.at[rel - 1], rsem.at[rel - 1], (px, py, pc)),
                        4 * px + 2 * py + pc, (px, py, pc)))
        for cp, _, _ in cps:
            cp.start()
        for rel, (cp, peer, dev) in enumerate(cps):
            cp.wait_send()
            _remote(x_ref, buf.at[peer], ssem.at[rel], rsem.at[rel], dev).wait_recv()
        acc = buf[0]
        for k in range(1, 8):
            acc = acc + buf[k]
        o_ref[...] = acc

    vm = pl.BlockSpec(memory_space=pltpu.VMEM)
    sem = pltpu.SemaphoreType.DMA
    return pl.pallas_call(
        body, out_shape=jax.ShapeDtypeStruct((r, n), F32), in_specs=[vm], out_specs=vm,
        scratch_shapes=[pltpu.VMEM((8, r, n), F32), sem((7,)), sem((7,))], name=name)(slab)


def _slab(arrs, row_mult):
    flat = jnp.concatenate([a.reshape(-1) for a in arrs])
    unit = 128 * row_mult
    total = -(-flat.size // unit) * unit
    return jnp.pad(flat, (0, total - flat.size)).reshape(-1, 128)


def _unslab(slab, shapes):
    flat = slab.reshape(-1)
    out, off = [], 0
    for s in shapes:
        size = int(np.prod(s))
        out.append(flat[off:off + size].reshape(s))
        off += size
    return out


def _cols_from_chips(a):
    return jnp.transpose(a, (1, 0, 2)).reshape(a.shape[1], -1)


def _cols_to_chips(a, parts):
    r = a.shape[0]
    return jnp.transpose(a.reshape(r, parts, -1), (1, 0, 2))


BIG = ("w_in", "w_out", "up", "down")
GATHER_RIDES = {("proj", 0): (("w_out", 0), ("up", 0)), ("mix_out", 0): (("down", 0),),
                ("ffn_up_a", 0): (("w_in", 1), ("w_out", 1)), ("ffn_up_g", 0): (("up", 1),),
                ("ffn_down", 0): (("down", 1),)}
REDUCE_RIDES = {("ffn_down_dx", 0): ("up",), ("ffn_up_a_dx", 0): ("w_in", "w_out"), ("ffn_up_g_dx", 0): ("down",)}


class _LocalWeights:
    def __init__(self, meta, win, wout, up_a, up_g, down, w2p, cw):
        self._meta, self._w = meta, {"win": win, "wout": wout, "up_a": up_a, "up_g": up_g, "down": down, "w2p": w2p,
                                     "cw": cw}

    def meta(self):
        return self._meta

    def get(self, kind, l):
        return self._w[kind][l]

    def mm(self, site, l, a, b, **kw):
        return _mm(a, b, name=site, **kw)

    def grads_done(self, l, g):
        pass


class _ChipWeights:
    def __init__(self, w_in, w_out, ffn_up, ffn_down, meta_tokens, gla_gate_w2, ffn_conv_w):
        self.x, self.y, self.c = _place()
        self.q = 2 * self.x + self.y
        halves = lambda a: a.astype(BF16).reshape(2, a.shape[0] // 2, a.shape[1])
        self.own = {(k, l): halves(a[l]) for k, a in zip(BIG, (w_in, w_out, ffn_up, ffn_down)) for l in range(DEPTH)}
        self.landed, self.swapped, self.full, self.n_swaps = {}, {}, {}, 0
        self.sh_shapes = [meta_tokens.shape, gla_gate_w2.shape, ffn_conv_w.shape]
        self.own["small", 0] = _slab([meta_tokens, gla_gate_w2, ffn_conv_w], 16).reshape(2, -1, 128)
        first = [("w_in", 0), ("small", 0)]
        for key, arr in zip(first, _chip_exchange([self.own[k] for k in first], "bcast", name="gather_first")):
            self.landed[key] = arr
        sh = self._whole("small", 0).reshape(4, -1, 128)
        parts = [_unslab(sh[k], self.sh_shapes) for k in range(4)]
        self._meta = jnp.concatenate([p[0] for p in parts], axis=-1)
        self.w2 = jnp.concatenate([p[1] for p in parts], axis=-1)
        self.cw = jnp.concatenate([p[2] for p in parts], axis=-1)
        self.partial, self.slots = {}, {}

    def _whole(self, kind, l):
        if (kind, l) not in self.full:
            if (kind, l) not in self.swapped:
                keys = [k for k in self.landed if k not in self.swapped]
                got = _pair_swap([self.landed[k] for k in keys], name=f"gather_swap_{self.n_swaps}")
                self.n_swaps += 1
                self.swapped.update(zip(keys, got))
            own, land, swap = self.own[kind, l], self.landed[kind, l], self.swapped[kind, l]
            full = jnp.zeros((4,) + own.shape, own.dtype)
            full = lax.dynamic_update_slice(full, own[None], (self.q, 0, 0, 0))
            for j, m in enumerate((2, 1, 3)):
                qj = jnp.bitwise_xor(self.q, m)
                full = lax.dynamic_update_slice(full, land[j][None, None], (qj, self.c, 0, 0))
                full = lax.dynamic_update_slice(full, swap[j][None, None], (qj, 1 - self.c, 0, 0))
            self.full[kind, l] = full.reshape(4, 2 * own.shape[1], own.shape[2])
        return self.full[kind, l]

    def meta(self):
        return self._meta

    def get(self, kind, l):
        if kind == "win":
            return jnp.pad(_cols_from_chips(self._whole("w_in", l)), ((0, 0), (0, IN_PAD - IN_WIDTH)))
        if kind == "wout":
            return self._whole("w_out", l).reshape(D_MODEL, D_MODEL)
        if kind == "up_a":
            return _cols_from_chips(self._whole("up", l)[0:2])
        if kind == "up_g":
            return _cols_from_chips(self._whole("up", l)[2:4])
        if kind == "down":
            return self._whole("down", l).reshape(D_FF, D_MODEL)
        if kind == "w2p":
            return jnp.pad(self.w2[l], ((0, 128 - GLA_RANK), (0, 0))).astype(BF16)
        return self.cw[l]

    def mm(self, site, l, a, b, **kw):
        if (site, l) in GATHER_RIDES:
            keys = GATHER_RIDES[site, l]
            out, got = _mm(a, b, name=site, carry=([self.own[k] for k in keys], "bcast"), **kw)
            self.landed.update(zip(keys, got))
            return out
        if (site, l) in REDUCE_RIDES and all((k, DEPTH - 1) in self.partial for k in REDUCE_RIDES[site, l]):
            keys = [(k, DEPTH - 1) for k in REDUCE_RIDES[site, l]]
            out, got = _mm(a, b, name=site, carry=([self.partial[k] for k in keys], "scatter"), **kw)
            self.slots.update(zip(keys, got))
            return out
        return _mm(a, b, name=site, **kw)

    def grads_done(self, l, g):
        d = D_MODEL
        rows = lambda a: jnp.transpose(a.reshape(4, 2, a.shape[0] // 8, a.shape[1]), (1, 0, 2, 3))
        cols = lambda a, n: jnp.transpose(a.reshape(2, a.shape[0] // 2, n, a.shape[1] // n), (0, 2, 1, 3))
        big = {"w_in": cols(g["w_in"][l][:, :IN_WIDTH], 4), "w_out": rows(g["w_out"][l]),
               "up": jnp.concatenate([cols(g["up_a"][l], 2), cols(g["up_g"][l], 2)], axis=1), "down": rows(g["down"][l])}
        from_sib = _pair_exchange([big[k] for k in BIG], name=f"grads_pair_exchange_{l}")
        for k, theirs in zip(BIG, from_sib):
            mine = lax.dynamic_index_in_dim(big[k], self.c, 0, keepdims=False)
            self.partial[k, l] = _add2(mine, theirs, out_dtype=BF16, name=f"pair_sum_{k}_{l}")

    def reduce(self):
        keys = [(k, l) for l in range(DEPTH) for k in BIG]
        late = [k for k in keys if k not in self.slots]
        self.slots.update(zip(late, _chip_exchange([self.partial[k] for k in late], "scatter",
                                                   name="grads_chip_exchange")))
        half = {}
        for k in keys:
            own = lax.dynamic_index_in_dim(self.partial[k], self.q, 0, keepdims=False)
            half[k] = _sum_slots(own, self.slots[k], name=f"chip_sum_{k[0]}_{k[1]}")
        other = dict(zip(keys, _pair_swap([half[k] for k in keys], name="grads_pair_swap")))
        whole = lambda k: jnp.where(self.c == 0, jnp.concatenate([half[k], other[k]], axis=0),
                                    jnp.concatenate([other[k], half[k]], axis=0))
        return [jnp.stack([whole((k, l)) for l in range(DEPTH)]) for k in BIG]


def _local_step(x_rows, target_rows, wts, pre_mix_norm, gla_gate_b, ret_norm_w, gla_norm_w, post_mix_norm,
                pre_ffn_norm, ffn_conv_b, post_ffn_norm):
    d = D_MODEL
    lp = x_rows.shape[0] + FRONT + BACK
    row = lambda a, l: a[l][None, :]
    rtab = _ret_tables(lp)
    gtab = _gla_tables()
    h0 = jnp.concatenate([jnp.zeros((PADF, d), F32), wts.meta(), x_rows, jnp.zeros((BACK, d), F32)], axis=0)
    target = jnp.pad(target_rows, ((FRONT, BACK), (0, 0)))

    saved = []
    h = h0
    _, hn = _resid_norm(h0, None, None, row(pre_mix_norm, 0), name="norm_in")
    loss_local = dy = None
    for l in range(DEPTH):
        s = {"h_in": h, "hn": hn}
        s["proj"] = wts.mm("proj", l, hn, wts.get("win", l))
        s["o_ret"], s["st_ret"] = _retention(s["proj"], rtab, name="retention")
        s["o_gla"], s["st_gla"] = _gla(s["proj"], wts.get("w2p", l), row(gla_gate_b, l), gtab, name="gla")
        s["merged"] = _merge(s["o_ret"], s["o_gla"], s["proj"], row(ret_norm_w, l), row(gla_norm_w, l), name="merge")
        s["m"] = wts.mm("mix_out", l, s["merged"], wts.get("wout", l))
        s["h_mid"], s["hn2"] = _resid_norm(h, s["m"], row(post_mix_norm, l), row(pre_ffn_norm, l), name="resid_mix")
        s["ua"] = wts.mm("ffn_up_a", l, s["hn2"], wts.get("up_a", l))
        s["ug"] = wts.mm("ffn_up_g", l, s["hn2"], wts.get("up_g", l))
        cw_a, cw_g = wts.get("cw", l)[:, :D_FF], wts.get("cw", l)[:, D_FF:]
        cb_a, cb_g = ffn_conv_b[l][None, :D_FF], ffn_conv_b[l][None, D_FF:]
        s["conv"] = (cw_a, cw_g, cb_a, cb_g)
        s["act"], s["ca"], s["cg"] = _conv_act(s["ua"], s["ug"], cw_a, cw_g, cb_a, cb_g, name="conv_act")
        s["f"] = wts.mm("ffn_down", l, s["act"], wts.get("down", l))
        if l + 1 < DEPTH:
            h, hn = _resid_norm(s["h_mid"], s["f"], row(post_ffn_norm, l), row(pre_mix_norm, l + 1), name="resid_ffn")
        else:
            loss_local, dy = _loss_head(s["h_mid"], s["f"], row(post_ffn_norm, l), target, name="loss_head")
        saved.append(s)

    g = {k: [None] * DEPTH for k in ("pre_mix", "w_in", "w2", "gb", "ret_n", "gla_n", "w_out", "post_mix", "pre_ffn",
                                     "up_a", "up_g", "cw", "cb", "down", "post_ffn")}
    dh_out, dhn_next = dy, None
    for l in reversed(range(DEPTH)):
        s = saved[l]
        cw_a, cw_g, cb_a, cb_g = s["conv"]
        if l + 1 < DEPTH:
            dh, df, g["pre_mix"][l + 1], g["post_ffn"][l] = _resid_norm_bwd(
                dh_out, dhn_next, saved[l + 1]["h_in"], s["f"], row(pre_mix_norm, l + 1), row(post_ffn_norm, l),
                name="resid_ffn_bwd")
        else:
            dh, df, _, g["post_ffn"][l] = _resid_norm_bwd(dh_out, None, None, s["f"], None, row(post_ffn_norm, l),
                                                          name="loss_head_bwd")
        dact = wts.mm("ffn_down_dx", l, df, wts.get("down", l), nt=True)
        g["down"][l] = _mm_tn(s["act"], df, tn=512, name="ffn_down_dw")
        du_a, du_g, dcw_a, dcw_g, dcb_a, dcb_g = _conv_act_bwd(s["ua"], s["ug"], s["ca"], s["cg"], dact, cw_a, cw_g,
                                                               name="conv_act_bwd")
        g["cw"][l] = jnp.concatenate([dcw_a, dcw_g], axis=1)
        g["cb"][l] = jnp.concatenate([dcb_a, dcb_g], axis=1)[0]
        g["up_a"][l] = _mm_tn(s["hn2"], du_a, tn=1408, name="ffn_up_a_dw")
        g["up_g"][l] = _mm_tn(s["hn2"], du_g, tn=1408, name="ffn_up_g_dw")
        dhn2 = wts.mm("ffn_up_a_dx", l, du_a, wts.get("up_a", l), nt=True)
        dhn2 = wts.mm("ffn_up_g_dx", l, du_g, wts.get("up_g", l), nt=True, add=dhn2)
        dh, dm, g["pre_ffn"][l], g["post_mix"][l] = _resid_norm_bwd(
            dh, dhn2, s["h_mid"], s["m"], row(pre_ffn_norm, l), row(post_mix_norm, l), name="resid_mix_bwd")
        g["w_out"][l] = _mm_tn(s["merged"], dm, name="mix_out_dw")
        dmerged = wts.mm("mix_out_dx", l, dm, wts.get("wout", l), nt=True)
        do_ret, do_gla, drg, dgr, g["ret_n"][l], g["gla_n"][l] = _merge_bwd(
            dmerged, s["o_ret"], s["o_gla"], s["proj"], row(ret_norm_w, l), row(gla_norm_w, l), name="merge_bwd")
        drq, drk, drv = _retention_bwd(s["proj"], do_ret, s["st_ret"], rtab, name="retention_bwd")
        dgq, dgk, dgv, dga, dw2, dgb = _gla_bwd(s["proj"], do_gla, s["st_gla"], wts.get("w2p", l), row(gla_gate_b, l), gtab,
                                                name="gla_bwd")
        g["w2"][l], g["gb"][l] = dw2[:GLA_RANK], dgb[0]
        dproj = jnp.concatenate([drq, drk, drv, drg, dgq, dgk, dgv, dgr, dga,
                                 jnp.zeros((lp, IN_PAD - C_GA - 128), BF16)], axis=1)
        g["w_in"][l] = _mm_tn(s["hn"], dproj, tn=1280, name="proj_dw")
        dhn_next = wts.mm("proj_dx", l, dproj, wts.get("win", l), nt=True)
        dh_out = dh
        wts.grads_done(l, g)
    dh0, _, g["pre_mix"][0], _ = _resid_norm_bwd(dh_out, dhn_next, h0, None, row(pre_mix_norm, 0), None,
                                                 name="norm_in_bwd")
    return loss_local, dh0, g


def kernel(x, meta_tokens, pre_mix_norm, w_in, gla_gate_w2, gla_gate_b, ret_norm_w, gla_norm_w, w_out, post_mix_norm, pre_ffn_norm, ffn_up, ffn_conv_w, ffn_conv_b, ffn_down, post_ffn_norm, loss_target, m_meta_tokens, m_pre_mix_norm, m_w_in, m_gla_gate_w2, m_gla_gate_b, m_ret_norm_w, m_gla_norm_w, m_w_out, m_post_mix_norm, m_pre_ffn_norm, m_ffn_up, m_ffn_conv_w, m_ffn_conv_b, m_ffn_down, m_post_ffn_norm, v_meta_tokens, v_pre_mix_norm, v_w_in, v_gla_gate_w2, v_gla_gate_b, v_ret_norm_w, v_gla_norm_w, v_w_out, v_post_mix_norm, v_pre_ffn_norm, v_ffn_up, v_ffn_conv_w, v_ffn_conv_b, v_ffn_down, v_post_ffn_norm):
    xi, yi, ci = _place()
    chip = 2 * xi + yi
    seq = x.shape[1]
    d = D_MODEL
    wts = _ChipWeights(w_in, w_out, ffn_up, ffn_down, meta_tokens, gla_gate_w2, ffn_conv_w)
    loss_local, dh0, g = _local_step(x[0], loss_target[0], wts, pre_mix_norm, gla_gate_b, ret_norm_w, gla_norm_w,
                                     post_mix_norm, pre_ffn_norm, ffn_conv_b, post_ffn_norm)
    grad_x = dh0[FRONT:FRONT + seq][None]
    names = ("w_in", "w_out", "ffn_up", "ffn_down")
    g_w_in, g_w_out, g_ffn_up, g_ffn_down = wts.reduce()

    small_full = [dh0[PADF:FRONT], jnp.stack(g["pre_mix"])[:, 0], jnp.stack(g["w2"]), jnp.stack(g["gb"]),
                  jnp.stack(g["ret_n"])[:, 0], jnp.stack(g["gla_n"])[:, 0], jnp.stack(g["post_mix"])[:, 0],
                  jnp.stack(g["pre_ffn"])[:, 0], jnp.stack(g["cw"]), jnp.stack(g["cb"]),
                  jnp.stack(g["post_ffn"])[:, 0]]
    small_sum = _unslab(_allreduce_small(_slab(small_full, 8), name="small_allreduce"), [a.shape for a in small_full])
    (g_meta, g_pre_mix, g_w2, g_gb, g_ret_n, g_gla_n, g_post_mix, g_pre_ffn, g_cw, g_cb, g_post_ffn) = small_sum
    g_meta = lax.dynamic_slice_in_dim(g_meta, chip * 256, 256, axis=1)
    g_w2 = lax.dynamic_slice_in_dim(g_w2, chip * 64, 64, axis=2)
    g_cw = lax.dynamic_slice_in_dim(g_cw, chip * 1408, 1408, axis=2)

    grads = [g_meta, g_pre_mix, g_w_in, g_w2, g_gb, g_ret_n, g_gla_n, g_w_out, g_post_mix, g_pre_ffn, g_ffn_up,
             g_cw, g_cb, g_ffn_down, g_post_ffn]
    ws = [meta_tokens, pre_mix_norm, w_in, gla_gate_w2, gla_gate_b, ret_norm_w, gla_norm_w, w_out, post_mix_norm,
          pre_ffn_norm, ffn_up, ffn_conv_w, ffn_conv_b, ffn_down, post_ffn_norm]
    ms = [m_meta_tokens, m_pre_mix_norm, m_w_in, m_gla_gate_w2, m_gla_gate_b, m_ret_norm_w, m_gla_norm_w, m_w_out,
          m_post_mix_norm, m_pre_ffn_norm, m_ffn_up, m_ffn_conv_w, m_ffn_conv_b, m_ffn_down, m_post_ffn_norm]
    vs = [v_meta_tokens, v_pre_mix_norm, v_w_in, v_gla_gate_w2, v_gla_gate_b, v_ret_norm_w, v_gla_norm_w, v_w_out,
          v_post_mix_norm, v_pre_ffn_norm, v_ffn_up, v_ffn_conv_w, v_ffn_conv_b, v_ffn_down, v_post_ffn_norm]
    big_idx = (2, 7, 10, 13)
    deltas, new_m, new_v = [None] * 15, [None] * 15, [None] * 15
    for i, nm in zip(big_idx, names):
        deltas[i], new_m[i], new_v[i] = _adamw(ws[i], grads[i], ms[i], vs[i], name=f"adamw_{nm}")
    small_idx = [i for i in range(15) if i not in big_idx]
    shapes = [ws[i].shape for i in small_idx]
    sd, sm, sv = _adamw(_slab([ws[i] for i in small_idx], 8), _slab([grads[i] for i in small_idx], 8),
                        _slab([ms[i] for i in small_idx], 8), _slab([vs[i] for i in small_idx], 8), name="adamw_small")
    for i, a, b, c_ in zip(small_idx, _unslab(sd, shapes), _unslab(sm, shapes), _unslab(sv, shapes)):
        deltas[i], new_m[i], new_v[i] = a, b, c_

    loss = lax.psum(loss_local, ("x", "y", "c"))
    return (loss, grad_x, *grads, *deltas, *new_m, *new_v)
```

```python
import functools
import math

import numpy as np
import jax
import jax.numpy as jnp
from jax import lax
from jax.experimental import pallas as pl
from jax.experimental.pallas import tpu as pltpu

F32 = jnp.float32
BF16 = jnp.bfloat16

D_MODEL = 1024
DEPTH = 2
N_META = 16
EPS = 1e-6
RET_HEADS = 4
RET_DK = 128
GLA_HEADS = 4
GLA_DK = 64
GLA_DV = 128
GLA_QK = GLA_HEADS * GLA_DK
GLA_V = GLA_HEADS * GLA_DV
GLA_RANK = 16
GLA_TAU = 16.0
D_FF = 2816
ROPE_BASE = 10000.0
IN_WIDTH = 3600
IN_PAD = 3840
C_RQ, C_RK, C_RV, C_RG, C_GQ, C_GK, C_GV, C_GR, C_GA = 0, 512, 1024, 1536, 2048, 2304, 2560, 3072, 3584

FRONT = 64
BACK = 64
PADF = FRONT - N_META
RET_CHUNK = 128
GLA_CHUNK = 64
GLA_SUB = 16
BLK = 640

ADAM_LR, ADAM_B1, ADAM_B2, ADAM_EPS, ADAM_WD, ADAM_STEP = 0.001, 0.9, 0.999, 1e-08, 0.01, 10

VMEM_LIMIT = 56 * 2 ** 20
MESH = pl.DeviceIdType.MESH


def _cp(*sem):
    return pltpu.CompilerParams(dimension_semantics=sem, vmem_limit_bytes=VMEM_LIMIT)


def _tile(n, cands):
    for t in cands:
        if n % t == 0:
            return t
    raise ValueError(f"no tile for {n} in {cands}")


def _row_tile(n):
    return _tile(n, (640, 512, 320, 256, 128, 64))


def _mm(a, b, *, nt=False, add=None, out_dtype=F32, tn=None, name, carry=None):
    m, k = a.shape
    n = b.shape[0] if nt else b.shape[1]
    tm = _tile(m, (640, 320, 256, 128, 64))
    tn = n if tn is None else tn
    dn = (((1,), (1,)), ((), ())) if nt else (((1,), (0,)), ((), ()))
    nj, ni = n // tn, m // tm
    n_in = 2 + (add is not None)
    c_arrs, c_mode = carry if carry is not None else ((), None)
    nc = len(c_arrs)

    def body(*refs):
        a_ref, b_ref = refs[:2]
        c_ref = refs[2] if add is not None else None
        o_ref = refs[n_in + nc]
        if nc:
            c_ins, c_outs = refs[n_in:n_in + nc], refs[n_in + nc + 1:n_in + 2 * nc + 1]
            ssem, rsem = refs[n_in + 2 * nc + 1:]
            j, i = pl.program_id(0), pl.program_id(1)

            @pl.when((j == 0) & (i == 0))
            def _():
                for cp in _chip_copies(c_ins, c_outs, ssem, rsem, c_mode):
                    cp.start()
        r = lax.dot_general(a_ref[...].astype(BF16), b_ref[...].astype(BF16), dn, preferred_element_type=F32)
        if add is not None:
            r = r + c_ref[...]
        o_ref[...] = r.astype(o_ref.dtype)
        if nc:
            @pl.when((j == nj - 1) & (i == ni - 1))
            def _():
                for cp in _chip_copies(c_ins, c_outs, ssem, rsem, c_mode):
                    cp.wait()

    b_spec = pl.BlockSpec((tn, k), lambda j, i: (j, 0)) if nt else pl.BlockSpec((k, tn), lambda j, i: (0, j))
    in_specs = [pl.BlockSpec((tm, k), lambda j, i: (i, 0)), b_spec]
    args = [a, b]
    if add is not None:
        in_specs.append(pl.BlockSpec((tm, tn), lambda j, i: (i, j)))
        args.append(add)
    out_shape = jax.ShapeDtypeStruct((m, n), out_dtype)
    out_spec = pl.BlockSpec((tm, tn), lambda j, i: (i, j))
    if not nc:
        return pl.pallas_call(
            body, out_shape=out_shape, grid=(nj, ni), in_specs=in_specs, out_specs=out_spec,
            compiler_params=_cp("parallel", "parallel"), name=name)(*args)
    sem = pltpu.SemaphoreType.DMA
    outs = pl.pallas_call(
        body, out_shape=(out_shape,) + tuple(_landing_shape(x) for x in c_arrs), grid=(nj, ni),
        in_specs=in_specs + [ANY] * nc, out_specs=(out_spec,) + (ANY,) * nc,
        scratch_shapes=[sem((nc, 3)), sem((nc, 3))],
        compiler_params=_cp("arbitrary", "arbitrary"), name=name)(*args, *c_arrs)
    return outs[0], list(outs[1:])


def _mm_tn(a, b, *, tn=None, blocked=False, name):
    m, k = a.shape
    n = b.shape[1]
    tm = _tile(m, (1664, 640, 320, 256, 128, 64))
    tn = n if tn is None else tn
    if blocked:
        out_shape = jax.ShapeDtypeStruct((n // tn, k, tn), F32)
        out_spec = pl.BlockSpec((None, k, tn), lambda j, i: (j, 0, 0))
    else:
        out_shape = jax.ShapeDtypeStruct((k, n), F32)
        out_spec = pl.BlockSpec((k, tn), lambda j, i: (0, j))

    def body(a_ref, b_ref, o_ref):
        @pl.when(pl.program_id(1) == 0)
        def _():
            o_ref[...] = jnp.zeros_like(o_ref)
        o_ref[...] += lax.dot_general(a_ref[...].astype(BF16), b_ref[...].astype(BF16),
                                      (((0,), (0,)), ((), ())), preferred_element_type=F32)

    return pl.pallas_call(
        body, out_shape=out_shape, grid=(n // tn, m // tm),
        in_specs=[pl.BlockSpec((tm, k), lambda j, i: (i, 0)), pl.BlockSpec((tm, tn), lambda j, i: (i, j))],
        out_specs=out_spec, compiler_params=_cp("parallel", "arbitrary"), name=name)(a, b)


def _rms(x, w):
    r = lax.rsqrt(jnp.mean(x * x, axis=-1, keepdims=True) + EPS)
    return x * r * w


def _rms_bwd(x, w, dy):
    r = lax.rsqrt(jnp.mean(x * x, axis=-1, keepdims=True) + EPS)
    xh = x * r
    dxh = dy * w
    dx = r * (dxh - xh * jnp.mean(dxh * xh, axis=-1, keepdims=True))
    return dx, jnp.sum(dy * xh, axis=0, keepdims=True)


def _resid_norm(h, t, w_post, w_next, *, name):
    lp, d = h.shape
    tm = _row_tile(lp)
    has_t = t is not None

    def body(*refs):
        if has_t:
            h_ref, t_ref, wp_ref, wn_ref, ho_ref, hn_ref = refs
            hv = h_ref[...] + _rms(t_ref[...], wp_ref[...])
            ho_ref[...] = hv
        else:
            h_ref, wn_ref, hn_ref = refs
            hv = h_ref[...]
        hn_ref[...] = _rms(hv, wn_ref[...]).astype(BF16)

    row = pl.BlockSpec((tm, d), lambda i: (i, 0))
    vec = pl.BlockSpec((1, d), lambda i: (0, 0))
    if has_t:
        return pl.pallas_call(
            body, out_shape=(jax.ShapeDtypeStruct((lp, d), F32), jax.ShapeDtypeStruct((lp, d), BF16)),
            grid=(lp // tm,), in_specs=[row, row, vec, vec], out_specs=(row, row),
            compiler_params=_cp("parallel"), name=name)(h, t, w_post, w_next)
    return h, pl.pallas_call(
        body, out_shape=jax.ShapeDtypeStruct((lp, d), BF16), grid=(lp // tm,), in_specs=[row, vec],
        out_specs=row, compiler_params=_cp("parallel"), name=name)(h, w_next)


def _resid_norm_bwd(dh_out, dhn, h_new, t, w_next, w_post, *, name):
    lp, d = h_new.shape if h_new is not None else t.shape
    tm = _row_tile(lp)
    has_n = dhn is not None
    has_t = t is not None

    def body(*refs):
        refs = list(refs)
        dho_ref = refs.pop(0)
        if has_n:
            dhn_ref, hn_ref, wn_ref = refs.pop(0), refs.pop(0), refs.pop(0)
        if has_t:
            t_ref, wp_ref = refs.pop(0), refs.pop(0)
        dh_ref = refs.pop(0) if has_n else None
        dt_ref = refs.pop(0) if has_t else None
        dwn_ref = refs.pop(0) if has_n else None
        dwp_ref = refs.pop(0) if has_t else None
        first = pl.program_id(0) == 0
        dh = dho_ref[...]
        if has_n:
            dx, dwn = _rms_bwd(hn_ref[...], wn_ref[...], dhn_ref[...])
            dh = dh + dx
            dh_ref[...] = dh

            @pl.when(first)
            def _():
                dwn_ref[...] = jnp.zeros_like(dwn_ref)
            dwn_ref[...] += dwn
        if has_t:
            dt, dwp = _rms_bwd(t_ref[...], wp_ref[...], dh)
            dt_ref[...] = dt.astype(BF16)

            @pl.when(first)
            def _():
                dwp_ref[...] = jnp.zeros_like(dwp_ref)
            dwp_ref[...] += dwp

    row = pl.BlockSpec((tm, d), lambda i: (i, 0))
    vec = pl.BlockSpec((1, d), lambda i: (0, 0))
    args, in_specs, out_shape, out_specs = [dh_out], [row], [], []
    if has_n:
        args += [dhn, h_new, w_next]
        in_specs += [row, row, vec]
    if has_t:
        args += [t, w_post]
        in_specs += [row, vec]
    if has_n:
        out_shape.append(jax.ShapeDtypeStruct((lp, d), F32)); out_specs.append(row)
    if has_t:
        out_shape.append(jax.ShapeDtypeStruct((lp, d), BF16)); out_specs.append(row)
    if has_n:
        out_shape.append(jax.ShapeDtypeStruct((1, d), F32)); out_specs.append(vec)
    if has_t:
        out_shape.append(jax.ShapeDtypeStruct((1, d), F32)); out_specs.append(vec)
    outs = list(pl.pallas_call(body, out_shape=tuple(out_shape), grid=(lp // tm,), in_specs=in_specs,
                               out_specs=tuple(out_specs), compiler_params=_cp("arbitrary"), name=name)(*args))
    dh = outs.pop(0) if has_n else dh_out
    dt = outs.pop(0) if has_t else None
    dwn = outs.pop(0) if has_n else None
    dwp = outs.pop(0) if has_t else None
    return dh, dt, dwn, dwp


def _loss_head(h, f, w_post, target, *, name):
    lp, d = h.shape
    tm = _row_tile(lp)

    def body(h_ref, f_ref, w_ref, t_ref, loss_ref, dy_ref):
        i = pl.program_id(0)
        y = h_ref[...] + _rms(f_ref[...], w_ref[...])
        rows = i * tm + lax.broadcasted_iota(jnp.int32, (tm, 1), 0)
        tok = (rows >= FRONT) & (rows < lp - BACK)
        err = jnp.where(tok, y - t_ref[...], 0.0)
        dy_ref[...] = err * (1.0 / d)

        @pl.when(i == 0)
        def _():
            loss_ref[...] = jnp.zeros_like(loss_ref)
        part = jnp.sum(jnp.sum(err * err, axis=1, keepdims=True), axis=0, keepdims=True) * (0.5 / d)
        loss_ref[...] += jnp.broadcast_to(part, loss_ref.shape)

    row = pl.BlockSpec((tm, d), lambda i: (i, 0))
    loss, dy = pl.pallas_call(
        body, out_shape=(jax.ShapeDtypeStruct((8, 128), F32), jax.ShapeDtypeStruct((lp, d), F32)),
        grid=(lp // tm,), in_specs=[row, row, pl.BlockSpec((1, d), lambda i: (0, 0)), row],
        out_specs=(pl.BlockSpec((8, 128), lambda i: (0, 0)), row),
        compiler_params=_cp("arbitrary"), name=name)(h, f, w_post, target)
    return loss[0, 0], dy


_GELU_C = math.sqrt(2.0 / math.pi)


def _gelu_and_grad(a):
    a2 = a * a
    t = jnp.tanh(a * (_GELU_C + (_GELU_C * 0.044715) * a2))
    ha = 0.5 * a
    h1 = 0.5 + 0.5 * t
    return a * h1, h1 + ha * (1.0 - t * t) * (_GELU_C + (3.0 * _GELU_C * 0.044715) * a2)


def _gelu(a):
    t = jnp.tanh(a * (_GELU_C + (_GELU_C * 0.044715) * (a * a)))
    return a * (0.5 + 0.5 * t)


def _conv3(parts, n, w, b):
    xx = jnp.concatenate(parts, axis=0)
    return b + xx[8:8 + n] * w[2:3] + pltpu.roll(xx, 1, 0)[8:8 + n] * w[1:2] + pltpu.roll(xx, 2, 0)[8:8 + n] * w[0:1]


def _conv_act(ua, ug, wa, wg, ba, bg, *, name):
    lp, n = ua.shape
    tm = _row_tile(lp)
    tc = _tile(n, (256, 128))
    nb8 = tm // 8

    def body(ua_ref, uap_ref, ug_ref, ugp_ref, wa_ref, wg_ref, ba_ref, bg_ref, o_ref):
        i = pl.program_id(0)
        ca = _conv3([uap_ref[...], ua_ref[...]], tm, wa_ref[...], ba_ref[...])
        cg = _conv3([ugp_ref[...], ug_ref[...]], tm, wg_ref[...], bg_ref[...])
        rows = i * tm + lax.broadcasted_iota(jnp.int32, (tm, 1), 0)
        ok = (rows >= PADF) & (rows < lp - BACK)
        o_ref[...] = jnp.where(ok, _gelu(ca) * cg, 0.0).astype(BF16)

    cur = pl.BlockSpec((tm, tc), lambda i, j: (i, j))
    prev = pl.BlockSpec((8, tc), lambda i, j: (jnp.maximum(i * nb8 - 1, 0), j))
    w3 = pl.BlockSpec((3, tc), lambda i, j: (0, j))
    b1 = pl.BlockSpec((1, tc), lambda i, j: (0, j))
    return pl.pallas_call(
        body, out_shape=jax.ShapeDtypeStruct((lp, n), BF16), grid=(lp // tm, n // tc),
        in_specs=[cur, prev, cur, prev, w3, w3, b1, b1], out_specs=cur,
        compiler_params=_cp("parallel", "parallel"), name=name)(ua, ua, ug, ug, wa, wg, ba, bg)


def _conv_act_bwd(ua, ug, dact, wa, wg, ba, bg, *, name):
    lp, n = ua.shape
    tm = _row_tile(lp)
    tc = _tile(n, (256, 128))
    nb8 = tm // 8
    last8 = lp // 8 - 1
    ext = tm + 8

    def body(ua_ref, uap_ref, uan_ref, ug_ref, ugp_ref, ugn_ref, da_ref, dan_ref, wa_ref, wg_ref, ba_ref, bg_ref,
             dua_ref, dug_ref, dwa_ref, dwg_ref, dba_ref, dbg_ref):
        i = pl.program_id(1)
        rows = i * tm + lax.broadcasted_iota(jnp.int32, (ext, 1), 0)
        ok = (rows >= PADF) & (rows < lp - BACK)
        dact_e = jnp.where(ok, jnp.concatenate([da_ref[...], dan_ref[...]], axis=0), 0.0)
        ca = _conv3([uap_ref[...], ua_ref[...], uan_ref[...]], ext, wa_ref[...], ba_ref[...])
        cg = _conv3([ugp_ref[...], ug_ref[...], ugn_ref[...]], ext, wg_ref[...], bg_ref[...])
        gel, gel_d = _gelu_and_grad(ca)
        dca = dact_e * cg * gel_d
        dcg = dact_e * gel

        @pl.when(i == 0)
        def _():
            dwa_ref[...] = jnp.zeros_like(dwa_ref)
            dwg_ref[...] = jnp.zeros_like(dwg_ref)
            dba_ref[...] = jnp.zeros_like(dba_ref)
            dbg_ref[...] = jnp.zeros_like(dbg_ref)

        def back(dc, w, x, du_ref, dw_ref, db_ref):
            d0, d1, d2 = dc[:tm], pltpu.roll(dc, ext - 1, 0)[:tm], pltpu.roll(dc, ext - 2, 0)[:tm]
            du_ref[...] = (d0 * w[2:3] + d1 * w[1:2] + d2 * w[0:1]).astype(BF16)
            s = lambda v: jnp.sum(v, axis=0, keepdims=True)
            dw_ref[0:1, :] += s(d2 * x)
            dw_ref[1:2, :] += s(d1 * x)
            dw_ref[2:3, :] += s(d0 * x)
            db_ref[...] += s(d0)

        back(dca, wa_ref[...], ua_ref[...], dua_ref, dwa_ref, dba_ref)
        back(dcg, wg_ref[...], ug_ref[...], dug_ref, dwg_ref, dbg_ref)

    cur = pl.BlockSpec((tm, tc), lambda j, i: (i, j))
    prev = pl.BlockSpec((8, tc), lambda j, i: (jnp.maximum(i * nb8 - 1, 0), j))
    nxt = pl.BlockSpec((8, tc), lambda j, i: (jnp.minimum((i + 1) * nb8, last8), j))
    w3 = pl.BlockSpec((3, tc), lambda j, i: (0, j))
    b1 = pl.BlockSpec((1, tc), lambda j, i: (0, j))
    return pl.pallas_call(
        body,
        out_shape=(jax.ShapeDtypeStruct((lp, n), BF16), jax.ShapeDtypeStruct((lp, n), BF16),
                   jax.ShapeDtypeStruct((3, n), F32), jax.ShapeDtypeStruct((3, n), F32),
                   jax.ShapeDtypeStruct((1, n), F32), jax.ShapeDtypeStruct((1, n), F32)),
        grid=(n // tc, lp // tm),
        in_specs=[cur, prev, nxt, cur, prev, nxt, cur, nxt, w3, w3, b1, b1],
        out_specs=(cur, cur, w3, w3, b1, b1),
        compiler_params=_cp("parallel", "arbitrary"), name=name)(ua, ua, ua, ug, ug, ug, dact, dact, wa, wg, ba, bg)


def _sigmoid(x):
    return 1.0 / (1.0 + jnp.exp(-x))


def _merge(o_ret, o_gla, proj, w_ret, w_gla, *, name):
    lp = o_ret.shape[0]
    tm = _row_tile(lp)

    def body(or_ref, og_ref, rg_ref, gr_ref, wr_ref, wg_ref, m_ref):
        oret, ogla = or_ref[...], og_ref[...]
        yr, yg = [], []
        for h in range(4):
            hs = slice(128 * h, 128 * h + 128)
            o = oret[:, hs]
            xc = o - jnp.mean(o, axis=-1, keepdims=True)
            yr.append(xc * lax.rsqrt(jnp.mean(xc * xc, axis=-1, keepdims=True) + EPS))
            o = ogla[:, hs]
            yg.append(o * lax.rsqrt(jnp.mean(o * o, axis=-1, keepdims=True) + EPS))
        rg, gr = rg_ref[...], gr_ref[...]
        m_ref[:, 0:512] = (jnp.concatenate(yr, axis=1) * wr_ref[...] * (rg * _sigmoid(rg))).astype(BF16)
        m_ref[:, 512:1024] = (jnp.concatenate(yg, axis=1) * wg_ref[...] * (gr * _sigmoid(gr))).astype(BF16)

    row = pl.BlockSpec((tm, 512), lambda i: (i, 0))
    vec = pl.BlockSpec((1, 512), lambda i: (0, 0))
    return pl.pallas_call(
        body, out_shape=jax.ShapeDtypeStruct((lp, 1024), BF16), grid=(lp // tm,),
        in_specs=[row, row, pl.BlockSpec((tm, 512), lambda i: (i, C_RG // 512)),
                  pl.BlockSpec((tm, 512), lambda i: (i, C_GR // 512)), vec, vec],
        out_specs=pl.BlockSpec((tm, 1024), lambda i: (i, 0)),
        compiler_params=_cp("parallel"), name=name)(o_ret, o_gla, proj, proj, w_ret, w_gla)


def _merge_bwd(dm, o_ret, o_gla, proj, w_ret, w_gla, *, name):
    lp = o_ret.shape[0]
    tm = _row_tile(lp)

    def body(dm_ref, or_ref, og_ref, rg_ref, gr_ref, wr_ref, wg_ref, dor_ref, dog_ref, drg_ref, dgr_ref, dwr_ref, dwg_ref):
        @pl.when(pl.program_id(0) == 0)
        def _():
            dwr_ref[...] = jnp.zeros_like(dwr_ref)
            dwg_ref[...] = jnp.zeros_like(dwg_ref)

        def group(d, o_all, gate, w, center):
            sg = _sigmoid(gate)
            s = gate * sg
            ds = sg * (1.0 + gate * (1.0 - sg))
            xh, rr = [], []
            for h in range(4):
                o = o_all[:, 128 * h:128 * h + 128]
                if center:
                    o = o - jnp.mean(o, axis=-1, keepdims=True)
                r = lax.rsqrt(jnp.mean(o * o, axis=-1, keepdims=True) + EPS)
                xh.append(o * r)
                rr.append(r)
            xh_all = jnp.concatenate(xh, axis=1)
            dgate = d * xh_all * w * ds
            dw = jnp.sum(d * xh_all * s, axis=0, keepdims=True)
            dxh_all = d * w * s
            do = []
            for h in range(4):
                dxh = dxh_all[:, 128 * h:128 * h + 128]
                t = dxh - xh[h] * jnp.mean(dxh * xh[h], axis=-1, keepdims=True)
                if center:
                    t = t - jnp.mean(dxh, axis=-1, keepdims=True)
                do.append(rr[h] * t)
            return jnp.concatenate(do, axis=1), dgate, dw

        dmv = dm_ref[...]
        do, dg, dw = group(dmv[:, 0:512], or_ref[...], rg_ref[...], wr_ref[...], True)
        dor_ref[...] = do
        drg_ref[...] = dg.astype(BF16)
        dwr_ref[...] += dw
        do, dg, dw = group(dmv[:, 512:1024], og_ref[...], gr_ref[...], wg_ref[...], False)
        dog_ref[...] = do
        dgr_ref[...] = dg.astype(BF16)
        dwg_ref[...] += dw

    row = pl.BlockSpec((tm, 512), lambda i: (i, 0))
    vec = pl.BlockSpec((1, 512), lambda i: (0, 0))
    return pl.pallas_call(
        body,
        out_shape=(jax.ShapeDtypeStruct((lp, 512), F32), jax.ShapeDtypeStruct((lp, 512), F32),
                   jax.ShapeDtypeStruct((lp, 512), BF16), jax.ShapeDtypeStruct((lp, 512), BF16),
                   jax.ShapeDtypeStruct((1, 512), F32), jax.ShapeDtypeStruct((1, 512), F32)),
        grid=(lp // tm,),
        in_specs=[pl.BlockSpec((tm, 1024), lambda i: (i, 0)), row, row,
                  pl.BlockSpec((tm, 512), lambda i: (i, C_RG // 512)),
                  pl.BlockSpec((tm, 512), lambda i: (i, C_GR // 512)), vec, vec],
        out_specs=(row, row, row, row, vec, vec),
        compiler_params=_cp("arbitrary"), name=name)(dm, o_ret, o_gla, proj, proj, w_ret, w_gla)


def _dot(a, b):
    return lax.dot_general(a, b, (((1,), (0,)), ((), ())), preferred_element_type=F32)


def _dot_nt(a, b):
    return lax.dot_general(a, b, (((1,), (1,)), ((), ())), preferred_element_type=F32)


def _dot_tn(a, b):
    return lax.dot_general(a, b, (((0,), (0,)), ((), ())), preferred_element_type=F32)


def _ret_tables(lp):
    cr = RET_CHUNK
    pos = jnp.arange(lp, dtype=F32) - float(PADF)
    half = RET_DK // 2
    inv = ROPE_BASE ** (-jnp.arange(half, dtype=F32) / half)
    ang = pos[:, None] * inv[None, :]
    c, s = jnp.cos(ang), jnp.sin(ang)
    rope_c = jnp.concatenate([c, c], axis=1)
    rope_s = jnp.concatenate([-s, s], axis=1)
    log_g = np.log(1.0 - 2.0 ** (-5.0 - np.arange(RET_HEADS, dtype=np.float64)))
    idx = np.arange(cr, dtype=np.float64)
    diff = idx[:, None] - idx[None, :]
    dmat = np.where(diff >= 0, np.exp(log_g[:, None, None] * np.maximum(diff, 0.0)), 0.0)
    zeta = np.exp(log_g[:, None] * (cr - 1.0 - idx)[None, :])
    xi = np.exp(log_g[:, None] * (idx + 1.0)[None, :])
    gc = np.exp(log_g * cr)
    f = lambda a: jnp.asarray(a.astype(np.float32))
    return (rope_c, rope_s, f(dmat), f(np.broadcast_to(zeta[:, :, None], (RET_HEADS, cr, 128))),
            f(np.broadcast_to(xi[:, :, None], (RET_HEADS, cr, 128))),
            f(np.broadcast_to(gc[:, None, None], (RET_HEADS, 8, 128))))


def _rope(t, c, s):
    return t * c + pltpu.roll(t, 64, 1) * s


def _rope_t(d, c, s):
    return d * c + pltpu.roll(d * s, 64, 1)


def _ret_specs(nblk, rev):
    ix = (lambda i: nblk - 1 - i) if rev else (lambda i: i)
    cr = RET_CHUNK
    col = lambda base: pl.BlockSpec((BLK, 512), lambda i: (ix(i), base // 512))
    tab = pl.BlockSpec((BLK, 128), lambda i: (ix(i), 0))
    sq = pl.BlockSpec((RET_HEADS, cr, cr), lambda i: (0, 0, 0))
    hv = pl.BlockSpec((RET_HEADS, cr, 128), lambda i: (0, 0, 0))
    g8 = pl.BlockSpec((RET_HEADS, 8, 128), lambda i: (0, 0, 0))
    st = pl.BlockSpec((RET_HEADS, BLK // cr, 128, 128), lambda i: (0, ix(i), 0, 0))
    out = pl.BlockSpec((BLK, 512), lambda i: (ix(i), 0))
    return col, tab, sq, hv, g8, st, out


def _retention(proj, tables, *, name):
    lp = proj.shape[0]
    nblk, cr = lp // BLK, RET_CHUNK
    scale = RET_DK ** -0.5

    def body(q_ref, k_ref, v_ref, c_ref, s_ref, d_ref, z_ref, x_ref, g_ref, o_ref, st_ref, state):
        @pl.when(pl.program_id(0) == 0)
        def _():
            state[...] = jnp.zeros_like(state)

        def chunk(ci, carry):
            sl = pl.ds(pl.multiple_of(ci * cr, cr), cr)
            c, s = c_ref[sl, :], s_ref[sl, :]
            for h in range(RET_HEADS):
                hs = slice(128 * h, 128 * h + 128)
                q = _rope(q_ref[sl, hs], c, s)
                k = _rope(k_ref[sl, hs], c, s) * scale
                qb, kb, vb = q.astype(BF16), k.astype(BF16), v_ref[sl, hs].astype(BF16)
                st = state[h]
                st_ref[h, ci] = st
                sc = _dot_nt(qb, kb) * d_ref[h]
                o_ref[sl, hs] = _dot(sc.astype(BF16), vb) + _dot(qb, st.astype(BF16)) * x_ref[h]
                state[h] = st * g_ref[h][0:1, :] + _dot_tn((k * z_ref[h]).astype(BF16), vb)
            return carry

        lax.fori_loop(0, BLK // cr, chunk, 0)

    col, tab, sq, hv, g8, st, out = _ret_specs(nblk, False)
    return pl.pallas_call(
        body,
        out_shape=(jax.ShapeDtypeStruct((lp, 512), F32), jax.ShapeDtypeStruct((4, lp // cr, 128, 128), F32)),
        grid=(nblk,), in_specs=[col(C_RQ), col(C_RK), col(C_RV), tab, tab, sq, hv, hv, g8],
        out_specs=(out, st), scratch_shapes=[pltpu.VMEM((RET_HEADS, 128, 128), F32)],
        compiler_params=_cp("arbitrary"), name=name)(proj, proj, proj, *tables)


def _retention_bwd(proj, do, states, tables, *, name):
    lp = proj.shape[0]
    nblk, cr = lp // BLK, RET_CHUNK
    nch = BLK // cr
    scale = RET_DK ** -0.5

    def body(q_ref, k_ref, v_ref, do_ref, st_ref, c_ref, s_ref, d_ref, z_ref, x_ref, g_ref, dq_ref, dk_ref, dv_ref, dstate):
        @pl.when(pl.program_id(0) == 0)
        def _():
            dstate[...] = jnp.zeros_like(dstate)

        def chunk(cc, carry):
            ci = nch - 1 - cc
            sl = pl.ds(pl.multiple_of(ci * cr, cr), cr)
            c, s = c_ref[sl, :], s_ref[sl, :]
            for h in range(RET_HEADS):
                hs = slice(128 * h, 128 * h + 128)
                dmat, zeta, xi = d_ref[h], z_ref[h], x_ref[h]
                q = _rope(q_ref[sl, hs], c, s)
                k = _rope(k_ref[sl, hs], c, s) * scale
                qb, kb, vb = q.astype(BF16), k.astype(BF16), v_ref[sl, hs].astype(BF16)
                kzb = (k * zeta).astype(BF16)
                dov = do_ref[sl, hs]
                dob, doxb = dov.astype(BF16), (dov * xi).astype(BF16)
                stb = st_ref[h, ci].astype(BF16)
                dsn = dstate[h]
                dsnb = dsn.astype(BF16)
                scb = (_dot_nt(qb, kb) * dmat).astype(BF16)
                dscb = (_dot_nt(dob, vb) * dmat).astype(BF16)
                dq = _dot(dscb, kb) + _dot_nt(doxb, stb)
                dk = _dot_tn(dscb, qb) + _dot_nt(vb, dsnb) * zeta
                dv = _dot_tn(scb, dob) + _dot(kzb, dsnb)
                dstate[h] = dsn * g_ref[h][0:1, :] + _dot_tn(qb, doxb)
                dq_ref[sl, hs] = _rope_t(dq, c, s).astype(BF16)
                dk_ref[sl, hs] = _rope_t(dk * scale, c, s).astype(BF16)
                dv_ref[sl, hs] = dv.astype(BF16)
            return carry

        lax.fori_loop(0, nch, chunk, 0)

    col, tab, sq, hv, g8, st, out = _ret_specs(nblk, True)
    o3 = jax.ShapeDtypeStruct((lp, 512), BF16)
    return pl.pallas_call(
        body, out_shape=(o3, o3, o3), grid=(nblk,),
        in_specs=[col(C_RQ), col(C_RK), col(C_RV), out, st, tab, tab, sq, hv, hv, g8],
        out_specs=(out, out, out), scratch_shapes=[pltpu.VMEM((RET_HEADS, 128, 128), F32)],
        compiler_params=_cp("arbitrary"), name=name)(proj, proj, proj, do, states, *tables)


def _gla_tables():
    c = GLA_CHUNK
    tri = np.tril(np.ones((c, c), np.float32))
    ones_qv = np.kron(np.eye(GLA_HEADS, dtype=np.float32), np.ones((GLA_DK, GLA_DV), np.float32))
    return (jnp.asarray(tri, BF16), jnp.asarray(tri.T.copy(), BF16), jnp.asarray(ones_qv, BF16),
            jnp.asarray(ones_qv.T.copy(), BF16))


def _split3(x):
    hi = x.astype(BF16)
    r1 = x - hi.astype(F32)
    mid = r1.astype(BF16)
    lo = (r1 - mid.astype(F32)).astype(BF16)
    return hi, mid, lo


def _tri_sum(tri, x):
    hi, mid, lo = _split3(x)
    return _dot(tri, hi) + _dot(tri, mid) + _dot(tri, lo)


def _head_masks(width, per):
    lane = lax.broadcasted_iota(jnp.int32, (1, width), 1)
    return [((lane >= per * h) & (lane < per * (h + 1))).astype(F32) for h in range(GLA_HEADS)]


def _stack_heads(x, masks):
    return jnp.concatenate([x * m for m in masks], axis=0)


def _gla_gate(ga, w2, b, ok, tri):
    z = _dot(ga.astype(BF16), w2) + b
    la = (jnp.minimum(z, 0.0) - jnp.log(1.0 + jnp.exp(-jnp.abs(z)))) * (1.0 / GLA_TAU)
    la = jnp.where(ok, la, 0.0)
    return z, _tri_sum(tri, la)


def _gla_rows(i_blk, ci, lp):
    c = GLA_CHUNK
    rows = i_blk * BLK + ci * c + lax.broadcasted_iota(jnp.int32, (c, 1), 0)
    return (rows >= PADF) & (rows < lp - BACK)


def _gla_off_parts(a, qs, k, g, hm_q):
    s = GLA_SUB
    ra = g[s * a - 1:s * a, :]
    ga_ = g[s * a:s * a + s, :]
    eq = jnp.exp(ga_ - ra)
    ek = jnp.exp(jnp.minimum(ra - g, 0.0))
    qh = qs[s * a:s * a + s, :] * eq
    kh = k * ek
    qst = _stack_heads(qh, hm_q).astype(BF16)
    col = lax.broadcasted_iota(jnp.int32, (GLA_HEADS * s, GLA_CHUNK), 1)
    pmask = col < s * a
    p = jnp.where(pmask, _dot_nt(qst, kh.astype(BF16)), 0.0)
    return eq, ek, qh, kh, qst, pmask, p


def _lag_mask(j):
    r = lax.broadcasted_iota(jnp.int32, (GLA_CHUNK, 1), 0)
    return (jnp.bitwise_and(r, GLA_SUB - 1) >= j).astype(F32)


def _roll_rows(x, j):
    return x if j == 0 else pltpu.roll(x, j, 0)


def _gla(proj, w2p, b, tables, *, name):
    lp = proj.shape[0]
    nblk, c, s = lp // BLK, GLA_CHUNK, GLA_SUB
    nch = BLK // c
    na = c // s

    def body(q_ref, k_ref, v_ref, a_ref, w_ref, b_ref, tri_ref, ones_ref, o_ref, st_ref, state):
        i_blk = pl.program_id(0)

        @pl.when(i_blk == 0)
        def _():
            state[...] = jnp.zeros_like(state)
        hm_q = _head_masks(GLA_QK, GLA_DK)
        tri, ones_qv, w2, bias = tri_ref[...], ones_ref[...], w_ref[...], b_ref[...]

        def chunk(ci, carry):
            sl = pl.ds(pl.multiple_of(ci * c, c), c)
            ok = _gla_rows(i_blk, ci, lp)
            k, v = k_ref[sl, :], v_ref[sl, :]
            vb = v.astype(BF16)
            qs = q_ref[sl, :] * (GLA_DK ** -0.5)
            _, g = _gla_gate(a_ref[sl, :], w2, bias, ok, tri)
            last = g[c - 1:c, :]
            st = state[...]
            st_ref[ci] = st
            qst = _stack_heads(qs * jnp.exp(g), hm_q).astype(BF16)
            oi = _dot_nt(qst, st.astype(BF16))
            o = jnp.concatenate([oi[c * h:c * h + c, :] for h in range(GLA_HEADS)], axis=1)
            ke = k * jnp.exp(last - g)
            f = _dot_tn(vb, ke.astype(BF16))
            upd = f[0:GLA_DV, :] * hm_q[0]
            for h in range(1, GLA_HEADS):
                upd = upd + f[GLA_DV * h:GLA_DV * (h + 1), :] * hm_q[h]
            state[...] = st * jnp.exp(last) + upd
            off = [jnp.zeros((s, GLA_V), F32)]
            for a in range(1, na):
                p = _gla_off_parts(a, qs, k, g, hm_q)[-1]
                ob = _dot(p.astype(BF16), vb)
                off.append(jnp.concatenate(
                    [ob[s * h:s * h + s, GLA_DV * h:GLA_DV * (h + 1)] for h in range(GLA_HEADS)], axis=1))
            o = o + jnp.concatenate(off, axis=0)
            ws = []
            for j in range(s):
                ej = jnp.exp(jnp.minimum(g - _roll_rows(g, j), 0.0))
                ws.append((qs * _roll_rows(k, j) * ej * _lag_mask(j)).astype(BF16))
            ball = _dot(jnp.concatenate(ws, axis=0), ones_qv)
            for j in range(s):
                o = o + ball[c * j:c * j + c, :] * _roll_rows(v, j)
            o_ref[sl, :] = o
            return carry

        lax.fori_loop(0, nch, chunk, 0)

    tri, _, ones_qv, _ = tables
    full = lambda arr: pl.BlockSpec(arr.shape, lambda i: (0,) * arr.ndim)
    return pl.pallas_call(
        body,
        out_shape=(jax.ShapeDtypeStruct((lp, GLA_V), F32), jax.ShapeDtypeStruct((lp // c, GLA_DV, GLA_QK), F32)),
        grid=(nblk,),
        in_specs=[pl.BlockSpec((BLK, GLA_QK), lambda i: (i, C_GQ // GLA_QK)),
                  pl.BlockSpec((BLK, GLA_QK), lambda i: (i, C_GK // GLA_QK)),
                  pl.BlockSpec((BLK, GLA_V), lambda i: (i, C_GV // GLA_V)),
                  pl.BlockSpec((BLK, 128), lambda i: (i, C_GA // 128)),
                  full(w2p), full(b), full(tri), full(ones_qv)],
        out_specs=(pl.BlockSpec((BLK, GLA_V), lambda i: (i, 0)),
                   pl.BlockSpec((nch, GLA_DV, GLA_QK), lambda i: (i, 0, 0))),
        scratch_shapes=[pltpu.VMEM((GLA_DV, GLA_QK), F32)],
        compiler_params=_cp("arbitrary"), name=name)(proj, proj, proj, proj, w2p, b, tri, ones_qv)


def _gla_bwd(proj, do, states, w2p, b, tables, *, name):
    lp = proj.shape[0]
    nblk, c, s = lp // BLK, GLA_CHUNK, GLA_SUB
    nch = BLK // c
    na = c // s

    def body(q_ref, k_ref, v_ref, a_ref, do_ref, st_ref, w_ref, b_ref, tri_ref, trit_ref, ones_ref, onest_ref,
             dq_ref, dk_ref, dv_ref, da_ref, dw_ref, db_ref, dstate, dqs_s, dk_s, dg_s, dv_s):
        i_blk = nblk - 1 - pl.program_id(0)

        @pl.when(pl.program_id(0) == 0)
        def _():
            dstate[...] = jnp.zeros_like(dstate)
            dw_ref[...] = jnp.zeros_like(dw_ref)
            db_ref[...] = jnp.zeros_like(db_ref)
        hm_q = _head_masks(GLA_QK, GLA_DK)
        hm_v = _head_masks(GLA_V, GLA_DV)
        tri, trit, ones_qv, ones_vq = tri_ref[...], trit_ref[...], ones_ref[...], onest_ref[...]
        w2, bias = w_ref[...], b_ref[...]
        rsum = lambda x: jnp.sum(x, axis=0, keepdims=True)

        def chunk(cc, carry):
            ci = nch - 1 - cc
            sl = pl.ds(pl.multiple_of(ci * c, c), c)
            ok = _gla_rows(i_blk, ci, lp)
            k, v, ga = k_ref[sl, :], v_ref[sl, :], a_ref[sl, :]
            vb = v.astype(BF16)
            qs = q_ref[sl, :] * (GLA_DK ** -0.5)
            z, g = _gla_gate(ga, w2, bias, ok, tri)
            last = g[c - 1:c, :]
            elast = jnp.exp(last)
            eg = jnp.exp(g)
            ekl = jnp.exp(last - g)
            qe, ke = qs * eg, k * ekl
            dov = do_ref[sl, :]
            st = st_ref[ci]
            dsn = dstate[...]
            qst = _stack_heads(qe, hm_q).astype(BF16)
            dost = jnp.concatenate([dov[:, GLA_DV * h:GLA_DV * (h + 1)] for h in range(GLA_HEADS)], axis=0).astype(BF16)
            dqe_st = _dot(dost, st.astype(BF16))
            dqe = dqe_st[0:c, :] * hm_q[0]
            for h in range(1, GLA_HEADS):
                dqe = dqe + dqe_st[c * h:c * h + c, :] * hm_q[h]
            dstate[...] = _dot_tn(dost, qst) + dsn * elast
            dlast = rsum(dsn * st) * elast
            df = _stack_heads(dsn, hm_q).astype(BF16)
            dv_s[...] = _dot_nt(ke.astype(BF16), df)
            dke = _dot(vb, df)
            xk = dke * ke
            dqs_s[...] = dqe * eg
            dk_s[...] = dke * ekl
            dg_s[...] = dqe * qe - xk
            dlast = dlast + rsum(xk)
            for a in range(1, na):
                eq, ek, qh, kh, qsa, pmask, p = _gla_off_parts(a, qs, k, g, hm_q)
                rows = slice(s * a, s * a + s)
                dofull = _stack_heads(dov[rows, :], hm_v).astype(BF16)
                dp = jnp.where(pmask, _dot_nt(dofull, vb), 0.0).astype(BF16)
                dv_s[...] += _dot_tn(p.astype(BF16), dofull)
                dq_st = _dot(dp, kh.astype(BF16))
                dqh = dq_st[0:s, :] * hm_q[0]
                for h in range(1, GLA_HEADS):
                    dqh = dqh + dq_st[s * h:s * h + s, :] * hm_q[h]
                dkh = _dot_tn(dp, qsa)
                xq = dqh * qh
                xkh = dkh * kh
                dqs_s[rows, :] += dqh * eq
                dg_s[rows, :] += xq
                dk_s[...] += dkh * ek
                dg_s[...] -= xkh
                dg_s[s * a - 1:s * a, :] += rsum(xkh) - rsum(xq)
            kes, qes, ws, dbs = [], [], [], []
            for j in range(s):
                em = jnp.exp(jnp.minimum(g - _roll_rows(g, j), 0.0)) * _lag_mask(j)
                kes.append(_roll_rows(k, j) * em)
                qes.append(qs * em)
                ws.append((qs * kes[j]).astype(BF16))
                dbs.append((dov * _roll_rows(v, j)).astype(BF16))
            ball = _dot(jnp.concatenate(ws, axis=0), ones_qv)
            dwall = _dot(jnp.concatenate(dbs, axis=0), ones_vq)
            for j in range(s):
                back = (lambda x: x) if j == 0 else (lambda x, j=j: pltpu.roll(x, c - j, 0))
                dw = dwall[c * j:c * j + c, :]
                dv_s[...] += back(ball[c * j:c * j + c, :] * dov)
                dqs_s[...] += dw * kes[j]
                dk_s[...] += back(dw * qes[j])
                x = dw * qs * kes[j]
                dg_s[...] += x - back(x)
            dg_s[c - 1:c, :] += dlast
            dla = jnp.where(ok, _tri_sum(trit, dg_s[...]), 0.0)
            dz = dla * (1.0 / GLA_TAU) / (1.0 + jnp.exp(z))
            dzb = dz.astype(BF16)
            dq_ref[sl, :] = (dqs_s[...] * (GLA_DK ** -0.5)).astype(BF16)
            dk_ref[sl, :] = dk_s[...].astype(BF16)
            dv_ref[sl, :] = dv_s[...].astype(BF16)
            da_ref[sl, :] = _dot_nt(dzb, w2).astype(BF16)
            dw_ref[...] += _dot_tn(ga.astype(BF16), dzb)
            db_ref[...] += rsum(dz)
            return carry

        lax.fori_loop(0, nch, chunk, 0)

    tri, trit, ones_qv, ones_vq = tables
    full = lambda arr: pl.BlockSpec(arr.shape, lambda i: (0,) * arr.ndim)
    rev = lambda i: nblk - 1 - i
    qk = jax.ShapeDtypeStruct((lp, GLA_QK), BF16)
    return pl.pallas_call(
        body,
        out_shape=(qk, qk, jax.ShapeDtypeStruct((lp, GLA_V), BF16), jax.ShapeDtypeStruct((lp, 128), BF16),
                   jax.ShapeDtypeStruct((128, GLA_QK), F32), jax.ShapeDtypeStruct((1, GLA_QK), F32)),
        grid=(nblk,),
        in_specs=[pl.BlockSpec((BLK, GLA_QK), lambda i: (rev(i), C_GQ // GLA_QK)),
                  pl.BlockSpec((BLK, GLA_QK), lambda i: (rev(i), C_GK // GLA_QK)),
                  pl.BlockSpec((BLK, GLA_V), lambda i: (rev(i), C_GV // GLA_V)),
                  pl.BlockSpec((BLK, 128), lambda i: (rev(i), C_GA // 128)),
                  pl.BlockSpec((BLK, GLA_V), lambda i: (rev(i), 0)),
                  pl.BlockSpec((nch, GLA_DV, GLA_QK), lambda i: (rev(i), 0, 0)),
                  full(w2p), full(b), full(tri), full(trit), full(ones_qv), full(ones_vq)],
        out_specs=(pl.BlockSpec((BLK, GLA_QK), lambda i: (rev(i), 0)),
                   pl.BlockSpec((BLK, GLA_QK), lambda i: (rev(i), 0)),
                   pl.BlockSpec((BLK, GLA_V), lambda i: (rev(i), 0)),
                   pl.BlockSpec((BLK, 128), lambda i: (rev(i), 0)),
                   pl.BlockSpec((128, GLA_QK), lambda i: (0, 0)),
                   pl.BlockSpec((1, GLA_QK), lambda i: (0, 0))),
        scratch_shapes=[pltpu.VMEM((GLA_DV, GLA_QK), F32), pltpu.VMEM((c, GLA_QK), F32),
                        pltpu.VMEM((c, GLA_QK), F32), pltpu.VMEM((c, GLA_QK), F32), pltpu.VMEM((c, GLA_V), F32)],
        compiler_params=_cp("arbitrary"), name=name)(proj, proj, proj, proj, do, states, w2p, b, tri, trit, ones_qv, ones_vq)


def _as2d(a):
    return a.reshape(-1, a.shape[-1])


def _ew_tile(r):
    return _tile(r, (512, 256, 128, 64, 32, 16, 8))


def _add2(a, b, *, out_dtype, name):
    a2, b2 = _as2d(a), _as2d(b)
    r, n = a2.shape
    tm = _ew_tile(r)

    def body(a_ref, b_ref, o_ref):
        o_ref[...] = (a_ref[...] + b_ref[...]).astype(o_ref.dtype)

    blk = pl.BlockSpec((tm, n), lambda i: (i, 0))
    return pl.pallas_call(body, out_shape=jax.ShapeDtypeStruct((r, n), out_dtype), grid=(r // tm,), in_specs=[blk, blk],
                          out_specs=blk, compiler_params=_cp("parallel"), name=name)(a2, b2).reshape(a.shape)


def _sum_slots(own, q, *, name):
    shape = own.shape
    q3 = q.reshape(3, -1, shape[-1])
    own2 = _as2d(own)
    r, n = own2.shape
    tm = _ew_tile(r)

    def body(own_ref, q_ref, o_ref):
        f = lambda i: q_ref[i].astype(F32)
        o_ref[...] = ((own_ref[...].astype(F32) + f(0)) + f(1)) + f(2)

    blk = pl.BlockSpec((tm, n), lambda i: (i, 0))
    return pl.pallas_call(
        body, out_shape=jax.ShapeDtypeStruct((r, n), F32), grid=(r // tm,),
        in_specs=[blk, pl.BlockSpec((3, tm, n), lambda i: (0, i, 0))], out_specs=blk,
        compiler_params=_cp("parallel"), name=name)(own2, q3).reshape(shape)


def _adamw(w, g, m, v, *, name):
    shape = w.shape
    w2, g2, m2, v2 = _as2d(w), _as2d(g), _as2d(m), _as2d(v)
    r, n = w2.shape
    tm = _ew_tile(r)
    c1 = 1.0 - ADAM_B1 ** ADAM_STEP
    c2 = 1.0 - ADAM_B2 ** ADAM_STEP

    def body(w_ref, g_ref, m_ref, v_ref, d_ref, mo_ref, vo_ref):
        gv = g_ref[...]
        mn = ADAM_B1 * m_ref[...] + (1.0 - ADAM_B1) * gv
        vn = ADAM_B2 * v_ref[...] + (1.0 - ADAM_B2) * (gv * gv)
        mo_ref[...] = mn
        vo_ref[...] = vn
        d_ref[...] = -ADAM_LR * ((mn / c1) / (jnp.sqrt(vn / c2) + ADAM_EPS) + ADAM_WD * w_ref[...])

    blk = pl.BlockSpec((tm, n), lambda i: (i, 0))
    o = jax.ShapeDtypeStruct((r, n), F32)
    d, mo, vo = pl.pallas_call(body, out_shape=(o, o, o), grid=(r // tm,), in_specs=[blk] * 4, out_specs=(blk,) * 3,
                               compiler_params=_cp("parallel"), name=name)(w2, g2, m2, v2)
    return d.reshape(shape), mo.reshape(shape), vo.reshape(shape)


ANY = pl.BlockSpec(memory_space=pl.ANY)


def _place():
    return lax.axis_index("x"), lax.axis_index("y"), lax.axis_index("c")


def _other_chips(x, y):
    return [(1 - x, y), (x, 1 - y), (1 - x, 1 - y)]


def _remote(src, dst, ssem, rsem, dev):
    return pltpu.make_async_remote_copy(src_ref=src, dst_ref=dst, send_sem=ssem, recv_sem=rsem, device_id=dev,
                                        device_id_type=MESH)


def _allgather_chips(arrs, *, name):
    n = len(arrs)

    def body(*refs):
        ins, outs = refs[:n], refs[n:2 * n]
        s1, r1, s2, r2 = refs[2 * n:]
        x, y, c = _place()
        q = 2 * x + y
        chips = _other_chips(x, y)
        qs = [2 * cx + cy for cx, cy in chips]
        sib = (x, y, 1 - c)
        first, passed = [], []
        for k in range(n):
            for j, chip in enumerate(chips):
                first.append(_remote(ins[k].at[c], outs[k].at[c, q], s1.at[k, j], r1.at[k, j], (*chip, c)))
        for cp in first:
            cp.start()
        for k in range(n):
            for j, chip in enumerate(chips):
                land = outs[k].at[c, qs[j]]
                _remote(land, land, s1.at[k, j], r1.at[k, j], (*chip, c)).wait_recv()
                fw = _remote(land, land, s2.at[k, j], r2.at[k, j], sib)
                fw.start()
                passed.append(fw)
        for k in range(n):
            for j in range(3):
                land = outs[k].at[1 - c, qs[j]]
                _remote(land, land, s2.at[k, j], r2.at[k, j], sib).wait_recv()
        for cp in first + passed:
            cp.wait_send()

    sem = pltpu.SemaphoreType.DMA
    outs = pl.pallas_call(
        body, out_shape=tuple(jax.ShapeDtypeStruct((2, 4) + a.shape[1:], a.dtype) for a in arrs),
        in_specs=[ANY] * n, out_specs=(ANY,) * n,
        scratch_shapes=[sem((n, 3)), sem((n, 3)), sem((n, 3)), sem((n, 3))], name=name)(*arrs)
    chip = 2 * lax.axis_index("x") + lax.axis_index("y")
    return [lax.dynamic_update_slice_in_dim(o, a[:, None], chip, axis=1) for o, a in zip(outs, arrs)]


def _pair_exchange(arrs, *, name):
    n = len(arrs)

    def body(*refs):
        ins, outs = refs[:n], refs[n:2 * n]
        ssem, rsem = refs[2 * n:]
        x, y, c = _place()
        cps = [_remote(ins[k].at[:, 1 - c], outs[k], ssem.at[k], rsem.at[k], (x, y, 1 - c)) for k in range(n)]
        for cp in cps:
            cp.start()
        for cp in cps:
            cp.wait()

    sem = pltpu.SemaphoreType.DMA
    return pl.pallas_call(
        body, out_shape=tuple(jax.ShapeDtypeStruct((a.shape[0],) + a.shape[2:], a.dtype) for a in arrs),
        in_specs=[ANY] * n, out_specs=(ANY,) * n, scratch_shapes=[sem((n,)), sem((n,))], name=name)(*arrs)


def _pair_sum(mine, theirs, c, *, name):
    _, _, r, n = mine.shape
    tm = _ew_tile(r)

    def body(c_ref, a_ref, b_ref, o_ref):
        o_ref[...] = (a_ref[...] + b_ref[...]).astype(BF16)

    blk = pl.BlockSpec((None, tm, n), lambda s, i, c_ref: (s, i, 0))
    return pl.pallas_call(
        body, out_shape=jax.ShapeDtypeStruct((4, r, n), BF16),
        grid_spec=pltpu.PrefetchScalarGridSpec(
            num_scalar_prefetch=1, grid=(4, r // tm),
            in_specs=[pl.BlockSpec((None, None, tm, n), lambda s, i, c_ref: (s, c_ref[0], i, 0)), blk], out_specs=blk),
        compiler_params=_cp("parallel", "parallel"), name=name)(jnp.reshape(c, (1,)).astype(jnp.int32), mine, theirs)


def _chip_copies(ins, outs, ssem, rsem, mode):
    x, y, c = _place()
    cps = []
    for k in range(len(ins)):
        for j, (cx, cy) in enumerate(_other_chips(x, y)):
            src = ins[k].at[2 * cx + cy] if mode == "scatter" else ins[k].at[c]
            cps.append(_remote(src, outs[k].at[j], ssem.at[k, j], rsem.at[k, j], (cx, cy, c)))
    return cps


def _landing_shape(a):
    return jax.ShapeDtypeStruct((3,) + a.shape[1:], a.dtype)


def _chip_exchange(arrs, mode, *, name):
    n = len(arrs)

    def body(*refs):
        ins, outs = refs[:n], refs[n:2 * n]
        ssem, rsem = refs[2 * n:]
        cps = _chip_copies(ins, outs, ssem, rsem, mode)
        for cp in cps:
            cp.start()
        for cp in cps:
            cp.wait()

    sem = pltpu.SemaphoreType.DMA
    return list(pl.pallas_call(
        body, out_shape=tuple(_landing_shape(a) for a in arrs),
        in_specs=[ANY] * n, out_specs=(ANY,) * n, scratch_shapes=[sem((n, 3)), sem((n, 3))], name=name)(*arrs))


def _pair_swap(arrs, *, name):
    n = len(arrs)

    def body(*refs):
        ins, outs = refs[:n], refs[n:2 * n]
        ssem, rsem = refs[2 * n:]
        x, y, c = _place()
        cps = [_remote(ins[k], outs[k], ssem.at[k], rsem.at[k], (x, y, 1 - c)) for k in range(n)]
        for cp in cps:
            cp.start()
        for cp in cps:
            cp.wait()

    sem = pltpu.SemaphoreType.DMA
    return pl.pallas_call(
        body, out_shape=tuple(jax.ShapeDtypeStruct(a.shape, a.dtype) for a in arrs),
        in_specs=[ANY] * n, out_specs=(ANY,) * n, scratch_shapes=[sem((n,)), sem((n,))], name=name)(*arrs)


def _allreduce_small(slab, *, name):
    r, n = slab.shape

    def body(x_ref, o_ref, buf, ssem, rsem):
        x, y, c = _place()
        me = 4 * x + 2 * y + c
        buf[me] = x_ref[...]
        cps = []
        for rel in range(1, 8):
            bx, by, bc = (rel >> 2) & 1, (rel >> 1) & 1, rel & 1
            px, py, pc = (x + bx) % 2, (y + by) % 2, (c + bc) % 2
            cps.append((_remote(x_ref, buf.at[me], ssem.at[rel - 1], rsem.at[rel - 1], (px, py, pc)),
                        4 * px + 2 * py + pc, (px, py, pc)))
        for cp, _, _ in cps:
            cp.start()
        for rel, (cp, peer, dev) in enumerate(cps):
            cp.wait_send()
            _remote(x_ref, buf.at[peer], ssem.at[rel], rsem.at[rel], dev).wait_recv()
        acc = buf[0]
        for k in range(1, 8):
            acc = acc + buf[k]
        o_ref[...] = acc

    vm = pl.BlockSpec(memory_space=pltpu.VMEM)
    sem = pltpu.SemaphoreType.DMA
    return pl.pallas_call(
        body, out_shape=jax.ShapeDtypeStruct((r, n), F32), in_specs=[vm], out_specs=vm,
        scratch_shapes=[pltpu.VMEM((8, r, n), F32), sem((7,)), sem((7,))], name=name)(slab)


def _slab(arrs, row_mult):
    flat = jnp.concatenate([a.reshape(-1) for a in arrs])
    unit = 128 * row_mult
    total = -(-flat.size // unit) * unit
    return jnp.pad(flat, (0, total - flat.size)).reshape(-1, 128)


def _unslab(slab, shapes):
    flat = slab.reshape(-1)
    out, off = [], 0
    for s in shapes:
        size = int(np.prod(s))
        out.append(flat[off:off + size].reshape(s))
        off += size
    return out


def _cols_from_chips(a):
    return jnp.transpose(a, (1, 0, 2)).reshape(a.shape[1], -1)


def _cols_to_chips(a, parts):
    r = a.shape[0]
    return jnp.transpose(a.reshape(r, parts, -1), (1, 0, 2))


BIG = ("w_in", "w_out", "up", "down")
GATHER_RIDES = {("proj", 0): (("w_out", 0), ("up", 0)), ("mix_out", 0): (("down", 0),),
                ("ffn_up_a", 0): (("w_in", 1), ("w_out", 1)), ("ffn_up_g", 0): (("up", 1),),
                ("ffn_down", 0): (("down", 1),)}
REDUCE_RIDES = {("ffn_down_dx", 0): ("up",), ("ffn_up_a_dx", 0): ("w_in", "w_out"), ("ffn_up_g_dx", 0): ("down",)}


class _LocalWeights:
    def __init__(self, meta, win, wout, up_a, up_g, down, w2p, cw):
        self._meta, self._w = meta, {"win": win, "wout": wout, "up_a": up_a, "up_g": up_g, "down": down, "w2p": w2p,
                                     "cw": cw}

    def meta(self):
        return self._meta

    def get(self, kind, l):
        return self._w[kind][l]

    def mm(self, site, l, a, b, **kw):
        return _mm(a, b, name=site, **kw)

    def grads_done(self, l, g):
        pass


class _ChipWeights:
    def __init__(self, w_in, w_out, ffn_up, ffn_down, meta_tokens, gla_gate_w2, ffn_conv_w):
        self.x, self.y, self.c = _place()
        self.q = 2 * self.x + self.y
        halves = lambda a: a.astype(BF16).reshape(2, a.shape[0] // 2, a.shape[1])
        self.own = {(k, l): halves(a[l]) for k, a in zip(BIG, (w_in, w_out, ffn_up, ffn_down)) for l in range(DEPTH)}
        self.landed, self.swapped, self.full, self.n_swaps = {}, {}, {}, 0
        self.sh_shapes = [meta_tokens.shape, gla_gate_w2.shape, ffn_conv_w.shape]
        self.own["small", 0] = _slab([meta_tokens, gla_gate_w2, ffn_conv_w], 16).reshape(2, -1, 128)
        first = [("w_in", 0), ("small", 0)]
        for key, arr in zip(first, _chip_exchange([self.own[k] for k in first], "bcast", name="gather_first")):
            self.landed[key] = arr
        sh = self._whole("small", 0).reshape(4, -1, 128)
        parts = [_unslab(sh[k], self.sh_shapes) for k in range(4)]
        self._meta = jnp.concatenate([p[0] for p in parts], axis=-1)
        self.w2 = jnp.concatenate([p[1] for p in parts], axis=-1)
        self.cw = jnp.concatenate([p[2] for p in parts], axis=-1)
        self.partial, self.slots = {}, {}

    def _whole(self, kind, l):
        if (kind, l) not in self.full:
            if (kind, l) not in self.swapped:
                keys = [k for k in self.landed if k not in self.swapped]
                got = _pair_swap([self.landed[k] for k in keys], name=f"gather_swap_{self.n_swaps}")
                self.n_swaps += 1
                self.swapped.update(zip(keys, got))
            own, land, swap = self.own[kind, l], self.landed[kind, l], self.swapped[kind, l]
            pieces = jnp.concatenate([own, jnp.stack([land, swap], axis=1).reshape((6,) + own.shape[1:])], axis=0)
            rel = jnp.bitwise_xor(jnp.arange(4, dtype=jnp.int32)[:, None], self.q)
            j = jnp.where(rel == 2, 0, jnp.where(rel == 1, 1, 2))
            h = jnp.arange(2, dtype=jnp.int32)[None, :]
            idx = jnp.where(rel == 0, h, 2 + 2 * j + (h != self.c).astype(jnp.int32))
            full = jnp.take(pieces, idx.reshape(8), axis=0)
            self.full[kind, l] = full.reshape(4, 2 * own.shape[1], own.shape[2])
        return self.full[kind, l]

    def meta(self):
        return self._meta

    def get(self, kind, l):
        if kind == "win":
            return jnp.pad(_cols_from_chips(self._whole("w_in", l)), ((0, 0), (0, IN_PAD - IN_WIDTH)))
        if kind == "wout":
            return self._whole("w_out", l).reshape(D_MODEL, D_MODEL)
        if kind == "up_a":
            return _cols_from_chips(self._whole("up", l)[0:2])
        if kind == "up_g":
            return _cols_from_chips(self._whole("up", l)[2:4])
        if kind == "down":
            return self._whole("down", l).reshape(D_FF, D_MODEL)
        if kind == "w2p":
            return jnp.pad(self.w2[l], ((0, 128 - GLA_RANK), (0, 0))).astype(BF16)
        return self.cw[l]

    def mm(self, site, l, a, b, **kw):
        if (site, l) in GATHER_RIDES:
            keys = GATHER_RIDES[site, l]
            out, got = _mm(a, b, name=site, carry=([self.own[k] for k in keys], "bcast"), **kw)
            self.landed.update(zip(keys, got))
            return out
        if (site, l) in REDUCE_RIDES and all((k, DEPTH - 1) in self.partial for k in REDUCE_RIDES[site, l]):
            keys = [(k, DEPTH - 1) for k in REDUCE_RIDES[site, l]]
            out, got = _mm(a, b, name=site, carry=([self.partial[k] for k in keys], "scatter"), **kw)
            self.slots.update(zip(keys, got))
            return out
        return _mm(a, b, name=site, **kw)

    def grads_done(self, l, g):
        split = lambda a: a.reshape(4, 2, a.shape[-2] // 2, a.shape[-1]) if a.ndim == 3 else \
            a.reshape(4, 2, a.shape[0] // 8, a.shape[1])
        big = {"w_in": split(_cols_to_chips(g["w_in"][l][:, :IN_WIDTH], 4)), "w_out": split(g["w_out"][l]),
               "up": split(jnp.concatenate([g["up_a"][l], g["up_g"][l]], axis=0)), "down": split(g["down"][l])}
        from_sib = _pair_exchange([big[k] for k in BIG], name=f"grads_pair_exchange_{l}")
        for k, theirs in zip(BIG, from_sib):
            self.partial[k, l] = _pair_sum(big[k], theirs, self.c, name=f"pair_sum_{k}_{l}")

    def reduce(self):
        keys = [(k, l) for l in range(DEPTH) for k in BIG]
        late = [k for k in keys if k not in self.slots]
        self.slots.update(zip(late, _chip_exchange([self.partial[k] for k in late], "scatter",
                                                   name="grads_chip_exchange")))
        half = {}
        for k in keys:
            own = lax.dynamic_index_in_dim(self.partial[k], self.q, 0, keepdims=False)
            half[k] = _sum_slots(own, self.slots[k], name=f"chip_sum_{k[0]}_{k[1]}")
        other = dict(zip(keys, _pair_swap([half[k] for k in keys], name="grads_pair_swap")))
        whole = lambda k: jnp.where(self.c == 0, jnp.concatenate([half[k], other[k]], axis=0),
                                    jnp.concatenate([other[k], half[k]], axis=0))
        return [jnp.stack([whole((k, l)) for l in range(DEPTH)]) for k in BIG]


def _local_step(x_rows, target_rows, wts, pre_mix_norm, gla_gate_b, ret_norm_w, gla_norm_w, post_mix_norm,
                pre_ffn_norm, ffn_conv_b, post_ffn_norm):
    d = D_MODEL
    lp = x_rows.shape[0] + FRONT + BACK
    row = lambda a, l: a[l][None, :]
    rtab = _ret_tables(lp)
    gtab = _gla_tables()
    h0 = jnp.concatenate([jnp.zeros((PADF, d), F32), wts.meta(), x_rows, jnp.zeros((BACK, d), F32)], axis=0)
    target = jnp.pad(target_rows, ((FRONT, BACK), (0, 0)))

    saved = []
    h = h0
    _, hn = _resid_norm(h0, None, None, row(pre_mix_norm, 0), name="norm_in")
    loss_local = dy = None
    for l in range(DEPTH):
        s = {"h_in": h, "hn": hn}
        s["proj"] = wts.mm("proj", l, hn, wts.get("win", l))
        s["o_ret"], s["st_ret"] = _retention(s["proj"], rtab, name="retention")
        s["o_gla"], s["st_gla"] = _gla(s["proj"], wts.get("w2p", l), row(gla_gate_b, l), gtab, name="gla")
        s["merged"] = _merge(s["o_ret"], s["o_gla"], s["proj"], row(ret_norm_w, l), row(gla_norm_w, l), name="merge")
        s["m"] = wts.mm("mix_out", l, s["merged"], wts.get("wout", l))
        s["h_mid"], s["hn2"] = _resid_norm(h, s["m"], row(post_mix_norm, l), row(pre_ffn_norm, l), name="resid_mix")
        s["ua"] = wts.mm("ffn_up_a", l, s["hn2"], wts.get("up_a", l))
        s["ug"] = wts.mm("ffn_up_g", l, s["hn2"], wts.get("up_g", l))
        cw_a, cw_g = wts.get("cw", l)[:, :D_FF], wts.get("cw", l)[:, D_FF:]
        cb_a, cb_g = ffn_conv_b[l][None, :D_FF], ffn_conv_b[l][None, D_FF:]
        s["conv"] = (cw_a, cw_g, cb_a, cb_g)
        s["act"] = _conv_act(s["ua"], s["ug"], cw_a, cw_g, cb_a, cb_g, name="conv_act")
        s["f"] = wts.mm("ffn_down", l, s["act"], wts.get("down", l))
        if l + 1 < DEPTH:
            h, hn = _resid_norm(s["h_mid"], s["f"], row(post_ffn_norm, l), row(pre_mix_norm, l + 1), name="resid_ffn")
        else:
            loss_local, dy = _loss_head(s["h_mid"], s["f"], row(post_ffn_norm, l), target, name="loss_head")
        saved.append(s)

    g = {k: [None] * DEPTH for k in ("pre_mix", "w_in", "w2", "gb", "ret_n", "gla_n", "w_out", "post_mix", "pre_ffn",
                                     "up_a", "up_g", "cw", "cb", "down", "post_ffn")}
    dh_out, dhn_next = dy, None
    for l in reversed(range(DEPTH)):
        s = saved[l]
        cw_a, cw_g, cb_a, cb_g = s["conv"]
        if l + 1 < DEPTH:
            dh, df, g["pre_mix"][l + 1], g["post_ffn"][l] = _resid_norm_bwd(
                dh_out, dhn_next, saved[l + 1]["h_in"], s["f"], row(pre_mix_norm, l + 1), row(post_ffn_norm, l),
                name="resid_ffn_bwd")
        else:
            dh, df, _, g["post_ffn"][l] = _resid_norm_bwd(dh_out, None, None, s["f"], None, row(post_ffn_norm, l),
                                                          name="loss_head_bwd")
        dact = wts.mm("ffn_down_dx", l, df, wts.get("down", l), nt=True)
        g["down"][l] = _mm_tn(s["act"], df, tn=512, name="ffn_down_dw")
        du_a, du_g, dcw_a, dcw_g, dcb_a, dcb_g = _conv_act_bwd(s["ua"], s["ug"], dact, cw_a, cw_g, cb_a, cb_g,
                                                               name="conv_act_bwd")
        g["cw"][l] = jnp.concatenate([dcw_a, dcw_g], axis=1)
        g["cb"][l] = jnp.concatenate([dcb_a, dcb_g], axis=1)[0]
        g["up_a"][l] = _mm_tn(s["hn2"], du_a, tn=D_FF // 2, blocked=True, name="ffn_up_a_dw")
        g["up_g"][l] = _mm_tn(s["hn2"], du_g, tn=D_FF // 2, blocked=True, name="ffn_up_g_dw")
        dhn2 = wts.mm("ffn_up_a_dx", l, du_a, wts.get("up_a", l), nt=True)
        dhn2 = wts.mm("ffn_up_g_dx", l, du_g, wts.get("up_g", l), nt=True, add=dhn2)
        dh, dm, g["pre_ffn"][l], g["post_mix"][l] = _resid_norm_bwd(
            dh, dhn2, s["h_mid"], s["m"], row(pre_ffn_norm, l), row(post_mix_norm, l), name="resid_mix_bwd")
        g["w_out"][l] = _mm_tn(s["merged"], dm, name="mix_out_dw")
        dmerged = wts.mm("mix_out_dx", l, dm, wts.get("wout", l), nt=True)
        do_ret, do_gla, drg, dgr, g["ret_n"][l], g["gla_n"][l] = _merge_bwd(
            dmerged, s["o_ret"], s["o_gla"], s["proj"], row(ret_norm_w, l), row(gla_norm_w, l), name="merge_bwd")
        drq, drk, drv = _retention_bwd(s["proj"], do_ret, s["st_ret"], rtab, name="retention_bwd")
        dgq, dgk, dgv, dga, dw2, dgb = _gla_bwd(s["proj"], do_gla, s["st_gla"], wts.get("w2p", l), row(gla_gate_b, l), gtab,
                                                name="gla_bwd")
        g["w2"][l], g["gb"][l] = dw2[:GLA_RANK], dgb[0]
        dproj = jnp.concatenate([drq, drk, drv, drg, dgq, dgk, dgv, dgr, dga,
                                 jnp.zeros((lp, IN_PAD - C_GA - 128), BF16)], axis=1)
        g["w_in"][l] = _mm_tn(s["hn"], dproj, tn=1280, name="proj_dw")
        dhn_next = wts.mm("proj_dx", l, dproj, wts.get("win", l), nt=True)
        dh_out = dh
        wts.grads_done(l, g)
    dh0, _, g["pre_mix"][0], _ = _resid_norm_bwd(dh_out, dhn_next, h0, None, row(pre_mix_norm, 0), None,
                                                 name="norm_in_bwd")
    return loss_local, dh0, g


def kernel(x, meta_tokens, pre_mix_norm, w_in, gla_gate_w2, gla_gate_b, ret_norm_w, gla_norm_w, w_out, post_mix_norm, pre_ffn_norm, ffn_up, ffn_conv_w, ffn_conv_b, ffn_down, post_ffn_norm, loss_target, m_meta_tokens, m_pre_mix_norm, m_w_in, m_gla_gate_w2, m_gla_gate_b, m_ret_norm_w, m_gla_norm_w, m_w_out, m_post_mix_norm, m_pre_ffn_norm, m_ffn_up, m_ffn_conv_w, m_ffn_conv_b, m_ffn_down, m_post_ffn_norm, v_meta_tokens, v_pre_mix_norm, v_w_in, v_gla_gate_w2, v_gla_gate_b, v_ret_norm_w, v_gla_norm_w, v_w_out, v_post_mix_norm, v_pre_ffn_norm, v_ffn_up, v_ffn_conv_w, v_ffn_conv_b, v_ffn_down, v_post_ffn_norm):
    xi, yi, ci = _place()
    chip = 2 * xi + yi
    seq = x.shape[1]
    d = D_MODEL
    wts = _ChipWeights(w_in, w_out, ffn_up, ffn_down, meta_tokens, gla_gate_w2, ffn_conv_w)
    loss_local, dh0, g = _local_step(x[0], loss_target[0], wts, pre_mix_norm, gla_gate_b, ret_norm_w, gla_norm_w,
                                     post_mix_norm, pre_ffn_norm, ffn_conv_b, post_ffn_norm)
    grad_x = dh0[FRONT:FRONT + seq][None]
    names = ("w_in", "w_out", "ffn_up", "ffn_down")
    g_w_in, g_w_out, g_ffn_up, g_ffn_down = wts.reduce()

    small_full = [dh0[PADF:FRONT], jnp.stack(g["pre_mix"])[:, 0], jnp.stack(g["w2"]), jnp.stack(g["gb"]),
                  jnp.stack(g["ret_n"])[:, 0], jnp.stack(g["gla_n"])[:, 0], jnp.stack(g["post_mix"])[:, 0],
                  jnp.stack(g["pre_ffn"])[:, 0], jnp.stack(g["cw"]), jnp.stack(g["cb"]),
                  jnp.stack(g["post_ffn"])[:, 0]]
    small_sum = _unslab(_allreduce_small(_slab(small_full, 8), name="small_allreduce"), [a.shape for a in small_full])
    (g_meta, g_pre_mix, g_w2, g_gb, g_ret_n, g_gla_n, g_post_mix, g_pre_ffn, g_cw, g_cb, g_post_ffn) = small_sum
    g_meta = lax.dynamic_slice_in_dim(g_meta, chip * 256, 256, axis=1)
    g_w2 = lax.dynamic_slice_in_dim(g_w2, chip * 64, 64, axis=2)
    g_cw = lax.dynamic_slice_in_dim(g_cw, chip * 1408, 1408, axis=2)

    grads = [g_meta, g_pre_mix, g_w_in, g_w2, g_gb, g_ret_n, g_gla_n, g_w_out, g_post_mix, g_pre_ffn, g_ffn_up,
             g_cw, g_cb, g_ffn_down, g_post_ffn]
    ws = [meta_tokens, pre_mix_norm, w_in, gla_gate_w2, gla_gate_b, ret_norm_w, gla_norm_w, w_out, post_mix_norm,
          pre_ffn_norm, ffn_up, ffn_conv_w, ffn_conv_b, ffn_down, post_ffn_norm]
    ms = [m_meta_tokens, m_pre_mix_norm, m_w_in, m_gla_gate_w2, m_gla_gate_b, m_ret_norm_w, m_gla_norm_w, m_w_out,
          m_post_mix_norm, m_pre_ffn_norm, m_ffn_up, m_ffn_conv_w, m_ffn_conv_b, m_ffn_down, m_post_ffn_norm]
    vs = [v_meta_tokens, v_pre_mix_norm, v_w_in, v_gla_gate_w2, v_gla_gate_b, v_ret_norm_w, v_gla_norm_w, v_w_out,
          v_post_mix_norm, v_pre_ffn_norm, v_ffn_up, v_ffn_conv_w, v_ffn_conv_b, v_ffn_down, v_post_ffn_norm]
    big_idx = (2, 7, 10, 13)
    deltas, new_m, new_v = [None] * 15, [None] * 15, [None] * 15
    for i, nm in zip(big_idx, names):
        deltas[i], new_m[i], new_v[i] = _adamw(ws[i], grads[i], ms[i], vs[i], name=f"adamw_{nm}")
    small_idx = [i for i in range(15) if i not in big_idx]
    shapes = [ws[i].shape for i in small_idx]
    sd, sm, sv = _adamw(_slab([ws[i] for i in small_idx], 8), _slab([grads[i] for i in small_idx], 8),
                        _slab([ms[i] for i in small_idx], 8), _slab([vs[i] for i in small_idx], 8), name="adamw_small")
    for i, a, b, c_ in zip(small_idx, _unslab(sd, shapes), _unslab(sm, shapes), _unslab(sv, shapes)):
        deltas[i], new_m[i], new_v[i] = a, b, c_

    loss = lax.psum(loss_local, ("x", "y", "c"))
    return (loss, grad_x, *grads, *deltas, *new_m, *new_v)
```

```python
import functools
import math

import numpy as np
import jax
import jax.numpy as jnp
from jax import lax
from jax.experimental import pallas as pl
from jax.experimental.pallas import tpu as pltpu

F32 = jnp.float32
BF16 = jnp.bfloat16

D_MODEL = 1024
DEPTH = 2
N_META = 16
EPS = 1e-6
RET_HEADS = 4
RET_DK = 128
GLA_HEADS = 4
GLA_DK = 64
GLA_DV = 128
GLA_QK = GLA_HEADS * GLA_DK
GLA_V = GLA_HEADS * GLA_DV
GLA_RANK = 16
GLA_TAU = 16.0
D_FF = 2816
ROPE_BASE = 10000.0
IN_WIDTH = 3600
IN_PAD = 3840
C_RQ, C_RK, C_RV, C_RG, C_GR, C_GQ, C_GK, C_GV, C_GA = 0, 512, 1024, 1536, 2048, 2560, 2816, 3072, 3584
P_RET, P_GATE, P_GLA = 1536, 1024, 1280


def _to_kernel_cols(w):
    pad = jnp.zeros(w.shape[:-1] + (IN_PAD - IN_WIDTH,), w.dtype)
    return jnp.concatenate([w[..., 0:2048], w[..., 3072:3584], w[..., 2048:3072], w[..., 3584:3600], pad], axis=-1)


def _to_reference_cols(d_ret, d_gate, d_gla):
    return jnp.concatenate([d_ret, d_gate[..., 0:512], d_gla[..., 0:1024], d_gate[..., 512:1024],
                            d_gla[..., 1024:1024 + GLA_RANK]], axis=-1)

FRONT = 64
BACK = 64
PADF = FRONT - N_META
RET_CHUNK = 128
GLA_CHUNK = 64
GLA_SUB = 16
BLK = 640

ADAM_LR, ADAM_B1, ADAM_B2, ADAM_EPS, ADAM_WD, ADAM_STEP = 0.001, 0.9, 0.999, 1e-08, 0.01, 10

VMEM_LIMIT = 56 * 2 ** 20
MESH = pl.DeviceIdType.MESH


def _cp(*sem):
    return pltpu.CompilerParams(dimension_semantics=sem, vmem_limit_bytes=VMEM_LIMIT)


def _tile(n, cands):
    for t in cands:
        if n % t == 0:
            return t
    raise ValueError(f"no tile for {n} in {cands}")


def _row_tile(n):
    return _tile(n, (640, 512, 320, 256, 128, 64))


def _mm(a, b, *, nt=False, add=None, out_dtype=F32, tn=None, name, carry=None):
    m, k = a.shape
    n = b.shape[0] if nt else b.shape[1]
    tm = _tile(m, (640, 320, 256, 128, 64))
    tn = n if tn is None else tn
    dn = (((1,), (1,)), ((), ())) if nt else (((1,), (0,)), ((), ()))
    nj, ni = n // tn, m // tm
    n_in = 2 + (add is not None)
    c_arrs, c_mode = carry if carry is not None else ((), None)
    nc = len(c_arrs)

    def body(*refs):
        a_ref, b_ref = refs[:2]
        c_ref = refs[2] if add is not None else None
        o_ref = refs[n_in + nc]
        if nc:
            c_ins, c_outs = refs[n_in:n_in + nc], refs[n_in + nc + 1:n_in + 2 * nc + 1]
            ssem, rsem = refs[n_in + 2 * nc + 1:]
            j, i = pl.program_id(0), pl.program_id(1)

            @pl.when((j == 0) & (i == 0))
            def _():
                for cp in _chip_copies(c_ins, c_outs, ssem, rsem, c_mode):
                    cp.start()
        r = lax.dot_general(a_ref[...].astype(BF16), b_ref[...].astype(BF16), dn, preferred_element_type=F32)
        if add is not None:
            r = r + c_ref[...]
        o_ref[...] = r.astype(o_ref.dtype)
        if nc:
            @pl.when((j == nj - 1) & (i == ni - 1))
            def _():
                for cp in _chip_copies(c_ins, c_outs, ssem, rsem, c_mode):
                    cp.wait()

    b_spec = pl.BlockSpec((tn, k), lambda j, i: (j, 0)) if nt else pl.BlockSpec((k, tn), lambda j, i: (0, j))
    in_specs = [pl.BlockSpec((tm, k), lambda j, i: (i, 0)), b_spec]
    args = [a, b]
    if add is not None:
        in_specs.append(pl.BlockSpec((tm, tn), lambda j, i: (i, j)))
        args.append(add)
    out_shape = jax.ShapeDtypeStruct((m, n), out_dtype)
    out_spec = pl.BlockSpec((tm, tn), lambda j, i: (i, j))
    if not nc:
        return pl.pallas_call(
            body, out_shape=out_shape, grid=(nj, ni), in_specs=in_specs, out_specs=out_spec,
            compiler_params=_cp("parallel", "parallel"), name=name)(*args)
    sem = pltpu.SemaphoreType.DMA
    outs = pl.pallas_call(
        body, out_shape=(out_shape,) + tuple(_landing_shape(x) for x in c_arrs), grid=(nj, ni),
        in_specs=in_specs + [ANY] * nc, out_specs=(out_spec,) + (ANY,) * nc,
        scratch_shapes=[sem((nc, 3)), sem((nc, 3))],
        compiler_params=_cp("arbitrary", "arbitrary"), name=name)(*args, *c_arrs)
    return outs[0], list(outs[1:])


def _mm_nt_sum(a_list, b_list, *, name):
    m, n = a_list[0].shape[0], b_list[0].shape[0]
    tm = _tile(m, (640, 320, 256, 128, 64))
    np_ = len(a_list)

    def body(*refs):
        acc = None
        for a_ref, b_ref in zip(refs[:np_], refs[np_:2 * np_]):
            r = lax.dot_general(a_ref[...].astype(BF16), b_ref[...].astype(BF16), (((1,), (1,)), ((), ())),
                                preferred_element_type=F32)
            acc = r if acc is None else acc + r
        refs[2 * np_][...] = acc

    return pl.pallas_call(
        body, out_shape=jax.ShapeDtypeStruct((m, n), F32), grid=(m // tm,),
        in_specs=[pl.BlockSpec((tm, a.shape[1]), lambda i: (i, 0)) for a in a_list]
        + [pl.BlockSpec(b.shape, lambda i: (0, 0)) for b in b_list],
        out_specs=pl.BlockSpec((tm, n), lambda i: (i, 0)),
        compiler_params=_cp("parallel"), name=name)(*a_list, *b_list)


def _mm_tn(a, b, *, tn=None, blocked=False, name):
    m, k = a.shape
    n = b.shape[1]
    tm = _tile(m, (1664, 640, 320, 256, 128, 64))
    tn = n if tn is None else tn
    if blocked:
        out_shape = jax.ShapeDtypeStruct((n // tn, k, tn), F32)
        out_spec = pl.BlockSpec((None, k, tn), lambda j, i: (j, 0, 0))
    else:
        out_shape = jax.ShapeDtypeStruct((k, n), F32)
        out_spec = pl.BlockSpec((k, tn), lambda j, i: (0, j))

    def body(a_ref, b_ref, o_ref):
        @pl.when(pl.program_id(1) == 0)
        def _():
            o_ref[...] = jnp.zeros_like(o_ref)
        o_ref[...] += lax.dot_general(a_ref[...].astype(BF16), b_ref[...].astype(BF16),
                                      (((0,), (0,)), ((), ())), preferred_element_type=F32)

    return pl.pallas_call(
        body, out_shape=out_shape, grid=(n // tn, m // tm),
        in_specs=[pl.BlockSpec((tm, k), lambda j, i: (i, 0)), pl.BlockSpec((tm, tn), lambda j, i: (i, j))],
        out_specs=out_spec, compiler_params=_cp("parallel", "arbitrary"), name=name)(a, b)


def _rms(x, w):
    r = lax.rsqrt(jnp.mean(x * x, axis=-1, keepdims=True) + EPS)
    return x * r * w


def _rms_bwd(x, w, dy):
    r = lax.rsqrt(jnp.mean(x * x, axis=-1, keepdims=True) + EPS)
    xh = x * r
    dxh = dy * w
    dx = r * (dxh - xh * jnp.mean(dxh * xh, axis=-1, keepdims=True))
    return dx, jnp.sum(dy * xh, axis=0, keepdims=True)


def _resid_norm(h, t, w_post, w_next, *, name):
    lp, d = h.shape
    tm = _row_tile(lp)
    has_t = t is not None

    def body(*refs):
        if has_t:
            h_ref, t_ref, wp_ref, wn_ref, ho_ref, hn_ref = refs
            hv = h_ref[...] + _rms(t_ref[...], wp_ref[...])
            ho_ref[...] = hv
        else:
            h_ref, wn_ref, hn_ref = refs
            hv = h_ref[...]
        hn_ref[...] = _rms(hv, wn_ref[...]).astype(BF16)

    row = pl.BlockSpec((tm, d), lambda i: (i, 0))
    vec = pl.BlockSpec((1, d), lambda i: (0, 0))
    if has_t:
        return pl.pallas_call(
            body, out_shape=(jax.ShapeDtypeStruct((lp, d), F32), jax.ShapeDtypeStruct((lp, d), BF16)),
            grid=(lp // tm,), in_specs=[row, row, vec, vec], out_specs=(row, row),
            compiler_params=_cp("parallel"), name=name)(h, t, w_post, w_next)
    return h, pl.pallas_call(
        body, out_shape=jax.ShapeDtypeStruct((lp, d), BF16), grid=(lp // tm,), in_specs=[row, vec],
        out_specs=row, compiler_params=_cp("parallel"), name=name)(h, w_next)


def _resid_norm_bwd(dh_out, dhn, h_new, t, w_next, w_post, *, name):
    lp, d = h_new.shape if h_new is not None else t.shape
    tm = _row_tile(lp)
    has_n = dhn is not None
    has_t = t is not None

    def body(*refs):
        refs = list(refs)
        dho_ref = refs.pop(0)
        if has_n:
            dhn_ref, hn_ref, wn_ref = refs.pop(0), refs.pop(0), refs.pop(0)
        if has_t:
            t_ref, wp_ref = refs.pop(0), refs.pop(0)
        dh_ref = refs.pop(0) if has_n else None
        dt_ref = refs.pop(0) if has_t else None
        dwn_ref = refs.pop(0) if has_n else None
        dwp_ref = refs.pop(0) if has_t else None
        first = pl.program_id(0) == 0
        dh = dho_ref[...]
        if has_n:
            dx, dwn = _rms_bwd(hn_ref[...], wn_ref[...], dhn_ref[...])
            dh = dh + dx
            dh_ref[...] = dh

            @pl.when(first)
            def _():
                dwn_ref[...] = jnp.zeros_like(dwn_ref)
            dwn_ref[...] += dwn
        if has_t:
            dt, dwp = _rms_bwd(t_ref[...], wp_ref[...], dh)
            dt_ref[...] = dt.astype(BF16)

            @pl.when(first)
            def _():
                dwp_ref[...] = jnp.zeros_like(dwp_ref)
            dwp_ref[...] += dwp

    row = pl.BlockSpec((tm, d), lambda i: (i, 0))
    vec = pl.BlockSpec((1, d), lambda i: (0, 0))
    args, in_specs, out_shape, out_specs = [dh_out], [row], [], []
    if has_n:
        args += [dhn, h_new, w_next]
        in_specs += [row, row, vec]
    if has_t:
        args += [t, w_post]
        in_specs += [row, vec]
    if has_n:
        out_shape.append(jax.ShapeDtypeStruct((lp, d), F32)); out_specs.append(row)
    if has_t:
        out_shape.append(jax.ShapeDtypeStruct((lp, d), BF16)); out_specs.append(row)
    if has_n:
        out_shape.append(jax.ShapeDtypeStruct((1, d), F32)); out_specs.append(vec)
    if has_t:
        out_shape.append(jax.ShapeDtypeStruct((1, d), F32)); out_specs.append(vec)
    outs = list(pl.pallas_call(body, out_shape=tuple(out_shape), grid=(lp // tm,), in_specs=in_specs,
                               out_specs=tuple(out_specs), compiler_params=_cp("arbitrary"), name=name)(*args))
    dh = outs.pop(0) if has_n else dh_out
    dt = outs.pop(0) if has_t else None
    dwn = outs.pop(0) if has_n else None
    dwp = outs.pop(0) if has_t else None
    return dh, dt, dwn, dwp


def _loss_head(h, f, w_post, target, *, name):
    lp, d = h.shape
    tm = _row_tile(lp)

    def body(h_ref, f_ref, w_ref, t_ref, loss_ref, dy_ref):
        i = pl.program_id(0)
        y = h_ref[...] + _rms(f_ref[...], w_ref[...])
        rows = i * tm + lax.broadcasted_iota(jnp.int32, (tm, 1), 0)
        tok = (rows >= FRONT) & (rows < lp - BACK)
        err = jnp.where(tok, y - t_ref[...], 0.0)
        dy_ref[...] = err * (1.0 / d)

        @pl.when(i == 0)
        def _():
            loss_ref[...] = jnp.zeros_like(loss_ref)
        part = jnp.sum(jnp.sum(err * err, axis=1, keepdims=True), axis=0, keepdims=True) * (0.5 / d)
        loss_ref[...] += jnp.broadcast_to(part, loss_ref.shape)

    row = pl.BlockSpec((tm, d), lambda i: (i, 0))
    loss, dy = pl.pallas_call(
        body, out_shape=(jax.ShapeDtypeStruct((8, 128), F32), jax.ShapeDtypeStruct((lp, d), F32)),
        grid=(lp // tm,), in_specs=[row, row, pl.BlockSpec((1, d), lambda i: (0, 0)), row],
        out_specs=(pl.BlockSpec((8, 128), lambda i: (0, 0)), row),
        compiler_params=_cp("arbitrary"), name=name)(h, f, w_post, target)
    return loss[0, 0], dy


_GELU_C = math.sqrt(2.0 / math.pi)


def _gelu_and_grad(a):
    a2 = a * a
    t = jnp.tanh(a * (_GELU_C + (_GELU_C * 0.044715) * a2))
    ha = 0.5 * a
    h1 = 0.5 + 0.5 * t
    return a * h1, h1 + ha * (1.0 - t * t) * (_GELU_C + (3.0 * _GELU_C * 0.044715) * a2)


def _gelu(a):
    t = jnp.tanh(a * (_GELU_C + (_GELU_C * 0.044715) * (a * a)))
    return a * (0.5 + 0.5 * t)


def _conv3(parts, n, w, b):
    xx = jnp.concatenate(parts, axis=0)
    return b + xx[8:8 + n] * w[2:3] + pltpu.roll(xx, 1, 0)[8:8 + n] * w[1:2] + pltpu.roll(xx, 2, 0)[8:8 + n] * w[0:1]


def _conv_act(ua, ug, wa, wg, ba, bg, *, name):
    lp, n = ua.shape
    tm = _row_tile(lp)
    tc = _tile(n, (256, 128))
    nb8 = tm // 8

    def body(ua_ref, uap_ref, ug_ref, ugp_ref, wa_ref, wg_ref, ba_ref, bg_ref, o_ref):
        i = pl.program_id(0)
        ca = _conv3([uap_ref[...], ua_ref[...]], tm, wa_ref[...], ba_ref[...])
        cg = _conv3([ugp_ref[...], ug_ref[...]], tm, wg_ref[...], bg_ref[...])
        rows = i * tm + lax.broadcasted_iota(jnp.int32, (tm, 1), 0)
        ok = (rows >= PADF) & (rows < lp - BACK)
        o_ref[...] = jnp.where(ok, _gelu(ca) * cg, 0.0).astype(BF16)

    cur = pl.BlockSpec((tm, tc), lambda i, j: (i, j))
    prev = pl.BlockSpec((8, tc), lambda i, j: (jnp.maximum(i * nb8 - 1, 0), j))
    w3 = pl.BlockSpec((3, tc), lambda i, j: (0, j))
    b1 = pl.BlockSpec((1, tc), lambda i, j: (0, j))
    return pl.pallas_call(
        body, out_shape=jax.ShapeDtypeStruct((lp, n), BF16), grid=(lp // tm, n // tc),
        in_specs=[cur, prev, cur, prev, w3, w3, b1, b1], out_specs=cur,
        compiler_params=_cp("parallel", "parallel"), name=name)(ua, ua, ug, ug, wa, wg, ba, bg)


def _conv_act_bwd(ua, ug, dact, wa, wg, ba, bg, *, name):
    lp, n = ua.shape
    tm = _row_tile(lp)
    tc = _tile(n, (256, 128))
    nb8 = tm // 8
    last8 = lp // 8 - 1
    ext = tm + 8

    def body(ua_ref, uap_ref, uan_ref, ug_ref, ugp_ref, ugn_ref, da_ref, dan_ref, wa_ref, wg_ref, ba_ref, bg_ref,
             dua_ref, dug_ref, dwa_ref, dwg_ref, dba_ref, dbg_ref):
        i = pl.program_id(1)
        wa, wg = wa_ref[...], wg_ref[...]

        def conv(parts, w, b):
            xx = jnp.concatenate(parts, axis=0)
            x, x1, x2 = xx[8:8 + ext], pltpu.roll(xx, 1, 0)[8:8 + ext], pltpu.roll(xx, 2, 0)[8:8 + ext]
            return b + x * w[2:3] + x1 * w[1:2] + x2 * w[0:1], x, x1, x2

        ca, xa, xa1, xa2 = conv([uap_ref[...], ua_ref[...], uan_ref[...]], wa, ba_ref[...])
        cg, xg, xg1, xg2 = conv([ugp_ref[...], ug_ref[...], ugn_ref[...]], wg, bg_ref[...])
        rows = i * tm + lax.broadcasted_iota(jnp.int32, (ext, 1), 0)
        ok = (rows >= PADF) & (rows < lp - BACK)
        dact_e = jnp.where(ok, jnp.concatenate([da_ref[...], dan_ref[...]], axis=0), 0.0)
        gel, gel_d = _gelu_and_grad(ca)
        dca = dact_e * cg * gel_d
        dcg = dact_e * gel

        def back(dc, w):
            return (dc[:tm] * w[2:3] + pltpu.roll(dc, ext - 1, 0)[:tm] * w[1:2]
                    + pltpu.roll(dc, ext - 2, 0)[:tm] * w[0:1])

        dua_ref[...] = back(dca, wa).astype(BF16)
        dug_ref[...] = back(dcg, wg).astype(BF16)

        @pl.when(i == 0)
        def _():
            dwa_ref[...] = jnp.zeros_like(dwa_ref)
            dwg_ref[...] = jnp.zeros_like(dwg_ref)
            dba_ref[...] = jnp.zeros_like(dba_ref)
            dbg_ref[...] = jnp.zeros_like(dbg_ref)

        def wsum(dw_ref, db_ref, dc, x, x1, x2):
            d = dc[:tm]
            s = lambda v: jnp.sum(v, axis=0, keepdims=True)
            dw_ref[0:1, :] += s(d * x2[:tm])
            dw_ref[1:2, :] += s(d * x1[:tm])
            dw_ref[2:3, :] += s(d * x[:tm])
            db_ref[...] += s(d)

        wsum(dwa_ref, dba_ref, dca, xa, xa1, xa2)
        wsum(dwg_ref, dbg_ref, dcg, xg, xg1, xg2)

    cur = pl.BlockSpec((tm, tc), lambda j, i: (i, j))
    prev = pl.BlockSpec((8, tc), lambda j, i: (jnp.maximum(i * nb8 - 1, 0), j))
    nxt = pl.BlockSpec((8, tc), lambda j, i: (jnp.minimum((i + 1) * nb8, last8), j))
    w3 = pl.BlockSpec((3, tc), lambda j, i: (0, j))
    b1 = pl.BlockSpec((1, tc), lambda j, i: (0, j))
    return pl.pallas_call(
        body,
        out_shape=(jax.ShapeDtypeStruct((lp, n), BF16), jax.ShapeDtypeStruct((lp, n), BF16),
                   jax.ShapeDtypeStruct((3, n), F32), jax.ShapeDtypeStruct((3, n), F32),
                   jax.ShapeDtypeStruct((1, n), F32), jax.ShapeDtypeStruct((1, n), F32)),
        grid=(n // tc, lp // tm),
        in_specs=[cur, prev, nxt, cur, prev, nxt, cur, nxt, w3, w3, b1, b1],
        out_specs=(cur, cur, w3, w3, b1, b1),
        compiler_params=_cp("parallel", "arbitrary"), name=name)(ua, ua, ua, ug, ug, ug, dact, dact, wa, wg, ba, bg)


def _sigmoid(x):
    return 1.0 / (1.0 + jnp.exp(-x))


def _merge(o_ret, o_gla, proj, w_ret, w_gla, *, name):
    lp = o_ret.shape[0]
    tm = _row_tile(lp)

    def body(or_ref, og_ref, rg_ref, gr_ref, wr_ref, wg_ref, m_ref):
        oret, ogla = or_ref[...], og_ref[...]
        yr, yg = [], []
        for h in range(4):
            hs = slice(128 * h, 128 * h + 128)
            o = oret[:, hs]
            xc = o - jnp.mean(o, axis=-1, keepdims=True)
            yr.append(xc * lax.rsqrt(jnp.mean(xc * xc, axis=-1, keepdims=True) + EPS))
            o = ogla[:, hs]
            yg.append(o * lax.rsqrt(jnp.mean(o * o, axis=-1, keepdims=True) + EPS))
        rg, gr = rg_ref[...], gr_ref[...]
        m_ref[:, 0:512] = (jnp.concatenate(yr, axis=1) * wr_ref[...] * (rg * _sigmoid(rg))).astype(BF16)
        m_ref[:, 512:1024] = (jnp.concatenate(yg, axis=1) * wg_ref[...] * (gr * _sigmoid(gr))).astype(BF16)

    row = pl.BlockSpec((tm, 512), lambda i: (i, 0))
    vec = pl.BlockSpec((1, 512), lambda i: (0, 0))
    return pl.pallas_call(
        body, out_shape=jax.ShapeDtypeStruct((lp, 1024), BF16), grid=(lp // tm,),
        in_specs=[row, row, pl.BlockSpec((tm, 512), lambda i: (i, C_RG // 512)),
                  pl.BlockSpec((tm, 512), lambda i: (i, C_GR // 512)), vec, vec],
        out_specs=pl.BlockSpec((tm, 1024), lambda i: (i, 0)),
        compiler_params=_cp("parallel"), name=name)(o_ret, o_gla, proj, proj, w_ret, w_gla)


def _merge_bwd(dm, o_ret, o_gla, proj, w_ret, w_gla, *, name):
    lp = o_ret.shape[0]
    tm = _row_tile(lp)

    def body(dm_ref, or_ref, og_ref, rg_ref, gr_ref, wr_ref, wg_ref, dor_ref, dog_ref, dgate_ref, dwr_ref, dwg_ref):
        @pl.when(pl.program_id(0) == 0)
        def _():
            dwr_ref[...] = jnp.zeros_like(dwr_ref)
            dwg_ref[...] = jnp.zeros_like(dwg_ref)

        def group(d, o_all, gate, w, center):
            sg = _sigmoid(gate)
            s = gate * sg
            ds = sg * (1.0 + gate * (1.0 - sg))
            xh, rr = [], []
            for h in range(4):
                o = o_all[:, 128 * h:128 * h + 128]
                if center:
                    o = o - jnp.mean(o, axis=-1, keepdims=True)
                r = lax.rsqrt(jnp.mean(o * o, axis=-1, keepdims=True) + EPS)
                xh.append(o * r)
                rr.append(r)
            xh_all = jnp.concatenate(xh, axis=1)
            dgate = d * xh_all * w * ds
            dw = jnp.sum(d * xh_all * s, axis=0, keepdims=True)
            dxh_all = d * w * s
            do = []
            for h in range(4):
                dxh = dxh_all[:, 128 * h:128 * h + 128]
                t = dxh - xh[h] * jnp.mean(dxh * xh[h], axis=-1, keepdims=True)
                if center:
                    t = t - jnp.mean(dxh, axis=-1, keepdims=True)
                do.append(rr[h] * t)
            return jnp.concatenate(do, axis=1), dgate, dw

        dmv = dm_ref[...]
        do, dg, dw = group(dmv[:, 0:512], or_ref[...], rg_ref[...], wr_ref[...], True)
        dor_ref[...] = do
        dgate_ref[:, 0:512] = dg.astype(BF16)
        dwr_ref[...] += dw
        do, dg, dw = group(dmv[:, 512:1024], og_ref[...], gr_ref[...], wg_ref[...], False)
        dog_ref[...] = do
        dgate_ref[:, 512:1024] = dg.astype(BF16)
        dwg_ref[...] += dw

    row = pl.BlockSpec((tm, 512), lambda i: (i, 0))
    vec = pl.BlockSpec((1, 512), lambda i: (0, 0))
    return pl.pallas_call(
        body,
        out_shape=(jax.ShapeDtypeStruct((lp, 512), F32), jax.ShapeDtypeStruct((lp, 512), F32),
                   jax.ShapeDtypeStruct((lp, P_GATE), BF16),
                   jax.ShapeDtypeStruct((1, 512), F32), jax.ShapeDtypeStruct((1, 512), F32)),
        grid=(lp // tm,),
        in_specs=[pl.BlockSpec((tm, 1024), lambda i: (i, 0)), row, row,
                  pl.BlockSpec((tm, 512), lambda i: (i, C_RG // 512)),
                  pl.BlockSpec((tm, 512), lambda i: (i, C_GR // 512)), vec, vec],
        out_specs=(row, row, pl.BlockSpec((tm, P_GATE), lambda i: (i, 0)), vec, vec),
        compiler_params=_cp("arbitrary"), name=name)(dm, o_ret, o_gla, proj, proj, w_ret, w_gla)


def _dot(a, b):
    return lax.dot_general(a, b, (((1,), (0,)), ((), ())), preferred_element_type=F32)


def _dot_nt(a, b):
    return lax.dot_general(a, b, (((1,), (1,)), ((), ())), preferred_element_type=F32)


def _dot_tn(a, b):
    return lax.dot_general(a, b, (((0,), (0,)), ((), ())), preferred_element_type=F32)


def _ret_tables(lp):
    cr = RET_CHUNK
    pos = jnp.arange(lp, dtype=F32) - float(PADF)
    half = RET_DK // 2
    inv = ROPE_BASE ** (-jnp.arange(half, dtype=F32) / half)
    ang = pos[:, None] * inv[None, :]
    c, s = jnp.cos(ang), jnp.sin(ang)
    rope_c = jnp.concatenate([c, c], axis=1)
    rope_s = jnp.concatenate([-s, s], axis=1)
    log_g = np.log(1.0 - 2.0 ** (-5.0 - np.arange(RET_HEADS, dtype=np.float64)))
    idx = np.arange(cr, dtype=np.float64)
    diff = idx[:, None] - idx[None, :]
    dmat = np.where(diff >= 0, np.exp(log_g[:, None, None] * np.maximum(diff, 0.0)), 0.0)
    zeta = np.exp(log_g[:, None] * (cr - 1.0 - idx)[None, :])
    xi = np.exp(log_g[:, None] * (idx + 1.0)[None, :])
    gc = np.exp(log_g * cr)
    f = lambda a: jnp.asarray(a.astype(np.float32))
    return (rope_c, rope_s, f(dmat), f(np.broadcast_to(zeta[:, :, None], (RET_HEADS, cr, 128))),
            f(np.broadcast_to(xi[:, :, None], (RET_HEADS, cr, 128))),
            f(np.broadcast_to(gc[:, None, None], (RET_HEADS, 8, 128))))


def _rope(t, c, s):
    return t * c + pltpu.roll(t, 64, 1) * s


def _rope_t(d, c, s):
    return d * c + pltpu.roll(d * s, 64, 1)


def _ret_specs(nblk, rev):
    ix = (lambda i: nblk - 1 - i) if rev else (lambda i: i)
    cr = RET_CHUNK
    col = lambda base: pl.BlockSpec((BLK, 512), lambda i: (ix(i), base // 512))
    tab = pl.BlockSpec((BLK, 128), lambda i: (ix(i), 0))
    sq = pl.BlockSpec((RET_HEADS, cr, cr), lambda i: (0, 0, 0))
    hv = pl.BlockSpec((RET_HEADS, cr, 128), lambda i: (0, 0, 0))
    g8 = pl.BlockSpec((RET_HEADS, 8, 128), lambda i: (0, 0, 0))
    st = pl.BlockSpec((RET_HEADS, BLK // cr, 128, 128), lambda i: (0, ix(i), 0, 0))
    out = pl.BlockSpec((BLK, 512), lambda i: (ix(i), 0))
    return col, tab, sq, hv, g8, st, out


def _retention(proj, tables, *, name):
    lp = proj.shape[0]
    nblk, cr = lp // BLK, RET_CHUNK
    scale = RET_DK ** -0.5

    def body(q_ref, k_ref, v_ref, c_ref, s_ref, d_ref, z_ref, x_ref, g_ref, o_ref, st_ref, state):
        @pl.when(pl.program_id(0) == 0)
        def _():
            state[...] = jnp.zeros_like(state)

        def chunk(ci, carry):
            sl = pl.ds(pl.multiple_of(ci * cr, cr), cr)
            c, s = c_ref[sl, :], s_ref[sl, :]
            for h in range(RET_HEADS):
                hs = slice(128 * h, 128 * h + 128)
                q = _rope(q_ref[sl, hs], c, s)
                k = _rope(k_ref[sl, hs], c, s) * scale
                qb, kb, vb = q.astype(BF16), k.astype(BF16), v_ref[sl, hs].astype(BF16)
                st = state[h]
                st_ref[h, ci] = st
                sc = _dot_nt(qb, kb) * d_ref[h]
                o_ref[sl, hs] = _dot(sc.astype(BF16), vb) + _dot(qb, st.astype(BF16)) * x_ref[h]
                state[h] = st * g_ref[h][0:1, :] + _dot_tn((k * z_ref[h]).astype(BF16), vb)
            return carry

        lax.fori_loop(0, BLK // cr, chunk, 0)

    col, tab, sq, hv, g8, st, out = _ret_specs(nblk, False)
    return pl.pallas_call(
        body,
        out_shape=(jax.ShapeDtypeStruct((lp, 512), F32), jax.ShapeDtypeStruct((4, lp // cr, 128, 128), F32)),
        grid=(nblk,), in_specs=[col(C_RQ), col(C_RK), col(C_RV), tab, tab, sq, hv, hv, g8],
        out_specs=(out, st), scratch_shapes=[pltpu.VMEM((RET_HEADS, 128, 128), F32)],
        compiler_params=_cp("arbitrary"), name=name)(proj, proj, proj, *tables)


def _retention_bwd(proj, do, states, tables, *, name):
    lp = proj.shape[0]
    nblk, cr = lp // BLK, RET_CHUNK
    nch = BLK // cr
    scale = RET_DK ** -0.5

    def body(q_ref, k_ref, v_ref, do_ref, st_ref, c_ref, s_ref, d_ref, z_ref, x_ref, g_ref, dqkv_ref, dstate):
        @pl.when(pl.program_id(0) == 0)
        def _():
            dstate[...] = jnp.zeros_like(dstate)

        def chunk(cc, carry):
            ci = nch - 1 - cc
            sl = pl.ds(pl.multiple_of(ci * cr, cr), cr)
            c, s = c_ref[sl, :], s_ref[sl, :]
            for h in range(RET_HEADS):
                hs = slice(128 * h, 128 * h + 128)
                dmat, zeta, xi = d_ref[h], z_ref[h], x_ref[h]
                q = _rope(q_ref[sl, hs], c, s)
                k = _rope(k_ref[sl, hs], c, s) * scale
                qb, kb, vb = q.astype(BF16), k.astype(BF16), v_ref[sl, hs].astype(BF16)
                kzb = (k * zeta).astype(BF16)
                dov = do_ref[sl, hs]
                dob, doxb = dov.astype(BF16), (dov * xi).astype(BF16)
                stb = st_ref[h, ci].astype(BF16)
                dsn = dstate[h]
                dsnb = dsn.astype(BF16)
                scb = (_dot_nt(qb, kb) * dmat).astype(BF16)
                dscb = (_dot_nt(dob, vb) * dmat).astype(BF16)
                dq = _dot(dscb, kb) + _dot_nt(doxb, stb)
                dk = _dot_tn(dscb, qb) + _dot_nt(vb, dsnb) * zeta
                dv = _dot_tn(scb, dob) + _dot(kzb, dsnb)
                dstate[h] = dsn * g_ref[h][0:1, :] + _dot_tn(qb, doxb)
                dqkv_ref[sl, 128 * h:128 * h + 128] = _rope_t(dq, c, s).astype(BF16)
                dqkv_ref[sl, 512 + 128 * h:640 + 128 * h] = _rope_t(dk * scale, c, s).astype(BF16)
                dqkv_ref[sl, 1024 + 128 * h:1152 + 128 * h] = dv.astype(BF16)
            return carry

        lax.fori_loop(0, nch, chunk, 0)

    col, tab, sq, hv, g8, st, out = _ret_specs(nblk, True)
    return pl.pallas_call(
        body, out_shape=jax.ShapeDtypeStruct((lp, P_RET), BF16), grid=(nblk,),
        in_specs=[col(C_RQ), col(C_RK), col(C_RV), out, st, tab, tab, sq, hv, hv, g8],
        out_specs=pl.BlockSpec((BLK, P_RET), lambda i: (nblk - 1 - i, 0)),
        scratch_shapes=[pltpu.VMEM((RET_HEADS, 128, 128), F32)],
        compiler_params=_cp("arbitrary"), name=name)(proj, proj, proj, do, states, *tables)


def _gla_tables():
    c = GLA_CHUNK
    tri = np.tril(np.ones((c, c), np.float32))
    ones_qv = np.kron(np.eye(GLA_HEADS, dtype=np.float32), np.ones((GLA_DK, GLA_DV), np.float32))
    return (jnp.asarray(tri, BF16), jnp.asarray(tri.T.copy(), BF16), jnp.asarray(ones_qv, BF16),
            jnp.asarray(ones_qv.T.copy(), BF16))


def _split3(x):
    hi = x.astype(BF16)
    r1 = x - hi.astype(F32)
    mid = r1.astype(BF16)
    lo = (r1 - mid.astype(F32)).astype(BF16)
    return hi, mid, lo


def _tri_sum(tri, x):
    hi, mid, lo = _split3(x)
    return _dot(tri, hi) + _dot(tri, mid) + _dot(tri, lo)


def _head_masks(width, per):
    lane = lax.broadcasted_iota(jnp.int32, (1, width), 1)
    return [((lane >= per * h) & (lane < per * (h + 1))).astype(F32) for h in range(GLA_HEADS)]


def _stack_heads(x, masks):
    return jnp.concatenate([x * m for m in masks], axis=0)


def _gla_gate(ga, w2, b, ok, tri):
    z = _dot(ga.astype(BF16), w2) + b
    la = (jnp.minimum(z, 0.0) - jnp.log(1.0 + jnp.exp(-jnp.abs(z)))) * (1.0 / GLA_TAU)
    la = jnp.where(ok, la, 0.0)
    return z, _tri_sum(tri, la)


def _gla_rows(i_blk, ci, lp):
    c = GLA_CHUNK
    rows = i_blk * BLK + ci * c + lax.broadcasted_iota(jnp.int32, (c, 1), 0)
    return (rows >= PADF) & (rows < lp - BACK)


def _gla_off_parts(a, qs, k, g, hm_q):
    s = GLA_SUB
    ra = g[s * a - 1:s * a, :]
    ga_ = g[s * a:s * a + s, :]
    eq = jnp.exp(ga_ - ra)
    ek = jnp.exp(jnp.minimum(ra - g, 0.0))
    qh = qs[s * a:s * a + s, :] * eq
    kh = k * ek
    qst = _stack_heads(qh, hm_q).astype(BF16)
    col = lax.broadcasted_iota(jnp.int32, (GLA_HEADS * s, GLA_CHUNK), 1)
    pmask = col < s * a
    p = jnp.where(pmask, _dot_nt(qst, kh.astype(BF16)), 0.0)
    return eq, ek, qh, kh, qst, pmask, p


def _lag_mask(j):
    r = lax.broadcasted_iota(jnp.int32, (GLA_CHUNK, 1), 0)
    return (jnp.bitwise_and(r, GLA_SUB - 1) >= j).astype(F32)


def _roll_rows(x, j):
    return x if j == 0 else pltpu.roll(x, j, 0)


def _gla(proj, w2p, b, tables, *, name):
    lp = proj.shape[0]
    nblk, c, s = lp // BLK, GLA_CHUNK, GLA_SUB
    nch = BLK // c
    na = c // s

    def body(q_ref, k_ref, v_ref, a_ref, w_ref, b_ref, tri_ref, ones_ref, o_ref, st_ref, state):
        i_blk = pl.program_id(0)

        @pl.when(i_blk == 0)
        def _():
            state[...] = jnp.zeros_like(state)
        hm_q = _head_masks(GLA_QK, GLA_DK)
        tri, ones_qv, w2, bias = tri_ref[...], ones_ref[...], w_ref[...], b_ref[...]

        def chunk(ci, carry):
            sl = pl.ds(pl.multiple_of(ci * c, c), c)
            ok = _gla_rows(i_blk, ci, lp)
            k, v = k_ref[sl, :], v_ref[sl, :]
            vb = v.astype(BF16)
            qs = q_ref[sl, :] * (GLA_DK ** -0.5)
            _, g = _gla_gate(a_ref[sl, :], w2, bias, ok, tri)
            last = g[c - 1:c, :]
            st = state[...]
            st_ref[ci] = st
            qst = _stack_heads(qs * jnp.exp(g), hm_q).astype(BF16)
            oi = _dot_nt(qst, st.astype(BF16))
            o = jnp.concatenate([oi[c * h:c * h + c, :] for h in range(GLA_HEADS)], axis=1)
            ke = k * jnp.exp(last - g)
            f = _dot_tn(vb, ke.astype(BF16))
            upd = f[0:GLA_DV, :] * hm_q[0]
            for h in range(1, GLA_HEADS):
                upd = upd + f[GLA_DV * h:GLA_DV * (h + 1), :] * hm_q[h]
            state[...] = st * jnp.exp(last) + upd
            off = [jnp.zeros((s, GLA_V), F32)]
            for a in range(1, na):
                p = _gla_off_parts(a, qs, k, g, hm_q)[-1]
                ob = _dot(p.astype(BF16), vb)
                off.append(jnp.concatenate(
                    [ob[s * h:s * h + s, GLA_DV * h:GLA_DV * (h + 1)] for h in range(GLA_HEADS)], axis=1))
            o = o + jnp.concatenate(off, axis=0)
            ws = []
            for j in range(s):
                ej = jnp.exp(jnp.minimum(g - _roll_rows(g, j), 0.0))
                ws.append((qs * _roll_rows(k, j) * ej * _lag_mask(j)).astype(BF16))
            ball = _dot(jnp.concatenate(ws, axis=0), ones_qv)
            for j in range(s):
                o = o + ball[c * j:c * j + c, :] * _roll_rows(v, j)
            o_ref[sl, :] = o
            return carry

        lax.fori_loop(0, nch, chunk, 0)

    tri, _, ones_qv, _ = tables
    full = lambda arr: pl.BlockSpec(arr.shape, lambda i: (0,) * arr.ndim)
    return pl.pallas_call(
        body,
        out_shape=(jax.ShapeDtypeStruct((lp, GLA_V), F32), jax.ShapeDtypeStruct((lp // c, GLA_DV, GLA_QK), F32)),
        grid=(nblk,),
        in_specs=[pl.BlockSpec((BLK, GLA_QK), lambda i: (i, C_GQ // GLA_QK)),
                  pl.BlockSpec((BLK, GLA_QK), lambda i: (i, C_GK // GLA_QK)),
                  pl.BlockSpec((BLK, GLA_V), lambda i: (i, C_GV // GLA_V)),
                  pl.BlockSpec((BLK, 128), lambda i: (i, C_GA // 128)),
                  full(w2p), full(b), full(tri), full(ones_qv)],
        out_specs=(pl.BlockSpec((BLK, GLA_V), lambda i: (i, 0)),
                   pl.BlockSpec((nch, GLA_DV, GLA_QK), lambda i: (i, 0, 0))),
        scratch_shapes=[pltpu.VMEM((GLA_DV, GLA_QK), F32)],
        compiler_params=_cp("arbitrary"), name=name)(proj, proj, proj, proj, w2p, b, tri, ones_qv)


def _gla_bwd(proj, do, states, w2p, b, tables, *, name):
    lp = proj.shape[0]
    nblk, c, s = lp // BLK, GLA_CHUNK, GLA_SUB
    nch = BLK // c
    na = c // s

    def body(q_ref, k_ref, v_ref, a_ref, do_ref, st_ref, w_ref, b_ref, tri_ref, trit_ref, ones_ref, onest_ref,
             dp_ref, dw_ref, db_ref, dstate, dqs_s, dk_s, dg_s, dv_s):
        i_blk = nblk - 1 - pl.program_id(0)

        @pl.when(pl.program_id(0) == 0)
        def _():
            dstate[...] = jnp.zeros_like(dstate)
            dw_ref[...] = jnp.zeros_like(dw_ref)
            db_ref[...] = jnp.zeros_like(db_ref)
        hm_q = _head_masks(GLA_QK, GLA_DK)
        hm_v = _head_masks(GLA_V, GLA_DV)
        tri, trit, ones_qv, ones_vq = tri_ref[...], trit_ref[...], ones_ref[...], onest_ref[...]
        w2, bias = w_ref[...], b_ref[...]
        rsum = lambda x: jnp.sum(x, axis=0, keepdims=True)

        def chunk(cc, carry):
            ci = nch - 1 - cc
            sl = pl.ds(pl.multiple_of(ci * c, c), c)
            ok = _gla_rows(i_blk, ci, lp)
            k, v, ga = k_ref[sl, :], v_ref[sl, :], a_ref[sl, :]
            vb = v.astype(BF16)
            qs = q_ref[sl, :] * (GLA_DK ** -0.5)
            z, g = _gla_gate(ga, w2, bias, ok, tri)
            last = g[c - 1:c, :]
            elast = jnp.exp(last)
            eg = jnp.exp(g)
            ekl = jnp.exp(last - g)
            qe, ke = qs * eg, k * ekl
            dov = do_ref[sl, :]
            st = st_ref[ci]
            dsn = dstate[...]
            qst = _stack_heads(qe, hm_q).astype(BF16)
            dost = jnp.concatenate([dov[:, GLA_DV * h:GLA_DV * (h + 1)] for h in range(GLA_HEADS)], axis=0).astype(BF16)
            dqe_st = _dot(dost, st.astype(BF16))
            dqe = dqe_st[0:c, :] * hm_q[0]
            for h in range(1, GLA_HEADS):
                dqe = dqe + dqe_st[c * h:c * h + c, :] * hm_q[h]
            dstate[...] = _dot_tn(dost, qst) + dsn * elast
            dlast = rsum(dsn * st) * elast
            df = _stack_heads(dsn, hm_q).astype(BF16)
            dv_s[...] = _dot_nt(ke.astype(BF16), df)
            dke = _dot(vb, df)
            xk = dke * ke
            dqs_s[...] = dqe * eg
            dk_s[...] = dke * ekl
            dg_s[...] = dqe * qe - xk
            dlast = dlast + rsum(xk)
            for a in range(1, na):
                eq, ek, qh, kh, qsa, pmask, p = _gla_off_parts(a, qs, k, g, hm_q)
                rows = slice(s * a, s * a + s)
                dofull = _stack_heads(dov[rows, :], hm_v).astype(BF16)
                dp = jnp.where(pmask, _dot_nt(dofull, vb), 0.0).astype(BF16)
                dv_s[...] += _dot_tn(p.astype(BF16), dofull)
                dq_st = _dot(dp, kh.astype(BF16))
                dqh = dq_st[0:s, :] * hm_q[0]
                for h in range(1, GLA_HEADS):
                    dqh = dqh + dq_st[s * h:s * h + s, :] * hm_q[h]
                dkh = _dot_tn(dp, qsa)
                xq = dqh * qh
                xkh = dkh * kh
                dqs_s[rows, :] += dqh * eq
                dg_s[rows, :] += xq
                dk_s[...] += dkh * ek
                dg_s[...] -= xkh
                dg_s[s * a - 1:s * a, :] += rsum(xkh) - rsum(xq)
            kes, qes, ws, dbs = [], [], [], []
            for j in range(s):
                em = jnp.exp(jnp.minimum(g - _roll_rows(g, j), 0.0)) * _lag_mask(j)
                kes.append(_roll_rows(k, j) * em)
                qes.append(qs * em)
                ws.append((qs * kes[j]).astype(BF16))
                dbs.append((dov * _roll_rows(v, j)).astype(BF16))
            ball = _dot(jnp.concatenate(ws, axis=0), ones_qv)
            dwall = _dot(jnp.concatenate(dbs, axis=0), ones_vq)
            for j in range(s):
                back = (lambda x: x) if j == 0 else (lambda x, j=j: pltpu.roll(x, c - j, 0))
                dw = dwall[c * j:c * j + c, :]
                dv_s[...] += back(ball[c * j:c * j + c, :] * dov)
                dqs_s[...] += dw * kes[j]
                dk_s[...] += back(dw * qes[j])
                x = dw * qs * kes[j]
                dg_s[...] += x - back(x)
            dg_s[c - 1:c, :] += dlast
            dla = jnp.where(ok, _tri_sum(trit, dg_s[...]), 0.0)
            dz = dla * (1.0 / GLA_TAU) / (1.0 + jnp.exp(z))
            dzb = dz.astype(BF16)
            dp_ref[sl, 0:256] = (dqs_s[...] * (GLA_DK ** -0.5)).astype(BF16)
            dp_ref[sl, 256:512] = dk_s[...].astype(BF16)
            dp_ref[sl, 512:1024] = dv_s[...].astype(BF16)
            dp_ref[sl, 1024:1152] = _dot_nt(dzb, w2).astype(BF16)
            dp_ref[sl, 1152:1280] = jnp.zeros((c, 128), BF16)
            dw_ref[...] += _dot_tn(ga.astype(BF16), dzb)
            db_ref[...] += rsum(dz)
            return carry

        lax.fori_loop(0, nch, chunk, 0)

    tri, trit, ones_qv, ones_vq = tables
    full = lambda arr: pl.BlockSpec(arr.shape, lambda i: (0,) * arr.ndim)
    rev = lambda i: nblk - 1 - i
    return pl.pallas_call(
        body,
        out_shape=(jax.ShapeDtypeStruct((lp, P_GLA), BF16),
                   jax.ShapeDtypeStruct((128, GLA_QK), F32), jax.ShapeDtypeStruct((1, GLA_QK), F32)),
        grid=(nblk,),
        in_specs=[pl.BlockSpec((BLK, GLA_QK), lambda i: (rev(i), C_GQ // GLA_QK)),
                  pl.BlockSpec((BLK, GLA_QK), lambda i: (rev(i), C_GK // GLA_QK)),
                  pl.BlockSpec((BLK, GLA_V), lambda i: (rev(i), C_GV // GLA_V)),
                  pl.BlockSpec((BLK, 128), lambda i: (rev(i), C_GA // 128)),
                  pl.BlockSpec((BLK, GLA_V), lambda i: (rev(i), 0)),
                  pl.BlockSpec((nch, GLA_DV, GLA_QK), lambda i: (rev(i), 0, 0)),
                  full(w2p), full(b), full(tri), full(trit), full(ones_qv), full(ones_vq)],
        out_specs=(pl.BlockSpec((BLK, P_GLA), lambda i: (rev(i), 0)),
                   pl.BlockSpec((128, GLA_QK), lambda i: (0, 0)),
                   pl.BlockSpec((1, GLA_QK), lambda i: (0, 0))),
        scratch_shapes=[pltpu.VMEM((GLA_DV, GLA_QK), F32), pltpu.VMEM((c, GLA_QK), F32),
                        pltpu.VMEM((c, GLA_QK), F32), pltpu.VMEM((c, GLA_QK), F32), pltpu.VMEM((c, GLA_V), F32)],
        compiler_params=_cp("arbitrary"), name=name)(proj, proj, proj, proj, do, states, w2p, b, tri, trit, ones_qv, ones_vq)


def _as2d(a):
    return a.reshape(-1, a.shape[-1])


def _ew_tile(r):
    return _tile(r, (512, 256, 128, 64, 32, 16, 8))


def _add2(a, b, *, out_dtype, name):
    a2, b2 = _as2d(a), _as2d(b)
    r, n = a2.shape
    tm = _ew_tile(r)

    def body(a_ref, b_ref, o_ref):
        o_ref[...] = (a_ref[...] + b_ref[...]).astype(o_ref.dtype)

    blk = pl.BlockSpec((tm, n), lambda i: (i, 0))
    return pl.pallas_call(body, out_shape=jax.ShapeDtypeStruct((r, n), out_dtype), grid=(r // tm,), in_specs=[blk, blk],
                          out_specs=blk, compiler_params=_cp("parallel"), name=name)(a2, b2).reshape(a.shape)


def _sum_slots(own, q, *, name):
    shape = own.shape
    q3 = q.reshape(3, -1, shape[-1])
    own2 = _as2d(own)
    r, n = own2.shape
    tm = _ew_tile(r)

    def body(own_ref, q_ref, o_ref):
        f = lambda i: q_ref[i].astype(F32)
        o_ref[...] = ((own_ref[...].astype(F32) + f(0)) + f(1)) + f(2)

    blk = pl.BlockSpec((tm, n), lambda i: (i, 0))
    return pl.pallas_call(
        body, out_shape=jax.ShapeDtypeStruct((r, n), F32), grid=(r // tm,),
        in_specs=[blk, pl.BlockSpec((3, tm, n), lambda i: (0, i, 0))], out_specs=blk,
        compiler_params=_cp("parallel"), name=name)(own2, q3).reshape(shape)


def _adamw(w, g, m, v, *, name):
    shape = w.shape
    w2, g2, m2, v2 = _as2d(w), _as2d(g), _as2d(m), _as2d(v)
    r, n = w2.shape
    tm = _ew_tile(r)
    c1 = 1.0 - ADAM_B1 ** ADAM_STEP
    c2 = 1.0 - ADAM_B2 ** ADAM_STEP

    def body(w_ref, g_ref, m_ref, v_ref, d_ref, mo_ref, vo_ref):
        gv = g_ref[...]
        mn = ADAM_B1 * m_ref[...] + (1.0 - ADAM_B1) * gv
        vn = ADAM_B2 * v_ref[...] + (1.0 - ADAM_B2) * (gv * gv)
        mo_ref[...] = mn
        vo_ref[...] = vn
        d_ref[...] = -ADAM_LR * ((mn / c1) / (jnp.sqrt(vn / c2) + ADAM_EPS) + ADAM_WD * w_ref[...])

    blk = pl.BlockSpec((tm, n), lambda i: (i, 0))
    o = jax.ShapeDtypeStruct((r, n), F32)
    d, mo, vo = pl.pallas_call(body, out_shape=(o, o, o), grid=(r // tm,), in_specs=[blk] * 4, out_specs=(blk,) * 3,
                               compiler_params=_cp("parallel"), name=name)(w2, g2, m2, v2)
    return d.reshape(shape), mo.reshape(shape), vo.reshape(shape)


ANY = pl.BlockSpec(memory_space=pl.ANY)


def _place():
    return lax.axis_index("x"), lax.axis_index("y"), lax.axis_index("c")


def _other_chips(x, y):
    return [(1 - x, y), (x, 1 - y), (1 - x, 1 - y)]


def _remote(src, dst, ssem, rsem, dev):
    return pltpu.make_async_remote_copy(src_ref=src, dst_ref=dst, send_sem=ssem, recv_sem=rsem, device_id=dev,
                                        device_id_type=MESH)


def _allgather_chips(arrs, *, name):
    n = len(arrs)

    def body(*refs):
        ins, outs = refs[:n], refs[n:2 * n]
        s1, r1, s2, r2 = refs[2 * n:]
        x, y, c = _place()
        q = 2 * x + y
        chips = _other_chips(x, y)
        qs = [2 * cx + cy for cx, cy in chips]
        sib = (x, y, 1 - c)
        first, passed = [], []
        for k in range(n):
            for j, chip in enumerate(chips):
                first.append(_remote(ins[k].at[c], outs[k].at[c, q], s1.at[k, j], r1.at[k, j], (*chip, c)))
        for cp in first:
            cp.start()
        for k in range(n):
            for j, chip in enumerate(chips):
                land = outs[k].at[c, qs[j]]
                _remote(land, land, s1.at[k, j], r1.at[k, j], (*chip, c)).wait_recv()
                fw = _remote(land, land, s2.at[k, j], r2.at[k, j], sib)
                fw.start()
                passed.append(fw)
        for k in range(n):
            for j in range(3):
                land = outs[k].at[1 - c, qs[j]]
                _remote(land, land, s2.at[k, j], r2.at[k, j], sib).wait_recv()
        for cp in first + passed:
            cp.wait_send()

    sem = pltpu.SemaphoreType.DMA
    outs = pl.pallas_call(
        body, out_shape=tuple(jax.ShapeDtypeStruct((2, 4) + a.shape[1:], a.dtype) for a in arrs),
        in_specs=[ANY] * n, out_specs=(ANY,) * n,
        scratch_shapes=[sem((n, 3)), sem((n, 3)), sem((n, 3)), sem((n, 3))], name=name)(*arrs)
    chip = 2 * lax.axis_index("x") + lax.axis_index("y")
    return [lax.dynamic_update_slice_in_dim(o, a[:, None], chip, axis=1) for o, a in zip(outs, arrs)]


def _pair_exchange(arrs, *, name):
    n = len(arrs)

    def body(*refs):
        ins, outs = refs[:n], refs[n:2 * n]
        ssem, rsem = refs[2 * n:]
        x, y, c = _place()
        cps = [_remote(ins[k].at[:, 1 - c], outs[k], ssem.at[k], rsem.at[k], (x, y, 1 - c)) for k in range(n)]
        for cp in cps:
            cp.start()
        for cp in cps:
            cp.wait()

    sem = pltpu.SemaphoreType.DMA
    return pl.pallas_call(
        body, out_shape=tuple(jax.ShapeDtypeStruct((a.shape[0],) + a.shape[2:], a.dtype) for a in arrs),
        in_specs=[ANY] * n, out_specs=(ANY,) * n, scratch_shapes=[sem((n,)), sem((n,))], name=name)(*arrs)


def _pair_sum(mine, theirs, c, *, name):
    _, _, r, n = mine.shape
    tm = r if r <= 512 else _ew_tile(r)

    def body(c_ref, a_ref, b_ref, o_ref):
        o_ref[...] = (a_ref[...] + b_ref[...]).astype(BF16)

    blk = pl.BlockSpec((None, tm, n), lambda s, i, c_ref: (s, i, 0))
    return pl.pallas_call(
        body, out_shape=jax.ShapeDtypeStruct((4, r, n), BF16),
        grid_spec=pltpu.PrefetchScalarGridSpec(
            num_scalar_prefetch=1, grid=(4, r // tm),
            in_specs=[pl.BlockSpec((None, None, tm, n), lambda s, i, c_ref: (s, c_ref[0], i, 0)), blk], out_specs=blk),
        compiler_params=_cp("parallel", "parallel"), name=name)(jnp.reshape(c, (1,)).astype(jnp.int32), mine, theirs)


def _chip_copies(ins, outs, ssem, rsem, mode):
    x, y, c = _place()
    cps = []
    for k in range(len(ins)):
        for j, (cx, cy) in enumerate(_other_chips(x, y)):
            src = ins[k].at[2 * cx + cy] if mode == "scatter" else ins[k].at[c]
            cps.append(_remote(src, outs[k].at[j], ssem.at[k, j], rsem.at[k, j], (cx, cy, c)))
    return cps


def _landing_shape(a):
    return jax.ShapeDtypeStruct((3,) + a.shape[1:], a.dtype)


def _chip_exchange(arrs, mode, *, name):
    n = len(arrs)

    def body(*refs):
        ins, outs = refs[:n], refs[n:2 * n]
        ssem, rsem = refs[2 * n:]
        cps = _chip_copies(ins, outs, ssem, rsem, mode)
        for cp in cps:
            cp.start()
        for cp in cps:
            cp.wait()

    sem = pltpu.SemaphoreType.DMA
    return list(pl.pallas_call(
        body, out_shape=tuple(_landing_shape(a) for a in arrs),
        in_specs=[ANY] * n, out_specs=(ANY,) * n, scratch_shapes=[sem((n, 3)), sem((n, 3))], name=name)(*arrs))


def _pair_swap(arrs, *, name):
    n = len(arrs)

    def body(*refs):
        ins, outs = refs[:n], refs[n:2 * n]
        ssem, rsem = refs[2 * n:]
        x, y, c = _place()
        cps = [_remote(ins[k], outs[k], ssem.at[k], rsem.at[k], (x, y, 1 - c)) for k in range(n)]
        for cp in cps:
            cp.start()
        for cp in cps:
            cp.wait()

    sem = pltpu.SemaphoreType.DMA
    return pl.pallas_call(
        body, out_shape=tuple(jax.ShapeDtypeStruct(a.shape, a.dtype) for a in arrs),
        in_specs=[ANY] * n, out_specs=(ANY,) * n, scratch_shapes=[sem((n,)), sem((n,))], name=name)(*arrs)


def _allreduce_small(slab, *, name):
    r, n = slab.shape

    def body(x_ref, o_ref, buf, ssem, rsem):
        x, y, c = _place()
        me = 4 * x + 2 * y + c
        buf[me] = x_ref[...]
        cps = []
        for rel in range(1, 8):
            bx, by, bc = (rel >> 2) & 1, (rel >> 1) & 1, rel & 1
            px, py, pc = (x + bx) % 2, (y + by) % 2, (c + bc) % 2
            cps.append((_remote(x_ref, buf.at[me], ssem.at[rel - 1], rsem.at[rel - 1], (px, py, pc)),
                        4 * px + 2 * py + pc, (px, py, pc)))
        for cp, _, _ in cps:
            cp.start()
        for rel, (cp, peer, dev) in enumerate(cps):
            cp.wait_send()
            _remote(x_ref, buf.at[peer], ssem.at[rel], rsem.at[rel], dev).wait_recv()
        acc = buf[0]
        for k in range(1, 8):
            acc = acc + buf[k]
        o_ref[...] = acc

    vm = pl.BlockSpec(memory_space=pltpu.VMEM)
    sem = pltpu.SemaphoreType.DMA
    return pl.pallas_call(
        body, out_shape=jax.ShapeDtypeStruct((r, n), F32), in_specs=[vm], out_specs=vm,
        scratch_shapes=[pltpu.VMEM((8, r, n), F32), sem((7,)), sem((7,))], name=name)(slab)


def _slab(arrs, row_mult):
    flat = jnp.concatenate([a.reshape(-1) for a in arrs])
    unit = 128 * row_mult
    total = -(-flat.size // unit) * unit
    return jnp.pad(flat, (0, total - flat.size)).reshape(-1, 128)


def _unslab(slab, shapes):
    flat = slab.reshape(-1)
    out, off = [], 0
    for s in shapes:
        size = int(np.prod(s))
        out.append(flat[off:off + size].reshape(s))
        off += size
    return out


def _cols_from_chips(a):
    return jnp.transpose(a, (1, 0, 2)).reshape(a.shape[1], -1)


def _cols_to_chips(a, parts):
    r = a.shape[0]
    return jnp.transpose(a.reshape(r, parts, -1), (1, 0, 2))


BIG = ("w_in", "w_out", "up", "down")
GATHER_RIDES = {("proj", 0): (("w_out", 0), ("up", 0)), ("mix_out", 0): (("down", 0),),
                ("ffn_up_a", 0): (("w_in", 1), ("w_out", 1)), ("ffn_up_g", 0): (("up", 1),),
                ("ffn_down", 0): (("down", 1),)}
REDUCE_RIDES = {("ffn_down_dx", 0): ("up",), ("ffn_up_a_dx", 0): ("w_in", "w_out"), ("ffn_up_g_dx", 0): ("down",)}


class _LocalWeights:
    def __init__(self, meta, win, wout, up_a, up_g, down, w2p, cw):
        self._meta, self._w = meta, {"win": win, "wout": wout, "up_a": up_a, "up_g": up_g, "down": down, "w2p": w2p,
                                     "cw": cw}

    def meta(self):
        return self._meta

    def get(self, kind, l):
        return self._w[kind][l]

    def mm(self, site, l, a, b, **kw):
        return _mm(a, b, name=site, **kw)

    def grads_done(self, l, g):
        pass


class _ChipWeights:
    def __init__(self, w_in, w_out, ffn_up, ffn_down, meta_tokens, gla_gate_w2, ffn_conv_w):
        self.x, self.y, self.c = _place()
        self.q = 2 * self.x + self.y
        halves = lambda a: a.astype(BF16).reshape(2, a.shape[0] // 2, a.shape[1])
        self.own = {(k, l): halves(a[l]) for k, a in zip(BIG, (w_in, w_out, ffn_up, ffn_down)) for l in range(DEPTH)}
        self.landed, self.swapped, self.full, self.n_swaps = {}, {}, {}, 0
        self.sh_shapes = [meta_tokens.shape, gla_gate_w2.shape, ffn_conv_w.shape]
        self.own["small", 0] = _slab([meta_tokens, gla_gate_w2, ffn_conv_w], 16).reshape(2, -1, 128)
        first = [("w_in", 0), ("small", 0)]
        for key, arr in zip(first, _chip_exchange([self.own[k] for k in first], "bcast", name="gather_first")):
            self.landed[key] = arr
        sh = self._whole("small", 0).reshape(4, -1, 128)
        parts = [_unslab(sh[k], self.sh_shapes) for k in range(4)]
        self._meta = jnp.concatenate([p[0] for p in parts], axis=-1)
        self.w2 = jnp.concatenate([p[1] for p in parts], axis=-1)
        self.cw = jnp.concatenate([p[2] for p in parts], axis=-1)
        self.partial, self.slots = {}, {}

    def _whole(self, kind, l):
        if (kind, l) not in self.full:
            if (kind, l) not in self.swapped:
                keys = [k for k in self.landed if k not in self.swapped]
                got = _pair_swap([self.landed[k] for k in keys], name=f"gather_swap_{self.n_swaps}")
                self.n_swaps += 1
                self.swapped.update(zip(keys, got))
            own, land, swap = self.own[kind, l], self.landed[kind, l], self.swapped[kind, l]
            pieces = jnp.concatenate([own, jnp.stack([land, swap], axis=1).reshape((6,) + own.shape[1:])], axis=0)
            rel = jnp.bitwise_xor(jnp.arange(4, dtype=jnp.int32)[:, None], self.q)
            j = jnp.where(rel == 2, 0, jnp.where(rel == 1, 1, 2))
            h = jnp.arange(2, dtype=jnp.int32)[None, :]
            idx = jnp.where(rel == 0, h, 2 + 2 * j + (h != self.c).astype(jnp.int32))
            full = jnp.take(pieces, idx.reshape(8), axis=0)
            self.full[kind, l] = full.reshape(4, 2 * own.shape[1], own.shape[2])
        return self.full[kind, l]

    def meta(self):
        return self._meta

    def get(self, kind, l):
        if kind == "win":
            return _to_kernel_cols(_cols_from_chips(self._whole("w_in", l)))
        if kind == "wout":
            return self._whole("w_out", l).reshape(D_MODEL, D_MODEL)
        if kind == "up_a":
            return _cols_from_chips(self._whole("up", l)[0:2])
        if kind == "up_g":
            return _cols_from_chips(self._whole("up", l)[2:4])
        if kind == "down":
            return self._whole("down", l).reshape(D_FF, D_MODEL)
        if kind == "w2p":
            return jnp.pad(self.w2[l], ((0, 128 - GLA_RANK), (0, 0))).astype(BF16)
        return self.cw[l]

    def mm(self, site, l, a, b, **kw):
        if (site, l) in GATHER_RIDES:
            keys = GATHER_RIDES[site, l]
            out, got = _mm(a, b, name=site, carry=([self.own[k] for k in keys], "bcast"), **kw)
            self.landed.update(zip(keys, got))
            return out
        if (site, l) in REDUCE_RIDES and all((k, DEPTH - 1) in self.partial for k in REDUCE_RIDES[site, l]):
            keys = [(k, DEPTH - 1) for k in REDUCE_RIDES[site, l]]
            out, got = _mm(a, b, name=site, carry=([self.partial[k] for k in keys], "scatter"), **kw)
            self.slots.update(zip(keys, got))
            return out
        return _mm(a, b, name=site, **kw)

    def grads_done(self, l, g):
        split = lambda a: a.reshape(4, 2, a.shape[-2] // 2, a.shape[-1]) if a.ndim == 3 else \
            a.reshape(4, 2, a.shape[0] // 8, a.shape[1])
        big = {"w_in": split(_cols_to_chips(g["w_in"][l], 4)), "w_out": split(g["w_out"][l]),
               "up": split(jnp.concatenate([g["up_a"][l], g["up_g"][l]], axis=0)), "down": split(g["down"][l])}
        from_sib = _pair_exchange([big[k] for k in BIG], name=f"grads_pair_exchange_{l}")
        for k, theirs in zip(BIG, from_sib):
            self.partial[k, l] = _pair_sum(big[k], theirs, self.c, name=f"pair_sum_{k}_{l}")

    def reduce(self):
        keys = [(k, l) for l in range(DEPTH) for k in BIG]
        late = [k for k in keys if k not in self.slots]
        self.slots.update(zip(late, _chip_exchange([self.partial[k] for k in late], "scatter",
                                                   name="grads_chip_exchange")))
        half = {}
        for k in keys:
            own = lax.dynamic_index_in_dim(self.partial[k], self.q, 0, keepdims=False)
            half[k] = _sum_slots(own, self.slots[k], name=f"chip_sum_{k[0]}_{k[1]}")
        other = dict(zip(keys, _pair_swap([half[k] for k in keys], name="grads_pair_swap")))
        whole = lambda k: jnp.where(self.c == 0, jnp.concatenate([half[k], other[k]], axis=0),
                                    jnp.concatenate([other[k], half[k]], axis=0))
        return [jnp.stack([whole((k, l)) for l in range(DEPTH)]) for k in BIG]


def _local_step(x_rows, target_rows, wts, pre_mix_norm, gla_gate_b, ret_norm_w, gla_norm_w, post_mix_norm,
                pre_ffn_norm, ffn_conv_b, post_ffn_norm):
    d = D_MODEL
    lp = x_rows.shape[0] + FRONT + BACK
    row = lambda a, l: a[l][None, :]
    rtab = _ret_tables(lp)
    gtab = _gla_tables()
    h0 = jnp.concatenate([jnp.zeros((PADF, d), F32), wts.meta(), x_rows, jnp.zeros((BACK, d), F32)], axis=0)
    target = jnp.pad(target_rows, ((FRONT, BACK), (0, 0)))

    saved = []
    h = h0
    _, hn = _resid_norm(h0, None, None, row(pre_mix_norm, 0), name="norm_in")
    loss_local = dy = None
    for l in range(DEPTH):
        s = {"h_in": h, "hn": hn}
        s["proj"] = wts.mm("proj", l, hn, wts.get("win", l))
        s["o_ret"], s["st_ret"] = _retention(s["proj"], rtab, name="retention")
        s["o_gla"], s["st_gla"] = _gla(s["proj"], wts.get("w2p", l), row(gla_gate_b, l), gtab, name="gla")
        s["merged"] = _merge(s["o_ret"], s["o_gla"], s["proj"], row(ret_norm_w, l), row(gla_norm_w, l), name="merge")
        s["m"] = wts.mm("mix_out", l, s["merged"], wts.get("wout", l))
        s["h_mid"], s["hn2"] = _resid_norm(h, s["m"], row(post_mix_norm, l), row(pre_ffn_norm, l), name="resid_mix")
        s["ua"] = wts.mm("ffn_up_a", l, s["hn2"], wts.get("up_a", l))
        s["ug"] = wts.mm("ffn_up_g", l, s["hn2"], wts.get("up_g", l))
        cw_a, cw_g = wts.get("cw", l)[:, :D_FF], wts.get("cw", l)[:, D_FF:]
        cb_a, cb_g = ffn_conv_b[l][None, :D_FF], ffn_conv_b[l][None, D_FF:]
        s["conv"] = (cw_a, cw_g, cb_a, cb_g)
        s["act"] = _conv_act(s["ua"], s["ug"], cw_a, cw_g, cb_a, cb_g, name="conv_act")
        s["f"] = wts.mm("ffn_down", l, s["act"], wts.get("down", l))
        if l + 1 < DEPTH:
            h, hn = _resid_norm(s["h_mid"], s["f"], row(post_ffn_norm, l), row(pre_mix_norm, l + 1), name="resid_ffn")
        else:
            loss_local, dy = _loss_head(s["h_mid"], s["f"], row(post_ffn_norm, l), target, name="loss_head")
        saved.append(s)

    g = {k: [None] * DEPTH for k in ("pre_mix", "w_in", "w2", "gb", "ret_n", "gla_n", "w_out", "post_mix", "pre_ffn",
                                     "up_a", "up_g", "cw", "cb", "down", "post_ffn")}
    dh_out, dhn_next = dy, None
    for l in reversed(range(DEPTH)):
        s = saved[l]
        cw_a, cw_g, cb_a, cb_g = s["conv"]
        if l + 1 < DEPTH:
            dh, df, g["pre_mix"][l + 1], g["post_ffn"][l] = _resid_norm_bwd(
                dh_out, dhn_next, saved[l + 1]["h_in"], s["f"], row(pre_mix_norm, l + 1), row(post_ffn_norm, l),
                name="resid_ffn_bwd")
        else:
            dh, df, _, g["post_ffn"][l] = _resid_norm_bwd(dh_out, None, None, s["f"], None, row(post_ffn_norm, l),
                                                          name="loss_head_bwd")
        dact = wts.mm("ffn_down_dx", l, df, wts.get("down", l), nt=True)
        g["down"][l] = _mm_tn(s["act"], df, tn=512, name="ffn_down_dw")
        du_a, du_g, dcw_a, dcw_g, dcb_a, dcb_g = _conv_act_bwd(s["ua"], s["ug"], dact, cw_a, cw_g, cb_a, cb_g,
                                                               name="conv_act_bwd")
        g["cw"][l] = jnp.concatenate([dcw_a, dcw_g], axis=1)
        g["cb"][l] = jnp.concatenate([dcb_a, dcb_g], axis=1)[0]
        g["up_a"][l] = _mm_tn(s["hn2"], du_a, tn=D_FF // 2, blocked=True, name="ffn_up_a_dw")
        g["up_g"][l] = _mm_tn(s["hn2"], du_g, tn=D_FF // 2, blocked=True, name="ffn_up_g_dw")
        dhn2 = wts.mm("ffn_up_a_dx", l, du_a, wts.get("up_a", l), nt=True)
        dhn2 = wts.mm("ffn_up_g_dx", l, du_g, wts.get("up_g", l), nt=True, add=dhn2)
        dh, dm, g["pre_ffn"][l], g["post_mix"][l] = _resid_norm_bwd(
            dh, dhn2, s["h_mid"], s["m"], row(pre_ffn_norm, l), row(post_mix_norm, l), name="resid_mix_bwd")
        g["w_out"][l] = _mm_tn(s["merged"], dm, name="mix_out_dw")
        dmerged = wts.mm("mix_out_dx", l, dm, wts.get("wout", l), nt=True)
        do_ret, do_gla, d_gate, g["ret_n"][l], g["gla_n"][l] = _merge_bwd(
            dmerged, s["o_ret"], s["o_gla"], s["proj"], row(ret_norm_w, l), row(gla_norm_w, l), name="merge_bwd")
        d_ret = _retention_bwd(s["proj"], do_ret, s["st_ret"], rtab, name="retention_bwd")
        d_gla, dw2, dgb = _gla_bwd(s["proj"], do_gla, s["st_gla"], wts.get("w2p", l), row(gla_gate_b, l), gtab,
                                   name="gla_bwd")
        g["w2"][l], g["gb"][l] = dw2[:GLA_RANK], dgb[0]
        pieces = (d_ret, d_gate, d_gla)
        g["w_in"][l] = _to_reference_cols(*[_mm_tn(s["hn"], p, name=f"proj_dw_{i}") for i, p in enumerate(pieces)])
        win = wts.get("win", l)
        dhn_next = _mm_nt_sum(pieces, [win[:, 0:P_RET], win[:, P_RET:P_RET + P_GATE], win[:, P_RET + P_GATE:]],
                              name="proj_dx")
        dh_out = dh
        wts.grads_done(l, g)
    dh0, _, g["pre_mix"][0], _ = _resid_norm_bwd(dh_out, dhn_next, h0, None, row(pre_mix_norm, 0), None,
                                                 name="norm_in_bwd")
    return loss_local, dh0, g


def kernel(x, meta_tokens, pre_mix_norm, w_in, gla_gate_w2, gla_gate_b, ret_norm_w, gla_norm_w, w_out, post_mix_norm, pre_ffn_norm, ffn_up, ffn_conv_w, ffn_conv_b, ffn_down, post_ffn_norm, loss_target, m_meta_tokens, m_pre_mix_norm, m_w_in, m_gla_gate_w2, m_gla_gate_b, m_ret_norm_w, m_gla_norm_w, m_w_out, m_post_mix_norm, m_pre_ffn_norm, m_ffn_up, m_ffn_conv_w, m_ffn_conv_b, m_ffn_down, m_post_ffn_norm, v_meta_tokens, v_pre_mix_norm, v_w_in, v_gla_gate_w2, v_gla_gate_b, v_ret_norm_w, v_gla_norm_w, v_w_out, v_post_mix_norm, v_pre_ffn_norm, v_ffn_up, v_ffn_conv_w, v_ffn_conv_b, v_ffn_down, v_post_ffn_norm):
    xi, yi, ci = _place()
    chip = 2 * xi + yi
    seq = x.shape[1]
    d = D_MODEL
    wts = _ChipWeights(w_in, w_out, ffn_up, ffn_down, meta_tokens, gla_gate_w2, ffn_conv_w)
    loss_local, dh0, g = _local_step(x[0], loss_target[0], wts, pre_mix_norm, gla_gate_b, ret_norm_w, gla_norm_w,
                                     post_mix_norm, pre_ffn_norm, ffn_conv_b, post_ffn_norm)
    grad_x = dh0[FRONT:FRONT + seq][None]
    names = ("w_in", "w_out", "ffn_up", "ffn_down")
    g_w_in, g_w_out, g_ffn_up, g_ffn_down = wts.reduce()

    small_full = [dh0[PADF:FRONT], jnp.stack(g["pre_mix"])[:, 0], jnp.stack(g["w2"]), jnp.stack(g["gb"]),
                  jnp.stack(g["ret_n"])[:, 0], jnp.stack(g["gla_n"])[:, 0], jnp.stack(g["post_mix"])[:, 0],
                  jnp.stack(g["pre_ffn"])[:, 0], jnp.stack(g["cw"]), jnp.stack(g["cb"]),
                  jnp.stack(g["post_ffn"])[:, 0]]
    small_sum = _unslab(_allreduce_small(_slab(small_full, 8), name="small_allreduce"), [a.shape for a in small_full])
    (g_meta, g_pre_mix, g_w2, g_gb, g_ret_n, g_gla_n, g_post_mix, g_pre_ffn, g_cw, g_cb, g_post_ffn) = small_sum
    g_meta = lax.dynamic_slice_in_dim(g_meta, chip * 256, 256, axis=1)
    g_w2 = lax.dynamic_slice_in_dim(g_w2, chip * 64, 64, axis=2)
    g_cw = lax.dynamic_slice_in_dim(g_cw, chip * 1408, 1408, axis=2)

    grads = [g_meta, g_pre_mix, g_w_in, g_w2, g_gb, g_ret_n, g_gla_n, g_w_out, g_post_mix, g_pre_ffn, g_ffn_up,
             g_cw, g_cb, g_ffn_down, g_post_ffn]
    ws = [meta_tokens, pre_mix_norm, w_in, gla_gate_w2, gla_gate_b, ret_norm_w, gla_norm_w, w_out, post_mix_norm,
          pre_ffn_norm, ffn_up, ffn_conv_w, ffn_conv_b, ffn_down, post_ffn_norm]
    ms = [m_meta_tokens, m_pre_mix_norm, m_w_in, m_gla_gate_w2, m_gla_gate_b, m_ret_norm_w, m_gla_norm_w, m_w_out,
          m_post_mix_norm, m_pre_ffn_norm, m_ffn_up, m_ffn_conv_w, m_ffn_conv_b, m_ffn_down, m_post_ffn_norm]
    vs = [v_meta_tokens, v_pre_mix_norm, v_w_in, v_gla_gate_w2, v_gla_gate_b, v_ret_norm_w, v_gla_norm_w, v_w_out,
          v_post_mix_norm, v_pre_ffn_norm, v_ffn_up, v_ffn_conv_w, v_ffn_conv_b, v_ffn_down, v_post_ffn_norm]
    big_idx = (2, 7, 10, 13)
    deltas, new_m, new_v = [None] * 15, [None] * 15, [None] * 15
    for i, nm in zip(big_idx, names):
        deltas[i], new_m[i], new_v[i] = _adamw(ws[i], grads[i], ms[i], vs[i], name=f"adamw_{nm}")
    small_idx = [i for i in range(15) if i not in big_idx]
    shapes = [ws[i].shape for i in small_idx]
    sd, sm, sv = _adamw(_slab([ws[i] for i in small_idx], 8), _slab([grads[i] for i in small_idx], 8),
                        _slab([ms[i] for i in small_idx], 8), _slab([vs[i] for i in small_idx], 8), name="adamw_small")
    for i, a, b, c_ in zip(small_idx, _unslab(sd, shapes), _unslab(sm, shapes), _unslab(sv, shapes)):
        deltas[i], new_m[i], new_v[i] = a, b, c_

    loss = lax.psum(loss_local, ("x", "y", "c"))
    return (loss, grad_x, *grads, *deltas, *new_m, *new_v)
```

```python
import functools
import math

import numpy as np
import jax
import jax.numpy as jnp
from jax import lax
from jax.experimental import pallas as pl
from jax.experimental.pallas import tpu as pltpu

F32 = jnp.float32
BF16 = jnp.bfloat16

D_MODEL = 1024
DEPTH = 2
N_META = 16
EPS = 1e-6
RET_HEADS = 4
RET_DK = 128
GLA_HEADS = 4
GLA_DK = 64
GLA_DV = 128
GLA_QK = GLA_HEADS * GLA_DK
GLA_V = GLA_HEADS * GLA_DV
GLA_RANK = 16
GLA_TAU = 16.0
D_FF = 2816
ROPE_BASE = 10000.0
IN_WIDTH = 3600
IN_PAD = 3840
C_RQ, C_RK, C_RV, C_RG, C_GR, C_GQ, C_GK, C_GV, C_GA = 0, 512, 1024, 1536, 2048, 2560, 2816, 3072, 3584
P_RET, P_GATE, P_GLA = 1536, 1024, 1280


def _to_kernel_cols(w):
    pad = jnp.zeros(w.shape[:-1] + (IN_PAD - IN_WIDTH,), w.dtype)
    return jnp.concatenate([w[..., 0:2048], w[..., 3072:3584], w[..., 2048:3072], w[..., 3584:3600], pad], axis=-1)


def _to_reference_cols(d_ret, d_gate, d_gla):
    return jnp.concatenate([d_ret, d_gate[..., 0:512], d_gla[..., 0:1024], d_gate[..., 512:1024],
                            d_gla[..., 1024:1024 + GLA_RANK]], axis=-1)

FRONT = 64
BACK = 64
PADF = FRONT - N_META
RET_CHUNK = 128
GLA_CHUNK = 64
GLA_SUB = 16
BLK = 640

ADAM_LR, ADAM_B1, ADAM_B2, ADAM_EPS, ADAM_WD, ADAM_STEP = 0.001, 0.9, 0.999, 1e-08, 0.01, 10

VMEM_LIMIT = 56 * 2 ** 20
MESH = pl.DeviceIdType.MESH


def _cp(*sem):
    return pltpu.CompilerParams(dimension_semantics=sem, vmem_limit_bytes=VMEM_LIMIT)


def _tile(n, cands):
    for t in cands:
        if n % t == 0:
            return t
    raise ValueError(f"no tile for {n} in {cands}")


def _row_tile(n):
    return _tile(n, (640, 512, 320, 256, 128, 64))


def _mm(a, b, *, nt=False, add=None, out_dtype=F32, tn=None, name, carry=None):
    m, k = a.shape
    n = b.shape[0] if nt else b.shape[1]
    tm = _tile(m, (640, 320, 256, 128, 64))
    tn = n if tn is None else tn
    dn = (((1,), (1,)), ((), ())) if nt else (((1,), (0,)), ((), ()))
    nj, ni = n // tn, m // tm
    n_in = 2 + (add is not None)
    c_arrs, c_mode = carry if carry is not None else ((), None)
    nc = len(c_arrs)

    def body(*refs):
        a_ref, b_ref = refs[:2]
        c_ref = refs[2] if add is not None else None
        o_ref = refs[n_in + nc]
        if nc:
            c_ins, c_outs = refs[n_in:n_in + nc], refs[n_in + nc + 1:n_in + 2 * nc + 1]
            ssem, rsem = refs[n_in + 2 * nc + 1:]
            j, i = pl.program_id(0), pl.program_id(1)

            @pl.when((j == 0) & (i == 0))
            def _():
                for cp in _chip_copies(c_ins, c_outs, ssem, rsem, c_mode)[0]:
                    cp.start()
        r = lax.dot_general(a_ref[...].astype(BF16), b_ref[...].astype(BF16), dn, preferred_element_type=F32)
        if add is not None:
            r = r + c_ref[...]
        o_ref[...] = r.astype(o_ref.dtype)
        if nc:
            @pl.when((j == nj - 1) & (i == ni - 1))
            def _():
                _chip_wait(*_chip_copies(c_ins, c_outs, ssem, rsem, c_mode))

    b_spec = pl.BlockSpec((tn, k), lambda j, i: (j, 0)) if nt else pl.BlockSpec((k, tn), lambda j, i: (0, j))
    in_specs = [pl.BlockSpec((tm, k), lambda j, i: (i, 0)), b_spec]
    args = [a, b]
    if add is not None:
        in_specs.append(pl.BlockSpec((tm, tn), lambda j, i: (i, j)))
        args.append(add)
    out_shape = jax.ShapeDtypeStruct((m, n), out_dtype)
    out_spec = pl.BlockSpec((tm, tn), lambda j, i: (i, j))
    if not nc:
        return pl.pallas_call(
            body, out_shape=out_shape, grid=(nj, ni), in_specs=in_specs, out_specs=out_spec,
            compiler_params=_cp("parallel", "parallel"), name=name)(*args)
    sem = pltpu.SemaphoreType.DMA
    outs = pl.pallas_call(
        body, out_shape=(out_shape,) + tuple(_landing_shape(x, c_mode) for x in c_arrs), grid=(nj, ni),
        in_specs=in_specs + [ANY] * nc, out_specs=(out_spec,) + (ANY,) * nc,
        scratch_shapes=[sem((nc, 3)), sem((nc, 3))],
        compiler_params=_cp("arbitrary", "arbitrary"), name=name)(*args, *c_arrs)
    return outs[0], list(outs[1:])


def _mm_nt_sum(a_list, b_list, *, name):
    m, n = a_list[0].shape[0], b_list[0].shape[0]
    tm = _tile(m, (640, 320, 256, 128, 64))
    np_ = len(a_list)

    def body(*refs):
        acc = None
        for a_ref, b_ref in zip(refs[:np_], refs[np_:2 * np_]):
            r = lax.dot_general(a_ref[...].astype(BF16), b_ref[...].astype(BF16), (((1,), (1,)), ((), ())),
                                preferred_element_type=F32)
            acc = r if acc is None else acc + r
        refs[2 * np_][...] = acc

    return pl.pallas_call(
        body, out_shape=jax.ShapeDtypeStruct((m, n), F32), grid=(m // tm,),
        in_specs=[pl.BlockSpec((tm, a.shape[1]), lambda i: (i, 0)) for a in a_list]
        + [pl.BlockSpec(b.shape, lambda i: (0, 0)) for b in b_list],
        out_specs=pl.BlockSpec((tm, n), lambda i: (i, 0)),
        compiler_params=_cp("parallel"), name=name)(*a_list, *b_list)


def _mm_tn(a, b, *, tn=None, blocks=None, into=None, name):
    m, k = a.shape
    n = b.shape[1]
    tm = _tile(m, (1664, 640, 320, 256, 128, 64))
    tn = n if tn is None else tn
    if blocks is not None:
        total, first = blocks
        out_shape = jax.ShapeDtypeStruct((total, k, tn), F32)
        out_spec = pl.BlockSpec((None, k, tn), lambda j, i: (first + j, 0, 0))
    else:
        out_shape = jax.ShapeDtypeStruct((k, n), F32)
        out_spec = pl.BlockSpec((k, tn), lambda j, i: (0, j))

    def body(a_ref, b_ref, *rest):
        o_ref = rest[-1]

        @pl.when(pl.program_id(1) == 0)
        def _():
            o_ref[...] = jnp.zeros_like(o_ref)
        o_ref[...] += lax.dot_general(a_ref[...].astype(BF16), b_ref[...].astype(BF16),
                                      (((0,), (0,)), ((), ())), preferred_element_type=F32)

    in_specs = [pl.BlockSpec((tm, k), lambda j, i: (i, 0)), pl.BlockSpec((tm, tn), lambda j, i: (i, j))]
    args, alias = [a, b], {}
    if into is not None:
        in_specs.append(pl.BlockSpec(memory_space=pl.ANY))
        args.append(into)
        alias = {2: 0}
    return pl.pallas_call(
        body, out_shape=out_shape, grid=(n // tn, m // tm), in_specs=in_specs, out_specs=out_spec,
        input_output_aliases=alias, compiler_params=_cp("parallel", "arbitrary"), name=name)(*args)


def _rms(x, w):
    r = lax.rsqrt(jnp.mean(x * x, axis=-1, keepdims=True) + EPS)
    return x * r * w


def _rms_bwd(x, w, dy):
    r = lax.rsqrt(jnp.mean(x * x, axis=-1, keepdims=True) + EPS)
    xh = x * r
    dxh = dy * w
    dx = r * (dxh - xh * jnp.mean(dxh * xh, axis=-1, keepdims=True))
    return dx, jnp.sum(dy * xh, axis=0, keepdims=True)


def _resid_norm(h, t, w_post, w_next, *, name):
    lp, d = h.shape
    tm = _row_tile(lp)
    has_t = t is not None

    def body(*refs):
        if has_t:
            h_ref, t_ref, wp_ref, wn_ref, ho_ref, hn_ref = refs
            hv = h_ref[...] + _rms(t_ref[...], wp_ref[...])
            ho_ref[...] = hv
        else:
            h_ref, wn_ref, hn_ref = refs
            hv = h_ref[...]
        hn_ref[...] = _rms(hv, wn_ref[...]).astype(BF16)

    row = pl.BlockSpec((tm, d), lambda i: (i, 0))
    vec = pl.BlockSpec((1, d), lambda i: (0, 0))
    if has_t:
        return pl.pallas_call(
            body, out_shape=(jax.ShapeDtypeStruct((lp, d), F32), jax.ShapeDtypeStruct((lp, d), BF16)),
            grid=(lp // tm,), in_specs=[row, row, vec, vec], out_specs=(row, row),
            compiler_params=_cp("parallel"), name=name)(h, t, w_post, w_next)
    return h, pl.pallas_call(
        body, out_shape=jax.ShapeDtypeStruct((lp, d), BF16), grid=(lp // tm,), in_specs=[row, vec],
        out_specs=row, compiler_params=_cp("parallel"), name=name)(h, w_next)


def _resid_norm_bwd(dh_out, dhn, h_new, t, w_next, w_post, *, name):
    lp, d = h_new.shape if h_new is not None else t.shape
    tm = _row_tile(lp)
    has_n = dhn is not None
    has_t = t is not None

    def body(*refs):
        refs = list(refs)
        dho_ref = refs.pop(0)
        if has_n:
            dhn_ref, hn_ref, wn_ref = refs.pop(0), refs.pop(0), refs.pop(0)
        if has_t:
            t_ref, wp_ref = refs.pop(0), refs.pop(0)
        dh_ref = refs.pop(0) if has_n else None
        dt_ref = refs.pop(0) if has_t else None
        dwn_ref = refs.pop(0) if has_n else None
        dwp_ref = refs.pop(0) if has_t else None
        first = pl.program_id(0) == 0
        dh = dho_ref[...]
        if has_n:
            dx, dwn = _rms_bwd(hn_ref[...], wn_ref[...], dhn_ref[...])
            dh = dh + dx
            dh_ref[...] = dh

            @pl.when(first)
            def _():
                dwn_ref[...] = jnp.zeros_like(dwn_ref)
            dwn_ref[...] += dwn
        if has_t:
            dt, dwp = _rms_bwd(t_ref[...], wp_ref[...], dh)
            dt_ref[...] = dt.astype(BF16)

            @pl.when(first)
            def _():
                dwp_ref[...] = jnp.zeros_like(dwp_ref)
            dwp_ref[...] += dwp

    row = pl.BlockSpec((tm, d), lambda i: (i, 0))
    vec = pl.BlockSpec((1, d), lambda i: (0, 0))
    args, in_specs, out_shape, out_specs = [dh_out], [row], [], []
    if has_n:
        args += [dhn, h_new, w_next]
        in_specs += [row, row, vec]
    if has_t:
        args += [t, w_post]
        in_specs += [row, vec]
    if has_n:
        out_shape.append(jax.ShapeDtypeStruct((lp, d), F32)); out_specs.append(row)
    if has_t:
        out_shape.append(jax.ShapeDtypeStruct((lp, d), BF16)); out_specs.append(row)
    if has_n:
        out_shape.append(jax.ShapeDtypeStruct((1, d), F32)); out_specs.append(vec)
    if has_t:
        out_shape.append(jax.ShapeDtypeStruct((1, d), F32)); out_specs.append(vec)
    outs = list(pl.pallas_call(body, out_shape=tuple(out_shape), grid=(lp // tm,), in_specs=in_specs,
                               out_specs=tuple(out_specs), compiler_params=_cp("arbitrary"), name=name)(*args))
    dh = outs.pop(0) if has_n else dh_out
    dt = outs.pop(0) if has_t else None
    dwn = outs.pop(0) if has_n else None
    dwp = outs.pop(0) if has_t else None
    return dh, dt, dwn, dwp


def _loss_head(h, f, w_post, target, *, name):
    lp, d = h.shape
    tm = _row_tile(lp)

    def body(h_ref, f_ref, w_ref, t_ref, loss_ref, dy_ref):
        i = pl.program_id(0)
        y = h_ref[...] + _rms(f_ref[...], w_ref[...])
        rows = i * tm + lax.broadcasted_iota(jnp.int32, (tm, 1), 0)
        tok = (rows >= FRONT) & (rows < lp - BACK)
        err = jnp.where(tok, y - t_ref[...], 0.0)
        dy_ref[...] = err * (1.0 / d)

        @pl.when(i == 0)
        def _():
            loss_ref[...] = jnp.zeros_like(loss_ref)
        part = jnp.sum(jnp.sum(err * err, axis=1, keepdims=True), axis=0, keepdims=True) * (0.5 / d)
        loss_ref[...] += jnp.broadcast_to(part, loss_ref.shape)

    row = pl.BlockSpec((tm, d), lambda i: (i, 0))
    loss, dy = pl.pallas_call(
        body, out_shape=(jax.ShapeDtypeStruct((8, 128), F32), jax.ShapeDtypeStruct((lp, d), F32)),
        grid=(lp // tm,), in_specs=[row, row, pl.BlockSpec((1, d), lambda i: (0, 0)), row],
        out_specs=(pl.BlockSpec((8, 128), lambda i: (0, 0)), row),
        compiler_params=_cp("arbitrary"), name=name)(h, f, w_post, target)
    return loss[0, 0], dy


_GELU_C = math.sqrt(2.0 / math.pi)


def _gelu_and_grad(a):
    a2 = a * a
    t = jnp.tanh(a * (_GELU_C + (_GELU_C * 0.044715) * a2))
    ha = 0.5 * a
    h1 = 0.5 + 0.5 * t
    return a * h1, h1 + ha * (1.0 - t * t) * (_GELU_C + (3.0 * _GELU_C * 0.044715) * a2)


def _gelu(a):
    t = jnp.tanh(a * (_GELU_C + (_GELU_C * 0.044715) * (a * a)))
    return a * (0.5 + 0.5 * t)


def _conv3(parts, n, w, b):
    xx = jnp.concatenate(parts, axis=0)
    return b + xx[8:8 + n] * w[2:3] + pltpu.roll(xx, 1, 0)[8:8 + n] * w[1:2] + pltpu.roll(xx, 2, 0)[8:8 + n] * w[0:1]


def _conv_act(ua, ug, wa, wg, ba, bg, *, name):
    lp, n = ua.shape
    tm = _row_tile(lp)
    tc = _tile(n, (256, 128))
    nb8 = tm // 8

    def body(ua_ref, uap_ref, ug_ref, ugp_ref, wa_ref, wg_ref, ba_ref, bg_ref, o_ref):
        i = pl.program_id(0)
        ca = _conv3([uap_ref[...], ua_ref[...]], tm, wa_ref[...], ba_ref[...])
        cg = _conv3([ugp_ref[...], ug_ref[...]], tm, wg_ref[...], bg_ref[...])
        rows = i * tm + lax.broadcasted_iota(jnp.int32, (tm, 1), 0)
        ok = (rows >= PADF) & (rows < lp - BACK)
        o_ref[...] = jnp.where(ok, _gelu(ca) * cg, 0.0).astype(BF16)

    cur = pl.BlockSpec((tm, tc), lambda i, j: (i, j))
    prev = pl.BlockSpec((8, tc), lambda i, j: (jnp.maximum(i * nb8 - 1, 0), j))
    w3 = pl.BlockSpec((3, tc), lambda i, j: (0, j))
    b1 = pl.BlockSpec((1, tc), lambda i, j: (0, j))
    return pl.pallas_call(
        body, out_shape=jax.ShapeDtypeStruct((lp, n), BF16), grid=(lp // tm, n // tc),
        in_specs=[cur, prev, cur, prev, w3, w3, b1, b1], out_specs=cur,
        compiler_params=_cp("parallel", "parallel"), name=name)(ua, ua, ug, ug, wa, wg, ba, bg)


def _conv_act_bwd(ua, ug, dact, wa, wg, ba, bg, *, name):
    lp, n = ua.shape
    tm = _row_tile(lp)
    tc = _tile(n, (256, 128))
    nb8 = tm // 8
    last8 = lp // 8 - 1
    ext = tm + 8

    def body(ua_ref, uap_ref, uan_ref, ug_ref, ugp_ref, ugn_ref, da_ref, dan_ref, wa_ref, wg_ref, ba_ref, bg_ref,
             dua_ref, dug_ref, dwa_ref, dwg_ref, dba_ref, dbg_ref):
        i = pl.program_id(1)
        wa, wg = wa_ref[...], wg_ref[...]

        def conv(parts, w, b):
            xx = jnp.concatenate(parts, axis=0)
            x, x1, x2 = xx[8:8 + ext], pltpu.roll(xx, 1, 0)[8:8 + ext], pltpu.roll(xx, 2, 0)[8:8 + ext]
            return b + x * w[2:3] + x1 * w[1:2] + x2 * w[0:1], x, x1, x2

        ca, xa, xa1, xa2 = conv([uap_ref[...], ua_ref[...], uan_ref[...]], wa, ba_ref[...])
        cg, xg, xg1, xg2 = conv([ugp_ref[...], ug_ref[...], ugn_ref[...]], wg, bg_ref[...])
        rows = i * tm + lax.broadcasted_iota(jnp.int32, (ext, 1), 0)
        ok = (rows >= PADF) & (rows < lp - BACK)
        dact_e = jnp.where(ok, jnp.concatenate([da_ref[...], dan_ref[...]], axis=0), 0.0)
        gel, gel_d = _gelu_and_grad(ca)
        dca = dact_e * cg * gel_d
        dcg = dact_e * gel

        def back(dc, w):
            return (dc[:tm] * w[2:3] + pltpu.roll(dc, ext - 1, 0)[:tm] * w[1:2]
                    + pltpu.roll(dc, ext - 2, 0)[:tm] * w[0:1])

        dua_ref[...] = back(dca, wa).astype(BF16)
        dug_ref[...] = back(dcg, wg).astype(BF16)

        @pl.when(i == 0)
        def _():
            dwa_ref[...] = jnp.zeros_like(dwa_ref)
            dwg_ref[...] = jnp.zeros_like(dwg_ref)
            dba_ref[...] = jnp.zeros_like(dba_ref)
            dbg_ref[...] = jnp.zeros_like(dbg_ref)

        def wsum(dw_ref, db_ref, dc, x, x1, x2):
            d = dc[:tm]
            s = lambda v: jnp.sum(v, axis=0, keepdims=True)
            dw_ref[0:1, :] += s(d * x2[:tm])
            dw_ref[1:2, :] += s(d * x1[:tm])
            dw_ref[2:3, :] += s(d * x[:tm])
            db_ref[...] += s(d)

        wsum(dwa_ref, dba_ref, dca, xa, xa1, xa2)
        wsum(dwg_ref, dbg_ref, dcg, xg, xg1, xg2)

    cur = pl.BlockSpec((tm, tc), lambda j, i: (i, j))
    prev = pl.BlockSpec((8, tc), lambda j, i: (jnp.maximum(i * nb8 - 1, 0), j))
    nxt = pl.BlockSpec((8, tc), lambda j, i: (jnp.minimum((i + 1) * nb8, last8), j))
    w3 = pl.BlockSpec((3, tc), lambda j, i: (0, j))
    b1 = pl.BlockSpec((1, tc), lambda j, i: (0, j))
    return pl.pallas_call(
        body,
        out_shape=(jax.ShapeDtypeStruct((lp, n), BF16), jax.ShapeDtypeStruct((lp, n), BF16),
                   jax.ShapeDtypeStruct((3, n), F32), jax.ShapeDtypeStruct((3, n), F32),
                   jax.ShapeDtypeStruct((1, n), F32), jax.ShapeDtypeStruct((1, n), F32)),
        grid=(n // tc, lp // tm),
        in_specs=[cur, prev, nxt, cur, prev, nxt, cur, nxt, w3, w3, b1, b1],
        out_specs=(cur, cur, w3, w3, b1, b1),
        compiler_params=_cp("parallel", "arbitrary"), name=name)(ua, ua, ua, ug, ug, ug, dact, dact, wa, wg, ba, bg)


def _sigmoid(x):
    return 1.0 / (1.0 + jnp.exp(-x))


def _merge(o_ret, o_gla, proj, w_ret, w_gla, *, name):
    lp = o_ret.shape[0]
    tm = _row_tile(lp)

    def body(or_ref, og_ref, rg_ref, gr_ref, wr_ref, wg_ref, m_ref):
        oret, ogla = or_ref[...], og_ref[...]
        yr, yg = [], []
        for h in range(4):
            hs = slice(128 * h, 128 * h + 128)
            o = oret[:, hs]
            xc = o - jnp.mean(o, axis=-1, keepdims=True)
            yr.append(xc * lax.rsqrt(jnp.mean(xc * xc, axis=-1, keepdims=True) + EPS))
            o = ogla[:, hs]
            yg.append(o * lax.rsqrt(jnp.mean(o * o, axis=-1, keepdims=True) + EPS))
        rg, gr = rg_ref[...], gr_ref[...]
        m_ref[:, 0:512] = (jnp.concatenate(yr, axis=1) * wr_ref[...] * (rg * _sigmoid(rg))).astype(BF16)
        m_ref[:, 512:1024] = (jnp.concatenate(yg, axis=1) * wg_ref[...] * (gr * _sigmoid(gr))).astype(BF16)

    row = pl.BlockSpec((tm, 512), lambda i: (i, 0))
    vec = pl.BlockSpec((1, 512), lambda i: (0, 0))
    return pl.pallas_call(
        body, out_shape=jax.ShapeDtypeStruct((lp, 1024), BF16), grid=(lp // tm,),
        in_specs=[row, row, pl.BlockSpec((tm, 512), lambda i: (i, C_RG // 512)),
                  pl.BlockSpec((tm, 512), lambda i: (i, C_GR // 512)), vec, vec],
        out_specs=pl.BlockSpec((tm, 1024), lambda i: (i, 0)),
        compiler_params=_cp("parallel"), name=name)(o_ret, o_gla, proj, proj, w_ret, w_gla)


def _merge_bwd(dm, o_ret, o_gla, proj, w_ret, w_gla, *, name):
    lp = o_ret.shape[0]
    tm = _row_tile(lp)

    def body(dm_ref, or_ref, og_ref, rg_ref, gr_ref, wr_ref, wg_ref, dor_ref, dog_ref, dgate_ref, dwr_ref, dwg_ref):
        @pl.when(pl.program_id(0) == 0)
        def _():
            dwr_ref[...] = jnp.zeros_like(dwr_ref)
            dwg_ref[...] = jnp.zeros_like(dwg_ref)

        def group(d, o_all, gate, w, center):
            sg = _sigmoid(gate)
            s = gate * sg
            ds = sg * (1.0 + gate * (1.0 - sg))
            xh, rr = [], []
            for h in range(4):
                o = o_all[:, 128 * h:128 * h + 128]
                if center:
                    o = o - jnp.mean(o, axis=-1, keepdims=True)
                r = lax.rsqrt(jnp.mean(o * o, axis=-1, keepdims=True) + EPS)
                xh.append(o * r)
                rr.append(r)
            xh_all = jnp.concatenate(xh, axis=1)
            dgate = d * xh_all * w * ds
            dw = jnp.sum(d * xh_all * s, axis=0, keepdims=True)
            dxh_all = d * w * s
            do = []
            for h in range(4):
                dxh = dxh_all[:, 128 * h:128 * h + 128]
                t = dxh - xh[h] * jnp.mean(dxh * xh[h], axis=-1, keepdims=True)
                if center:
                    t = t - jnp.mean(dxh, axis=-1, keepdims=True)
                do.append(rr[h] * t)
            return jnp.concatenate(do, axis=1), dgate, dw

        dmv = dm_ref[...]
        do, dg, dw = group(dmv[:, 0:512], or_ref[...], rg_ref[...], wr_ref[...], True)
        dor_ref[...] = do
        dgate_ref[:, 0:512] = dg.astype(BF16)
        dwr_ref[...] += dw
        do, dg, dw = group(dmv[:, 512:1024], og_ref[...], gr_ref[...], wg_ref[...], False)
        dog_ref[...] = do
        dgate_ref[:, 512:1024] = dg.astype(BF16)
        dwg_ref[...] += dw

    row = pl.BlockSpec((tm, 512), lambda i: (i, 0))
    vec = pl.BlockSpec((1, 512), lambda i: (0, 0))
    return pl.pallas_call(
        body,
        out_shape=(jax.ShapeDtypeStruct((lp, 512), F32), jax.ShapeDtypeStruct((lp, 512), F32),
                   jax.ShapeDtypeStruct((lp, P_GATE), BF16),
                   jax.ShapeDtypeStruct((1, 512), F32), jax.ShapeDtypeStruct((1, 512), F32)),
        grid=(lp // tm,),
        in_specs=[pl.BlockSpec((tm, 1024), lambda i: (i, 0)), row, row,
                  pl.BlockSpec((tm, 512), lambda i: (i, C_RG // 512)),
                  pl.BlockSpec((tm, 512), lambda i: (i, C_GR // 512)), vec, vec],
        out_specs=(row, row, pl.BlockSpec((tm, P_GATE), lambda i: (i, 0)), vec, vec),
        compiler_params=_cp("arbitrary"), name=name)(dm, o_ret, o_gla, proj, proj, w_ret, w_gla)


def _dot(a, b):
    return lax.dot_general(a, b, (((1,), (0,)), ((), ())), preferred_element_type=F32)


def _dot_nt(a, b):
    return lax.dot_general(a, b, (((1,), (1,)), ((), ())), preferred_element_type=F32)


def _dot_tn(a, b):
    return lax.dot_general(a, b, (((0,), (0,)), ((), ())), preferred_element_type=F32)


def _ret_tables(lp):
    cr = RET_CHUNK
    pos = np.arange(lp, dtype=np.float32) - np.float32(PADF)
    half = RET_DK // 2
    inv = (np.float32(ROPE_BASE) ** (-np.arange(half, dtype=np.float32) / np.float32(half))).astype(np.float32)
    ang = (pos[:, None] * inv[None, :]).astype(np.float32)
    c, s = np.cos(ang).astype(np.float32), np.sin(ang).astype(np.float32)
    rope_c = jnp.asarray(np.concatenate([c, c], axis=1))
    rope_s = jnp.asarray(np.concatenate([-s, s], axis=1))
    log_g = np.log(1.0 - 2.0 ** (-5.0 - np.arange(RET_HEADS, dtype=np.float64)))
    idx = np.arange(cr, dtype=np.float64)
    diff = idx[:, None] - idx[None, :]
    dmat = np.where(diff >= 0, np.exp(log_g[:, None, None] * np.maximum(diff, 0.0)), 0.0)
    zeta = np.exp(log_g[:, None] * (cr - 1.0 - idx)[None, :])
    xi = np.exp(log_g[:, None] * (idx + 1.0)[None, :])
    gc = np.exp(log_g * cr)
    f = lambda a: jnp.asarray(a.astype(np.float32))
    return (rope_c, rope_s, f(dmat), f(np.broadcast_to(zeta[:, :, None], (RET_HEADS, cr, 128))),
            f(np.broadcast_to(xi[:, :, None], (RET_HEADS, cr, 128))),
            f(np.broadcast_to(gc[:, None, None], (RET_HEADS, 8, 128))))


def _rope(t, c, s):
    return t * c + pltpu.roll(t, 64, 1) * s


def _rope_t(d, c, s):
    return d * c + pltpu.roll(d * s, 64, 1)


def _ret_specs(nblk, rev):
    ix = (lambda i: nblk - 1 - i) if rev else (lambda i: i)
    cr = RET_CHUNK
    col = lambda base: pl.BlockSpec((BLK, 512), lambda i: (ix(i), base // 512))
    tab = pl.BlockSpec((BLK, 128), lambda i: (ix(i), 0))
    sq = pl.BlockSpec((RET_HEADS, cr, cr), lambda i: (0, 0, 0))
    hv = pl.BlockSpec((RET_HEADS, cr, 128), lambda i: (0, 0, 0))
    g8 = pl.BlockSpec((RET_HEADS, 8, 128), lambda i: (0, 0, 0))
    st = pl.BlockSpec((RET_HEADS, BLK // cr, 128, 128), lambda i: (0, ix(i), 0, 0))
    out = pl.BlockSpec((BLK, 512), lambda i: (ix(i), 0))
    return col, tab, sq, hv, g8, st, out


def _retention(proj, tables, *, name):
    lp = proj.shape[0]
    nblk, cr = lp // BLK, RET_CHUNK
    scale = RET_DK ** -0.5

    def body(q_ref, k_ref, v_ref, c_ref, s_ref, d_ref, z_ref, x_ref, g_ref, o_ref, st_ref, state):
        @pl.when(pl.program_id(0) == 0)
        def _():
            state[...] = jnp.zeros_like(state)

        def chunk(ci, carry):
            sl = pl.ds(pl.multiple_of(ci * cr, cr), cr)
            c, s = c_ref[sl, :], s_ref[sl, :]
            for h in range(RET_HEADS):
                hs = slice(128 * h, 128 * h + 128)
                q = _rope(q_ref[sl, hs], c, s)
                k = _rope(k_ref[sl, hs], c, s) * scale
                qb, kb, vb = q.astype(BF16), k.astype(BF16), v_ref[sl, hs].astype(BF16)
                st = state[h]
                st_ref[h, ci] = st
                sc = _dot_nt(qb, kb) * d_ref[h]
                o_ref[sl, hs] = _dot(sc.astype(BF16), vb) + _dot(qb, st.astype(BF16)) * x_ref[h]
                state[h] = st * g_ref[h][0:1, :] + _dot_tn((k * z_ref[h]).astype(BF16), vb)
            return carry

        lax.fori_loop(0, BLK // cr, chunk, 0)

    col, tab, sq, hv, g8, st, out = _ret_specs(nblk, False)
    return pl.pallas_call(
        body,
        out_shape=(jax.ShapeDtypeStruct((lp, 512), F32), jax.ShapeDtypeStruct((4, lp // cr, 128, 128), F32)),
        grid=(nblk,), in_specs=[col(C_RQ), col(C_RK), col(C_RV), tab, tab, sq, hv, hv, g8],
        out_specs=(out, st), scratch_shapes=[pltpu.VMEM((RET_HEADS, 128, 128), F32)],
        compiler_params=_cp("arbitrary"), name=name)(proj, proj, proj, *tables)


def _retention_bwd(proj, do, states, tables, *, name):
    lp = proj.shape[0]
    nblk, cr = lp // BLK, RET_CHUNK
    nch = BLK // cr
    scale = RET_DK ** -0.5

    def body(q_ref, k_ref, v_ref, do_ref, st_ref, c_ref, s_ref, d_ref, z_ref, x_ref, g_ref, dqkv_ref, dstate):
        @pl.when(pl.program_id(0) == 0)
        def _():
            dstate[...] = jnp.zeros_like(dstate)

        def chunk(cc, carry):
            ci = nch - 1 - cc
            sl = pl.ds(pl.multiple_of(ci * cr, cr), cr)
            c, s = c_ref[sl, :], s_ref[sl, :]
            for h in range(RET_HEADS):
                hs = slice(128 * h, 128 * h + 128)
                dmat, zeta, xi = d_ref[h], z_ref[h], x_ref[h]
                q = _rope(q_ref[sl, hs], c, s)
                k = _rope(k_ref[sl, hs], c, s) * scale
                qb, kb, vb = q.astype(BF16), k.astype(BF16), v_ref[sl, hs].astype(BF16)
                kzb = (k * zeta).astype(BF16)
                dov = do_ref[sl, hs]
                dob, doxb = dov.astype(BF16), (dov * xi).astype(BF16)
                stb = st_ref[h, ci].astype(BF16)
                dsn = dstate[h]
                dsnb = dsn.astype(BF16)
                scb = (_dot_nt(qb, kb) * dmat).astype(BF16)
                dscb = (_dot_nt(dob, vb) * dmat).astype(BF16)
                dq = _dot(dscb, kb) + _dot_nt(doxb, stb)
                dk = _dot_tn(dscb, qb) + _dot_nt(vb, dsnb) * zeta
                dv = _dot_tn(scb, dob) + _dot(kzb, dsnb)
                dstate[h] = dsn * g_ref[h][0:1, :] + _dot_tn(qb, doxb)
                dqkv_ref[sl, 128 * h:128 * h + 128] = _rope_t(dq, c, s).astype(BF16)
                dqkv_ref[sl, 512 + 128 * h:640 + 128 * h] = _rope_t(dk * scale, c, s).astype(BF16)
                dqkv_ref[sl, 1024 + 128 * h:1152 + 128 * h] = dv.astype(BF16)
            return carry

        lax.fori_loop(0, nch, chunk, 0)

    col, tab, sq, hv, g8, st, out = _ret_specs(nblk, True)
    return pl.pallas_call(
        body, out_shape=jax.ShapeDtypeStruct((lp, P_RET), BF16), grid=(nblk,),
        in_specs=[col(C_RQ), col(C_RK), col(C_RV), out, st, tab, tab, sq, hv, hv, g8],
        out_specs=pl.BlockSpec((BLK, P_RET), lambda i: (nblk - 1 - i, 0)),
        scratch_shapes=[pltpu.VMEM((RET_HEADS, 128, 128), F32)],
        compiler_params=_cp("arbitrary"), name=name)(proj, proj, proj, do, states, *tables)


def _gla_tables():
    c = GLA_CHUNK
    tri = np.tril(np.ones((c, c), np.float32))
    ones_qv = np.kron(np.eye(GLA_HEADS, dtype=np.float32), np.ones((GLA_DK, GLA_DV), np.float32))
    return (jnp.asarray(tri, BF16), jnp.asarray(tri.T.copy(), BF16), jnp.asarray(ones_qv, BF16),
            jnp.asarray(ones_qv.T.copy(), BF16))


def _split3(x):
    hi = x.astype(BF16)
    r1 = x - hi.astype(F32)
    mid = r1.astype(BF16)
    lo = (r1 - mid.astype(F32)).astype(BF16)
    return hi, mid, lo


def _tri_sum(tri, x):
    hi, mid, lo = _split3(x)
    return _dot(tri, hi) + _dot(tri, mid) + _dot(tri, lo)


def _head_masks(width, per):
    lane = lax.broadcasted_iota(jnp.int32, (1, width), 1)
    return [((lane >= per * h) & (lane < per * (h + 1))).astype(F32) for h in range(GLA_HEADS)]


def _stack_heads(x, masks):
    return jnp.concatenate([x * m for m in masks], axis=0)


def _gla_gate(ga, w2, b, ok, tri):
    z = _dot(ga.astype(BF16), w2) + b
    la = (jnp.minimum(z, 0.0) - jnp.log(1.0 + jnp.exp(-jnp.abs(z)))) * (1.0 / GLA_TAU)
    la = jnp.where(ok, la, 0.0)
    return z, _tri_sum(tri, la)


def _gla_rows(i_blk, ci, lp):
    c = GLA_CHUNK
    rows = i_blk * BLK + ci * c + lax.broadcasted_iota(jnp.int32, (c, 1), 0)
    return (rows >= PADF) & (rows < lp - BACK)


def _gla_off_parts(a, qs, k, g, hm_q):
    s = GLA_SUB
    ra = g[s * a - 1:s * a, :]
    ga_ = g[s * a:s * a + s, :]
    eq = jnp.exp(ga_ - ra)
    ek = jnp.exp(jnp.minimum(ra - g, 0.0))
    qh = qs[s * a:s * a + s, :] * eq
    kh = k * ek
    qst = _stack_heads(qh, hm_q).astype(BF16)
    col = lax.broadcasted_iota(jnp.int32, (GLA_HEADS * s, GLA_CHUNK), 1)
    pmask = col < s * a
    p = jnp.where(pmask, _dot_nt(qst, kh.astype(BF16)), 0.0)
    return eq, ek, qh, kh, qst, pmask, p


def _lag_mask(j):
    r = lax.broadcasted_iota(jnp.int32, (GLA_CHUNK, 1), 0)
    return (jnp.bitwise_and(r, GLA_SUB - 1) >= j).astype(F32)


def _roll_rows(x, j):
    return x if j == 0 else pltpu.roll(x, j, 0)


def _gla(proj, w2p, b, tables, *, name):
    lp = proj.shape[0]
    nblk, c, s = lp // BLK, GLA_CHUNK, GLA_SUB
    nch = BLK // c
    na = c // s

    def body(q_ref, k_ref, v_ref, a_ref, w_ref, b_ref, tri_ref, ones_ref, o_ref, st_ref, state):
        i_blk = pl.program_id(0)

        @pl.when(i_blk == 0)
        def _():
            state[...] = jnp.zeros_like(state)
        hm_q = _head_masks(GLA_QK, GLA_DK)
        tri, ones_qv, w2, bias = tri_ref[...], ones_ref[...], w_ref[...], b_ref[...]

        def chunk(ci, carry):
            sl = pl.ds(pl.multiple_of(ci * c, c), c)
            ok = _gla_rows(i_blk, ci, lp)
            k, v = k_ref[sl, :], v_ref[sl, :]
            vb = v.astype(BF16)
            qs = q_ref[sl, :] * (GLA_DK ** -0.5)
            _, g = _gla_gate(a_ref[sl, :], w2, bias, ok, tri)
            last = g[c - 1:c, :]
            st = state[...]
            st_ref[ci] = st
            qst = _stack_heads(qs * jnp.exp(g), hm_q).astype(BF16)
            oi = _dot_nt(qst, st.astype(BF16))
            o = jnp.concatenate([oi[c * h:c * h + c, :] for h in range(GLA_HEADS)], axis=1)
            ke = k * jnp.exp(last - g)
            f = _dot_tn(vb, ke.astype(BF16))
            upd = f[0:GLA_DV, :] * hm_q[0]
            for h in range(1, GLA_HEADS):
                upd = upd + f[GLA_DV * h:GLA_DV * (h + 1), :] * hm_q[h]
            state[...] = st * jnp.exp(last) + upd
            off = [jnp.zeros((s, GLA_V), F32)]
            for a in range(1, na):
                p = _gla_off_parts(a, qs, k, g, hm_q)[-1]
                ob = _dot(p.astype(BF16), vb)
                off.append(jnp.concatenate(
                    [ob[s * h:s * h + s, GLA_DV * h:GLA_DV * (h + 1)] for h in range(GLA_HEADS)], axis=1))
            o = o + jnp.concatenate(off, axis=0)
            ws = []
            for j in range(s):
                ej = jnp.exp(jnp.minimum(g - _roll_rows(g, j), 0.0))
                ws.append((qs * _roll_rows(k, j) * ej * _lag_mask(j)).astype(BF16))
            ball = _dot(jnp.concatenate(ws, axis=0), ones_qv)
            for j in range(s):
                o = o + ball[c * j:c * j + c, :] * _roll_rows(v, j)
            o_ref[sl, :] = o
            return carry

        lax.fori_loop(0, nch, chunk, 0)

    tri, _, ones_qv, _ = tables
    full = lambda arr: pl.BlockSpec(arr.shape, lambda i: (0,) * arr.ndim)
    return pl.pallas_call(
        body,
        out_shape=(jax.ShapeDtypeStruct((lp, GLA_V), F32), jax.ShapeDtypeStruct((lp // c, GLA_DV, GLA_QK), F32)),
        grid=(nblk,),
        in_specs=[pl.BlockSpec((BLK, GLA_QK), lambda i: (i, C_GQ // GLA_QK)),
                  pl.BlockSpec((BLK, GLA_QK), lambda i: (i, C_GK // GLA_QK)),
                  pl.BlockSpec((BLK, GLA_V), lambda i: (i, C_GV // GLA_V)),
                  pl.BlockSpec((BLK, 128), lambda i: (i, C_GA // 128)),
                  full(w2p), full(b), full(tri), full(ones_qv)],
        out_specs=(pl.BlockSpec((BLK, GLA_V), lambda i: (i, 0)),
                   pl.BlockSpec((nch, GLA_DV, GLA_QK), lambda i: (i, 0, 0))),
        scratch_shapes=[pltpu.VMEM((GLA_DV, GLA_QK), F32)],
        compiler_params=_cp("arbitrary"), name=name)(proj, proj, proj, proj, w2p, b, tri, ones_qv)


def _gla_bwd(proj, do, states, w2p, b, tables, *, name):
    lp = proj.shape[0]
    nblk, c, s = lp // BLK, GLA_CHUNK, GLA_SUB
    nch = BLK // c
    na = c // s

    def body(q_ref, k_ref, v_ref, a_ref, do_ref, st_ref, w_ref, b_ref, tri_ref, trit_ref, ones_ref, onest_ref,
             dp_ref, dw_ref, db_ref, dstate, dqs_s, dk_s, dg_s, dv_s):
        i_blk = nblk - 1 - pl.program_id(0)

        @pl.when(pl.program_id(0) == 0)
        def _():
            dstate[...] = jnp.zeros_like(dstate)
            dw_ref[...] = jnp.zeros_like(dw_ref)
            db_ref[...] = jnp.zeros_like(db_ref)
        hm_q = _head_masks(GLA_QK, GLA_DK)
        hm_v = _head_masks(GLA_V, GLA_DV)
        tri, trit, ones_qv, ones_vq = tri_ref[...], trit_ref[...], ones_ref[...], onest_ref[...]
        w2, bias = w_ref[...], b_ref[...]
        rsum = lambda x: jnp.sum(x, axis=0, keepdims=True)

        def chunk(cc, carry):
            ci = nch - 1 - cc
            sl = pl.ds(pl.multiple_of(ci * c, c), c)
            ok = _gla_rows(i_blk, ci, lp)
            k, v, ga = k_ref[sl, :], v_ref[sl, :], a_ref[sl, :]
            vb = v.astype(BF16)
            qs = q_ref[sl, :] * (GLA_DK ** -0.5)
            z, g = _gla_gate(ga, w2, bias, ok, tri)
            last = g[c - 1:c, :]
            elast = jnp.exp(last)
            eg = jnp.exp(g)
            ekl = jnp.exp(last - g)
            qe, ke = qs * eg, k * ekl
            dov = do_ref[sl, :]
            st = st_ref[ci]
            dsn = dstate[...]
            qst = _stack_heads(qe, hm_q).astype(BF16)
            dost = jnp.concatenate([dov[:, GLA_DV * h:GLA_DV * (h + 1)] for h in range(GLA_HEADS)], axis=0).astype(BF16)
            dqe_st = _dot(dost, st.astype(BF16))
            dqe = dqe_st[0:c, :] * hm_q[0]
            for h in range(1, GLA_HEADS):
                dqe = dqe + dqe_st[c * h:c * h + c, :] * hm_q[h]
            dstate[...] = _dot_tn(dost, qst) + dsn * elast
            dlast = rsum(dsn * st) * elast
            df = _stack_heads(dsn, hm_q).astype(BF16)
            dv_s[...] = _dot_nt(ke.astype(BF16), df)
            dke = _dot(vb, df)
            xk = dke * ke
            dqs_s[...] = dqe * eg
            dk_s[...] = dke * ekl
            dg_s[...] = dqe * qe - xk
            dlast = dlast + rsum(xk)
            for a in range(1, na):
                eq, ek, qh, kh, qsa, pmask, p = _gla_off_parts(a, qs, k, g, hm_q)
                rows = slice(s * a, s * a + s)
                dofull = _stack_heads(dov[rows, :], hm_v).astype(BF16)
                dp = jnp.where(pmask, _dot_nt(dofull, vb), 0.0).astype(BF16)
                dv_s[...] += _dot_tn(p.astype(BF16), dofull)
                dq_st = _dot(dp, kh.astype(BF16))
                dqh = dq_st[0:s, :] * hm_q[0]
                for h in range(1, GLA_HEADS):
                    dqh = dqh + dq_st[s * h:s * h + s, :] * hm_q[h]
                dkh = _dot_tn(dp, qsa)
                xq = dqh * qh
                xkh = dkh * kh
                dqs_s[rows, :] += dqh * eq
                dg_s[rows, :] += xq
                dk_s[...] += dkh * ek
                dg_s[...] -= xkh
                dg_s[s * a - 1:s * a, :] += rsum(xkh) - rsum(xq)
            kes, qes, ws, dbs = [], [], [], []
            for j in range(s):
                em = jnp.exp(jnp.minimum(g - _roll_rows(g, j), 0.0)) * _lag_mask(j)
                kes.append(_roll_rows(k, j) * em)
                qes.append(qs * em)
                ws.append((qs * kes[j]).astype(BF16))
                dbs.append((dov * _roll_rows(v, j)).astype(BF16))
            ball = _dot(jnp.concatenate(ws, axis=0), ones_qv)
            dwall = _dot(jnp.concatenate(dbs, axis=0), ones_vq)
            for j in range(s):
                back = (lambda x: x) if j == 0 else (lambda x, j=j: pltpu.roll(x, c - j, 0))
                dw = dwall[c * j:c * j + c, :]
                dv_s[...] += back(ball[c * j:c * j + c, :] * dov)
                dqs_s[...] += dw * kes[j]
                dk_s[...] += back(dw * qes[j])
                x = dw * qs * kes[j]
                dg_s[...] += x - back(x)
            dg_s[c - 1:c, :] += dlast
            dla = jnp.where(ok, _tri_sum(trit, dg_s[...]), 0.0)
            dz = dla * (1.0 / GLA_TAU) / (1.0 + jnp.exp(z))
            dzb = dz.astype(BF16)
            dp_ref[sl, 0:256] = (dqs_s[...] * (GLA_DK ** -0.5)).astype(BF16)
            dp_ref[sl, 256:512] = dk_s[...].astype(BF16)
            dp_ref[sl, 512:1024] = dv_s[...].astype(BF16)
            dp_ref[sl, 1024:1152] = _dot_nt(dzb, w2).astype(BF16)
            dp_ref[sl, 1152:1280] = jnp.zeros((c, 128), BF16)
            dw_ref[...] += _dot_tn(ga.astype(BF16), dzb)
            db_ref[...] += rsum(dz)
            return carry

        lax.fori_loop(0, nch, chunk, 0)

    tri, trit, ones_qv, ones_vq = tables
    full = lambda arr: pl.BlockSpec(arr.shape, lambda i: (0,) * arr.ndim)
    rev = lambda i: nblk - 1 - i
    return pl.pallas_call(
        body,
        out_shape=(jax.ShapeDtypeStruct((lp, P_GLA), BF16),
                   jax.ShapeDtypeStruct((128, GLA_QK), F32), jax.ShapeDtypeStruct((1, GLA_QK), F32)),
        grid=(nblk,),
        in_specs=[pl.BlockSpec((BLK, GLA_QK), lambda i: (rev(i), C_GQ // GLA_QK)),
                  pl.BlockSpec((BLK, GLA_QK), lambda i: (rev(i), C_GK // GLA_QK)),
                  pl.BlockSpec((BLK, GLA_V), lambda i: (rev(i), C_GV // GLA_V)),
                  pl.BlockSpec((BLK, 128), lambda i: (rev(i), C_GA // 128)),
                  pl.BlockSpec((BLK, GLA_V), lambda i: (rev(i), 0)),
                  pl.BlockSpec((nch, GLA_DV, GLA_QK), lambda i: (rev(i), 0, 0)),
                  full(w2p), full(b), full(tri), full(trit), full(ones_qv), full(ones_vq)],
        out_specs=(pl.BlockSpec((BLK, P_GLA), lambda i: (rev(i), 0)),
                   pl.BlockSpec((128, GLA_QK), lambda i: (0, 0)),
                   pl.BlockSpec((1, GLA_QK), lambda i: (0, 0))),
        scratch_shapes=[pltpu.VMEM((GLA_DV, GLA_QK), F32), pltpu.VMEM((c, GLA_QK), F32),
                        pltpu.VMEM((c, GLA_QK), F32), pltpu.VMEM((c, GLA_QK), F32), pltpu.VMEM((c, GLA_V), F32)],
        compiler_params=_cp("arbitrary"), name=name)(proj, proj, proj, proj, do, states, w2p, b, tri, trit, ones_qv, ones_vq)


def _as2d(a):
    return a.reshape(-1, a.shape[-1])


def _ew_tile(r):
    return _tile(r, (512, 256, 128, 64, 32, 16, 8))


def _add2(a, b, *, out_dtype, name):
    a2, b2 = _as2d(a), _as2d(b)
    r, n = a2.shape
    tm = _ew_tile(r)

    def body(a_ref, b_ref, o_ref):
        o_ref[...] = (a_ref[...] + b_ref[...]).astype(o_ref.dtype)

    blk = pl.BlockSpec((tm, n), lambda i: (i, 0))
    return pl.pallas_call(body, out_shape=jax.ShapeDtypeStruct((r, n), out_dtype), grid=(r // tm,), in_specs=[blk, blk],
                          out_specs=blk, compiler_params=_cp("parallel"), name=name)(a2, b2).reshape(a.shape)


def _sum_slots(own, q, *, name):
    shape = own.shape
    q3 = q.reshape(3, -1, shape[-1])
    own2 = _as2d(own)
    r, n = own2.shape
    tm = _ew_tile(r)

    def body(own_ref, q_ref, o_ref):
        f = lambda i: q_ref[i].astype(F32)
        o_ref[...] = ((own_ref[...].astype(F32) + f(0)) + f(1)) + f(2)

    blk = pl.BlockSpec((tm, n), lambda i: (i, 0))
    return pl.pallas_call(
        body, out_shape=jax.ShapeDtypeStruct((r, n), F32), grid=(r // tm,),
        in_specs=[blk, pl.BlockSpec((3, tm, n), lambda i: (0, i, 0))], out_specs=blk,
        compiler_params=_cp("parallel"), name=name)(own2, q3).reshape(shape)


def _adamw(w, g, m, v, *, name):
    shape = w.shape
    w2, g2, m2, v2 = _as2d(w), _as2d(g), _as2d(m), _as2d(v)
    r, n = w2.shape
    tm = _ew_tile(r)
    c1 = 1.0 - ADAM_B1 ** ADAM_STEP
    c2 = 1.0 - ADAM_B2 ** ADAM_STEP

    def body(w_ref, g_ref, m_ref, v_ref, d_ref, mo_ref, vo_ref):
        gv = g_ref[...]
        mn = ADAM_B1 * m_ref[...] + (1.0 - ADAM_B1) * gv
        vn = ADAM_B2 * v_ref[...] + (1.0 - ADAM_B2) * (gv * gv)
        mo_ref[...] = mn
        vo_ref[...] = vn
        d_ref[...] = -ADAM_LR * ((mn / c1) / (jnp.sqrt(vn / c2) + ADAM_EPS) + ADAM_WD * w_ref[...])

    blk = pl.BlockSpec((tm, n), lambda i: (i, 0))
    o = jax.ShapeDtypeStruct((r, n), F32)
    d, mo, vo = pl.pallas_call(body, out_shape=(o, o, o), grid=(r // tm,), in_specs=[blk] * 4, out_specs=(blk,) * 3,
                               compiler_params=_cp("parallel"), name=name)(w2, g2, m2, v2)
    return d.reshape(shape), mo.reshape(shape), vo.reshape(shape)


ANY = pl.BlockSpec(memory_space=pl.ANY)


def _place():
    return lax.axis_index("x"), lax.axis_index("y"), lax.axis_index("c")


def _other_chips(x, y):
    return [(1 - x, y), (x, 1 - y), (1 - x, 1 - y)]


def _remote(src, dst, ssem, rsem, dev):
    return pltpu.make_async_remote_copy(src_ref=src, dst_ref=dst, send_sem=ssem, recv_sem=rsem, device_id=dev,
                                        device_id_type=MESH)


def _allgather_chips(arrs, *, name):
    n = len(arrs)

    def body(*refs):
        ins, outs = refs[:n], refs[n:2 * n]
        s1, r1, s2, r2 = refs[2 * n:]
        x, y, c = _place()
        q = 2 * x + y
        chips = _other_chips(x, y)
        qs = [2 * cx + cy for cx, cy in chips]
        sib = (x, y, 1 - c)
        first, passed = [], []
        for k in range(n):
            for j, chip in enumerate(chips):
                first.append(_remote(ins[k].at[c], outs[k].at[c, q], s1.at[k, j], r1.at[k, j], (*chip, c)))
        for cp in first:
            cp.start()
        for k in range(n):
            for j, chip in enumerate(chips):
                land = outs[k].at[c, qs[j]]
                _remote(land, land, s1.at[k, j], r1.at[k, j], (*chip, c)).wait_recv()
                fw = _remote(land, land, s2.at[k, j], r2.at[k, j], sib)
                fw.start()
                passed.append(fw)
        for k in range(n):
            for j in range(3):
                land = outs[k].at[1 - c, qs[j]]
                _remote(land, land, s2.at[k, j], r2.at[k, j], sib).wait_recv()
        for cp in first + passed:
            cp.wait_send()

    sem = pltpu.SemaphoreType.DMA
    outs = pl.pallas_call(
        body, out_shape=tuple(jax.ShapeDtypeStruct((2, 4) + a.shape[1:], a.dtype) for a in arrs),
        in_specs=[ANY] * n, out_specs=(ANY,) * n,
        scratch_shapes=[sem((n, 3)), sem((n, 3)), sem((n, 3)), sem((n, 3))], name=name)(*arrs)
    chip = 2 * lax.axis_index("x") + lax.axis_index("y")
    return [lax.dynamic_update_slice_in_dim(o, a[:, None], chip, axis=1) for o, a in zip(outs, arrs)]


def _pair_exchange(arrs, *, name):
    n = len(arrs)

    def body(*refs):
        ins, outs = refs[:n], refs[n:2 * n]
        ssem, rsem = refs[2 * n:]
        x, y, c = _place()
        cps = [_remote(ins[k].at[:, 1 - c], outs[k], ssem.at[k], rsem.at[k], (x, y, 1 - c)) for k in range(n)]
        for cp in cps:
            cp.start()
        for cp in cps:
            cp.wait()

    sem = pltpu.SemaphoreType.DMA
    return pl.pallas_call(
        body, out_shape=tuple(jax.ShapeDtypeStruct((a.shape[0],) + a.shape[2:], a.dtype) for a in arrs),
        in_specs=[ANY] * n, out_specs=(ANY,) * n, scratch_shapes=[sem((n,)), sem((n,))], name=name)(*arrs)


def _pair_sum(mine, theirs, c, *, name):
    _, _, r, n = mine.shape
    tm = r if r <= 512 else _ew_tile(r)

    def body(c_ref, a_ref, b_ref, o_ref):
        o_ref[...] = (a_ref[...] + b_ref[...]).astype(BF16)

    blk = pl.BlockSpec((None, tm, n), lambda s, i, c_ref: (s, i, 0))
    return pl.pallas_call(
        body, out_shape=jax.ShapeDtypeStruct((4, r, n), BF16),
        grid_spec=pltpu.PrefetchScalarGridSpec(
            num_scalar_prefetch=1, grid=(4, r // tm),
            in_specs=[pl.BlockSpec((None, None, tm, n), lambda s, i, c_ref: (s, c_ref[0], i, 0)), blk], out_specs=blk),
        compiler_params=_cp("parallel", "parallel"), name=name)(jnp.reshape(c, (1,)).astype(jnp.int32), mine, theirs)


def _chip_copies(ins, outs, ssem, rsem, mode):
    x, y, c = _place()
    q = 2 * x + y
    sends, recvs = [], []
    for k in range(len(ins)):
        for j, (cx, cy) in enumerate(_other_chips(x, y)):
            sem = (ssem.at[k, j], rsem.at[k, j], (cx, cy, c))
            if mode == "scatter":
                sends.append(_remote(ins[k].at[2 * cx + cy], outs[k].at[j], *sem))
                recvs.append(sends[-1])
            else:
                sends.append(_remote(ins[k].at[c], outs[k].at[2 * q + c], *sem))
                recvs.append(_remote(ins[k].at[c], outs[k].at[2 * (2 * cx + cy) + c], *sem))
    return sends, recvs


def _chip_wait(sends, recvs):
    for cp in sends:
        cp.wait_send()
    for cp in recvs:
        cp.wait_recv()


def _landing_shape(a, mode):
    return jax.ShapeDtypeStruct(((3,) if mode == "scatter" else (8,)) + a.shape[1:], a.dtype)


def _chip_exchange(arrs, mode, *, name):
    n = len(arrs)

    def body(*refs):
        ins, outs = refs[:n], refs[n:2 * n]
        ssem, rsem = refs[2 * n:]
        sends, recvs = _chip_copies(ins, outs, ssem, rsem, mode)
        for cp in sends:
            cp.start()
        _chip_wait(sends, recvs)

    sem = pltpu.SemaphoreType.DMA
    return list(pl.pallas_call(
        body, out_shape=tuple(_landing_shape(a, mode) for a in arrs),
        in_specs=[ANY] * n, out_specs=(ANY,) * n, scratch_shapes=[sem((n, 3)), sem((n, 3))], name=name)(*arrs))


def _pair_fill(bufs, owns, *, name):
    n = len(bufs)

    def body(*refs):
        own, outs = refs[n:2 * n], refs[2 * n:3 * n]
        ssem, rsem = refs[3 * n:]
        x, y, c = _place()
        q = 2 * x + y
        sib = (x, y, 1 - c)
        sends, recvs = [], []
        for k in range(n):
            for j, (cx, cy) in enumerate(_other_chips(x, y)):
                mine, theirs = outs[k].at[2 * (2 * cx + cy) + c], outs[k].at[2 * (2 * cx + cy) + 1 - c]
                sends.append(_remote(mine, mine, ssem.at[k, j], rsem.at[k, j], sib))
                recvs.append(_remote(mine, theirs, ssem.at[k, j], rsem.at[k, j], sib))
            slots = outs[k].at[pl.ds(2 * q, 2)]
            sends.append(_remote(own[k], slots, ssem.at[k, 3], rsem.at[k, 3], sib))
            recvs.append(sends[-1])
        for cp in sends:
            cp.start()
        _chip_wait(sends, recvs)

    sem = pltpu.SemaphoreType.DMA
    return list(pl.pallas_call(
        body, out_shape=tuple(jax.ShapeDtypeStruct(b.shape, b.dtype) for b in bufs),
        in_specs=[ANY] * (2 * n), out_specs=(ANY,) * n, scratch_shapes=[sem((n, 4)), sem((n, 4))],
        input_output_aliases={k: k for k in range(n)}, name=name)(*bufs, *owns))


def _pair_swap(arrs, *, name):
    n = len(arrs)

    def body(*refs):
        ins, outs = refs[:n], refs[n:2 * n]
        ssem, rsem = refs[2 * n:]
        x, y, c = _place()
        cps = [_remote(ins[k], outs[k], ssem.at[k], rsem.at[k], (x, y, 1 - c)) for k in range(n)]
        for cp in cps:
            cp.start()
        for cp in cps:
            cp.wait()

    sem = pltpu.SemaphoreType.DMA
    return pl.pallas_call(
        body, out_shape=tuple(jax.ShapeDtypeStruct(a.shape, a.dtype) for a in arrs),
        in_specs=[ANY] * n, out_specs=(ANY,) * n, scratch_shapes=[sem((n,)), sem((n,))], name=name)(*arrs)


def _allreduce_small(slab, *, name):
    r, n = slab.shape

    def body(x_ref, o_ref, buf, ssem, rsem):
        x, y, c = _place()
        me = 4 * x + 2 * y + c
        buf[me] = x_ref[...]
        cps = []
        for rel in range(1, 8):
            bx, by, bc = (rel >> 2) & 1, (rel >> 1) & 1, rel & 1
            px, py, pc = (x + bx) % 2, (y + by) % 2, (c + bc) % 2
            cps.append((_remote(x_ref, buf.at[me], ssem.at[rel - 1], rsem.at[rel - 1], (px, py, pc)),
                        4 * px + 2 * py + pc, (px, py, pc)))
        for cp, _, _ in cps:
            cp.start()
        for rel, (cp, peer, dev) in enumerate(cps):
            cp.wait_send()
            _remote(x_ref, buf.at[peer], ssem.at[rel], rsem.at[rel], dev).wait_recv()
        acc = buf[0]
        for k in range(1, 8):
            acc = acc + buf[k]
        o_ref[...] = acc

    vm = pl.BlockSpec(memory_space=pltpu.VMEM)
    sem = pltpu.SemaphoreType.DMA
    return pl.pallas_call(
        body, out_shape=jax.ShapeDtypeStruct((r, n), F32), in_specs=[vm], out_specs=vm,
        scratch_shapes=[pltpu.VMEM((8, r, n), F32), sem((7,)), sem((7,))], name=name)(slab)


def _slab(arrs, row_mult):
    flat = jnp.concatenate([a.reshape(-1) for a in arrs])
    unit = 128 * row_mult
    total = -(-flat.size // unit) * unit
    return jnp.pad(flat, (0, total - flat.size)).reshape(-1, 128)


def _unslab(slab, shapes):
    flat = slab.reshape(-1)
    out, off = [], 0
    for s in shapes:
        size = int(np.prod(s))
        out.append(flat[off:off + size].reshape(s))
        off += size
    return out


def _cols_from_chips(a):
    return jnp.transpose(a, (1, 0, 2)).reshape(a.shape[1], -1)


def _cols_to_chips(a, parts):
    r = a.shape[0]
    return jnp.transpose(a.reshape(r, parts, -1), (1, 0, 2))


BIG = ("w_in", "w_out", "up", "down")
GATHER_RIDES = {("proj", 0): (("w_out", 0), ("up", 0)), ("mix_out", 0): (("down", 0),),
                ("ffn_up_a", 0): (("w_in", 1), ("w_out", 1)), ("ffn_up_g", 0): (("up", 1),),
                ("ffn_down", 0): (("down", 1),)}
REDUCE_RIDES = {("ffn_down_dx", 0): ("up",), ("ffn_up_a_dx", 0): ("w_in", "w_out"), ("ffn_up_g_dx", 0): ("down",)}


class _LocalWeights:
    def __init__(self, meta, win, wout, up_a, up_g, down, w2p, cw):
        self._meta, self._w = meta, {"win": win, "wout": wout, "up_a": up_a, "up_g": up_g, "down": down, "w2p": w2p,
                                     "cw": cw}

    def meta(self):
        return self._meta

    def get(self, kind, l):
        return self._w[kind][l]

    def mm(self, site, l, a, b, **kw):
        return _mm(a, b, name=site, **kw)

    def grads_done(self, l, g):
        pass


class _ChipWeights:
    def __init__(self, w_in, w_out, ffn_up, ffn_down, meta_tokens, gla_gate_w2, ffn_conv_w):
        self.x, self.y, self.c = _place()
        self.q = 2 * self.x + self.y
        halves = lambda a: a.astype(BF16).reshape(2, a.shape[0] // 2, a.shape[1])
        self.own = {(k, l): halves(a[l]) for k, a in zip(BIG, (w_in, w_out, ffn_up, ffn_down)) for l in range(DEPTH)}
        self.landed, self.swapped, self.full, self.n_swaps = {}, {}, {}, 0
        self.sh_shapes = [meta_tokens.shape, gla_gate_w2.shape, ffn_conv_w.shape]
        self.own["small", 0] = _slab([meta_tokens, gla_gate_w2, ffn_conv_w], 16).reshape(2, -1, 128)
        first = [("w_in", 0), ("small", 0)]
        for key, arr in zip(first, _chip_exchange([self.own[k] for k in first], "bcast", name="gather_first")):
            self.landed[key] = arr
        sh = self._whole("small", 0).reshape(4, -1, 128)
        parts = [_unslab(sh[k], self.sh_shapes) for k in range(4)]
        self._meta = jnp.concatenate([p[0] for p in parts], axis=-1)
        self.w2 = jnp.concatenate([p[1] for p in parts], axis=-1)
        self.cw = jnp.concatenate([p[2] for p in parts], axis=-1)
        self.partial, self.slots = {}, {}

    def _whole(self, kind, l):
        if (kind, l) not in self.full:
            keys = [k for k in self.landed if k not in self.full]
            got = _pair_fill([self.landed[k] for k in keys], [self.own[k] for k in keys],
                             name=f"gather_fill_{self.n_swaps}")
            self.n_swaps += 1
            for k, buf in zip(keys, got):
                self.full[k] = buf.reshape(4, 2 * buf.shape[1], buf.shape[2])
        return self.full[kind, l]

    def meta(self):
        return self._meta

    def get(self, kind, l):
        if kind == "win":
            return _to_kernel_cols(_cols_from_chips(self._whole("w_in", l)))
        if kind == "wout":
            return self._whole("w_out", l).reshape(D_MODEL, D_MODEL)
        if kind == "up_a":
            return _cols_from_chips(self._whole("up", l)[0:2])
        if kind == "up_g":
            return _cols_from_chips(self._whole("up", l)[2:4])
        if kind == "down":
            return self._whole("down", l).reshape(D_FF, D_MODEL)
        if kind == "w2p":
            return jnp.pad(self.w2[l], ((0, 128 - GLA_RANK), (0, 0))).astype(BF16)
        return self.cw[l]

    def mm(self, site, l, a, b, **kw):
        if (site, l) in GATHER_RIDES:
            keys = GATHER_RIDES[site, l]
            out, got = _mm(a, b, name=site, carry=([self.own[k] for k in keys], "bcast"), **kw)
            self.landed.update(zip(keys, got))
            return out
        if (site, l) in REDUCE_RIDES and all((k, DEPTH - 1) in self.partial for k in REDUCE_RIDES[site, l]):
            keys = [(k, DEPTH - 1) for k in REDUCE_RIDES[site, l]]
            out, got = _mm(a, b, name=site, carry=([self.partial[k] for k in keys], "scatter"), **kw)
            self.slots.update(zip(keys, got))
            return out
        return _mm(a, b, name=site, **kw)

    def grads_done(self, l, g):
        split = lambda a: a.reshape(4, 2, a.shape[-2] // 2, a.shape[-1]) if a.ndim == 3 else \
            a.reshape(4, 2, a.shape[0] // 8, a.shape[1])
        big = {"w_in": split(_cols_to_chips(g["w_in"][l], 4)), "w_out": split(g["w_out"][l]),
               "up": split(g["up"][l]), "down": split(g["down"][l])}
        from_sib = _pair_exchange([big[k] for k in BIG], name=f"grads_pair_exchange_{l}")
        for k, theirs in zip(BIG, from_sib):
            self.partial[k, l] = _pair_sum(big[k], theirs, self.c, name=f"pair_sum_{k}_{l}")

    def reduce(self):
        keys = [(k, l) for l in range(DEPTH) for k in BIG]
        late = [k for k in keys if k not in self.slots]
        self.slots.update(zip(late, _chip_exchange([self.partial[k] for k in late], "scatter",
                                                   name="grads_chip_exchange")))
        half = {}
        for k in keys:
            own = lax.dynamic_index_in_dim(self.partial[k], self.q, 0, keepdims=False)
            half[k] = _sum_slots(own, self.slots[k], name=f"chip_sum_{k[0]}_{k[1]}")
        other = dict(zip(keys, _pair_swap([half[k] for k in keys], name="grads_pair_swap")))
        whole = lambda k: jnp.where(self.c == 0, jnp.concatenate([half[k], other[k]], axis=0),
                                    jnp.concatenate([other[k], half[k]], axis=0))
        return [jnp.stack([whole((k, l)) for l in range(DEPTH)]) for k in BIG]


def _local_step(x_rows, target_rows, wts, pre_mix_norm, gla_gate_b, ret_norm_w, gla_norm_w, post_mix_norm,
                pre_ffn_norm, ffn_conv_b, post_ffn_norm):
    d = D_MODEL
    lp = x_rows.shape[0] + FRONT + BACK
    row = lambda a, l: a[l][None, :]
    rtab = _ret_tables(lp)
    gtab = _gla_tables()
    h0 = jnp.concatenate([jnp.zeros((PADF, d), F32), wts.meta(), x_rows, jnp.zeros((BACK, d), F32)], axis=0)
    target = jnp.pad(target_rows, ((FRONT, BACK), (0, 0)))

    saved = []
    h = h0
    _, hn = _resid_norm(h0, None, None, row(pre_mix_norm, 0), name="norm_in")
    loss_local = dy = None
    for l in range(DEPTH):
        s = {"h_in": h, "hn": hn}
        s["proj"] = wts.mm("proj", l, hn, wts.get("win", l))
        s["o_ret"], s["st_ret"] = _retention(s["proj"], rtab, name="retention")
        s["o_gla"], s["st_gla"] = _gla(s["proj"], wts.get("w2p", l), row(gla_gate_b, l), gtab, name="gla")
        s["merged"] = _merge(s["o_ret"], s["o_gla"], s["proj"], row(ret_norm_w, l), row(gla_norm_w, l), name="merge")
        s["m"] = wts.mm("mix_out", l, s["merged"], wts.get("wout", l))
        s["h_mid"], s["hn2"] = _resid_norm(h, s["m"], row(post_mix_norm, l), row(pre_ffn_norm, l), name="resid_mix")
        s["ua"] = wts.mm("ffn_up_a", l, s["hn2"], wts.get("up_a", l))
        s["ug"] = wts.mm("ffn_up_g", l, s["hn2"], wts.get("up_g", l))
        cw_a, cw_g = wts.get("cw", l)[:, :D_FF], wts.get("cw", l)[:, D_FF:]
        cb_a, cb_g = ffn_conv_b[l][None, :D_FF], ffn_conv_b[l][None, D_FF:]
        s["conv"] = (cw_a, cw_g, cb_a, cb_g)
        s["act"] = _conv_act(s["ua"], s["ug"], cw_a, cw_g, cb_a, cb_g, name="conv_act")
        s["f"] = wts.mm("ffn_down", l, s["act"], wts.get("down", l))
        if l + 1 < DEPTH:
            h, hn = _resid_norm(s["h_mid"], s["f"], row(post_ffn_norm, l), row(pre_mix_norm, l + 1), name="resid_ffn")
        else:
            loss_local, dy = _loss_head(s["h_mid"], s["f"], row(post_ffn_norm, l), target, name="loss_head")
        saved.append(s)

    g = {k: [None] * DEPTH for k in ("pre_mix", "w_in", "w2", "gb", "ret_n", "gla_n", "w_out", "post_mix", "pre_ffn",
                                     "up", "cw", "cb", "down", "post_ffn")}
    dh_out, dhn_next = dy, None
    for l in reversed(range(DEPTH)):
        s = saved[l]
        cw_a, cw_g, cb_a, cb_g = s["conv"]
        if l + 1 < DEPTH:
            dh, df, g["pre_mix"][l + 1], g["post_ffn"][l] = _resid_norm_bwd(
                dh_out, dhn_next, saved[l + 1]["h_in"], s["f"], row(pre_mix_norm, l + 1), row(post_ffn_norm, l),
                name="resid_ffn_bwd")
        else:
            dh, df, _, g["post_ffn"][l] = _resid_norm_bwd(dh_out, None, None, s["f"], None, row(post_ffn_norm, l),
                                                          name="loss_head_bwd")
        dact = wts.mm("ffn_down_dx", l, df, wts.get("down", l), nt=True)
        g["down"][l] = _mm_tn(s["act"], df, tn=512, name="ffn_down_dw")
        du_a, du_g, dcw_a, dcw_g, dcb_a, dcb_g = _conv_act_bwd(s["ua"], s["ug"], dact, cw_a, cw_g, cb_a, cb_g,
                                                               name="conv_act_bwd")
        g["cw"][l] = jnp.concatenate([dcw_a, dcw_g], axis=1)
        g["cb"][l] = jnp.concatenate([dcb_a, dcb_g], axis=1)[0]
        half_up = _mm_tn(s["hn2"], du_a, tn=D_FF // 2, blocks=(4, 0), name="ffn_up_a_dw")
        g["up"][l] = _mm_tn(s["hn2"], du_g, tn=D_FF // 2, blocks=(4, 2), into=half_up, name="ffn_up_g_dw")
        dhn2 = wts.mm("ffn_up_a_dx", l, du_a, wts.get("up_a", l), nt=True)
        dhn2 = wts.mm("ffn_up_g_dx", l, du_g, wts.get("up_g", l), nt=True, add=dhn2)
        dh, dm, g["pre_ffn"][l], g["post_mix"][l] = _resid_norm_bwd(
            dh, dhn2, s["h_mid"], s["m"], row(pre_ffn_norm, l), row(post_mix_norm, l), name="resid_mix_bwd")
        g["w_out"][l] = _mm_tn(s["merged"], dm, name="mix_out_dw")
        dmerged = wts.mm("mix_out_dx", l, dm, wts.get("wout", l), nt=True)
        do_ret, do_gla, d_gate, g["ret_n"][l], g["gla_n"][l] = _merge_bwd(
            dmerged, s["o_ret"], s["o_gla"], s["proj"], row(ret_norm_w, l), row(gla_norm_w, l), name="merge_bwd")
        d_ret = _retention_bwd(s["proj"], do_ret, s["st_ret"], rtab, name="retention_bwd")
        d_gla, dw2, dgb = _gla_bwd(s["proj"], do_gla, s["st_gla"], wts.get("w2p", l), row(gla_gate_b, l), gtab,
                                   name="gla_bwd")
        g["w2"][l], g["gb"][l] = dw2[:GLA_RANK], dgb[0]
        pieces = (d_ret, d_gate, d_gla)
        g["w_in"][l] = _to_reference_cols(*[_mm_tn(s["hn"], p, name=f"proj_dw_{i}") for i, p in enumerate(pieces)])
        win = wts.get("win", l)
        dhn_next = _mm_nt_sum(pieces, [win[:, 0:P_RET], win[:, P_RET:P_RET + P_GATE], win[:, P_RET + P_GATE:]],
                              name="proj_dx")
        dh_out = dh
        wts.grads_done(l, g)
    dh0, _, g["pre_mix"][0], _ = _resid_norm_bwd(dh_out, dhn_next, h0, None, row(pre_mix_norm, 0), None,
                                                 name="norm_in_bwd")
    return loss_local, dh0, g


def kernel(x, meta_tokens, pre_mix_norm, w_in, gla_gate_w2, gla_gate_b, ret_norm_w, gla_norm_w, w_out, post_mix_norm, pre_ffn_norm, ffn_up, ffn_conv_w, ffn_conv_b, ffn_down, post_ffn_norm, loss_target, m_meta_tokens, m_pre_mix_norm, m_w_in, m_gla_gate_w2, m_gla_gate_b, m_ret_norm_w, m_gla_norm_w, m_w_out, m_post_mix_norm, m_pre_ffn_norm, m_ffn_up, m_ffn_conv_w, m_ffn_conv_b, m_ffn_down, m_post_ffn_norm, v_meta_tokens, v_pre_mix_norm, v_w_in, v_gla_gate_w2, v_gla_gate_b, v_ret_norm_w, v_gla_norm_w, v_w_out, v_post_mix_norm, v_pre_ffn_norm, v_ffn_up, v_ffn_conv_w, v_ffn_conv_b, v_ffn_down, v_post_ffn_norm):
    xi, yi, ci = _place()
    chip = 2 * xi + yi
    seq = x.shape[1]
    d = D_MODEL
    wts = _ChipWeights(w_in, w_out, ffn_up, ffn_down, meta_tokens, gla_gate_w2, ffn_conv_w)
    loss_local, dh0, g = _local_step(x[0], loss_target[0], wts, pre_mix_norm, gla_gate_b, ret_norm_w, gla_norm_w,
                                     post_mix_norm, pre_ffn_norm, ffn_conv_b, post_ffn_norm)
    grad_x = dh0[FRONT:FRONT + seq][None]
    names = ("w_in", "w_out", "ffn_up", "ffn_down")
    g_w_in, g_w_out, g_ffn_up, g_ffn_down = wts.reduce()

    small_full = [dh0[PADF:FRONT], jnp.stack(g["pre_mix"])[:, 0], jnp.stack(g["w2"]), jnp.stack(g["gb"]),
                  jnp.stack(g["ret_n"])[:, 0], jnp.stack(g["gla_n"])[:, 0], jnp.stack(g["post_mix"])[:, 0],
                  jnp.stack(g["pre_ffn"])[:, 0], jnp.stack(g["cw"]), jnp.stack(g["cb"]),
                  jnp.stack(g["post_ffn"])[:, 0]]
    small_sum = _unslab(_allreduce_small(_slab(small_full, 8), name="small_allreduce"), [a.shape for a in small_full])
    (g_meta, g_pre_mix, g_w2, g_gb, g_ret_n, g_gla_n, g_post_mix, g_pre_ffn, g_cw, g_cb, g_post_ffn) = small_sum
    g_meta = lax.dynamic_slice_in_dim(g_meta, chip * 256, 256, axis=1)
    g_w2 = lax.dynamic_slice_in_dim(g_w2, chip * 64, 64, axis=2)
    g_cw = lax.dynamic_slice_in_dim(g_cw, chip * 1408, 1408, axis=2)

    grads = [g_meta, g_pre_mix, g_w_in, g_w2, g_gb, g_ret_n, g_gla_n, g_w_out, g_post_mix, g_pre_ffn, g_ffn_up,
             g_cw, g_cb, g_ffn_down, g_post_ffn]
    ws = [meta_tokens, pre_mix_norm, w_in, gla_gate_w2, gla_gate_b, ret_norm_w, gla_norm_w, w_out, post_mix_norm,
          pre_ffn_norm, ffn_up, ffn_conv_w, ffn_conv_b, ffn_down, post_ffn_norm]
    ms = [m_meta_tokens, m_pre_mix_norm, m_w_in, m_gla_gate_w2, m_gla_gate_b, m_ret_norm_w, m_gla_norm_w, m_w_out,
          m_post_mix_norm, m_pre_ffn_norm, m_ffn_up, m_ffn_conv_w, m_ffn_conv_b, m_ffn_down, m_post_ffn_norm]
    vs = [v_meta_tokens, v_pre_mix_norm, v_w_in, v_gla_gate_w2, v_gla_gate_b, v_ret_norm_w, v_gla_norm_w, v_w_out,
          v_post_mix_norm, v_pre_ffn_norm, v_ffn_up, v_ffn_conv_w, v_ffn_conv_b, v_ffn_down, v_post_ffn_norm]
    big_idx = (2, 7, 10, 13)
    deltas, new_m, new_v = [None] * 15, [None] * 15, [None] * 15
    for i, nm in zip(big_idx, names):
        deltas[i], new_m[i], new_v[i] = _adamw(ws[i], grads[i], ms[i], vs[i], name=f"adamw_{nm}")
    small_idx = [i for i in range(15) if i not in big_idx]
    shapes = [ws[i].shape for i in small_idx]
    sd, sm, sv = _adamw(_slab([ws[i] for i in small_idx], 8), _slab([grads[i] for i in small_idx], 8),
                        _slab([ms[i] for i in small_idx], 8), _slab([vs[i] for i in small_idx], 8), name="adamw_small")
    for i, a, b, c_ in zip(small_idx, _unslab(sd, shapes), _unslab(sm, shapes), _unslab(sv, shapes)):
        deltas[i], new_m[i], new_v[i] = a, b, c_

    loss = lax.psum(loss_local, ("x", "y", "c"))
    return (loss, grad_x, *grads, *deltas, *new_m, *new_v)
```

```python
import functools
import math

import numpy as np
import jax
import jax.numpy as jnp
from jax import lax
from jax.experimental import pallas as pl
from jax.experimental.pallas import tpu as pltpu

F32 = jnp.float32
BF16 = jnp.bfloat16

D_MODEL = 1024
DEPTH = 2
N_META = 16
EPS = 1e-6
RET_HEADS = 4
RET_DK = 128
GLA_HEADS = 4
GLA_DK = 64
GLA_DV = 128
GLA_QK = GLA_HEADS * GLA_DK
GLA_V = GLA_HEADS * GLA_DV
GLA_RANK = 16
GLA_TAU = 16.0
D_FF = 2816
ROPE_BASE = 10000.0
IN_WIDTH = 3600
IN_PAD = 3840
C_RQ, C_RK, C_RV, C_RG, C_GR, C_GQ, C_GK, C_GV, C_GA = 0, 512, 1024, 1536, 2048, 2560, 2816, 3072, 3584
P_RET, P_GATE, P_GLA = 1536, 1024, 1280


def _to_kernel_cols(w):
    pad = jnp.zeros(w.shape[:-1] + (IN_PAD - IN_WIDTH,), w.dtype)
    return jnp.concatenate([w[..., 0:2048], w[..., 3072:3584], w[..., 2048:3072], w[..., 3584:3600], pad], axis=-1)


def _to_reference_cols(d_ret, d_gate, d_gla):
    return jnp.concatenate([d_ret, d_gate[..., 0:512], d_gla[..., 0:1024], d_gate[..., 512:1024],
                            d_gla[..., 1024:1024 + GLA_RANK]], axis=-1)

FRONT = 64
BACK = 64
PADF = FRONT - N_META
RET_CHUNK = 128
GLA_CHUNK = 64
GLA_SUB = 16
GLA_SUB2 = 4
BLK = 640

ADAM_LR, ADAM_B1, ADAM_B2, ADAM_EPS, ADAM_WD, ADAM_STEP = 0.001, 0.9, 0.999, 1e-08, 0.01, 10

VMEM_LIMIT = 56 * 2 ** 20
MM_VMEM_BUDGET = 40 * 2 ** 20
MESH = pl.DeviceIdType.MESH


def _cp(*sem):
    return pltpu.CompilerParams(dimension_semantics=sem, vmem_limit_bytes=VMEM_LIMIT)


def _tile(n, cands):
    for t in cands:
        if n % t == 0:
            return t
    raise ValueError(f"no tile for {n} in {cands}")


def _row_tile(n):
    return _tile(n, (640, 512, 320, 256, 128, 64))


def _mm(a, b, *, nt=False, add=None, out_dtype=F32, tn=None, name, carry=None):
    m, k = a.shape
    n = b.shape[0] if nt else b.shape[1]
    tm = _tile(m, (640, 320, 256, 128, 64))
    if tn is None:
        step_bytes = lambda t: 2 * (tm * k * a.dtype.itemsize + t * k * b.dtype.itemsize
                                    + tm * t * (jnp.dtype(out_dtype).itemsize + (4 if add is not None else 0)))
        tn = next(t for t in range(n, 0, -128) if n % t == 0 and (step_bytes(t) <= MM_VMEM_BUDGET or t == 128))
    dn = (((1,), (1,)), ((), ())) if nt else (((1,), (0,)), ((), ()))
    nj, ni = n // tn, m // tm
    n_in = 2 + (add is not None)
    c_arrs, c_mode = carry if carry is not None else ((), None)
    nc = len(c_arrs)

    def body(*refs):
        a_ref, b_ref = refs[:2]
        c_ref = refs[2] if add is not None else None
        o_ref = refs[n_in + nc]
        if nc:
            c_ins, c_outs = refs[n_in:n_in + nc], refs[n_in + nc + 1:n_in + 2 * nc + 1]
            ssem, rsem = refs[n_in + 2 * nc + 1:]
            j, i = pl.program_id(0), pl.program_id(1)

            @pl.when((j == 0) & (i == 0))
            def _():
                for cp in _chip_copies(c_ins, c_outs, ssem, rsem, c_mode)[0]:
                    cp.start()
        r = lax.dot_general(a_ref[...].astype(BF16), b_ref[...].astype(BF16), dn, preferred_element_type=F32)
        if add is not None:
            r = r + c_ref[...]
        o_ref[...] = r.astype(o_ref.dtype)
        if nc:
            @pl.when((j == nj - 1) & (i == ni - 1))
            def _():
                _chip_wait(*_chip_copies(c_ins, c_outs, ssem, rsem, c_mode))

    b_spec = pl.BlockSpec((tn, k), lambda j, i: (j, 0)) if nt else pl.BlockSpec((k, tn), lambda j, i: (0, j))
    in_specs = [pl.BlockSpec((tm, k), lambda j, i: (i, 0)), b_spec]
    args = [a, b]
    if add is not None:
        in_specs.append(pl.BlockSpec((tm, tn), lambda j, i: (i, j)))
        args.append(add)
    out_shape = jax.ShapeDtypeStruct((m, n), out_dtype)
    out_spec = pl.BlockSpec((tm, tn), lambda j, i: (i, j))
    if not nc:
        return pl.pallas_call(
            body, out_shape=out_shape, grid=(nj, ni), in_specs=in_specs, out_specs=out_spec,
            compiler_params=_cp("parallel", "parallel"), name=name)(*args)
    sem = pltpu.SemaphoreType.DMA
    outs = pl.pallas_call(
        body, out_shape=(out_shape,) + tuple(_landing_shape(x, c_mode) for x in c_arrs), grid=(nj, ni),
        in_specs=in_specs + [ANY] * nc, out_specs=(out_spec,) + (ANY,) * nc,
        scratch_shapes=[sem((nc, 3)), sem((nc, 3))],
        compiler_params=_cp("arbitrary", "arbitrary"), name=name)(*args, *c_arrs)
    return outs[0], list(outs[1:])


def _mm_nt_sum(a_list, b_list, *, name):
    m, n = a_list[0].shape[0], b_list[0].shape[0]
    tm = _tile(m, (640, 320, 256, 128, 64))
    np_ = len(a_list)

    def body(*refs):
        acc = None
        for a_ref, b_ref in zip(refs[:np_], refs[np_:2 * np_]):
            r = lax.dot_general(a_ref[...].astype(BF16), b_ref[...].astype(BF16), (((1,), (1,)), ((), ())),
                                preferred_element_type=F32)
            acc = r if acc is None else acc + r
        refs[2 * np_][...] = acc

    return pl.pallas_call(
        body, out_shape=jax.ShapeDtypeStruct((m, n), F32), grid=(m // tm,),
        in_specs=[pl.BlockSpec((tm, a.shape[1]), lambda i: (i, 0)) for a in a_list]
        + [pl.BlockSpec(b.shape, lambda i: (0, 0)) for b in b_list],
        out_specs=pl.BlockSpec((tm, n), lambda i: (i, 0)),
        compiler_params=_cp("parallel"), name=name)(*a_list, *b_list)


def _mm_tn(a, b, *, tn=None, blocks=None, into=None, name):
    m, k = a.shape
    n = b.shape[1]
    tm = _tile(m, (1664, 640, 320, 256, 128, 64))
    tn = n if tn is None else tn
    if blocks is not None:
        total, first = blocks
        out_shape = jax.ShapeDtypeStruct((total, k, tn), F32)
        out_spec = pl.BlockSpec((None, k, tn), lambda j, i: (first + j, 0, 0))
    else:
        out_shape = jax.ShapeDtypeStruct((k, n), F32)
        out_spec = pl.BlockSpec((k, tn), lambda j, i: (0, j))

    def body(a_ref, b_ref, *rest):
        o_ref = rest[-1]

        @pl.when(pl.program_id(1) == 0)
        def _():
            o_ref[...] = jnp.zeros_like(o_ref)
        o_ref[...] += lax.dot_general(a_ref[...].astype(BF16), b_ref[...].astype(BF16),
                                      (((0,), (0,)), ((), ())), preferred_element_type=F32)

    in_specs = [pl.BlockSpec((tm, k), lambda j, i: (i, 0)), pl.BlockSpec((tm, tn), lambda j, i: (i, j))]
    args, alias = [a, b], {}
    if into is not None:
        in_specs.append(pl.BlockSpec(memory_space=pl.ANY))
        args.append(into)
        alias = {2: 0}
    return pl.pallas_call(
        body, out_shape=out_shape, grid=(n // tn, m // tm), in_specs=in_specs, out_specs=out_spec,
        input_output_aliases=alias, compiler_params=_cp("parallel", "arbitrary"), name=name)(*args)


def _rms(x, w):
    r = lax.rsqrt(jnp.mean(x * x, axis=-1, keepdims=True) + EPS)
    return x * r * w


def _rms_bwd(x, w, dy):
    r = lax.rsqrt(jnp.mean(x * x, axis=-1, keepdims=True) + EPS)
    xh = x * r
    dxh = dy * w
    dx = r * (dxh - xh * jnp.mean(dxh * xh, axis=-1, keepdims=True))
    return dx, jnp.sum(dy * xh, axis=0, keepdims=True)


def _resid_norm(h, t, w_post, w_next, *, name):
    lp, d = h.shape
    tm = _row_tile(lp)
    has_t = t is not None

    def body(*refs):
        if has_t:
            h_ref, t_ref, wp_ref, wn_ref, ho_ref, hn_ref = refs
            hv = h_ref[...] + _rms(t_ref[...], wp_ref[...])
            ho_ref[...] = hv
        else:
            h_ref, wn_ref, hn_ref = refs
            hv = h_ref[...]
        hn_ref[...] = _rms(hv, wn_ref[...]).astype(BF16)

    row = pl.BlockSpec((tm, d), lambda i: (i, 0))
    vec = pl.BlockSpec((1, d), lambda i: (0, 0))
    if has_t:
        return pl.pallas_call(
            body, out_shape=(jax.ShapeDtypeStruct((lp, d), F32), jax.ShapeDtypeStruct((lp, d), BF16)),
            grid=(lp // tm,), in_specs=[row, row, vec, vec], out_specs=(row, row),
            compiler_params=_cp("parallel"), name=name)(h, t, w_post, w_next)
    return h, pl.pallas_call(
        body, out_shape=jax.ShapeDtypeStruct((lp, d), BF16), grid=(lp // tm,), in_specs=[row, vec],
        out_specs=row, compiler_params=_cp("parallel"), name=name)(h, w_next)


def _resid_norm_bwd(dh_out, dhn, h_new, t, w_next, w_post, *, name):
    lp, d = h_new.shape if h_new is not None else t.shape
    tm = _row_tile(lp)
    has_n = dhn is not None
    has_t = t is not None

    def body(*refs):
        refs = list(refs)
        dho_ref = refs.pop(0)
        if has_n:
            dhn_ref, hn_ref, wn_ref = refs.pop(0), refs.pop(0), refs.pop(0)
        if has_t:
            t_ref, wp_ref = refs.pop(0), refs.pop(0)
        dh_ref = refs.pop(0) if has_n else None
        dt_ref = refs.pop(0) if has_t else None
        dwn_ref = refs.pop(0) if has_n else None
        dwp_ref = refs.pop(0) if has_t else None
        first = pl.program_id(0) == 0
        dh = dho_ref[...]
        if has_n:
            dx, dwn = _rms_bwd(hn_ref[...], wn_ref[...], dhn_ref[...])
            dh = dh + dx
            dh_ref[...] = dh

            @pl.when(first)
            def _():
                dwn_ref[...] = jnp.zeros_like(dwn_ref)
            dwn_ref[...] += dwn
        if has_t:
            dt, dwp = _rms_bwd(t_ref[...], wp_ref[...], dh)
            dt_ref[...] = dt.astype(BF16)

            @pl.when(first)
            def _():
                dwp_ref[...] = jnp.zeros_like(dwp_ref)
            dwp_ref[...] += dwp

    row = pl.BlockSpec((tm, d), lambda i: (i, 0))
    vec = pl.BlockSpec((1, d), lambda i: (0, 0))
    args, in_specs, out_shape, out_specs = [dh_out], [row], [], []
    if has_n:
        args += [dhn, h_new, w_next]
        in_specs += [row, row, vec]
    if has_t:
        args += [t, w_post]
        in_specs += [row, vec]
    if has_n:
        out_shape.append(jax.ShapeDtypeStruct((lp, d), F32)); out_specs.append(row)
    if has_t:
        out_shape.append(jax.ShapeDtypeStruct((lp, d), BF16)); out_specs.append(row)
    if has_n:
        out_shape.append(jax.ShapeDtypeStruct((1, d), F32)); out_specs.append(vec)
    if has_t:
        out_shape.append(jax.ShapeDtypeStruct((1, d), F32)); out_specs.append(vec)
    outs = list(pl.pallas_call(body, out_shape=tuple(out_shape), grid=(lp // tm,), in_specs=in_specs,
                               out_specs=tuple(out_specs), compiler_params=_cp("arbitrary"), name=name)(*args))
    dh = outs.pop(0) if has_n else dh_out
    dt = outs.pop(0) if has_t else None
    dwn = outs.pop(0) if has_n else None
    dwp = outs.pop(0) if has_t else None
    return dh, dt, dwn, dwp


def _loss_head(h, f, w_post, target, *, name):
    lp, d = h.shape
    tm = _row_tile(lp)

    def body(h_ref, f_ref, w_ref, t_ref, loss_ref, dy_ref):
        i = pl.program_id(0)
        y = h_ref[...] + _rms(f_ref[...], w_ref[...])
        rows = i * tm + lax.broadcasted_iota(jnp.int32, (tm, 1), 0)
        tok = (rows >= FRONT) & (rows < lp - BACK)
        err = jnp.where(tok, y - t_ref[...], 0.0)
        dy_ref[...] = err * (1.0 / d)

        @pl.when(i == 0)
        def _():
            loss_ref[...] = jnp.zeros_like(loss_ref)
        part = jnp.sum(jnp.sum(err * err, axis=1, keepdims=True), axis=0, keepdims=True) * (0.5 / d)
        loss_ref[...] += jnp.broadcast_to(part, loss_ref.shape)

    row = pl.BlockSpec((tm, d), lambda i: (i, 0))
    loss, dy = pl.pallas_call(
        body, out_shape=(jax.ShapeDtypeStruct((8, 128), F32), jax.ShapeDtypeStruct((lp, d), F32)),
        grid=(lp // tm,), in_specs=[row, row, pl.BlockSpec((1, d), lambda i: (0, 0)), row],
        out_specs=(pl.BlockSpec((8, 128), lambda i: (0, 0)), row),
        compiler_params=_cp("arbitrary"), name=name)(h, f, w_post, target)
    return loss[0, 0], dy


_GELU_C = math.sqrt(2.0 / math.pi)


def _gelu_and_grad(a):
    a2 = a * a
    t = jnp.tanh(a * (_GELU_C + (_GELU_C * 0.044715) * a2))
    ha = 0.5 * a
    h1 = 0.5 + 0.5 * t
    return a * h1, h1 + ha * (1.0 - t * t) * (_GELU_C + (3.0 * _GELU_C * 0.044715) * a2)


def _gelu(a):
    t = jnp.tanh(a * (_GELU_C + (_GELU_C * 0.044715) * (a * a)))
    return a * (0.5 + 0.5 * t)


def _conv3(parts, n, w, b):
    xx = jnp.concatenate(parts, axis=0)
    return b + xx[8:8 + n] * w[2:3] + pltpu.roll(xx, 1, 0)[8:8 + n] * w[1:2] + pltpu.roll(xx, 2, 0)[8:8 + n] * w[0:1]


def _conv_act(ua, ug, wa, wg, ba, bg, *, name):
    lp, n = ua.shape
    tm = _row_tile(lp)
    tc = _tile(n, (256, 128))
    nb8 = tm // 8

    def body(ua_ref, uap_ref, ug_ref, ugp_ref, wa_ref, wg_ref, ba_ref, bg_ref, o_ref):
        i = pl.program_id(0)
        ca = _conv3([uap_ref[...], ua_ref[...]], tm, wa_ref[...], ba_ref[...])
        cg = _conv3([ugp_ref[...], ug_ref[...]], tm, wg_ref[...], bg_ref[...])
        rows = i * tm + lax.broadcasted_iota(jnp.int32, (tm, 1), 0)
        ok = (rows >= PADF) & (rows < lp - BACK)
        o_ref[...] = jnp.where(ok, _gelu(ca) * cg, 0.0).astype(BF16)

    cur = pl.BlockSpec((tm, tc), lambda i, j: (i, j))
    prev = pl.BlockSpec((8, tc), lambda i, j: (jnp.maximum(i * nb8 - 1, 0), j))
    w3 = pl.BlockSpec((3, tc), lambda i, j: (0, j))
    b1 = pl.BlockSpec((1, tc), lambda i, j: (0, j))
    return pl.pallas_call(
        body, out_shape=jax.ShapeDtypeStruct((lp, n), BF16), grid=(lp // tm, n // tc),
        in_specs=[cur, prev, cur, prev, w3, w3, b1, b1], out_specs=cur,
        compiler_params=_cp("parallel", "parallel"), name=name)(ua, ua, ug, ug, wa, wg, ba, bg)


def _conv_act_bwd(ua, ug, dact, wa, wg, ba, bg, *, name):
    lp, n = ua.shape
    tm = _row_tile(lp)
    tc = _tile(n, (256, 128))
    nb8 = tm // 8
    last8 = lp // 8 - 1
    ext = tm + 8

    def body(ua_ref, uap_ref, uan_ref, ug_ref, ugp_ref, ugn_ref, da_ref, dan_ref, wa_ref, wg_ref, ba_ref, bg_ref,
             dua_ref, dug_ref, dwa_ref, dwg_ref, dba_ref, dbg_ref):
        i = pl.program_id(1)
        wa, wg = wa_ref[...], wg_ref[...]

        def conv(parts, w, b):
            xx = jnp.concatenate(parts, axis=0)
            x, x1, x2 = xx[8:8 + ext], pltpu.roll(xx, 1, 0)[8:8 + ext], pltpu.roll(xx, 2, 0)[8:8 + ext]
            return b + x * w[2:3] + x1 * w[1:2] + x2 * w[0:1], x, x1, x2

        ca, xa, xa1, xa2 = conv([uap_ref[...], ua_ref[...], uan_ref[...]], wa, ba_ref[...])
        cg, xg, xg1, xg2 = conv([ugp_ref[...], ug_ref[...], ugn_ref[...]], wg, bg_ref[...])
        rows = i * tm + lax.broadcasted_iota(jnp.int32, (ext, 1), 0)
        ok = (rows >= PADF) & (rows < lp - BACK)
        dact_e = jnp.where(ok, jnp.concatenate([da_ref[...], dan_ref[...]], axis=0), 0.0)
        gel, gel_d = _gelu_and_grad(ca)
        dca = dact_e * cg * gel_d
        dcg = dact_e * gel

        def back(dc, w):
            return (dc[:tm] * w[2:3] + pltpu.roll(dc, ext - 1, 0)[:tm] * w[1:2]
                    + pltpu.roll(dc, ext - 2, 0)[:tm] * w[0:1])

        dua_ref[...] = back(dca, wa).astype(BF16)
        dug_ref[...] = back(dcg, wg).astype(BF16)

        @pl.when(i == 0)
        def _():
            dwa_ref[...] = jnp.zeros_like(dwa_ref)
            dwg_ref[...] = jnp.zeros_like(dwg_ref)
            dba_ref[...] = jnp.zeros_like(dba_ref)
            dbg_ref[...] = jnp.zeros_like(dbg_ref)

        def wsum(dw_ref, db_ref, dc, x, x1, x2):
            d = dc[:tm]
            s = lambda v: jnp.sum(v, axis=0, keepdims=True)
            dw_ref[0:1, :] += s(d * x2[:tm])
            dw_ref[1:2, :] += s(d * x1[:tm])
            dw_ref[2:3, :] += s(d * x[:tm])
            db_ref[...] += s(d)

        wsum(dwa_ref, dba_ref, dca, xa, xa1, xa2)
        wsum(dwg_ref, dbg_ref, dcg, xg, xg1, xg2)

    cur = pl.BlockSpec((tm, tc), lambda j, i: (i, j))
    prev = pl.BlockSpec((8, tc), lambda j, i: (jnp.maximum(i * nb8 - 1, 0), j))
    nxt = pl.BlockSpec((8, tc), lambda j, i: (jnp.minimum((i + 1) * nb8, last8), j))
    w3 = pl.BlockSpec((3, tc), lambda j, i: (0, j))
    b1 = pl.BlockSpec((1, tc), lambda j, i: (0, j))
    return pl.pallas_call(
        body,
        out_shape=(jax.ShapeDtypeStruct((lp, n), BF16), jax.ShapeDtypeStruct((lp, n), BF16),
                   jax.ShapeDtypeStruct((3, n), F32), jax.ShapeDtypeStruct((3, n), F32),
                   jax.ShapeDtypeStruct((1, n), F32), jax.ShapeDtypeStruct((1, n), F32)),
        grid=(n // tc, lp // tm),
        in_specs=[cur, prev, nxt, cur, prev, nxt, cur, nxt, w3, w3, b1, b1],
        out_specs=(cur, cur, w3, w3, b1, b1),
        compiler_params=_cp("parallel", "arbitrary"), name=name)(ua, ua, ua, ug, ug, ug, dact, dact, wa, wg, ba, bg)


def _sigmoid(x):
    return 1.0 / (1.0 + jnp.exp(-x))


def _merge(o_ret, o_gla, proj, w_ret, w_gla, *, name):
    lp = o_ret.shape[0]
    tm = _row_tile(lp)

    def body(or_ref, og_ref, rg_ref, gr_ref, wr_ref, wg_ref, m_ref):
        oret, ogla = or_ref[...], og_ref[...]
        yr, yg = [], []
        for h in range(4):
            hs = slice(128 * h, 128 * h + 128)
            o = oret[:, hs]
            xc = o - jnp.mean(o, axis=-1, keepdims=True)
            yr.append(xc * lax.rsqrt(jnp.mean(xc * xc, axis=-1, keepdims=True) + EPS))
            o = ogla[:, hs]
            yg.append(o * lax.rsqrt(jnp.mean(o * o, axis=-1, keepdims=True) + EPS))
        rg, gr = rg_ref[...], gr_ref[...]
        m_ref[:, 0:512] = (jnp.concatenate(yr, axis=1) * wr_ref[...] * (rg * _sigmoid(rg))).astype(BF16)
        m_ref[:, 512:1024] = (jnp.concatenate(yg, axis=1) * wg_ref[...] * (gr * _sigmoid(gr))).astype(BF16)

    row = pl.BlockSpec((tm, 512), lambda i: (i, 0))
    vec = pl.BlockSpec((1, 512), lambda i: (0, 0))
    return pl.pallas_call(
        body, out_shape=jax.ShapeDtypeStruct((lp, 1024), BF16), grid=(lp // tm,),
        in_specs=[row, row, pl.BlockSpec((tm, 512), lambda i: (i, C_RG // 512)),
                  pl.BlockSpec((tm, 512), lambda i: (i, C_GR // 512)), vec, vec],
        out_specs=pl.BlockSpec((tm, 1024), lambda i: (i, 0)),
        compiler_params=_cp("parallel"), name=name)(o_ret, o_gla, proj, proj, w_ret, w_gla)


def _merge_bwd(dm, o_ret, o_gla, proj, w_ret, w_gla, *, name):
    lp = o_ret.shape[0]
    tm = _row_tile(lp)

    def body(dm_ref, or_ref, og_ref, rg_ref, gr_ref, wr_ref, wg_ref, dor_ref, dog_ref, dgate_ref, dwr_ref, dwg_ref):
        @pl.when(pl.program_id(0) == 0)
        def _():
            dwr_ref[...] = jnp.zeros_like(dwr_ref)
            dwg_ref[...] = jnp.zeros_like(dwg_ref)

        def group(d, o_all, gate, w, center):
            sg = _sigmoid(gate)
            s = gate * sg
            ds = sg * (1.0 + gate * (1.0 - sg))
            xh, rr = [], []
            for h in range(4):
                o = o_all[:, 128 * h:128 * h + 128]
                if center:
                    o = o - jnp.mean(o, axis=-1, keepdims=True)
                r = lax.rsqrt(jnp.mean(o * o, axis=-1, keepdims=True) + EPS)
                xh.append(o * r)
                rr.append(r)
            xh_all = jnp.concatenate(xh, axis=1)
            dgate = d * xh_all * w * ds
            dw = jnp.sum(d * xh_all * s, axis=0, keepdims=True)
            dxh_all = d * w * s
            do = []
            for h in range(4):
                dxh = dxh_all[:, 128 * h:128 * h + 128]
                t = dxh - xh[h] * jnp.mean(dxh * xh[h], axis=-1, keepdims=True)
                if center:
                    t = t - jnp.mean(dxh, axis=-1, keepdims=True)
                do.append(rr[h] * t)
            return jnp.concatenate(do, axis=1), dgate, dw

        dmv = dm_ref[...]
        do, dg, dw = group(dmv[:, 0:512], or_ref[...], rg_ref[...], wr_ref[...], True)
        dor_ref[...] = do
        dgate_ref[:, 0:512] = dg.astype(BF16)
        dwr_ref[...] += dw
        do, dg, dw = group(dmv[:, 512:1024], og_ref[...], gr_ref[...], wg_ref[...], False)
        dog_ref[...] = do
        dgate_ref[:, 512:1024] = dg.astype(BF16)
        dwg_ref[...] += dw

    row = pl.BlockSpec((tm, 512), lambda i: (i, 0))
    vec = pl.BlockSpec((1, 512), lambda i: (0, 0))
    return pl.pallas_call(
        body,
        out_shape=(jax.ShapeDtypeStruct((lp, 512), F32), jax.ShapeDtypeStruct((lp, 512), F32),
                   jax.ShapeDtypeStruct((lp, P_GATE), BF16),
                   jax.ShapeDtypeStruct((1, 512), F32), jax.ShapeDtypeStruct((1, 512), F32)),
        grid=(lp // tm,),
        in_specs=[pl.BlockSpec((tm, 1024), lambda i: (i, 0)), row, row,
                  pl.BlockSpec((tm, 512), lambda i: (i, C_RG // 512)),
                  pl.BlockSpec((tm, 512), lambda i: (i, C_GR // 512)), vec, vec],
        out_specs=(row, row, pl.BlockSpec((tm, P_GATE), lambda i: (i, 0)), vec, vec),
        compiler_params=_cp("arbitrary"), name=name)(dm, o_ret, o_gla, proj, proj, w_ret, w_gla)


def _dot(a, b):
    return lax.dot_general(a, b, (((1,), (0,)), ((), ())), preferred_element_type=F32)


def _dot_nt(a, b):
    return lax.dot_general(a, b, (((1,), (1,)), ((), ())), preferred_element_type=F32)


def _dot_tn(a, b):
    return lax.dot_general(a, b, (((0,), (0,)), ((), ())), preferred_element_type=F32)


def _ret_tables(lp):
    cr = RET_CHUNK
    pos = np.arange(lp, dtype=np.float32) - np.float32(PADF)
    half = RET_DK // 2
    inv = (np.float32(ROPE_BASE) ** (-np.arange(half, dtype=np.float32) / np.float32(half))).astype(np.float32)
    ang = (pos[:, None] * inv[None, :]).astype(np.float32)
    c, s = np.cos(ang).astype(np.float32), np.sin(ang).astype(np.float32)
    rope_c = jnp.asarray(np.concatenate([c, c], axis=1))
    rope_s = jnp.asarray(np.concatenate([-s, s], axis=1))
    log_g = np.log(1.0 - 2.0 ** (-5.0 - np.arange(RET_HEADS, dtype=np.float64)))
    idx = np.arange(cr, dtype=np.float64)
    diff = idx[:, None] - idx[None, :]
    dmat = np.where(diff >= 0, np.exp(log_g[:, None, None] * np.maximum(diff, 0.0)), 0.0)
    zeta = np.exp(log_g[:, None] * (cr - 1.0 - idx)[None, :])
    xi = np.exp(log_g[:, None] * (idx + 1.0)[None, :])
    gc = np.exp(log_g * cr)
    f = lambda a: jnp.asarray(a.astype(np.float32))
    return (rope_c, rope_s, f(dmat), f(np.broadcast_to(zeta[:, :, None], (RET_HEADS, cr, 128))),
            f(np.broadcast_to(xi[:, :, None], (RET_HEADS, cr, 128))),
            f(np.broadcast_to(gc[:, None, None], (RET_HEADS, 8, 128))))


def _rope(t, c, s):
    return t * c + pltpu.roll(t, 64, 1) * s


def _rope_t(d, c, s):
    return d * c + pltpu.roll(d * s, 64, 1)


def _ret_specs(nblk, rev):
    ix = (lambda i: nblk - 1 - i) if rev else (lambda i: i)
    cr = RET_CHUNK
    col = lambda base: pl.BlockSpec((BLK, 512), lambda i: (ix(i), base // 512))
    tab = pl.BlockSpec((BLK, 128), lambda i: (ix(i), 0))
    sq = pl.BlockSpec((RET_HEADS, cr, cr), lambda i: (0, 0, 0))
    hv = pl.BlockSpec((RET_HEADS, cr, 128), lambda i: (0, 0, 0))
    g8 = pl.BlockSpec((RET_HEADS, 8, 128), lambda i: (0, 0, 0))
    st = pl.BlockSpec((RET_HEADS, BLK // cr, 128, 128), lambda i: (0, ix(i), 0, 0))
    out = pl.BlockSpec((BLK, 512), lambda i: (ix(i), 0))
    return col, tab, sq, hv, g8, st, out


def _retention(proj, tables, *, name):
    lp = proj.shape[0]
    nblk, cr = lp // BLK, RET_CHUNK
    scale = RET_DK ** -0.5

    def body(q_ref, k_ref, v_ref, c_ref, s_ref, d_ref, z_ref, x_ref, g_ref, o_ref, st_ref, state):
        @pl.when(pl.program_id(0) == 0)
        def _():
            state[...] = jnp.zeros_like(state)

        def chunk(ci, carry):
            sl = pl.ds(pl.multiple_of(ci * cr, cr), cr)
            c, s = c_ref[sl, :], s_ref[sl, :]
            for h in range(RET_HEADS):
                hs = slice(128 * h, 128 * h + 128)
                q = _rope(q_ref[sl, hs], c, s)
                k = _rope(k_ref[sl, hs], c, s) * scale
                qb, kb, vb = q.astype(BF16), k.astype(BF16), v_ref[sl, hs].astype(BF16)
                st = state[h]
                st_ref[h, ci] = st
                sc = _dot_nt(qb, kb) * d_ref[h]
                o_ref[sl, hs] = _dot(sc.astype(BF16), vb) + _dot(qb, st.astype(BF16)) * x_ref[h]
                state[h] = st * g_ref[h][0:1, :] + _dot_tn((k * z_ref[h]).astype(BF16), vb)
            return carry

        lax.fori_loop(0, BLK // cr, chunk, 0)

    col, tab, sq, hv, g8, st, out = _ret_specs(nblk, False)
    return pl.pallas_call(
        body,
        out_shape=(jax.ShapeDtypeStruct((lp, 512), F32), jax.ShapeDtypeStruct((4, lp // cr, 128, 128), F32)),
        grid=(nblk,), in_specs=[col(C_RQ), col(C_RK), col(C_RV), tab, tab, sq, hv, hv, g8],
        out_specs=(out, st), scratch_shapes=[pltpu.VMEM((RET_HEADS, 128, 128), F32)],
        compiler_params=_cp("arbitrary"), name=name)(proj, proj, proj, *tables)


def _retention_bwd(proj, do, states, tables, *, name):
    lp = proj.shape[0]
    nblk, cr = lp // BLK, RET_CHUNK
    nch = BLK // cr
    scale = RET_DK ** -0.5

    def body(q_ref, k_ref, v_ref, do_ref, st_ref, c_ref, s_ref, d_ref, z_ref, x_ref, g_ref, dqkv_ref, dstate):
        @pl.when(pl.program_id(0) == 0)
        def _():
            dstate[...] = jnp.zeros_like(dstate)

        def chunk(cc, carry):
            ci = nch - 1 - cc
            sl = pl.ds(pl.multiple_of(ci * cr, cr), cr)
            c, s = c_ref[sl, :], s_ref[sl, :]
            for h in range(RET_HEADS):
                hs = slice(128 * h, 128 * h + 128)
                dmat, zeta, xi = d_ref[h], z_ref[h], x_ref[h]
                q = _rope(q_ref[sl, hs], c, s)
                k = _rope(k_ref[sl, hs], c, s) * scale
                qb, kb, vb = q.astype(BF16), k.astype(BF16), v_ref[sl, hs].astype(BF16)
                kzb = (k * zeta).astype(BF16)
                dov = do_ref[sl, hs]
                dob, doxb = dov.astype(BF16), (dov * xi).astype(BF16)
                stb = st_ref[h, ci].astype(BF16)
                dsn = dstate[h]
                dsnb = dsn.astype(BF16)
                scb = (_dot_nt(qb, kb) * dmat).astype(BF16)
                dscb = (_dot_nt(dob, vb) * dmat).astype(BF16)
                dq = _dot(dscb, kb) + _dot_nt(doxb, stb)
                dk = _dot_tn(dscb, qb) + _dot_nt(vb, dsnb) * zeta
                dv = _dot_tn(scb, dob) + _dot(kzb, dsnb)
                dstate[h] = dsn * g_ref[h][0:1, :] + _dot_tn(qb, doxb)
                dqkv_ref[sl, 128 * h:128 * h + 128] = _rope_t(dq, c, s).astype(BF16)
                dqkv_ref[sl, 512 + 128 * h:640 + 128 * h] = _rope_t(dk * scale, c, s).astype(BF16)
                dqkv_ref[sl, 1024 + 128 * h:1152 + 128 * h] = dv.astype(BF16)
            return carry

        lax.fori_loop(0, nch, chunk, 0)

    col, tab, sq, hv, g8, st, out = _ret_specs(nblk, True)
    return pl.pallas_call(
        body, out_shape=jax.ShapeDtypeStruct((lp, P_RET), BF16), grid=(nblk,),
        in_specs=[col(C_RQ), col(C_RK), col(C_RV), out, st, tab, tab, sq, hv, hv, g8],
        out_specs=pl.BlockSpec((BLK, P_RET), lambda i: (nblk - 1 - i, 0)),
        scratch_shapes=[pltpu.VMEM((RET_HEADS, 128, 128), F32)],
        compiler_params=_cp("arbitrary"), name=name)(proj, proj, proj, do, states, *tables)


def _gla_tables():
    c = GLA_CHUNK
    tri = np.tril(np.ones((c, c), np.float32))
    ones_qv = np.kron(np.eye(GLA_HEADS, dtype=np.float32), np.ones((GLA_DK, GLA_DV), np.float32))
    return (jnp.asarray(tri, BF16), jnp.asarray(tri.T.copy(), BF16), jnp.asarray(ones_qv, BF16),
            jnp.asarray(ones_qv.T.copy(), BF16))


def _split3(x):
    hi = x.astype(BF16)
    r1 = x - hi.astype(F32)
    mid = r1.astype(BF16)
    lo = (r1 - mid.astype(F32)).astype(BF16)
    return hi, mid, lo


def _tri_sum(tri, x):
    hi, mid, lo = _split3(x)
    return _dot(tri, hi) + _dot(tri, mid) + _dot(tri, lo)


def _head_masks(width, per):
    lane = lax.broadcasted_iota(jnp.int32, (1, width), 1)
    return [((lane >= per * h) & (lane < per * (h + 1))).astype(F32) for h in range(GLA_HEADS)]


def _stack_heads(x, masks):
    return jnp.concatenate([x * m for m in masks], axis=0)


def _gla_gate(ga, w2, b, ok, tri):
    z = _dot(ga.astype(BF16), w2) + b
    la = (jnp.minimum(z, 0.0) - jnp.log(1.0 + jnp.exp(-jnp.abs(z)))) * (1.0 / GLA_TAU)
    la = jnp.where(ok, la, 0.0)
    return z, _tri_sum(tri, la)


def _gla_rows(i_blk, ci, lp):
    c = GLA_CHUNK
    rows = i_blk * BLK + ci * c + lax.broadcasted_iota(jnp.int32, (c, 1), 0)
    return (rows >= PADF) & (rows < lp - BACK)


N_SUB = GLA_CHUNK // GLA_SUB - 1
N_SUB2 = GLA_SUB // GLA_SUB2 - 1


def _gla_masks():
    c, s1, s2 = GLA_CHUNK, GLA_SUB, GLA_SUB2
    sh1, sh2 = s1.bit_length() - 1, s2.bit_length() - 1
    r = lax.broadcasted_iota(jnp.int32, (c, GLA_QK), 0)
    blk, within = jnp.right_shift(r, sh1), jnp.bitwise_and(r, s1 - 1)
    grp = jnp.right_shift(within, sh2)
    rowm = [(blk == a).astype(F32) for a in range(1, N_SUB + 1)] + [(grp == b).astype(F32) for b in range(1, N_SUB2 + 1)]
    keym = ([(r < s1 * a).astype(F32) for a in range(1, N_SUB + 1)]
            + [(within < s2 * b).astype(F32) for b in range(1, N_SUB2 + 1)])
    rs = lax.broadcasted_iota(jnp.int32, (GLA_HEADS * c, c), 0)
    ts = lax.broadcasted_iota(jnp.int32, (GLA_HEADS * c, c), 1)
    same = (jnp.right_shift(jnp.bitwise_and(rs, c - 1), sh1) == jnp.right_shift(ts, sh1)).astype(F32)
    lag = [(jnp.bitwise_and(r, s2 - 1) >= j).astype(F32) for j in range(s2)]
    return rowm, keym, same, lag


def _gla_hats(qs, k, g, masks, hm_q):
    c, s1, s2 = GLA_CHUNK, GLA_SUB, GLA_SUB2
    rowm, keym, same, _ = masks
    refs = [g[s1 * a - 1:s1 * a, :] for a in range(1, N_SUB + 1)]
    for b in range(1, N_SUB2 + 1):
        refs.append(jnp.concatenate([jnp.broadcast_to(g[s1 * i + s2 * b - 1:s1 * i + s2 * b, :], (s1, GLA_QK))
                                     for i in range(c // s1)], axis=0))
    eqs = [jnp.exp(jnp.minimum(g - r, 0.0)) * m for r, m in zip(refs, rowm)]
    eks = [jnp.exp(jnp.minimum(r - g, 0.0)) * m for r, m in zip(refs, keym)]
    qhs, khs = [qs * e for e in eqs], [k * e for e in eks]
    qst = [_stack_heads(q, hm_q).astype(BF16) for q in qhs]
    khb = [x.astype(BF16) for x in khs]
    qa, qb = jnp.concatenate(qst[:N_SUB], axis=1), jnp.concatenate(qst[N_SUB:], axis=1)
    ka, kb = jnp.concatenate(khb[:N_SUB], axis=1), jnp.concatenate(khb[N_SUB:], axis=1)
    p = _dot_nt(qa, ka) + _dot_nt(qb, kb) * same
    return eqs, eks, qhs, khs, qa, qb, ka, kb, p


def _roll_rows(x, j):
    return x if j == 0 else pltpu.roll(x, j, 0)


def _gla(proj, w2p, b, tables, *, name):
    lp = proj.shape[0]
    nblk, c, s2 = lp // BLK, GLA_CHUNK, GLA_SUB2
    nch = BLK // c

    def body(q_ref, k_ref, v_ref, a_ref, w_ref, b_ref, tri_ref, ones_ref, o_ref, st_ref, state):
        i_blk = pl.program_id(0)

        @pl.when(i_blk == 0)
        def _():
            state[...] = jnp.zeros_like(state)
        hm_q = _head_masks(GLA_QK, GLA_DK)
        masks = _gla_masks()
        tri, ones_qv, w2, bias = tri_ref[...], ones_ref[...], w_ref[...], b_ref[...]

        def chunk(ci, carry):
            sl = pl.ds(pl.multiple_of(ci * c, c), c)
            ok = _gla_rows(i_blk, ci, lp)
            k, v = k_ref[sl, :], v_ref[sl, :]
            vb = v.astype(BF16)
            qs = q_ref[sl, :] * (GLA_DK ** -0.5)
            _, g = _gla_gate(a_ref[sl, :], w2, bias, ok, tri)
            last = g[c - 1:c, :]
            st = state[...]
            st_ref[ci] = st
            qst = _stack_heads(qs * jnp.exp(g), hm_q).astype(BF16)
            oi = _dot_nt(qst, st.astype(BF16))
            o = jnp.concatenate([oi[c * h:c * h + c, :] for h in range(GLA_HEADS)], axis=1)
            ke = k * jnp.exp(last - g)
            f = _dot_tn(vb, ke.astype(BF16))
            upd = f[0:GLA_DV, :] * hm_q[0]
            for h in range(1, GLA_HEADS):
                upd = upd + f[GLA_DV * h:GLA_DV * (h + 1), :] * hm_q[h]
            state[...] = st * jnp.exp(last) + upd
            p = _gla_hats(qs, k, g, masks, hm_q)[-1]
            ob = _dot(p.astype(BF16), vb)
            o = o + jnp.concatenate([ob[c * h:c * h + c, GLA_DV * h:GLA_DV * (h + 1)] for h in range(GLA_HEADS)],
                                    axis=1)
            ws = []
            for j in range(s2):
                ej = jnp.exp(jnp.minimum(g - _roll_rows(g, j), 0.0))
                ws.append((qs * _roll_rows(k, j) * ej * masks[3][j]).astype(BF16))
            ball = _dot(jnp.concatenate(ws, axis=0), ones_qv)
            for j in range(s2):
                o = o + ball[c * j:c * j + c, :] * _roll_rows(v, j)
            o_ref[sl, :] = o
            return carry

        lax.fori_loop(0, nch, chunk, 0)

    tri, _, ones_qv, _ = tables
    full = lambda arr: pl.BlockSpec(arr.shape, lambda i: (0,) * arr.ndim)
    return pl.pallas_call(
        body,
        out_shape=(jax.ShapeDtypeStruct((lp, GLA_V), F32), jax.ShapeDtypeStruct((lp // c, GLA_DV, GLA_QK), F32)),
        grid=(nblk,),
        in_specs=[pl.BlockSpec((BLK, GLA_QK), lambda i: (i, C_GQ // GLA_QK)),
                  pl.BlockSpec((BLK, GLA_QK), lambda i: (i, C_GK // GLA_QK)),
                  pl.BlockSpec((BLK, GLA_V), lambda i: (i, C_GV // GLA_V)),
                  pl.BlockSpec((BLK, 128), lambda i: (i, C_GA // 128)),
                  full(w2p), full(b), full(tri), full(ones_qv)],
        out_specs=(pl.BlockSpec((BLK, GLA_V), lambda i: (i, 0)),
                   pl.BlockSpec((nch, GLA_DV, GLA_QK), lambda i: (i, 0, 0))),
        scratch_shapes=[pltpu.VMEM((GLA_DV, GLA_QK), F32)],
        compiler_params=_cp("arbitrary"), name=name)(proj, proj, proj, proj, w2p, b, tri, ones_qv)


def _gla_bwd(proj, do, states, w2p, b, tables, *, name):
    lp = proj.shape[0]
    nblk, c, s1, s2 = lp // BLK, GLA_CHUNK, GLA_SUB, GLA_SUB2
    nch = BLK // c

    def body(q_ref, k_ref, v_ref, a_ref, do_ref, st_ref, w_ref, b_ref, tri_ref, trit_ref, ones_ref, onest_ref,
             dp_ref, dw_ref, db_ref, dstate, dqs_s, dk_s, dg_s, dv_s):
        i_blk = nblk - 1 - pl.program_id(0)

        @pl.when(pl.program_id(0) == 0)
        def _():
            dstate[...] = jnp.zeros_like(dstate)
            dw_ref[...] = jnp.zeros_like(dw_ref)
            db_ref[...] = jnp.zeros_like(db_ref)
        hm_q = _head_masks(GLA_QK, GLA_DK)
        hm_v = _head_masks(GLA_V, GLA_DV)
        masks = _gla_masks()
        tri, trit, ones_qv, ones_vq = tri_ref[...], trit_ref[...], ones_ref[...], onest_ref[...]
        w2, bias = w_ref[...], b_ref[...]
        rsum = lambda x: jnp.sum(x, axis=0, keepdims=True)

        def chunk(cc, carry):
            ci = nch - 1 - cc
            sl = pl.ds(pl.multiple_of(ci * c, c), c)
            ok = _gla_rows(i_blk, ci, lp)
            k, v, ga = k_ref[sl, :], v_ref[sl, :], a_ref[sl, :]
            vb = v.astype(BF16)
            qs = q_ref[sl, :] * (GLA_DK ** -0.5)
            z, g = _gla_gate(ga, w2, bias, ok, tri)
            last = g[c - 1:c, :]
            elast = jnp.exp(last)
            eg = jnp.exp(g)
            ekl = jnp.exp(last - g)
            qe, ke = qs * eg, k * ekl
            dov = do_ref[sl, :]
            st = st_ref[ci]
            dsn = dstate[...]
            qst = _stack_heads(qe, hm_q).astype(BF16)
            dost = jnp.concatenate([dov[:, GLA_DV * h:GLA_DV * (h + 1)] for h in range(GLA_HEADS)], axis=0).astype(BF16)
            dqe_st = _dot(dost, st.astype(BF16))
            dqe = dqe_st[0:c, :] * hm_q[0]
            for h in range(1, GLA_HEADS):
                dqe = dqe + dqe_st[c * h:c * h + c, :] * hm_q[h]
            dstate[...] = _dot_tn(dost, qst) + dsn * elast
            dlast = rsum(dsn * st) * elast
            df = _stack_heads(dsn, hm_q).astype(BF16)
            dv_s[...] = _dot_nt(ke.astype(BF16), df)
            dke = _dot(vb, df)
            xk = dke * ke
            dqs_s[...] = dqe * eg
            dk_s[...] = dke * ekl
            dg_s[...] = dqe * qe - xk
            dlast = dlast + rsum(xk)
            eqs, eks, qhs, khs, qa, qb, ka, kb, p = _gla_hats(qs, k, g, masks, hm_q)
            dost_v = _stack_heads(dov, hm_v).astype(BF16)
            dp = _dot_nt(dost_v, vb)
            dv_s[...] += _dot_tn(p.astype(BF16), dost_v)
            dpa, dpb = dp.astype(BF16), (dp * masks[2]).astype(BF16)
            dq_all = (_dot(dpa, ka), _dot(dpb, kb))
            dk_all = (_dot_tn(dpa, qa), _dot_tn(dpb, qb))
            for t in range(N_SUB + N_SUB2):
                lvl, i = (0, t) if t < N_SUB else (1, t - N_SUB)
                cols = slice(GLA_QK * i, GLA_QK * (i + 1))
                dq_st = dq_all[lvl][:, cols]
                dqh = dq_st[0:c, :] * hm_q[0]
                for h in range(1, GLA_HEADS):
                    dqh = dqh + dq_st[c * h:c * h + c, :] * hm_q[h]
                dkh = dk_all[lvl][:, cols]
                xq, xkh = dqh * qhs[t], dkh * khs[t]
                dqs_s[...] += dqh * eqs[t]
                dk_s[...] += dkh * eks[t]
                dg_s[...] += xq - xkh
                back_ref = xkh - xq
                if lvl == 0:
                    row = s1 * (i + 1) - 1
                    dg_s[row:row + 1, :] += rsum(back_ref)
                else:
                    for blk in range(c // s1):
                        row = s1 * blk + s2 * (i + 1) - 1
                        dg_s[row:row + 1, :] += rsum(back_ref[s1 * blk:s1 * blk + s1, :])
            kes, qes, ws, dbs = [], [], [], []
            for j in range(s2):
                em = jnp.exp(jnp.minimum(g - _roll_rows(g, j), 0.0)) * masks[3][j]
                kes.append(_roll_rows(k, j) * em)
                qes.append(qs * em)
                ws.append((qs * kes[j]).astype(BF16))
                dbs.append((dov * _roll_rows(v, j)).astype(BF16))
            ball = _dot(jnp.concatenate(ws, axis=0), ones_qv)
            dwall = _dot(jnp.concatenate(dbs, axis=0), ones_vq)
            for j in range(s2):
                back = (lambda x: x) if j == 0 else (lambda x, j=j: pltpu.roll(x, c - j, 0))
                dw = dwall[c * j:c * j + c, :]
                dv_s[...] += back(ball[c * j:c * j + c, :] * dov)
                dqs_s[...] += dw * kes[j]
                dk_s[...] += back(dw * qes[j])
                x = dw * qs * kes[j]
                dg_s[...] += x - back(x)
            dg_s[c - 1:c, :] += dlast
            dla = jnp.where(ok, _tri_sum(trit, dg_s[...]), 0.0)
            dz = dla * (1.0 / GLA_TAU) / (1.0 + jnp.exp(z))
            dzb = dz.astype(BF16)
            dp_ref[sl, 0:256] = (dqs_s[...] * (GLA_DK ** -0.5)).astype(BF16)
            dp_ref[sl, 256:512] = dk_s[...].astype(BF16)
            dp_ref[sl, 512:1024] = dv_s[...].astype(BF16)
            dp_ref[sl, 1024:1152] = _dot_nt(dzb, w2).astype(BF16)
            dp_ref[sl, 1152:1280] = jnp.zeros((c, 128), BF16)
            dw_ref[...] += _dot_tn(ga.astype(BF16), dzb)
            db_ref[...] += rsum(dz)
            return carry

        lax.fori_loop(0, nch, chunk, 0)

    tri, trit, ones_qv, ones_vq = tables
    full = lambda arr: pl.BlockSpec(arr.shape, lambda i: (0,) * arr.ndim)
    rev = lambda i: nblk - 1 - i
    return pl.pallas_call(
        body,
        out_shape=(jax.ShapeDtypeStruct((lp, P_GLA), BF16),
                   jax.ShapeDtypeStruct((128, GLA_QK), F32), jax.ShapeDtypeStruct((1, GLA_QK), F32)),
        grid=(nblk,),
        in_specs=[pl.BlockSpec((BLK, GLA_QK), lambda i: (rev(i), C_GQ // GLA_QK)),
                  pl.BlockSpec((BLK, GLA_QK), lambda i: (rev(i), C_GK // GLA_QK)),
                  pl.BlockSpec((BLK, GLA_V), lambda i: (rev(i), C_GV // GLA_V)),
                  pl.BlockSpec((BLK, 128), lambda i: (rev(i), C_GA // 128)),
                  pl.BlockSpec((BLK, GLA_V), lambda i: (rev(i), 0)),
                  pl.BlockSpec((nch, GLA_DV, GLA_QK), lambda i: (rev(i), 0, 0)),
                  full(w2p), full(b), full(tri), full(trit), full(ones_qv), full(ones_vq)],
        out_specs=(pl.BlockSpec((BLK, P_GLA), lambda i: (rev(i), 0)),
                   pl.BlockSpec((128, GLA_QK), lambda i: (0, 0)),
                   pl.BlockSpec((1, GLA_QK), lambda i: (0, 0))),
        scratch_shapes=[pltpu.VMEM((GLA_DV, GLA_QK), F32), pltpu.VMEM((c, GLA_QK), F32),
                        pltpu.VMEM((c, GLA_QK), F32), pltpu.VMEM((c, GLA_QK), F32), pltpu.VMEM((c, GLA_V), F32)],
        compiler_params=_cp("arbitrary"), name=name)(proj, proj, proj, proj, do, states, w2p, b, tri, trit, ones_qv, ones_vq)


def _as2d(a):
    return a.reshape(-1, a.shape[-1])


def _ew_tile(r):
    return _tile(r, (512, 256, 128, 64, 32, 16, 8))


def _add2(a, b, *, out_dtype, name):
    a2, b2 = _as2d(a), _as2d(b)
    r, n = a2.shape
    tm = _ew_tile(r)

    def body(a_ref, b_ref, o_ref):
        o_ref[...] = (a_ref[...] + b_ref[...]).astype(o_ref.dtype)

    blk = pl.BlockSpec((tm, n), lambda i: (i, 0))
    return pl.pallas_call(body, out_shape=jax.ShapeDtypeStruct((r, n), out_dtype), grid=(r // tm,), in_specs=[blk, blk],
                          out_specs=blk, compiler_params=_cp("parallel"), name=name)(a2, b2).reshape(a.shape)


def _sum_slots(own, q, *, name):
    shape = own.shape
    q3 = q.reshape(3, -1, shape[-1])
    own2 = _as2d(own)
    r, n = own2.shape
    tm = _ew_tile(r)

    def body(own_ref, q_ref, o_ref):
        f = lambda i: q_ref[i].astype(F32)
        o_ref[...] = ((own_ref[...].astype(F32) + f(0)) + f(1)) + f(2)

    blk = pl.BlockSpec((tm, n), lambda i: (i, 0))
    return pl.pallas_call(
        body, out_shape=jax.ShapeDtypeStruct((r, n), F32), grid=(r // tm,),
        in_specs=[blk, pl.BlockSpec((3, tm, n), lambda i: (0, i, 0))], out_specs=blk,
        compiler_params=_cp("parallel"), name=name)(own2, q3).reshape(shape)


def _adamw(w, g, m, v, *, name):
    shape = w.shape
    w2, g2, m2, v2 = _as2d(w), _as2d(g), _as2d(m), _as2d(v)
    r, n = w2.shape
    tm = _ew_tile(r)
    c1 = 1.0 - ADAM_B1 ** ADAM_STEP
    c2 = 1.0 - ADAM_B2 ** ADAM_STEP

    def body(w_ref, g_ref, m_ref, v_ref, d_ref, mo_ref, vo_ref):
        gv = g_ref[...]
        mn = ADAM_B1 * m_ref[...] + (1.0 - ADAM_B1) * gv
        vn = ADAM_B2 * v_ref[...] + (1.0 - ADAM_B2) * (gv * gv)
        mo_ref[...] = mn
        vo_ref[...] = vn
        d_ref[...] = -ADAM_LR * ((mn / c1) / (jnp.sqrt(vn / c2) + ADAM_EPS) + ADAM_WD * w_ref[...])

    blk = pl.BlockSpec((tm, n), lambda i: (i, 0))
    o = jax.ShapeDtypeStruct((r, n), F32)
    d, mo, vo = pl.pallas_call(body, out_shape=(o, o, o), grid=(r // tm,), in_specs=[blk] * 4, out_specs=(blk,) * 3,
                               compiler_params=_cp("parallel"), name=name)(w2, g2, m2, v2)
    return d.reshape(shape), mo.reshape(shape), vo.reshape(shape)


ANY = pl.BlockSpec(memory_space=pl.ANY)


def _place():
    return lax.axis_index("x"), lax.axis_index("y"), lax.axis_index("c")


def _other_chips(x, y):
    return [(1 - x, y), (x, 1 - y), (1 - x, 1 - y)]


def _remote(src, dst, ssem, rsem, dev):
    return pltpu.make_async_remote_copy(src_ref=src, dst_ref=dst, send_sem=ssem, recv_sem=rsem, device_id=dev,
                                        device_id_type=MESH)


def _allgather_chips(arrs, *, name):
    n = len(arrs)

    def body(*refs):
        ins, outs = refs[:n], refs[n:2 * n]
        s1, r1, s2, r2 = refs[2 * n:]
        x, y, c = _place()
        q = 2 * x + y
        chips = _other_chips(x, y)
        qs = [2 * cx + cy for cx, cy in chips]
        sib = (x, y, 1 - c)
        first, passed = [], []
        for k in range(n):
            for j, chip in enumerate(chips):
                first.append(_remote(ins[k].at[c], outs[k].at[c, q], s1.at[k, j], r1.at[k, j], (*chip, c)))
        for cp in first:
            cp.start()
        for k in range(n):
            for j, chip in enumerate(chips):
                land = outs[k].at[c, qs[j]]
                _remote(land, land, s1.at[k, j], r1.at[k, j], (*chip, c)).wait_recv()
                fw = _remote(land, land, s2.at[k, j], r2.at[k, j], sib)
                fw.start()
                passed.append(fw)
        for k in range(n):
            for j in range(3):
                land = outs[k].at[1 - c, qs[j]]
                _remote(land, land, s2.at[k, j], r2.at[k, j], sib).wait_recv()
        for cp in first + passed:
            cp.wait_send()

    sem = pltpu.SemaphoreType.DMA
    outs = pl.pallas_call(
        body, out_shape=tuple(jax.ShapeDtypeStruct((2, 4) + a.shape[1:], a.dtype) for a in arrs),
        in_specs=[ANY] * n, out_specs=(ANY,) * n,
        scratch_shapes=[sem((n, 3)), sem((n, 3)), sem((n, 3)), sem((n, 3))], name=name)(*arrs)
    chip = 2 * lax.axis_index("x") + lax.axis_index("y")
    return [lax.dynamic_update_slice_in_dim(o, a[:, None], chip, axis=1) for o, a in zip(outs, arrs)]


def _pair_exchange(arrs, *, name):
    n = len(arrs)

    def body(*refs):
        ins, outs = refs[:n], refs[n:2 * n]
        ssem, rsem = refs[2 * n:]
        x, y, c = _place()
        cps = [_remote(ins[k].at[:, 1 - c], outs[k], ssem.at[k], rsem.at[k], (x, y, 1 - c)) for k in range(n)]
        for cp in cps:
            cp.start()
        for cp in cps:
            cp.wait()

    sem = pltpu.SemaphoreType.DMA
    return pl.pallas_call(
        body, out_shape=tuple(jax.ShapeDtypeStruct((a.shape[0],) + a.shape[2:], a.dtype) for a in arrs),
        in_specs=[ANY] * n, out_specs=(ANY,) * n, scratch_shapes=[sem((n,)), sem((n,))], name=name)(*arrs)


def _pair_sum(mine, theirs, c, *, name):
    _, _, r, n = mine.shape
    tm = r if r <= 512 else _ew_tile(r)

    def body(c_ref, a_ref, b_ref, o_ref):
        o_ref[...] = (a_ref[...] + b_ref[...]).astype(BF16)

    blk = pl.BlockSpec((None, tm, n), lambda s, i, c_ref: (s, i, 0))
    return pl.pallas_call(
        body, out_shape=jax.ShapeDtypeStruct((4, r, n), BF16),
        grid_spec=pltpu.PrefetchScalarGridSpec(
            num_scalar_prefetch=1, grid=(4, r // tm),
            in_specs=[pl.BlockSpec((None, None, tm, n), lambda s, i, c_ref: (s, c_ref[0], i, 0)), blk], out_specs=blk),
        compiler_params=_cp("parallel", "parallel"), name=name)(jnp.reshape(c, (1,)).astype(jnp.int32), mine, theirs)


def _chip_copies(ins, outs, ssem, rsem, mode):
    x, y, c = _place()
    q = 2 * x + y
    sends, recvs = [], []
    for k in range(len(ins)):
        for j, (cx, cy) in enumerate(_other_chips(x, y)):
            sem = (ssem.at[k, j], rsem.at[k, j], (cx, cy, c))
            if mode == "scatter":
                sends.append(_remote(ins[k].at[2 * cx + cy], outs[k].at[j], *sem))
                recvs.append(sends[-1])
            else:
                sends.append(_remote(ins[k].at[c], outs[k].at[2 * q + c], *sem))
                recvs.append(_remote(ins[k].at[c], outs[k].at[2 * (2 * cx + cy) + c], *sem))
    return sends, recvs


def _chip_wait(sends, recvs):
    for cp in sends:
        cp.wait_send()
    for cp in recvs:
        cp.wait_recv()


def _landing_shape(a, mode):
    return jax.ShapeDtypeStruct(((3,) if mode == "scatter" else (8,)) + a.shape[1:], a.dtype)


def _chip_exchange(arrs, mode, *, name):
    n = len(arrs)

    def body(*refs):
        ins, outs = refs[:n], refs[n:2 * n]
        ssem, rsem = refs[2 * n:]
        sends, recvs = _chip_copies(ins, outs, ssem, rsem, mode)
        for cp in sends:
            cp.start()
        _chip_wait(sends, recvs)

    sem = pltpu.SemaphoreType.DMA
    return list(pl.pallas_call(
        body, out_shape=tuple(_landing_shape(a, mode) for a in arrs),
        in_specs=[ANY] * n, out_specs=(ANY,) * n, scratch_shapes=[sem((n, 3)), sem((n, 3))], name=name)(*arrs))


def _pair_fill(bufs, owns, *, name):
    n = len(bufs)

    def body(*refs):
        own, outs = refs[n:2 * n], refs[2 * n:3 * n]
        ssem, rsem = refs[3 * n:]
        x, y, c = _place()
        q = 2 * x + y
        sib = (x, y, 1 - c)
        sends, recvs = [], []
        for k in range(n):
            for j, (cx, cy) in enumerate(_other_chips(x, y)):
                mine, theirs = outs[k].at[2 * (2 * cx + cy) + c], outs[k].at[2 * (2 * cx + cy) + 1 - c]
                sends.append(_remote(mine, mine, ssem.at[k, j], rsem.at[k, j], sib))
                recvs.append(_remote(mine, theirs, ssem.at[k, j], rsem.at[k, j], sib))
            slots = outs[k].at[pl.ds(2 * q, 2)]
            sends.append(_remote(own[k], slots, ssem.at[k, 3], rsem.at[k, 3], sib))
            recvs.append(sends[-1])
        for cp in sends:
            cp.start()
        _chip_wait(sends, recvs)

    sem = pltpu.SemaphoreType.DMA
    return list(pl.pallas_call(
        body, out_shape=tuple(jax.ShapeDtypeStruct(b.shape, b.dtype) for b in bufs),
        in_specs=[ANY] * (2 * n), out_specs=(ANY,) * n, scratch_shapes=[sem((n, 4)), sem((n, 4))],
        input_output_aliases={k: k for k in range(n)}, name=name)(*bufs, *owns))


def _pair_swap(arrs, *, name):
    n = len(arrs)

    def body(*refs):
        ins, outs = refs[:n], refs[n:2 * n]
        ssem, rsem = refs[2 * n:]
        x, y, c = _place()
        cps = [_remote(ins[k], outs[k], ssem.at[k], rsem.at[k], (x, y, 1 - c)) for k in range(n)]
        for cp in cps:
            cp.start()
        for cp in cps:
            cp.wait()

    sem = pltpu.SemaphoreType.DMA
    return pl.pallas_call(
        body, out_shape=tuple(jax.ShapeDtypeStruct(a.shape, a.dtype) for a in arrs),
        in_specs=[ANY] * n, out_specs=(ANY,) * n, scratch_shapes=[sem((n,)), sem((n,))], name=name)(*arrs)


def _allreduce_small(slab, *, name):
    r, n = slab.shape

    def body(x_ref, o_ref, buf, ssem, rsem):
        x, y, c = _place()
        me = 4 * x + 2 * y + c
        buf[me] = x_ref[...]
        cps = []
        for rel in range(1, 8):
            bx, by, bc = (rel >> 2) & 1, (rel >> 1) & 1, rel & 1
            px, py, pc = (x + bx) % 2, (y + by) % 2, (c + bc) % 2
            cps.append((_remote(x_ref, buf.at[me], ssem.at[rel - 1], rsem.at[rel - 1], (px, py, pc)),
                        4 * px + 2 * py + pc, (px, py, pc)))
        for cp, _, _ in cps:
            cp.start()
        for rel, (cp, peer, dev) in enumerate(cps):
            cp.wait_send()
            _remote(x_ref, buf.at[peer], ssem.at[rel], rsem.at[rel], dev).wait_recv()
        acc = buf[0]
        for k in range(1, 8):
            acc = acc + buf[k]
        o_ref[...] = acc

    vm = pl.BlockSpec(memory_space=pltpu.VMEM)
    sem = pltpu.SemaphoreType.DMA
    return pl.pallas_call(
        body, out_shape=jax.ShapeDtypeStruct((r, n), F32), in_specs=[vm], out_specs=vm,
        scratch_shapes=[pltpu.VMEM((8, r, n), F32), sem((7,)), sem((7,))], name=name)(slab)


def _slab(arrs, row_mult):
    flat = jnp.concatenate([a.reshape(-1) for a in arrs])
    unit = 128 * row_mult
    total = -(-flat.size // unit) * unit
    return jnp.pad(flat, (0, total - flat.size)).reshape(-1, 128)


def _unslab(slab, shapes):
    flat = slab.reshape(-1)
    out, off = [], 0
    for s in shapes:
        size = int(np.prod(s))
        out.append(flat[off:off + size].reshape(s))
        off += size
    return out


def _cols_from_chips(a):
    return jnp.transpose(a, (1, 0, 2)).reshape(a.shape[1], -1)


def _cols_to_chips(a, parts):
    r = a.shape[0]
    return jnp.transpose(a.reshape(r, parts, -1), (1, 0, 2))


BIG = ("w_in", "w_out", "up", "down")
GATHER_RIDES = {("proj", 0): (("w_out", 0), ("up", 0)), ("mix_out", 0): (("down", 0),),
                ("ffn_up_a", 0): (("w_in", 1), ("w_out", 1)), ("ffn_up_g", 0): (("up", 1),),
                ("ffn_down", 0): (("down", 1),)}
REDUCE_RIDES = {("ffn_down_dx", 0): ("up",), ("ffn_up_a_dx", 0): ("w_in", "w_out"), ("ffn_up_g_dx", 0): ("down",)}


class _LocalWeights:
    def __init__(self, meta, win, wout, up_a, up_g, down, w2p, cw):
        self._meta, self._w = meta, {"win": win, "wout": wout, "up_a": up_a, "up_g": up_g, "down": down, "w2p": w2p,
                                     "cw": cw}

    def meta(self):
        return self._meta

    def get(self, kind, l):
        return self._w[kind][l]

    def mm(self, site, l, a, b, **kw):
        return _mm(a, b, name=site, **kw)

    def grads_done(self, l, g):
        pass


class _ChipWeights:
    def __init__(self, w_in, w_out, ffn_up, ffn_down, meta_tokens, gla_gate_w2, ffn_conv_w):
        self.x, self.y, self.c = _place()
        self.q = 2 * self.x + self.y
        halves = lambda a: a.astype(BF16).reshape(2, a.shape[0] // 2, a.shape[1])
        self.own = {(k, l): halves(a[l]) for k, a in zip(BIG, (w_in, w_out, ffn_up, ffn_down)) for l in range(DEPTH)}
        self.landed, self.swapped, self.full, self.n_swaps = {}, {}, {}, 0
        self.sh_shapes = [meta_tokens.shape, gla_gate_w2.shape, ffn_conv_w.shape]
        self.own["small", 0] = _slab([meta_tokens, gla_gate_w2, ffn_conv_w], 16).reshape(2, -1, 128)
        first = [("w_in", 0), ("small", 0)]
        for key, arr in zip(first, _chip_exchange([self.own[k] for k in first], "bcast", name="gather_first")):
            self.landed[key] = arr
        sh = self._whole("small", 0).reshape(4, -1, 128)
        parts = [_unslab(sh[k], self.sh_shapes) for k in range(4)]
        self._meta = jnp.concatenate([p[0] for p in parts], axis=-1)
        self.w2 = jnp.concatenate([p[1] for p in parts], axis=-1)
        self.cw = jnp.concatenate([p[2] for p in parts], axis=-1)
        self.partial, self.slots = {}, {}

    def _whole(self, kind, l):
        if (kind, l) not in self.full:
            keys = [k for k in self.landed if k not in self.full]
            got = _pair_fill([self.landed[k] for k in keys], [self.own[k] for k in keys],
                             name=f"gather_fill_{self.n_swaps}")
            self.n_swaps += 1
            for k, buf in zip(keys, got):
                self.full[k] = buf.reshape(4, 2 * buf.shape[1], buf.shape[2])
        return self.full[kind, l]

    def meta(self):
        return self._meta

    def get(self, kind, l):
        if kind == "win":
            return _to_kernel_cols(_cols_from_chips(self._whole("w_in", l)))
        if kind == "wout":
            return self._whole("w_out", l).reshape(D_MODEL, D_MODEL)
        if kind == "up_a":
            return _cols_from_chips(self._whole("up", l)[0:2])
        if kind == "up_g":
            return _cols_from_chips(self._whole("up", l)[2:4])
        if kind == "down":
            return self._whole("down", l).reshape(D_FF, D_MODEL)
        if kind == "w2p":
            return jnp.pad(self.w2[l], ((0, 128 - GLA_RANK), (0, 0))).astype(BF16)
        return self.cw[l]

    def mm(self, site, l, a, b, **kw):
        if (site, l) in GATHER_RIDES:
            keys = GATHER_RIDES[site, l]
            out, got = _mm(a, b, name=site, carry=([self.own[k] for k in keys], "bcast"), **kw)
            self.landed.update(zip(keys, got))
            return out
        if (site, l) in REDUCE_RIDES and all((k, DEPTH - 1) in self.partial for k in REDUCE_RIDES[site, l]):
            keys = [(k, DEPTH - 1) for k in REDUCE_RIDES[site, l]]
            out, got = _mm(a, b, name=site, carry=([self.partial[k] for k in keys], "scatter"), **kw)
            self.slots.update(zip(keys, got))
            return out
        return _mm(a, b, name=site, **kw)

    def grads_done(self, l, g):
        split = lambda a: a.reshape(4, 2, a.shape[-2] // 2, a.shape[-1]) if a.ndim == 3 else \
            a.reshape(4, 2, a.shape[0] // 8, a.shape[1])
        big = {"w_in": split(_cols_to_chips(g["w_in"][l], 4)), "w_out": split(g["w_out"][l]),
               "up": split(g["up"][l]), "down": split(g["down"][l])}
        from_sib = _pair_exchange([big[k] for k in BIG], name=f"grads_pair_exchange_{l}")
        for k, theirs in zip(BIG, from_sib):
            self.partial[k, l] = _pair_sum(big[k], theirs, self.c, name=f"pair_sum_{k}_{l}")

    def reduce(self):
        keys = [(k, l) for l in range(DEPTH) for k in BIG]
        late = [k for k in keys if k not in self.slots]
        self.slots.update(zip(late, _chip_exchange([self.partial[k] for k in late], "scatter",
                                                   name="grads_chip_exchange")))
        half = {}
        for k in keys:
            own = lax.dynamic_index_in_dim(self.partial[k], self.q, 0, keepdims=False)
            half[k] = _sum_slots(own, self.slots[k], name=f"chip_sum_{k[0]}_{k[1]}")
        other = dict(zip(keys, _pair_swap([half[k] for k in keys], name="grads_pair_swap")))
        whole = lambda k: jnp.where(self.c == 0, jnp.concatenate([half[k], other[k]], axis=0),
                                    jnp.concatenate([other[k], half[k]], axis=0))
        return [jnp.stack([whole((k, l)) for l in range(DEPTH)]) for k in BIG]


def _local_step(x_rows, target_rows, wts, pre_mix_norm, gla_gate_b, ret_norm_w, gla_norm_w, post_mix_norm,
                pre_ffn_norm, ffn_conv_b, post_ffn_norm):
    d = D_MODEL
    lp = x_rows.shape[0] + FRONT + BACK
    row = lambda a, l: a[l][None, :]
    rtab = _ret_tables(lp)
    gtab = _gla_tables()
    h0 = jnp.concatenate([jnp.zeros((PADF, d), F32), wts.meta(), x_rows, jnp.zeros((BACK, d), F32)], axis=0)
    target = jnp.pad(target_rows, ((FRONT, BACK), (0, 0)))

    saved = []
    h = h0
    _, hn = _resid_norm(h0, None, None, row(pre_mix_norm, 0), name="norm_in")
    loss_local = dy = None
    for l in range(DEPTH):
        s = {"h_in": h, "hn": hn}
        s["proj"] = wts.mm("proj", l, hn, wts.get("win", l))
        s["o_ret"], s["st_ret"] = _retention(s["proj"], rtab, name="retention")
        s["o_gla"], s["st_gla"] = _gla(s["proj"], wts.get("w2p", l), row(gla_gate_b, l), gtab, name="gla")
        s["merged"] = _merge(s["o_ret"], s["o_gla"], s["proj"], row(ret_norm_w, l), row(gla_norm_w, l), name="merge")
        s["m"] = wts.mm("mix_out", l, s["merged"], wts.get("wout", l))
        s["h_mid"], s["hn2"] = _resid_norm(h, s["m"], row(post_mix_norm, l), row(pre_ffn_norm, l), name="resid_mix")
        s["ua"] = wts.mm("ffn_up_a", l, s["hn2"], wts.get("up_a", l))
        s["ug"] = wts.mm("ffn_up_g", l, s["hn2"], wts.get("up_g", l))
        cw_a, cw_g = wts.get("cw", l)[:, :D_FF], wts.get("cw", l)[:, D_FF:]
        cb_a, cb_g = ffn_conv_b[l][None, :D_FF], ffn_conv_b[l][None, D_FF:]
        s["conv"] = (cw_a, cw_g, cb_a, cb_g)
        s["act"] = _conv_act(s["ua"], s["ug"], cw_a, cw_g, cb_a, cb_g, name="conv_act")
        s["f"] = wts.mm("ffn_down", l, s["act"], wts.get("down", l))
        if l + 1 < DEPTH:
            h, hn = _resid_norm(s["h_mid"], s["f"], row(post_ffn_norm, l), row(pre_mix_norm, l + 1), name="resid_ffn")
        else:
            loss_local, dy = _loss_head(s["h_mid"], s["f"], row(post_ffn_norm, l), target, name="loss_head")
        saved.append(s)

    g = {k: [None] * DEPTH for k in ("pre_mix", "w_in", "w2", "gb", "ret_n", "gla_n", "w_out", "post_mix", "pre_ffn",
                                     "up", "cw", "cb", "down", "post_ffn")}
    dh_out, dhn_next = dy, None
    for l in reversed(range(DEPTH)):
        s = saved[l]
        cw_a, cw_g, cb_a, cb_g = s["conv"]
        if l + 1 < DEPTH:
            dh, df, g["pre_mix"][l + 1], g["post_ffn"][l] = _resid_norm_bwd(
                dh_out, dhn_next, saved[l + 1]["h_in"], s["f"], row(pre_mix_norm, l + 1), row(post_ffn_norm, l),
                name="resid_ffn_bwd")
        else:
            dh, df, _, g["post_ffn"][l] = _resid_norm_bwd(dh_out, None, None, s["f"], None, row(post_ffn_norm, l),
                                                          name="loss_head_bwd")
        dact = wts.mm("ffn_down_dx", l, df, wts.get("down", l), nt=True)
        g["down"][l] = _mm_tn(s["act"], df, tn=512, name="ffn_down_dw")
        du_a, du_g, dcw_a, dcw_g, dcb_a, dcb_g = _conv_act_bwd(s["ua"], s["ug"], dact, cw_a, cw_g, cb_a, cb_g,
                                                               name="conv_act_bwd")
        g["cw"][l] = jnp.concatenate([dcw_a, dcw_g], axis=1)
        g["cb"][l] = jnp.concatenate([dcb_a, dcb_g], axis=1)[0]
        half_up = _mm_tn(s["hn2"], du_a, tn=D_FF // 2, blocks=(4, 0), name="ffn_up_a_dw")
        g["up"][l] = _mm_tn(s["hn2"], du_g, tn=D_FF // 2, blocks=(4, 2), into=half_up, name="ffn_up_g_dw")
        dhn2 = wts.mm("ffn_up_a_dx", l, du_a, wts.get("up_a", l), nt=True)
        dhn2 = wts.mm("ffn_up_g_dx", l, du_g, wts.get("up_g", l), nt=True, add=dhn2)
        dh, dm, g["pre_ffn"][l], g["post_mix"][l] = _resid_norm_bwd(
            dh, dhn2, s["h_mid"], s["m"], row(pre_ffn_norm, l), row(post_mix_norm, l), name="resid_mix_bwd")
        g["w_out"][l] = _mm_tn(s["merged"], dm, name="mix_out_dw")
        dmerged = wts.mm("mix_out_dx", l, dm, wts.get("wout", l), nt=True)
        do_ret, do_gla, d_gate, g["ret_n"][l], g["gla_n"][l] = _merge_bwd(
            dmerged, s["o_ret"], s["o_gla"], s["proj"], row(ret_norm_w, l), row(gla_norm_w, l), name="merge_bwd")
        d_ret = _retention_bwd(s["proj"], do_ret, s["st_ret"], rtab, name="retention_bwd")
        d_gla, dw2, dgb = _gla_bwd(s["proj"], do_gla, s["st_gla"], wts.get("w2p", l), row(gla_gate_b, l), gtab,
                                   name="gla_bwd")
        g["w2"][l], g["gb"][l] = dw2[:GLA_RANK], dgb[0]
        pieces = (d_ret, d_gate, d_gla)
        g["w_in"][l] = _to_reference_cols(*[_mm_tn(s["hn"], p, name=f"proj_dw_{i}") for i, p in enumerate(pieces)])
        win = wts.get("win", l)
        dhn_next = _mm_nt_sum(pieces, [win[:, 0:P_RET], win[:, P_RET:P_RET + P_GATE], win[:, P_RET + P_GATE:]],
                              name="proj_dx")
        dh_out = dh
        wts.grads_done(l, g)
    dh0, _, g["pre_mix"][0], _ = _resid_norm_bwd(dh_out, dhn_next, h0, None, row(pre_mix_norm, 0), None,
                                                 name="norm_in_bwd")
    return loss_local, dh0, g


def kernel(x, meta_tokens, pre_mix_norm, w_in, gla_gate_w2, gla_gate_b, ret_norm_w, gla_norm_w, w_out, post_mix_norm, pre_ffn_norm, ffn_up, ffn_conv_w, ffn_conv_b, ffn_down, post_ffn_norm, loss_target, m_meta_tokens, m_pre_mix_norm, m_w_in, m_gla_gate_w2, m_gla_gate_b, m_ret_norm_w, m_gla_norm_w, m_w_out, m_post_mix_norm, m_pre_ffn_norm, m_ffn_up, m_ffn_conv_w, m_ffn_conv_b, m_ffn_down, m_post_ffn_norm, v_meta_tokens, v_pre_mix_norm, v_w_in, v_gla_gate_w2, v_gla_gate_b, v_ret_norm_w, v_gla_norm_w, v_w_out, v_post_mix_norm, v_pre_ffn_norm, v_ffn_up, v_ffn_conv_w, v_ffn_conv_b, v_ffn_down, v_post_ffn_norm):
    xi, yi, ci = _place()
    chip = 2 * xi + yi
    seq = x.shape[1]
    d = D_MODEL
    wts = _ChipWeights(w_in, w_out, ffn_up, ffn_down, meta_tokens, gla_gate_w2, ffn_conv_w)
    loss_local, dh0, g = _local_step(x[0], loss_target[0], wts, pre_mix_norm, gla_gate_b, ret_norm_w, gla_norm_w,
                                     post_mix_norm, pre_ffn_norm, ffn_conv_b, post_ffn_norm)
    grad_x = dh0[FRONT:FRONT + seq][None]
    names = ("w_in", "w_out", "ffn_up", "ffn_down")
    g_w_in, g_w_out, g_ffn_up, g_ffn_down = wts.reduce()

    small_full = [dh0[PADF:FRONT], jnp.stack(g["pre_mix"])[:, 0], jnp.stack(g["w2"]), jnp.stack(g["gb"]),
                  jnp.stack(g["ret_n"])[:, 0], jnp.stack(g["gla_n"])[:, 0], jnp.stack(g["post_mix"])[:, 0],
                  jnp.stack(g["pre_ffn"])[:, 0], jnp.stack(g["cw"]), jnp.stack(g["cb"]),
                  jnp.stack(g["post_ffn"])[:, 0]]
    small_sum = _unslab(_allreduce_small(_slab(small_full, 8), name="small_allreduce"), [a.shape for a in small_full])
    (g_meta, g_pre_mix, g_w2, g_gb, g_ret_n, g_gla_n, g_post_mix, g_pre_ffn, g_cw, g_cb, g_post_ffn) = small_sum
    g_meta = lax.dynamic_slice_in_dim(g_meta, chip * 256, 256, axis=1)
    g_w2 = lax.dynamic_slice_in_dim(g_w2, chip * 64, 64, axis=2)
    g_cw = lax.dynamic_slice_in_dim(g_cw, chip * 1408, 1408, axis=2)

    grads = [g_meta, g_pre_mix, g_w_in, g_w2, g_gb, g_ret_n, g_gla_n, g_w_out, g_post_mix, g_pre_ffn, g_ffn_up,
             g_cw, g_cb, g_ffn_down, g_post_ffn]
    ws = [meta_tokens, pre_mix_norm, w_in, gla_gate_w2, gla_gate_b, ret_norm_w, gla_norm_w, w_out, post_mix_norm,
          pre_ffn_norm, ffn_up, ffn_conv_w, ffn_conv_b, ffn_down, post_ffn_norm]
    ms = [m_meta_tokens, m_pre_mix_norm, m_w_in, m_gla_gate_w2, m_gla_gate_b, m_ret_norm_w, m_gla_norm_w, m_w_out,
          m_post_mix_norm, m_pre_ffn_norm, m_ffn_up, m_ffn_conv_w, m_ffn_conv_b, m_ffn_down, m_post_ffn_norm]
    vs = [v_meta_tokens, v_pre_mix_norm, v_w_in, v_gla_gate_w2, v_gla_gate_b, v_ret_norm_w, v_gla_norm_w, v_w_out,
          v_post_mix_norm, v_pre_ffn_norm, v_ffn_up, v_ffn_conv_w, v_ffn_conv_b, v_ffn_down, v_post_ffn_norm]
    big_idx = (2, 7, 10, 13)
    deltas, new_m, new_v = [None] * 15, [None] * 15, [None] * 15
    for i, nm in zip(big_idx, names):
        deltas[i], new_m[i], new_v[i] = _adamw(ws[i], grads[i], ms[i], vs[i], name=f"adamw_{nm}")
    small_idx = [i for i in range(15) if i not in big_idx]
    shapes = [ws[i].shape for i in small_idx]
    sd, sm, sv = _adamw(_slab([ws[i] for i in small_idx], 8), _slab([grads[i] for i in small_idx], 8),
                        _slab([ms[i] for i in small_idx], 8), _slab([vs[i] for i in small_idx], 8), name="adamw_small")
    for i, a, b, c_ in zip(small_idx, _unslab(sd, shapes), _unslab(sm, shapes), _unslab(sv, shapes)):
        deltas[i], new_m[i], new_v[i] = a, b, c_

    loss = lax.psum(loss_local, ("x", "y", "c"))
    return (loss, grad_x, *grads, *deltas, *new_m, *new_v)
```

```python
import functools
import math

import numpy as np
import jax
import jax.numpy as jnp
from jax import lax
from jax.experimental import pallas as pl
from jax.experimental.pallas import tpu as pltpu

F32 = jnp.float32
BF16 = jnp.bfloat16

D_MODEL = 1024
DEPTH = 2
N_META = 16
EPS = 1e-6
RET_HEADS = 4
RET_DK = 128
GLA_HEADS = 4
GLA_DK = 64
GLA_DV = 128
GLA_QK = GLA_HEADS * GLA_DK
GLA_V = GLA_HEADS * GLA_DV
GLA_RANK = 16
GLA_TAU = 16.0
D_FF = 2816
ROPE_BASE = 10000.0
IN_WIDTH = 3600
IN_PAD = 3840
C_RQ, C_RK, C_RV, C_RG, C_GR, C_GQ, C_GK, C_GV, C_GA = 0, 512, 1024, 1536, 2048, 2560, 2816, 3072, 3584
P_RET, P_GATE, P_GLA = 1536, 1024, 1280


def _to_kernel_cols(w):
    pad = jnp.zeros(w.shape[:-1] + (IN_PAD - IN_WIDTH,), w.dtype)
    return jnp.concatenate([w[..., 0:2048], w[..., 3072:3584], w[..., 2048:3072], w[..., 3584:3600], pad], axis=-1)


def _to_reference_cols(d_ret, d_gate, d_gla):
    return jnp.concatenate([d_ret, d_gate[..., 0:512], d_gla[..., 0:1024], d_gate[..., 512:1024],
                            d_gla[..., 1024:1024 + GLA_RANK]], axis=-1)

FRONT = 64
BACK = 64
PADF = FRONT - N_META
RET_CHUNK = 128
GLA_CHUNK = 64
GLA_SUB = 16
GLA_SUB2 = 4
BLK = 640

ADAM_LR, ADAM_B1, ADAM_B2, ADAM_EPS, ADAM_WD, ADAM_STEP = 0.001, 0.9, 0.999, 1e-08, 0.01, 10

VMEM_LIMIT = 56 * 2 ** 20
MM_VMEM_BUDGET = 40 * 2 ** 20
MESH = pl.DeviceIdType.MESH


def _cp(*sem):
    return pltpu.CompilerParams(dimension_semantics=sem, vmem_limit_bytes=VMEM_LIMIT)


def _tile(n, cands):
    for t in cands:
        if n % t == 0:
            return t
    raise ValueError(f"no tile for {n} in {cands}")


def _row_tile(n):
    return _tile(n, (640, 512, 320, 256, 128, 64))


def _mm(a, b, *, nt=False, add=None, out_dtype=F32, tn=None, name, carry=None):
    m, k = a.shape
    n = b.shape[0] if nt else b.shape[1]
    tm = _tile(m, (640, 320, 256, 128, 64))
    if tn is None:
        step_bytes = lambda t: 2 * (tm * k * a.dtype.itemsize + t * k * b.dtype.itemsize
                                    + tm * t * (jnp.dtype(out_dtype).itemsize + (4 if add is not None else 0)))
        tn = next(t for t in range(n, 0, -128) if n % t == 0 and (step_bytes(t) <= MM_VMEM_BUDGET or t == 128))
    dn = (((1,), (1,)), ((), ())) if nt else (((1,), (0,)), ((), ()))
    nj, ni = n // tn, m // tm
    n_in = 2 + (add is not None)
    c_arrs, c_mode = carry if carry is not None else ((), None)
    nc = len(c_arrs)

    def body(*refs):
        a_ref, b_ref = refs[:2]
        c_ref = refs[2] if add is not None else None
        o_ref = refs[n_in + nc]
        if nc:
            c_ins, c_outs = refs[n_in:n_in + nc], refs[n_in + nc + 1:n_in + 2 * nc + 1]
            ssem, rsem = refs[n_in + 2 * nc + 1:]
            j, i = pl.program_id(0), pl.program_id(1)

            @pl.when((j == 0) & (i == 0))
            def _():
                for cp in _chip_copies(c_ins, c_outs, ssem, rsem, c_mode)[0]:
                    cp.start()
        r = lax.dot_general(a_ref[...].astype(BF16), b_ref[...].astype(BF16), dn, preferred_element_type=F32)
        if add is not None:
            r = r + c_ref[...]
        o_ref[...] = r.astype(o_ref.dtype)
        if nc:
            @pl.when((j == nj - 1) & (i == ni - 1))
            def _():
                _chip_wait(*_chip_copies(c_ins, c_outs, ssem, rsem, c_mode))

    b_spec = pl.BlockSpec((tn, k), lambda j, i: (j, 0)) if nt else pl.BlockSpec((k, tn), lambda j, i: (0, j))
    in_specs = [pl.BlockSpec((tm, k), lambda j, i: (i, 0)), b_spec]
    args = [a, b]
    if add is not None:
        in_specs.append(pl.BlockSpec((tm, tn), lambda j, i: (i, j)))
        args.append(add)
    out_shape = jax.ShapeDtypeStruct((m, n), out_dtype)
    out_spec = pl.BlockSpec((tm, tn), lambda j, i: (i, j))
    if not nc:
        return pl.pallas_call(
            body, out_shape=out_shape, grid=(nj, ni), in_specs=in_specs, out_specs=out_spec,
            compiler_params=_cp("parallel", "parallel"), name=name)(*args)
    sem = pltpu.SemaphoreType.DMA
    outs = pl.pallas_call(
        body, out_shape=(out_shape,) + tuple(_landing_shape(x, c_mode) for x in c_arrs), grid=(nj, ni),
        in_specs=in_specs + [ANY] * nc, out_specs=(out_spec,) + (ANY,) * nc,
        scratch_shapes=[sem((nc, 3)), sem((nc, 3))],
        compiler_params=_cp("arbitrary", "arbitrary"), name=name)(*args, *c_arrs)
    return outs[0], list(outs[1:])


def _call_with_carry(body, *, out_shape, grid, in_specs, out_specs, args, semantics, carry, name, aliases=None):
    if carry is None:
        return pl.pallas_call(body, out_shape=out_shape, grid=grid, in_specs=in_specs, out_specs=out_specs,
                              input_output_aliases=aliases or {}, compiler_params=_cp(*semantics), name=name)(*args)
    c_arrs, c_mode = carry
    n_in, nc = len(args), len(c_arrs)

    def carried(*refs):
        c_ins, c_outs = refs[n_in:n_in + nc], refs[n_in + nc + 1:n_in + 2 * nc + 1]
        ssem, rsem = refs[n_in + 2 * nc + 1:]
        ids = [pl.program_id(d) for d in range(len(grid))]
        first = functools.reduce(lambda u, v: u & v, [i == 0 for i in ids])
        last = functools.reduce(lambda u, v: u & v, [i == g - 1 for i, g in zip(ids, grid)])

        @pl.when(first)
        def _():
            for cp in _chip_copies(c_ins, c_outs, ssem, rsem, c_mode)[0]:
                cp.start()
        body(*refs[:n_in], refs[n_in + nc])

        @pl.when(last)
        def _():
            _chip_wait(*_chip_copies(c_ins, c_outs, ssem, rsem, c_mode))

    sem = pltpu.SemaphoreType.DMA
    outs = pl.pallas_call(
        carried, out_shape=(out_shape,) + tuple(_landing_shape(x, c_mode) for x in c_arrs), grid=grid,
        in_specs=list(in_specs) + [ANY] * nc, out_specs=(out_specs,) + (ANY,) * nc,
        scratch_shapes=[sem((nc, 3)), sem((nc, 3))], input_output_aliases=aliases or {},
        compiler_params=_cp(*(("arbitrary",) * len(grid))), name=name)(*args, *c_arrs)
    return outs[0], list(outs[1:])


def _mm_nt_sum(a_list, b_list, *, name, carry=None):
    m, n = a_list[0].shape[0], b_list[0].shape[0]
    tm = _tile(m, (640, 320, 256, 128, 64))
    np_ = len(a_list)

    def body(*refs):
        acc = None
        for a_ref, b_ref in zip(refs[:np_], refs[np_:2 * np_]):
            r = lax.dot_general(a_ref[...].astype(BF16), b_ref[...].astype(BF16), (((1,), (1,)), ((), ())),
                                preferred_element_type=F32)
            acc = r if acc is None else acc + r
        refs[2 * np_][...] = acc

    return _call_with_carry(
        body, out_shape=jax.ShapeDtypeStruct((m, n), F32), grid=(m // tm,),
        in_specs=[pl.BlockSpec((tm, a.shape[1]), lambda i: (i, 0)) for a in a_list]
        + [pl.BlockSpec(b.shape, lambda i: (0, 0)) for b in b_list],
        out_specs=pl.BlockSpec((tm, n), lambda i: (i, 0)), args=[*a_list, *b_list], semantics=("parallel",),
        carry=carry, name=name)


def _mm_tn(a, b, *, tn=None, blocks=None, into=None, name, carry=None):
    m, k = a.shape
    n = b.shape[1]
    tm = _tile(m, (1664, 640, 320, 256, 128, 64))
    tn = n if tn is None else tn
    if blocks is not None:
        total, first = blocks
        out_shape = jax.ShapeDtypeStruct((total, k, tn), F32)
        out_spec = pl.BlockSpec((None, k, tn), lambda j, i: (first + j, 0, 0))
    else:
        out_shape = jax.ShapeDtypeStruct((k, n), F32)
        out_spec = pl.BlockSpec((k, tn), lambda j, i: (0, j))

    def body(a_ref, b_ref, *rest):
        o_ref = rest[-1]

        @pl.when(pl.program_id(1) == 0)
        def _():
            o_ref[...] = jnp.zeros_like(o_ref)
        o_ref[...] += lax.dot_general(a_ref[...].astype(BF16), b_ref[...].astype(BF16),
                                      (((0,), (0,)), ((), ())), preferred_element_type=F32)

    in_specs = [pl.BlockSpec((tm, k), lambda j, i: (i, 0)), pl.BlockSpec((tm, tn), lambda j, i: (i, j))]
    args, alias = [a, b], {}
    if into is not None:
        in_specs.append(pl.BlockSpec(memory_space=pl.ANY))
        args.append(into)
        alias = {2: 0}
    return _call_with_carry(body, out_shape=out_shape, grid=(n // tn, m // tm), in_specs=in_specs, out_specs=out_spec,
                            args=args, semantics=("parallel", "arbitrary"), carry=carry, name=name, aliases=alias)


def _rms(x, w):
    r = lax.rsqrt(jnp.mean(x * x, axis=-1, keepdims=True) + EPS)
    return x * r * w


def _rms_bwd(x, w, dy):
    r = lax.rsqrt(jnp.mean(x * x, axis=-1, keepdims=True) + EPS)
    xh = x * r
    dxh = dy * w
    dx = r * (dxh - xh * jnp.mean(dxh * xh, axis=-1, keepdims=True))
    return dx, jnp.sum(dy * xh, axis=0, keepdims=True)


def _resid_norm(h, t, w_post, w_next, *, name):
    lp, d = h.shape
    tm = _row_tile(lp)
    has_t = t is not None

    def body(*refs):
        if has_t:
            h_ref, t_ref, wp_ref, wn_ref, ho_ref, hn_ref = refs
            hv = h_ref[...] + _rms(t_ref[...], wp_ref[...])
            ho_ref[...] = hv
        else:
            h_ref, wn_ref, hn_ref = refs
            hv = h_ref[...]
        hn_ref[...] = _rms(hv, wn_ref[...]).astype(BF16)

    row = pl.BlockSpec((tm, d), lambda i: (i, 0))
    vec = pl.BlockSpec((1, d), lambda i: (0, 0))
    if has_t:
        return pl.pallas_call(
            body, out_shape=(jax.ShapeDtypeStruct((lp, d), F32), jax.ShapeDtypeStruct((lp, d), BF16)),
            grid=(lp // tm,), in_specs=[row, row, vec, vec], out_specs=(row, row),
            compiler_params=_cp("parallel"), name=name)(h, t, w_post, w_next)
    return h, pl.pallas_call(
        body, out_shape=jax.ShapeDtypeStruct((lp, d), BF16), grid=(lp // tm,), in_specs=[row, vec],
        out_specs=row, compiler_params=_cp("parallel"), name=name)(h, w_next)


def _resid_norm_bwd(dh_out, dhn, h_new, t, w_next, w_post, *, name):
    lp, d = h_new.shape if h_new is not None else t.shape
    tm = _row_tile(lp)
    has_n = dhn is not None
    has_t = t is not None

    def body(*refs):
        refs = list(refs)
        dho_ref = refs.pop(0)
        if has_n:
            dhn_ref, hn_ref, wn_ref = refs.pop(0), refs.pop(0), refs.pop(0)
        if has_t:
            t_ref, wp_ref = refs.pop(0), refs.pop(0)
        dh_ref = refs.pop(0) if has_n else None
        dt_ref = refs.pop(0) if has_t else None
        dwn_ref = refs.pop(0) if has_n else None
        dwp_ref = refs.pop(0) if has_t else None
        first = pl.program_id(0) == 0
        dh = dho_ref[...]
        if has_n:
            dx, dwn = _rms_bwd(hn_ref[...], wn_ref[...], dhn_ref[...])
            dh = dh + dx
            dh_ref[...] = dh

            @pl.when(first)
            def _():
                dwn_ref[...] = jnp.zeros_like(dwn_ref)
            dwn_ref[...] += dwn
        if has_t:
            dt, dwp = _rms_bwd(t_ref[...], wp_ref[...], dh)
            dt_ref[...] = dt.astype(BF16)

            @pl.when(first)
            def _():
                dwp_ref[...] = jnp.zeros_like(dwp_ref)
            dwp_ref[...] += dwp

    row = pl.BlockSpec((tm, d), lambda i: (i, 0))
    vec = pl.BlockSpec((1, d), lambda i: (0, 0))
    args, in_specs, out_shape, out_specs = [dh_out], [row], [], []
    if has_n:
        args += [dhn, h_new, w_next]
        in_specs += [row, row, vec]
    if has_t:
        args += [t, w_post]
        in_specs += [row, vec]
    if has_n:
        out_shape.append(jax.ShapeDtypeStruct((lp, d), F32)); out_specs.append(row)
    if has_t:
        out_shape.append(jax.ShapeDtypeStruct((lp, d), BF16)); out_specs.append(row)
    if has_n:
        out_shape.append(jax.ShapeDtypeStruct((1, d), F32)); out_specs.append(vec)
    if has_t:
        out_shape.append(jax.ShapeDtypeStruct((1, d), F32)); out_specs.append(vec)
    outs = list(pl.pallas_call(body, out_shape=tuple(out_shape), grid=(lp // tm,), in_specs=in_specs,
                               out_specs=tuple(out_specs), compiler_params=_cp("arbitrary"), name=name)(*args))
    dh = outs.pop(0) if has_n else dh_out
    dt = outs.pop(0) if has_t else None
    dwn = outs.pop(0) if has_n else None
    dwp = outs.pop(0) if has_t else None
    return dh, dt, dwn, dwp


def _loss_head(h, f, w_post, target, *, name):
    lp, d = h.shape
    tm = _row_tile(lp)

    def body(h_ref, f_ref, w_ref, t_ref, loss_ref, dy_ref):
        i = pl.program_id(0)
        y = h_ref[...] + _rms(f_ref[...], w_ref[...])
        rows = i * tm + lax.broadcasted_iota(jnp.int32, (tm, 1), 0)
        tok = (rows >= FRONT) & (rows < lp - BACK)
        err = jnp.where(tok, y - t_ref[...], 0.0)
        dy_ref[...] = err * (1.0 / d)

        @pl.when(i == 0)
        def _():
            loss_ref[...] = jnp.zeros_like(loss_ref)
        part = jnp.sum(jnp.sum(err * err, axis=1, keepdims=True), axis=0, keepdims=True) * (0.5 / d)
        loss_ref[...] += jnp.broadcast_to(part, loss_ref.shape)

    row = pl.BlockSpec((tm, d), lambda i: (i, 0))
    loss, dy = pl.pallas_call(
        body, out_shape=(jax.ShapeDtypeStruct((8, 128), F32), jax.ShapeDtypeStruct((lp, d), F32)),
        grid=(lp // tm,), in_specs=[row, row, pl.BlockSpec((1, d), lambda i: (0, 0)), row],
        out_specs=(pl.BlockSpec((8, 128), lambda i: (0, 0)), row),
        compiler_params=_cp("arbitrary"), name=name)(h, f, w_post, target)
    return loss[0, 0], dy


_GELU_C = math.sqrt(2.0 / math.pi)


def _gelu_and_grad(a):
    a2 = a * a
    t = jnp.tanh(a * (_GELU_C + (_GELU_C * 0.044715) * a2))
    ha = 0.5 * a
    h1 = 0.5 + 0.5 * t
    return a * h1, h1 + ha * (1.0 - t * t) * (_GELU_C + (3.0 * _GELU_C * 0.044715) * a2)


def _gelu(a):
    t = jnp.tanh(a * (_GELU_C + (_GELU_C * 0.044715) * (a * a)))
    return a * (0.5 + 0.5 * t)


def _conv3(parts, n, w, b):
    xx = jnp.concatenate(parts, axis=0)
    return b + xx[8:8 + n] * w[2:3] + pltpu.roll(xx, 1, 0)[8:8 + n] * w[1:2] + pltpu.roll(xx, 2, 0)[8:8 + n] * w[0:1]


def _conv_act(ua, ug, wa, wg, ba, bg, *, name):
    lp, n = ua.shape
    tm = _row_tile(lp)
    tc = _tile(n, (256, 128))
    nb8 = tm // 8

    def body(ua_ref, uap_ref, ug_ref, ugp_ref, wa_ref, wg_ref, ba_ref, bg_ref, o_ref):
        i = pl.program_id(0)
        ca = _conv3([uap_ref[...], ua_ref[...]], tm, wa_ref[...], ba_ref[...])
        cg = _conv3([ugp_ref[...], ug_ref[...]], tm, wg_ref[...], bg_ref[...])
        rows = i * tm + lax.broadcasted_iota(jnp.int32, (tm, 1), 0)
        ok = (rows >= PADF) & (rows < lp - BACK)
        o_ref[...] = jnp.where(ok, _gelu(ca) * cg, 0.0).astype(BF16)

    cur = pl.BlockSpec((tm, tc), lambda i, j: (i, j))
    prev = pl.BlockSpec((8, tc), lambda i, j: (jnp.maximum(i * nb8 - 1, 0), j))
    w3 = pl.BlockSpec((3, tc), lambda i, j: (0, j))
    b1 = pl.BlockSpec((1, tc), lambda i, j: (0, j))
    return pl.pallas_call(
        body, out_shape=jax.ShapeDtypeStruct((lp, n), BF16), grid=(lp // tm, n // tc),
        in_specs=[cur, prev, cur, prev, w3, w3, b1, b1], out_specs=cur,
        compiler_params=_cp("parallel", "parallel"), name=name)(ua, ua, ug, ug, wa, wg, ba, bg)


def _conv_act_bwd(ua, ug, dact, wa, wg, ba, bg, *, name):
    lp, n = ua.shape
    tm = _row_tile(lp)
    tc = _tile(n, (256, 128))
    nb8 = tm // 8
    last8 = lp // 8 - 1
    ext = tm + 8

    def body(ua_ref, uap_ref, uan_ref, ug_ref, ugp_ref, ugn_ref, da_ref, dan_ref, wa_ref, wg_ref, ba_ref, bg_ref,
             dua_ref, dug_ref, dwa_ref, dwg_ref, dba_ref, dbg_ref):
        i = pl.program_id(1)
        wa, wg = wa_ref[...], wg_ref[...]

        def conv(parts, w, b):
            xx = jnp.concatenate(parts, axis=0)
            x, x1, x2 = xx[8:8 + ext], pltpu.roll(xx, 1, 0)[8:8 + ext], pltpu.roll(xx, 2, 0)[8:8 + ext]
            return b + x * w[2:3] + x1 * w[1:2] + x2 * w[0:1], x, x1, x2

        ca, xa, xa1, xa2 = conv([uap_ref[...], ua_ref[...], uan_ref[...]], wa, ba_ref[...])
        cg, xg, xg1, xg2 = conv([ugp_ref[...], ug_ref[...], ugn_ref[...]], wg, bg_ref[...])
        rows = i * tm + lax.broadcasted_iota(jnp.int32, (ext, 1), 0)
        ok = (rows >= PADF) & (rows < lp - BACK)
        dact_e = jnp.where(ok, jnp.concatenate([da_ref[...], dan_ref[...]], axis=0), 0.0)
        gel, gel_d = _gelu_and_grad(ca)
        dca = dact_e * cg * gel_d
        dcg = dact_e * gel

        def back(dc, w):
            return (dc[:tm] * w[2:3] + pltpu.roll(dc, ext - 1, 0)[:tm] * w[1:2]
                    + pltpu.roll(dc, ext - 2, 0)[:tm] * w[0:1])

        dua_ref[...] = back(dca, wa).astype(BF16)
        dug_ref[...] = back(dcg, wg).astype(BF16)

        @pl.when(i == 0)
        def _():
            dwa_ref[...] = jnp.zeros_like(dwa_ref)
            dwg_ref[...] = jnp.zeros_like(dwg_ref)
            dba_ref[...] = jnp.zeros_like(dba_ref)
            dbg_ref[...] = jnp.zeros_like(dbg_ref)

        def wsum(dw_ref, db_ref, dc, x, x1, x2):
            d = dc[:tm]
            s = lambda v: jnp.sum(v, axis=0, keepdims=True)
            dw_ref[0:1, :] += s(d * x2[:tm])
            dw_ref[1:2, :] += s(d * x1[:tm])
            dw_ref[2:3, :] += s(d * x[:tm])
            db_ref[...] += s(d)

        wsum(dwa_ref, dba_ref, dca, xa, xa1, xa2)
        wsum(dwg_ref, dbg_ref, dcg, xg, xg1, xg2)

    cur = pl.BlockSpec((tm, tc), lambda j, i: (i, j))
    prev = pl.BlockSpec((8, tc), lambda j, i: (jnp.maximum(i * nb8 - 1, 0), j))
    nxt = pl.BlockSpec((8, tc), lambda j, i: (jnp.minimum((i + 1) * nb8, last8), j))
    w3 = pl.BlockSpec((3, tc), lambda j, i: (0, j))
    b1 = pl.BlockSpec((1, tc), lambda j, i: (0, j))
    return pl.pallas_call(
        body,
        out_shape=(jax.ShapeDtypeStruct((lp, n), BF16), jax.ShapeDtypeStruct((lp, n), BF16),
                   jax.ShapeDtypeStruct((3, n), F32), jax.ShapeDtypeStruct((3, n), F32),
                   jax.ShapeDtypeStruct((1, n), F32), jax.ShapeDtypeStruct((1, n), F32)),
        grid=(n // tc, lp // tm),
        in_specs=[cur, prev, nxt, cur, prev, nxt, cur, nxt, w3, w3, b1, b1],
        out_specs=(cur, cur, w3, w3, b1, b1),
        compiler_params=_cp("parallel", "arbitrary"), name=name)(ua, ua, ua, ug, ug, ug, dact, dact, wa, wg, ba, bg)


def _sigmoid(x):
    return 1.0 / (1.0 + jnp.exp(-x))


def _merge(o_ret, o_gla, proj, w_ret, w_gla, *, name):
    lp = o_ret.shape[0]
    tm = _row_tile(lp)

    def body(or_ref, og_ref, rg_ref, gr_ref, wr_ref, wg_ref, m_ref):
        oret, ogla = or_ref[...], og_ref[...]
        yr, yg = [], []
        for h in range(4):
            hs = slice(128 * h, 128 * h + 128)
            o = oret[:, hs]
            xc = o - jnp.mean(o, axis=-1, keepdims=True)
            yr.append(xc * lax.rsqrt(jnp.mean(xc * xc, axis=-1, keepdims=True) + EPS))
            o = ogla[:, hs]
            yg.append(o * lax.rsqrt(jnp.mean(o * o, axis=-1, keepdims=True) + EPS))
        rg, gr = rg_ref[...], gr_ref[...]
        m_ref[:, 0:512] = (jnp.concatenate(yr, axis=1) * wr_ref[...] * (rg * _sigmoid(rg))).astype(BF16)
        m_ref[:, 512:1024] = (jnp.concatenate(yg, axis=1) * wg_ref[...] * (gr * _sigmoid(gr))).astype(BF16)

    row = pl.BlockSpec((tm, 512), lambda i: (i, 0))
    vec = pl.BlockSpec((1, 512), lambda i: (0, 0))
    return pl.pallas_call(
        body, out_shape=jax.ShapeDtypeStruct((lp, 1024), BF16), grid=(lp // tm,),
        in_specs=[row, row, pl.BlockSpec((tm, 512), lambda i: (i, C_RG // 512)),
                  pl.BlockSpec((tm, 512), lambda i: (i, C_GR // 512)), vec, vec],
        out_specs=pl.BlockSpec((tm, 1024), lambda i: (i, 0)),
        compiler_params=_cp("parallel"), name=name)(o_ret, o_gla, proj, proj, w_ret, w_gla)


def _merge_bwd(dm, o_ret, o_gla, proj, w_ret, w_gla, *, name):
    lp = o_ret.shape[0]
    tm = _row_tile(lp)

    def body(dm_ref, or_ref, og_ref, rg_ref, gr_ref, wr_ref, wg_ref, dor_ref, dog_ref, dgate_ref, dwr_ref, dwg_ref):
        @pl.when(pl.program_id(0) == 0)
        def _():
            dwr_ref[...] = jnp.zeros_like(dwr_ref)
            dwg_ref[...] = jnp.zeros_like(dwg_ref)

        def group(d, o_all, gate, w, center):
            sg = _sigmoid(gate)
            s = gate * sg
            ds = sg * (1.0 + gate * (1.0 - sg))
            xh, rr = [], []
            for h in range(4):
                o = o_all[:, 128 * h:128 * h + 128]
                if center:
                    o = o - jnp.mean(o, axis=-1, keepdims=True)
                r = lax.rsqrt(jnp.mean(o * o, axis=-1, keepdims=True) + EPS)
                xh.append(o * r)
                rr.append(r)
            xh_all = jnp.concatenate(xh, axis=1)
            dgate = d * xh_all * w * ds
            dw = jnp.sum(d * xh_all * s, axis=0, keepdims=True)
            dxh_all = d * w * s
            do = []
            for h in range(4):
                dxh = dxh_all[:, 128 * h:128 * h + 128]
                t = dxh - xh[h] * jnp.mean(dxh * xh[h], axis=-1, keepdims=True)
                if center:
                    t = t - jnp.mean(dxh, axis=-1, keepdims=True)
                do.append(rr[h] * t)
            return jnp.concatenate(do, axis=1), dgate, dw

        dmv = dm_ref[...]
        do, dg, dw = group(dmv[:, 0:512], or_ref[...], rg_ref[...], wr_ref[...], True)
        dor_ref[...] = do
        dgate_ref[:, 0:512] = dg.astype(BF16)
        dwr_ref[...] += dw
        do, dg, dw = group(dmv[:, 512:1024], og_ref[...], gr_ref[...], wg_ref[...], False)
        dog_ref[...] = do
        dgate_ref[:, 512:1024] = dg.astype(BF16)
        dwg_ref[...] += dw

    row = pl.BlockSpec((tm, 512), lambda i: (i, 0))
    vec = pl.BlockSpec((1, 512), lambda i: (0, 0))
    return pl.pallas_call(
        body,
        out_shape=(jax.ShapeDtypeStruct((lp, 512), F32), jax.ShapeDtypeStruct((lp, 512), F32),
                   jax.ShapeDtypeStruct((lp, P_GATE), BF16),
                   jax.ShapeDtypeStruct((1, 512), F32), jax.ShapeDtypeStruct((1, 512), F32)),
        grid=(lp // tm,),
        in_specs=[pl.BlockSpec((tm, 1024), lambda i: (i, 0)), row, row,
                  pl.BlockSpec((tm, 512), lambda i: (i, C_RG // 512)),
                  pl.BlockSpec((tm, 512), lambda i: (i, C_GR // 512)), vec, vec],
        out_specs=(row, row, pl.BlockSpec((tm, P_GATE), lambda i: (i, 0)), vec, vec),
        compiler_params=_cp("arbitrary"), name=name)(dm, o_ret, o_gla, proj, proj, w_ret, w_gla)


def _dot(a, b):
    return lax.dot_general(a, b, (((1,), (0,)), ((), ())), preferred_element_type=F32)


def _dot_nt(a, b):
    return lax.dot_general(a, b, (((1,), (1,)), ((), ())), preferred_element_type=F32)


def _dot_tn(a, b):
    return lax.dot_general(a, b, (((0,), (0,)), ((), ())), preferred_element_type=F32)


def _ret_tables(lp):
    cr = RET_CHUNK
    pos = np.arange(lp, dtype=np.float32) - np.float32(PADF)
    half = RET_DK // 2
    inv = (np.float32(ROPE_BASE) ** (-np.arange(half, dtype=np.float32) / np.float32(half))).astype(np.float32)
    ang = (pos[:, None] * inv[None, :]).astype(np.float32)
    c, s = np.cos(ang).astype(np.float32), np.sin(ang).astype(np.float32)
    rope_c = jnp.asarray(np.concatenate([c, c], axis=1))
    rope_s = jnp.asarray(np.concatenate([-s, s], axis=1))
    log_g = np.log(1.0 - 2.0 ** (-5.0 - np.arange(RET_HEADS, dtype=np.float64)))
    idx = np.arange(cr, dtype=np.float64)
    diff = idx[:, None] - idx[None, :]
    dmat = np.where(diff >= 0, np.exp(log_g[:, None, None] * np.maximum(diff, 0.0)), 0.0)
    zeta = np.exp(log_g[:, None] * (cr - 1.0 - idx)[None, :])
    xi = np.exp(log_g[:, None] * (idx + 1.0)[None, :])
    gc = np.exp(log_g * cr)
    f = lambda a: jnp.asarray(a.astype(np.float32))
    return (rope_c, rope_s, f(dmat), f(np.broadcast_to(zeta[:, :, None], (RET_HEADS, cr, 128))),
            f(np.broadcast_to(xi[:, :, None], (RET_HEADS, cr, 128))),
            f(np.broadcast_to(gc[:, None, None], (RET_HEADS, 8, 128))))


def _rope(t, c, s):
    return t * c + pltpu.roll(t, 64, 1) * s


def _rope_t(d, c, s):
    return d * c + pltpu.roll(d * s, 64, 1)


def _ret_specs(nblk, rev):
    ix = (lambda i: nblk - 1 - i) if rev else (lambda i: i)
    cr = RET_CHUNK
    col = lambda base: pl.BlockSpec((BLK, 512), lambda i: (ix(i), base // 512))
    tab = pl.BlockSpec((BLK, 128), lambda i: (ix(i), 0))
    sq = pl.BlockSpec((RET_HEADS, cr, cr), lambda i: (0, 0, 0))
    hv = pl.BlockSpec((RET_HEADS, cr, 128), lambda i: (0, 0, 0))
    g8 = pl.BlockSpec((RET_HEADS, 8, 128), lambda i: (0, 0, 0))
    st = pl.BlockSpec((RET_HEADS, BLK // cr, 128, 128), lambda i: (0, ix(i), 0, 0))
    out = pl.BlockSpec((BLK, 512), lambda i: (ix(i), 0))
    return col, tab, sq, hv, g8, st, out


def _retention(proj, tables, *, name):
    lp = proj.shape[0]
    nblk, cr = lp // BLK, RET_CHUNK
    scale = RET_DK ** -0.5

    def body(q_ref, k_ref, v_ref, c_ref, s_ref, d_ref, z_ref, x_ref, g_ref, o_ref, st_ref, state):
        @pl.when(pl.program_id(0) == 0)
        def _():
            state[...] = jnp.zeros_like(state)

        def chunk(ci, carry):
            sl = pl.ds(pl.multiple_of(ci * cr, cr), cr)
            c, s = c_ref[sl, :], s_ref[sl, :]
            for h in range(RET_HEADS):
                hs = slice(128 * h, 128 * h + 128)
                q = _rope(q_ref[sl, hs], c, s)
                k = _rope(k_ref[sl, hs], c, s) * scale
                qb, kb, vb = q.astype(BF16), k.astype(BF16), v_ref[sl, hs].astype(BF16)
                st = state[h]
                st_ref[h, ci] = st
                sc = _dot_nt(qb, kb) * d_ref[h]
                o_ref[sl, hs] = _dot(sc.astype(BF16), vb) + _dot(qb, st.astype(BF16)) * x_ref[h]
                state[h] = st * g_ref[h][0:1, :] + _dot_tn((k * z_ref[h]).astype(BF16), vb)
            return carry

        lax.fori_loop(0, BLK // cr, chunk, 0)

    col, tab, sq, hv, g8, st, out = _ret_specs(nblk, False)
    return pl.pallas_call(
        body,
        out_shape=(jax.ShapeDtypeStruct((lp, 512), F32), jax.ShapeDtypeStruct((4, lp // cr, 128, 128), F32)),
        grid=(nblk,), in_specs=[col(C_RQ), col(C_RK), col(C_RV), tab, tab, sq, hv, hv, g8],
        out_specs=(out, st), scratch_shapes=[pltpu.VMEM((RET_HEADS, 128, 128), F32)],
        compiler_params=_cp("arbitrary"), name=name)(proj, proj, proj, *tables)


def _retention_bwd(proj, do, states, tables, *, name):
    lp = proj.shape[0]
    nblk, cr = lp // BLK, RET_CHUNK
    nch = BLK // cr
    scale = RET_DK ** -0.5

    def body(q_ref, k_ref, v_ref, do_ref, st_ref, c_ref, s_ref, d_ref, z_ref, x_ref, g_ref, dqkv_ref, dstate):
        @pl.when(pl.program_id(0) == 0)
        def _():
            dstate[...] = jnp.zeros_like(dstate)

        def chunk(cc, carry):
            ci = nch - 1 - cc
            sl = pl.ds(pl.multiple_of(ci * cr, cr), cr)
            c, s = c_ref[sl, :], s_ref[sl, :]
            for h in range(RET_HEADS):
                hs = slice(128 * h, 128 * h + 128)
                dmat, zeta, xi = d_ref[h], z_ref[h], x_ref[h]
                q = _rope(q_ref[sl, hs], c, s)
                k = _rope(k_ref[sl, hs], c, s) * scale
                qb, kb, vb = q.astype(BF16), k.astype(BF16), v_ref[sl, hs].astype(BF16)
                kzb = (k * zeta).astype(BF16)
                dov = do_ref[sl, hs]
                dob, doxb = dov.astype(BF16), (dov * xi).astype(BF16)
                stb = st_ref[h, ci].astype(BF16)
                dsn = dstate[h]
                dsnb = dsn.astype(BF16)
                scb = (_dot_nt(qb, kb) * dmat).astype(BF16)
                dscb = (_dot_nt(dob, vb) * dmat).astype(BF16)
                dq = _dot(dscb, kb) + _dot_nt(doxb, stb)
                dk = _dot_tn(dscb, qb) + _dot_nt(vb, dsnb) * zeta
                dv = _dot_tn(scb, dob) + _dot(kzb, dsnb)
                dstate[h] = dsn * g_ref[h][0:1, :] + _dot_tn(qb, doxb)
                dqkv_ref[sl, 128 * h:128 * h + 128] = _rope_t(dq, c, s).astype(BF16)
                dqkv_ref[sl, 512 + 128 * h:640 + 128 * h] = _rope_t(dk * scale, c, s).astype(BF16)
                dqkv_ref[sl, 1024 + 128 * h:1152 + 128 * h] = dv.astype(BF16)
            return carry

        lax.fori_loop(0, nch, chunk, 0)

    col, tab, sq, hv, g8, st, out = _ret_specs(nblk, True)
    return pl.pallas_call(
        body, out_shape=jax.ShapeDtypeStruct((lp, P_RET), BF16), grid=(nblk,),
        in_specs=[col(C_RQ), col(C_RK), col(C_RV), out, st, tab, tab, sq, hv, hv, g8],
        out_specs=pl.BlockSpec((BLK, P_RET), lambda i: (nblk - 1 - i, 0)),
        scratch_shapes=[pltpu.VMEM((RET_HEADS, 128, 128), F32)],
        compiler_params=_cp("arbitrary"), name=name)(proj, proj, proj, do, states, *tables)


def _gla_tables():
    c = GLA_CHUNK
    tri = np.tril(np.ones((c, c), np.float32))
    ones_qv = np.kron(np.eye(GLA_HEADS, dtype=np.float32), np.ones((GLA_DK, GLA_DV), np.float32))
    return (jnp.asarray(tri, BF16), jnp.asarray(tri.T.copy(), BF16), jnp.asarray(ones_qv, BF16),
            jnp.asarray(ones_qv.T.copy(), BF16))


def _split3(x):
    hi = x.astype(BF16)
    r1 = x - hi.astype(F32)
    mid = r1.astype(BF16)
    lo = (r1 - mid.astype(F32)).astype(BF16)
    return hi, mid, lo


def _tri_sum(tri, x):
    hi, mid, lo = _split3(x)
    return _dot(tri, hi) + _dot(tri, mid) + _dot(tri, lo)


def _head_masks(width, per):
    lane = lax.broadcasted_iota(jnp.int32, (1, width), 1)
    return [((lane >= per * h) & (lane < per * (h + 1))).astype(F32) for h in range(GLA_HEADS)]


def _stack_heads(x, masks):
    return jnp.concatenate([x * m for m in masks], axis=0)


def _gla_gate(ga, w2, b, ok, tri):
    z = _dot(ga.astype(BF16), w2) + b
    la = (jnp.minimum(z, 0.0) - jnp.log(1.0 + jnp.exp(-jnp.abs(z)))) * (1.0 / GLA_TAU)
    la = jnp.where(ok, la, 0.0)
    return z, _tri_sum(tri, la)


def _gla_rows(i_blk, ci, lp):
    c = GLA_CHUNK
    rows = i_blk * BLK + ci * c + lax.broadcasted_iota(jnp.int32, (c, 1), 0)
    return (rows >= PADF) & (rows < lp - BACK)


N_SUB = GLA_CHUNK // GLA_SUB - 1
N_SUB2 = GLA_SUB // GLA_SUB2 - 1


def _gla_masks():
    c, s1, s2 = GLA_CHUNK, GLA_SUB, GLA_SUB2
    sh1, sh2 = s1.bit_length() - 1, s2.bit_length() - 1
    r = lax.broadcasted_iota(jnp.int32, (c, GLA_QK), 0)
    blk, within = jnp.right_shift(r, sh1), jnp.bitwise_and(r, s1 - 1)
    grp = jnp.right_shift(within, sh2)
    rowm = [(blk == a).astype(F32) for a in range(1, N_SUB + 1)] + [(grp == b).astype(F32) for b in range(1, N_SUB2 + 1)]
    keym = ([(r < s1 * a).astype(F32) for a in range(1, N_SUB + 1)]
            + [(within < s2 * b).astype(F32) for b in range(1, N_SUB2 + 1)])
    rs = lax.broadcasted_iota(jnp.int32, (GLA_HEADS * c, c), 0)
    ts = lax.broadcasted_iota(jnp.int32, (GLA_HEADS * c, c), 1)
    same = (jnp.right_shift(jnp.bitwise_and(rs, c - 1), sh1) == jnp.right_shift(ts, sh1)).astype(F32)
    lag = [(jnp.bitwise_and(r, s2 - 1) >= j).astype(F32) for j in range(s2)]
    return rowm, keym, same, lag


def _gla_hats(qs, k, g, masks, hm_q):
    c, s1, s2 = GLA_CHUNK, GLA_SUB, GLA_SUB2
    rowm, keym, same, _ = masks
    refs = [g[s1 * a - 1:s1 * a, :] for a in range(1, N_SUB + 1)]
    for b in range(1, N_SUB2 + 1):
        refs.append(jnp.concatenate([jnp.broadcast_to(g[s1 * i + s2 * b - 1:s1 * i + s2 * b, :], (s1, GLA_QK))
                                     for i in range(c // s1)], axis=0))
    eqs = [jnp.exp(jnp.minimum(g - r, 0.0)) * m for r, m in zip(refs, rowm)]
    eks = [jnp.exp(jnp.minimum(r - g, 0.0)) * m for r, m in zip(refs, keym)]
    qhs, khs = [qs * e for e in eqs], [k * e for e in eks]
    qst = [_stack_heads(q, hm_q).astype(BF16) for q in qhs]
    khb = [x.astype(BF16) for x in khs]
    qa, qb = jnp.concatenate(qst[:N_SUB], axis=1), jnp.concatenate(qst[N_SUB:], axis=1)
    ka, kb = jnp.concatenate(khb[:N_SUB], axis=1), jnp.concatenate(khb[N_SUB:], axis=1)
    p = _dot_nt(qa, ka) + _dot_nt(qb, kb) * same
    return eqs, eks, qhs, khs, qa, qb, ka, kb, p


def _roll_rows(x, j):
    return x if j == 0 else pltpu.roll(x, j, 0)


def _gla(proj, w2p, b, tables, *, name):
    lp = proj.shape[0]
    nblk, c, s2 = lp // BLK, GLA_CHUNK, GLA_SUB2
    nch = BLK // c

    def body(q_ref, k_ref, v_ref, a_ref, w_ref, b_ref, tri_ref, ones_ref, o_ref, st_ref, state):
        i_blk = pl.program_id(0)

        @pl.when(i_blk == 0)
        def _():
            state[...] = jnp.zeros_like(state)
        hm_q = _head_masks(GLA_QK, GLA_DK)
        masks = _gla_masks()
        tri, ones_qv, w2, bias = tri_ref[...], ones_ref[...], w_ref[...], b_ref[...]

        def chunk(ci, carry):
            sl = pl.ds(pl.multiple_of(ci * c, c), c)
            ok = _gla_rows(i_blk, ci, lp)
            k, v = k_ref[sl, :], v_ref[sl, :]
            vb = v.astype(BF16)
            qs = q_ref[sl, :] * (GLA_DK ** -0.5)
            _, g = _gla_gate(a_ref[sl, :], w2, bias, ok, tri)
            last = g[c - 1:c, :]
            st = state[...]
            st_ref[ci] = st
            qst = _stack_heads(qs * jnp.exp(g), hm_q).astype(BF16)
            oi = _dot_nt(qst, st.astype(BF16))
            o = jnp.concatenate([oi[c * h:c * h + c, :] for h in range(GLA_HEADS)], axis=1)
            ke = k * jnp.exp(last - g)
            f = _dot_tn(vb, ke.astype(BF16))
            upd = f[0:GLA_DV, :] * hm_q[0]
            for h in range(1, GLA_HEADS):
                upd = upd + f[GLA_DV * h:GLA_DV * (h + 1), :] * hm_q[h]
            state[...] = st * jnp.exp(last) + upd
            p = _gla_hats(qs, k, g, masks, hm_q)[-1]
            ob = _dot(p.astype(BF16), vb)
            o = o + jnp.concatenate([ob[c * h:c * h + c, GLA_DV * h:GLA_DV * (h + 1)] for h in range(GLA_HEADS)],
                                    axis=1)
            ws = []
            for j in range(s2):
                ej = jnp.exp(jnp.minimum(g - _roll_rows(g, j), 0.0))
                ws.append((qs * _roll_rows(k, j) * ej * masks[3][j]).astype(BF16))
            ball = _dot(jnp.concatenate(ws, axis=0), ones_qv)
            for j in range(s2):
                o = o + ball[c * j:c * j + c, :] * _roll_rows(v, j)
            o_ref[sl, :] = o
            return carry

        lax.fori_loop(0, nch, chunk, 0)

    tri, _, ones_qv, _ = tables
    full = lambda arr: pl.BlockSpec(arr.shape, lambda i: (0,) * arr.ndim)
    return pl.pallas_call(
        body,
        out_shape=(jax.ShapeDtypeStruct((lp, GLA_V), F32), jax.ShapeDtypeStruct((lp // c, GLA_DV, GLA_QK), F32)),
        grid=(nblk,),
        in_specs=[pl.BlockSpec((BLK, GLA_QK), lambda i: (i, C_GQ // GLA_QK)),
                  pl.BlockSpec((BLK, GLA_QK), lambda i: (i, C_GK // GLA_QK)),
                  pl.BlockSpec((BLK, GLA_V), lambda i: (i, C_GV // GLA_V)),
                  pl.BlockSpec((BLK, 128), lambda i: (i, C_GA // 128)),
                  full(w2p), full(b), full(tri), full(ones_qv)],
        out_specs=(pl.BlockSpec((BLK, GLA_V), lambda i: (i, 0)),
                   pl.BlockSpec((nch, GLA_DV, GLA_QK), lambda i: (i, 0, 0))),
        scratch_shapes=[pltpu.VMEM((GLA_DV, GLA_QK), F32)],
        compiler_params=_cp("arbitrary"), name=name)(proj, proj, proj, proj, w2p, b, tri, ones_qv)


def _gla_bwd(proj, do, states, w2p, b, tables, *, name):
    lp = proj.shape[0]
    nblk, c, s1, s2 = lp // BLK, GLA_CHUNK, GLA_SUB, GLA_SUB2
    nch = BLK // c

    def body(q_ref, k_ref, v_ref, a_ref, do_ref, st_ref, w_ref, b_ref, tri_ref, trit_ref, ones_ref, onest_ref,
             dp_ref, dw_ref, db_ref, dstate, dqs_s, dk_s, dg_s, dv_s):
        i_blk = nblk - 1 - pl.program_id(0)

        @pl.when(pl.program_id(0) == 0)
        def _():
            dstate[...] = jnp.zeros_like(dstate)
            dw_ref[...] = jnp.zeros_like(dw_ref)
            db_ref[...] = jnp.zeros_like(db_ref)
        hm_q = _head_masks(GLA_QK, GLA_DK)
        hm_v = _head_masks(GLA_V, GLA_DV)
        masks = _gla_masks()
        tri, trit, ones_qv, ones_vq = tri_ref[...], trit_ref[...], ones_ref[...], onest_ref[...]
        w2, bias = w_ref[...], b_ref[...]
        rsum = lambda x: jnp.sum(x, axis=0, keepdims=True)

        def chunk(cc, carry):
            ci = nch - 1 - cc
            sl = pl.ds(pl.multiple_of(ci * c, c), c)
            ok = _gla_rows(i_blk, ci, lp)
            k, v, ga = k_ref[sl, :], v_ref[sl, :], a_ref[sl, :]
            vb = v.astype(BF16)
            qs = q_ref[sl, :] * (GLA_DK ** -0.5)
            z, g = _gla_gate(ga, w2, bias, ok, tri)
            last = g[c - 1:c, :]
            elast = jnp.exp(last)
            eg = jnp.exp(g)
            ekl = jnp.exp(last - g)
            qe, ke = qs * eg, k * ekl
            dov = do_ref[sl, :]
            st = st_ref[ci]
            dsn = dstate[...]
            qst = _stack_heads(qe, hm_q).astype(BF16)
            dost = jnp.concatenate([dov[:, GLA_DV * h:GLA_DV * (h + 1)] for h in range(GLA_HEADS)], axis=0).astype(BF16)
            dqe_st = _dot(dost, st.astype(BF16))
            dqe = dqe_st[0:c, :] * hm_q[0]
            for h in range(1, GLA_HEADS):
                dqe = dqe + dqe_st[c * h:c * h + c, :] * hm_q[h]
            dstate[...] = _dot_tn(dost, qst) + dsn * elast
            dlast = rsum(dsn * st) * elast
            df = _stack_heads(dsn, hm_q).astype(BF16)
            dv_s[...] = _dot_nt(ke.astype(BF16), df)
            dke = _dot(vb, df)
            xk = dke * ke
            dqs_s[...] = dqe * eg
            dk_s[...] = dke * ekl
            dg_s[...] = dqe * qe - xk
            dlast = dlast + rsum(xk)
            eqs, eks, qhs, khs, qa, qb, ka, kb, p = _gla_hats(qs, k, g, masks, hm_q)
            dost_v = _stack_heads(dov, hm_v).astype(BF16)
            dp = _dot_nt(dost_v, vb)
            dv_s[...] += _dot_tn(p.astype(BF16), dost_v)
            dpa, dpb = dp.astype(BF16), (dp * masks[2]).astype(BF16)
            dq_all = (_dot(dpa, ka), _dot(dpb, kb))
            dk_all = (_dot_tn(dpa, qa), _dot_tn(dpb, qb))
            for t in range(N_SUB + N_SUB2):
                lvl, i = (0, t) if t < N_SUB else (1, t - N_SUB)
                cols = slice(GLA_QK * i, GLA_QK * (i + 1))
                dq_st = dq_all[lvl][:, cols]
                dqh = dq_st[0:c, :] * hm_q[0]
                for h in range(1, GLA_HEADS):
                    dqh = dqh + dq_st[c * h:c * h + c, :] * hm_q[h]
                dkh = dk_all[lvl][:, cols]
                xq, xkh = dqh * qhs[t], dkh * khs[t]
                dqs_s[...] += dqh * eqs[t]
                dk_s[...] += dkh * eks[t]
                dg_s[...] += xq - xkh
                back_ref = xkh - xq
                if lvl == 0:
                    row = s1 * (i + 1) - 1
                    dg_s[row:row + 1, :] += rsum(back_ref)
                else:
                    for blk in range(c // s1):
                        row = s1 * blk + s2 * (i + 1) - 1
                        dg_s[row:row + 1, :] += rsum(back_ref[s1 * blk:s1 * blk + s1, :])
            kes, qes, ws, dbs = [], [], [], []
            for j in range(s2):
                em = jnp.exp(jnp.minimum(g - _roll_rows(g, j), 0.0)) * masks[3][j]
                kes.append(_roll_rows(k, j) * em)
                qes.append(qs * em)
                ws.append((qs * kes[j]).astype(BF16))
                dbs.append((dov * _roll_rows(v, j)).astype(BF16))
            ball = _dot(jnp.concatenate(ws, axis=0), ones_qv)
            dwall = _dot(jnp.concatenate(dbs, axis=0), ones_vq)
            for j in range(s2):
                back = (lambda x: x) if j == 0 else (lambda x, j=j: pltpu.roll(x, c - j, 0))
                dw = dwall[c * j:c * j + c, :]
                dv_s[...] += back(ball[c * j:c * j + c, :] * dov)
                dqs_s[...] += dw * kes[j]
                dk_s[...] += back(dw * qes[j])
                x = dw * qs * kes[j]
                dg_s[...] += x - back(x)
            dg_s[c - 1:c, :] += dlast
            dla = jnp.where(ok, _tri_sum(trit, dg_s[...]), 0.0)
            dz = dla * (1.0 / GLA_TAU) / (1.0 + jnp.exp(z))
            dzb = dz.astype(BF16)
            dp_ref[sl, 0:256] = (dqs_s[...] * (GLA_DK ** -0.5)).astype(BF16)
            dp_ref[sl, 256:512] = dk_s[...].astype(BF16)
            dp_ref[sl, 512:1024] = dv_s[...].astype(BF16)
            dp_ref[sl, 1024:1152] = _dot_nt(dzb, w2).astype(BF16)
            dp_ref[sl, 1152:1280] = jnp.zeros((c, 128), BF16)
            dw_ref[...] += _dot_tn(ga.astype(BF16), dzb)
            db_ref[...] += rsum(dz)
            return carry

        lax.fori_loop(0, nch, chunk, 0)

    tri, trit, ones_qv, ones_vq = tables
    full = lambda arr: pl.BlockSpec(arr.shape, lambda i: (0,) * arr.ndim)
    rev = lambda i: nblk - 1 - i
    return pl.pallas_call(
        body,
        out_shape=(jax.ShapeDtypeStruct((lp, P_GLA), BF16),
                   jax.ShapeDtypeStruct((128, GLA_QK), F32), jax.ShapeDtypeStruct((1, GLA_QK), F32)),
        grid=(nblk,),
        in_specs=[pl.BlockSpec((BLK, GLA_QK), lambda i: (rev(i), C_GQ // GLA_QK)),
                  pl.BlockSpec((BLK, GLA_QK), lambda i: (rev(i), C_GK // GLA_QK)),
                  pl.BlockSpec((BLK, GLA_V), lambda i: (rev(i), C_GV // GLA_V)),
                  pl.BlockSpec((BLK, 128), lambda i: (rev(i), C_GA // 128)),
                  pl.BlockSpec((BLK, GLA_V), lambda i: (rev(i), 0)),
                  pl.BlockSpec((nch, GLA_DV, GLA_QK), lambda i: (rev(i), 0, 0)),
                  full(w2p), full(b), full(tri), full(trit), full(ones_qv), full(ones_vq)],
        out_specs=(pl.BlockSpec((BLK, P_GLA), lambda i: (rev(i), 0)),
                   pl.BlockSpec((128, GLA_QK), lambda i: (0, 0)),
                   pl.BlockSpec((1, GLA_QK), lambda i: (0, 0))),
        scratch_shapes=[pltpu.VMEM((GLA_DV, GLA_QK), F32), pltpu.VMEM((c, GLA_QK), F32),
                        pltpu.VMEM((c, GLA_QK), F32), pltpu.VMEM((c, GLA_QK), F32), pltpu.VMEM((c, GLA_V), F32)],
        compiler_params=_cp("arbitrary"), name=name)(proj, proj, proj, proj, do, states, w2p, b, tri, trit, ones_qv, ones_vq)


def _as2d(a):
    return a.reshape(-1, a.shape[-1])


def _ew_tile(r):
    return _tile(r, (512, 256, 128, 64, 32, 16, 8))


def _add2(a, b, *, out_dtype, name):
    a2, b2 = _as2d(a), _as2d(b)
    r, n = a2.shape
    tm = _ew_tile(r)

    def body(a_ref, b_ref, o_ref):
        o_ref[...] = (a_ref[...] + b_ref[...]).astype(o_ref.dtype)

    blk = pl.BlockSpec((tm, n), lambda i: (i, 0))
    return pl.pallas_call(body, out_shape=jax.ShapeDtypeStruct((r, n), out_dtype), grid=(r // tm,), in_specs=[blk, blk],
                          out_specs=blk, compiler_params=_cp("parallel"), name=name)(a2, b2).reshape(a.shape)


def _sum_slots(own, q, *, name):
    shape = own.shape
    q3 = q.reshape(3, -1, shape[-1])
    own2 = _as2d(own)
    r, n = own2.shape
    tm = _ew_tile(r)

    def body(own_ref, q_ref, o_ref):
        f = lambda i: q_ref[i].astype(F32)
        o_ref[...] = ((own_ref[...].astype(F32) + f(0)) + f(1)) + f(2)

    blk = pl.BlockSpec((tm, n), lambda i: (i, 0))
    return pl.pallas_call(
        body, out_shape=jax.ShapeDtypeStruct((r, n), F32), grid=(r // tm,),
        in_specs=[blk, pl.BlockSpec((3, tm, n), lambda i: (0, i, 0))], out_specs=blk,
        compiler_params=_cp("parallel"), name=name)(own2, q3).reshape(shape)


def _adamw(w, g, m, v, *, name):
    shape = w.shape
    w2, g2, m2, v2 = _as2d(w), _as2d(g), _as2d(m), _as2d(v)
    r, n = w2.shape
    tm = _ew_tile(r)
    c1 = 1.0 - ADAM_B1 ** ADAM_STEP
    c2 = 1.0 - ADAM_B2 ** ADAM_STEP

    def body(w_ref, g_ref, m_ref, v_ref, d_ref, mo_ref, vo_ref):
        gv = g_ref[...]
        mn = ADAM_B1 * m_ref[...] + (1.0 - ADAM_B1) * gv
        vn = ADAM_B2 * v_ref[...] + (1.0 - ADAM_B2) * (gv * gv)
        mo_ref[...] = mn
        vo_ref[...] = vn
        d_ref[...] = -ADAM_LR * ((mn / c1) / (jnp.sqrt(vn / c2) + ADAM_EPS) + ADAM_WD * w_ref[...])

    blk = pl.BlockSpec((tm, n), lambda i: (i, 0))
    o = jax.ShapeDtypeStruct((r, n), F32)
    d, mo, vo = pl.pallas_call(body, out_shape=(o, o, o), grid=(r // tm,), in_specs=[blk] * 4, out_specs=(blk,) * 3,
                               compiler_params=_cp("parallel"), name=name)(w2, g2, m2, v2)
    return d.reshape(shape), mo.reshape(shape), vo.reshape(shape)


ANY = pl.BlockSpec(memory_space=pl.ANY)


def _place():
    return lax.axis_index("x"), lax.axis_index("y"), lax.axis_index("c")


def _other_chips(x, y):
    return [(1 - x, y), (x, 1 - y), (1 - x, 1 - y)]


def _remote(src, dst, ssem, rsem, dev):
    return pltpu.make_async_remote_copy(src_ref=src, dst_ref=dst, send_sem=ssem, recv_sem=rsem, device_id=dev,
                                        device_id_type=MESH)


def _allgather_chips(arrs, *, name):
    n = len(arrs)

    def body(*refs):
        ins, outs = refs[:n], refs[n:2 * n]
        s1, r1, s2, r2 = refs[2 * n:]
        x, y, c = _place()
        q = 2 * x + y
        chips = _other_chips(x, y)
        qs = [2 * cx + cy for cx, cy in chips]
        sib = (x, y, 1 - c)
        first, passed = [], []
        for k in range(n):
            for j, chip in enumerate(chips):
                first.append(_remote(ins[k].at[c], outs[k].at[c, q], s1.at[k, j], r1.at[k, j], (*chip, c)))
        for cp in first:
            cp.start()
        for k in range(n):
            for j, chip in enumerate(chips):
                land = outs[k].at[c, qs[j]]
                _remote(land, land, s1.at[k, j], r1.at[k, j], (*chip, c)).wait_recv()
                fw = _remote(land, land, s2.at[k, j], r2.at[k, j], sib)
                fw.start()
                passed.append(fw)
        for k in range(n):
            for j in range(3):
                land = outs[k].at[1 - c, qs[j]]
                _remote(land, land, s2.at[k, j], r2.at[k, j], sib).wait_recv()
        for cp in first + passed:
            cp.wait_send()

    sem = pltpu.SemaphoreType.DMA
    outs = pl.pallas_call(
        body, out_shape=tuple(jax.ShapeDtypeStruct((2, 4) + a.shape[1:], a.dtype) for a in arrs),
        in_specs=[ANY] * n, out_specs=(ANY,) * n,
        scratch_shapes=[sem((n, 3)), sem((n, 3)), sem((n, 3)), sem((n, 3))], name=name)(*arrs)
    chip = 2 * lax.axis_index("x") + lax.axis_index("y")
    return [lax.dynamic_update_slice_in_dim(o, a[:, None], chip, axis=1) for o, a in zip(outs, arrs)]


def _pair_exchange(arrs, *, name):
    n = len(arrs)

    def body(*refs):
        ins, outs = refs[:n], refs[n:2 * n]
        ssem, rsem = refs[2 * n:]
        x, y, c = _place()
        cps = [_remote(ins[k].at[:, 1 - c], outs[k], ssem.at[k], rsem.at[k], (x, y, 1 - c)) for k in range(n)]
        for cp in cps:
            cp.start()
        for cp in cps:
            cp.wait()

    sem = pltpu.SemaphoreType.DMA
    return pl.pallas_call(
        body, out_shape=tuple(jax.ShapeDtypeStruct((a.shape[0],) + a.shape[2:], a.dtype) for a in arrs),
        in_specs=[ANY] * n, out_specs=(ANY,) * n, scratch_shapes=[sem((n,)), sem((n,))], name=name)(*arrs)


def _pair_sum(mine, theirs, c, *, name):
    _, _, r, n = mine.shape
    tm = r if r <= 512 else _ew_tile(r)

    def body(c_ref, a_ref, b_ref, o_ref):
        o_ref[...] = (a_ref[...] + b_ref[...]).astype(BF16)

    blk = pl.BlockSpec((None, tm, n), lambda s, i, c_ref: (s, i, 0))
    return pl.pallas_call(
        body, out_shape=jax.ShapeDtypeStruct((4, r, n), BF16),
        grid_spec=pltpu.PrefetchScalarGridSpec(
            num_scalar_prefetch=1, grid=(4, r // tm),
            in_specs=[pl.BlockSpec((None, None, tm, n), lambda s, i, c_ref: (s, c_ref[0], i, 0)), blk], out_specs=blk),
        compiler_params=_cp("parallel", "parallel"), name=name)(jnp.reshape(c, (1,)).astype(jnp.int32), mine, theirs)


def _chip_copies(ins, outs, ssem, rsem, mode):
    x, y, c = _place()
    q = 2 * x + y
    sends, recvs = [], []
    for k in range(len(ins)):
        for j, (cx, cy) in enumerate(_other_chips(x, y)):
            sem = (ssem.at[k, j], rsem.at[k, j], (cx, cy, c))
            if mode == "scatter":
                sends.append(_remote(ins[k].at[2 * cx + cy], outs[k].at[j], *sem))
                recvs.append(sends[-1])
            else:
                sends.append(_remote(ins[k].at[c], outs[k].at[2 * q + c], *sem))
                recvs.append(_remote(ins[k].at[c], outs[k].at[2 * (2 * cx + cy) + c], *sem))
    return sends, recvs


def _chip_wait(sends, recvs):
    for cp in sends:
        cp.wait_send()
    for cp in recvs:
        cp.wait_recv()


def _landing_shape(a, mode):
    return jax.ShapeDtypeStruct(((3,) if mode == "scatter" else (8,)) + a.shape[1:], a.dtype)


def _chip_exchange(arrs, mode, *, name):
    n = len(arrs)

    def body(*refs):
        ins, outs = refs[:n], refs[n:2 * n]
        ssem, rsem = refs[2 * n:]
        sends, recvs = _chip_copies(ins, outs, ssem, rsem, mode)
        for cp in sends:
            cp.start()
        _chip_wait(sends, recvs)

    sem = pltpu.SemaphoreType.DMA
    return list(pl.pallas_call(
        body, out_shape=tuple(_landing_shape(a, mode) for a in arrs),
        in_specs=[ANY] * n, out_specs=(ANY,) * n, scratch_shapes=[sem((n, 3)), sem((n, 3))], name=name)(*arrs))


def _pair_fill(bufs, owns, *, name):
    n = len(bufs)

    def body(*refs):
        own, outs = refs[n:2 * n], refs[2 * n:3 * n]
        ssem, rsem = refs[3 * n:]
        x, y, c = _place()
        q = 2 * x + y
        sib = (x, y, 1 - c)
        sends, recvs = [], []
        for k in range(n):
            for j, (cx, cy) in enumerate(_other_chips(x, y)):
                mine, theirs = outs[k].at[2 * (2 * cx + cy) + c], outs[k].at[2 * (2 * cx + cy) + 1 - c]
                sends.append(_remote(mine, mine, ssem.at[k, j], rsem.at[k, j], sib))
                recvs.append(_remote(mine, theirs, ssem.at[k, j], rsem.at[k, j], sib))
            slots = outs[k].at[pl.ds(2 * q, 2)]
            sends.append(_remote(own[k], slots, ssem.at[k, 3], rsem.at[k, 3], sib))
            recvs.append(sends[-1])
        for cp in sends:
            cp.start()
        _chip_wait(sends, recvs)

    sem = pltpu.SemaphoreType.DMA
    return list(pl.pallas_call(
        body, out_shape=tuple(jax.ShapeDtypeStruct(b.shape, b.dtype) for b in bufs),
        in_specs=[ANY] * (2 * n), out_specs=(ANY,) * n, scratch_shapes=[sem((n, 4)), sem((n, 4))],
        input_output_aliases={k: k for k in range(n)}, name=name)(*bufs, *owns))


def _pair_swap(arrs, *, name):
    n = len(arrs)

    def body(*refs):
        ins, outs = refs[:n], refs[n:2 * n]
        ssem, rsem = refs[2 * n:]
        x, y, c = _place()
        cps = [_remote(ins[k], outs[k], ssem.at[k], rsem.at[k], (x, y, 1 - c)) for k in range(n)]
        for cp in cps:
            cp.start()
        for cp in cps:
            cp.wait()

    sem = pltpu.SemaphoreType.DMA
    return pl.pallas_call(
        body, out_shape=tuple(jax.ShapeDtypeStruct(a.shape, a.dtype) for a in arrs),
        in_specs=[ANY] * n, out_specs=(ANY,) * n, scratch_shapes=[sem((n,)), sem((n,))], name=name)(*arrs)


def _allreduce_small(slab, *, name):
    r, n = slab.shape

    def body(x_ref, o_ref, buf, ssem, rsem):
        x, y, c = _place()
        me = 4 * x + 2 * y + c
        buf[me] = x_ref[...]
        cps = []
        for rel in range(1, 8):
            bx, by, bc = (rel >> 2) & 1, (rel >> 1) & 1, rel & 1
            px, py, pc = (x + bx) % 2, (y + by) % 2, (c + bc) % 2
            cps.append((_remote(x_ref, buf.at[me], ssem.at[rel - 1], rsem.at[rel - 1], (px, py, pc)),
                        4 * px + 2 * py + pc, (px, py, pc)))
        for cp, _, _ in cps:
            cp.start()
        for rel, (cp, peer, dev) in enumerate(cps):
            cp.wait_send()
            _remote(x_ref, buf.at[peer], ssem.at[rel], rsem.at[rel], dev).wait_recv()
        acc = buf[0]
        for k in range(1, 8):
            acc = acc + buf[k]
        o_ref[...] = acc

    vm = pl.BlockSpec(memory_space=pltpu.VMEM)
    sem = pltpu.SemaphoreType.DMA
    return pl.pallas_call(
        body, out_shape=jax.ShapeDtypeStruct((r, n), F32), in_specs=[vm], out_specs=vm,
        scratch_shapes=[pltpu.VMEM((8, r, n), F32), sem((7,)), sem((7,))], name=name)(slab)


def _slab(arrs, row_mult):
    flat = jnp.concatenate([a.reshape(-1) for a in arrs])
    unit = 128 * row_mult
    total = -(-flat.size // unit) * unit
    return jnp.pad(flat, (0, total - flat.size)).reshape(-1, 128)


def _unslab(slab, shapes):
    flat = slab.reshape(-1)
    out, off = [], 0
    for s in shapes:
        size = int(np.prod(s))
        out.append(flat[off:off + size].reshape(s))
        off += size
    return out


def _cols_from_chips(a):
    return jnp.transpose(a, (1, 0, 2)).reshape(a.shape[1], -1)


def _cols_to_chips(a, parts):
    r = a.shape[0]
    return jnp.transpose(a.reshape(r, parts, -1), (1, 0, 2))


BIG = ("w_in", "w_out", "up", "down")
GATHER_RIDES = {("proj", 0): (("w_out", 0), ("up", 0)), ("mix_out", 0): (("down", 0),),
                ("ffn_up_a", 0): (("w_in", 1), ("w_out", 1)), ("ffn_up_g", 0): (("up", 1),),
                ("ffn_down", 0): (("down", 1),)}
REDUCE_RIDES = {("ffn_down_dx", 0): (("up",), 1), ("ffn_up_a_dx", 0): (("w_in", "w_out"), 1),
                ("ffn_up_g_dx", 0): (("down",), 1),
                ("proj_dx", 0): (("up",), 0), ("proj_dw_0", 0): (("down",), 0), ("proj_dw_1", 0): (("w_out",), 0)}


class _LocalWeights:
    def __init__(self, meta, win, wout, up_a, up_g, down, w2p, cw):
        self._meta, self._w = meta, {"win": win, "wout": wout, "up_a": up_a, "up_g": up_g, "down": down, "w2p": w2p,
                                     "cw": cw}

    def meta(self):
        return self._meta

    def get(self, kind, l):
        return self._w[kind][l]

    def mm(self, site, l, a, b, fn=None, **kw):
        return (fn or _mm)(a, b, name=site, **kw)

    def grads_done(self, l, g, kinds):
        pass


class _ChipWeights:
    def __init__(self, w_in, w_out, ffn_up, ffn_down, meta_tokens, gla_gate_w2, ffn_conv_w):
        self.x, self.y, self.c = _place()
        self.q = 2 * self.x + self.y
        halves = lambda a: a.astype(BF16).reshape(2, a.shape[0] // 2, a.shape[1])
        self.own = {(k, l): halves(a[l]) for k, a in zip(BIG, (w_in, w_out, ffn_up, ffn_down)) for l in range(DEPTH)}
        self.landed, self.swapped, self.full, self.n_swaps = {}, {}, {}, 0
        self.sh_shapes = [meta_tokens.shape, gla_gate_w2.shape, ffn_conv_w.shape]
        self.own["small", 0] = _slab([meta_tokens, gla_gate_w2, ffn_conv_w], 16).reshape(2, -1, 128)
        first = [("w_in", 0), ("small", 0)]
        for key, arr in zip(first, _chip_exchange([self.own[k] for k in first], "bcast", name="gather_first")):
            self.landed[key] = arr
        sh = self._whole("small", 0).reshape(4, -1, 128)
        parts = [_unslab(sh[k], self.sh_shapes) for k in range(4)]
        self._meta = jnp.concatenate([p[0] for p in parts], axis=-1)
        self.w2 = jnp.concatenate([p[1] for p in parts], axis=-1)
        self.cw = jnp.concatenate([p[2] for p in parts], axis=-1)
        self.partial, self.slots = {}, {}

    def _whole(self, kind, l):
        if (kind, l) not in self.full:
            keys = [k for k in self.landed if k not in self.full]
            got = _pair_fill([self.landed[k] for k in keys], [self.own[k] for k in keys],
                             name=f"gather_fill_{self.n_swaps}")
            self.n_swaps += 1
            for k, buf in zip(keys, got):
                self.full[k] = buf.reshape(4, 2 * buf.shape[1], buf.shape[2])
        return self.full[kind, l]

    def meta(self):
        return self._meta

    def get(self, kind, l):
        if kind == "win":
            return _to_kernel_cols(_cols_from_chips(self._whole("w_in", l)))
        if kind == "wout":
            return self._whole("w_out", l).reshape(D_MODEL, D_MODEL)
        if kind == "up_a":
            return _cols_from_chips(self._whole("up", l)[0:2])
        if kind == "up_g":
            return _cols_from_chips(self._whole("up", l)[2:4])
        if kind == "down":
            return self._whole("down", l).reshape(D_FF, D_MODEL)
        if kind == "w2p":
            return jnp.pad(self.w2[l], ((0, 128 - GLA_RANK), (0, 0))).astype(BF16)
        return self.cw[l]

    def mm(self, site, l, a, b, fn=None, **kw):
        fn = fn or _mm
        if (site, l) in GATHER_RIDES:
            keys = GATHER_RIDES[site, l]
            out, got = fn(a, b, name=site, carry=([self.own[k] for k in keys], "bcast"), **kw)
            self.landed.update(zip(keys, got))
            return out
        if (site, l) in REDUCE_RIDES:
            kinds, gl = REDUCE_RIDES[site, l]
            keys = [(k, gl) for k in kinds]
            if all(k in self.partial and k not in self.slots for k in keys):
                out, got = fn(a, b, name=site, carry=([self.partial[k] for k in keys], "scatter"), **kw)
                self.slots.update(zip(keys, got))
                return out
        return fn(a, b, name=site, **kw)

    def grads_done(self, l, g, kinds):
        split = lambda a: a.reshape(4, 2, a.shape[-2] // 2, a.shape[-1]) if a.ndim == 3 else \
            a.reshape(4, 2, a.shape[0] // 8, a.shape[1])
        src = {"w_in": lambda: _cols_to_chips(g["w_in"][l], 4), "w_out": lambda: g["w_out"][l],
               "up": lambda: g["up"][l], "down": lambda: g["down"][l]}
        big = {k: split(src[k]()) for k in kinds}
        from_sib = _pair_exchange([big[k] for k in kinds], name=f"grads_pair_exchange_{l}_{kinds[0]}")
        for k, theirs in zip(kinds, from_sib):
            self.partial[k, l] = _pair_sum(big[k], theirs, self.c, name=f"pair_sum_{k}_{l}")

    def reduce(self):
        keys = [(k, l) for l in range(DEPTH) for k in BIG]
        late = [k for k in keys if k not in self.slots]
        self.slots.update(zip(late, _chip_exchange([self.partial[k] for k in late], "scatter",
                                                   name="grads_chip_exchange")))
        half = {}
        for k in keys:
            own = lax.dynamic_index_in_dim(self.partial[k], self.q, 0, keepdims=False)
            half[k] = _sum_slots(own, self.slots[k], name=f"chip_sum_{k[0]}_{k[1]}")
        other = dict(zip(keys, _pair_swap([half[k] for k in keys], name="grads_pair_swap")))
        whole = lambda k: jnp.where(self.c == 0, jnp.concatenate([half[k], other[k]], axis=0),
                                    jnp.concatenate([other[k], half[k]], axis=0))
        return [jnp.stack([whole((k, l)) for l in range(DEPTH)]) for k in BIG]


def _local_step(x_rows, target_rows, wts, pre_mix_norm, gla_gate_b, ret_norm_w, gla_norm_w, post_mix_norm,
                pre_ffn_norm, ffn_conv_b, post_ffn_norm):
    d = D_MODEL
    lp = x_rows.shape[0] + FRONT + BACK
    row = lambda a, l: a[l][None, :]
    rtab = _ret_tables(lp)
    gtab = _gla_tables()
    h0 = jnp.concatenate([jnp.zeros((PADF, d), F32), wts.meta(), x_rows, jnp.zeros((BACK, d), F32)], axis=0)
    target = jnp.pad(target_rows, ((FRONT, BACK), (0, 0)))

    saved = []
    h = h0
    _, hn = _resid_norm(h0, None, None, row(pre_mix_norm, 0), name="norm_in")
    loss_local = dy = None
    for l in range(DEPTH):
        s = {"h_in": h, "hn": hn}
        s["proj"] = wts.mm("proj", l, hn, wts.get("win", l))
        s["o_ret"], s["st_ret"] = _retention(s["proj"], rtab, name="retention")
        s["o_gla"], s["st_gla"] = _gla(s["proj"], wts.get("w2p", l), row(gla_gate_b, l), gtab, name="gla")
        s["merged"] = _merge(s["o_ret"], s["o_gla"], s["proj"], row(ret_norm_w, l), row(gla_norm_w, l), name="merge")
        s["m"] = wts.mm("mix_out", l, s["merged"], wts.get("wout", l))
        s["h_mid"], s["hn2"] = _resid_norm(h, s["m"], row(post_mix_norm, l), row(pre_ffn_norm, l), name="resid_mix")
        s["ua"] = wts.mm("ffn_up_a", l, s["hn2"], wts.get("up_a", l))
        s["ug"] = wts.mm("ffn_up_g", l, s["hn2"], wts.get("up_g", l))
        cw_a, cw_g = wts.get("cw", l)[:, :D_FF], wts.get("cw", l)[:, D_FF:]
        cb_a, cb_g = ffn_conv_b[l][None, :D_FF], ffn_conv_b[l][None, D_FF:]
        s["conv"] = (cw_a, cw_g, cb_a, cb_g)
        s["act"] = _conv_act(s["ua"], s["ug"], cw_a, cw_g, cb_a, cb_g, name="conv_act")
        s["f"] = wts.mm("ffn_down", l, s["act"], wts.get("down", l))
        if l + 1 < DEPTH:
            h, hn = _resid_norm(s["h_mid"], s["f"], row(post_ffn_norm, l), row(pre_mix_norm, l + 1), name="resid_ffn")
        else:
            loss_local, dy = _loss_head(s["h_mid"], s["f"], row(post_ffn_norm, l), target, name="loss_head")
        saved.append(s)

    g = {k: [None] * DEPTH for k in ("pre_mix", "w_in", "w2", "gb", "ret_n", "gla_n", "w_out", "post_mix", "pre_ffn",
                                     "up", "cw", "cb", "down", "post_ffn")}
    dh_out, dhn_next = dy, None
    for l in reversed(range(DEPTH)):
        s = saved[l]
        cw_a, cw_g, cb_a, cb_g = s["conv"]
        if l + 1 < DEPTH:
            dh, df, g["pre_mix"][l + 1], g["post_ffn"][l] = _resid_norm_bwd(
                dh_out, dhn_next, saved[l + 1]["h_in"], s["f"], row(pre_mix_norm, l + 1), row(post_ffn_norm, l),
                name="resid_ffn_bwd")
        else:
            dh, df, _, g["post_ffn"][l] = _resid_norm_bwd(dh_out, None, None, s["f"], None, row(post_ffn_norm, l),
                                                          name="loss_head_bwd")
        dact = wts.mm("ffn_down_dx", l, df, wts.get("down", l), nt=True)
        g["down"][l] = _mm_tn(s["act"], df, tn=512, name="ffn_down_dw")
        du_a, du_g, dcw_a, dcw_g, dcb_a, dcb_g = _conv_act_bwd(s["ua"], s["ug"], dact, cw_a, cw_g, cb_a, cb_g,
                                                               name="conv_act_bwd")
        g["cw"][l] = jnp.concatenate([dcw_a, dcw_g], axis=1)
        g["cb"][l] = jnp.concatenate([dcb_a, dcb_g], axis=1)[0]
        half_up = _mm_tn(s["hn2"], du_a, tn=D_FF // 2, blocks=(4, 0), name="ffn_up_a_dw")
        g["up"][l] = _mm_tn(s["hn2"], du_g, tn=D_FF // 2, blocks=(4, 2), into=half_up, name="ffn_up_g_dw")
        dhn2 = wts.mm("ffn_up_a_dx", l, du_a, wts.get("up_a", l), nt=True)
        dhn2 = wts.mm("ffn_up_g_dx", l, du_g, wts.get("up_g", l), nt=True, add=dhn2)
        dh, dm, g["pre_ffn"][l], g["post_mix"][l] = _resid_norm_bwd(
            dh, dhn2, s["h_mid"], s["m"], row(pre_ffn_norm, l), row(post_mix_norm, l), name="resid_mix_bwd")
        g["w_out"][l] = _mm_tn(s["merged"], dm, name="mix_out_dw")
        wts.grads_done(l, g, ("w_out", "up", "down"))
        dmerged = wts.mm("mix_out_dx", l, dm, wts.get("wout", l), nt=True)
        do_ret, do_gla, d_gate, g["ret_n"][l], g["gla_n"][l] = _merge_bwd(
            dmerged, s["o_ret"], s["o_gla"], s["proj"], row(ret_norm_w, l), row(gla_norm_w, l), name="merge_bwd")
        d_ret = _retention_bwd(s["proj"], do_ret, s["st_ret"], rtab, name="retention_bwd")
        d_gla, dw2, dgb = _gla_bwd(s["proj"], do_gla, s["st_gla"], wts.get("w2p", l), row(gla_gate_b, l), gtab,
                                   name="gla_bwd")
        g["w2"][l], g["gb"][l] = dw2[:GLA_RANK], dgb[0]
        pieces = (d_ret, d_gate, d_gla)
        g["w_in"][l] = _to_reference_cols(*[wts.mm(f"proj_dw_{i}", l, s["hn"], p, fn=_mm_tn)
                                            for i, p in enumerate(pieces)])
        win = wts.get("win", l)
        dhn_next = wts.mm("proj_dx", l, pieces, [win[:, 0:P_RET], win[:, P_RET:P_RET + P_GATE], win[:, P_RET + P_GATE:]],
                          fn=_mm_nt_sum)
        dh_out = dh
        wts.grads_done(l, g, ("w_in",))
    dh0, _, g["pre_mix"][0], _ = _resid_norm_bwd(dh_out, dhn_next, h0, None, row(pre_mix_norm, 0), None,
                                                 name="norm_in_bwd")
    return loss_local, dh0, g


def kernel(x, meta_tokens, pre_mix_norm, w_in, gla_gate_w2, gla_gate_b, ret_norm_w, gla_norm_w, w_out, post_mix_norm, pre_ffn_norm, ffn_up, ffn_conv_w, ffn_conv_b, ffn_down, post_ffn_norm, loss_target, m_meta_tokens, m_pre_mix_norm, m_w_in, m_gla_gate_w2, m_gla_gate_b, m_ret_norm_w, m_gla_norm_w, m_w_out, m_post_mix_norm, m_pre_ffn_norm, m_ffn_up, m_ffn_conv_w, m_ffn_conv_b, m_ffn_down, m_post_ffn_norm, v_meta_tokens, v_pre_mix_norm, v_w_in, v_gla_gate_w2, v_gla_gate_b, v_ret_norm_w, v_gla_norm_w, v_w_out, v_post_mix_norm, v_pre_ffn_norm, v_ffn_up, v_ffn_conv_w, v_ffn_conv_b, v_ffn_down, v_post_ffn_norm):
    xi, yi, ci = _place()
    chip = 2 * xi + yi
    seq = x.shape[1]
    d = D_MODEL
    wts = _ChipWeights(w_in, w_out, ffn_up, ffn_down, meta_tokens, gla_gate_w2, ffn_conv_w)
    loss_local, dh0, g = _local_step(x[0], loss_target[0], wts, pre_mix_norm, gla_gate_b, ret_norm_w, gla_norm_w,
                                     post_mix_norm, pre_ffn_norm, ffn_conv_b, post_ffn_norm)
    grad_x = dh0[FRONT:FRONT + seq][None]
    names = ("w_in", "w_out", "ffn_up", "ffn_down")
    g_w_in, g_w_out, g_ffn_up, g_ffn_down = wts.reduce()

    small_full = [dh0[PADF:FRONT], jnp.stack(g["pre_mix"])[:, 0], jnp.stack(g["w2"]), jnp.stack(g["gb"]),
                  jnp.stack(g["ret_n"])[:, 0], jnp.stack(g["gla_n"])[:, 0], jnp.stack(g["post_mix"])[:, 0],
                  jnp.stack(g["pre_ffn"])[:, 0], jnp.stack(g["cw"]), jnp.stack(g["cb"]),
                  jnp.stack(g["post_ffn"])[:, 0]]
    small_sum = _unslab(_allreduce_small(_slab(small_full, 8), name="small_allreduce"), [a.shape for a in small_full])
    (g_meta, g_pre_mix, g_w2, g_gb, g_ret_n, g_gla_n, g_post_mix, g_pre_ffn, g_cw, g_cb, g_post_ffn) = small_sum
    g_meta = lax.dynamic_slice_in_dim(g_meta, chip * 256, 256, axis=1)
    g_w2 = lax.dynamic_slice_in_dim(g_w2, chip * 64, 64, axis=2)
    g_cw = lax.dynamic_slice_in_dim(g_cw, chip * 1408, 1408, axis=2)

    grads = [g_meta, g_pre_mix, g_w_in, g_w2, g_gb, g_ret_n, g_gla_n, g_w_out, g_post_mix, g_pre_ffn, g_ffn_up,
             g_cw, g_cb, g_ffn_down, g_post_ffn]
    ws = [meta_tokens, pre_mix_norm, w_in, gla_gate_w2, gla_gate_b, ret_norm_w, gla_norm_w, w_out, post_mix_norm,
          pre_ffn_norm, ffn_up, ffn_conv_w, ffn_conv_b, ffn_down, post_ffn_norm]
    ms = [m_meta_tokens, m_pre_mix_norm, m_w_in, m_gla_gate_w2, m_gla_gate_b, m_ret_norm_w, m_gla_norm_w, m_w_out,
          m_post_mix_norm, m_pre_ffn_norm, m_ffn_up, m_ffn_conv_w, m_ffn_conv_b, m_ffn_down, m_post_ffn_norm]
    vs = [v_meta_tokens, v_pre_mix_norm, v_w_in, v_gla_gate_w2, v_gla_gate_b, v_ret_norm_w, v_gla_norm_w, v_w_out,
          v_post_mix_norm, v_pre_ffn_norm, v_ffn_up, v_ffn_conv_w, v_ffn_conv_b, v_ffn_down, v_post_ffn_norm]
    big_idx = (2, 7, 10, 13)
    deltas, new_m, new_v = [None] * 15, [None] * 15, [None] * 15
    for i, nm in zip(big_idx, names):
        deltas[i], new_m[i], new_v[i] = _adamw(ws[i], grads[i], ms[i], vs[i], name=f"adamw_{nm}")
    small_idx = [i for i in range(15) if i not in big_idx]
    shapes = [ws[i].shape for i in small_idx]
    sd, sm, sv = _adamw(_slab([ws[i] for i in small_idx], 8), _slab([grads[i] for i in small_idx], 8),
                        _slab([ms[i] for i in small_idx], 8), _slab([vs[i] for i in small_idx], 8), name="adamw_small")
    for i, a, b, c_ in zip(small_idx, _unslab(sd, shapes), _unslab(sm, shapes), _unslab(sv, shapes)):
        deltas[i], new_m[i], new_v[i] = a, b, c_

    loss = lax.psum(loss_local, ("x", "y", "c"))
    return (loss, grad_x, *grads, *deltas, *new_m, *new_v)
```

```python
import functools
import math

import numpy as np
import jax
import jax.numpy as jnp
from jax import lax
from jax.experimental import pallas as pl
from jax.experimental.pallas import tpu as pltpu

F32 = jnp.float32
BF16 = jnp.bfloat16

D_MODEL = 1024
DEPTH = 2
N_META = 16
EPS = 1e-6
RET_HEADS = 4
RET_DK = 128
GLA_HEADS = 4
GLA_DK = 64
GLA_DV = 128
GLA_QK = GLA_HEADS * GLA_DK
GLA_V = GLA_HEADS * GLA_DV
GLA_RANK = 16
GLA_TAU = 16.0
D_FF = 2816
ROPE_BASE = 10000.0
IN_WIDTH = 3600
IN_PAD = 3840
C_RQ, C_RK, C_RV, C_RG, C_GR, C_GQ, C_GK, C_GV, C_GA = 0, 512, 1024, 1536, 2048, 2560, 2816, 3072, 3584
P_RET, P_GATE, P_GLA = 1536, 1024, 1280


def _to_kernel_cols(w):
    pad = jnp.zeros(w.shape[:-1] + (IN_PAD - IN_WIDTH,), w.dtype)
    return jnp.concatenate([w[..., 0:2048], w[..., 3072:3584], w[..., 2048:3072], w[..., 3584:3600], pad], axis=-1)


def _to_reference_chips(d_ret, d_gate, d_gla):
    segs = [(d_ret, 0, 0, 1536), (d_gate, 0, 1536, 512), (d_gla, 0, 2048, 1024), (d_gate, 512, 3072, 512),
            (d_gla, 1024, 3584, GLA_RANK)]
    per = IN_WIDTH // 4
    chips = []
    for j in range(4):
        lo, hi, parts = per * j, per * (j + 1), []
        for piece, p0, r0, width in segs:
            a, b = max(lo, r0), min(hi, r0 + width)
            if a < b:
                parts.append(piece[:, p0 + a - r0:p0 + b - r0])
        chips.append(jnp.concatenate(parts, axis=1))
    return jnp.stack(chips)

FRONT = 64
BACK = 64
PADF = FRONT - N_META
RET_CHUNK = 128
GLA_CHUNK = 64
GLA_SUB = 16
GLA_SUB2 = 4
BLK = 640

ADAM_LR, ADAM_B1, ADAM_B2, ADAM_EPS, ADAM_WD, ADAM_STEP = 0.001, 0.9, 0.999, 1e-08, 0.01, 10

VMEM_LIMIT = 56 * 2 ** 20
MM_VMEM_BUDGET = 40 * 2 ** 20
MESH = pl.DeviceIdType.MESH


def _cp(*sem):
    return pltpu.CompilerParams(dimension_semantics=sem, vmem_limit_bytes=VMEM_LIMIT)


def _tile(n, cands):
    for t in cands:
        if n % t == 0:
            return t
    raise ValueError(f"no tile for {n} in {cands}")


def _row_tile(n):
    return _tile(n, (640, 512, 320, 256, 128, 64))


def _mm(a, b, *, nt=False, add=None, out_dtype=F32, tn=None, name, carry=None):
    m, k = a.shape
    n = b.shape[0] if nt else b.shape[1]
    tm = _tile(m, (640, 320, 256, 128, 64))
    if tn is None:
        step_bytes = lambda t: 2 * (tm * k * a.dtype.itemsize + t * k * b.dtype.itemsize
                                    + tm * t * (jnp.dtype(out_dtype).itemsize + (4 if add is not None else 0)))
        tn = next(t for t in range(n, 0, -128) if n % t == 0 and (step_bytes(t) <= MM_VMEM_BUDGET or t == 128))
    dn = (((1,), (1,)), ((), ())) if nt else (((1,), (0,)), ((), ()))
    nj, ni = n // tn, m // tm
    n_in = 2 + (add is not None)
    c_arrs, c_mode = carry if carry is not None else ((), None)
    nc = len(c_arrs)

    def body(*refs):
        a_ref, b_ref = refs[:2]
        c_ref = refs[2] if add is not None else None
        o_ref = refs[n_in + nc]
        if nc:
            c_ins, c_outs = refs[n_in:n_in + nc], refs[n_in + nc + 1:n_in + 2 * nc + 1]
            ssem, rsem = refs[n_in + 2 * nc + 1:]
            j, i = pl.program_id(0), pl.program_id(1)

            @pl.when((j == 0) & (i == 0))
            def _():
                for cp in _chip_copies(c_ins, c_outs, ssem, rsem, c_mode)[0]:
                    cp.start()
        r = lax.dot_general(a_ref[...].astype(BF16), b_ref[...].astype(BF16), dn, preferred_element_type=F32)
        if add is not None:
            r = r + c_ref[...]
        o_ref[...] = r.astype(o_ref.dtype)
        if nc:
            @pl.when((j == nj - 1) & (i == ni - 1))
            def _():
                _chip_wait(*_chip_copies(c_ins, c_outs, ssem, rsem, c_mode))

    b_spec = pl.BlockSpec((tn, k), lambda j, i: (j, 0)) if nt else pl.BlockSpec((k, tn), lambda j, i: (0, j))
    in_specs = [pl.BlockSpec((tm, k), lambda j, i: (i, 0)), b_spec]
    args = [a, b]
    if add is not None:
        in_specs.append(pl.BlockSpec((tm, tn), lambda j, i: (i, j)))
        args.append(add)
    out_shape = jax.ShapeDtypeStruct((m, n), out_dtype)
    out_spec = pl.BlockSpec((tm, tn), lambda j, i: (i, j))
    if not nc:
        return pl.pallas_call(
            body, out_shape=out_shape, grid=(nj, ni), in_specs=in_specs, out_specs=out_spec,
            compiler_params=_cp("parallel", "parallel"), name=name)(*args)
    sem = pltpu.SemaphoreType.DMA
    outs = pl.pallas_call(
        body, out_shape=(out_shape,) + tuple(_landing_shape(x, c_mode) for x in c_arrs), grid=(nj, ni),
        in_specs=in_specs + [ANY] * nc, out_specs=(out_spec,) + (ANY,) * nc,
        scratch_shapes=[sem((nc, 3)), sem((nc, 3))],
        compiler_params=_cp("arbitrary", "arbitrary"), name=name)(*args, *c_arrs)
    return outs[0], list(outs[1:])


def _call_with_carry(body, *, out_shape, grid, in_specs, out_specs, args, semantics, carry, name, aliases=None):
    if carry is None:
        return pl.pallas_call(body, out_shape=out_shape, grid=grid, in_specs=in_specs, out_specs=out_specs,
                              input_output_aliases=aliases or {}, compiler_params=_cp(*semantics), name=name)(*args)
    c_arrs, c_mode = carry
    n_in, nc = len(args), len(c_arrs)

    def carried(*refs):
        c_ins, c_outs = refs[n_in:n_in + nc], refs[n_in + nc + 1:n_in + 2 * nc + 1]
        ssem, rsem = refs[n_in + 2 * nc + 1:]
        ids = [pl.program_id(d) for d in range(len(grid))]
        first = functools.reduce(lambda u, v: u & v, [i == 0 for i in ids])
        last = functools.reduce(lambda u, v: u & v, [i == g - 1 for i, g in zip(ids, grid)])

        @pl.when(first)
        def _():
            for cp in _chip_copies(c_ins, c_outs, ssem, rsem, c_mode)[0]:
                cp.start()
        body(*refs[:n_in], refs[n_in + nc])

        @pl.when(last)
        def _():
            _chip_wait(*_chip_copies(c_ins, c_outs, ssem, rsem, c_mode))

    sem = pltpu.SemaphoreType.DMA
    outs = pl.pallas_call(
        carried, out_shape=(out_shape,) + tuple(_landing_shape(x, c_mode) for x in c_arrs), grid=grid,
        in_specs=list(in_specs) + [ANY] * nc, out_specs=(out_specs,) + (ANY,) * nc,
        scratch_shapes=[sem((nc, 3)), sem((nc, 3))], input_output_aliases=aliases or {},
        compiler_params=_cp(*(("arbitrary",) * len(grid))), name=name)(*args, *c_arrs)
    return outs[0], list(outs[1:])


def _mm_nt_sum(a_list, b_list, *, name, carry=None):
    m, n = a_list[0].shape[0], b_list[0].shape[0]
    tm = _tile(m, (640, 320, 256, 128, 64))
    np_ = len(a_list)

    def body(*refs):
        acc = None
        for a_ref, b_ref in zip(refs[:np_], refs[np_:2 * np_]):
            r = lax.dot_general(a_ref[...].astype(BF16), b_ref[...].astype(BF16), (((1,), (1,)), ((), ())),
                                preferred_element_type=F32)
            acc = r if acc is None else acc + r
        refs[2 * np_][...] = acc

    return _call_with_carry(
        body, out_shape=jax.ShapeDtypeStruct((m, n), F32), grid=(m // tm,),
        in_specs=[pl.BlockSpec((tm, a.shape[1]), lambda i: (i, 0)) for a in a_list]
        + [pl.BlockSpec(b.shape, lambda i: (0, 0)) for b in b_list],
        out_specs=pl.BlockSpec((tm, n), lambda i: (i, 0)), args=[*a_list, *b_list], semantics=("parallel",),
        carry=carry, name=name)


def _mm_tn(a, b, *, tn=None, blocks=None, into=None, name, carry=None):
    m, k = a.shape
    n = b.shape[1]
    tm = _tile(m, (1664, 640, 320, 256, 128, 64))
    tn = n if tn is None else tn
    if blocks is not None:
        total, first = blocks
        out_shape = jax.ShapeDtypeStruct((total, k, tn), F32)
        out_spec = pl.BlockSpec((None, k, tn), lambda j, i: (first + j, 0, 0))
    else:
        out_shape = jax.ShapeDtypeStruct((k, n), F32)
        out_spec = pl.BlockSpec((k, tn), lambda j, i: (0, j))

    def body(a_ref, b_ref, *rest):
        o_ref = rest[-1]

        @pl.when(pl.program_id(1) == 0)
        def _():
            o_ref[...] = jnp.zeros_like(o_ref)
        o_ref[...] += lax.dot_general(a_ref[...].astype(BF16), b_ref[...].astype(BF16),
                                      (((0,), (0,)), ((), ())), preferred_element_type=F32)

    in_specs = [pl.BlockSpec((tm, k), lambda j, i: (i, 0)), pl.BlockSpec((tm, tn), lambda j, i: (i, j))]
    args, alias = [a, b], {}
    if into is not None:
        in_specs.append(pl.BlockSpec(memory_space=pl.ANY))
        args.append(into)
        alias = {2: 0}
    return _call_with_carry(body, out_shape=out_shape, grid=(n // tn, m // tm), in_specs=in_specs, out_specs=out_spec,
                            args=args, semantics=("parallel", "arbitrary"), carry=carry, name=name, aliases=alias)


def _rms(x, w):
    r = lax.rsqrt(jnp.mean(x * x, axis=-1, keepdims=True) + EPS)
    return x * r * w


def _rms_bwd(x, w, dy):
    r = lax.rsqrt(jnp.mean(x * x, axis=-1, keepdims=True) + EPS)
    xh = x * r
    dxh = dy * w
    dx = r * (dxh - xh * jnp.mean(dxh * xh, axis=-1, keepdims=True))
    return dx, jnp.sum(dy * xh, axis=0, keepdims=True)


def _resid_norm(h, t, w_post, w_next, *, name):
    lp, d = h.shape
    tm = _row_tile(lp)
    has_t = t is not None

    def body(*refs):
        if has_t:
            h_ref, t_ref, wp_ref, wn_ref, ho_ref, hn_ref = refs
            hv = h_ref[...] + _rms(t_ref[...], wp_ref[...])
            ho_ref[...] = hv
        else:
            h_ref, wn_ref, hn_ref = refs
            hv = h_ref[...]
        hn_ref[...] = _rms(hv, wn_ref[...]).astype(BF16)

    row = pl.BlockSpec((tm, d), lambda i: (i, 0))
    vec = pl.BlockSpec((1, d), lambda i: (0, 0))
    if has_t:
        return pl.pallas_call(
            body, out_shape=(jax.ShapeDtypeStruct((lp, d), F32), jax.ShapeDtypeStruct((lp, d), BF16)),
            grid=(lp // tm,), in_specs=[row, row, vec, vec], out_specs=(row, row),
            compiler_params=_cp("parallel"), name=name)(h, t, w_post, w_next)
    return h, pl.pallas_call(
        body, out_shape=jax.ShapeDtypeStruct((lp, d), BF16), grid=(lp // tm,), in_specs=[row, vec],
        out_specs=row, compiler_params=_cp("parallel"), name=name)(h, w_next)


def _resid_norm_bwd(dh_out, dhn, h_new, t, w_next, w_post, *, name):
    lp, d = h_new.shape if h_new is not None else t.shape
    tm = _row_tile(lp)
    has_n = dhn is not None
    has_t = t is not None

    def body(*refs):
        refs = list(refs)
        dho_ref = refs.pop(0)
        if has_n:
            dhn_ref, hn_ref, wn_ref = refs.pop(0), refs.pop(0), refs.pop(0)
        if has_t:
            t_ref, wp_ref = refs.pop(0), refs.pop(0)
        dh_ref = refs.pop(0) if has_n else None
        dt_ref = refs.pop(0) if has_t else None
        dwn_ref = refs.pop(0) if has_n else None
        dwp_ref = refs.pop(0) if has_t else None
        first = pl.program_id(0) == 0
        dh = dho_ref[...]
        if has_n:
            dx, dwn = _rms_bwd(hn_ref[...], wn_ref[...], dhn_ref[...])
            dh = dh + dx
            dh_ref[...] = dh

            @pl.when(first)
            def _():
                dwn_ref[...] = jnp.zeros_like(dwn_ref)
            dwn_ref[...] += dwn
        if has_t:
            dt, dwp = _rms_bwd(t_ref[...], wp_ref[...], dh)
            dt_ref[...] = dt.astype(BF16)

            @pl.when(first)
            def _():
                dwp_ref[...] = jnp.zeros_like(dwp_ref)
            dwp_ref[...] += dwp

    row = pl.BlockSpec((tm, d), lambda i: (i, 0))
    vec = pl.BlockSpec((1, d), lambda i: (0, 0))
    args, in_specs, out_shape, out_specs = [dh_out], [row], [], []
    if has_n:
        args += [dhn, h_new, w_next]
        in_specs += [row, row, vec]
    if has_t:
        args += [t, w_post]
        in_specs += [row, vec]
    if has_n:
        out_shape.append(jax.ShapeDtypeStruct((lp, d), F32)); out_specs.append(row)
    if has_t:
        out_shape.append(jax.ShapeDtypeStruct((lp, d), BF16)); out_specs.append(row)
    if has_n:
        out_shape.append(jax.ShapeDtypeStruct((1, d), F32)); out_specs.append(vec)
    if has_t:
        out_shape.append(jax.ShapeDtypeStruct((1, d), F32)); out_specs.append(vec)
    outs = list(pl.pallas_call(body, out_shape=tuple(out_shape), grid=(lp // tm,), in_specs=in_specs,
                               out_specs=tuple(out_specs), compiler_params=_cp("arbitrary"), name=name)(*args))
    dh = outs.pop(0) if has_n else dh_out
    dt = outs.pop(0) if has_t else None
    dwn = outs.pop(0) if has_n else None
    dwp = outs.pop(0) if has_t else None
    return dh, dt, dwn, dwp


def _loss_head(h, f, w_post, target, *, name):
    lp, d = h.shape
    tm = _row_tile(lp)

    def body(h_ref, f_ref, w_ref, t_ref, loss_ref, dy_ref, df_ref, dw_ref):
        i = pl.program_id(0)
        f, w = f_ref[...], w_ref[...]
        y = h_ref[...] + _rms(f, w)
        rows = i * tm + lax.broadcasted_iota(jnp.int32, (tm, 1), 0)
        tok = (rows >= FRONT) & (rows < lp - BACK)
        err = jnp.where(tok, y - t_ref[...], 0.0)
        dy = err * (1.0 / d)
        dy_ref[...] = dy
        df, dw = _rms_bwd(f, w, dy)
        df_ref[...] = df.astype(BF16)

        @pl.when(i == 0)
        def _():
            loss_ref[...] = jnp.zeros_like(loss_ref)
            dw_ref[...] = jnp.zeros_like(dw_ref)
        part = jnp.sum(jnp.sum(err * err, axis=1, keepdims=True), axis=0, keepdims=True) * (0.5 / d)
        loss_ref[...] += jnp.broadcast_to(part, loss_ref.shape)
        dw_ref[...] += dw

    row = pl.BlockSpec((tm, d), lambda i: (i, 0))
    vec = pl.BlockSpec((1, d), lambda i: (0, 0))
    loss, dy, df, dw = pl.pallas_call(
        body, out_shape=(jax.ShapeDtypeStruct((8, 128), F32), jax.ShapeDtypeStruct((lp, d), F32),
                         jax.ShapeDtypeStruct((lp, d), BF16), jax.ShapeDtypeStruct((1, d), F32)),
        grid=(lp // tm,), in_specs=[row, row, vec, row],
        out_specs=(pl.BlockSpec((8, 128), lambda i: (0, 0)), row, row, vec),
        compiler_params=_cp("arbitrary"), name=name)(h, f, w_post, target)
    return loss[0, 0], dy, df, dw


_GELU_C = math.sqrt(2.0 / math.pi)


def _gelu_and_grad(a):
    a2 = a * a
    t = jnp.tanh(a * (_GELU_C + (_GELU_C * 0.044715) * a2))
    ha = 0.5 * a
    h1 = 0.5 + 0.5 * t
    return a * h1, h1 + ha * (1.0 - t * t) * (_GELU_C + (3.0 * _GELU_C * 0.044715) * a2)


def _gelu(a):
    t = jnp.tanh(a * (_GELU_C + (_GELU_C * 0.044715) * (a * a)))
    return a * (0.5 + 0.5 * t)


def _conv3(parts, n, w, b):
    xx = jnp.concatenate(parts, axis=0)
    return b + xx[8:8 + n] * w[2:3] + pltpu.roll(xx, 1, 0)[8:8 + n] * w[1:2] + pltpu.roll(xx, 2, 0)[8:8 + n] * w[0:1]


def _conv_act(ua, ug, wa, wg, ba, bg, *, name):
    lp, n = ua.shape
    tm = _row_tile(lp)
    tc = _tile(n, (256, 128))
    nb8 = tm // 8

    def body(ua_ref, uap_ref, ug_ref, ugp_ref, wa_ref, wg_ref, ba_ref, bg_ref, o_ref):
        i = pl.program_id(0)
        ca = _conv3([uap_ref[...], ua_ref[...]], tm, wa_ref[...], ba_ref[...])
        cg = _conv3([ugp_ref[...], ug_ref[...]], tm, wg_ref[...], bg_ref[...])
        rows = i * tm + lax.broadcasted_iota(jnp.int32, (tm, 1), 0)
        ok = (rows >= PADF) & (rows < lp - BACK)
        o_ref[...] = jnp.where(ok, _gelu(ca) * cg, 0.0).astype(BF16)

    cur = pl.BlockSpec((tm, tc), lambda i, j: (i, j))
    prev = pl.BlockSpec((8, tc), lambda i, j: (jnp.maximum(i * nb8 - 1, 0), j))
    w3 = pl.BlockSpec((3, tc), lambda i, j: (0, j))
    b1 = pl.BlockSpec((1, tc), lambda i, j: (0, j))
    return pl.pallas_call(
        body, out_shape=jax.ShapeDtypeStruct((lp, n), BF16), grid=(lp // tm, n // tc),
        in_specs=[cur, prev, cur, prev, w3, w3, b1, b1], out_specs=cur,
        compiler_params=_cp("parallel", "parallel"), name=name)(ua, ua, ug, ug, wa, wg, ba, bg)


def _conv_act_bwd(ua, ug, dact, wa, wg, ba, bg, *, name):
    lp, n = ua.shape
    tm = _row_tile(lp)
    tc = _tile(n, (256, 128))
    nb8 = tm // 8
    last8 = lp // 8 - 1
    ext = tm + 8

    def body(ua_ref, uap_ref, uan_ref, ug_ref, ugp_ref, ugn_ref, da_ref, dan_ref, wa_ref, wg_ref, ba_ref, bg_ref,
             dua_ref, dug_ref, dwa_ref, dwg_ref, dba_ref, dbg_ref):
        i = pl.program_id(1)
        wa, wg = wa_ref[...], wg_ref[...]

        def conv(parts, w, b):
            xx = jnp.concatenate(parts, axis=0)
            x, x1, x2 = xx[8:8 + ext], pltpu.roll(xx, 1, 0)[8:8 + ext], pltpu.roll(xx, 2, 0)[8:8 + ext]
            return b + x * w[2:3] + x1 * w[1:2] + x2 * w[0:1], x, x1, x2

        ca, xa, xa1, xa2 = conv([uap_ref[...], ua_ref[...], uan_ref[...]], wa, ba_ref[...])
        cg, xg, xg1, xg2 = conv([ugp_ref[...], ug_ref[...], ugn_ref[...]], wg, bg_ref[...])
        rows = i * tm + lax.broadcasted_iota(jnp.int32, (ext, 1), 0)
        ok = (rows >= PADF) & (rows < lp - BACK)
        dact_e = jnp.where(ok, jnp.concatenate([da_ref[...], dan_ref[...]], axis=0), 0.0)
        gel, gel_d = _gelu_and_grad(ca)
        dca = dact_e * cg * gel_d
        dcg = dact_e * gel

        def back(dc, w):
            return (dc[:tm] * w[2:3] + pltpu.roll(dc, ext - 1, 0)[:tm] * w[1:2]
                    + pltpu.roll(dc, ext - 2, 0)[:tm] * w[0:1])

        dua_ref[...] = back(dca, wa).astype(BF16)
        dug_ref[...] = back(dcg, wg).astype(BF16)

        @pl.when(i == 0)
        def _():
            dwa_ref[...] = jnp.zeros_like(dwa_ref)
            dwg_ref[...] = jnp.zeros_like(dwg_ref)
            dba_ref[...] = jnp.zeros_like(dba_ref)
            dbg_ref[...] = jnp.zeros_like(dbg_ref)

        def wsum(dw_ref, db_ref, dc, x, x1, x2):
            d = dc[:tm]
            s = lambda v: jnp.sum(v, axis=0, keepdims=True)
            dw_ref[0:1, :] += s(d * x2[:tm])
            dw_ref[1:2, :] += s(d * x1[:tm])
            dw_ref[2:3, :] += s(d * x[:tm])
            db_ref[...] += s(d)

        wsum(dwa_ref, dba_ref, dca, xa, xa1, xa2)
        wsum(dwg_ref, dbg_ref, dcg, xg, xg1, xg2)

    cur = pl.BlockSpec((tm, tc), lambda j, i: (i, j))
    prev = pl.BlockSpec((8, tc), lambda j, i: (jnp.maximum(i * nb8 - 1, 0), j))
    nxt = pl.BlockSpec((8, tc), lambda j, i: (jnp.minimum((i + 1) * nb8, last8), j))
    w3 = pl.BlockSpec((3, tc), lambda j, i: (0, j))
    b1 = pl.BlockSpec((1, tc), lambda j, i: (0, j))
    return pl.pallas_call(
        body,
        out_shape=(jax.ShapeDtypeStruct((lp, n), BF16), jax.ShapeDtypeStruct((lp, n), BF16),
                   jax.ShapeDtypeStruct((3, n), F32), jax.ShapeDtypeStruct((3, n), F32),
                   jax.ShapeDtypeStruct((1, n), F32), jax.ShapeDtypeStruct((1, n), F32)),
        grid=(n // tc, lp // tm),
        in_specs=[cur, prev, nxt, cur, prev, nxt, cur, nxt, w3, w3, b1, b1],
        out_specs=(cur, cur, w3, w3, b1, b1),
        compiler_params=_cp("parallel", "arbitrary"), name=name)(ua, ua, ua, ug, ug, ug, dact, dact, wa, wg, ba, bg)


def _sigmoid(x):
    return 1.0 / (1.0 + jnp.exp(-x))


def _merge(o_ret, o_gla, proj, w_ret, w_gla, *, name):
    lp = o_ret.shape[0]
    tm = _row_tile(lp)

    def body(or_ref, og_ref, rg_ref, gr_ref, wr_ref, wg_ref, m_ref):
        oret, ogla = or_ref[...], og_ref[...]
        yr, yg = [], []
        for h in range(4):
            hs = slice(128 * h, 128 * h + 128)
            o = oret[:, hs]
            xc = o - jnp.mean(o, axis=-1, keepdims=True)
            yr.append(xc * lax.rsqrt(jnp.mean(xc * xc, axis=-1, keepdims=True) + EPS))
            o = ogla[:, hs]
            yg.append(o * lax.rsqrt(jnp.mean(o * o, axis=-1, keepdims=True) + EPS))
        rg, gr = rg_ref[...], gr_ref[...]
        m_ref[:, 0:512] = (jnp.concatenate(yr, axis=1) * wr_ref[...] * (rg * _sigmoid(rg))).astype(BF16)
        m_ref[:, 512:1024] = (jnp.concatenate(yg, axis=1) * wg_ref[...] * (gr * _sigmoid(gr))).astype(BF16)

    row = pl.BlockSpec((tm, 512), lambda i: (i, 0))
    vec = pl.BlockSpec((1, 512), lambda i: (0, 0))
    return pl.pallas_call(
        body, out_shape=jax.ShapeDtypeStruct((lp, 1024), BF16), grid=(lp // tm,),
        in_specs=[row, row, pl.BlockSpec((tm, 512), lambda i: (i, C_RG // 512)),
                  pl.BlockSpec((tm, 512), lambda i: (i, C_GR // 512)), vec, vec],
        out_specs=pl.BlockSpec((tm, 1024), lambda i: (i, 0)),
        compiler_params=_cp("parallel"), name=name)(o_ret, o_gla, proj, proj, w_ret, w_gla)


def _merge_bwd(dm, o_ret, o_gla, proj, w_ret, w_gla, *, name):
    lp = o_ret.shape[0]
    tm = _row_tile(lp)

    def body(dm_ref, or_ref, og_ref, rg_ref, gr_ref, wr_ref, wg_ref, dor_ref, dog_ref, dgate_ref, dwr_ref, dwg_ref):
        @pl.when(pl.program_id(0) == 0)
        def _():
            dwr_ref[...] = jnp.zeros_like(dwr_ref)
            dwg_ref[...] = jnp.zeros_like(dwg_ref)

        def group(d, o_all, gate, w, center):
            sg = _sigmoid(gate)
            s = gate * sg
            ds = sg * (1.0 + gate * (1.0 - sg))
            xh, rr = [], []
            for h in range(4):
                o = o_all[:, 128 * h:128 * h + 128]
                if center:
                    o = o - jnp.mean(o, axis=-1, keepdims=True)
                r = lax.rsqrt(jnp.mean(o * o, axis=-1, keepdims=True) + EPS)
                xh.append(o * r)
                rr.append(r)
            xh_all = jnp.concatenate(xh, axis=1)
            dgate = d * xh_all * w * ds
            dw = jnp.sum(d * xh_all * s, axis=0, keepdims=True)
            dxh_all = d * w * s
            do = []
            for h in range(4):
                dxh = dxh_all[:, 128 * h:128 * h + 128]
                t = dxh - xh[h] * jnp.mean(dxh * xh[h], axis=-1, keepdims=True)
                if center:
                    t = t - jnp.mean(dxh, axis=-1, keepdims=True)
                do.append(rr[h] * t)
            return jnp.concatenate(do, axis=1), dgate, dw

        dmv = dm_ref[...]
        do, dg, dw = group(dmv[:, 0:512], or_ref[...], rg_ref[...], wr_ref[...], True)
        dor_ref[...] = do
        dgate_ref[:, 0:512] = dg.astype(BF16)
        dwr_ref[...] += dw
        do, dg, dw = group(dmv[:, 512:1024], og_ref[...], gr_ref[...], wg_ref[...], False)
        dog_ref[...] = do
        dgate_ref[:, 512:1024] = dg.astype(BF16)
        dwg_ref[...] += dw

    row = pl.BlockSpec((tm, 512), lambda i: (i, 0))
    vec = pl.BlockSpec((1, 512), lambda i: (0, 0))
    return pl.pallas_call(
        body,
        out_shape=(jax.ShapeDtypeStruct((lp, 512), F32), jax.ShapeDtypeStruct((lp, 512), F32),
                   jax.ShapeDtypeStruct((lp, P_GATE), BF16),
                   jax.ShapeDtypeStruct((1, 512), F32), jax.ShapeDtypeStruct((1, 512), F32)),
        grid=(lp // tm,),
        in_specs=[pl.BlockSpec((tm, 1024), lambda i: (i, 0)), row, row,
                  pl.BlockSpec((tm, 512), lambda i: (i, C_RG // 512)),
                  pl.BlockSpec((tm, 512), lambda i: (i, C_GR // 512)), vec, vec],
        out_specs=(row, row, pl.BlockSpec((tm, P_GATE), lambda i: (i, 0)), vec, vec),
        compiler_params=_cp("arbitrary"), name=name)(dm, o_ret, o_gla, proj, proj, w_ret, w_gla)


def _dot(a, b):
    return lax.dot_general(a, b, (((1,), (0,)), ((), ())), preferred_element_type=F32)


def _dot_nt(a, b):
    return lax.dot_general(a, b, (((1,), (1,)), ((), ())), preferred_element_type=F32)


def _dot_tn(a, b):
    return lax.dot_general(a, b, (((0,), (0,)), ((), ())), preferred_element_type=F32)


def _ret_tables(lp):
    cr = RET_CHUNK
    pos = np.arange(lp, dtype=np.float32) - np.float32(PADF)
    half = RET_DK // 2
    inv = (np.float32(ROPE_BASE) ** (-np.arange(half, dtype=np.float32) / np.float32(half))).astype(np.float32)
    ang = (pos[:, None] * inv[None, :]).astype(np.float32)
    c, s = np.cos(ang).astype(np.float32), np.sin(ang).astype(np.float32)
    rope_c = jnp.asarray(np.concatenate([c, c], axis=1))
    rope_s = jnp.asarray(np.concatenate([-s, s], axis=1))
    log_g = np.log(1.0 - 2.0 ** (-5.0 - np.arange(RET_HEADS, dtype=np.float64)))
    idx = np.arange(cr, dtype=np.float64)
    diff = idx[:, None] - idx[None, :]
    dmat = np.where(diff >= 0, np.exp(log_g[:, None, None] * np.maximum(diff, 0.0)), 0.0)
    zeta = np.exp(log_g[:, None] * (cr - 1.0 - idx)[None, :])
    xi = np.exp(log_g[:, None] * (idx + 1.0)[None, :])
    gc = np.exp(log_g * cr)
    f = lambda a: jnp.asarray(a.astype(np.float32))
    return (rope_c, rope_s, f(dmat), f(np.broadcast_to(zeta[:, :, None], (RET_HEADS, cr, 128))),
            f(np.broadcast_to(xi[:, :, None], (RET_HEADS, cr, 128))),
            f(np.broadcast_to(gc[:, None, None], (RET_HEADS, 8, 128))))


def _rope(t, c, s):
    return t * c + pltpu.roll(t, 64, 1) * s


def _rope_t(d, c, s):
    return d * c + pltpu.roll(d * s, 64, 1)


def _ret_specs(nblk, rev):
    ix = (lambda i: nblk - 1 - i) if rev else (lambda i: i)
    cr = RET_CHUNK
    col = lambda base: pl.BlockSpec((BLK, 512), lambda i: (ix(i), base // 512))
    tab = pl.BlockSpec((BLK, 128), lambda i: (ix(i), 0))
    sq = pl.BlockSpec((RET_HEADS, cr, cr), lambda i: (0, 0, 0))
    hv = pl.BlockSpec((RET_HEADS, cr, 128), lambda i: (0, 0, 0))
    g8 = pl.BlockSpec((RET_HEADS, 8, 128), lambda i: (0, 0, 0))
    st = pl.BlockSpec((RET_HEADS, BLK // cr, 128, 128), lambda i: (0, ix(i), 0, 0))
    out = pl.BlockSpec((BLK, 512), lambda i: (ix(i), 0))
    return col, tab, sq, hv, g8, st, out


def _retention(proj, tables, *, name):
    lp = proj.shape[0]
    nblk, cr = lp // BLK, RET_CHUNK
    scale = RET_DK ** -0.5

    def body(q_ref, k_ref, v_ref, c_ref, s_ref, d_ref, z_ref, x_ref, g_ref, o_ref, st_ref, state):
        @pl.when(pl.program_id(0) == 0)
        def _():
            state[...] = jnp.zeros_like(state)

        def chunk(ci, carry):
            sl = pl.ds(pl.multiple_of(ci * cr, cr), cr)
            c, s = c_ref[sl, :], s_ref[sl, :]
            for h in range(RET_HEADS):
                hs = slice(128 * h, 128 * h + 128)
                q = _rope(q_ref[sl, hs], c, s)
                k = _rope(k_ref[sl, hs], c, s) * scale
                qb, kb, vb = q.astype(BF16), k.astype(BF16), v_ref[sl, hs].astype(BF16)
                st = state[h]
                st_ref[h, ci] = st
                sc = _dot_nt(qb, kb) * d_ref[h]
                o_ref[sl, hs] = _dot(sc.astype(BF16), vb) + _dot(qb, st.astype(BF16)) * x_ref[h]
                state[h] = st * g_ref[h][0:1, :] + _dot_tn((k * z_ref[h]).astype(BF16), vb)
            return carry

        lax.fori_loop(0, BLK // cr, chunk, 0)

    col, tab, sq, hv, g8, st, out = _ret_specs(nblk, False)
    return pl.pallas_call(
        body,
        out_shape=(jax.ShapeDtypeStruct((lp, 512), F32), jax.ShapeDtypeStruct((4, lp // cr, 128, 128), F32)),
        grid=(nblk,), in_specs=[col(C_RQ), col(C_RK), col(C_RV), tab, tab, sq, hv, hv, g8],
        out_specs=(out, st), scratch_shapes=[pltpu.VMEM((RET_HEADS, 128, 128), F32)],
        compiler_params=_cp("arbitrary"), name=name)(proj, proj, proj, *tables)


def _retention_bwd(proj, do, states, tables, *, name):
    lp = proj.shape[0]
    nblk, cr = lp // BLK, RET_CHUNK
    nch = BLK // cr
    scale = RET_DK ** -0.5

    def body(q_ref, k_ref, v_ref, do_ref, st_ref, c_ref, s_ref, d_ref, z_ref, x_ref, g_ref, dqkv_ref, dstate):
        @pl.when(pl.program_id(0) == 0)
        def _():
            dstate[...] = jnp.zeros_like(dstate)

        def chunk(cc, carry):
            ci = nch - 1 - cc
            sl = pl.ds(pl.multiple_of(ci * cr, cr), cr)
            c, s = c_ref[sl, :], s_ref[sl, :]
            for h in range(RET_HEADS):
                hs = slice(128 * h, 128 * h + 128)
                dmat, zeta, xi = d_ref[h], z_ref[h], x_ref[h]
                q = _rope(q_ref[sl, hs], c, s)
                k = _rope(k_ref[sl, hs], c, s) * scale
                qb, kb, vb = q.astype(BF16), k.astype(BF16), v_ref[sl, hs].astype(BF16)
                kzb = (k * zeta).astype(BF16)
                dov = do_ref[sl, hs]
                dob, doxb = dov.astype(BF16), (dov * xi).astype(BF16)
                stb = st_ref[h, ci].astype(BF16)
                dsn = dstate[h]
                dsnb = dsn.astype(BF16)
                scb = (_dot_nt(qb, kb) * dmat).astype(BF16)
                dscb = (_dot_nt(dob, vb) * dmat).astype(BF16)
                dq = _dot(dscb, kb) + _dot_nt(doxb, stb)
                dk = _dot_tn(dscb, qb) + _dot_nt(vb, dsnb) * zeta
                dv = _dot_tn(scb, dob) + _dot(kzb, dsnb)
                dstate[h] = dsn * g_ref[h][0:1, :] + _dot_tn(qb, doxb)
                dqkv_ref[sl, 128 * h:128 * h + 128] = _rope_t(dq, c, s).astype(BF16)
                dqkv_ref[sl, 512 + 128 * h:640 + 128 * h] = _rope_t(dk * scale, c, s).astype(BF16)
                dqkv_ref[sl, 1024 + 128 * h:1152 + 128 * h] = dv.astype(BF16)
            return carry

        lax.fori_loop(0, nch, chunk, 0)

    col, tab, sq, hv, g8, st, out = _ret_specs(nblk, True)
    return pl.pallas_call(
        body, out_shape=jax.ShapeDtypeStruct((lp, P_RET), BF16), grid=(nblk,),
        in_specs=[col(C_RQ), col(C_RK), col(C_RV), out, st, tab, tab, sq, hv, hv, g8],
        out_specs=pl.BlockSpec((BLK, P_RET), lambda i: (nblk - 1 - i, 0)),
        scratch_shapes=[pltpu.VMEM((RET_HEADS, 128, 128), F32)],
        compiler_params=_cp("arbitrary"), name=name)(proj, proj, proj, do, states, *tables)


def _gla_tables():
    c = GLA_CHUNK
    tri = np.tril(np.ones((c, c), np.float32))
    ones_qv = np.kron(np.eye(GLA_HEADS, dtype=np.float32), np.ones((GLA_DK, GLA_DV), np.float32))
    return (jnp.asarray(tri, BF16), jnp.asarray(tri.T.copy(), BF16), jnp.asarray(ones_qv, BF16),
            jnp.asarray(ones_qv.T.copy(), BF16))


def _split3(x):
    hi = x.astype(BF16)
    r1 = x - hi.astype(F32)
    mid = r1.astype(BF16)
    lo = (r1 - mid.astype(F32)).astype(BF16)
    return hi, mid, lo


def _tri_sum(tri, x):
    hi, mid, lo = _split3(x)
    return _dot(tri, hi) + _dot(tri, mid) + _dot(tri, lo)


def _head_masks(width, per):
    lane = lax.broadcasted_iota(jnp.int32, (1, width), 1)
    return [((lane >= per * h) & (lane < per * (h + 1))).astype(F32) for h in range(GLA_HEADS)]


def _stack_heads(x, masks):
    return jnp.concatenate([x * m for m in masks], axis=0)


def _gla_gate(ga, w2, b, ok, tri):
    z = _dot(ga.astype(BF16), w2) + b
    la = (jnp.minimum(z, 0.0) - jnp.log(1.0 + jnp.exp(-jnp.abs(z)))) * (1.0 / GLA_TAU)
    la = jnp.where(ok, la, 0.0)
    return z, _tri_sum(tri, la)


def _gla_rows(i_blk, ci, lp):
    c = GLA_CHUNK
    rows = i_blk * BLK + ci * c + lax.broadcasted_iota(jnp.int32, (c, 1), 0)
    return (rows >= PADF) & (rows < lp - BACK)


N_SUB = GLA_CHUNK // GLA_SUB - 1
N_SUB2 = GLA_SUB // GLA_SUB2 - 1


def _gla_masks():
    c, s1, s2 = GLA_CHUNK, GLA_SUB, GLA_SUB2
    sh1, sh2 = s1.bit_length() - 1, s2.bit_length() - 1
    r = lax.broadcasted_iota(jnp.int32, (c, GLA_QK), 0)
    blk, within = jnp.right_shift(r, sh1), jnp.bitwise_and(r, s1 - 1)
    grp = jnp.right_shift(within, sh2)
    rowm = [(blk == a).astype(F32) for a in range(1, N_SUB + 1)] + [(grp == b).astype(F32) for b in range(1, N_SUB2 + 1)]
    keym = ([(r < s1 * a).astype(F32) for a in range(1, N_SUB + 1)]
            + [(within < s2 * b).astype(F32) for b in range(1, N_SUB2 + 1)])
    rs = lax.broadcasted_iota(jnp.int32, (GLA_HEADS * c, c), 0)
    ts = lax.broadcasted_iota(jnp.int32, (GLA_HEADS * c, c), 1)
    same = (jnp.right_shift(jnp.bitwise_and(rs, c - 1), sh1) == jnp.right_shift(ts, sh1)).astype(F32)
    lag = [(jnp.bitwise_and(r, s2 - 1) >= j).astype(F32) for j in range(s2)]
    return rowm, keym, same, lag


def _gla_hats(qs, k, g, masks, hm_q):
    c, s1, s2 = GLA_CHUNK, GLA_SUB, GLA_SUB2
    rowm, keym, same, _ = masks
    refs = [g[s1 * a - 1:s1 * a, :] for a in range(1, N_SUB + 1)]
    for b in range(1, N_SUB2 + 1):
        refs.append(jnp.concatenate([jnp.broadcast_to(g[s1 * i + s2 * b - 1:s1 * i + s2 * b, :], (s1, GLA_QK))
                                     for i in range(c // s1)], axis=0))
    eqs = [jnp.exp(jnp.minimum(g - r, 0.0)) * m for r, m in zip(refs, rowm)]
    eks = [jnp.exp(jnp.minimum(r - g, 0.0)) * m for r, m in zip(refs, keym)]
    qhs, khs = [qs * e for e in eqs], [k * e for e in eks]
    qst = [_stack_heads(q, hm_q).astype(BF16) for q in qhs]
    khb = [x.astype(BF16) for x in khs]
    qa, qb = jnp.concatenate(qst[:N_SUB], axis=1), jnp.concatenate(qst[N_SUB:], axis=1)
    ka, kb = jnp.concatenate(khb[:N_SUB], axis=1), jnp.concatenate(khb[N_SUB:], axis=1)
    p = _dot_nt(qa, ka) + _dot_nt(qb, kb) * same
    return eqs, eks, qhs, khs, qa, qb, ka, kb, p


def _roll_rows(x, j):
    return x if j == 0 else pltpu.roll(x, j, 0)


def _gla(proj, w2p, b, tables, *, name):
    lp = proj.shape[0]
    nblk, c, s2 = lp // BLK, GLA_CHUNK, GLA_SUB2
    nch = BLK // c

    def body(q_ref, k_ref, v_ref, a_ref, w_ref, b_ref, tri_ref, ones_ref, o_ref, st_ref, state):
        i_blk = pl.program_id(0)

        @pl.when(i_blk == 0)
        def _():
            state[...] = jnp.zeros_like(state)
        hm_q = _head_masks(GLA_QK, GLA_DK)
        masks = _gla_masks()
        tri, ones_qv, w2, bias = tri_ref[...], ones_ref[...], w_ref[...], b_ref[...]

        def chunk(ci, carry):
            sl = pl.ds(pl.multiple_of(ci * c, c), c)
            ok = _gla_rows(i_blk, ci, lp)
            k, v = k_ref[sl, :], v_ref[sl, :]
            vb = v.astype(BF16)
            qs = q_ref[sl, :] * (GLA_DK ** -0.5)
            _, g = _gla_gate(a_ref[sl, :], w2, bias, ok, tri)
            last = g[c - 1:c, :]
            st = state[...]
            st_ref[ci] = st
            qst = _stack_heads(qs * jnp.exp(g), hm_q).astype(BF16)
            oi = _dot_nt(qst, st.astype(BF16))
            o = jnp.concatenate([oi[c * h:c * h + c, :] for h in range(GLA_HEADS)], axis=1)
            ke = k * jnp.exp(last - g)
            f = _dot_tn(vb, ke.astype(BF16))
            upd = f[0:GLA_DV, :] * hm_q[0]
            for h in range(1, GLA_HEADS):
                upd = upd + f[GLA_DV * h:GLA_DV * (h + 1), :] * hm_q[h]
            state[...] = st * jnp.exp(last) + upd
            p = _gla_hats(qs, k, g, masks, hm_q)[-1]
            ob = _dot(p.astype(BF16), vb)
            o = o + jnp.concatenate([ob[c * h:c * h + c, GLA_DV * h:GLA_DV * (h + 1)] for h in range(GLA_HEADS)],
                                    axis=1)
            ws = []
            for j in range(s2):
                ej = jnp.exp(jnp.minimum(g - _roll_rows(g, j), 0.0))
                ws.append((qs * _roll_rows(k, j) * ej * masks[3][j]).astype(BF16))
            ball = _dot(jnp.concatenate(ws, axis=0), ones_qv)
            for j in range(s2):
                o = o + ball[c * j:c * j + c, :] * _roll_rows(v, j)
            o_ref[sl, :] = o
            return carry

        lax.fori_loop(0, nch, chunk, 0)

    tri, _, ones_qv, _ = tables
    full = lambda arr: pl.BlockSpec(arr.shape, lambda i: (0,) * arr.ndim)
    return pl.pallas_call(
        body,
        out_shape=(jax.ShapeDtypeStruct((lp, GLA_V), F32), jax.ShapeDtypeStruct((lp // c, GLA_DV, GLA_QK), F32)),
        grid=(nblk,),
        in_specs=[pl.BlockSpec((BLK, GLA_QK), lambda i: (i, C_GQ // GLA_QK)),
                  pl.BlockSpec((BLK, GLA_QK), lambda i: (i, C_GK // GLA_QK)),
                  pl.BlockSpec((BLK, GLA_V), lambda i: (i, C_GV // GLA_V)),
                  pl.BlockSpec((BLK, 128), lambda i: (i, C_GA // 128)),
                  full(w2p), full(b), full(tri), full(ones_qv)],
        out_specs=(pl.BlockSpec((BLK, GLA_V), lambda i: (i, 0)),
                   pl.BlockSpec((nch, GLA_DV, GLA_QK), lambda i: (i, 0, 0))),
        scratch_shapes=[pltpu.VMEM((GLA_DV, GLA_QK), F32)],
        compiler_params=_cp("arbitrary"), name=name)(proj, proj, proj, proj, w2p, b, tri, ones_qv)


def _gla_bwd(proj, do, states, w2p, b, tables, *, name):
    lp = proj.shape[0]
    nblk, c, s1, s2 = lp // BLK, GLA_CHUNK, GLA_SUB, GLA_SUB2
    nch = BLK // c

    def body(q_ref, k_ref, v_ref, a_ref, do_ref, st_ref, w_ref, b_ref, tri_ref, trit_ref, ones_ref, onest_ref,
             dp_ref, dw_ref, db_ref, dstate, dqs_s, dk_s, dg_s, dv_s):
        i_blk = nblk - 1 - pl.program_id(0)

        @pl.when(pl.program_id(0) == 0)
        def _():
            dstate[...] = jnp.zeros_like(dstate)
            dw_ref[...] = jnp.zeros_like(dw_ref)
            db_ref[...] = jnp.zeros_like(db_ref)
        hm_q = _head_masks(GLA_QK, GLA_DK)
        hm_v = _head_masks(GLA_V, GLA_DV)
        masks = _gla_masks()
        tri, trit, ones_qv, ones_vq = tri_ref[...], trit_ref[...], ones_ref[...], onest_ref[...]
        w2, bias = w_ref[...], b_ref[...]
        rsum = lambda x: jnp.sum(x, axis=0, keepdims=True)

        def chunk(cc, carry):
            ci = nch - 1 - cc
            sl = pl.ds(pl.multiple_of(ci * c, c), c)
            ok = _gla_rows(i_blk, ci, lp)
            k, v, ga = k_ref[sl, :], v_ref[sl, :], a_ref[sl, :]
            vb = v.astype(BF16)
            qs = q_ref[sl, :] * (GLA_DK ** -0.5)
            z, g = _gla_gate(ga, w2, bias, ok, tri)
            last = g[c - 1:c, :]
            elast = jnp.exp(last)
            eg = jnp.exp(g)
            ekl = jnp.exp(last - g)
            qe, ke = qs * eg, k * ekl
            dov = do_ref[sl, :]
            st = st_ref[ci]
            dsn = dstate[...]
            qst = _stack_heads(qe, hm_q).astype(BF16)
            dost = jnp.concatenate([dov[:, GLA_DV * h:GLA_DV * (h + 1)] for h in range(GLA_HEADS)], axis=0).astype(BF16)
            dqe_st = _dot(dost, st.astype(BF16))
            dqe = dqe_st[0:c, :] * hm_q[0]
            for h in range(1, GLA_HEADS):
                dqe = dqe + dqe_st[c * h:c * h + c, :] * hm_q[h]
            dstate[...] = _dot_tn(dost, qst) + dsn * elast
            dlast = rsum(dsn * st) * elast
            df = _stack_heads(dsn, hm_q).astype(BF16)
            dv_s[...] = _dot_nt(ke.astype(BF16), df)
            dke = _dot(vb, df)
            xk = dke * ke
            dqs_s[...] = dqe * eg
            dk_s[...] = dke * ekl
            dg_s[...] = dqe * qe - xk
            dlast = dlast + rsum(xk)
            eqs, eks, qhs, khs, qa, qb, ka, kb, p = _gla_hats(qs, k, g, masks, hm_q)
            dost_v = _stack_heads(dov, hm_v).astype(BF16)
            dp = _dot_nt(dost_v, vb)
            dv_s[...] += _dot_tn(p.astype(BF16), dost_v)
            dpa, dpb = dp.astype(BF16), (dp * masks[2]).astype(BF16)
            dq_all = (_dot(dpa, ka), _dot(dpb, kb))
            dk_all = (_dot_tn(dpa, qa), _dot_tn(dpb, qb))
            for t in range(N_SUB + N_SUB2):
                lvl, i = (0, t) if t < N_SUB else (1, t - N_SUB)
                cols = slice(GLA_QK * i, GLA_QK * (i + 1))
                dq_st = dq_all[lvl][:, cols]
                dqh = dq_st[0:c, :] * hm_q[0]
                for h in range(1, GLA_HEADS):
                    dqh = dqh + dq_st[c * h:c * h + c, :] * hm_q[h]
                dkh = dk_all[lvl][:, cols]
                xq, xkh = dqh * qhs[t], dkh * khs[t]
                dqs_s[...] += dqh * eqs[t]
                dk_s[...] += dkh * eks[t]
                dg_s[...] += xq - xkh
                back_ref = xkh - xq
                if lvl == 0:
                    row = s1 * (i + 1) - 1
                    dg_s[row:row + 1, :] += rsum(back_ref)
                else:
                    for blk in range(c // s1):
                        row = s1 * blk + s2 * (i + 1) - 1
                        dg_s[row:row + 1, :] += rsum(back_ref[s1 * blk:s1 * blk + s1, :])
            kes, qes, ws, dbs = [], [], [], []
            for j in range(s2):
                em = jnp.exp(jnp.minimum(g - _roll_rows(g, j), 0.0)) * masks[3][j]
                kes.append(_roll_rows(k, j) * em)
                qes.append(qs * em)
                ws.append((qs * kes[j]).astype(BF16))
                dbs.append((dov * _roll_rows(v, j)).astype(BF16))
            ball = _dot(jnp.concatenate(ws, axis=0), ones_qv)
            dwall = _dot(jnp.concatenate(dbs, axis=0), ones_vq)
            for j in range(s2):
                back = (lambda x: x) if j == 0 else (lambda x, j=j: pltpu.roll(x, c - j, 0))
                dw = dwall[c * j:c * j + c, :]
                dv_s[...] += back(ball[c * j:c * j + c, :] * dov)
                dqs_s[...] += dw * kes[j]
                dk_s[...] += back(dw * qes[j])
                x = dw * qs * kes[j]
                dg_s[...] += x - back(x)
            dg_s[c - 1:c, :] += dlast
            dla = jnp.where(ok, _tri_sum(trit, dg_s[...]), 0.0)
            dz = dla * (1.0 / GLA_TAU) / (1.0 + jnp.exp(z))
            dzb = dz.astype(BF16)
            dp_ref[sl, 0:256] = (dqs_s[...] * (GLA_DK ** -0.5)).astype(BF16)
            dp_ref[sl, 256:512] = dk_s[...].astype(BF16)
            dp_ref[sl, 512:1024] = dv_s[...].astype(BF16)
            dp_ref[sl, 1024:1152] = _dot_nt(dzb, w2).astype(BF16)
            dp_ref[sl, 1152:1280] = jnp.zeros((c, 128), BF16)
            dw_ref[...] += _dot_tn(ga.astype(BF16), dzb)
            db_ref[...] += rsum(dz)
            return carry

        lax.fori_loop(0, nch, chunk, 0)

    tri, trit, ones_qv, ones_vq = tables
    full = lambda arr: pl.BlockSpec(arr.shape, lambda i: (0,) * arr.ndim)
    rev = lambda i: nblk - 1 - i
    return pl.pallas_call(
        body,
        out_shape=(jax.ShapeDtypeStruct((lp, P_GLA), BF16),
                   jax.ShapeDtypeStruct((128, GLA_QK), F32), jax.ShapeDtypeStruct((1, GLA_QK), F32)),
        grid=(nblk,),
        in_specs=[pl.BlockSpec((BLK, GLA_QK), lambda i: (rev(i), C_GQ // GLA_QK)),
                  pl.BlockSpec((BLK, GLA_QK), lambda i: (rev(i), C_GK // GLA_QK)),
                  pl.BlockSpec((BLK, GLA_V), lambda i: (rev(i), C_GV // GLA_V)),
                  pl.BlockSpec((BLK, 128), lambda i: (rev(i), C_GA // 128)),
                  pl.BlockSpec((BLK, GLA_V), lambda i: (rev(i), 0)),
                  pl.BlockSpec((nch, GLA_DV, GLA_QK), lambda i: (rev(i), 0, 0)),
                  full(w2p), full(b), full(tri), full(trit), full(ones_qv), full(ones_vq)],
        out_specs=(pl.BlockSpec((BLK, P_GLA), lambda i: (rev(i), 0)),
                   pl.BlockSpec((128, GLA_QK), lambda i: (0, 0)),
                   pl.BlockSpec((1, GLA_QK), lambda i: (0, 0))),
        scratch_shapes=[pltpu.VMEM((GLA_DV, GLA_QK), F32), pltpu.VMEM((c, GLA_QK), F32),
                        pltpu.VMEM((c, GLA_QK), F32), pltpu.VMEM((c, GLA_QK), F32), pltpu.VMEM((c, GLA_V), F32)],
        compiler_params=_cp("arbitrary"), name=name)(proj, proj, proj, proj, do, states, w2p, b, tri, trit, ones_qv, ones_vq)


def _as2d(a):
    return a.reshape(-1, a.shape[-1])


def _ew_tile(r):
    return _tile(r, (512, 256, 128, 64, 32, 16, 8))


def _add2(a, b, *, out_dtype, name):
    a2, b2 = _as2d(a), _as2d(b)
    r, n = a2.shape
    tm = _ew_tile(r)

    def body(a_ref, b_ref, o_ref):
        o_ref[...] = (a_ref[...] + b_ref[...]).astype(o_ref.dtype)

    blk = pl.BlockSpec((tm, n), lambda i: (i, 0))
    return pl.pallas_call(body, out_shape=jax.ShapeDtypeStruct((r, n), out_dtype), grid=(r // tm,), in_specs=[blk, blk],
                          out_specs=blk, compiler_params=_cp("parallel"), name=name)(a2, b2).reshape(a.shape)


def _sum_slots(own, q, *, name):
    shape = own.shape
    q3 = q.reshape(3, -1, shape[-1])
    own2 = _as2d(own)
    r, n = own2.shape
    tm = _ew_tile(r)

    def body(own_ref, q_ref, o_ref):
        f = lambda i: q_ref[i].astype(F32)
        o_ref[...] = ((own_ref[...].astype(F32) + f(0)) + f(1)) + f(2)

    blk = pl.BlockSpec((tm, n), lambda i: (i, 0))
    return pl.pallas_call(
        body, out_shape=jax.ShapeDtypeStruct((r, n), F32), grid=(r // tm,),
        in_specs=[blk, pl.BlockSpec((3, tm, n), lambda i: (0, i, 0))], out_specs=blk,
        compiler_params=_cp("parallel"), name=name)(own2, q3).reshape(shape)


def _adamw(w, g, m, v, *, name):
    shape = w.shape
    w2, g2, m2, v2 = _as2d(w), _as2d(g), _as2d(m), _as2d(v)
    r, n = w2.shape
    tm = _ew_tile(r)

    def body(w_ref, g_ref, m_ref, v_ref, d_ref, mo_ref, vo_ref):
        d_ref[...], mo_ref[...], vo_ref[...] = _adam_math(w_ref[...], g_ref[...], m_ref[...], v_ref[...])

    blk = pl.BlockSpec((tm, n), lambda i: (i, 0))
    o = jax.ShapeDtypeStruct((r, n), F32)
    d, mo, vo = pl.pallas_call(body, out_shape=(o, o, o), grid=(r // tm,), in_specs=[blk] * 4, out_specs=(blk,) * 3,
                               compiler_params=_cp("parallel"), name=name)(w2, g2, m2, v2)
    return d.reshape(shape), mo.reshape(shape), vo.reshape(shape)


def _adam_math(w, gv, m, v):
    c1 = 1.0 - ADAM_B1 ** ADAM_STEP
    c2 = 1.0 - ADAM_B2 ** ADAM_STEP
    mn = ADAM_B1 * m + (1.0 - ADAM_B1) * gv
    vn = ADAM_B2 * v + (1.0 - ADAM_B2) * (gv * gv)
    return -ADAM_LR * ((mn / c1) / (jnp.sqrt(vn / c2) + ADAM_EPS) + ADAM_WD * w), mn, vn


def _adamw_halves(w, m, v, mine, theirs, c, *, name):
    depth, rows, n = w.shape
    r2 = rows // 2
    tm = next(t for t in range(min(r2, 256), 0, -8) if r2 % t == 0)
    steps = r2 // tm

    def body(c_ref, w_ref, m_ref, v_ref, *rest):
        halves, (g_ref, d_ref, mo_ref, vo_ref) = rest[:2 * depth], rest[2 * depth:]
        l, h = pl.program_id(0), pl.program_id(1)
        gv = None
        for k in range(depth):
            gk = jnp.where(h == c_ref[0], halves[2 * k][...], halves[2 * k + 1][...])
            gv = gk if gv is None else jnp.where(l == k, gk, gv)
        g_ref[...] = gv
        d_ref[...], mo_ref[...], vo_ref[...] = _adam_math(w_ref[...], gv, m_ref[...], v_ref[...])

    big = pl.BlockSpec((None, tm, n), lambda l, h, i, c_ref: (l, h * steps + i, 0))
    half = lambda k: pl.BlockSpec((tm, n), lambda l, h, i, c_ref: (jnp.where(l == k, i, 0), 0))
    o = jax.ShapeDtypeStruct(w.shape, F32)
    args = [a for k in range(depth) for a in (mine[k], theirs[k])]
    return pl.pallas_call(
        body, out_shape=(o, o, o, o),
        grid_spec=pltpu.PrefetchScalarGridSpec(
            num_scalar_prefetch=1, grid=(depth, 2, steps),
            in_specs=[big, big, big] + [half(k) for k in range(depth) for _ in range(2)], out_specs=(big,) * 4),
        compiler_params=_cp("arbitrary", "arbitrary", "arbitrary"), name=name)(
            jnp.reshape(c, (1,)).astype(jnp.int32), w, m, v, *args)


ANY = pl.BlockSpec(memory_space=pl.ANY)


def _place():
    return lax.axis_index("x"), lax.axis_index("y"), lax.axis_index("c")


def _other_chips(x, y):
    return [(1 - x, y), (x, 1 - y), (1 - x, 1 - y)]


def _remote(src, dst, ssem, rsem, dev):
    return pltpu.make_async_remote_copy(src_ref=src, dst_ref=dst, send_sem=ssem, recv_sem=rsem, device_id=dev,
                                        device_id_type=MESH)


def _allgather_chips(arrs, *, name):
    n = len(arrs)

    def body(*refs):
        ins, outs = refs[:n], refs[n:2 * n]
        s1, r1, s2, r2 = refs[2 * n:]
        x, y, c = _place()
        q = 2 * x + y
        chips = _other_chips(x, y)
        qs = [2 * cx + cy for cx, cy in chips]
        sib = (x, y, 1 - c)
        first, passed = [], []
        for k in range(n):
            for j, chip in enumerate(chips):
                first.append(_remote(ins[k].at[c], outs[k].at[c, q], s1.at[k, j], r1.at[k, j], (*chip, c)))
        for cp in first:
            cp.start()
        for k in range(n):
            for j, chip in enumerate(chips):
                land = outs[k].at[c, qs[j]]
                _remote(land, land, s1.at[k, j], r1.at[k, j], (*chip, c)).wait_recv()
                fw = _remote(land, land, s2.at[k, j], r2.at[k, j], sib)
                fw.start()
                passed.append(fw)
        for k in range(n):
            for j in range(3):
                land = outs[k].at[1 - c, qs[j]]
                _remote(land, land, s2.at[k, j], r2.at[k, j], sib).wait_recv()
        for cp in first + passed:
            cp.wait_send()

    sem = pltpu.SemaphoreType.DMA
    outs = pl.pallas_call(
        body, out_shape=tuple(jax.ShapeDtypeStruct((2, 4) + a.shape[1:], a.dtype) for a in arrs),
        in_specs=[ANY] * n, out_specs=(ANY,) * n,
        scratch_shapes=[sem((n, 3)), sem((n, 3)), sem((n, 3)), sem((n, 3))], name=name)(*arrs)
    chip = 2 * lax.axis_index("x") + lax.axis_index("y")
    return [lax.dynamic_update_slice_in_dim(o, a[:, None], chip, axis=1) for o, a in zip(outs, arrs)]


def _pair_exchange(arrs, *, name):
    n = len(arrs)

    def body(*refs):
        ins, outs = refs[:n], refs[n:2 * n]
        ssem, rsem = refs[2 * n:]
        x, y, c = _place()
        cps = [_remote(ins[k].at[:, 1 - c], outs[k], ssem.at[k], rsem.at[k], (x, y, 1 - c)) for k in range(n)]
        for cp in cps:
            cp.start()
        for cp in cps:
            cp.wait()

    sem = pltpu.SemaphoreType.DMA
    return pl.pallas_call(
        body, out_shape=tuple(jax.ShapeDtypeStruct((a.shape[0],) + a.shape[2:], a.dtype) for a in arrs),
        in_specs=[ANY] * n, out_specs=(ANY,) * n, scratch_shapes=[sem((n,)), sem((n,))], name=name)(*arrs)


def _pair_sum(mine, theirs, c, *, name):
    _, _, r, n = mine.shape
    tm = r if r <= 512 else _ew_tile(r)

    def body(c_ref, a_ref, b_ref, o_ref):
        o_ref[...] = (a_ref[...] + b_ref[...]).astype(BF16)

    blk = pl.BlockSpec((None, tm, n), lambda s, i, c_ref: (s, i, 0))
    return pl.pallas_call(
        body, out_shape=jax.ShapeDtypeStruct((4, r, n), BF16),
        grid_spec=pltpu.PrefetchScalarGridSpec(
            num_scalar_prefetch=1, grid=(4, r // tm),
            in_specs=[pl.BlockSpec((None, None, tm, n), lambda s, i, c_ref: (s, c_ref[0], i, 0)), blk], out_specs=blk),
        compiler_params=_cp("parallel", "parallel"), name=name)(jnp.reshape(c, (1,)).astype(jnp.int32), mine, theirs)


def _chip_copies(ins, outs, ssem, rsem, mode):
    x, y, c = _place()
    q = 2 * x + y
    sends, recvs = [], []
    for k in range(len(ins)):
        for j, (cx, cy) in enumerate(_other_chips(x, y)):
            sem = (ssem.at[k, j], rsem.at[k, j], (cx, cy, c))
            if mode == "scatter":
                sends.append(_remote(ins[k].at[2 * cx + cy], outs[k].at[j], *sem))
                recvs.append(sends[-1])
            else:
                sends.append(_remote(ins[k].at[c], outs[k].at[2 * q + c], *sem))
                recvs.append(_remote(ins[k].at[c], outs[k].at[2 * (2 * cx + cy) + c], *sem))
    return sends, recvs


def _chip_wait(sends, recvs):
    for cp in sends:
        cp.wait_send()
    for cp in recvs:
        cp.wait_recv()


def _landing_shape(a, mode):
    return jax.ShapeDtypeStruct(((3,) if mode == "scatter" else (8,)) + a.shape[1:], a.dtype)


def _chip_exchange(arrs, mode, *, name):
    n = len(arrs)

    def body(*refs):
        ins, outs = refs[:n], refs[n:2 * n]
        ssem, rsem = refs[2 * n:]
        sends, recvs = _chip_copies(ins, outs, ssem, rsem, mode)
        for cp in sends:
            cp.start()
        _chip_wait(sends, recvs)

    sem = pltpu.SemaphoreType.DMA
    return list(pl.pallas_call(
        body, out_shape=tuple(_landing_shape(a, mode) for a in arrs),
        in_specs=[ANY] * n, out_specs=(ANY,) * n, scratch_shapes=[sem((n, 3)), sem((n, 3))], name=name)(*arrs))


def _pair_fill(bufs, owns, *, name):
    n = len(bufs)

    def body(*refs):
        own, outs = refs[n:2 * n], refs[2 * n:3 * n]
        ssem, rsem = refs[3 * n:]
        x, y, c = _place()
        q = 2 * x + y
        sib = (x, y, 1 - c)
        sends, recvs = [], []
        for k in range(n):
            for j, (cx, cy) in enumerate(_other_chips(x, y)):
                mine, theirs = outs[k].at[2 * (2 * cx + cy) + c], outs[k].at[2 * (2 * cx + cy) + 1 - c]
                sends.append(_remote(mine, mine, ssem.at[k, j], rsem.at[k, j], sib))
                recvs.append(_remote(mine, theirs, ssem.at[k, j], rsem.at[k, j], sib))
            slots = outs[k].at[pl.ds(2 * q, 2)]
            sends.append(_remote(own[k], slots, ssem.at[k, 3], rsem.at[k, 3], sib))
            recvs.append(sends[-1])
        for cp in sends:
            cp.start()
        _chip_wait(sends, recvs)

    sem = pltpu.SemaphoreType.DMA
    return list(pl.pallas_call(
        body, out_shape=tuple(jax.ShapeDtypeStruct(b.shape, b.dtype) for b in bufs),
        in_specs=[ANY] * (2 * n), out_specs=(ANY,) * n, scratch_shapes=[sem((n, 4)), sem((n, 4))],
        input_output_aliases={k: k for k in range(n)}, name=name)(*bufs, *owns))


def _pair_swap(arrs, *, name):
    n = len(arrs)

    def body(*refs):
        ins, outs = refs[:n], refs[n:2 * n]
        ssem, rsem = refs[2 * n:]
        x, y, c = _place()
        cps = [_remote(ins[k], outs[k], ssem.at[k], rsem.at[k], (x, y, 1 - c)) for k in range(n)]
        for cp in cps:
            cp.start()
        for cp in cps:
            cp.wait()

    sem = pltpu.SemaphoreType.DMA
    return pl.pallas_call(
        body, out_shape=tuple(jax.ShapeDtypeStruct(a.shape, a.dtype) for a in arrs),
        in_specs=[ANY] * n, out_specs=(ANY,) * n, scratch_shapes=[sem((n,)), sem((n,))], name=name)(*arrs)


def _allreduce_small(slab, *, name):
    r, n = slab.shape

    def body(x_ref, o_ref, buf, ssem, rsem):
        x, y, c = _place()
        me = 4 * x + 2 * y + c
        buf[me] = x_ref[...]
        cps = []
        for rel in range(1, 8):
            bx, by, bc = (rel >> 2) & 1, (rel >> 1) & 1, rel & 1
            px, py, pc = (x + bx) % 2, (y + by) % 2, (c + bc) % 2
            cps.append((_remote(x_ref, buf.at[me], ssem.at[rel - 1], rsem.at[rel - 1], (px, py, pc)),
                        4 * px + 2 * py + pc, (px, py, pc)))
        for cp, _, _ in cps:
            cp.start()
        for rel, (cp, peer, dev) in enumerate(cps):
            cp.wait_send()
            _remote(x_ref, buf.at[peer], ssem.at[rel], rsem.at[rel], dev).wait_recv()
        acc = buf[0]
        for k in range(1, 8):
            acc = acc + buf[k]
        o_ref[...] = acc

    vm = pl.BlockSpec(memory_space=pltpu.VMEM)
    sem = pltpu.SemaphoreType.DMA
    return pl.pallas_call(
        body, out_shape=jax.ShapeDtypeStruct((r, n), F32), in_specs=[vm], out_specs=vm,
        scratch_shapes=[pltpu.VMEM((8, r, n), F32), sem((7,)), sem((7,))], name=name)(slab)


def _slab(arrs, row_mult):
    flat = jnp.concatenate([a.reshape(-1) for a in arrs])
    unit = 128 * row_mult
    total = -(-flat.size // unit) * unit
    return jnp.pad(flat, (0, total - flat.size)).reshape(-1, 128)


def _unslab(slab, shapes):
    flat = slab.reshape(-1)
    out, off = [], 0
    for s in shapes:
        size = int(np.prod(s))
        out.append(flat[off:off + size].reshape(s))
        off += size
    return out


def _cols_from_chips(a):
    return jnp.transpose(a, (1, 0, 2)).reshape(a.shape[1], -1)


def _cols_to_chips(a, parts):
    r = a.shape[0]
    return jnp.transpose(a.reshape(r, parts, -1), (1, 0, 2))


BIG = ("w_in", "w_out", "up", "down")
GATHER_RIDES = {("proj", 0): (("w_out", 0), ("up", 0)), ("mix_out", 0): (("down", 0),),
                ("ffn_up_a", 0): (("w_in", 1), ("w_out", 1)), ("ffn_up_g", 0): (("up", 1),),
                ("ffn_down", 0): (("down", 1),)}
REDUCE_RIDES = {("ffn_down_dx", 0): (("up",), 1), ("ffn_up_a_dx", 0): (("w_in", "w_out"), 1),
                ("ffn_up_g_dx", 0): (("down",), 1),
                ("proj_dx", 0): (("up",), 0), ("proj_dw_0", 0): (("down",), 0), ("proj_dw_1", 0): (("w_out",), 0)}


class _LocalWeights:
    def __init__(self, meta, win, wout, up_a, up_g, down, w2p, cw):
        self._meta, self._w = meta, {"win": win, "wout": wout, "up_a": up_a, "up_g": up_g, "down": down, "w2p": w2p,
                                     "cw": cw}

    def meta(self):
        return self._meta

    def get(self, kind, l):
        return self._w[kind][l]

    def mm(self, site, l, a, b, fn=None, **kw):
        return (fn or _mm)(a, b, name=site, **kw)

    def grads_done(self, l, g, kinds):
        pass


class _ChipWeights:
    def __init__(self, w_in, w_out, ffn_up, ffn_down, meta_tokens, gla_gate_w2, ffn_conv_w):
        self.x, self.y, self.c = _place()
        self.q = 2 * self.x + self.y
        halves = lambda a: a.astype(BF16).reshape(2, a.shape[0] // 2, a.shape[1])
        self.own = {(k, l): halves(a[l]) for k, a in zip(BIG, (w_in, w_out, ffn_up, ffn_down)) for l in range(DEPTH)}
        self.landed, self.swapped, self.full, self.n_swaps = {}, {}, {}, 0
        self.sh_shapes = [meta_tokens.shape, gla_gate_w2.shape, ffn_conv_w.shape]
        self.own["small", 0] = _slab([meta_tokens, gla_gate_w2, ffn_conv_w], 16).reshape(2, -1, 128)
        first = [("w_in", 0), ("small", 0)]
        for key, arr in zip(first, _chip_exchange([self.own[k] for k in first], "bcast", name="gather_first")):
            self.landed[key] = arr
        sh = self._whole("small", 0).reshape(4, -1, 128)
        parts = [_unslab(sh[k], self.sh_shapes) for k in range(4)]
        self._meta = jnp.concatenate([p[0] for p in parts], axis=-1)
        self.w2 = jnp.concatenate([p[1] for p in parts], axis=-1)
        self.cw = jnp.concatenate([p[2] for p in parts], axis=-1)
        self.partial, self.slots = {}, {}

    def _whole(self, kind, l):
        if (kind, l) not in self.full:
            keys = [k for k in self.landed if k not in self.full]
            got = _pair_fill([self.landed[k] for k in keys], [self.own[k] for k in keys],
                             name=f"gather_fill_{self.n_swaps}")
            self.n_swaps += 1
            for k, buf in zip(keys, got):
                self.full[k] = buf.reshape(4, 2 * buf.shape[1], buf.shape[2])
        return self.full[kind, l]

    def meta(self):
        return self._meta

    def get(self, kind, l):
        if kind == "win":
            return _to_kernel_cols(_cols_from_chips(self._whole("w_in", l)))
        if kind == "wout":
            return self._whole("w_out", l).reshape(D_MODEL, D_MODEL)
        if kind == "up_a":
            return _cols_from_chips(self._whole("up", l)[0:2])
        if kind == "up_g":
            return _cols_from_chips(self._whole("up", l)[2:4])
        if kind == "down":
            return self._whole("down", l).reshape(D_FF, D_MODEL)
        if kind == "w2p":
            return jnp.pad(self.w2[l], ((0, 128 - GLA_RANK), (0, 0))).astype(BF16)
        return self.cw[l]

    def mm(self, site, l, a, b, fn=None, **kw):
        fn = fn or _mm
        if (site, l) in GATHER_RIDES:
            keys = GATHER_RIDES[site, l]
            out, got = fn(a, b, name=site, carry=([self.own[k] for k in keys], "bcast"), **kw)
            self.landed.update(zip(keys, got))
            return out
        if (site, l) in REDUCE_RIDES:
            kinds, gl = REDUCE_RIDES[site, l]
            keys = [(k, gl) for k in kinds]
            if all(k in self.partial and k not in self.slots for k in keys):
                out, got = fn(a, b, name=site, carry=([self.partial[k] for k in keys], "scatter"), **kw)
                self.slots.update(zip(keys, got))
                return out
        return fn(a, b, name=site, **kw)

    def grads_done(self, l, g, kinds):
        split = lambda a: a.reshape(4, 2, a.shape[-2] // 2, a.shape[-1]) if a.ndim == 3 else \
            a.reshape(4, 2, a.shape[0] // 8, a.shape[1])
        src = {"w_in": lambda: g["w_in"][l], "w_out": lambda: g["w_out"][l],
               "up": lambda: g["up"][l], "down": lambda: g["down"][l]}
        big = {k: split(src[k]()) for k in kinds}
        from_sib = _pair_exchange([big[k] for k in kinds], name=f"grads_pair_exchange_{l}_{kinds[0]}")
        for k, theirs in zip(kinds, from_sib):
            self.partial[k, l] = _pair_sum(big[k], theirs, self.c, name=f"pair_sum_{k}_{l}")

    def reduce(self):
        keys = [(k, l) for l in range(DEPTH) for k in BIG]
        late = [k for k in keys if k not in self.slots]
        self.slots.update(zip(late, _chip_exchange([self.partial[k] for k in late], "scatter",
                                                   name="grads_chip_exchange")))
        half = {}
        for k in keys:
            own = lax.dynamic_index_in_dim(self.partial[k], self.q, 0, keepdims=False)
            half[k] = _sum_slots(own, self.slots[k], name=f"chip_sum_{k[0]}_{k[1]}")
        other = dict(zip(keys, _pair_swap([half[k] for k in keys], name="grads_pair_swap")))
        return [([half[k, l] for l in range(DEPTH)], [other[k, l] for l in range(DEPTH)]) for k in BIG]


def _local_step(x_rows, target_rows, wts, pre_mix_norm, gla_gate_b, ret_norm_w, gla_norm_w, post_mix_norm,
                pre_ffn_norm, ffn_conv_b, post_ffn_norm):
    d = D_MODEL
    lp = x_rows.shape[0] + FRONT + BACK
    row = lambda a, l: a[l][None, :]
    rtab = _ret_tables(lp)
    gtab = _gla_tables()
    h0 = jnp.concatenate([jnp.zeros((PADF, d), F32), wts.meta(), x_rows, jnp.zeros((BACK, d), F32)], axis=0)
    target = jnp.pad(target_rows, ((FRONT, BACK), (0, 0)))

    saved = []
    h = h0
    _, hn = _resid_norm(h0, None, None, row(pre_mix_norm, 0), name="norm_in")
    loss_local = dy = None
    for l in range(DEPTH):
        s = {"h_in": h, "hn": hn}
        s["proj"] = wts.mm("proj", l, hn, wts.get("win", l))
        s["o_ret"], s["st_ret"] = _retention(s["proj"], rtab, name="retention")
        s["o_gla"], s["st_gla"] = _gla(s["proj"], wts.get("w2p", l), row(gla_gate_b, l), gtab, name="gla")
        s["merged"] = _merge(s["o_ret"], s["o_gla"], s["proj"], row(ret_norm_w, l), row(gla_norm_w, l), name="merge")
        s["m"] = wts.mm("mix_out", l, s["merged"], wts.get("wout", l))
        s["h_mid"], s["hn2"] = _resid_norm(h, s["m"], row(post_mix_norm, l), row(pre_ffn_norm, l), name="resid_mix")
        s["ua"] = wts.mm("ffn_up_a", l, s["hn2"], wts.get("up_a", l))
        s["ug"] = wts.mm("ffn_up_g", l, s["hn2"], wts.get("up_g", l))
        cw_a, cw_g = wts.get("cw", l)[:, :D_FF], wts.get("cw", l)[:, D_FF:]
        cb_a, cb_g = ffn_conv_b[l][None, :D_FF], ffn_conv_b[l][None, D_FF:]
        s["conv"] = (cw_a, cw_g, cb_a, cb_g)
        s["act"] = _conv_act(s["ua"], s["ug"], cw_a, cw_g, cb_a, cb_g, name="conv_act")
        s["f"] = wts.mm("ffn_down", l, s["act"], wts.get("down", l))
        if l + 1 < DEPTH:
            h, hn = _resid_norm(s["h_mid"], s["f"], row(post_ffn_norm, l), row(pre_mix_norm, l + 1), name="resid_ffn")
        else:
            loss_local, dy, df_last, dw_last = _loss_head(s["h_mid"], s["f"], row(post_ffn_norm, l), target,
                                                          name="loss_head")
        saved.append(s)

    g = {k: [None] * DEPTH for k in ("pre_mix", "w_in", "w2", "gb", "ret_n", "gla_n", "w_out", "post_mix", "pre_ffn",
                                     "up", "cw", "cb", "down", "post_ffn")}
    dh_out, dhn_next = dy, None
    for l in reversed(range(DEPTH)):
        s = saved[l]
        cw_a, cw_g, cb_a, cb_g = s["conv"]
        if l + 1 < DEPTH:
            dh, df, g["pre_mix"][l + 1], g["post_ffn"][l] = _resid_norm_bwd(
                dh_out, dhn_next, saved[l + 1]["h_in"], s["f"], row(pre_mix_norm, l + 1), row(post_ffn_norm, l),
                name="resid_ffn_bwd")
        else:
            dh, df, g["post_ffn"][l] = dh_out, df_last, dw_last
        dact = wts.mm("ffn_down_dx", l, df, wts.get("down", l), nt=True)
        g["down"][l] = _mm_tn(s["act"], df, tn=512, name="ffn_down_dw")
        du_a, du_g, dcw_a, dcw_g, dcb_a, dcb_g = _conv_act_bwd(s["ua"], s["ug"], dact, cw_a, cw_g, cb_a, cb_g,
                                                               name="conv_act_bwd")
        g["cw"][l] = jnp.concatenate([dcw_a, dcw_g], axis=1)
        g["cb"][l] = jnp.concatenate([dcb_a, dcb_g], axis=1)[0]
        half_up = _mm_tn(s["hn2"], du_a, tn=D_FF // 2, blocks=(4, 0), name="ffn_up_a_dw")
        g["up"][l] = _mm_tn(s["hn2"], du_g, tn=D_FF // 2, blocks=(4, 2), into=half_up, name="ffn_up_g_dw")
        dhn2 = wts.mm("ffn_up_a_dx", l, du_a, wts.get("up_a", l), nt=True)
        dhn2 = wts.mm("ffn_up_g_dx", l, du_g, wts.get("up_g", l), nt=True, add=dhn2)
        dh, dm, g["pre_ffn"][l], g["post_mix"][l] = _resid_norm_bwd(
            dh, dhn2, s["h_mid"], s["m"], row(pre_ffn_norm, l), row(post_mix_norm, l), name="resid_mix_bwd")
        g["w_out"][l] = _mm_tn(s["merged"], dm, name="mix_out_dw")
        wts.grads_done(l, g, ("w_out", "up", "down"))
        dmerged = wts.mm("mix_out_dx", l, dm, wts.get("wout", l), nt=True)
        do_ret, do_gla, d_gate, g["ret_n"][l], g["gla_n"][l] = _merge_bwd(
            dmerged, s["o_ret"], s["o_gla"], s["proj"], row(ret_norm_w, l), row(gla_norm_w, l), name="merge_bwd")
        d_ret = _retention_bwd(s["proj"], do_ret, s["st_ret"], rtab, name="retention_bwd")
        d_gla, dw2, dgb = _gla_bwd(s["proj"], do_gla, s["st_gla"], wts.get("w2p", l), row(gla_gate_b, l), gtab,
                                   name="gla_bwd")
        g["w2"][l], g["gb"][l] = dw2[:GLA_RANK], dgb[0]
        pieces = (d_ret, d_gate, d_gla)
        g["w_in"][l] = _to_reference_chips(*[wts.mm(f"proj_dw_{i}", l, s["hn"], p, fn=_mm_tn)
                                             for i, p in enumerate(pieces)])
        win = wts.get("win", l)
        dhn_next = wts.mm("proj_dx", l, pieces, [win[:, 0:P_RET], win[:, P_RET:P_RET + P_GATE], win[:, P_RET + P_GATE:]],
                          fn=_mm_nt_sum)
        dh_out = dh
        wts.grads_done(l, g, ("w_in",))
    dh0, _, g["pre_mix"][0], _ = _resid_norm_bwd(dh_out, dhn_next, h0, None, row(pre_mix_norm, 0), None,
                                                 name="norm_in_bwd")
    return loss_local, dh0, g


def kernel(x, meta_tokens, pre_mix_norm, w_in, gla_gate_w2, gla_gate_b, ret_norm_w, gla_norm_w, w_out, post_mix_norm, pre_ffn_norm, ffn_up, ffn_conv_w, ffn_conv_b, ffn_down, post_ffn_norm, loss_target, m_meta_tokens, m_pre_mix_norm, m_w_in, m_gla_gate_w2, m_gla_gate_b, m_ret_norm_w, m_gla_norm_w, m_w_out, m_post_mix_norm, m_pre_ffn_norm, m_ffn_up, m_ffn_conv_w, m_ffn_conv_b, m_ffn_down, m_post_ffn_norm, v_meta_tokens, v_pre_mix_norm, v_w_in, v_gla_gate_w2, v_gla_gate_b, v_ret_norm_w, v_gla_norm_w, v_w_out, v_post_mix_norm, v_pre_ffn_norm, v_ffn_up, v_ffn_conv_w, v_ffn_conv_b, v_ffn_down, v_post_ffn_norm):
    xi, yi, ci = _place()
    chip = 2 * xi + yi
    seq = x.shape[1]
    d = D_MODEL
    wts = _ChipWeights(w_in, w_out, ffn_up, ffn_down, meta_tokens, gla_gate_w2, ffn_conv_w)
    loss_local, dh0, g = _local_step(x[0], loss_target[0], wts, pre_mix_norm, gla_gate_b, ret_norm_w, gla_norm_w,
                                     post_mix_norm, pre_ffn_norm, ffn_conv_b, post_ffn_norm)
    grad_x = dh0[FRONT:FRONT + seq][None]
    names = ("w_in", "w_out", "ffn_up", "ffn_down")
    big_halves = wts.reduce()

    small_full = [dh0[PADF:FRONT], jnp.stack(g["pre_mix"])[:, 0], jnp.stack(g["w2"]), jnp.stack(g["gb"]),
                  jnp.stack(g["ret_n"])[:, 0], jnp.stack(g["gla_n"])[:, 0], jnp.stack(g["post_mix"])[:, 0],
                  jnp.stack(g["pre_ffn"])[:, 0], jnp.stack(g["cw"]), jnp.stack(g["cb"]),
                  jnp.stack(g["post_ffn"])[:, 0]]
    small_sum = _unslab(_allreduce_small(_slab(small_full, 8), name="small_allreduce"), [a.shape for a in small_full])
    (g_meta, g_pre_mix, g_w2, g_gb, g_ret_n, g_gla_n, g_post_mix, g_pre_ffn, g_cw, g_cb, g_post_ffn) = small_sum
    g_meta = lax.dynamic_slice_in_dim(g_meta, chip * 256, 256, axis=1)
    g_w2 = lax.dynamic_slice_in_dim(g_w2, chip * 64, 64, axis=2)
    g_cw = lax.dynamic_slice_in_dim(g_cw, chip * 1408, 1408, axis=2)

    grads = [g_meta, g_pre_mix, None, g_w2, g_gb, g_ret_n, g_gla_n, None, g_post_mix, g_pre_ffn, None,
             g_cw, g_cb, None, g_post_ffn]
    ws = [meta_tokens, pre_mix_norm, w_in, gla_gate_w2, gla_gate_b, ret_norm_w, gla_norm_w, w_out, post_mix_norm,
          pre_ffn_norm, ffn_up, ffn_conv_w, ffn_conv_b, ffn_down, post_ffn_norm]
    ms = [m_meta_tokens, m_pre_mix_norm, m_w_in, m_gla_gate_w2, m_gla_gate_b, m_ret_norm_w, m_gla_norm_w, m_w_out,
          m_post_mix_norm, m_pre_ffn_norm, m_ffn_up, m_ffn_conv_w, m_ffn_conv_b, m_ffn_down, m_post_ffn_norm]
    vs = [v_meta_tokens, v_pre_mix_norm, v_w_in, v_gla_gate_w2, v_gla_gate_b, v_ret_norm_w, v_gla_norm_w, v_w_out,
          v_post_mix_norm, v_pre_ffn_norm, v_ffn_up, v_ffn_conv_w, v_ffn_conv_b, v_ffn_down, v_post_ffn_norm]
    big_idx = (2, 7, 10, 13)
    deltas, new_m, new_v = [None] * 15, [None] * 15, [None] * 15
    for i, nm, (mine, theirs) in zip(big_idx, names, big_halves):
        grads[i], deltas[i], new_m[i], new_v[i] = _adamw_halves(ws[i], ms[i], vs[i], mine, theirs, ci,
                                                                name=f"adamw_{nm}")
    small_idx = [i for i in range(15) if i not in big_idx]
    shapes = [ws[i].shape for i in small_idx]
    sd, sm, sv = _adamw(_slab([ws[i] for i in small_idx], 8), _slab([grads[i] for i in small_idx], 8),
                        _slab([ms[i] for i in small_idx], 8), _slab([vs[i] for i in small_idx], 8), name="adamw_small")
    for i, a, b, c_ in zip(small_idx, _unslab(sd, shapes), _unslab(sm, shapes), _unslab(sv, shapes)):
        deltas[i], new_m[i], new_v[i] = a, b, c_

    loss = lax.psum(loss_local, ("x", "y", "c"))
    return (loss, grad_x, *grads, *deltas, *new_m, *new_v)
```

```python
import functools
import math

import numpy as np
import jax
import jax.numpy as jnp
from jax import lax
from jax.experimental import pallas as pl
from jax.experimental.pallas import tpu as pltpu

F32 = jnp.float32
BF16 = jnp.bfloat16

D_MODEL = 1024
DEPTH = 2
N_META = 16
EPS = 1e-6
RET_HEADS = 4
RET_DK = 128
GLA_HEADS = 4
GLA_DK = 64
GLA_DV = 128
GLA_QK = GLA_HEADS * GLA_DK
GLA_V = GLA_HEADS * GLA_DV
GLA_RANK = 16
GLA_TAU = 16.0
D_FF = 2816
ROPE_BASE = 10000.0
IN_WIDTH = 3600
IN_PAD = 3840
C_RQ, C_RK, C_RV, C_RG, C_GR, C_GQ, C_GK, C_GV, C_GA = 0, 512, 1024, 1536, 2048, 2560, 2816, 3072, 3584
P_RET, P_GATE, P_GLA = 1536, 1024, 1280


def _to_kernel_cols(w):
    pad = jnp.zeros(w.shape[:-1] + (IN_PAD - IN_WIDTH,), w.dtype)
    return jnp.concatenate([w[..., 0:2048], w[..., 3072:3584], w[..., 2048:3072], w[..., 3584:3600], pad], axis=-1)


def _to_reference_chips(d_ret, d_gate, d_gla):
    segs = [(d_ret, 0, 0, 1536), (d_gate, 0, 1536, 512), (d_gla, 0, 2048, 1024), (d_gate, 512, 3072, 512),
            (d_gla, 1024, 3584, GLA_RANK)]
    per = IN_WIDTH // 4
    chips = []
    for j in range(4):
        lo, hi, parts = per * j, per * (j + 1), []
        for piece, p0, r0, width in segs:
            a, b = max(lo, r0), min(hi, r0 + width)
            if a < b:
                parts.append(piece[:, p0 + a - r0:p0 + b - r0])
        chips.append(jnp.concatenate(parts, axis=1))
    return jnp.stack(chips)

FRONT = 64
BACK = 64
PADF = FRONT - N_META
RET_CHUNK = 128
GLA_CHUNK = 64
GLA_SUB = 16
GLA_SUB2 = 4
BLK = 640

ADAM_LR, ADAM_B1, ADAM_B2, ADAM_EPS, ADAM_WD, ADAM_STEP = 0.001, 0.9, 0.999, 1e-08, 0.01, 10

VMEM_LIMIT = 56 * 2 ** 20
MM_VMEM_BUDGET = 40 * 2 ** 20
MESH = pl.DeviceIdType.MESH


def _cp(*sem):
    return pltpu.CompilerParams(dimension_semantics=sem, vmem_limit_bytes=VMEM_LIMIT)


def _tile(n, cands):
    for t in cands:
        if n % t == 0:
            return t
    raise ValueError(f"no tile for {n} in {cands}")


def _row_tile(n):
    return _tile(n, (640, 512, 320, 256, 128, 64))


def _mm(a, b, *, nt=False, add=None, out_dtype=F32, tn=None, name, carry=None):
    m, k = a.shape
    n = b.shape[0] if nt else b.shape[1]
    tm = _tile(m, (640, 320, 256, 128, 64))
    if tn is None:
        step_bytes = lambda t: 2 * (tm * k * a.dtype.itemsize + t * k * b.dtype.itemsize
                                    + tm * t * (jnp.dtype(out_dtype).itemsize + (4 if add is not None else 0)))
        tn = next(t for t in range(n, 0, -128) if n % t == 0 and (step_bytes(t) <= MM_VMEM_BUDGET or t == 128))
    dn = (((1,), (1,)), ((), ())) if nt else (((1,), (0,)), ((), ()))
    nj, ni = n // tn, m // tm
    n_in = 2 + (add is not None)
    c_arrs, c_mode = carry if carry is not None else ((), None)
    nc = len(c_arrs)

    def body(*refs):
        a_ref, b_ref = refs[:2]
        c_ref = refs[2] if add is not None else None
        o_ref = refs[n_in + nc]
        if nc:
            c_ins, c_outs = refs[n_in:n_in + nc], refs[n_in + nc + 1:n_in + 2 * nc + 1]
            ssem, rsem = refs[n_in + 2 * nc + 1:]
            j, i = pl.program_id(0), pl.program_id(1)

            @pl.when((j == 0) & (i == 0))
            def _():
                for cp in _chip_copies(c_ins, c_outs, ssem, rsem, c_mode)[0]:
                    cp.start()
        r = lax.dot_general(a_ref[...].astype(BF16), b_ref[...].astype(BF16), dn, preferred_element_type=F32)
        if add is not None:
            r = r + c_ref[...]
        o_ref[...] = r.astype(o_ref.dtype)
        if nc:
            @pl.when((j == nj - 1) & (i == ni - 1))
            def _():
                _chip_wait(*_chip_copies(c_ins, c_outs, ssem, rsem, c_mode))

    b_spec = pl.BlockSpec((tn, k), lambda j, i: (j, 0)) if nt else pl.BlockSpec((k, tn), lambda j, i: (0, j))
    in_specs = [pl.BlockSpec((tm, k), lambda j, i: (i, 0)), b_spec]
    args = [a, b]
    if add is not None:
        in_specs.append(pl.BlockSpec((tm, tn), lambda j, i: (i, j)))
        args.append(add)
    out_shape = jax.ShapeDtypeStruct((m, n), out_dtype)
    out_spec = pl.BlockSpec((tm, tn), lambda j, i: (i, j))
    if not nc:
        return pl.pallas_call(
            body, out_shape=out_shape, grid=(nj, ni), in_specs=in_specs, out_specs=out_spec,
            compiler_params=_cp("parallel", "parallel"), name=name)(*args)
    sem = pltpu.SemaphoreType.DMA
    outs = pl.pallas_call(
        body, out_shape=(out_shape,) + tuple(_landing_shape(x, c_mode) for x in c_arrs), grid=(nj, ni),
        in_specs=in_specs + [ANY] * nc, out_specs=(out_spec,) + (ANY,) * nc,
        scratch_shapes=[sem((nc, 3)), sem((nc, 3))],
        compiler_params=_cp("arbitrary", "arbitrary"), name=name)(*args, *c_arrs)
    return outs[0], list(outs[1:])


def _call_with_carry(body, *, out_shape, grid, in_specs, out_specs, args, semantics, carry, name, aliases=None):
    if carry is None:
        return pl.pallas_call(body, out_shape=out_shape, grid=grid, in_specs=in_specs, out_specs=out_specs,
                              input_output_aliases=aliases or {}, compiler_params=_cp(*semantics), name=name)(*args)
    c_arrs, c_mode = carry
    n_in, nc = len(args), len(c_arrs)

    def carried(*refs):
        c_ins, c_outs = refs[n_in:n_in + nc], refs[n_in + nc + 1:n_in + 2 * nc + 1]
        ssem, rsem = refs[n_in + 2 * nc + 1:]
        ids = [pl.program_id(d) for d in range(len(grid))]
        first = functools.reduce(lambda u, v: u & v, [i == 0 for i in ids])
        last = functools.reduce(lambda u, v: u & v, [i == g - 1 for i, g in zip(ids, grid)])

        @pl.when(first)
        def _():
            for cp in _chip_copies(c_ins, c_outs, ssem, rsem, c_mode)[0]:
                cp.start()
        body(*refs[:n_in], refs[n_in + nc])

        @pl.when(last)
        def _():
            _chip_wait(*_chip_copies(c_ins, c_outs, ssem, rsem, c_mode))

    sem = pltpu.SemaphoreType.DMA
    outs = pl.pallas_call(
        carried, out_shape=(out_shape,) + tuple(_landing_shape(x, c_mode) for x in c_arrs), grid=grid,
        in_specs=list(in_specs) + [ANY] * nc, out_specs=(out_specs,) + (ANY,) * nc,
        scratch_shapes=[sem((nc, 3)), sem((nc, 3))], input_output_aliases=aliases or {},
        compiler_params=_cp(*(("arbitrary",) * len(grid))), name=name)(*args, *c_arrs)
    return outs[0], list(outs[1:])


def _mm_nt_sum(a_list, b_list, *, name, carry=None):
    m, n = a_list[0].shape[0], b_list[0].shape[0]
    tm = _tile(m, (640, 320, 256, 128, 64))
    np_ = len(a_list)

    def body(*refs):
        acc = None
        for a_ref, b_ref in zip(refs[:np_], refs[np_:2 * np_]):
            r = lax.dot_general(a_ref[...].astype(BF16), b_ref[...].astype(BF16), (((1,), (1,)), ((), ())),
                                preferred_element_type=F32)
            acc = r if acc is None else acc + r
        refs[2 * np_][...] = acc

    return _call_with_carry(
        body, out_shape=jax.ShapeDtypeStruct((m, n), F32), grid=(m // tm,),
        in_specs=[pl.BlockSpec((tm, a.shape[1]), lambda i: (i, 0)) for a in a_list]
        + [pl.BlockSpec(b.shape, lambda i: (0, 0)) for b in b_list],
        out_specs=pl.BlockSpec((tm, n), lambda i: (i, 0)), args=[*a_list, *b_list], semantics=("parallel",),
        carry=carry, name=name)


def _mm_tn(a, b, *, tn=None, blocks=None, into=None, name, carry=None):
    m, k = a.shape
    n = b.shape[1]
    tm = _tile(m, (1664, 640, 320, 256, 128, 64))
    tn = n if tn is None else tn
    if blocks is not None:
        total, first = blocks
        out_shape = jax.ShapeDtypeStruct((total, k, tn), F32)
        out_spec = pl.BlockSpec((None, k, tn), lambda j, i: (first + j, 0, 0))
    else:
        out_shape = jax.ShapeDtypeStruct((k, n), F32)
        out_spec = pl.BlockSpec((k, tn), lambda j, i: (0, j))

    def body(a_ref, b_ref, *rest):
        o_ref = rest[-1]

        @pl.when(pl.program_id(1) == 0)
        def _():
            o_ref[...] = jnp.zeros_like(o_ref)
        o_ref[...] += lax.dot_general(a_ref[...].astype(BF16), b_ref[...].astype(BF16),
                                      (((0,), (0,)), ((), ())), preferred_element_type=F32)

    in_specs = [pl.BlockSpec((tm, k), lambda j, i: (i, 0)), pl.BlockSpec((tm, tn), lambda j, i: (i, j))]
    args, alias = [a, b], {}
    if into is not None:
        in_specs.append(pl.BlockSpec(memory_space=pl.ANY))
        args.append(into)
        alias = {2: 0}
    return _call_with_carry(body, out_shape=out_shape, grid=(n // tn, m // tm), in_specs=in_specs, out_specs=out_spec,
                            args=args, semantics=("parallel", "arbitrary"), carry=carry, name=name, aliases=alias)


def _rms(x, w):
    r = lax.rsqrt(jnp.mean(x * x, axis=-1, keepdims=True) + EPS)
    return x * r * w


def _rms_bwd(x, w, dy):
    r = lax.rsqrt(jnp.mean(x * x, axis=-1, keepdims=True) + EPS)
    xh = x * r
    dxh = dy * w
    dx = r * (dxh - xh * jnp.mean(dxh * xh, axis=-1, keepdims=True))
    return dx, jnp.sum(dy * xh, axis=0, keepdims=True)


def _resid_norm(h, t, w_post, w_next, *, name):
    lp, d = h.shape
    tm = _row_tile(lp)
    has_t = t is not None

    def body(*refs):
        if has_t:
            h_ref, t_ref, wp_ref, wn_ref, ho_ref, hn_ref = refs
            hv = h_ref[...] + _rms(t_ref[...], wp_ref[...])
            ho_ref[...] = hv
        else:
            h_ref, wn_ref, hn_ref = refs
            hv = h_ref[...]
        hn_ref[...] = _rms(hv, wn_ref[...]).astype(BF16)

    row = pl.BlockSpec((tm, d), lambda i: (i, 0))
    vec = pl.BlockSpec((1, d), lambda i: (0, 0))
    if has_t:
        return pl.pallas_call(
            body, out_shape=(jax.ShapeDtypeStruct((lp, d), F32), jax.ShapeDtypeStruct((lp, d), BF16)),
            grid=(lp // tm,), in_specs=[row, row, vec, vec], out_specs=(row, row),
            compiler_params=_cp("parallel"), name=name)(h, t, w_post, w_next)
    return h, pl.pallas_call(
        body, out_shape=jax.ShapeDtypeStruct((lp, d), BF16), grid=(lp // tm,), in_specs=[row, vec],
        out_specs=row, compiler_params=_cp("parallel"), name=name)(h, w_next)


def _resid_norm_bwd(dh_out, dhn, h_new, t, w_next, w_post, *, name):
    lp, d = h_new.shape if h_new is not None else t.shape
    tm = _row_tile(lp)
    has_n = dhn is not None
    has_t = t is not None

    def body(*refs):
        refs = list(refs)
        dho_ref = refs.pop(0)
        if has_n:
            dhn_ref, hn_ref, wn_ref = refs.pop(0), refs.pop(0), refs.pop(0)
        if has_t:
            t_ref, wp_ref = refs.pop(0), refs.pop(0)
        dh_ref = refs.pop(0) if has_n else None
        dt_ref = refs.pop(0) if has_t else None
        dwn_ref = refs.pop(0) if has_n else None
        dwp_ref = refs.pop(0) if has_t else None
        first = pl.program_id(0) == 0
        dh = dho_ref[...]
        if has_n:
            dx, dwn = _rms_bwd(hn_ref[...], wn_ref[...], dhn_ref[...])
            dh = dh + dx
            dh_ref[...] = dh

            @pl.when(first)
            def _():
                dwn_ref[...] = jnp.zeros_like(dwn_ref)
            dwn_ref[...] += dwn
        if has_t:
            dt, dwp = _rms_bwd(t_ref[...], wp_ref[...], dh)
            dt_ref[...] = dt.astype(BF16)

            @pl.when(first)
            def _():
                dwp_ref[...] = jnp.zeros_like(dwp_ref)
            dwp_ref[...] += dwp

    row = pl.BlockSpec((tm, d), lambda i: (i, 0))
    vec = pl.BlockSpec((1, d), lambda i: (0, 0))
    args, in_specs, out_shape, out_specs = [dh_out], [row], [], []
    if has_n:
        args += [dhn, h_new, w_next]
        in_specs += [row, row, vec]
    if has_t:
        args += [t, w_post]
        in_specs += [row, vec]
    if has_n:
        out_shape.append(jax.ShapeDtypeStruct((lp, d), F32)); out_specs.append(row)
    if has_t:
        out_shape.append(jax.ShapeDtypeStruct((lp, d), BF16)); out_specs.append(row)
    if has_n:
        out_shape.append(jax.ShapeDtypeStruct((1, d), F32)); out_specs.append(vec)
    if has_t:
        out_shape.append(jax.ShapeDtypeStruct((1, d), F32)); out_specs.append(vec)
    outs = list(pl.pallas_call(body, out_shape=tuple(out_shape), grid=(lp // tm,), in_specs=in_specs,
                               out_specs=tuple(out_specs), compiler_params=_cp("arbitrary"), name=name)(*args))
    dh = outs.pop(0) if has_n else dh_out
    dt = outs.pop(0) if has_t else None
    dwn = outs.pop(0) if has_n else None
    dwp = outs.pop(0) if has_t else None
    return dh, dt, dwn, dwp


def _loss_head(h, f, w_post, target, *, name):
    lp, d = h.shape
    tm = _row_tile(lp)

    def body(h_ref, f_ref, w_ref, t_ref, loss_ref, dy_ref, df_ref, dw_ref):
        i = pl.program_id(0)
        f, w = f_ref[...], w_ref[...]
        y = h_ref[...] + _rms(f, w)
        rows = i * tm + lax.broadcasted_iota(jnp.int32, (tm, 1), 0)
        tok = (rows >= FRONT) & (rows < lp - BACK)
        err = jnp.where(tok, y - t_ref[...], 0.0)
        dy = err * (1.0 / d)
        dy_ref[...] = dy
        df, dw = _rms_bwd(f, w, dy)
        df_ref[...] = df.astype(BF16)

        @pl.when(i == 0)
        def _():
            loss_ref[...] = jnp.zeros_like(loss_ref)
            dw_ref[...] = jnp.zeros_like(dw_ref)
        part = jnp.sum(jnp.sum(err * err, axis=1, keepdims=True), axis=0, keepdims=True) * (0.5 / d)
        loss_ref[...] += jnp.broadcast_to(part, loss_ref.shape)
        dw_ref[...] += dw

    row = pl.BlockSpec((tm, d), lambda i: (i, 0))
    vec = pl.BlockSpec((1, d), lambda i: (0, 0))
    loss, dy, df, dw = pl.pallas_call(
        body, out_shape=(jax.ShapeDtypeStruct((8, 128), F32), jax.ShapeDtypeStruct((lp, d), F32),
                         jax.ShapeDtypeStruct((lp, d), BF16), jax.ShapeDtypeStruct((1, d), F32)),
        grid=(lp // tm,), in_specs=[row, row, vec, row],
        out_specs=(pl.BlockSpec((8, 128), lambda i: (0, 0)), row, row, vec),
        compiler_params=_cp("arbitrary"), name=name)(h, f, w_post, target)
    return loss[0, 0], dy, df, dw


_GELU_C = math.sqrt(2.0 / math.pi)


def _gelu_and_grad(a):
    a2 = a * a
    t = jnp.tanh(a * (_GELU_C + (_GELU_C * 0.044715) * a2))
    ha = 0.5 * a
    h1 = 0.5 + 0.5 * t
    return a * h1, h1 + ha * (1.0 - t * t) * (_GELU_C + (3.0 * _GELU_C * 0.044715) * a2)


def _gelu(a):
    t = jnp.tanh(a * (_GELU_C + (_GELU_C * 0.044715) * (a * a)))
    return a * (0.5 + 0.5 * t)


def _conv3(parts, n, w, b):
    xx = jnp.concatenate(parts, axis=0)
    return b + xx[8:8 + n] * w[2:3] + pltpu.roll(xx, 1, 0)[8:8 + n] * w[1:2] + pltpu.roll(xx, 2, 0)[8:8 + n] * w[0:1]


def _conv_act(ua, ug, wa, wg, ba, bg, *, name):
    lp, n = ua.shape
    tm = _row_tile(lp)
    tc = _tile(n, (256, 128))
    nb8 = tm // 8

    def body(ua_ref, uap_ref, ug_ref, ugp_ref, wa_ref, wg_ref, ba_ref, bg_ref, o_ref):
        i = pl.program_id(0)
        ca = _conv3([uap_ref[...], ua_ref[...]], tm, wa_ref[...], ba_ref[...])
        cg = _conv3([ugp_ref[...], ug_ref[...]], tm, wg_ref[...], bg_ref[...])
        rows = i * tm + lax.broadcasted_iota(jnp.int32, (tm, 1), 0)
        ok = (rows >= PADF) & (rows < lp - BACK)
        o_ref[...] = jnp.where(ok, _gelu(ca) * cg, 0.0).astype(BF16)

    cur = pl.BlockSpec((tm, tc), lambda i, j: (i, j))
    prev = pl.BlockSpec((8, tc), lambda i, j: (jnp.maximum(i * nb8 - 1, 0), j))
    w3 = pl.BlockSpec((3, tc), lambda i, j: (0, j))
    b1 = pl.BlockSpec((1, tc), lambda i, j: (0, j))
    return pl.pallas_call(
        body, out_shape=jax.ShapeDtypeStruct((lp, n), BF16), grid=(lp // tm, n // tc),
        in_specs=[cur, prev, cur, prev, w3, w3, b1, b1], out_specs=cur,
        compiler_params=_cp("parallel", "parallel"), name=name)(ua, ua, ug, ug, wa, wg, ba, bg)


def _conv_act_down(ua, ug, wa, wg, ba, bg, down, *, name):
    lp, n = ua.shape
    d = down.shape[1]
    tm = _tile(lp, (320, 256, 128, 64))
    tc = _tile(n, (256, 128))
    nb8 = tm // 8

    def body(ua_ref, uap_ref, ug_ref, ugp_ref, wa_ref, wg_ref, ba_ref, bg_ref, dn_ref, act_ref, f_ref):
        i = pl.program_id(0)
        rows = i * tm + lax.broadcasted_iota(jnp.int32, (tm, 1), 0)
        ok = (rows >= PADF) & (rows < lp - BACK)
        acc = None
        for j in range(n // tc):
            cs = slice(tc * j, tc * j + tc)
            ca = _conv3([uap_ref[:, cs], ua_ref[:, cs]], tm, wa_ref[:, cs], ba_ref[:, cs])
            cg = _conv3([ugp_ref[:, cs], ug_ref[:, cs]], tm, wg_ref[:, cs], bg_ref[:, cs])
            act = jnp.where(ok, _gelu(ca) * cg, 0.0).astype(BF16)
            act_ref[:, cs] = act
            part = _dot(act, dn_ref[cs, :])
            acc = part if acc is None else acc + part
        f_ref[...] = acc

    cur = pl.BlockSpec((tm, n), lambda i: (i, 0))
    prev = pl.BlockSpec((8, n), lambda i: (jnp.maximum(i * nb8 - 1, 0), 0))
    w3 = pl.BlockSpec((3, n), lambda i: (0, 0))
    b1 = pl.BlockSpec((1, n), lambda i: (0, 0))
    return pl.pallas_call(
        body, out_shape=(jax.ShapeDtypeStruct((lp, n), BF16), jax.ShapeDtypeStruct((lp, d), F32)),
        grid=(lp // tm,),
        in_specs=[cur, prev, cur, prev, w3, w3, b1, b1, pl.BlockSpec(down.shape, lambda i: (0, 0))],
        out_specs=(cur, pl.BlockSpec((tm, d), lambda i: (i, 0))),
        compiler_params=_cp("parallel"), name=name)(ua, ua, ug, ug, wa, wg, ba, bg, down)


def _conv_act_bwd(ua, ug, dact, wa, wg, ba, bg, *, name):
    lp, n = ua.shape
    tm = _row_tile(lp)
    tc = _tile(n, (256, 128))
    nb8 = tm // 8
    last8 = lp // 8 - 1
    ext = tm + 8

    def body(ua_ref, uap_ref, uan_ref, ug_ref, ugp_ref, ugn_ref, da_ref, dan_ref, wa_ref, wg_ref, ba_ref, bg_ref,
             dua_ref, dug_ref, dwa_ref, dwg_ref, dba_ref, dbg_ref):
        i = pl.program_id(1)
        wa, wg = wa_ref[...], wg_ref[...]

        def conv(parts, w, b):
            xx = jnp.concatenate(parts, axis=0)
            x, x1, x2 = xx[8:8 + ext], pltpu.roll(xx, 1, 0)[8:8 + ext], pltpu.roll(xx, 2, 0)[8:8 + ext]
            return b + x * w[2:3] + x1 * w[1:2] + x2 * w[0:1], x, x1, x2

        ca, xa, xa1, xa2 = conv([uap_ref[...], ua_ref[...], uan_ref[...]], wa, ba_ref[...])
        cg, xg, xg1, xg2 = conv([ugp_ref[...], ug_ref[...], ugn_ref[...]], wg, bg_ref[...])
        rows = i * tm + lax.broadcasted_iota(jnp.int32, (ext, 1), 0)
        ok = (rows >= PADF) & (rows < lp - BACK)
        dact_e = jnp.where(ok, jnp.concatenate([da_ref[...], dan_ref[...]], axis=0), 0.0)
        gel, gel_d = _gelu_and_grad(ca)
        dca = dact_e * cg * gel_d
        dcg = dact_e * gel

        def back(dc, w):
            return (dc[:tm] * w[2:3] + pltpu.roll(dc, ext - 1, 0)[:tm] * w[1:2]
                    + pltpu.roll(dc, ext - 2, 0)[:tm] * w[0:1])

        dua_ref[...] = back(dca, wa).astype(BF16)
        dug_ref[...] = back(dcg, wg).astype(BF16)

        @pl.when(i == 0)
        def _():
            dwa_ref[...] = jnp.zeros_like(dwa_ref)
            dwg_ref[...] = jnp.zeros_like(dwg_ref)
            dba_ref[...] = jnp.zeros_like(dba_ref)
            dbg_ref[...] = jnp.zeros_like(dbg_ref)

        def wsum(dw_ref, db_ref, dc, x, x1, x2):
            d = dc[:tm]
            s = lambda v: jnp.sum(v, axis=0, keepdims=True)
            dw_ref[0:1, :] += s(d * x2[:tm])
            dw_ref[1:2, :] += s(d * x1[:tm])
            dw_ref[2:3, :] += s(d * x[:tm])
            db_ref[...] += s(d)

        wsum(dwa_ref, dba_ref, dca, xa, xa1, xa2)
        wsum(dwg_ref, dbg_ref, dcg, xg, xg1, xg2)

    cur = pl.BlockSpec((tm, tc), lambda j, i: (i, j))
    prev = pl.BlockSpec((8, tc), lambda j, i: (jnp.maximum(i * nb8 - 1, 0), j))
    nxt = pl.BlockSpec((8, tc), lambda j, i: (jnp.minimum((i + 1) * nb8, last8), j))
    w3 = pl.BlockSpec((3, tc), lambda j, i: (0, j))
    b1 = pl.BlockSpec((1, tc), lambda j, i: (0, j))
    return pl.pallas_call(
        body,
        out_shape=(jax.ShapeDtypeStruct((lp, n), BF16), jax.ShapeDtypeStruct((lp, n), BF16),
                   jax.ShapeDtypeStruct((3, n), F32), jax.ShapeDtypeStruct((3, n), F32),
                   jax.ShapeDtypeStruct((1, n), F32), jax.ShapeDtypeStruct((1, n), F32)),
        grid=(n // tc, lp // tm),
        in_specs=[cur, prev, nxt, cur, prev, nxt, cur, nxt, w3, w3, b1, b1],
        out_specs=(cur, cur, w3, w3, b1, b1),
        compiler_params=_cp("parallel", "arbitrary"), name=name)(ua, ua, ua, ug, ug, ug, dact, dact, wa, wg, ba, bg)


def _sigmoid(x):
    return 1.0 / (1.0 + jnp.exp(-x))


def _merge(o_ret, o_gla, proj, w_ret, w_gla, *, name):
    lp = o_ret.shape[0]
    tm = _row_tile(lp)

    def body(or_ref, og_ref, rg_ref, gr_ref, wr_ref, wg_ref, m_ref):
        oret, ogla = or_ref[...], og_ref[...]
        yr, yg = [], []
        for h in range(4):
            hs = slice(128 * h, 128 * h + 128)
            o = oret[:, hs]
            xc = o - jnp.mean(o, axis=-1, keepdims=True)
            yr.append(xc * lax.rsqrt(jnp.mean(xc * xc, axis=-1, keepdims=True) + EPS))
            o = ogla[:, hs]
            yg.append(o * lax.rsqrt(jnp.mean(o * o, axis=-1, keepdims=True) + EPS))
        rg, gr = rg_ref[...], gr_ref[...]
        m_ref[:, 0:512] = (jnp.concatenate(yr, axis=1) * wr_ref[...] * (rg * _sigmoid(rg))).astype(BF16)
        m_ref[:, 512:1024] = (jnp.concatenate(yg, axis=1) * wg_ref[...] * (gr * _sigmoid(gr))).astype(BF16)

    row = pl.BlockSpec((tm, 512), lambda i: (i, 0))
    vec = pl.BlockSpec((1, 512), lambda i: (0, 0))
    return pl.pallas_call(
        body, out_shape=jax.ShapeDtypeStruct((lp, 1024), BF16), grid=(lp // tm,),
        in_specs=[row, row, pl.BlockSpec((tm, 512), lambda i: (i, C_RG // 512)),
                  pl.BlockSpec((tm, 512), lambda i: (i, C_GR // 512)), vec, vec],
        out_specs=pl.BlockSpec((tm, 1024), lambda i: (i, 0)),
        compiler_params=_cp("parallel"), name=name)(o_ret, o_gla, proj, proj, w_ret, w_gla)


def _merge_bwd(dm, o_ret, o_gla, proj, w_ret, w_gla, *, name):
    lp = o_ret.shape[0]
    tm = _row_tile(lp)

    def body(dm_ref, or_ref, og_ref, rg_ref, gr_ref, wr_ref, wg_ref, dor_ref, dog_ref, dgate_ref, dwr_ref, dwg_ref):
        @pl.when(pl.program_id(0) == 0)
        def _():
            dwr_ref[...] = jnp.zeros_like(dwr_ref)
            dwg_ref[...] = jnp.zeros_like(dwg_ref)

        def group(d, o_all, gate, w, center):
            sg = _sigmoid(gate)
            s = gate * sg
            ds = sg * (1.0 + gate * (1.0 - sg))
            xh, rr = [], []
            for h in range(4):
                o = o_all[:, 128 * h:128 * h + 128]
                if center:
                    o = o - jnp.mean(o, axis=-1, keepdims=True)
                r = lax.rsqrt(jnp.mean(o * o, axis=-1, keepdims=True) + EPS)
                xh.append(o * r)
                rr.append(r)
            xh_all = jnp.concatenate(xh, axis=1)
            dgate = d * xh_all * w * ds
            dw = jnp.sum(d * xh_all * s, axis=0, keepdims=True)
            dxh_all = d * w * s
            do = []
            for h in range(4):
                dxh = dxh_all[:, 128 * h:128 * h + 128]
                t = dxh - xh[h] * jnp.mean(dxh * xh[h], axis=-1, keepdims=True)
                if center:
                    t = t - jnp.mean(dxh, axis=-1, keepdims=True)
                do.append(rr[h] * t)
            return jnp.concatenate(do, axis=1), dgate, dw

        dmv = dm_ref[...]
        do, dg, dw = group(dmv[:, 0:512], or_ref[...], rg_ref[...], wr_ref[...], True)
        dor_ref[...] = do
        dgate_ref[:, 0:512] = dg.astype(BF16)
        dwr_ref[...] += dw
        do, dg, dw = group(dmv[:, 512:1024], og_ref[...], gr_ref[...], wg_ref[...], False)
        dog_ref[...] = do
        dgate_ref[:, 512:1024] = dg.astype(BF16)
        dwg_ref[...] += dw

    row = pl.BlockSpec((tm, 512), lambda i: (i, 0))
    vec = pl.BlockSpec((1, 512), lambda i: (0, 0))
    return pl.pallas_call(
        body,
        out_shape=(jax.ShapeDtypeStruct((lp, 512), F32), jax.ShapeDtypeStruct((lp, 512), F32),
                   jax.ShapeDtypeStruct((lp, P_GATE), BF16),
                   jax.ShapeDtypeStruct((1, 512), F32), jax.ShapeDtypeStruct((1, 512), F32)),
        grid=(lp // tm,),
        in_specs=[pl.BlockSpec((tm, 1024), lambda i: (i, 0)), row, row,
                  pl.BlockSpec((tm, 512), lambda i: (i, C_RG // 512)),
                  pl.BlockSpec((tm, 512), lambda i: (i, C_GR // 512)), vec, vec],
        out_specs=(row, row, pl.BlockSpec((tm, P_GATE), lambda i: (i, 0)), vec, vec),
        compiler_params=_cp("arbitrary"), name=name)(dm, o_ret, o_gla, proj, proj, w_ret, w_gla)


def _dot(a, b):
    return lax.dot_general(a, b, (((1,), (0,)), ((), ())), preferred_element_type=F32)


def _dot_nt(a, b):
    return lax.dot_general(a, b, (((1,), (1,)), ((), ())), preferred_element_type=F32)


def _dot_tn(a, b):
    return lax.dot_general(a, b, (((0,), (0,)), ((), ())), preferred_element_type=F32)


def _ret_tables(lp):
    cr = RET_CHUNK
    pos = np.arange(lp, dtype=np.float32) - np.float32(PADF)
    half = RET_DK // 2
    inv = (np.float32(ROPE_BASE) ** (-np.arange(half, dtype=np.float32) / np.float32(half))).astype(np.float32)
    ang = (pos[:, None] * inv[None, :]).astype(np.float32)
    c, s = np.cos(ang).astype(np.float32), np.sin(ang).astype(np.float32)
    rope_c = jnp.asarray(np.concatenate([c, c], axis=1))
    rope_s = jnp.asarray(np.concatenate([-s, s], axis=1))
    log_g = np.log(1.0 - 2.0 ** (-5.0 - np.arange(RET_HEADS, dtype=np.float64)))
    idx = np.arange(cr, dtype=np.float64)
    diff = idx[:, None] - idx[None, :]
    dmat = np.where(diff >= 0, np.exp(log_g[:, None, None] * np.maximum(diff, 0.0)), 0.0)
    zeta = np.exp(log_g[:, None] * (cr - 1.0 - idx)[None, :])
    xi = np.exp(log_g[:, None] * (idx + 1.0)[None, :])
    gc = np.exp(log_g * cr)
    f = lambda a: jnp.asarray(a.astype(np.float32))
    return (rope_c, rope_s, f(dmat), f(np.broadcast_to(zeta[:, :, None], (RET_HEADS, cr, 128))),
            f(np.broadcast_to(xi[:, :, None], (RET_HEADS, cr, 128))),
            f(np.broadcast_to(gc[:, None, None], (RET_HEADS, 8, 128))))


def _rope(t, c, s):
    return t * c + pltpu.roll(t, 64, 1) * s


def _rope_t(d, c, s):
    return d * c + pltpu.roll(d * s, 64, 1)


def _ret_specs(nblk, rev):
    ix = (lambda i: nblk - 1 - i) if rev else (lambda i: i)
    cr = RET_CHUNK
    col = lambda base: pl.BlockSpec((BLK, 512), lambda i: (ix(i), base // 512))
    tab = pl.BlockSpec((BLK, 128), lambda i: (ix(i), 0))
    sq = pl.BlockSpec((RET_HEADS, cr, cr), lambda i: (0, 0, 0))
    hv = pl.BlockSpec((RET_HEADS, cr, 128), lambda i: (0, 0, 0))
    g8 = pl.BlockSpec((RET_HEADS, 8, 128), lambda i: (0, 0, 0))
    st = pl.BlockSpec((RET_HEADS, BLK // cr, 128, 128), lambda i: (0, ix(i), 0, 0))
    out = pl.BlockSpec((BLK, 512), lambda i: (ix(i), 0))
    return col, tab, sq, hv, g8, st, out


def _retention(proj, tables, *, name):
    lp = proj.shape[0]
    nblk, cr = lp // BLK, RET_CHUNK
    scale = RET_DK ** -0.5

    def body(q_ref, k_ref, v_ref, c_ref, s_ref, d_ref, z_ref, x_ref, g_ref, o_ref, st_ref, state):
        @pl.when(pl.program_id(0) == 0)
        def _():
            state[...] = jnp.zeros_like(state)

        def chunk(ci, carry):
            sl = pl.ds(pl.multiple_of(ci * cr, cr), cr)
            c, s = c_ref[sl, :], s_ref[sl, :]
            for h in range(RET_HEADS):
                hs = slice(128 * h, 128 * h + 128)
                q = _rope(q_ref[sl, hs], c, s)
                k = _rope(k_ref[sl, hs], c, s) * scale
                qb, kb, vb = q.astype(BF16), k.astype(BF16), v_ref[sl, hs].astype(BF16)
                st = state[h]
                st_ref[h, ci] = st
                sc = _dot_nt(qb, kb) * d_ref[h]
                o_ref[sl, hs] = _dot(sc.astype(BF16), vb) + _dot(qb, st.astype(BF16)) * x_ref[h]
                state[h] = st * g_ref[h][0:1, :] + _dot_tn((k * z_ref[h]).astype(BF16), vb)
            return carry

        lax.fori_loop(0, BLK // cr, chunk, 0)

    col, tab, sq, hv, g8, st, out = _ret_specs(nblk, False)
    return pl.pallas_call(
        body,
        out_shape=(jax.ShapeDtypeStruct((lp, 512), F32), jax.ShapeDtypeStruct((4, lp // cr, 128, 128), F32)),
        grid=(nblk,), in_specs=[col(C_RQ), col(C_RK), col(C_RV), tab, tab, sq, hv, hv, g8],
        out_specs=(out, st), scratch_shapes=[pltpu.VMEM((RET_HEADS, 128, 128), F32)],
        compiler_params=_cp("arbitrary"), name=name)(proj, proj, proj, *tables)


def _retention_bwd(proj, do, states, tables, *, name):
    lp = proj.shape[0]
    nblk, cr = lp // BLK, RET_CHUNK
    nch = BLK // cr
    scale = RET_DK ** -0.5

    def body(q_ref, k_ref, v_ref, do_ref, st_ref, c_ref, s_ref, d_ref, z_ref, x_ref, g_ref, dqkv_ref, dstate):
        @pl.when(pl.program_id(0) == 0)
        def _():
            dstate[...] = jnp.zeros_like(dstate)

        def chunk(cc, carry):
            ci = nch - 1 - cc
            sl = pl.ds(pl.multiple_of(ci * cr, cr), cr)
            c, s = c_ref[sl, :], s_ref[sl, :]
            for h in range(RET_HEADS):
                hs = slice(128 * h, 128 * h + 128)
                dmat, zeta, xi = d_ref[h], z_ref[h], x_ref[h]
                q = _rope(q_ref[sl, hs], c, s)
                k = _rope(k_ref[sl, hs], c, s) * scale
                qb, kb, vb = q.astype(BF16), k.astype(BF16), v_ref[sl, hs].astype(BF16)
                kzb = (k * zeta).astype(BF16)
                dov = do_ref[sl, hs]
                dob, doxb = dov.astype(BF16), (dov * xi).astype(BF16)
                stb = st_ref[h, ci].astype(BF16)
                dsn = dstate[h]
                dsnb = dsn.astype(BF16)
                scb = (_dot_nt(qb, kb) * dmat).astype(BF16)
                dscb = (_dot_nt(dob, vb) * dmat).astype(BF16)
                dq = _dot(dscb, kb) + _dot_nt(doxb, stb)
                dk = _dot_tn(dscb, qb) + _dot_nt(vb, dsnb) * zeta
                dv = _dot_tn(scb, dob) + _dot(kzb, dsnb)
                dstate[h] = dsn * g_ref[h][0:1, :] + _dot_tn(qb, doxb)
                dqkv_ref[sl, 128 * h:128 * h + 128] = _rope_t(dq, c, s).astype(BF16)
                dqkv_ref[sl, 512 + 128 * h:640 + 128 * h] = _rope_t(dk * scale, c, s).astype(BF16)
                dqkv_ref[sl, 1024 + 128 * h:1152 + 128 * h] = dv.astype(BF16)
            return carry

        lax.fori_loop(0, nch, chunk, 0)

    col, tab, sq, hv, g8, st, out = _ret_specs(nblk, True)
    return pl.pallas_call(
        body, out_shape=jax.ShapeDtypeStruct((lp, P_RET), BF16), grid=(nblk,),
        in_specs=[col(C_RQ), col(C_RK), col(C_RV), out, st, tab, tab, sq, hv, hv, g8],
        out_specs=pl.BlockSpec((BLK, P_RET), lambda i: (nblk - 1 - i, 0)),
        scratch_shapes=[pltpu.VMEM((RET_HEADS, 128, 128), F32)],
        compiler_params=_cp("arbitrary"), name=name)(proj, proj, proj, do, states, *tables)


def _gla_tables():
    c = GLA_CHUNK
    tri = np.tril(np.ones((c, c), np.float32))
    ones_qv = np.kron(np.eye(GLA_HEADS, dtype=np.float32), np.ones((GLA_DK, GLA_DV), np.float32))
    return (jnp.asarray(tri, BF16), jnp.asarray(tri.T.copy(), BF16), jnp.asarray(ones_qv, BF16),
            jnp.asarray(ones_qv.T.copy(), BF16))


def _split3(x):
    hi = x.astype(BF16)
    r1 = x - hi.astype(F32)
    mid = r1.astype(BF16)
    lo = (r1 - mid.astype(F32)).astype(BF16)
    return hi, mid, lo


def _tri_sum(tri, x):
    hi, mid, lo = _split3(x)
    return _dot(tri, hi) + _dot(tri, mid) + _dot(tri, lo)


def _head_masks(width, per):
    lane = lax.broadcasted_iota(jnp.int32, (1, width), 1)
    return [((lane >= per * h) & (lane < per * (h + 1))).astype(F32) for h in range(GLA_HEADS)]


def _stack_heads(x, masks):
    return jnp.concatenate([x * m for m in masks], axis=0)


def _gla_gate(ga, w2, b, ok, tri):
    z = _dot(ga.astype(BF16), w2) + b
    la = (jnp.minimum(z, 0.0) - jnp.log(1.0 + jnp.exp(-jnp.abs(z)))) * (1.0 / GLA_TAU)
    la = jnp.where(ok, la, 0.0)
    return z, _tri_sum(tri, la)


def _gla_rows(i_blk, ci, lp):
    c = GLA_CHUNK
    rows = i_blk * BLK + ci * c + lax.broadcasted_iota(jnp.int32, (c, 1), 0)
    return (rows >= PADF) & (rows < lp - BACK)


N_SUB = GLA_CHUNK // GLA_SUB - 1
N_SUB2 = GLA_SUB // GLA_SUB2 - 1


def _gla_masks():
    c, s1, s2 = GLA_CHUNK, GLA_SUB, GLA_SUB2
    sh1, sh2 = s1.bit_length() - 1, s2.bit_length() - 1
    r = lax.broadcasted_iota(jnp.int32, (c, GLA_QK), 0)
    blk, within = jnp.right_shift(r, sh1), jnp.bitwise_and(r, s1 - 1)
    grp = jnp.right_shift(within, sh2)
    rowm = [(blk == a).astype(F32) for a in range(1, N_SUB + 1)] + [(grp == b).astype(F32) for b in range(1, N_SUB2 + 1)]
    keym = ([(r < s1 * a).astype(F32) for a in range(1, N_SUB + 1)]
            + [(within < s2 * b).astype(F32) for b in range(1, N_SUB2 + 1)])
    rs = lax.broadcasted_iota(jnp.int32, (GLA_HEADS * c, c), 0)
    ts = lax.broadcasted_iota(jnp.int32, (GLA_HEADS * c, c), 1)
    same = (jnp.right_shift(jnp.bitwise_and(rs, c - 1), sh1) == jnp.right_shift(ts, sh1)).astype(F32)
    lag = [(jnp.bitwise_and(r, s2 - 1) >= j).astype(F32) for j in range(s2)]
    return rowm, keym, same, lag


def _gla_hats(qs, k, g, masks, hm_q):
    c, s1, s2 = GLA_CHUNK, GLA_SUB, GLA_SUB2
    rowm, keym, same, _ = masks
    refs = [g[s1 * a - 1:s1 * a, :] for a in range(1, N_SUB + 1)]
    for b in range(1, N_SUB2 + 1):
        refs.append(jnp.concatenate([jnp.broadcast_to(g[s1 * i + s2 * b - 1:s1 * i + s2 * b, :], (s1, GLA_QK))
                                     for i in range(c // s1)], axis=0))
    eqs = [jnp.exp(jnp.minimum(g - r, 0.0)) * m for r, m in zip(refs, rowm)]
    eks = [jnp.exp(jnp.minimum(r - g, 0.0)) * m for r, m in zip(refs, keym)]
    qhs, khs = [qs * e for e in eqs], [k * e for e in eks]
    qst = [_stack_heads(q, hm_q).astype(BF16) for q in qhs]
    khb = [x.astype(BF16) for x in khs]
    qa, qb = jnp.concatenate(qst[:N_SUB], axis=1), jnp.concatenate(qst[N_SUB:], axis=1)
    ka, kb = jnp.concatenate(khb[:N_SUB], axis=1), jnp.concatenate(khb[N_SUB:], axis=1)
    p = _dot_nt(qa, ka) + _dot_nt(qb, kb) * same
    return eqs, eks, qhs, khs, qa, qb, ka, kb, p


def _roll_rows(x, j):
    return x if j == 0 else pltpu.roll(x, j, 0)


def _gla(proj, w2p, b, tables, *, name):
    lp = proj.shape[0]
    nblk, c, s2 = lp // BLK, GLA_CHUNK, GLA_SUB2
    nch = BLK // c

    def body(q_ref, k_ref, v_ref, a_ref, w_ref, b_ref, tri_ref, ones_ref, o_ref, st_ref, state):
        i_blk = pl.program_id(0)

        @pl.when(i_blk == 0)
        def _():
            state[...] = jnp.zeros_like(state)
        hm_q = _head_masks(GLA_QK, GLA_DK)
        masks = _gla_masks()
        tri, ones_qv, w2, bias = tri_ref[...], ones_ref[...], w_ref[...], b_ref[...]

        def chunk(ci, carry):
            sl = pl.ds(pl.multiple_of(ci * c, c), c)
            ok = _gla_rows(i_blk, ci, lp)
            k, v = k_ref[sl, :], v_ref[sl, :]
            vb = v.astype(BF16)
            qs = q_ref[sl, :] * (GLA_DK ** -0.5)
            _, g = _gla_gate(a_ref[sl, :], w2, bias, ok, tri)
            last = g[c - 1:c, :]
            st = state[...]
            st_ref[ci] = st
            qst = _stack_heads(qs * jnp.exp(g), hm_q).astype(BF16)
            oi = _dot_nt(qst, st.astype(BF16))
            o = jnp.concatenate([oi[c * h:c * h + c, :] for h in range(GLA_HEADS)], axis=1)
            ke = k * jnp.exp(last - g)
            f = _dot_tn(vb, ke.astype(BF16))
            upd = f[0:GLA_DV, :] * hm_q[0]
            for h in range(1, GLA_HEADS):
                upd = upd + f[GLA_DV * h:GLA_DV * (h + 1), :] * hm_q[h]
            state[...] = st * jnp.exp(last) + upd
            p = _gla_hats(qs, k, g, masks, hm_q)[-1]
            ob = _dot(p.astype(BF16), vb)
            o = o + jnp.concatenate([ob[c * h:c * h + c, GLA_DV * h:GLA_DV * (h + 1)] for h in range(GLA_HEADS)],
                                    axis=1)
            ws = []
            for j in range(s2):
                ej = jnp.exp(jnp.minimum(g - _roll_rows(g, j), 0.0))
                ws.append((qs * _roll_rows(k, j) * ej * masks[3][j]).astype(BF16))
            ball = _dot(jnp.concatenate(ws, axis=0), ones_qv)
            for j in range(s2):
                o = o + ball[c * j:c * j + c, :] * _roll_rows(v, j)
            o_ref[sl, :] = o
            return carry

        lax.fori_loop(0, nch, chunk, 0)

    tri, _, ones_qv, _ = tables
    full = lambda arr: pl.BlockSpec(arr.shape, lambda i: (0,) * arr.ndim)
    return pl.pallas_call(
        body,
        out_shape=(jax.ShapeDtypeStruct((lp, GLA_V), F32), jax.ShapeDtypeStruct((lp // c, GLA_DV, GLA_QK), F32)),
        grid=(nblk,),
        in_specs=[pl.BlockSpec((BLK, GLA_QK), lambda i: (i, C_GQ // GLA_QK)),
                  pl.BlockSpec((BLK, GLA_QK), lambda i: (i, C_GK // GLA_QK)),
                  pl.BlockSpec((BLK, GLA_V), lambda i: (i, C_GV // GLA_V)),
                  pl.BlockSpec((BLK, 128), lambda i: (i, C_GA // 128)),
                  full(w2p), full(b), full(tri), full(ones_qv)],
        out_specs=(pl.BlockSpec((BLK, GLA_V), lambda i: (i, 0)),
                   pl.BlockSpec((nch, GLA_DV, GLA_QK), lambda i: (i, 0, 0))),
        scratch_shapes=[pltpu.VMEM((GLA_DV, GLA_QK), F32)],
        compiler_params=_cp("arbitrary"), name=name)(proj, proj, proj, proj, w2p, b, tri, ones_qv)


def _gla_bwd(proj, do, states, w2p, b, tables, *, name):
    lp = proj.shape[0]
    nblk, c, s1, s2 = lp // BLK, GLA_CHUNK, GLA_SUB, GLA_SUB2
    nch = BLK // c

    def body(q_ref, k_ref, v_ref, a_ref, do_ref, st_ref, w_ref, b_ref, tri_ref, trit_ref, ones_ref, onest_ref,
             dp_ref, dw_ref, db_ref, dstate, dqs_s, dk_s, dg_s, dv_s):
        i_blk = nblk - 1 - pl.program_id(0)

        @pl.when(pl.program_id(0) == 0)
        def _():
            dstate[...] = jnp.zeros_like(dstate)
            dw_ref[...] = jnp.zeros_like(dw_ref)
            db_ref[...] = jnp.zeros_like(db_ref)
        hm_q = _head_masks(GLA_QK, GLA_DK)
        hm_v = _head_masks(GLA_V, GLA_DV)
        masks = _gla_masks()
        tri, trit, ones_qv, ones_vq = tri_ref[...], trit_ref[...], ones_ref[...], onest_ref[...]
        w2, bias = w_ref[...], b_ref[...]
        rsum = lambda x: jnp.sum(x, axis=0, keepdims=True)

        def chunk(cc, carry):
            ci = nch - 1 - cc
            sl = pl.ds(pl.multiple_of(ci * c, c), c)
            ok = _gla_rows(i_blk, ci, lp)
            k, v, ga = k_ref[sl, :], v_ref[sl, :], a_ref[sl, :]
            vb = v.astype(BF16)
            qs = q_ref[sl, :] * (GLA_DK ** -0.5)
            z, g = _gla_gate(ga, w2, bias, ok, tri)
            last = g[c - 1:c, :]
            elast = jnp.exp(last)
            eg = jnp.exp(g)
            ekl = jnp.exp(last - g)
            qe, ke = qs * eg, k * ekl
            dov = do_ref[sl, :]
            st = st_ref[ci]
            dsn = dstate[...]
            qst = _stack_heads(qe, hm_q).astype(BF16)
            dost = jnp.concatenate([dov[:, GLA_DV * h:GLA_DV * (h + 1)] for h in range(GLA_HEADS)], axis=0).astype(BF16)
            dqe_st = _dot(dost, st.astype(BF16))
            dqe = dqe_st[0:c, :] * hm_q[0]
            for h in range(1, GLA_HEADS):
                dqe = dqe + dqe_st[c * h:c * h + c, :] * hm_q[h]
            dstate[...] = _dot_tn(dost, qst) + dsn * elast
            dlast = rsum(dsn * st) * elast
            df = _stack_heads(dsn, hm_q).astype(BF16)
            dv_s[...] = _dot_nt(ke.astype(BF16), df)
            dke = _dot(vb, df)
            xk = dke * ke
            dqs_s[...] = dqe * eg
            dk_s[...] = dke * ekl
            dg_s[...] = dqe * qe - xk
            dlast = dlast + rsum(xk)
            eqs, eks, qhs, khs, qa, qb, ka, kb, p = _gla_hats(qs, k, g, masks, hm_q)
            dost_v = _stack_heads(dov, hm_v).astype(BF16)
            dp = _dot_nt(dost_v, vb)
            dv_s[...] += _dot_tn(p.astype(BF16), dost_v)
            dpa, dpb = dp.astype(BF16), (dp * masks[2]).astype(BF16)
            dq_all = (_dot(dpa, ka), _dot(dpb, kb))
            dk_all = (_dot_tn(dpa, qa), _dot_tn(dpb, qb))
            for t in range(N_SUB + N_SUB2):
                lvl, i = (0, t) if t < N_SUB else (1, t - N_SUB)
                cols = slice(GLA_QK * i, GLA_QK * (i + 1))
                dq_st = dq_all[lvl][:, cols]
                dqh = dq_st[0:c, :] * hm_q[0]
                for h in range(1, GLA_HEADS):
                    dqh = dqh + dq_st[c * h:c * h + c, :] * hm_q[h]
                dkh = dk_all[lvl][:, cols]
                xq, xkh = dqh * qhs[t], dkh * khs[t]
                dqs_s[...] += dqh * eqs[t]
                dk_s[...] += dkh * eks[t]
                dg_s[...] += xq - xkh
                back_ref = xkh - xq
                if lvl == 0:
                    row = s1 * (i + 1) - 1
                    dg_s[row:row + 1, :] += rsum(back_ref)
                else:
                    for blk in range(c // s1):
                        row = s1 * blk + s2 * (i + 1) - 1
                        dg_s[row:row + 1, :] += rsum(back_ref[s1 * blk:s1 * blk + s1, :])
            kes, qes, ws, dbs = [], [], [], []
            for j in range(s2):
                em = jnp.exp(jnp.minimum(g - _roll_rows(g, j), 0.0)) * masks[3][j]
                kes.append(_roll_rows(k, j) * em)
                qes.append(qs * em)
                ws.append((qs * kes[j]).astype(BF16))
                dbs.append((dov * _roll_rows(v, j)).astype(BF16))
            ball = _dot(jnp.concatenate(ws, axis=0), ones_qv)
            dwall = _dot(jnp.concatenate(dbs, axis=0), ones_vq)
            for j in range(s2):
                back = (lambda x: x) if j == 0 else (lambda x, j=j: pltpu.roll(x, c - j, 0))
                dw = dwall[c * j:c * j + c, :]
                dv_s[...] += back(ball[c * j:c * j + c, :] * dov)
                dqs_s[...] += dw * kes[j]
                dk_s[...] += back(dw * qes[j])
                x = dw * qs * kes[j]
                dg_s[...] += x - back(x)
            dg_s[c - 1:c, :] += dlast
            dla = jnp.where(ok, _tri_sum(trit, dg_s[...]), 0.0)
            dz = dla * (1.0 / GLA_TAU) / (1.0 + jnp.exp(z))
            dzb = dz.astype(BF16)
            dp_ref[sl, 0:256] = (dqs_s[...] * (GLA_DK ** -0.5)).astype(BF16)
            dp_ref[sl, 256:512] = dk_s[...].astype(BF16)
            dp_ref[sl, 512:1024] = dv_s[...].astype(BF16)
            dp_ref[sl, 1024:1152] = _dot_nt(dzb, w2).astype(BF16)
            dp_ref[sl, 1152:1280] = jnp.zeros((c, 128), BF16)
            dw_ref[...] += _dot_tn(ga.astype(BF16), dzb)
            db_ref[...] += rsum(dz)
            return carry

        lax.fori_loop(0, nch, chunk, 0)

    tri, trit, ones_qv, ones_vq = tables
    full = lambda arr: pl.BlockSpec(arr.shape, lambda i: (0,) * arr.ndim)
    rev = lambda i: nblk - 1 - i
    return pl.pallas_call(
        body,
        out_shape=(jax.ShapeDtypeStruct((lp, P_GLA), BF16),
                   jax.ShapeDtypeStruct((128, GLA_QK), F32), jax.ShapeDtypeStruct((1, GLA_QK), F32)),
        grid=(nblk,),
        in_specs=[pl.BlockSpec((BLK, GLA_QK), lambda i: (rev(i), C_GQ // GLA_QK)),
                  pl.BlockSpec((BLK, GLA_QK), lambda i: (rev(i), C_GK // GLA_QK)),
                  pl.BlockSpec((BLK, GLA_V), lambda i: (rev(i), C_GV // GLA_V)),
                  pl.BlockSpec((BLK, 128), lambda i: (rev(i), C_GA // 128)),
                  pl.BlockSpec((BLK, GLA_V), lambda i: (rev(i), 0)),
                  pl.BlockSpec((nch, GLA_DV, GLA_QK), lambda i: (rev(i), 0, 0)),
                  full(w2p), full(b), full(tri), full(trit), full(ones_qv), full(ones_vq)],
        out_specs=(pl.BlockSpec((BLK, P_GLA), lambda i: (rev(i), 0)),
                   pl.BlockSpec((128, GLA_QK), lambda i: (0, 0)),
                   pl.BlockSpec((1, GLA_QK), lambda i: (0, 0))),
        scratch_shapes=[pltpu.VMEM((GLA_DV, GLA_QK), F32), pltpu.VMEM((c, GLA_QK), F32),
                        pltpu.VMEM((c, GLA_QK), F32), pltpu.VMEM((c, GLA_QK), F32), pltpu.VMEM((c, GLA_V), F32)],
        compiler_params=_cp("arbitrary"), name=name)(proj, proj, proj, proj, do, states, w2p, b, tri, trit, ones_qv, ones_vq)


def _as2d(a):
    return a.reshape(-1, a.shape[-1])


def _ew_tile(r):
    return _tile(r, (512, 256, 128, 64, 32, 16, 8))


def _add2(a, b, *, out_dtype, name):
    a2, b2 = _as2d(a), _as2d(b)
    r, n = a2.shape
    tm = _ew_tile(r)

    def body(a_ref, b_ref, o_ref):
        o_ref[...] = (a_ref[...] + b_ref[...]).astype(o_ref.dtype)

    blk = pl.BlockSpec((tm, n), lambda i: (i, 0))
    return pl.pallas_call(body, out_shape=jax.ShapeDtypeStruct((r, n), out_dtype), grid=(r // tm,), in_specs=[blk, blk],
                          out_specs=blk, compiler_params=_cp("parallel"), name=name)(a2, b2).reshape(a.shape)


def _sum_slots(own, q, *, name):
    shape = own.shape
    q3 = q.reshape(3, -1, shape[-1])
    own2 = _as2d(own)
    r, n = own2.shape
    tm = _ew_tile(r)

    def body(own_ref, q_ref, o_ref):
        f = lambda i: q_ref[i].astype(F32)
        o_ref[...] = ((own_ref[...].astype(F32) + f(0)) + f(1)) + f(2)

    blk = pl.BlockSpec((tm, n), lambda i: (i, 0))
    return pl.pallas_call(
        body, out_shape=jax.ShapeDtypeStruct((r, n), F32), grid=(r // tm,),
        in_specs=[blk, pl.BlockSpec((3, tm, n), lambda i: (0, i, 0))], out_specs=blk,
        compiler_params=_cp("parallel"), name=name)(own2, q3).reshape(shape)


def _adamw(w, g, m, v, *, name):
    shape = w.shape
    w2, g2, m2, v2 = _as2d(w), _as2d(g), _as2d(m), _as2d(v)
    r, n = w2.shape
    tm = _ew_tile(r)

    def body(w_ref, g_ref, m_ref, v_ref, d_ref, mo_ref, vo_ref):
        d_ref[...], mo_ref[...], vo_ref[...] = _adam_math(w_ref[...], g_ref[...], m_ref[...], v_ref[...])

    blk = pl.BlockSpec((tm, n), lambda i: (i, 0))
    o = jax.ShapeDtypeStruct((r, n), F32)
    d, mo, vo = pl.pallas_call(body, out_shape=(o, o, o), grid=(r // tm,), in_specs=[blk] * 4, out_specs=(blk,) * 3,
                               compiler_params=_cp("parallel"), name=name)(w2, g2, m2, v2)
    return d.reshape(shape), mo.reshape(shape), vo.reshape(shape)


def _adam_math(w, gv, m, v):
    c1 = 1.0 - ADAM_B1 ** ADAM_STEP
    c2 = 1.0 - ADAM_B2 ** ADAM_STEP
    mn = ADAM_B1 * m + (1.0 - ADAM_B1) * gv
    vn = ADAM_B2 * v + (1.0 - ADAM_B2) * (gv * gv)
    return -ADAM_LR * ((mn / c1) / (jnp.sqrt(vn / c2) + ADAM_EPS) + ADAM_WD * w), mn, vn


def _adamw_halves(w, m, v, mine, theirs, c, *, name):
    depth, rows, n = w.shape
    r2 = rows // 2
    tm = next(t for t in range(min(r2, 256), 0, -8) if r2 % t == 0)
    steps = r2 // tm

    def body(c_ref, w_ref, m_ref, v_ref, *rest):
        halves, (g_ref, d_ref, mo_ref, vo_ref) = rest[:2 * depth], rest[2 * depth:]
        l, h = pl.program_id(0), pl.program_id(1)
        gv = None
        for k in range(depth):
            gk = jnp.where(h == c_ref[0], halves[2 * k][...], halves[2 * k + 1][...])
            gv = gk if gv is None else jnp.where(l == k, gk, gv)
        g_ref[...] = gv
        d_ref[...], mo_ref[...], vo_ref[...] = _adam_math(w_ref[...], gv, m_ref[...], v_ref[...])

    big = pl.BlockSpec((tm, n), lambda l, h, i, c_ref: ((2 * l + h) * steps + i, 0))
    half = lambda k: pl.BlockSpec((tm, n), lambda l, h, i, c_ref: (jnp.where(l == k, i, 0), 0))
    o = jax.ShapeDtypeStruct((depth * rows, n), F32)
    args = [a for k in range(depth) for a in (mine[k], theirs[k])]
    outs = pl.pallas_call(
        body, out_shape=(o, o, o, o),
        grid_spec=pltpu.PrefetchScalarGridSpec(
            num_scalar_prefetch=1, grid=(depth, 2, steps),
            in_specs=[big, big, big] + [half(k) for k in range(depth) for _ in range(2)], out_specs=(big,) * 4),
        compiler_params=_cp("arbitrary", "arbitrary", "arbitrary"), name=name)(
            jnp.reshape(c, (1,)).astype(jnp.int32), _as2d(w), _as2d(m), _as2d(v), *args)
    return [a.reshape(w.shape) for a in outs]


ANY = pl.BlockSpec(memory_space=pl.ANY)


def _place():
    return lax.axis_index("x"), lax.axis_index("y"), lax.axis_index("c")


def _other_chips(x, y):
    return [(1 - x, y), (x, 1 - y), (1 - x, 1 - y)]


def _remote(src, dst, ssem, rsem, dev):
    return pltpu.make_async_remote_copy(src_ref=src, dst_ref=dst, send_sem=ssem, recv_sem=rsem, device_id=dev,
                                        device_id_type=MESH)


def _allgather_chips(arrs, *, name):
    n = len(arrs)

    def body(*refs):
        ins, outs = refs[:n], refs[n:2 * n]
        s1, r1, s2, r2 = refs[2 * n:]
        x, y, c = _place()
        q = 2 * x + y
        chips = _other_chips(x, y)
        qs = [2 * cx + cy for cx, cy in chips]
        sib = (x, y, 1 - c)
        first, passed = [], []
        for k in range(n):
            for j, chip in enumerate(chips):
                first.append(_remote(ins[k].at[c], outs[k].at[c, q], s1.at[k, j], r1.at[k, j], (*chip, c)))
        for cp in first:
            cp.start()
        for k in range(n):
            for j, chip in enumerate(chips):
                land = outs[k].at[c, qs[j]]
                _remote(land, land, s1.at[k, j], r1.at[k, j], (*chip, c)).wait_recv()
                fw = _remote(land, land, s2.at[k, j], r2.at[k, j], sib)
                fw.start()
                passed.append(fw)
        for k in range(n):
            for j in range(3):
                land = outs[k].at[1 - c, qs[j]]
                _remote(land, land, s2.at[k, j], r2.at[k, j], sib).wait_recv()
        for cp in first + passed:
            cp.wait_send()

    sem = pltpu.SemaphoreType.DMA
    outs = pl.pallas_call(
        body, out_shape=tuple(jax.ShapeDtypeStruct((2, 4) + a.shape[1:], a.dtype) for a in arrs),
        in_specs=[ANY] * n, out_specs=(ANY,) * n,
        scratch_shapes=[sem((n, 3)), sem((n, 3)), sem((n, 3)), sem((n, 3))], name=name)(*arrs)
    chip = 2 * lax.axis_index("x") + lax.axis_index("y")
    return [lax.dynamic_update_slice_in_dim(o, a[:, None], chip, axis=1) for o, a in zip(outs, arrs)]


def _pair_exchange(arrs, *, name):
    n = len(arrs)

    def body(*refs):
        ins, outs = refs[:n], refs[n:2 * n]
        ssem, rsem = refs[2 * n:]
        x, y, c = _place()
        cps = [_remote(ins[k].at[:, 1 - c], outs[k], ssem.at[k], rsem.at[k], (x, y, 1 - c)) for k in range(n)]
        for cp in cps:
            cp.start()
        for cp in cps:
            cp.wait()

    sem = pltpu.SemaphoreType.DMA
    return pl.pallas_call(
        body, out_shape=tuple(jax.ShapeDtypeStruct((a.shape[0],) + a.shape[2:], a.dtype) for a in arrs),
        in_specs=[ANY] * n, out_specs=(ANY,) * n, scratch_shapes=[sem((n,)), sem((n,))], name=name)(*arrs)


def _pair_sum(mine, theirs, c, *, name):
    _, _, r, n = mine.shape
    tm = r if r <= 512 else _ew_tile(r)

    def body(c_ref, a_ref, b_ref, o_ref):
        o_ref[...] = (a_ref[...] + b_ref[...]).astype(BF16)

    blk = pl.BlockSpec((None, tm, n), lambda s, i, c_ref: (s, i, 0))
    return pl.pallas_call(
        body, out_shape=jax.ShapeDtypeStruct((4, r, n), BF16),
        grid_spec=pltpu.PrefetchScalarGridSpec(
            num_scalar_prefetch=1, grid=(4, r // tm),
            in_specs=[pl.BlockSpec((None, None, tm, n), lambda s, i, c_ref: (s, c_ref[0], i, 0)), blk], out_specs=blk),
        compiler_params=_cp("parallel", "parallel"), name=name)(jnp.reshape(c, (1,)).astype(jnp.int32), mine, theirs)


def _chip_copies(ins, outs, ssem, rsem, mode):
    x, y, c = _place()
    q = 2 * x + y
    sends, recvs = [], []
    for k in range(len(ins)):
        for j, (cx, cy) in enumerate(_other_chips(x, y)):
            sem = (ssem.at[k, j], rsem.at[k, j], (cx, cy, c))
            if mode == "scatter":
                sends.append(_remote(ins[k].at[2 * cx + cy], outs[k].at[j], *sem))
                recvs.append(sends[-1])
            else:
                sends.append(_remote(ins[k].at[c], outs[k].at[2 * q + c], *sem))
                recvs.append(_remote(ins[k].at[c], outs[k].at[2 * (2 * cx + cy) + c], *sem))
    return sends, recvs


def _chip_wait(sends, recvs):
    for cp in sends:
        cp.wait_send()
    for cp in recvs:
        cp.wait_recv()


def _landing_shape(a, mode):
    return jax.ShapeDtypeStruct(((3,) if mode == "scatter" else (8,)) + a.shape[1:], a.dtype)


def _chip_exchange(arrs, mode, *, name):
    n = len(arrs)

    def body(*refs):
        ins, outs = refs[:n], refs[n:2 * n]
        ssem, rsem = refs[2 * n:]
        sends, recvs = _chip_copies(ins, outs, ssem, rsem, mode)
        for cp in sends:
            cp.start()
        _chip_wait(sends, recvs)

    sem = pltpu.SemaphoreType.DMA
    return list(pl.pallas_call(
        body, out_shape=tuple(_landing_shape(a, mode) for a in arrs),
        in_specs=[ANY] * n, out_specs=(ANY,) * n, scratch_shapes=[sem((n, 3)), sem((n, 3))], name=name)(*arrs))


def _pair_fill(bufs, owns, *, name):
    n = len(bufs)

    def body(*refs):
        own, outs = refs[n:2 * n], refs[2 * n:3 * n]
        ssem, rsem = refs[3 * n:]
        x, y, c = _place()
        q = 2 * x + y
        sib = (x, y, 1 - c)
        sends, recvs = [], []
        for k in range(n):
            for j, (cx, cy) in enumerate(_other_chips(x, y)):
                mine, theirs = outs[k].at[2 * (2 * cx + cy) + c], outs[k].at[2 * (2 * cx + cy) + 1 - c]
                sends.append(_remote(mine, mine, ssem.at[k, j], rsem.at[k, j], sib))
                recvs.append(_remote(mine, theirs, ssem.at[k, j], rsem.at[k, j], sib))
            slots = outs[k].at[pl.ds(2 * q, 2)]
            sends.append(_remote(own[k], slots, ssem.at[k, 3], rsem.at[k, 3], sib))
            recvs.append(sends[-1])
        for cp in sends:
            cp.start()
        _chip_wait(sends, recvs)

    sem = pltpu.SemaphoreType.DMA
    return list(pl.pallas_call(
        body, out_shape=tuple(jax.ShapeDtypeStruct(b.shape, b.dtype) for b in bufs),
        in_specs=[ANY] * (2 * n), out_specs=(ANY,) * n, scratch_shapes=[sem((n, 4)), sem((n, 4))],
        input_output_aliases={k: k for k in range(n)}, name=name)(*bufs, *owns))


def _pair_swap(arrs, *, name):
    n = len(arrs)

    def body(*refs):
        ins, outs = refs[:n], refs[n:2 * n]
        ssem, rsem = refs[2 * n:]
        x, y, c = _place()
        cps = [_remote(ins[k], outs[k], ssem.at[k], rsem.at[k], (x, y, 1 - c)) for k in range(n)]
        for cp in cps:
            cp.start()
        for cp in cps:
            cp.wait()

    sem = pltpu.SemaphoreType.DMA
    return pl.pallas_call(
        body, out_shape=tuple(jax.ShapeDtypeStruct(a.shape, a.dtype) for a in arrs),
        in_specs=[ANY] * n, out_specs=(ANY,) * n, scratch_shapes=[sem((n,)), sem((n,))], name=name)(*arrs)


def _allreduce_small(slab, *, name):
    r, n = slab.shape

    def body(x_ref, o_ref, buf, ssem, rsem):
        x, y, c = _place()
        me = 4 * x + 2 * y + c
        buf[me] = x_ref[...]
        cps = []
        for rel in range(1, 8):
            bx, by, bc = (rel >> 2) & 1, (rel >> 1) & 1, rel & 1
            px, py, pc = (x + bx) % 2, (y + by) % 2, (c + bc) % 2
            cps.append((_remote(x_ref, buf.at[me], ssem.at[rel - 1], rsem.at[rel - 1], (px, py, pc)),
                        4 * px + 2 * py + pc, (px, py, pc)))
        for cp, _, _ in cps:
            cp.start()
        for rel, (cp, peer, dev) in enumerate(cps):
            cp.wait_send()
            _remote(x_ref, buf.at[peer], ssem.at[rel], rsem.at[rel], dev).wait_recv()
        acc = buf[0]
        for k in range(1, 8):
            acc = acc + buf[k]
        o_ref[...] = acc

    vm = pl.BlockSpec(memory_space=pltpu.VMEM)
    sem = pltpu.SemaphoreType.DMA
    return pl.pallas_call(
        body, out_shape=jax.ShapeDtypeStruct((r, n), F32), in_specs=[vm], out_specs=vm,
        scratch_shapes=[pltpu.VMEM((8, r, n), F32), sem((7,)), sem((7,))], name=name)(slab)


def _slab(arrs, row_mult):
    flat = jnp.concatenate([a.reshape(-1) for a in arrs])
    unit = 128 * row_mult
    total = -(-flat.size // unit) * unit
    return jnp.pad(flat, (0, total - flat.size)).reshape(-1, 128)


def _unslab(slab, shapes):
    flat = slab.reshape(-1)
    out, off = [], 0
    for s in shapes:
        size = int(np.prod(s))
        out.append(flat[off:off + size].reshape(s))
        off += size
    return out


def _cols_from_chips(a):
    return jnp.transpose(a, (1, 0, 2)).reshape(a.shape[1], -1)


def _cols_to_chips(a, parts):
    r = a.shape[0]
    return jnp.transpose(a.reshape(r, parts, -1), (1, 0, 2))


BIG = ("w_in", "w_out", "up", "down")
GATHER_RIDES = {("proj", 0): (("w_out", 0), ("up", 0)), ("mix_out", 0): (("down", 0),),
                ("ffn_up_a", 0): (("w_in", 1), ("w_out", 1)), ("ffn_up_g", 0): (("up", 1),),
                ("proj", 1): (("down", 1),)}
REDUCE_RIDES = {("ffn_down_dx", 0): (("up",), 1), ("ffn_up_a_dx", 0): (("w_in", "w_out"), 1),
                ("ffn_up_g_dx", 0): (("down",), 1),
                ("proj_dx", 0): (("up",), 0), ("proj_dw_0", 0): (("down",), 0), ("proj_dw_1", 0): (("w_out",), 0)}


class _LocalWeights:
    def __init__(self, meta, win, wout, up_a, up_g, down, w2p, cw):
        self._meta, self._w = meta, {"win": win, "wout": wout, "up_a": up_a, "up_g": up_g, "down": down, "w2p": w2p,
                                     "cw": cw}

    def meta(self):
        return self._meta

    def get(self, kind, l):
        return self._w[kind][l]

    def mm(self, site, l, a, b, fn=None, **kw):
        return (fn or _mm)(a, b, name=site, **kw)

    def grads_done(self, l, g, kinds):
        pass


class _ChipWeights:
    def __init__(self, w_in, w_out, ffn_up, ffn_down, meta_tokens, gla_gate_w2, ffn_conv_w):
        self.x, self.y, self.c = _place()
        self.q = 2 * self.x + self.y
        halves = lambda a: a.astype(BF16).reshape(2, a.shape[0] // 2, a.shape[1])
        self.own = {(k, l): halves(a[l]) for k, a in zip(BIG, (w_in, w_out, ffn_up, ffn_down)) for l in range(DEPTH)}
        self.landed, self.swapped, self.full, self.n_swaps = {}, {}, {}, 0
        self.sh_shapes = [meta_tokens.shape, gla_gate_w2.shape, ffn_conv_w.shape]
        self.own["small", 0] = _slab([meta_tokens, gla_gate_w2, ffn_conv_w], 16).reshape(2, -1, 128)
        first = [("w_in", 0), ("small", 0)]
        for key, arr in zip(first, _chip_exchange([self.own[k] for k in first], "bcast", name="gather_first")):
            self.landed[key] = arr
        sh = self._whole("small", 0).reshape(4, -1, 128)
        parts = [_unslab(sh[k], self.sh_shapes) for k in range(4)]
        self._meta = jnp.concatenate([p[0] for p in parts], axis=-1)
        self.w2 = jnp.concatenate([p[1] for p in parts], axis=-1)
        self.cw = jnp.concatenate([p[2] for p in parts], axis=-1)
        self.partial, self.slots = {}, {}

    def _whole(self, kind, l):
        if (kind, l) not in self.full:
            keys = [k for k in self.landed if k not in self.full]
            got = _pair_fill([self.landed[k] for k in keys], [self.own[k] for k in keys],
                             name=f"gather_fill_{self.n_swaps}")
            self.n_swaps += 1
            for k, buf in zip(keys, got):
                self.full[k] = buf.reshape(4, 2 * buf.shape[1], buf.shape[2])
        return self.full[kind, l]

    def meta(self):
        return self._meta

    def get(self, kind, l):
        if kind == "win":
            return _to_kernel_cols(_cols_from_chips(self._whole("w_in", l)))
        if kind == "wout":
            return self._whole("w_out", l).reshape(D_MODEL, D_MODEL)
        if kind == "up_a":
            return _cols_from_chips(self._whole("up", l)[0:2])
        if kind == "up_g":
            return _cols_from_chips(self._whole("up", l)[2:4])
        if kind == "down":
            return self._whole("down", l).reshape(D_FF, D_MODEL)
        if kind == "w2p":
            return jnp.pad(self.w2[l], ((0, 128 - GLA_RANK), (0, 0))).astype(BF16)
        return self.cw[l]

    def mm(self, site, l, a, b, fn=None, **kw):
        fn = fn or _mm
        if (site, l) in GATHER_RIDES:
            keys = GATHER_RIDES[site, l]
            out, got = fn(a, b, name=site, carry=([self.own[k] for k in keys], "bcast"), **kw)
            self.landed.update(zip(keys, got))
            return out
        if (site, l) in REDUCE_RIDES:
            kinds, gl = REDUCE_RIDES[site, l]
            keys = [(k, gl) for k in kinds]
            if all(k in self.partial and k not in self.slots for k in keys):
                out, got = fn(a, b, name=site, carry=([self.partial[k] for k in keys], "scatter"), **kw)
                self.slots.update(zip(keys, got))
                return out
        return fn(a, b, name=site, **kw)

    def grads_done(self, l, g, kinds):
        split = lambda a: a.reshape(4, 2, a.shape[-2] // 2, a.shape[-1]) if a.ndim == 3 else \
            a.reshape(4, 2, a.shape[0] // 8, a.shape[1])
        src = {"w_in": lambda: g["w_in"][l], "w_out": lambda: g["w_out"][l],
               "up": lambda: g["up"][l], "down": lambda: g["down"][l]}
        big = {k: split(src[k]()) for k in kinds}
        from_sib = _pair_exchange([big[k] for k in kinds], name=f"grads_pair_exchange_{l}_{kinds[0]}")
        for k, theirs in zip(kinds, from_sib):
            self.partial[k, l] = _pair_sum(big[k], theirs, self.c, name=f"pair_sum_{k}_{l}")

    def reduce(self):
        keys = [(k, l) for l in range(DEPTH) for k in BIG]
        late = [k for k in keys if k not in self.slots]
        self.slots.update(zip(late, _chip_exchange([self.partial[k] for k in late], "scatter",
                                                   name="grads_chip_exchange")))
        half = {}
        for k in keys:
            own = lax.dynamic_index_in_dim(self.partial[k], self.q, 0, keepdims=False)
            half[k] = _sum_slots(own, self.slots[k], name=f"chip_sum_{k[0]}_{k[1]}")
        other = dict(zip(keys, _pair_swap([half[k] for k in keys], name="grads_pair_swap")))
        return [([half[k, l] for l in range(DEPTH)], [other[k, l] for l in range(DEPTH)]) for k in BIG]


def _local_step(x_rows, target_rows, wts, pre_mix_norm, gla_gate_b, ret_norm_w, gla_norm_w, post_mix_norm,
                pre_ffn_norm, ffn_conv_b, post_ffn_norm):
    d = D_MODEL
    lp = x_rows.shape[0] + FRONT + BACK
    row = lambda a, l: a[l][None, :]
    rtab = _ret_tables(lp)
    gtab = _gla_tables()
    h0 = jnp.concatenate([jnp.zeros((PADF, d), F32), wts.meta(), x_rows, jnp.zeros((BACK, d), F32)], axis=0)
    target = jnp.pad(target_rows, ((FRONT, BACK), (0, 0)))

    saved = []
    h = h0
    _, hn = _resid_norm(h0, None, None, row(pre_mix_norm, 0), name="norm_in")
    loss_local = dy = None
    for l in range(DEPTH):
        s = {"h_in": h, "hn": hn}
        s["proj"] = wts.mm("proj", l, hn, wts.get("win", l))
        s["o_ret"], s["st_ret"] = _retention(s["proj"], rtab, name="retention")
        s["o_gla"], s["st_gla"] = _gla(s["proj"], wts.get("w2p", l), row(gla_gate_b, l), gtab, name="gla")
        s["merged"] = _merge(s["o_ret"], s["o_gla"], s["proj"], row(ret_norm_w, l), row(gla_norm_w, l), name="merge")
        s["m"] = wts.mm("mix_out", l, s["merged"], wts.get("wout", l))
        s["h_mid"], s["hn2"] = _resid_norm(h, s["m"], row(post_mix_norm, l), row(pre_ffn_norm, l), name="resid_mix")
        s["ua"] = wts.mm("ffn_up_a", l, s["hn2"], wts.get("up_a", l))
        s["ug"] = wts.mm("ffn_up_g", l, s["hn2"], wts.get("up_g", l))
        cw_a, cw_g = wts.get("cw", l)[:, :D_FF], wts.get("cw", l)[:, D_FF:]
        cb_a, cb_g = ffn_conv_b[l][None, :D_FF], ffn_conv_b[l][None, D_FF:]
        s["conv"] = (cw_a, cw_g, cb_a, cb_g)
        s["act"], s["f"] = _conv_act_down(s["ua"], s["ug"], cw_a, cw_g, cb_a, cb_g, wts.get("down", l),
                                          name="conv_act_down")
        if l + 1 < DEPTH:
            h, hn = _resid_norm(s["h_mid"], s["f"], row(post_ffn_norm, l), row(pre_mix_norm, l + 1), name="resid_ffn")
        else:
            loss_local, dy, df_last, dw_last = _loss_head(s["h_mid"], s["f"], row(post_ffn_norm, l), target,
                                                          name="loss_head")
        saved.append(s)

    g = {k: [None] * DEPTH for k in ("pre_mix", "w_in", "w2", "gb", "ret_n", "gla_n", "w_out", "post_mix", "pre_ffn",
                                     "up", "cw", "cb", "down", "post_ffn")}
    dh_out, dhn_next = dy, None
    for l in reversed(range(DEPTH)):
        s = saved[l]
        cw_a, cw_g, cb_a, cb_g = s["conv"]
        if l + 1 < DEPTH:
            dh, df, g["pre_mix"][l + 1], g["post_ffn"][l] = _resid_norm_bwd(
                dh_out, dhn_next, saved[l + 1]["h_in"], s["f"], row(pre_mix_norm, l + 1), row(post_ffn_norm, l),
                name="resid_ffn_bwd")
        else:
            dh, df, g["post_ffn"][l] = dh_out, df_last, dw_last
        dact = wts.mm("ffn_down_dx", l, df, wts.get("down", l), nt=True)
        g["down"][l] = _mm_tn(s["act"], df, tn=512, name="ffn_down_dw")
        du_a, du_g, dcw_a, dcw_g, dcb_a, dcb_g = _conv_act_bwd(s["ua"], s["ug"], dact, cw_a, cw_g, cb_a, cb_g,
                                                               name="conv_act_bwd")
        g["cw"][l] = jnp.concatenate([dcw_a, dcw_g], axis=1)
        g["cb"][l] = jnp.concatenate([dcb_a, dcb_g], axis=1)[0]
        half_up = _mm_tn(s["hn2"], du_a, tn=D_FF // 2, blocks=(4, 0), name="ffn_up_a_dw")
        g["up"][l] = _mm_tn(s["hn2"], du_g, tn=D_FF // 2, blocks=(4, 2), into=half_up, name="ffn_up_g_dw")
        dhn2 = wts.mm("ffn_up_a_dx", l, du_a, wts.get("up_a", l), nt=True)
        dhn2 = wts.mm("ffn_up_g_dx", l, du_g, wts.get("up_g", l), nt=True, add=dhn2)
        dh, dm, g["pre_ffn"][l], g["post_mix"][l] = _resid_norm_bwd(
            dh, dhn2, s["h_mid"], s["m"], row(pre_ffn_norm, l), row(post_mix_norm, l), name="resid_mix_bwd")
        g["w_out"][l] = _mm_tn(s["merged"], dm, name="mix_out_dw")
        wts.grads_done(l, g, ("w_out", "up", "down"))
        dmerged = wts.mm("mix_out_dx", l, dm, wts.get("wout", l), nt=True)
        do_ret, do_gla, d_gate, g["ret_n"][l], g["gla_n"][l] = _merge_bwd(
            dmerged, s["o_ret"], s["o_gla"], s["proj"], row(ret_norm_w, l), row(gla_norm_w, l), name="merge_bwd")
        d_ret = _retention_bwd(s["proj"], do_ret, s["st_ret"], rtab, name="retention_bwd")
        d_gla, dw2, dgb = _gla_bwd(s["proj"], do_gla, s["st_gla"], wts.get("w2p", l), row(gla_gate_b, l), gtab,
                                   name="gla_bwd")
        g["w2"][l], g["gb"][l] = dw2[:GLA_RANK], dgb[0]
        pieces = (d_ret, d_gate, d_gla)
        g["w_in"][l] = _to_reference_chips(*[wts.mm(f"proj_dw_{i}", l, s["hn"], p, fn=_mm_tn)
                                             for i, p in enumerate(pieces)])
        win = wts.get("win", l)
        dhn_next = wts.mm("proj_dx", l, pieces, [win[:, 0:P_RET], win[:, P_RET:P_RET + P_GATE], win[:, P_RET + P_GATE:]],
                          fn=_mm_nt_sum)
        dh_out = dh
        wts.grads_done(l, g, ("w_in",))
    dh0, _, g["pre_mix"][0], _ = _resid_norm_bwd(dh_out, dhn_next, h0, None, row(pre_mix_norm, 0), None,
                                                 name="norm_in_bwd")
    return loss_local, dh0, g


def kernel(x, meta_tokens, pre_mix_norm, w_in, gla_gate_w2, gla_gate_b, ret_norm_w, gla_norm_w, w_out, post_mix_norm, pre_ffn_norm, ffn_up, ffn_conv_w, ffn_conv_b, ffn_down, post_ffn_norm, loss_target, m_meta_tokens, m_pre_mix_norm, m_w_in, m_gla_gate_w2, m_gla_gate_b, m_ret_norm_w, m_gla_norm_w, m_w_out, m_post_mix_norm, m_pre_ffn_norm, m_ffn_up, m_ffn_conv_w, m_ffn_conv_b, m_ffn_down, m_post_ffn_norm, v_meta_tokens, v_pre_mix_norm, v_w_in, v_gla_gate_w2, v_gla_gate_b, v_ret_norm_w, v_gla_norm_w, v_w_out, v_post_mix_norm, v_pre_ffn_norm, v_ffn_up, v_ffn_conv_w, v_ffn_conv_b, v_ffn_down, v_post_ffn_norm):
    xi, yi, ci = _place()
    chip = 2 * xi + yi
    seq = x.shape[1]
    d = D_MODEL
    wts = _ChipWeights(w_in, w_out, ffn_up, ffn_down, meta_tokens, gla_gate_w2, ffn_conv_w)
    loss_local, dh0, g = _local_step(x[0], loss_target[0], wts, pre_mix_norm, gla_gate_b, ret_norm_w, gla_norm_w,
                                     post_mix_norm, pre_ffn_norm, ffn_conv_b, post_ffn_norm)
    grad_x = dh0[FRONT:FRONT + seq][None]
    names = ("w_in", "w_out", "ffn_up", "ffn_down")
    big_halves = wts.reduce()

    small_full = [dh0[PADF:FRONT], jnp.stack(g["pre_mix"])[:, 0], jnp.stack(g["w2"]), jnp.stack(g["gb"]),
                  jnp.stack(g["ret_n"])[:, 0], jnp.stack(g["gla_n"])[:, 0], jnp.stack(g["post_mix"])[:, 0],
                  jnp.stack(g["pre_ffn"])[:, 0], jnp.stack(g["cw"]), jnp.stack(g["cb"]),
                  jnp.stack(g["post_ffn"])[:, 0]]
    small_sum = _unslab(_allreduce_small(_slab(small_full, 8), name="small_allreduce"), [a.shape for a in small_full])
    (g_meta, g_pre_mix, g_w2, g_gb, g_ret_n, g_gla_n, g_post_mix, g_pre_ffn, g_cw, g_cb, g_post_ffn) = small_sum
    g_meta = lax.dynamic_slice_in_dim(g_meta, chip * 256, 256, axis=1)
    g_w2 = lax.dynamic_slice_in_dim(g_w2, chip * 64, 64, axis=2)
    g_cw = lax.dynamic_slice_in_dim(g_cw, chip * 1408, 1408, axis=2)

    grads = [g_meta, g_pre_mix, None, g_w2, g_gb, g_ret_n, g_gla_n, None, g_post_mix, g_pre_ffn, None,
             g_cw, g_cb, None, g_post_ffn]
    ws = [meta_tokens, pre_mix_norm, w_in, gla_gate_w2, gla_gate_b, ret_norm_w, gla_norm_w, w_out, post_mix_norm,
          pre_ffn_norm, ffn_up, ffn_conv_w, ffn_conv_b, ffn_down, post_ffn_norm]
    ms = [m_meta_tokens, m_pre_mix_norm, m_w_in, m_gla_gate_w2, m_gla_gate_b, m_ret_norm_w, m_gla_norm_w, m_w_out,
          m_post_mix_norm, m_pre_ffn_norm, m_ffn_up, m_ffn_conv_w, m_ffn_conv_b, m_ffn_down, m_post_ffn_norm]
    vs = [v_meta_tokens, v_pre_mix_norm, v_w_in, v_gla_gate_w2, v_gla_gate_b, v_ret_norm_w, v_gla_norm_w, v_w_out,
          v_post_mix_norm, v_pre_ffn_norm, v_ffn_up, v_ffn_conv_w, v_ffn_conv_b, v_ffn_down, v_post_ffn_norm]
    big_idx = (2, 7, 10, 13)
    deltas, new_m, new_v = [None] * 15, [None] * 15, [None] * 15
    for i, nm, (mine, theirs) in zip(big_idx, names, big_halves):
        grads[i], deltas[i], new_m[i], new_v[i] = _adamw_halves(ws[i], ms[i], vs[i], mine, theirs, ci,
                                                                name=f"adamw_{nm}")
    small_idx = [i for i in range(15) if i not in big_idx]
    shapes = [ws[i].shape for i in small_idx]
    sd, sm, sv = _adamw(_slab([ws[i] for i in small_idx], 8), _slab([grads[i] for i in small_idx], 8),
                        _slab([ms[i] for i in small_idx], 8), _slab([vs[i] for i in small_idx], 8), name="adamw_small")
    for i, a, b, c_ in zip(small_idx, _unslab(sd, shapes), _unslab(sm, shapes), _unslab(sv, shapes)):
        deltas[i], new_m[i], new_v[i] = a, b, c_

    loss = lax.psum(loss_local, ("x", "y", "c"))
    return (loss, grad_x, *grads, *deltas, *new_m, *new_v)
```

```python
import functools
import math

import numpy as np
import jax
import jax.numpy as jnp
from jax import lax
from jax.experimental import pallas as pl
from jax.experimental.pallas import tpu as pltpu

F32 = jnp.float32
BF16 = jnp.bfloat16

D_MODEL = 1024
DEPTH = 2
N_META = 16
EPS = 1e-6
RET_HEADS = 4
RET_DK = 128
GLA_HEADS = 4
GLA_DK = 64
GLA_DV = 128
GLA_QK = GLA_HEADS * GLA_DK
GLA_V = GLA_HEADS * GLA_DV
GLA_RANK = 16
GLA_TAU = 16.0
D_FF = 2816
ROPE_BASE = 10000.0
IN_WIDTH = 3600
IN_PAD = 3840
C_RQ, C_RK, C_RV, C_RG, C_GR, C_GQ, C_GK, C_GV, C_GA = 0, 512, 1024, 1536, 2048, 2560, 2816, 3072, 3584
P_RET, P_GATE, P_GLA = 1536, 1024, 1280


def _to_kernel_cols(w):
    pad = jnp.zeros(w.shape[:-1] + (IN_PAD - IN_WIDTH,), w.dtype)
    return jnp.concatenate([w[..., 0:2048], w[..., 3072:3584], w[..., 2048:3072], w[..., 3584:3600], pad], axis=-1)


def _to_reference_chips(d_ret, d_gate, d_gla):
    segs = [(d_ret, 0, 0, 1536), (d_gate, 0, 1536, 512), (d_gla, 0, 2048, 1024), (d_gate, 512, 3072, 512),
            (d_gla, 1024, 3584, GLA_RANK)]
    per = IN_WIDTH // 4
    chips = []
    for j in range(4):
        lo, hi, parts = per * j, per * (j + 1), []
        for piece, p0, r0, width in segs:
            a, b = max(lo, r0), min(hi, r0 + width)
            if a < b:
                parts.append(piece[:, p0 + a - r0:p0 + b - r0])
        chips.append(jnp.concatenate(parts, axis=1))
    return jnp.stack(chips)

FRONT = 64
BACK = 64
PADF = FRONT - N_META
RET_CHUNK = 128
GLA_CHUNK = 64
GLA_SUB = 16
GLA_SUB2 = 4
BLK = 640

ADAM_LR, ADAM_B1, ADAM_B2, ADAM_EPS, ADAM_WD, ADAM_STEP = 0.001, 0.9, 0.999, 1e-08, 0.01, 10

VMEM_LIMIT = 56 * 2 ** 20
MM_VMEM_BUDGET = 40 * 2 ** 20
MESH = pl.DeviceIdType.MESH


def _cp(*sem):
    return pltpu.CompilerParams(dimension_semantics=sem, vmem_limit_bytes=VMEM_LIMIT)


def _tile(n, cands):
    for t in cands:
        if n % t == 0:
            return t
    raise ValueError(f"no tile for {n} in {cands}")


def _row_tile(n):
    return _tile(n, (640, 512, 320, 256, 128, 64))


def _mm(a, b, *, nt=False, add=None, out_dtype=F32, tn=None, name, carry=None):
    m, k = a.shape
    n = b.shape[0] if nt else b.shape[1]
    tm = _tile(m, (640, 320, 256, 128, 64))
    if tn is None:
        step_bytes = lambda t: 2 * (tm * k * a.dtype.itemsize + t * k * b.dtype.itemsize
                                    + tm * t * (jnp.dtype(out_dtype).itemsize + (4 if add is not None else 0)))
        tn = next(t for t in range(n, 0, -128) if n % t == 0 and (step_bytes(t) <= MM_VMEM_BUDGET or t == 128))
    dn = (((1,), (1,)), ((), ())) if nt else (((1,), (0,)), ((), ()))
    nj, ni = n // tn, m // tm
    n_in = 2 + (add is not None)
    c_arrs, c_mode = carry if carry is not None else ((), None)
    nc = len(c_arrs)

    def body(*refs):
        a_ref, b_ref = refs[:2]
        c_ref = refs[2] if add is not None else None
        o_ref = refs[n_in + nc]
        if nc:
            c_ins, c_outs = refs[n_in:n_in + nc], refs[n_in + nc + 1:n_in + 2 * nc + 1]
            ssem, rsem = refs[n_in + 2 * nc + 1:]
            j, i = pl.program_id(0), pl.program_id(1)

            @pl.when((j == 0) & (i == 0))
            def _():
                for cp in _chip_copies(c_ins, c_outs, ssem, rsem, c_mode)[0]:
                    cp.start()
        r = lax.dot_general(a_ref[...].astype(BF16), b_ref[...].astype(BF16), dn, preferred_element_type=F32)
        if add is not None:
            r = r + c_ref[...]
        o_ref[...] = r.astype(o_ref.dtype)
        if nc:
            @pl.when((j == nj - 1) & (i == ni - 1))
            def _():
                _chip_wait(*_chip_copies(c_ins, c_outs, ssem, rsem, c_mode))

    b_spec = pl.BlockSpec((tn, k), lambda j, i: (j, 0)) if nt else pl.BlockSpec((k, tn), lambda j, i: (0, j))
    in_specs = [pl.BlockSpec((tm, k), lambda j, i: (i, 0)), b_spec]
    args = [a, b]
    if add is not None:
        in_specs.append(pl.BlockSpec((tm, tn), lambda j, i: (i, j)))
        args.append(add)
    out_shape = jax.ShapeDtypeStruct((m, n), out_dtype)
    out_spec = pl.BlockSpec((tm, tn), lambda j, i: (i, j))
    if not nc:
        return pl.pallas_call(
            body, out_shape=out_shape, grid=(nj, ni), in_specs=in_specs, out_specs=out_spec,
            compiler_params=_cp("parallel", "parallel"), name=name)(*args)
    sem = pltpu.SemaphoreType.DMA
    outs = pl.pallas_call(
        body, out_shape=(out_shape,) + tuple(_landing_shape(x, c_mode) for x in c_arrs), grid=(nj, ni),
        in_specs=in_specs + [ANY] * nc, out_specs=(out_spec,) + (ANY,) * nc,
        scratch_shapes=[sem((nc, 3)), sem((nc, 3))],
        compiler_params=_cp("arbitrary", "arbitrary"), name=name)(*args, *c_arrs)
    return outs[0], list(outs[1:])


def _call_with_carry(body, *, out_shape, grid, in_specs, out_specs, args, semantics, carry, name, aliases=None):
    if carry is None:
        return pl.pallas_call(body, out_shape=out_shape, grid=grid, in_specs=in_specs, out_specs=out_specs,
                              input_output_aliases=aliases or {}, compiler_params=_cp(*semantics), name=name)(*args)
    c_arrs, c_mode = carry
    n_in, nc = len(args), len(c_arrs)

    def carried(*refs):
        c_ins, c_outs = refs[n_in:n_in + nc], refs[n_in + nc + 1:n_in + 2 * nc + 1]
        ssem, rsem = refs[n_in + 2 * nc + 1:]
        ids = [pl.program_id(d) for d in range(len(grid))]
        first = functools.reduce(lambda u, v: u & v, [i == 0 for i in ids])
        last = functools.reduce(lambda u, v: u & v, [i == g - 1 for i, g in zip(ids, grid)])

        @pl.when(first)
        def _():
            for cp in _chip_copies(c_ins, c_outs, ssem, rsem, c_mode)[0]:
                cp.start()
        body(*refs[:n_in], refs[n_in + nc])

        @pl.when(last)
        def _():
            _chip_wait(*_chip_copies(c_ins, c_outs, ssem, rsem, c_mode))

    sem = pltpu.SemaphoreType.DMA
    outs = pl.pallas_call(
        carried, out_shape=(out_shape,) + tuple(_landing_shape(x, c_mode) for x in c_arrs), grid=grid,
        in_specs=list(in_specs) + [ANY] * nc, out_specs=(out_specs,) + (ANY,) * nc,
        scratch_shapes=[sem((nc, 3)), sem((nc, 3))], input_output_aliases=aliases or {},
        compiler_params=_cp(*(("arbitrary",) * len(grid))), name=name)(*args, *c_arrs)
    return outs[0], list(outs[1:])


def _mm_nt_sum(a_list, b_list, *, name, carry=None):
    m, n = a_list[0].shape[0], b_list[0].shape[0]
    tm = _tile(m, (640, 320, 256, 128, 64))
    np_ = len(a_list)

    def body(*refs):
        acc = None
        for a_ref, b_ref in zip(refs[:np_], refs[np_:2 * np_]):
            r = lax.dot_general(a_ref[...].astype(BF16), b_ref[...].astype(BF16), (((1,), (1,)), ((), ())),
                                preferred_element_type=F32)
            acc = r if acc is None else acc + r
        refs[2 * np_][...] = acc

    return _call_with_carry(
        body, out_shape=jax.ShapeDtypeStruct((m, n), F32), grid=(m // tm,),
        in_specs=[pl.BlockSpec((tm, a.shape[1]), lambda i: (i, 0)) for a in a_list]
        + [pl.BlockSpec(b.shape, lambda i: (0, 0)) for b in b_list],
        out_specs=pl.BlockSpec((tm, n), lambda i: (i, 0)), args=[*a_list, *b_list], semantics=("parallel",),
        carry=carry, name=name)


def _mm_tn(a, b, *, tn=None, blocks=None, into=None, name, carry=None):
    m, k = a.shape
    n = b.shape[1]
    tm = _tile(m, (1664, 640, 320, 256, 128, 64))
    tn = n if tn is None else tn
    if blocks is not None:
        total, first = blocks
        out_shape = jax.ShapeDtypeStruct((total, k, tn), F32)
        out_spec = pl.BlockSpec((None, k, tn), lambda j, i: (first + j, 0, 0))
    else:
        out_shape = jax.ShapeDtypeStruct((k, n), F32)
        out_spec = pl.BlockSpec((k, tn), lambda j, i: (0, j))

    def body(a_ref, b_ref, *rest):
        o_ref = rest[-1]

        @pl.when(pl.program_id(1) == 0)
        def _():
            o_ref[...] = jnp.zeros_like(o_ref)
        o_ref[...] += lax.dot_general(a_ref[...].astype(BF16), b_ref[...].astype(BF16),
                                      (((0,), (0,)), ((), ())), preferred_element_type=F32)

    in_specs = [pl.BlockSpec((tm, k), lambda j, i: (i, 0)), pl.BlockSpec((tm, tn), lambda j, i: (i, j))]
    args, alias = [a, b], {}
    if into is not None:
        in_specs.append(pl.BlockSpec(memory_space=pl.ANY))
        args.append(into)
        alias = {2: 0}
    return _call_with_carry(body, out_shape=out_shape, grid=(n // tn, m // tm), in_specs=in_specs, out_specs=out_spec,
                            args=args, semantics=("parallel", "arbitrary"), carry=carry, name=name, aliases=alias)


def _rms(x, w):
    r = lax.rsqrt(jnp.mean(x * x, axis=-1, keepdims=True) + EPS)
    return x * r * w


def _rms_bwd(x, w, dy):
    r = lax.rsqrt(jnp.mean(x * x, axis=-1, keepdims=True) + EPS)
    xh = x * r
    dxh = dy * w
    dx = r * (dxh - xh * jnp.mean(dxh * xh, axis=-1, keepdims=True))
    return dx, jnp.sum(dy * xh, axis=0, keepdims=True)


def _resid_norm(h, t, w_post, w_next, *, name):
    lp, d = h.shape
    tm = _row_tile(lp)
    has_t = t is not None

    def body(*refs):
        if has_t:
            h_ref, t_ref, wp_ref, wn_ref, ho_ref, hn_ref = refs
            hv = h_ref[...] + _rms(t_ref[...], wp_ref[...])
            ho_ref[...] = hv
        else:
            h_ref, wn_ref, hn_ref = refs
            hv = h_ref[...]
        hn_ref[...] = _rms(hv, wn_ref[...]).astype(BF16)

    row = pl.BlockSpec((tm, d), lambda i: (i, 0))
    vec = pl.BlockSpec((1, d), lambda i: (0, 0))
    if has_t:
        return pl.pallas_call(
            body, out_shape=(jax.ShapeDtypeStruct((lp, d), F32), jax.ShapeDtypeStruct((lp, d), BF16)),
            grid=(lp // tm,), in_specs=[row, row, vec, vec], out_specs=(row, row),
            compiler_params=_cp("parallel"), name=name)(h, t, w_post, w_next)
    return h, pl.pallas_call(
        body, out_shape=jax.ShapeDtypeStruct((lp, d), BF16), grid=(lp // tm,), in_specs=[row, vec],
        out_specs=row, compiler_params=_cp("parallel"), name=name)(h, w_next)


def _resid_norm_bwd(dh_out, dhn, h_new, t, w_next, w_post, *, name):
    lp, d = h_new.shape if h_new is not None else t.shape
    tm = _row_tile(lp)
    has_n = dhn is not None
    has_t = t is not None

    def body(*refs):
        refs = list(refs)
        dho_ref = refs.pop(0)
        if has_n:
            dhn_ref, hn_ref, wn_ref = refs.pop(0), refs.pop(0), refs.pop(0)
        if has_t:
            t_ref, wp_ref = refs.pop(0), refs.pop(0)
        dh_ref = refs.pop(0) if has_n else None
        dt_ref = refs.pop(0) if has_t else None
        dwn_ref = refs.pop(0) if has_n else None
        dwp_ref = refs.pop(0) if has_t else None
        first = pl.program_id(0) == 0
        dh = dho_ref[...]
        if has_n:
            dx, dwn = _rms_bwd(hn_ref[...], wn_ref[...], dhn_ref[...])
            dh = dh + dx
            dh_ref[...] = dh

            @pl.when(first)
            def _():
                dwn_ref[...] = jnp.zeros_like(dwn_ref)
            dwn_ref[...] += dwn
        if has_t:
            dt, dwp = _rms_bwd(t_ref[...], wp_ref[...], dh)
            dt_ref[...] = dt.astype(BF16)

            @pl.when(first)
            def _():
                dwp_ref[...] = jnp.zeros_like(dwp_ref)
            dwp_ref[...] += dwp

    row = pl.BlockSpec((tm, d), lambda i: (i, 0))
    vec = pl.BlockSpec((1, d), lambda i: (0, 0))
    args, in_specs, out_shape, out_specs = [dh_out], [row], [], []
    if has_n:
        args += [dhn, h_new, w_next]
        in_specs += [row, row, vec]
    if has_t:
        args += [t, w_post]
        in_specs += [row, vec]
    if has_n:
        out_shape.append(jax.ShapeDtypeStruct((lp, d), F32)); out_specs.append(row)
    if has_t:
        out_shape.append(jax.ShapeDtypeStruct((lp, d), BF16)); out_specs.append(row)
    if has_n:
        out_shape.append(jax.ShapeDtypeStruct((1, d), F32)); out_specs.append(vec)
    if has_t:
        out_shape.append(jax.ShapeDtypeStruct((1, d), F32)); out_specs.append(vec)
    outs = list(pl.pallas_call(body, out_shape=tuple(out_shape), grid=(lp // tm,), in_specs=in_specs,
                               out_specs=tuple(out_specs), compiler_params=_cp("arbitrary"), name=name)(*args))
    dh = outs.pop(0) if has_n else dh_out
    dt = outs.pop(0) if has_t else None
    dwn = outs.pop(0) if has_n else None
    dwp = outs.pop(0) if has_t else None
    return dh, dt, dwn, dwp


def _loss_head(h, f, w_post, target, *, name):
    lp, d = h.shape
    tm = _row_tile(lp)

    def body(h_ref, f_ref, w_ref, t_ref, loss_ref, dy_ref, df_ref, dw_ref):
        i = pl.program_id(0)
        f, w = f_ref[...], w_ref[...]
        y = h_ref[...] + _rms(f, w)
        rows = i * tm + lax.broadcasted_iota(jnp.int32, (tm, 1), 0)
        tok = (rows >= FRONT) & (rows < lp - BACK)
        err = jnp.where(tok, y - t_ref[...], 0.0)
        dy = err * (1.0 / d)
        dy_ref[...] = dy
        df, dw = _rms_bwd(f, w, dy)
        df_ref[...] = df.astype(BF16)

        @pl.when(i == 0)
        def _():
            loss_ref[...] = jnp.zeros_like(loss_ref)
            dw_ref[...] = jnp.zeros_like(dw_ref)
        part = jnp.sum(jnp.sum(err * err, axis=1, keepdims=True), axis=0, keepdims=True) * (0.5 / d)
        loss_ref[...] += jnp.broadcast_to(part, loss_ref.shape)
        dw_ref[...] += dw

    row = pl.BlockSpec((tm, d), lambda i: (i, 0))
    vec = pl.BlockSpec((1, d), lambda i: (0, 0))
    loss, dy, df, dw = pl.pallas_call(
        body, out_shape=(jax.ShapeDtypeStruct((8, 128), F32), jax.ShapeDtypeStruct((lp, d), F32),
                         jax.ShapeDtypeStruct((lp, d), BF16), jax.ShapeDtypeStruct((1, d), F32)),
        grid=(lp // tm,), in_specs=[row, row, vec, row],
        out_specs=(pl.BlockSpec((8, 128), lambda i: (0, 0)), row, row, vec),
        compiler_params=_cp("arbitrary"), name=name)(h, f, w_post, target)
    return loss[0, 0], dy, df, dw


_GELU_C = math.sqrt(2.0 / math.pi)


def _gelu_and_grad(a):
    a2 = a * a
    t = jnp.tanh(a * (_GELU_C + (_GELU_C * 0.044715) * a2))
    ha = 0.5 * a
    h1 = 0.5 + 0.5 * t
    return a * h1, h1 + ha * (1.0 - t * t) * (_GELU_C + (3.0 * _GELU_C * 0.044715) * a2)


def _gelu(a):
    t = jnp.tanh(a * (_GELU_C + (_GELU_C * 0.044715) * (a * a)))
    return a * (0.5 + 0.5 * t)


def _conv3(parts, n, w, b):
    xx = jnp.concatenate(parts, axis=0)
    return b + xx[8:8 + n] * w[2:3] + pltpu.roll(xx, 1, 0)[8:8 + n] * w[1:2] + pltpu.roll(xx, 2, 0)[8:8 + n] * w[0:1]


def _conv_act(ua, ug, wa, wg, ba, bg, *, name):
    lp, n = ua.shape
    tm = _row_tile(lp)
    tc = _tile(n, (256, 128))
    nb8 = tm // 8

    def body(ua_ref, uap_ref, ug_ref, ugp_ref, wa_ref, wg_ref, ba_ref, bg_ref, o_ref):
        i = pl.program_id(0)
        ca = _conv3([uap_ref[...], ua_ref[...]], tm, wa_ref[...], ba_ref[...])
        cg = _conv3([ugp_ref[...], ug_ref[...]], tm, wg_ref[...], bg_ref[...])
        rows = i * tm + lax.broadcasted_iota(jnp.int32, (tm, 1), 0)
        ok = (rows >= PADF) & (rows < lp - BACK)
        o_ref[...] = jnp.where(ok, _gelu(ca) * cg, 0.0).astype(BF16)

    cur = pl.BlockSpec((tm, tc), lambda i, j: (i, j))
    prev = pl.BlockSpec((8, tc), lambda i, j: (jnp.maximum(i * nb8 - 1, 0), j))
    w3 = pl.BlockSpec((3, tc), lambda i, j: (0, j))
    b1 = pl.BlockSpec((1, tc), lambda i, j: (0, j))
    return pl.pallas_call(
        body, out_shape=jax.ShapeDtypeStruct((lp, n), BF16), grid=(lp // tm, n // tc),
        in_specs=[cur, prev, cur, prev, w3, w3, b1, b1], out_specs=cur,
        compiler_params=_cp("parallel", "parallel"), name=name)(ua, ua, ug, ug, wa, wg, ba, bg)


def _conv_act_down(ua, ug, wa, wg, ba, bg, down, *, name):
    lp, n = ua.shape
    d = down.shape[1]
    tm = _tile(lp, (320, 256, 128, 64))
    tc = _tile(n, (256, 128))
    nb8 = tm // 8

    def body(ua_ref, uap_ref, ug_ref, ugp_ref, wa_ref, wg_ref, ba_ref, bg_ref, dn_ref, act_ref, f_ref):
        i = pl.program_id(0)
        rows = i * tm + lax.broadcasted_iota(jnp.int32, (tm, 1), 0)
        ok = (rows >= PADF) & (rows < lp - BACK)
        acc = None
        for j in range(n // tc):
            cs = slice(tc * j, tc * j + tc)
            ca = _conv3([uap_ref[:, cs], ua_ref[:, cs]], tm, wa_ref[:, cs], ba_ref[:, cs])
            cg = _conv3([ugp_ref[:, cs], ug_ref[:, cs]], tm, wg_ref[:, cs], bg_ref[:, cs])
            act = jnp.where(ok, _gelu(ca) * cg, 0.0).astype(BF16)
            act_ref[:, cs] = act
            part = _dot(act, dn_ref[cs, :])
            acc = part if acc is None else acc + part
        f_ref[...] = acc

    cur = pl.BlockSpec((tm, n), lambda i: (i, 0))
    prev = pl.BlockSpec((8, n), lambda i: (jnp.maximum(i * nb8 - 1, 0), 0))
    w3 = pl.BlockSpec((3, n), lambda i: (0, 0))
    b1 = pl.BlockSpec((1, n), lambda i: (0, 0))
    return pl.pallas_call(
        body, out_shape=(jax.ShapeDtypeStruct((lp, n), BF16), jax.ShapeDtypeStruct((lp, d), F32)),
        grid=(lp // tm,),
        in_specs=[cur, prev, cur, prev, w3, w3, b1, b1, pl.BlockSpec(down.shape, lambda i: (0, 0))],
        out_specs=(cur, pl.BlockSpec((tm, d), lambda i: (i, 0))),
        compiler_params=_cp("parallel"), name=name)(ua, ua, ug, ug, wa, wg, ba, bg, down)


def _conv_act_bwd(ua, ug, dact, wa, wg, ba, bg, up_a, up_g, *, name):
    lp, n = ua.shape
    d = up_a.shape[0]
    tm = _tile(lp, (320, 256, 128, 64))
    tc = _tile(n, (256, 128))
    nb8 = tm // 8
    last8 = lp // 8 - 1
    ext = tm + 8

    def body(ua_ref, uap_ref, uan_ref, ug_ref, ugp_ref, ugn_ref, da_ref, dan_ref, wa_ref, wg_ref, ba_ref, bg_ref,
             upa_ref, upg_ref, dua_ref, dug_ref, dwa_ref, dwg_ref, dba_ref, dbg_ref, dhn_ref):
        i = pl.program_id(0)

        @pl.when(i == 0)
        def _():
            dwa_ref[...] = jnp.zeros_like(dwa_ref)
            dwg_ref[...] = jnp.zeros_like(dwg_ref)
            dba_ref[...] = jnp.zeros_like(dba_ref)
            dbg_ref[...] = jnp.zeros_like(dbg_ref)
        rows = i * tm + lax.broadcasted_iota(jnp.int32, (ext, 1), 0)
        ok = (rows >= PADF) & (rows < lp - BACK)

        def conv(parts, w, b):
            xx = jnp.concatenate(parts, axis=0)
            x, x1, x2 = xx[8:8 + ext], pltpu.roll(xx, 1, 0)[8:8 + ext], pltpu.roll(xx, 2, 0)[8:8 + ext]
            return b + x * w[2:3] + x1 * w[1:2] + x2 * w[0:1], x, x1, x2

        def back(dc, w):
            return (dc[:tm] * w[2:3] + pltpu.roll(dc, ext - 1, 0)[:tm] * w[1:2]
                    + pltpu.roll(dc, ext - 2, 0)[:tm] * w[0:1])

        def wsum(dw_ref, db_ref, cs, dc, x, x1, x2):
            dd = dc[:tm]
            s = lambda v: jnp.sum(v, axis=0, keepdims=True)
            dw_ref[0:1, cs] += s(dd * x2[:tm])
            dw_ref[1:2, cs] += s(dd * x1[:tm])
            dw_ref[2:3, cs] += s(dd * x[:tm])
            db_ref[:, cs] += s(dd)

        acc = None
        for j in range(n // tc):
            cs = slice(tc * j, tc * j + tc)
            wa, wg = wa_ref[:, cs], wg_ref[:, cs]
            ca, xa, xa1, xa2 = conv([uap_ref[:, cs], ua_ref[:, cs], uan_ref[:, cs]], wa, ba_ref[:, cs])
            cg, xg, xg1, xg2 = conv([ugp_ref[:, cs], ug_ref[:, cs], ugn_ref[:, cs]], wg, bg_ref[:, cs])
            dact_e = jnp.where(ok, jnp.concatenate([da_ref[:, cs], dan_ref[:, cs]], axis=0), 0.0)
            gel, gel_d = _gelu_and_grad(ca)
            dca = dact_e * cg * gel_d
            dcg = dact_e * gel
            du_a, du_g = back(dca, wa).astype(BF16), back(dcg, wg).astype(BF16)
            dua_ref[:, cs] = du_a
            dug_ref[:, cs] = du_g
            wsum(dwa_ref, dba_ref, cs, dca, xa, xa1, xa2)
            wsum(dwg_ref, dbg_ref, cs, dcg, xg, xg1, xg2)
            part = _dot_nt(du_a, upa_ref[:, cs]) + _dot_nt(du_g, upg_ref[:, cs])
            acc = part if acc is None else acc + part
        dhn_ref[...] = acc

    cur = pl.BlockSpec((tm, n), lambda i: (i, 0))
    prev = pl.BlockSpec((8, n), lambda i: (jnp.maximum(i * nb8 - 1, 0), 0))
    nxt = pl.BlockSpec((8, n), lambda i: (jnp.minimum((i + 1) * nb8, last8), 0))
    w3 = pl.BlockSpec((3, n), lambda i: (0, 0))
    b1 = pl.BlockSpec((1, n), lambda i: (0, 0))
    whole = pl.BlockSpec(memory_space=pltpu.VMEM)
    return pl.pallas_call(
        body,
        out_shape=(jax.ShapeDtypeStruct((lp, n), BF16), jax.ShapeDtypeStruct((lp, n), BF16),
                   jax.ShapeDtypeStruct((3, n), F32), jax.ShapeDtypeStruct((3, n), F32),
                   jax.ShapeDtypeStruct((1, n), F32), jax.ShapeDtypeStruct((1, n), F32),
                   jax.ShapeDtypeStruct((lp, d), F32)),
        grid=(lp // tm,),
        in_specs=[cur, prev, nxt, cur, prev, nxt, cur, nxt, w3, w3, b1, b1, whole, whole],
        out_specs=(cur, cur, w3, w3, b1, b1, pl.BlockSpec((tm, d), lambda i: (i, 0))),
        compiler_params=_cp("arbitrary"), name=name)(ua, ua, ua, ug, ug, ug, dact, dact, wa, wg, ba, bg, up_a, up_g)


def _sigmoid(x):
    return 1.0 / (1.0 + jnp.exp(-x))


def _merge(o_ret, o_gla, proj, w_ret, w_gla, *, name):
    lp = o_ret.shape[0]
    tm = _row_tile(lp)

    def body(or_ref, og_ref, rg_ref, gr_ref, wr_ref, wg_ref, m_ref):
        oret, ogla = or_ref[...], og_ref[...]
        yr, yg = [], []
        for h in range(4):
            hs = slice(128 * h, 128 * h + 128)
            o = oret[:, hs]
            xc = o - jnp.mean(o, axis=-1, keepdims=True)
            yr.append(xc * lax.rsqrt(jnp.mean(xc * xc, axis=-1, keepdims=True) + EPS))
            o = ogla[:, hs]
            yg.append(o * lax.rsqrt(jnp.mean(o * o, axis=-1, keepdims=True) + EPS))
        rg, gr = rg_ref[...], gr_ref[...]
        m_ref[:, 0:512] = (jnp.concatenate(yr, axis=1) * wr_ref[...] * (rg * _sigmoid(rg))).astype(BF16)
        m_ref[:, 512:1024] = (jnp.concatenate(yg, axis=1) * wg_ref[...] * (gr * _sigmoid(gr))).astype(BF16)

    row = pl.BlockSpec((tm, 512), lambda i: (i, 0))
    vec = pl.BlockSpec((1, 512), lambda i: (0, 0))
    return pl.pallas_call(
        body, out_shape=jax.ShapeDtypeStruct((lp, 1024), BF16), grid=(lp // tm,),
        in_specs=[row, row, pl.BlockSpec((tm, 512), lambda i: (i, C_RG // 512)),
                  pl.BlockSpec((tm, 512), lambda i: (i, C_GR // 512)), vec, vec],
        out_specs=pl.BlockSpec((tm, 1024), lambda i: (i, 0)),
        compiler_params=_cp("parallel"), name=name)(o_ret, o_gla, proj, proj, w_ret, w_gla)


def _merge_bwd(dm, o_ret, o_gla, proj, w_ret, w_gla, *, name):
    lp = o_ret.shape[0]
    tm = _row_tile(lp)

    def body(dm_ref, or_ref, og_ref, rg_ref, gr_ref, wr_ref, wg_ref, dor_ref, dog_ref, dgate_ref, dwr_ref, dwg_ref):
        @pl.when(pl.program_id(0) == 0)
        def _():
            dwr_ref[...] = jnp.zeros_like(dwr_ref)
            dwg_ref[...] = jnp.zeros_like(dwg_ref)

        def group(d, o_all, gate, w, center):
            sg = _sigmoid(gate)
            s = gate * sg
            ds = sg * (1.0 + gate * (1.0 - sg))
            xh, rr = [], []
            for h in range(4):
                o = o_all[:, 128 * h:128 * h + 128]
                if center:
                    o = o - jnp.mean(o, axis=-1, keepdims=True)
                r = lax.rsqrt(jnp.mean(o * o, axis=-1, keepdims=True) + EPS)
                xh.append(o * r)
                rr.append(r)
            xh_all = jnp.concatenate(xh, axis=1)
            dgate = d * xh_all * w * ds
            dw = jnp.sum(d * xh_all * s, axis=0, keepdims=True)
            dxh_all = d * w * s
            do = []
            for h in range(4):
                dxh = dxh_all[:, 128 * h:128 * h + 128]
                t = dxh - xh[h] * jnp.mean(dxh * xh[h], axis=-1, keepdims=True)
                if center:
                    t = t - jnp.mean(dxh, axis=-1, keepdims=True)
                do.append(rr[h] * t)
            return jnp.concatenate(do, axis=1), dgate, dw

        dmv = dm_ref[...]
        do, dg, dw = group(dmv[:, 0:512], or_ref[...], rg_ref[...], wr_ref[...], True)
        dor_ref[...] = do
        dgate_ref[:, 0:512] = dg.astype(BF16)
        dwr_ref[...] += dw
        do, dg, dw = group(dmv[:, 512:1024], og_ref[...], gr_ref[...], wg_ref[...], False)
        dog_ref[...] = do
        dgate_ref[:, 512:1024] = dg.astype(BF16)
        dwg_ref[...] += dw

    row = pl.BlockSpec((tm, 512), lambda i: (i, 0))
    vec = pl.BlockSpec((1, 512), lambda i: (0, 0))
    return pl.pallas_call(
        body,
        out_shape=(jax.ShapeDtypeStruct((lp, 512), F32), jax.ShapeDtypeStruct((lp, 512), F32),
                   jax.ShapeDtypeStruct((lp, P_GATE), BF16),
                   jax.ShapeDtypeStruct((1, 512), F32), jax.ShapeDtypeStruct((1, 512), F32)),
        grid=(lp // tm,),
        in_specs=[pl.BlockSpec((tm, 1024), lambda i: (i, 0)), row, row,
                  pl.BlockSpec((tm, 512), lambda i: (i, C_RG // 512)),
                  pl.BlockSpec((tm, 512), lambda i: (i, C_GR // 512)), vec, vec],
        out_specs=(row, row, pl.BlockSpec((tm, P_GATE), lambda i: (i, 0)), vec, vec),
        compiler_params=_cp("arbitrary"), name=name)(dm, o_ret, o_gla, proj, proj, w_ret, w_gla)


def _dot(a, b):
    return lax.dot_general(a, b, (((1,), (0,)), ((), ())), preferred_element_type=F32)


def _dot_nt(a, b):
    return lax.dot_general(a, b, (((1,), (1,)), ((), ())), preferred_element_type=F32)


def _dot_tn(a, b):
    return lax.dot_general(a, b, (((0,), (0,)), ((), ())), preferred_element_type=F32)


def _ret_tables(lp):
    cr = RET_CHUNK
    pos = np.arange(lp, dtype=np.float32) - np.float32(PADF)
    half = RET_DK // 2
    inv = (np.float32(ROPE_BASE) ** (-np.arange(half, dtype=np.float32) / np.float32(half))).astype(np.float32)
    ang = (pos[:, None] * inv[None, :]).astype(np.float32)
    c, s = np.cos(ang).astype(np.float32), np.sin(ang).astype(np.float32)
    rope_c = jnp.asarray(np.concatenate([c, c], axis=1))
    rope_s = jnp.asarray(np.concatenate([-s, s], axis=1))
    log_g = np.log(1.0 - 2.0 ** (-5.0 - np.arange(RET_HEADS, dtype=np.float64)))
    idx = np.arange(cr, dtype=np.float64)
    diff = idx[:, None] - idx[None, :]
    dmat = np.where(diff >= 0, np.exp(log_g[:, None, None] * np.maximum(diff, 0.0)), 0.0)
    zeta = np.exp(log_g[:, None] * (cr - 1.0 - idx)[None, :])
    xi = np.exp(log_g[:, None] * (idx + 1.0)[None, :])
    gc = np.exp(log_g * cr)
    f = lambda a: jnp.asarray(a.astype(np.float32))
    return (rope_c, rope_s, f(dmat), f(np.broadcast_to(zeta[:, :, None], (RET_HEADS, cr, 128))),
            f(np.broadcast_to(xi[:, :, None], (RET_HEADS, cr, 128))),
            f(np.broadcast_to(gc[:, None, None], (RET_HEADS, 8, 128))))


def _rope(t, c, s):
    return t * c + pltpu.roll(t, 64, 1) * s


def _rope_t(d, c, s):
    return d * c + pltpu.roll(d * s, 64, 1)


def _ret_specs(nblk, rev):
    ix = (lambda i: nblk - 1 - i) if rev else (lambda i: i)
    cr = RET_CHUNK
    col = lambda base: pl.BlockSpec((BLK, 512), lambda i: (ix(i), base // 512))
    tab = pl.BlockSpec((BLK, 128), lambda i: (ix(i), 0))
    sq = pl.BlockSpec((RET_HEADS, cr, cr), lambda i: (0, 0, 0))
    hv = pl.BlockSpec((RET_HEADS, cr, 128), lambda i: (0, 0, 0))
    g8 = pl.BlockSpec((RET_HEADS, 8, 128), lambda i: (0, 0, 0))
    st = pl.BlockSpec((RET_HEADS, BLK // cr, 128, 128), lambda i: (0, ix(i), 0, 0))
    out = pl.BlockSpec((BLK, 512), lambda i: (ix(i), 0))
    return col, tab, sq, hv, g8, st, out


def _retention(proj, tables, *, name):
    lp = proj.shape[0]
    nblk, cr = lp // BLK, RET_CHUNK
    scale = RET_DK ** -0.5

    def body(q_ref, k_ref, v_ref, c_ref, s_ref, d_ref, z_ref, x_ref, g_ref, o_ref, st_ref, state):
        @pl.when(pl.program_id(0) == 0)
        def _():
            state[...] = jnp.zeros_like(state)

        def chunk(ci, carry):
            sl = pl.ds(pl.multiple_of(ci * cr, cr), cr)
            c, s = c_ref[sl, :], s_ref[sl, :]
            for h in range(RET_HEADS):
                hs = slice(128 * h, 128 * h + 128)
                q = _rope(q_ref[sl, hs], c, s)
                k = _rope(k_ref[sl, hs], c, s) * scale
                qb, kb, vb = q.astype(BF16), k.astype(BF16), v_ref[sl, hs].astype(BF16)
                st = state[h]
                st_ref[h, ci] = st
                sc = _dot_nt(qb, kb) * d_ref[h]
                o_ref[sl, hs] = _dot(sc.astype(BF16), vb) + _dot(qb, st.astype(BF16)) * x_ref[h]
                state[h] = st * g_ref[h][0:1, :] + _dot_tn((k * z_ref[h]).astype(BF16), vb)
            return carry

        lax.fori_loop(0, BLK // cr, chunk, 0)

    col, tab, sq, hv, g8, st, out = _ret_specs(nblk, False)
    return pl.pallas_call(
        body,
        out_shape=(jax.ShapeDtypeStruct((lp, 512), F32), jax.ShapeDtypeStruct((4, lp // cr, 128, 128), F32)),
        grid=(nblk,), in_specs=[col(C_RQ), col(C_RK), col(C_RV), tab, tab, sq, hv, hv, g8],
        out_specs=(out, st), scratch_shapes=[pltpu.VMEM((RET_HEADS, 128, 128), F32)],
        compiler_params=_cp("arbitrary"), name=name)(proj, proj, proj, *tables)


def _retention_bwd(proj, do, states, tables, *, name):
    lp = proj.shape[0]
    nblk, cr = lp // BLK, RET_CHUNK
    nch = BLK // cr
    scale = RET_DK ** -0.5

    def body(q_ref, k_ref, v_ref, do_ref, st_ref, c_ref, s_ref, d_ref, z_ref, x_ref, g_ref, dqkv_ref, dstate):
        @pl.when(pl.program_id(0) == 0)
        def _():
            dstate[...] = jnp.zeros_like(dstate)

        def chunk(cc, carry):
            ci = nch - 1 - cc
            sl = pl.ds(pl.multiple_of(ci * cr, cr), cr)
            c, s = c_ref[sl, :], s_ref[sl, :]
            for h in range(RET_HEADS):
                hs = slice(128 * h, 128 * h + 128)
                dmat, zeta, xi = d_ref[h], z_ref[h], x_ref[h]
                q = _rope(q_ref[sl, hs], c, s)
                k = _rope(k_ref[sl, hs], c, s) * scale
                qb, kb, vb = q.astype(BF16), k.astype(BF16), v_ref[sl, hs].astype(BF16)
                kzb = (k * zeta).astype(BF16)
                dov = do_ref[sl, hs]
                dob, doxb = dov.astype(BF16), (dov * xi).astype(BF16)
                stb = st_ref[h, ci].astype(BF16)
                dsn = dstate[h]
                dsnb = dsn.astype(BF16)
                scb = (_dot_nt(qb, kb) * dmat).astype(BF16)
                dscb = (_dot_nt(dob, vb) * dmat).astype(BF16)
                dq = _dot(dscb, kb) + _dot_nt(doxb, stb)
                dk = _dot_tn(dscb, qb) + _dot_nt(vb, dsnb) * zeta
                dv = _dot_tn(scb, dob) + _dot(kzb, dsnb)
                dstate[h] = dsn * g_ref[h][0:1, :] + _dot_tn(qb, doxb)
                dqkv_ref[sl, 128 * h:128 * h + 128] = _rope_t(dq, c, s).astype(BF16)
                dqkv_ref[sl, 512 + 128 * h:640 + 128 * h] = _rope_t(dk * scale, c, s).astype(BF16)
                dqkv_ref[sl, 1024 + 128 * h:1152 + 128 * h] = dv.astype(BF16)
            return carry

        lax.fori_loop(0, nch, chunk, 0)

    col, tab, sq, hv, g8, st, out = _ret_specs(nblk, True)
    return pl.pallas_call(
        body, out_shape=jax.ShapeDtypeStruct((lp, P_RET), BF16), grid=(nblk,),
        in_specs=[col(C_RQ), col(C_RK), col(C_RV), out, st, tab, tab, sq, hv, hv, g8],
        out_specs=pl.BlockSpec((BLK, P_RET), lambda i: (nblk - 1 - i, 0)),
        scratch_shapes=[pltpu.VMEM((RET_HEADS, 128, 128), F32)],
        compiler_params=_cp("arbitrary"), name=name)(proj, proj, proj, do, states, *tables)


def _gla_tables():
    c = GLA_CHUNK
    tri = np.tril(np.ones((c, c), np.float32))
    ones_qv = np.kron(np.eye(GLA_HEADS, dtype=np.float32), np.ones((GLA_DK, GLA_DV), np.float32))
    return (jnp.asarray(tri, BF16), jnp.asarray(tri.T.copy(), BF16), jnp.asarray(ones_qv, BF16),
            jnp.asarray(ones_qv.T.copy(), BF16))


def _split3(x):
    hi = x.astype(BF16)
    r1 = x - hi.astype(F32)
    mid = r1.astype(BF16)
    lo = (r1 - mid.astype(F32)).astype(BF16)
    return hi, mid, lo


def _tri_sum(tri, x):
    hi, mid, lo = _split3(x)
    return _dot(tri, hi) + _dot(tri, mid) + _dot(tri, lo)


def _head_masks(width, per):
    lane = lax.broadcasted_iota(jnp.int32, (1, width), 1)
    return [((lane >= per * h) & (lane < per * (h + 1))).astype(F32) for h in range(GLA_HEADS)]


def _stack_heads(x, masks):
    return jnp.concatenate([x * m for m in masks], axis=0)


def _gla_gate(ga, w2, b, ok, tri):
    z = _dot(ga.astype(BF16), w2) + b
    la = (jnp.minimum(z, 0.0) - jnp.log(1.0 + jnp.exp(-jnp.abs(z)))) * (1.0 / GLA_TAU)
    la = jnp.where(ok, la, 0.0)
    return z, _tri_sum(tri, la)


def _gla_rows(i_blk, ci, lp):
    c = GLA_CHUNK
    rows = i_blk * BLK + ci * c + lax.broadcasted_iota(jnp.int32, (c, 1), 0)
    return (rows >= PADF) & (rows < lp - BACK)


N_SUB = GLA_CHUNK // GLA_SUB - 1
N_SUB2 = GLA_SUB // GLA_SUB2 - 1


def _gla_masks():
    c, s1, s2 = GLA_CHUNK, GLA_SUB, GLA_SUB2
    sh1, sh2 = s1.bit_length() - 1, s2.bit_length() - 1
    r = lax.broadcasted_iota(jnp.int32, (c, GLA_QK), 0)
    blk, within = jnp.right_shift(r, sh1), jnp.bitwise_and(r, s1 - 1)
    grp = jnp.right_shift(within, sh2)
    rowm = [(blk == a).astype(F32) for a in range(1, N_SUB + 1)] + [(grp == b).astype(F32) for b in range(1, N_SUB2 + 1)]
    keym = ([(r < s1 * a).astype(F32) for a in range(1, N_SUB + 1)]
            + [(within < s2 * b).astype(F32) for b in range(1, N_SUB2 + 1)])
    rs = lax.broadcasted_iota(jnp.int32, (GLA_HEADS * c, c), 0)
    ts = lax.broadcasted_iota(jnp.int32, (GLA_HEADS * c, c), 1)
    same = (jnp.right_shift(jnp.bitwise_and(rs, c - 1), sh1) == jnp.right_shift(ts, sh1)).astype(F32)
    lag = [(jnp.bitwise_and(r, s2 - 1) >= j).astype(F32) for j in range(s2)]
    return rowm, keym, same, lag


def _gla_hats(qs, k, g, masks, hm_q):
    c, s1, s2 = GLA_CHUNK, GLA_SUB, GLA_SUB2
    rowm, keym, same, _ = masks
    refs = [g[s1 * a - 1:s1 * a, :] for a in range(1, N_SUB + 1)]
    for b in range(1, N_SUB2 + 1):
        refs.append(jnp.concatenate([jnp.broadcast_to(g[s1 * i + s2 * b - 1:s1 * i + s2 * b, :], (s1, GLA_QK))
                                     for i in range(c // s1)], axis=0))
    eqs = [jnp.exp(jnp.minimum(g - r, 0.0)) * m for r, m in zip(refs, rowm)]
    eks = [jnp.exp(jnp.minimum(r - g, 0.0)) * m for r, m in zip(refs, keym)]
    qhs, khs = [qs * e for e in eqs], [k * e for e in eks]
    qst = [_stack_heads(q, hm_q).astype(BF16) for q in qhs]
    khb = [x.astype(BF16) for x in khs]
    qa, qb = jnp.concatenate(qst[:N_SUB], axis=1), jnp.concatenate(qst[N_SUB:], axis=1)
    ka, kb = jnp.concatenate(khb[:N_SUB], axis=1), jnp.concatenate(khb[N_SUB:], axis=1)
    p = _dot_nt(qa, ka) + _dot_nt(qb, kb) * same
    return eqs, eks, qhs, khs, qa, qb, ka, kb, p


def _roll_rows(x, j):
    return x if j == 0 else pltpu.roll(x, j, 0)


def _gla(proj, w2p, b, tables, *, name):
    lp = proj.shape[0]
    nblk, c, s2 = lp // BLK, GLA_CHUNK, GLA_SUB2
    nch = BLK // c

    def body(q_ref, k_ref, v_ref, a_ref, w_ref, b_ref, tri_ref, ones_ref, o_ref, st_ref, state):
        i_blk = pl.program_id(0)

        @pl.when(i_blk == 0)
        def _():
            state[...] = jnp.zeros_like(state)
        hm_q = _head_masks(GLA_QK, GLA_DK)
        masks = _gla_masks()
        tri, ones_qv, w2, bias = tri_ref[...], ones_ref[...], w_ref[...], b_ref[...]

        def chunk(ci, carry):
            sl = pl.ds(pl.multiple_of(ci * c, c), c)
            ok = _gla_rows(i_blk, ci, lp)
            k, v = k_ref[sl, :], v_ref[sl, :]
            vb = v.astype(BF16)
            qs = q_ref[sl, :] * (GLA_DK ** -0.5)
            _, g = _gla_gate(a_ref[sl, :], w2, bias, ok, tri)
            last = g[c - 1:c, :]
            st = state[...]
            st_ref[ci] = st
            qst = _stack_heads(qs * jnp.exp(g), hm_q).astype(BF16)
            oi = _dot_nt(qst, st.astype(BF16))
            o = jnp.concatenate([oi[c * h:c * h + c, :] for h in range(GLA_HEADS)], axis=1)
            ke = k * jnp.exp(last - g)
            f = _dot_tn(vb, ke.astype(BF16))
            upd = f[0:GLA_DV, :] * hm_q[0]
            for h in range(1, GLA_HEADS):
                upd = upd + f[GLA_DV * h:GLA_DV * (h + 1), :] * hm_q[h]
            state[...] = st * jnp.exp(last) + upd
            p = _gla_hats(qs, k, g, masks, hm_q)[-1]
            ob = _dot(p.astype(BF16), vb)
            o = o + jnp.concatenate([ob[c * h:c * h + c, GLA_DV * h:GLA_DV * (h + 1)] for h in range(GLA_HEADS)],
                                    axis=1)
            ws = []
            for j in range(s2):
                ej = jnp.exp(jnp.minimum(g - _roll_rows(g, j), 0.0))
                ws.append((qs * _roll_rows(k, j) * ej * masks[3][j]).astype(BF16))
            ball = _dot(jnp.concatenate(ws, axis=0), ones_qv)
            for j in range(s2):
                o = o + ball[c * j:c * j + c, :] * _roll_rows(v, j)
            o_ref[sl, :] = o
            return carry

        lax.fori_loop(0, nch, chunk, 0)

    tri, _, ones_qv, _ = tables
    full = lambda arr: pl.BlockSpec(arr.shape, lambda i: (0,) * arr.ndim)
    return pl.pallas_call(
        body,
        out_shape=(jax.ShapeDtypeStruct((lp, GLA_V), F32), jax.ShapeDtypeStruct((lp // c, GLA_DV, GLA_QK), F32)),
        grid=(nblk,),
        in_specs=[pl.BlockSpec((BLK, GLA_QK), lambda i: (i, C_GQ // GLA_QK)),
                  pl.BlockSpec((BLK, GLA_QK), lambda i: (i, C_GK // GLA_QK)),
                  pl.BlockSpec((BLK, GLA_V), lambda i: (i, C_GV // GLA_V)),
                  pl.BlockSpec((BLK, 128), lambda i: (i, C_GA // 128)),
                  full(w2p), full(b), full(tri), full(ones_qv)],
        out_specs=(pl.BlockSpec((BLK, GLA_V), lambda i: (i, 0)),
                   pl.BlockSpec((nch, GLA_DV, GLA_QK), lambda i: (i, 0, 0))),
        scratch_shapes=[pltpu.VMEM((GLA_DV, GLA_QK), F32)],
        compiler_params=_cp("arbitrary"), name=name)(proj, proj, proj, proj, w2p, b, tri, ones_qv)


def _gla_bwd(proj, do, states, w2p, b, tables, *, name):
    lp = proj.shape[0]
    nblk, c, s1, s2 = lp // BLK, GLA_CHUNK, GLA_SUB, GLA_SUB2
    nch = BLK // c

    def body(q_ref, k_ref, v_ref, a_ref, do_ref, st_ref, w_ref, b_ref, tri_ref, trit_ref, ones_ref, onest_ref,
             dp_ref, dw_ref, db_ref, dstate, dqs_s, dk_s, dg_s, dv_s):
        i_blk = nblk - 1 - pl.program_id(0)

        @pl.when(pl.program_id(0) == 0)
        def _():
            dstate[...] = jnp.zeros_like(dstate)
            dw_ref[...] = jnp.zeros_like(dw_ref)
            db_ref[...] = jnp.zeros_like(db_ref)
        hm_q = _head_masks(GLA_QK, GLA_DK)
        hm_v = _head_masks(GLA_V, GLA_DV)
        masks = _gla_masks()
        tri, trit, ones_qv, ones_vq = tri_ref[...], trit_ref[...], ones_ref[...], onest_ref[...]
        w2, bias = w_ref[...], b_ref[...]
        rsum = lambda x: jnp.sum(x, axis=0, keepdims=True)

        def chunk(cc, carry):
            ci = nch - 1 - cc
            sl = pl.ds(pl.multiple_of(ci * c, c), c)
            ok = _gla_rows(i_blk, ci, lp)
            k, v, ga = k_ref[sl, :], v_ref[sl, :], a_ref[sl, :]
            vb = v.astype(BF16)
            qs = q_ref[sl, :] * (GLA_DK ** -0.5)
            z, g = _gla_gate(ga, w2, bias, ok, tri)
            last = g[c - 1:c, :]
            elast = jnp.exp(last)
            eg = jnp.exp(g)
            ekl = jnp.exp(last - g)
            qe, ke = qs * eg, k * ekl
            dov = do_ref[sl, :]
            st = st_ref[ci]
            dsn = dstate[...]
            qst = _stack_heads(qe, hm_q).astype(BF16)
            dost = jnp.concatenate([dov[:, GLA_DV * h:GLA_DV * (h + 1)] for h in range(GLA_HEADS)], axis=0).astype(BF16)
            dqe_st = _dot(dost, st.astype(BF16))
            dqe = dqe_st[0:c, :] * hm_q[0]
            for h in range(1, GLA_HEADS):
                dqe = dqe + dqe_st[c * h:c * h + c, :] * hm_q[h]
            dstate[...] = _dot_tn(dost, qst) + dsn * elast
            dlast = rsum(dsn * st) * elast
            df = _stack_heads(dsn, hm_q).astype(BF16)
            dv_s[...] = _dot_nt(ke.astype(BF16), df)
            dke = _dot(vb, df)
            xk = dke * ke
            dqs_s[...] = dqe * eg
            dk_s[...] = dke * ekl
            dg_s[...] = dqe * qe - xk
            dlast = dlast + rsum(xk)
            eqs, eks, qhs, khs, qa, qb, ka, kb, p = _gla_hats(qs, k, g, masks, hm_q)
            dost_v = _stack_heads(dov, hm_v).astype(BF16)
            dp = _dot_nt(dost_v, vb)
            dv_s[...] += _dot_tn(p.astype(BF16), dost_v)
            dpa, dpb = dp.astype(BF16), (dp * masks[2]).astype(BF16)
            dq_all = (_dot(dpa, ka), _dot(dpb, kb))
            dk_all = (_dot_tn(dpa, qa), _dot_tn(dpb, qb))
            for t in range(N_SUB + N_SUB2):
                lvl, i = (0, t) if t < N_SUB else (1, t - N_SUB)
                cols = slice(GLA_QK * i, GLA_QK * (i + 1))
                dq_st = dq_all[lvl][:, cols]
                dqh = dq_st[0:c, :] * hm_q[0]
                for h in range(1, GLA_HEADS):
                    dqh = dqh + dq_st[c * h:c * h + c, :] * hm_q[h]
                dkh = dk_all[lvl][:, cols]
                xq, xkh = dqh * qhs[t], dkh * khs[t]
                dqs_s[...] += dqh * eqs[t]
                dk_s[...] += dkh * eks[t]
                dg_s[...] += xq - xkh
                back_ref = xkh - xq
                if lvl == 0:
                    row = s1 * (i + 1) - 1
                    dg_s[row:row + 1, :] += rsum(back_ref)
                else:
                    for blk in range(c // s1):
                        row = s1 * blk + s2 * (i + 1) - 1
                        dg_s[row:row + 1, :] += rsum(back_ref[s1 * blk:s1 * blk + s1, :])
            kes, qes, ws, dbs = [], [], [], []
            for j in range(s2):
                em = jnp.exp(jnp.minimum(g - _roll_rows(g, j), 0.0)) * masks[3][j]
                kes.append(_roll_rows(k, j) * em)
                qes.append(qs * em)
                ws.append((qs * kes[j]).astype(BF16))
                dbs.append((dov * _roll_rows(v, j)).astype(BF16))
            ball = _dot(jnp.concatenate(ws, axis=0), ones_qv)
            dwall = _dot(jnp.concatenate(dbs, axis=0), ones_vq)
            for j in range(s2):
                back = (lambda x: x) if j == 0 else (lambda x, j=j: pltpu.roll(x, c - j, 0))
                dw = dwall[c * j:c * j + c, :]
                dv_s[...] += back(ball[c * j:c * j + c, :] * dov)
                dqs_s[...] += dw * kes[j]
                dk_s[...] += back(dw * qes[j])
                x = dw * qs * kes[j]
                dg_s[...] += x - back(x)
            dg_s[c - 1:c, :] += dlast
            dla = jnp.where(ok, _tri_sum(trit, dg_s[...]), 0.0)
            dz = dla * (1.0 / GLA_TAU) / (1.0 + jnp.exp(z))
            dzb = dz.astype(BF16)
            dp_ref[sl, 0:256] = (dqs_s[...] * (GLA_DK ** -0.5)).astype(BF16)
            dp_ref[sl, 256:512] = dk_s[...].astype(BF16)
            dp_ref[sl, 512:1024] = dv_s[...].astype(BF16)
            dp_ref[sl, 1024:1152] = _dot_nt(dzb, w2).astype(BF16)
            dp_ref[sl, 1152:1280] = jnp.zeros((c, 128), BF16)
            dw_ref[...] += _dot_tn(ga.astype(BF16), dzb)
            db_ref[...] += rsum(dz)
            return carry

        lax.fori_loop(0, nch, chunk, 0)

    tri, trit, ones_qv, ones_vq = tables
    full = lambda arr: pl.BlockSpec(arr.shape, lambda i: (0,) * arr.ndim)
    rev = lambda i: nblk - 1 - i
    return pl.pallas_call(
        body,
        out_shape=(jax.ShapeDtypeStruct((lp, P_GLA), BF16),
                   jax.ShapeDtypeStruct((128, GLA_QK), F32), jax.ShapeDtypeStruct((1, GLA_QK), F32)),
        grid=(nblk,),
        in_specs=[pl.BlockSpec((BLK, GLA_QK), lambda i: (rev(i), C_GQ // GLA_QK)),
                  pl.BlockSpec((BLK, GLA_QK), lambda i: (rev(i), C_GK // GLA_QK)),
                  pl.BlockSpec((BLK, GLA_V), lambda i: (rev(i), C_GV // GLA_V)),
                  pl.BlockSpec((BLK, 128), lambda i: (rev(i), C_GA // 128)),
                  pl.BlockSpec((BLK, GLA_V), lambda i: (rev(i), 0)),
                  pl.BlockSpec((nch, GLA_DV, GLA_QK), lambda i: (rev(i), 0, 0)),
                  full(w2p), full(b), full(tri), full(trit), full(ones_qv), full(ones_vq)],
        out_specs=(pl.BlockSpec((BLK, P_GLA), lambda i: (rev(i), 0)),
                   pl.BlockSpec((128, GLA_QK), lambda i: (0, 0)),
                   pl.BlockSpec((1, GLA_QK), lambda i: (0, 0))),
        scratch_shapes=[pltpu.VMEM((GLA_DV, GLA_QK), F32), pltpu.VMEM((c, GLA_QK), F32),
                        pltpu.VMEM((c, GLA_QK), F32), pltpu.VMEM((c, GLA_QK), F32), pltpu.VMEM((c, GLA_V), F32)],
        compiler_params=_cp("arbitrary"), name=name)(proj, proj, proj, proj, do, states, w2p, b, tri, trit, ones_qv, ones_vq)


def _as2d(a):
    return a.reshape(-1, a.shape[-1])


def _ew_tile(r):
    return _tile(r, (512, 256, 128, 64, 32, 16, 8))


def _add2(a, b, *, out_dtype, name):
    a2, b2 = _as2d(a), _as2d(b)
    r, n = a2.shape
    tm = _ew_tile(r)

    def body(a_ref, b_ref, o_ref):
        o_ref[...] = (a_ref[...] + b_ref[...]).astype(o_ref.dtype)

    blk = pl.BlockSpec((tm, n), lambda i: (i, 0))
    return pl.pallas_call(body, out_shape=jax.ShapeDtypeStruct((r, n), out_dtype), grid=(r // tm,), in_specs=[blk, blk],
                          out_specs=blk, compiler_params=_cp("parallel"), name=name)(a2, b2).reshape(a.shape)


def _sum_slots(own, q, *, name):
    shape = own.shape
    q3 = q.reshape(3, -1, shape[-1])
    own2 = _as2d(own)
    r, n = own2.shape
    tm = _ew_tile(r)

    def body(own_ref, q_ref, o_ref):
        f = lambda i: q_ref[i].astype(F32)
        o_ref[...] = ((own_ref[...].astype(F32) + f(0)) + f(1)) + f(2)

    blk = pl.BlockSpec((tm, n), lambda i: (i, 0))
    return pl.pallas_call(
        body, out_shape=jax.ShapeDtypeStruct((r, n), F32), grid=(r // tm,),
        in_specs=[blk, pl.BlockSpec((3, tm, n), lambda i: (0, i, 0))], out_specs=blk,
        compiler_params=_cp("parallel"), name=name)(own2, q3).reshape(shape)


def _adamw(w, g, m, v, *, name):
    shape = w.shape
    w2, g2, m2, v2 = _as2d(w), _as2d(g), _as2d(m), _as2d(v)
    r, n = w2.shape
    tm = _ew_tile(r)

    def body(w_ref, g_ref, m_ref, v_ref, d_ref, mo_ref, vo_ref):
        d_ref[...], mo_ref[...], vo_ref[...] = _adam_math(w_ref[...], g_ref[...], m_ref[...], v_ref[...])

    blk = pl.BlockSpec((tm, n), lambda i: (i, 0))
    o = jax.ShapeDtypeStruct((r, n), F32)
    d, mo, vo = pl.pallas_call(body, out_shape=(o, o, o), grid=(r // tm,), in_specs=[blk] * 4, out_specs=(blk,) * 3,
                               compiler_params=_cp("parallel"), name=name)(w2, g2, m2, v2)
    return d.reshape(shape), mo.reshape(shape), vo.reshape(shape)


def _adam_math(w, gv, m, v):
    c1 = 1.0 - ADAM_B1 ** ADAM_STEP
    c2 = 1.0 - ADAM_B2 ** ADAM_STEP
    mn = ADAM_B1 * m + (1.0 - ADAM_B1) * gv
    vn = ADAM_B2 * v + (1.0 - ADAM_B2) * (gv * gv)
    return -ADAM_LR * ((mn / c1) / (jnp.sqrt(vn / c2) + ADAM_EPS) + ADAM_WD * w), mn, vn


def _adamw_halves(w, m, v, mine, theirs, c, *, name):
    depth, rows, n = w.shape
    r2 = rows // 2
    tm = next(t for t in range(min(r2, 256), 0, -8) if r2 % t == 0)
    steps = r2 // tm

    def body(c_ref, w_ref, m_ref, v_ref, *rest):
        halves, (g_ref, d_ref, mo_ref, vo_ref) = rest[:2 * depth], rest[2 * depth:]
        l, h = pl.program_id(0), pl.program_id(1)
        gv = None
        for k in range(depth):
            gk = jnp.where(h == c_ref[0], halves[2 * k][...], halves[2 * k + 1][...])
            gv = gk if gv is None else jnp.where(l == k, gk, gv)
        g_ref[...] = gv
        d_ref[...], mo_ref[...], vo_ref[...] = _adam_math(w_ref[...], gv, m_ref[...], v_ref[...])

    big = pl.BlockSpec((tm, n), lambda l, h, i, c_ref: ((2 * l + h) * steps + i, 0))
    half = lambda k: pl.BlockSpec((tm, n), lambda l, h, i, c_ref: (jnp.where(l == k, i, 0), 0))
    o = jax.ShapeDtypeStruct((depth * rows, n), F32)
    args = [a for k in range(depth) for a in (mine[k], theirs[k])]
    outs = pl.pallas_call(
        body, out_shape=(o, o, o, o),
        grid_spec=pltpu.PrefetchScalarGridSpec(
            num_scalar_prefetch=1, grid=(depth, 2, steps),
            in_specs=[big, big, big] + [half(k) for k in range(depth) for _ in range(2)], out_specs=(big,) * 4),
        compiler_params=_cp("arbitrary", "arbitrary", "arbitrary"), name=name)(
            jnp.reshape(c, (1,)).astype(jnp.int32), _as2d(w), _as2d(m), _as2d(v), *args)
    return [a.reshape(w.shape) for a in outs]


ANY = pl.BlockSpec(memory_space=pl.ANY)


def _place():
    return lax.axis_index("x"), lax.axis_index("y"), lax.axis_index("c")


def _other_chips(x, y):
    return [(1 - x, y), (x, 1 - y), (1 - x, 1 - y)]


def _remote(src, dst, ssem, rsem, dev):
    return pltpu.make_async_remote_copy(src_ref=src, dst_ref=dst, send_sem=ssem, recv_sem=rsem, device_id=dev,
                                        device_id_type=MESH)


def _allgather_chips(arrs, *, name):
    n = len(arrs)

    def body(*refs):
        ins, outs = refs[:n], refs[n:2 * n]
        s1, r1, s2, r2 = refs[2 * n:]
        x, y, c = _place()
        q = 2 * x + y
        chips = _other_chips(x, y)
        qs = [2 * cx + cy for cx, cy in chips]
        sib = (x, y, 1 - c)
        first, passed = [], []
        for k in range(n):
            for j, chip in enumerate(chips):
                first.append(_remote(ins[k].at[c], outs[k].at[c, q], s1.at[k, j], r1.at[k, j], (*chip, c)))
        for cp in first:
            cp.start()
        for k in range(n):
            for j, chip in enumerate(chips):
                land = outs[k].at[c, qs[j]]
                _remote(land, land, s1.at[k, j], r1.at[k, j], (*chip, c)).wait_recv()
                fw = _remote(land, land, s2.at[k, j], r2.at[k, j], sib)
                fw.start()
                passed.append(fw)
        for k in range(n):
            for j in range(3):
                land = outs[k].at[1 - c, qs[j]]
                _remote(land, land, s2.at[k, j], r2.at[k, j], sib).wait_recv()
        for cp in first + passed:
            cp.wait_send()

    sem = pltpu.SemaphoreType.DMA
    outs = pl.pallas_call(
        body, out_shape=tuple(jax.ShapeDtypeStruct((2, 4) + a.shape[1:], a.dtype) for a in arrs),
        in_specs=[ANY] * n, out_specs=(ANY,) * n,
        scratch_shapes=[sem((n, 3)), sem((n, 3)), sem((n, 3)), sem((n, 3))], name=name)(*arrs)
    chip = 2 * lax.axis_index("x") + lax.axis_index("y")
    return [lax.dynamic_update_slice_in_dim(o, a[:, None], chip, axis=1) for o, a in zip(outs, arrs)]


def _pair_exchange(arrs, *, name):
    n = len(arrs)

    def body(*refs):
        ins, outs = refs[:n], refs[n:2 * n]
        ssem, rsem = refs[2 * n:]
        x, y, c = _place()
        cps = [_remote(ins[k].at[:, 1 - c], outs[k], ssem.at[k], rsem.at[k], (x, y, 1 - c)) for k in range(n)]
        for cp in cps:
            cp.start()
        for cp in cps:
            cp.wait()

    sem = pltpu.SemaphoreType.DMA
    return pl.pallas_call(
        body, out_shape=tuple(jax.ShapeDtypeStruct((a.shape[0],) + a.shape[2:], a.dtype) for a in arrs),
        in_specs=[ANY] * n, out_specs=(ANY,) * n, scratch_shapes=[sem((n,)), sem((n,))], name=name)(*arrs)


def _pair_sum(mine, theirs, c, *, name):
    _, _, r, n = mine.shape
    tm = r if r <= 512 else _ew_tile(r)

    def body(c_ref, a_ref, b_ref, o_ref):
        o_ref[...] = (a_ref[...] + b_ref[...]).astype(BF16)

    blk = pl.BlockSpec((None, tm, n), lambda s, i, c_ref: (s, i, 0))
    return pl.pallas_call(
        body, out_shape=jax.ShapeDtypeStruct((4, r, n), BF16),
        grid_spec=pltpu.PrefetchScalarGridSpec(
            num_scalar_prefetch=1, grid=(4, r // tm),
            in_specs=[pl.BlockSpec((None, None, tm, n), lambda s, i, c_ref: (s, c_ref[0], i, 0)), blk], out_specs=blk),
        compiler_params=_cp("parallel", "parallel"), name=name)(jnp.reshape(c, (1,)).astype(jnp.int32), mine, theirs)


def _chip_copies(ins, outs, ssem, rsem, mode):
    x, y, c = _place()
    q = 2 * x + y
    sends, recvs = [], []
    for k in range(len(ins)):
        for j, (cx, cy) in enumerate(_other_chips(x, y)):
            sem = (ssem.at[k, j], rsem.at[k, j], (cx, cy, c))
            if mode == "scatter":
                sends.append(_remote(ins[k].at[2 * cx + cy], outs[k].at[j], *sem))
                recvs.append(sends[-1])
            else:
                sends.append(_remote(ins[k].at[c], outs[k].at[2 * q + c], *sem))
                recvs.append(_remote(ins[k].at[c], outs[k].at[2 * (2 * cx + cy) + c], *sem))
    return sends, recvs


def _chip_wait(sends, recvs):
    for cp in sends:
        cp.wait_send()
    for cp in recvs:
        cp.wait_recv()


def _landing_shape(a, mode):
    return jax.ShapeDtypeStruct(((3,) if mode == "scatter" else (8,)) + a.shape[1:], a.dtype)


def _chip_exchange(arrs, mode, *, name):
    n = len(arrs)

    def body(*refs):
        ins, outs = refs[:n], refs[n:2 * n]
        ssem, rsem = refs[2 * n:]
        sends, recvs = _chip_copies(ins, outs, ssem, rsem, mode)
        for cp in sends:
            cp.start()
        _chip_wait(sends, recvs)

    sem = pltpu.SemaphoreType.DMA
    return list(pl.pallas_call(
        body, out_shape=tuple(_landing_shape(a, mode) for a in arrs),
        in_specs=[ANY] * n, out_specs=(ANY,) * n, scratch_shapes=[sem((n, 3)), sem((n, 3))], name=name)(*arrs))


def _pair_fill(bufs, owns, *, name):
    n = len(bufs)

    def body(*refs):
        own, outs = refs[n:2 * n], refs[2 * n:3 * n]
        ssem, rsem = refs[3 * n:]
        x, y, c = _place()
        q = 2 * x + y
        sib = (x, y, 1 - c)
        sends, recvs = [], []
        for k in range(n):
            for j, (cx, cy) in enumerate(_other_chips(x, y)):
                mine, theirs = outs[k].at[2 * (2 * cx + cy) + c], outs[k].at[2 * (2 * cx + cy) + 1 - c]
                sends.append(_remote(mine, mine, ssem.at[k, j], rsem.at[k, j], sib))
                recvs.append(_remote(mine, theirs, ssem.at[k, j], rsem.at[k, j], sib))
            slots = outs[k].at[pl.ds(2 * q, 2)]
            sends.append(_remote(own[k], slots, ssem.at[k, 3], rsem.at[k, 3], sib))
            recvs.append(sends[-1])
        for cp in sends:
            cp.start()
        _chip_wait(sends, recvs)

    sem = pltpu.SemaphoreType.DMA
    return list(pl.pallas_call(
        body, out_shape=tuple(jax.ShapeDtypeStruct(b.shape, b.dtype) for b in bufs),
        in_specs=[ANY] * (2 * n), out_specs=(ANY,) * n, scratch_shapes=[sem((n, 4)), sem((n, 4))],
        input_output_aliases={k: k for k in range(n)}, name=name)(*bufs, *owns))


def _pair_swap(arrs, *, name):
    n = len(arrs)

    def body(*refs):
        ins, outs = refs[:n], refs[n:2 * n]
        ssem, rsem = refs[2 * n:]
        x, y, c = _place()
        cps = [_remote(ins[k], outs[k], ssem.at[k], rsem.at[k], (x, y, 1 - c)) for k in range(n)]
        for cp in cps:
            cp.start()
        for cp in cps:
            cp.wait()

    sem = pltpu.SemaphoreType.DMA
    return pl.pallas_call(
        body, out_shape=tuple(jax.ShapeDtypeStruct(a.shape, a.dtype) for a in arrs),
        in_specs=[ANY] * n, out_specs=(ANY,) * n, scratch_shapes=[sem((n,)), sem((n,))], name=name)(*arrs)


def _allreduce_small(slab, *, name):
    r, n = slab.shape

    def body(x_ref, o_ref, buf, ssem, rsem):
        x, y, c = _place()
        me = 4 * x + 2 * y + c
        buf[me] = x_ref[...]
        cps = []
        for rel in range(1, 8):
            bx, by, bc = (rel >> 2) & 1, (rel >> 1) & 1, rel & 1
            px, py, pc = (x + bx) % 2, (y + by) % 2, (c + bc) % 2
            cps.append((_remote(x_ref, buf.at[me], ssem.at[rel - 1], rsem.at[rel - 1], (px, py, pc)),
                        4 * px + 2 * py + pc, (px, py, pc)))
        for cp, _, _ in cps:
            cp.start()
        for rel, (cp, peer, dev) in enumerate(cps):
            cp.wait_send()
            _remote(x_ref, buf.at[peer], ssem.at[rel], rsem.at[rel], dev).wait_recv()
        acc = buf[0]
        for k in range(1, 8):
            acc = acc + buf[k]
        o_ref[...] = acc

    vm = pl.BlockSpec(memory_space=pltpu.VMEM)
    sem = pltpu.SemaphoreType.DMA
    return pl.pallas_call(
        body, out_shape=jax.ShapeDtypeStruct((r, n), F32), in_specs=[vm], out_specs=vm,
        scratch_shapes=[pltpu.VMEM((8, r, n), F32), sem((7,)), sem((7,))], name=name)(slab)


def _slab(arrs, row_mult):
    flat = jnp.concatenate([a.reshape(-1) for a in arrs])
    unit = 128 * row_mult
    total = -(-flat.size // unit) * unit
    return jnp.pad(flat, (0, total - flat.size)).reshape(-1, 128)


def _unslab(slab, shapes):
    flat = slab.reshape(-1)
    out, off = [], 0
    for s in shapes:
        size = int(np.prod(s))
        out.append(flat[off:off + size].reshape(s))
        off += size
    return out


def _cols_from_chips(a):
    return jnp.transpose(a, (1, 0, 2)).reshape(a.shape[1], -1)


def _cols_to_chips(a, parts):
    r = a.shape[0]
    return jnp.transpose(a.reshape(r, parts, -1), (1, 0, 2))


BIG = ("w_in", "w_out", "up", "down")
GATHER_RIDES = {("proj", 0): (("w_out", 0), ("up", 0)), ("mix_out", 0): (("down", 0),),
                ("ffn_up_a", 0): (("w_in", 1), ("w_out", 1)), ("ffn_up_g", 0): (("up", 1),),
                ("proj", 1): (("down", 1),)}
REDUCE_RIDES = {("ffn_down_dx", 0): (("up",), 1), ("ffn_down_dw", 0): (("w_in", "w_out"), 1),
                ("mix_out_dx", 0): (("down",), 1),
                ("proj_dx", 0): (("up",), 0), ("proj_dw_0", 0): (("down",), 0), ("proj_dw_1", 0): (("w_out",), 0)}


class _LocalWeights:
    def __init__(self, meta, win, wout, up_a, up_g, down, w2p, cw):
        self._meta, self._w = meta, {"win": win, "wout": wout, "up_a": up_a, "up_g": up_g, "down": down, "w2p": w2p,
                                     "cw": cw}

    def meta(self):
        return self._meta

    def get(self, kind, l):
        return self._w[kind][l]

    def mm(self, site, l, a, b, fn=None, **kw):
        return (fn or _mm)(a, b, name=site, **kw)

    def grads_done(self, l, g, kinds):
        pass


class _ChipWeights:
    def __init__(self, w_in, w_out, ffn_up, ffn_down, meta_tokens, gla_gate_w2, ffn_conv_w):
        self.x, self.y, self.c = _place()
        self.q = 2 * self.x + self.y
        halves = lambda a: a.astype(BF16).reshape(2, a.shape[0] // 2, a.shape[1])
        self.own = {(k, l): halves(a[l]) for k, a in zip(BIG, (w_in, w_out, ffn_up, ffn_down)) for l in range(DEPTH)}
        self.landed, self.swapped, self.full, self.n_swaps = {}, {}, {}, 0
        self.sh_shapes = [meta_tokens.shape, gla_gate_w2.shape, ffn_conv_w.shape]
        self.own["small", 0] = _slab([meta_tokens, gla_gate_w2, ffn_conv_w], 16).reshape(2, -1, 128)
        first = [("w_in", 0), ("small", 0)]
        for key, arr in zip(first, _chip_exchange([self.own[k] for k in first], "bcast", name="gather_first")):
            self.landed[key] = arr
        sh = self._whole("small", 0).reshape(4, -1, 128)
        parts = [_unslab(sh[k], self.sh_shapes) for k in range(4)]
        self._meta = jnp.concatenate([p[0] for p in parts], axis=-1)
        self.w2 = jnp.concatenate([p[1] for p in parts], axis=-1)
        self.cw = jnp.concatenate([p[2] for p in parts], axis=-1)
        self.partial, self.slots = {}, {}

    def _whole(self, kind, l):
        if (kind, l) not in self.full:
            keys = [k for k in self.landed if k not in self.full]
            got = _pair_fill([self.landed[k] for k in keys], [self.own[k] for k in keys],
                             name=f"gather_fill_{self.n_swaps}")
            self.n_swaps += 1
            for k, buf in zip(keys, got):
                self.full[k] = buf.reshape(4, 2 * buf.shape[1], buf.shape[2])
        return self.full[kind, l]

    def meta(self):
        return self._meta

    def get(self, kind, l):
        if kind == "win":
            return _to_kernel_cols(_cols_from_chips(self._whole("w_in", l)))
        if kind == "wout":
            return self._whole("w_out", l).reshape(D_MODEL, D_MODEL)
        if kind == "up_a":
            return _cols_from_chips(self._whole("up", l)[0:2])
        if kind == "up_g":
            return _cols_from_chips(self._whole("up", l)[2:4])
        if kind == "down":
            return self._whole("down", l).reshape(D_FF, D_MODEL)
        if kind == "w2p":
            return jnp.pad(self.w2[l], ((0, 128 - GLA_RANK), (0, 0))).astype(BF16)
        return self.cw[l]

    def mm(self, site, l, a, b, fn=None, **kw):
        fn = fn or _mm
        if (site, l) in GATHER_RIDES:
            keys = GATHER_RIDES[site, l]
            out, got = fn(a, b, name=site, carry=([self.own[k] for k in keys], "bcast"), **kw)
            self.landed.update(zip(keys, got))
            return out
        if (site, l) in REDUCE_RIDES:
            kinds, gl = REDUCE_RIDES[site, l]
            keys = [(k, gl) for k in kinds]
            if all(k in self.partial and k not in self.slots for k in keys):
                out, got = fn(a, b, name=site, carry=([self.partial[k] for k in keys], "scatter"), **kw)
                self.slots.update(zip(keys, got))
                return out
        return fn(a, b, name=site, **kw)

    def grads_done(self, l, g, kinds):
        split = lambda a: a.reshape(4, 2, a.shape[-2] // 2, a.shape[-1]) if a.ndim == 3 else \
            a.reshape(4, 2, a.shape[0] // 8, a.shape[1])
        src = {"w_in": lambda: g["w_in"][l], "w_out": lambda: g["w_out"][l],
               "up": lambda: g["up"][l], "down": lambda: g["down"][l]}
        big = {k: split(src[k]()) for k in kinds}
        from_sib = _pair_exchange([big[k] for k in kinds], name=f"grads_pair_exchange_{l}_{kinds[0]}")
        for k, theirs in zip(kinds, from_sib):
            self.partial[k, l] = _pair_sum(big[k], theirs, self.c, name=f"pair_sum_{k}_{l}")

    def reduce(self):
        keys = [(k, l) for l in range(DEPTH) for k in BIG]
        late = [k for k in keys if k not in self.slots]
        self.slots.update(zip(late, _chip_exchange([self.partial[k] for k in late], "scatter",
                                                   name="grads_chip_exchange")))
        half = {}
        for k in keys:
            own = lax.dynamic_index_in_dim(self.partial[k], self.q, 0, keepdims=False)
            half[k] = _sum_slots(own, self.slots[k], name=f"chip_sum_{k[0]}_{k[1]}")
        other = dict(zip(keys, _pair_swap([half[k] for k in keys], name="grads_pair_swap")))
        return [([half[k, l] for l in range(DEPTH)], [other[k, l] for l in range(DEPTH)]) for k in BIG]


def _local_step(x_rows, target_rows, wts, pre_mix_norm, gla_gate_b, ret_norm_w, gla_norm_w, post_mix_norm,
                pre_ffn_norm, ffn_conv_b, post_ffn_norm):
    d = D_MODEL
    lp = x_rows.shape[0] + FRONT + BACK
    row = lambda a, l: a[l][None, :]
    rtab = _ret_tables(lp)
    gtab = _gla_tables()
    h0 = jnp.concatenate([jnp.zeros((PADF, d), F32), wts.meta(), x_rows, jnp.zeros((BACK, d), F32)], axis=0)
    target = jnp.pad(target_rows, ((FRONT, BACK), (0, 0)))

    saved = []
    h = h0
    _, hn = _resid_norm(h0, None, None, row(pre_mix_norm, 0), name="norm_in")
    loss_local = dy = None
    for l in range(DEPTH):
        s = {"h_in": h, "hn": hn}
        s["proj"] = wts.mm("proj", l, hn, wts.get("win", l))
        s["o_ret"], s["st_ret"] = _retention(s["proj"], rtab, name="retention")
        s["o_gla"], s["st_gla"] = _gla(s["proj"], wts.get("w2p", l), row(gla_gate_b, l), gtab, name="gla")
        s["merged"] = _merge(s["o_ret"], s["o_gla"], s["proj"], row(ret_norm_w, l), row(gla_norm_w, l), name="merge")
        s["m"] = wts.mm("mix_out", l, s["merged"], wts.get("wout", l))
        s["h_mid"], s["hn2"] = _resid_norm(h, s["m"], row(post_mix_norm, l), row(pre_ffn_norm, l), name="resid_mix")
        s["ua"] = wts.mm("ffn_up_a", l, s["hn2"], wts.get("up_a", l))
        s["ug"] = wts.mm("ffn_up_g", l, s["hn2"], wts.get("up_g", l))
        cw_a, cw_g = wts.get("cw", l)[:, :D_FF], wts.get("cw", l)[:, D_FF:]
        cb_a, cb_g = ffn_conv_b[l][None, :D_FF], ffn_conv_b[l][None, D_FF:]
        s["conv"] = (cw_a, cw_g, cb_a, cb_g)
        s["act"], s["f"] = _conv_act_down(s["ua"], s["ug"], cw_a, cw_g, cb_a, cb_g, wts.get("down", l),
                                          name="conv_act_down")
        if l + 1 < DEPTH:
            h, hn = _resid_norm(s["h_mid"], s["f"], row(post_ffn_norm, l), row(pre_mix_norm, l + 1), name="resid_ffn")
        else:
            loss_local, dy, df_last, dw_last = _loss_head(s["h_mid"], s["f"], row(post_ffn_norm, l), target,
                                                          name="loss_head")
        saved.append(s)

    g = {k: [None] * DEPTH for k in ("pre_mix", "w_in", "w2", "gb", "ret_n", "gla_n", "w_out", "post_mix", "pre_ffn",
                                     "up", "cw", "cb", "down", "post_ffn")}
    dh_out, dhn_next = dy, None
    for l in reversed(range(DEPTH)):
        s = saved[l]
        cw_a, cw_g, cb_a, cb_g = s["conv"]
        if l + 1 < DEPTH:
            dh, df, g["pre_mix"][l + 1], g["post_ffn"][l] = _resid_norm_bwd(
                dh_out, dhn_next, saved[l + 1]["h_in"], s["f"], row(pre_mix_norm, l + 1), row(post_ffn_norm, l),
                name="resid_ffn_bwd")
        else:
            dh, df, g["post_ffn"][l] = dh_out, df_last, dw_last
        dact = wts.mm("ffn_down_dx", l, df, wts.get("down", l), nt=True)
        g["down"][l] = wts.mm("ffn_down_dw", l, s["act"], df, fn=_mm_tn, tn=512)
        du_a, du_g, dcw_a, dcw_g, dcb_a, dcb_g, dhn2 = _conv_act_bwd(
            s["ua"], s["ug"], dact, cw_a, cw_g, cb_a, cb_g, wts.get("up_a", l), wts.get("up_g", l), name="conv_act_bwd")
        g["cw"][l] = jnp.concatenate([dcw_a, dcw_g], axis=1)
        g["cb"][l] = jnp.concatenate([dcb_a, dcb_g], axis=1)[0]
        half_up = _mm_tn(s["hn2"], du_a, tn=D_FF // 2, blocks=(4, 0), name="ffn_up_a_dw")
        g["up"][l] = _mm_tn(s["hn2"], du_g, tn=D_FF // 2, blocks=(4, 2), into=half_up, name="ffn_up_g_dw")
        dh, dm, g["pre_ffn"][l], g["post_mix"][l] = _resid_norm_bwd(
            dh, dhn2, s["h_mid"], s["m"], row(pre_ffn_norm, l), row(post_mix_norm, l), name="resid_mix_bwd")
        g["w_out"][l] = _mm_tn(s["merged"], dm, name="mix_out_dw")
        wts.grads_done(l, g, ("w_out", "up", "down"))
        dmerged = wts.mm("mix_out_dx", l, dm, wts.get("wout", l), nt=True)
        do_ret, do_gla, d_gate, g["ret_n"][l], g["gla_n"][l] = _merge_bwd(
            dmerged, s["o_ret"], s["o_gla"], s["proj"], row(ret_norm_w, l), row(gla_norm_w, l), name="merge_bwd")
        d_ret = _retention_bwd(s["proj"], do_ret, s["st_ret"], rtab, name="retention_bwd")
        d_gla, dw2, dgb = _gla_bwd(s["proj"], do_gla, s["st_gla"], wts.get("w2p", l), row(gla_gate_b, l), gtab,
                                   name="gla_bwd")
        g["w2"][l], g["gb"][l] = dw2[:GLA_RANK], dgb[0]
        pieces = (d_ret, d_gate, d_gla)
        g["w_in"][l] = _to_reference_chips(*[wts.mm(f"proj_dw_{i}", l, s["hn"], p, fn=_mm_tn)
                                             for i, p in enumerate(pieces)])
        win = wts.get("win", l)
        dhn_next = wts.mm("proj_dx", l, pieces, [win[:, 0:P_RET], win[:, P_RET:P_RET + P_GATE], win[:, P_RET + P_GATE:]],
                          fn=_mm_nt_sum)
        dh_out = dh
        wts.grads_done(l, g, ("w_in",))
    dh0, _, g["pre_mix"][0], _ = _resid_norm_bwd(dh_out, dhn_next, h0, None, row(pre_mix_norm, 0), None,
                                                 name="norm_in_bwd")
    return loss_local, dh0, g


def kernel(x, meta_tokens, pre_mix_norm, w_in, gla_gate_w2, gla_gate_b, ret_norm_w, gla_norm_w, w_out, post_mix_norm, pre_ffn_norm, ffn_up, ffn_conv_w, ffn_conv_b, ffn_down, post_ffn_norm, loss_target, m_meta_tokens, m_pre_mix_norm, m_w_in, m_gla_gate_w2, m_gla_gate_b, m_ret_norm_w, m_gla_norm_w, m_w_out, m_post_mix_norm, m_pre_ffn_norm, m_ffn_up, m_ffn_conv_w, m_ffn_conv_b, m_ffn_down, m_post_ffn_norm, v_meta_tokens, v_pre_mix_norm, v_w_in, v_gla_gate_w2, v_gla_gate_b, v_ret_norm_w, v_gla_norm_w, v_w_out, v_post_mix_norm, v_pre_ffn_norm, v_ffn_up, v_ffn_conv_w, v_ffn_conv_b, v_ffn_down, v_post_ffn_norm):
    xi, yi, ci = _place()
    chip = 2 * xi + yi
    seq = x.shape[1]
    d = D_MODEL
    wts = _ChipWeights(w_in, w_out, ffn_up, ffn_down, meta_tokens, gla_gate_w2, ffn_conv_w)
    loss_local, dh0, g = _local_step(x[0], loss_target[0], wts, pre_mix_norm, gla_gate_b, ret_norm_w, gla_norm_w,
                                     post_mix_norm, pre_ffn_norm, ffn_conv_b, post_ffn_norm)
    grad_x = dh0[FRONT:FRONT + seq][None]
    names = ("w_in", "w_out", "ffn_up", "ffn_down")
    big_halves = wts.reduce()

    small_full = [dh0[PADF:FRONT], jnp.stack(g["pre_mix"])[:, 0], jnp.stack(g["w2"]), jnp.stack(g["gb"]),
                  jnp.stack(g["ret_n"])[:, 0], jnp.stack(g["gla_n"])[:, 0], jnp.stack(g["post_mix"])[:, 0],
                  jnp.stack(g["pre_ffn"])[:, 0], jnp.stack(g["cw"]), jnp.stack(g["cb"]),
                  jnp.stack(g["post_ffn"])[:, 0]]
    small_sum = _unslab(_allreduce_small(_slab(small_full, 8), name="small_allreduce"), [a.shape for a in small_full])
    (g_meta, g_pre_mix, g_w2, g_gb, g_ret_n, g_gla_n, g_post_mix, g_pre_ffn, g_cw, g_cb, g_post_ffn) = small_sum
    g_meta = lax.dynamic_slice_in_dim(g_meta, chip * 256, 256, axis=1)
    g_w2 = lax.dynamic_slice_in_dim(g_w2, chip * 64, 64, axis=2)
    g_cw = lax.dynamic_slice_in_dim(g_cw, chip * 1408, 1408, axis=2)

    grads = [g_meta, g_pre_mix, None, g_w2, g_gb, g_ret_n, g_gla_n, None, g_post_mix, g_pre_ffn, None,
             g_cw, g_cb, None, g_post_ffn]
    ws = [meta_tokens, pre_mix_norm, w_in, gla_gate_w2, gla_gate_b, ret_norm_w, gla_norm_w, w_out, post_mix_norm,
          pre_ffn_norm, ffn_up, ffn_conv_w, ffn_conv_b, ffn_down, post_ffn_norm]
    ms = [m_meta_tokens, m_pre_mix_norm, m_w_in, m_gla_gate_w2, m_gla_gate_b, m_ret_norm_w, m_gla_norm_w, m_w_out,
          m_post_mix_norm, m_pre_ffn_norm, m_ffn_up, m_ffn_conv_w, m_ffn_conv_b, m_ffn_down, m_post_ffn_norm]
    vs = [v_meta_tokens, v_pre_mix_norm, v_w_in, v_gla_gate_w2, v_gla_gate_b, v_ret_norm_w, v_gla_norm_w, v_w_out,
          v_post_mix_norm, v_pre_ffn_norm, v_ffn_up, v_ffn_conv_w, v_ffn_conv_b, v_ffn_down, v_post_ffn_norm]
    big_idx = (2, 7, 10, 13)
    deltas, new_m, new_v = [None] * 15, [None] * 15, [None] * 15
    for i, nm, (mine, theirs) in zip(big_idx, names, big_halves):
        grads[i], deltas[i], new_m[i], new_v[i] = _adamw_halves(ws[i], ms[i], vs[i], mine, theirs, ci,
                                                                name=f"adamw_{nm}")
    small_idx = [i for i in range(15) if i not in big_idx]
    shapes = [ws[i].shape for i in small_idx]
    sd, sm, sv = _adamw(_slab([ws[i] for i in small_idx], 8), _slab([grads[i] for i in small_idx], 8),
                        _slab([ms[i] for i in small_idx], 8), _slab([vs[i] for i in small_idx], 8), name="adamw_small")
    for i, a, b, c_ in zip(small_idx, _unslab(sd, shapes), _unslab(sm, shapes), _unslab(sv, shapes)):
        deltas[i], new_m[i], new_v[i] = a, b, c_

    loss = lax.psum(loss_local, ("x", "y", "c"))
    return (loss, grad_x, *grads, *deltas, *new_m, *new_v)
```

```python
import functools
import math

import numpy as np
import jax
import jax.numpy as jnp
from jax import lax
from jax.experimental import pallas as pl
from jax.experimental.pallas import tpu as pltpu

F32 = jnp.float32
BF16 = jnp.bfloat16

D_MODEL = 1024
DEPTH = 2
N_META = 16
EPS = 1e-6
RET_HEADS = 4
RET_DK = 128
GLA_HEADS = 4
GLA_DK = 64
GLA_DV = 128
GLA_QK = GLA_HEADS * GLA_DK
GLA_V = GLA_HEADS * GLA_DV
GLA_RANK = 16
GLA_TAU = 16.0
D_FF = 2816
ROPE_BASE = 10000.0
IN_WIDTH = 3600
IN_PAD = 3840
C_RQ, C_RK, C_RV, C_RG, C_GR, C_GQ, C_GK, C_GV, C_GA = 0, 512, 1024, 1536, 2048, 2560, 2816, 3072, 3584
P_RET, P_GATE, P_GLA = 1536, 1024, 1280


def _to_kernel_cols(w):
    pad = jnp.zeros(w.shape[:-1] + (IN_PAD - IN_WIDTH,), w.dtype)
    return jnp.concatenate([w[..., 0:2048], w[..., 3072:3584], w[..., 2048:3072], w[..., 3584:3600], pad], axis=-1)


def _to_reference_chips(d_ret, d_gate, d_gla):
    segs = [(d_ret, 0, 0, 1536), (d_gate, 0, 1536, 512), (d_gla, 0, 2048, 1024), (d_gate, 512, 3072, 512),
            (d_gla, 1024, 3584, GLA_RANK)]
    per = IN_WIDTH // 4
    chips = []
    for j in range(4):
        lo, hi, parts = per * j, per * (j + 1), []
        for piece, p0, r0, width in segs:
            a, b = max(lo, r0), min(hi, r0 + width)
            if a < b:
                parts.append(piece[:, p0 + a - r0:p0 + b - r0])
        chips.append(jnp.concatenate(parts, axis=1))
    return jnp.stack(chips)

FRONT = 64
BACK = 64
PADF = FRONT - N_META
RET_CHUNK = 128
GLA_CHUNK = 64
GLA_SUB = 16
GLA_SUB2 = 4
BLK = 640

ADAM_LR, ADAM_B1, ADAM_B2, ADAM_EPS, ADAM_WD, ADAM_STEP = 0.001, 0.9, 0.999, 1e-08, 0.01, 10

VMEM_LIMIT = 56 * 2 ** 20
MM_VMEM_BUDGET = 40 * 2 ** 20
MESH = pl.DeviceIdType.MESH


def _cp(*sem):
    return pltpu.CompilerParams(dimension_semantics=sem, vmem_limit_bytes=VMEM_LIMIT)


def _tile(n, cands):
    for t in cands:
        if n % t == 0:
            return t
    raise ValueError(f"no tile for {n} in {cands}")


def _row_tile(n):
    return _tile(n, (640, 512, 320, 256, 128, 64))


def _mm(a, b, *, nt=False, add=None, out_dtype=F32, tn=None, name, carry=None):
    m, k = a.shape
    n = b.shape[0] if nt else b.shape[1]
    tm = _tile(m, (640, 320, 256, 128, 64))
    if tn is None:
        step_bytes = lambda t: 2 * (tm * k * a.dtype.itemsize + t * k * b.dtype.itemsize
                                    + tm * t * (jnp.dtype(out_dtype).itemsize + (4 if add is not None else 0)))
        tn = next(t for t in range(n, 0, -128) if n % t == 0 and (step_bytes(t) <= MM_VMEM_BUDGET or t == 128))
    dn = (((1,), (1,)), ((), ())) if nt else (((1,), (0,)), ((), ()))
    nj, ni = n // tn, m // tm
    n_in = 2 + (add is not None)
    c_arrs, c_mode = carry if carry is not None else ((), None)
    nc = len(c_arrs)

    def body(*refs):
        a_ref, b_ref = refs[:2]
        c_ref = refs[2] if add is not None else None
        o_ref = refs[n_in + nc]
        if nc:
            c_ins, c_outs = refs[n_in:n_in + nc], refs[n_in + nc + 1:n_in + 2 * nc + 1]
            ssem, rsem = refs[n_in + 2 * nc + 1:]
            j, i = pl.program_id(0), pl.program_id(1)

            @pl.when((j == 0) & (i == 0))
            def _():
                for cp in _chip_copies(c_ins, c_outs, ssem, rsem, c_mode)[0]:
                    cp.start()
        r = lax.dot_general(a_ref[...].astype(BF16), b_ref[...].astype(BF16), dn, preferred_element_type=F32)
        if add is not None:
            r = r + c_ref[...]
        o_ref[...] = r.astype(o_ref.dtype)
        if nc:
            @pl.when((j == nj - 1) & (i == ni - 1))
            def _():
                _chip_wait(*_chip_copies(c_ins, c_outs, ssem, rsem, c_mode))

    b_spec = pl.BlockSpec((tn, k), lambda j, i: (j, 0)) if nt else pl.BlockSpec((k, tn), lambda j, i: (0, j))
    in_specs = [pl.BlockSpec((tm, k), lambda j, i: (i, 0)), b_spec]
    args = [a, b]
    if add is not None:
        in_specs.append(pl.BlockSpec((tm, tn), lambda j, i: (i, j)))
        args.append(add)
    out_shape = jax.ShapeDtypeStruct((m, n), out_dtype)
    out_spec = pl.BlockSpec((tm, tn), lambda j, i: (i, j))
    if not nc:
        return pl.pallas_call(
            body, out_shape=out_shape, grid=(nj, ni), in_specs=in_specs, out_specs=out_spec,
            compiler_params=_cp("parallel", "parallel"), name=name)(*args)
    sem = pltpu.SemaphoreType.DMA
    outs = pl.pallas_call(
        body, out_shape=(out_shape,) + tuple(_landing_shape(x, c_mode) for x in c_arrs), grid=(nj, ni),
        in_specs=in_specs + [ANY] * nc, out_specs=(out_spec,) + (ANY,) * nc,
        scratch_shapes=[sem((nc, 3)), sem((nc, 3))],
        compiler_params=_cp("arbitrary", "arbitrary"), name=name)(*args, *c_arrs)
    return outs[0], list(outs[1:])


def _call_with_carry(body, *, out_shape, grid, in_specs, out_specs, args, semantics, carry, name, aliases=None):
    if carry is None:
        return pl.pallas_call(body, out_shape=out_shape, grid=grid, in_specs=in_specs, out_specs=out_specs,
                              input_output_aliases=aliases or {}, compiler_params=_cp(*semantics), name=name)(*args)
    c_arrs, c_mode = carry
    n_in, nc = len(args), len(c_arrs)

    def carried(*refs):
        c_ins, c_outs = refs[n_in:n_in + nc], refs[n_in + nc + 1:n_in + 2 * nc + 1]
        ssem, rsem = refs[n_in + 2 * nc + 1:]
        ids = [pl.program_id(d) for d in range(len(grid))]
        first = functools.reduce(lambda u, v: u & v, [i == 0 for i in ids])
        last = functools.reduce(lambda u, v: u & v, [i == g - 1 for i, g in zip(ids, grid)])

        @pl.when(first)
        def _():
            for cp in _chip_copies(c_ins, c_outs, ssem, rsem, c_mode)[0]:
                cp.start()
        body(*refs[:n_in], refs[n_in + nc])

        @pl.when(last)
        def _():
            _chip_wait(*_chip_copies(c_ins, c_outs, ssem, rsem, c_mode))

    sem = pltpu.SemaphoreType.DMA
    outs = pl.pallas_call(
        carried, out_shape=(out_shape,) + tuple(_landing_shape(x, c_mode) for x in c_arrs), grid=grid,
        in_specs=list(in_specs) + [ANY] * nc, out_specs=(out_specs,) + (ANY,) * nc,
        scratch_shapes=[sem((nc, 3)), sem((nc, 3))], input_output_aliases=aliases or {},
        compiler_params=_cp(*(("arbitrary",) * len(grid))), name=name)(*args, *c_arrs)
    return outs[0], list(outs[1:])


def _mm_nt_sum(a_list, b_list, *, name, carry=None):
    m, n = a_list[0].shape[0], b_list[0].shape[0]
    tm = _tile(m, (640, 320, 256, 128, 64))
    np_ = len(a_list)

    def body(*refs):
        acc = None
        for a_ref, b_ref in zip(refs[:np_], refs[np_:2 * np_]):
            r = lax.dot_general(a_ref[...].astype(BF16), b_ref[...].astype(BF16), (((1,), (1,)), ((), ())),
                                preferred_element_type=F32)
            acc = r if acc is None else acc + r
        refs[2 * np_][...] = acc

    return _call_with_carry(
        body, out_shape=jax.ShapeDtypeStruct((m, n), F32), grid=(m // tm,),
        in_specs=[pl.BlockSpec((tm, a.shape[1]), lambda i: (i, 0)) for a in a_list]
        + [pl.BlockSpec(b.shape, lambda i: (0, 0)) for b in b_list],
        out_specs=pl.BlockSpec((tm, n), lambda i: (i, 0)), args=[*a_list, *b_list], semantics=("parallel",),
        carry=carry, name=name)


def _mm_tn(a, b, *, tn=None, blocks=None, into=None, name, carry=None):
    m, k = a.shape
    n = b.shape[1]
    tm = _tile(m, (1664, 640, 320, 256, 128, 64))
    tn = n if tn is None else tn
    if blocks is not None:
        total, first = blocks
        out_shape = jax.ShapeDtypeStruct((total, k, tn), F32)
        out_spec = pl.BlockSpec((None, k, tn), lambda j, i: (first + j, 0, 0))
    else:
        out_shape = jax.ShapeDtypeStruct((k, n), F32)
        out_spec = pl.BlockSpec((k, tn), lambda j, i: (0, j))

    def body(a_ref, b_ref, *rest):
        o_ref = rest[-1]

        @pl.when(pl.program_id(1) == 0)
        def _():
            o_ref[...] = jnp.zeros_like(o_ref)
        o_ref[...] += lax.dot_general(a_ref[...].astype(BF16), b_ref[...].astype(BF16),
                                      (((0,), (0,)), ((), ())), preferred_element_type=F32)

    in_specs = [pl.BlockSpec((tm, k), lambda j, i: (i, 0)), pl.BlockSpec((tm, tn), lambda j, i: (i, j))]
    args, alias = [a, b], {}
    if into is not None:
        in_specs.append(pl.BlockSpec(memory_space=pl.ANY))
        args.append(into)
        alias = {2: 0}
    return _call_with_carry(body, out_shape=out_shape, grid=(n // tn, m // tm), in_specs=in_specs, out_specs=out_spec,
                            args=args, semantics=("parallel", "arbitrary"), carry=carry, name=name, aliases=alias)


def _rms(x, w):
    r = lax.rsqrt(jnp.mean(x * x, axis=-1, keepdims=True) + EPS)
    return x * r * w


def _rms_bwd(x, w, dy):
    r = lax.rsqrt(jnp.mean(x * x, axis=-1, keepdims=True) + EPS)
    xh = x * r
    dxh = dy * w
    dx = r * (dxh - xh * jnp.mean(dxh * xh, axis=-1, keepdims=True))
    return dx, jnp.sum(dy * xh, axis=0, keepdims=True)


def _resid_norm(h, t, w_post, w_next, *, name):
    lp, d = h.shape
    tm = _row_tile(lp)
    has_t = t is not None

    def body(*refs):
        if has_t:
            h_ref, t_ref, wp_ref, wn_ref, ho_ref, hn_ref = refs
            hv = h_ref[...] + _rms(t_ref[...], wp_ref[...])
            ho_ref[...] = hv
        else:
            h_ref, wn_ref, hn_ref = refs
            hv = h_ref[...]
        hn_ref[...] = _rms(hv, wn_ref[...]).astype(BF16)

    row = pl.BlockSpec((tm, d), lambda i: (i, 0))
    vec = pl.BlockSpec((1, d), lambda i: (0, 0))
    if has_t:
        return pl.pallas_call(
            body, out_shape=(jax.ShapeDtypeStruct((lp, d), F32), jax.ShapeDtypeStruct((lp, d), BF16)),
            grid=(lp // tm,), in_specs=[row, row, vec, vec], out_specs=(row, row),
            compiler_params=_cp("parallel"), name=name)(h, t, w_post, w_next)
    return h, pl.pallas_call(
        body, out_shape=jax.ShapeDtypeStruct((lp, d), BF16), grid=(lp // tm,), in_specs=[row, vec],
        out_specs=row, compiler_params=_cp("parallel"), name=name)(h, w_next)


def _resid_norm_bwd(dh_out, dhn, h_new, t, w_next, w_post, *, name):
    lp, d = h_new.shape if h_new is not None else t.shape
    tm = _row_tile(lp)
    has_n = dhn is not None
    has_t = t is not None

    def body(*refs):
        refs = list(refs)
        dho_ref = refs.pop(0)
        if has_n:
            dhn_ref, hn_ref, wn_ref = refs.pop(0), refs.pop(0), refs.pop(0)
        if has_t:
            t_ref, wp_ref = refs.pop(0), refs.pop(0)
        dh_ref = refs.pop(0) if has_n else None
        dt_ref = refs.pop(0) if has_t else None
        dwn_ref = refs.pop(0) if has_n else None
        dwp_ref = refs.pop(0) if has_t else None
        first = pl.program_id(0) == 0
        dh = dho_ref[...]
        if has_n:
            dx, dwn = _rms_bwd(hn_ref[...], wn_ref[...], dhn_ref[...])
            dh = dh + dx
            dh_ref[...] = dh

            @pl.when(first)
            def _():
                dwn_ref[...] = jnp.zeros_like(dwn_ref)
            dwn_ref[...] += dwn
        if has_t:
            dt, dwp = _rms_bwd(t_ref[...], wp_ref[...], dh)
            dt_ref[...] = dt.astype(BF16)

            @pl.when(first)
            def _():
                dwp_ref[...] = jnp.zeros_like(dwp_ref)
            dwp_ref[...] += dwp

    row = pl.BlockSpec((tm, d), lambda i: (i, 0))
    vec = pl.BlockSpec((1, d), lambda i: (0, 0))
    args, in_specs, out_shape, out_specs = [dh_out], [row], [], []
    if has_n:
        args += [dhn, h_new, w_next]
        in_specs += [row, row, vec]
    if has_t:
        args += [t, w_post]
        in_specs += [row, vec]
    if has_n:
        out_shape.append(jax.ShapeDtypeStruct((lp, d), F32)); out_specs.append(row)
    if has_t:
        out_shape.append(jax.ShapeDtypeStruct((lp, d), BF16)); out_specs.append(row)
    if has_n:
        out_shape.append(jax.ShapeDtypeStruct((1, d), F32)); out_specs.append(vec)
    if has_t:
        out_shape.append(jax.ShapeDtypeStruct((1, d), F32)); out_specs.append(vec)
    outs = list(pl.pallas_call(body, out_shape=tuple(out_shape), grid=(lp // tm,), in_specs=in_specs,
                               out_specs=tuple(out_specs), compiler_params=_cp("arbitrary"), name=name)(*args))
    dh = outs.pop(0) if has_n else dh_out
    dt = outs.pop(0) if has_t else None
    dwn = outs.pop(0) if has_n else None
    dwp = outs.pop(0) if has_t else None
    return dh, dt, dwn, dwp


def _loss_head(h, f, w_post, target, *, name):
    lp, d = h.shape
    tm = _row_tile(lp)

    def body(h_ref, f_ref, w_ref, t_ref, loss_ref, dy_ref, df_ref, dw_ref):
        i = pl.program_id(0)
        f, w = f_ref[...], w_ref[...]
        y = h_ref[...] + _rms(f, w)
        rows = i * tm + lax.broadcasted_iota(jnp.int32, (tm, 1), 0)
        tok = (rows >= FRONT) & (rows < lp - BACK)
        err = jnp.where(tok, y - t_ref[...], 0.0)
        dy = err * (1.0 / d)
        dy_ref[...] = dy
        df, dw = _rms_bwd(f, w, dy)
        df_ref[...] = df.astype(BF16)

        @pl.when(i == 0)
        def _():
            loss_ref[...] = jnp.zeros_like(loss_ref)
            dw_ref[...] = jnp.zeros_like(dw_ref)
        part = jnp.sum(jnp.sum(err * err, axis=1, keepdims=True), axis=0, keepdims=True) * (0.5 / d)
        loss_ref[...] += jnp.broadcast_to(part, loss_ref.shape)
        dw_ref[...] += dw

    row = pl.BlockSpec((tm, d), lambda i: (i, 0))
    vec = pl.BlockSpec((1, d), lambda i: (0, 0))
    loss, dy, df, dw = pl.pallas_call(
        body, out_shape=(jax.ShapeDtypeStruct((8, 128), F32), jax.ShapeDtypeStruct((lp, d), F32),
                         jax.ShapeDtypeStruct((lp, d), BF16), jax.ShapeDtypeStruct((1, d), F32)),
        grid=(lp // tm,), in_specs=[row, row, vec, row],
        out_specs=(pl.BlockSpec((8, 128), lambda i: (0, 0)), row, row, vec),
        compiler_params=_cp("arbitrary"), name=name)(h, f, w_post, target)
    return loss[0, 0], dy, df, dw


_GELU_C = math.sqrt(2.0 / math.pi)


def _gelu_and_grad(a):
    a2 = a * a
    t = jnp.tanh(a * (_GELU_C + (_GELU_C * 0.044715) * a2))
    ha = 0.5 * a
    h1 = 0.5 + 0.5 * t
    return a * h1, h1 + ha * (1.0 - t * t) * (_GELU_C + (3.0 * _GELU_C * 0.044715) * a2)


def _gelu(a):
    t = jnp.tanh(a * (_GELU_C + (_GELU_C * 0.044715) * (a * a)))
    return a * (0.5 + 0.5 * t)


def _conv3(parts, n, w, b):
    xx = jnp.concatenate(parts, axis=0)
    return b + xx[8:8 + n] * w[2:3] + pltpu.roll(xx, 1, 0)[8:8 + n] * w[1:2] + pltpu.roll(xx, 2, 0)[8:8 + n] * w[0:1]


def _conv_act(ua, ug, wa, wg, ba, bg, *, name):
    lp, n = ua.shape
    tm = _row_tile(lp)
    tc = _tile(n, (256, 128))
    nb8 = tm // 8

    def body(ua_ref, uap_ref, ug_ref, ugp_ref, wa_ref, wg_ref, ba_ref, bg_ref, o_ref):
        i = pl.program_id(0)
        ca = _conv3([uap_ref[...], ua_ref[...]], tm, wa_ref[...], ba_ref[...])
        cg = _conv3([ugp_ref[...], ug_ref[...]], tm, wg_ref[...], bg_ref[...])
        rows = i * tm + lax.broadcasted_iota(jnp.int32, (tm, 1), 0)
        ok = (rows >= PADF) & (rows < lp - BACK)
        o_ref[...] = jnp.where(ok, _gelu(ca) * cg, 0.0).astype(BF16)

    cur = pl.BlockSpec((tm, tc), lambda i, j: (i, j))
    prev = pl.BlockSpec((8, tc), lambda i, j: (jnp.maximum(i * nb8 - 1, 0), j))
    w3 = pl.BlockSpec((3, tc), lambda i, j: (0, j))
    b1 = pl.BlockSpec((1, tc), lambda i, j: (0, j))
    return pl.pallas_call(
        body, out_shape=jax.ShapeDtypeStruct((lp, n), BF16), grid=(lp // tm, n // tc),
        in_specs=[cur, prev, cur, prev, w3, w3, b1, b1], out_specs=cur,
        compiler_params=_cp("parallel", "parallel"), name=name)(ua, ua, ug, ug, wa, wg, ba, bg)


def _conv_act_down(ua, ug, wa, wg, ba, bg, down, *, name):
    lp, n = ua.shape
    d = down.shape[1]
    tm = _tile(lp, (320, 256, 128, 64))
    tc = _tile(n, (256, 128))
    nb8 = tm // 8

    def body(ua_ref, uap_ref, ug_ref, ugp_ref, wa_ref, wg_ref, ba_ref, bg_ref, dn_ref, act_ref, f_ref):
        i = pl.program_id(0)
        rows = i * tm + lax.broadcasted_iota(jnp.int32, (tm, 1), 0)
        ok = (rows >= PADF) & (rows < lp - BACK)
        acc = None
        for j in range(n // tc):
            cs = slice(tc * j, tc * j + tc)
            ca = _conv3([uap_ref[:, cs], ua_ref[:, cs]], tm, wa_ref[:, cs], ba_ref[:, cs])
            cg = _conv3([ugp_ref[:, cs], ug_ref[:, cs]], tm, wg_ref[:, cs], bg_ref[:, cs])
            act = jnp.where(ok, _gelu(ca) * cg, 0.0).astype(BF16)
            act_ref[:, cs] = act
            part = _dot(act, dn_ref[cs, :])
            acc = part if acc is None else acc + part
        f_ref[...] = acc

    cur = pl.BlockSpec((tm, n), lambda i: (i, 0))
    prev = pl.BlockSpec((8, n), lambda i: (jnp.maximum(i * nb8 - 1, 0), 0))
    w3 = pl.BlockSpec((3, n), lambda i: (0, 0))
    b1 = pl.BlockSpec((1, n), lambda i: (0, 0))
    return pl.pallas_call(
        body, out_shape=(jax.ShapeDtypeStruct((lp, n), BF16), jax.ShapeDtypeStruct((lp, d), F32)),
        grid=(lp // tm,),
        in_specs=[cur, prev, cur, prev, w3, w3, b1, b1, pl.BlockSpec(down.shape, lambda i: (0, 0))],
        out_specs=(cur, pl.BlockSpec((tm, d), lambda i: (i, 0))),
        compiler_params=_cp("parallel"), name=name)(ua, ua, ug, ug, wa, wg, ba, bg, down)


def _conv_act_bwd(ua, ug, df, down, wa, wg, ba, bg, up_a, up_g, *, name):
    lp, n = ua.shape
    d = up_a.shape[0]
    tm = _tile(lp, (320, 256, 128, 64))
    tc = _tile(n, (256, 128))
    nb8 = tm // 8
    last8 = lp // 8 - 1
    last16 = lp // 16 - 1
    ext = tm + 8

    def body(ua_ref, uap_ref, uan_ref, ug_ref, ugp_ref, ugn_ref, df_ref, dfn_ref, dn_ref, wa_ref, wg_ref, ba_ref,
             bg_ref, upa_ref, upg_ref, dua_ref, dug_ref, dwa_ref, dwg_ref, dba_ref, dbg_ref, dhn_ref):
        i = pl.program_id(0)
        df_ext = jnp.concatenate([df_ref[...], dfn_ref[...]], axis=0)

        @pl.when(i == 0)
        def _():
            dwa_ref[...] = jnp.zeros_like(dwa_ref)
            dwg_ref[...] = jnp.zeros_like(dwg_ref)
            dba_ref[...] = jnp.zeros_like(dba_ref)
            dbg_ref[...] = jnp.zeros_like(dbg_ref)
        rows = i * tm + lax.broadcasted_iota(jnp.int32, (ext, 1), 0)
        ok = (rows >= PADF) & (rows < lp - BACK)

        def conv(parts, w, b):
            xx = jnp.concatenate(parts, axis=0)
            x, x1, x2 = xx[8:8 + ext], pltpu.roll(xx, 1, 0)[8:8 + ext], pltpu.roll(xx, 2, 0)[8:8 + ext]
            return b + x * w[2:3] + x1 * w[1:2] + x2 * w[0:1], x, x1, x2

        def back(dc, w):
            return (dc[:tm] * w[2:3] + pltpu.roll(dc, ext - 1, 0)[:tm] * w[1:2]
                    + pltpu.roll(dc, ext - 2, 0)[:tm] * w[0:1])

        def wsum(dw_ref, db_ref, cs, dc, x, x1, x2):
            dd = dc[:tm]
            s = lambda v: jnp.sum(v, axis=0, keepdims=True)
            dw_ref[0:1, cs] += s(dd * x2[:tm])
            dw_ref[1:2, cs] += s(dd * x1[:tm])
            dw_ref[2:3, cs] += s(dd * x[:tm])
            db_ref[:, cs] += s(dd)

        acc = None
        nchunks = n // tc
        dact_of = lambda j: _dot_nt(df_ext, dn_ref[tc * j:tc * j + tc, :])[:ext]
        dact_next = dact_of(0)
        for j in range(nchunks):
            cs = slice(tc * j, tc * j + tc)
            dact_cur = dact_next
            if j + 1 < nchunks:
                dact_next = dact_of(j + 1)
            wa, wg = wa_ref[:, cs], wg_ref[:, cs]
            ca, xa, xa1, xa2 = conv([uap_ref[:, cs], ua_ref[:, cs], uan_ref[:, cs]], wa, ba_ref[:, cs])
            cg, xg, xg1, xg2 = conv([ugp_ref[:, cs], ug_ref[:, cs], ugn_ref[:, cs]], wg, bg_ref[:, cs])
            dact_e = jnp.where(ok, dact_cur, 0.0)
            gel, gel_d = _gelu_and_grad(ca)
            dca = dact_e * cg * gel_d
            dcg = dact_e * gel
            du_a, du_g = back(dca, wa).astype(BF16), back(dcg, wg).astype(BF16)
            dua_ref[:, cs] = du_a
            dug_ref[:, cs] = du_g
            wsum(dwa_ref, dba_ref, cs, dca, xa, xa1, xa2)
            wsum(dwg_ref, dbg_ref, cs, dcg, xg, xg1, xg2)
            part = _dot_nt(du_a, upa_ref[:, cs]) + _dot_nt(du_g, upg_ref[:, cs])
            acc = part if acc is None else acc + part
        dhn_ref[...] = acc

    cur = pl.BlockSpec((tm, n), lambda i: (i, 0))
    prev = pl.BlockSpec((8, n), lambda i: (jnp.maximum(i * nb8 - 1, 0), 0))
    nxt = pl.BlockSpec((8, n), lambda i: (jnp.minimum((i + 1) * nb8, last8), 0))
    w3 = pl.BlockSpec((3, n), lambda i: (0, 0))
    b1 = pl.BlockSpec((1, n), lambda i: (0, 0))
    whole = pl.BlockSpec(memory_space=pltpu.VMEM)
    return pl.pallas_call(
        body,
        out_shape=(jax.ShapeDtypeStruct((lp, n), BF16), jax.ShapeDtypeStruct((lp, n), BF16),
                   jax.ShapeDtypeStruct((3, n), F32), jax.ShapeDtypeStruct((3, n), F32),
                   jax.ShapeDtypeStruct((1, n), F32), jax.ShapeDtypeStruct((1, n), F32),
                   jax.ShapeDtypeStruct((lp, d), F32)),
        grid=(lp // tm,),
        in_specs=[cur, prev, nxt, cur, prev, nxt, pl.BlockSpec((tm, d), lambda i: (i, 0)),
                  pl.BlockSpec((16, d), lambda i: (jnp.minimum((i + 1) * (tm // 16), last16), 0)), whole,
                  w3, w3, b1, b1, whole, whole],
        out_specs=(cur, cur, w3, w3, b1, b1, pl.BlockSpec((tm, d), lambda i: (i, 0))),
        compiler_params=_cp("arbitrary"), name=name)(ua, ua, ua, ug, ug, ug, df, df, down, wa, wg, ba, bg, up_a, up_g)


def _sigmoid(x):
    return 1.0 / (1.0 + jnp.exp(-x))


def _merge(o_ret, o_gla, proj, w_ret, w_gla, *, name):
    lp = o_ret.shape[0]
    tm = _row_tile(lp)

    def body(or_ref, og_ref, rg_ref, gr_ref, wr_ref, wg_ref, m_ref):
        oret, ogla = or_ref[...], og_ref[...]
        yr, yg = [], []
        for h in range(4):
            hs = slice(128 * h, 128 * h + 128)
            o = oret[:, hs]
            xc = o - jnp.mean(o, axis=-1, keepdims=True)
            yr.append(xc * lax.rsqrt(jnp.mean(xc * xc, axis=-1, keepdims=True) + EPS))
            o = ogla[:, hs]
            yg.append(o * lax.rsqrt(jnp.mean(o * o, axis=-1, keepdims=True) + EPS))
        rg, gr = rg_ref[...], gr_ref[...]
        m_ref[:, 0:512] = (jnp.concatenate(yr, axis=1) * wr_ref[...] * (rg * _sigmoid(rg))).astype(BF16)
        m_ref[:, 512:1024] = (jnp.concatenate(yg, axis=1) * wg_ref[...] * (gr * _sigmoid(gr))).astype(BF16)

    row = pl.BlockSpec((tm, 512), lambda i: (i, 0))
    vec = pl.BlockSpec((1, 512), lambda i: (0, 0))
    return pl.pallas_call(
        body, out_shape=jax.ShapeDtypeStruct((lp, 1024), BF16), grid=(lp // tm,),
        in_specs=[row, row, pl.BlockSpec((tm, 512), lambda i: (i, C_RG // 512)),
                  pl.BlockSpec((tm, 512), lambda i: (i, C_GR // 512)), vec, vec],
        out_specs=pl.BlockSpec((tm, 1024), lambda i: (i, 0)),
        compiler_params=_cp("parallel"), name=name)(o_ret, o_gla, proj, proj, w_ret, w_gla)


def _merge_bwd(dm, o_ret, o_gla, proj, w_ret, w_gla, *, name):
    lp = o_ret.shape[0]
    tm = _row_tile(lp)

    def body(dm_ref, or_ref, og_ref, rg_ref, gr_ref, wr_ref, wg_ref, dor_ref, dog_ref, dgate_ref, dwr_ref, dwg_ref):
        @pl.when(pl.program_id(0) == 0)
        def _():
            dwr_ref[...] = jnp.zeros_like(dwr_ref)
            dwg_ref[...] = jnp.zeros_like(dwg_ref)

        def group(d, o_all, gate, w, center):
            sg = _sigmoid(gate)
            s = gate * sg
            ds = sg * (1.0 + gate * (1.0 - sg))
            xh, rr = [], []
            for h in range(4):
                o = o_all[:, 128 * h:128 * h + 128]
                if center:
                    o = o - jnp.mean(o, axis=-1, keepdims=True)
                r = lax.rsqrt(jnp.mean(o * o, axis=-1, keepdims=True) + EPS)
                xh.append(o * r)
                rr.append(r)
            xh_all = jnp.concatenate(xh, axis=1)
            dgate = d * xh_all * w * ds
            dw = jnp.sum(d * xh_all * s, axis=0, keepdims=True)
            dxh_all = d * w * s
            do = []
            for h in range(4):
                dxh = dxh_all[:, 128 * h:128 * h + 128]
                t = dxh - xh[h] * jnp.mean(dxh * xh[h], axis=-1, keepdims=True)
                if center:
                    t = t - jnp.mean(dxh, axis=-1, keepdims=True)
                do.append(rr[h] * t)
            return jnp.concatenate(do, axis=1), dgate, dw

        dmv = dm_ref[...]
        do, dg, dw = group(dmv[:, 0:512], or_ref[...], rg_ref[...], wr_ref[...], True)
        dor_ref[...] = do
        dgate_ref[:, 0:512] = dg.astype(BF16)
        dwr_ref[...] += dw
        do, dg, dw = group(dmv[:, 512:1024], og_ref[...], gr_ref[...], wg_ref[...], False)
        dog_ref[...] = do
        dgate_ref[:, 512:1024] = dg.astype(BF16)
        dwg_ref[...] += dw

    row = pl.BlockSpec((tm, 512), lambda i: (i, 0))
    vec = pl.BlockSpec((1, 512), lambda i: (0, 0))
    return pl.pallas_call(
        body,
        out_shape=(jax.ShapeDtypeStruct((lp, 512), F32), jax.ShapeDtypeStruct((lp, 512), F32),
                   jax.ShapeDtypeStruct((lp, P_GATE), BF16),
                   jax.ShapeDtypeStruct((1, 512), F32), jax.ShapeDtypeStruct((1, 512), F32)),
        grid=(lp // tm,),
        in_specs=[pl.BlockSpec((tm, 1024), lambda i: (i, 0)), row, row,
                  pl.BlockSpec((tm, 512), lambda i: (i, C_RG // 512)),
                  pl.BlockSpec((tm, 512), lambda i: (i, C_GR // 512)), vec, vec],
        out_specs=(row, row, pl.BlockSpec((tm, P_GATE), lambda i: (i, 0)), vec, vec),
        compiler_params=_cp("arbitrary"), name=name)(dm, o_ret, o_gla, proj, proj, w_ret, w_gla)


def _dot(a, b):
    return lax.dot_general(a, b, (((1,), (0,)), ((), ())), preferred_element_type=F32)


def _dot_nt(a, b):
    return lax.dot_general(a, b, (((1,), (1,)), ((), ())), preferred_element_type=F32)


def _dot_tn(a, b):
    return lax.dot_general(a, b, (((0,), (0,)), ((), ())), preferred_element_type=F32)


def _ret_tables(lp):
    cr = RET_CHUNK
    pos = np.arange(lp, dtype=np.float32) - np.float32(PADF)
    half = RET_DK // 2
    inv = (np.float32(ROPE_BASE) ** (-np.arange(half, dtype=np.float32) / np.float32(half))).astype(np.float32)
    ang = (pos[:, None] * inv[None, :]).astype(np.float32)
    c, s = np.cos(ang).astype(np.float32), np.sin(ang).astype(np.float32)
    rope_c = jnp.asarray(np.concatenate([c, c], axis=1))
    rope_s = jnp.asarray(np.concatenate([-s, s], axis=1))
    log_g = np.log(1.0 - 2.0 ** (-5.0 - np.arange(RET_HEADS, dtype=np.float64)))
    idx = np.arange(cr, dtype=np.float64)
    diff = idx[:, None] - idx[None, :]
    dmat = np.where(diff >= 0, np.exp(log_g[:, None, None] * np.maximum(diff, 0.0)), 0.0)
    zeta = np.exp(log_g[:, None] * (cr - 1.0 - idx)[None, :])
    xi = np.exp(log_g[:, None] * (idx + 1.0)[None, :])
    gc = np.exp(log_g * cr)
    f = lambda a: jnp.asarray(a.astype(np.float32))
    return (rope_c, rope_s, f(dmat), f(np.broadcast_to(zeta[:, :, None], (RET_HEADS, cr, 128))),
            f(np.broadcast_to(xi[:, :, None], (RET_HEADS, cr, 128))),
            f(np.broadcast_to(gc[:, None, None], (RET_HEADS, 8, 128))))


def _rope(t, c, s):
    return t * c + pltpu.roll(t, 64, 1) * s


def _rope_t(d, c, s):
    return d * c + pltpu.roll(d * s, 64, 1)


def _ret_specs(nblk, rev):
    ix = (lambda i: nblk - 1 - i) if rev else (lambda i: i)
    cr = RET_CHUNK
    col = lambda base: pl.BlockSpec((BLK, 512), lambda i: (ix(i), base // 512))
    tab = pl.BlockSpec((BLK, 128), lambda i: (ix(i), 0))
    sq = pl.BlockSpec((RET_HEADS, cr, cr), lambda i: (0, 0, 0))
    hv = pl.BlockSpec((RET_HEADS, cr, 128), lambda i: (0, 0, 0))
    g8 = pl.BlockSpec((RET_HEADS, 8, 128), lambda i: (0, 0, 0))
    st = pl.BlockSpec((RET_HEADS, BLK // cr, 128, 128), lambda i: (0, ix(i), 0, 0))
    out = pl.BlockSpec((BLK, 512), lambda i: (ix(i), 0))
    return col, tab, sq, hv, g8, st, out


def _retention(proj, tables, *, name):
    lp = proj.shape[0]
    nblk, cr = lp // BLK, RET_CHUNK
    scale = RET_DK ** -0.5

    def body(q_ref, k_ref, v_ref, c_ref, s_ref, d_ref, z_ref, x_ref, g_ref, o_ref, st_ref, state):
        @pl.when(pl.program_id(0) == 0)
        def _():
            state[...] = jnp.zeros_like(state)

        def chunk(ci, carry):
            sl = pl.ds(pl.multiple_of(ci * cr, cr), cr)
            c, s = c_ref[sl, :], s_ref[sl, :]
            for h in range(RET_HEADS):
                hs = slice(128 * h, 128 * h + 128)
                q = _rope(q_ref[sl, hs], c, s)
                k = _rope(k_ref[sl, hs], c, s) * scale
                qb, kb, vb = q.astype(BF16), k.astype(BF16), v_ref[sl, hs].astype(BF16)
                st = state[h]
                st_ref[h, ci] = st
                sc = _dot_nt(qb, kb) * d_ref[h]
                o_ref[sl, hs] = _dot(sc.astype(BF16), vb) + _dot(qb, st.astype(BF16)) * x_ref[h]
                state[h] = st * g_ref[h][0:1, :] + _dot_tn((k * z_ref[h]).astype(BF16), vb)
            return carry

        lax.fori_loop(0, BLK // cr, chunk, 0)

    col, tab, sq, hv, g8, st, out = _ret_specs(nblk, False)
    return pl.pallas_call(
        body,
        out_shape=(jax.ShapeDtypeStruct((lp, 512), F32), jax.ShapeDtypeStruct((4, lp // cr, 128, 128), F32)),
        grid=(nblk,), in_specs=[col(C_RQ), col(C_RK), col(C_RV), tab, tab, sq, hv, hv, g8],
        out_specs=(out, st), scratch_shapes=[pltpu.VMEM((RET_HEADS, 128, 128), F32)],
        compiler_params=_cp("arbitrary"), name=name)(proj, proj, proj, *tables)


def _retention_bwd(proj, do, states, tables, *, name):
    lp = proj.shape[0]
    nblk, cr = lp // BLK, RET_CHUNK
    nch = BLK // cr
    scale = RET_DK ** -0.5

    def body(q_ref, k_ref, v_ref, do_ref, st_ref, c_ref, s_ref, d_ref, z_ref, x_ref, g_ref, dqkv_ref, dstate):
        @pl.when(pl.program_id(0) == 0)
        def _():
            dstate[...] = jnp.zeros_like(dstate)

        def chunk(cc, carry):
            ci = nch - 1 - cc
            sl = pl.ds(pl.multiple_of(ci * cr, cr), cr)
            c, s = c_ref[sl, :], s_ref[sl, :]
            for h in range(RET_HEADS):
                hs = slice(128 * h, 128 * h + 128)
                dmat, zeta, xi = d_ref[h], z_ref[h], x_ref[h]
                q = _rope(q_ref[sl, hs], c, s)
                k = _rope(k_ref[sl, hs], c, s) * scale
                qb, kb, vb = q.astype(BF16), k.astype(BF16), v_ref[sl, hs].astype(BF16)
                kzb = (k * zeta).astype(BF16)
                dov = do_ref[sl, hs]
                dob, doxb = dov.astype(BF16), (dov * xi).astype(BF16)
                stb = st_ref[h, ci].astype(BF16)
                dsn = dstate[h]
                dsnb = dsn.astype(BF16)
                scb = (_dot_nt(qb, kb) * dmat).astype(BF16)
                dscb = (_dot_nt(dob, vb) * dmat).astype(BF16)
                dq = _dot(dscb, kb) + _dot_nt(doxb, stb)
                dk = _dot_tn(dscb, qb) + _dot_nt(vb, dsnb) * zeta
                dv = _dot_tn(scb, dob) + _dot(kzb, dsnb)
                dstate[h] = dsn * g_ref[h][0:1, :] + _dot_tn(qb, doxb)
                dqkv_ref[sl, 128 * h:128 * h + 128] = _rope_t(dq, c, s).astype(BF16)
                dqkv_ref[sl, 512 + 128 * h:640 + 128 * h] = _rope_t(dk * scale, c, s).astype(BF16)
                dqkv_ref[sl, 1024 + 128 * h:1152 + 128 * h] = dv.astype(BF16)
            return carry

        lax.fori_loop(0, nch, chunk, 0)

    col, tab, sq, hv, g8, st, out = _ret_specs(nblk, True)
    return pl.pallas_call(
        body, out_shape=jax.ShapeDtypeStruct((lp, P_RET), BF16), grid=(nblk,),
        in_specs=[col(C_RQ), col(C_RK), col(C_RV), out, st, tab, tab, sq, hv, hv, g8],
        out_specs=pl.BlockSpec((BLK, P_RET), lambda i: (nblk - 1 - i, 0)),
        scratch_shapes=[pltpu.VMEM((RET_HEADS, 128, 128), F32)],
        compiler_params=_cp("arbitrary"), name=name)(proj, proj, proj, do, states, *tables)


def _gla_tables():
    c = GLA_CHUNK
    tri = np.tril(np.ones((c, c), np.float32))
    ones_qv = np.kron(np.eye(GLA_HEADS, dtype=np.float32), np.ones((GLA_DK, GLA_DV), np.float32))
    return (jnp.asarray(tri, BF16), jnp.asarray(tri.T.copy(), BF16), jnp.asarray(ones_qv, BF16),
            jnp.asarray(ones_qv.T.copy(), BF16))


def _split3(x):
    hi = x.astype(BF16)
    r1 = x - hi.astype(F32)
    mid = r1.astype(BF16)
    lo = (r1 - mid.astype(F32)).astype(BF16)
    return hi, mid, lo


def _tri_sum(tri, x):
    hi, mid, lo = _split3(x)
    return _dot(tri, hi) + _dot(tri, mid) + _dot(tri, lo)


def _head_masks(width, per):
    lane = lax.broadcasted_iota(jnp.int32, (1, width), 1)
    return [((lane >= per * h) & (lane < per * (h + 1))).astype(F32) for h in range(GLA_HEADS)]


def _stack_heads(x, masks):
    return jnp.concatenate([x * m for m in masks], axis=0)


def _gla_gate(ga, w2, b, ok, tri):
    z = _dot(ga.astype(BF16), w2) + b
    la = (jnp.minimum(z, 0.0) - jnp.log(1.0 + jnp.exp(-jnp.abs(z)))) * (1.0 / GLA_TAU)
    la = jnp.where(ok, la, 0.0)
    return z, _tri_sum(tri, la)


def _gla_rows(i_blk, ci, lp):
    c = GLA_CHUNK
    rows = i_blk * BLK + ci * c + lax.broadcasted_iota(jnp.int32, (c, 1), 0)
    return (rows >= PADF) & (rows < lp - BACK)


N_SUB = GLA_CHUNK // GLA_SUB - 1
N_SUB2 = GLA_SUB // GLA_SUB2 - 1


def _gla_masks():
    c, s1, s2 = GLA_CHUNK, GLA_SUB, GLA_SUB2
    sh1, sh2 = s1.bit_length() - 1, s2.bit_length() - 1
    r = lax.broadcasted_iota(jnp.int32, (c, GLA_QK), 0)
    blk, within = jnp.right_shift(r, sh1), jnp.bitwise_and(r, s1 - 1)
    grp = jnp.right_shift(within, sh2)
    rowm = [(blk == a).astype(F32) for a in range(1, N_SUB + 1)] + [(grp == b).astype(F32) for b in range(1, N_SUB2 + 1)]
    keym = ([(r < s1 * a).astype(F32) for a in range(1, N_SUB + 1)]
            + [(within < s2 * b).astype(F32) for b in range(1, N_SUB2 + 1)])
    rs = lax.broadcasted_iota(jnp.int32, (GLA_HEADS * c, c), 0)
    ts = lax.broadcasted_iota(jnp.int32, (GLA_HEADS * c, c), 1)
    same = (jnp.right_shift(jnp.bitwise_and(rs, c - 1), sh1) == jnp.right_shift(ts, sh1)).astype(F32)
    lag = [(jnp.bitwise_and(r, s2 - 1) >= j).astype(F32) for j in range(s2)]
    return rowm, keym, same, lag


def _gla_hats(qs, k, g, masks, hm_q):
    c, s1, s2 = GLA_CHUNK, GLA_SUB, GLA_SUB2
    rowm, keym, same, _ = masks
    refs = [g[s1 * a - 1:s1 * a, :] for a in range(1, N_SUB + 1)]
    for b in range(1, N_SUB2 + 1):
        refs.append(jnp.concatenate([jnp.broadcast_to(g[s1 * i + s2 * b - 1:s1 * i + s2 * b, :], (s1, GLA_QK))
                                     for i in range(c // s1)], axis=0))
    eqs = [jnp.exp(jnp.minimum(g - r, 0.0)) * m for r, m in zip(refs, rowm)]
    eks = [jnp.exp(jnp.minimum(r - g, 0.0)) * m for r, m in zip(refs, keym)]
    qhs, khs = [qs * e for e in eqs], [k * e for e in eks]
    qst = [_stack_heads(q, hm_q).astype(BF16) for q in qhs]
    khb = [x.astype(BF16) for x in khs]
    qa, qb = jnp.concatenate(qst[:N_SUB], axis=1), jnp.concatenate(qst[N_SUB:], axis=1)
    ka, kb = jnp.concatenate(khb[:N_SUB], axis=1), jnp.concatenate(khb[N_SUB:], axis=1)
    p = _dot_nt(qa, ka) + _dot_nt(qb, kb) * same
    return eqs, eks, qhs, khs, qa, qb, ka, kb, p


def _roll_rows(x, j):
    return x if j == 0 else pltpu.roll(x, j, 0)


def _gla(proj, w2p, b, tables, *, name):
    lp = proj.shape[0]
    nblk, c, s2 = lp // BLK, GLA_CHUNK, GLA_SUB2
    nch = BLK // c

    def body(q_ref, k_ref, v_ref, a_ref, w_ref, b_ref, tri_ref, ones_ref, o_ref, st_ref, state):
        i_blk = pl.program_id(0)

        @pl.when(i_blk == 0)
        def _():
            state[...] = jnp.zeros_like(state)
        hm_q = _head_masks(GLA_QK, GLA_DK)
        masks = _gla_masks()
        tri, ones_qv, w2, bias = tri_ref[...], ones_ref[...], w_ref[...], b_ref[...]

        def chunk(ci, carry):
            sl = pl.ds(pl.multiple_of(ci * c, c), c)
            ok = _gla_rows(i_blk, ci, lp)
            k, v = k_ref[sl, :], v_ref[sl, :]
            vb = v.astype(BF16)
            qs = q_ref[sl, :] * (GLA_DK ** -0.5)
            _, g = _gla_gate(a_ref[sl, :], w2, bias, ok, tri)
            last = g[c - 1:c, :]
            st = state[...]
            st_ref[ci] = st
            qst = _stack_heads(qs * jnp.exp(g), hm_q).astype(BF16)
            oi = _dot_nt(qst, st.astype(BF16))
            o = jnp.concatenate([oi[c * h:c * h + c, :] for h in range(GLA_HEADS)], axis=1)
            ke = k * jnp.exp(last - g)
            f = _dot_tn(vb, ke.astype(BF16))
            upd = f[0:GLA_DV, :] * hm_q[0]
            for h in range(1, GLA_HEADS):
                upd = upd + f[GLA_DV * h:GLA_DV * (h + 1), :] * hm_q[h]
            state[...] = st * jnp.exp(last) + upd
            p = _gla_hats(qs, k, g, masks, hm_q)[-1]
            ob = _dot(p.astype(BF16), vb)
            o = o + jnp.concatenate([ob[c * h:c * h + c, GLA_DV * h:GLA_DV * (h + 1)] for h in range(GLA_HEADS)],
                                    axis=1)
            ws = []
            for j in range(s2):
                ej = jnp.exp(jnp.minimum(g - _roll_rows(g, j), 0.0))
                ws.append((qs * _roll_rows(k, j) * ej * masks[3][j]).astype(BF16))
            ball = _dot(jnp.concatenate(ws, axis=0), ones_qv)
            for j in range(s2):
                o = o + ball[c * j:c * j + c, :] * _roll_rows(v, j)
            o_ref[sl, :] = o
            return carry

        lax.fori_loop(0, nch, chunk, 0)

    tri, _, ones_qv, _ = tables
    full = lambda arr: pl.BlockSpec(arr.shape, lambda i: (0,) * arr.ndim)
    return pl.pallas_call(
        body,
        out_shape=(jax.ShapeDtypeStruct((lp, GLA_V), F32), jax.ShapeDtypeStruct((lp // c, GLA_DV, GLA_QK), F32)),
        grid=(nblk,),
        in_specs=[pl.BlockSpec((BLK, GLA_QK), lambda i: (i, C_GQ // GLA_QK)),
                  pl.BlockSpec((BLK, GLA_QK), lambda i: (i, C_GK // GLA_QK)),
                  pl.BlockSpec((BLK, GLA_V), lambda i: (i, C_GV // GLA_V)),
                  pl.BlockSpec((BLK, 128), lambda i: (i, C_GA // 128)),
                  full(w2p), full(b), full(tri), full(ones_qv)],
        out_specs=(pl.BlockSpec((BLK, GLA_V), lambda i: (i, 0)),
                   pl.BlockSpec((nch, GLA_DV, GLA_QK), lambda i: (i, 0, 0))),
        scratch_shapes=[pltpu.VMEM((GLA_DV, GLA_QK), F32)],
        compiler_params=_cp("arbitrary"), name=name)(proj, proj, proj, proj, w2p, b, tri, ones_qv)


def _gla_bwd(proj, do, states, w2p, b, tables, *, name):
    lp = proj.shape[0]
    nblk, c, s1, s2 = lp // BLK, GLA_CHUNK, GLA_SUB, GLA_SUB2
    nch = BLK // c

    def body(q_ref, k_ref, v_ref, a_ref, do_ref, st_ref, w_ref, b_ref, tri_ref, trit_ref, ones_ref, onest_ref,
             dp_ref, dw_ref, db_ref, dstate, dqs_s, dk_s, dg_s, dv_s):
        i_blk = nblk - 1 - pl.program_id(0)

        @pl.when(pl.program_id(0) == 0)
        def _():
            dstate[...] = jnp.zeros_like(dstate)
            dw_ref[...] = jnp.zeros_like(dw_ref)
            db_ref[...] = jnp.zeros_like(db_ref)
        hm_q = _head_masks(GLA_QK, GLA_DK)
        hm_v = _head_masks(GLA_V, GLA_DV)
        masks = _gla_masks()
        tri, trit, ones_qv, ones_vq = tri_ref[...], trit_ref[...], ones_ref[...], onest_ref[...]
        w2, bias = w_ref[...], b_ref[...]
        rsum = lambda x: jnp.sum(x, axis=0, keepdims=True)

        def chunk(cc, carry):
            ci = nch - 1 - cc
            sl = pl.ds(pl.multiple_of(ci * c, c), c)
            ok = _gla_rows(i_blk, ci, lp)
            k, v, ga = k_ref[sl, :], v_ref[sl, :], a_ref[sl, :]
            vb = v.astype(BF16)
            qs = q_ref[sl, :] * (GLA_DK ** -0.5)
            z, g = _gla_gate(ga, w2, bias, ok, tri)
            last = g[c - 1:c, :]
            elast = jnp.exp(last)
            eg = jnp.exp(g)
            ekl = jnp.exp(last - g)
            qe, ke = qs * eg, k * ekl
            dov = do_ref[sl, :]
            st = st_ref[ci]
            dsn = dstate[...]
            qst = _stack_heads(qe, hm_q).astype(BF16)
            dost = jnp.concatenate([dov[:, GLA_DV * h:GLA_DV * (h + 1)] for h in range(GLA_HEADS)], axis=0).astype(BF16)
            dqe_st = _dot(dost, st.astype(BF16))
            dqe = dqe_st[0:c, :] * hm_q[0]
            for h in range(1, GLA_HEADS):
                dqe = dqe + dqe_st[c * h:c * h + c, :] * hm_q[h]
            dstate[...] = _dot_tn(dost, qst) + dsn * elast
            dlast = rsum(dsn * st) * elast
            df = _stack_heads(dsn, hm_q).astype(BF16)
            dv_s[...] = _dot_nt(ke.astype(BF16), df)
            dke = _dot(vb, df)
            xk = dke * ke
            dqs_s[...] = dqe * eg
            dk_s[...] = dke * ekl
            dg_s[...] = dqe * qe - xk
            dlast = dlast + rsum(xk)
            eqs, eks, qhs, khs, qa, qb, ka, kb, p = _gla_hats(qs, k, g, masks, hm_q)
            dost_v = _stack_heads(dov, hm_v).astype(BF16)
            dp = _dot_nt(dost_v, vb)
            dv_s[...] += _dot_tn(p.astype(BF16), dost_v)
            dpa, dpb = dp.astype(BF16), (dp * masks[2]).astype(BF16)
            dq_all = (_dot(dpa, ka), _dot(dpb, kb))
            dk_all = (_dot_tn(dpa, qa), _dot_tn(dpb, qb))
            for t in range(N_SUB + N_SUB2):
                lvl, i = (0, t) if t < N_SUB else (1, t - N_SUB)
                cols = slice(GLA_QK * i, GLA_QK * (i + 1))
                dq_st = dq_all[lvl][:, cols]
                dqh = dq_st[0:c, :] * hm_q[0]
                for h in range(1, GLA_HEADS):
                    dqh = dqh + dq_st[c * h:c * h + c, :] * hm_q[h]
                dkh = dk_all[lvl][:, cols]
                xq, xkh = dqh * qhs[t], dkh * khs[t]
                dqs_s[...] += dqh * eqs[t]
                dk_s[...] += dkh * eks[t]
                dg_s[...] += xq - xkh
                back_ref = xkh - xq
                if lvl == 0:
                    row = s1 * (i + 1) - 1
                    dg_s[row:row + 1, :] += rsum(back_ref)
                else:
                    for blk in range(c // s1):
                        row = s1 * blk + s2 * (i + 1) - 1
                        dg_s[row:row + 1, :] += rsum(back_ref[s1 * blk:s1 * blk + s1, :])
            kes, qes, ws, dbs = [], [], [], []
            for j in range(s2):
                em = jnp.exp(jnp.minimum(g - _roll_rows(g, j), 0.0)) * masks[3][j]
                kes.append(_roll_rows(k, j) * em)
                qes.append(qs * em)
                ws.append((qs * kes[j]).astype(BF16))
                dbs.append((dov * _roll_rows(v, j)).astype(BF16))
            ball = _dot(jnp.concatenate(ws, axis=0), ones_qv)
            dwall = _dot(jnp.concatenate(dbs, axis=0), ones_vq)
            for j in range(s2):
                back = (lambda x: x) if j == 0 else (lambda x, j=j: pltpu.roll(x, c - j, 0))
                dw = dwall[c * j:c * j + c, :]
                dv_s[...] += back(ball[c * j:c * j + c, :] * dov)
                dqs_s[...] += dw * kes[j]
                dk_s[...] += back(dw * qes[j])
                x = dw * qs * kes[j]
                dg_s[...] += x - back(x)
            dg_s[c - 1:c, :] += dlast
            dla = jnp.where(ok, _tri_sum(trit, dg_s[...]), 0.0)
            dz = dla * (1.0 / GLA_TAU) / (1.0 + jnp.exp(z))
            dzb = dz.astype(BF16)
            dp_ref[sl, 0:256] = (dqs_s[...] * (GLA_DK ** -0.5)).astype(BF16)
            dp_ref[sl, 256:512] = dk_s[...].astype(BF16)
            dp_ref[sl, 512:1024] = dv_s[...].astype(BF16)
            dp_ref[sl, 1024:1152] = _dot_nt(dzb, w2).astype(BF16)
            dp_ref[sl, 1152:1280] = jnp.zeros((c, 128), BF16)
            dw_ref[...] += _dot_tn(ga.astype(BF16), dzb)
            db_ref[...] += rsum(dz)
            return carry

        lax.fori_loop(0, nch, chunk, 0)

    tri, trit, ones_qv, ones_vq = tables
    full = lambda arr: pl.BlockSpec(arr.shape, lambda i: (0,) * arr.ndim)
    rev = lambda i: nblk - 1 - i
    return pl.pallas_call(
        body,
        out_shape=(jax.ShapeDtypeStruct((lp, P_GLA), BF16),
                   jax.ShapeDtypeStruct((128, GLA_QK), F32), jax.ShapeDtypeStruct((1, GLA_QK), F32)),
        grid=(nblk,),
        in_specs=[pl.BlockSpec((BLK, GLA_QK), lambda i: (rev(i), C_GQ // GLA_QK)),
                  pl.BlockSpec((BLK, GLA_QK), lambda i: (rev(i), C_GK // GLA_QK)),
                  pl.BlockSpec((BLK, GLA_V), lambda i: (rev(i), C_GV // GLA_V)),
                  pl.BlockSpec((BLK, 128), lambda i: (rev(i), C_GA // 128)),
                  pl.BlockSpec((BLK, GLA_V), lambda i: (rev(i), 0)),
                  pl.BlockSpec((nch, GLA_DV, GLA_QK), lambda i: (rev(i), 0, 0)),
                  full(w2p), full(b), full(tri), full(trit), full(ones_qv), full(ones_vq)],
        out_specs=(pl.BlockSpec((BLK, P_GLA), lambda i: (rev(i), 0)),
                   pl.BlockSpec((128, GLA_QK), lambda i: (0, 0)),
                   pl.BlockSpec((1, GLA_QK), lambda i: (0, 0))),
        scratch_shapes=[pltpu.VMEM((GLA_DV, GLA_QK), F32), pltpu.VMEM((c, GLA_QK), F32),
                        pltpu.VMEM((c, GLA_QK), F32), pltpu.VMEM((c, GLA_QK), F32), pltpu.VMEM((c, GLA_V), F32)],
        compiler_params=_cp("arbitrary"), name=name)(proj, proj, proj, proj, do, states, w2p, b, tri, trit, ones_qv, ones_vq)


def _as2d(a):
    return a.reshape(-1, a.shape[-1])


def _ew_tile(r):
    return _tile(r, (512, 256, 128, 64, 32, 16, 8))


def _add2(a, b, *, out_dtype, name):
    a2, b2 = _as2d(a), _as2d(b)
    r, n = a2.shape
    tm = _ew_tile(r)

    def body(a_ref, b_ref, o_ref):
        o_ref[...] = (a_ref[...] + b_ref[...]).astype(o_ref.dtype)

    blk = pl.BlockSpec((tm, n), lambda i: (i, 0))
    return pl.pallas_call(body, out_shape=jax.ShapeDtypeStruct((r, n), out_dtype), grid=(r // tm,), in_specs=[blk, blk],
                          out_specs=blk, compiler_params=_cp("parallel"), name=name)(a2, b2).reshape(a.shape)


def _sum_slots(own, q, *, name):
    shape = own.shape
    q3 = q.reshape(3, -1, shape[-1])
    own2 = _as2d(own)
    r, n = own2.shape
    tm = _ew_tile(r)

    def body(own_ref, q_ref, o_ref):
        f = lambda i: q_ref[i].astype(F32)
        o_ref[...] = ((own_ref[...].astype(F32) + f(0)) + f(1)) + f(2)

    blk = pl.BlockSpec((tm, n), lambda i: (i, 0))
    return pl.pallas_call(
        body, out_shape=jax.ShapeDtypeStruct((r, n), F32), grid=(r // tm,),
        in_specs=[blk, pl.BlockSpec((3, tm, n), lambda i: (0, i, 0))], out_specs=blk,
        compiler_params=_cp("parallel"), name=name)(own2, q3).reshape(shape)


def _adamw(w, g, m, v, *, name):
    shape = w.shape
    w2, g2, m2, v2 = _as2d(w), _as2d(g), _as2d(m), _as2d(v)
    r, n = w2.shape
    tm = _ew_tile(r)

    def body(w_ref, g_ref, m_ref, v_ref, d_ref, mo_ref, vo_ref):
        d_ref[...], mo_ref[...], vo_ref[...] = _adam_math(w_ref[...], g_ref[...], m_ref[...], v_ref[...])

    blk = pl.BlockSpec((tm, n), lambda i: (i, 0))
    o = jax.ShapeDtypeStruct((r, n), F32)
    d, mo, vo = pl.pallas_call(body, out_shape=(o, o, o), grid=(r // tm,), in_specs=[blk] * 4, out_specs=(blk,) * 3,
                               compiler_params=_cp("parallel"), name=name)(w2, g2, m2, v2)
    return d.reshape(shape), mo.reshape(shape), vo.reshape(shape)


def _adam_math(w, gv, m, v):
    c1 = 1.0 - ADAM_B1 ** ADAM_STEP
    c2 = 1.0 - ADAM_B2 ** ADAM_STEP
    mn = ADAM_B1 * m + (1.0 - ADAM_B1) * gv
    vn = ADAM_B2 * v + (1.0 - ADAM_B2) * (gv * gv)
    return -ADAM_LR * ((mn / c1) / (jnp.sqrt(vn / c2) + ADAM_EPS) + ADAM_WD * w), mn, vn


def _adamw_halves(w, m, v, mine, theirs, c, *, name):
    depth, rows, n = w.shape
    r2 = rows // 2
    tm = next(t for t in range(min(r2, 256), 0, -8) if r2 % t == 0)
    steps = r2 // tm

    def body(c_ref, w_ref, m_ref, v_ref, *rest):
        halves, (g_ref, d_ref, mo_ref, vo_ref) = rest[:2 * depth], rest[2 * depth:]
        l, h = pl.program_id(0), pl.program_id(1)
        gv = None
        for k in range(depth):
            gk = jnp.where(h == c_ref[0], halves[2 * k][...], halves[2 * k + 1][...])
            gv = gk if gv is None else jnp.where(l == k, gk, gv)
        g_ref[...] = gv
        d_ref[...], mo_ref[...], vo_ref[...] = _adam_math(w_ref[...], gv, m_ref[...], v_ref[...])

    big = pl.BlockSpec((tm, n), lambda l, h, i, c_ref: ((2 * l + h) * steps + i, 0))
    half = lambda k: pl.BlockSpec((tm, n), lambda l, h, i, c_ref: (jnp.where(l == k, i, 0), 0))
    o = jax.ShapeDtypeStruct((depth * rows, n), F32)
    args = [a for k in range(depth) for a in (mine[k], theirs[k])]
    outs = pl.pallas_call(
        body, out_shape=(o, o, o, o),
        grid_spec=pltpu.PrefetchScalarGridSpec(
            num_scalar_prefetch=1, grid=(depth, 2, steps),
            in_specs=[big, big, big] + [half(k) for k in range(depth) for _ in range(2)], out_specs=(big,) * 4),
        compiler_params=_cp("arbitrary", "arbitrary", "arbitrary"), name=name)(
            jnp.reshape(c, (1,)).astype(jnp.int32), _as2d(w), _as2d(m), _as2d(v), *args)
    return [a.reshape(w.shape) for a in outs]


ANY = pl.BlockSpec(memory_space=pl.ANY)


def _place():
    return lax.axis_index("x"), lax.axis_index("y"), lax.axis_index("c")


def _other_chips(x, y):
    return [(1 - x, y), (x, 1 - y), (1 - x, 1 - y)]


def _remote(src, dst, ssem, rsem, dev):
    return pltpu.make_async_remote_copy(src_ref=src, dst_ref=dst, send_sem=ssem, recv_sem=rsem, device_id=dev,
                                        device_id_type=MESH)


def _allgather_chips(arrs, *, name):
    n = len(arrs)

    def body(*refs):
        ins, outs = refs[:n], refs[n:2 * n]
        s1, r1, s2, r2 = refs[2 * n:]
        x, y, c = _place()
        q = 2 * x + y
        chips = _other_chips(x, y)
        qs = [2 * cx + cy for cx, cy in chips]
        sib = (x, y, 1 - c)
        first, passed = [], []
        for k in range(n):
            for j, chip in enumerate(chips):
                first.append(_remote(ins[k].at[c], outs[k].at[c, q], s1.at[k, j], r1.at[k, j], (*chip, c)))
        for cp in first:
            cp.start()
        for k in range(n):
            for j, chip in enumerate(chips):
                land = outs[k].at[c, qs[j]]
                _remote(land, land, s1.at[k, j], r1.at[k, j], (*chip, c)).wait_recv()
                fw = _remote(land, land, s2.at[k, j], r2.at[k, j], sib)
                fw.start()
                passed.append(fw)
        for k in range(n):
            for j in range(3):
                land = outs[k].at[1 - c, qs[j]]
                _remote(land, land, s2.at[k, j], r2.at[k, j], sib).wait_recv()
        for cp in first + passed:
            cp.wait_send()

    sem = pltpu.SemaphoreType.DMA
    outs = pl.pallas_call(
        body, out_shape=tuple(jax.ShapeDtypeStruct((2, 4) + a.shape[1:], a.dtype) for a in arrs),
        in_specs=[ANY] * n, out_specs=(ANY,) * n,
        scratch_shapes=[sem((n, 3)), sem((n, 3)), sem((n, 3)), sem((n, 3))], name=name)(*arrs)
    chip = 2 * lax.axis_index("x") + lax.axis_index("y")
    return [lax.dynamic_update_slice_in_dim(o, a[:, None], chip, axis=1) for o, a in zip(outs, arrs)]


def _pair_exchange(arrs, *, name):
    n = len(arrs)

    def body(*refs):
        ins, outs = refs[:n], refs[n:2 * n]
        ssem, rsem = refs[2 * n:]
        x, y, c = _place()
        cps = [_remote(ins[k].at[:, 1 - c], outs[k], ssem.at[k], rsem.at[k], (x, y, 1 - c)) for k in range(n)]
        for cp in cps:
            cp.start()
        for cp in cps:
            cp.wait()

    sem = pltpu.SemaphoreType.DMA
    return pl.pallas_call(
        body, out_shape=tuple(jax.ShapeDtypeStruct((a.shape[0],) + a.shape[2:], a.dtype) for a in arrs),
        in_specs=[ANY] * n, out_specs=(ANY,) * n, scratch_shapes=[sem((n,)), sem((n,))], name=name)(*arrs)


def _pair_sum(mine, theirs, c, *, name):
    _, _, r, n = mine.shape
    tm = r if r <= 512 else _ew_tile(r)

    def body(c_ref, a_ref, b_ref, o_ref):
        o_ref[...] = (a_ref[...] + b_ref[...]).astype(BF16)

    blk = pl.BlockSpec((None, tm, n), lambda s, i, c_ref: (s, i, 0))
    return pl.pallas_call(
        body, out_shape=jax.ShapeDtypeStruct((4, r, n), BF16),
        grid_spec=pltpu.PrefetchScalarGridSpec(
            num_scalar_prefetch=1, grid=(4, r // tm),
            in_specs=[pl.BlockSpec((None, None, tm, n), lambda s, i, c_ref: (s, c_ref[0], i, 0)), blk], out_specs=blk),
        compiler_params=_cp("parallel", "parallel"), name=name)(jnp.reshape(c, (1,)).astype(jnp.int32), mine, theirs)


def _chip_copies(ins, outs, ssem, rsem, mode):
    x, y, c = _place()
    q = 2 * x + y
    sends, recvs = [], []
    for k in range(len(ins)):
        for j, (cx, cy) in enumerate(_other_chips(x, y)):
            sem = (ssem.at[k, j], rsem.at[k, j], (cx, cy, c))
            if mode == "scatter":
                sends.append(_remote(ins[k].at[2 * cx + cy], outs[k].at[j], *sem))
                recvs.append(sends[-1])
            else:
                sends.append(_remote(ins[k].at[c], outs[k].at[2 * q + c], *sem))
                recvs.append(_remote(ins[k].at[c], outs[k].at[2 * (2 * cx + cy) + c], *sem))
    return sends, recvs


def _chip_wait(sends, recvs):
    for cp in sends:
        cp.wait_send()
    for cp in recvs:
        cp.wait_recv()


def _landing_shape(a, mode):
    return jax.ShapeDtypeStruct(((3,) if mode == "scatter" else (8,)) + a.shape[1:], a.dtype)


def _chip_exchange(arrs, mode, *, name):
    n = len(arrs)

    def body(*refs):
        ins, outs = refs[:n], refs[n:2 * n]
        ssem, rsem = refs[2 * n:]
        sends, recvs = _chip_copies(ins, outs, ssem, rsem, mode)
        for cp in sends:
            cp.start()
        _chip_wait(sends, recvs)

    sem = pltpu.SemaphoreType.DMA
    return list(pl.pallas_call(
        body, out_shape=tuple(_landing_shape(a, mode) for a in arrs),
        in_specs=[ANY] * n, out_specs=(ANY,) * n, scratch_shapes=[sem((n, 3)), sem((n, 3))], name=name)(*arrs))


def _pair_fill(bufs, owns, *, name):
    n = len(bufs)

    def body(*refs):
        own, outs = refs[n:2 * n], refs[2 * n:3 * n]
        ssem, rsem = refs[3 * n:]
        x, y, c = _place()
        q = 2 * x + y
        sib = (x, y, 1 - c)
        sends, recvs = [], []
        for k in range(n):
            for j, (cx, cy) in enumerate(_other_chips(x, y)):
                mine, theirs = outs[k].at[2 * (2 * cx + cy) + c], outs[k].at[2 * (2 * cx + cy) + 1 - c]
                sends.append(_remote(mine, mine, ssem.at[k, j], rsem.at[k, j], sib))
                recvs.append(_remote(mine, theirs, ssem.at[k, j], rsem.at[k, j], sib))
            slots = outs[k].at[pl.ds(2 * q, 2)]
            sends.append(_remote(own[k], slots, ssem.at[k, 3], rsem.at[k, 3], sib))
            recvs.append(sends[-1])
        for cp in sends:
            cp.start()
        _chip_wait(sends, recvs)

    sem = pltpu.SemaphoreType.DMA
    return list(pl.pallas_call(
        body, out_shape=tuple(jax.ShapeDtypeStruct(b.shape, b.dtype) for b in bufs),
        in_specs=[ANY] * (2 * n), out_specs=(ANY,) * n, scratch_shapes=[sem((n, 4)), sem((n, 4))],
        input_output_aliases={k: k for k in range(n)}, name=name)(*bufs, *owns))


def _pair_swap(arrs, *, name):
    n = len(arrs)

    def body(*refs):
        ins, outs = refs[:n], refs[n:2 * n]
        ssem, rsem = refs[2 * n:]
        x, y, c = _place()
        cps = [_remote(ins[k], outs[k], ssem.at[k], rsem.at[k], (x, y, 1 - c)) for k in range(n)]
        for cp in cps:
            cp.start()
        for cp in cps:
            cp.wait()

    sem = pltpu.SemaphoreType.DMA
    return pl.pallas_call(
        body, out_shape=tuple(jax.ShapeDtypeStruct(a.shape, a.dtype) for a in arrs),
        in_specs=[ANY] * n, out_specs=(ANY,) * n, scratch_shapes=[sem((n,)), sem((n,))], name=name)(*arrs)


def _allreduce_small(slab, *, name):
    r, n = slab.shape

    def body(x_ref, o_ref, buf, ssem, rsem):
        x, y, c = _place()
        me = 4 * x + 2 * y + c
        buf[me] = x_ref[...]
        cps = []
        for rel in range(1, 8):
            bx, by, bc = (rel >> 2) & 1, (rel >> 1) & 1, rel & 1
            px, py, pc = (x + bx) % 2, (y + by) % 2, (c + bc) % 2
            cps.append((_remote(x_ref, buf.at[me], ssem.at[rel - 1], rsem.at[rel - 1], (px, py, pc)),
                        4 * px + 2 * py + pc, (px, py, pc)))
        for cp, _, _ in cps:
            cp.start()
        for rel, (cp, peer, dev) in enumerate(cps):
            cp.wait_send()
            _remote(x_ref, buf.at[peer], ssem.at[rel], rsem.at[rel], dev).wait_recv()
        acc = buf[0]
        for k in range(1, 8):
            acc = acc + buf[k]
        o_ref[...] = acc

    vm = pl.BlockSpec(memory_space=pltpu.VMEM)
    sem = pltpu.SemaphoreType.DMA
    return pl.pallas_call(
        body, out_shape=jax.ShapeDtypeStruct((r, n), F32), in_specs=[vm], out_specs=vm,
        scratch_shapes=[pltpu.VMEM((8, r, n), F32), sem((7,)), sem((7,))], name=name)(slab)


def _slab(arrs, row_mult):
    flat = jnp.concatenate([a.reshape(-1) for a in arrs])
    unit = 128 * row_mult
    total = -(-flat.size // unit) * unit
    return jnp.pad(flat, (0, total - flat.size)).reshape(-1, 128)


def _unslab(slab, shapes):
    flat = slab.reshape(-1)
    out, off = [], 0
    for s in shapes:
        size = int(np.prod(s))
        out.append(flat[off:off + size].reshape(s))
        off += size
    return out


def _cols_from_chips(a):
    return jnp.transpose(a, (1, 0, 2)).reshape(a.shape[1], -1)


def _cols_to_chips(a, parts):
    r = a.shape[0]
    return jnp.transpose(a.reshape(r, parts, -1), (1, 0, 2))


BIG = ("w_in", "w_out", "up", "down")
GATHER_RIDES = {("proj", 0): (("w_out", 0), ("up", 0)), ("mix_out", 0): (("down", 0),),
                ("ffn_up_a", 0): (("w_in", 1), ("w_out", 1)), ("ffn_up_g", 0): (("up", 1),),
                ("proj", 1): (("down", 1),)}
REDUCE_RIDES = {("ffn_up_a_dw", 0): (("up",), 1), ("ffn_down_dw", 0): (("w_in", "w_out"), 1),
                ("mix_out_dx", 0): (("down",), 1),
                ("proj_dx", 0): (("up",), 0), ("proj_dw_0", 0): (("down",), 0), ("proj_dw_1", 0): (("w_out",), 0)}


class _LocalWeights:
    def __init__(self, meta, win, wout, up_a, up_g, down, w2p, cw):
        self._meta, self._w = meta, {"win": win, "wout": wout, "up_a": up_a, "up_g": up_g, "down": down, "w2p": w2p,
                                     "cw": cw}

    def meta(self):
        return self._meta

    def get(self, kind, l):
        return self._w[kind][l]

    def mm(self, site, l, a, b, fn=None, **kw):
        return (fn or _mm)(a, b, name=site, **kw)

    def grads_done(self, l, g, kinds):
        pass


class _ChipWeights:
    def __init__(self, w_in, w_out, ffn_up, ffn_down, meta_tokens, gla_gate_w2, ffn_conv_w):
        self.x, self.y, self.c = _place()
        self.q = 2 * self.x + self.y
        halves = lambda a: a.astype(BF16).reshape(2, a.shape[0] // 2, a.shape[1])
        self.own = {(k, l): halves(a[l]) for k, a in zip(BIG, (w_in, w_out, ffn_up, ffn_down)) for l in range(DEPTH)}
        self.landed, self.swapped, self.full, self.n_swaps = {}, {}, {}, 0
        self.sh_shapes = [meta_tokens.shape, gla_gate_w2.shape, ffn_conv_w.shape]
        self.own["small", 0] = _slab([meta_tokens, gla_gate_w2, ffn_conv_w], 16).reshape(2, -1, 128)
        first = [("w_in", 0), ("small", 0)]
        for key, arr in zip(first, _chip_exchange([self.own[k] for k in first], "bcast", name="gather_first")):
            self.landed[key] = arr
        sh = self._whole("small", 0).reshape(4, -1, 128)
        parts = [_unslab(sh[k], self.sh_shapes) for k in range(4)]
        self._meta = jnp.concatenate([p[0] for p in parts], axis=-1)
        self.w2 = jnp.concatenate([p[1] for p in parts], axis=-1)
        self.cw = jnp.concatenate([p[2] for p in parts], axis=-1)
        self.partial, self.slots = {}, {}

    def _whole(self, kind, l):
        if (kind, l) not in self.full:
            keys = [k for k in self.landed if k not in self.full]
            got = _pair_fill([self.landed[k] for k in keys], [self.own[k] for k in keys],
                             name=f"gather_fill_{self.n_swaps}")
            self.n_swaps += 1
            for k, buf in zip(keys, got):
                self.full[k] = buf.reshape(4, 2 * buf.shape[1], buf.shape[2])
        return self.full[kind, l]

    def meta(self):
        return self._meta

    def get(self, kind, l):
        if kind == "win":
            return _to_kernel_cols(_cols_from_chips(self._whole("w_in", l)))
        if kind == "wout":
            return self._whole("w_out", l).reshape(D_MODEL, D_MODEL)
        if kind == "up_a":
            return _cols_from_chips(self._whole("up", l)[0:2])
        if kind == "up_g":
            return _cols_from_chips(self._whole("up", l)[2:4])
        if kind == "down":
            return self._whole("down", l).reshape(D_FF, D_MODEL)
        if kind == "w2p":
            return jnp.pad(self.w2[l], ((0, 128 - GLA_RANK), (0, 0))).astype(BF16)
        return self.cw[l]

    def mm(self, site, l, a, b, fn=None, **kw):
        fn = fn or _mm
        if (site, l) in GATHER_RIDES:
            keys = GATHER_RIDES[site, l]
            out, got = fn(a, b, name=site, carry=([self.own[k] for k in keys], "bcast"), **kw)
            self.landed.update(zip(keys, got))
            return out
        if (site, l) in REDUCE_RIDES:
            kinds, gl = REDUCE_RIDES[site, l]
            keys = [(k, gl) for k in kinds]
            if all(k in self.partial and k not in self.slots for k in keys):
                out, got = fn(a, b, name=site, carry=([self.partial[k] for k in keys], "scatter"), **kw)
                self.slots.update(zip(keys, got))
                return out
        return fn(a, b, name=site, **kw)

    def grads_done(self, l, g, kinds):
        split = lambda a: a.reshape(4, 2, a.shape[-2] // 2, a.shape[-1]) if a.ndim == 3 else \
            a.reshape(4, 2, a.shape[0] // 8, a.shape[1])
        src = {"w_in": lambda: g["w_in"][l], "w_out": lambda: g["w_out"][l],
               "up": lambda: g["up"][l], "down": lambda: g["down"][l]}
        big = {k: split(src[k]()) for k in kinds}
        from_sib = _pair_exchange([big[k] for k in kinds], name=f"grads_pair_exchange_{l}_{kinds[0]}")
        for k, theirs in zip(kinds, from_sib):
            self.partial[k, l] = _pair_sum(big[k], theirs, self.c, name=f"pair_sum_{k}_{l}")

    def reduce(self):
        keys = [(k, l) for l in range(DEPTH) for k in BIG]
        late = [k for k in keys if k not in self.slots]
        self.slots.update(zip(late, _chip_exchange([self.partial[k] for k in late], "scatter",
                                                   name="grads_chip_exchange")))
        half = {}
        for k in keys:
            own = lax.dynamic_index_in_dim(self.partial[k], self.q, 0, keepdims=False)
            half[k] = _sum_slots(own, self.slots[k], name=f"chip_sum_{k[0]}_{k[1]}")
        other = dict(zip(keys, _pair_swap([half[k] for k in keys], name="grads_pair_swap")))
        return [([half[k, l] for l in range(DEPTH)], [other[k, l] for l in range(DEPTH)]) for k in BIG]


def _local_step(x_rows, target_rows, wts, pre_mix_norm, gla_gate_b, ret_norm_w, gla_norm_w, post_mix_norm,
                pre_ffn_norm, ffn_conv_b, post_ffn_norm):
    d = D_MODEL
    lp = x_rows.shape[0] + FRONT + BACK
    row = lambda a, l: a[l][None, :]
    rtab = _ret_tables(lp)
    gtab = _gla_tables()
    h0 = jnp.concatenate([jnp.zeros((PADF, d), F32), wts.meta(), x_rows, jnp.zeros((BACK, d), F32)], axis=0)
    target = jnp.pad(target_rows, ((FRONT, BACK), (0, 0)))

    saved = []
    h = h0
    _, hn = _resid_norm(h0, None, None, row(pre_mix_norm, 0), name="norm_in")
    loss_local = dy = None
    for l in range(DEPTH):
        s = {"h_in": h, "hn": hn}
        s["proj"] = wts.mm("proj", l, hn, wts.get("win", l))
        s["o_ret"], s["st_ret"] = _retention(s["proj"], rtab, name="retention")
        s["o_gla"], s["st_gla"] = _gla(s["proj"], wts.get("w2p", l), row(gla_gate_b, l), gtab, name="gla")
        s["merged"] = _merge(s["o_ret"], s["o_gla"], s["proj"], row(ret_norm_w, l), row(gla_norm_w, l), name="merge")
        s["m"] = wts.mm("mix_out", l, s["merged"], wts.get("wout", l))
        s["h_mid"], s["hn2"] = _resid_norm(h, s["m"], row(post_mix_norm, l), row(pre_ffn_norm, l), name="resid_mix")
        s["ua"] = wts.mm("ffn_up_a", l, s["hn2"], wts.get("up_a", l))
        s["ug"] = wts.mm("ffn_up_g", l, s["hn2"], wts.get("up_g", l))
        cw_a, cw_g = wts.get("cw", l)[:, :D_FF], wts.get("cw", l)[:, D_FF:]
        cb_a, cb_g = ffn_conv_b[l][None, :D_FF], ffn_conv_b[l][None, D_FF:]
        s["conv"] = (cw_a, cw_g, cb_a, cb_g)
        s["act"], s["f"] = _conv_act_down(s["ua"], s["ug"], cw_a, cw_g, cb_a, cb_g, wts.get("down", l),
                                          name="conv_act_down")
        if l + 1 < DEPTH:
            h, hn = _resid_norm(s["h_mid"], s["f"], row(post_ffn_norm, l), row(pre_mix_norm, l + 1), name="resid_ffn")
        else:
            loss_local, dy, df_last, dw_last = _loss_head(s["h_mid"], s["f"], row(post_ffn_norm, l), target,
                                                          name="loss_head")
        saved.append(s)

    g = {k: [None] * DEPTH for k in ("pre_mix", "w_in", "w2", "gb", "ret_n", "gla_n", "w_out", "post_mix", "pre_ffn",
                                     "up", "cw", "cb", "down", "post_ffn")}
    dh_out, dhn_next = dy, None
    for l in reversed(range(DEPTH)):
        s = saved[l]
        cw_a, cw_g, cb_a, cb_g = s["conv"]
        if l + 1 < DEPTH:
            dh, df, g["pre_mix"][l + 1], g["post_ffn"][l] = _resid_norm_bwd(
                dh_out, dhn_next, saved[l + 1]["h_in"], s["f"], row(pre_mix_norm, l + 1), row(post_ffn_norm, l),
                name="resid_ffn_bwd")
        else:
            dh, df, g["post_ffn"][l] = dh_out, df_last, dw_last
        g["down"][l] = wts.mm("ffn_down_dw", l, s["act"], df, fn=_mm_tn, tn=512)
        du_a, du_g, dcw_a, dcw_g, dcb_a, dcb_g, dhn2 = _conv_act_bwd(
            s["ua"], s["ug"], df, wts.get("down", l), cw_a, cw_g, cb_a, cb_g, wts.get("up_a", l), wts.get("up_g", l),
            name="conv_act_bwd")
        g["cw"][l] = jnp.concatenate([dcw_a, dcw_g], axis=1)
        g["cb"][l] = jnp.concatenate([dcb_a, dcb_g], axis=1)[0]
        half_up = wts.mm("ffn_up_a_dw", l, s["hn2"], du_a, fn=_mm_tn, tn=D_FF // 2, blocks=(4, 0))
        g["up"][l] = _mm_tn(s["hn2"], du_g, tn=D_FF // 2, blocks=(4, 2), into=half_up, name="ffn_up_g_dw")
        dh, dm, g["pre_ffn"][l], g["post_mix"][l] = _resid_norm_bwd(
            dh, dhn2, s["h_mid"], s["m"], row(pre_ffn_norm, l), row(post_mix_norm, l), name="resid_mix_bwd")
        g["w_out"][l] = _mm_tn(s["merged"], dm, name="mix_out_dw")
        wts.grads_done(l, g, ("w_out", "up", "down"))
        dmerged = wts.mm("mix_out_dx", l, dm, wts.get("wout", l), nt=True)
        do_ret, do_gla, d_gate, g["ret_n"][l], g["gla_n"][l] = _merge_bwd(
            dmerged, s["o_ret"], s["o_gla"], s["proj"], row(ret_norm_w, l), row(gla_norm_w, l), name="merge_bwd")
        d_ret = _retention_bwd(s["proj"], do_ret, s["st_ret"], rtab, name="retention_bwd")
        d_gla, dw2, dgb = _gla_bwd(s["proj"], do_gla, s["st_gla"], wts.get("w2p", l), row(gla_gate_b, l), gtab,
                                   name="gla_bwd")
        g["w2"][l], g["gb"][l] = dw2[:GLA_RANK], dgb[0]
        pieces = (d_ret, d_gate, d_gla)
        g["w_in"][l] = _to_reference_chips(*[wts.mm(f"proj_dw_{i}", l, s["hn"], p, fn=_mm_tn)
                                             for i, p in enumerate(pieces)])
        win = wts.get("win", l)
        dhn_next = wts.mm("proj_dx", l, pieces, [win[:, 0:P_RET], win[:, P_RET:P_RET + P_GATE], win[:, P_RET + P_GATE:]],
                          fn=_mm_nt_sum)
        dh_out = dh
        wts.grads_done(l, g, ("w_in",))
    dh0, _, g["pre_mix"][0], _ = _resid_norm_bwd(dh_out, dhn_next, h0, None, row(pre_mix_norm, 0), None,
                                                 name="norm_in_bwd")
    return loss_local, dh0, g


def kernel(x, meta_tokens, pre_mix_norm, w_in, gla_gate_w2, gla_gate_b, ret_norm_w, gla_norm_w, w_out, post_mix_norm, pre_ffn_norm, ffn_up, ffn_conv_w, ffn_conv_b, ffn_down, post_ffn_norm, loss_target, m_meta_tokens, m_pre_mix_norm, m_w_in, m_gla_gate_w2, m_gla_gate_b, m_ret_norm_w, m_gla_norm_w, m_w_out, m_post_mix_norm, m_pre_ffn_norm, m_ffn_up, m_ffn_conv_w, m_ffn_conv_b, m_ffn_down, m_post_ffn_norm, v_meta_tokens, v_pre_mix_norm, v_w_in, v_gla_gate_w2, v_gla_gate_b, v_ret_norm_w, v_gla_norm_w, v_w_out, v_post_mix_norm, v_pre_ffn_norm, v_ffn_up, v_ffn_conv_w, v_ffn_conv_b, v_ffn_down, v_post_ffn_norm):
    xi, yi, ci = _place()
    chip = 2 * xi + yi
    seq = x.shape[1]
    d = D_MODEL
    wts = _ChipWeights(w_in, w_out, ffn_up, ffn_down, meta_tokens, gla_gate_w2, ffn_conv_w)
    loss_local, dh0, g = _local_step(x[0], loss_target[0], wts, pre_mix_norm, gla_gate_b, ret_norm_w, gla_norm_w,
                                     post_mix_norm, pre_ffn_norm, ffn_conv_b, post_ffn_norm)
    grad_x = dh0[FRONT:FRONT + seq][None]
    names = ("w_in", "w_out", "ffn_up", "ffn_down")
    big_halves = wts.reduce()

    small_full = [dh0[PADF:FRONT], jnp.stack(g["pre_mix"])[:, 0], jnp.stack(g["w2"]), jnp.stack(g["gb"]),
                  jnp.stack(g["ret_n"])[:, 0], jnp.stack(g["gla_n"])[:, 0], jnp.stack(g["post_mix"])[:, 0],
                  jnp.stack(g["pre_ffn"])[:, 0], jnp.stack(g["cw"]), jnp.stack(g["cb"]),
                  jnp.stack(g["post_ffn"])[:, 0]]
    small_sum = _unslab(_allreduce_small(_slab(small_full, 8), name="small_allreduce"), [a.shape for a in small_full])
    (g_meta, g_pre_mix, g_w2, g_gb, g_ret_n, g_gla_n, g_post_mix, g_pre_ffn, g_cw, g_cb, g_post_ffn) = small_sum
    g_meta = lax.dynamic_slice_in_dim(g_meta, chip * 256, 256, axis=1)
    g_w2 = lax.dynamic_slice_in_dim(g_w2, chip * 64, 64, axis=2)
    g_cw = lax.dynamic_slice_in_dim(g_cw, chip * 1408, 1408, axis=2)

    grads = [g_meta, g_pre_mix, None, g_w2, g_gb, g_ret_n, g_gla_n, None, g_post_mix, g_pre_ffn, None,
             g_cw, g_cb, None, g_post_ffn]
    ws = [meta_tokens, pre_mix_norm, w_in, gla_gate_w2, gla_gate_b, ret_norm_w, gla_norm_w, w_out, post_mix_norm,
          pre_ffn_norm, ffn_up, ffn_conv_w, ffn_conv_b, ffn_down, post_ffn_norm]
    ms = [m_meta_tokens, m_pre_mix_norm, m_w_in, m_gla_gate_w2, m_gla_gate_b, m_ret_norm_w, m_gla_norm_w, m_w_out,
          m_post_mix_norm, m_pre_ffn_norm, m_ffn_up, m_ffn_conv_w, m_ffn_conv_b, m_ffn_down, m_post_ffn_norm]
    vs = [v_meta_tokens, v_pre_mix_norm, v_w_in, v_gla_gate_w2, v_gla_gate_b, v_ret_norm_w, v_gla_norm_w, v_w_out,
          v_post_mix_norm, v_pre_ffn_norm, v_ffn_up, v_ffn_conv_w, v_ffn_conv_b, v_ffn_down, v_post_ffn_norm]
    big_idx = (2, 7, 10, 13)
    deltas, new_m, new_v = [None] * 15, [None] * 15, [None] * 15
    for i, nm, (mine, theirs) in zip(big_idx, names, big_halves):
        grads[i], deltas[i], new_m[i], new_v[i] = _adamw_halves(ws[i], ms[i], vs[i], mine, theirs, ci,
                                                                name=f"adamw_{nm}")
    small_idx = [i for i in range(15) if i not in big_idx]
    shapes = [ws[i].shape for i in small_idx]
    sd, sm, sv = _adamw(_slab([ws[i] for i in small_idx], 8), _slab([grads[i] for i in small_idx], 8),
                        _slab([ms[i] for i in small_idx], 8), _slab([vs[i] for i in small_idx], 8), name="adamw_small")
    for i, a, b, c_ in zip(small_idx, _unslab(sd, shapes), _unslab(sm, shapes), _unslab(sv, shapes)):
        deltas[i], new_m[i], new_v[i] = a, b, c_

    loss = lax.psum(loss_local, ("x", "y", "c"))
    return (loss, grad_x, *grads, *deltas, *new_m, *new_v)
```

```python
import functools
import math

import numpy as np
import jax
import jax.numpy as jnp
from jax import lax
from jax.experimental import pallas as pl
from jax.experimental.pallas import tpu as pltpu

F32 = jnp.float32
BF16 = jnp.bfloat16

D_MODEL = 1024
DEPTH = 2
N_META = 16
EPS = 1e-6
RET_HEADS = 4
RET_DK = 128
GLA_HEADS = 4
GLA_DK = 64
GLA_DV = 128
GLA_QK = GLA_HEADS * GLA_DK
GLA_V = GLA_HEADS * GLA_DV
GLA_RANK = 16
GLA_TAU = 16.0
D_FF = 2816
ROPE_BASE = 10000.0
IN_WIDTH = 3600
IN_PAD = 3840
C_RQ, C_RK, C_RV, C_RG, C_GR, C_GQ, C_GK, C_GV, C_GA = 0, 512, 1024, 1536, 2048, 2560, 2816, 3072, 3584
P_RET, P_GATE, P_GLA = 1536, 1024, 1280


def _to_kernel_cols(w):
    pad = jnp.zeros(w.shape[:-1] + (IN_PAD - IN_WIDTH,), w.dtype)
    return jnp.concatenate([w[..., 0:2048], w[..., 3072:3584], w[..., 2048:3072], w[..., 3584:3600], pad], axis=-1)


def _to_reference_chips(d_ret, d_gate, d_gla):
    segs = [(d_ret, 0, 0, 1536), (d_gate, 0, 1536, 512), (d_gla, 0, 2048, 1024), (d_gate, 512, 3072, 512),
            (d_gla, 1024, 3584, GLA_RANK)]
    per = IN_WIDTH // 4
    chips = []
    for j in range(4):
        lo, hi, parts = per * j, per * (j + 1), []
        for piece, p0, r0, width in segs:
            a, b = max(lo, r0), min(hi, r0 + width)
            if a < b:
                parts.append(piece[:, p0 + a - r0:p0 + b - r0])
        chips.append(jnp.concatenate(parts, axis=1))
    return jnp.stack(chips)

FRONT = 64
BACK = 64
PADF = FRONT - N_META
RET_CHUNK = 128
GLA_CHUNK = 64
GLA_SUB = 16
GLA_SUB2 = 4
BLK = 640

ADAM_LR, ADAM_B1, ADAM_B2, ADAM_EPS, ADAM_WD, ADAM_STEP = 0.001, 0.9, 0.999, 1e-08, 0.01, 10

VMEM_LIMIT = 56 * 2 ** 20
MM_VMEM_BUDGET = 40 * 2 ** 20
MESH = pl.DeviceIdType.MESH


def _cp(*sem):
    return pltpu.CompilerParams(dimension_semantics=sem, vmem_limit_bytes=VMEM_LIMIT)


def _tile(n, cands):
    for t in cands:
        if n % t == 0:
            return t
    raise ValueError(f"no tile for {n} in {cands}")


def _row_tile(n):
    return _tile(n, (640, 512, 320, 256, 128, 64))


def _mm(a, b, *, nt=False, add=None, out_dtype=F32, tn=None, name, carry=None):
    m, k = a.shape
    n = b.shape[0] if nt else b.shape[1]
    tm = _tile(m, (640, 320, 256, 128, 64))
    if tn is None:
        step_bytes = lambda t: 2 * (tm * k * a.dtype.itemsize + t * k * b.dtype.itemsize
                                    + tm * t * (jnp.dtype(out_dtype).itemsize + (4 if add is not None else 0)))
        tn = next(t for t in range(n, 0, -128) if n % t == 0 and (step_bytes(t) <= MM_VMEM_BUDGET or t == 128))
    dn = (((1,), (1,)), ((), ())) if nt else (((1,), (0,)), ((), ()))
    nj, ni = n // tn, m // tm
    n_in = 2 + (add is not None)
    c_arrs, c_mode = carry if carry is not None else ((), None)
    nc = len(c_arrs)

    def body(*refs):
        a_ref, b_ref = refs[:2]
        c_ref = refs[2] if add is not None else None
        o_ref = refs[n_in + nc]
        if nc:
            c_ins, c_outs = refs[n_in:n_in + nc], refs[n_in + nc + 1:n_in + 2 * nc + 1]
            ssem, rsem = refs[n_in + 2 * nc + 1:]
            j, i = pl.program_id(0), pl.program_id(1)

            @pl.when((j == 0) & (i == 0))
            def _():
                for cp in _chip_copies(c_ins, c_outs, ssem, rsem, c_mode)[0]:
                    cp.start()
        r = lax.dot_general(a_ref[...].astype(BF16), b_ref[...].astype(BF16), dn, preferred_element_type=F32)
        if add is not None:
            r = r + c_ref[...]
        o_ref[...] = r.astype(o_ref.dtype)
        if nc:
            @pl.when((j == nj - 1) & (i == ni - 1))
            def _():
                _chip_wait(*_chip_copies(c_ins, c_outs, ssem, rsem, c_mode))

    b_spec = pl.BlockSpec((tn, k), lambda j, i: (j, 0)) if nt else pl.BlockSpec((k, tn), lambda j, i: (0, j))
    in_specs = [pl.BlockSpec((tm, k), lambda j, i: (i, 0)), b_spec]
    args = [a, b]
    if add is not None:
        in_specs.append(pl.BlockSpec((tm, tn), lambda j, i: (i, j)))
        args.append(add)
    out_shape = jax.ShapeDtypeStruct((m, n), out_dtype)
    out_spec = pl.BlockSpec((tm, tn), lambda j, i: (i, j))
    if not nc:
        return pl.pallas_call(
            body, out_shape=out_shape, grid=(nj, ni), in_specs=in_specs, out_specs=out_spec,
            compiler_params=_cp("parallel", "parallel"), name=name)(*args)
    sem = pltpu.SemaphoreType.DMA
    outs = pl.pallas_call(
        body, out_shape=(out_shape,) + tuple(_landing_shape(x, c_mode) for x in c_arrs), grid=(nj, ni),
        in_specs=in_specs + [ANY] * nc, out_specs=(out_spec,) + (ANY,) * nc,
        scratch_shapes=[sem((nc, 3)), sem((nc, 3))],
        compiler_params=_cp("arbitrary", "arbitrary"), name=name)(*args, *c_arrs)
    return outs[0], list(outs[1:])


def _call_with_carry(body, *, out_shape, grid, in_specs, out_specs, args, semantics, carry, name, aliases=None):
    if carry is None:
        return pl.pallas_call(body, out_shape=out_shape, grid=grid, in_specs=in_specs, out_specs=out_specs,
                              input_output_aliases=aliases or {}, compiler_params=_cp(*semantics), name=name)(*args)
    c_arrs, c_mode = carry
    n_in, nc = len(args), len(c_arrs)

    def carried(*refs):
        c_ins, c_outs = refs[n_in:n_in + nc], refs[n_in + nc + 1:n_in + 2 * nc + 1]
        ssem, rsem = refs[n_in + 2 * nc + 1:]
        ids = [pl.program_id(d) for d in range(len(grid))]
        first = functools.reduce(lambda u, v: u & v, [i == 0 for i in ids])
        last = functools.reduce(lambda u, v: u & v, [i == g - 1 for i, g in zip(ids, grid)])

        @pl.when(first)
        def _():
            for cp in _chip_copies(c_ins, c_outs, ssem, rsem, c_mode)[0]:
                cp.start()
        body(*refs[:n_in], refs[n_in + nc])

        @pl.when(last)
        def _():
            _chip_wait(*_chip_copies(c_ins, c_outs, ssem, rsem, c_mode))

    sem = pltpu.SemaphoreType.DMA
    outs = pl.pallas_call(
        carried, out_shape=(out_shape,) + tuple(_landing_shape(x, c_mode) for x in c_arrs), grid=grid,
        in_specs=list(in_specs) + [ANY] * nc, out_specs=(out_specs,) + (ANY,) * nc,
        scratch_shapes=[sem((nc, 3)), sem((nc, 3))], input_output_aliases=aliases or {},
        compiler_params=_cp(*(("arbitrary",) * len(grid))), name=name)(*args, *c_arrs)
    return outs[0], list(outs[1:])


def _mm_nt_sum(a_list, b_list, *, name, carry=None):
    m, n = a_list[0].shape[0], b_list[0].shape[0]
    tm = _tile(m, (640, 320, 256, 128, 64))
    np_ = len(a_list)

    def body(*refs):
        acc = None
        for a_ref, b_ref in zip(refs[:np_], refs[np_:2 * np_]):
            r = lax.dot_general(a_ref[...].astype(BF16), b_ref[...].astype(BF16), (((1,), (1,)), ((), ())),
                                preferred_element_type=F32)
            acc = r if acc is None else acc + r
        refs[2 * np_][...] = acc

    return _call_with_carry(
        body, out_shape=jax.ShapeDtypeStruct((m, n), F32), grid=(m // tm,),
        in_specs=[pl.BlockSpec((tm, a.shape[1]), lambda i: (i, 0)) for a in a_list]
        + [pl.BlockSpec(b.shape, lambda i: (0, 0)) for b in b_list],
        out_specs=pl.BlockSpec((tm, n), lambda i: (i, 0)), args=[*a_list, *b_list], semantics=("parallel",),
        carry=carry, name=name)


def _mm_tn(a, b, *, tn=None, blocks=None, into=None, name, carry=None):
    m, k = a.shape
    n = b.shape[1]
    tm = _tile(m, (1664, 640, 320, 256, 128, 64))
    tn = n if tn is None else tn
    if blocks is not None:
        total, first = blocks
        out_shape = jax.ShapeDtypeStruct((total, k, tn), F32)
        out_spec = pl.BlockSpec((None, k, tn), lambda j, i: (first + j, 0, 0))
    else:
        out_shape = jax.ShapeDtypeStruct((k, n), F32)
        out_spec = pl.BlockSpec((k, tn), lambda j, i: (0, j))

    def body(a_ref, b_ref, *rest):
        o_ref = rest[-1]

        @pl.when(pl.program_id(1) == 0)
        def _():
            o_ref[...] = jnp.zeros_like(o_ref)
        o_ref[...] += lax.dot_general(a_ref[...].astype(BF16), b_ref[...].astype(BF16),
                                      (((0,), (0,)), ((), ())), preferred_element_type=F32)

    in_specs = [pl.BlockSpec((tm, k), lambda j, i: (i, 0)), pl.BlockSpec((tm, tn), lambda j, i: (i, j))]
    args, alias = [a, b], {}
    if into is not None:
        in_specs.append(pl.BlockSpec(memory_space=pl.ANY))
        args.append(into)
        alias = {2: 0}
    return _call_with_carry(body, out_shape=out_shape, grid=(n // tn, m // tm), in_specs=in_specs, out_specs=out_spec,
                            args=args, semantics=("parallel", "arbitrary"), carry=carry, name=name, aliases=alias)


def _rms(x, w):
    r = lax.rsqrt(jnp.mean(x * x, axis=-1, keepdims=True) + EPS)
    return x * r * w


def _rms_bwd(x, w, dy):
    r = lax.rsqrt(jnp.mean(x * x, axis=-1, keepdims=True) + EPS)
    xh = x * r
    dxh = dy * w
    dx = r * (dxh - xh * jnp.mean(dxh * xh, axis=-1, keepdims=True))
    return dx, jnp.sum(dy * xh, axis=0, keepdims=True)


def _resid_norm(h, t, w_post, w_next, *, name):
    lp, d = h.shape
    tm = _row_tile(lp)
    has_t = t is not None

    def body(*refs):
        if has_t:
            h_ref, t_ref, wp_ref, wn_ref, ho_ref, hn_ref = refs
            hv = h_ref[...] + _rms(t_ref[...], wp_ref[...])
            ho_ref[...] = hv
        else:
            h_ref, wn_ref, hn_ref = refs
            hv = h_ref[...]
        hn_ref[...] = _rms(hv, wn_ref[...]).astype(BF16)

    row = pl.BlockSpec((tm, d), lambda i: (i, 0))
    vec = pl.BlockSpec((1, d), lambda i: (0, 0))
    if has_t:
        return pl.pallas_call(
            body, out_shape=(jax.ShapeDtypeStruct((lp, d), F32), jax.ShapeDtypeStruct((lp, d), BF16)),
            grid=(lp // tm,), in_specs=[row, row, vec, vec], out_specs=(row, row),
            compiler_params=_cp("parallel"), name=name)(h, t, w_post, w_next)
    return h, pl.pallas_call(
        body, out_shape=jax.ShapeDtypeStruct((lp, d), BF16), grid=(lp // tm,), in_specs=[row, vec],
        out_specs=row, compiler_params=_cp("parallel"), name=name)(h, w_next)


def _resid_norm_bwd(dh_out, dhn, h_new, t, w_next, w_post, *, name):
    lp, d = h_new.shape if h_new is not None else t.shape
    tm = _row_tile(lp)
    has_n = dhn is not None
    has_t = t is not None

    def body(*refs):
        refs = list(refs)
        dho_ref = refs.pop(0)
        if has_n:
            dhn_ref, hn_ref, wn_ref = refs.pop(0), refs.pop(0), refs.pop(0)
        if has_t:
            t_ref, wp_ref = refs.pop(0), refs.pop(0)
        dh_ref = refs.pop(0) if has_n else None
        dt_ref = refs.pop(0) if has_t else None
        dwn_ref = refs.pop(0) if has_n else None
        dwp_ref = refs.pop(0) if has_t else None
        first = pl.program_id(0) == 0
        dh = dho_ref[...]
        if has_n:
            dx, dwn = _rms_bwd(hn_ref[...], wn_ref[...], dhn_ref[...])
            dh = dh + dx
            dh_ref[...] = dh

            @pl.when(first)
            def _():
                dwn_ref[...] = jnp.zeros_like(dwn_ref)
            dwn_ref[...] += dwn
        if has_t:
            dt, dwp = _rms_bwd(t_ref[...], wp_ref[...], dh)
            dt_ref[...] = dt.astype(BF16)

            @pl.when(first)
            def _():
                dwp_ref[...] = jnp.zeros_like(dwp_ref)
            dwp_ref[...] += dwp

    row = pl.BlockSpec((tm, d), lambda i: (i, 0))
    vec = pl.BlockSpec((1, d), lambda i: (0, 0))
    args, in_specs, out_shape, out_specs = [dh_out], [row], [], []
    if has_n:
        args += [dhn, h_new, w_next]
        in_specs += [row, row, vec]
    if has_t:
        args += [t, w_post]
        in_specs += [row, vec]
    if has_n:
        out_shape.append(jax.ShapeDtypeStruct((lp, d), F32)); out_specs.append(row)
    if has_t:
        out_shape.append(jax.ShapeDtypeStruct((lp, d), BF16)); out_specs.append(row)
    if has_n:
        out_shape.append(jax.ShapeDtypeStruct((1, d), F32)); out_specs.append(vec)
    if has_t:
        out_shape.append(jax.ShapeDtypeStruct((1, d), F32)); out_specs.append(vec)
    outs = list(pl.pallas_call(body, out_shape=tuple(out_shape), grid=(lp // tm,), in_specs=in_specs,
                               out_specs=tuple(out_specs), compiler_params=_cp("arbitrary"), name=name)(*args))
    dh = outs.pop(0) if has_n else dh_out
    dt = outs.pop(0) if has_t else None
    dwn = outs.pop(0) if has_n else None
    dwp = outs.pop(0) if has_t else None
    return dh, dt, dwn, dwp


def _loss_head(h, f, w_post, target, *, name):
    lp, d = h.shape
    tm = _row_tile(lp)

    def body(h_ref, f_ref, w_ref, t_ref, loss_ref, dy_ref, df_ref, dw_ref):
        i = pl.program_id(0)
        f, w = f_ref[...], w_ref[...]
        y = h_ref[...] + _rms(f, w)
        rows = i * tm + lax.broadcasted_iota(jnp.int32, (tm, 1), 0)
        tok = (rows >= FRONT) & (rows < lp - BACK)
        err = jnp.where(tok, y - t_ref[...], 0.0)
        dy = err * (1.0 / d)
        dy_ref[...] = dy
        df, dw = _rms_bwd(f, w, dy)
        df_ref[...] = df.astype(BF16)

        @pl.when(i == 0)
        def _():
            loss_ref[...] = jnp.zeros_like(loss_ref)
            dw_ref[...] = jnp.zeros_like(dw_ref)
        part = jnp.sum(jnp.sum(err * err, axis=1, keepdims=True), axis=0, keepdims=True) * (0.5 / d)
        loss_ref[...] += jnp.broadcast_to(part, loss_ref.shape)
        dw_ref[...] += dw

    row = pl.BlockSpec((tm, d), lambda i: (i, 0))
    vec = pl.BlockSpec((1, d), lambda i: (0, 0))
    loss, dy, df, dw = pl.pallas_call(
        body, out_shape=(jax.ShapeDtypeStruct((8, 128), F32), jax.ShapeDtypeStruct((lp, d), F32),
                         jax.ShapeDtypeStruct((lp, d), BF16), jax.ShapeDtypeStruct((1, d), F32)),
        grid=(lp // tm,), in_specs=[row, row, vec, row],
        out_specs=(pl.BlockSpec((8, 128), lambda i: (0, 0)), row, row, vec),
        compiler_params=_cp("arbitrary"), name=name)(h, f, w_post, target)
    return loss[0, 0], dy, df, dw


_GELU_C = math.sqrt(2.0 / math.pi)


def _gelu_and_grad(a):
    a2 = a * a
    t = jnp.tanh(a * (_GELU_C + (_GELU_C * 0.044715) * a2))
    ha = 0.5 * a
    h1 = 0.5 + 0.5 * t
    return a * h1, h1 + ha * (1.0 - t * t) * (_GELU_C + (3.0 * _GELU_C * 0.044715) * a2)


def _gelu(a):
    t = jnp.tanh(a * (_GELU_C + (_GELU_C * 0.044715) * (a * a)))
    return a * (0.5 + 0.5 * t)


def _conv3(parts, n, w, b):
    xx = jnp.concatenate(parts, axis=0)
    return b + xx[8:8 + n] * w[2:3] + pltpu.roll(xx, 1, 0)[8:8 + n] * w[1:2] + pltpu.roll(xx, 2, 0)[8:8 + n] * w[0:1]


def _conv_act(ua, ug, wa, wg, ba, bg, *, name):
    lp, n = ua.shape
    tm = _row_tile(lp)
    tc = _tile(n, (256, 128))
    nb8 = tm // 8

    def body(ua_ref, uap_ref, ug_ref, ugp_ref, wa_ref, wg_ref, ba_ref, bg_ref, o_ref):
        i = pl.program_id(0)
        ca = _conv3([uap_ref[...], ua_ref[...]], tm, wa_ref[...], ba_ref[...])
        cg = _conv3([ugp_ref[...], ug_ref[...]], tm, wg_ref[...], bg_ref[...])
        rows = i * tm + lax.broadcasted_iota(jnp.int32, (tm, 1), 0)
        ok = (rows >= PADF) & (rows < lp - BACK)
        o_ref[...] = jnp.where(ok, _gelu(ca) * cg, 0.0).astype(BF16)

    cur = pl.BlockSpec((tm, tc), lambda i, j: (i, j))
    prev = pl.BlockSpec((8, tc), lambda i, j: (jnp.maximum(i * nb8 - 1, 0), j))
    w3 = pl.BlockSpec((3, tc), lambda i, j: (0, j))
    b1 = pl.BlockSpec((1, tc), lambda i, j: (0, j))
    return pl.pallas_call(
        body, out_shape=jax.ShapeDtypeStruct((lp, n), BF16), grid=(lp // tm, n // tc),
        in_specs=[cur, prev, cur, prev, w3, w3, b1, b1], out_specs=cur,
        compiler_params=_cp("parallel", "parallel"), name=name)(ua, ua, ug, ug, wa, wg, ba, bg)


def _conv_act_down(ua, ug, wa, wg, ba, bg, down, *, name):
    lp, n = ua.shape
    d = down.shape[1]
    tm = _tile(lp, (320, 256, 128, 64))
    tc = _tile(n, (256, 128))
    nb8 = tm // 8

    def body(ua_ref, uap_ref, ug_ref, ugp_ref, wa_ref, wg_ref, ba_ref, bg_ref, dn_ref, act_ref, f_ref):
        i = pl.program_id(0)
        rows = i * tm + lax.broadcasted_iota(jnp.int32, (tm, 1), 0)
        ok = (rows >= PADF) & (rows < lp - BACK)
        acc = None
        for j in range(n // tc):
            cs = slice(tc * j, tc * j + tc)
            ca = _conv3([uap_ref[:, cs], ua_ref[:, cs]], tm, wa_ref[:, cs], ba_ref[:, cs])
            cg = _conv3([ugp_ref[:, cs], ug_ref[:, cs]], tm, wg_ref[:, cs], bg_ref[:, cs])
            act = jnp.where(ok, _gelu(ca) * cg, 0.0).astype(BF16)
            act_ref[:, cs] = act
            part = _dot(act, dn_ref[cs, :])
            acc = part if acc is None else acc + part
        f_ref[...] = acc

    cur = pl.BlockSpec((tm, n), lambda i: (i, 0))
    prev = pl.BlockSpec((8, n), lambda i: (jnp.maximum(i * nb8 - 1, 0), 0))
    w3 = pl.BlockSpec((3, n), lambda i: (0, 0))
    b1 = pl.BlockSpec((1, n), lambda i: (0, 0))
    return pl.pallas_call(
        body, out_shape=(jax.ShapeDtypeStruct((lp, n), BF16), jax.ShapeDtypeStruct((lp, d), F32)),
        grid=(lp // tm,),
        in_specs=[cur, prev, cur, prev, w3, w3, b1, b1, pl.BlockSpec(down.shape, lambda i: (0, 0))],
        out_specs=(cur, pl.BlockSpec((tm, d), lambda i: (i, 0))),
        compiler_params=_cp("parallel"), name=name)(ua, ua, ug, ug, wa, wg, ba, bg, down)


def _ffn_fwd(hn, up_a, up_g, wa, wg, ba, bg, down, *, name, carry=None):
    lp, d = hn.shape
    n = up_a.shape[1]
    tm = _tile(lp, (320, 256, 128, 64))
    tc = _tile(n, (256, 128))
    nchunks = n // tc
    c_arrs, c_mode = carry if carry is not None else ((), None)
    nc = len(c_arrs)
    steps = lp // tm

    def body(hn_ref, hnp_ref, upa_ref, upg_ref, wa_ref, wg_ref, ba_ref, bg_ref, dn_ref, *rest):
        c_ins = rest[:nc]
        ua_ref, ug_ref, act_ref, f_ref = rest[nc:nc + 4]
        c_outs = rest[nc + 4:2 * nc + 4]
        i = pl.program_id(0)
        if nc:
            ssem, rsem = rest[2 * nc + 4:]

            @pl.when(i == 0)
            def _():
                for cp in _chip_copies(c_ins, c_outs, ssem, rsem, c_mode)[0]:
                    cp.start()
        rows = i * tm + lax.broadcasted_iota(jnp.int32, (tm, 1), 0)
        ok = (rows >= PADF) & (rows < lp - BACK)
        x = jnp.concatenate([hnp_ref[...], hn_ref[...]], axis=0)
        u_of = lambda j: (_dot(x, upa_ref[:, tc * j:tc * j + tc]), _dot(x, upg_ref[:, tc * j:tc * j + tc]))
        u_next = u_of(0)
        acc = None
        for j in range(nchunks):
            cs = slice(tc * j, tc * j + tc)
            ua, ug = u_next
            if j + 1 < nchunks:
                u_next = u_of(j + 1)
            ua_ref[:, cs] = ua[16:]
            ug_ref[:, cs] = ug[16:]
            ca = _conv3([ua[8:]], tm, wa_ref[:, cs], ba_ref[:, cs])
            cg = _conv3([ug[8:]], tm, wg_ref[:, cs], bg_ref[:, cs])
            act = jnp.where(ok, _gelu(ca) * cg, 0.0).astype(BF16)
            act_ref[:, cs] = act
            part = _dot(act, dn_ref[cs, :])
            acc = part if acc is None else acc + part
        f_ref[...] = acc
        if nc:
            @pl.when(i == steps - 1)
            def _():
                _chip_wait(*_chip_copies(c_ins, c_outs, ssem, rsem, c_mode))

    whole = pl.BlockSpec(memory_space=pltpu.VMEM)
    wide = pl.BlockSpec((tm, n), lambda i: (i, 0))
    w3 = pl.BlockSpec((3, n), lambda i: (0, 0))
    b1 = pl.BlockSpec((1, n), lambda i: (0, 0))
    sem = pltpu.SemaphoreType.DMA
    outs = pl.pallas_call(
        body,
        out_shape=(jax.ShapeDtypeStruct((lp, n), F32), jax.ShapeDtypeStruct((lp, n), F32),
                   jax.ShapeDtypeStruct((lp, n), BF16), jax.ShapeDtypeStruct((lp, d), F32))
        + tuple(_landing_shape(a, c_mode) for a in c_arrs),
        grid=(steps,),
        in_specs=[pl.BlockSpec((tm, d), lambda i: (i, 0)),
                  pl.BlockSpec((16, d), lambda i: (jnp.maximum(i * (tm // 16) - 1, 0), 0)),
                  whole, whole, w3, w3, b1, b1, whole] + [ANY] * nc,
        out_specs=(wide, wide, wide, pl.BlockSpec((tm, d), lambda i: (i, 0))) + (ANY,) * nc,
        scratch_shapes=[sem((nc, 3)), sem((nc, 3))] if nc else [],
        compiler_params=_cp("arbitrary"), name=name)(hn, hn, up_a, up_g, wa, wg, ba, bg, down, *c_arrs)
    return outs[:4], list(outs[4:])


def _conv_act_bwd(ua, ug, df, down, wa, wg, ba, bg, up_a, up_g, *, name):
    lp, n = ua.shape
    d = up_a.shape[0]
    tm = _tile(lp, (320, 256, 128, 64))
    tc = _tile(n, (256, 128))
    nb8 = tm // 8
    last8 = lp // 8 - 1
    last16 = lp // 16 - 1
    ext = tm + 8

    def body(ua_ref, uap_ref, uan_ref, ug_ref, ugp_ref, ugn_ref, df_ref, dfn_ref, dn_ref, wa_ref, wg_ref, ba_ref,
             bg_ref, upa_ref, upg_ref, dua_ref, dug_ref, dwa_ref, dwg_ref, dba_ref, dbg_ref, dhn_ref):
        i = pl.program_id(0)
        df_ext = jnp.concatenate([df_ref[...], dfn_ref[...]], axis=0)

        @pl.when(i == 0)
        def _():
            dwa_ref[...] = jnp.zeros_like(dwa_ref)
            dwg_ref[...] = jnp.zeros_like(dwg_ref)
            dba_ref[...] = jnp.zeros_like(dba_ref)
            dbg_ref[...] = jnp.zeros_like(dbg_ref)
        rows = i * tm + lax.broadcasted_iota(jnp.int32, (ext, 1), 0)
        ok = (rows >= PADF) & (rows < lp - BACK)

        def conv(parts, w, b):
            xx = jnp.concatenate(parts, axis=0)
            x, x1, x2 = xx[8:8 + ext], pltpu.roll(xx, 1, 0)[8:8 + ext], pltpu.roll(xx, 2, 0)[8:8 + ext]
            return b + x * w[2:3] + x1 * w[1:2] + x2 * w[0:1], x, x1, x2

        def back(dc, w):
            return (dc[:tm] * w[2:3] + pltpu.roll(dc, ext - 1, 0)[:tm] * w[1:2]
                    + pltpu.roll(dc, ext - 2, 0)[:tm] * w[0:1])

        def wsum(dw_ref, db_ref, cs, dc, x, x1, x2):
            dd = dc[:tm]
            s = lambda v: jnp.sum(v, axis=0, keepdims=True)
            dw_ref[0:1, cs] += s(dd * x2[:tm])
            dw_ref[1:2, cs] += s(dd * x1[:tm])
            dw_ref[2:3, cs] += s(dd * x[:tm])
            db_ref[:, cs] += s(dd)

        acc = None
        nchunks = n // tc
        dact_of = lambda j: _dot_nt(df_ext, dn_ref[tc * j:tc * j + tc, :])[:ext]
        dact_next = dact_of(0)
        for j in range(nchunks):
            cs = slice(tc * j, tc * j + tc)
            dact_cur = dact_next
            if j + 1 < nchunks:
                dact_next = dact_of(j + 1)
            wa, wg = wa_ref[:, cs], wg_ref[:, cs]
            ca, xa, xa1, xa2 = conv([uap_ref[:, cs], ua_ref[:, cs], uan_ref[:, cs]], wa, ba_ref[:, cs])
            cg, xg, xg1, xg2 = conv([ugp_ref[:, cs], ug_ref[:, cs], ugn_ref[:, cs]], wg, bg_ref[:, cs])
            dact_e = jnp.where(ok, dact_cur, 0.0)
            gel, gel_d = _gelu_and_grad(ca)
            dca = dact_e * cg * gel_d
            dcg = dact_e * gel
            du_a, du_g = back(dca, wa).astype(BF16), back(dcg, wg).astype(BF16)
            dua_ref[:, cs] = du_a
            dug_ref[:, cs] = du_g
            wsum(dwa_ref, dba_ref, cs, dca, xa, xa1, xa2)
            wsum(dwg_ref, dbg_ref, cs, dcg, xg, xg1, xg2)
            part = _dot_nt(du_a, upa_ref[:, cs]) + _dot_nt(du_g, upg_ref[:, cs])
            acc = part if acc is None else acc + part
        dhn_ref[...] = acc

    cur = pl.BlockSpec((tm, n), lambda i: (i, 0))
    prev = pl.BlockSpec((8, n), lambda i: (jnp.maximum(i * nb8 - 1, 0), 0))
    nxt = pl.BlockSpec((8, n), lambda i: (jnp.minimum((i + 1) * nb8, last8), 0))
    w3 = pl.BlockSpec((3, n), lambda i: (0, 0))
    b1 = pl.BlockSpec((1, n), lambda i: (0, 0))
    whole = pl.BlockSpec(memory_space=pltpu.VMEM)
    return pl.pallas_call(
        body,
        out_shape=(jax.ShapeDtypeStruct((lp, n), BF16), jax.ShapeDtypeStruct((lp, n), BF16),
                   jax.ShapeDtypeStruct((3, n), F32), jax.ShapeDtypeStruct((3, n), F32),
                   jax.ShapeDtypeStruct((1, n), F32), jax.ShapeDtypeStruct((1, n), F32),
                   jax.ShapeDtypeStruct((lp, d), F32)),
        grid=(lp // tm,),
        in_specs=[cur, prev, nxt, cur, prev, nxt, pl.BlockSpec((tm, d), lambda i: (i, 0)),
                  pl.BlockSpec((16, d), lambda i: (jnp.minimum((i + 1) * (tm // 16), last16), 0)), whole,
                  w3, w3, b1, b1, whole, whole],
        out_specs=(cur, cur, w3, w3, b1, b1, pl.BlockSpec((tm, d), lambda i: (i, 0))),
        compiler_params=_cp("arbitrary"), name=name)(ua, ua, ua, ug, ug, ug, df, df, down, wa, wg, ba, bg, up_a, up_g)


def _sigmoid(x):
    return 1.0 / (1.0 + jnp.exp(-x))


def _merge(o_ret, o_gla, proj, w_ret, w_gla, *, name):
    lp = o_ret.shape[0]
    tm = _row_tile(lp)

    def body(or_ref, og_ref, rg_ref, gr_ref, wr_ref, wg_ref, m_ref):
        oret, ogla = or_ref[...], og_ref[...]
        yr, yg = [], []
        for h in range(4):
            hs = slice(128 * h, 128 * h + 128)
            o = oret[:, hs]
            xc = o - jnp.mean(o, axis=-1, keepdims=True)
            yr.append(xc * lax.rsqrt(jnp.mean(xc * xc, axis=-1, keepdims=True) + EPS))
            o = ogla[:, hs]
            yg.append(o * lax.rsqrt(jnp.mean(o * o, axis=-1, keepdims=True) + EPS))
        rg, gr = rg_ref[...], gr_ref[...]
        m_ref[:, 0:512] = (jnp.concatenate(yr, axis=1) * wr_ref[...] * (rg * _sigmoid(rg))).astype(BF16)
        m_ref[:, 512:1024] = (jnp.concatenate(yg, axis=1) * wg_ref[...] * (gr * _sigmoid(gr))).astype(BF16)

    row = pl.BlockSpec((tm, 512), lambda i: (i, 0))
    vec = pl.BlockSpec((1, 512), lambda i: (0, 0))
    return pl.pallas_call(
        body, out_shape=jax.ShapeDtypeStruct((lp, 1024), BF16), grid=(lp // tm,),
        in_specs=[row, row, pl.BlockSpec((tm, 512), lambda i: (i, C_RG // 512)),
                  pl.BlockSpec((tm, 512), lambda i: (i, C_GR // 512)), vec, vec],
        out_specs=pl.BlockSpec((tm, 1024), lambda i: (i, 0)),
        compiler_params=_cp("parallel"), name=name)(o_ret, o_gla, proj, proj, w_ret, w_gla)


def _merge_bwd(dm, o_ret, o_gla, proj, w_ret, w_gla, *, name):
    lp = o_ret.shape[0]
    tm = _row_tile(lp)

    def body(dm_ref, or_ref, og_ref, rg_ref, gr_ref, wr_ref, wg_ref, dor_ref, dog_ref, dgate_ref, dwr_ref, dwg_ref):
        @pl.when(pl.program_id(0) == 0)
        def _():
            dwr_ref[...] = jnp.zeros_like(dwr_ref)
            dwg_ref[...] = jnp.zeros_like(dwg_ref)

        def group(d, o_all, gate, w, center):
            sg = _sigmoid(gate)
            s = gate * sg
            ds = sg * (1.0 + gate * (1.0 - sg))
            xh, rr = [], []
            for h in range(4):
                o = o_all[:, 128 * h:128 * h + 128]
                if center:
                    o = o - jnp.mean(o, axis=-1, keepdims=True)
                r = lax.rsqrt(jnp.mean(o * o, axis=-1, keepdims=True) + EPS)
                xh.append(o * r)
                rr.append(r)
            xh_all = jnp.concatenate(xh, axis=1)
            dgate = d * xh_all * w * ds
            dw = jnp.sum(d * xh_all * s, axis=0, keepdims=True)
            dxh_all = d * w * s
            do = []
            for h in range(4):
                dxh = dxh_all[:, 128 * h:128 * h + 128]
                t = dxh - xh[h] * jnp.mean(dxh * xh[h], axis=-1, keepdims=True)
                if center:
                    t = t - jnp.mean(dxh, axis=-1, keepdims=True)
                do.append(rr[h] * t)
            return jnp.concatenate(do, axis=1), dgate, dw

        dmv = dm_ref[...]
        do, dg, dw = group(dmv[:, 0:512], or_ref[...], rg_ref[...], wr_ref[...], True)
        dor_ref[...] = do
        dgate_ref[:, 0:512] = dg.astype(BF16)
        dwr_ref[...] += dw
        do, dg, dw = group(dmv[:, 512:1024], og_ref[...], gr_ref[...], wg_ref[...], False)
        dog_ref[...] = do
        dgate_ref[:, 512:1024] = dg.astype(BF16)
        dwg_ref[...] += dw

    row = pl.BlockSpec((tm, 512), lambda i: (i, 0))
    vec = pl.BlockSpec((1, 512), lambda i: (0, 0))
    return pl.pallas_call(
        body,
        out_shape=(jax.ShapeDtypeStruct((lp, 512), F32), jax.ShapeDtypeStruct((lp, 512), F32),
                   jax.ShapeDtypeStruct((lp, P_GATE), BF16),
                   jax.ShapeDtypeStruct((1, 512), F32), jax.ShapeDtypeStruct((1, 512), F32)),
        grid=(lp // tm,),
        in_specs=[pl.BlockSpec((tm, 1024), lambda i: (i, 0)), row, row,
                  pl.BlockSpec((tm, 512), lambda i: (i, C_RG // 512)),
                  pl.BlockSpec((tm, 512), lambda i: (i, C_GR // 512)), vec, vec],
        out_specs=(row, row, pl.BlockSpec((tm, P_GATE), lambda i: (i, 0)), vec, vec),
        compiler_params=_cp("arbitrary"), name=name)(dm, o_ret, o_gla, proj, proj, w_ret, w_gla)


def _dot(a, b):
    return lax.dot_general(a, b, (((1,), (0,)), ((), ())), preferred_element_type=F32)


def _dot_nt(a, b):
    return lax.dot_general(a, b, (((1,), (1,)), ((), ())), preferred_element_type=F32)


def _dot_tn(a, b):
    return lax.dot_general(a, b, (((0,), (0,)), ((), ())), preferred_element_type=F32)


def _ret_tables(lp):
    cr = RET_CHUNK
    pos = np.arange(lp, dtype=np.float32) - np.float32(PADF)
    half = RET_DK // 2
    inv = (np.float32(ROPE_BASE) ** (-np.arange(half, dtype=np.float32) / np.float32(half))).astype(np.float32)
    ang = (pos[:, None] * inv[None, :]).astype(np.float32)
    c, s = np.cos(ang).astype(np.float32), np.sin(ang).astype(np.float32)
    rope_c = jnp.asarray(np.concatenate([c, c], axis=1))
    rope_s = jnp.asarray(np.concatenate([-s, s], axis=1))
    log_g = np.log(1.0 - 2.0 ** (-5.0 - np.arange(RET_HEADS, dtype=np.float64)))
    idx = np.arange(cr, dtype=np.float64)
    diff = idx[:, None] - idx[None, :]
    dmat = np.where(diff >= 0, np.exp(log_g[:, None, None] * np.maximum(diff, 0.0)), 0.0)
    zeta = np.exp(log_g[:, None] * (cr - 1.0 - idx)[None, :])
    xi = np.exp(log_g[:, None] * (idx + 1.0)[None, :])
    gc = np.exp(log_g * cr)
    f = lambda a: jnp.asarray(a.astype(np.float32))
    return (rope_c, rope_s, f(dmat), f(np.broadcast_to(zeta[:, :, None], (RET_HEADS, cr, 128))),
            f(np.broadcast_to(xi[:, :, None], (RET_HEADS, cr, 128))),
            f(np.broadcast_to(gc[:, None, None], (RET_HEADS, 8, 128))))


def _rope(t, c, s):
    return t * c + pltpu.roll(t, 64, 1) * s


def _rope_t(d, c, s):
    return d * c + pltpu.roll(d * s, 64, 1)


def _ret_specs(nblk, rev):
    ix = (lambda i: nblk - 1 - i) if rev else (lambda i: i)
    cr = RET_CHUNK
    col = lambda base: pl.BlockSpec((BLK, 512), lambda i: (ix(i), base // 512))
    tab = pl.BlockSpec((BLK, 128), lambda i: (ix(i), 0))
    sq = pl.BlockSpec((RET_HEADS, cr, cr), lambda i: (0, 0, 0))
    hv = pl.BlockSpec((RET_HEADS, cr, 128), lambda i: (0, 0, 0))
    g8 = pl.BlockSpec((RET_HEADS, 8, 128), lambda i: (0, 0, 0))
    st = pl.BlockSpec((RET_HEADS, BLK // cr, 128, 128), lambda i: (0, ix(i), 0, 0))
    out = pl.BlockSpec((BLK, 512), lambda i: (ix(i), 0))
    return col, tab, sq, hv, g8, st, out


def _retention(proj, tables, *, name):
    lp = proj.shape[0]
    nblk, cr = lp // BLK, RET_CHUNK
    scale = RET_DK ** -0.5

    def body(q_ref, k_ref, v_ref, c_ref, s_ref, d_ref, z_ref, x_ref, g_ref, o_ref, st_ref, state):
        @pl.when(pl.program_id(0) == 0)
        def _():
            state[...] = jnp.zeros_like(state)

        def chunk(ci, carry):
            sl = pl.ds(pl.multiple_of(ci * cr, cr), cr)
            c, s = c_ref[sl, :], s_ref[sl, :]
            for h in range(RET_HEADS):
                hs = slice(128 * h, 128 * h + 128)
                q = _rope(q_ref[sl, hs], c, s)
                k = _rope(k_ref[sl, hs], c, s) * scale
                qb, kb, vb = q.astype(BF16), k.astype(BF16), v_ref[sl, hs].astype(BF16)
                st = state[h]
                st_ref[h, ci] = st
                sc = _dot_nt(qb, kb) * d_ref[h]
                o_ref[sl, hs] = _dot(sc.astype(BF16), vb) + _dot(qb, st.astype(BF16)) * x_ref[h]
                state[h] = st * g_ref[h][0:1, :] + _dot_tn((k * z_ref[h]).astype(BF16), vb)
            return carry

        lax.fori_loop(0, BLK // cr, chunk, 0)

    col, tab, sq, hv, g8, st, out = _ret_specs(nblk, False)
    return pl.pallas_call(
        body,
        out_shape=(jax.ShapeDtypeStruct((lp, 512), F32), jax.ShapeDtypeStruct((4, lp // cr, 128, 128), F32)),
        grid=(nblk,), in_specs=[col(C_RQ), col(C_RK), col(C_RV), tab, tab, sq, hv, hv, g8],
        out_specs=(out, st), scratch_shapes=[pltpu.VMEM((RET_HEADS, 128, 128), F32)],
        compiler_params=_cp("arbitrary"), name=name)(proj, proj, proj, *tables)


def _retention_bwd(proj, do, states, tables, *, name):
    lp = proj.shape[0]
    nblk, cr = lp // BLK, RET_CHUNK
    nch = BLK // cr
    scale = RET_DK ** -0.5

    def body(q_ref, k_ref, v_ref, do_ref, st_ref, c_ref, s_ref, d_ref, z_ref, x_ref, g_ref, dqkv_ref, dstate):
        @pl.when(pl.program_id(0) == 0)
        def _():
            dstate[...] = jnp.zeros_like(dstate)

        def chunk(cc, carry):
            ci = nch - 1 - cc
            sl = pl.ds(pl.multiple_of(ci * cr, cr), cr)
            c, s = c_ref[sl, :], s_ref[sl, :]
            for h in range(RET_HEADS):
                hs = slice(128 * h, 128 * h + 128)
                dmat, zeta, xi = d_ref[h], z_ref[h], x_ref[h]
                q = _rope(q_ref[sl, hs], c, s)
                k = _rope(k_ref[sl, hs], c, s) * scale
                qb, kb, vb = q.astype(BF16), k.astype(BF16), v_ref[sl, hs].astype(BF16)
                kzb = (k * zeta).astype(BF16)
                dov = do_ref[sl, hs]
                dob, doxb = dov.astype(BF16), (dov * xi).astype(BF16)
                stb = st_ref[h, ci].astype(BF16)
                dsn = dstate[h]
                dsnb = dsn.astype(BF16)
                scb = (_dot_nt(qb, kb) * dmat).astype(BF16)
                dscb = (_dot_nt(dob, vb) * dmat).astype(BF16)
                dq = _dot(dscb, kb) + _dot_nt(doxb, stb)
                dk = _dot_tn(dscb, qb) + _dot_nt(vb, dsnb) * zeta
                dv = _dot_tn(scb, dob) + _dot(kzb, dsnb)
                dstate[h] = dsn * g_ref[h][0:1, :] + _dot_tn(qb, doxb)
                dqkv_ref[sl, 128 * h:128 * h + 128] = _rope_t(dq, c, s).astype(BF16)
                dqkv_ref[sl, 512 + 128 * h:640 + 128 * h] = _rope_t(dk * scale, c, s).astype(BF16)
                dqkv_ref[sl, 1024 + 128 * h:1152 + 128 * h] = dv.astype(BF16)
            return carry

        lax.fori_loop(0, nch, chunk, 0)

    col, tab, sq, hv, g8, st, out = _ret_specs(nblk, True)
    return pl.pallas_call(
        body, out_shape=jax.ShapeDtypeStruct((lp, P_RET), BF16), grid=(nblk,),
        in_specs=[col(C_RQ), col(C_RK), col(C_RV), out, st, tab, tab, sq, hv, hv, g8],
        out_specs=pl.BlockSpec((BLK, P_RET), lambda i: (nblk - 1 - i, 0)),
        scratch_shapes=[pltpu.VMEM((RET_HEADS, 128, 128), F32)],
        compiler_params=_cp("arbitrary"), name=name)(proj, proj, proj, do, states, *tables)


def _gla_tables():
    c = GLA_CHUNK
    tri = np.tril(np.ones((c, c), np.float32))
    ones_qv = np.kron(np.eye(GLA_HEADS, dtype=np.float32), np.ones((GLA_DK, GLA_DV), np.float32))
    return (jnp.asarray(tri, BF16), jnp.asarray(tri.T.copy(), BF16), jnp.asarray(ones_qv, BF16),
            jnp.asarray(ones_qv.T.copy(), BF16))


def _split3(x):
    hi = x.astype(BF16)
    r1 = x - hi.astype(F32)
    mid = r1.astype(BF16)
    lo = (r1 - mid.astype(F32)).astype(BF16)
    return hi, mid, lo


def _tri_sum(tri, x):
    hi, mid, lo = _split3(x)
    return _dot(tri, hi) + _dot(tri, mid) + _dot(tri, lo)


def _head_masks(width, per):
    lane = lax.broadcasted_iota(jnp.int32, (1, width), 1)
    return [((lane >= per * h) & (lane < per * (h + 1))).astype(F32) for h in range(GLA_HEADS)]


def _stack_heads(x, masks):
    return jnp.concatenate([x * m for m in masks], axis=0)


def _gla_gate(ga, w2, b, ok, tri):
    z = _dot(ga.astype(BF16), w2) + b
    la = (jnp.minimum(z, 0.0) - jnp.log(1.0 + jnp.exp(-jnp.abs(z)))) * (1.0 / GLA_TAU)
    la = jnp.where(ok, la, 0.0)
    return z, _tri_sum(tri, la)


def _gla_rows(i_blk, ci, lp):
    c = GLA_CHUNK
    rows = i_blk * BLK + ci * c + lax.broadcasted_iota(jnp.int32, (c, 1), 0)
    return (rows >= PADF) & (rows < lp - BACK)


N_SUB = GLA_CHUNK // GLA_SUB - 1
N_SUB2 = GLA_SUB // GLA_SUB2 - 1


def _gla_masks():
    c, s1, s2 = GLA_CHUNK, GLA_SUB, GLA_SUB2
    sh1, sh2 = s1.bit_length() - 1, s2.bit_length() - 1
    r = lax.broadcasted_iota(jnp.int32, (c, GLA_QK), 0)
    blk, within = jnp.right_shift(r, sh1), jnp.bitwise_and(r, s1 - 1)
    grp = jnp.right_shift(within, sh2)
    rowm = [(blk == a).astype(F32) for a in range(1, N_SUB + 1)] + [(grp == b).astype(F32) for b in range(1, N_SUB2 + 1)]
    keym = ([(r < s1 * a).astype(F32) for a in range(1, N_SUB + 1)]
            + [(within < s2 * b).astype(F32) for b in range(1, N_SUB2 + 1)])
    rs = lax.broadcasted_iota(jnp.int32, (GLA_HEADS * c, c), 0)
    ts = lax.broadcasted_iota(jnp.int32, (GLA_HEADS * c, c), 1)
    same = (jnp.right_shift(jnp.bitwise_and(rs, c - 1), sh1) == jnp.right_shift(ts, sh1)).astype(F32)
    lag = [(jnp.bitwise_and(r, s2 - 1) >= j).astype(F32) for j in range(s2)]
    return rowm, keym, same, lag


def _gla_hats(qs, k, g, masks, hm_q):
    c, s1, s2 = GLA_CHUNK, GLA_SUB, GLA_SUB2
    rowm, keym, same, _ = masks
    refs = [g[s1 * a - 1:s1 * a, :] for a in range(1, N_SUB + 1)]
    for b in range(1, N_SUB2 + 1):
        refs.append(jnp.concatenate([jnp.broadcast_to(g[s1 * i + s2 * b - 1:s1 * i + s2 * b, :], (s1, GLA_QK))
                                     for i in range(c // s1)], axis=0))
    eqs = [jnp.exp(jnp.minimum(g - r, 0.0)) * m for r, m in zip(refs, rowm)]
    eks = [jnp.exp(jnp.minimum(r - g, 0.0)) * m for r, m in zip(refs, keym)]
    qhs, khs = [qs * e for e in eqs], [k * e for e in eks]
    qst = [_stack_heads(q, hm_q).astype(BF16) for q in qhs]
    khb = [x.astype(BF16) for x in khs]
    qa, qb = jnp.concatenate(qst[:N_SUB], axis=1), jnp.concatenate(qst[N_SUB:], axis=1)
    ka, kb = jnp.concatenate(khb[:N_SUB], axis=1), jnp.concatenate(khb[N_SUB:], axis=1)
    p = _dot_nt(qa, ka) + _dot_nt(qb, kb) * same
    return eqs, eks, qhs, khs, qa, qb, ka, kb, p


def _roll_rows(x, j):
    return x if j == 0 else pltpu.roll(x, j, 0)


def _gla(proj, w2p, b, tables, *, name):
    lp = proj.shape[0]
    nblk, c, s2 = lp // BLK, GLA_CHUNK, GLA_SUB2
    nch = BLK // c

    def body(q_ref, k_ref, v_ref, a_ref, w_ref, b_ref, tri_ref, ones_ref, o_ref, st_ref, state):
        i_blk = pl.program_id(0)

        @pl.when(i_blk == 0)
        def _():
            state[...] = jnp.zeros_like(state)
        hm_q = _head_masks(GLA_QK, GLA_DK)
        masks = _gla_masks()
        tri, ones_qv, w2, bias = tri_ref[...], ones_ref[...], w_ref[...], b_ref[...]

        def chunk(ci, carry):
            sl = pl.ds(pl.multiple_of(ci * c, c), c)
            ok = _gla_rows(i_blk, ci, lp)
            k, v = k_ref[sl, :], v_ref[sl, :]
            vb = v.astype(BF16)
            qs = q_ref[sl, :] * (GLA_DK ** -0.5)
            _, g = _gla_gate(a_ref[sl, :], w2, bias, ok, tri)
            last = g[c - 1:c, :]
            st = state[...]
            st_ref[ci] = st
            qst = _stack_heads(qs * jnp.exp(g), hm_q).astype(BF16)
            oi = _dot_nt(qst, st.astype(BF16))
            o = jnp.concatenate([oi[c * h:c * h + c, :] for h in range(GLA_HEADS)], axis=1)
            ke = k * jnp.exp(last - g)
            f = _dot_tn(vb, ke.astype(BF16))
            upd = f[0:GLA_DV, :] * hm_q[0]
            for h in range(1, GLA_HEADS):
                upd = upd + f[GLA_DV * h:GLA_DV * (h + 1), :] * hm_q[h]
            state[...] = st * jnp.exp(last) + upd
            p = _gla_hats(qs, k, g, masks, hm_q)[-1]
            ob = _dot(p.astype(BF16), vb)
            o = o + jnp.concatenate([ob[c * h:c * h + c, GLA_DV * h:GLA_DV * (h + 1)] for h in range(GLA_HEADS)],
                                    axis=1)
            ws = []
            for j in range(s2):
                ej = jnp.exp(jnp.minimum(g - _roll_rows(g, j), 0.0))
                ws.append((qs * _roll_rows(k, j) * ej * masks[3][j]).astype(BF16))
            ball = _dot(jnp.concatenate(ws, axis=0), ones_qv)
            for j in range(s2):
                o = o + ball[c * j:c * j + c, :] * _roll_rows(v, j)
            o_ref[sl, :] = o
            return carry

        lax.fori_loop(0, nch, chunk, 0)

    tri, _, ones_qv, _ = tables
    full = lambda arr: pl.BlockSpec(arr.shape, lambda i: (0,) * arr.ndim)
    return pl.pallas_call(
        body,
        out_shape=(jax.ShapeDtypeStruct((lp, GLA_V), F32), jax.ShapeDtypeStruct((lp // c, GLA_DV, GLA_QK), F32)),
        grid=(nblk,),
        in_specs=[pl.BlockSpec((BLK, GLA_QK), lambda i: (i, C_GQ // GLA_QK)),
                  pl.BlockSpec((BLK, GLA_QK), lambda i: (i, C_GK // GLA_QK)),
                  pl.BlockSpec((BLK, GLA_V), lambda i: (i, C_GV // GLA_V)),
                  pl.BlockSpec((BLK, 128), lambda i: (i, C_GA // 128)),
                  full(w2p), full(b), full(tri), full(ones_qv)],
        out_specs=(pl.BlockSpec((BLK, GLA_V), lambda i: (i, 0)),
                   pl.BlockSpec((nch, GLA_DV, GLA_QK), lambda i: (i, 0, 0))),
        scratch_shapes=[pltpu.VMEM((GLA_DV, GLA_QK), F32)],
        compiler_params=_cp("arbitrary"), name=name)(proj, proj, proj, proj, w2p, b, tri, ones_qv)


def _gla_bwd(proj, do, states, w2p, b, tables, *, name):
    lp = proj.shape[0]
    nblk, c, s1, s2 = lp // BLK, GLA_CHUNK, GLA_SUB, GLA_SUB2
    nch = BLK // c

    def body(q_ref, k_ref, v_ref, a_ref, do_ref, st_ref, w_ref, b_ref, tri_ref, trit_ref, ones_ref, onest_ref,
             dp_ref, dw_ref, db_ref, dstate, dqs_s, dk_s, dg_s, dv_s):
        i_blk = nblk - 1 - pl.program_id(0)

        @pl.when(pl.program_id(0) == 0)
        def _():
            dstate[...] = jnp.zeros_like(dstate)
            dw_ref[...] = jnp.zeros_like(dw_ref)
            db_ref[...] = jnp.zeros_like(db_ref)
        hm_q = _head_masks(GLA_QK, GLA_DK)
        hm_v = _head_masks(GLA_V, GLA_DV)
        masks = _gla_masks()
        tri, trit, ones_qv, ones_vq = tri_ref[...], trit_ref[...], ones_ref[...], onest_ref[...]
        w2, bias = w_ref[...], b_ref[...]
        rsum = lambda x: jnp.sum(x, axis=0, keepdims=True)

        def chunk(cc, carry):
            ci = nch - 1 - cc
            sl = pl.ds(pl.multiple_of(ci * c, c), c)
            ok = _gla_rows(i_blk, ci, lp)
            k, v, ga = k_ref[sl, :], v_ref[sl, :], a_ref[sl, :]
            vb = v.astype(BF16)
            qs = q_ref[sl, :] * (GLA_DK ** -0.5)
            z, g = _gla_gate(ga, w2, bias, ok, tri)
            last = g[c - 1:c, :]
            elast = jnp.exp(last)
            eg = jnp.exp(g)
            ekl = jnp.exp(last - g)
            qe, ke = qs * eg, k * ekl
            dov = do_ref[sl, :]
            st = st_ref[ci]
            dsn = dstate[...]
            qst = _stack_heads(qe, hm_q).astype(BF16)
            dost = jnp.concatenate([dov[:, GLA_DV * h:GLA_DV * (h + 1)] for h in range(GLA_HEADS)], axis=0).astype(BF16)
            dqe_st = _dot(dost, st.astype(BF16))
            dqe = dqe_st[0:c, :] * hm_q[0]
            for h in range(1, GLA_HEADS):
                dqe = dqe + dqe_st[c * h:c * h + c, :] * hm_q[h]
            dstate[...] = _dot_tn(dost, qst) + dsn * elast
            dlast = rsum(dsn * st) * elast
            df = _stack_heads(dsn, hm_q).astype(BF16)
            dv_s[...] = _dot_nt(ke.astype(BF16), df)
            dke = _dot(vb, df)
            xk = dke * ke
            dqs_s[...] = dqe * eg
            dk_s[...] = dke * ekl
            dg_s[...] = dqe * qe - xk
            dlast = dlast + rsum(xk)
            eqs, eks, qhs, khs, qa, qb, ka, kb, p = _gla_hats(qs, k, g, masks, hm_q)
            dost_v = _stack_heads(dov, hm_v).astype(BF16)
            dp = _dot_nt(dost_v, vb)
            dv_s[...] += _dot_tn(p.astype(BF16), dost_v)
            dpa, dpb = dp.astype(BF16), (dp * masks[2]).astype(BF16)
            dq_all = (_dot(dpa, ka), _dot(dpb, kb))
            dk_all = (_dot_tn(dpa, qa), _dot_tn(dpb, qb))
            for t in range(N_SUB + N_SUB2):
                lvl, i = (0, t) if t < N_SUB else (1, t - N_SUB)
                cols = slice(GLA_QK * i, GLA_QK * (i + 1))
                dq_st = dq_all[lvl][:, cols]
                dqh = dq_st[0:c, :] * hm_q[0]
                for h in range(1, GLA_HEADS):
                    dqh = dqh + dq_st[c * h:c * h + c, :] * hm_q[h]
                dkh = dk_all[lvl][:, cols]
                xq, xkh = dqh * qhs[t], dkh * khs[t]
                dqs_s[...] += dqh * eqs[t]
                dk_s[...] += dkh * eks[t]
                dg_s[...] += xq - xkh
                back_ref = xkh - xq
                if lvl == 0:
                    row = s1 * (i + 1) - 1
                    dg_s[row:row + 1, :] += rsum(back_ref)
                else:
                    for blk in range(c // s1):
                        row = s1 * blk + s2 * (i + 1) - 1
                        dg_s[row:row + 1, :] += rsum(back_ref[s1 * blk:s1 * blk + s1, :])
            kes, qes, ws, dbs = [], [], [], []
            for j in range(s2):
                em = jnp.exp(jnp.minimum(g - _roll_rows(g, j), 0.0)) * masks[3][j]
                kes.append(_roll_rows(k, j) * em)
                qes.append(qs * em)
                ws.append((qs * kes[j]).astype(BF16))
                dbs.append((dov * _roll_rows(v, j)).astype(BF16))
            ball = _dot(jnp.concatenate(ws, axis=0), ones_qv)
            dwall = _dot(jnp.concatenate(dbs, axis=0), ones_vq)
            for j in range(s2):
                back = (lambda x: x) if j == 0 else (lambda x, j=j: pltpu.roll(x, c - j, 0))
                dw = dwall[c * j:c * j + c, :]
                dv_s[...] += back(ball[c * j:c * j + c, :] * dov)
                dqs_s[...] += dw * kes[j]
                dk_s[...] += back(dw * qes[j])
                x = dw * qs * kes[j]
                dg_s[...] += x - back(x)
            dg_s[c - 1:c, :] += dlast
            dla = jnp.where(ok, _tri_sum(trit, dg_s[...]), 0.0)
            dz = dla * (1.0 / GLA_TAU) / (1.0 + jnp.exp(z))
            dzb = dz.astype(BF16)
            dp_ref[sl, 0:256] = (dqs_s[...] * (GLA_DK ** -0.5)).astype(BF16)
            dp_ref[sl, 256:512] = dk_s[...].astype(BF16)
            dp_ref[sl, 512:1024] = dv_s[...].astype(BF16)
            dp_ref[sl, 1024:1152] = _dot_nt(dzb, w2).astype(BF16)
            dp_ref[sl, 1152:1280] = jnp.zeros((c, 128), BF16)
            dw_ref[...] += _dot_tn(ga.astype(BF16), dzb)
            db_ref[...] += rsum(dz)
            return carry

        lax.fori_loop(0, nch, chunk, 0)

    tri, trit, ones_qv, ones_vq = tables
    full = lambda arr: pl.BlockSpec(arr.shape, lambda i: (0,) * arr.ndim)
    rev = lambda i: nblk - 1 - i
    return pl.pallas_call(
        body,
        out_shape=(jax.ShapeDtypeStruct((lp, P_GLA), BF16),
                   jax.ShapeDtypeStruct((128, GLA_QK), F32), jax.ShapeDtypeStruct((1, GLA_QK), F32)),
        grid=(nblk,),
        in_specs=[pl.BlockSpec((BLK, GLA_QK), lambda i: (rev(i), C_GQ // GLA_QK)),
                  pl.BlockSpec((BLK, GLA_QK), lambda i: (rev(i), C_GK // GLA_QK)),
                  pl.BlockSpec((BLK, GLA_V), lambda i: (rev(i), C_GV // GLA_V)),
                  pl.BlockSpec((BLK, 128), lambda i: (rev(i), C_GA // 128)),
                  pl.BlockSpec((BLK, GLA_V), lambda i: (rev(i), 0)),
                  pl.BlockSpec((nch, GLA_DV, GLA_QK), lambda i: (rev(i), 0, 0)),
                  full(w2p), full(b), full(tri), full(trit), full(ones_qv), full(ones_vq)],
        out_specs=(pl.BlockSpec((BLK, P_GLA), lambda i: (rev(i), 0)),
                   pl.BlockSpec((128, GLA_QK), lambda i: (0, 0)),
                   pl.BlockSpec((1, GLA_QK), lambda i: (0, 0))),
        scratch_shapes=[pltpu.VMEM((GLA_DV, GLA_QK), F32), pltpu.VMEM((c, GLA_QK), F32),
                        pltpu.VMEM((c, GLA_QK), F32), pltpu.VMEM((c, GLA_QK), F32), pltpu.VMEM((c, GLA_V), F32)],
        compiler_params=_cp("arbitrary"), name=name)(proj, proj, proj, proj, do, states, w2p, b, tri, trit, ones_qv, ones_vq)


def _as2d(a):
    return a.reshape(-1, a.shape[-1])


def _ew_tile(r):
    return _tile(r, (512, 256, 128, 64, 32, 16, 8))


def _add2(a, b, *, out_dtype, name):
    a2, b2 = _as2d(a), _as2d(b)
    r, n = a2.shape
    tm = _ew_tile(r)

    def body(a_ref, b_ref, o_ref):
        o_ref[...] = (a_ref[...] + b_ref[...]).astype(o_ref.dtype)

    blk = pl.BlockSpec((tm, n), lambda i: (i, 0))
    return pl.pallas_call(body, out_shape=jax.ShapeDtypeStruct((r, n), out_dtype), grid=(r // tm,), in_specs=[blk, blk],
                          out_specs=blk, compiler_params=_cp("parallel"), name=name)(a2, b2).reshape(a.shape)


def _sum_slots(own, q, *, name):
    shape = own.shape
    q3 = q.reshape(3, -1, shape[-1])
    own2 = _as2d(own)
    r, n = own2.shape
    tm = _ew_tile(r)

    def body(own_ref, q_ref, o_ref):
        f = lambda i: q_ref[i].astype(F32)
        o_ref[...] = ((own_ref[...].astype(F32) + f(0)) + f(1)) + f(2)

    blk = pl.BlockSpec((tm, n), lambda i: (i, 0))
    return pl.pallas_call(
        body, out_shape=jax.ShapeDtypeStruct((r, n), F32), grid=(r // tm,),
        in_specs=[blk, pl.BlockSpec((3, tm, n), lambda i: (0, i, 0))], out_specs=blk,
        compiler_params=_cp("parallel"), name=name)(own2, q3).reshape(shape)


def _adamw(w, g, m, v, *, name):
    shape = w.shape
    w2, g2, m2, v2 = _as2d(w), _as2d(g), _as2d(m), _as2d(v)
    r, n = w2.shape
    tm = _ew_tile(r)

    def body(w_ref, g_ref, m_ref, v_ref, d_ref, mo_ref, vo_ref):
        d_ref[...], mo_ref[...], vo_ref[...] = _adam_math(w_ref[...], g_ref[...], m_ref[...], v_ref[...])

    blk = pl.BlockSpec((tm, n), lambda i: (i, 0))
    o = jax.ShapeDtypeStruct((r, n), F32)
    d, mo, vo = pl.pallas_call(body, out_shape=(o, o, o), grid=(r // tm,), in_specs=[blk] * 4, out_specs=(blk,) * 3,
                               compiler_params=_cp("parallel"), name=name)(w2, g2, m2, v2)
    return d.reshape(shape), mo.reshape(shape), vo.reshape(shape)


def _adam_math(w, gv, m, v):
    c1 = 1.0 - ADAM_B1 ** ADAM_STEP
    c2 = 1.0 - ADAM_B2 ** ADAM_STEP
    mn = ADAM_B1 * m + (1.0 - ADAM_B1) * gv
    vn = ADAM_B2 * v + (1.0 - ADAM_B2) * (gv * gv)
    return -ADAM_LR * ((mn / c1) / (jnp.sqrt(vn / c2) + ADAM_EPS) + ADAM_WD * w), mn, vn


def _adamw_halves(w, m, v, mine, theirs, c, *, name):
    depth, rows, n = w.shape
    r2 = rows // 2
    tm = next(t for t in range(min(r2, 256), 0, -8) if r2 % t == 0)
    steps = r2 // tm

    def body(c_ref, w_ref, m_ref, v_ref, *rest):
        halves, (g_ref, d_ref, mo_ref, vo_ref) = rest[:2 * depth], rest[2 * depth:]
        l, h = pl.program_id(0), pl.program_id(1)
        gv = None
        for k in range(depth):
            gk = jnp.where(h == c_ref[0], halves[2 * k][...], halves[2 * k + 1][...])
            gv = gk if gv is None else jnp.where(l == k, gk, gv)
        g_ref[...] = gv
        d_ref[...], mo_ref[...], vo_ref[...] = _adam_math(w_ref[...], gv, m_ref[...], v_ref[...])

    big = pl.BlockSpec((tm, n), lambda l, h, i, c_ref: ((2 * l + h) * steps + i, 0))
    half = lambda k: pl.BlockSpec((tm, n), lambda l, h, i, c_ref: (jnp.where(l == k, i, 0), 0))
    o = jax.ShapeDtypeStruct((depth * rows, n), F32)
    args = [a for k in range(depth) for a in (mine[k], theirs[k])]
    outs = pl.pallas_call(
        body, out_shape=(o, o, o, o),
        grid_spec=pltpu.PrefetchScalarGridSpec(
            num_scalar_prefetch=1, grid=(depth, 2, steps),
            in_specs=[big, big, big] + [half(k) for k in range(depth) for _ in range(2)], out_specs=(big,) * 4),
        compiler_params=_cp("arbitrary", "arbitrary", "arbitrary"), name=name)(
            jnp.reshape(c, (1,)).astype(jnp.int32), _as2d(w), _as2d(m), _as2d(v), *args)
    return [a.reshape(w.shape) for a in outs]


ANY = pl.BlockSpec(memory_space=pl.ANY)


def _place():
    return lax.axis_index("x"), lax.axis_index("y"), lax.axis_index("c")


def _other_chips(x, y):
    return [(1 - x, y), (x, 1 - y), (1 - x, 1 - y)]


def _remote(src, dst, ssem, rsem, dev):
    return pltpu.make_async_remote_copy(src_ref=src, dst_ref=dst, send_sem=ssem, recv_sem=rsem, device_id=dev,
                                        device_id_type=MESH)


def _allgather_chips(arrs, *, name):
    n = len(arrs)

    def body(*refs):
        ins, outs = refs[:n], refs[n:2 * n]
        s1, r1, s2, r2 = refs[2 * n:]
        x, y, c = _place()
        q = 2 * x + y
        chips = _other_chips(x, y)
        qs = [2 * cx + cy for cx, cy in chips]
        sib = (x, y, 1 - c)
        first, passed = [], []
        for k in range(n):
            for j, chip in enumerate(chips):
                first.append(_remote(ins[k].at[c], outs[k].at[c, q], s1.at[k, j], r1.at[k, j], (*chip, c)))
        for cp in first:
            cp.start()
        for k in range(n):
            for j, chip in enumerate(chips):
                land = outs[k].at[c, qs[j]]
                _remote(land, land, s1.at[k, j], r1.at[k, j], (*chip, c)).wait_recv()
                fw = _remote(land, land, s2.at[k, j], r2.at[k, j], sib)
                fw.start()
                passed.append(fw)
        for k in range(n):
            for j in range(3):
                land = outs[k].at[1 - c, qs[j]]
                _remote(land, land, s2.at[k, j], r2.at[k, j], sib).wait_recv()
        for cp in first + passed:
            cp.wait_send()

    sem = pltpu.SemaphoreType.DMA
    outs = pl.pallas_call(
        body, out_shape=tuple(jax.ShapeDtypeStruct((2, 4) + a.shape[1:], a.dtype) for a in arrs),
        in_specs=[ANY] * n, out_specs=(ANY,) * n,
        scratch_shapes=[sem((n, 3)), sem((n, 3)), sem((n, 3)), sem((n, 3))], name=name)(*arrs)
    chip = 2 * lax.axis_index("x") + lax.axis_index("y")
    return [lax.dynamic_update_slice_in_dim(o, a[:, None], chip, axis=1) for o, a in zip(outs, arrs)]


def _pair_exchange(arrs, *, name):
    n = len(arrs)

    def body(*refs):
        ins, outs = refs[:n], refs[n:2 * n]
        ssem, rsem = refs[2 * n:]
        x, y, c = _place()
        cps = [_remote(ins[k].at[:, 1 - c], outs[k], ssem.at[k], rsem.at[k], (x, y, 1 - c)) for k in range(n)]
        for cp in cps:
            cp.start()
        for cp in cps:
            cp.wait()

    sem = pltpu.SemaphoreType.DMA
    return pl.pallas_call(
        body, out_shape=tuple(jax.ShapeDtypeStruct((a.shape[0],) + a.shape[2:], a.dtype) for a in arrs),
        in_specs=[ANY] * n, out_specs=(ANY,) * n, scratch_shapes=[sem((n,)), sem((n,))], name=name)(*arrs)


def _pair_sum(mine, theirs, c, *, name):
    _, _, r, n = mine.shape
    tm = r if r <= 512 else _ew_tile(r)

    def body(c_ref, a_ref, b_ref, o_ref):
        o_ref[...] = (a_ref[...] + b_ref[...]).astype(BF16)

    blk = pl.BlockSpec((None, tm, n), lambda s, i, c_ref: (s, i, 0))
    return pl.pallas_call(
        body, out_shape=jax.ShapeDtypeStruct((4, r, n), BF16),
        grid_spec=pltpu.PrefetchScalarGridSpec(
            num_scalar_prefetch=1, grid=(4, r // tm),
            in_specs=[pl.BlockSpec((None, None, tm, n), lambda s, i, c_ref: (s, c_ref[0], i, 0)), blk], out_specs=blk),
        compiler_params=_cp("parallel", "parallel"), name=name)(jnp.reshape(c, (1,)).astype(jnp.int32), mine, theirs)


def _chip_copies(ins, outs, ssem, rsem, mode):
    x, y, c = _place()
    q = 2 * x + y
    sends, recvs = [], []
    for k in range(len(ins)):
        for j, (cx, cy) in enumerate(_other_chips(x, y)):
            sem = (ssem.at[k, j], rsem.at[k, j], (cx, cy, c))
            if mode == "scatter":
                sends.append(_remote(ins[k].at[2 * cx + cy], outs[k].at[j], *sem))
                recvs.append(sends[-1])
            else:
                sends.append(_remote(ins[k].at[c], outs[k].at[2 * q + c], *sem))
                recvs.append(_remote(ins[k].at[c], outs[k].at[2 * (2 * cx + cy) + c], *sem))
    return sends, recvs


def _chip_wait(sends, recvs):
    for cp in sends:
        cp.wait_send()
    for cp in recvs:
        cp.wait_recv()


def _landing_shape(a, mode):
    return jax.ShapeDtypeStruct(((3,) if mode == "scatter" else (8,)) + a.shape[1:], a.dtype)


def _chip_exchange(arrs, mode, *, name):
    n = len(arrs)

    def body(*refs):
        ins, outs = refs[:n], refs[n:2 * n]
        ssem, rsem = refs[2 * n:]
        sends, recvs = _chip_copies(ins, outs, ssem, rsem, mode)
        for cp in sends:
            cp.start()
        _chip_wait(sends, recvs)

    sem = pltpu.SemaphoreType.DMA
    return list(pl.pallas_call(
        body, out_shape=tuple(_landing_shape(a, mode) for a in arrs),
        in_specs=[ANY] * n, out_specs=(ANY,) * n, scratch_shapes=[sem((n, 3)), sem((n, 3))], name=name)(*arrs))


def _pair_fill(bufs, owns, *, name):
    n = len(bufs)

    def body(*refs):
        own, outs = refs[n:2 * n], refs[2 * n:3 * n]
        ssem, rsem = refs[3 * n:]
        x, y, c = _place()
        q = 2 * x + y
        sib = (x, y, 1 - c)
        sends, recvs = [], []
        for k in range(n):
            for j, (cx, cy) in enumerate(_other_chips(x, y)):
                mine, theirs = outs[k].at[2 * (2 * cx + cy) + c], outs[k].at[2 * (2 * cx + cy) + 1 - c]
                sends.append(_remote(mine, mine, ssem.at[k, j], rsem.at[k, j], sib))
                recvs.append(_remote(mine, theirs, ssem.at[k, j], rsem.at[k, j], sib))
            slots = outs[k].at[pl.ds(2 * q, 2)]
            sends.append(_remote(own[k], slots, ssem.at[k, 3], rsem.at[k, 3], sib))
            recvs.append(sends[-1])
        for cp in sends:
            cp.start()
        _chip_wait(sends, recvs)

    sem = pltpu.SemaphoreType.DMA
    return list(pl.pallas_call(
        body, out_shape=tuple(jax.ShapeDtypeStruct(b.shape, b.dtype) for b in bufs),
        in_specs=[ANY] * (2 * n), out_specs=(ANY,) * n, scratch_shapes=[sem((n, 4)), sem((n, 4))],
        input_output_aliases={k: k for k in range(n)}, name=name)(*bufs, *owns))


def _pair_swap(arrs, *, name):
    n = len(arrs)

    def body(*refs):
        ins, outs = refs[:n], refs[n:2 * n]
        ssem, rsem = refs[2 * n:]
        x, y, c = _place()
        cps = [_remote(ins[k], outs[k], ssem.at[k], rsem.at[k], (x, y, 1 - c)) for k in range(n)]
        for cp in cps:
            cp.start()
        for cp in cps:
            cp.wait()

    sem = pltpu.SemaphoreType.DMA
    return pl.pallas_call(
        body, out_shape=tuple(jax.ShapeDtypeStruct(a.shape, a.dtype) for a in arrs),
        in_specs=[ANY] * n, out_specs=(ANY,) * n, scratch_shapes=[sem((n,)), sem((n,))], name=name)(*arrs)


def _allreduce_small(slab, *, name):
    r, n = slab.shape

    def body(x_ref, o_ref, buf, ssem, rsem):
        x, y, c = _place()
        me = 4 * x + 2 * y + c
        buf[me] = x_ref[...]
        cps = []
        for rel in range(1, 8):
            bx, by, bc = (rel >> 2) & 1, (rel >> 1) & 1, rel & 1
            px, py, pc = (x + bx) % 2, (y + by) % 2, (c + bc) % 2
            cps.append((_remote(x_ref, buf.at[me], ssem.at[rel - 1], rsem.at[rel - 1], (px, py, pc)),
                        4 * px + 2 * py + pc, (px, py, pc)))
        for cp, _, _ in cps:
            cp.start()
        for rel, (cp, peer, dev) in enumerate(cps):
            cp.wait_send()
            _remote(x_ref, buf.at[peer], ssem.at[rel], rsem.at[rel], dev).wait_recv()
        acc = buf[0]
        for k in range(1, 8):
            acc = acc + buf[k]
        o_ref[...] = acc

    vm = pl.BlockSpec(memory_space=pltpu.VMEM)
    sem = pltpu.SemaphoreType.DMA
    return pl.pallas_call(
        body, out_shape=jax.ShapeDtypeStruct((r, n), F32), in_specs=[vm], out_specs=vm,
        scratch_shapes=[pltpu.VMEM((8, r, n), F32), sem((7,)), sem((7,))], name=name)(slab)


def _slab(arrs, row_mult):
    flat = jnp.concatenate([a.reshape(-1) for a in arrs])
    unit = 128 * row_mult
    total = -(-flat.size // unit) * unit
    return jnp.pad(flat, (0, total - flat.size)).reshape(-1, 128)


def _unslab(slab, shapes):
    flat = slab.reshape(-1)
    out, off = [], 0
    for s in shapes:
        size = int(np.prod(s))
        out.append(flat[off:off + size].reshape(s))
        off += size
    return out


def _cols_from_chips(a):
    return jnp.transpose(a, (1, 0, 2)).reshape(a.shape[1], -1)


def _cols_to_chips(a, parts):
    r = a.shape[0]
    return jnp.transpose(a.reshape(r, parts, -1), (1, 0, 2))


BIG = ("w_in", "w_out", "up", "down")
GATHER_RIDES = {("proj", 0): (("w_out", 0), ("up", 0)), ("mix_out", 0): (("down", 0),),
                ("ffn_fwd", 0): (("w_in", 1), ("w_out", 1), ("up", 1), ("down", 1))}
REDUCE_RIDES = {("ffn_up_a_dw", 0): (("up",), 1), ("ffn_down_dw", 0): (("w_in", "w_out"), 1),
                ("mix_out_dx", 0): (("down",), 1),
                ("proj_dx", 0): (("up",), 0), ("proj_dw_0", 0): (("down",), 0), ("proj_dw_1", 0): (("w_out",), 0)}


class _LocalWeights:
    def __init__(self, meta, win, wout, up_a, up_g, down, w2p, cw):
        self._meta, self._w = meta, {"win": win, "wout": wout, "up_a": up_a, "up_g": up_g, "down": down, "w2p": w2p,
                                     "cw": cw}

    def meta(self):
        return self._meta

    def get(self, kind, l):
        return self._w[kind][l]

    def mm(self, site, l, a, b, fn=None, **kw):
        return (fn or _mm)(a, b, name=site, **kw)

    def ffn_fwd(self, l, hn, wa, wg, ba, bg):
        return _ffn_fwd(hn, self.get("up_a", l), self.get("up_g", l), wa, wg, ba, bg, self.get("down", l),
                        name="ffn_fwd")[0]

    def grads_done(self, l, g, kinds):
        pass


class _ChipWeights:
    def __init__(self, w_in, w_out, ffn_up, ffn_down, meta_tokens, gla_gate_w2, ffn_conv_w):
        self.x, self.y, self.c = _place()
        self.q = 2 * self.x + self.y
        halves = lambda a: a.astype(BF16).reshape(2, a.shape[0] // 2, a.shape[1])
        self.own = {(k, l): halves(a[l]) for k, a in zip(BIG, (w_in, w_out, ffn_up, ffn_down)) for l in range(DEPTH)}
        self.landed, self.swapped, self.full, self.n_swaps = {}, {}, {}, 0
        self.sh_shapes = [meta_tokens.shape, gla_gate_w2.shape, ffn_conv_w.shape]
        self.own["small", 0] = _slab([meta_tokens, gla_gate_w2, ffn_conv_w], 16).reshape(2, -1, 128)
        first = [("w_in", 0), ("small", 0)]
        for key, arr in zip(first, _chip_exchange([self.own[k] for k in first], "bcast", name="gather_first")):
            self.landed[key] = arr
        sh = self._whole("small", 0).reshape(4, -1, 128)
        parts = [_unslab(sh[k], self.sh_shapes) for k in range(4)]
        self._meta = jnp.concatenate([p[0] for p in parts], axis=-1)
        self.w2 = jnp.concatenate([p[1] for p in parts], axis=-1)
        self.cw = jnp.concatenate([p[2] for p in parts], axis=-1)
        self.partial, self.slots = {}, {}

    def _whole(self, kind, l):
        if (kind, l) not in self.full:
            keys = [k for k in self.landed if k not in self.full]
            got = _pair_fill([self.landed[k] for k in keys], [self.own[k] for k in keys],
                             name=f"gather_fill_{self.n_swaps}")
            self.n_swaps += 1
            for k, buf in zip(keys, got):
                self.full[k] = buf.reshape(4, 2 * buf.shape[1], buf.shape[2])
        return self.full[kind, l]

    def meta(self):
        return self._meta

    def get(self, kind, l):
        if kind == "win":
            return _to_kernel_cols(_cols_from_chips(self._whole("w_in", l)))
        if kind == "wout":
            return self._whole("w_out", l).reshape(D_MODEL, D_MODEL)
        if kind == "up_a":
            return _cols_from_chips(self._whole("up", l)[0:2])
        if kind == "up_g":
            return _cols_from_chips(self._whole("up", l)[2:4])
        if kind == "down":
            return self._whole("down", l).reshape(D_FF, D_MODEL)
        if kind == "w2p":
            return jnp.pad(self.w2[l], ((0, 128 - GLA_RANK), (0, 0))).astype(BF16)
        return self.cw[l]

    def mm(self, site, l, a, b, fn=None, **kw):
        fn = fn or _mm
        if (site, l) in GATHER_RIDES:
            keys = GATHER_RIDES[site, l]
            out, got = fn(a, b, name=site, carry=([self.own[k] for k in keys], "bcast"), **kw)
            self.landed.update(zip(keys, got))
            return out
        if (site, l) in REDUCE_RIDES:
            kinds, gl = REDUCE_RIDES[site, l]
            keys = [(k, gl) for k in kinds]
            if all(k in self.partial and k not in self.slots for k in keys):
                out, got = fn(a, b, name=site, carry=([self.partial[k] for k in keys], "scatter"), **kw)
                self.slots.update(zip(keys, got))
                return out
        return fn(a, b, name=site, **kw)

    def ffn_fwd(self, l, hn, wa, wg, ba, bg):
        keys = GATHER_RIDES.get(("ffn_fwd", l), ())
        outs, got = _ffn_fwd(hn, self.get("up_a", l), self.get("up_g", l), wa, wg, ba, bg, self.get("down", l),
                             name="ffn_fwd", carry=([self.own[k] for k in keys], "bcast") if keys else None)
        self.landed.update(zip(keys, got))
        return outs

    def grads_done(self, l, g, kinds):
        split = lambda a: a.reshape(4, 2, a.shape[-2] // 2, a.shape[-1]) if a.ndim == 3 else \
            a.reshape(4, 2, a.shape[0] // 8, a.shape[1])
        src = {"w_in": lambda: g["w_in"][l], "w_out": lambda: g["w_out"][l],
               "up": lambda: g["up"][l], "down": lambda: g["down"][l]}
        big = {k: split(src[k]()) for k in kinds}
        from_sib = _pair_exchange([big[k] for k in kinds], name=f"grads_pair_exchange_{l}_{kinds[0]}")
        for k, theirs in zip(kinds, from_sib):
            self.partial[k, l] = _pair_sum(big[k], theirs, self.c, name=f"pair_sum_{k}_{l}")

    def reduce(self):
        keys = [(k, l) for l in range(DEPTH) for k in BIG]
        late = [k for k in keys if k not in self.slots]
        self.slots.update(zip(late, _chip_exchange([self.partial[k] for k in late], "scatter",
                                                   name="grads_chip_exchange")))
        half = {}
        for k in keys:
            own = lax.dynamic_index_in_dim(self.partial[k], self.q, 0, keepdims=False)
            half[k] = _sum_slots(own, self.slots[k], name=f"chip_sum_{k[0]}_{k[1]}")
        other = dict(zip(keys, _pair_swap([half[k] for k in keys], name="grads_pair_swap")))
        return [([half[k, l] for l in range(DEPTH)], [other[k, l] for l in range(DEPTH)]) for k in BIG]


def _local_step(x_rows, target_rows, wts, pre_mix_norm, gla_gate_b, ret_norm_w, gla_norm_w, post_mix_norm,
                pre_ffn_norm, ffn_conv_b, post_ffn_norm):
    d = D_MODEL
    lp = x_rows.shape[0] + FRONT + BACK
    row = lambda a, l: a[l][None, :]
    rtab = _ret_tables(lp)
    gtab = _gla_tables()
    h0 = jnp.concatenate([jnp.zeros((PADF, d), F32), wts.meta(), x_rows, jnp.zeros((BACK, d), F32)], axis=0)
    target = jnp.pad(target_rows, ((FRONT, BACK), (0, 0)))

    saved = []
    h = h0
    _, hn = _resid_norm(h0, None, None, row(pre_mix_norm, 0), name="norm_in")
    loss_local = dy = None
    for l in range(DEPTH):
        s = {"h_in": h, "hn": hn}
        s["proj"] = wts.mm("proj", l, hn, wts.get("win", l))
        s["o_ret"], s["st_ret"] = _retention(s["proj"], rtab, name="retention")
        s["o_gla"], s["st_gla"] = _gla(s["proj"], wts.get("w2p", l), row(gla_gate_b, l), gtab, name="gla")
        s["merged"] = _merge(s["o_ret"], s["o_gla"], s["proj"], row(ret_norm_w, l), row(gla_norm_w, l), name="merge")
        s["m"] = wts.mm("mix_out", l, s["merged"], wts.get("wout", l))
        s["h_mid"], s["hn2"] = _resid_norm(h, s["m"], row(post_mix_norm, l), row(pre_ffn_norm, l), name="resid_mix")
        cw_a, cw_g = wts.get("cw", l)[:, :D_FF], wts.get("cw", l)[:, D_FF:]
        cb_a, cb_g = ffn_conv_b[l][None, :D_FF], ffn_conv_b[l][None, D_FF:]
        s["conv"] = (cw_a, cw_g, cb_a, cb_g)
        s["ua"], s["ug"], s["act"], s["f"] = wts.ffn_fwd(l, s["hn2"], cw_a, cw_g, cb_a, cb_g)
        if l + 1 < DEPTH:
            h, hn = _resid_norm(s["h_mid"], s["f"], row(post_ffn_norm, l), row(pre_mix_norm, l + 1), name="resid_ffn")
        else:
            loss_local, dy, df_last, dw_last = _loss_head(s["h_mid"], s["f"], row(post_ffn_norm, l), target,
                                                          name="loss_head")
        saved.append(s)

    g = {k: [None] * DEPTH for k in ("pre_mix", "w_in", "w2", "gb", "ret_n", "gla_n", "w_out", "post_mix", "pre_ffn",
                                     "up", "cw", "cb", "down", "post_ffn")}
    dh_out, dhn_next = dy, None
    for l in reversed(range(DEPTH)):
        s = saved[l]
        cw_a, cw_g, cb_a, cb_g = s["conv"]
        if l + 1 < DEPTH:
            dh, df, g["pre_mix"][l + 1], g["post_ffn"][l] = _resid_norm_bwd(
                dh_out, dhn_next, saved[l + 1]["h_in"], s["f"], row(pre_mix_norm, l + 1), row(post_ffn_norm, l),
                name="resid_ffn_bwd")
        else:
            dh, df, g["post_ffn"][l] = dh_out, df_last, dw_last
        g["down"][l] = wts.mm("ffn_down_dw", l, s["act"], df, fn=_mm_tn, tn=512)
        du_a, du_g, dcw_a, dcw_g, dcb_a, dcb_g, dhn2 = _conv_act_bwd(
            s["ua"], s["ug"], df, wts.get("down", l), cw_a, cw_g, cb_a, cb_g, wts.get("up_a", l), wts.get("up_g", l),
            name="conv_act_bwd")
        g["cw"][l] = jnp.concatenate([dcw_a, dcw_g], axis=1)
        g["cb"][l] = jnp.concatenate([dcb_a, dcb_g], axis=1)[0]
        half_up = wts.mm("ffn_up_a_dw", l, s["hn2"], du_a, fn=_mm_tn, tn=D_FF // 2, blocks=(4, 0))
        g["up"][l] = _mm_tn(s["hn2"], du_g, tn=D_FF // 2, blocks=(4, 2), into=half_up, name="ffn_up_g_dw")
        dh, dm, g["pre_ffn"][l], g["post_mix"][l] = _resid_norm_bwd(
            dh, dhn2, s["h_mid"], s["m"], row(pre_ffn_norm, l), row(post_mix_norm, l), name="resid_mix_bwd")
        g["w_out"][l] = _mm_tn(s["merged"], dm, name="mix_out_dw")
        wts.grads_done(l, g, ("w_out", "up", "down"))
        dmerged = wts.mm("mix_out_dx", l, dm, wts.get("wout", l), nt=True)
        do_ret, do_gla, d_gate, g["ret_n"][l], g["gla_n"][l] = _merge_bwd(
            dmerged, s["o_ret"], s["o_gla"], s["proj"], row(ret_norm_w, l), row(gla_norm_w, l), name="merge_bwd")
        d_ret = _retention_bwd(s["proj"], do_ret, s["st_ret"], rtab, name="retention_bwd")
        d_gla, dw2, dgb = _gla_bwd(s["proj"], do_gla, s["st_gla"], wts.get("w2p", l), row(gla_gate_b, l), gtab,
                                   name="gla_bwd")
        g["w2"][l], g["gb"][l] = dw2[:GLA_RANK], dgb[0]
        pieces = (d_ret, d_gate, d_gla)
        g["w_in"][l] = _to_reference_chips(*[wts.mm(f"proj_dw_{i}", l, s["hn"], p, fn=_mm_tn)
                                             for i, p in enumerate(pieces)])
        win = wts.get("win", l)
        dhn_next = wts.mm("proj_dx", l, pieces, [win[:, 0:P_RET], win[:, P_RET:P_RET + P_GATE], win[:, P_RET + P_GATE:]],
                          fn=_mm_nt_sum)
        dh_out = dh
        wts.grads_done(l, g, ("w_in",))
    dh0, _, g["pre_mix"][0], _ = _resid_norm_bwd(dh_out, dhn_next, h0, None, row(pre_mix_norm, 0), None,
                                                 name="norm_in_bwd")
    return loss_local, dh0, g


def kernel(x, meta_tokens, pre_mix_norm, w_in, gla_gate_w2, gla_gate_b, ret_norm_w, gla_norm_w, w_out, post_mix_norm, pre_ffn_norm, ffn_up, ffn_conv_w, ffn_conv_b, ffn_down, post_ffn_norm, loss_target, m_meta_tokens, m_pre_mix_norm, m_w_in, m_gla_gate_w2, m_gla_gate_b, m_ret_norm_w, m_gla_norm_w, m_w_out, m_post_mix_norm, m_pre_ffn_norm, m_ffn_up, m_ffn_conv_w, m_ffn_conv_b, m_ffn_down, m_post_ffn_norm, v_meta_tokens, v_pre_mix_norm, v_w_in, v_gla_gate_w2, v_gla_gate_b, v_ret_norm_w, v_gla_norm_w, v_w_out, v_post_mix_norm, v_pre_ffn_norm, v_ffn_up, v_ffn_conv_w, v_ffn_conv_b, v_ffn_down, v_post_ffn_norm):
    xi, yi, ci = _place()
    chip = 2 * xi + yi
    seq = x.shape[1]
    d = D_MODEL
    wts = _ChipWeights(w_in, w_out, ffn_up, ffn_down, meta_tokens, gla_gate_w2, ffn_conv_w)
    loss_local, dh0, g = _local_step(x[0], loss_target[0], wts, pre_mix_norm, gla_gate_b, ret_norm_w, gla_norm_w,
                                     post_mix_norm, pre_ffn_norm, ffn_conv_b, post_ffn_norm)
    grad_x = dh0[FRONT:FRONT + seq][None]
    names = ("w_in", "w_out", "ffn_up", "ffn_down")
    big_halves = wts.reduce()

    small_full = [dh0[PADF:FRONT], jnp.stack(g["pre_mix"])[:, 0], jnp.stack(g["w2"]), jnp.stack(g["gb"]),
                  jnp.stack(g["ret_n"])[:, 0], jnp.stack(g["gla_n"])[:, 0], jnp.stack(g["post_mix"])[:, 0],
                  jnp.stack(g["pre_ffn"])[:, 0], jnp.stack(g["cw"]), jnp.stack(g["cb"]),
                  jnp.stack(g["post_ffn"])[:, 0]]
    small_sum = _unslab(_allreduce_small(_slab(small_full, 8), name="small_allreduce"), [a.shape for a in small_full])
    (g_meta, g_pre_mix, g_w2, g_gb, g_ret_n, g_gla_n, g_post_mix, g_pre_ffn, g_cw, g_cb, g_post_ffn) = small_sum
    g_meta = lax.dynamic_slice_in_dim(g_meta, chip * 256, 256, axis=1)
    g_w2 = lax.dynamic_slice_in_dim(g_w2, chip * 64, 64, axis=2)
    g_cw = lax.dynamic_slice_in_dim(g_cw, chip * 1408, 1408, axis=2)

    grads = [g_meta, g_pre_mix, None, g_w2, g_gb, g_ret_n, g_gla_n, None, g_post_mix, g_pre_ffn, None,
             g_cw, g_cb, None, g_post_ffn]
    ws = [meta_tokens, pre_mix_norm, w_in, gla_gate_w2, gla_gate_b, ret_norm_w, gla_norm_w, w_out, post_mix_norm,
          pre_ffn_norm, ffn_up, ffn_conv_w, ffn_conv_b, ffn_down, post_ffn_norm]
    ms = [m_meta_tokens, m_pre_mix_norm, m_w_in, m_gla_gate_w2, m_gla_gate_b, m_ret_norm_w, m_gla_norm_w, m_w_out,
          m_post_mix_norm, m_pre_ffn_norm, m_ffn_up, m_ffn_conv_w, m_ffn_conv_b, m_ffn_down, m_post_ffn_norm]
    vs = [v_meta_tokens, v_pre_mix_norm, v_w_in, v_gla_gate_w2, v_gla_gate_b, v_ret_norm_w, v_gla_norm_w, v_w_out,
          v_post_mix_norm, v_pre_ffn_norm, v_ffn_up, v_ffn_conv_w, v_ffn_conv_b, v_ffn_down, v_post_ffn_norm]
    big_idx = (2, 7, 10, 13)
    deltas, new_m, new_v = [None] * 15, [None] * 15, [None] * 15
    for i, nm, (mine, theirs) in zip(big_idx, names, big_halves):
        grads[i], deltas[i], new_m[i], new_v[i] = _adamw_halves(ws[i], ms[i], vs[i], mine, theirs, ci,
                                                                name=f"adamw_{nm}")
    small_idx = [i for i in range(15) if i not in big_idx]
    shapes = [ws[i].shape for i in small_idx]
    sd, sm, sv = _adamw(_slab([ws[i] for i in small_idx], 8), _slab([grads[i] for i in small_idx], 8),
                        _slab([ms[i] for i in small_idx], 8), _slab([vs[i] for i in small_idx], 8), name="adamw_small")
    for i, a, b, c_ in zip(small_idx, _unslab(sd, shapes), _unslab(sm, shapes), _unslab(sv, shapes)):
        deltas[i], new_m[i], new_v[i] = a, b, c_

    loss = lax.psum(loss_local, ("x", "y", "c"))
    return (loss, grad_x, *grads, *deltas, *new_m, *new_v)
```

```python
import functools
import math

import numpy as np
import jax
import jax.numpy as jnp
from jax import lax
from jax.experimental import pallas as pl
from jax.experimental.pallas import tpu as pltpu

F32 = jnp.float32
BF16 = jnp.bfloat16

D_MODEL = 1024
DEPTH = 2
N_META = 16
EPS = 1e-6
RET_HEADS = 4
RET_DK = 128
GLA_HEADS = 4
GLA_DK = 64
GLA_DV = 128
GLA_QK = GLA_HEADS * GLA_DK
GLA_V = GLA_HEADS * GLA_DV
GLA_RANK = 16
GLA_TAU = 16.0
D_FF = 2816
ROPE_BASE = 10000.0
IN_WIDTH = 3600
IN_PAD = 3840
C_RQ, C_RK, C_RV, C_RG, C_GR, C_GQ, C_GK, C_GV, C_GA = 0, 512, 1024, 1536, 2048, 2560, 2816, 3072, 3584
P_RET, P_GATE, P_GLA = 1536, 1024, 1280


def _to_kernel_cols(w):
    pad = jnp.zeros(w.shape[:-1] + (IN_PAD - IN_WIDTH,), w.dtype)
    return jnp.concatenate([w[..., 0:2048], w[..., 3072:3584], w[..., 2048:3072], w[..., 3584:3600], pad], axis=-1)


def _to_reference_chips(d_ret, d_gate, d_gla):
    segs = [(d_ret, 0, 0, 1536), (d_gate, 0, 1536, 512), (d_gla, 0, 2048, 1024), (d_gate, 512, 3072, 512),
            (d_gla, 1024, 3584, GLA_RANK)]
    per = IN_WIDTH // 4
    chips = []
    for j in range(4):
        lo, hi, parts = per * j, per * (j + 1), []
        for piece, p0, r0, width in segs:
            a, b = max(lo, r0), min(hi, r0 + width)
            if a < b:
                parts.append(piece[:, p0 + a - r0:p0 + b - r0])
        chips.append(jnp.concatenate(parts, axis=1))
    return jnp.stack(chips)

FRONT = 64
BACK = 64
PADF = FRONT - N_META
RET_CHUNK = 128
GLA_CHUNK = 64
GLA_SUB = 16
GLA_SUB2 = 4
BLK = 640

ADAM_LR, ADAM_B1, ADAM_B2, ADAM_EPS, ADAM_WD, ADAM_STEP = 0.001, 0.9, 0.999, 1e-08, 0.01, 10

VMEM_LIMIT = 56 * 2 ** 20
MM_VMEM_BUDGET = 40 * 2 ** 20
MESH = pl.DeviceIdType.MESH


def _cp(*sem):
    return pltpu.CompilerParams(dimension_semantics=sem, vmem_limit_bytes=VMEM_LIMIT)


def _tile(n, cands):
    for t in cands:
        if n % t == 0:
            return t
    raise ValueError(f"no tile for {n} in {cands}")


def _row_tile(n):
    return _tile(n, (640, 512, 320, 256, 128, 64))


def _mm(a, b, *, nt=False, add=None, out_dtype=F32, tn=None, name, carry=None):
    m, k = a.shape
    n = b.shape[0] if nt else b.shape[1]
    tm = _tile(m, (640, 320, 256, 128, 64))
    if tn is None:
        step_bytes = lambda t: 2 * (tm * k * a.dtype.itemsize + t * k * b.dtype.itemsize
                                    + tm * t * (jnp.dtype(out_dtype).itemsize + (4 if add is not None else 0)))
        tn = next(t for t in range(n, 0, -128) if n % t == 0 and (step_bytes(t) <= MM_VMEM_BUDGET or t == 128))
    dn = (((1,), (1,)), ((), ())) if nt else (((1,), (0,)), ((), ()))
    nj, ni = n // tn, m // tm
    n_in = 2 + (add is not None)
    c_arrs, c_mode = carry if carry is not None else ((), None)
    nc = len(c_arrs)

    def body(*refs):
        a_ref, b_ref = refs[:2]
        c_ref = refs[2] if add is not None else None
        o_ref = refs[n_in + nc]
        if nc:
            c_ins, c_outs = refs[n_in:n_in + nc], refs[n_in + nc + 1:n_in + 2 * nc + 1]
            ssem, rsem = refs[n_in + 2 * nc + 1:]
            j, i = pl.program_id(0), pl.program_id(1)

            @pl.when((j == 0) & (i == 0))
            def _():
                for cp in _chip_copies(c_ins, c_outs, ssem, rsem, c_mode)[0]:
                    cp.start()
        r = lax.dot_general(a_ref[...].astype(BF16), b_ref[...].astype(BF16), dn, preferred_element_type=F32)
        if add is not None:
            r = r + c_ref[...]
        o_ref[...] = r.astype(o_ref.dtype)
        if nc:
            @pl.when((j == nj - 1) & (i == ni - 1))
            def _():
                _chip_wait(*_chip_copies(c_ins, c_outs, ssem, rsem, c_mode))

    b_spec = pl.BlockSpec((tn, k), lambda j, i: (j, 0)) if nt else pl.BlockSpec((k, tn), lambda j, i: (0, j))
    in_specs = [pl.BlockSpec((tm, k), lambda j, i: (i, 0)), b_spec]
    args = [a, b]
    if add is not None:
        in_specs.append(pl.BlockSpec((tm, tn), lambda j, i: (i, j)))
        args.append(add)
    out_shape = jax.ShapeDtypeStruct((m, n), out_dtype)
    out_spec = pl.BlockSpec((tm, tn), lambda j, i: (i, j))
    if not nc:
        return pl.pallas_call(
            body, out_shape=out_shape, grid=(nj, ni), in_specs=in_specs, out_specs=out_spec,
            compiler_params=_cp("parallel", "parallel"), name=name)(*args)
    sem = pltpu.SemaphoreType.DMA
    outs = pl.pallas_call(
        body, out_shape=(out_shape,) + tuple(_landing_shape(x, c_mode) for x in c_arrs), grid=(nj, ni),
        in_specs=in_specs + [ANY] * nc, out_specs=(out_spec,) + (ANY,) * nc,
        scratch_shapes=[sem((nc, 3)), sem((nc, 3))],
        compiler_params=_cp("arbitrary", "arbitrary"), name=name)(*args, *c_arrs)
    return outs[0], list(outs[1:])


def _call_with_carry(body, *, out_shape, grid, in_specs, out_specs, args, semantics, carry, name, aliases=None):
    if carry is None:
        return pl.pallas_call(body, out_shape=out_shape, grid=grid, in_specs=in_specs, out_specs=out_specs,
                              input_output_aliases=aliases or {}, compiler_params=_cp(*semantics), name=name)(*args)
    c_arrs, c_mode = carry
    n_in, nc = len(args), len(c_arrs)

    def carried(*refs):
        c_ins, c_outs = refs[n_in:n_in + nc], refs[n_in + nc + 1:n_in + 2 * nc + 1]
        ssem, rsem = refs[n_in + 2 * nc + 1:]
        ids = [pl.program_id(d) for d in range(len(grid))]
        first = functools.reduce(lambda u, v: u & v, [i == 0 for i in ids])
        last = functools.reduce(lambda u, v: u & v, [i == g - 1 for i, g in zip(ids, grid)])

        @pl.when(first)
        def _():
            for cp in _chip_copies(c_ins, c_outs, ssem, rsem, c_mode)[0]:
                cp.start()
        body(*refs[:n_in], refs[n_in + nc])

        @pl.when(last)
        def _():
            _chip_wait(*_chip_copies(c_ins, c_outs, ssem, rsem, c_mode))

    sem = pltpu.SemaphoreType.DMA
    outs = pl.pallas_call(
        carried, out_shape=(out_shape,) + tuple(_landing_shape(x, c_mode) for x in c_arrs), grid=grid,
        in_specs=list(in_specs) + [ANY] * nc, out_specs=(out_specs,) + (ANY,) * nc,
        scratch_shapes=[sem((nc, 3)), sem((nc, 3))], input_output_aliases=aliases or {},
        compiler_params=_cp(*(("arbitrary",) * len(grid))), name=name)(*args, *c_arrs)
    return outs[0], list(outs[1:])


def _mm_nt_sum(a_list, b_list, *, name, carry=None):
    m, n = a_list[0].shape[0], b_list[0].shape[0]
    tm = _tile(m, (640, 320, 256, 128, 64))
    np_ = len(a_list)

    def body(*refs):
        acc = None
        for a_ref, b_ref in zip(refs[:np_], refs[np_:2 * np_]):
            r = lax.dot_general(a_ref[...].astype(BF16), b_ref[...].astype(BF16), (((1,), (1,)), ((), ())),
                                preferred_element_type=F32)
            acc = r if acc is None else acc + r
        refs[2 * np_][...] = acc

    return _call_with_carry(
        body, out_shape=jax.ShapeDtypeStruct((m, n), F32), grid=(m // tm,),
        in_specs=[pl.BlockSpec((tm, a.shape[1]), lambda i: (i, 0)) for a in a_list]
        + [pl.BlockSpec(b.shape, lambda i: (0, 0)) for b in b_list],
        out_specs=pl.BlockSpec((tm, n), lambda i: (i, 0)), args=[*a_list, *b_list], semantics=("parallel",),
        carry=carry, name=name)


def _mm_tn(a, b, *, tn=None, blocks=None, into=None, name, carry=None):
    m, k = a.shape
    n = b.shape[1]
    tm = _tile(m, (1664, 640, 320, 256, 128, 64))
    tn = n if tn is None else tn
    if blocks is not None:
        total, first = blocks
        out_shape = jax.ShapeDtypeStruct((total, k, tn), F32)
        out_spec = pl.BlockSpec((None, k, tn), lambda j, i: (first + j, 0, 0))
    else:
        out_shape = jax.ShapeDtypeStruct((k, n), F32)
        out_spec = pl.BlockSpec((k, tn), lambda j, i: (0, j))

    def body(a_ref, b_ref, *rest):
        o_ref = rest[-1]

        @pl.when(pl.program_id(1) == 0)
        def _():
            o_ref[...] = jnp.zeros_like(o_ref)
        o_ref[...] += lax.dot_general(a_ref[...].astype(BF16), b_ref[...].astype(BF16),
                                      (((0,), (0,)), ((), ())), preferred_element_type=F32)

    in_specs = [pl.BlockSpec((tm, k), lambda j, i: (i, 0)), pl.BlockSpec((tm, tn), lambda j, i: (i, j))]
    args, alias = [a, b], {}
    if into is not None:
        in_specs.append(pl.BlockSpec(memory_space=pl.ANY))
        args.append(into)
        alias = {2: 0}
    return _call_with_carry(body, out_shape=out_shape, grid=(n // tn, m // tm), in_specs=in_specs, out_specs=out_spec,
                            args=args, semantics=("parallel", "arbitrary"), carry=carry, name=name, aliases=alias)


def _rms(x, w):
    r = lax.rsqrt(jnp.mean(x * x, axis=-1, keepdims=True) + EPS)
    return x * r * w


def _rms_bwd(x, w, dy):
    r = lax.rsqrt(jnp.mean(x * x, axis=-1, keepdims=True) + EPS)
    xh = x * r
    dxh = dy * w
    dx = r * (dxh - xh * jnp.mean(dxh * xh, axis=-1, keepdims=True))
    return dx, jnp.sum(dy * xh, axis=0, keepdims=True)


def _resid_norm(h, t, w_post, w_next, *, name):
    lp, d = h.shape
    tm = _row_tile(lp)
    has_t = t is not None

    def body(*refs):
        if has_t:
            h_ref, t_ref, wp_ref, wn_ref, ho_ref, hn_ref = refs
            hv = h_ref[...] + _rms(t_ref[...], wp_ref[...])
            ho_ref[...] = hv
        else:
            h_ref, wn_ref, hn_ref = refs
            hv = h_ref[...]
        hn_ref[...] = _rms(hv, wn_ref[...]).astype(BF16)

    row = pl.BlockSpec((tm, d), lambda i: (i, 0))
    vec = pl.BlockSpec((1, d), lambda i: (0, 0))
    if has_t:
        return pl.pallas_call(
            body, out_shape=(jax.ShapeDtypeStruct((lp, d), F32), jax.ShapeDtypeStruct((lp, d), BF16)),
            grid=(lp // tm,), in_specs=[row, row, vec, vec], out_specs=(row, row),
            compiler_params=_cp("parallel"), name=name)(h, t, w_post, w_next)
    return h, pl.pallas_call(
        body, out_shape=jax.ShapeDtypeStruct((lp, d), BF16), grid=(lp // tm,), in_specs=[row, vec],
        out_specs=row, compiler_params=_cp("parallel"), name=name)(h, w_next)


def _resid_norm_bwd(dh_out, dhn, h_new, t, w_next, w_post, *, name):
    lp, d = h_new.shape if h_new is not None else t.shape
    tm = _row_tile(lp)
    has_n = dhn is not None
    has_t = t is not None

    def body(*refs):
        refs = list(refs)
        dho_ref = refs.pop(0)
        if has_n:
            dhn_ref, hn_ref, wn_ref = refs.pop(0), refs.pop(0), refs.pop(0)
        if has_t:
            t_ref, wp_ref = refs.pop(0), refs.pop(0)
        dh_ref = refs.pop(0) if has_n else None
        dt_ref = refs.pop(0) if has_t else None
        dwn_ref = refs.pop(0) if has_n else None
        dwp_ref = refs.pop(0) if has_t else None
        first = pl.program_id(0) == 0
        dh = dho_ref[...]
        if has_n:
            dx, dwn = _rms_bwd(hn_ref[...], wn_ref[...], dhn_ref[...])
            dh = dh + dx
            dh_ref[...] = dh

            @pl.when(first)
            def _():
                dwn_ref[...] = jnp.zeros_like(dwn_ref)
            dwn_ref[...] += dwn
        if has_t:
            dt, dwp = _rms_bwd(t_ref[...], wp_ref[...], dh)
            dt_ref[...] = dt.astype(BF16)

            @pl.when(first)
            def _():
                dwp_ref[...] = jnp.zeros_like(dwp_ref)
            dwp_ref[...] += dwp

    row = pl.BlockSpec((tm, d), lambda i: (i, 0))
    vec = pl.BlockSpec((1, d), lambda i: (0, 0))
    args, in_specs, out_shape, out_specs = [dh_out], [row], [], []
    if has_n:
        args += [dhn, h_new, w_next]
        in_specs += [row, row, vec]
    if has_t:
        args += [t, w_post]
        in_specs += [row, vec]
    if has_n:
        out_shape.append(jax.ShapeDtypeStruct((lp, d), F32)); out_specs.append(row)
    if has_t:
        out_shape.append(jax.ShapeDtypeStruct((lp, d), BF16)); out_specs.append(row)
    if has_n:
        out_shape.append(jax.ShapeDtypeStruct((1, d), F32)); out_specs.append(vec)
    if has_t:
        out_shape.append(jax.ShapeDtypeStruct((1, d), F32)); out_specs.append(vec)
    outs = list(pl.pallas_call(body, out_shape=tuple(out_shape), grid=(lp // tm,), in_specs=in_specs,
                               out_specs=tuple(out_specs), compiler_params=_cp("arbitrary"), name=name)(*args))
    dh = outs.pop(0) if has_n else dh_out
    dt = outs.pop(0) if has_t else None
    dwn = outs.pop(0) if has_n else None
    dwp = outs.pop(0) if has_t else None
    return dh, dt, dwn, dwp


def _loss_head(h, f, w_post, target, *, name):
    lp, d = h.shape
    tm = _row_tile(lp)

    def body(h_ref, f_ref, w_ref, t_ref, loss_ref, dy_ref, df_ref, dw_ref):
        i = pl.program_id(0)
        f, w = f_ref[...], w_ref[...]
        y = h_ref[...] + _rms(f, w)
        rows = i * tm + lax.broadcasted_iota(jnp.int32, (tm, 1), 0)
        tok = (rows >= FRONT) & (rows < lp - BACK)
        err = jnp.where(tok, y - t_ref[...], 0.0)
        dy = err * (1.0 / d)
        dy_ref[...] = dy
        df, dw = _rms_bwd(f, w, dy)
        df_ref[...] = df.astype(BF16)

        @pl.when(i == 0)
        def _():
            loss_ref[...] = jnp.zeros_like(loss_ref)
            dw_ref[...] = jnp.zeros_like(dw_ref)
        part = jnp.sum(jnp.sum(err * err, axis=1, keepdims=True), axis=0, keepdims=True) * (0.5 / d)
        loss_ref[...] += jnp.broadcast_to(part, loss_ref.shape)
        dw_ref[...] += dw

    row = pl.BlockSpec((tm, d), lambda i: (i, 0))
    vec = pl.BlockSpec((1, d), lambda i: (0, 0))
    loss, dy, df, dw = pl.pallas_call(
        body, out_shape=(jax.ShapeDtypeStruct((8, 128), F32), jax.ShapeDtypeStruct((lp, d), F32),
                         jax.ShapeDtypeStruct((lp, d), BF16), jax.ShapeDtypeStruct((1, d), F32)),
        grid=(lp // tm,), in_specs=[row, row, vec, row],
        out_specs=(pl.BlockSpec((8, 128), lambda i: (0, 0)), row, row, vec),
        compiler_params=_cp("arbitrary"), name=name)(h, f, w_post, target)
    return loss[0, 0], dy, df, dw


_GELU_C = math.sqrt(2.0 / math.pi)


def _gelu_and_grad(a):
    a2 = a * a
    t = jnp.tanh(a * (_GELU_C + (_GELU_C * 0.044715) * a2))
    ha = 0.5 * a
    h1 = 0.5 + 0.5 * t
    return a * h1, h1 + ha * (1.0 - t * t) * (_GELU_C + (3.0 * _GELU_C * 0.044715) * a2)


def _gelu(a):
    t = jnp.tanh(a * (_GELU_C + (_GELU_C * 0.044715) * (a * a)))
    return a * (0.5 + 0.5 * t)


def _conv3(parts, n, w, b):
    xx = jnp.concatenate(parts, axis=0)
    return b + xx[8:8 + n] * w[2:3] + pltpu.roll(xx, 1, 0)[8:8 + n] * w[1:2] + pltpu.roll(xx, 2, 0)[8:8 + n] * w[0:1]


def _conv_act(ua, ug, wa, wg, ba, bg, *, name):
    lp, n = ua.shape
    tm = _row_tile(lp)
    tc = _tile(n, (256, 128))
    nb8 = tm // 8

    def body(ua_ref, uap_ref, ug_ref, ugp_ref, wa_ref, wg_ref, ba_ref, bg_ref, o_ref):
        i = pl.program_id(0)
        ca = _conv3([uap_ref[...], ua_ref[...]], tm, wa_ref[...], ba_ref[...])
        cg = _conv3([ugp_ref[...], ug_ref[...]], tm, wg_ref[...], bg_ref[...])
        rows = i * tm + lax.broadcasted_iota(jnp.int32, (tm, 1), 0)
        ok = (rows >= PADF) & (rows < lp - BACK)
        o_ref[...] = jnp.where(ok, _gelu(ca) * cg, 0.0).astype(BF16)

    cur = pl.BlockSpec((tm, tc), lambda i, j: (i, j))
    prev = pl.BlockSpec((8, tc), lambda i, j: (jnp.maximum(i * nb8 - 1, 0), j))
    w3 = pl.BlockSpec((3, tc), lambda i, j: (0, j))
    b1 = pl.BlockSpec((1, tc), lambda i, j: (0, j))
    return pl.pallas_call(
        body, out_shape=jax.ShapeDtypeStruct((lp, n), BF16), grid=(lp // tm, n // tc),
        in_specs=[cur, prev, cur, prev, w3, w3, b1, b1], out_specs=cur,
        compiler_params=_cp("parallel", "parallel"), name=name)(ua, ua, ug, ug, wa, wg, ba, bg)


def _conv_act_down(ua, ug, wa, wg, ba, bg, down, *, name):
    lp, n = ua.shape
    d = down.shape[1]
    tm = _tile(lp, (320, 256, 128, 64))
    tc = _tile(n, (256, 128))
    nb8 = tm // 8

    def body(ua_ref, uap_ref, ug_ref, ugp_ref, wa_ref, wg_ref, ba_ref, bg_ref, dn_ref, act_ref, f_ref):
        i = pl.program_id(0)
        rows = i * tm + lax.broadcasted_iota(jnp.int32, (tm, 1), 0)
        ok = (rows >= PADF) & (rows < lp - BACK)
        acc = None
        for j in range(n // tc):
            cs = slice(tc * j, tc * j + tc)
            ca = _conv3([uap_ref[:, cs], ua_ref[:, cs]], tm, wa_ref[:, cs], ba_ref[:, cs])
            cg = _conv3([ugp_ref[:, cs], ug_ref[:, cs]], tm, wg_ref[:, cs], bg_ref[:, cs])
            act = jnp.where(ok, _gelu(ca) * cg, 0.0).astype(BF16)
            act_ref[:, cs] = act
            part = _dot(act, dn_ref[cs, :])
            acc = part if acc is None else acc + part
        f_ref[...] = acc

    cur = pl.BlockSpec((tm, n), lambda i: (i, 0))
    prev = pl.BlockSpec((8, n), lambda i: (jnp.maximum(i * nb8 - 1, 0), 0))
    w3 = pl.BlockSpec((3, n), lambda i: (0, 0))
    b1 = pl.BlockSpec((1, n), lambda i: (0, 0))
    return pl.pallas_call(
        body, out_shape=(jax.ShapeDtypeStruct((lp, n), BF16), jax.ShapeDtypeStruct((lp, d), F32)),
        grid=(lp // tm,),
        in_specs=[cur, prev, cur, prev, w3, w3, b1, b1, pl.BlockSpec(down.shape, lambda i: (0, 0))],
        out_specs=(cur, pl.BlockSpec((tm, d), lambda i: (i, 0))),
        compiler_params=_cp("parallel"), name=name)(ua, ua, ug, ug, wa, wg, ba, bg, down)


def _ffn_fwd(hn, up_a, up_g, wa, wg, ba, bg, down, *, name, carry=None):
    lp, d = hn.shape
    n = up_a.shape[1]
    tm = _tile(lp, (320, 256, 128, 64))
    tc = _tile(n, (256, 128))
    nchunks = n // tc
    c_arrs, c_mode = carry if carry is not None else ((), None)
    nc = len(c_arrs)
    steps = lp // tm

    def body(hn_ref, hnp_ref, upa_ref, upg_ref, wa_ref, wg_ref, ba_ref, bg_ref, dn_ref, *rest):
        c_ins = rest[:nc]
        ua_ref, ug_ref, act_ref, f_ref = rest[nc:nc + 4]
        c_outs = rest[nc + 4:2 * nc + 4]
        i = pl.program_id(0)
        if nc:
            ssem, rsem = rest[2 * nc + 4:]

            @pl.when(i == 0)
            def _():
                for cp in _chip_copies(c_ins, c_outs, ssem, rsem, c_mode)[0]:
                    cp.start()
        rows = i * tm + lax.broadcasted_iota(jnp.int32, (tm, 1), 0)
        ok = (rows >= PADF) & (rows < lp - BACK)
        x = jnp.concatenate([hnp_ref[...], hn_ref[...]], axis=0)
        u_of = lambda j: (_dot(x, upa_ref[:, tc * j:tc * j + tc]), _dot(x, upg_ref[:, tc * j:tc * j + tc]))
        u_next = u_of(0)
        acc = None
        for j in range(nchunks):
            cs = slice(tc * j, tc * j + tc)
            ua, ug = u_next
            if j + 1 < nchunks:
                u_next = u_of(j + 1)
            ua_ref[:, cs] = ua[16:]
            ug_ref[:, cs] = ug[16:]
            ca = _conv3([ua[8:]], tm, wa_ref[:, cs], ba_ref[:, cs])
            cg = _conv3([ug[8:]], tm, wg_ref[:, cs], bg_ref[:, cs])
            act = jnp.where(ok, _gelu(ca) * cg, 0.0).astype(BF16)
            act_ref[:, cs] = act
            part = _dot(act, dn_ref[cs, :])
            acc = part if acc is None else acc + part
        f_ref[...] = acc
        if nc:
            @pl.when(i == steps - 1)
            def _():
                _chip_wait(*_chip_copies(c_ins, c_outs, ssem, rsem, c_mode))

    whole = pl.BlockSpec(memory_space=pltpu.VMEM)
    wide = pl.BlockSpec((tm, n), lambda i: (i, 0))
    w3 = pl.BlockSpec((3, n), lambda i: (0, 0))
    b1 = pl.BlockSpec((1, n), lambda i: (0, 0))
    sem = pltpu.SemaphoreType.DMA
    outs = pl.pallas_call(
        body,
        out_shape=(jax.ShapeDtypeStruct((lp, n), F32), jax.ShapeDtypeStruct((lp, n), F32),
                   jax.ShapeDtypeStruct((lp, n), BF16), jax.ShapeDtypeStruct((lp, d), F32))
        + tuple(_landing_shape(a, c_mode) for a in c_arrs),
        grid=(steps,),
        in_specs=[pl.BlockSpec((tm, d), lambda i: (i, 0)),
                  pl.BlockSpec((16, d), lambda i: (jnp.maximum(i * (tm // 16) - 1, 0), 0)),
                  whole, whole, w3, w3, b1, b1, whole] + [ANY] * nc,
        out_specs=(wide, wide, wide, pl.BlockSpec((tm, d), lambda i: (i, 0))) + (ANY,) * nc,
        scratch_shapes=[sem((nc, 3)), sem((nc, 3))] if nc else [],
        compiler_params=_cp("arbitrary"), name=name)(hn, hn, up_a, up_g, wa, wg, ba, bg, down, *c_arrs)
    return outs[:4], list(outs[4:])


def _conv_act_bwd(ua, ug, df, down, wa, wg, ba, bg, up_a, up_g, *, name):
    lp, n = ua.shape
    d = up_a.shape[0]
    tm = _tile(lp, (320, 256, 128, 64))
    tc = _tile(n, (256, 128))
    nb8 = tm // 8
    last8 = lp // 8 - 1
    last16 = lp // 16 - 1
    ext = tm + 8

    def body(ua_ref, uap_ref, uan_ref, ug_ref, ugp_ref, ugn_ref, df_ref, dfn_ref, dn_ref, wa_ref, wg_ref, ba_ref,
             bg_ref, upa_ref, upg_ref, dua_ref, dug_ref, dwa_ref, dwg_ref, dba_ref, dbg_ref, dhn_ref):
        i = pl.program_id(0)
        df_ext = jnp.concatenate([df_ref[...], dfn_ref[...]], axis=0)

        @pl.when(i == 0)
        def _():
            dwa_ref[...] = jnp.zeros_like(dwa_ref)
            dwg_ref[...] = jnp.zeros_like(dwg_ref)
            dba_ref[...] = jnp.zeros_like(dba_ref)
            dbg_ref[...] = jnp.zeros_like(dbg_ref)
        rows = i * tm + lax.broadcasted_iota(jnp.int32, (ext, 1), 0)
        ok = (rows >= PADF) & (rows < lp - BACK)

        def conv(parts, w, b):
            xx = jnp.concatenate(parts, axis=0)
            x, x1, x2 = xx[8:8 + ext], pltpu.roll(xx, 1, 0)[8:8 + ext], pltpu.roll(xx, 2, 0)[8:8 + ext]
            return b + x * w[2:3] + x1 * w[1:2] + x2 * w[0:1], x, x1, x2

        def back(dc, w):
            return (dc[:tm] * w[2:3] + pltpu.roll(dc, ext - 1, 0)[:tm] * w[1:2]
                    + pltpu.roll(dc, ext - 2, 0)[:tm] * w[0:1])

        def wsum(dw_ref, db_ref, cs, dc, x, x1, x2):
            dd = dc[:tm]
            s = lambda v: jnp.sum(v, axis=0, keepdims=True)
            dw_ref[0:1, cs] += s(dd * x2[:tm])
            dw_ref[1:2, cs] += s(dd * x1[:tm])
            dw_ref[2:3, cs] += s(dd * x[:tm])
            db_ref[:, cs] += s(dd)

        acc = None
        nchunks = n // tc
        dact_of = lambda j: _dot_nt(df_ext, dn_ref[tc * j:tc * j + tc, :])[:ext]
        dact_next = dact_of(0)
        for j in range(nchunks):
            cs = slice(tc * j, tc * j + tc)
            dact_cur = dact_next
            if j + 1 < nchunks:
                dact_next = dact_of(j + 1)
            wa, wg = wa_ref[:, cs], wg_ref[:, cs]
            ca, xa, xa1, xa2 = conv([uap_ref[:, cs], ua_ref[:, cs], uan_ref[:, cs]], wa, ba_ref[:, cs])
            cg, xg, xg1, xg2 = conv([ugp_ref[:, cs], ug_ref[:, cs], ugn_ref[:, cs]], wg, bg_ref[:, cs])
            dact_e = jnp.where(ok, dact_cur, 0.0)
            gel, gel_d = _gelu_and_grad(ca)
            dca = dact_e * cg * gel_d
            dcg = dact_e * gel
            du_a, du_g = back(dca, wa).astype(BF16), back(dcg, wg).astype(BF16)
            dua_ref[:, cs] = du_a
            dug_ref[:, cs] = du_g
            wsum(dwa_ref, dba_ref, cs, dca, xa, xa1, xa2)
            wsum(dwg_ref, dbg_ref, cs, dcg, xg, xg1, xg2)
            part = _dot_nt(du_a, upa_ref[:, cs]) + _dot_nt(du_g, upg_ref[:, cs])
            acc = part if acc is None else acc + part
        dhn_ref[...] = acc

    cur = pl.BlockSpec((tm, n), lambda i: (i, 0))
    prev = pl.BlockSpec((8, n), lambda i: (jnp.maximum(i * nb8 - 1, 0), 0))
    nxt = pl.BlockSpec((8, n), lambda i: (jnp.minimum((i + 1) * nb8, last8), 0))
    w3 = pl.BlockSpec((3, n), lambda i: (0, 0))
    b1 = pl.BlockSpec((1, n), lambda i: (0, 0))
    whole = pl.BlockSpec(memory_space=pltpu.VMEM)
    return pl.pallas_call(
        body,
        out_shape=(jax.ShapeDtypeStruct((lp, n), BF16), jax.ShapeDtypeStruct((lp, n), BF16),
                   jax.ShapeDtypeStruct((3, n), F32), jax.ShapeDtypeStruct((3, n), F32),
                   jax.ShapeDtypeStruct((1, n), F32), jax.ShapeDtypeStruct((1, n), F32),
                   jax.ShapeDtypeStruct((lp, d), F32)),
        grid=(lp // tm,),
        in_specs=[cur, prev, nxt, cur, prev, nxt, pl.BlockSpec((tm, d), lambda i: (i, 0)),
                  pl.BlockSpec((16, d), lambda i: (jnp.minimum((i + 1) * (tm // 16), last16), 0)), whole,
                  w3, w3, b1, b1, whole, whole],
        out_specs=(cur, cur, w3, w3, b1, b1, pl.BlockSpec((tm, d), lambda i: (i, 0))),
        compiler_params=_cp("arbitrary"), name=name)(ua, ua, ua, ug, ug, ug, df, df, down, wa, wg, ba, bg, up_a, up_g)


def _sigmoid(x):
    return 1.0 / (1.0 + jnp.exp(-x))


def _merge(o_ret, o_gla, proj, w_ret, w_gla, *, name):
    lp = o_ret.shape[0]
    tm = _row_tile(lp)

    def body(or_ref, og_ref, rg_ref, gr_ref, wr_ref, wg_ref, m_ref):
        oret, ogla = or_ref[...], og_ref[...]
        yr, yg = [], []
        for h in range(4):
            hs = slice(128 * h, 128 * h + 128)
            o = oret[:, hs]
            xc = o - jnp.mean(o, axis=-1, keepdims=True)
            yr.append(xc * lax.rsqrt(jnp.mean(xc * xc, axis=-1, keepdims=True) + EPS))
            o = ogla[:, hs]
            yg.append(o * lax.rsqrt(jnp.mean(o * o, axis=-1, keepdims=True) + EPS))
        rg, gr = rg_ref[...], gr_ref[...]
        m_ref[:, 0:512] = (jnp.concatenate(yr, axis=1) * wr_ref[...] * (rg * _sigmoid(rg))).astype(BF16)
        m_ref[:, 512:1024] = (jnp.concatenate(yg, axis=1) * wg_ref[...] * (gr * _sigmoid(gr))).astype(BF16)

    row = pl.BlockSpec((tm, 512), lambda i: (i, 0))
    vec = pl.BlockSpec((1, 512), lambda i: (0, 0))
    return pl.pallas_call(
        body, out_shape=jax.ShapeDtypeStruct((lp, 1024), BF16), grid=(lp // tm,),
        in_specs=[row, row, pl.BlockSpec((tm, 512), lambda i: (i, C_RG // 512)),
                  pl.BlockSpec((tm, 512), lambda i: (i, C_GR // 512)), vec, vec],
        out_specs=pl.BlockSpec((tm, 1024), lambda i: (i, 0)),
        compiler_params=_cp("parallel"), name=name)(o_ret, o_gla, proj, proj, w_ret, w_gla)


def _merge_bwd(dm, o_ret, o_gla, proj, w_ret, w_gla, *, name):
    lp = o_ret.shape[0]
    tm = _row_tile(lp)

    def body(dm_ref, or_ref, og_ref, rg_ref, gr_ref, wr_ref, wg_ref, dor_ref, dog_ref, dgate_ref, dwr_ref, dwg_ref):
        @pl.when(pl.program_id(0) == 0)
        def _():
            dwr_ref[...] = jnp.zeros_like(dwr_ref)
            dwg_ref[...] = jnp.zeros_like(dwg_ref)

        def group(d, o_all, gate, w, center):
            sg = _sigmoid(gate)
            s = gate * sg
            ds = sg * (1.0 + gate * (1.0 - sg))
            xh, rr = [], []
            for h in range(4):
                o = o_all[:, 128 * h:128 * h + 128]
                if center:
                    o = o - jnp.mean(o, axis=-1, keepdims=True)
                r = lax.rsqrt(jnp.mean(o * o, axis=-1, keepdims=True) + EPS)
                xh.append(o * r)
                rr.append(r)
            xh_all = jnp.concatenate(xh, axis=1)
            dgate = d * xh_all * w * ds
            dw = jnp.sum(d * xh_all * s, axis=0, keepdims=True)
            dxh_all = d * w * s
            do = []
            for h in range(4):
                dxh = dxh_all[:, 128 * h:128 * h + 128]
                t = dxh - xh[h] * jnp.mean(dxh * xh[h], axis=-1, keepdims=True)
                if center:
                    t = t - jnp.mean(dxh, axis=-1, keepdims=True)
                do.append(rr[h] * t)
            return jnp.concatenate(do, axis=1), dgate, dw

        dmv = dm_ref[...]
        do, dg, dw = group(dmv[:, 0:512], or_ref[...], rg_ref[...], wr_ref[...], True)
        dor_ref[...] = do
        dgate_ref[:, 0:512] = dg.astype(BF16)
        dwr_ref[...] += dw
        do, dg, dw = group(dmv[:, 512:1024], og_ref[...], gr_ref[...], wg_ref[...], False)
        dog_ref[...] = do
        dgate_ref[:, 512:1024] = dg.astype(BF16)
        dwg_ref[...] += dw

    row = pl.BlockSpec((tm, 512), lambda i: (i, 0))
    vec = pl.BlockSpec((1, 512), lambda i: (0, 0))
    return pl.pallas_call(
        body,
        out_shape=(jax.ShapeDtypeStruct((lp, 512), F32), jax.ShapeDtypeStruct((lp, 512), F32),
                   jax.ShapeDtypeStruct((lp, P_GATE), BF16),
                   jax.ShapeDtypeStruct((1, 512), F32), jax.ShapeDtypeStruct((1, 512), F32)),
        grid=(lp // tm,),
        in_specs=[pl.BlockSpec((tm, 1024), lambda i: (i, 0)), row, row,
                  pl.BlockSpec((tm, 512), lambda i: (i, C_RG // 512)),
                  pl.BlockSpec((tm, 512), lambda i: (i, C_GR // 512)), vec, vec],
        out_specs=(row, row, pl.BlockSpec((tm, P_GATE), lambda i: (i, 0)), vec, vec),
        compiler_params=_cp("arbitrary"), name=name)(dm, o_ret, o_gla, proj, proj, w_ret, w_gla)


def _dot(a, b):
    return lax.dot_general(a, b, (((1,), (0,)), ((), ())), preferred_element_type=F32)


def _dot_nt(a, b):
    return lax.dot_general(a, b, (((1,), (1,)), ((), ())), preferred_element_type=F32)


def _dot_tn(a, b):
    return lax.dot_general(a, b, (((0,), (0,)), ((), ())), preferred_element_type=F32)


def _ret_tables(lp):
    cr = RET_CHUNK
    pos = np.arange(lp, dtype=np.float32) - np.float32(PADF)
    half = RET_DK // 2
    inv = (np.float32(ROPE_BASE) ** (-np.arange(half, dtype=np.float32) / np.float32(half))).astype(np.float32)
    ang = (pos[:, None] * inv[None, :]).astype(np.float32)
    c, s = np.cos(ang).astype(np.float32), np.sin(ang).astype(np.float32)
    rope_c = jnp.asarray(np.concatenate([c, c], axis=1))
    rope_s = jnp.asarray(np.concatenate([-s, s], axis=1))
    log_g = np.log(1.0 - 2.0 ** (-5.0 - np.arange(RET_HEADS, dtype=np.float64)))
    idx = np.arange(cr, dtype=np.float64)
    diff = idx[:, None] - idx[None, :]
    dmat = np.where(diff >= 0, np.exp(log_g[:, None, None] * np.maximum(diff, 0.0)), 0.0)
    zeta = np.exp(log_g[:, None] * (cr - 1.0 - idx)[None, :])
    xi = np.exp(log_g[:, None] * (idx + 1.0)[None, :])
    gc = np.exp(log_g * cr)
    f = lambda a: jnp.asarray(a.astype(np.float32))
    return (rope_c, rope_s, f(dmat), f(np.broadcast_to(zeta[:, :, None], (RET_HEADS, cr, 128))),
            f(np.broadcast_to(xi[:, :, None], (RET_HEADS, cr, 128))),
            f(np.broadcast_to(gc[:, None, None], (RET_HEADS, 8, 128))))


def _rope(t, c, s):
    return t * c + pltpu.roll(t, 64, 1) * s


def _rope_t(d, c, s):
    return d * c + pltpu.roll(d * s, 64, 1)


def _ret_specs(nblk, rev):
    ix = (lambda i: nblk - 1 - i) if rev else (lambda i: i)
    cr = RET_CHUNK
    col = lambda base: pl.BlockSpec((BLK, 512), lambda i: (ix(i), base // 512))
    tab = pl.BlockSpec((BLK, 128), lambda i: (ix(i), 0))
    sq = pl.BlockSpec((RET_HEADS, cr, cr), lambda i: (0, 0, 0))
    hv = pl.BlockSpec((RET_HEADS, cr, 128), lambda i: (0, 0, 0))
    g8 = pl.BlockSpec((RET_HEADS, 8, 128), lambda i: (0, 0, 0))
    st = pl.BlockSpec((RET_HEADS, BLK // cr, 128, 128), lambda i: (0, ix(i), 0, 0))
    out = pl.BlockSpec((BLK, 512), lambda i: (ix(i), 0))
    return col, tab, sq, hv, g8, st, out


def _retention(proj, tables, *, name):
    lp = proj.shape[0]
    nblk, cr = lp // BLK, RET_CHUNK
    scale = RET_DK ** -0.5

    def body(q_ref, k_ref, v_ref, c_ref, s_ref, d_ref, z_ref, x_ref, g_ref, o_ref, st_ref, state):
        @pl.when(pl.program_id(0) == 0)
        def _():
            state[...] = jnp.zeros_like(state)

        def chunk(ci, carry):
            sl = pl.ds(pl.multiple_of(ci * cr, cr), cr)
            c, s = c_ref[sl, :], s_ref[sl, :]
            for h in range(RET_HEADS):
                hs = slice(128 * h, 128 * h + 128)
                q = _rope(q_ref[sl, hs], c, s)
                k = _rope(k_ref[sl, hs], c, s) * scale
                qb, kb, vb = q.astype(BF16), k.astype(BF16), v_ref[sl, hs].astype(BF16)
                st = state[h]
                st_ref[h, ci] = st
                sc = _dot_nt(qb, kb) * d_ref[h]
                o_ref[sl, hs] = _dot(sc.astype(BF16), vb) + _dot(qb, st.astype(BF16)) * x_ref[h]
                state[h] = st * g_ref[h][0:1, :] + _dot_tn((k * z_ref[h]).astype(BF16), vb)
            return carry

        lax.fori_loop(0, BLK // cr, chunk, 0)

    col, tab, sq, hv, g8, st, out = _ret_specs(nblk, False)
    return pl.pallas_call(
        body,
        out_shape=(jax.ShapeDtypeStruct((lp, 512), F32), jax.ShapeDtypeStruct((4, lp // cr, 128, 128), F32)),
        grid=(nblk,), in_specs=[col(C_RQ), col(C_RK), col(C_RV), tab, tab, sq, hv, hv, g8],
        out_specs=(out, st), scratch_shapes=[pltpu.VMEM((RET_HEADS, 128, 128), F32)],
        compiler_params=_cp("arbitrary"), name=name)(proj, proj, proj, *tables)


def _retention_bwd(proj, do, states, tables, *, name):
    lp = proj.shape[0]
    nblk, cr = lp // BLK, RET_CHUNK
    nch = BLK // cr
    scale = RET_DK ** -0.5

    def body(q_ref, k_ref, v_ref, do_ref, st_ref, c_ref, s_ref, d_ref, z_ref, x_ref, g_ref, dqkv_ref, dstate):
        @pl.when(pl.program_id(0) == 0)
        def _():
            dstate[...] = jnp.zeros_like(dstate)

        def chunk(cc, carry):
            ci = nch - 1 - cc
            sl = pl.ds(pl.multiple_of(ci * cr, cr), cr)
            c, s = c_ref[sl, :], s_ref[sl, :]
            for h in range(RET_HEADS):
                hs = slice(128 * h, 128 * h + 128)
                dmat, zeta, xi = d_ref[h], z_ref[h], x_ref[h]
                q = _rope(q_ref[sl, hs], c, s)
                k = _rope(k_ref[sl, hs], c, s) * scale
                qb, kb, vb = q.astype(BF16), k.astype(BF16), v_ref[sl, hs].astype(BF16)
                kzb = (k * zeta).astype(BF16)
                dov = do_ref[sl, hs]
                dob, doxb = dov.astype(BF16), (dov * xi).astype(BF16)
                stb = st_ref[h, ci].astype(BF16)
                dsn = dstate[h]
                dsnb = dsn.astype(BF16)
                scb = (_dot_nt(qb, kb) * dmat).astype(BF16)
                dscb = (_dot_nt(dob, vb) * dmat).astype(BF16)
                dq = _dot(dscb, kb) + _dot_nt(doxb, stb)
                dk = _dot_tn(dscb, qb) + _dot_nt(vb, dsnb) * zeta
                dv = _dot_tn(scb, dob) + _dot(kzb, dsnb)
                dstate[h] = dsn * g_ref[h][0:1, :] + _dot_tn(qb, doxb)
                dqkv_ref[sl, 128 * h:128 * h + 128] = _rope_t(dq, c, s).astype(BF16)
                dqkv_ref[sl, 512 + 128 * h:640 + 128 * h] = _rope_t(dk * scale, c, s).astype(BF16)
                dqkv_ref[sl, 1024 + 128 * h:1152 + 128 * h] = dv.astype(BF16)
            return carry

        lax.fori_loop(0, nch, chunk, 0)

    col, tab, sq, hv, g8, st, out = _ret_specs(nblk, True)
    return pl.pallas_call(
        body, out_shape=jax.ShapeDtypeStruct((lp, P_RET), BF16), grid=(nblk,),
        in_specs=[col(C_RQ), col(C_RK), col(C_RV), out, st, tab, tab, sq, hv, hv, g8],
        out_specs=pl.BlockSpec((BLK, P_RET), lambda i: (nblk - 1 - i, 0)),
        scratch_shapes=[pltpu.VMEM((RET_HEADS, 128, 128), F32)],
        compiler_params=_cp("arbitrary"), name=name)(proj, proj, proj, do, states, *tables)


def _gla_tables():
    c = GLA_CHUNK
    tri = np.tril(np.ones((c, c), np.float32))
    ones_qv = np.kron(np.eye(GLA_HEADS, dtype=np.float32), np.ones((GLA_DK, GLA_DV), np.float32))
    return (jnp.asarray(tri, BF16), jnp.asarray(tri.T.copy(), BF16), jnp.asarray(ones_qv, BF16),
            jnp.asarray(ones_qv.T.copy(), BF16))


def _tri_sum(tri, x):
    hi = x.astype(BF16)
    lo = (x - hi.astype(F32)).astype(BF16)
    return _dot(tri, hi) + _dot(tri, lo)


def _head_masks(width, per):
    lane = lax.broadcasted_iota(jnp.int32, (1, width), 1)
    return [((lane >= per * h) & (lane < per * (h + 1))).astype(F32) for h in range(GLA_HEADS)]


def _stack_heads(x, masks):
    return jnp.concatenate([x * m for m in masks], axis=0)


def _gla_gate(ga, w2, b, ok, tri):
    z = _dot(ga.astype(BF16), w2) + b
    la = (jnp.minimum(z, 0.0) - jnp.log(1.0 + jnp.exp(-jnp.abs(z)))) * (1.0 / GLA_TAU)
    la = jnp.where(ok, la, 0.0)
    return z, _tri_sum(tri, la)


def _gla_rows(i_blk, ci, lp):
    c = GLA_CHUNK
    rows = i_blk * BLK + ci * c + lax.broadcasted_iota(jnp.int32, (c, 1), 0)
    return (rows >= PADF) & (rows < lp - BACK)


N_SUB = GLA_CHUNK // GLA_SUB - 1
N_SUB2 = GLA_SUB // GLA_SUB2 - 1


def _gla_masks():
    c, s1, s2 = GLA_CHUNK, GLA_SUB, GLA_SUB2
    sh1, sh2 = s1.bit_length() - 1, s2.bit_length() - 1
    r = lax.broadcasted_iota(jnp.int32, (c, GLA_QK), 0)
    blk, within = jnp.right_shift(r, sh1), jnp.bitwise_and(r, s1 - 1)
    grp = jnp.right_shift(within, sh2)
    rowm = [(blk == a).astype(F32) for a in range(1, N_SUB + 1)] + [(grp == b).astype(F32) for b in range(1, N_SUB2 + 1)]
    keym = ([(r < s1 * a).astype(F32) for a in range(1, N_SUB + 1)]
            + [(within < s2 * b).astype(F32) for b in range(1, N_SUB2 + 1)])
    rs = lax.broadcasted_iota(jnp.int32, (GLA_HEADS * c, c), 0)
    ts = lax.broadcasted_iota(jnp.int32, (GLA_HEADS * c, c), 1)
    same = (jnp.right_shift(jnp.bitwise_and(rs, c - 1), sh1) == jnp.right_shift(ts, sh1)).astype(F32)
    lag = [(jnp.bitwise_and(r, s2 - 1) >= j).astype(F32) for j in range(s2)]
    return rowm, keym, same, lag


def _gla_hats(qs, k, g, masks, hm_q):
    c, s1, s2 = GLA_CHUNK, GLA_SUB, GLA_SUB2
    rowm, keym, same, _ = masks
    refs = [g[s1 * a - 1:s1 * a, :] for a in range(1, N_SUB + 1)]
    for b in range(1, N_SUB2 + 1):
        refs.append(jnp.concatenate([jnp.broadcast_to(g[s1 * i + s2 * b - 1:s1 * i + s2 * b, :], (s1, GLA_QK))
                                     for i in range(c // s1)], axis=0))
    eqs = [jnp.exp(jnp.minimum(g - r, 0.0)) * m for r, m in zip(refs, rowm)]
    eks = [jnp.exp(jnp.minimum(r - g, 0.0)) * m for r, m in zip(refs, keym)]
    qhs, khs = [qs * e for e in eqs], [k * e for e in eks]
    qst = [_stack_heads(q, hm_q).astype(BF16) for q in qhs]
    khb = [x.astype(BF16) for x in khs]
    qa, qb = jnp.concatenate(qst[:N_SUB], axis=1), jnp.concatenate(qst[N_SUB:], axis=1)
    ka, kb = jnp.concatenate(khb[:N_SUB], axis=1), jnp.concatenate(khb[N_SUB:], axis=1)
    p = _dot_nt(qa, ka) + _dot_nt(qb, kb) * same
    return eqs, eks, qhs, khs, qa, qb, ka, kb, p


def _roll_rows(x, j):
    return x if j == 0 else pltpu.roll(x, j, 0)


def _gla(proj, w2p, b, tables, *, name):
    lp = proj.shape[0]
    nblk, c, s2 = lp // BLK, GLA_CHUNK, GLA_SUB2
    nch = BLK // c

    def body(q_ref, k_ref, v_ref, a_ref, w_ref, b_ref, tri_ref, ones_ref, o_ref, st_ref, gz_ref, state):
        i_blk = pl.program_id(0)

        @pl.when(i_blk == 0)
        def _():
            state[...] = jnp.zeros_like(state)
        hm_q = _head_masks(GLA_QK, GLA_DK)
        masks = _gla_masks()
        tri, ones_qv, w2, bias = tri_ref[...], ones_ref[...], w_ref[...], b_ref[...]

        def chunk(ci, carry):
            sl = pl.ds(pl.multiple_of(ci * c, c), c)
            ok = _gla_rows(i_blk, ci, lp)
            k, v = k_ref[sl, :], v_ref[sl, :]
            vb = v.astype(BF16)
            qs = q_ref[sl, :] * (GLA_DK ** -0.5)
            z, g = _gla_gate(a_ref[sl, :], w2, bias, ok, tri)
            gz_ref[sl, 0:GLA_QK] = g
            gz_ref[sl, GLA_QK:2 * GLA_QK] = z
            last = g[c - 1:c, :]
            st = state[...]
            st_ref[ci] = st
            qst = _stack_heads(qs * jnp.exp(g), hm_q).astype(BF16)
            oi = _dot_nt(qst, st.astype(BF16))
            o = jnp.concatenate([oi[c * h:c * h + c, :] for h in range(GLA_HEADS)], axis=1)
            ke = k * jnp.exp(last - g)
            f = _dot_tn(vb, ke.astype(BF16))
            upd = f[0:GLA_DV, :] * hm_q[0]
            for h in range(1, GLA_HEADS):
                upd = upd + f[GLA_DV * h:GLA_DV * (h + 1), :] * hm_q[h]
            state[...] = st * jnp.exp(last) + upd
            p = _gla_hats(qs, k, g, masks, hm_q)[-1]
            ob = _dot(p.astype(BF16), vb)
            o = o + jnp.concatenate([ob[c * h:c * h + c, GLA_DV * h:GLA_DV * (h + 1)] for h in range(GLA_HEADS)],
                                    axis=1)
            ws = []
            for j in range(s2):
                ej = jnp.exp(jnp.minimum(g - _roll_rows(g, j), 0.0))
                ws.append((qs * _roll_rows(k, j) * ej * masks[3][j]).astype(BF16))
            ball = _dot(jnp.concatenate(ws, axis=0), ones_qv)
            for j in range(s2):
                o = o + ball[c * j:c * j + c, :] * _roll_rows(v, j)
            o_ref[sl, :] = o
            return carry

        lax.fori_loop(0, nch, chunk, 0)

    tri, _, ones_qv, _ = tables
    full = lambda arr: pl.BlockSpec(arr.shape, lambda i: (0,) * arr.ndim)
    return pl.pallas_call(
        body,
        out_shape=(jax.ShapeDtypeStruct((lp, GLA_V), F32), jax.ShapeDtypeStruct((lp // c, GLA_DV, GLA_QK), F32),
                   jax.ShapeDtypeStruct((lp, 2 * GLA_QK), F32)),
        grid=(nblk,),
        in_specs=[pl.BlockSpec((BLK, GLA_QK), lambda i: (i, C_GQ // GLA_QK)),
                  pl.BlockSpec((BLK, GLA_QK), lambda i: (i, C_GK // GLA_QK)),
                  pl.BlockSpec((BLK, GLA_V), lambda i: (i, C_GV // GLA_V)),
                  pl.BlockSpec((BLK, 128), lambda i: (i, C_GA // 128)),
                  full(w2p), full(b), full(tri), full(ones_qv)],
        out_specs=(pl.BlockSpec((BLK, GLA_V), lambda i: (i, 0)),
                   pl.BlockSpec((nch, GLA_DV, GLA_QK), lambda i: (i, 0, 0)),
                   pl.BlockSpec((BLK, 2 * GLA_QK), lambda i: (i, 0))),
        scratch_shapes=[pltpu.VMEM((GLA_DV, GLA_QK), F32)],
        compiler_params=_cp("arbitrary"), name=name)(proj, proj, proj, proj, w2p, b, tri, ones_qv)


def _gla_bwd(proj, do, states, gz, w2p, tables, *, name):
    lp = proj.shape[0]
    nblk, c, s1, s2 = lp // BLK, GLA_CHUNK, GLA_SUB, GLA_SUB2
    nch = BLK // c

    def body(q_ref, k_ref, v_ref, a_ref, do_ref, st_ref, gz_ref, w_ref, trit_ref, ones_ref, onest_ref,
             dp_ref, dw_ref, db_ref, dstate, dqs_s, dk_s, dg_s, dv_s):
        i_blk = nblk - 1 - pl.program_id(0)

        @pl.when(pl.program_id(0) == 0)
        def _():
            dstate[...] = jnp.zeros_like(dstate)
            dw_ref[...] = jnp.zeros_like(dw_ref)
            db_ref[...] = jnp.zeros_like(db_ref)
        hm_q = _head_masks(GLA_QK, GLA_DK)
        hm_v = _head_masks(GLA_V, GLA_DV)
        masks = _gla_masks()
        trit, ones_qv, ones_vq = trit_ref[...], ones_ref[...], onest_ref[...]
        w2 = w_ref[...]
        rsum = lambda x: jnp.sum(x, axis=0, keepdims=True)

        def chunk(cc, carry):
            ci = nch - 1 - cc
            sl = pl.ds(pl.multiple_of(ci * c, c), c)
            ok = _gla_rows(i_blk, ci, lp)
            k, v, ga = k_ref[sl, :], v_ref[sl, :], a_ref[sl, :]
            vb = v.astype(BF16)
            qs = q_ref[sl, :] * (GLA_DK ** -0.5)
            g, z = gz_ref[sl, 0:GLA_QK], gz_ref[sl, GLA_QK:2 * GLA_QK]
            last = g[c - 1:c, :]
            elast = jnp.exp(last)
            eg = jnp.exp(g)
            ekl = jnp.exp(last - g)
            qe, ke = qs * eg, k * ekl
            dov = do_ref[sl, :]
            st = st_ref[ci]
            dsn = dstate[...]
            qst = _stack_heads(qe, hm_q).astype(BF16)
            dost = jnp.concatenate([dov[:, GLA_DV * h:GLA_DV * (h + 1)] for h in range(GLA_HEADS)], axis=0).astype(BF16)
            dqe_st = _dot(dost, st.astype(BF16))
            dqe = dqe_st[0:c, :] * hm_q[0]
            for h in range(1, GLA_HEADS):
                dqe = dqe + dqe_st[c * h:c * h + c, :] * hm_q[h]
            dstate[...] = _dot_tn(dost, qst) + dsn * elast
            dlast = rsum(dsn * st) * elast
            df = _stack_heads(dsn, hm_q).astype(BF16)
            dv_s[...] = _dot_nt(ke.astype(BF16), df)
            dke = _dot(vb, df)
            xk = dke * ke
            dqs_s[...] = dqe * eg
            dk_s[...] = dke * ekl
            dg_s[...] = dqe * qe - xk
            dlast = dlast + rsum(xk)
            eqs, eks, qhs, khs, qa, qb, ka, kb, p = _gla_hats(qs, k, g, masks, hm_q)
            dost_v = _stack_heads(dov, hm_v).astype(BF16)
            dp = _dot_nt(dost_v, vb)
            dv_s[...] += _dot_tn(p.astype(BF16), dost_v)
            dpa, dpb = dp.astype(BF16), (dp * masks[2]).astype(BF16)
            dq_all = (_dot(dpa, ka), _dot(dpb, kb))
            dk_all = (_dot_tn(dpa, qa), _dot_tn(dpb, qb))
            for t in range(N_SUB + N_SUB2):
                lvl, i = (0, t) if t < N_SUB else (1, t - N_SUB)
                cols = slice(GLA_QK * i, GLA_QK * (i + 1))
                dq_st = dq_all[lvl][:, cols]
                dqh = dq_st[0:c, :] * hm_q[0]
                for h in range(1, GLA_HEADS):
                    dqh = dqh + dq_st[c * h:c * h + c, :] * hm_q[h]
                dkh = dk_all[lvl][:, cols]
                xq, xkh = dqh * qhs[t], dkh * khs[t]
                dqs_s[...] += dqh * eqs[t]
                dk_s[...] += dkh * eks[t]
                dg_s[...] += xq - xkh
                back_ref = xkh - xq
                if lvl == 0:
                    row = s1 * (i + 1) - 1
                    dg_s[row:row + 1, :] += rsum(back_ref)
                else:
                    for blk in range(c // s1):
                        row = s1 * blk + s2 * (i + 1) - 1
                        dg_s[row:row + 1, :] += rsum(back_ref[s1 * blk:s1 * blk + s1, :])
            kes, qes, ws, dbs = [], [], [], []
            for j in range(s2):
                em = jnp.exp(jnp.minimum(g - _roll_rows(g, j), 0.0)) * masks[3][j]
                kes.append(_roll_rows(k, j) * em)
                qes.append(qs * em)
                ws.append((qs * kes[j]).astype(BF16))
                dbs.append((dov * _roll_rows(v, j)).astype(BF16))
            ball = _dot(jnp.concatenate(ws, axis=0), ones_qv)
            dwall = _dot(jnp.concatenate(dbs, axis=0), ones_vq)
            for j in range(s2):
                back = (lambda x: x) if j == 0 else (lambda x, j=j: pltpu.roll(x, c - j, 0))
                dw = dwall[c * j:c * j + c, :]
                dv_s[...] += back(ball[c * j:c * j + c, :] * dov)
                dqs_s[...] += dw * kes[j]
                dk_s[...] += back(dw * qes[j])
                x = dw * qs * kes[j]
                dg_s[...] += x - back(x)
            dg_s[c - 1:c, :] += dlast
            dla = jnp.where(ok, _tri_sum(trit, dg_s[...]), 0.0)
            dz = dla * (1.0 / GLA_TAU) / (1.0 + jnp.exp(z))
            dzb = dz.astype(BF16)
            dp_ref[sl, 0:256] = (dqs_s[...] * (GLA_DK ** -0.5)).astype(BF16)
            dp_ref[sl, 256:512] = dk_s[...].astype(BF16)
            dp_ref[sl, 512:1024] = dv_s[...].astype(BF16)
            dp_ref[sl, 1024:1152] = _dot_nt(dzb, w2).astype(BF16)
            dp_ref[sl, 1152:1280] = jnp.zeros((c, 128), BF16)
            dw_ref[...] += _dot_tn(ga.astype(BF16), dzb)
            db_ref[...] += rsum(dz)
            return carry

        lax.fori_loop(0, nch, chunk, 0)

    tri, trit, ones_qv, ones_vq = tables
    full = lambda arr: pl.BlockSpec(arr.shape, lambda i: (0,) * arr.ndim)
    rev = lambda i: nblk - 1 - i
    return pl.pallas_call(
        body,
        out_shape=(jax.ShapeDtypeStruct((lp, P_GLA), BF16),
                   jax.ShapeDtypeStruct((128, GLA_QK), F32), jax.ShapeDtypeStruct((1, GLA_QK), F32)),
        grid=(nblk,),
        in_specs=[pl.BlockSpec((BLK, GLA_QK), lambda i: (rev(i), C_GQ // GLA_QK)),
                  pl.BlockSpec((BLK, GLA_QK), lambda i: (rev(i), C_GK // GLA_QK)),
                  pl.BlockSpec((BLK, GLA_V), lambda i: (rev(i), C_GV // GLA_V)),
                  pl.BlockSpec((BLK, 128), lambda i: (rev(i), C_GA // 128)),
                  pl.BlockSpec((BLK, GLA_V), lambda i: (rev(i), 0)),
                  pl.BlockSpec((nch, GLA_DV, GLA_QK), lambda i: (rev(i), 0, 0)),
                  pl.BlockSpec((BLK, 2 * GLA_QK), lambda i: (rev(i), 0)),
                  full(w2p), full(trit), full(ones_qv), full(ones_vq)],
        out_specs=(pl.BlockSpec((BLK, P_GLA), lambda i: (rev(i), 0)),
                   pl.BlockSpec((128, GLA_QK), lambda i: (0, 0)),
                   pl.BlockSpec((1, GLA_QK), lambda i: (0, 0))),
        scratch_shapes=[pltpu.VMEM((GLA_DV, GLA_QK), F32), pltpu.VMEM((c, GLA_QK), F32),
                        pltpu.VMEM((c, GLA_QK), F32), pltpu.VMEM((c, GLA_QK), F32), pltpu.VMEM((c, GLA_V), F32)],
        compiler_params=_cp("arbitrary"), name=name)(proj, proj, proj, proj, do, states, gz, w2p, trit, ones_qv, ones_vq)


def _as2d(a):
    return a.reshape(-1, a.shape[-1])


def _ew_tile(r):
    return _tile(r, (512, 256, 128, 64, 32, 16, 8))


def _add2(a, b, *, out_dtype, name):
    a2, b2 = _as2d(a), _as2d(b)
    r, n = a2.shape
    tm = _ew_tile(r)

    def body(a_ref, b_ref, o_ref):
        o_ref[...] = (a_ref[...] + b_ref[...]).astype(o_ref.dtype)

    blk = pl.BlockSpec((tm, n), lambda i: (i, 0))
    return pl.pallas_call(body, out_shape=jax.ShapeDtypeStruct((r, n), out_dtype), grid=(r // tm,), in_specs=[blk, blk],
                          out_specs=blk, compiler_params=_cp("parallel"), name=name)(a2, b2).reshape(a.shape)


def _sum_slots(own, q, *, name):
    shape = own.shape
    q3 = q.reshape(3, -1, shape[-1])
    own2 = _as2d(own)
    r, n = own2.shape
    tm = _ew_tile(r)

    def body(own_ref, q_ref, o_ref):
        f = lambda i: q_ref[i].astype(F32)
        o_ref[...] = ((own_ref[...].astype(F32) + f(0)) + f(1)) + f(2)

    blk = pl.BlockSpec((tm, n), lambda i: (i, 0))
    return pl.pallas_call(
        body, out_shape=jax.ShapeDtypeStruct((r, n), F32), grid=(r // tm,),
        in_specs=[blk, pl.BlockSpec((3, tm, n), lambda i: (0, i, 0))], out_specs=blk,
        compiler_params=_cp("parallel"), name=name)(own2, q3).reshape(shape)


def _adamw(w, g, m, v, *, name):
    shape = w.shape
    w2, g2, m2, v2 = _as2d(w), _as2d(g), _as2d(m), _as2d(v)
    r, n = w2.shape
    tm = _ew_tile(r)

    def body(w_ref, g_ref, m_ref, v_ref, d_ref, mo_ref, vo_ref):
        d_ref[...], mo_ref[...], vo_ref[...] = _adam_math(w_ref[...], g_ref[...], m_ref[...], v_ref[...])

    blk = pl.BlockSpec((tm, n), lambda i: (i, 0))
    o = jax.ShapeDtypeStruct((r, n), F32)
    d, mo, vo = pl.pallas_call(body, out_shape=(o, o, o), grid=(r // tm,), in_specs=[blk] * 4, out_specs=(blk,) * 3,
                               compiler_params=_cp("parallel"), name=name)(w2, g2, m2, v2)
    return d.reshape(shape), mo.reshape(shape), vo.reshape(shape)


def _adam_math(w, gv, m, v):
    c1 = 1.0 - ADAM_B1 ** ADAM_STEP
    c2 = 1.0 - ADAM_B2 ** ADAM_STEP
    mn = ADAM_B1 * m + (1.0 - ADAM_B1) * gv
    vn = ADAM_B2 * v + (1.0 - ADAM_B2) * (gv * gv)
    return -ADAM_LR * ((mn / c1) / (jnp.sqrt(vn / c2) + ADAM_EPS) + ADAM_WD * w), mn, vn


def _adamw_halves(w, m, v, mine, theirs, c, *, name):
    depth, rows, n = w.shape
    r2 = rows // 2
    tm = next(t for t in range(min(r2, 256), 0, -8) if r2 % t == 0)
    steps = r2 // tm

    def body(c_ref, w_ref, m_ref, v_ref, *rest):
        halves, (g_ref, d_ref, mo_ref, vo_ref) = rest[:2 * depth], rest[2 * depth:]
        l, h = pl.program_id(0), pl.program_id(1)
        gv = None
        for k in range(depth):
            gk = jnp.where(h == c_ref[0], halves[2 * k][...], halves[2 * k + 1][...])
            gv = gk if gv is None else jnp.where(l == k, gk, gv)
        g_ref[...] = gv
        d_ref[...], mo_ref[...], vo_ref[...] = _adam_math(w_ref[...], gv, m_ref[...], v_ref[...])

    big = pl.BlockSpec((tm, n), lambda l, h, i, c_ref: ((2 * l + h) * steps + i, 0))
    half = lambda k: pl.BlockSpec((tm, n), lambda l, h, i, c_ref: (jnp.where(l == k, i, 0), 0))
    o = jax.ShapeDtypeStruct((depth * rows, n), F32)
    args = [a for k in range(depth) for a in (mine[k], theirs[k])]
    outs = pl.pallas_call(
        body, out_shape=(o, o, o, o),
        grid_spec=pltpu.PrefetchScalarGridSpec(
            num_scalar_prefetch=1, grid=(depth, 2, steps),
            in_specs=[big, big, big] + [half(k) for k in range(depth) for _ in range(2)], out_specs=(big,) * 4),
        compiler_params=_cp("arbitrary", "arbitrary", "arbitrary"), name=name)(
            jnp.reshape(c, (1,)).astype(jnp.int32), _as2d(w), _as2d(m), _as2d(v), *args)
    return [a.reshape(w.shape) for a in outs]


ANY = pl.BlockSpec(memory_space=pl.ANY)


def _place():
    return lax.axis_index("x"), lax.axis_index("y"), lax.axis_index("c")


def _other_chips(x, y):
    return [(1 - x, y), (x, 1 - y), (1 - x, 1 - y)]


def _remote(src, dst, ssem, rsem, dev):
    return pltpu.make_async_remote_copy(src_ref=src, dst_ref=dst, send_sem=ssem, recv_sem=rsem, device_id=dev,
                                        device_id_type=MESH)


def _allgather_chips(arrs, *, name):
    n = len(arrs)

    def body(*refs):
        ins, outs = refs[:n], refs[n:2 * n]
        s1, r1, s2, r2 = refs[2 * n:]
        x, y, c = _place()
        q = 2 * x + y
        chips = _other_chips(x, y)
        qs = [2 * cx + cy for cx, cy in chips]
        sib = (x, y, 1 - c)
        first, passed = [], []
        for k in range(n):
            for j, chip in enumerate(chips):
                first.append(_remote(ins[k].at[c], outs[k].at[c, q], s1.at[k, j], r1.at[k, j], (*chip, c)))
        for cp in first:
            cp.start()
        for k in range(n):
            for j, chip in enumerate(chips):
                land = outs[k].at[c, qs[j]]
                _remote(land, land, s1.at[k, j], r1.at[k, j], (*chip, c)).wait_recv()
                fw = _remote(land, land, s2.at[k, j], r2.at[k, j], sib)
                fw.start()
                passed.append(fw)
        for k in range(n):
            for j in range(3):
                land = outs[k].at[1 - c, qs[j]]
                _remote(land, land, s2.at[k, j], r2.at[k, j], sib).wait_recv()
        for cp in first + passed:
            cp.wait_send()

    sem = pltpu.SemaphoreType.DMA
    outs = pl.pallas_call(
        body, out_shape=tuple(jax.ShapeDtypeStruct((2, 4) + a.shape[1:], a.dtype) for a in arrs),
        in_specs=[ANY] * n, out_specs=(ANY,) * n,
        scratch_shapes=[sem((n, 3)), sem((n, 3)), sem((n, 3)), sem((n, 3))], name=name)(*arrs)
    chip = 2 * lax.axis_index("x") + lax.axis_index("y")
    return [lax.dynamic_update_slice_in_dim(o, a[:, None], chip, axis=1) for o, a in zip(outs, arrs)]


def _pair_exchange(arrs, *, name):
    n = len(arrs)

    def body(*refs):
        ins, outs = refs[:n], refs[n:2 * n]
        ssem, rsem = refs[2 * n:]
        x, y, c = _place()
        cps = [_remote(ins[k].at[:, 1 - c], outs[k], ssem.at[k], rsem.at[k], (x, y, 1 - c)) for k in range(n)]
        for cp in cps:
            cp.start()
        for cp in cps:
            cp.wait()

    sem = pltpu.SemaphoreType.DMA
    return pl.pallas_call(
        body, out_shape=tuple(jax.ShapeDtypeStruct((a.shape[0],) + a.shape[2:], a.dtype) for a in arrs),
        in_specs=[ANY] * n, out_specs=(ANY,) * n, scratch_shapes=[sem((n,)), sem((n,))], name=name)(*arrs)


def _pair_sum(mine, theirs, c, *, name):
    _, _, r, n = mine.shape
    tm = r if r <= 512 else _ew_tile(r)

    def body(c_ref, a_ref, b_ref, o_ref):
        o_ref[...] = (a_ref[...] + b_ref[...]).astype(BF16)

    blk = pl.BlockSpec((None, tm, n), lambda s, i, c_ref: (s, i, 0))
    return pl.pallas_call(
        body, out_shape=jax.ShapeDtypeStruct((4, r, n), BF16),
        grid_spec=pltpu.PrefetchScalarGridSpec(
            num_scalar_prefetch=1, grid=(4, r // tm),
            in_specs=[pl.BlockSpec((None, None, tm, n), lambda s, i, c_ref: (s, c_ref[0], i, 0)), blk], out_specs=blk),
        compiler_params=_cp("parallel", "parallel"), name=name)(jnp.reshape(c, (1,)).astype(jnp.int32), mine, theirs)


def _chip_copies(ins, outs, ssem, rsem, mode):
    x, y, c = _place()
    q = 2 * x + y
    sends, recvs = [], []
    for k in range(len(ins)):
        for j, (cx, cy) in enumerate(_other_chips(x, y)):
            sem = (ssem.at[k, j], rsem.at[k, j], (cx, cy, c))
            if mode == "scatter":
                sends.append(_remote(ins[k].at[2 * cx + cy], outs[k].at[j], *sem))
                recvs.append(sends[-1])
            else:
                sends.append(_remote(ins[k].at[c], outs[k].at[2 * q + c], *sem))
                recvs.append(_remote(ins[k].at[c], outs[k].at[2 * (2 * cx + cy) + c], *sem))
    return sends, recvs


def _chip_wait(sends, recvs):
    for cp in sends:
        cp.wait_send()
    for cp in recvs:
        cp.wait_recv()


def _landing_shape(a, mode):
    return jax.ShapeDtypeStruct(((3,) if mode == "scatter" else (8,)) + a.shape[1:], a.dtype)


def _chip_exchange(arrs, mode, *, name):
    n = len(arrs)

    def body(*refs):
        ins, outs = refs[:n], refs[n:2 * n]
        ssem, rsem = refs[2 * n:]
        sends, recvs = _chip_copies(ins, outs, ssem, rsem, mode)
        for cp in sends:
            cp.start()
        _chip_wait(sends, recvs)

    sem = pltpu.SemaphoreType.DMA
    return list(pl.pallas_call(
        body, out_shape=tuple(_landing_shape(a, mode) for a in arrs),
        in_specs=[ANY] * n, out_specs=(ANY,) * n, scratch_shapes=[sem((n, 3)), sem((n, 3))], name=name)(*arrs))


def _pair_fill(bufs, owns, *, name):
    n = len(bufs)

    def body(*refs):
        own, outs = refs[n:2 * n], refs[2 * n:3 * n]
        ssem, rsem = refs[3 * n:]
        x, y, c = _place()
        q = 2 * x + y
        sib = (x, y, 1 - c)
        sends, recvs = [], []
        for k in range(n):
            for j, (cx, cy) in enumerate(_other_chips(x, y)):
                mine, theirs = outs[k].at[2 * (2 * cx + cy) + c], outs[k].at[2 * (2 * cx + cy) + 1 - c]
                sends.append(_remote(mine, mine, ssem.at[k, j], rsem.at[k, j], sib))
                recvs.append(_remote(mine, theirs, ssem.at[k, j], rsem.at[k, j], sib))
            slots = outs[k].at[pl.ds(2 * q, 2)]
            sends.append(_remote(own[k], slots, ssem.at[k, 3], rsem.at[k, 3], sib))
            recvs.append(sends[-1])
        for cp in sends:
            cp.start()
        _chip_wait(sends, recvs)

    sem = pltpu.SemaphoreType.DMA
    return list(pl.pallas_call(
        body, out_shape=tuple(jax.ShapeDtypeStruct(b.shape, b.dtype) for b in bufs),
        in_specs=[ANY] * (2 * n), out_specs=(ANY,) * n, scratch_shapes=[sem((n, 4)), sem((n, 4))],
        input_output_aliases={k: k for k in range(n)}, name=name)(*bufs, *owns))


def _pair_swap(arrs, *, name):
    n = len(arrs)

    def body(*refs):
        ins, outs = refs[:n], refs[n:2 * n]
        ssem, rsem = refs[2 * n:]
        x, y, c = _place()
        cps = [_remote(ins[k], outs[k], ssem.at[k], rsem.at[k], (x, y, 1 - c)) for k in range(n)]
        for cp in cps:
            cp.start()
        for cp in cps:
            cp.wait()

    sem = pltpu.SemaphoreType.DMA
    return pl.pallas_call(
        body, out_shape=tuple(jax.ShapeDtypeStruct(a.shape, a.dtype) for a in arrs),
        in_specs=[ANY] * n, out_specs=(ANY,) * n, scratch_shapes=[sem((n,)), sem((n,))], name=name)(*arrs)


def _allreduce_small(slab, *, name):
    r, n = slab.shape

    def body(x_ref, o_ref, buf, ssem, rsem):
        x, y, c = _place()
        me = 4 * x + 2 * y + c
        buf[me] = x_ref[...]
        cps = []
        for rel in range(1, 8):
            bx, by, bc = (rel >> 2) & 1, (rel >> 1) & 1, rel & 1
            px, py, pc = (x + bx) % 2, (y + by) % 2, (c + bc) % 2
            cps.append((_remote(x_ref, buf.at[me], ssem.at[rel - 1], rsem.at[rel - 1], (px, py, pc)),
                        4 * px + 2 * py + pc, (px, py, pc)))
        for cp, _, _ in cps:
            cp.start()
        for rel, (cp, peer, dev) in enumerate(cps):
            cp.wait_send()
            _remote(x_ref, buf.at[peer], ssem.at[rel], rsem.at[rel], dev).wait_recv()
        acc = buf[0]
        for k in range(1, 8):
            acc = acc + buf[k]
        o_ref[...] = acc

    vm = pl.BlockSpec(memory_space=pltpu.VMEM)
    sem = pltpu.SemaphoreType.DMA
    return pl.pallas_call(
        body, out_shape=jax.ShapeDtypeStruct((r, n), F32), in_specs=[vm], out_specs=vm,
        scratch_shapes=[pltpu.VMEM((8, r, n), F32), sem((7,)), sem((7,))], name=name)(slab)


def _slab(arrs, row_mult):
    flat = jnp.concatenate([a.reshape(-1) for a in arrs])
    unit = 128 * row_mult
    total = -(-flat.size // unit) * unit
    return jnp.pad(flat, (0, total - flat.size)).reshape(-1, 128)


def _unslab(slab, shapes):
    flat = slab.reshape(-1)
    out, off = [], 0
    for s in shapes:
        size = int(np.prod(s))
        out.append(flat[off:off + size].reshape(s))
        off += size
    return out


def _cols_from_chips(a):
    return jnp.transpose(a, (1, 0, 2)).reshape(a.shape[1], -1)


def _cols_to_chips(a, parts):
    r = a.shape[0]
    return jnp.transpose(a.reshape(r, parts, -1), (1, 0, 2))


BIG = ("w_in", "w_out", "up", "down")
GATHER_RIDES = {("proj", 0): (("w_out", 0), ("up", 0)), ("mix_out", 0): (("down", 0),),
                ("ffn_fwd", 0): (("w_in", 1), ("w_out", 1), ("up", 1), ("down", 1))}
REDUCE_RIDES = {("ffn_up_a_dw", 0): (("up",), 1), ("ffn_down_dw", 0): (("w_in", "w_out"), 1),
                ("mix_out_dx", 0): (("down",), 1),
                ("proj_dx", 0): (("up",), 0), ("proj_dw_0", 0): (("down",), 0), ("proj_dw_1", 0): (("w_out",), 0)}


class _LocalWeights:
    def __init__(self, meta, win, wout, up_a, up_g, down, w2p, cw):
        self._meta, self._w = meta, {"win": win, "wout": wout, "up_a": up_a, "up_g": up_g, "down": down, "w2p": w2p,
                                     "cw": cw}

    def meta(self):
        return self._meta

    def get(self, kind, l):
        return self._w[kind][l]

    def mm(self, site, l, a, b, fn=None, **kw):
        return (fn or _mm)(a, b, name=site, **kw)

    def ffn_fwd(self, l, hn, wa, wg, ba, bg):
        return _ffn_fwd(hn, self.get("up_a", l), self.get("up_g", l), wa, wg, ba, bg, self.get("down", l),
                        name="ffn_fwd")[0]

    def grads_done(self, l, g, kinds):
        pass


class _ChipWeights:
    def __init__(self, w_in, w_out, ffn_up, ffn_down, meta_tokens, gla_gate_w2, ffn_conv_w):
        self.x, self.y, self.c = _place()
        self.q = 2 * self.x + self.y
        halves = lambda a: a.astype(BF16).reshape(2, a.shape[0] // 2, a.shape[1])
        self.own = {(k, l): halves(a[l]) for k, a in zip(BIG, (w_in, w_out, ffn_up, ffn_down)) for l in range(DEPTH)}
        self.landed, self.swapped, self.full, self.n_swaps = {}, {}, {}, 0
        self.sh_shapes = [meta_tokens.shape, gla_gate_w2.shape, ffn_conv_w.shape]
        self.own["small", 0] = _slab([meta_tokens, gla_gate_w2, ffn_conv_w], 16).reshape(2, -1, 128)
        first = [("w_in", 0), ("small", 0)]
        for key, arr in zip(first, _chip_exchange([self.own[k] for k in first], "bcast", name="gather_first")):
            self.landed[key] = arr
        sh = self._whole("small", 0).reshape(4, -1, 128)
        parts = [_unslab(sh[k], self.sh_shapes) for k in range(4)]
        self._meta = jnp.concatenate([p[0] for p in parts], axis=-1)
        self.w2 = jnp.concatenate([p[1] for p in parts], axis=-1)
        self.cw = jnp.concatenate([p[2] for p in parts], axis=-1)
        self.partial, self.slots = {}, {}

    def _whole(self, kind, l):
        if (kind, l) not in self.full:
            keys = [k for k in self.landed if k not in self.full]
            got = _pair_fill([self.landed[k] for k in keys], [self.own[k] for k in keys],
                             name=f"gather_fill_{self.n_swaps}")
            self.n_swaps += 1
            for k, buf in zip(keys, got):
                self.full[k] = buf.reshape(4, 2 * buf.shape[1], buf.shape[2])
        return self.full[kind, l]

    def meta(self):
        return self._meta

    def get(self, kind, l):
        if kind == "win":
            return _to_kernel_cols(_cols_from_chips(self._whole("w_in", l)))
        if kind == "wout":
            return self._whole("w_out", l).reshape(D_MODEL, D_MODEL)
        if kind == "up_a":
            return _cols_from_chips(self._whole("up", l)[0:2])
        if kind == "up_g":
            return _cols_from_chips(self._whole("up", l)[2:4])
        if kind == "down":
            return self._whole("down", l).reshape(D_FF, D_MODEL)
        if kind == "w2p":
            return jnp.pad(self.w2[l], ((0, 128 - GLA_RANK), (0, 0))).astype(BF16)
        return self.cw[l]

    def mm(self, site, l, a, b, fn=None, **kw):
        fn = fn or _mm
        if (site, l) in GATHER_RIDES:
            keys = GATHER_RIDES[site, l]
            out, got = fn(a, b, name=site, carry=([self.own[k] for k in keys], "bcast"), **kw)
            self.landed.update(zip(keys, got))
            return out
        if (site, l) in REDUCE_RIDES:
            kinds, gl = REDUCE_RIDES[site, l]
            keys = [(k, gl) for k in kinds]
            if all(k in self.partial and k not in self.slots for k in keys):
                out, got = fn(a, b, name=site, carry=([self.partial[k] for k in keys], "scatter"), **kw)
                self.slots.update(zip(keys, got))
                return out
        return fn(a, b, name=site, **kw)

    def ffn_fwd(self, l, hn, wa, wg, ba, bg):
        keys = GATHER_RIDES.get(("ffn_fwd", l), ())
        outs, got = _ffn_fwd(hn, self.get("up_a", l), self.get("up_g", l), wa, wg, ba, bg, self.get("down", l),
                             name="ffn_fwd", carry=([self.own[k] for k in keys], "bcast") if keys else None)
        self.landed.update(zip(keys, got))
        return outs

    def grads_done(self, l, g, kinds):
        split = lambda a: a.reshape(4, 2, a.shape[-2] // 2, a.shape[-1]) if a.ndim == 3 else \
            a.reshape(4, 2, a.shape[0] // 8, a.shape[1])
        src = {"w_in": lambda: g["w_in"][l], "w_out": lambda: g["w_out"][l],
               "up": lambda: g["up"][l], "down": lambda: g["down"][l]}
        big = {k: split(src[k]()) for k in kinds}
        from_sib = _pair_exchange([big[k] for k in kinds], name=f"grads_pair_exchange_{l}_{kinds[0]}")
        for k, theirs in zip(kinds, from_sib):
            self.partial[k, l] = _pair_sum(big[k], theirs, self.c, name=f"pair_sum_{k}_{l}")

    def reduce(self):
        keys = [(k, l) for l in range(DEPTH) for k in BIG]
        late = [k for k in keys if k not in self.slots]
        self.slots.update(zip(late, _chip_exchange([self.partial[k] for k in late], "scatter",
                                                   name="grads_chip_exchange")))
        half = {}
        for k in keys:
            own = lax.dynamic_index_in_dim(self.partial[k], self.q, 0, keepdims=False)
            half[k] = _sum_slots(own, self.slots[k], name=f"chip_sum_{k[0]}_{k[1]}")
        other = dict(zip(keys, _pair_swap([half[k] for k in keys], name="grads_pair_swap")))
        return [([half[k, l] for l in range(DEPTH)], [other[k, l] for l in range(DEPTH)]) for k in BIG]


def _local_step(x_rows, target_rows, wts, pre_mix_norm, gla_gate_b, ret_norm_w, gla_norm_w, post_mix_norm,
                pre_ffn_norm, ffn_conv_b, post_ffn_norm):
    d = D_MODEL
    lp = x_rows.shape[0] + FRONT + BACK
    row = lambda a, l: a[l][None, :]
    rtab = _ret_tables(lp)
    gtab = _gla_tables()
    h0 = jnp.concatenate([jnp.zeros((PADF, d), F32), wts.meta(), x_rows, jnp.zeros((BACK, d), F32)], axis=0)
    target = jnp.pad(target_rows, ((FRONT, BACK), (0, 0)))

    saved = []
    h = h0
    _, hn = _resid_norm(h0, None, None, row(pre_mix_norm, 0), name="norm_in")
    loss_local = dy = None
    for l in range(DEPTH):
        s = {"h_in": h, "hn": hn}
        s["proj"] = wts.mm("proj", l, hn, wts.get("win", l))
        s["o_ret"], s["st_ret"] = _retention(s["proj"], rtab, name="retention")
        s["o_gla"], s["st_gla"], s["gz"] = _gla(s["proj"], wts.get("w2p", l), row(gla_gate_b, l), gtab, name="gla")
        s["merged"] = _merge(s["o_ret"], s["o_gla"], s["proj"], row(ret_norm_w, l), row(gla_norm_w, l), name="merge")
        s["m"] = wts.mm("mix_out", l, s["merged"], wts.get("wout", l))
        s["h_mid"], s["hn2"] = _resid_norm(h, s["m"], row(post_mix_norm, l), row(pre_ffn_norm, l), name="resid_mix")
        cw_a, cw_g = wts.get("cw", l)[:, :D_FF], wts.get("cw", l)[:, D_FF:]
        cb_a, cb_g = ffn_conv_b[l][None, :D_FF], ffn_conv_b[l][None, D_FF:]
        s["conv"] = (cw_a, cw_g, cb_a, cb_g)
        s["ua"], s["ug"], s["act"], s["f"] = wts.ffn_fwd(l, s["hn2"], cw_a, cw_g, cb_a, cb_g)
        if l + 1 < DEPTH:
            h, hn = _resid_norm(s["h_mid"], s["f"], row(post_ffn_norm, l), row(pre_mix_norm, l + 1), name="resid_ffn")
        else:
            loss_local, dy, df_last, dw_last = _loss_head(s["h_mid"], s["f"], row(post_ffn_norm, l), target,
                                                          name="loss_head")
        saved.append(s)

    g = {k: [None] * DEPTH for k in ("pre_mix", "w_in", "w2", "gb", "ret_n", "gla_n", "w_out", "post_mix", "pre_ffn",
                                     "up", "cw", "cb", "down", "post_ffn")}
    dh_out, dhn_next = dy, None
    for l in reversed(range(DEPTH)):
        s = saved[l]
        cw_a, cw_g, cb_a, cb_g = s["conv"]
        if l + 1 < DEPTH:
            dh, df, g["pre_mix"][l + 1], g["post_ffn"][l] = _resid_norm_bwd(
                dh_out, dhn_next, saved[l + 1]["h_in"], s["f"], row(pre_mix_norm, l + 1), row(post_ffn_norm, l),
                name="resid_ffn_bwd")
        else:
            dh, df, g["post_ffn"][l] = dh_out, df_last, dw_last
        g["down"][l] = wts.mm("ffn_down_dw", l, s["act"], df, fn=_mm_tn, tn=512)
        du_a, du_g, dcw_a, dcw_g, dcb_a, dcb_g, dhn2 = _conv_act_bwd(
            s["ua"], s["ug"], df, wts.get("down", l), cw_a, cw_g, cb_a, cb_g, wts.get("up_a", l), wts.get("up_g", l),
            name="conv_act_bwd")
        g["cw"][l] = jnp.concatenate([dcw_a, dcw_g], axis=1)
        g["cb"][l] = jnp.concatenate([dcb_a, dcb_g], axis=1)[0]
        half_up = wts.mm("ffn_up_a_dw", l, s["hn2"], du_a, fn=_mm_tn, tn=D_FF // 2, blocks=(4, 0))
        g["up"][l] = _mm_tn(s["hn2"], du_g, tn=D_FF // 2, blocks=(4, 2), into=half_up, name="ffn_up_g_dw")
        dh, dm, g["pre_ffn"][l], g["post_mix"][l] = _resid_norm_bwd(
            dh, dhn2, s["h_mid"], s["m"], row(pre_ffn_norm, l), row(post_mix_norm, l), name="resid_mix_bwd")
        g["w_out"][l] = _mm_tn(s["merged"], dm, name="mix_out_dw")
        wts.grads_done(l, g, ("w_out", "up", "down"))
        dmerged = wts.mm("mix_out_dx", l, dm, wts.get("wout", l), nt=True)
        do_ret, do_gla, d_gate, g["ret_n"][l], g["gla_n"][l] = _merge_bwd(
            dmerged, s["o_ret"], s["o_gla"], s["proj"], row(ret_norm_w, l), row(gla_norm_w, l), name="merge_bwd")
        d_ret = _retention_bwd(s["proj"], do_ret, s["st_ret"], rtab, name="retention_bwd")
        d_gla, dw2, dgb = _gla_bwd(s["proj"], do_gla, s["st_gla"], s["gz"], wts.get("w2p", l), gtab, name="gla_bwd")
        g["w2"][l], g["gb"][l] = dw2[:GLA_RANK], dgb[0]
        pieces = (d_ret, d_gate, d_gla)
        g["w_in"][l] = _to_reference_chips(*[wts.mm(f"proj_dw_{i}", l, s["hn"], p, fn=_mm_tn)
                                             for i, p in enumerate(pieces)])
        win = wts.get("win", l)
        dhn_next = wts.mm("proj_dx", l, pieces, [win[:, 0:P_RET], win[:, P_RET:P_RET + P_GATE], win[:, P_RET + P_GATE:]],
                          fn=_mm_nt_sum)
        dh_out = dh
        wts.grads_done(l, g, ("w_in",))
    dh0, _, g["pre_mix"][0], _ = _resid_norm_bwd(dh_out, dhn_next, h0, None, row(pre_mix_norm, 0), None,
                                                 name="norm_in_bwd")
    return loss_local, dh0, g


def kernel(x, meta_tokens, pre_mix_norm, w_in, gla_gate_w2, gla_gate_b, ret_norm_w, gla_norm_w, w_out, post_mix_norm, pre_ffn_norm, ffn_up, ffn_conv_w, ffn_conv_b, ffn_down, post_ffn_norm, loss_target, m_meta_tokens, m_pre_mix_norm, m_w_in, m_gla_gate_w2, m_gla_gate_b, m_ret_norm_w, m_gla_norm_w, m_w_out, m_post_mix_norm, m_pre_ffn_norm, m_ffn_up, m_ffn_conv_w, m_ffn_conv_b, m_ffn_down, m_post_ffn_norm, v_meta_tokens, v_pre_mix_norm, v_w_in, v_gla_gate_w2, v_gla_gate_b, v_ret_norm_w, v_gla_norm_w, v_w_out, v_post_mix_norm, v_pre_ffn_norm, v_ffn_up, v_ffn_conv_w, v_ffn_conv_b, v_ffn_down, v_post_ffn_norm):
    xi, yi, ci = _place()
    chip = 2 * xi + yi
    seq = x.shape[1]
    d = D_MODEL
    wts = _ChipWeights(w_in, w_out, ffn_up, ffn_down, meta_tokens, gla_gate_w2, ffn_conv_w)
    loss_local, dh0, g = _local_step(x[0], loss_target[0], wts, pre_mix_norm, gla_gate_b, ret_norm_w, gla_norm_w,
                                     post_mix_norm, pre_ffn_norm, ffn_conv_b, post_ffn_norm)
    grad_x = dh0[FRONT:FRONT + seq][None]
    names = ("w_in", "w_out", "ffn_up", "ffn_down")
    big_halves = wts.reduce()

    small_full = [dh0[PADF:FRONT], jnp.stack(g["pre_mix"])[:, 0], jnp.stack(g["w2"]), jnp.stack(g["gb"]),
                  jnp.stack(g["ret_n"])[:, 0], jnp.stack(g["gla_n"])[:, 0], jnp.stack(g["post_mix"])[:, 0],
                  jnp.stack(g["pre_ffn"])[:, 0], jnp.stack(g["cw"]), jnp.stack(g["cb"]),
                  jnp.stack(g["post_ffn"])[:, 0]]
    small_sum = _unslab(_allreduce_small(_slab(small_full, 8), name="small_allreduce"), [a.shape for a in small_full])
    (g_meta, g_pre_mix, g_w2, g_gb, g_ret_n, g_gla_n, g_post_mix, g_pre_ffn, g_cw, g_cb, g_post_ffn) = small_sum
    g_meta = lax.dynamic_slice_in_dim(g_meta, chip * 256, 256, axis=1)
    g_w2 = lax.dynamic_slice_in_dim(g_w2, chip * 64, 64, axis=2)
    g_cw = lax.dynamic_slice_in_dim(g_cw, chip * 1408, 1408, axis=2)

    grads = [g_meta, g_pre_mix, None, g_w2, g_gb, g_ret_n, g_gla_n, None, g_post_mix, g_pre_ffn, None,
             g_cw, g_cb, None, g_post_ffn]
    ws = [meta_tokens, pre_mix_norm, w_in, gla_gate_w2, gla_gate_b, ret_norm_w, gla_norm_w, w_out, post_mix_norm,
          pre_ffn_norm, ffn_up, ffn_conv_w, ffn_conv_b, ffn_down, post_ffn_norm]
    ms = [m_meta_tokens, m_pre_mix_norm, m_w_in, m_gla_gate_w2, m_gla_gate_b, m_ret_norm_w, m_gla_norm_w, m_w_out,
          m_post_mix_norm, m_pre_ffn_norm, m_ffn_up, m_ffn_conv_w, m_ffn_conv_b, m_ffn_down, m_post_ffn_norm]
    vs = [v_meta_tokens, v_pre_mix_norm, v_w_in, v_gla_gate_w2, v_gla_gate_b, v_ret_norm_w, v_gla_norm_w, v_w_out,
          v_post_mix_norm, v_pre_ffn_norm, v_ffn_up, v_ffn_conv_w, v_ffn_conv_b, v_ffn_down, v_post_ffn_norm]
    big_idx = (2, 7, 10, 13)
    deltas, new_m, new_v = [None] * 15, [None] * 15, [None] * 15
    for i, nm, (mine, theirs) in zip(big_idx, names, big_halves):
        grads[i], deltas[i], new_m[i], new_v[i] = _adamw_halves(ws[i], ms[i], vs[i], mine, theirs, ci,
                                                                name=f"adamw_{nm}")
    small_idx = [i for i in range(15) if i not in big_idx]
    shapes = [ws[i].shape for i in small_idx]
    sd, sm, sv = _adamw(_slab([ws[i] for i in small_idx], 8), _slab([grads[i] for i in small_idx], 8),
                        _slab([ms[i] for i in small_idx], 8), _slab([vs[i] for i in small_idx], 8), name="adamw_small")
    for i, a, b, c_ in zip(small_idx, _unslab(sd, shapes), _unslab(sm, shapes), _unslab(sv, shapes)):
        deltas[i], new_m[i], new_v[i] = a, b, c_

    loss = lax.psum(loss_local, ("x", "y", "c"))
    return (loss, grad_x, *grads, *deltas, *new_m, *new_v)
```

```python
import functools
import math

import numpy as np
import jax
import jax.numpy as jnp
from jax import lax
from jax.experimental import pallas as pl
from jax.experimental.pallas import tpu as pltpu

F32 = jnp.float32
BF16 = jnp.bfloat16

D_MODEL = 1024
DEPTH = 2
N_META = 16
EPS = 1e-6
RET_HEADS = 4
RET_DK = 128
GLA_HEADS = 4
GLA_DK = 64
GLA_DV = 128
GLA_QK = GLA_HEADS * GLA_DK
GLA_V = GLA_HEADS * GLA_DV
GLA_RANK = 16
GLA_TAU = 16.0
D_FF = 2816
ROPE_BASE = 10000.0
IN_WIDTH = 3600
IN_PAD = 3840
C_RQ, C_RK, C_RV, C_RG, C_GR, C_GQ, C_GK, C_GV, C_GA = 0, 512, 1024, 1536, 2048, 2560, 2816, 3072, 3584
P_RET, P_GATE, P_GLA = 1536, 1024, 1280


def _to_kernel_cols(w):
    pad = jnp.zeros(w.shape[:-1] + (IN_PAD - IN_WIDTH,), w.dtype)
    return jnp.concatenate([w[..., 0:2048], w[..., 3072:3584], w[..., 2048:3072], w[..., 3584:3600], pad], axis=-1)


def _to_reference_chips(d_ret, d_gate, d_gla):
    segs = [(d_ret, 0, 0, 1536), (d_gate, 0, 1536, 512), (d_gla, 0, 2048, 1024), (d_gate, 512, 3072, 512),
            (d_gla, 1024, 3584, GLA_RANK)]
    per = IN_WIDTH // 4
    chips = []
    for j in range(4):
        lo, hi, parts = per * j, per * (j + 1), []
        for piece, p0, r0, width in segs:
            a, b = max(lo, r0), min(hi, r0 + width)
            if a < b:
                parts.append(piece[:, p0 + a - r0:p0 + b - r0])
        chips.append(jnp.concatenate(parts, axis=1))
    return jnp.stack(chips)

FRONT = 64
BACK = 64
PADF = FRONT - N_META
RET_CHUNK = 128
GLA_CHUNK = 64
GLA_SUB = 16
GLA_SUB2 = 4
BLK = 640

ADAM_LR, ADAM_B1, ADAM_B2, ADAM_EPS, ADAM_WD, ADAM_STEP = 0.001, 0.9, 0.999, 1e-08, 0.01, 10

VMEM_LIMIT = 56 * 2 ** 20
MM_VMEM_BUDGET = 40 * 2 ** 20
MESH = pl.DeviceIdType.MESH


def _cp(*sem):
    return pltpu.CompilerParams(dimension_semantics=sem, vmem_limit_bytes=VMEM_LIMIT)


def _tile(n, cands):
    for t in cands:
        if n % t == 0:
            return t
    raise ValueError(f"no tile for {n} in {cands}")


def _row_tile(n):
    return _tile(n, (640, 512, 320, 256, 128, 64))


def _mm(a, b, *, nt=False, add=None, out_dtype=F32, tn=None, name, carry=None):
    m, k = a.shape
    n = b.shape[0] if nt else b.shape[1]
    tm = _tile(m, (640, 320, 256, 128, 64))
    if tn is None:
        step_bytes = lambda t: 2 * (tm * k * a.dtype.itemsize + t * k * b.dtype.itemsize
                                    + tm * t * (jnp.dtype(out_dtype).itemsize + (4 if add is not None else 0)))
        tn = next(t for t in range(n, 0, -128) if n % t == 0 and (step_bytes(t) <= MM_VMEM_BUDGET or t == 128))
    dn = (((1,), (1,)), ((), ())) if nt else (((1,), (0,)), ((), ()))
    nj, ni = n // tn, m // tm
    n_in = 2 + (add is not None)
    c_arrs, c_mode = carry if carry is not None else ((), None)
    nc = len(c_arrs)

    def body(*refs):
        a_ref, b_ref = refs[:2]
        c_ref = refs[2] if add is not None else None
        o_ref = refs[n_in + nc]
        if nc:
            c_ins, c_outs = refs[n_in:n_in + nc], refs[n_in + nc + 1:n_in + 2 * nc + 1]
            ssem, rsem = refs[n_in + 2 * nc + 1:]
            j, i = pl.program_id(0), pl.program_id(1)

            @pl.when((j == 0) & (i == 0))
            def _():
                for cp in _chip_copies(c_ins, c_outs, ssem, rsem, c_mode)[0]:
                    cp.start()
        r = lax.dot_general(a_ref[...].astype(BF16), b_ref[...].astype(BF16), dn, preferred_element_type=F32)
        if add is not None:
            r = r + c_ref[...]
        o_ref[...] = r.astype(o_ref.dtype)
        if nc:
            @pl.when((j == nj - 1) & (i == ni - 1))
            def _():
                _chip_wait(*_chip_copies(c_ins, c_outs, ssem, rsem, c_mode))

    b_spec = pl.BlockSpec((tn, k), lambda j, i: (j, 0)) if nt else pl.BlockSpec((k, tn), lambda j, i: (0, j))
    in_specs = [pl.BlockSpec((tm, k), lambda j, i: (i, 0)), b_spec]
    args = [a, b]
    if add is not None:
        in_specs.append(pl.BlockSpec((tm, tn), lambda j, i: (i, j)))
        args.append(add)
    out_shape = jax.ShapeDtypeStruct((m, n), out_dtype)
    out_spec = pl.BlockSpec((tm, tn), lambda j, i: (i, j))
    if not nc:
        return pl.pallas_call(
            body, out_shape=out_shape, grid=(nj, ni), in_specs=in_specs, out_specs=out_spec,
            compiler_params=_cp("parallel", "parallel"), name=name)(*args)
    sem = pltpu.SemaphoreType.DMA
    outs = pl.pallas_call(
        body, out_shape=(out_shape,) + tuple(_landing_shape(x, c_mode) for x in c_arrs), grid=(nj, ni),
        in_specs=in_specs + [ANY] * nc, out_specs=(out_spec,) + (ANY,) * nc,
        scratch_shapes=[sem((nc, 3)), sem((nc, 3))],
        compiler_params=_cp("arbitrary", "arbitrary"), name=name)(*args, *c_arrs)
    return outs[0], list(outs[1:])


def _call_with_carry(body, *, out_shape, grid, in_specs, out_specs, args, semantics, carry, name, aliases=None):
    if carry is None:
        return pl.pallas_call(body, out_shape=out_shape, grid=grid, in_specs=in_specs, out_specs=out_specs,
                              input_output_aliases=aliases or {}, compiler_params=_cp(*semantics), name=name)(*args)
    c_arrs, c_mode = carry
    n_in, nc = len(args), len(c_arrs)

    def carried(*refs):
        c_ins, c_outs = refs[n_in:n_in + nc], refs[n_in + nc + 1:n_in + 2 * nc + 1]
        ssem, rsem = refs[n_in + 2 * nc + 1:]
        ids = [pl.program_id(d) for d in range(len(grid))]
        first = functools.reduce(lambda u, v: u & v, [i == 0 for i in ids])
        last = functools.reduce(lambda u, v: u & v, [i == g - 1 for i, g in zip(ids, grid)])

        @pl.when(first)
        def _():
            for cp in _chip_copies(c_ins, c_outs, ssem, rsem, c_mode)[0]:
                cp.start()
        body(*refs[:n_in], refs[n_in + nc])

        @pl.when(last)
        def _():
            _chip_wait(*_chip_copies(c_ins, c_outs, ssem, rsem, c_mode))

    sem = pltpu.SemaphoreType.DMA
    outs = pl.pallas_call(
        carried, out_shape=(out_shape,) + tuple(_landing_shape(x, c_mode) for x in c_arrs), grid=grid,
        in_specs=list(in_specs) + [ANY] * nc, out_specs=(out_specs,) + (ANY,) * nc,
        scratch_shapes=[sem((nc, 3)), sem((nc, 3))], input_output_aliases=aliases or {},
        compiler_params=_cp(*(("arbitrary",) * len(grid))), name=name)(*args, *c_arrs)
    return outs[0], list(outs[1:])


def _mm_nt_sum(a_list, b_list, *, name, carry=None):
    m, n = a_list[0].shape[0], b_list[0].shape[0]
    tm = _tile(m, (640, 320, 256, 128, 64))
    np_ = len(a_list)

    def body(*refs):
        acc = None
        for a_ref, b_ref in zip(refs[:np_], refs[np_:2 * np_]):
            r = lax.dot_general(a_ref[...].astype(BF16), b_ref[...].astype(BF16), (((1,), (1,)), ((), ())),
                                preferred_element_type=F32)
            acc = r if acc is None else acc + r
        refs[2 * np_][...] = acc

    return _call_with_carry(
        body, out_shape=jax.ShapeDtypeStruct((m, n), F32), grid=(m // tm,),
        in_specs=[pl.BlockSpec((tm, a.shape[1]), lambda i: (i, 0)) for a in a_list]
        + [pl.BlockSpec(b.shape, lambda i: (0, 0)) for b in b_list],
        out_specs=pl.BlockSpec((tm, n), lambda i: (i, 0)), args=[*a_list, *b_list], semantics=("parallel",),
        carry=carry, name=name)


def _mm_tn(a, b, *, tn=None, blocks=None, into=None, name, carry=None):
    m, k = a.shape
    n = b.shape[1]
    tm = _tile(m, (1664, 640, 320, 256, 128, 64))
    tn = n if tn is None else tn
    if blocks is not None:
        total, first = blocks
        out_shape = jax.ShapeDtypeStruct((total, k, tn), F32)
        out_spec = pl.BlockSpec((None, k, tn), lambda j, i: (first + j, 0, 0))
    else:
        out_shape = jax.ShapeDtypeStruct((k, n), F32)
        out_spec = pl.BlockSpec((k, tn), lambda j, i: (0, j))

    def body(a_ref, b_ref, *rest):
        o_ref = rest[-1]

        @pl.when(pl.program_id(1) == 0)
        def _():
            o_ref[...] = jnp.zeros_like(o_ref)
        o_ref[...] += lax.dot_general(a_ref[...].astype(BF16), b_ref[...].astype(BF16),
                                      (((0,), (0,)), ((), ())), preferred_element_type=F32)

    in_specs = [pl.BlockSpec((tm, k), lambda j, i: (i, 0)), pl.BlockSpec((tm, tn), lambda j, i: (i, j))]
    args, alias = [a, b], {}
    if into is not None:
        in_specs.append(pl.BlockSpec(memory_space=pl.ANY))
        args.append(into)
        alias = {2: 0}
    return _call_with_carry(body, out_shape=out_shape, grid=(n // tn, m // tm), in_specs=in_specs, out_specs=out_spec,
                            args=args, semantics=("parallel", "arbitrary"), carry=carry, name=name, aliases=alias)


def _rms(x, w):
    r = lax.rsqrt(jnp.mean(x * x, axis=-1, keepdims=True) + EPS)
    return x * r * w


def _rms_bwd(x, w, dy):
    r = lax.rsqrt(jnp.mean(x * x, axis=-1, keepdims=True) + EPS)
    xh = x * r
    dxh = dy * w
    dx = r * (dxh - xh * jnp.mean(dxh * xh, axis=-1, keepdims=True))
    return dx, jnp.sum(dy * xh, axis=0, keepdims=True)


def _resid_norm(h, t, w_post, w_next, *, name):
    lp, d = h.shape
    tm = _row_tile(lp)
    has_t = t is not None

    def body(*refs):
        if has_t:
            h_ref, t_ref, wp_ref, wn_ref, ho_ref, hn_ref = refs
            hv = h_ref[...] + _rms(t_ref[...], wp_ref[...])
            ho_ref[...] = hv
        else:
            h_ref, wn_ref, hn_ref = refs
            hv = h_ref[...]
        hn_ref[...] = _rms(hv, wn_ref[...]).astype(BF16)

    row = pl.BlockSpec((tm, d), lambda i: (i, 0))
    vec = pl.BlockSpec((1, d), lambda i: (0, 0))
    if has_t:
        return pl.pallas_call(
            body, out_shape=(jax.ShapeDtypeStruct((lp, d), F32), jax.ShapeDtypeStruct((lp, d), BF16)),
            grid=(lp // tm,), in_specs=[row, row, vec, vec], out_specs=(row, row),
            compiler_params=_cp("parallel"), name=name)(h, t, w_post, w_next)
    return h, pl.pallas_call(
        body, out_shape=jax.ShapeDtypeStruct((lp, d), BF16), grid=(lp // tm,), in_specs=[row, vec],
        out_specs=row, compiler_params=_cp("parallel"), name=name)(h, w_next)


def _resid_norm_bwd(dh_out, dhn, h_new, t, w_next, w_post, *, name):
    lp, d = h_new.shape if h_new is not None else t.shape
    tm = _row_tile(lp)
    has_n = dhn is not None
    has_t = t is not None

    def body(*refs):
        refs = list(refs)
        dho_ref = refs.pop(0)
        if has_n:
            dhn_ref, hn_ref, wn_ref = refs.pop(0), refs.pop(0), refs.pop(0)
        if has_t:
            t_ref, wp_ref = refs.pop(0), refs.pop(0)
        dh_ref = refs.pop(0) if has_n else None
        dt_ref = refs.pop(0) if has_t else None
        dwn_ref = refs.pop(0) if has_n else None
        dwp_ref = refs.pop(0) if has_t else None
        first = pl.program_id(0) == 0
        dh = dho_ref[...]
        if has_n:
            dx, dwn = _rms_bwd(hn_ref[...], wn_ref[...], dhn_ref[...])
            dh = dh + dx
            dh_ref[...] = dh

            @pl.when(first)
            def _():
                dwn_ref[...] = jnp.zeros_like(dwn_ref)
            dwn_ref[...] += dwn
        if has_t:
            dt, dwp = _rms_bwd(t_ref[...], wp_ref[...], dh)
            dt_ref[...] = dt.astype(BF16)

            @pl.when(first)
            def _():
                dwp_ref[...] = jnp.zeros_like(dwp_ref)
            dwp_ref[...] += dwp

    row = pl.BlockSpec((tm, d), lambda i: (i, 0))
    vec = pl.BlockSpec((1, d), lambda i: (0, 0))
    args, in_specs, out_shape, out_specs = [dh_out], [row], [], []
    if has_n:
        args += [dhn, h_new, w_next]
        in_specs += [row, row, vec]
    if has_t:
        args += [t, w_post]
        in_specs += [row, vec]
    if has_n:
        out_shape.append(jax.ShapeDtypeStruct((lp, d), F32)); out_specs.append(row)
    if has_t:
        out_shape.append(jax.ShapeDtypeStruct((lp, d), BF16)); out_specs.append(row)
    if has_n:
        out_shape.append(jax.ShapeDtypeStruct((1, d), F32)); out_specs.append(vec)
    if has_t:
        out_shape.append(jax.ShapeDtypeStruct((1, d), F32)); out_specs.append(vec)
    outs = list(pl.pallas_call(body, out_shape=tuple(out_shape), grid=(lp // tm,), in_specs=in_specs,
                               out_specs=tuple(out_specs), compiler_params=_cp("arbitrary"), name=name)(*args))
    dh = outs.pop(0) if has_n else dh_out
    dt = outs.pop(0) if has_t else None
    dwn = outs.pop(0) if has_n else None
    dwp = outs.pop(0) if has_t else None
    return dh, dt, dwn, dwp


def _loss_head(h, f, w_post, target, *, name):
    lp, d = h.shape
    tm = _row_tile(lp)

    def body(h_ref, f_ref, w_ref, t_ref, loss_ref, dy_ref, df_ref, dw_ref):
        i = pl.program_id(0)
        f, w = f_ref[...], w_ref[...]
        y = h_ref[...] + _rms(f, w)
        rows = i * tm + lax.broadcasted_iota(jnp.int32, (tm, 1), 0)
        tok = (rows >= FRONT) & (rows < lp - BACK)
        err = jnp.where(tok, y - t_ref[...], 0.0)
        dy = err * (1.0 / d)
        dy_ref[...] = dy
        df, dw = _rms_bwd(f, w, dy)
        df_ref[...] = df.astype(BF16)

        @pl.when(i == 0)
        def _():
            loss_ref[...] = jnp.zeros_like(loss_ref)
            dw_ref[...] = jnp.zeros_like(dw_ref)
        part = jnp.sum(jnp.sum(err * err, axis=1, keepdims=True), axis=0, keepdims=True) * (0.5 / d)
        loss_ref[...] += jnp.broadcast_to(part, loss_ref.shape)
        dw_ref[...] += dw

    row = pl.BlockSpec((tm, d), lambda i: (i, 0))
    vec = pl.BlockSpec((1, d), lambda i: (0, 0))
    loss, dy, df, dw = pl.pallas_call(
        body, out_shape=(jax.ShapeDtypeStruct((8, 128), F32), jax.ShapeDtypeStruct((lp, d), F32),
                         jax.ShapeDtypeStruct((lp, d), BF16), jax.ShapeDtypeStruct((1, d), F32)),
        grid=(lp // tm,), in_specs=[row, row, vec, row],
        out_specs=(pl.BlockSpec((8, 128), lambda i: (0, 0)), row, row, vec),
        compiler_params=_cp("arbitrary"), name=name)(h, f, w_post, target)
    return loss[0, 0], dy, df, dw


_GELU_C = math.sqrt(2.0 / math.pi)


def _gelu_and_grad(a):
    a2 = a * a
    t = jnp.tanh(a * (_GELU_C + (_GELU_C * 0.044715) * a2))
    ha = 0.5 * a
    h1 = 0.5 + 0.5 * t
    return a * h1, h1 + ha * (1.0 - t * t) * (_GELU_C + (3.0 * _GELU_C * 0.044715) * a2)


def _gelu(a):
    t = jnp.tanh(a * (_GELU_C + (_GELU_C * 0.044715) * (a * a)))
    return a * (0.5 + 0.5 * t)


def _conv3(parts, n, w, b):
    xx = jnp.concatenate(parts, axis=0)
    return b + xx[8:8 + n] * w[2:3] + pltpu.roll(xx, 1, 0)[8:8 + n] * w[1:2] + pltpu.roll(xx, 2, 0)[8:8 + n] * w[0:1]


def _conv_act(ua, ug, wa, wg, ba, bg, *, name):
    lp, n = ua.shape
    tm = _row_tile(lp)
    tc = _tile(n, (256, 128))
    nb8 = tm // 8

    def body(ua_ref, uap_ref, ug_ref, ugp_ref, wa_ref, wg_ref, ba_ref, bg_ref, o_ref):
        i = pl.program_id(0)
        ca = _conv3([uap_ref[...], ua_ref[...]], tm, wa_ref[...], ba_ref[...])
        cg = _conv3([ugp_ref[...], ug_ref[...]], tm, wg_ref[...], bg_ref[...])
        rows = i * tm + lax.broadcasted_iota(jnp.int32, (tm, 1), 0)
        ok = (rows >= PADF) & (rows < lp - BACK)
        o_ref[...] = jnp.where(ok, _gelu(ca) * cg, 0.0).astype(BF16)

    cur = pl.BlockSpec((tm, tc), lambda i, j: (i, j))
    prev = pl.BlockSpec((8, tc), lambda i, j: (jnp.maximum(i * nb8 - 1, 0), j))
    w3 = pl.BlockSpec((3, tc), lambda i, j: (0, j))
    b1 = pl.BlockSpec((1, tc), lambda i, j: (0, j))
    return pl.pallas_call(
        body, out_shape=jax.ShapeDtypeStruct((lp, n), BF16), grid=(lp // tm, n // tc),
        in_specs=[cur, prev, cur, prev, w3, w3, b1, b1], out_specs=cur,
        compiler_params=_cp("parallel", "parallel"), name=name)(ua, ua, ug, ug, wa, wg, ba, bg)


def _conv_act_down(ua, ug, wa, wg, ba, bg, down, *, name):
    lp, n = ua.shape
    d = down.shape[1]
    tm = _tile(lp, (320, 256, 128, 64))
    tc = _tile(n, (256, 128))
    nb8 = tm // 8

    def body(ua_ref, uap_ref, ug_ref, ugp_ref, wa_ref, wg_ref, ba_ref, bg_ref, dn_ref, act_ref, f_ref):
        i = pl.program_id(0)
        rows = i * tm + lax.broadcasted_iota(jnp.int32, (tm, 1), 0)
        ok = (rows >= PADF) & (rows < lp - BACK)
        acc = None
        for j in range(n // tc):
            cs = slice(tc * j, tc * j + tc)
            ca = _conv3([uap_ref[:, cs], ua_ref[:, cs]], tm, wa_ref[:, cs], ba_ref[:, cs])
            cg = _conv3([ugp_ref[:, cs], ug_ref[:, cs]], tm, wg_ref[:, cs], bg_ref[:, cs])
            act = jnp.where(ok, _gelu(ca) * cg, 0.0).astype(BF16)
            act_ref[:, cs] = act
            part = _dot(act, dn_ref[cs, :])
            acc = part if acc is None else acc + part
        f_ref[...] = acc

    cur = pl.BlockSpec((tm, n), lambda i: (i, 0))
    prev = pl.BlockSpec((8, n), lambda i: (jnp.maximum(i * nb8 - 1, 0), 0))
    w3 = pl.BlockSpec((3, n), lambda i: (0, 0))
    b1 = pl.BlockSpec((1, n), lambda i: (0, 0))
    return pl.pallas_call(
        body, out_shape=(jax.ShapeDtypeStruct((lp, n), BF16), jax.ShapeDtypeStruct((lp, d), F32)),
        grid=(lp // tm,),
        in_specs=[cur, prev, cur, prev, w3, w3, b1, b1, pl.BlockSpec(down.shape, lambda i: (0, 0))],
        out_specs=(cur, pl.BlockSpec((tm, d), lambda i: (i, 0))),
        compiler_params=_cp("parallel"), name=name)(ua, ua, ug, ug, wa, wg, ba, bg, down)


def _ffn_fwd(h, m, w_post, w_next, up_a, up_g, wa, wg, ba, bg, down, *, name, carry=None):
    lp, d = h.shape
    n = up_a.shape[1]
    tm = _tile(lp, (320, 256, 128, 64))
    tc = _tile(n, (256, 128))
    nchunks = n // tc
    c_arrs, c_mode = carry if carry is not None else ((), None)
    nc = len(c_arrs)
    steps = lp // tm
    n_out = 6

    def body(h_ref, hp_ref, m_ref, mp_ref, wp_ref, wn_ref, upa_ref, upg_ref, wa_ref, wg_ref, ba_ref, bg_ref, dn_ref,
             *rest):
        c_ins = rest[:nc]
        hmid_ref, hn_ref, ua_ref, ug_ref, act_ref, f_ref = rest[nc:nc + n_out]
        c_outs = rest[nc + n_out:2 * nc + n_out]
        i = pl.program_id(0)
        if nc:
            ssem, rsem = rest[2 * nc + n_out:]

            @pl.when(i == 0)
            def _():
                for cp in _chip_copies(c_ins, c_outs, ssem, rsem, c_mode)[0]:
                    cp.start()
        rows = i * tm + lax.broadcasted_iota(jnp.int32, (tm, 1), 0)
        ok = (rows >= PADF) & (rows < lp - BACK)
        hv = (jnp.concatenate([hp_ref[...], h_ref[...]], axis=0)
              + _rms(jnp.concatenate([mp_ref[...], m_ref[...]], axis=0), wp_ref[...]))
        x = _rms(hv, wn_ref[...]).astype(BF16)
        hmid_ref[...] = hv[16:]
        hn_ref[...] = x[16:]
        u_of = lambda j: (_dot(x, upa_ref[:, tc * j:tc * j + tc]), _dot(x, upg_ref[:, tc * j:tc * j + tc]))
        u_next = u_of(0)
        acc = None
        for j in range(nchunks):
            cs = slice(tc * j, tc * j + tc)
            ua, ug = u_next
            if j + 1 < nchunks:
                u_next = u_of(j + 1)
            ua_ref[:, cs] = ua[16:]
            ug_ref[:, cs] = ug[16:]
            ca = _conv3([ua[8:]], tm, wa_ref[:, cs], ba_ref[:, cs])
            cg = _conv3([ug[8:]], tm, wg_ref[:, cs], bg_ref[:, cs])
            act = jnp.where(ok, _gelu(ca) * cg, 0.0).astype(BF16)
            act_ref[:, cs] = act
            part = _dot(act, dn_ref[cs, :])
            acc = part if acc is None else acc + part
        f_ref[...] = acc
        if nc:
            @pl.when(i == steps - 1)
            def _():
                _chip_wait(*_chip_copies(c_ins, c_outs, ssem, rsem, c_mode))

    whole = pl.BlockSpec(memory_space=pltpu.VMEM)
    wide = pl.BlockSpec((tm, n), lambda i: (i, 0))
    w3 = pl.BlockSpec((3, n), lambda i: (0, 0))
    b1 = pl.BlockSpec((1, n), lambda i: (0, 0))
    sem = pltpu.SemaphoreType.DMA
    row = pl.BlockSpec((tm, d), lambda i: (i, 0))
    prev16 = pl.BlockSpec((16, d), lambda i: (jnp.maximum(i * (tm // 16) - 1, 0), 0))
    vec = pl.BlockSpec((1, d), lambda i: (0, 0))
    outs = pl.pallas_call(
        body,
        out_shape=(jax.ShapeDtypeStruct((lp, d), F32), jax.ShapeDtypeStruct((lp, d), BF16),
                   jax.ShapeDtypeStruct((lp, n), F32), jax.ShapeDtypeStruct((lp, n), F32),
                   jax.ShapeDtypeStruct((lp, n), BF16), jax.ShapeDtypeStruct((lp, d), F32))
        + tuple(_landing_shape(a, c_mode) for a in c_arrs),
        grid=(steps,),
        in_specs=[row, prev16, row, prev16, vec, vec, whole, whole, w3, w3, b1, b1, whole] + [ANY] * nc,
        out_specs=(row, row, wide, wide, wide, row) + (ANY,) * nc,
        scratch_shapes=[sem((nc, 3)), sem((nc, 3))] if nc else [],
        compiler_params=_cp("arbitrary"), name=name)(h, h, m, m, w_post, w_next, up_a, up_g, wa, wg, ba, bg, down,
                                                     *c_arrs)
    return outs[:n_out], list(outs[n_out:])


def _conv_act_bwd(ua, ug, df, down, wa, wg, ba, bg, up_a, up_g, *, name):
    lp, n = ua.shape
    d = up_a.shape[0]
    tm = _tile(lp, (320, 256, 128, 64))
    tc = _tile(n, (256, 128))
    nb8 = tm // 8
    last8 = lp // 8 - 1
    last16 = lp // 16 - 1
    ext = tm + 8

    def body(ua_ref, uap_ref, uan_ref, ug_ref, ugp_ref, ugn_ref, df_ref, dfn_ref, dn_ref, wa_ref, wg_ref, ba_ref,
             bg_ref, upa_ref, upg_ref, dua_ref, dug_ref, dwa_ref, dwg_ref, dba_ref, dbg_ref, dhn_ref):
        i = pl.program_id(0)
        df_ext = jnp.concatenate([df_ref[...], dfn_ref[...]], axis=0)

        @pl.when(i == 0)
        def _():
            dwa_ref[...] = jnp.zeros_like(dwa_ref)
            dwg_ref[...] = jnp.zeros_like(dwg_ref)
            dba_ref[...] = jnp.zeros_like(dba_ref)
            dbg_ref[...] = jnp.zeros_like(dbg_ref)
        rows = i * tm + lax.broadcasted_iota(jnp.int32, (ext, 1), 0)
        ok = (rows >= PADF) & (rows < lp - BACK)

        def conv(parts, w, b):
            xx = jnp.concatenate(parts, axis=0)
            x, x1, x2 = xx[8:8 + ext], pltpu.roll(xx, 1, 0)[8:8 + ext], pltpu.roll(xx, 2, 0)[8:8 + ext]
            return b + x * w[2:3] + x1 * w[1:2] + x2 * w[0:1], x, x1, x2

        def back(dc, w):
            return (dc[:tm] * w[2:3] + pltpu.roll(dc, ext - 1, 0)[:tm] * w[1:2]
                    + pltpu.roll(dc, ext - 2, 0)[:tm] * w[0:1])

        def wsum(dw_ref, db_ref, cs, dc, x, x1, x2):
            dd = dc[:tm]
            s = lambda v: jnp.sum(v, axis=0, keepdims=True)
            dw_ref[0:1, cs] += s(dd * x2[:tm])
            dw_ref[1:2, cs] += s(dd * x1[:tm])
            dw_ref[2:3, cs] += s(dd * x[:tm])
            db_ref[:, cs] += s(dd)

        acc = None
        nchunks = n // tc
        dact_of = lambda j: _dot_nt(df_ext, dn_ref[tc * j:tc * j + tc, :])[:ext]
        dact_next = dact_of(0)
        for j in range(nchunks):
            cs = slice(tc * j, tc * j + tc)
            dact_cur = dact_next
            if j + 1 < nchunks:
                dact_next = dact_of(j + 1)
            wa, wg = wa_ref[:, cs], wg_ref[:, cs]
            ca, xa, xa1, xa2 = conv([uap_ref[:, cs], ua_ref[:, cs], uan_ref[:, cs]], wa, ba_ref[:, cs])
            cg, xg, xg1, xg2 = conv([ugp_ref[:, cs], ug_ref[:, cs], ugn_ref[:, cs]], wg, bg_ref[:, cs])
            dact_e = jnp.where(ok, dact_cur, 0.0)
            gel, gel_d = _gelu_and_grad(ca)
            dca = dact_e * cg * gel_d
            dcg = dact_e * gel
            du_a, du_g = back(dca, wa).astype(BF16), back(dcg, wg).astype(BF16)
            dua_ref[:, cs] = du_a
            dug_ref[:, cs] = du_g
            wsum(dwa_ref, dba_ref, cs, dca, xa, xa1, xa2)
            wsum(dwg_ref, dbg_ref, cs, dcg, xg, xg1, xg2)
            part = _dot_nt(du_a, upa_ref[:, cs]) + _dot_nt(du_g, upg_ref[:, cs])
            acc = part if acc is None else acc + part
        dhn_ref[...] = acc

    cur = pl.BlockSpec((tm, n), lambda i: (i, 0))
    prev = pl.BlockSpec((8, n), lambda i: (jnp.maximum(i * nb8 - 1, 0), 0))
    nxt = pl.BlockSpec((8, n), lambda i: (jnp.minimum((i + 1) * nb8, last8), 0))
    w3 = pl.BlockSpec((3, n), lambda i: (0, 0))
    b1 = pl.BlockSpec((1, n), lambda i: (0, 0))
    whole = pl.BlockSpec(memory_space=pltpu.VMEM)
    return pl.pallas_call(
        body,
        out_shape=(jax.ShapeDtypeStruct((lp, n), BF16), jax.ShapeDtypeStruct((lp, n), BF16),
                   jax.ShapeDtypeStruct((3, n), F32), jax.ShapeDtypeStruct((3, n), F32),
                   jax.ShapeDtypeStruct((1, n), F32), jax.ShapeDtypeStruct((1, n), F32),
                   jax.ShapeDtypeStruct((lp, d), F32)),
        grid=(lp // tm,),
        in_specs=[cur, prev, nxt, cur, prev, nxt, pl.BlockSpec((tm, d), lambda i: (i, 0)),
                  pl.BlockSpec((16, d), lambda i: (jnp.minimum((i + 1) * (tm // 16), last16), 0)), whole,
                  w3, w3, b1, b1, whole, whole],
        out_specs=(cur, cur, w3, w3, b1, b1, pl.BlockSpec((tm, d), lambda i: (i, 0))),
        compiler_params=_cp("arbitrary"), name=name)(ua, ua, ua, ug, ug, ug, df, df, down, wa, wg, ba, bg, up_a, up_g)


def _sigmoid(x):
    return 1.0 / (1.0 + jnp.exp(-x))


def _merge(o_ret, o_gla, proj, w_ret, w_gla, *, name):
    lp = o_ret.shape[0]
    tm = _row_tile(lp)

    def body(or_ref, og_ref, rg_ref, gr_ref, wr_ref, wg_ref, m_ref):
        oret, ogla = or_ref[...], og_ref[...]
        yr, yg = [], []
        for h in range(4):
            hs = slice(128 * h, 128 * h + 128)
            o = oret[:, hs]
            xc = o - jnp.mean(o, axis=-1, keepdims=True)
            yr.append(xc * lax.rsqrt(jnp.mean(xc * xc, axis=-1, keepdims=True) + EPS))
            o = ogla[:, hs]
            yg.append(o * lax.rsqrt(jnp.mean(o * o, axis=-1, keepdims=True) + EPS))
        rg, gr = rg_ref[...], gr_ref[...]
        m_ref[:, 0:512] = (jnp.concatenate(yr, axis=1) * wr_ref[...] * (rg * _sigmoid(rg))).astype(BF16)
        m_ref[:, 512:1024] = (jnp.concatenate(yg, axis=1) * wg_ref[...] * (gr * _sigmoid(gr))).astype(BF16)

    row = pl.BlockSpec((tm, 512), lambda i: (i, 0))
    vec = pl.BlockSpec((1, 512), lambda i: (0, 0))
    return pl.pallas_call(
        body, out_shape=jax.ShapeDtypeStruct((lp, 1024), BF16), grid=(lp // tm,),
        in_specs=[row, row, pl.BlockSpec((tm, 512), lambda i: (i, C_RG // 512)),
                  pl.BlockSpec((tm, 512), lambda i: (i, C_GR // 512)), vec, vec],
        out_specs=pl.BlockSpec((tm, 1024), lambda i: (i, 0)),
        compiler_params=_cp("parallel"), name=name)(o_ret, o_gla, proj, proj, w_ret, w_gla)


def _merge_bwd(dm, o_ret, o_gla, proj, w_ret, w_gla, *, name):
    lp = o_ret.shape[0]
    tm = _row_tile(lp)

    def body(dm_ref, or_ref, og_ref, rg_ref, gr_ref, wr_ref, wg_ref, dor_ref, dog_ref, dgate_ref, dwr_ref, dwg_ref):
        @pl.when(pl.program_id(0) == 0)
        def _():
            dwr_ref[...] = jnp.zeros_like(dwr_ref)
            dwg_ref[...] = jnp.zeros_like(dwg_ref)

        def group(d, o_all, gate, w, center):
            sg = _sigmoid(gate)
            s = gate * sg
            ds = sg * (1.0 + gate * (1.0 - sg))
            xh, rr = [], []
            for h in range(4):
                o = o_all[:, 128 * h:128 * h + 128]
                if center:
                    o = o - jnp.mean(o, axis=-1, keepdims=True)
                r = lax.rsqrt(jnp.mean(o * o, axis=-1, keepdims=True) + EPS)
                xh.append(o * r)
                rr.append(r)
            xh_all = jnp.concatenate(xh, axis=1)
            dgate = d * xh_all * w * ds
            dw = jnp.sum(d * xh_all * s, axis=0, keepdims=True)
            dxh_all = d * w * s
            do = []
            for h in range(4):
                dxh = dxh_all[:, 128 * h:128 * h + 128]
                t = dxh - xh[h] * jnp.mean(dxh * xh[h], axis=-1, keepdims=True)
                if center:
                    t = t - jnp.mean(dxh, axis=-1, keepdims=True)
                do.append(rr[h] * t)
            return jnp.concatenate(do, axis=1), dgate, dw

        dmv = dm_ref[...]
        do, dg, dw = group(dmv[:, 0:512], or_ref[...], rg_ref[...], wr_ref[...], True)
        dor_ref[...] = do
        dgate_ref[:, 0:512] = dg.astype(BF16)
        dwr_ref[...] += dw
        do, dg, dw = group(dmv[:, 512:1024], og_ref[...], gr_ref[...], wg_ref[...], False)
        dog_ref[...] = do
        dgate_ref[:, 512:1024] = dg.astype(BF16)
        dwg_ref[...] += dw

    row = pl.BlockSpec((tm, 512), lambda i: (i, 0))
    vec = pl.BlockSpec((1, 512), lambda i: (0, 0))
    return pl.pallas_call(
        body,
        out_shape=(jax.ShapeDtypeStruct((lp, 512), F32), jax.ShapeDtypeStruct((lp, 512), F32),
                   jax.ShapeDtypeStruct((lp, P_GATE), BF16),
                   jax.ShapeDtypeStruct((1, 512), F32), jax.ShapeDtypeStruct((1, 512), F32)),
        grid=(lp // tm,),
        in_specs=[pl.BlockSpec((tm, 1024), lambda i: (i, 0)), row, row,
                  pl.BlockSpec((tm, 512), lambda i: (i, C_RG // 512)),
                  pl.BlockSpec((tm, 512), lambda i: (i, C_GR // 512)), vec, vec],
        out_specs=(row, row, pl.BlockSpec((tm, P_GATE), lambda i: (i, 0)), vec, vec),
        compiler_params=_cp("arbitrary"), name=name)(dm, o_ret, o_gla, proj, proj, w_ret, w_gla)


def _dot(a, b):
    return lax.dot_general(a, b, (((1,), (0,)), ((), ())), preferred_element_type=F32)


def _dot_nt(a, b):
    return lax.dot_general(a, b, (((1,), (1,)), ((), ())), preferred_element_type=F32)


def _dot_tn(a, b):
    return lax.dot_general(a, b, (((0,), (0,)), ((), ())), preferred_element_type=F32)


def _ret_tables(lp):
    cr = RET_CHUNK
    pos = np.arange(lp, dtype=np.float32) - np.float32(PADF)
    half = RET_DK // 2
    inv = (np.float32(ROPE_BASE) ** (-np.arange(half, dtype=np.float32) / np.float32(half))).astype(np.float32)
    ang = (pos[:, None] * inv[None, :]).astype(np.float32)
    c, s = np.cos(ang).astype(np.float32), np.sin(ang).astype(np.float32)
    rope_c = jnp.asarray(np.concatenate([c, c], axis=1))
    rope_s = jnp.asarray(np.concatenate([-s, s], axis=1))
    log_g = np.log(1.0 - 2.0 ** (-5.0 - np.arange(RET_HEADS, dtype=np.float64)))
    idx = np.arange(cr, dtype=np.float64)
    diff = idx[:, None] - idx[None, :]
    dmat = np.where(diff >= 0, np.exp(log_g[:, None, None] * np.maximum(diff, 0.0)), 0.0)
    zeta = np.exp(log_g[:, None] * (cr - 1.0 - idx)[None, :])
    xi = np.exp(log_g[:, None] * (idx + 1.0)[None, :])
    gc = np.exp(log_g * cr)
    f = lambda a: jnp.asarray(a.astype(np.float32))
    return (rope_c, rope_s, f(dmat), f(np.broadcast_to(zeta[:, :, None], (RET_HEADS, cr, 128))),
            f(np.broadcast_to(xi[:, :, None], (RET_HEADS, cr, 128))),
            f(np.broadcast_to(gc[:, None, None], (RET_HEADS, 8, 128))))


def _rope(t, c, s):
    return t * c + pltpu.roll(t, 64, 1) * s


def _rope_t(d, c, s):
    return d * c + pltpu.roll(d * s, 64, 1)


def _ret_specs(nblk, rev):
    ix = (lambda i: nblk - 1 - i) if rev else (lambda i: i)
    cr = RET_CHUNK
    col = lambda base: pl.BlockSpec((BLK, 512), lambda i: (ix(i), base // 512))
    tab = pl.BlockSpec((BLK, 128), lambda i: (ix(i), 0))
    sq = pl.BlockSpec((RET_HEADS, cr, cr), lambda i: (0, 0, 0))
    hv = pl.BlockSpec((RET_HEADS, cr, 128), lambda i: (0, 0, 0))
    g8 = pl.BlockSpec((RET_HEADS, 8, 128), lambda i: (0, 0, 0))
    st = pl.BlockSpec((RET_HEADS, BLK // cr, 128, 128), lambda i: (0, ix(i), 0, 0))
    out = pl.BlockSpec((BLK, 512), lambda i: (ix(i), 0))
    return col, tab, sq, hv, g8, st, out


def _retention(proj, tables, *, name):
    lp = proj.shape[0]
    nblk, cr = lp // BLK, RET_CHUNK
    scale = RET_DK ** -0.5

    def body(q_ref, k_ref, v_ref, c_ref, s_ref, d_ref, z_ref, x_ref, g_ref, o_ref, st_ref, state):
        @pl.when(pl.program_id(0) == 0)
        def _():
            state[...] = jnp.zeros_like(state)

        def chunk(ci, carry):
            sl = pl.ds(pl.multiple_of(ci * cr, cr), cr)
            c, s = c_ref[sl, :], s_ref[sl, :]
            for h in range(RET_HEADS):
                hs = slice(128 * h, 128 * h + 128)
                q = _rope(q_ref[sl, hs], c, s)
                k = _rope(k_ref[sl, hs], c, s) * scale
                qb, kb, vb = q.astype(BF16), k.astype(BF16), v_ref[sl, hs].astype(BF16)
                st = state[h]
                st_ref[h, ci] = st
                sc = _dot_nt(qb, kb) * d_ref[h]
                o_ref[sl, hs] = _dot(sc.astype(BF16), vb) + _dot(qb, st.astype(BF16)) * x_ref[h]
                state[h] = st * g_ref[h][0:1, :] + _dot_tn((k * z_ref[h]).astype(BF16), vb)
            return carry

        lax.fori_loop(0, BLK // cr, chunk, 0)

    col, tab, sq, hv, g8, st, out = _ret_specs(nblk, False)
    return pl.pallas_call(
        body,
        out_shape=(jax.ShapeDtypeStruct((lp, 512), F32), jax.ShapeDtypeStruct((4, lp // cr, 128, 128), F32)),
        grid=(nblk,), in_specs=[col(C_RQ), col(C_RK), col(C_RV), tab, tab, sq, hv, hv, g8],
        out_specs=(out, st), scratch_shapes=[pltpu.VMEM((RET_HEADS, 128, 128), F32)],
        compiler_params=_cp("arbitrary"), name=name)(proj, proj, proj, *tables)


def _retention_bwd(proj, do, states, tables, *, name):
    lp = proj.shape[0]
    nblk, cr = lp // BLK, RET_CHUNK
    nch = BLK // cr
    scale = RET_DK ** -0.5

    def body(q_ref, k_ref, v_ref, do_ref, st_ref, c_ref, s_ref, d_ref, z_ref, x_ref, g_ref, dqkv_ref, dstate):
        @pl.when(pl.program_id(0) == 0)
        def _():
            dstate[...] = jnp.zeros_like(dstate)

        def chunk(cc, carry):
            ci = nch - 1 - cc
            sl = pl.ds(pl.multiple_of(ci * cr, cr), cr)
            c, s = c_ref[sl, :], s_ref[sl, :]
            for h in range(RET_HEADS):
                hs = slice(128 * h, 128 * h + 128)
                dmat, zeta, xi = d_ref[h], z_ref[h], x_ref[h]
                q = _rope(q_ref[sl, hs], c, s)
                k = _rope(k_ref[sl, hs], c, s) * scale
                qb, kb, vb = q.astype(BF16), k.astype(BF16), v_ref[sl, hs].astype(BF16)
                kzb = (k * zeta).astype(BF16)
                dov = do_ref[sl, hs]
                dob, doxb = dov.astype(BF16), (dov * xi).astype(BF16)
                stb = st_ref[h, ci].astype(BF16)
                dsn = dstate[h]
                dsnb = dsn.astype(BF16)
                scb = (_dot_nt(qb, kb) * dmat).astype(BF16)
                dscb = (_dot_nt(dob, vb) * dmat).astype(BF16)
                dq = _dot(dscb, kb) + _dot_nt(doxb, stb)
                dk = _dot_tn(dscb, qb) + _dot_nt(vb, dsnb) * zeta
                dv = _dot_tn(scb, dob) + _dot(kzb, dsnb)
                dstate[h] = dsn * g_ref[h][0:1, :] + _dot_tn(qb, doxb)
                dqkv_ref[sl, 128 * h:128 * h + 128] = _rope_t(dq, c, s).astype(BF16)
                dqkv_ref[sl, 512 + 128 * h:640 + 128 * h] = _rope_t(dk * scale, c, s).astype(BF16)
                dqkv_ref[sl, 1024 + 128 * h:1152 + 128 * h] = dv.astype(BF16)
            return carry

        lax.fori_loop(0, nch, chunk, 0)

    col, tab, sq, hv, g8, st, out = _ret_specs(nblk, True)
    return pl.pallas_call(
        body, out_shape=jax.ShapeDtypeStruct((lp, P_RET), BF16), grid=(nblk,),
        in_specs=[col(C_RQ), col(C_RK), col(C_RV), out, st, tab, tab, sq, hv, hv, g8],
        out_specs=pl.BlockSpec((BLK, P_RET), lambda i: (nblk - 1 - i, 0)),
        scratch_shapes=[pltpu.VMEM((RET_HEADS, 128, 128), F32)],
        compiler_params=_cp("arbitrary"), name=name)(proj, proj, proj, do, states, *tables)


def _gla_tables():
    c = GLA_CHUNK
    tri = np.tril(np.ones((c, c), np.float32))
    ones_qv = np.kron(np.eye(GLA_HEADS, dtype=np.float32), np.ones((GLA_DK, GLA_DV), np.float32))
    return (jnp.asarray(tri, BF16), jnp.asarray(tri.T.copy(), BF16), jnp.asarray(ones_qv, BF16),
            jnp.asarray(ones_qv.T.copy(), BF16))


def _tri_sum(tri, x):
    hi = x.astype(BF16)
    lo = (x - hi.astype(F32)).astype(BF16)
    return _dot(tri, hi) + _dot(tri, lo)


def _head_masks(width, per):
    lane = lax.broadcasted_iota(jnp.int32, (1, width), 1)
    return [((lane >= per * h) & (lane < per * (h + 1))).astype(F32) for h in range(GLA_HEADS)]


def _stack_heads(x, masks):
    return jnp.concatenate([x * m for m in masks], axis=0)


def _gla_gate(ga, w2, b, ok, tri):
    z = _dot(ga.astype(BF16), w2) + b
    la = (jnp.minimum(z, 0.0) - jnp.log(1.0 + jnp.exp(-jnp.abs(z)))) * (1.0 / GLA_TAU)
    la = jnp.where(ok, la, 0.0)
    return z, _tri_sum(tri, la)


def _gla_rows(i_blk, ci, lp):
    c = GLA_CHUNK
    rows = i_blk * BLK + ci * c + lax.broadcasted_iota(jnp.int32, (c, 1), 0)
    return (rows >= PADF) & (rows < lp - BACK)


N_SUB = GLA_CHUNK // GLA_SUB - 1
N_SUB2 = GLA_SUB // GLA_SUB2 - 1


def _gla_masks():
    c, s1, s2 = GLA_CHUNK, GLA_SUB, GLA_SUB2
    sh1, sh2 = s1.bit_length() - 1, s2.bit_length() - 1
    r = lax.broadcasted_iota(jnp.int32, (c, GLA_QK), 0)
    blk, within = jnp.right_shift(r, sh1), jnp.bitwise_and(r, s1 - 1)
    grp = jnp.right_shift(within, sh2)
    rowm = [(blk == a).astype(F32) for a in range(1, N_SUB + 1)] + [(grp == b).astype(F32) for b in range(1, N_SUB2 + 1)]
    keym = ([(r < s1 * a).astype(F32) for a in range(1, N_SUB + 1)]
            + [(within < s2 * b).astype(F32) for b in range(1, N_SUB2 + 1)])
    rs = lax.broadcasted_iota(jnp.int32, (GLA_HEADS * c, c), 0)
    ts = lax.broadcasted_iota(jnp.int32, (GLA_HEADS * c, c), 1)
    same = (jnp.right_shift(jnp.bitwise_and(rs, c - 1), sh1) == jnp.right_shift(ts, sh1)).astype(F32)
    lag = [(jnp.bitwise_and(r, s2 - 1) >= j).astype(F32) for j in range(s2)]
    return rowm, keym, same, lag


def _gla_hats(qs, k, g, masks, hm_q):
    c, s1, s2 = GLA_CHUNK, GLA_SUB, GLA_SUB2
    rowm, keym, same, _ = masks
    refs = [g[s1 * a - 1:s1 * a, :] for a in range(1, N_SUB + 1)]
    for b in range(1, N_SUB2 + 1):
        refs.append(jnp.concatenate([jnp.broadcast_to(g[s1 * i + s2 * b - 1:s1 * i + s2 * b, :], (s1, GLA_QK))
                                     for i in range(c // s1)], axis=0))
    eqs = [jnp.exp(jnp.minimum(g - r, 0.0)) * m for r, m in zip(refs, rowm)]
    eks = [jnp.exp(jnp.minimum(r - g, 0.0)) * m for r, m in zip(refs, keym)]
    qhs, khs = [qs * e for e in eqs], [k * e for e in eks]
    qst = [_stack_heads(q, hm_q).astype(BF16) for q in qhs]
    khb = [x.astype(BF16) for x in khs]
    qa, qb = jnp.concatenate(qst[:N_SUB], axis=1), jnp.concatenate(qst[N_SUB:], axis=1)
    ka, kb = jnp.concatenate(khb[:N_SUB], axis=1), jnp.concatenate(khb[N_SUB:], axis=1)
    p = _dot_nt(qa, ka) + _dot_nt(qb, kb) * same
    return eqs, eks, qhs, khs, qa, qb, ka, kb, p


def _roll_rows(x, j):
    return x if j == 0 else pltpu.roll(x, j, 0)


def _gla(proj, w2p, b, tables, *, name):
    lp = proj.shape[0]
    nblk, c, s2 = lp // BLK, GLA_CHUNK, GLA_SUB2
    nch = BLK // c

    def body(q_ref, k_ref, v_ref, a_ref, w_ref, b_ref, tri_ref, ones_ref, o_ref, st_ref, gz_ref, state):
        i_blk = pl.program_id(0)

        @pl.when(i_blk == 0)
        def _():
            state[...] = jnp.zeros_like(state)
        hm_q = _head_masks(GLA_QK, GLA_DK)
        masks = _gla_masks()
        tri, ones_qv, w2, bias = tri_ref[...], ones_ref[...], w_ref[...], b_ref[...]

        def chunk(ci, carry):
            sl = pl.ds(pl.multiple_of(ci * c, c), c)
            ok = _gla_rows(i_blk, ci, lp)
            k, v = k_ref[sl, :], v_ref[sl, :]
            vb = v.astype(BF16)
            qs = q_ref[sl, :] * (GLA_DK ** -0.5)
            z, g = _gla_gate(a_ref[sl, :], w2, bias, ok, tri)
            gz_ref[sl, 0:GLA_QK] = g
            gz_ref[sl, GLA_QK:2 * GLA_QK] = z
            last = g[c - 1:c, :]
            st = state[...]
            st_ref[ci] = st
            qst = _stack_heads(qs * jnp.exp(g), hm_q).astype(BF16)
            oi = _dot_nt(qst, st.astype(BF16))
            o = jnp.concatenate([oi[c * h:c * h + c, :] for h in range(GLA_HEADS)], axis=1)
            ke = k * jnp.exp(last - g)
            f = _dot_tn(vb, ke.astype(BF16))
            upd = f[0:GLA_DV, :] * hm_q[0]
            for h in range(1, GLA_HEADS):
                upd = upd + f[GLA_DV * h:GLA_DV * (h + 1), :] * hm_q[h]
            state[...] = st * jnp.exp(last) + upd
            p = _gla_hats(qs, k, g, masks, hm_q)[-1]
            ob = _dot(p.astype(BF16), vb)
            o = o + jnp.concatenate([ob[c * h:c * h + c, GLA_DV * h:GLA_DV * (h + 1)] for h in range(GLA_HEADS)],
                                    axis=1)
            ws = []
            for j in range(s2):
                ej = jnp.exp(jnp.minimum(g - _roll_rows(g, j), 0.0))
                ws.append((qs * _roll_rows(k, j) * ej * masks[3][j]).astype(BF16))
            ball = _dot(jnp.concatenate(ws, axis=0), ones_qv)
            for j in range(s2):
                o = o + ball[c * j:c * j + c, :] * _roll_rows(v, j)
            o_ref[sl, :] = o
            return carry

        lax.fori_loop(0, nch, chunk, 0)

    tri, _, ones_qv, _ = tables
    full = lambda arr: pl.BlockSpec(arr.shape, lambda i: (0,) * arr.ndim)
    return pl.pallas_call(
        body,
        out_shape=(jax.ShapeDtypeStruct((lp, GLA_V), F32), jax.ShapeDtypeStruct((lp // c, GLA_DV, GLA_QK), F32),
                   jax.ShapeDtypeStruct((lp, 2 * GLA_QK), F32)),
        grid=(nblk,),
        in_specs=[pl.BlockSpec((BLK, GLA_QK), lambda i: (i, C_GQ // GLA_QK)),
                  pl.BlockSpec((BLK, GLA_QK), lambda i: (i, C_GK // GLA_QK)),
                  pl.BlockSpec((BLK, GLA_V), lambda i: (i, C_GV // GLA_V)),
                  pl.BlockSpec((BLK, 128), lambda i: (i, C_GA // 128)),
                  full(w2p), full(b), full(tri), full(ones_qv)],
        out_specs=(pl.BlockSpec((BLK, GLA_V), lambda i: (i, 0)),
                   pl.BlockSpec((nch, GLA_DV, GLA_QK), lambda i: (i, 0, 0)),
                   pl.BlockSpec((BLK, 2 * GLA_QK), lambda i: (i, 0))),
        scratch_shapes=[pltpu.VMEM((GLA_DV, GLA_QK), F32)],
        compiler_params=_cp("arbitrary"), name=name)(proj, proj, proj, proj, w2p, b, tri, ones_qv)


def _gla_bwd(proj, do, states, gz, w2p, tables, *, name):
    lp = proj.shape[0]
    nblk, c, s1, s2 = lp // BLK, GLA_CHUNK, GLA_SUB, GLA_SUB2
    nch = BLK // c

    def body(q_ref, k_ref, v_ref, a_ref, do_ref, st_ref, gz_ref, w_ref, trit_ref, ones_ref, onest_ref,
             dp_ref, dw_ref, db_ref, dstate, dqs_s, dk_s, dg_s, dv_s):
        i_blk = nblk - 1 - pl.program_id(0)

        @pl.when(pl.program_id(0) == 0)
        def _():
            dstate[...] = jnp.zeros_like(dstate)
            dw_ref[...] = jnp.zeros_like(dw_ref)
            db_ref[...] = jnp.zeros_like(db_ref)
        hm_q = _head_masks(GLA_QK, GLA_DK)
        hm_v = _head_masks(GLA_V, GLA_DV)
        masks = _gla_masks()
        trit, ones_qv, ones_vq = trit_ref[...], ones_ref[...], onest_ref[...]
        w2 = w_ref[...]
        rsum = lambda x: jnp.sum(x, axis=0, keepdims=True)

        def chunk(cc, carry):
            ci = nch - 1 - cc
            sl = pl.ds(pl.multiple_of(ci * c, c), c)
            ok = _gla_rows(i_blk, ci, lp)
            k, v, ga = k_ref[sl, :], v_ref[sl, :], a_ref[sl, :]
            vb = v.astype(BF16)
            qs = q_ref[sl, :] * (GLA_DK ** -0.5)
            g, z = gz_ref[sl, 0:GLA_QK], gz_ref[sl, GLA_QK:2 * GLA_QK]
            last = g[c - 1:c, :]
            elast = jnp.exp(last)
            eg = jnp.exp(g)
            ekl = jnp.exp(last - g)
            qe, ke = qs * eg, k * ekl
            dov = do_ref[sl, :]
            st = st_ref[ci]
            dsn = dstate[...]
            qst = _stack_heads(qe, hm_q).astype(BF16)
            dost = jnp.concatenate([dov[:, GLA_DV * h:GLA_DV * (h + 1)] for h in range(GLA_HEADS)], axis=0).astype(BF16)
            dqe_st = _dot(dost, st.astype(BF16))
            dqe = dqe_st[0:c, :] * hm_q[0]
            for h in range(1, GLA_HEADS):
                dqe = dqe + dqe_st[c * h:c * h + c, :] * hm_q[h]
            dstate[...] = _dot_tn(dost, qst) + dsn * elast
            dlast = rsum(dsn * st) * elast
            df = _stack_heads(dsn, hm_q).astype(BF16)
            dv_s[...] = _dot_nt(ke.astype(BF16), df)
            dke = _dot(vb, df)
            xk = dke * ke
            dqs_s[...] = dqe * eg
            dk_s[...] = dke * ekl
            dg_s[...] = dqe * qe - xk
            dlast = dlast + rsum(xk)
            eqs, eks, qhs, khs, qa, qb, ka, kb, p = _gla_hats(qs, k, g, masks, hm_q)
            dost_v = _stack_heads(dov, hm_v).astype(BF16)
            dp = _dot_nt(dost_v, vb)
            dv_s[...] += _dot_tn(p.astype(BF16), dost_v)
            dpa, dpb = dp.astype(BF16), (dp * masks[2]).astype(BF16)
            dq_all = (_dot(dpa, ka), _dot(dpb, kb))
            dk_all = (_dot_tn(dpa, qa), _dot_tn(dpb, qb))
            for t in range(N_SUB + N_SUB2):
                lvl, i = (0, t) if t < N_SUB else (1, t - N_SUB)
                cols = slice(GLA_QK * i, GLA_QK * (i + 1))
                dq_st = dq_all[lvl][:, cols]
                dqh = dq_st[0:c, :] * hm_q[0]
                for h in range(1, GLA_HEADS):
                    dqh = dqh + dq_st[c * h:c * h + c, :] * hm_q[h]
                dkh = dk_all[lvl][:, cols]
                xq, xkh = dqh * qhs[t], dkh * khs[t]
                dqs_s[...] += dqh * eqs[t]
                dk_s[...] += dkh * eks[t]
                dg_s[...] += xq - xkh
                back_ref = xkh - xq
                if lvl == 0:
                    row = s1 * (i + 1) - 1
                    dg_s[row:row + 1, :] += rsum(back_ref)
                else:
                    for blk in range(c // s1):
                        row = s1 * blk + s2 * (i + 1) - 1
                        dg_s[row:row + 1, :] += rsum(back_ref[s1 * blk:s1 * blk + s1, :])
            kes, qes, ws, dbs = [], [], [], []
            for j in range(s2):
                em = jnp.exp(jnp.minimum(g - _roll_rows(g, j), 0.0)) * masks[3][j]
                kes.append(_roll_rows(k, j) * em)
                qes.append(qs * em)
                ws.append((qs * kes[j]).astype(BF16))
                dbs.append((dov * _roll_rows(v, j)).astype(BF16))
            ball = _dot(jnp.concatenate(ws, axis=0), ones_qv)
            dwall = _dot(jnp.concatenate(dbs, axis=0), ones_vq)
            for j in range(s2):
                back = (lambda x: x) if j == 0 else (lambda x, j=j: pltpu.roll(x, c - j, 0))
                dw = dwall[c * j:c * j + c, :]
                dv_s[...] += back(ball[c * j:c * j + c, :] * dov)
                dqs_s[...] += dw * kes[j]
                dk_s[...] += back(dw * qes[j])
                x = dw * qs * kes[j]
                dg_s[...] += x - back(x)
            dg_s[c - 1:c, :] += dlast
            dla = jnp.where(ok, _tri_sum(trit, dg_s[...]), 0.0)
            dz = dla * (1.0 / GLA_TAU) / (1.0 + jnp.exp(z))
            dzb = dz.astype(BF16)
            dp_ref[sl, 0:256] = (dqs_s[...] * (GLA_DK ** -0.5)).astype(BF16)
            dp_ref[sl, 256:512] = dk_s[...].astype(BF16)
            dp_ref[sl, 512:1024] = dv_s[...].astype(BF16)
            dp_ref[sl, 1024:1152] = _dot_nt(dzb, w2).astype(BF16)
            dp_ref[sl, 1152:1280] = jnp.zeros((c, 128), BF16)
            dw_ref[...] += _dot_tn(ga.astype(BF16), dzb)
            db_ref[...] += rsum(dz)
            return carry

        lax.fori_loop(0, nch, chunk, 0)

    tri, trit, ones_qv, ones_vq = tables
    full = lambda arr: pl.BlockSpec(arr.shape, lambda i: (0,) * arr.ndim)
    rev = lambda i: nblk - 1 - i
    return pl.pallas_call(
        body,
        out_shape=(jax.ShapeDtypeStruct((lp, P_GLA), BF16),
                   jax.ShapeDtypeStruct((128, GLA_QK), F32), jax.ShapeDtypeStruct((1, GLA_QK), F32)),
        grid=(nblk,),
        in_specs=[pl.BlockSpec((BLK, GLA_QK), lambda i: (rev(i), C_GQ // GLA_QK)),
                  pl.BlockSpec((BLK, GLA_QK), lambda i: (rev(i), C_GK // GLA_QK)),
                  pl.BlockSpec((BLK, GLA_V), lambda i: (rev(i), C_GV // GLA_V)),
                  pl.BlockSpec((BLK, 128), lambda i: (rev(i), C_GA // 128)),
                  pl.BlockSpec((BLK, GLA_V), lambda i: (rev(i), 0)),
                  pl.BlockSpec((nch, GLA_DV, GLA_QK), lambda i: (rev(i), 0, 0)),
                  pl.BlockSpec((BLK, 2 * GLA_QK), lambda i: (rev(i), 0)),
                  full(w2p), full(trit), full(ones_qv), full(ones_vq)],
        out_specs=(pl.BlockSpec((BLK, P_GLA), lambda i: (rev(i), 0)),
                   pl.BlockSpec((128, GLA_QK), lambda i: (0, 0)),
                   pl.BlockSpec((1, GLA_QK), lambda i: (0, 0))),
        scratch_shapes=[pltpu.VMEM((GLA_DV, GLA_QK), F32), pltpu.VMEM((c, GLA_QK), F32),
                        pltpu.VMEM((c, GLA_QK), F32), pltpu.VMEM((c, GLA_QK), F32), pltpu.VMEM((c, GLA_V), F32)],
        compiler_params=_cp("arbitrary"), name=name)(proj, proj, proj, proj, do, states, gz, w2p, trit, ones_qv, ones_vq)


def _as2d(a):
    return a.reshape(-1, a.shape[-1])


def _ew_tile(r):
    return _tile(r, (512, 256, 128, 64, 32, 16, 8))


def _add2(a, b, *, out_dtype, name):
    a2, b2 = _as2d(a), _as2d(b)
    r, n = a2.shape
    tm = _ew_tile(r)

    def body(a_ref, b_ref, o_ref):
        o_ref[...] = (a_ref[...] + b_ref[...]).astype(o_ref.dtype)

    blk = pl.BlockSpec((tm, n), lambda i: (i, 0))
    return pl.pallas_call(body, out_shape=jax.ShapeDtypeStruct((r, n), out_dtype), grid=(r // tm,), in_specs=[blk, blk],
                          out_specs=blk, compiler_params=_cp("parallel"), name=name)(a2, b2).reshape(a.shape)


def _sum_slots(own, q, *, name):
    shape = own.shape
    q3 = q.reshape(3, -1, shape[-1])
    own2 = _as2d(own)
    r, n = own2.shape
    tm = _ew_tile(r)

    def body(own_ref, q_ref, o_ref):
        f = lambda i: q_ref[i].astype(F32)
        o_ref[...] = ((own_ref[...].astype(F32) + f(0)) + f(1)) + f(2)

    blk = pl.BlockSpec((tm, n), lambda i: (i, 0))
    return pl.pallas_call(
        body, out_shape=jax.ShapeDtypeStruct((r, n), F32), grid=(r // tm,),
        in_specs=[blk, pl.BlockSpec((3, tm, n), lambda i: (0, i, 0))], out_specs=blk,
        compiler_params=_cp("parallel"), name=name)(own2, q3).reshape(shape)


def _adamw(w, g, m, v, *, name):
    shape = w.shape
    w2, g2, m2, v2 = _as2d(w), _as2d(g), _as2d(m), _as2d(v)
    r, n = w2.shape
    tm = _ew_tile(r)

    def body(w_ref, g_ref, m_ref, v_ref, d_ref, mo_ref, vo_ref):
        d_ref[...], mo_ref[...], vo_ref[...] = _adam_math(w_ref[...], g_ref[...], m_ref[...], v_ref[...])

    blk = pl.BlockSpec((tm, n), lambda i: (i, 0))
    o = jax.ShapeDtypeStruct((r, n), F32)
    d, mo, vo = pl.pallas_call(body, out_shape=(o, o, o), grid=(r // tm,), in_specs=[blk] * 4, out_specs=(blk,) * 3,
                               compiler_params=_cp("parallel"), name=name)(w2, g2, m2, v2)
    return d.reshape(shape), mo.reshape(shape), vo.reshape(shape)


def _adam_math(w, gv, m, v):
    c1 = 1.0 - ADAM_B1 ** ADAM_STEP
    c2 = 1.0 - ADAM_B2 ** ADAM_STEP
    mn = ADAM_B1 * m + (1.0 - ADAM_B1) * gv
    vn = ADAM_B2 * v + (1.0 - ADAM_B2) * (gv * gv)
    return -ADAM_LR * ((mn / c1) / (jnp.sqrt(vn / c2) + ADAM_EPS) + ADAM_WD * w), mn, vn


def _adamw_halves(w, m, v, mine, theirs, c, *, name):
    depth, rows, n = w.shape
    r2 = rows // 2
    tm = next(t for t in range(min(r2, 256), 0, -8) if r2 % t == 0)
    steps = r2 // tm

    def body(c_ref, w_ref, m_ref, v_ref, *rest):
        halves, (g_ref, d_ref, mo_ref, vo_ref) = rest[:2 * depth], rest[2 * depth:]
        l, h = pl.program_id(0), pl.program_id(1)
        gv = None
        for k in range(depth):
            gk = jnp.where(h == c_ref[0], halves[2 * k][...], halves[2 * k + 1][...])
            gv = gk if gv is None else jnp.where(l == k, gk, gv)
        g_ref[...] = gv
        d_ref[...], mo_ref[...], vo_ref[...] = _adam_math(w_ref[...], gv, m_ref[...], v_ref[...])

    big = pl.BlockSpec((tm, n), lambda l, h, i, c_ref: ((2 * l + h) * steps + i, 0))
    half = lambda k: pl.BlockSpec((tm, n), lambda l, h, i, c_ref: (jnp.where(l == k, i, 0), 0))
    o = jax.ShapeDtypeStruct((depth * rows, n), F32)
    args = [a for k in range(depth) for a in (mine[k], theirs[k])]
    outs = pl.pallas_call(
        body, out_shape=(o, o, o, o),
        grid_spec=pltpu.PrefetchScalarGridSpec(
            num_scalar_prefetch=1, grid=(depth, 2, steps),
            in_specs=[big, big, big] + [half(k) for k in range(depth) for _ in range(2)], out_specs=(big,) * 4),
        compiler_params=_cp("arbitrary", "arbitrary", "arbitrary"), name=name)(
            jnp.reshape(c, (1,)).astype(jnp.int32), _as2d(w), _as2d(m), _as2d(v), *args)
    return [a.reshape(w.shape) for a in outs]


ANY = pl.BlockSpec(memory_space=pl.ANY)


def _place():
    return lax.axis_index("x"), lax.axis_index("y"), lax.axis_index("c")


def _other_chips(x, y):
    return [(1 - x, y), (x, 1 - y), (1 - x, 1 - y)]


def _remote(src, dst, ssem, rsem, dev):
    return pltpu.make_async_remote_copy(src_ref=src, dst_ref=dst, send_sem=ssem, recv_sem=rsem, device_id=dev,
                                        device_id_type=MESH)


def _allgather_chips(arrs, *, name):
    n = len(arrs)

    def body(*refs):
        ins, outs = refs[:n], refs[n:2 * n]
        s1, r1, s2, r2 = refs[2 * n:]
        x, y, c = _place()
        q = 2 * x + y
        chips = _other_chips(x, y)
        qs = [2 * cx + cy for cx, cy in chips]
        sib = (x, y, 1 - c)
        first, passed = [], []
        for k in range(n):
            for j, chip in enumerate(chips):
                first.append(_remote(ins[k].at[c], outs[k].at[c, q], s1.at[k, j], r1.at[k, j], (*chip, c)))
        for cp in first:
            cp.start()
        for k in range(n):
            for j, chip in enumerate(chips):
                land = outs[k].at[c, qs[j]]
                _remote(land, land, s1.at[k, j], r1.at[k, j], (*chip, c)).wait_recv()
                fw = _remote(land, land, s2.at[k, j], r2.at[k, j], sib)
                fw.start()
                passed.append(fw)
        for k in range(n):
            for j in range(3):
                land = outs[k].at[1 - c, qs[j]]
                _remote(land, land, s2.at[k, j], r2.at[k, j], sib).wait_recv()
        for cp in first + passed:
            cp.wait_send()

    sem = pltpu.SemaphoreType.DMA
    outs = pl.pallas_call(
        body, out_shape=tuple(jax.ShapeDtypeStruct((2, 4) + a.shape[1:], a.dtype) for a in arrs),
        in_specs=[ANY] * n, out_specs=(ANY,) * n,
        scratch_shapes=[sem((n, 3)), sem((n, 3)), sem((n, 3)), sem((n, 3))], name=name)(*arrs)
    chip = 2 * lax.axis_index("x") + lax.axis_index("y")
    return [lax.dynamic_update_slice_in_dim(o, a[:, None], chip, axis=1) for o, a in zip(outs, arrs)]


def _pair_exchange(arrs, *, name):
    n = len(arrs)

    def body(*refs):
        ins, outs = refs[:n], refs[n:2 * n]
        ssem, rsem = refs[2 * n:]
        x, y, c = _place()
        cps = [_remote(ins[k].at[:, 1 - c], outs[k], ssem.at[k], rsem.at[k], (x, y, 1 - c)) for k in range(n)]
        for cp in cps:
            cp.start()
        for cp in cps:
            cp.wait()

    sem = pltpu.SemaphoreType.DMA
    return pl.pallas_call(
        body, out_shape=tuple(jax.ShapeDtypeStruct((a.shape[0],) + a.shape[2:], a.dtype) for a in arrs),
        in_specs=[ANY] * n, out_specs=(ANY,) * n, scratch_shapes=[sem((n,)), sem((n,))], name=name)(*arrs)


def _pair_sum(mine, theirs, c, *, name):
    _, _, r, n = mine.shape
    tm = r if r <= 512 else _ew_tile(r)

    def body(c_ref, a_ref, b_ref, o_ref):
        o_ref[...] = (a_ref[...] + b_ref[...]).astype(BF16)

    blk = pl.BlockSpec((None, tm, n), lambda s, i, c_ref: (s, i, 0))
    return pl.pallas_call(
        body, out_shape=jax.ShapeDtypeStruct((4, r, n), BF16),
        grid_spec=pltpu.PrefetchScalarGridSpec(
            num_scalar_prefetch=1, grid=(4, r // tm),
            in_specs=[pl.BlockSpec((None, None, tm, n), lambda s, i, c_ref: (s, c_ref[0], i, 0)), blk], out_specs=blk),
        compiler_params=_cp("parallel", "parallel"), name=name)(jnp.reshape(c, (1,)).astype(jnp.int32), mine, theirs)


def _chip_copies(ins, outs, ssem, rsem, mode):
    x, y, c = _place()
    q = 2 * x + y
    sends, recvs = [], []
    for k in range(len(ins)):
        for j, (cx, cy) in enumerate(_other_chips(x, y)):
            sem = (ssem.at[k, j], rsem.at[k, j], (cx, cy, c))
            if mode == "scatter":
                sends.append(_remote(ins[k].at[2 * cx + cy], outs[k].at[j], *sem))
                recvs.append(sends[-1])
            else:
                sends.append(_remote(ins[k].at[c], outs[k].at[2 * q + c], *sem))
                recvs.append(_remote(ins[k].at[c], outs[k].at[2 * (2 * cx + cy) + c], *sem))
    return sends, recvs


def _chip_wait(sends, recvs):
    for cp in sends:
        cp.wait_send()
    for cp in recvs:
        cp.wait_recv()


def _landing_shape(a, mode):
    return jax.ShapeDtypeStruct(((3,) if mode == "scatter" else (8,)) + a.shape[1:], a.dtype)


def _chip_exchange(arrs, mode, *, name):
    n = len(arrs)

    def body(*refs):
        ins, outs = refs[:n], refs[n:2 * n]
        ssem, rsem = refs[2 * n:]
        sends, recvs = _chip_copies(ins, outs, ssem, rsem, mode)
        for cp in sends:
            cp.start()
        _chip_wait(sends, recvs)

    sem = pltpu.SemaphoreType.DMA
    return list(pl.pallas_call(
        body, out_shape=tuple(_landing_shape(a, mode) for a in arrs),
        in_specs=[ANY] * n, out_specs=(ANY,) * n, scratch_shapes=[sem((n, 3)), sem((n, 3))], name=name)(*arrs))


def _pair_fill(bufs, owns, *, name):
    n = len(bufs)

    def body(*refs):
        own, outs = refs[n:2 * n], refs[2 * n:3 * n]
        ssem, rsem = refs[3 * n:]
        x, y, c = _place()
        q = 2 * x + y
        sib = (x, y, 1 - c)
        sends, recvs = [], []
        for k in range(n):
            for j, (cx, cy) in enumerate(_other_chips(x, y)):
                mine, theirs = outs[k].at[2 * (2 * cx + cy) + c], outs[k].at[2 * (2 * cx + cy) + 1 - c]
                sends.append(_remote(mine, mine, ssem.at[k, j], rsem.at[k, j], sib))
                recvs.append(_remote(mine, theirs, ssem.at[k, j], rsem.at[k, j], sib))
            slots = outs[k].at[pl.ds(2 * q, 2)]
            sends.append(_remote(own[k], slots, ssem.at[k, 3], rsem.at[k, 3], sib))
            recvs.append(sends[-1])
        for cp in sends:
            cp.start()
        _chip_wait(sends, recvs)

    sem = pltpu.SemaphoreType.DMA
    return list(pl.pallas_call(
        body, out_shape=tuple(jax.ShapeDtypeStruct(b.shape, b.dtype) for b in bufs),
        in_specs=[ANY] * (2 * n), out_specs=(ANY,) * n, scratch_shapes=[sem((n, 4)), sem((n, 4))],
        input_output_aliases={k: k for k in range(n)}, name=name)(*bufs, *owns))


def _pair_swap(arrs, *, name):
    n = len(arrs)

    def body(*refs):
        ins, outs = refs[:n], refs[n:2 * n]
        ssem, rsem = refs[2 * n:]
        x, y, c = _place()
        cps = [_remote(ins[k], outs[k], ssem.at[k], rsem.at[k], (x, y, 1 - c)) for k in range(n)]
        for cp in cps:
            cp.start()
        for cp in cps:
            cp.wait()

    sem = pltpu.SemaphoreType.DMA
    return pl.pallas_call(
        body, out_shape=tuple(jax.ShapeDtypeStruct(a.shape, a.dtype) for a in arrs),
        in_specs=[ANY] * n, out_specs=(ANY,) * n, scratch_shapes=[sem((n,)), sem((n,))], name=name)(*arrs)


def _allreduce_small(slab, *, name):
    r, n = slab.shape

    def body(x_ref, o_ref, buf, ssem, rsem):
        x, y, c = _place()
        me = 4 * x + 2 * y + c
        buf[me] = x_ref[...]
        cps = []
        for rel in range(1, 8):
            bx, by, bc = (rel >> 2) & 1, (rel >> 1) & 1, rel & 1
            px, py, pc = (x + bx) % 2, (y + by) % 2, (c + bc) % 2
            cps.append((_remote(x_ref, buf.at[me], ssem.at[rel - 1], rsem.at[rel - 1], (px, py, pc)),
                        4 * px + 2 * py + pc, (px, py, pc)))
        for cp, _, _ in cps:
            cp.start()
        for rel, (cp, peer, dev) in enumerate(cps):
            cp.wait_send()
            _remote(x_ref, buf.at[peer], ssem.at[rel], rsem.at[rel], dev).wait_recv()
        acc = buf[0]
        for k in range(1, 8):
            acc = acc + buf[k]
        o_ref[...] = acc

    vm = pl.BlockSpec(memory_space=pltpu.VMEM)
    sem = pltpu.SemaphoreType.DMA
    return pl.pallas_call(
        body, out_shape=jax.ShapeDtypeStruct((r, n), F32), in_specs=[vm], out_specs=vm,
        scratch_shapes=[pltpu.VMEM((8, r, n), F32), sem((7,)), sem((7,))], name=name)(slab)


def _slab(arrs, row_mult):
    flat = jnp.concatenate([a.reshape(-1) for a in arrs])
    unit = 128 * row_mult
    total = -(-flat.size // unit) * unit
    return jnp.pad(flat, (0, total - flat.size)).reshape(-1, 128)


def _unslab(slab, shapes):
    flat = slab.reshape(-1)
    out, off = [], 0
    for s in shapes:
        size = int(np.prod(s))
        out.append(flat[off:off + size].reshape(s))
        off += size
    return out


def _cols_from_chips(a):
    return jnp.transpose(a, (1, 0, 2)).reshape(a.shape[1], -1)


def _cols_to_chips(a, parts):
    r = a.shape[0]
    return jnp.transpose(a.reshape(r, parts, -1), (1, 0, 2))


BIG = ("w_in", "w_out", "up", "down")
GATHER_RIDES = {("proj", 0): (("w_out", 0), ("up", 0)), ("mix_out", 0): (("down", 0),),
                ("ffn_fwd", 0): (("w_in", 1), ("w_out", 1), ("up", 1), ("down", 1))}
REDUCE_RIDES = {("ffn_up_a_dw", 0): (("up",), 1), ("ffn_down_dw", 0): (("w_in", "w_out"), 1),
                ("mix_out_dx", 0): (("down",), 1),
                ("proj_dx", 0): (("up",), 0), ("proj_dw_0", 0): (("down",), 0), ("proj_dw_1", 0): (("w_out",), 0)}


class _LocalWeights:
    def __init__(self, meta, win, wout, up_a, up_g, down, w2p, cw):
        self._meta, self._w = meta, {"win": win, "wout": wout, "up_a": up_a, "up_g": up_g, "down": down, "w2p": w2p,
                                     "cw": cw}

    def meta(self):
        return self._meta

    def get(self, kind, l):
        return self._w[kind][l]

    def mm(self, site, l, a, b, fn=None, **kw):
        return (fn or _mm)(a, b, name=site, **kw)

    def ffn_fwd(self, l, h, m, w_post, w_next, wa, wg, ba, bg):
        return _ffn_fwd(h, m, w_post, w_next, self.get("up_a", l), self.get("up_g", l), wa, wg, ba, bg,
                        self.get("down", l), name="ffn_fwd")[0]

    def grads_done(self, l, g, kinds):
        pass


class _ChipWeights:
    def __init__(self, w_in, w_out, ffn_up, ffn_down, meta_tokens, gla_gate_w2, ffn_conv_w):
        self.x, self.y, self.c = _place()
        self.q = 2 * self.x + self.y
        halves = lambda a: a.astype(BF16).reshape(2, a.shape[0] // 2, a.shape[1])
        self.own = {(k, l): halves(a[l]) for k, a in zip(BIG, (w_in, w_out, ffn_up, ffn_down)) for l in range(DEPTH)}
        self.landed, self.swapped, self.full, self.n_swaps = {}, {}, {}, 0
        self.sh_shapes = [meta_tokens.shape, gla_gate_w2.shape, ffn_conv_w.shape]
        self.own["small", 0] = _slab([meta_tokens, gla_gate_w2, ffn_conv_w], 16).reshape(2, -1, 128)
        first = [("w_in", 0), ("small", 0)]
        for key, arr in zip(first, _chip_exchange([self.own[k] for k in first], "bcast", name="gather_first")):
            self.landed[key] = arr
        sh = self._whole("small", 0).reshape(4, -1, 128)
        parts = [_unslab(sh[k], self.sh_shapes) for k in range(4)]
        self._meta = jnp.concatenate([p[0] for p in parts], axis=-1)
        self.w2 = jnp.concatenate([p[1] for p in parts], axis=-1)
        self.cw = jnp.concatenate([p[2] for p in parts], axis=-1)
        self.partial, self.slots = {}, {}

    def _whole(self, kind, l):
        if (kind, l) not in self.full:
            keys = [k for k in self.landed if k not in self.full]
            got = _pair_fill([self.landed[k] for k in keys], [self.own[k] for k in keys],
                             name=f"gather_fill_{self.n_swaps}")
            self.n_swaps += 1
            for k, buf in zip(keys, got):
                self.full[k] = buf.reshape(4, 2 * buf.shape[1], buf.shape[2])
        return self.full[kind, l]

    def meta(self):
        return self._meta

    def get(self, kind, l):
        if kind == "win":
            return _to_kernel_cols(_cols_from_chips(self._whole("w_in", l)))
        if kind == "wout":
            return self._whole("w_out", l).reshape(D_MODEL, D_MODEL)
        if kind == "up_a":
            return _cols_from_chips(self._whole("up", l)[0:2])
        if kind == "up_g":
            return _cols_from_chips(self._whole("up", l)[2:4])
        if kind == "down":
            return self._whole("down", l).reshape(D_FF, D_MODEL)
        if kind == "w2p":
            return jnp.pad(self.w2[l], ((0, 128 - GLA_RANK), (0, 0))).astype(BF16)
        return self.cw[l]

    def mm(self, site, l, a, b, fn=None, **kw):
        fn = fn or _mm
        if (site, l) in GATHER_RIDES:
            keys = GATHER_RIDES[site, l]
            out, got = fn(a, b, name=site, carry=([self.own[k] for k in keys], "bcast"), **kw)
            self.landed.update(zip(keys, got))
            return out
        if (site, l) in REDUCE_RIDES:
            kinds, gl = REDUCE_RIDES[site, l]
            keys = [(k, gl) for k in kinds]
            if all(k in self.partial and k not in self.slots for k in keys):
                out, got = fn(a, b, name=site, carry=([self.partial[k] for k in keys], "scatter"), **kw)
                self.slots.update(zip(keys, got))
                return out
        return fn(a, b, name=site, **kw)

    def ffn_fwd(self, l, h, m, w_post, w_next, wa, wg, ba, bg):
        keys = GATHER_RIDES.get(("ffn_fwd", l), ())
        outs, got = _ffn_fwd(h, m, w_post, w_next, self.get("up_a", l), self.get("up_g", l), wa, wg, ba, bg,
                             self.get("down", l), name="ffn_fwd",
                             carry=([self.own[k] for k in keys], "bcast") if keys else None)
        self.landed.update(zip(keys, got))
        return outs

    def grads_done(self, l, g, kinds):
        split = lambda a: a.reshape(4, 2, a.shape[-2] // 2, a.shape[-1]) if a.ndim == 3 else \
            a.reshape(4, 2, a.shape[0] // 8, a.shape[1])
        src = {"w_in": lambda: g["w_in"][l], "w_out": lambda: g["w_out"][l],
               "up": lambda: g["up"][l], "down": lambda: g["down"][l]}
        big = {k: split(src[k]()) for k in kinds}
        from_sib = _pair_exchange([big[k] for k in kinds], name=f"grads_pair_exchange_{l}_{kinds[0]}")
        for k, theirs in zip(kinds, from_sib):
            self.partial[k, l] = _pair_sum(big[k], theirs, self.c, name=f"pair_sum_{k}_{l}")

    def reduce(self):
        keys = [(k, l) for l in range(DEPTH) for k in BIG]
        late = [k for k in keys if k not in self.slots]
        self.slots.update(zip(late, _chip_exchange([self.partial[k] for k in late], "scatter",
                                                   name="grads_chip_exchange")))
        half = {}
        for k in keys:
            own = lax.dynamic_index_in_dim(self.partial[k], self.q, 0, keepdims=False)
            half[k] = _sum_slots(own, self.slots[k], name=f"chip_sum_{k[0]}_{k[1]}")
        other = dict(zip(keys, _pair_swap([half[k] for k in keys], name="grads_pair_swap")))
        return [([half[k, l] for l in range(DEPTH)], [other[k, l] for l in range(DEPTH)]) for k in BIG]


def _local_step(x_rows, target_rows, wts, pre_mix_norm, gla_gate_b, ret_norm_w, gla_norm_w, post_mix_norm,
                pre_ffn_norm, ffn_conv_b, post_ffn_norm):
    d = D_MODEL
    lp = x_rows.shape[0] + FRONT + BACK
    row = lambda a, l: a[l][None, :]
    rtab = _ret_tables(lp)
    gtab = _gla_tables()
    h0 = jnp.concatenate([jnp.zeros((PADF, d), F32), wts.meta(), x_rows, jnp.zeros((BACK, d), F32)], axis=0)
    target = jnp.pad(target_rows, ((FRONT, BACK), (0, 0)))

    saved = []
    h = h0
    _, hn = _resid_norm(h0, None, None, row(pre_mix_norm, 0), name="norm_in")
    loss_local = dy = None
    for l in range(DEPTH):
        s = {"h_in": h, "hn": hn}
        s["proj"] = wts.mm("proj", l, hn, wts.get("win", l))
        s["o_ret"], s["st_ret"] = _retention(s["proj"], rtab, name="retention")
        s["o_gla"], s["st_gla"], s["gz"] = _gla(s["proj"], wts.get("w2p", l), row(gla_gate_b, l), gtab, name="gla")
        s["merged"] = _merge(s["o_ret"], s["o_gla"], s["proj"], row(ret_norm_w, l), row(gla_norm_w, l), name="merge")
        s["m"] = wts.mm("mix_out", l, s["merged"], wts.get("wout", l))
        cw_a, cw_g = wts.get("cw", l)[:, :D_FF], wts.get("cw", l)[:, D_FF:]
        cb_a, cb_g = ffn_conv_b[l][None, :D_FF], ffn_conv_b[l][None, D_FF:]
        s["conv"] = (cw_a, cw_g, cb_a, cb_g)
        s["h_mid"], s["hn2"], s["ua"], s["ug"], s["act"], s["f"] = wts.ffn_fwd(
            l, h, s["m"], row(post_mix_norm, l), row(pre_ffn_norm, l), cw_a, cw_g, cb_a, cb_g)
        if l + 1 < DEPTH:
            h, hn = _resid_norm(s["h_mid"], s["f"], row(post_ffn_norm, l), row(pre_mix_norm, l + 1), name="resid_ffn")
        else:
            loss_local, dy, df_last, dw_last = _loss_head(s["h_mid"], s["f"], row(post_ffn_norm, l), target,
                                                          name="loss_head")
        saved.append(s)

    g = {k: [None] * DEPTH for k in ("pre_mix", "w_in", "w2", "gb", "ret_n", "gla_n", "w_out", "post_mix", "pre_ffn",
                                     "up", "cw", "cb", "down", "post_ffn")}
    dh_out, dhn_next = dy, None
    for l in reversed(range(DEPTH)):
        s = saved[l]
        cw_a, cw_g, cb_a, cb_g = s["conv"]
        if l + 1 < DEPTH:
            dh, df, g["pre_mix"][l + 1], g["post_ffn"][l] = _resid_norm_bwd(
                dh_out, dhn_next, saved[l + 1]["h_in"], s["f"], row(pre_mix_norm, l + 1), row(post_ffn_norm, l),
                name="resid_ffn_bwd")
        else:
            dh, df, g["post_ffn"][l] = dh_out, df_last, dw_last
        g["down"][l] = wts.mm("ffn_down_dw", l, s["act"], df, fn=_mm_tn, tn=512)
        du_a, du_g, dcw_a, dcw_g, dcb_a, dcb_g, dhn2 = _conv_act_bwd(
            s["ua"], s["ug"], df, wts.get("down", l), cw_a, cw_g, cb_a, cb_g, wts.get("up_a", l), wts.get("up_g", l),
            name="conv_act_bwd")
        g["cw"][l] = jnp.concatenate([dcw_a, dcw_g], axis=1)
        g["cb"][l] = jnp.concatenate([dcb_a, dcb_g], axis=1)[0]
        half_up = wts.mm("ffn_up_a_dw", l, s["hn2"], du_a, fn=_mm_tn, tn=D_FF // 2, blocks=(4, 0))
        g["up"][l] = _mm_tn(s["hn2"], du_g, tn=D_FF // 2, blocks=(4, 2), into=half_up, name="ffn_up_g_dw")
        dh, dm, g["pre_ffn"][l], g["post_mix"][l] = _resid_norm_bwd(
            dh, dhn2, s["h_mid"], s["m"], row(pre_ffn_norm, l), row(post_mix_norm, l), name="resid_mix_bwd")
        g["w_out"][l] = _mm_tn(s["merged"], dm, name="mix_out_dw")
        wts.grads_done(l, g, ("w_out", "up", "down"))
        dmerged = wts.mm("mix_out_dx", l, dm, wts.get("wout", l), nt=True)
        do_ret, do_gla, d_gate, g["ret_n"][l], g["gla_n"][l] = _merge_bwd(
            dmerged, s["o_ret"], s["o_gla"], s["proj"], row(ret_norm_w, l), row(gla_norm_w, l), name="merge_bwd")
        d_ret = _retention_bwd(s["proj"], do_ret, s["st_ret"], rtab, name="retention_bwd")
        d_gla, dw2, dgb = _gla_bwd(s["proj"], do_gla, s["st_gla"], s["gz"], wts.get("w2p", l), gtab, name="gla_bwd")
        g["w2"][l], g["gb"][l] = dw2[:GLA_RANK], dgb[0]
        pieces = (d_ret, d_gate, d_gla)
        g["w_in"][l] = _to_reference_chips(*[wts.mm(f"proj_dw_{i}", l, s["hn"], p, fn=_mm_tn)
                                             for i, p in enumerate(pieces)])
        win = wts.get("win", l)
        dhn_next = wts.mm("proj_dx", l, pieces, [win[:, 0:P_RET], win[:, P_RET:P_RET + P_GATE], win[:, P_RET + P_GATE:]],
                          fn=_mm_nt_sum)
        dh_out = dh
        wts.grads_done(l, g, ("w_in",))
    dh0, _, g["pre_mix"][0], _ = _resid_norm_bwd(dh_out, dhn_next, h0, None, row(pre_mix_norm, 0), None,
                                                 name="norm_in_bwd")
    return loss_local, dh0, g


def kernel(x, meta_tokens, pre_mix_norm, w_in, gla_gate_w2, gla_gate_b, ret_norm_w, gla_norm_w, w_out, post_mix_norm, pre_ffn_norm, ffn_up, ffn_conv_w, ffn_conv_b, ffn_down, post_ffn_norm, loss_target, m_meta_tokens, m_pre_mix_norm, m_w_in, m_gla_gate_w2, m_gla_gate_b, m_ret_norm_w, m_gla_norm_w, m_w_out, m_post_mix_norm, m_pre_ffn_norm, m_ffn_up, m_ffn_conv_w, m_ffn_conv_b, m_ffn_down, m_post_ffn_norm, v_meta_tokens, v_pre_mix_norm, v_w_in, v_gla_gate_w2, v_gla_gate_b, v_ret_norm_w, v_gla_norm_w, v_w_out, v_post_mix_norm, v_pre_ffn_norm, v_ffn_up, v_ffn_conv_w, v_ffn_conv_b, v_ffn_down, v_post_ffn_norm):
    xi, yi, ci = _place()
    chip = 2 * xi + yi
    seq = x.shape[1]
    d = D_MODEL
    wts = _ChipWeights(w_in, w_out, ffn_up, ffn_down, meta_tokens, gla_gate_w2, ffn_conv_w)
    loss_local, dh0, g = _local_step(x[0], loss_target[0], wts, pre_mix_norm, gla_gate_b, ret_norm_w, gla_norm_w,
                                     post_mix_norm, pre_ffn_norm, ffn_conv_b, post_ffn_norm)
    grad_x = dh0[FRONT:FRONT + seq][None]
    names = ("w_in", "w_out", "ffn_up", "ffn_down")
    big_halves = wts.reduce()

    small_full = [dh0[PADF:FRONT], jnp.stack(g["pre_mix"])[:, 0], jnp.stack(g["w2"]), jnp.stack(g["gb"]),
                  jnp.stack(g["ret_n"])[:, 0], jnp.stack(g["gla_n"])[:, 0], jnp.stack(g["post_mix"])[:, 0],
                  jnp.stack(g["pre_ffn"])[:, 0], jnp.stack(g["cw"]), jnp.stack(g["cb"]),
                  jnp.stack(g["post_ffn"])[:, 0]]
    small_sum = _unslab(_allreduce_small(_slab(small_full, 8), name="small_allreduce"), [a.shape for a in small_full])
    (g_meta, g_pre_mix, g_w2, g_gb, g_ret_n, g_gla_n, g_post_mix, g_pre_ffn, g_cw, g_cb, g_post_ffn) = small_sum
    g_meta = lax.dynamic_slice_in_dim(g_meta, chip * 256, 256, axis=1)
    g_w2 = lax.dynamic_slice_in_dim(g_w2, chip * 64, 64, axis=2)
    g_cw = lax.dynamic_slice_in_dim(g_cw, chip * 1408, 1408, axis=2)

    grads = [g_meta, g_pre_mix, None, g_w2, g_gb, g_ret_n, g_gla_n, None, g_post_mix, g_pre_ffn, None,
             g_cw, g_cb, None, g_post_ffn]
    ws = [meta_tokens, pre_mix_norm, w_in, gla_gate_w2, gla_gate_b, ret_norm_w, gla_norm_w, w_out, post_mix_norm,
          pre_ffn_norm, ffn_up, ffn_conv_w, ffn_conv_b, ffn_down, post_ffn_norm]
    ms = [m_meta_tokens, m_pre_mix_norm, m_w_in, m_gla_gate_w2, m_gla_gate_b, m_ret_norm_w, m_gla_norm_w, m_w_out,
          m_post_mix_norm, m_pre_ffn_norm, m_ffn_up, m_ffn_conv_w, m_ffn_conv_b, m_ffn_down, m_post_ffn_norm]
    vs = [v_meta_tokens, v_pre_mix_norm, v_w_in, v_gla_gate_w2, v_gla_gate_b, v_ret_norm_w, v_gla_norm_w, v_w_out,
          v_post_mix_norm, v_pre_ffn_norm, v_ffn_up, v_ffn_conv_w, v_ffn_conv_b, v_ffn_down, v_post_ffn_norm]
    big_idx = (2, 7, 10, 13)
    deltas, new_m, new_v = [None] * 15, [None] * 15, [None] * 15
    for i, nm, (mine, theirs) in zip(big_idx, names, big_halves):
        grads[i], deltas[i], new_m[i], new_v[i] = _adamw_halves(ws[i], ms[i], vs[i], mine, theirs, ci,
                                                                name=f"adamw_{nm}")
    small_idx = [i for i in range(15) if i not in big_idx]
    shapes = [ws[i].shape for i in small_idx]
    sd, sm, sv = _adamw(_slab([ws[i] for i in small_idx], 8), _slab([grads[i] for i in small_idx], 8),
                        _slab([ms[i] for i in small_idx], 8), _slab([vs[i] for i in small_idx], 8), name="adamw_small")
    for i, a, b, c_ in zip(small_idx, _unslab(sd, shapes), _unslab(sm, shapes), _unslab(sv, shapes)):
        deltas[i], new_m[i], new_v[i] = a, b, c_

    loss = lax.psum(loss_local, ("x", "y", "c"))
    return (loss, grad_x, *grads, *deltas, *new_m, *new_v)
```

```python
import functools
import math

import numpy as np
import jax
import jax.numpy as jnp
from jax import lax
from jax.experimental import pallas as pl
from jax.experimental.pallas import tpu as pltpu

F32 = jnp.float32
BF16 = jnp.bfloat16

D_MODEL = 1024
DEPTH = 2
N_META = 16
EPS = 1e-6
RET_HEADS = 4
RET_DK = 128
GLA_HEADS = 4
GLA_DK = 64
GLA_DV = 128
GLA_QK = GLA_HEADS * GLA_DK
GLA_V = GLA_HEADS * GLA_DV
GLA_RANK = 16
GLA_TAU = 16.0
D_FF = 2816
ROPE_BASE = 10000.0
IN_WIDTH = 3600
IN_PAD = 3840
C_RQ, C_RK, C_RV, C_RG, C_GR, C_GQ, C_GK, C_GV, C_GA = 0, 512, 1024, 1536, 2048, 2560, 2816, 3072, 3584
P_RET, P_GATE, P_GLA = 1536, 1024, 1280


def _to_kernel_cols(w):
    pad = jnp.zeros(w.shape[:-1] + (IN_PAD - IN_WIDTH,), w.dtype)
    return jnp.concatenate([w[..., 0:2048], w[..., 3072:3584], w[..., 2048:3072], w[..., 3584:3600], pad], axis=-1)


def _to_reference_chips(d_ret, d_gate, d_gla):
    segs = [(d_ret, 0, 0, 1536), (d_gate, 0, 1536, 512), (d_gla, 0, 2048, 1024), (d_gate, 512, 3072, 512),
            (d_gla, 1024, 3584, GLA_RANK)]
    per = IN_WIDTH // 4
    chips = []
    for j in range(4):
        lo, hi, parts = per * j, per * (j + 1), []
        for piece, p0, r0, width in segs:
            a, b = max(lo, r0), min(hi, r0 + width)
            if a < b:
                parts.append(piece[:, p0 + a - r0:p0 + b - r0])
        chips.append(jnp.concatenate(parts, axis=1))
    return jnp.stack(chips)

FRONT = 64
BACK = 64
PADF = FRONT - N_META
RET_CHUNK = 128
GLA_CHUNK = 64
GLA_SUB = 16
GLA_SUB2 = 4
BLK = 640

ADAM_LR, ADAM_B1, ADAM_B2, ADAM_EPS, ADAM_WD, ADAM_STEP = 0.001, 0.9, 0.999, 1e-08, 0.01, 10

VMEM_LIMIT = 56 * 2 ** 20
MM_VMEM_BUDGET = 40 * 2 ** 20
MESH = pl.DeviceIdType.MESH


def _cp(*sem):
    return pltpu.CompilerParams(dimension_semantics=sem, vmem_limit_bytes=VMEM_LIMIT)


def _tile(n, cands):
    for t in cands:
        if n % t == 0:
            return t
    raise ValueError(f"no tile for {n} in {cands}")


def _row_tile(n):
    return _tile(n, (640, 512, 320, 256, 128, 64))


def _mm(a, b, *, nt=False, add=None, out_dtype=F32, tn=None, name, carry=None):
    m, k = a.shape
    n = b.shape[0] if nt else b.shape[1]
    tm = _tile(m, (640, 320, 256, 128, 64))
    if tn is None:
        step_bytes = lambda t: 2 * (tm * k * a.dtype.itemsize + t * k * b.dtype.itemsize
                                    + tm * t * (jnp.dtype(out_dtype).itemsize + (4 if add is not None else 0)))
        tn = next(t for t in range(n, 0, -128) if n % t == 0 and (step_bytes(t) <= MM_VMEM_BUDGET or t == 128))
    dn = (((1,), (1,)), ((), ())) if nt else (((1,), (0,)), ((), ()))
    nj, ni = n // tn, m // tm
    n_in = 2 + (add is not None)
    c_arrs, c_mode = carry if carry is not None else ((), None)
    nc = len(c_arrs)

    def body(*refs):
        a_ref, b_ref = refs[:2]
        c_ref = refs[2] if add is not None else None
        o_ref = refs[n_in + nc]
        if nc:
            c_ins, c_outs = refs[n_in:n_in + nc], refs[n_in + nc + 1:n_in + 2 * nc + 1]
            ssem, rsem = refs[n_in + 2 * nc + 1:]
            j, i = pl.program_id(0), pl.program_id(1)

            @pl.when((j == 0) & (i == 0))
            def _():
                for cp in _chip_copies(c_ins, c_outs, ssem, rsem, c_mode)[0]:
                    cp.start()
        r = lax.dot_general(a_ref[...].astype(BF16), b_ref[...].astype(BF16), dn, preferred_element_type=F32)
        if add is not None:
            r = r + c_ref[...]
        o_ref[...] = r.astype(o_ref.dtype)
        if nc:
            @pl.when((j == nj - 1) & (i == ni - 1))
            def _():
                _chip_wait(*_chip_copies(c_ins, c_outs, ssem, rsem, c_mode))

    b_spec = pl.BlockSpec((tn, k), lambda j, i: (j, 0)) if nt else pl.BlockSpec((k, tn), lambda j, i: (0, j))
    in_specs = [pl.BlockSpec((tm, k), lambda j, i: (i, 0)), b_spec]
    args = [a, b]
    if add is not None:
        in_specs.append(pl.BlockSpec((tm, tn), lambda j, i: (i, j)))
        args.append(add)
    out_shape = jax.ShapeDtypeStruct((m, n), out_dtype)
    out_spec = pl.BlockSpec((tm, tn), lambda j, i: (i, j))
    if not nc:
        return pl.pallas_call(
            body, out_shape=out_shape, grid=(nj, ni), in_specs=in_specs, out_specs=out_spec,
            compiler_params=_cp("parallel", "parallel"), name=name)(*args)
    sem = pltpu.SemaphoreType.DMA
    outs = pl.pallas_call(
        body, out_shape=(out_shape,) + tuple(_landing_shape(x, c_mode) for x in c_arrs), grid=(nj, ni),
        in_specs=in_specs + [ANY] * nc, out_specs=(out_spec,) + (ANY,) * nc,
        scratch_shapes=[sem((nc, 3)), sem((nc, 3))],
        compiler_params=_cp("arbitrary", "arbitrary"), name=name)(*args, *c_arrs)
    return outs[0], list(outs[1:])


def _call_with_carry(body, *, out_shape, grid, in_specs, out_specs, args, semantics, carry, name, aliases=None):
    if carry is None:
        return pl.pallas_call(body, out_shape=out_shape, grid=grid, in_specs=in_specs, out_specs=out_specs,
                              input_output_aliases=aliases or {}, compiler_params=_cp(*semantics), name=name)(*args)
    c_arrs, c_mode = carry
    n_in, nc = len(args), len(c_arrs)

    def carried(*refs):
        c_ins, c_outs = refs[n_in:n_in + nc], refs[n_in + nc + 1:n_in + 2 * nc + 1]
        ssem, rsem = refs[n_in + 2 * nc + 1:]
        ids = [pl.program_id(d) for d in range(len(grid))]
        first = functools.reduce(lambda u, v: u & v, [i == 0 for i in ids])
        last = functools.reduce(lambda u, v: u & v, [i == g - 1 for i, g in zip(ids, grid)])

        @pl.when(first)
        def _():
            for cp in _chip_copies(c_ins, c_outs, ssem, rsem, c_mode)[0]:
                cp.start()
        body(*refs[:n_in], refs[n_in + nc])

        @pl.when(last)
        def _():
            _chip_wait(*_chip_copies(c_ins, c_outs, ssem, rsem, c_mode))

    sem = pltpu.SemaphoreType.DMA
    outs = pl.pallas_call(
        carried, out_shape=(out_shape,) + tuple(_landing_shape(x, c_mode) for x in c_arrs), grid=grid,
        in_specs=list(in_specs) + [ANY] * nc, out_specs=(out_specs,) + (ANY,) * nc,
        scratch_shapes=[sem((nc, 3)), sem((nc, 3))], input_output_aliases=aliases or {},
        compiler_params=_cp(*(("arbitrary",) * len(grid))), name=name)(*args, *c_arrs)
    return outs[0], list(outs[1:])


def _mm_nt_sum(a_list, b_list, *, name, carry=None):
    m, n = a_list[0].shape[0], b_list[0].shape[0]
    tm = _tile(m, (640, 320, 256, 128, 64))
    np_ = len(a_list)

    def body(*refs):
        acc = None
        for a_ref, b_ref in zip(refs[:np_], refs[np_:2 * np_]):
            r = lax.dot_general(a_ref[...].astype(BF16), b_ref[...].astype(BF16), (((1,), (1,)), ((), ())),
                                preferred_element_type=F32)
            acc = r if acc is None else acc + r
        refs[2 * np_][...] = acc

    return _call_with_carry(
        body, out_shape=jax.ShapeDtypeStruct((m, n), F32), grid=(m // tm,),
        in_specs=[pl.BlockSpec((tm, a.shape[1]), lambda i: (i, 0)) for a in a_list]
        + [pl.BlockSpec(b.shape, lambda i: (0, 0)) for b in b_list],
        out_specs=pl.BlockSpec((tm, n), lambda i: (i, 0)), args=[*a_list, *b_list], semantics=("parallel",),
        carry=carry, name=name)


def _mm_tn(a, b, *, tn=None, blocks=None, into=None, name, carry=None):
    m, k = a.shape
    n = b.shape[1]
    tm = _tile(m, (1664, 640, 320, 256, 128, 64))
    tn = n if tn is None else tn
    if blocks is not None:
        total, first = blocks
        out_shape = jax.ShapeDtypeStruct((total, k, tn), F32)
        out_spec = pl.BlockSpec((None, k, tn), lambda j, i: (first + j, 0, 0))
    else:
        out_shape = jax.ShapeDtypeStruct((k, n), F32)
        out_spec = pl.BlockSpec((k, tn), lambda j, i: (0, j))

    def body(a_ref, b_ref, *rest):
        o_ref = rest[-1]

        @pl.when(pl.program_id(1) == 0)
        def _():
            o_ref[...] = jnp.zeros_like(o_ref)
        o_ref[...] += lax.dot_general(a_ref[...].astype(BF16), b_ref[...].astype(BF16),
                                      (((0,), (0,)), ((), ())), preferred_element_type=F32)

    in_specs = [pl.BlockSpec((tm, k), lambda j, i: (i, 0)), pl.BlockSpec((tm, tn), lambda j, i: (i, j))]
    args, alias = [a, b], {}
    if into is not None:
        in_specs.append(pl.BlockSpec(memory_space=pl.ANY))
        args.append(into)
        alias = {2: 0}
    return _call_with_carry(body, out_shape=out_shape, grid=(n // tn, m // tm), in_specs=in_specs, out_specs=out_spec,
                            args=args, semantics=("parallel", "arbitrary"), carry=carry, name=name, aliases=alias)


def _rms(x, w):
    r = lax.rsqrt(jnp.mean(x * x, axis=-1, keepdims=True) + EPS)
    return x * r * w


def _rms_bwd(x, w, dy):
    r = lax.rsqrt(jnp.mean(x * x, axis=-1, keepdims=True) + EPS)
    xh = x * r
    dxh = dy * w
    dx = r * (dxh - xh * jnp.mean(dxh * xh, axis=-1, keepdims=True))
    return dx, jnp.sum(dy * xh, axis=0, keepdims=True)


def _resid_norm(h, t, w_post, w_next, *, name):
    lp, d = h.shape
    tm = _row_tile(lp)
    has_t = t is not None

    def body(*refs):
        if has_t:
            h_ref, t_ref, wp_ref, wn_ref, ho_ref, hn_ref = refs
            hv = h_ref[...] + _rms(t_ref[...], wp_ref[...])
            ho_ref[...] = hv
        else:
            h_ref, wn_ref, hn_ref = refs
            hv = h_ref[...]
        hn_ref[...] = _rms(hv, wn_ref[...]).astype(BF16)

    row = pl.BlockSpec((tm, d), lambda i: (i, 0))
    vec = pl.BlockSpec((1, d), lambda i: (0, 0))
    if has_t:
        return pl.pallas_call(
            body, out_shape=(jax.ShapeDtypeStruct((lp, d), F32), jax.ShapeDtypeStruct((lp, d), BF16)),
            grid=(lp // tm,), in_specs=[row, row, vec, vec], out_specs=(row, row),
            compiler_params=_cp("parallel"), name=name)(h, t, w_post, w_next)
    return h, pl.pallas_call(
        body, out_shape=jax.ShapeDtypeStruct((lp, d), BF16), grid=(lp // tm,), in_specs=[row, vec],
        out_specs=row, compiler_params=_cp("parallel"), name=name)(h, w_next)


def _resid_norm_bwd(dh_out, dhn, h_new, t, w_next, w_post, *, name):
    lp, d = h_new.shape if h_new is not None else t.shape
    tm = _row_tile(lp)
    has_n = dhn is not None
    has_t = t is not None

    def body(*refs):
        refs = list(refs)
        dho_ref = refs.pop(0)
        if has_n:
            dhn_ref, hn_ref, wn_ref = refs.pop(0), refs.pop(0), refs.pop(0)
        if has_t:
            t_ref, wp_ref = refs.pop(0), refs.pop(0)
        dh_ref = refs.pop(0) if has_n else None
        dt_ref = refs.pop(0) if has_t else None
        dwn_ref = refs.pop(0) if has_n else None
        dwp_ref = refs.pop(0) if has_t else None
        first = pl.program_id(0) == 0
        dh = dho_ref[...]
        if has_n:
            dx, dwn = _rms_bwd(hn_ref[...], wn_ref[...], dhn_ref[...])
            dh = dh + dx
            dh_ref[...] = dh

            @pl.when(first)
            def _():
                dwn_ref[...] = jnp.zeros_like(dwn_ref)
            dwn_ref[...] += dwn
        if has_t:
            dt, dwp = _rms_bwd(t_ref[...], wp_ref[...], dh)
            dt_ref[...] = dt.astype(BF16)

            @pl.when(first)
            def _():
                dwp_ref[...] = jnp.zeros_like(dwp_ref)
            dwp_ref[...] += dwp

    row = pl.BlockSpec((tm, d), lambda i: (i, 0))
    vec = pl.BlockSpec((1, d), lambda i: (0, 0))
    args, in_specs, out_shape, out_specs = [dh_out], [row], [], []
    if has_n:
        args += [dhn, h_new, w_next]
        in_specs += [row, row, vec]
    if has_t:
        args += [t, w_post]
        in_specs += [row, vec]
    if has_n:
        out_shape.append(jax.ShapeDtypeStruct((lp, d), F32)); out_specs.append(row)
    if has_t:
        out_shape.append(jax.ShapeDtypeStruct((lp, d), BF16)); out_specs.append(row)
    if has_n:
        out_shape.append(jax.ShapeDtypeStruct((1, d), F32)); out_specs.append(vec)
    if has_t:
        out_shape.append(jax.ShapeDtypeStruct((1, d), F32)); out_specs.append(vec)
    outs = list(pl.pallas_call(body, out_shape=tuple(out_shape), grid=(lp // tm,), in_specs=in_specs,
                               out_specs=tuple(out_specs), compiler_params=_cp("arbitrary"), name=name)(*args))
    dh = outs.pop(0) if has_n else dh_out
    dt = outs.pop(0) if has_t else None
    dwn = outs.pop(0) if has_n else None
    dwp = outs.pop(0) if has_t else None
    return dh, dt, dwn, dwp


def _loss_head(h, f, w_post, target, *, name):
    lp, d = h.shape
    tm = _row_tile(lp)

    def body(h_ref, f_ref, w_ref, t_ref, loss_ref, dy_ref, df_ref, dw_ref):
        i = pl.program_id(0)
        f, w = f_ref[...], w_ref[...]
        y = h_ref[...] + _rms(f, w)
        rows = i * tm + lax.broadcasted_iota(jnp.int32, (tm, 1), 0)
        tok = (rows >= FRONT) & (rows < lp - BACK)
        err = jnp.where(tok, y - t_ref[...], 0.0)
        dy = err * (1.0 / d)
        dy_ref[...] = dy
        df, dw = _rms_bwd(f, w, dy)
        df_ref[...] = df.astype(BF16)

        @pl.when(i == 0)
        def _():
            loss_ref[...] = jnp.zeros_like(loss_ref)
            dw_ref[...] = jnp.zeros_like(dw_ref)
        part = jnp.sum(jnp.sum(err * err, axis=1, keepdims=True), axis=0, keepdims=True) * (0.5 / d)
        loss_ref[...] += jnp.broadcast_to(part, loss_ref.shape)
        dw_ref[...] += dw

    row = pl.BlockSpec((tm, d), lambda i: (i, 0))
    vec = pl.BlockSpec((1, d), lambda i: (0, 0))
    loss, dy, df, dw = pl.pallas_call(
        body, out_shape=(jax.ShapeDtypeStruct((8, 128), F32), jax.ShapeDtypeStruct((lp, d), F32),
                         jax.ShapeDtypeStruct((lp, d), BF16), jax.ShapeDtypeStruct((1, d), F32)),
        grid=(lp // tm,), in_specs=[row, row, vec, row],
        out_specs=(pl.BlockSpec((8, 128), lambda i: (0, 0)), row, row, vec),
        compiler_params=_cp("arbitrary"), name=name)(h, f, w_post, target)
    return loss[0, 0], dy, df, dw


_GELU_C = math.sqrt(2.0 / math.pi)


def _gelu_and_grad(a):
    a2 = a * a
    t = jnp.tanh(a * (_GELU_C + (_GELU_C * 0.044715) * a2))
    ha = 0.5 * a
    h1 = 0.5 + 0.5 * t
    return a * h1, h1 + ha * (1.0 - t * t) * (_GELU_C + (3.0 * _GELU_C * 0.044715) * a2)


def _gelu(a):
    t = jnp.tanh(a * (_GELU_C + (_GELU_C * 0.044715) * (a * a)))
    return a * (0.5 + 0.5 * t)


def _conv3(parts, n, w, b):
    xx = jnp.concatenate(parts, axis=0)
    return b + xx[8:8 + n] * w[2:3] + pltpu.roll(xx, 1, 0)[8:8 + n] * w[1:2] + pltpu.roll(xx, 2, 0)[8:8 + n] * w[0:1]


def _conv_act(ua, ug, wa, wg, ba, bg, *, name):
    lp, n = ua.shape
    tm = _row_tile(lp)
    tc = _tile(n, (256, 128))
    nb8 = tm // 8

    def body(ua_ref, uap_ref, ug_ref, ugp_ref, wa_ref, wg_ref, ba_ref, bg_ref, o_ref):
        i = pl.program_id(0)
        ca = _conv3([uap_ref[...], ua_ref[...]], tm, wa_ref[...], ba_ref[...])
        cg = _conv3([ugp_ref[...], ug_ref[...]], tm, wg_ref[...], bg_ref[...])
        rows = i * tm + lax.broadcasted_iota(jnp.int32, (tm, 1), 0)
        ok = (rows >= PADF) & (rows < lp - BACK)
        o_ref[...] = jnp.where(ok, _gelu(ca) * cg, 0.0).astype(BF16)

    cur = pl.BlockSpec((tm, tc), lambda i, j: (i, j))
    prev = pl.BlockSpec((8, tc), lambda i, j: (jnp.maximum(i * nb8 - 1, 0), j))
    w3 = pl.BlockSpec((3, tc), lambda i, j: (0, j))
    b1 = pl.BlockSpec((1, tc), lambda i, j: (0, j))
    return pl.pallas_call(
        body, out_shape=jax.ShapeDtypeStruct((lp, n), BF16), grid=(lp // tm, n // tc),
        in_specs=[cur, prev, cur, prev, w3, w3, b1, b1], out_specs=cur,
        compiler_params=_cp("parallel", "parallel"), name=name)(ua, ua, ug, ug, wa, wg, ba, bg)


def _conv_act_down(ua, ug, wa, wg, ba, bg, down, *, name):
    lp, n = ua.shape
    d = down.shape[1]
    tm = _tile(lp, (320, 256, 128, 64))
    tc = _tile(n, (256, 128))
    nb8 = tm // 8

    def body(ua_ref, uap_ref, ug_ref, ugp_ref, wa_ref, wg_ref, ba_ref, bg_ref, dn_ref, act_ref, f_ref):
        i = pl.program_id(0)
        rows = i * tm + lax.broadcasted_iota(jnp.int32, (tm, 1), 0)
        ok = (rows >= PADF) & (rows < lp - BACK)
        acc = None
        for j in range(n // tc):
            cs = slice(tc * j, tc * j + tc)
            ca = _conv3([uap_ref[:, cs], ua_ref[:, cs]], tm, wa_ref[:, cs], ba_ref[:, cs])
            cg = _conv3([ugp_ref[:, cs], ug_ref[:, cs]], tm, wg_ref[:, cs], bg_ref[:, cs])
            act = jnp.where(ok, _gelu(ca) * cg, 0.0).astype(BF16)
            act_ref[:, cs] = act
            part = _dot(act, dn_ref[cs, :])
            acc = part if acc is None else acc + part
        f_ref[...] = acc

    cur = pl.BlockSpec((tm, n), lambda i: (i, 0))
    prev = pl.BlockSpec((8, n), lambda i: (jnp.maximum(i * nb8 - 1, 0), 0))
    w3 = pl.BlockSpec((3, n), lambda i: (0, 0))
    b1 = pl.BlockSpec((1, n), lambda i: (0, 0))
    return pl.pallas_call(
        body, out_shape=(jax.ShapeDtypeStruct((lp, n), BF16), jax.ShapeDtypeStruct((lp, d), F32)),
        grid=(lp // tm,),
        in_specs=[cur, prev, cur, prev, w3, w3, b1, b1, pl.BlockSpec(down.shape, lambda i: (0, 0))],
        out_specs=(cur, pl.BlockSpec((tm, d), lambda i: (i, 0))),
        compiler_params=_cp("parallel"), name=name)(ua, ua, ug, ug, wa, wg, ba, bg, down)


def _ffn_fwd(h, m, w_post, w_next, up_a, up_g, wa, wg, ba, bg, down, *, name, carry=None):
    lp, d = h.shape
    n = up_a.shape[1]
    tm = _tile(lp, (320, 256, 128, 64))
    tc = _tile(n, (256, 128))
    nchunks = n // tc
    c_arrs, c_mode = carry if carry is not None else ((), None)
    nc = len(c_arrs)
    steps = lp // tm
    n_out = 6

    def body(h_ref, hp_ref, m_ref, mp_ref, wp_ref, wn_ref, upa_ref, upg_ref, wa_ref, wg_ref, ba_ref, bg_ref, dn_ref,
             *rest):
        c_ins = rest[:nc]
        hmid_ref, hn_ref, ua_ref, ug_ref, act_ref, f_ref = rest[nc:nc + n_out]
        c_outs = rest[nc + n_out:2 * nc + n_out]
        i = pl.program_id(0)
        if nc:
            ssem, rsem = rest[2 * nc + n_out:]

            @pl.when(i == 0)
            def _():
                for cp in _chip_copies(c_ins, c_outs, ssem, rsem, c_mode)[0]:
                    cp.start()
        rows = i * tm + lax.broadcasted_iota(jnp.int32, (tm, 1), 0)
        ok = (rows >= PADF) & (rows < lp - BACK)
        hv = (jnp.concatenate([hp_ref[...], h_ref[...]], axis=0)
              + _rms(jnp.concatenate([mp_ref[...], m_ref[...]], axis=0), wp_ref[...]))
        x = _rms(hv, wn_ref[...]).astype(BF16)
        hmid_ref[...] = hv[16:]
        hn_ref[...] = x[16:]
        u_of = lambda j: (_dot(x, upa_ref[:, tc * j:tc * j + tc]), _dot(x, upg_ref[:, tc * j:tc * j + tc]))
        u_next = u_of(0)
        acc = None
        for j in range(nchunks):
            cs = slice(tc * j, tc * j + tc)
            ua, ug = u_next
            if j + 1 < nchunks:
                u_next = u_of(j + 1)
            ua_ref[:, cs] = ua[16:]
            ug_ref[:, cs] = ug[16:]
            ca = _conv3([ua[8:]], tm, wa_ref[:, cs], ba_ref[:, cs])
            cg = _conv3([ug[8:]], tm, wg_ref[:, cs], bg_ref[:, cs])
            act = jnp.where(ok, _gelu(ca) * cg, 0.0).astype(BF16)
            act_ref[:, cs] = act
            part = _dot(act, dn_ref[cs, :])
            acc = part if acc is None else acc + part
        f_ref[...] = acc
        if nc:
            @pl.when(i == steps - 1)
            def _():
                _chip_wait(*_chip_copies(c_ins, c_outs, ssem, rsem, c_mode))

    whole = pl.BlockSpec(memory_space=pltpu.VMEM)
    wide = pl.BlockSpec((tm, n), lambda i: (i, 0))
    w3 = pl.BlockSpec((3, n), lambda i: (0, 0))
    b1 = pl.BlockSpec((1, n), lambda i: (0, 0))
    sem = pltpu.SemaphoreType.DMA
    row = pl.BlockSpec((tm, d), lambda i: (i, 0))
    prev16 = pl.BlockSpec((16, d), lambda i: (jnp.maximum(i * (tm // 16) - 1, 0), 0))
    vec = pl.BlockSpec((1, d), lambda i: (0, 0))
    outs = pl.pallas_call(
        body,
        out_shape=(jax.ShapeDtypeStruct((lp, d), F32), jax.ShapeDtypeStruct((lp, d), BF16),
                   jax.ShapeDtypeStruct((lp, n), F32), jax.ShapeDtypeStruct((lp, n), F32),
                   jax.ShapeDtypeStruct((lp, n), BF16), jax.ShapeDtypeStruct((lp, d), F32))
        + tuple(_landing_shape(a, c_mode) for a in c_arrs),
        grid=(steps,),
        in_specs=[row, prev16, row, prev16, vec, vec, whole, whole, w3, w3, b1, b1, whole] + [ANY] * nc,
        out_specs=(row, row, wide, wide, wide, row) + (ANY,) * nc,
        scratch_shapes=[sem((nc, 3)), sem((nc, 3))] if nc else [],
        compiler_params=_cp("arbitrary"), name=name)(h, h, m, m, w_post, w_next, up_a, up_g, wa, wg, ba, bg, down,
                                                     *c_arrs)
    return outs[:n_out], list(outs[n_out:])


def _conv_act_bwd(ua, ug, df, down, wa, wg, ba, bg, up_a, up_g, *, name):
    lp, n = ua.shape
    d = up_a.shape[0]
    tm = _tile(lp, (320, 256, 128, 64))
    tc = _tile(n, (256, 128))
    nb8 = tm // 8
    last8 = lp // 8 - 1
    last16 = lp // 16 - 1
    ext = tm + 8

    def body(ua_ref, uap_ref, uan_ref, ug_ref, ugp_ref, ugn_ref, df_ref, dfn_ref, dn_ref, wa_ref, wg_ref, ba_ref,
             bg_ref, upa_ref, upg_ref, dua_ref, dug_ref, dwa_ref, dwg_ref, dba_ref, dbg_ref, dhn_ref):
        i = pl.program_id(0)
        df_ext = jnp.concatenate([df_ref[...], dfn_ref[...]], axis=0)

        @pl.when(i == 0)
        def _():
            dwa_ref[...] = jnp.zeros_like(dwa_ref)
            dwg_ref[...] = jnp.zeros_like(dwg_ref)
            dba_ref[...] = jnp.zeros_like(dba_ref)
            dbg_ref[...] = jnp.zeros_like(dbg_ref)
        rows = i * tm + lax.broadcasted_iota(jnp.int32, (ext, 1), 0)
        ok = (rows >= PADF) & (rows < lp - BACK)

        def conv(parts, w, b):
            xx = jnp.concatenate(parts, axis=0)
            x, x1, x2 = xx[8:8 + ext], pltpu.roll(xx, 1, 0)[8:8 + ext], pltpu.roll(xx, 2, 0)[8:8 + ext]
            return b + x * w[2:3] + x1 * w[1:2] + x2 * w[0:1], x, x1, x2

        def back(dc, w):
            return (dc[:tm] * w[2:3] + pltpu.roll(dc, ext - 1, 0)[:tm] * w[1:2]
                    + pltpu.roll(dc, ext - 2, 0)[:tm] * w[0:1])

        def wsum(dw_ref, db_ref, cs, dc, x, x1, x2):
            dd = dc[:tm]
            s = lambda v: jnp.sum(v, axis=0, keepdims=True)
            dw_ref[0:1, cs] += s(dd * x2[:tm])
            dw_ref[1:2, cs] += s(dd * x1[:tm])
            dw_ref[2:3, cs] += s(dd * x[:tm])
            db_ref[:, cs] += s(dd)

        acc = None
        nchunks = n // tc
        dact_of = lambda j: _dot_nt(df_ext, dn_ref[tc * j:tc * j + tc, :])[:ext]
        dact_next = dact_of(0)
        for j in range(nchunks):
            cs = slice(tc * j, tc * j + tc)
            dact_cur = dact_next
            if j + 1 < nchunks:
                dact_next = dact_of(j + 1)
            wa, wg = wa_ref[:, cs], wg_ref[:, cs]
            ca, xa, xa1, xa2 = conv([uap_ref[:, cs], ua_ref[:, cs], uan_ref[:, cs]], wa, ba_ref[:, cs])
            cg, xg, xg1, xg2 = conv([ugp_ref[:, cs], ug_ref[:, cs], ugn_ref[:, cs]], wg, bg_ref[:, cs])
            dact_e = jnp.where(ok, dact_cur, 0.0)
            gel, gel_d = _gelu_and_grad(ca)
            dca = dact_e * cg * gel_d
            dcg = dact_e * gel
            du_a, du_g = back(dca, wa).astype(BF16), back(dcg, wg).astype(BF16)
            dua_ref[:, cs] = du_a
            dug_ref[:, cs] = du_g
            wsum(dwa_ref, dba_ref, cs, dca, xa, xa1, xa2)
            wsum(dwg_ref, dbg_ref, cs, dcg, xg, xg1, xg2)
            part = _dot_nt(du_a, upa_ref[:, cs]) + _dot_nt(du_g, upg_ref[:, cs])
            acc = part if acc is None else acc + part
        dhn_ref[...] = acc

    cur = pl.BlockSpec((tm, n), lambda i: (i, 0))
    prev = pl.BlockSpec((8, n), lambda i: (jnp.maximum(i * nb8 - 1, 0), 0))
    nxt = pl.BlockSpec((8, n), lambda i: (jnp.minimum((i + 1) * nb8, last8), 0))
    w3 = pl.BlockSpec((3, n), lambda i: (0, 0))
    b1 = pl.BlockSpec((1, n), lambda i: (0, 0))
    whole = pl.BlockSpec(memory_space=pltpu.VMEM)
    return pl.pallas_call(
        body,
        out_shape=(jax.ShapeDtypeStruct((lp, n), BF16), jax.ShapeDtypeStruct((lp, n), BF16),
                   jax.ShapeDtypeStruct((3, n), F32), jax.ShapeDtypeStruct((3, n), F32),
                   jax.ShapeDtypeStruct((1, n), F32), jax.ShapeDtypeStruct((1, n), F32),
                   jax.ShapeDtypeStruct((lp, d), F32)),
        grid=(lp // tm,),
        in_specs=[cur, prev, nxt, cur, prev, nxt, pl.BlockSpec((tm, d), lambda i: (i, 0)),
                  pl.BlockSpec((16, d), lambda i: (jnp.minimum((i + 1) * (tm // 16), last16), 0)), whole,
                  w3, w3, b1, b1, whole, whole],
        out_specs=(cur, cur, w3, w3, b1, b1, pl.BlockSpec((tm, d), lambda i: (i, 0))),
        compiler_params=_cp("arbitrary"), name=name)(ua, ua, ua, ug, ug, ug, df, df, down, wa, wg, ba, bg, up_a, up_g)


def _sigmoid(x):
    return 1.0 / (1.0 + jnp.exp(-x))


def _merge_mix_out(o_ret, o_gla, proj, w_ret, w_gla, wout, *, name, carry=None):
    lp = o_ret.shape[0]
    d = wout.shape[1]
    tm = _row_tile(lp)
    steps = lp // tm
    c_arrs, c_mode = carry if carry is not None else ((), None)
    nc = len(c_arrs)

    def body(or_ref, og_ref, rg_ref, gr_ref, wr_ref, wg_ref, wo_ref, *rest):
        c_ins = rest[:nc]
        m_ref, out_ref = rest[nc:nc + 2]
        c_outs = rest[nc + 2:2 * nc + 2]
        i = pl.program_id(0)
        if nc:
            ssem, rsem = rest[2 * nc + 2:]

            @pl.when(i == 0)
            def _():
                for cp in _chip_copies(c_ins, c_outs, ssem, rsem, c_mode)[0]:
                    cp.start()
        oret, ogla = or_ref[...], og_ref[...]
        yr, yg = [], []
        for h in range(4):
            hs = slice(128 * h, 128 * h + 128)
            o = oret[:, hs]
            xc = o - jnp.mean(o, axis=-1, keepdims=True)
            yr.append(xc * lax.rsqrt(jnp.mean(xc * xc, axis=-1, keepdims=True) + EPS))
            o = ogla[:, hs]
            yg.append(o * lax.rsqrt(jnp.mean(o * o, axis=-1, keepdims=True) + EPS))
        rg, gr = rg_ref[...], gr_ref[...]
        ret = (jnp.concatenate(yr, axis=1) * wr_ref[...] * (rg * _sigmoid(rg))).astype(BF16)
        gla = (jnp.concatenate(yg, axis=1) * wg_ref[...] * (gr * _sigmoid(gr))).astype(BF16)
        m_ref[:, 0:512] = ret
        m_ref[:, 512:1024] = gla
        out_ref[...] = _dot(ret, wo_ref[0:512, :]) + _dot(gla, wo_ref[512:1024, :])
        if nc:
            @pl.when(i == steps - 1)
            def _():
                _chip_wait(*_chip_copies(c_ins, c_outs, ssem, rsem, c_mode))

    row = pl.BlockSpec((tm, 512), lambda i: (i, 0))
    vec = pl.BlockSpec((1, 512), lambda i: (0, 0))
    wide = pl.BlockSpec((tm, 1024), lambda i: (i, 0))
    sem = pltpu.SemaphoreType.DMA
    outs = pl.pallas_call(
        body, out_shape=(jax.ShapeDtypeStruct((lp, 1024), BF16), jax.ShapeDtypeStruct((lp, d), F32))
        + tuple(_landing_shape(a, c_mode) for a in c_arrs),
        grid=(steps,),
        in_specs=[row, row, pl.BlockSpec((tm, 512), lambda i: (i, C_RG // 512)),
                  pl.BlockSpec((tm, 512), lambda i: (i, C_GR // 512)), vec, vec,
                  pl.BlockSpec(memory_space=pltpu.VMEM)] + [ANY] * nc,
        out_specs=(wide, pl.BlockSpec((tm, d), lambda i: (i, 0))) + (ANY,) * nc,
        scratch_shapes=[sem((nc, 3)), sem((nc, 3))] if nc else [],
        compiler_params=_cp("arbitrary"), name=name)(o_ret, o_gla, proj, proj, w_ret, w_gla, wout, *c_arrs)
    return outs[:2], list(outs[2:])


def _merge_bwd(dm, o_ret, o_gla, proj, w_ret, w_gla, *, name):
    lp = o_ret.shape[0]
    tm = _row_tile(lp)

    def body(dm_ref, or_ref, og_ref, rg_ref, gr_ref, wr_ref, wg_ref, dor_ref, dog_ref, dgate_ref, dwr_ref, dwg_ref):
        @pl.when(pl.program_id(0) == 0)
        def _():
            dwr_ref[...] = jnp.zeros_like(dwr_ref)
            dwg_ref[...] = jnp.zeros_like(dwg_ref)

        def group(d, o_all, gate, w, center):
            sg = _sigmoid(gate)
            s = gate * sg
            ds = sg * (1.0 + gate * (1.0 - sg))
            xh, rr = [], []
            for h in range(4):
                o = o_all[:, 128 * h:128 * h + 128]
                if center:
                    o = o - jnp.mean(o, axis=-1, keepdims=True)
                r = lax.rsqrt(jnp.mean(o * o, axis=-1, keepdims=True) + EPS)
                xh.append(o * r)
                rr.append(r)
            xh_all = jnp.concatenate(xh, axis=1)
            dgate = d * xh_all * w * ds
            dw = jnp.sum(d * xh_all * s, axis=0, keepdims=True)
            dxh_all = d * w * s
            do = []
            for h in range(4):
                dxh = dxh_all[:, 128 * h:128 * h + 128]
                t = dxh - xh[h] * jnp.mean(dxh * xh[h], axis=-1, keepdims=True)
                if center:
                    t = t - jnp.mean(dxh, axis=-1, keepdims=True)
                do.append(rr[h] * t)
            return jnp.concatenate(do, axis=1), dgate, dw

        dmv = dm_ref[...]
        do, dg, dw = group(dmv[:, 0:512], or_ref[...], rg_ref[...], wr_ref[...], True)
        dor_ref[...] = do
        dgate_ref[:, 0:512] = dg.astype(BF16)
        dwr_ref[...] += dw
        do, dg, dw = group(dmv[:, 512:1024], og_ref[...], gr_ref[...], wg_ref[...], False)
        dog_ref[...] = do
        dgate_ref[:, 512:1024] = dg.astype(BF16)
        dwg_ref[...] += dw

    row = pl.BlockSpec((tm, 512), lambda i: (i, 0))
    vec = pl.BlockSpec((1, 512), lambda i: (0, 0))
    return pl.pallas_call(
        body,
        out_shape=(jax.ShapeDtypeStruct((lp, 512), F32), jax.ShapeDtypeStruct((lp, 512), F32),
                   jax.ShapeDtypeStruct((lp, P_GATE), BF16),
                   jax.ShapeDtypeStruct((1, 512), F32), jax.ShapeDtypeStruct((1, 512), F32)),
        grid=(lp // tm,),
        in_specs=[pl.BlockSpec((tm, 1024), lambda i: (i, 0)), row, row,
                  pl.BlockSpec((tm, 512), lambda i: (i, C_RG // 512)),
                  pl.BlockSpec((tm, 512), lambda i: (i, C_GR // 512)), vec, vec],
        out_specs=(row, row, pl.BlockSpec((tm, P_GATE), lambda i: (i, 0)), vec, vec),
        compiler_params=_cp("arbitrary"), name=name)(dm, o_ret, o_gla, proj, proj, w_ret, w_gla)


def _dot(a, b):
    return lax.dot_general(a, b, (((1,), (0,)), ((), ())), preferred_element_type=F32)


def _dot_nt(a, b):
    return lax.dot_general(a, b, (((1,), (1,)), ((), ())), preferred_element_type=F32)


def _dot_tn(a, b):
    return lax.dot_general(a, b, (((0,), (0,)), ((), ())), preferred_element_type=F32)


def _ret_tables(lp):
    cr = RET_CHUNK
    pos = np.arange(lp, dtype=np.float32) - np.float32(PADF)
    half = RET_DK // 2
    inv = (np.float32(ROPE_BASE) ** (-np.arange(half, dtype=np.float32) / np.float32(half))).astype(np.float32)
    ang = (pos[:, None] * inv[None, :]).astype(np.float32)
    c, s = np.cos(ang).astype(np.float32), np.sin(ang).astype(np.float32)
    rope_c = jnp.asarray(np.concatenate([c, c], axis=1))
    rope_s = jnp.asarray(np.concatenate([-s, s], axis=1))
    log_g = np.log(1.0 - 2.0 ** (-5.0 - np.arange(RET_HEADS, dtype=np.float64)))
    idx = np.arange(cr, dtype=np.float64)
    diff = idx[:, None] - idx[None, :]
    dmat = np.where(diff >= 0, np.exp(log_g[:, None, None] * np.maximum(diff, 0.0)), 0.0)
    zeta = np.exp(log_g[:, None] * (cr - 1.0 - idx)[None, :])
    xi = np.exp(log_g[:, None] * (idx + 1.0)[None, :])
    gc = np.exp(log_g * cr)
    f = lambda a: jnp.asarray(a.astype(np.float32))
    return (rope_c, rope_s, f(dmat), f(np.broadcast_to(zeta[:, :, None], (RET_HEADS, cr, 128))),
            f(np.broadcast_to(xi[:, :, None], (RET_HEADS, cr, 128))),
            f(np.broadcast_to(gc[:, None, None], (RET_HEADS, 8, 128))))


def _rope(t, c, s):
    return t * c + pltpu.roll(t, 64, 1) * s


def _rope_t(d, c, s):
    return d * c + pltpu.roll(d * s, 64, 1)


def _ret_specs(nblk, rev):
    ix = (lambda i: nblk - 1 - i) if rev else (lambda i: i)
    cr = RET_CHUNK
    col = lambda base: pl.BlockSpec((BLK, 512), lambda i: (ix(i), base // 512))
    tab = pl.BlockSpec((BLK, 128), lambda i: (ix(i), 0))
    sq = pl.BlockSpec((RET_HEADS, cr, cr), lambda i: (0, 0, 0))
    hv = pl.BlockSpec((RET_HEADS, cr, 128), lambda i: (0, 0, 0))
    g8 = pl.BlockSpec((RET_HEADS, 8, 128), lambda i: (0, 0, 0))
    st = pl.BlockSpec((RET_HEADS, BLK // cr, 128, 128), lambda i: (0, ix(i), 0, 0))
    out = pl.BlockSpec((BLK, 512), lambda i: (ix(i), 0))
    return col, tab, sq, hv, g8, st, out


def _retention(proj, tables, *, name):
    lp = proj.shape[0]
    nblk, cr = lp // BLK, RET_CHUNK
    scale = RET_DK ** -0.5

    def body(q_ref, k_ref, v_ref, c_ref, s_ref, d_ref, z_ref, x_ref, g_ref, o_ref, st_ref, state):
        @pl.when(pl.program_id(0) == 0)
        def _():
            state[...] = jnp.zeros_like(state)

        def chunk(ci, carry):
            sl = pl.ds(pl.multiple_of(ci * cr, cr), cr)
            c, s = c_ref[sl, :], s_ref[sl, :]
            for h in range(RET_HEADS):
                hs = slice(128 * h, 128 * h + 128)
                q = _rope(q_ref[sl, hs], c, s)
                k = _rope(k_ref[sl, hs], c, s) * scale
                qb, kb, vb = q.astype(BF16), k.astype(BF16), v_ref[sl, hs].astype(BF16)
                st = state[h]
                st_ref[h, ci] = st
                sc = _dot_nt(qb, kb) * d_ref[h]
                o_ref[sl, hs] = _dot(sc.astype(BF16), vb) + _dot(qb, st.astype(BF16)) * x_ref[h]
                state[h] = st * g_ref[h][0:1, :] + _dot_tn((k * z_ref[h]).astype(BF16), vb)
            return carry

        lax.fori_loop(0, BLK // cr, chunk, 0)

    col, tab, sq, hv, g8, st, out = _ret_specs(nblk, False)
    return pl.pallas_call(
        body,
        out_shape=(jax.ShapeDtypeStruct((lp, 512), F32), jax.ShapeDtypeStruct((4, lp // cr, 128, 128), F32)),
        grid=(nblk,), in_specs=[col(C_RQ), col(C_RK), col(C_RV), tab, tab, sq, hv, hv, g8],
        out_specs=(out, st), scratch_shapes=[pltpu.VMEM((RET_HEADS, 128, 128), F32)],
        compiler_params=_cp("arbitrary"), name=name)(proj, proj, proj, *tables)


def _retention_bwd(proj, do, states, tables, *, name):
    lp = proj.shape[0]
    nblk, cr = lp // BLK, RET_CHUNK
    nch = BLK // cr
    scale = RET_DK ** -0.5

    def body(q_ref, k_ref, v_ref, do_ref, st_ref, c_ref, s_ref, d_ref, z_ref, x_ref, g_ref, dqkv_ref, dstate):
        @pl.when(pl.program_id(0) == 0)
        def _():
            dstate[...] = jnp.zeros_like(dstate)

        def chunk(cc, carry):
            ci = nch - 1 - cc
            sl = pl.ds(pl.multiple_of(ci * cr, cr), cr)
            c, s = c_ref[sl, :], s_ref[sl, :]
            for h in range(RET_HEADS):
                hs = slice(128 * h, 128 * h + 128)
                dmat, zeta, xi = d_ref[h], z_ref[h], x_ref[h]
                q = _rope(q_ref[sl, hs], c, s)
                k = _rope(k_ref[sl, hs], c, s) * scale
                qb, kb, vb = q.astype(BF16), k.astype(BF16), v_ref[sl, hs].astype(BF16)
                kzb = (k * zeta).astype(BF16)
                dov = do_ref[sl, hs]
                dob, doxb = dov.astype(BF16), (dov * xi).astype(BF16)
                stb = st_ref[h, ci].astype(BF16)
                dsn = dstate[h]
                dsnb = dsn.astype(BF16)
                scb = (_dot_nt(qb, kb) * dmat).astype(BF16)
                dscb = (_dot_nt(dob, vb) * dmat).astype(BF16)
                dq = _dot(dscb, kb) + _dot_nt(doxb, stb)
                dk = _dot_tn(dscb, qb) + _dot_nt(vb, dsnb) * zeta
                dv = _dot_tn(scb, dob) + _dot(kzb, dsnb)
                dstate[h] = dsn * g_ref[h][0:1, :] + _dot_tn(qb, doxb)
                dqkv_ref[sl, 128 * h:128 * h + 128] = _rope_t(dq, c, s).astype(BF16)
                dqkv_ref[sl, 512 + 128 * h:640 + 128 * h] = _rope_t(dk * scale, c, s).astype(BF16)
                dqkv_ref[sl, 1024 + 128 * h:1152 + 128 * h] = dv.astype(BF16)
            return carry

        lax.fori_loop(0, nch, chunk, 0)

    col, tab, sq, hv, g8, st, out = _ret_specs(nblk, True)
    return pl.pallas_call(
        body, out_shape=jax.ShapeDtypeStruct((lp, P_RET), BF16), grid=(nblk,),
        in_specs=[col(C_RQ), col(C_RK), col(C_RV), out, st, tab, tab, sq, hv, hv, g8],
        out_specs=pl.BlockSpec((BLK, P_RET), lambda i: (nblk - 1 - i, 0)),
        scratch_shapes=[pltpu.VMEM((RET_HEADS, 128, 128), F32)],
        compiler_params=_cp("arbitrary"), name=name)(proj, proj, proj, do, states, *tables)


def _gla_tables():
    c = GLA_CHUNK
    tri = np.tril(np.ones((c, c), np.float32))
    ones_qv = np.kron(np.eye(GLA_HEADS, dtype=np.float32), np.ones((GLA_DK, GLA_DV), np.float32))
    return (jnp.asarray(tri, BF16), jnp.asarray(tri.T.copy(), BF16), jnp.asarray(ones_qv, BF16),
            jnp.asarray(ones_qv.T.copy(), BF16))


def _tri_sum(tri, x):
    hi = x.astype(BF16)
    lo = (x - hi.astype(F32)).astype(BF16)
    return _dot(tri, hi) + _dot(tri, lo)


def _head_masks(width, per):
    lane = lax.broadcasted_iota(jnp.int32, (1, width), 1)
    return [((lane >= per * h) & (lane < per * (h + 1))).astype(F32) for h in range(GLA_HEADS)]


def _stack_heads(x, masks):
    return jnp.concatenate([x * m for m in masks], axis=0)


def _gla_gate(ga, w2, b, ok, tri):
    z = _dot(ga.astype(BF16), w2) + b
    la = (jnp.minimum(z, 0.0) - jnp.log(1.0 + jnp.exp(-jnp.abs(z)))) * (1.0 / GLA_TAU)
    la = jnp.where(ok, la, 0.0)
    return z, _tri_sum(tri, la)


def _gla_rows(i_blk, ci, lp):
    c = GLA_CHUNK
    rows = i_blk * BLK + ci * c + lax.broadcasted_iota(jnp.int32, (c, 1), 0)
    return (rows >= PADF) & (rows < lp - BACK)


N_SUB = GLA_CHUNK // GLA_SUB - 1
N_SUB2 = GLA_SUB // GLA_SUB2 - 1


def _gla_masks():
    c, s1, s2 = GLA_CHUNK, GLA_SUB, GLA_SUB2
    sh1, sh2 = s1.bit_length() - 1, s2.bit_length() - 1
    r = lax.broadcasted_iota(jnp.int32, (c, GLA_QK), 0)
    blk, within = jnp.right_shift(r, sh1), jnp.bitwise_and(r, s1 - 1)
    grp = jnp.right_shift(within, sh2)
    rowm = [(blk == a).astype(F32) for a in range(1, N_SUB + 1)] + [(grp == b).astype(F32) for b in range(1, N_SUB2 + 1)]
    keym = ([(r < s1 * a).astype(F32) for a in range(1, N_SUB + 1)]
            + [(within < s2 * b).astype(F32) for b in range(1, N_SUB2 + 1)])
    rs = lax.broadcasted_iota(jnp.int32, (GLA_HEADS * c, c), 0)
    ts = lax.broadcasted_iota(jnp.int32, (GLA_HEADS * c, c), 1)
    same = (jnp.right_shift(jnp.bitwise_and(rs, c - 1), sh1) == jnp.right_shift(ts, sh1)).astype(F32)
    lag = [(jnp.bitwise_and(r, s2 - 1) >= j).astype(F32) for j in range(s2)]
    return rowm, keym, same, lag


def _gla_hats(qs, k, g, masks, hm_q):
    c, s1, s2 = GLA_CHUNK, GLA_SUB, GLA_SUB2
    rowm, keym, same, _ = masks
    refs = [g[s1 * a - 1:s1 * a, :] for a in range(1, N_SUB + 1)]
    for b in range(1, N_SUB2 + 1):
        refs.append(jnp.concatenate([jnp.broadcast_to(g[s1 * i + s2 * b - 1:s1 * i + s2 * b, :], (s1, GLA_QK))
                                     for i in range(c // s1)], axis=0))
    eqs = [jnp.exp(jnp.minimum(g - r, 0.0)) * m for r, m in zip(refs, rowm)]
    eks = [jnp.exp(jnp.minimum(r - g, 0.0)) * m for r, m in zip(refs, keym)]
    qhs, khs = [qs * e for e in eqs], [k * e for e in eks]
    qst = [_stack_heads(q, hm_q).astype(BF16) for q in qhs]
    khb = [x.astype(BF16) for x in khs]
    qa, qb = jnp.concatenate(qst[:N_SUB], axis=1), jnp.concatenate(qst[N_SUB:], axis=1)
    ka, kb = jnp.concatenate(khb[:N_SUB], axis=1), jnp.concatenate(khb[N_SUB:], axis=1)
    p = _dot_nt(qa, ka) + _dot_nt(qb, kb) * same
    return eqs, eks, qhs, khs, qa, qb, ka, kb, p


def _roll_rows(x, j):
    return x if j == 0 else pltpu.roll(x, j, 0)


def _gla(proj, w2p, b, tables, *, name):
    lp = proj.shape[0]
    nblk, c, s2 = lp // BLK, GLA_CHUNK, GLA_SUB2
    nch = BLK // c

    def body(q_ref, k_ref, v_ref, a_ref, w_ref, b_ref, tri_ref, ones_ref, o_ref, st_ref, gz_ref, state):
        i_blk = pl.program_id(0)

        @pl.when(i_blk == 0)
        def _():
            state[...] = jnp.zeros_like(state)
        hm_q = _head_masks(GLA_QK, GLA_DK)
        masks = _gla_masks()
        tri, ones_qv, w2, bias = tri_ref[...], ones_ref[...], w_ref[...], b_ref[...]

        def chunk(ci, carry):
            sl = pl.ds(pl.multiple_of(ci * c, c), c)
            ok = _gla_rows(i_blk, ci, lp)
            k, v = k_ref[sl, :], v_ref[sl, :]
            vb = v.astype(BF16)
            qs = q_ref[sl, :] * (GLA_DK ** -0.5)
            z, g = _gla_gate(a_ref[sl, :], w2, bias, ok, tri)
            gz_ref[sl, 0:GLA_QK] = g
            gz_ref[sl, GLA_QK:2 * GLA_QK] = z
            last = g[c - 1:c, :]
            st = state[...]
            st_ref[ci] = st
            qst = _stack_heads(qs * jnp.exp(g), hm_q).astype(BF16)
            oi = _dot_nt(qst, st.astype(BF16))
            o = jnp.concatenate([oi[c * h:c * h + c, :] for h in range(GLA_HEADS)], axis=1)
            ke = k * jnp.exp(last - g)
            f = _dot_tn(vb, ke.astype(BF16))
            upd = f[0:GLA_DV, :] * hm_q[0]
            for h in range(1, GLA_HEADS):
                upd = upd + f[GLA_DV * h:GLA_DV * (h + 1), :] * hm_q[h]
            state[...] = st * jnp.exp(last) + upd
            p = _gla_hats(qs, k, g, masks, hm_q)[-1]
            ob = _dot(p.astype(BF16), vb)
            o = o + jnp.concatenate([ob[c * h:c * h + c, GLA_DV * h:GLA_DV * (h + 1)] for h in range(GLA_HEADS)],
                                    axis=1)
            ws = []
            for j in range(s2):
                ej = jnp.exp(jnp.minimum(g - _roll_rows(g, j), 0.0))
                ws.append((qs * _roll_rows(k, j) * ej * masks[3][j]).astype(BF16))
            ball = _dot(jnp.concatenate(ws, axis=0), ones_qv)
            for j in range(s2):
                o = o + ball[c * j:c * j + c, :] * _roll_rows(v, j)
            o_ref[sl, :] = o
            return carry

        lax.fori_loop(0, nch, chunk, 0)

    tri, _, ones_qv, _ = tables
    full = lambda arr: pl.BlockSpec(arr.shape, lambda i: (0,) * arr.ndim)
    return pl.pallas_call(
        body,
        out_shape=(jax.ShapeDtypeStruct((lp, GLA_V), F32), jax.ShapeDtypeStruct((lp // c, GLA_DV, GLA_QK), F32),
                   jax.ShapeDtypeStruct((lp, 2 * GLA_QK), F32)),
        grid=(nblk,),
        in_specs=[pl.BlockSpec((BLK, GLA_QK), lambda i: (i, C_GQ // GLA_QK)),
                  pl.BlockSpec((BLK, GLA_QK), lambda i: (i, C_GK // GLA_QK)),
                  pl.BlockSpec((BLK, GLA_V), lambda i: (i, C_GV // GLA_V)),
                  pl.BlockSpec((BLK, 128), lambda i: (i, C_GA // 128)),
                  full(w2p), full(b), full(tri), full(ones_qv)],
        out_specs=(pl.BlockSpec((BLK, GLA_V), lambda i: (i, 0)),
                   pl.BlockSpec((nch, GLA_DV, GLA_QK), lambda i: (i, 0, 0)),
                   pl.BlockSpec((BLK, 2 * GLA_QK), lambda i: (i, 0))),
        scratch_shapes=[pltpu.VMEM((GLA_DV, GLA_QK), F32)],
        compiler_params=_cp("arbitrary"), name=name)(proj, proj, proj, proj, w2p, b, tri, ones_qv)


def _gla_bwd(proj, do, states, gz, w2p, tables, *, name):
    lp = proj.shape[0]
    nblk, c, s1, s2 = lp // BLK, GLA_CHUNK, GLA_SUB, GLA_SUB2
    nch = BLK // c

    def body(q_ref, k_ref, v_ref, a_ref, do_ref, st_ref, gz_ref, w_ref, trit_ref, ones_ref, onest_ref,
             dp_ref, dw_ref, db_ref, dstate, dqs_s, dk_s, dg_s, dv_s):
        i_blk = nblk - 1 - pl.program_id(0)

        @pl.when(pl.program_id(0) == 0)
        def _():
            dstate[...] = jnp.zeros_like(dstate)
            dw_ref[...] = jnp.zeros_like(dw_ref)
            db_ref[...] = jnp.zeros_like(db_ref)
        hm_q = _head_masks(GLA_QK, GLA_DK)
        hm_v = _head_masks(GLA_V, GLA_DV)
        masks = _gla_masks()
        trit, ones_qv, ones_vq = trit_ref[...], ones_ref[...], onest_ref[...]
        w2 = w_ref[...]
        rsum = lambda x: jnp.sum(x, axis=0, keepdims=True)

        def chunk(cc, carry):
            ci = nch - 1 - cc
            sl = pl.ds(pl.multiple_of(ci * c, c), c)
            ok = _gla_rows(i_blk, ci, lp)
            k, v, ga = k_ref[sl, :], v_ref[sl, :], a_ref[sl, :]
            vb = v.astype(BF16)
            qs = q_ref[sl, :] * (GLA_DK ** -0.5)
            g, z = gz_ref[sl, 0:GLA_QK], gz_ref[sl, GLA_QK:2 * GLA_QK]
            last = g[c - 1:c, :]
            elast = jnp.exp(last)
            eg = jnp.exp(g)
            ekl = jnp.exp(last - g)
            qe, ke = qs * eg, k * ekl
            dov = do_ref[sl, :]
            st = st_ref[ci]
            dsn = dstate[...]
            qst = _stack_heads(qe, hm_q).astype(BF16)
            dost = jnp.concatenate([dov[:, GLA_DV * h:GLA_DV * (h + 1)] for h in range(GLA_HEADS)], axis=0).astype(BF16)
            dqe_st = _dot(dost, st.astype(BF16))
            dqe = dqe_st[0:c, :] * hm_q[0]
            for h in range(1, GLA_HEADS):
                dqe = dqe + dqe_st[c * h:c * h + c, :] * hm_q[h]
            dstate[...] = _dot_tn(dost, qst) + dsn * elast
            dlast = rsum(dsn * st) * elast
            df = _stack_heads(dsn, hm_q).astype(BF16)
            dv_s[...] = _dot_nt(ke.astype(BF16), df)
            dke = _dot(vb, df)
            xk = dke * ke
            dqs_s[...] = dqe * eg
            dk_s[...] = dke * ekl
            dg_s[...] = dqe * qe - xk
            dlast = dlast + rsum(xk)
            eqs, eks, qhs, khs, qa, qb, ka, kb, p = _gla_hats(qs, k, g, masks, hm_q)
            dost_v = _stack_heads(dov, hm_v).astype(BF16)
            dp = _dot_nt(dost_v, vb)
            dv_s[...] += _dot_tn(p.astype(BF16), dost_v)
            dpa, dpb = dp.astype(BF16), (dp * masks[2]).astype(BF16)
            dq_all = (_dot(dpa, ka), _dot(dpb, kb))
            dk_all = (_dot_tn(dpa, qa), _dot_tn(dpb, qb))
            for t in range(N_SUB + N_SUB2):
                lvl, i = (0, t) if t < N_SUB else (1, t - N_SUB)
                cols = slice(GLA_QK * i, GLA_QK * (i + 1))
                dq_st = dq_all[lvl][:, cols]
                dqh = dq_st[0:c, :] * hm_q[0]
                for h in range(1, GLA_HEADS):
                    dqh = dqh + dq_st[c * h:c * h + c, :] * hm_q[h]
                dkh = dk_all[lvl][:, cols]
                xq, xkh = dqh * qhs[t], dkh * khs[t]
                dqs_s[...] += dqh * eqs[t]
                dk_s[...] += dkh * eks[t]
                dg_s[...] += xq - xkh
                back_ref = xkh - xq
                if lvl == 0:
                    row = s1 * (i + 1) - 1
                    dg_s[row:row + 1, :] += rsum(back_ref)
                else:
                    for blk in range(c // s1):
                        row = s1 * blk + s2 * (i + 1) - 1
                        dg_s[row:row + 1, :] += rsum(back_ref[s1 * blk:s1 * blk + s1, :])
            kes, qes, ws, dbs = [], [], [], []
            for j in range(s2):
                em = jnp.exp(jnp.minimum(g - _roll_rows(g, j), 0.0)) * masks[3][j]
                kes.append(_roll_rows(k, j) * em)
                qes.append(qs * em)
                ws.append((qs * kes[j]).astype(BF16))
                dbs.append((dov * _roll_rows(v, j)).astype(BF16))
            ball = _dot(jnp.concatenate(ws, axis=0), ones_qv)
            dwall = _dot(jnp.concatenate(dbs, axis=0), ones_vq)
            for j in range(s2):
                back = (lambda x: x) if j == 0 else (lambda x, j=j: pltpu.roll(x, c - j, 0))
                dw = dwall[c * j:c * j + c, :]
                dv_s[...] += back(ball[c * j:c * j + c, :] * dov)
                dqs_s[...] += dw * kes[j]
                dk_s[...] += back(dw * qes[j])
                x = dw * qs * kes[j]
                dg_s[...] += x - back(x)
            dg_s[c - 1:c, :] += dlast
            dla = jnp.where(ok, _tri_sum(trit, dg_s[...]), 0.0)
            dz = dla * (1.0 / GLA_TAU) / (1.0 + jnp.exp(z))
            dzb = dz.astype(BF16)
            dp_ref[sl, 0:256] = (dqs_s[...] * (GLA_DK ** -0.5)).astype(BF16)
            dp_ref[sl, 256:512] = dk_s[...].astype(BF16)
            dp_ref[sl, 512:1024] = dv_s[...].astype(BF16)
            dp_ref[sl, 1024:1152] = _dot_nt(dzb, w2).astype(BF16)
            dp_ref[sl, 1152:1280] = jnp.zeros((c, 128), BF16)
            dw_ref[...] += _dot_tn(ga.astype(BF16), dzb)
            db_ref[...] += rsum(dz)
            return carry

        lax.fori_loop(0, nch, chunk, 0)

    tri, trit, ones_qv, ones_vq = tables
    full = lambda arr: pl.BlockSpec(arr.shape, lambda i: (0,) * arr.ndim)
    rev = lambda i: nblk - 1 - i
    return pl.pallas_call(
        body,
        out_shape=(jax.ShapeDtypeStruct((lp, P_GLA), BF16),
                   jax.ShapeDtypeStruct((128, GLA_QK), F32), jax.ShapeDtypeStruct((1, GLA_QK), F32)),
        grid=(nblk,),
        in_specs=[pl.BlockSpec((BLK, GLA_QK), lambda i: (rev(i), C_GQ // GLA_QK)),
                  pl.BlockSpec((BLK, GLA_QK), lambda i: (rev(i), C_GK // GLA_QK)),
                  pl.BlockSpec((BLK, GLA_V), lambda i: (rev(i), C_GV // GLA_V)),
                  pl.BlockSpec((BLK, 128), lambda i: (rev(i), C_GA // 128)),
                  pl.BlockSpec((BLK, GLA_V), lambda i: (rev(i), 0)),
                  pl.BlockSpec((nch, GLA_DV, GLA_QK), lambda i: (rev(i), 0, 0)),
                  pl.BlockSpec((BLK, 2 * GLA_QK), lambda i: (rev(i), 0)),
                  full(w2p), full(trit), full(ones_qv), full(ones_vq)],
        out_specs=(pl.BlockSpec((BLK, P_GLA), lambda i: (rev(i), 0)),
                   pl.BlockSpec((128, GLA_QK), lambda i: (0, 0)),
                   pl.BlockSpec((1, GLA_QK), lambda i: (0, 0))),
        scratch_shapes=[pltpu.VMEM((GLA_DV, GLA_QK), F32), pltpu.VMEM((c, GLA_QK), F32),
                        pltpu.VMEM((c, GLA_QK), F32), pltpu.VMEM((c, GLA_QK), F32), pltpu.VMEM((c, GLA_V), F32)],
        compiler_params=_cp("arbitrary"), name=name)(proj, proj, proj, proj, do, states, gz, w2p, trit, ones_qv, ones_vq)


def _as2d(a):
    return a.reshape(-1, a.shape[-1])


def _ew_tile(r):
    return _tile(r, (512, 256, 128, 64, 32, 16, 8))


def _add2(a, b, *, out_dtype, name):
    a2, b2 = _as2d(a), _as2d(b)
    r, n = a2.shape
    tm = _ew_tile(r)

    def body(a_ref, b_ref, o_ref):
        o_ref[...] = (a_ref[...] + b_ref[...]).astype(o_ref.dtype)

    blk = pl.BlockSpec((tm, n), lambda i: (i, 0))
    return pl.pallas_call(body, out_shape=jax.ShapeDtypeStruct((r, n), out_dtype), grid=(r // tm,), in_specs=[blk, blk],
                          out_specs=blk, compiler_params=_cp("parallel"), name=name)(a2, b2).reshape(a.shape)


def _sum_slots(own, q, *, name):
    shape = own.shape
    q3 = q.reshape(3, -1, shape[-1])
    own2 = _as2d(own)
    r, n = own2.shape
    tm = _ew_tile(r)

    def body(own_ref, q_ref, o_ref):
        f = lambda i: q_ref[i].astype(F32)
        o_ref[...] = ((own_ref[...].astype(F32) + f(0)) + f(1)) + f(2)

    blk = pl.BlockSpec((tm, n), lambda i: (i, 0))
    return pl.pallas_call(
        body, out_shape=jax.ShapeDtypeStruct((r, n), F32), grid=(r // tm,),
        in_specs=[blk, pl.BlockSpec((3, tm, n), lambda i: (0, i, 0))], out_specs=blk,
        compiler_params=_cp("parallel"), name=name)(own2, q3).reshape(shape)


def _adamw(w, g, m, v, *, name):
    shape = w.shape
    w2, g2, m2, v2 = _as2d(w), _as2d(g), _as2d(m), _as2d(v)
    r, n = w2.shape
    tm = _ew_tile(r)

    def body(w_ref, g_ref, m_ref, v_ref, d_ref, mo_ref, vo_ref):
        d_ref[...], mo_ref[...], vo_ref[...] = _adam_math(w_ref[...], g_ref[...], m_ref[...], v_ref[...])

    blk = pl.BlockSpec((tm, n), lambda i: (i, 0))
    o = jax.ShapeDtypeStruct((r, n), F32)
    d, mo, vo = pl.pallas_call(body, out_shape=(o, o, o), grid=(r // tm,), in_specs=[blk] * 4, out_specs=(blk,) * 3,
                               compiler_params=_cp("parallel"), name=name)(w2, g2, m2, v2)
    return d.reshape(shape), mo.reshape(shape), vo.reshape(shape)


def _adam_math(w, gv, m, v):
    c1 = 1.0 - ADAM_B1 ** ADAM_STEP
    c2 = 1.0 - ADAM_B2 ** ADAM_STEP
    mn = ADAM_B1 * m + (1.0 - ADAM_B1) * gv
    vn = ADAM_B2 * v + (1.0 - ADAM_B2) * (gv * gv)
    return -ADAM_LR * ((mn / c1) / (jnp.sqrt(vn / c2) + ADAM_EPS) + ADAM_WD * w), mn, vn


def _adamw_halves(w, m, v, mine, theirs, c, *, name):
    depth, rows, n = w.shape
    r2 = rows // 2
    tm = next(t for t in range(min(r2, 256), 0, -8) if r2 % t == 0)
    steps = r2 // tm

    def body(c_ref, w_ref, m_ref, v_ref, *rest):
        halves, (g_ref, d_ref, mo_ref, vo_ref) = rest[:2 * depth], rest[2 * depth:]
        l, h = pl.program_id(0), pl.program_id(1)
        gv = None
        for k in range(depth):
            gk = jnp.where(h == c_ref[0], halves[2 * k][...], halves[2 * k + 1][...])
            gv = gk if gv is None else jnp.where(l == k, gk, gv)
        g_ref[...] = gv
        d_ref[...], mo_ref[...], vo_ref[...] = _adam_math(w_ref[...], gv, m_ref[...], v_ref[...])

    big = pl.BlockSpec((tm, n), lambda l, h, i, c_ref: ((2 * l + h) * steps + i, 0))
    half = lambda k: pl.BlockSpec((tm, n), lambda l, h, i, c_ref: (jnp.where(l == k, i, 0), 0))
    o = jax.ShapeDtypeStruct((depth * rows, n), F32)
    args = [a for k in range(depth) for a in (mine[k], theirs[k])]
    outs = pl.pallas_call(
        body, out_shape=(o, o, o, o),
        grid_spec=pltpu.PrefetchScalarGridSpec(
            num_scalar_prefetch=1, grid=(depth, 2, steps),
            in_specs=[big, big, big] + [half(k) for k in range(depth) for _ in range(2)], out_specs=(big,) * 4),
        compiler_params=_cp("arbitrary", "arbitrary", "arbitrary"), name=name)(
            jnp.reshape(c, (1,)).astype(jnp.int32), _as2d(w), _as2d(m), _as2d(v), *args)
    return [a.reshape(w.shape) for a in outs]


ANY = pl.BlockSpec(memory_space=pl.ANY)


def _place():
    return lax.axis_index("x"), lax.axis_index("y"), lax.axis_index("c")


def _other_chips(x, y):
    return [(1 - x, y), (x, 1 - y), (1 - x, 1 - y)]


def _remote(src, dst, ssem, rsem, dev):
    return pltpu.make_async_remote_copy(src_ref=src, dst_ref=dst, send_sem=ssem, recv_sem=rsem, device_id=dev,
                                        device_id_type=MESH)


def _allgather_chips(arrs, *, name):
    n = len(arrs)

    def body(*refs):
        ins, outs = refs[:n], refs[n:2 * n]
        s1, r1, s2, r2 = refs[2 * n:]
        x, y, c = _place()
        q = 2 * x + y
        chips = _other_chips(x, y)
        qs = [2 * cx + cy for cx, cy in chips]
        sib = (x, y, 1 - c)
        first, passed = [], []
        for k in range(n):
            for j, chip in enumerate(chips):
                first.append(_remote(ins[k].at[c], outs[k].at[c, q], s1.at[k, j], r1.at[k, j], (*chip, c)))
        for cp in first:
            cp.start()
        for k in range(n):
            for j, chip in enumerate(chips):
                land = outs[k].at[c, qs[j]]
                _remote(land, land, s1.at[k, j], r1.at[k, j], (*chip, c)).wait_recv()
                fw = _remote(land, land, s2.at[k, j], r2.at[k, j], sib)
                fw.start()
                passed.append(fw)
        for k in range(n):
            for j in range(3):
                land = outs[k].at[1 - c, qs[j]]
                _remote(land, land, s2.at[k, j], r2.at[k, j], sib).wait_recv()
        for cp in first + passed:
            cp.wait_send()

    sem = pltpu.SemaphoreType.DMA
    outs = pl.pallas_call(
        body, out_shape=tuple(jax.ShapeDtypeStruct((2, 4) + a.shape[1:], a.dtype) for a in arrs),
        in_specs=[ANY] * n, out_specs=(ANY,) * n,
        scratch_shapes=[sem((n, 3)), sem((n, 3)), sem((n, 3)), sem((n, 3))], name=name)(*arrs)
    chip = 2 * lax.axis_index("x") + lax.axis_index("y")
    return [lax.dynamic_update_slice_in_dim(o, a[:, None], chip, axis=1) for o, a in zip(outs, arrs)]


def _pair_exchange(arrs, *, name):
    n = len(arrs)

    def body(*refs):
        ins, outs = refs[:n], refs[n:2 * n]
        ssem, rsem = refs[2 * n:]
        x, y, c = _place()
        cps = [_remote(ins[k].at[:, 1 - c], outs[k], ssem.at[k], rsem.at[k], (x, y, 1 - c)) for k in range(n)]
        for cp in cps:
            cp.start()
        for cp in cps:
            cp.wait()

    sem = pltpu.SemaphoreType.DMA
    return pl.pallas_call(
        body, out_shape=tuple(jax.ShapeDtypeStruct((a.shape[0],) + a.shape[2:], a.dtype) for a in arrs),
        in_specs=[ANY] * n, out_specs=(ANY,) * n, scratch_shapes=[sem((n,)), sem((n,))], name=name)(*arrs)


def _pair_sum(mine, theirs, c, *, name):
    _, _, r, n = mine.shape
    tm = r if r <= 512 else _ew_tile(r)

    def body(c_ref, a_ref, b_ref, o_ref):
        o_ref[...] = (a_ref[...] + b_ref[...]).astype(BF16)

    blk = pl.BlockSpec((None, tm, n), lambda s, i, c_ref: (s, i, 0))
    return pl.pallas_call(
        body, out_shape=jax.ShapeDtypeStruct((4, r, n), BF16),
        grid_spec=pltpu.PrefetchScalarGridSpec(
            num_scalar_prefetch=1, grid=(4, r // tm),
            in_specs=[pl.BlockSpec((None, None, tm, n), lambda s, i, c_ref: (s, c_ref[0], i, 0)), blk], out_specs=blk),
        compiler_params=_cp("parallel", "parallel"), name=name)(jnp.reshape(c, (1,)).astype(jnp.int32), mine, theirs)


def _chip_copies(ins, outs, ssem, rsem, mode):
    x, y, c = _place()
    q = 2 * x + y
    sends, recvs = [], []
    for k in range(len(ins)):
        for j, (cx, cy) in enumerate(_other_chips(x, y)):
            sem = (ssem.at[k, j], rsem.at[k, j], (cx, cy, c))
            if mode == "scatter":
                sends.append(_remote(ins[k].at[2 * cx + cy], outs[k].at[j], *sem))
                recvs.append(sends[-1])
            else:
                sends.append(_remote(ins[k].at[c], outs[k].at[2 * q + c], *sem))
                recvs.append(_remote(ins[k].at[c], outs[k].at[2 * (2 * cx + cy) + c], *sem))
    return sends, recvs


def _chip_wait(sends, recvs):
    for cp in sends:
        cp.wait_send()
    for cp in recvs:
        cp.wait_recv()


def _landing_shape(a, mode):
    return jax.ShapeDtypeStruct(((3,) if mode == "scatter" else (8,)) + a.shape[1:], a.dtype)


def _chip_exchange(arrs, mode, *, name):
    n = len(arrs)

    def body(*refs):
        ins, outs = refs[:n], refs[n:2 * n]
        ssem, rsem = refs[2 * n:]
        sends, recvs = _chip_copies(ins, outs, ssem, rsem, mode)
        for cp in sends:
            cp.start()
        _chip_wait(sends, recvs)

    sem = pltpu.SemaphoreType.DMA
    return list(pl.pallas_call(
        body, out_shape=tuple(_landing_shape(a, mode) for a in arrs),
        in_specs=[ANY] * n, out_specs=(ANY,) * n, scratch_shapes=[sem((n, 3)), sem((n, 3))], name=name)(*arrs))


def _pair_fill(bufs, owns, *, name):
    n = len(bufs)

    def body(*refs):
        own, outs = refs[n:2 * n], refs[2 * n:3 * n]
        ssem, rsem = refs[3 * n:]
        x, y, c = _place()
        q = 2 * x + y
        sib = (x, y, 1 - c)
        sends, recvs = [], []
        for k in range(n):
            for j, (cx, cy) in enumerate(_other_chips(x, y)):
                mine, theirs = outs[k].at[2 * (2 * cx + cy) + c], outs[k].at[2 * (2 * cx + cy) + 1 - c]
                sends.append(_remote(mine, mine, ssem.at[k, j], rsem.at[k, j], sib))
                recvs.append(_remote(mine, theirs, ssem.at[k, j], rsem.at[k, j], sib))
            slots = outs[k].at[pl.ds(2 * q, 2)]
            sends.append(_remote(own[k], slots, ssem.at[k, 3], rsem.at[k, 3], sib))
            recvs.append(sends[-1])
        for cp in sends:
            cp.start()
        _chip_wait(sends, recvs)

    sem = pltpu.SemaphoreType.DMA
    return list(pl.pallas_call(
        body, out_shape=tuple(jax.ShapeDtypeStruct(b.shape, b.dtype) for b in bufs),
        in_specs=[ANY] * (2 * n), out_specs=(ANY,) * n, scratch_shapes=[sem((n, 4)), sem((n, 4))],
        input_output_aliases={k: k for k in range(n)}, name=name)(*bufs, *owns))


def _pair_swap(arrs, *, name):
    n = len(arrs)

    def body(*refs):
        ins, outs = refs[:n], refs[n:2 * n]
        ssem, rsem = refs[2 * n:]
        x, y, c = _place()
        cps = [_remote(ins[k], outs[k], ssem.at[k], rsem.at[k], (x, y, 1 - c)) for k in range(n)]
        for cp in cps:
            cp.start()
        for cp in cps:
            cp.wait()

    sem = pltpu.SemaphoreType.DMA
    return pl.pallas_call(
        body, out_shape=tuple(jax.ShapeDtypeStruct(a.shape, a.dtype) for a in arrs),
        in_specs=[ANY] * n, out_specs=(ANY,) * n, scratch_shapes=[sem((n,)), sem((n,))], name=name)(*arrs)


def _allreduce_small(slab, *, name):
    r, n = slab.shape

    def body(x_ref, o_ref, buf, ssem, rsem):
        x, y, c = _place()
        me = 4 * x + 2 * y + c
        buf[me] = x_ref[...]
        cps = []
        for rel in range(1, 8):
            bx, by, bc = (rel >> 2) & 1, (rel >> 1) & 1, rel & 1
            px, py, pc = (x + bx) % 2, (y + by) % 2, (c + bc) % 2
            cps.append((_remote(x_ref, buf.at[me], ssem.at[rel - 1], rsem.at[rel - 1], (px, py, pc)),
                        4 * px + 2 * py + pc, (px, py, pc)))
        for cp, _, _ in cps:
            cp.start()
        for rel, (cp, peer, dev) in enumerate(cps):
            cp.wait_send()
            _remote(x_ref, buf.at[peer], ssem.at[rel], rsem.at[rel], dev).wait_recv()
        acc = buf[0]
        for k in range(1, 8):
            acc = acc + buf[k]
        o_ref[...] = acc

    vm = pl.BlockSpec(memory_space=pltpu.VMEM)
    sem = pltpu.SemaphoreType.DMA
    return pl.pallas_call(
        body, out_shape=jax.ShapeDtypeStruct((r, n), F32), in_specs=[vm], out_specs=vm,
        scratch_shapes=[pltpu.VMEM((8, r, n), F32), sem((7,)), sem((7,))], name=name)(slab)


def _slab(arrs, row_mult):
    flat = jnp.concatenate([a.reshape(-1) for a in arrs])
    unit = 128 * row_mult
    total = -(-flat.size // unit) * unit
    return jnp.pad(flat, (0, total - flat.size)).reshape(-1, 128)


def _unslab(slab, shapes):
    flat = slab.reshape(-1)
    out, off = [], 0
    for s in shapes:
        size = int(np.prod(s))
        out.append(flat[off:off + size].reshape(s))
        off += size
    return out


def _cols_from_chips(a):
    return jnp.transpose(a, (1, 0, 2)).reshape(a.shape[1], -1)


def _cols_to_chips(a, parts):
    r = a.shape[0]
    return jnp.transpose(a.reshape(r, parts, -1), (1, 0, 2))


BIG = ("w_in", "w_out", "up", "down")
GATHER_RIDES = {("proj", 0): (("w_out", 0), ("up", 0)), ("mix_out", 0): (("down", 0),),
                ("ffn_fwd", 0): (("w_in", 1), ("w_out", 1), ("up", 1), ("down", 1))}
REDUCE_RIDES = {("ffn_up_a_dw", 0): (("up",), 1), ("ffn_down_dw", 0): (("w_in", "w_out"), 1),
                ("mix_out_dx", 0): (("down",), 1),
                ("proj_dx", 0): (("up",), 0), ("proj_dw_0", 0): (("down",), 0), ("proj_dw_1", 0): (("w_out",), 0)}


class _LocalWeights:
    def __init__(self, meta, win, wout, up_a, up_g, down, w2p, cw):
        self._meta, self._w = meta, {"win": win, "wout": wout, "up_a": up_a, "up_g": up_g, "down": down, "w2p": w2p,
                                     "cw": cw}

    def meta(self):
        return self._meta

    def get(self, kind, l):
        return self._w[kind][l]

    def mm(self, site, l, a, b, fn=None, **kw):
        return (fn or _mm)(a, b, name=site, **kw)

    def ffn_fwd(self, l, h, m, w_post, w_next, wa, wg, ba, bg):
        return _ffn_fwd(h, m, w_post, w_next, self.get("up_a", l), self.get("up_g", l), wa, wg, ba, bg,
                        self.get("down", l), name="ffn_fwd")[0]

    def merge_mix(self, l, o_ret, o_gla, proj, w_ret, w_gla):
        return _merge_mix_out(o_ret, o_gla, proj, w_ret, w_gla, self.get("wout", l), name="mix_out")[0]

    def grads_done(self, l, g, kinds):
        pass


class _ChipWeights:
    def __init__(self, w_in, w_out, ffn_up, ffn_down, meta_tokens, gla_gate_w2, ffn_conv_w):
        self.x, self.y, self.c = _place()
        self.q = 2 * self.x + self.y
        halves = lambda a: a.astype(BF16).reshape(2, a.shape[0] // 2, a.shape[1])
        self.own = {(k, l): halves(a[l]) for k, a in zip(BIG, (w_in, w_out, ffn_up, ffn_down)) for l in range(DEPTH)}
        self.landed, self.swapped, self.full, self.n_swaps = {}, {}, {}, 0
        self.sh_shapes = [meta_tokens.shape, gla_gate_w2.shape, ffn_conv_w.shape]
        self.own["small", 0] = _slab([meta_tokens, gla_gate_w2, ffn_conv_w], 16).reshape(2, -1, 128)
        first = [("w_in", 0), ("small", 0)]
        for key, arr in zip(first, _chip_exchange([self.own[k] for k in first], "bcast", name="gather_first")):
            self.landed[key] = arr
        sh = self._whole("small", 0).reshape(4, -1, 128)
        parts = [_unslab(sh[k], self.sh_shapes) for k in range(4)]
        self._meta = jnp.concatenate([p[0] for p in parts], axis=-1)
        self.w2 = jnp.concatenate([p[1] for p in parts], axis=-1)
        self.cw = jnp.concatenate([p[2] for p in parts], axis=-1)
        self.partial, self.slots = {}, {}

    def _whole(self, kind, l):
        if (kind, l) not in self.full:
            keys = [k for k in self.landed if k not in self.full]
            got = _pair_fill([self.landed[k] for k in keys], [self.own[k] for k in keys],
                             name=f"gather_fill_{self.n_swaps}")
            self.n_swaps += 1
            for k, buf in zip(keys, got):
                self.full[k] = buf.reshape(4, 2 * buf.shape[1], buf.shape[2])
        return self.full[kind, l]

    def meta(self):
        return self._meta

    def get(self, kind, l):
        if kind == "win":
            return _to_kernel_cols(_cols_from_chips(self._whole("w_in", l)))
        if kind == "wout":
            return self._whole("w_out", l).reshape(D_MODEL, D_MODEL)
        if kind == "up_a":
            return _cols_from_chips(self._whole("up", l)[0:2])
        if kind == "up_g":
            return _cols_from_chips(self._whole("up", l)[2:4])
        if kind == "down":
            return self._whole("down", l).reshape(D_FF, D_MODEL)
        if kind == "w2p":
            return jnp.pad(self.w2[l], ((0, 128 - GLA_RANK), (0, 0))).astype(BF16)
        return self.cw[l]

    def mm(self, site, l, a, b, fn=None, **kw):
        fn = fn or _mm
        if (site, l) in GATHER_RIDES:
            keys = GATHER_RIDES[site, l]
            out, got = fn(a, b, name=site, carry=([self.own[k] for k in keys], "bcast"), **kw)
            self.landed.update(zip(keys, got))
            return out
        if (site, l) in REDUCE_RIDES:
            kinds, gl = REDUCE_RIDES[site, l]
            keys = [(k, gl) for k in kinds]
            if all(k in self.partial and k not in self.slots for k in keys):
                out, got = fn(a, b, name=site, carry=([self.partial[k] for k in keys], "scatter"), **kw)
                self.slots.update(zip(keys, got))
                return out
        return fn(a, b, name=site, **kw)

    def merge_mix(self, l, o_ret, o_gla, proj, w_ret, w_gla):
        keys = GATHER_RIDES.get(("mix_out", l), ())
        outs, got = _merge_mix_out(o_ret, o_gla, proj, w_ret, w_gla, self.get("wout", l), name="mix_out",
                                   carry=([self.own[k] for k in keys], "bcast") if keys else None)
        self.landed.update(zip(keys, got))
        return outs

    def ffn_fwd(self, l, h, m, w_post, w_next, wa, wg, ba, bg):
        keys = GATHER_RIDES.get(("ffn_fwd", l), ())
        outs, got = _ffn_fwd(h, m, w_post, w_next, self.get("up_a", l), self.get("up_g", l), wa, wg, ba, bg,
                             self.get("down", l), name="ffn_fwd",
                             carry=([self.own[k] for k in keys], "bcast") if keys else None)
        self.landed.update(zip(keys, got))
        return outs

    def grads_done(self, l, g, kinds):
        split = lambda a: a.reshape(4, 2, a.shape[-2] // 2, a.shape[-1]) if a.ndim == 3 else \
            a.reshape(4, 2, a.shape[0] // 8, a.shape[1])
        src = {"w_in": lambda: g["w_in"][l], "w_out": lambda: g["w_out"][l],
               "up": lambda: g["up"][l], "down": lambda: g["down"][l]}
        big = {k: split(src[k]()) for k in kinds}
        from_sib = _pair_exchange([big[k] for k in kinds], name=f"grads_pair_exchange_{l}_{kinds[0]}")
        for k, theirs in zip(kinds, from_sib):
            self.partial[k, l] = _pair_sum(big[k], theirs, self.c, name=f"pair_sum_{k}_{l}")

    def reduce(self):
        keys = [(k, l) for l in range(DEPTH) for k in BIG]
        late = [k for k in keys if k not in self.slots]
        self.slots.update(zip(late, _chip_exchange([self.partial[k] for k in late], "scatter",
                                                   name="grads_chip_exchange")))
        half = {}
        for k in keys:
            own = lax.dynamic_index_in_dim(self.partial[k], self.q, 0, keepdims=False)
            half[k] = _sum_slots(own, self.slots[k], name=f"chip_sum_{k[0]}_{k[1]}")
        other = dict(zip(keys, _pair_swap([half[k] for k in keys], name="grads_pair_swap")))
        return [([half[k, l] for l in range(DEPTH)], [other[k, l] for l in range(DEPTH)]) for k in BIG]


def _local_step(x_rows, target_rows, wts, pre_mix_norm, gla_gate_b, ret_norm_w, gla_norm_w, post_mix_norm,
                pre_ffn_norm, ffn_conv_b, post_ffn_norm):
    d = D_MODEL
    lp = x_rows.shape[0] + FRONT + BACK
    row = lambda a, l: a[l][None, :]
    rtab = _ret_tables(lp)
    gtab = _gla_tables()
    h0 = jnp.concatenate([jnp.zeros((PADF, d), F32), wts.meta(), x_rows, jnp.zeros((BACK, d), F32)], axis=0)
    target = jnp.pad(target_rows, ((FRONT, BACK), (0, 0)))

    saved = []
    h = h0
    _, hn = _resid_norm(h0, None, None, row(pre_mix_norm, 0), name="norm_in")
    loss_local = dy = None
    for l in range(DEPTH):
        s = {"h_in": h, "hn": hn}
        s["proj"] = wts.mm("proj", l, hn, wts.get("win", l))
        s["o_ret"], s["st_ret"] = _retention(s["proj"], rtab, name="retention")
        s["o_gla"], s["st_gla"], s["gz"] = _gla(s["proj"], wts.get("w2p", l), row(gla_gate_b, l), gtab, name="gla")
        s["merged"], s["m"] = wts.merge_mix(l, s["o_ret"], s["o_gla"], s["proj"], row(ret_norm_w, l),
                                            row(gla_norm_w, l))
        cw_a, cw_g = wts.get("cw", l)[:, :D_FF], wts.get("cw", l)[:, D_FF:]
        cb_a, cb_g = ffn_conv_b[l][None, :D_FF], ffn_conv_b[l][None, D_FF:]
        s["conv"] = (cw_a, cw_g, cb_a, cb_g)
        s["h_mid"], s["hn2"], s["ua"], s["ug"], s["act"], s["f"] = wts.ffn_fwd(
            l, h, s["m"], row(post_mix_norm, l), row(pre_ffn_norm, l), cw_a, cw_g, cb_a, cb_g)
        if l + 1 < DEPTH:
            h, hn = _resid_norm(s["h_mid"], s["f"], row(post_ffn_norm, l), row(pre_mix_norm, l + 1), name="resid_ffn")
        else:
            loss_local, dy, df_last, dw_last = _loss_head(s["h_mid"], s["f"], row(post_ffn_norm, l), target,
                                                          name="loss_head")
        saved.append(s)

    g = {k: [None] * DEPTH for k in ("pre_mix", "w_in", "w2", "gb", "ret_n", "gla_n", "w_out", "post_mix", "pre_ffn",
                                     "up", "cw", "cb", "down", "post_ffn")}
    dh_out, dhn_next = dy, None
    for l in reversed(range(DEPTH)):
        s = saved[l]
        cw_a, cw_g, cb_a, cb_g = s["conv"]
        if l + 1 < DEPTH:
            dh, df, g["pre_mix"][l + 1], g["post_ffn"][l] = _resid_norm_bwd(
                dh_out, dhn_next, saved[l + 1]["h_in"], s["f"], row(pre_mix_norm, l + 1), row(post_ffn_norm, l),
                name="resid_ffn_bwd")
        else:
            dh, df, g["post_ffn"][l] = dh_out, df_last, dw_last
        g["down"][l] = wts.mm("ffn_down_dw", l, s["act"], df, fn=_mm_tn, tn=512)
        du_a, du_g, dcw_a, dcw_g, dcb_a, dcb_g, dhn2 = _conv_act_bwd(
            s["ua"], s["ug"], df, wts.get("down", l), cw_a, cw_g, cb_a, cb_g, wts.get("up_a", l), wts.get("up_g", l),
            name="conv_act_bwd")
        g["cw"][l] = jnp.concatenate([dcw_a, dcw_g], axis=1)
        g["cb"][l] = jnp.concatenate([dcb_a, dcb_g], axis=1)[0]
        half_up = wts.mm("ffn_up_a_dw", l, s["hn2"], du_a, fn=_mm_tn, tn=D_FF // 2, blocks=(4, 0))
        g["up"][l] = _mm_tn(s["hn2"], du_g, tn=D_FF // 2, blocks=(4, 2), into=half_up, name="ffn_up_g_dw")
        dh, dm, g["pre_ffn"][l], g["post_mix"][l] = _resid_norm_bwd(
            dh, dhn2, s["h_mid"], s["m"], row(pre_ffn_norm, l), row(post_mix_norm, l), name="resid_mix_bwd")
        g["w_out"][l] = _mm_tn(s["merged"], dm, name="mix_out_dw")
        wts.grads_done(l, g, ("w_out", "up", "down"))
        dmerged = wts.mm("mix_out_dx", l, dm, wts.get("wout", l), nt=True)
        do_ret, do_gla, d_gate, g["ret_n"][l], g["gla_n"][l] = _merge_bwd(
            dmerged, s["o_ret"], s["o_gla"], s["proj"], row(ret_norm_w, l), row(gla_norm_w, l), name="merge_bwd")
        d_ret = _retention_bwd(s["proj"], do_ret, s["st_ret"], rtab, name="retention_bwd")
        d_gla, dw2, dgb = _gla_bwd(s["proj"], do_gla, s["st_gla"], s["gz"], wts.get("w2p", l), gtab, name="gla_bwd")
        g["w2"][l], g["gb"][l] = dw2[:GLA_RANK], dgb[0]
        pieces = (d_ret, d_gate, d_gla)
        g["w_in"][l] = _to_reference_chips(*[wts.mm(f"proj_dw_{i}", l, s["hn"], p, fn=_mm_tn)
                                             for i, p in enumerate(pieces)])
        win = wts.get("win", l)
        dhn_next = wts.mm("proj_dx", l, pieces, [win[:, 0:P_RET], win[:, P_RET:P_RET + P_GATE], win[:, P_RET + P_GATE:]],
                          fn=_mm_nt_sum)
        dh_out = dh
        wts.grads_done(l, g, ("w_in",))
    dh0, _, g["pre_mix"][0], _ = _resid_norm_bwd(dh_out, dhn_next, h0, None, row(pre_mix_norm, 0), None,
                                                 name="norm_in_bwd")
    return loss_local, dh0, g


def kernel(x, meta_tokens, pre_mix_norm, w_in, gla_gate_w2, gla_gate_b, ret_norm_w, gla_norm_w, w_out, post_mix_norm, pre_ffn_norm, ffn_up, ffn_conv_w, ffn_conv_b, ffn_down, post_ffn_norm, loss_target, m_meta_tokens, m_pre_mix_norm, m_w_in, m_gla_gate_w2, m_gla_gate_b, m_ret_norm_w, m_gla_norm_w, m_w_out, m_post_mix_norm, m_pre_ffn_norm, m_ffn_up, m_ffn_conv_w, m_ffn_conv_b, m_ffn_down, m_post_ffn_norm, v_meta_tokens, v_pre_mix_norm, v_w_in, v_gla_gate_w2, v_gla_gate_b, v_ret_norm_w, v_gla_norm_w, v_w_out, v_post_mix_norm, v_pre_ffn_norm, v_ffn_up, v_ffn_conv_w, v_ffn_conv_b, v_ffn_down, v_post_ffn_norm):
    xi, yi, ci = _place()
    chip = 2 * xi + yi
    seq = x.shape[1]
    d = D_MODEL
    wts = _ChipWeights(w_in, w_out, ffn_up, ffn_down, meta_tokens, gla_gate_w2, ffn_conv_w)
    loss_local, dh0, g = _local_step(x[0], loss_target[0], wts, pre_mix_norm, gla_gate_b, ret_norm_w, gla_norm_w,
                                     post_mix_norm, pre_ffn_norm, ffn_conv_b, post_ffn_norm)
    grad_x = dh0[FRONT:FRONT + seq][None]
    names = ("w_in", "w_out", "ffn_up", "ffn_down")
    big_halves = wts.reduce()

    small_full = [dh0[PADF:FRONT], jnp.stack(g["pre_mix"])[:, 0], jnp.stack(g["w2"]), jnp.stack(g["gb"]),
                  jnp.stack(g["ret_n"])[:, 0], jnp.stack(g["gla_n"])[:, 0], jnp.stack(g["post_mix"])[:, 0],
                  jnp.stack(g["pre_ffn"])[:, 0], jnp.stack(g["cw"]), jnp.stack(g["cb"]),
                  jnp.stack(g["post_ffn"])[:, 0]]
    small_sum = _unslab(_allreduce_small(_slab(small_full, 8), name="small_allreduce"), [a.shape for a in small_full])
    (g_meta, g_pre_mix, g_w2, g_gb, g_ret_n, g_gla_n, g_post_mix, g_pre_ffn, g_cw, g_cb, g_post_ffn) = small_sum
    g_meta = lax.dynamic_slice_in_dim(g_meta, chip * 256, 256, axis=1)
    g_w2 = lax.dynamic_slice_in_dim(g_w2, chip * 64, 64, axis=2)
    g_cw = lax.dynamic_slice_in_dim(g_cw, chip * 1408, 1408, axis=2)

    grads = [g_meta, g_pre_mix, None, g_w2, g_gb, g_ret_n, g_gla_n, None, g_post_mix, g_pre_ffn, None,
             g_cw, g_cb, None, g_post_ffn]
    ws = [meta_tokens, pre_mix_norm, w_in, gla_gate_w2, gla_gate_b, ret_norm_w, gla_norm_w, w_out, post_mix_norm,
          pre_ffn_norm, ffn_up, ffn_conv_w, ffn_conv_b, ffn_down, post_ffn_norm]
    ms = [m_meta_tokens, m_pre_mix_norm, m_w_in, m_gla_gate_w2, m_gla_gate_b, m_ret_norm_w, m_gla_norm_w, m_w_out,
          m_post_mix_norm, m_pre_ffn_norm, m_ffn_up, m_ffn_conv_w, m_ffn_conv_b, m_ffn_down, m_post_ffn_norm]
    vs = [v_meta_tokens, v_pre_mix_norm, v_w_in, v_gla_gate_w2, v_gla_gate_b, v_ret_norm_w, v_gla_norm_w, v_w_out,
          v_post_mix_norm, v_pre_ffn_norm, v_ffn_up, v_ffn_conv_w, v_ffn_conv_b, v_ffn_down, v_post_ffn_norm]
    big_idx = (2, 7, 10, 13)
    deltas, new_m, new_v = [None] * 15, [None] * 15, [None] * 15
    for i, nm, (mine, theirs) in zip(big_idx, names, big_halves):
        grads[i], deltas[i], new_m[i], new_v[i] = _adamw_halves(ws[i], ms[i], vs[i], mine, theirs, ci,
                                                                name=f"adamw_{nm}")
    small_idx = [i for i in range(15) if i not in big_idx]
    shapes = [ws[i].shape for i in small_idx]
    sd, sm, sv = _adamw(_slab([ws[i] for i in small_idx], 8), _slab([grads[i] for i in small_idx], 8),
                        _slab([ms[i] for i in small_idx], 8), _slab([vs[i] for i in small_idx], 8), name="adamw_small")
    for i, a, b, c_ in zip(small_idx, _unslab(sd, shapes), _unslab(sm, shapes), _unslab(sv, shapes)):
        deltas[i], new_m[i], new_v[i] = a, b, c_

    loss = lax.psum(loss_local, ("x", "y", "c"))
    return (loss, grad_x, *grads, *deltas, *new_m, *new_v)
```

```python
import functools
import math

import numpy as np
import jax
import jax.numpy as jnp
from jax import lax
from jax.experimental import pallas as pl
from jax.experimental.pallas import tpu as pltpu

F32 = jnp.float32
BF16 = jnp.bfloat16

D_MODEL = 1024
DEPTH = 2
N_META = 16
EPS = 1e-6
RET_HEADS = 4
RET_DK = 128
GLA_HEADS = 4
GLA_DK = 64
GLA_DV = 128
GLA_QK = GLA_HEADS * GLA_DK
GLA_V = GLA_HEADS * GLA_DV
GLA_RANK = 16
GLA_TAU = 16.0
D_FF = 2816
ROPE_BASE = 10000.0
IN_WIDTH = 3600
IN_PAD = 3840
C_RQ, C_RK, C_RV, C_RG, C_GR, C_GQ, C_GK, C_GV, C_GA = 0, 512, 1024, 1536, 2048, 2560, 2816, 3072, 3584
P_RET, P_GATE, P_GLA = 1536, 1024, 1280


def _to_kernel_cols(w):
    pad = jnp.zeros(w.shape[:-1] + (IN_PAD - IN_WIDTH,), w.dtype)
    return jnp.concatenate([w[..., 0:2048], w[..., 3072:3584], w[..., 2048:3072], w[..., 3584:3600], pad], axis=-1)


def _to_reference_chips(d_ret, d_gate, d_gla):
    segs = [(d_ret, 0, 0, 1536), (d_gate, 0, 1536, 512), (d_gla, 0, 2048, 1024), (d_gate, 512, 3072, 512),
            (d_gla, 1024, 3584, GLA_RANK)]
    per = IN_WIDTH // 4
    chips = []
    for j in range(4):
        lo, hi, parts = per * j, per * (j + 1), []
        for piece, p0, r0, width in segs:
            a, b = max(lo, r0), min(hi, r0 + width)
            if a < b:
                parts.append(piece[:, p0 + a - r0:p0 + b - r0])
        chips.append(jnp.concatenate(parts, axis=1))
    return jnp.stack(chips)

FRONT = 64
BACK = 64
PADF = FRONT - N_META
RET_CHUNK = 128
GLA_CHUNK = 64
GLA_SUB = 16
GLA_SUB2 = 4
BLK = 640

ADAM_LR, ADAM_B1, ADAM_B2, ADAM_EPS, ADAM_WD, ADAM_STEP = 0.001, 0.9, 0.999, 1e-08, 0.01, 10

VMEM_LIMIT = 56 * 2 ** 20
MM_VMEM_BUDGET = 40 * 2 ** 20
MESH = pl.DeviceIdType.MESH


def _cp(*sem):
    return pltpu.CompilerParams(dimension_semantics=sem, vmem_limit_bytes=VMEM_LIMIT)


def _tile(n, cands):
    for t in cands:
        if n % t == 0:
            return t
    raise ValueError(f"no tile for {n} in {cands}")


def _row_tile(n):
    return _tile(n, (640, 512, 320, 256, 128, 64))


def _mm(a, b, *, nt=False, add=None, out_dtype=F32, tn=None, name, carry=None):
    m, k = a.shape
    n = b.shape[0] if nt else b.shape[1]
    tm = _tile(m, (640, 320, 256, 128, 64))
    if tn is None:
        step_bytes = lambda t: 2 * (tm * k * a.dtype.itemsize + t * k * b.dtype.itemsize
                                    + tm * t * (jnp.dtype(out_dtype).itemsize + (4 if add is not None else 0)))
        tn = next(t for t in range(n, 0, -128) if n % t == 0 and (step_bytes(t) <= MM_VMEM_BUDGET or t == 128))
    dn = (((1,), (1,)), ((), ())) if nt else (((1,), (0,)), ((), ()))
    nj, ni = n // tn, m // tm
    n_in = 2 + (add is not None)
    c_arrs, c_mode = carry if carry is not None else ((), None)
    nc = len(c_arrs)

    def body(*refs):
        a_ref, b_ref = refs[:2]
        c_ref = refs[2] if add is not None else None
        o_ref = refs[n_in + nc]
        if nc:
            c_ins, c_outs = refs[n_in:n_in + nc], refs[n_in + nc + 1:n_in + 2 * nc + 1]
            ssem, rsem = refs[n_in + 2 * nc + 1:]
            j, i = pl.program_id(0), pl.program_id(1)

            @pl.when((j == 0) & (i == 0))
            def _():
                for cp in _chip_copies(c_ins, c_outs, ssem, rsem, c_mode)[0]:
                    cp.start()
        r = lax.dot_general(a_ref[...].astype(BF16), b_ref[...].astype(BF16), dn, preferred_element_type=F32)
        if add is not None:
            r = r + c_ref[...]
        o_ref[...] = r.astype(o_ref.dtype)
        if nc:
            @pl.when((j == nj - 1) & (i == ni - 1))
            def _():
                _chip_wait(*_chip_copies(c_ins, c_outs, ssem, rsem, c_mode))

    b_spec = pl.BlockSpec((tn, k), lambda j, i: (j, 0)) if nt else pl.BlockSpec((k, tn), lambda j, i: (0, j))
    in_specs = [pl.BlockSpec((tm, k), lambda j, i: (i, 0)), b_spec]
    args = [a, b]
    if add is not None:
        in_specs.append(pl.BlockSpec((tm, tn), lambda j, i: (i, j)))
        args.append(add)
    out_shape = jax.ShapeDtypeStruct((m, n), out_dtype)
    out_spec = pl.BlockSpec((tm, tn), lambda j, i: (i, j))
    if not nc:
        return pl.pallas_call(
            body, out_shape=out_shape, grid=(nj, ni), in_specs=in_specs, out_specs=out_spec,
            compiler_params=_cp("parallel", "parallel"), name=name)(*args)
    sem = pltpu.SemaphoreType.DMA
    outs = pl.pallas_call(
        body, out_shape=(out_shape,) + tuple(_landing_shape(x, c_mode) for x in c_arrs), grid=(nj, ni),
        in_specs=in_specs + [ANY] * nc, out_specs=(out_spec,) + (ANY,) * nc,
        scratch_shapes=[sem((nc, 3)), sem((nc, 3))],
        compiler_params=_cp("arbitrary", "arbitrary"), name=name)(*args, *c_arrs)
    return outs[0], list(outs[1:])


def _call_with_carry(body, *, out_shape, grid, in_specs, out_specs, args, semantics, carry, name, aliases=None):
    if carry is None:
        return pl.pallas_call(body, out_shape=out_shape, grid=grid, in_specs=in_specs, out_specs=out_specs,
                              input_output_aliases=aliases or {}, compiler_params=_cp(*semantics), name=name)(*args)
    c_arrs, c_mode = carry
    n_in, nc = len(args), len(c_arrs)

    def carried(*refs):
        c_ins, c_outs = refs[n_in:n_in + nc], refs[n_in + nc + 1:n_in + 2 * nc + 1]
        ssem, rsem = refs[n_in + 2 * nc + 1:]
        ids = [pl.program_id(d) for d in range(len(grid))]
        first = functools.reduce(lambda u, v: u & v, [i == 0 for i in ids])
        last = functools.reduce(lambda u, v: u & v, [i == g - 1 for i, g in zip(ids, grid)])

        @pl.when(first)
        def _():
            for cp in _chip_copies(c_ins, c_outs, ssem, rsem, c_mode)[0]:
                cp.start()
        body(*refs[:n_in], refs[n_in + nc])

        @pl.when(last)
        def _():
            _chip_wait(*_chip_copies(c_ins, c_outs, ssem, rsem, c_mode))

    sem = pltpu.SemaphoreType.DMA
    outs = pl.pallas_call(
        carried, out_shape=(out_shape,) + tuple(_landing_shape(x, c_mode) for x in c_arrs), grid=grid,
        in_specs=list(in_specs) + [ANY] * nc, out_specs=(out_specs,) + (ANY,) * nc,
        scratch_shapes=[sem((nc, 3)), sem((nc, 3))], input_output_aliases=aliases or {},
        compiler_params=_cp(*(("arbitrary",) * len(grid))), name=name)(*args, *c_arrs)
    return outs[0], list(outs[1:])


def _mm_nt_sum(a_list, b_list, *, name, carry=None):
    m, n = a_list[0].shape[0], b_list[0].shape[0]
    tm = _tile(m, (640, 320, 256, 128, 64))
    np_ = len(a_list)

    def body(*refs):
        acc = None
        for a_ref, b_ref in zip(refs[:np_], refs[np_:2 * np_]):
            r = lax.dot_general(a_ref[...].astype(BF16), b_ref[...].astype(BF16), (((1,), (1,)), ((), ())),
                                preferred_element_type=F32)
            acc = r if acc is None else acc + r
        refs[2 * np_][...] = acc

    return _call_with_carry(
        body, out_shape=jax.ShapeDtypeStruct((m, n), F32), grid=(m // tm,),
        in_specs=[pl.BlockSpec((tm, a.shape[1]), lambda i: (i, 0)) for a in a_list]
        + [pl.BlockSpec(b.shape, lambda i: (0, 0)) for b in b_list],
        out_specs=pl.BlockSpec((tm, n), lambda i: (i, 0)), args=[*a_list, *b_list], semantics=("parallel",),
        carry=carry, name=name)


def _mm_tn(a, b, *, tn=None, blocks=None, into=None, name, carry=None):
    m, k = a.shape
    n = b.shape[1]
    tm = _tile(m, (1664, 640, 320, 256, 128, 64))
    tn = n if tn is None else tn
    if blocks is not None:
        total, first = blocks
        out_shape = jax.ShapeDtypeStruct((total, k, tn), F32)
        out_spec = pl.BlockSpec((None, k, tn), lambda j, i: (first + j, 0, 0))
    else:
        out_shape = jax.ShapeDtypeStruct((k, n), F32)
        out_spec = pl.BlockSpec((k, tn), lambda j, i: (0, j))

    def body(a_ref, b_ref, *rest):
        o_ref = rest[-1]

        @pl.when(pl.program_id(1) == 0)
        def _():
            o_ref[...] = jnp.zeros_like(o_ref)
        o_ref[...] += lax.dot_general(a_ref[...].astype(BF16), b_ref[...].astype(BF16),
                                      (((0,), (0,)), ((), ())), preferred_element_type=F32)

    in_specs = [pl.BlockSpec((tm, k), lambda j, i: (i, 0)), pl.BlockSpec((tm, tn), lambda j, i: (i, j))]
    args, alias = [a, b], {}
    if into is not None:
        in_specs.append(pl.BlockSpec(memory_space=pl.ANY))
        args.append(into)
        alias = {2: 0}
    return _call_with_carry(body, out_shape=out_shape, grid=(n // tn, m // tm), in_specs=in_specs, out_specs=out_spec,
                            args=args, semantics=("parallel", "arbitrary"), carry=carry, name=name, aliases=alias)


def _rms(x, w):
    r = lax.rsqrt(jnp.mean(x * x, axis=-1, keepdims=True) + EPS)
    return x * r * w


def _rms_bwd(x, w, dy):
    r = lax.rsqrt(jnp.mean(x * x, axis=-1, keepdims=True) + EPS)
    xh = x * r
    dxh = dy * w
    dx = r * (dxh - xh * jnp.mean(dxh * xh, axis=-1, keepdims=True))
    return dx, jnp.sum(dy * xh, axis=0, keepdims=True)


def _resid_norm(h, t, w_post, w_next, *, name):
    lp, d = h.shape
    tm = _row_tile(lp)
    has_t = t is not None

    def body(*refs):
        if has_t:
            h_ref, t_ref, wp_ref, wn_ref, ho_ref, hn_ref = refs
            hv = h_ref[...] + _rms(t_ref[...], wp_ref[...])
            ho_ref[...] = hv
        else:
            h_ref, wn_ref, hn_ref = refs
            hv = h_ref[...]
        hn_ref[...] = _rms(hv, wn_ref[...]).astype(BF16)

    row = pl.BlockSpec((tm, d), lambda i: (i, 0))
    vec = pl.BlockSpec((1, d), lambda i: (0, 0))
    if has_t:
        return pl.pallas_call(
            body, out_shape=(jax.ShapeDtypeStruct((lp, d), F32), jax.ShapeDtypeStruct((lp, d), BF16)),
            grid=(lp // tm,), in_specs=[row, row, vec, vec], out_specs=(row, row),
            compiler_params=_cp("parallel"), name=name)(h, t, w_post, w_next)
    return h, pl.pallas_call(
        body, out_shape=jax.ShapeDtypeStruct((lp, d), BF16), grid=(lp // tm,), in_specs=[row, vec],
        out_specs=row, compiler_params=_cp("parallel"), name=name)(h, w_next)


def _resid_norm_bwd(dh_out, dhn, h_new, t, w_next, w_post, *, name):
    lp, d = h_new.shape if h_new is not None else t.shape
    tm = _row_tile(lp)
    has_n = dhn is not None
    has_t = t is not None

    def body(*refs):
        refs = list(refs)
        dho_ref = refs.pop(0)
        if has_n:
            dhn_ref, hn_ref, wn_ref = refs.pop(0), refs.pop(0), refs.pop(0)
        if has_t:
            t_ref, wp_ref = refs.pop(0), refs.pop(0)
        dh_ref = refs.pop(0) if has_n else None
        dt_ref = refs.pop(0) if has_t else None
        dwn_ref = refs.pop(0) if has_n else None
        dwp_ref = refs.pop(0) if has_t else None
        first = pl.program_id(0) == 0
        dh = dho_ref[...]
        if has_n:
            dx, dwn = _rms_bwd(hn_ref[...], wn_ref[...], dhn_ref[...])
            dh = dh + dx
            dh_ref[...] = dh

            @pl.when(first)
            def _():
                dwn_ref[...] = jnp.zeros_like(dwn_ref)
            dwn_ref[...] += dwn
        if has_t:
            dt, dwp = _rms_bwd(t_ref[...], wp_ref[...], dh)
            dt_ref[...] = dt.astype(BF16)

            @pl.when(first)
            def _():
                dwp_ref[...] = jnp.zeros_like(dwp_ref)
            dwp_ref[...] += dwp

    row = pl.BlockSpec((tm, d), lambda i: (i, 0))
    vec = pl.BlockSpec((1, d), lambda i: (0, 0))
    args, in_specs, out_shape, out_specs = [dh_out], [row], [], []
    if has_n:
        args += [dhn, h_new, w_next]
        in_specs += [row, row, vec]
    if has_t:
        args += [t, w_post]
        in_specs += [row, vec]
    if has_n:
        out_shape.append(jax.ShapeDtypeStruct((lp, d), F32)); out_specs.append(row)
    if has_t:
        out_shape.append(jax.ShapeDtypeStruct((lp, d), BF16)); out_specs.append(row)
    if has_n:
        out_shape.append(jax.ShapeDtypeStruct((1, d), F32)); out_specs.append(vec)
    if has_t:
        out_shape.append(jax.ShapeDtypeStruct((1, d), F32)); out_specs.append(vec)
    outs = list(pl.pallas_call(body, out_shape=tuple(out_shape), grid=(lp // tm,), in_specs=in_specs,
                               out_specs=tuple(out_specs), compiler_params=_cp("arbitrary"), name=name)(*args))
    dh = outs.pop(0) if has_n else dh_out
    dt = outs.pop(0) if has_t else None
    dwn = outs.pop(0) if has_n else None
    dwp = outs.pop(0) if has_t else None
    return dh, dt, dwn, dwp


def _loss_head(h, f, w_post, target, *, name):
    lp, d = h.shape
    tm = _row_tile(lp)

    def body(h_ref, f_ref, w_ref, t_ref, loss_ref, dy_ref, df_ref, dw_ref):
        i = pl.program_id(0)
        f, w = f_ref[...], w_ref[...]
        y = h_ref[...] + _rms(f, w)
        rows = i * tm + lax.broadcasted_iota(jnp.int32, (tm, 1), 0)
        tok = (rows >= FRONT) & (rows < lp - BACK)
        err = jnp.where(tok, y - t_ref[...], 0.0)
        dy = err * (1.0 / d)
        dy_ref[...] = dy
        df, dw = _rms_bwd(f, w, dy)
        df_ref[...] = df.astype(BF16)

        @pl.when(i == 0)
        def _():
            loss_ref[...] = jnp.zeros_like(loss_ref)
            dw_ref[...] = jnp.zeros_like(dw_ref)
        part = jnp.sum(jnp.sum(err * err, axis=1, keepdims=True), axis=0, keepdims=True) * (0.5 / d)
        loss_ref[...] += jnp.broadcast_to(part, loss_ref.shape)
        dw_ref[...] += dw

    row = pl.BlockSpec((tm, d), lambda i: (i, 0))
    vec = pl.BlockSpec((1, d), lambda i: (0, 0))
    loss, dy, df, dw = pl.pallas_call(
        body, out_shape=(jax.ShapeDtypeStruct((8, 128), F32), jax.ShapeDtypeStruct((lp, d), F32),
                         jax.ShapeDtypeStruct((lp, d), BF16), jax.ShapeDtypeStruct((1, d), F32)),
        grid=(lp // tm,), in_specs=[row, row, vec, row],
        out_specs=(pl.BlockSpec((8, 128), lambda i: (0, 0)), row, row, vec),
        compiler_params=_cp("arbitrary"), name=name)(h, f, w_post, target)
    return loss[0, 0], dy, df, dw


_GELU_C = math.sqrt(2.0 / math.pi)


def _gelu_and_grad(a):
    a2 = a * a
    t = jnp.tanh(a * (_GELU_C + (_GELU_C * 0.044715) * a2))
    ha = 0.5 * a
    h1 = 0.5 + 0.5 * t
    return a * h1, h1 + ha * (1.0 - t * t) * (_GELU_C + (3.0 * _GELU_C * 0.044715) * a2)


def _gelu(a):
    t = jnp.tanh(a * (_GELU_C + (_GELU_C * 0.044715) * (a * a)))
    return a * (0.5 + 0.5 * t)


def _conv3(parts, n, w, b):
    xx = jnp.concatenate(parts, axis=0)
    return b + xx[8:8 + n] * w[2:3] + pltpu.roll(xx, 1, 0)[8:8 + n] * w[1:2] + pltpu.roll(xx, 2, 0)[8:8 + n] * w[0:1]


def _conv_act(ua, ug, wa, wg, ba, bg, *, name):
    lp, n = ua.shape
    tm = _row_tile(lp)
    tc = _tile(n, (256, 128))
    nb8 = tm // 8

    def body(ua_ref, uap_ref, ug_ref, ugp_ref, wa_ref, wg_ref, ba_ref, bg_ref, o_ref):
        i = pl.program_id(0)
        ca = _conv3([uap_ref[...], ua_ref[...]], tm, wa_ref[...], ba_ref[...])
        cg = _conv3([ugp_ref[...], ug_ref[...]], tm, wg_ref[...], bg_ref[...])
        rows = i * tm + lax.broadcasted_iota(jnp.int32, (tm, 1), 0)
        ok = (rows >= PADF) & (rows < lp - BACK)
        o_ref[...] = jnp.where(ok, _gelu(ca) * cg, 0.0).astype(BF16)

    cur = pl.BlockSpec((tm, tc), lambda i, j: (i, j))
    prev = pl.BlockSpec((8, tc), lambda i, j: (jnp.maximum(i * nb8 - 1, 0), j))
    w3 = pl.BlockSpec((3, tc), lambda i, j: (0, j))
    b1 = pl.BlockSpec((1, tc), lambda i, j: (0, j))
    return pl.pallas_call(
        body, out_shape=jax.ShapeDtypeStruct((lp, n), BF16), grid=(lp // tm, n // tc),
        in_specs=[cur, prev, cur, prev, w3, w3, b1, b1], out_specs=cur,
        compiler_params=_cp("parallel", "parallel"), name=name)(ua, ua, ug, ug, wa, wg, ba, bg)


def _conv_act_down(ua, ug, wa, wg, ba, bg, down, *, name):
    lp, n = ua.shape
    d = down.shape[1]
    tm = _tile(lp, (320, 256, 128, 64))
    tc = _tile(n, (256, 128))
    nb8 = tm // 8

    def body(ua_ref, uap_ref, ug_ref, ugp_ref, wa_ref, wg_ref, ba_ref, bg_ref, dn_ref, act_ref, f_ref):
        i = pl.program_id(0)
        rows = i * tm + lax.broadcasted_iota(jnp.int32, (tm, 1), 0)
        ok = (rows >= PADF) & (rows < lp - BACK)
        acc = None
        for j in range(n // tc):
            cs = slice(tc * j, tc * j + tc)
            ca = _conv3([uap_ref[:, cs], ua_ref[:, cs]], tm, wa_ref[:, cs], ba_ref[:, cs])
            cg = _conv3([ugp_ref[:, cs], ug_ref[:, cs]], tm, wg_ref[:, cs], bg_ref[:, cs])
            act = jnp.where(ok, _gelu(ca) * cg, 0.0).astype(BF16)
            act_ref[:, cs] = act
            part = _dot(act, dn_ref[cs, :])
            acc = part if acc is None else acc + part
        f_ref[...] = acc

    cur = pl.BlockSpec((tm, n), lambda i: (i, 0))
    prev = pl.BlockSpec((8, n), lambda i: (jnp.maximum(i * nb8 - 1, 0), 0))
    w3 = pl.BlockSpec((3, n), lambda i: (0, 0))
    b1 = pl.BlockSpec((1, n), lambda i: (0, 0))
    return pl.pallas_call(
        body, out_shape=(jax.ShapeDtypeStruct((lp, n), BF16), jax.ShapeDtypeStruct((lp, d), F32)),
        grid=(lp // tm,),
        in_specs=[cur, prev, cur, prev, w3, w3, b1, b1, pl.BlockSpec(down.shape, lambda i: (0, 0))],
        out_specs=(cur, pl.BlockSpec((tm, d), lambda i: (i, 0))),
        compiler_params=_cp("parallel"), name=name)(ua, ua, ug, ug, wa, wg, ba, bg, down)


def _ffn_fwd(h, m, w_post, w_next, up_a, up_g, wa, wg, ba, bg, down, *, name, carry=None):
    lp, d = h.shape
    n = up_a.shape[1]
    tm = _tile(lp, (320, 256, 128, 64))
    tc = _tile(n, (256, 128))
    nchunks = n // tc
    c_arrs, c_mode = carry if carry is not None else ((), None)
    nc = len(c_arrs)
    steps = lp // tm
    n_out = 6

    def body(h_ref, hp_ref, m_ref, mp_ref, wp_ref, wn_ref, upa_ref, upg_ref, wa_ref, wg_ref, ba_ref, bg_ref, dn_ref,
             *rest):
        c_ins = rest[:nc]
        hmid_ref, hn_ref, ua_ref, ug_ref, act_ref, f_ref = rest[nc:nc + n_out]
        c_outs = rest[nc + n_out:2 * nc + n_out]
        i = pl.program_id(0)
        if nc:
            ssem, rsem = rest[2 * nc + n_out:]

            @pl.when(i == 0)
            def _():
                for cp in _chip_copies(c_ins, c_outs, ssem, rsem, c_mode)[0]:
                    cp.start()
        rows = i * tm + lax.broadcasted_iota(jnp.int32, (tm, 1), 0)
        ok = (rows >= PADF) & (rows < lp - BACK)
        hv = (jnp.concatenate([hp_ref[...], h_ref[...]], axis=0)
              + _rms(jnp.concatenate([mp_ref[...], m_ref[...]], axis=0), wp_ref[...]))
        x = _rms(hv, wn_ref[...]).astype(BF16)
        hmid_ref[...] = hv[16:]
        hn_ref[...] = x[16:]
        u_of = lambda j: (_dot(x, upa_ref[:, tc * j:tc * j + tc]), _dot(x, upg_ref[:, tc * j:tc * j + tc]))
        u_next = u_of(0)
        acc = None
        for j in range(nchunks):
            cs = slice(tc * j, tc * j + tc)
            ua, ug = u_next
            if j + 1 < nchunks:
                u_next = u_of(j + 1)
            ua_ref[:, cs] = ua[16:]
            ug_ref[:, cs] = ug[16:]
            ca = _conv3([ua[8:]], tm, wa_ref[:, cs], ba_ref[:, cs])
            cg = _conv3([ug[8:]], tm, wg_ref[:, cs], bg_ref[:, cs])
            act = jnp.where(ok, _gelu(ca) * cg, 0.0).astype(BF16)
            act_ref[:, cs] = act
            part = _dot(act, dn_ref[cs, :])
            acc = part if acc is None else acc + part
        f_ref[...] = acc
        if nc:
            @pl.when(i == steps - 1)
            def _():
                _chip_wait(*_chip_copies(c_ins, c_outs, ssem, rsem, c_mode))

    whole = pl.BlockSpec(memory_space=pltpu.VMEM)
    wide = pl.BlockSpec((tm, n), lambda i: (i, 0))
    w3 = pl.BlockSpec((3, n), lambda i: (0, 0))
    b1 = pl.BlockSpec((1, n), lambda i: (0, 0))
    sem = pltpu.SemaphoreType.DMA
    row = pl.BlockSpec((tm, d), lambda i: (i, 0))
    prev16 = pl.BlockSpec((16, d), lambda i: (jnp.maximum(i * (tm // 16) - 1, 0), 0))
    vec = pl.BlockSpec((1, d), lambda i: (0, 0))
    outs = pl.pallas_call(
        body,
        out_shape=(jax.ShapeDtypeStruct((lp, d), F32), jax.ShapeDtypeStruct((lp, d), BF16),
                   jax.ShapeDtypeStruct((lp, n), F32), jax.ShapeDtypeStruct((lp, n), F32),
                   jax.ShapeDtypeStruct((lp, n), BF16), jax.ShapeDtypeStruct((lp, d), F32))
        + tuple(_landing_shape(a, c_mode) for a in c_arrs),
        grid=(steps,),
        in_specs=[row, prev16, row, prev16, vec, vec, whole, whole, w3, w3, b1, b1, whole] + [ANY] * nc,
        out_specs=(row, row, wide, wide, wide, row) + (ANY,) * nc,
        scratch_shapes=[sem((nc, 3)), sem((nc, 3))] if nc else [],
        compiler_params=_cp("arbitrary"), name=name)(h, h, m, m, w_post, w_next, up_a, up_g, wa, wg, ba, bg, down,
                                                     *c_arrs)
    return outs[:n_out], list(outs[n_out:])


def _conv_act_bwd(ua, ug, df, down, wa, wg, ba, bg, up_a, up_g, *, name):
    lp, n = ua.shape
    d = up_a.shape[0]
    tm = _tile(lp, (320, 256, 128, 64))
    tc = _tile(n, (256, 128))
    nb8 = tm // 8
    last8 = lp // 8 - 1
    last16 = lp // 16 - 1
    ext = tm + 8

    def body(ua_ref, uap_ref, uan_ref, ug_ref, ugp_ref, ugn_ref, df_ref, dfn_ref, dn_ref, wa_ref, wg_ref, ba_ref,
             bg_ref, upa_ref, upg_ref, dua_ref, dug_ref, dwa_ref, dwg_ref, dba_ref, dbg_ref, dhn_ref):
        i = pl.program_id(0)
        df_ext = jnp.concatenate([df_ref[...], dfn_ref[...]], axis=0)

        @pl.when(i == 0)
        def _():
            dwa_ref[...] = jnp.zeros_like(dwa_ref)
            dwg_ref[...] = jnp.zeros_like(dwg_ref)
            dba_ref[...] = jnp.zeros_like(dba_ref)
            dbg_ref[...] = jnp.zeros_like(dbg_ref)
        rows = i * tm + lax.broadcasted_iota(jnp.int32, (ext, 1), 0)
        ok = (rows >= PADF) & (rows < lp - BACK)

        def conv(parts, w, b):
            xx = jnp.concatenate(parts, axis=0)
            x, x1, x2 = xx[8:8 + ext], pltpu.roll(xx, 1, 0)[8:8 + ext], pltpu.roll(xx, 2, 0)[8:8 + ext]
            return b + x * w[2:3] + x1 * w[1:2] + x2 * w[0:1], x, x1, x2

        def back(dc, w):
            return (dc[:tm] * w[2:3] + pltpu.roll(dc, ext - 1, 0)[:tm] * w[1:2]
                    + pltpu.roll(dc, ext - 2, 0)[:tm] * w[0:1])

        def wsum(dw_ref, db_ref, cs, dc, x, x1, x2):
            dd = dc[:tm]
            s = lambda v: jnp.sum(v, axis=0, keepdims=True)
            dw_ref[0:1, cs] += s(dd * x2[:tm])
            dw_ref[1:2, cs] += s(dd * x1[:tm])
            dw_ref[2:3, cs] += s(dd * x[:tm])
            db_ref[:, cs] += s(dd)

        acc = None
        nchunks = n // tc
        dact_of = lambda j: _dot_nt(df_ext, dn_ref[tc * j:tc * j + tc, :])[:ext]
        dact_next = dact_of(0)
        for j in range(nchunks):
            cs = slice(tc * j, tc * j + tc)
            dact_cur = dact_next
            if j + 1 < nchunks:
                dact_next = dact_of(j + 1)
            wa, wg = wa_ref[:, cs], wg_ref[:, cs]
            ca, xa, xa1, xa2 = conv([uap_ref[:, cs], ua_ref[:, cs], uan_ref[:, cs]], wa, ba_ref[:, cs])
            cg, xg, xg1, xg2 = conv([ugp_ref[:, cs], ug_ref[:, cs], ugn_ref[:, cs]], wg, bg_ref[:, cs])
            dact_e = jnp.where(ok, dact_cur, 0.0)
            gel, gel_d = _gelu_and_grad(ca)
            dca = dact_e * cg * gel_d
            dcg = dact_e * gel
            du_a, du_g = back(dca, wa).astype(BF16), back(dcg, wg).astype(BF16)
            dua_ref[:, cs] = du_a
            dug_ref[:, cs] = du_g
            wsum(dwa_ref, dba_ref, cs, dca, xa, xa1, xa2)
            wsum(dwg_ref, dbg_ref, cs, dcg, xg, xg1, xg2)
            part = _dot_nt(du_a, upa_ref[:, cs]) + _dot_nt(du_g, upg_ref[:, cs])
            acc = part if acc is None else acc + part
        dhn_ref[...] = acc

    cur = pl.BlockSpec((tm, n), lambda i: (i, 0))
    prev = pl.BlockSpec((8, n), lambda i: (jnp.maximum(i * nb8 - 1, 0), 0))
    nxt = pl.BlockSpec((8, n), lambda i: (jnp.minimum((i + 1) * nb8, last8), 0))
    w3 = pl.BlockSpec((3, n), lambda i: (0, 0))
    b1 = pl.BlockSpec((1, n), lambda i: (0, 0))
    whole = pl.BlockSpec(memory_space=pltpu.VMEM)
    return pl.pallas_call(
        body,
        out_shape=(jax.ShapeDtypeStruct((lp, n), BF16), jax.ShapeDtypeStruct((lp, n), BF16),
                   jax.ShapeDtypeStruct((3, n), F32), jax.ShapeDtypeStruct((3, n), F32),
                   jax.ShapeDtypeStruct((1, n), F32), jax.ShapeDtypeStruct((1, n), F32),
                   jax.ShapeDtypeStruct((lp, d), F32)),
        grid=(lp // tm,),
        in_specs=[cur, prev, nxt, cur, prev, nxt, pl.BlockSpec((tm, d), lambda i: (i, 0)),
                  pl.BlockSpec((16, d), lambda i: (jnp.minimum((i + 1) * (tm // 16), last16), 0)), whole,
                  w3, w3, b1, b1, whole, whole],
        out_specs=(cur, cur, w3, w3, b1, b1, pl.BlockSpec((tm, d), lambda i: (i, 0))),
        compiler_params=_cp("arbitrary"), name=name)(ua, ua, ua, ug, ug, ug, df, df, down, wa, wg, ba, bg, up_a, up_g)


def _sigmoid(x):
    return 1.0 / (1.0 + jnp.exp(-x))


def _merge_mix_out(o_ret, o_gla, proj, w_ret, w_gla, wout, *, name, carry=None):
    lp = o_ret.shape[0]
    d = wout.shape[1]
    tm = _row_tile(lp)
    steps = lp // tm
    c_arrs, c_mode = carry if carry is not None else ((), None)
    nc = len(c_arrs)

    def body(or_ref, og_ref, rg_ref, gr_ref, wr_ref, wg_ref, wo_ref, *rest):
        c_ins = rest[:nc]
        m_ref, out_ref = rest[nc:nc + 2]
        c_outs = rest[nc + 2:2 * nc + 2]
        i = pl.program_id(0)
        if nc:
            ssem, rsem = rest[2 * nc + 2:]

            @pl.when(i == 0)
            def _():
                for cp in _chip_copies(c_ins, c_outs, ssem, rsem, c_mode)[0]:
                    cp.start()
        oret, ogla = or_ref[...], og_ref[...]
        yr, yg = [], []
        for h in range(4):
            hs = slice(128 * h, 128 * h + 128)
            o = oret[:, hs]
            xc = o - jnp.mean(o, axis=-1, keepdims=True)
            yr.append(xc * lax.rsqrt(jnp.mean(xc * xc, axis=-1, keepdims=True) + EPS))
            o = ogla[:, hs]
            yg.append(o * lax.rsqrt(jnp.mean(o * o, axis=-1, keepdims=True) + EPS))
        rg, gr = rg_ref[...], gr_ref[...]
        ret = (jnp.concatenate(yr, axis=1) * wr_ref[...] * (rg * _sigmoid(rg))).astype(BF16)
        gla = (jnp.concatenate(yg, axis=1) * wg_ref[...] * (gr * _sigmoid(gr))).astype(BF16)
        m_ref[:, 0:512] = ret
        m_ref[:, 512:1024] = gla
        out_ref[...] = _dot(ret, wo_ref[0:512, :]) + _dot(gla, wo_ref[512:1024, :])
        if nc:
            @pl.when(i == steps - 1)
            def _():
                _chip_wait(*_chip_copies(c_ins, c_outs, ssem, rsem, c_mode))

    row = pl.BlockSpec((tm, 512), lambda i: (i, 0))
    vec = pl.BlockSpec((1, 512), lambda i: (0, 0))
    wide = pl.BlockSpec((tm, 1024), lambda i: (i, 0))
    sem = pltpu.SemaphoreType.DMA
    outs = pl.pallas_call(
        body, out_shape=(jax.ShapeDtypeStruct((lp, 1024), BF16), jax.ShapeDtypeStruct((lp, d), F32))
        + tuple(_landing_shape(a, c_mode) for a in c_arrs),
        grid=(steps,),
        in_specs=[row, row, pl.BlockSpec((tm, 512), lambda i: (i, C_RG // 512)),
                  pl.BlockSpec((tm, 512), lambda i: (i, C_GR // 512)), vec, vec,
                  pl.BlockSpec(memory_space=pltpu.VMEM)] + [ANY] * nc,
        out_specs=(wide, pl.BlockSpec((tm, d), lambda i: (i, 0))) + (ANY,) * nc,
        scratch_shapes=[sem((nc, 3)), sem((nc, 3))] if nc else [],
        compiler_params=_cp("arbitrary"), name=name)(o_ret, o_gla, proj, proj, w_ret, w_gla, wout, *c_arrs)
    return outs[:2], list(outs[2:])


def _merge_bwd(dm, wout, o_ret, o_gla, proj, w_ret, w_gla, *, name, carry=None):
    lp = o_ret.shape[0]
    tm = _row_tile(lp)
    steps = lp // tm
    c_arrs, c_mode = carry if carry is not None else ((), None)
    nc = len(c_arrs)

    def body(dm_ref, wo_ref, or_ref, og_ref, rg_ref, gr_ref, wr_ref, wg_ref, *rest):
        c_ins = rest[:nc]
        dor_ref, dog_ref, dgate_ref, dwr_ref, dwg_ref = rest[nc:nc + 5]
        c_outs = rest[nc + 5:2 * nc + 5]
        i = pl.program_id(0)
        if nc:
            ssem, rsem = rest[2 * nc + 5:]

            @pl.when(i == 0)
            def _():
                for cp in _chip_copies(c_ins, c_outs, ssem, rsem, c_mode)[0]:
                    cp.start()

        @pl.when(i == 0)
        def _():
            dwr_ref[...] = jnp.zeros_like(dwr_ref)
            dwg_ref[...] = jnp.zeros_like(dwg_ref)

        def group(d, o_all, gate, w, center):
            sg = _sigmoid(gate)
            s = gate * sg
            ds = sg * (1.0 + gate * (1.0 - sg))
            xh, rr = [], []
            for h in range(4):
                o = o_all[:, 128 * h:128 * h + 128]
                if center:
                    o = o - jnp.mean(o, axis=-1, keepdims=True)
                r = lax.rsqrt(jnp.mean(o * o, axis=-1, keepdims=True) + EPS)
                xh.append(o * r)
                rr.append(r)
            xh_all = jnp.concatenate(xh, axis=1)
            dgate = d * xh_all * w * ds
            dw = jnp.sum(d * xh_all * s, axis=0, keepdims=True)
            dxh_all = d * w * s
            do = []
            for h in range(4):
                dxh = dxh_all[:, 128 * h:128 * h + 128]
                t = dxh - xh[h] * jnp.mean(dxh * xh[h], axis=-1, keepdims=True)
                if center:
                    t = t - jnp.mean(dxh, axis=-1, keepdims=True)
                do.append(rr[h] * t)
            return jnp.concatenate(do, axis=1), dgate, dw

        dmb = dm_ref[...]
        do, dg, dw = group(_dot_nt(dmb, wo_ref[0:512, :]), or_ref[...], rg_ref[...], wr_ref[...], True)
        dor_ref[...] = do
        dgate_ref[:, 0:512] = dg.astype(BF16)
        dwr_ref[...] += dw
        do, dg, dw = group(_dot_nt(dmb, wo_ref[512:1024, :]), og_ref[...], gr_ref[...], wg_ref[...], False)
        dog_ref[...] = do
        dgate_ref[:, 512:1024] = dg.astype(BF16)
        dwg_ref[...] += dw
        if nc:
            @pl.when(i == steps - 1)
            def _():
                _chip_wait(*_chip_copies(c_ins, c_outs, ssem, rsem, c_mode))

    row = pl.BlockSpec((tm, 512), lambda i: (i, 0))
    vec = pl.BlockSpec((1, 512), lambda i: (0, 0))
    sem = pltpu.SemaphoreType.DMA
    outs = pl.pallas_call(
        body,
        out_shape=(jax.ShapeDtypeStruct((lp, 512), F32), jax.ShapeDtypeStruct((lp, 512), F32),
                   jax.ShapeDtypeStruct((lp, P_GATE), BF16),
                   jax.ShapeDtypeStruct((1, 512), F32), jax.ShapeDtypeStruct((1, 512), F32))
        + tuple(_landing_shape(a, c_mode) for a in c_arrs),
        grid=(steps,),
        in_specs=[pl.BlockSpec((tm, dm.shape[1]), lambda i: (i, 0)), pl.BlockSpec(memory_space=pltpu.VMEM), row, row,
                  pl.BlockSpec((tm, 512), lambda i: (i, C_RG // 512)),
                  pl.BlockSpec((tm, 512), lambda i: (i, C_GR // 512)), vec, vec] + [ANY] * nc,
        out_specs=(row, row, pl.BlockSpec((tm, P_GATE), lambda i: (i, 0)), vec, vec) + (ANY,) * nc,
        scratch_shapes=[sem((nc, 3)), sem((nc, 3))] if nc else [],
        compiler_params=_cp("arbitrary"), name=name)(dm, wout, o_ret, o_gla, proj, proj, w_ret, w_gla, *c_arrs)
    return outs[:5], list(outs[5:])


def _dot(a, b):
    return lax.dot_general(a, b, (((1,), (0,)), ((), ())), preferred_element_type=F32)


def _dot_nt(a, b):
    return lax.dot_general(a, b, (((1,), (1,)), ((), ())), preferred_element_type=F32)


def _dot_tn(a, b):
    return lax.dot_general(a, b, (((0,), (0,)), ((), ())), preferred_element_type=F32)


def _ret_tables(lp):
    cr = RET_CHUNK
    pos = np.arange(lp, dtype=np.float32) - np.float32(PADF)
    half = RET_DK // 2
    inv = (np.float32(ROPE_BASE) ** (-np.arange(half, dtype=np.float32) / np.float32(half))).astype(np.float32)
    ang = (pos[:, None] * inv[None, :]).astype(np.float32)
    c, s = np.cos(ang).astype(np.float32), np.sin(ang).astype(np.float32)
    rope_c = jnp.asarray(np.concatenate([c, c], axis=1))
    rope_s = jnp.asarray(np.concatenate([-s, s], axis=1))
    log_g = np.log(1.0 - 2.0 ** (-5.0 - np.arange(RET_HEADS, dtype=np.float64)))
    idx = np.arange(cr, dtype=np.float64)
    diff = idx[:, None] - idx[None, :]
    dmat = np.where(diff >= 0, np.exp(log_g[:, None, None] * np.maximum(diff, 0.0)), 0.0)
    zeta = np.exp(log_g[:, None] * (cr - 1.0 - idx)[None, :])
    xi = np.exp(log_g[:, None] * (idx + 1.0)[None, :])
    gc = np.exp(log_g * cr)
    f = lambda a: jnp.asarray(a.astype(np.float32))
    return (rope_c, rope_s, f(dmat), f(np.broadcast_to(zeta[:, :, None], (RET_HEADS, cr, 128))),
            f(np.broadcast_to(xi[:, :, None], (RET_HEADS, cr, 128))),
            f(np.broadcast_to(gc[:, None, None], (RET_HEADS, 8, 128))))


def _rope(t, c, s):
    return t * c + pltpu.roll(t, 64, 1) * s


def _rope_t(d, c, s):
    return d * c + pltpu.roll(d * s, 64, 1)


def _ret_specs(nblk, rev):
    ix = (lambda i: nblk - 1 - i) if rev else (lambda i: i)
    cr = RET_CHUNK
    col = lambda base: pl.BlockSpec((BLK, 512), lambda i: (ix(i), base // 512))
    tab = pl.BlockSpec((BLK, 128), lambda i: (ix(i), 0))
    sq = pl.BlockSpec((RET_HEADS, cr, cr), lambda i: (0, 0, 0))
    hv = pl.BlockSpec((RET_HEADS, cr, 128), lambda i: (0, 0, 0))
    g8 = pl.BlockSpec((RET_HEADS, 8, 128), lambda i: (0, 0, 0))
    st = pl.BlockSpec((RET_HEADS, BLK // cr, 128, 128), lambda i: (0, ix(i), 0, 0))
    out = pl.BlockSpec((BLK, 512), lambda i: (ix(i), 0))
    return col, tab, sq, hv, g8, st, out


def _retention(proj, tables, *, name):
    lp = proj.shape[0]
    nblk, cr = lp // BLK, RET_CHUNK
    scale = RET_DK ** -0.5

    def body(q_ref, k_ref, v_ref, c_ref, s_ref, d_ref, z_ref, x_ref, g_ref, o_ref, st_ref, state):
        @pl.when(pl.program_id(0) == 0)
        def _():
            state[...] = jnp.zeros_like(state)

        def chunk(ci, carry):
            sl = pl.ds(pl.multiple_of(ci * cr, cr), cr)
            c, s = c_ref[sl, :], s_ref[sl, :]
            for h in range(RET_HEADS):
                hs = slice(128 * h, 128 * h + 128)
                q = _rope(q_ref[sl, hs], c, s)
                k = _rope(k_ref[sl, hs], c, s) * scale
                qb, kb, vb = q.astype(BF16), k.astype(BF16), v_ref[sl, hs].astype(BF16)
                st = state[h]
                st_ref[h, ci] = st
                sc = _dot_nt(qb, kb) * d_ref[h]
                o_ref[sl, hs] = _dot(sc.astype(BF16), vb) + _dot(qb, st.astype(BF16)) * x_ref[h]
                state[h] = st * g_ref[h][0:1, :] + _dot_tn((k * z_ref[h]).astype(BF16), vb)
            return carry

        lax.fori_loop(0, BLK // cr, chunk, 0)

    col, tab, sq, hv, g8, st, out = _ret_specs(nblk, False)
    return pl.pallas_call(
        body,
        out_shape=(jax.ShapeDtypeStruct((lp, 512), F32), jax.ShapeDtypeStruct((4, lp // cr, 128, 128), F32)),
        grid=(nblk,), in_specs=[col(C_RQ), col(C_RK), col(C_RV), tab, tab, sq, hv, hv, g8],
        out_specs=(out, st), scratch_shapes=[pltpu.VMEM((RET_HEADS, 128, 128), F32)],
        compiler_params=_cp("arbitrary"), name=name)(proj, proj, proj, *tables)


def _retention_bwd(proj, do, states, tables, *, name):
    lp = proj.shape[0]
    nblk, cr = lp // BLK, RET_CHUNK
    nch = BLK // cr
    scale = RET_DK ** -0.5

    def body(q_ref, k_ref, v_ref, do_ref, st_ref, c_ref, s_ref, d_ref, z_ref, x_ref, g_ref, dqkv_ref, dstate):
        @pl.when(pl.program_id(0) == 0)
        def _():
            dstate[...] = jnp.zeros_like(dstate)

        def chunk(cc, carry):
            ci = nch - 1 - cc
            sl = pl.ds(pl.multiple_of(ci * cr, cr), cr)
            c, s = c_ref[sl, :], s_ref[sl, :]
            for h in range(RET_HEADS):
                hs = slice(128 * h, 128 * h + 128)
                dmat, zeta, xi = d_ref[h], z_ref[h], x_ref[h]
                q = _rope(q_ref[sl, hs], c, s)
                k = _rope(k_ref[sl, hs], c, s) * scale
                qb, kb, vb = q.astype(BF16), k.astype(BF16), v_ref[sl, hs].astype(BF16)
                kzb = (k * zeta).astype(BF16)
                dov = do_ref[sl, hs]
                dob, doxb = dov.astype(BF16), (dov * xi).astype(BF16)
                stb = st_ref[h, ci].astype(BF16)
                dsn = dstate[h]
                dsnb = dsn.astype(BF16)
                scb = (_dot_nt(qb, kb) * dmat).astype(BF16)
                dscb = (_dot_nt(dob, vb) * dmat).astype(BF16)
                dq = _dot(dscb, kb) + _dot_nt(doxb, stb)
                dk = _dot_tn(dscb, qb) + _dot_nt(vb, dsnb) * zeta
                dv = _dot_tn(scb, dob) + _dot(kzb, dsnb)
                dstate[h] = dsn * g_ref[h][0:1, :] + _dot_tn(qb, doxb)
                dqkv_ref[sl, 128 * h:128 * h + 128] = _rope_t(dq, c, s).astype(BF16)
                dqkv_ref[sl, 512 + 128 * h:640 + 128 * h] = _rope_t(dk * scale, c, s).astype(BF16)
                dqkv_ref[sl, 1024 + 128 * h:1152 + 128 * h] = dv.astype(BF16)
            return carry

        lax.fori_loop(0, nch, chunk, 0)

    col, tab, sq, hv, g8, st, out = _ret_specs(nblk, True)
    return pl.pallas_call(
        body, out_shape=jax.ShapeDtypeStruct((lp, P_RET), BF16), grid=(nblk,),
        in_specs=[col(C_RQ), col(C_RK), col(C_RV), out, st, tab, tab, sq, hv, hv, g8],
        out_specs=pl.BlockSpec((BLK, P_RET), lambda i: (nblk - 1 - i, 0)),
        scratch_shapes=[pltpu.VMEM((RET_HEADS, 128, 128), F32)],
        compiler_params=_cp("arbitrary"), name=name)(proj, proj, proj, do, states, *tables)


def _gla_tables():
    c = GLA_CHUNK
    tri = np.tril(np.ones((c, c), np.float32))
    ones_qv = np.kron(np.eye(GLA_HEADS, dtype=np.float32), np.ones((GLA_DK, GLA_DV), np.float32))
    return (jnp.asarray(tri, BF16), jnp.asarray(tri.T.copy(), BF16), jnp.asarray(ones_qv, BF16),
            jnp.asarray(ones_qv.T.copy(), BF16))


def _tri_sum(tri, x):
    hi = x.astype(BF16)
    lo = (x - hi.astype(F32)).astype(BF16)
    return _dot(tri, hi) + _dot(tri, lo)


def _head_masks(width, per):
    lane = lax.broadcasted_iota(jnp.int32, (1, width), 1)
    return [((lane >= per * h) & (lane < per * (h + 1))).astype(F32) for h in range(GLA_HEADS)]


def _stack_heads(x, masks):
    return jnp.concatenate([x * m for m in masks], axis=0)


def _gla_gate(ga, w2, b, ok, tri):
    z = _dot(ga.astype(BF16), w2) + b
    la = (jnp.minimum(z, 0.0) - jnp.log(1.0 + jnp.exp(-jnp.abs(z)))) * (1.0 / GLA_TAU)
    la = jnp.where(ok, la, 0.0)
    return z, _tri_sum(tri, la)


def _gla_rows(i_blk, ci, lp):
    c = GLA_CHUNK
    rows = i_blk * BLK + ci * c + lax.broadcasted_iota(jnp.int32, (c, 1), 0)
    return (rows >= PADF) & (rows < lp - BACK)


N_SUB = GLA_CHUNK // GLA_SUB - 1
N_SUB2 = GLA_SUB // GLA_SUB2 - 1


def _gla_masks():
    c, s1, s2 = GLA_CHUNK, GLA_SUB, GLA_SUB2
    sh1, sh2 = s1.bit_length() - 1, s2.bit_length() - 1
    r = lax.broadcasted_iota(jnp.int32, (c, GLA_QK), 0)
    blk, within = jnp.right_shift(r, sh1), jnp.bitwise_and(r, s1 - 1)
    grp = jnp.right_shift(within, sh2)
    rowm = [(blk == a).astype(F32) for a in range(1, N_SUB + 1)] + [(grp == b).astype(F32) for b in range(1, N_SUB2 + 1)]
    keym = ([(r < s1 * a).astype(F32) for a in range(1, N_SUB + 1)]
            + [(within < s2 * b).astype(F32) for b in range(1, N_SUB2 + 1)])
    rs = lax.broadcasted_iota(jnp.int32, (GLA_HEADS * c, c), 0)
    ts = lax.broadcasted_iota(jnp.int32, (GLA_HEADS * c, c), 1)
    same = (jnp.right_shift(jnp.bitwise_and(rs, c - 1), sh1) == jnp.right_shift(ts, sh1)).astype(F32)
    lag = [(jnp.bitwise_and(r, s2 - 1) >= j).astype(F32) for j in range(s2)]
    return rowm, keym, same, lag


def _gla_hats(qs, k, g, masks, hm_q):
    c, s1, s2 = GLA_CHUNK, GLA_SUB, GLA_SUB2
    rowm, keym, same, _ = masks
    refs = [g[s1 * a - 1:s1 * a, :] for a in range(1, N_SUB + 1)]
    for b in range(1, N_SUB2 + 1):
        refs.append(jnp.concatenate([jnp.broadcast_to(g[s1 * i + s2 * b - 1:s1 * i + s2 * b, :], (s1, GLA_QK))
                                     for i in range(c // s1)], axis=0))
    eqs = [jnp.exp(jnp.minimum(g - r, 0.0)) * m for r, m in zip(refs, rowm)]
    eks = [jnp.exp(jnp.minimum(r - g, 0.0)) * m for r, m in zip(refs, keym)]
    qhs, khs = [qs * e for e in eqs], [k * e for e in eks]
    qst = [_stack_heads(q, hm_q).astype(BF16) for q in qhs]
    khb = [x.astype(BF16) for x in khs]
    qa, qb = jnp.concatenate(qst[:N_SUB], axis=1), jnp.concatenate(qst[N_SUB:], axis=1)
    ka, kb = jnp.concatenate(khb[:N_SUB], axis=1), jnp.concatenate(khb[N_SUB:], axis=1)
    p = _dot_nt(qa, ka) + _dot_nt(qb, kb) * same
    return eqs, eks, qhs, khs, qa, qb, ka, kb, p


def _roll_rows(x, j):
    return x if j == 0 else pltpu.roll(x, j, 0)


def _gla(proj, w2p, b, tables, *, name):
    lp = proj.shape[0]
    nblk, c, s2 = lp // BLK, GLA_CHUNK, GLA_SUB2
    nch = BLK // c

    def body(q_ref, k_ref, v_ref, a_ref, w_ref, b_ref, tri_ref, ones_ref, o_ref, st_ref, gz_ref, state):
        i_blk = pl.program_id(0)

        @pl.when(i_blk == 0)
        def _():
            state[...] = jnp.zeros_like(state)
        hm_q = _head_masks(GLA_QK, GLA_DK)
        masks = _gla_masks()
        tri, ones_qv, w2, bias = tri_ref[...], ones_ref[...], w_ref[...], b_ref[...]

        def chunk(ci, carry):
            sl = pl.ds(pl.multiple_of(ci * c, c), c)
            ok = _gla_rows(i_blk, ci, lp)
            k, v = k_ref[sl, :], v_ref[sl, :]
            vb = v.astype(BF16)
            qs = q_ref[sl, :] * (GLA_DK ** -0.5)
            z, g = _gla_gate(a_ref[sl, :], w2, bias, ok, tri)
            gz_ref[sl, 0:GLA_QK] = g
            gz_ref[sl, GLA_QK:2 * GLA_QK] = z
            last = g[c - 1:c, :]
            st = state[...]
            st_ref[ci] = st
            qst = _stack_heads(qs * jnp.exp(g), hm_q).astype(BF16)
            oi = _dot_nt(qst, st.astype(BF16))
            o = jnp.concatenate([oi[c * h:c * h + c, :] for h in range(GLA_HEADS)], axis=1)
            ke = k * jnp.exp(last - g)
            f = _dot_tn(vb, ke.astype(BF16))
            upd = f[0:GLA_DV, :] * hm_q[0]
            for h in range(1, GLA_HEADS):
                upd = upd + f[GLA_DV * h:GLA_DV * (h + 1), :] * hm_q[h]
            state[...] = st * jnp.exp(last) + upd
            p = _gla_hats(qs, k, g, masks, hm_q)[-1]
            ob = _dot(p.astype(BF16), vb)
            o = o + jnp.concatenate([ob[c * h:c * h + c, GLA_DV * h:GLA_DV * (h + 1)] for h in range(GLA_HEADS)],
                                    axis=1)
            ws = []
            for j in range(s2):
                ej = jnp.exp(jnp.minimum(g - _roll_rows(g, j), 0.0))
                ws.append((qs * _roll_rows(k, j) * ej * masks[3][j]).astype(BF16))
            ball = _dot(jnp.concatenate(ws, axis=0), ones_qv)
            for j in range(s2):
                o = o + ball[c * j:c * j + c, :] * _roll_rows(v, j)
            o_ref[sl, :] = o
            return carry

        lax.fori_loop(0, nch, chunk, 0)

    tri, _, ones_qv, _ = tables
    full = lambda arr: pl.BlockSpec(arr.shape, lambda i: (0,) * arr.ndim)
    return pl.pallas_call(
        body,
        out_shape=(jax.ShapeDtypeStruct((lp, GLA_V), F32), jax.ShapeDtypeStruct((lp // c, GLA_DV, GLA_QK), F32),
                   jax.ShapeDtypeStruct((lp, 2 * GLA_QK), F32)),
        grid=(nblk,),
        in_specs=[pl.BlockSpec((BLK, GLA_QK), lambda i: (i, C_GQ // GLA_QK)),
                  pl.BlockSpec((BLK, GLA_QK), lambda i: (i, C_GK // GLA_QK)),
                  pl.BlockSpec((BLK, GLA_V), lambda i: (i, C_GV // GLA_V)),
                  pl.BlockSpec((BLK, 128), lambda i: (i, C_GA // 128)),
                  full(w2p), full(b), full(tri), full(ones_qv)],
        out_specs=(pl.BlockSpec((BLK, GLA_V), lambda i: (i, 0)),
                   pl.BlockSpec((nch, GLA_DV, GLA_QK), lambda i: (i, 0, 0)),
                   pl.BlockSpec((BLK, 2 * GLA_QK), lambda i: (i, 0))),
        scratch_shapes=[pltpu.VMEM((GLA_DV, GLA_QK), F32)],
        compiler_params=_cp("arbitrary"), name=name)(proj, proj, proj, proj, w2p, b, tri, ones_qv)


def _gla_bwd(proj, do, states, gz, w2p, tables, *, name):
    lp = proj.shape[0]
    nblk, c, s1, s2 = lp // BLK, GLA_CHUNK, GLA_SUB, GLA_SUB2
    nch = BLK // c

    def body(q_ref, k_ref, v_ref, a_ref, do_ref, st_ref, gz_ref, w_ref, trit_ref, ones_ref, onest_ref,
             dp_ref, dw_ref, db_ref, dstate, dqs_s, dk_s, dg_s, dv_s):
        i_blk = nblk - 1 - pl.program_id(0)

        @pl.when(pl.program_id(0) == 0)
        def _():
            dstate[...] = jnp.zeros_like(dstate)
            dw_ref[...] = jnp.zeros_like(dw_ref)
            db_ref[...] = jnp.zeros_like(db_ref)
        hm_q = _head_masks(GLA_QK, GLA_DK)
        hm_v = _head_masks(GLA_V, GLA_DV)
        masks = _gla_masks()
        trit, ones_qv, ones_vq = trit_ref[...], ones_ref[...], onest_ref[...]
        w2 = w_ref[...]
        rsum = lambda x: jnp.sum(x, axis=0, keepdims=True)

        def chunk(cc, carry):
            ci = nch - 1 - cc
            sl = pl.ds(pl.multiple_of(ci * c, c), c)
            ok = _gla_rows(i_blk, ci, lp)
            k, v, ga = k_ref[sl, :], v_ref[sl, :], a_ref[sl, :]
            vb = v.astype(BF16)
            qs = q_ref[sl, :] * (GLA_DK ** -0.5)
            g, z = gz_ref[sl, 0:GLA_QK], gz_ref[sl, GLA_QK:2 * GLA_QK]
            last = g[c - 1:c, :]
            elast = jnp.exp(last)
            eg = jnp.exp(g)
            ekl = jnp.exp(last - g)
            qe, ke = qs * eg, k * ekl
            dov = do_ref[sl, :]
            st = st_ref[ci]
            dsn = dstate[...]
            qst = _stack_heads(qe, hm_q).astype(BF16)
            dost = jnp.concatenate([dov[:, GLA_DV * h:GLA_DV * (h + 1)] for h in range(GLA_HEADS)], axis=0).astype(BF16)
            dqe_st = _dot(dost, st.astype(BF16))
            dqe = dqe_st[0:c, :] * hm_q[0]
            for h in range(1, GLA_HEADS):
                dqe = dqe + dqe_st[c * h:c * h + c, :] * hm_q[h]
            dstate[...] = _dot_tn(dost, qst) + dsn * elast
            dlast = rsum(dsn * st) * elast
            df = _stack_heads(dsn, hm_q).astype(BF16)
            dv_s[...] = _dot_nt(ke.astype(BF16), df)
            dke = _dot(vb, df)
            xk = dke * ke
            dqs_s[...] = dqe * eg
            dk_s[...] = dke * ekl
            dg_s[...] = dqe * qe - xk
            dlast = dlast + rsum(xk)
            eqs, eks, qhs, khs, qa, qb, ka, kb, p = _gla_hats(qs, k, g, masks, hm_q)
            dost_v = _stack_heads(dov, hm_v).astype(BF16)
            dp = _dot_nt(dost_v, vb)
            dv_s[...] += _dot_tn(p.astype(BF16), dost_v)
            dpa, dpb = dp.astype(BF16), (dp * masks[2]).astype(BF16)
            dq_all = (_dot(dpa, ka), _dot(dpb, kb))
            dk_all = (_dot_tn(dpa, qa), _dot_tn(dpb, qb))
            for t in range(N_SUB + N_SUB2):
                lvl, i = (0, t) if t < N_SUB else (1, t - N_SUB)
                cols = slice(GLA_QK * i, GLA_QK * (i + 1))
                dq_st = dq_all[lvl][:, cols]
                dqh = dq_st[0:c, :] * hm_q[0]
                for h in range(1, GLA_HEADS):
                    dqh = dqh + dq_st[c * h:c * h + c, :] * hm_q[h]
                dkh = dk_all[lvl][:, cols]
                xq, xkh = dqh * qhs[t], dkh * khs[t]
                dqs_s[...] += dqh * eqs[t]
                dk_s[...] += dkh * eks[t]
                dg_s[...] += xq - xkh
                back_ref = xkh - xq
                if lvl == 0:
                    row = s1 * (i + 1) - 1
                    dg_s[row:row + 1, :] += rsum(back_ref)
                else:
                    for blk in range(c // s1):
                        row = s1 * blk + s2 * (i + 1) - 1
                        dg_s[row:row + 1, :] += rsum(back_ref[s1 * blk:s1 * blk + s1, :])
            kes, qes, ws, dbs = [], [], [], []
            for j in range(s2):
                em = jnp.exp(jnp.minimum(g - _roll_rows(g, j), 0.0)) * masks[3][j]
                kes.append(_roll_rows(k, j) * em)
                qes.append(qs * em)
                ws.append((qs * kes[j]).astype(BF16))
                dbs.append((dov * _roll_rows(v, j)).astype(BF16))
            ball = _dot(jnp.concatenate(ws, axis=0), ones_qv)
            dwall = _dot(jnp.concatenate(dbs, axis=0), ones_vq)
            for j in range(s2):
                back = (lambda x: x) if j == 0 else (lambda x, j=j: pltpu.roll(x, c - j, 0))
                dw = dwall[c * j:c * j + c, :]
                dv_s[...] += back(ball[c * j:c * j + c, :] * dov)
                dqs_s[...] += dw * kes[j]
                dk_s[...] += back(dw * qes[j])
                x = dw * qs * kes[j]
                dg_s[...] += x - back(x)
            dg_s[c - 1:c, :] += dlast
            dla = jnp.where(ok, _tri_sum(trit, dg_s[...]), 0.0)
            dz = dla * (1.0 / GLA_TAU) / (1.0 + jnp.exp(z))
            dzb = dz.astype(BF16)
            dp_ref[sl, 0:256] = (dqs_s[...] * (GLA_DK ** -0.5)).astype(BF16)
            dp_ref[sl, 256:512] = dk_s[...].astype(BF16)
            dp_ref[sl, 512:1024] = dv_s[...].astype(BF16)
            dp_ref[sl, 1024:1152] = _dot_nt(dzb, w2).astype(BF16)
            dp_ref[sl, 1152:1280] = jnp.zeros((c, 128), BF16)
            dw_ref[...] += _dot_tn(ga.astype(BF16), dzb)
            db_ref[...] += rsum(dz)
            return carry

        lax.fori_loop(0, nch, chunk, 0)

    tri, trit, ones_qv, ones_vq = tables
    full = lambda arr: pl.BlockSpec(arr.shape, lambda i: (0,) * arr.ndim)
    rev = lambda i: nblk - 1 - i
    return pl.pallas_call(
        body,
        out_shape=(jax.ShapeDtypeStruct((lp, P_GLA), BF16),
                   jax.ShapeDtypeStruct((128, GLA_QK), F32), jax.ShapeDtypeStruct((1, GLA_QK), F32)),
        grid=(nblk,),
        in_specs=[pl.BlockSpec((BLK, GLA_QK), lambda i: (rev(i), C_GQ // GLA_QK)),
                  pl.BlockSpec((BLK, GLA_QK), lambda i: (rev(i), C_GK // GLA_QK)),
                  pl.BlockSpec((BLK, GLA_V), lambda i: (rev(i), C_GV // GLA_V)),
                  pl.BlockSpec((BLK, 128), lambda i: (rev(i), C_GA // 128)),
                  pl.BlockSpec((BLK, GLA_V), lambda i: (rev(i), 0)),
                  pl.BlockSpec((nch, GLA_DV, GLA_QK), lambda i: (rev(i), 0, 0)),
                  pl.BlockSpec((BLK, 2 * GLA_QK), lambda i: (rev(i), 0)),
                  full(w2p), full(trit), full(ones_qv), full(ones_vq)],
        out_specs=(pl.BlockSpec((BLK, P_GLA), lambda i: (rev(i), 0)),
                   pl.BlockSpec((128, GLA_QK), lambda i: (0, 0)),
                   pl.BlockSpec((1, GLA_QK), lambda i: (0, 0))),
        scratch_shapes=[pltpu.VMEM((GLA_DV, GLA_QK), F32), pltpu.VMEM((c, GLA_QK), F32),
                        pltpu.VMEM((c, GLA_QK), F32), pltpu.VMEM((c, GLA_QK), F32), pltpu.VMEM((c, GLA_V), F32)],
        compiler_params=_cp("arbitrary"), name=name)(proj, proj, proj, proj, do, states, gz, w2p, trit, ones_qv, ones_vq)


def _as2d(a):
    return a.reshape(-1, a.shape[-1])


def _ew_tile(r):
    return _tile(r, (512, 256, 128, 64, 32, 16, 8))


def _add2(a, b, *, out_dtype, name):
    a2, b2 = _as2d(a), _as2d(b)
    r, n = a2.shape
    tm = _ew_tile(r)

    def body(a_ref, b_ref, o_ref):
        o_ref[...] = (a_ref[...] + b_ref[...]).astype(o_ref.dtype)

    blk = pl.BlockSpec((tm, n), lambda i: (i, 0))
    return pl.pallas_call(body, out_shape=jax.ShapeDtypeStruct((r, n), out_dtype), grid=(r // tm,), in_specs=[blk, blk],
                          out_specs=blk, compiler_params=_cp("parallel"), name=name)(a2, b2).reshape(a.shape)


def _sum_slots(own, q, *, name):
    shape = own.shape
    q3 = q.reshape(3, -1, shape[-1])
    own2 = _as2d(own)
    r, n = own2.shape
    tm = _ew_tile(r)

    def body(own_ref, q_ref, o_ref):
        f = lambda i: q_ref[i].astype(F32)
        o_ref[...] = ((own_ref[...].astype(F32) + f(0)) + f(1)) + f(2)

    blk = pl.BlockSpec((tm, n), lambda i: (i, 0))
    return pl.pallas_call(
        body, out_shape=jax.ShapeDtypeStruct((r, n), F32), grid=(r // tm,),
        in_specs=[blk, pl.BlockSpec((3, tm, n), lambda i: (0, i, 0))], out_specs=blk,
        compiler_params=_cp("parallel"), name=name)(own2, q3).reshape(shape)


def _adamw(w, g, m, v, *, name):
    shape = w.shape
    w2, g2, m2, v2 = _as2d(w), _as2d(g), _as2d(m), _as2d(v)
    r, n = w2.shape
    tm = _ew_tile(r)

    def body(w_ref, g_ref, m_ref, v_ref, d_ref, mo_ref, vo_ref):
        d_ref[...], mo_ref[...], vo_ref[...] = _adam_math(w_ref[...], g_ref[...], m_ref[...], v_ref[...])

    blk = pl.BlockSpec((tm, n), lambda i: (i, 0))
    o = jax.ShapeDtypeStruct((r, n), F32)
    d, mo, vo = pl.pallas_call(body, out_shape=(o, o, o), grid=(r // tm,), in_specs=[blk] * 4, out_specs=(blk,) * 3,
                               compiler_params=_cp("parallel"), name=name)(w2, g2, m2, v2)
    return d.reshape(shape), mo.reshape(shape), vo.reshape(shape)


def _adam_math(w, gv, m, v):
    c1 = 1.0 - ADAM_B1 ** ADAM_STEP
    c2 = 1.0 - ADAM_B2 ** ADAM_STEP
    mn = ADAM_B1 * m + (1.0 - ADAM_B1) * gv
    vn = ADAM_B2 * v + (1.0 - ADAM_B2) * (gv * gv)
    return -ADAM_LR * ((mn / c1) / (jnp.sqrt(vn / c2) + ADAM_EPS) + ADAM_WD * w), mn, vn


def _adamw_halves(w, m, v, mine, theirs, c, *, name):
    depth, rows, n = w.shape
    r2 = rows // 2
    tm = next(t for t in range(min(r2, 256), 0, -8) if r2 % t == 0)
    steps = r2 // tm

    def body(c_ref, w_ref, m_ref, v_ref, *rest):
        halves, (g_ref, d_ref, mo_ref, vo_ref) = rest[:2 * depth], rest[2 * depth:]
        l, h = pl.program_id(0), pl.program_id(1)
        gv = None
        for k in range(depth):
            gk = jnp.where(h == c_ref[0], halves[2 * k][...], halves[2 * k + 1][...])
            gv = gk if gv is None else jnp.where(l == k, gk, gv)
        g_ref[...] = gv
        d_ref[...], mo_ref[...], vo_ref[...] = _adam_math(w_ref[...], gv, m_ref[...], v_ref[...])

    big = pl.BlockSpec((tm, n), lambda l, h, i, c_ref: ((2 * l + h) * steps + i, 0))
    half = lambda k: pl.BlockSpec((tm, n), lambda l, h, i, c_ref: (jnp.where(l == k, i, 0), 0))
    o = jax.ShapeDtypeStruct((depth * rows, n), F32)
    args = [a for k in range(depth) for a in (mine[k], theirs[k])]
    outs = pl.pallas_call(
        body, out_shape=(o, o, o, o),
        grid_spec=pltpu.PrefetchScalarGridSpec(
            num_scalar_prefetch=1, grid=(depth, 2, steps),
            in_specs=[big, big, big] + [half(k) for k in range(depth) for _ in range(2)], out_specs=(big,) * 4),
        compiler_params=_cp("arbitrary", "arbitrary", "arbitrary"), name=name)(
            jnp.reshape(c, (1,)).astype(jnp.int32), _as2d(w), _as2d(m), _as2d(v), *args)
    return [a.reshape(w.shape) for a in outs]


ANY = pl.BlockSpec(memory_space=pl.ANY)


def _place():
    return lax.axis_index("x"), lax.axis_index("y"), lax.axis_index("c")


def _other_chips(x, y):
    return [(1 - x, y), (x, 1 - y), (1 - x, 1 - y)]


def _remote(src, dst, ssem, rsem, dev):
    return pltpu.make_async_remote_copy(src_ref=src, dst_ref=dst, send_sem=ssem, recv_sem=rsem, device_id=dev,
                                        device_id_type=MESH)


def _allgather_chips(arrs, *, name):
    n = len(arrs)

    def body(*refs):
        ins, outs = refs[:n], refs[n:2 * n]
        s1, r1, s2, r2 = refs[2 * n:]
        x, y, c = _place()
        q = 2 * x + y
        chips = _other_chips(x, y)
        qs = [2 * cx + cy for cx, cy in chips]
        sib = (x, y, 1 - c)
        first, passed = [], []
        for k in range(n):
            for j, chip in enumerate(chips):
                first.append(_remote(ins[k].at[c], outs[k].at[c, q], s1.at[k, j], r1.at[k, j], (*chip, c)))
        for cp in first:
            cp.start()
        for k in range(n):
            for j, chip in enumerate(chips):
                land = outs[k].at[c, qs[j]]
                _remote(land, land, s1.at[k, j], r1.at[k, j], (*chip, c)).wait_recv()
                fw = _remote(land, land, s2.at[k, j], r2.at[k, j], sib)
                fw.start()
                passed.append(fw)
        for k in range(n):
            for j in range(3):
                land = outs[k].at[1 - c, qs[j]]
                _remote(land, land, s2.at[k, j], r2.at[k, j], sib).wait_recv()
        for cp in first + passed:
            cp.wait_send()

    sem = pltpu.SemaphoreType.DMA
    outs = pl.pallas_call(
        body, out_shape=tuple(jax.ShapeDtypeStruct((2, 4) + a.shape[1:], a.dtype) for a in arrs),
        in_specs=[ANY] * n, out_specs=(ANY,) * n,
        scratch_shapes=[sem((n, 3)), sem((n, 3)), sem((n, 3)), sem((n, 3))], name=name)(*arrs)
    chip = 2 * lax.axis_index("x") + lax.axis_index("y")
    return [lax.dynamic_update_slice_in_dim(o, a[:, None], chip, axis=1) for o, a in zip(outs, arrs)]


def _pair_exchange(arrs, *, name):
    n = len(arrs)

    def body(*refs):
        ins, outs = refs[:n], refs[n:2 * n]
        ssem, rsem = refs[2 * n:]
        x, y, c = _place()
        cps = [_remote(ins[k].at[:, 1 - c], outs[k], ssem.at[k], rsem.at[k], (x, y, 1 - c)) for k in range(n)]
        for cp in cps:
            cp.start()
        for cp in cps:
            cp.wait()

    sem = pltpu.SemaphoreType.DMA
    return pl.pallas_call(
        body, out_shape=tuple(jax.ShapeDtypeStruct((a.shape[0],) + a.shape[2:], a.dtype) for a in arrs),
        in_specs=[ANY] * n, out_specs=(ANY,) * n, scratch_shapes=[sem((n,)), sem((n,))], name=name)(*arrs)


def _pair_sum(mine, theirs, c, *, name):
    _, _, r, n = mine.shape
    tm = r if r <= 512 else _ew_tile(r)

    def body(c_ref, a_ref, b_ref, o_ref):
        o_ref[...] = (a_ref[...] + b_ref[...]).astype(BF16)

    blk = pl.BlockSpec((None, tm, n), lambda s, i, c_ref: (s, i, 0))
    return pl.pallas_call(
        body, out_shape=jax.ShapeDtypeStruct((4, r, n), BF16),
        grid_spec=pltpu.PrefetchScalarGridSpec(
            num_scalar_prefetch=1, grid=(4, r // tm),
            in_specs=[pl.BlockSpec((None, None, tm, n), lambda s, i, c_ref: (s, c_ref[0], i, 0)), blk], out_specs=blk),
        compiler_params=_cp("parallel", "parallel"), name=name)(jnp.reshape(c, (1,)).astype(jnp.int32), mine, theirs)


def _chip_copies(ins, outs, ssem, rsem, mode):
    x, y, c = _place()
    q = 2 * x + y
    sends, recvs = [], []
    for k in range(len(ins)):
        for j, (cx, cy) in enumerate(_other_chips(x, y)):
            sem = (ssem.at[k, j], rsem.at[k, j], (cx, cy, c))
            if mode == "scatter":
                sends.append(_remote(ins[k].at[2 * cx + cy], outs[k].at[j], *sem))
                recvs.append(sends[-1])
            else:
                sends.append(_remote(ins[k].at[c], outs[k].at[2 * q + c], *sem))
                recvs.append(_remote(ins[k].at[c], outs[k].at[2 * (2 * cx + cy) + c], *sem))
    return sends, recvs


def _chip_wait(sends, recvs):
    for cp in sends:
        cp.wait_send()
    for cp in recvs:
        cp.wait_recv()


def _landing_shape(a, mode):
    return jax.ShapeDtypeStruct(((3,) if mode == "scatter" else (8,)) + a.shape[1:], a.dtype)


def _chip_exchange(arrs, mode, *, name):
    n = len(arrs)

    def body(*refs):
        ins, outs = refs[:n], refs[n:2 * n]
        ssem, rsem = refs[2 * n:]
        sends, recvs = _chip_copies(ins, outs, ssem, rsem, mode)
        for cp in sends:
            cp.start()
        _chip_wait(sends, recvs)

    sem = pltpu.SemaphoreType.DMA
    return list(pl.pallas_call(
        body, out_shape=tuple(_landing_shape(a, mode) for a in arrs),
        in_specs=[ANY] * n, out_specs=(ANY,) * n, scratch_shapes=[sem((n, 3)), sem((n, 3))], name=name)(*arrs))


def _pair_fill(bufs, owns, *, name):
    n = len(bufs)

    def body(*refs):
        own, outs = refs[n:2 * n], refs[2 * n:3 * n]
        ssem, rsem = refs[3 * n:]
        x, y, c = _place()
        q = 2 * x + y
        sib = (x, y, 1 - c)
        sends, recvs = [], []
        for k in range(n):
            for j, (cx, cy) in enumerate(_other_chips(x, y)):
                mine, theirs = outs[k].at[2 * (2 * cx + cy) + c], outs[k].at[2 * (2 * cx + cy) + 1 - c]
                sends.append(_remote(mine, mine, ssem.at[k, j], rsem.at[k, j], sib))
                recvs.append(_remote(mine, theirs, ssem.at[k, j], rsem.at[k, j], sib))
            slots = outs[k].at[pl.ds(2 * q, 2)]
            sends.append(_remote(own[k], slots, ssem.at[k, 3], rsem.at[k, 3], sib))
            recvs.append(sends[-1])
        for cp in sends:
            cp.start()
        _chip_wait(sends, recvs)

    sem = pltpu.SemaphoreType.DMA
    return list(pl.pallas_call(
        body, out_shape=tuple(jax.ShapeDtypeStruct(b.shape, b.dtype) for b in bufs),
        in_specs=[ANY] * (2 * n), out_specs=(ANY,) * n, scratch_shapes=[sem((n, 4)), sem((n, 4))],
        input_output_aliases={k: k for k in range(n)}, name=name)(*bufs, *owns))


def _pair_swap(arrs, *, name):
    n = len(arrs)

    def body(*refs):
        ins, outs = refs[:n], refs[n:2 * n]
        ssem, rsem = refs[2 * n:]
        x, y, c = _place()
        cps = [_remote(ins[k], outs[k], ssem.at[k], rsem.at[k], (x, y, 1 - c)) for k in range(n)]
        for cp in cps:
            cp.start()
        for cp in cps:
            cp.wait()

    sem = pltpu.SemaphoreType.DMA
    return pl.pallas_call(
        body, out_shape=tuple(jax.ShapeDtypeStruct(a.shape, a.dtype) for a in arrs),
        in_specs=[ANY] * n, out_specs=(ANY,) * n, scratch_shapes=[sem((n,)), sem((n,))], name=name)(*arrs)


def _allreduce_small(slab, *, name):
    r, n = slab.shape

    def body(x_ref, o_ref, buf, ssem, rsem):
        x, y, c = _place()
        me = 4 * x + 2 * y + c
        buf[me] = x_ref[...]
        cps = []
        for rel in range(1, 8):
            bx, by, bc = (rel >> 2) & 1, (rel >> 1) & 1, rel & 1
            px, py, pc = (x + bx) % 2, (y + by) % 2, (c + bc) % 2
            cps.append((_remote(x_ref, buf.at[me], ssem.at[rel - 1], rsem.at[rel - 1], (px, py, pc)),
                        4 * px + 2 * py + pc, (px, py, pc)))
        for cp, _, _ in cps:
            cp.start()
        for rel, (cp, peer, dev) in enumerate(cps):
            cp.wait_send()
            _remote(x_ref, buf.at[peer], ssem.at[rel], rsem.at[rel], dev).wait_recv()
        acc = buf[0]
        for k in range(1, 8):
            acc = acc + buf[k]
        o_ref[...] = acc

    vm = pl.BlockSpec(memory_space=pltpu.VMEM)
    sem = pltpu.SemaphoreType.DMA
    return pl.pallas_call(
        body, out_shape=jax.ShapeDtypeStruct((r, n), F32), in_specs=[vm], out_specs=vm,
        scratch_shapes=[pltpu.VMEM((8, r, n), F32), sem((7,)), sem((7,))], name=name)(slab)


def _slab(arrs, row_mult):
    flat = jnp.concatenate([a.reshape(-1) for a in arrs])
    unit = 128 * row_mult
    total = -(-flat.size // unit) * unit
    return jnp.pad(flat, (0, total - flat.size)).reshape(-1, 128)


def _unslab(slab, shapes):
    flat = slab.reshape(-1)
    out, off = [], 0
    for s in shapes:
        size = int(np.prod(s))
        out.append(flat[off:off + size].reshape(s))
        off += size
    return out


def _cols_from_chips(a):
    return jnp.transpose(a, (1, 0, 2)).reshape(a.shape[1], -1)


def _cols_to_chips(a, parts):
    r = a.shape[0]
    return jnp.transpose(a.reshape(r, parts, -1), (1, 0, 2))


BIG = ("w_in", "w_out", "up", "down")
GATHER_RIDES = {("proj", 0): (("w_out", 0), ("up", 0)), ("mix_out", 0): (("down", 0),),
                ("ffn_fwd", 0): (("w_in", 1), ("w_out", 1), ("up", 1), ("down", 1))}
REDUCE_RIDES = {("ffn_up_a_dw", 0): (("up",), 1), ("ffn_down_dw", 0): (("w_in", "w_out"), 1),
                ("mix_out_dx", 0): (("down",), 1),
                ("proj_dx", 0): (("up",), 0), ("proj_dw_0", 0): (("down",), 0), ("proj_dw_1", 0): (("w_out",), 0)}


class _LocalWeights:
    def __init__(self, meta, win, wout, up_a, up_g, down, w2p, cw):
        self._meta, self._w = meta, {"win": win, "wout": wout, "up_a": up_a, "up_g": up_g, "down": down, "w2p": w2p,
                                     "cw": cw}

    def meta(self):
        return self._meta

    def get(self, kind, l):
        return self._w[kind][l]

    def mm(self, site, l, a, b, fn=None, **kw):
        return (fn or _mm)(a, b, name=site, **kw)

    def ffn_fwd(self, l, h, m, w_post, w_next, wa, wg, ba, bg):
        return _ffn_fwd(h, m, w_post, w_next, self.get("up_a", l), self.get("up_g", l), wa, wg, ba, bg,
                        self.get("down", l), name="ffn_fwd")[0]

    def merge_mix(self, l, o_ret, o_gla, proj, w_ret, w_gla):
        return _merge_mix_out(o_ret, o_gla, proj, w_ret, w_gla, self.get("wout", l), name="mix_out")[0]

    def merge_bwd(self, l, dm, o_ret, o_gla, proj, w_ret, w_gla):
        return _merge_bwd(dm, self.get("wout", l), o_ret, o_gla, proj, w_ret, w_gla, name="mix_out_dx")[0]

    def grads_done(self, l, g, kinds):
        pass


class _ChipWeights:
    def __init__(self, w_in, w_out, ffn_up, ffn_down, meta_tokens, gla_gate_w2, ffn_conv_w):
        self.x, self.y, self.c = _place()
        self.q = 2 * self.x + self.y
        halves = lambda a: a.astype(BF16).reshape(2, a.shape[0] // 2, a.shape[1])
        self.own = {(k, l): halves(a[l]) for k, a in zip(BIG, (w_in, w_out, ffn_up, ffn_down)) for l in range(DEPTH)}
        self.landed, self.swapped, self.full, self.n_swaps = {}, {}, {}, 0
        self.sh_shapes = [meta_tokens.shape, gla_gate_w2.shape, ffn_conv_w.shape]
        self.own["small", 0] = _slab([meta_tokens, gla_gate_w2, ffn_conv_w], 16).reshape(2, -1, 128)
        first = [("w_in", 0), ("small", 0)]
        for key, arr in zip(first, _chip_exchange([self.own[k] for k in first], "bcast", name="gather_first")):
            self.landed[key] = arr
        sh = self._whole("small", 0).reshape(4, -1, 128)
        parts = [_unslab(sh[k], self.sh_shapes) for k in range(4)]
        self._meta = jnp.concatenate([p[0] for p in parts], axis=-1)
        self.w2 = jnp.concatenate([p[1] for p in parts], axis=-1)
        self.cw = jnp.concatenate([p[2] for p in parts], axis=-1)
        self.partial, self.slots = {}, {}

    def _whole(self, kind, l):
        if (kind, l) not in self.full:
            keys = [k for k in self.landed if k not in self.full]
            got = _pair_fill([self.landed[k] for k in keys], [self.own[k] for k in keys],
                             name=f"gather_fill_{self.n_swaps}")
            self.n_swaps += 1
            for k, buf in zip(keys, got):
                self.full[k] = buf.reshape(4, 2 * buf.shape[1], buf.shape[2])
        return self.full[kind, l]

    def meta(self):
        return self._meta

    def get(self, kind, l):
        if kind == "win":
            return _to_kernel_cols(_cols_from_chips(self._whole("w_in", l)))
        if kind == "wout":
            return self._whole("w_out", l).reshape(D_MODEL, D_MODEL)
        if kind == "up_a":
            return _cols_from_chips(self._whole("up", l)[0:2])
        if kind == "up_g":
            return _cols_from_chips(self._whole("up", l)[2:4])
        if kind == "down":
            return self._whole("down", l).reshape(D_FF, D_MODEL)
        if kind == "w2p":
            return jnp.pad(self.w2[l], ((0, 128 - GLA_RANK), (0, 0))).astype(BF16)
        return self.cw[l]

    def mm(self, site, l, a, b, fn=None, **kw):
        fn = fn or _mm
        if (site, l) in GATHER_RIDES:
            keys = GATHER_RIDES[site, l]
            out, got = fn(a, b, name=site, carry=([self.own[k] for k in keys], "bcast"), **kw)
            self.landed.update(zip(keys, got))
            return out
        if (site, l) in REDUCE_RIDES:
            kinds, gl = REDUCE_RIDES[site, l]
            keys = [(k, gl) for k in kinds]
            if all(k in self.partial and k not in self.slots for k in keys):
                out, got = fn(a, b, name=site, carry=([self.partial[k] for k in keys], "scatter"), **kw)
                self.slots.update(zip(keys, got))
                return out
        return fn(a, b, name=site, **kw)

    def merge_bwd(self, l, dm, o_ret, o_gla, proj, w_ret, w_gla):
        carry, keys = None, []
        if ("mix_out_dx", l) in REDUCE_RIDES:
            kinds, gl = REDUCE_RIDES["mix_out_dx", l]
            keys = [(k, gl) for k in kinds]
            if all(k in self.partial and k not in self.slots for k in keys):
                carry = ([self.partial[k] for k in keys], "scatter")
        outs, got = _merge_bwd(dm, self.get("wout", l), o_ret, o_gla, proj, w_ret, w_gla, name="mix_out_dx",
                               carry=carry)
        if carry is not None:
            self.slots.update(zip(keys, got))
        return outs

    def merge_mix(self, l, o_ret, o_gla, proj, w_ret, w_gla):
        keys = GATHER_RIDES.get(("mix_out", l), ())
        outs, got = _merge_mix_out(o_ret, o_gla, proj, w_ret, w_gla, self.get("wout", l), name="mix_out",
                                   carry=([self.own[k] for k in keys], "bcast") if keys else None)
        self.landed.update(zip(keys, got))
        return outs

    def ffn_fwd(self, l, h, m, w_post, w_next, wa, wg, ba, bg):
        keys = GATHER_RIDES.get(("ffn_fwd", l), ())
        outs, got = _ffn_fwd(h, m, w_post, w_next, self.get("up_a", l), self.get("up_g", l), wa, wg, ba, bg,
                             self.get("down", l), name="ffn_fwd",
                             carry=([self.own[k] for k in keys], "bcast") if keys else None)
        self.landed.update(zip(keys, got))
        return outs

    def grads_done(self, l, g, kinds):
        split = lambda a: a.reshape(4, 2, a.shape[-2] // 2, a.shape[-1]) if a.ndim == 3 else \
            a.reshape(4, 2, a.shape[0] // 8, a.shape[1])
        src = {"w_in": lambda: g["w_in"][l], "w_out": lambda: g["w_out"][l],
               "up": lambda: g["up"][l], "down": lambda: g["down"][l]}
        big = {k: split(src[k]()) for k in kinds}
        from_sib = _pair_exchange([big[k] for k in kinds], name=f"grads_pair_exchange_{l}_{kinds[0]}")
        for k, theirs in zip(kinds, from_sib):
            self.partial[k, l] = _pair_sum(big[k], theirs, self.c, name=f"pair_sum_{k}_{l}")

    def reduce(self):
        keys = [(k, l) for l in range(DEPTH) for k in BIG]
        late = [k for k in keys if k not in self.slots]
        self.slots.update(zip(late, _chip_exchange([self.partial[k] for k in late], "scatter",
                                                   name="grads_chip_exchange")))
        half = {}
        for k in keys:
            own = lax.dynamic_index_in_dim(self.partial[k], self.q, 0, keepdims=False)
            half[k] = _sum_slots(own, self.slots[k], name=f"chip_sum_{k[0]}_{k[1]}")
        other = dict(zip(keys, _pair_swap([half[k] for k in keys], name="grads_pair_swap")))
        return [([half[k, l] for l in range(DEPTH)], [other[k, l] for l in range(DEPTH)]) for k in BIG]


def _local_step(x_rows, target_rows, wts, pre_mix_norm, gla_gate_b, ret_norm_w, gla_norm_w, post_mix_norm,
                pre_ffn_norm, ffn_conv_b, post_ffn_norm):
    d = D_MODEL
    lp = x_rows.shape[0] + FRONT + BACK
    row = lambda a, l: a[l][None, :]
    rtab = _ret_tables(lp)
    gtab = _gla_tables()
    h0 = jnp.concatenate([jnp.zeros((PADF, d), F32), wts.meta(), x_rows, jnp.zeros((BACK, d), F32)], axis=0)
    target = jnp.pad(target_rows, ((FRONT, BACK), (0, 0)))

    saved = []
    h = h0
    _, hn = _resid_norm(h0, None, None, row(pre_mix_norm, 0), name="norm_in")
    loss_local = dy = None
    for l in range(DEPTH):
        s = {"h_in": h, "hn": hn}
        s["proj"] = wts.mm("proj", l, hn, wts.get("win", l))
        s["o_ret"], s["st_ret"] = _retention(s["proj"], rtab, name="retention")
        s["o_gla"], s["st_gla"], s["gz"] = _gla(s["proj"], wts.get("w2p", l), row(gla_gate_b, l), gtab, name="gla")
        s["merged"], s["m"] = wts.merge_mix(l, s["o_ret"], s["o_gla"], s["proj"], row(ret_norm_w, l),
                                            row(gla_norm_w, l))
        cw_a, cw_g = wts.get("cw", l)[:, :D_FF], wts.get("cw", l)[:, D_FF:]
        cb_a, cb_g = ffn_conv_b[l][None, :D_FF], ffn_conv_b[l][None, D_FF:]
        s["conv"] = (cw_a, cw_g, cb_a, cb_g)
        s["h_mid"], s["hn2"], s["ua"], s["ug"], s["act"], s["f"] = wts.ffn_fwd(
            l, h, s["m"], row(post_mix_norm, l), row(pre_ffn_norm, l), cw_a, cw_g, cb_a, cb_g)
        if l + 1 < DEPTH:
            h, hn = _resid_norm(s["h_mid"], s["f"], row(post_ffn_norm, l), row(pre_mix_norm, l + 1), name="resid_ffn")
        else:
            loss_local, dy, df_last, dw_last = _loss_head(s["h_mid"], s["f"], row(post_ffn_norm, l), target,
                                                          name="loss_head")
        saved.append(s)

    g = {k: [None] * DEPTH for k in ("pre_mix", "w_in", "w2", "gb", "ret_n", "gla_n", "w_out", "post_mix", "pre_ffn",
                                     "up", "cw", "cb", "down", "post_ffn")}
    dh_out, dhn_next = dy, None
    for l in reversed(range(DEPTH)):
        s = saved[l]
        cw_a, cw_g, cb_a, cb_g = s["conv"]
        if l + 1 < DEPTH:
            dh, df, g["pre_mix"][l + 1], g["post_ffn"][l] = _resid_norm_bwd(
                dh_out, dhn_next, saved[l + 1]["h_in"], s["f"], row(pre_mix_norm, l + 1), row(post_ffn_norm, l),
                name="resid_ffn_bwd")
        else:
            dh, df, g["post_ffn"][l] = dh_out, df_last, dw_last
        g["down"][l] = wts.mm("ffn_down_dw", l, s["act"], df, fn=_mm_tn, tn=512)
        du_a, du_g, dcw_a, dcw_g, dcb_a, dcb_g, dhn2 = _conv_act_bwd(
            s["ua"], s["ug"], df, wts.get("down", l), cw_a, cw_g, cb_a, cb_g, wts.get("up_a", l), wts.get("up_g", l),
            name="conv_act_bwd")
        g["cw"][l] = jnp.concatenate([dcw_a, dcw_g], axis=1)
        g["cb"][l] = jnp.concatenate([dcb_a, dcb_g], axis=1)[0]
        half_up = wts.mm("ffn_up_a_dw", l, s["hn2"], du_a, fn=_mm_tn, tn=D_FF // 2, blocks=(4, 0))
        g["up"][l] = _mm_tn(s["hn2"], du_g, tn=D_FF // 2, blocks=(4, 2), into=half_up, name="ffn_up_g_dw")
        dh, dm, g["pre_ffn"][l], g["post_mix"][l] = _resid_norm_bwd(
            dh, dhn2, s["h_mid"], s["m"], row(pre_ffn_norm, l), row(post_mix_norm, l), name="resid_mix_bwd")
        g["w_out"][l] = _mm_tn(s["merged"], dm, name="mix_out_dw")
        wts.grads_done(l, g, ("w_out", "up", "down"))
        do_ret, do_gla, d_gate, g["ret_n"][l], g["gla_n"][l] = wts.merge_bwd(
            l, dm, s["o_ret"], s["o_gla"], s["proj"], row(ret_norm_w, l), row(gla_norm_w, l))
        d_ret = _retention_bwd(s["proj"], do_ret, s["st_ret"], rtab, name="retention_bwd")
        d_gla, dw2, dgb = _gla_bwd(s["proj"], do_gla, s["st_gla"], s["gz"], wts.get("w2p", l), gtab, name="gla_bwd")
        g["w2"][l], g["gb"][l] = dw2[:GLA_RANK], dgb[0]
        pieces = (d_ret, d_gate, d_gla)
        g["w_in"][l] = _to_reference_chips(*[wts.mm(f"proj_dw_{i}", l, s["hn"], p, fn=_mm_tn)
                                             for i, p in enumerate(pieces)])
        win = wts.get("win", l)
        dhn_next = wts.mm("proj_dx", l, pieces, [win[:, 0:P_RET], win[:, P_RET:P_RET + P_GATE], win[:, P_RET + P_GATE:]],
                          fn=_mm_nt_sum)
        dh_out = dh
        wts.grads_done(l, g, ("w_in",))
    dh0, _, g["pre_mix"][0], _ = _resid_norm_bwd(dh_out, dhn_next, h0, None, row(pre_mix_norm, 0), None,
                                                 name="norm_in_bwd")
    return loss_local, dh0, g


def kernel(x, meta_tokens, pre_mix_norm, w_in, gla_gate_w2, gla_gate_b, ret_norm_w, gla_norm_w, w_out, post_mix_norm, pre_ffn_norm, ffn_up, ffn_conv_w, ffn_conv_b, ffn_down, post_ffn_norm, loss_target, m_meta_tokens, m_pre_mix_norm, m_w_in, m_gla_gate_w2, m_gla_gate_b, m_ret_norm_w, m_gla_norm_w, m_w_out, m_post_mix_norm, m_pre_ffn_norm, m_ffn_up, m_ffn_conv_w, m_ffn_conv_b, m_ffn_down, m_post_ffn_norm, v_meta_tokens, v_pre_mix_norm, v_w_in, v_gla_gate_w2, v_gla_gate_b, v_ret_norm_w, v_gla_norm_w, v_w_out, v_post_mix_norm, v_pre_ffn_norm, v_ffn_up, v_ffn_conv_w, v_ffn_conv_b, v_ffn_down, v_post_ffn_norm):
    xi, yi, ci = _place()
    chip = 2 * xi + yi
    seq = x.shape[1]
    d = D_MODEL
    wts = _ChipWeights(w_in, w_out, ffn_up, ffn_down, meta_tokens, gla_gate_w2, ffn_conv_w)
    loss_local, dh0, g = _local_step(x[0], loss_target[0], wts, pre_mix_norm, gla_gate_b, ret_norm_w, gla_norm_w,
                                     post_mix_norm, pre_ffn_norm, ffn_conv_b, post_ffn_norm)
    grad_x = dh0[FRONT:FRONT + seq][None]
    names = ("w_in", "w_out", "ffn_up", "ffn_down")
    big_halves = wts.reduce()

    small_full = [dh0[PADF:FRONT], jnp.stack(g["pre_mix"])[:, 0], jnp.stack(g["w2"]), jnp.stack(g["gb"]),
                  jnp.stack(g["ret_n"])[:, 0], jnp.stack(g["gla_n"])[:, 0], jnp.stack(g["post_mix"])[:, 0],
                  jnp.stack(g["pre_ffn"])[:, 0], jnp.stack(g["cw"]), jnp.stack(g["cb"]),
                  jnp.stack(g["post_ffn"])[:, 0]]
    small_sum = _unslab(_allreduce_small(_slab(small_full, 8), name="small_allreduce"), [a.shape for a in small_full])
    (g_meta, g_pre_mix, g_w2, g_gb, g_ret_n, g_gla_n, g_post_mix, g_pre_ffn, g_cw, g_cb, g_post_ffn) = small_sum
    g_meta = lax.dynamic_slice_in_dim(g_meta, chip * 256, 256, axis=1)
    g_w2 = lax.dynamic_slice_in_dim(g_w2, chip * 64, 64, axis=2)
    g_cw = lax.dynamic_slice_in_dim(g_cw, chip * 1408, 1408, axis=2)

    grads = [g_meta, g_pre_mix, None, g_w2, g_gb, g_ret_n, g_gla_n, None, g_post_mix, g_pre_ffn, None,
             g_cw, g_cb, None, g_post_ffn]
    ws = [meta_tokens, pre_mix_norm, w_in, gla_gate_w2, gla_gate_b, ret_norm_w, gla_norm_w, w_out, post_mix_norm,
          pre_ffn_norm, ffn_up, ffn_conv_w, ffn_conv_b, ffn_down, post_ffn_norm]
    ms = [m_meta_tokens, m_pre_mix_norm, m_w_in, m_gla_gate_w2, m_gla_gate_b, m_ret_norm_w, m_gla_norm_w, m_w_out,
          m_post_mix_norm, m_pre_ffn_norm, m_ffn_up, m_ffn_conv_w, m_ffn_conv_b, m_ffn_down, m_post_ffn_norm]
    vs = [v_meta_tokens, v_pre_mix_norm, v_w_in, v_gla_gate_w2, v_gla_gate_b, v_ret_norm_w, v_gla_norm_w, v_w_out,
          v_post_mix_norm, v_pre_ffn_norm, v_ffn_up, v_ffn_conv_w, v_ffn_conv_b, v_ffn_down, v_post_ffn_norm]
    big_idx = (2, 7, 10, 13)
    deltas, new_m, new_v = [None] * 15, [None] * 15, [None] * 15
    for i, nm, (mine, theirs) in zip(big_idx, names, big_halves):
        grads[i], deltas[i], new_m[i], new_v[i] = _adamw_halves(ws[i], ms[i], vs[i], mine, theirs, ci,
                                                                name=f"adamw_{nm}")
    small_idx = [i for i in range(15) if i not in big_idx]
    shapes = [ws[i].shape for i in small_idx]
    sd, sm, sv = _adamw(_slab([ws[i] for i in small_idx], 8), _slab([grads[i] for i in small_idx], 8),
                        _slab([ms[i] for i in small_idx], 8), _slab([vs[i] for i in small_idx], 8), name="adamw_small")
    for i, a, b, c_ in zip(small_idx, _unslab(sd, shapes), _unslab(sm, shapes), _unslab(sv, shapes)):
        deltas[i], new_m[i], new_v[i] = a, b, c_

    loss = lax.psum(loss_local, ("x", "y", "c"))
    return (loss, grad_x, *grads, *deltas, *new_m, *new_v)
```

```python
import functools
import math

import numpy as np
import jax
import jax.numpy as jnp
from jax import lax
from jax.experimental import pallas as pl
from jax.experimental.pallas import tpu as pltpu

F32 = jnp.float32
BF16 = jnp.bfloat16

D_MODEL = 1024
DEPTH = 2
N_META = 16
EPS = 1e-6
RET_HEADS = 4
RET_DK = 128
GLA_HEADS = 4
GLA_DK = 64
GLA_DV = 128
GLA_QK = GLA_HEADS * GLA_DK
GLA_V = GLA_HEADS * GLA_DV
GLA_RANK = 16
GLA_TAU = 16.0
D_FF = 2816
ROPE_BASE = 10000.0
IN_WIDTH = 3600
IN_PAD = 3840
C_RQ, C_RK, C_RV, C_RG, C_GR, C_GQ, C_GK, C_GV, C_GA = 0, 512, 1024, 1536, 2048, 2560, 2816, 3072, 3584
P_RET, P_GATE, P_GLA = 1536, 1024, 1280


def _to_kernel_cols(w):
    pad = jnp.zeros(w.shape[:-1] + (IN_PAD - IN_WIDTH,), w.dtype)
    return jnp.concatenate([w[..., 0:2048], w[..., 3072:3584], w[..., 2048:3072], w[..., 3584:3600], pad], axis=-1)


def _to_reference_chips(d_ret, d_gate, d_gla):
    segs = [(d_ret, 0, 0, 1536), (d_gate, 0, 1536, 512), (d_gla, 0, 2048, 1024), (d_gate, 512, 3072, 512),
            (d_gla, 1024, 3584, GLA_RANK)]
    per = IN_WIDTH // 4
    chips = []
    for j in range(4):
        lo, hi, parts = per * j, per * (j + 1), []
        for piece, p0, r0, width in segs:
            a, b = max(lo, r0), min(hi, r0 + width)
            if a < b:
                parts.append(piece[:, p0 + a - r0:p0 + b - r0])
        chips.append(jnp.concatenate(parts, axis=1))
    return jnp.stack(chips)

FRONT = 64
BACK = 64
PADF = FRONT - N_META
RET_CHUNK = 128
GLA_CHUNK = 64
GLA_SUB = 16
GLA_SUB2 = 4
BLK = 640

ADAM_LR, ADAM_B1, ADAM_B2, ADAM_EPS, ADAM_WD, ADAM_STEP = 0.001, 0.9, 0.999, 1e-08, 0.01, 10

VMEM_LIMIT = 56 * 2 ** 20
MM_VMEM_BUDGET = 40 * 2 ** 20
MESH = pl.DeviceIdType.MESH


def _cp(*sem):
    return pltpu.CompilerParams(dimension_semantics=sem, vmem_limit_bytes=VMEM_LIMIT)


def _tile(n, cands):
    for t in cands:
        if n % t == 0:
            return t
    raise ValueError(f"no tile for {n} in {cands}")


def _row_tile(n):
    return _tile(n, (640, 512, 320, 256, 128, 64))


def _mm(a, b, *, nt=False, add=None, out_dtype=F32, tn=None, name, carry=None):
    m, k = a.shape
    n = b.shape[0] if nt else b.shape[1]
    tm = _tile(m, (640, 320, 256, 128, 64))
    if tn is None:
        step_bytes = lambda t: 2 * (tm * k * a.dtype.itemsize + t * k * b.dtype.itemsize
                                    + tm * t * (jnp.dtype(out_dtype).itemsize + (4 if add is not None else 0)))
        tn = next(t for t in range(n, 0, -128) if n % t == 0 and (step_bytes(t) <= MM_VMEM_BUDGET or t == 128))
    dn = (((1,), (1,)), ((), ())) if nt else (((1,), (0,)), ((), ()))
    nj, ni = n // tn, m // tm
    n_in = 2 + (add is not None)
    c_arrs, c_mode = carry if carry is not None else ((), None)
    nc = len(c_arrs)

    def body(*refs):
        a_ref, b_ref = refs[:2]
        c_ref = refs[2] if add is not None else None
        o_ref = refs[n_in + nc]
        if nc:
            c_ins, c_outs = refs[n_in:n_in + nc], refs[n_in + nc + 1:n_in + 2 * nc + 1]
            ssem, rsem = refs[n_in + 2 * nc + 1:]
            j, i = pl.program_id(0), pl.program_id(1)

            @pl.when((j == 0) & (i == 0))
            def _():
                for cp in _chip_copies(c_ins, c_outs, ssem, rsem, c_mode)[0]:
                    cp.start()
        r = lax.dot_general(a_ref[...].astype(BF16), b_ref[...].astype(BF16), dn, preferred_element_type=F32)
        if add is not None:
            r = r + c_ref[...]
        o_ref[...] = r.astype(o_ref.dtype)
        if nc:
            @pl.when((j == nj - 1) & (i == ni - 1))
            def _():
                _chip_wait(*_chip_copies(c_ins, c_outs, ssem, rsem, c_mode))

    b_spec = pl.BlockSpec((tn, k), lambda j, i: (j, 0)) if nt else pl.BlockSpec((k, tn), lambda j, i: (0, j))
    in_specs = [pl.BlockSpec((tm, k), lambda j, i: (i, 0)), b_spec]
    args = [a, b]
    if add is not None:
        in_specs.append(pl.BlockSpec((tm, tn), lambda j, i: (i, j)))
        args.append(add)
    out_shape = jax.ShapeDtypeStruct((m, n), out_dtype)
    out_spec = pl.BlockSpec((tm, tn), lambda j, i: (i, j))
    if not nc:
        return pl.pallas_call(
            body, out_shape=out_shape, grid=(nj, ni), in_specs=in_specs, out_specs=out_spec,
            compiler_params=_cp("parallel", "parallel"), name=name)(*args)
    sem = pltpu.SemaphoreType.DMA
    outs = pl.pallas_call(
        body, out_shape=(out_shape,) + tuple(_landing_shape(x, c_mode) for x in c_arrs), grid=(nj, ni),
        in_specs=in_specs + [ANY] * nc, out_specs=(out_spec,) + (ANY,) * nc,
        scratch_shapes=[sem((nc, 3)), sem((nc, 3))],
        compiler_params=_cp("arbitrary", "arbitrary"), name=name)(*args, *c_arrs)
    return outs[0], list(outs[1:])


def _call_with_carry(body, *, out_shape, grid, in_specs, out_specs, args, semantics, carry, name, aliases=None):
    if carry is None:
        return pl.pallas_call(body, out_shape=out_shape, grid=grid, in_specs=in_specs, out_specs=out_specs,
                              input_output_aliases=aliases or {}, compiler_params=_cp(*semantics), name=name)(*args)
    c_arrs, c_mode = carry
    n_in, nc = len(args), len(c_arrs)

    def carried(*refs):
        c_ins, c_outs = refs[n_in:n_in + nc], refs[n_in + nc + 1:n_in + 2 * nc + 1]
        ssem, rsem = refs[n_in + 2 * nc + 1:]
        ids = [pl.program_id(d) for d in range(len(grid))]
        first = functools.reduce(lambda u, v: u & v, [i == 0 for i in ids])
        last = functools.reduce(lambda u, v: u & v, [i == g - 1 for i, g in zip(ids, grid)])

        @pl.when(first)
        def _():
            for cp in _chip_copies(c_ins, c_outs, ssem, rsem, c_mode)[0]:
                cp.start()
        body(*refs[:n_in], refs[n_in + nc])

        @pl.when(last)
        def _():
            _chip_wait(*_chip_copies(c_ins, c_outs, ssem, rsem, c_mode))

    sem = pltpu.SemaphoreType.DMA
    outs = pl.pallas_call(
        carried, out_shape=(out_shape,) + tuple(_landing_shape(x, c_mode) for x in c_arrs), grid=grid,
        in_specs=list(in_specs) + [ANY] * nc, out_specs=(out_specs,) + (ANY,) * nc,
        scratch_shapes=[sem((nc, 3)), sem((nc, 3))], input_output_aliases=aliases or {},
        compiler_params=_cp(*(("arbitrary",) * len(grid))), name=name)(*args, *c_arrs)
    return outs[0], list(outs[1:])


def _mm_nt_sum(a_list, b_list, *, name, carry=None):
    m, n = a_list[0].shape[0], b_list[0].shape[0]
    tm = _tile(m, (640, 320, 256, 128, 64))
    np_ = len(a_list)

    def body(*refs):
        acc = None
        for a_ref, b_ref in zip(refs[:np_], refs[np_:2 * np_]):
            r = lax.dot_general(a_ref[...].astype(BF16), b_ref[...].astype(BF16), (((1,), (1,)), ((), ())),
                                preferred_element_type=F32)
            acc = r if acc is None else acc + r
        refs[2 * np_][...] = acc

    return _call_with_carry(
        body, out_shape=jax.ShapeDtypeStruct((m, n), F32), grid=(m // tm,),
        in_specs=[pl.BlockSpec((tm, a.shape[1]), lambda i: (i, 0)) for a in a_list]
        + [pl.BlockSpec(b.shape, lambda i: (0, 0)) for b in b_list],
        out_specs=pl.BlockSpec((tm, n), lambda i: (i, 0)), args=[*a_list, *b_list], semantics=("parallel",),
        carry=carry, name=name)


def _mm_tn(a, b, *, tn=None, blocks=None, into=None, name, carry=None):
    m, k = a.shape
    n = b.shape[1]
    tm = _tile(m, (1664, 640, 320, 256, 128, 64))
    tn = n if tn is None else tn
    if blocks is not None:
        total, first = blocks
        out_shape = jax.ShapeDtypeStruct((total, k, tn), F32)
        out_spec = pl.BlockSpec((None, k, tn), lambda j, i: (first + j, 0, 0))
    else:
        out_shape = jax.ShapeDtypeStruct((k, n), F32)
        out_spec = pl.BlockSpec((k, tn), lambda j, i: (0, j))

    def body(a_ref, b_ref, *rest):
        o_ref = rest[-1]

        @pl.when(pl.program_id(1) == 0)
        def _():
            o_ref[...] = jnp.zeros_like(o_ref)
        o_ref[...] += lax.dot_general(a_ref[...].astype(BF16), b_ref[...].astype(BF16),
                                      (((0,), (0,)), ((), ())), preferred_element_type=F32)

    in_specs = [pl.BlockSpec((tm, k), lambda j, i: (i, 0)), pl.BlockSpec((tm, tn), lambda j, i: (i, j))]
    args, alias = [a, b], {}
    if into is not None:
        in_specs.append(pl.BlockSpec(memory_space=pl.ANY))
        args.append(into)
        alias = {2: 0}
    return _call_with_carry(body, out_shape=out_shape, grid=(n // tn, m // tm), in_specs=in_specs, out_specs=out_spec,
                            args=args, semantics=("parallel", "arbitrary"), carry=carry, name=name, aliases=alias)


def _rms(x, w):
    r = lax.rsqrt(jnp.mean(x * x, axis=-1, keepdims=True) + EPS)
    return x * r * w


def _rms_bwd(x, w, dy):
    r = lax.rsqrt(jnp.mean(x * x, axis=-1, keepdims=True) + EPS)
    xh = x * r
    dxh = dy * w
    dx = r * (dxh - xh * jnp.mean(dxh * xh, axis=-1, keepdims=True))
    return dx, jnp.sum(dy * xh, axis=0, keepdims=True)


def _resid_norm(h, t, w_post, w_next, *, name):
    lp, d = h.shape
    tm = _row_tile(lp)
    has_t = t is not None

    def body(*refs):
        if has_t:
            h_ref, t_ref, wp_ref, wn_ref, ho_ref, hn_ref = refs
            hv = h_ref[...] + _rms(t_ref[...], wp_ref[...])
            ho_ref[...] = hv
        else:
            h_ref, wn_ref, hn_ref = refs
            hv = h_ref[...]
        hn_ref[...] = _rms(hv, wn_ref[...]).astype(BF16)

    row = pl.BlockSpec((tm, d), lambda i: (i, 0))
    vec = pl.BlockSpec((1, d), lambda i: (0, 0))
    if has_t:
        return pl.pallas_call(
            body, out_shape=(jax.ShapeDtypeStruct((lp, d), F32), jax.ShapeDtypeStruct((lp, d), BF16)),
            grid=(lp // tm,), in_specs=[row, row, vec, vec], out_specs=(row, row),
            compiler_params=_cp("parallel"), name=name)(h, t, w_post, w_next)
    return h, pl.pallas_call(
        body, out_shape=jax.ShapeDtypeStruct((lp, d), BF16), grid=(lp // tm,), in_specs=[row, vec],
        out_specs=row, compiler_params=_cp("parallel"), name=name)(h, w_next)


def _resid_norm_bwd(dh_out, dhn, h_new, t, w_next, w_post, *, name):
    lp, d = h_new.shape if h_new is not None else t.shape
    tm = _row_tile(lp)
    has_n = dhn is not None
    has_t = t is not None

    def body(*refs):
        refs = list(refs)
        dho_ref = refs.pop(0)
        if has_n:
            dhn_ref, hn_ref, wn_ref = refs.pop(0), refs.pop(0), refs.pop(0)
        if has_t:
            t_ref, wp_ref = refs.pop(0), refs.pop(0)
        dh_ref = refs.pop(0) if has_n else None
        dt_ref = refs.pop(0) if has_t else None
        dwn_ref = refs.pop(0) if has_n else None
        dwp_ref = refs.pop(0) if has_t else None
        first = pl.program_id(0) == 0
        dh = dho_ref[...]
        if has_n:
            dx, dwn = _rms_bwd(hn_ref[...], wn_ref[...], dhn_ref[...])
            dh = dh + dx
            dh_ref[...] = dh

            @pl.when(first)
            def _():
                dwn_ref[...] = jnp.zeros_like(dwn_ref)
            dwn_ref[...] += dwn
        if has_t:
            dt, dwp = _rms_bwd(t_ref[...], wp_ref[...], dh)
            dt_ref[...] = dt.astype(BF16)

            @pl.when(first)
            def _():
                dwp_ref[...] = jnp.zeros_like(dwp_ref)
            dwp_ref[...] += dwp

    row = pl.BlockSpec((tm, d), lambda i: (i, 0))
    vec = pl.BlockSpec((1, d), lambda i: (0, 0))
    args, in_specs, out_shape, out_specs = [dh_out], [row], [], []
    if has_n:
        args += [dhn, h_new, w_next]
        in_specs += [row, row, vec]
    if has_t:
        args += [t, w_post]
        in_specs += [row, vec]
    if has_n:
        out_shape.append(jax.ShapeDtypeStruct((lp, d), F32)); out_specs.append(row)
    if has_t:
        out_shape.append(jax.ShapeDtypeStruct((lp, d), BF16)); out_specs.append(row)
    if has_n:
        out_shape.append(jax.ShapeDtypeStruct((1, d), F32)); out_specs.append(vec)
    if has_t:
        out_shape.append(jax.ShapeDtypeStruct((1, d), F32)); out_specs.append(vec)
    outs = list(pl.pallas_call(body, out_shape=tuple(out_shape), grid=(lp // tm,), in_specs=in_specs,
                               out_specs=tuple(out_specs), compiler_params=_cp("arbitrary"), name=name)(*args))
    dh = outs.pop(0) if has_n else dh_out
    dt = outs.pop(0) if has_t else None
    dwn = outs.pop(0) if has_n else None
    dwp = outs.pop(0) if has_t else None
    return dh, dt, dwn, dwp


def _loss_head(h, f, w_post, target, *, name):
    lp, d = h.shape
    tm = _row_tile(lp)

    def body(h_ref, f_ref, w_ref, t_ref, loss_ref, dy_ref, df_ref, dw_ref):
        i = pl.program_id(0)
        f, w = f_ref[...], w_ref[...]
        y = h_ref[...] + _rms(f, w)
        rows = i * tm + lax.broadcasted_iota(jnp.int32, (tm, 1), 0)
        tok = (rows >= FRONT) & (rows < lp - BACK)
        err = jnp.where(tok, y - t_ref[...], 0.0)
        dy = err * (1.0 / d)
        dy_ref[...] = dy
        df, dw = _rms_bwd(f, w, dy)
        df_ref[...] = df.astype(BF16)

        @pl.when(i == 0)
        def _():
            loss_ref[...] = jnp.zeros_like(loss_ref)
            dw_ref[...] = jnp.zeros_like(dw_ref)
        part = jnp.sum(jnp.sum(err * err, axis=1, keepdims=True), axis=0, keepdims=True) * (0.5 / d)
        loss_ref[...] += jnp.broadcast_to(part, loss_ref.shape)
        dw_ref[...] += dw

    row = pl.BlockSpec((tm, d), lambda i: (i, 0))
    vec = pl.BlockSpec((1, d), lambda i: (0, 0))
    loss, dy, df, dw = pl.pallas_call(
        body, out_shape=(jax.ShapeDtypeStruct((8, 128), F32), jax.ShapeDtypeStruct((lp, d), F32),
                         jax.ShapeDtypeStruct((lp, d), BF16), jax.ShapeDtypeStruct((1, d), F32)),
        grid=(lp // tm,), in_specs=[row, row, vec, row],
        out_specs=(pl.BlockSpec((8, 128), lambda i: (0, 0)), row, row, vec),
        compiler_params=_cp("arbitrary"), name=name)(h, f, w_post, target)
    return loss[0, 0], dy, df, dw


_GELU_C = math.sqrt(2.0 / math.pi)


def _gelu_and_grad(a):
    a2 = a * a
    t = jnp.tanh(a * (_GELU_C + (_GELU_C * 0.044715) * a2))
    ha = 0.5 * a
    h1 = 0.5 + 0.5 * t
    return a * h1, h1 + ha * (1.0 - t * t) * (_GELU_C + (3.0 * _GELU_C * 0.044715) * a2)


def _gelu(a):
    t = jnp.tanh(a * (_GELU_C + (_GELU_C * 0.044715) * (a * a)))
    return a * (0.5 + 0.5 * t)


def _conv3(parts, n, w, b):
    xx = jnp.concatenate(parts, axis=0)
    return b + xx[8:8 + n] * w[2:3] + pltpu.roll(xx, 1, 0)[8:8 + n] * w[1:2] + pltpu.roll(xx, 2, 0)[8:8 + n] * w[0:1]


def _conv_act(ua, ug, wa, wg, ba, bg, *, name):
    lp, n = ua.shape
    tm = _row_tile(lp)
    tc = _tile(n, (256, 128))
    nb8 = tm // 8

    def body(ua_ref, uap_ref, ug_ref, ugp_ref, wa_ref, wg_ref, ba_ref, bg_ref, o_ref):
        i = pl.program_id(0)
        ca = _conv3([uap_ref[...], ua_ref[...]], tm, wa_ref[...], ba_ref[...])
        cg = _conv3([ugp_ref[...], ug_ref[...]], tm, wg_ref[...], bg_ref[...])
        rows = i * tm + lax.broadcasted_iota(jnp.int32, (tm, 1), 0)
        ok = (rows >= PADF) & (rows < lp - BACK)
        o_ref[...] = jnp.where(ok, _gelu(ca) * cg, 0.0).astype(BF16)

    cur = pl.BlockSpec((tm, tc), lambda i, j: (i, j))
    prev = pl.BlockSpec((8, tc), lambda i, j: (jnp.maximum(i * nb8 - 1, 0), j))
    w3 = pl.BlockSpec((3, tc), lambda i, j: (0, j))
    b1 = pl.BlockSpec((1, tc), lambda i, j: (0, j))
    return pl.pallas_call(
        body, out_shape=jax.ShapeDtypeStruct((lp, n), BF16), grid=(lp // tm, n // tc),
        in_specs=[cur, prev, cur, prev, w3, w3, b1, b1], out_specs=cur,
        compiler_params=_cp("parallel", "parallel"), name=name)(ua, ua, ug, ug, wa, wg, ba, bg)


def _conv_act_down(ua, ug, wa, wg, ba, bg, down, *, name):
    lp, n = ua.shape
    d = down.shape[1]
    tm = _tile(lp, (320, 256, 128, 64))
    tc = _tile(n, (256, 128))
    nb8 = tm // 8

    def body(ua_ref, uap_ref, ug_ref, ugp_ref, wa_ref, wg_ref, ba_ref, bg_ref, dn_ref, act_ref, f_ref):
        i = pl.program_id(0)
        rows = i * tm + lax.broadcasted_iota(jnp.int32, (tm, 1), 0)
        ok = (rows >= PADF) & (rows < lp - BACK)
        acc = None
        for j in range(n // tc):
            cs = slice(tc * j, tc * j + tc)
            ca = _conv3([uap_ref[:, cs], ua_ref[:, cs]], tm, wa_ref[:, cs], ba_ref[:, cs])
            cg = _conv3([ugp_ref[:, cs], ug_ref[:, cs]], tm, wg_ref[:, cs], bg_ref[:, cs])
            act = jnp.where(ok, _gelu(ca) * cg, 0.0).astype(BF16)
            act_ref[:, cs] = act
            part = _dot(act, dn_ref[cs, :])
            acc = part if acc is None else acc + part
        f_ref[...] = acc

    cur = pl.BlockSpec((tm, n), lambda i: (i, 0))
    prev = pl.BlockSpec((8, n), lambda i: (jnp.maximum(i * nb8 - 1, 0), 0))
    w3 = pl.BlockSpec((3, n), lambda i: (0, 0))
    b1 = pl.BlockSpec((1, n), lambda i: (0, 0))
    return pl.pallas_call(
        body, out_shape=(jax.ShapeDtypeStruct((lp, n), BF16), jax.ShapeDtypeStruct((lp, d), F32)),
        grid=(lp // tm,),
        in_specs=[cur, prev, cur, prev, w3, w3, b1, b1, pl.BlockSpec(down.shape, lambda i: (0, 0))],
        out_specs=(cur, pl.BlockSpec((tm, d), lambda i: (i, 0))),
        compiler_params=_cp("parallel"), name=name)(ua, ua, ug, ug, wa, wg, ba, bg, down)


def _ffn_fwd(h, m, w_post, w_next, up_a, up_g, wa, wg, ba, bg, down, *, name, carry=None, after=None):
    lp, d = h.shape
    n = up_a.shape[1]
    tm = _tile(lp, (320, 256, 128, 64))
    tc = _tile(n, (256, 128))
    nchunks = n // tc
    c_arrs, c_mode = carry if carry is not None else ((), None)
    nc = len(c_arrs)
    steps = lp // tm
    n_out = 6 if after is None else 8
    after = after or ()

    def body(h_ref, hp_ref, m_ref, mp_ref, wp_ref, wn_ref, upa_ref, upg_ref, wa_ref, wg_ref, ba_ref, bg_ref, dn_ref,
             *rest):
        after_refs, rest = rest[:len(after)], rest[len(after):]
        c_ins = rest[:nc]
        hmid_ref, hn_ref, ua_ref, ug_ref, act_ref, f_ref = rest[nc:nc + 6]
        c_outs = rest[nc + n_out:2 * nc + n_out]
        i = pl.program_id(0)
        if nc:
            ssem, rsem = rest[2 * nc + n_out:]

            @pl.when(i == 0)
            def _():
                for cp in _chip_copies(c_ins, c_outs, ssem, rsem, c_mode)[0]:
                    cp.start()
        rows = i * tm + lax.broadcasted_iota(jnp.int32, (tm, 1), 0)
        ok = (rows >= PADF) & (rows < lp - BACK)
        hv = (jnp.concatenate([hp_ref[...], h_ref[...]], axis=0)
              + _rms(jnp.concatenate([mp_ref[...], m_ref[...]], axis=0), wp_ref[...]))
        x = _rms(hv, wn_ref[...]).astype(BF16)
        hmid_ref[...] = hv[16:]
        hn_ref[...] = x[16:]
        u_of = lambda j: (_dot(x, upa_ref[:, tc * j:tc * j + tc]), _dot(x, upg_ref[:, tc * j:tc * j + tc]))
        u_next = u_of(0)
        acc = None
        for j in range(nchunks):
            cs = slice(tc * j, tc * j + tc)
            ua, ug = u_next
            if j + 1 < nchunks:
                u_next = u_of(j + 1)
            ua_ref[:, cs] = ua[16:]
            ug_ref[:, cs] = ug[16:]
            ca = _conv3([ua[8:]], tm, wa_ref[:, cs], ba_ref[:, cs])
            cg = _conv3([ug[8:]], tm, wg_ref[:, cs], bg_ref[:, cs])
            act = jnp.where(ok, _gelu(ca) * cg, 0.0).astype(BF16)
            act_ref[:, cs] = act
            part = _dot(act, dn_ref[cs, :])
            acc = part if acc is None else acc + part
        f_ref[...] = acc
        if after:
            h_new = hv[16:] + _rms(acc, after_refs[0][...])
            rest[nc + 6][...] = h_new
            rest[nc + 7][...] = _rms(h_new, after_refs[1][...]).astype(BF16)
        if nc:
            @pl.when(i == steps - 1)
            def _():
                _chip_wait(*_chip_copies(c_ins, c_outs, ssem, rsem, c_mode))

    whole = pl.BlockSpec(memory_space=pltpu.VMEM)
    wide = pl.BlockSpec((tm, n), lambda i: (i, 0))
    w3 = pl.BlockSpec((3, n), lambda i: (0, 0))
    b1 = pl.BlockSpec((1, n), lambda i: (0, 0))
    sem = pltpu.SemaphoreType.DMA
    row = pl.BlockSpec((tm, d), lambda i: (i, 0))
    prev16 = pl.BlockSpec((16, d), lambda i: (jnp.maximum(i * (tm // 16) - 1, 0), 0))
    vec = pl.BlockSpec((1, d), lambda i: (0, 0))
    outs = pl.pallas_call(
        body,
        out_shape=(jax.ShapeDtypeStruct((lp, d), F32), jax.ShapeDtypeStruct((lp, d), BF16),
                   jax.ShapeDtypeStruct((lp, n), F32), jax.ShapeDtypeStruct((lp, n), F32),
                   jax.ShapeDtypeStruct((lp, n), BF16), jax.ShapeDtypeStruct((lp, d), F32))
        + ((jax.ShapeDtypeStruct((lp, d), F32), jax.ShapeDtypeStruct((lp, d), BF16)) if after else ())
        + tuple(_landing_shape(a, c_mode) for a in c_arrs),
        grid=(steps,),
        in_specs=[row, prev16, row, prev16, vec, vec, whole, whole, w3, w3, b1, b1, whole] + [vec] * len(after)
        + [ANY] * nc,
        out_specs=(row, row, wide, wide, wide, row) + ((row, row) if after else ()) + (ANY,) * nc,
        scratch_shapes=[sem((nc, 3)), sem((nc, 3))] if nc else [],
        compiler_params=_cp("arbitrary"), name=name)(h, h, m, m, w_post, w_next, up_a, up_g, wa, wg, ba, bg, down,
                                                     *after, *c_arrs)
    return outs[:n_out], list(outs[n_out:])


def _conv_act_bwd(ua, ug, df, down, wa, wg, ba, bg, up_a, up_g, *, name):
    lp, n = ua.shape
    d = up_a.shape[0]
    tm = _tile(lp, (320, 256, 128, 64))
    tc = _tile(n, (256, 128))
    nb8 = tm // 8
    last8 = lp // 8 - 1
    last16 = lp // 16 - 1
    ext = tm + 8

    def body(ua_ref, uap_ref, uan_ref, ug_ref, ugp_ref, ugn_ref, df_ref, dfn_ref, dn_ref, wa_ref, wg_ref, ba_ref,
             bg_ref, upa_ref, upg_ref, dua_ref, dug_ref, dwa_ref, dwg_ref, dba_ref, dbg_ref, dhn_ref):
        i = pl.program_id(0)
        df_ext = jnp.concatenate([df_ref[...], dfn_ref[...]], axis=0)

        @pl.when(i == 0)
        def _():
            dwa_ref[...] = jnp.zeros_like(dwa_ref)
            dwg_ref[...] = jnp.zeros_like(dwg_ref)
            dba_ref[...] = jnp.zeros_like(dba_ref)
            dbg_ref[...] = jnp.zeros_like(dbg_ref)
        rows = i * tm + lax.broadcasted_iota(jnp.int32, (ext, 1), 0)
        ok = (rows >= PADF) & (rows < lp - BACK)

        def conv(parts, w, b):
            xx = jnp.concatenate(parts, axis=0)
            x, x1, x2 = xx[8:8 + ext], pltpu.roll(xx, 1, 0)[8:8 + ext], pltpu.roll(xx, 2, 0)[8:8 + ext]
            return b + x * w[2:3] + x1 * w[1:2] + x2 * w[0:1], x, x1, x2

        def back(dc, w):
            return (dc[:tm] * w[2:3] + pltpu.roll(dc, ext - 1, 0)[:tm] * w[1:2]
                    + pltpu.roll(dc, ext - 2, 0)[:tm] * w[0:1])

        def wsum(dw_ref, db_ref, cs, dc, x, x1, x2):
            dd = dc[:tm]
            s = lambda v: jnp.sum(v, axis=0, keepdims=True)
            dw_ref[0:1, cs] += s(dd * x2[:tm])
            dw_ref[1:2, cs] += s(dd * x1[:tm])
            dw_ref[2:3, cs] += s(dd * x[:tm])
            db_ref[:, cs] += s(dd)

        acc = None
        nchunks = n // tc
        dact_of = lambda j: _dot_nt(df_ext, dn_ref[tc * j:tc * j + tc, :])[:ext]
        dact_next = dact_of(0)
        for j in range(nchunks):
            cs = slice(tc * j, tc * j + tc)
            dact_cur = dact_next
            if j + 1 < nchunks:
                dact_next = dact_of(j + 1)
            wa, wg = wa_ref[:, cs], wg_ref[:, cs]
            ca, xa, xa1, xa2 = conv([uap_ref[:, cs], ua_ref[:, cs], uan_ref[:, cs]], wa, ba_ref[:, cs])
            cg, xg, xg1, xg2 = conv([ugp_ref[:, cs], ug_ref[:, cs], ugn_ref[:, cs]], wg, bg_ref[:, cs])
            dact_e = jnp.where(ok, dact_cur, 0.0)
            gel, gel_d = _gelu_and_grad(ca)
            dca = dact_e * cg * gel_d
            dcg = dact_e * gel
            du_a, du_g = back(dca, wa).astype(BF16), back(dcg, wg).astype(BF16)
            dua_ref[:, cs] = du_a
            dug_ref[:, cs] = du_g
            wsum(dwa_ref, dba_ref, cs, dca, xa, xa1, xa2)
            wsum(dwg_ref, dbg_ref, cs, dcg, xg, xg1, xg2)
            part = _dot_nt(du_a, upa_ref[:, cs]) + _dot_nt(du_g, upg_ref[:, cs])
            acc = part if acc is None else acc + part
        dhn_ref[...] = acc

    cur = pl.BlockSpec((tm, n), lambda i: (i, 0))
    prev = pl.BlockSpec((8, n), lambda i: (jnp.maximum(i * nb8 - 1, 0), 0))
    nxt = pl.BlockSpec((8, n), lambda i: (jnp.minimum((i + 1) * nb8, last8), 0))
    w3 = pl.BlockSpec((3, n), lambda i: (0, 0))
    b1 = pl.BlockSpec((1, n), lambda i: (0, 0))
    whole = pl.BlockSpec(memory_space=pltpu.VMEM)
    return pl.pallas_call(
        body,
        out_shape=(jax.ShapeDtypeStruct((lp, n), BF16), jax.ShapeDtypeStruct((lp, n), BF16),
                   jax.ShapeDtypeStruct((3, n), F32), jax.ShapeDtypeStruct((3, n), F32),
                   jax.ShapeDtypeStruct((1, n), F32), jax.ShapeDtypeStruct((1, n), F32),
                   jax.ShapeDtypeStruct((lp, d), F32)),
        grid=(lp // tm,),
        in_specs=[cur, prev, nxt, cur, prev, nxt, pl.BlockSpec((tm, d), lambda i: (i, 0)),
                  pl.BlockSpec((16, d), lambda i: (jnp.minimum((i + 1) * (tm // 16), last16), 0)), whole,
                  w3, w3, b1, b1, whole, whole],
        out_specs=(cur, cur, w3, w3, b1, b1, pl.BlockSpec((tm, d), lambda i: (i, 0))),
        compiler_params=_cp("arbitrary"), name=name)(ua, ua, ua, ug, ug, ug, df, df, down, wa, wg, ba, bg, up_a, up_g)


def _sigmoid(x):
    return 1.0 / (1.0 + jnp.exp(-x))


def _merge_mix_out(o_ret, o_gla, proj, w_ret, w_gla, wout, *, name, carry=None):
    lp = o_ret.shape[0]
    d = wout.shape[1]
    tm = _row_tile(lp)
    steps = lp // tm
    c_arrs, c_mode = carry if carry is not None else ((), None)
    nc = len(c_arrs)

    def body(or_ref, og_ref, rg_ref, gr_ref, wr_ref, wg_ref, wo_ref, *rest):
        c_ins = rest[:nc]
        m_ref, out_ref = rest[nc:nc + 2]
        c_outs = rest[nc + 2:2 * nc + 2]
        i = pl.program_id(0)
        if nc:
            ssem, rsem = rest[2 * nc + 2:]

            @pl.when(i == 0)
            def _():
                for cp in _chip_copies(c_ins, c_outs, ssem, rsem, c_mode)[0]:
                    cp.start()
        oret, ogla = or_ref[...], og_ref[...]
        yr, yg = [], []
        for h in range(4):
            hs = slice(128 * h, 128 * h + 128)
            o = oret[:, hs]
            xc = o - jnp.mean(o, axis=-1, keepdims=True)
            yr.append(xc * lax.rsqrt(jnp.mean(xc * xc, axis=-1, keepdims=True) + EPS))
            o = ogla[:, hs]
            yg.append(o * lax.rsqrt(jnp.mean(o * o, axis=-1, keepdims=True) + EPS))
        rg, gr = rg_ref[...], gr_ref[...]
        ret = (jnp.concatenate(yr, axis=1) * wr_ref[...] * (rg * _sigmoid(rg))).astype(BF16)
        gla = (jnp.concatenate(yg, axis=1) * wg_ref[...] * (gr * _sigmoid(gr))).astype(BF16)
        m_ref[:, 0:512] = ret
        m_ref[:, 512:1024] = gla
        out_ref[...] = _dot(ret, wo_ref[0:512, :]) + _dot(gla, wo_ref[512:1024, :])
        if nc:
            @pl.when(i == steps - 1)
            def _():
                _chip_wait(*_chip_copies(c_ins, c_outs, ssem, rsem, c_mode))

    row = pl.BlockSpec((tm, 512), lambda i: (i, 0))
    vec = pl.BlockSpec((1, 512), lambda i: (0, 0))
    wide = pl.BlockSpec((tm, 1024), lambda i: (i, 0))
    sem = pltpu.SemaphoreType.DMA
    outs = pl.pallas_call(
        body, out_shape=(jax.ShapeDtypeStruct((lp, 1024), BF16), jax.ShapeDtypeStruct((lp, d), F32))
        + tuple(_landing_shape(a, c_mode) for a in c_arrs),
        grid=(steps,),
        in_specs=[row, row, pl.BlockSpec((tm, 512), lambda i: (i, C_RG // 512)),
                  pl.BlockSpec((tm, 512), lambda i: (i, C_GR // 512)), vec, vec,
                  pl.BlockSpec(memory_space=pltpu.VMEM)] + [ANY] * nc,
        out_specs=(wide, pl.BlockSpec((tm, d), lambda i: (i, 0))) + (ANY,) * nc,
        scratch_shapes=[sem((nc, 3)), sem((nc, 3))] if nc else [],
        compiler_params=_cp("arbitrary"), name=name)(o_ret, o_gla, proj, proj, w_ret, w_gla, wout, *c_arrs)
    return outs[:2], list(outs[2:])


def _merge_bwd(dm, wout, o_ret, o_gla, proj, w_ret, w_gla, *, name, carry=None):
    lp = o_ret.shape[0]
    tm = _row_tile(lp)
    steps = lp // tm
    c_arrs, c_mode = carry if carry is not None else ((), None)
    nc = len(c_arrs)

    def body(dm_ref, wo_ref, or_ref, og_ref, rg_ref, gr_ref, wr_ref, wg_ref, *rest):
        c_ins = rest[:nc]
        dor_ref, dog_ref, dgate_ref, dwr_ref, dwg_ref = rest[nc:nc + 5]
        c_outs = rest[nc + 5:2 * nc + 5]
        i = pl.program_id(0)
        if nc:
            ssem, rsem = rest[2 * nc + 5:]

            @pl.when(i == 0)
            def _():
                for cp in _chip_copies(c_ins, c_outs, ssem, rsem, c_mode)[0]:
                    cp.start()

        @pl.when(i == 0)
        def _():
            dwr_ref[...] = jnp.zeros_like(dwr_ref)
            dwg_ref[...] = jnp.zeros_like(dwg_ref)

        def group(d, o_all, gate, w, center):
            sg = _sigmoid(gate)
            s = gate * sg
            ds = sg * (1.0 + gate * (1.0 - sg))
            xh, rr = [], []
            for h in range(4):
                o = o_all[:, 128 * h:128 * h + 128]
                if center:
                    o = o - jnp.mean(o, axis=-1, keepdims=True)
                r = lax.rsqrt(jnp.mean(o * o, axis=-1, keepdims=True) + EPS)
                xh.append(o * r)
                rr.append(r)
            xh_all = jnp.concatenate(xh, axis=1)
            dgate = d * xh_all * w * ds
            dw = jnp.sum(d * xh_all * s, axis=0, keepdims=True)
            dxh_all = d * w * s
            do = []
            for h in range(4):
                dxh = dxh_all[:, 128 * h:128 * h + 128]
                t = dxh - xh[h] * jnp.mean(dxh * xh[h], axis=-1, keepdims=True)
                if center:
                    t = t - jnp.mean(dxh, axis=-1, keepdims=True)
                do.append(rr[h] * t)
            return jnp.concatenate(do, axis=1), dgate, dw

        dmb = dm_ref[...]
        do, dg, dw = group(_dot_nt(dmb, wo_ref[0:512, :]), or_ref[...], rg_ref[...], wr_ref[...], True)
        dor_ref[...] = do
        dgate_ref[:, 0:512] = dg.astype(BF16)
        dwr_ref[...] += dw
        do, dg, dw = group(_dot_nt(dmb, wo_ref[512:1024, :]), og_ref[...], gr_ref[...], wg_ref[...], False)
        dog_ref[...] = do
        dgate_ref[:, 512:1024] = dg.astype(BF16)
        dwg_ref[...] += dw
        if nc:
            @pl.when(i == steps - 1)
            def _():
                _chip_wait(*_chip_copies(c_ins, c_outs, ssem, rsem, c_mode))

    row = pl.BlockSpec((tm, 512), lambda i: (i, 0))
    vec = pl.BlockSpec((1, 512), lambda i: (0, 0))
    sem = pltpu.SemaphoreType.DMA
    outs = pl.pallas_call(
        body,
        out_shape=(jax.ShapeDtypeStruct((lp, 512), F32), jax.ShapeDtypeStruct((lp, 512), F32),
                   jax.ShapeDtypeStruct((lp, P_GATE), BF16),
                   jax.ShapeDtypeStruct((1, 512), F32), jax.ShapeDtypeStruct((1, 512), F32))
        + tuple(_landing_shape(a, c_mode) for a in c_arrs),
        grid=(steps,),
        in_specs=[pl.BlockSpec((tm, dm.shape[1]), lambda i: (i, 0)), pl.BlockSpec(memory_space=pltpu.VMEM), row, row,
                  pl.BlockSpec((tm, 512), lambda i: (i, C_RG // 512)),
                  pl.BlockSpec((tm, 512), lambda i: (i, C_GR // 512)), vec, vec] + [ANY] * nc,
        out_specs=(row, row, pl.BlockSpec((tm, P_GATE), lambda i: (i, 0)), vec, vec) + (ANY,) * nc,
        scratch_shapes=[sem((nc, 3)), sem((nc, 3))] if nc else [],
        compiler_params=_cp("arbitrary"), name=name)(dm, wout, o_ret, o_gla, proj, proj, w_ret, w_gla, *c_arrs)
    return outs[:5], list(outs[5:])


def _dot(a, b):
    return lax.dot_general(a, b, (((1,), (0,)), ((), ())), preferred_element_type=F32)


def _dot_nt(a, b):
    return lax.dot_general(a, b, (((1,), (1,)), ((), ())), preferred_element_type=F32)


def _dot_tn(a, b):
    return lax.dot_general(a, b, (((0,), (0,)), ((), ())), preferred_element_type=F32)


def _ret_tables(lp):
    cr = RET_CHUNK
    pos = np.arange(lp, dtype=np.float32) - np.float32(PADF)
    half = RET_DK // 2
    inv = (np.float32(ROPE_BASE) ** (-np.arange(half, dtype=np.float32) / np.float32(half))).astype(np.float32)
    ang = (pos[:, None] * inv[None, :]).astype(np.float32)
    c, s = np.cos(ang).astype(np.float32), np.sin(ang).astype(np.float32)
    rope_c = jnp.asarray(np.concatenate([c, c], axis=1))
    rope_s = jnp.asarray(np.concatenate([-s, s], axis=1))
    log_g = np.log(1.0 - 2.0 ** (-5.0 - np.arange(RET_HEADS, dtype=np.float64)))
    idx = np.arange(cr, dtype=np.float64)
    diff = idx[:, None] - idx[None, :]
    dmat = np.where(diff >= 0, np.exp(log_g[:, None, None] * np.maximum(diff, 0.0)), 0.0)
    zeta = np.exp(log_g[:, None] * (cr - 1.0 - idx)[None, :])
    xi = np.exp(log_g[:, None] * (idx + 1.0)[None, :])
    gc = np.exp(log_g * cr)
    f = lambda a: jnp.asarray(a.astype(np.float32))
    return (rope_c, rope_s, f(dmat), f(np.broadcast_to(zeta[:, :, None], (RET_HEADS, cr, 128))),
            f(np.broadcast_to(xi[:, :, None], (RET_HEADS, cr, 128))),
            f(np.broadcast_to(gc[:, None, None], (RET_HEADS, 8, 128))))


def _rope(t, c, s):
    return t * c + pltpu.roll(t, 64, 1) * s


def _rope_t(d, c, s):
    return d * c + pltpu.roll(d * s, 64, 1)


def _ret_specs(nblk, rev):
    ix = (lambda i: nblk - 1 - i) if rev else (lambda i: i)
    cr = RET_CHUNK
    col = lambda base: pl.BlockSpec((BLK, 512), lambda i: (ix(i), base // 512))
    tab = pl.BlockSpec((BLK, 128), lambda i: (ix(i), 0))
    sq = pl.BlockSpec((RET_HEADS, cr, cr), lambda i: (0, 0, 0))
    hv = pl.BlockSpec((RET_HEADS, cr, 128), lambda i: (0, 0, 0))
    g8 = pl.BlockSpec((RET_HEADS, 8, 128), lambda i: (0, 0, 0))
    st = pl.BlockSpec((RET_HEADS, BLK // cr, 128, 128), lambda i: (0, ix(i), 0, 0))
    out = pl.BlockSpec((BLK, 512), lambda i: (ix(i), 0))
    return col, tab, sq, hv, g8, st, out


def _retention(proj, tables, *, name):
    lp = proj.shape[0]
    nblk, cr = lp // BLK, RET_CHUNK
    scale = RET_DK ** -0.5

    def body(q_ref, k_ref, v_ref, c_ref, s_ref, d_ref, z_ref, x_ref, g_ref, o_ref, st_ref, state):
        @pl.when(pl.program_id(0) == 0)
        def _():
            state[...] = jnp.zeros_like(state)

        def chunk(ci, carry):
            sl = pl.ds(pl.multiple_of(ci * cr, cr), cr)
            c, s = c_ref[sl, :], s_ref[sl, :]
            for h in range(RET_HEADS):
                hs = slice(128 * h, 128 * h + 128)
                q = _rope(q_ref[sl, hs], c, s)
                k = _rope(k_ref[sl, hs], c, s) * scale
                qb, kb, vb = q.astype(BF16), k.astype(BF16), v_ref[sl, hs].astype(BF16)
                st = state[h]
                st_ref[h, ci] = st
                sc = _dot_nt(qb, kb) * d_ref[h]
                o_ref[sl, hs] = _dot(sc.astype(BF16), vb) + _dot(qb, st.astype(BF16)) * x_ref[h]
                state[h] = st * g_ref[h][0:1, :] + _dot_tn((k * z_ref[h]).astype(BF16), vb)
            return carry

        lax.fori_loop(0, BLK // cr, chunk, 0)

    col, tab, sq, hv, g8, st, out = _ret_specs(nblk, False)
    return pl.pallas_call(
        body,
        out_shape=(jax.ShapeDtypeStruct((lp, 512), F32), jax.ShapeDtypeStruct((4, lp // cr, 128, 128), F32)),
        grid=(nblk,), in_specs=[col(C_RQ), col(C_RK), col(C_RV), tab, tab, sq, hv, hv, g8],
        out_specs=(out, st), scratch_shapes=[pltpu.VMEM((RET_HEADS, 128, 128), F32)],
        compiler_params=_cp("arbitrary"), name=name)(proj, proj, proj, *tables)


def _retention_bwd(proj, do, states, tables, *, name):
    lp = proj.shape[0]
    nblk, cr = lp // BLK, RET_CHUNK
    nch = BLK // cr
    scale = RET_DK ** -0.5

    def body(q_ref, k_ref, v_ref, do_ref, st_ref, c_ref, s_ref, d_ref, z_ref, x_ref, g_ref, dqkv_ref, dstate):
        @pl.when(pl.program_id(0) == 0)
        def _():
            dstate[...] = jnp.zeros_like(dstate)

        def chunk(cc, carry):
            ci = nch - 1 - cc
            sl = pl.ds(pl.multiple_of(ci * cr, cr), cr)
            c, s = c_ref[sl, :], s_ref[sl, :]
            for h in range(RET_HEADS):
                hs = slice(128 * h, 128 * h + 128)
                dmat, zeta, xi = d_ref[h], z_ref[h], x_ref[h]
                q = _rope(q_ref[sl, hs], c, s)
                k = _rope(k_ref[sl, hs], c, s) * scale
                qb, kb, vb = q.astype(BF16), k.astype(BF16), v_ref[sl, hs].astype(BF16)
                kzb = (k * zeta).astype(BF16)
                dov = do_ref[sl, hs]
                dob, doxb = dov.astype(BF16), (dov * xi).astype(BF16)
                stb = st_ref[h, ci].astype(BF16)
                dsn = dstate[h]
                dsnb = dsn.astype(BF16)
                scb = (_dot_nt(qb, kb) * dmat).astype(BF16)
                dscb = (_dot_nt(dob, vb) * dmat).astype(BF16)
                dq = _dot(dscb, kb) + _dot_nt(doxb, stb)
                dk = _dot_tn(dscb, qb) + _dot_nt(vb, dsnb) * zeta
                dv = _dot_tn(scb, dob) + _dot(kzb, dsnb)
                dstate[h] = dsn * g_ref[h][0:1, :] + _dot_tn(qb, doxb)
                dqkv_ref[sl, 128 * h:128 * h + 128] = _rope_t(dq, c, s).astype(BF16)
                dqkv_ref[sl, 512 + 128 * h:640 + 128 * h] = _rope_t(dk * scale, c, s).astype(BF16)
                dqkv_ref[sl, 1024 + 128 * h:1152 + 128 * h] = dv.astype(BF16)
            return carry

        lax.fori_loop(0, nch, chunk, 0)

    col, tab, sq, hv, g8, st, out = _ret_specs(nblk, True)
    return pl.pallas_call(
        body, out_shape=jax.ShapeDtypeStruct((lp, P_RET), BF16), grid=(nblk,),
        in_specs=[col(C_RQ), col(C_RK), col(C_RV), out, st, tab, tab, sq, hv, hv, g8],
        out_specs=pl.BlockSpec((BLK, P_RET), lambda i: (nblk - 1 - i, 0)),
        scratch_shapes=[pltpu.VMEM((RET_HEADS, 128, 128), F32)],
        compiler_params=_cp("arbitrary"), name=name)(proj, proj, proj, do, states, *tables)


def _gla_tables():
    c = GLA_CHUNK
    tri = np.tril(np.ones((c, c), np.float32))
    ones_qv = np.kron(np.eye(GLA_HEADS, dtype=np.float32), np.ones((GLA_DK, GLA_DV), np.float32))
    return (jnp.asarray(tri, BF16), jnp.asarray(tri.T.copy(), BF16), jnp.asarray(ones_qv, BF16),
            jnp.asarray(ones_qv.T.copy(), BF16))


def _tri_sum(tri, x):
    hi = x.astype(BF16)
    lo = (x - hi.astype(F32)).astype(BF16)
    return _dot(tri, hi) + _dot(tri, lo)


def _head_masks(width, per):
    lane = lax.broadcasted_iota(jnp.int32, (1, width), 1)
    return [((lane >= per * h) & (lane < per * (h + 1))).astype(F32) for h in range(GLA_HEADS)]


def _stack_heads(x, masks):
    return jnp.concatenate([x * m for m in masks], axis=0)


def _gla_gate(ga, w2, b, ok, tri):
    z = _dot(ga.astype(BF16), w2) + b
    la = (jnp.minimum(z, 0.0) - jnp.log(1.0 + jnp.exp(-jnp.abs(z)))) * (1.0 / GLA_TAU)
    la = jnp.where(ok, la, 0.0)
    return z, _tri_sum(tri, la)


def _gla_rows(i_blk, ci, lp):
    c = GLA_CHUNK
    rows = i_blk * BLK + ci * c + lax.broadcasted_iota(jnp.int32, (c, 1), 0)
    return (rows >= PADF) & (rows < lp - BACK)


N_SUB = GLA_CHUNK // GLA_SUB - 1
N_SUB2 = GLA_SUB // GLA_SUB2 - 1


def _gla_masks():
    c, s1, s2 = GLA_CHUNK, GLA_SUB, GLA_SUB2
    sh1, sh2 = s1.bit_length() - 1, s2.bit_length() - 1
    r = lax.broadcasted_iota(jnp.int32, (c, GLA_QK), 0)
    blk, within = jnp.right_shift(r, sh1), jnp.bitwise_and(r, s1 - 1)
    grp = jnp.right_shift(within, sh2)
    rowm = [(blk == a).astype(F32) for a in range(1, N_SUB + 1)] + [(grp == b).astype(F32) for b in range(1, N_SUB2 + 1)]
    keym = ([(r < s1 * a).astype(F32) for a in range(1, N_SUB + 1)]
            + [(within < s2 * b).astype(F32) for b in range(1, N_SUB2 + 1)])
    rs = lax.broadcasted_iota(jnp.int32, (GLA_HEADS * c, c), 0)
    ts = lax.broadcasted_iota(jnp.int32, (GLA_HEADS * c, c), 1)
    same = (jnp.right_shift(jnp.bitwise_and(rs, c - 1), sh1) == jnp.right_shift(ts, sh1)).astype(F32)
    lag = [(jnp.bitwise_and(r, s2 - 1) >= j).astype(F32) for j in range(s2)]
    return rowm, keym, same, lag


def _gla_hats(qs, k, g, masks, hm_q):
    c, s1, s2 = GLA_CHUNK, GLA_SUB, GLA_SUB2
    rowm, keym, same, _ = masks
    refs = [g[s1 * a - 1:s1 * a, :] for a in range(1, N_SUB + 1)]
    for b in range(1, N_SUB2 + 1):
        refs.append(jnp.concatenate([jnp.broadcast_to(g[s1 * i + s2 * b - 1:s1 * i + s2 * b, :], (s1, GLA_QK))
                                     for i in range(c // s1)], axis=0))
    eqs = [jnp.exp(jnp.minimum(g - r, 0.0)) * m for r, m in zip(refs, rowm)]
    eks = [jnp.exp(jnp.minimum(r - g, 0.0)) * m for r, m in zip(refs, keym)]
    qhs, khs = [qs * e for e in eqs], [k * e for e in eks]
    qst = [_stack_heads(q, hm_q).astype(BF16) for q in qhs]
    khb = [x.astype(BF16) for x in khs]
    qa, qb = jnp.concatenate(qst[:N_SUB], axis=1), jnp.concatenate(qst[N_SUB:], axis=1)
    ka, kb = jnp.concatenate(khb[:N_SUB], axis=1), jnp.concatenate(khb[N_SUB:], axis=1)
    p = _dot_nt(qa, ka) + _dot_nt(qb, kb) * same
    return eqs, eks, qhs, khs, qa, qb, ka, kb, p


def _roll_rows(x, j):
    return x if j == 0 else pltpu.roll(x, j, 0)


def _gla(proj, w2p, b, tables, *, name):
    lp = proj.shape[0]
    nblk, c, s2 = lp // BLK, GLA_CHUNK, GLA_SUB2
    nch = BLK // c

    def body(q_ref, k_ref, v_ref, a_ref, w_ref, b_ref, tri_ref, ones_ref, o_ref, st_ref, gz_ref, state):
        i_blk = pl.program_id(0)

        @pl.when(i_blk == 0)
        def _():
            state[...] = jnp.zeros_like(state)
        hm_q = _head_masks(GLA_QK, GLA_DK)
        masks = _gla_masks()
        tri, ones_qv, w2, bias = tri_ref[...], ones_ref[...], w_ref[...], b_ref[...]

        def chunk(ci, carry):
            sl = pl.ds(pl.multiple_of(ci * c, c), c)
            ok = _gla_rows(i_blk, ci, lp)
            k, v = k_ref[sl, :], v_ref[sl, :]
            vb = v.astype(BF16)
            qs = q_ref[sl, :] * (GLA_DK ** -0.5)
            z, g = _gla_gate(a_ref[sl, :], w2, bias, ok, tri)
            gz_ref[sl, 0:GLA_QK] = g
            gz_ref[sl, GLA_QK:2 * GLA_QK] = z
            last = g[c - 1:c, :]
            st = state[...]
            st_ref[ci] = st
            qst = _stack_heads(qs * jnp.exp(g), hm_q).astype(BF16)
            oi = _dot_nt(qst, st.astype(BF16))
            o = jnp.concatenate([oi[c * h:c * h + c, :] for h in range(GLA_HEADS)], axis=1)
            ke = k * jnp.exp(last - g)
            f = _dot_tn(vb, ke.astype(BF16))
            upd = f[0:GLA_DV, :] * hm_q[0]
            for h in range(1, GLA_HEADS):
                upd = upd + f[GLA_DV * h:GLA_DV * (h + 1), :] * hm_q[h]
            state[...] = st * jnp.exp(last) + upd
            p = _gla_hats(qs, k, g, masks, hm_q)[-1]
            ob = _dot(p.astype(BF16), vb)
            o = o + jnp.concatenate([ob[c * h:c * h + c, GLA_DV * h:GLA_DV * (h + 1)] for h in range(GLA_HEADS)],
                                    axis=1)
            ws = []
            for j in range(s2):
                ej = jnp.exp(jnp.minimum(g - _roll_rows(g, j), 0.0))
                ws.append((qs * _roll_rows(k, j) * ej * masks[3][j]).astype(BF16))
            ball = _dot(jnp.concatenate(ws, axis=0), ones_qv)
            for j in range(s2):
                o = o + ball[c * j:c * j + c, :] * _roll_rows(v, j)
            o_ref[sl, :] = o
            return carry

        lax.fori_loop(0, nch, chunk, 0)

    tri, _, ones_qv, _ = tables
    full = lambda arr: pl.BlockSpec(arr.shape, lambda i: (0,) * arr.ndim)
    return pl.pallas_call(
        body,
        out_shape=(jax.ShapeDtypeStruct((lp, GLA_V), F32), jax.ShapeDtypeStruct((lp // c, GLA_DV, GLA_QK), F32),
                   jax.ShapeDtypeStruct((lp, 2 * GLA_QK), F32)),
        grid=(nblk,),
        in_specs=[pl.BlockSpec((BLK, GLA_QK), lambda i: (i, C_GQ // GLA_QK)),
                  pl.BlockSpec((BLK, GLA_QK), lambda i: (i, C_GK // GLA_QK)),
                  pl.BlockSpec((BLK, GLA_V), lambda i: (i, C_GV // GLA_V)),
                  pl.BlockSpec((BLK, 128), lambda i: (i, C_GA // 128)),
                  full(w2p), full(b), full(tri), full(ones_qv)],
        out_specs=(pl.BlockSpec((BLK, GLA_V), lambda i: (i, 0)),
                   pl.BlockSpec((nch, GLA_DV, GLA_QK), lambda i: (i, 0, 0)),
                   pl.BlockSpec((BLK, 2 * GLA_QK), lambda i: (i, 0))),
        scratch_shapes=[pltpu.VMEM((GLA_DV, GLA_QK), F32)],
        compiler_params=_cp("arbitrary"), name=name)(proj, proj, proj, proj, w2p, b, tri, ones_qv)


def _gla_bwd(proj, do, states, gz, w2p, tables, *, name):
    lp = proj.shape[0]
    nblk, c, s1, s2 = lp // BLK, GLA_CHUNK, GLA_SUB, GLA_SUB2
    nch = BLK // c

    def body(q_ref, k_ref, v_ref, a_ref, do_ref, st_ref, gz_ref, w_ref, trit_ref, ones_ref, onest_ref,
             dp_ref, dw_ref, db_ref, dstate, dqs_s, dk_s, dg_s, dv_s):
        i_blk = nblk - 1 - pl.program_id(0)

        @pl.when(pl.program_id(0) == 0)
        def _():
            dstate[...] = jnp.zeros_like(dstate)
            dw_ref[...] = jnp.zeros_like(dw_ref)
            db_ref[...] = jnp.zeros_like(db_ref)
        hm_q = _head_masks(GLA_QK, GLA_DK)
        hm_v = _head_masks(GLA_V, GLA_DV)
        masks = _gla_masks()
        trit, ones_qv, ones_vq = trit_ref[...], ones_ref[...], onest_ref[...]
        w2 = w_ref[...]
        rsum = lambda x: jnp.sum(x, axis=0, keepdims=True)

        def chunk(cc, carry):
            ci = nch - 1 - cc
            sl = pl.ds(pl.multiple_of(ci * c, c), c)
            ok = _gla_rows(i_blk, ci, lp)
            k, v, ga = k_ref[sl, :], v_ref[sl, :], a_ref[sl, :]
            vb = v.astype(BF16)
            qs = q_ref[sl, :] * (GLA_DK ** -0.5)
            g, z = gz_ref[sl, 0:GLA_QK], gz_ref[sl, GLA_QK:2 * GLA_QK]
            last = g[c - 1:c, :]
            elast = jnp.exp(last)
            eg = jnp.exp(g)
            ekl = jnp.exp(last - g)
            qe, ke = qs * eg, k * ekl
            dov = do_ref[sl, :]
            st = st_ref[ci]
            dsn = dstate[...]
            qst = _stack_heads(qe, hm_q).astype(BF16)
            dost = jnp.concatenate([dov[:, GLA_DV * h:GLA_DV * (h + 1)] for h in range(GLA_HEADS)], axis=0).astype(BF16)
            dqe_st = _dot(dost, st.astype(BF16))
            dqe = dqe_st[0:c, :] * hm_q[0]
            for h in range(1, GLA_HEADS):
                dqe = dqe + dqe_st[c * h:c * h + c, :] * hm_q[h]
            dstate[...] = _dot_tn(dost, qst) + dsn * elast
            dlast = rsum(dsn * st) * elast
            df = _stack_heads(dsn, hm_q).astype(BF16)
            dv_s[...] = _dot_nt(ke.astype(BF16), df)
            dke = _dot(vb, df)
            xk = dke * ke
            dqs_s[...] = dqe * eg
            dk_s[...] = dke * ekl
            dg_s[...] = dqe * qe - xk
            dlast = dlast + rsum(xk)
            eqs, eks, qhs, khs, qa, qb, ka, kb, p = _gla_hats(qs, k, g, masks, hm_q)
            dost_v = _stack_heads(dov, hm_v).astype(BF16)
            dp = _dot_nt(dost_v, vb)
            dv_s[...] += _dot_tn(p.astype(BF16), dost_v)
            dpa, dpb = dp.astype(BF16), (dp * masks[2]).astype(BF16)
            dq_all = (_dot(dpa, ka), _dot(dpb, kb))
            dk_all = (_dot_tn(dpa, qa), _dot_tn(dpb, qb))
            for t in range(N_SUB + N_SUB2):
                lvl, i = (0, t) if t < N_SUB else (1, t - N_SUB)
                cols = slice(GLA_QK * i, GLA_QK * (i + 1))
                dq_st = dq_all[lvl][:, cols]
                dqh = dq_st[0:c, :] * hm_q[0]
                for h in range(1, GLA_HEADS):
                    dqh = dqh + dq_st[c * h:c * h + c, :] * hm_q[h]
                dkh = dk_all[lvl][:, cols]
                xq, xkh = dqh * qhs[t], dkh * khs[t]
                dqs_s[...] += dqh * eqs[t]
                dk_s[...] += dkh * eks[t]
                dg_s[...] += xq - xkh
                back_ref = xkh - xq
                if lvl == 0:
                    row = s1 * (i + 1) - 1
                    dg_s[row:row + 1, :] += rsum(back_ref)
                else:
                    for blk in range(c // s1):
                        row = s1 * blk + s2 * (i + 1) - 1
                        dg_s[row:row + 1, :] += rsum(back_ref[s1 * blk:s1 * blk + s1, :])
            kes, qes, ws, dbs = [], [], [], []
            for j in range(s2):
                em = jnp.exp(jnp.minimum(g - _roll_rows(g, j), 0.0)) * masks[3][j]
                kes.append(_roll_rows(k, j) * em)
                qes.append(qs * em)
                ws.append((qs * kes[j]).astype(BF16))
                dbs.append((dov * _roll_rows(v, j)).astype(BF16))
            ball = _dot(jnp.concatenate(ws, axis=0), ones_qv)
            dwall = _dot(jnp.concatenate(dbs, axis=0), ones_vq)
            for j in range(s2):
                back = (lambda x: x) if j == 0 else (lambda x, j=j: pltpu.roll(x, c - j, 0))
                dw = dwall[c * j:c * j + c, :]
                dv_s[...] += back(ball[c * j:c * j + c, :] * dov)
                dqs_s[...] += dw * kes[j]
                dk_s[...] += back(dw * qes[j])
                x = dw * qs * kes[j]
                dg_s[...] += x - back(x)
            dg_s[c - 1:c, :] += dlast
            dla = jnp.where(ok, _tri_sum(trit, dg_s[...]), 0.0)
            dz = dla * (1.0 / GLA_TAU) / (1.0 + jnp.exp(z))
            dzb = dz.astype(BF16)
            dp_ref[sl, 0:256] = (dqs_s[...] * (GLA_DK ** -0.5)).astype(BF16)
            dp_ref[sl, 256:512] = dk_s[...].astype(BF16)
            dp_ref[sl, 512:1024] = dv_s[...].astype(BF16)
            dp_ref[sl, 1024:1152] = _dot_nt(dzb, w2).astype(BF16)
            dp_ref[sl, 1152:1280] = jnp.zeros((c, 128), BF16)
            dw_ref[...] += _dot_tn(ga.astype(BF16), dzb)
            db_ref[...] += rsum(dz)
            return carry

        lax.fori_loop(0, nch, chunk, 0)

    tri, trit, ones_qv, ones_vq = tables
    full = lambda arr: pl.BlockSpec(arr.shape, lambda i: (0,) * arr.ndim)
    rev = lambda i: nblk - 1 - i
    return pl.pallas_call(
        body,
        out_shape=(jax.ShapeDtypeStruct((lp, P_GLA), BF16),
                   jax.ShapeDtypeStruct((128, GLA_QK), F32), jax.ShapeDtypeStruct((1, GLA_QK), F32)),
        grid=(nblk,),
        in_specs=[pl.BlockSpec((BLK, GLA_QK), lambda i: (rev(i), C_GQ // GLA_QK)),
                  pl.BlockSpec((BLK, GLA_QK), lambda i: (rev(i), C_GK // GLA_QK)),
                  pl.BlockSpec((BLK, GLA_V), lambda i: (rev(i), C_GV // GLA_V)),
                  pl.BlockSpec((BLK, 128), lambda i: (rev(i), C_GA // 128)),
                  pl.BlockSpec((BLK, GLA_V), lambda i: (rev(i), 0)),
                  pl.BlockSpec((nch, GLA_DV, GLA_QK), lambda i: (rev(i), 0, 0)),
                  pl.BlockSpec((BLK, 2 * GLA_QK), lambda i: (rev(i), 0)),
                  full(w2p), full(trit), full(ones_qv), full(ones_vq)],
        out_specs=(pl.BlockSpec((BLK, P_GLA), lambda i: (rev(i), 0)),
                   pl.BlockSpec((128, GLA_QK), lambda i: (0, 0)),
                   pl.BlockSpec((1, GLA_QK), lambda i: (0, 0))),
        scratch_shapes=[pltpu.VMEM((GLA_DV, GLA_QK), F32), pltpu.VMEM((c, GLA_QK), F32),
                        pltpu.VMEM((c, GLA_QK), F32), pltpu.VMEM((c, GLA_QK), F32), pltpu.VMEM((c, GLA_V), F32)],
        compiler_params=_cp("arbitrary"), name=name)(proj, proj, proj, proj, do, states, gz, w2p, trit, ones_qv, ones_vq)


def _as2d(a):
    return a.reshape(-1, a.shape[-1])


def _ew_tile(r):
    return _tile(r, (512, 256, 128, 64, 32, 16, 8))


def _add2(a, b, *, out_dtype, name):
    a2, b2 = _as2d(a), _as2d(b)
    r, n = a2.shape
    tm = _ew_tile(r)

    def body(a_ref, b_ref, o_ref):
        o_ref[...] = (a_ref[...] + b_ref[...]).astype(o_ref.dtype)

    blk = pl.BlockSpec((tm, n), lambda i: (i, 0))
    return pl.pallas_call(body, out_shape=jax.ShapeDtypeStruct((r, n), out_dtype), grid=(r // tm,), in_specs=[blk, blk],
                          out_specs=blk, compiler_params=_cp("parallel"), name=name)(a2, b2).reshape(a.shape)


def _sum_slots(own, q, *, name):
    shape = own.shape
    q3 = q.reshape(3, -1, shape[-1])
    own2 = _as2d(own)
    r, n = own2.shape
    tm = _ew_tile(r)

    def body(own_ref, q_ref, o_ref):
        f = lambda i: q_ref[i].astype(F32)
        o_ref[...] = ((own_ref[...].astype(F32) + f(0)) + f(1)) + f(2)

    blk = pl.BlockSpec((tm, n), lambda i: (i, 0))
    return pl.pallas_call(
        body, out_shape=jax.ShapeDtypeStruct((r, n), F32), grid=(r // tm,),
        in_specs=[blk, pl.BlockSpec((3, tm, n), lambda i: (0, i, 0))], out_specs=blk,
        compiler_params=_cp("parallel"), name=name)(own2, q3).reshape(shape)


def _adamw(w, g, m, v, *, name):
    shape = w.shape
    w2, g2, m2, v2 = _as2d(w), _as2d(g), _as2d(m), _as2d(v)
    r, n = w2.shape
    tm = _ew_tile(r)

    def body(w_ref, g_ref, m_ref, v_ref, d_ref, mo_ref, vo_ref):
        d_ref[...], mo_ref[...], vo_ref[...] = _adam_math(w_ref[...], g_ref[...], m_ref[...], v_ref[...])

    blk = pl.BlockSpec((tm, n), lambda i: (i, 0))
    o = jax.ShapeDtypeStruct((r, n), F32)
    d, mo, vo = pl.pallas_call(body, out_shape=(o, o, o), grid=(r // tm,), in_specs=[blk] * 4, out_specs=(blk,) * 3,
                               compiler_params=_cp("parallel"), name=name)(w2, g2, m2, v2)
    return d.reshape(shape), mo.reshape(shape), vo.reshape(shape)


def _adam_math(w, gv, m, v):
    c1 = 1.0 - ADAM_B1 ** ADAM_STEP
    c2 = 1.0 - ADAM_B2 ** ADAM_STEP
    mn = ADAM_B1 * m + (1.0 - ADAM_B1) * gv
    vn = ADAM_B2 * v + (1.0 - ADAM_B2) * (gv * gv)
    return -ADAM_LR * ((mn / c1) / (jnp.sqrt(vn / c2) + ADAM_EPS) + ADAM_WD * w), mn, vn


def _adamw_halves(w, m, v, mine, theirs, c, *, name):
    depth, rows, n = w.shape
    r2 = rows // 2
    tm = next(t for t in range(min(r2, 256), 0, -8) if r2 % t == 0)
    steps = r2 // tm

    def body(c_ref, w_ref, m_ref, v_ref, *rest):
        halves, (g_ref, d_ref, mo_ref, vo_ref) = rest[:2 * depth], rest[2 * depth:]
        l, h = pl.program_id(0), pl.program_id(1)
        gv = None
        for k in range(depth):
            gk = jnp.where(h == c_ref[0], halves[2 * k][...], halves[2 * k + 1][...])
            gv = gk if gv is None else jnp.where(l == k, gk, gv)
        g_ref[...] = gv
        d_ref[...], mo_ref[...], vo_ref[...] = _adam_math(w_ref[...], gv, m_ref[...], v_ref[...])

    big = pl.BlockSpec((tm, n), lambda l, h, i, c_ref: ((2 * l + h) * steps + i, 0))
    half = lambda k: pl.BlockSpec((tm, n), lambda l, h, i, c_ref: (jnp.where(l == k, i, 0), 0))
    o = jax.ShapeDtypeStruct((depth * rows, n), F32)
    args = [a for k in range(depth) for a in (mine[k], theirs[k])]
    outs = pl.pallas_call(
        body, out_shape=(o, o, o, o),
        grid_spec=pltpu.PrefetchScalarGridSpec(
            num_scalar_prefetch=1, grid=(depth, 2, steps),
            in_specs=[big, big, big] + [half(k) for k in range(depth) for _ in range(2)], out_specs=(big,) * 4),
        compiler_params=_cp("arbitrary", "arbitrary", "arbitrary"), name=name)(
            jnp.reshape(c, (1,)).astype(jnp.int32), _as2d(w), _as2d(m), _as2d(v), *args)
    return [a.reshape(w.shape) for a in outs]


ANY = pl.BlockSpec(memory_space=pl.ANY)


def _place():
    return lax.axis_index("x"), lax.axis_index("y"), lax.axis_index("c")


def _other_chips(x, y):
    return [(1 - x, y), (x, 1 - y), (1 - x, 1 - y)]


def _remote(src, dst, ssem, rsem, dev):
    return pltpu.make_async_remote_copy(src_ref=src, dst_ref=dst, send_sem=ssem, recv_sem=rsem, device_id=dev,
                                        device_id_type=MESH)


def _allgather_chips(arrs, *, name):
    n = len(arrs)

    def body(*refs):
        ins, outs = refs[:n], refs[n:2 * n]
        s1, r1, s2, r2 = refs[2 * n:]
        x, y, c = _place()
        q = 2 * x + y
        chips = _other_chips(x, y)
        qs = [2 * cx + cy for cx, cy in chips]
        sib = (x, y, 1 - c)
        first, passed = [], []
        for k in range(n):
            for j, chip in enumerate(chips):
                first.append(_remote(ins[k].at[c], outs[k].at[c, q], s1.at[k, j], r1.at[k, j], (*chip, c)))
        for cp in first:
            cp.start()
        for k in range(n):
            for j, chip in enumerate(chips):
                land = outs[k].at[c, qs[j]]
                _remote(land, land, s1.at[k, j], r1.at[k, j], (*chip, c)).wait_recv()
                fw = _remote(land, land, s2.at[k, j], r2.at[k, j], sib)
                fw.start()
                passed.append(fw)
        for k in range(n):
            for j in range(3):
                land = outs[k].at[1 - c, qs[j]]
                _remote(land, land, s2.at[k, j], r2.at[k, j], sib).wait_recv()
        for cp in first + passed:
            cp.wait_send()

    sem = pltpu.SemaphoreType.DMA
    outs = pl.pallas_call(
        body, out_shape=tuple(jax.ShapeDtypeStruct((2, 4) + a.shape[1:], a.dtype) for a in arrs),
        in_specs=[ANY] * n, out_specs=(ANY,) * n,
        scratch_shapes=[sem((n, 3)), sem((n, 3)), sem((n, 3)), sem((n, 3))], name=name)(*arrs)
    chip = 2 * lax.axis_index("x") + lax.axis_index("y")
    return [lax.dynamic_update_slice_in_dim(o, a[:, None], chip, axis=1) for o, a in zip(outs, arrs)]


def _pair_exchange(arrs, *, name):
    n = len(arrs)

    def body(*refs):
        ins, outs = refs[:n], refs[n:2 * n]
        ssem, rsem = refs[2 * n:]
        x, y, c = _place()
        cps = [_remote(ins[k].at[:, 1 - c], outs[k], ssem.at[k], rsem.at[k], (x, y, 1 - c)) for k in range(n)]
        for cp in cps:
            cp.start()
        for cp in cps:
            cp.wait()

    sem = pltpu.SemaphoreType.DMA
    return pl.pallas_call(
        body, out_shape=tuple(jax.ShapeDtypeStruct((a.shape[0],) + a.shape[2:], a.dtype) for a in arrs),
        in_specs=[ANY] * n, out_specs=(ANY,) * n, scratch_shapes=[sem((n,)), sem((n,))], name=name)(*arrs)


def _pair_sum(mine, theirs, c, *, name):
    _, _, r, n = mine.shape
    tm = r if r <= 512 else _ew_tile(r)

    def body(c_ref, a_ref, b_ref, o_ref):
        o_ref[...] = (a_ref[...] + b_ref[...]).astype(BF16)

    blk = pl.BlockSpec((None, tm, n), lambda s, i, c_ref: (s, i, 0))
    return pl.pallas_call(
        body, out_shape=jax.ShapeDtypeStruct((4, r, n), BF16),
        grid_spec=pltpu.PrefetchScalarGridSpec(
            num_scalar_prefetch=1, grid=(4, r // tm),
            in_specs=[pl.BlockSpec((None, None, tm, n), lambda s, i, c_ref: (s, c_ref[0], i, 0)), blk], out_specs=blk),
        compiler_params=_cp("parallel", "parallel"), name=name)(jnp.reshape(c, (1,)).astype(jnp.int32), mine, theirs)


def _chip_copies(ins, outs, ssem, rsem, mode):
    x, y, c = _place()
    q = 2 * x + y
    sends, recvs = [], []
    for k in range(len(ins)):
        for j, (cx, cy) in enumerate(_other_chips(x, y)):
            sem = (ssem.at[k, j], rsem.at[k, j], (cx, cy, c))
            if mode == "scatter":
                sends.append(_remote(ins[k].at[2 * cx + cy], outs[k].at[j], *sem))
                recvs.append(sends[-1])
            else:
                sends.append(_remote(ins[k].at[c], outs[k].at[2 * q + c], *sem))
                recvs.append(_remote(ins[k].at[c], outs[k].at[2 * (2 * cx + cy) + c], *sem))
    return sends, recvs


def _chip_wait(sends, recvs):
    for cp in sends:
        cp.wait_send()
    for cp in recvs:
        cp.wait_recv()


def _landing_shape(a, mode):
    return jax.ShapeDtypeStruct(((3,) if mode == "scatter" else (8,)) + a.shape[1:], a.dtype)


def _chip_exchange(arrs, mode, *, name):
    n = len(arrs)

    def body(*refs):
        ins, outs = refs[:n], refs[n:2 * n]
        ssem, rsem = refs[2 * n:]
        sends, recvs = _chip_copies(ins, outs, ssem, rsem, mode)
        for cp in sends:
            cp.start()
        _chip_wait(sends, recvs)

    sem = pltpu.SemaphoreType.DMA
    return list(pl.pallas_call(
        body, out_shape=tuple(_landing_shape(a, mode) for a in arrs),
        in_specs=[ANY] * n, out_specs=(ANY,) * n, scratch_shapes=[sem((n, 3)), sem((n, 3))], name=name)(*arrs))


def _pair_fill(bufs, owns, *, name):
    n = len(bufs)

    def body(*refs):
        own, outs = refs[n:2 * n], refs[2 * n:3 * n]
        ssem, rsem = refs[3 * n:]
        x, y, c = _place()
        q = 2 * x + y
        sib = (x, y, 1 - c)
        sends, recvs = [], []
        for k in range(n):
            for j, (cx, cy) in enumerate(_other_chips(x, y)):
                mine, theirs = outs[k].at[2 * (2 * cx + cy) + c], outs[k].at[2 * (2 * cx + cy) + 1 - c]
                sends.append(_remote(mine, mine, ssem.at[k, j], rsem.at[k, j], sib))
                recvs.append(_remote(mine, theirs, ssem.at[k, j], rsem.at[k, j], sib))
            slots = outs[k].at[pl.ds(2 * q, 2)]
            sends.append(_remote(own[k], slots, ssem.at[k, 3], rsem.at[k, 3], sib))
            recvs.append(sends[-1])
        for cp in sends:
            cp.start()
        _chip_wait(sends, recvs)

    sem = pltpu.SemaphoreType.DMA
    return list(pl.pallas_call(
        body, out_shape=tuple(jax.ShapeDtypeStruct(b.shape, b.dtype) for b in bufs),
        in_specs=[ANY] * (2 * n), out_specs=(ANY,) * n, scratch_shapes=[sem((n, 4)), sem((n, 4))],
        input_output_aliases={k: k for k in range(n)}, name=name)(*bufs, *owns))


def _pair_swap(arrs, *, name):
    n = len(arrs)

    def body(*refs):
        ins, outs = refs[:n], refs[n:2 * n]
        ssem, rsem = refs[2 * n:]
        x, y, c = _place()
        cps = [_remote(ins[k], outs[k], ssem.at[k], rsem.at[k], (x, y, 1 - c)) for k in range(n)]
        for cp in cps:
            cp.start()
        for cp in cps:
            cp.wait()

    sem = pltpu.SemaphoreType.DMA
    return pl.pallas_call(
        body, out_shape=tuple(jax.ShapeDtypeStruct(a.shape, a.dtype) for a in arrs),
        in_specs=[ANY] * n, out_specs=(ANY,) * n, scratch_shapes=[sem((n,)), sem((n,))], name=name)(*arrs)


def _allreduce_small(slab, *, name):
    r, n = slab.shape

    def body(x_ref, o_ref, buf, ssem, rsem):
        x, y, c = _place()
        me = 4 * x + 2 * y + c
        buf[me] = x_ref[...]
        cps = []
        for rel in range(1, 8):
            bx, by, bc = (rel >> 2) & 1, (rel >> 1) & 1, rel & 1
            px, py, pc = (x + bx) % 2, (y + by) % 2, (c + bc) % 2
            cps.append((_remote(x_ref, buf.at[me], ssem.at[rel - 1], rsem.at[rel - 1], (px, py, pc)),
                        4 * px + 2 * py + pc, (px, py, pc)))
        for cp, _, _ in cps:
            cp.start()
        for rel, (cp, peer, dev) in enumerate(cps):
            cp.wait_send()
            _remote(x_ref, buf.at[peer], ssem.at[rel], rsem.at[rel], dev).wait_recv()
        acc = buf[0]
        for k in range(1, 8):
            acc = acc + buf[k]
        o_ref[...] = acc

    vm = pl.BlockSpec(memory_space=pltpu.VMEM)
    sem = pltpu.SemaphoreType.DMA
    return pl.pallas_call(
        body, out_shape=jax.ShapeDtypeStruct((r, n), F32), in_specs=[vm], out_specs=vm,
        scratch_shapes=[pltpu.VMEM((8, r, n), F32), sem((7,)), sem((7,))], name=name)(slab)


def _slab(arrs, row_mult):
    flat = jnp.concatenate([a.reshape(-1) for a in arrs])
    unit = 128 * row_mult
    total = -(-flat.size // unit) * unit
    return jnp.pad(flat, (0, total - flat.size)).reshape(-1, 128)


def _unslab(slab, shapes):
    flat = slab.reshape(-1)
    out, off = [], 0
    for s in shapes:
        size = int(np.prod(s))
        out.append(flat[off:off + size].reshape(s))
        off += size
    return out


def _cols_from_chips(a):
    return jnp.transpose(a, (1, 0, 2)).reshape(a.shape[1], -1)


def _cols_to_chips(a, parts):
    r = a.shape[0]
    return jnp.transpose(a.reshape(r, parts, -1), (1, 0, 2))


BIG = ("w_in", "w_out", "up", "down")
GATHER_RIDES = {("proj", 0): (("w_out", 0), ("up", 0)), ("mix_out", 0): (("down", 0),),
                ("ffn_fwd", 0): (("w_in", 1), ("w_out", 1), ("up", 1), ("down", 1))}
REDUCE_RIDES = {("ffn_up_a_dw", 0): (("up",), 1), ("ffn_down_dw", 0): (("w_in", "w_out"), 1),
                ("mix_out_dx", 0): (("down",), 1),
                ("proj_dx", 0): (("up",), 0), ("proj_dw_0", 0): (("down",), 0), ("proj_dw_1", 0): (("w_out",), 0)}


class _LocalWeights:
    def __init__(self, meta, win, wout, up_a, up_g, down, w2p, cw):
        self._meta, self._w = meta, {"win": win, "wout": wout, "up_a": up_a, "up_g": up_g, "down": down, "w2p": w2p,
                                     "cw": cw}

    def meta(self):
        return self._meta

    def get(self, kind, l):
        return self._w[kind][l]

    def mm(self, site, l, a, b, fn=None, **kw):
        return (fn or _mm)(a, b, name=site, **kw)

    def ffn_fwd(self, l, h, m, w_post, w_next, wa, wg, ba, bg, after=None):
        return _ffn_fwd(h, m, w_post, w_next, self.get("up_a", l), self.get("up_g", l), wa, wg, ba, bg,
                        self.get("down", l), name="ffn_fwd", after=after)[0]

    def merge_mix(self, l, o_ret, o_gla, proj, w_ret, w_gla):
        return _merge_mix_out(o_ret, o_gla, proj, w_ret, w_gla, self.get("wout", l), name="mix_out")[0]

    def merge_bwd(self, l, dm, o_ret, o_gla, proj, w_ret, w_gla):
        return _merge_bwd(dm, self.get("wout", l), o_ret, o_gla, proj, w_ret, w_gla, name="mix_out_dx")[0]

    def grads_done(self, l, g, kinds):
        pass


class _ChipWeights:
    def __init__(self, w_in, w_out, ffn_up, ffn_down, meta_tokens, gla_gate_w2, ffn_conv_w):
        self.x, self.y, self.c = _place()
        self.q = 2 * self.x + self.y
        halves = lambda a: a.astype(BF16).reshape(2, a.shape[0] // 2, a.shape[1])
        self.own = {(k, l): halves(a[l]) for k, a in zip(BIG, (w_in, w_out, ffn_up, ffn_down)) for l in range(DEPTH)}
        self.landed, self.swapped, self.full, self.n_swaps = {}, {}, {}, 0
        self.sh_shapes = [meta_tokens.shape, gla_gate_w2.shape, ffn_conv_w.shape]
        self.own["small", 0] = _slab([meta_tokens, gla_gate_w2, ffn_conv_w], 16).reshape(2, -1, 128)
        first = [("w_in", 0), ("small", 0)]
        for key, arr in zip(first, _chip_exchange([self.own[k] for k in first], "bcast", name="gather_first")):
            self.landed[key] = arr
        sh = self._whole("small", 0).reshape(4, -1, 128)
        parts = [_unslab(sh[k], self.sh_shapes) for k in range(4)]
        self._meta = jnp.concatenate([p[0] for p in parts], axis=-1)
        self.w2 = jnp.concatenate([p[1] for p in parts], axis=-1)
        self.cw = jnp.concatenate([p[2] for p in parts], axis=-1)
        self.partial, self.slots = {}, {}

    def _whole(self, kind, l):
        if (kind, l) not in self.full:
            keys = [k for k in self.landed if k not in self.full]
            got = _pair_fill([self.landed[k] for k in keys], [self.own[k] for k in keys],
                             name=f"gather_fill_{self.n_swaps}")
            self.n_swaps += 1
            for k, buf in zip(keys, got):
                self.full[k] = buf.reshape(4, 2 * buf.shape[1], buf.shape[2])
        return self.full[kind, l]

    def meta(self):
        return self._meta

    def get(self, kind, l):
        if kind == "win":
            return _to_kernel_cols(_cols_from_chips(self._whole("w_in", l)))
        if kind == "wout":
            return self._whole("w_out", l).reshape(D_MODEL, D_MODEL)
        if kind == "up_a":
            return _cols_from_chips(self._whole("up", l)[0:2])
        if kind == "up_g":
            return _cols_from_chips(self._whole("up", l)[2:4])
        if kind == "down":
            return self._whole("down", l).reshape(D_FF, D_MODEL)
        if kind == "w2p":
            return jnp.pad(self.w2[l], ((0, 128 - GLA_RANK), (0, 0))).astype(BF16)
        return self.cw[l]

    def mm(self, site, l, a, b, fn=None, **kw):
        fn = fn or _mm
        if (site, l) in GATHER_RIDES:
            keys = GATHER_RIDES[site, l]
            out, got = fn(a, b, name=site, carry=([self.own[k] for k in keys], "bcast"), **kw)
            self.landed.update(zip(keys, got))
            return out
        if (site, l) in REDUCE_RIDES:
            kinds, gl = REDUCE_RIDES[site, l]
            keys = [(k, gl) for k in kinds]
            if all(k in self.partial and k not in self.slots for k in keys):
                out, got = fn(a, b, name=site, carry=([self.partial[k] for k in keys], "scatter"), **kw)
                self.slots.update(zip(keys, got))
                return out
        return fn(a, b, name=site, **kw)

    def merge_bwd(self, l, dm, o_ret, o_gla, proj, w_ret, w_gla):
        carry, keys = None, []
        if ("mix_out_dx", l) in REDUCE_RIDES:
            kinds, gl = REDUCE_RIDES["mix_out_dx", l]
            keys = [(k, gl) for k in kinds]
            if all(k in self.partial and k not in self.slots for k in keys):
                carry = ([self.partial[k] for k in keys], "scatter")
        outs, got = _merge_bwd(dm, self.get("wout", l), o_ret, o_gla, proj, w_ret, w_gla, name="mix_out_dx",
                               carry=carry)
        if carry is not None:
            self.slots.update(zip(keys, got))
        return outs

    def merge_mix(self, l, o_ret, o_gla, proj, w_ret, w_gla):
        keys = GATHER_RIDES.get(("mix_out", l), ())
        outs, got = _merge_mix_out(o_ret, o_gla, proj, w_ret, w_gla, self.get("wout", l), name="mix_out",
                                   carry=([self.own[k] for k in keys], "bcast") if keys else None)
        self.landed.update(zip(keys, got))
        return outs

    def ffn_fwd(self, l, h, m, w_post, w_next, wa, wg, ba, bg, after=None):
        keys = GATHER_RIDES.get(("ffn_fwd", l), ())
        outs, got = _ffn_fwd(h, m, w_post, w_next, self.get("up_a", l), self.get("up_g", l), wa, wg, ba, bg,
                             self.get("down", l), name="ffn_fwd", after=after,
                             carry=([self.own[k] for k in keys], "bcast") if keys else None)
        self.landed.update(zip(keys, got))
        return outs

    def grads_done(self, l, g, kinds):
        split = lambda a: a.reshape(4, 2, a.shape[-2] // 2, a.shape[-1]) if a.ndim == 3 else \
            a.reshape(4, 2, a.shape[0] // 8, a.shape[1])
        src = {"w_in": lambda: g["w_in"][l], "w_out": lambda: g["w_out"][l],
               "up": lambda: g["up"][l], "down": lambda: g["down"][l]}
        big = {k: split(src[k]()) for k in kinds}
        from_sib = _pair_exchange([big[k] for k in kinds], name=f"grads_pair_exchange_{l}_{kinds[0]}")
        for k, theirs in zip(kinds, from_sib):
            self.partial[k, l] = _pair_sum(big[k], theirs, self.c, name=f"pair_sum_{k}_{l}")

    def reduce(self):
        keys = [(k, l) for l in range(DEPTH) for k in BIG]
        late = [k for k in keys if k not in self.slots]
        self.slots.update(zip(late, _chip_exchange([self.partial[k] for k in late], "scatter",
                                                   name="grads_chip_exchange")))
        half = {}
        for k in keys:
            own = lax.dynamic_index_in_dim(self.partial[k], self.q, 0, keepdims=False)
            half[k] = _sum_slots(own, self.slots[k], name=f"chip_sum_{k[0]}_{k[1]}")
        other = dict(zip(keys, _pair_swap([half[k] for k in keys], name="grads_pair_swap")))
        return [([half[k, l] for l in range(DEPTH)], [other[k, l] for l in range(DEPTH)]) for k in BIG]


def _local_step(x_rows, target_rows, wts, pre_mix_norm, gla_gate_b, ret_norm_w, gla_norm_w, post_mix_norm,
                pre_ffn_norm, ffn_conv_b, post_ffn_norm):
    d = D_MODEL
    lp = x_rows.shape[0] + FRONT + BACK
    row = lambda a, l: a[l][None, :]
    rtab = _ret_tables(lp)
    gtab = _gla_tables()
    h0 = jnp.concatenate([jnp.zeros((PADF, d), F32), wts.meta(), x_rows, jnp.zeros((BACK, d), F32)], axis=0)
    target = jnp.pad(target_rows, ((FRONT, BACK), (0, 0)))

    saved = []
    h = h0
    _, hn = _resid_norm(h0, None, None, row(pre_mix_norm, 0), name="norm_in")
    loss_local = dy = None
    for l in range(DEPTH):
        s = {"h_in": h, "hn": hn}
        s["proj"] = wts.mm("proj", l, hn, wts.get("win", l))
        s["o_ret"], s["st_ret"] = _retention(s["proj"], rtab, name="retention")
        s["o_gla"], s["st_gla"], s["gz"] = _gla(s["proj"], wts.get("w2p", l), row(gla_gate_b, l), gtab, name="gla")
        s["merged"], s["m"] = wts.merge_mix(l, s["o_ret"], s["o_gla"], s["proj"], row(ret_norm_w, l),
                                            row(gla_norm_w, l))
        cw_a, cw_g = wts.get("cw", l)[:, :D_FF], wts.get("cw", l)[:, D_FF:]
        cb_a, cb_g = ffn_conv_b[l][None, :D_FF], ffn_conv_b[l][None, D_FF:]
        s["conv"] = (cw_a, cw_g, cb_a, cb_g)
        after = (row(post_ffn_norm, l), row(pre_mix_norm, l + 1)) if l + 1 < DEPTH else None
        outs = wts.ffn_fwd(l, h, s["m"], row(post_mix_norm, l), row(pre_ffn_norm, l), cw_a, cw_g, cb_a, cb_g,
                           after=after)
        s["h_mid"], s["hn2"], s["ua"], s["ug"], s["act"], s["f"] = outs[:6]
        if l + 1 < DEPTH:
            h, hn = outs[6], outs[7]
        else:
            loss_local, dy, df_last, dw_last = _loss_head(s["h_mid"], s["f"], row(post_ffn_norm, l), target,
                                                          name="loss_head")
        saved.append(s)

    g = {k: [None] * DEPTH for k in ("pre_mix", "w_in", "w2", "gb", "ret_n", "gla_n", "w_out", "post_mix", "pre_ffn",
                                     "up", "cw", "cb", "down", "post_ffn")}
    dh_out, dhn_next = dy, None
    for l in reversed(range(DEPTH)):
        s = saved[l]
        cw_a, cw_g, cb_a, cb_g = s["conv"]
        if l + 1 < DEPTH:
            dh, df, g["pre_mix"][l + 1], g["post_ffn"][l] = _resid_norm_bwd(
                dh_out, dhn_next, saved[l + 1]["h_in"], s["f"], row(pre_mix_norm, l + 1), row(post_ffn_norm, l),
                name="resid_ffn_bwd")
        else:
            dh, df, g["post_ffn"][l] = dh_out, df_last, dw_last
        g["down"][l] = wts.mm("ffn_down_dw", l, s["act"], df, fn=_mm_tn, tn=512)
        du_a, du_g, dcw_a, dcw_g, dcb_a, dcb_g, dhn2 = _conv_act_bwd(
            s["ua"], s["ug"], df, wts.get("down", l), cw_a, cw_g, cb_a, cb_g, wts.get("up_a", l), wts.get("up_g", l),
            name="conv_act_bwd")
        g["cw"][l] = jnp.concatenate([dcw_a, dcw_g], axis=1)
        g["cb"][l] = jnp.concatenate([dcb_a, dcb_g], axis=1)[0]
        half_up = wts.mm("ffn_up_a_dw", l, s["hn2"], du_a, fn=_mm_tn, tn=D_FF // 2, blocks=(4, 0))
        g["up"][l] = _mm_tn(s["hn2"], du_g, tn=D_FF // 2, blocks=(4, 2), into=half_up, name="ffn_up_g_dw")
        dh, dm, g["pre_ffn"][l], g["post_mix"][l] = _resid_norm_bwd(
            dh, dhn2, s["h_mid"], s["m"], row(pre_ffn_norm, l), row(post_mix_norm, l), name="resid_mix_bwd")
        g["w_out"][l] = _mm_tn(s["merged"], dm, name="mix_out_dw")
        wts.grads_done(l, g, ("w_out", "up", "down"))
        do_ret, do_gla, d_gate, g["ret_n"][l], g["gla_n"][l] = wts.merge_bwd(
            l, dm, s["o_ret"], s["o_gla"], s["proj"], row(ret_norm_w, l), row(gla_norm_w, l))
        d_ret = _retention_bwd(s["proj"], do_ret, s["st_ret"], rtab, name="retention_bwd")
        d_gla, dw2, dgb = _gla_bwd(s["proj"], do_gla, s["st_gla"], s["gz"], wts.get("w2p", l), gtab, name="gla_bwd")
        g["w2"][l], g["gb"][l] = dw2[:GLA_RANK], dgb[0]
        pieces = (d_ret, d_gate, d_gla)
        g["w_in"][l] = _to_reference_chips(*[wts.mm(f"proj_dw_{i}", l, s["hn"], p, fn=_mm_tn)
                                             for i, p in enumerate(pieces)])
        win = wts.get("win", l)
        dhn_next = wts.mm("proj_dx", l, pieces, [win[:, 0:P_RET], win[:, P_RET:P_RET + P_GATE], win[:, P_RET + P_GATE:]],
                          fn=_mm_nt_sum)
        dh_out = dh
        wts.grads_done(l, g, ("w_in",))
    dh0, _, g["pre_mix"][0], _ = _resid_norm_bwd(dh_out, dhn_next, h0, None, row(pre_mix_norm, 0), None,
                                                 name="norm_in_bwd")
    return loss_local, dh0, g


def kernel(x, meta_tokens, pre_mix_norm, w_in, gla_gate_w2, gla_gate_b, ret_norm_w, gla_norm_w, w_out, post_mix_norm, pre_ffn_norm, ffn_up, ffn_conv_w, ffn_conv_b, ffn_down, post_ffn_norm, loss_target, m_meta_tokens, m_pre_mix_norm, m_w_in, m_gla_gate_w2, m_gla_gate_b, m_ret_norm_w, m_gla_norm_w, m_w_out, m_post_mix_norm, m_pre_ffn_norm, m_ffn_up, m_ffn_conv_w, m_ffn_conv_b, m_ffn_down, m_post_ffn_norm, v_meta_tokens, v_pre_mix_norm, v_w_in, v_gla_gate_w2, v_gla_gate_b, v_ret_norm_w, v_gla_norm_w, v_w_out, v_post_mix_norm, v_pre_ffn_norm, v_ffn_up, v_ffn_conv_w, v_ffn_conv_b, v_ffn_down, v_post_ffn_norm):
    xi, yi, ci = _place()
    chip = 2 * xi + yi
    seq = x.shape[1]
    d = D_MODEL
    wts = _ChipWeights(w_in, w_out, ffn_up, ffn_down, meta_tokens, gla_gate_w2, ffn_conv_w)
    loss_local, dh0, g = _local_step(x[0], loss_target[0], wts, pre_mix_norm, gla_gate_b, ret_norm_w, gla_norm_w,
                                     post_mix_norm, pre_ffn_norm, ffn_conv_b, post_ffn_norm)
    grad_x = dh0[FRONT:FRONT + seq][None]
    names = ("w_in", "w_out", "ffn_up", "ffn_down")
    big_halves = wts.reduce()

    small_full = [dh0[PADF:FRONT], jnp.stack(g["pre_mix"])[:, 0], jnp.stack(g["w2"]), jnp.stack(g["gb"]),
                  jnp.stack(g["ret_n"])[:, 0], jnp.stack(g["gla_n"])[:, 0], jnp.stack(g["post_mix"])[:, 0],
                  jnp.stack(g["pre_ffn"])[:, 0], jnp.stack(g["cw"]), jnp.stack(g["cb"]),
                  jnp.stack(g["post_ffn"])[:, 0]]
    small_sum = _unslab(_allreduce_small(_slab(small_full, 8), name="small_allreduce"), [a.shape for a in small_full])
    (g_meta, g_pre_mix, g_w2, g_gb, g_ret_n, g_gla_n, g_post_mix, g_pre_ffn, g_cw, g_cb, g_post_ffn) = small_sum
    g_meta = lax.dynamic_slice_in_dim(g_meta, chip * 256, 256, axis=1)
    g_w2 = lax.dynamic_slice_in_dim(g_w2, chip * 64, 64, axis=2)
    g_cw = lax.dynamic_slice_in_dim(g_cw, chip * 1408, 1408, axis=2)

    grads = [g_meta, g_pre_mix, None, g_w2, g_gb, g_ret_n, g_gla_n, None, g_post_mix, g_pre_ffn, None,
             g_cw, g_cb, None, g_post_ffn]
    ws = [meta_tokens, pre_mix_norm, w_in, gla_gate_w2, gla_gate_b, ret_norm_w, gla_norm_w, w_out, post_mix_norm,
          pre_ffn_norm, ffn_up, ffn_conv_w, ffn_conv_b, ffn_down, post_ffn_norm]
    ms = [m_meta_tokens, m_pre_mix_norm, m_w_in, m_gla_gate_w2, m_gla_gate_b, m_ret_norm_w, m_gla_norm_w, m_w_out,
          m_post_mix_norm, m_pre_ffn_norm, m_ffn_up, m_ffn_conv_w, m_ffn_conv_b, m_ffn_down, m_post_ffn_norm]
    vs = [v_meta_tokens, v_pre_mix_norm, v_w_in, v_gla_gate_w2, v_gla_gate_b, v_ret_norm_w, v_gla_norm_w, v_w_out,
          v_post_mix_norm, v_pre_ffn_norm, v_ffn_up, v_ffn_conv_w, v_ffn_conv_b, v_ffn_down, v_post_ffn_norm]
    big_idx = (2, 7, 10, 13)
    deltas, new_m, new_v = [None] * 15, [None] * 15, [None] * 15
    for i, nm, (mine, theirs) in zip(big_idx, names, big_halves):
        grads[i], deltas[i], new_m[i], new_v[i] = _adamw_halves(ws[i], ms[i], vs[i], mine, theirs, ci,
                                                                name=f"adamw_{nm}")
    small_idx = [i for i in range(15) if i not in big_idx]
    shapes = [ws[i].shape for i in small_idx]
    sd, sm, sv = _adamw(_slab([ws[i] for i in small_idx], 8), _slab([grads[i] for i in small_idx], 8),
                        _slab([ms[i] for i in small_idx], 8), _slab([vs[i] for i in small_idx], 8), name="adamw_small")
    for i, a, b, c_ in zip(small_idx, _unslab(sd, shapes), _unslab(sm, shapes), _unslab(sv, shapes)):
        deltas[i], new_m[i], new_v[i] = a, b, c_

    loss = lax.psum(loss_local, ("x", "y", "c"))
    return (loss, grad_x, *grads, *deltas, *new_m, *new_v)
```

```python
import functools
import math

import numpy as np
import jax
import jax.numpy as jnp
from jax import lax
from jax.experimental import pallas as pl
from jax.experimental.pallas import tpu as pltpu

F32 = jnp.float32
BF16 = jnp.bfloat16

D_MODEL = 1024
DEPTH = 2
N_META = 16
EPS = 1e-6
RET_HEADS = 4
RET_DK = 128
GLA_HEADS = 4
GLA_DK = 64
GLA_DV = 128
GLA_QK = GLA_HEADS * GLA_DK
GLA_V = GLA_HEADS * GLA_DV
GLA_RANK = 16
GLA_TAU = 16.0
D_FF = 2816
ROPE_BASE = 10000.0
IN_WIDTH = 3600
IN_PAD = 3840
C_RQ, C_RK, C_RV, C_RG, C_GR, C_GQ, C_GK, C_GV, C_GA = 0, 512, 1024, 1536, 2048, 2560, 2816, 3072, 3584
P_RET, P_GATE, P_GLA = 1536, 1024, 1280


def _to_kernel_cols(w):
    pad = jnp.zeros(w.shape[:-1] + (IN_PAD - IN_WIDTH,), w.dtype)
    return jnp.concatenate([w[..., 0:2048], w[..., 3072:3584], w[..., 2048:3072], w[..., 3584:3600], pad], axis=-1)


def _to_reference_chips(d_ret, d_gate, d_gla):
    segs = [(d_ret, 0, 0, 1536), (d_gate, 0, 1536, 512), (d_gla, 0, 2048, 1024), (d_gate, 512, 3072, 512),
            (d_gla, 1024, 3584, GLA_RANK)]
    per = IN_WIDTH // 4
    chips = []
    for j in range(4):
        lo, hi, parts = per * j, per * (j + 1), []
        for piece, p0, r0, width in segs:
            a, b = max(lo, r0), min(hi, r0 + width)
            if a < b:
                parts.append(piece[:, p0 + a - r0:p0 + b - r0])
        chips.append(jnp.concatenate(parts, axis=1))
    return jnp.stack(chips)

FRONT = 64
BACK = 64
PADF = FRONT - N_META
RET_CHUNK = 128
GLA_CHUNK = 64
GLA_SUB = 16
GLA_SUB2 = 4
BLK = 640

ADAM_LR, ADAM_B1, ADAM_B2, ADAM_EPS, ADAM_WD, ADAM_STEP = 0.001, 0.9, 0.999, 1e-08, 0.01, 10

VMEM_LIMIT = 56 * 2 ** 20
MM_VMEM_BUDGET = 40 * 2 ** 20
MESH = pl.DeviceIdType.MESH


def _cp(*sem):
    return pltpu.CompilerParams(dimension_semantics=sem, vmem_limit_bytes=VMEM_LIMIT)


def _tile(n, cands):
    for t in cands:
        if n % t == 0:
            return t
    raise ValueError(f"no tile for {n} in {cands}")


def _row_tile(n):
    return _tile(n, (640, 512, 320, 256, 128, 64))


def _mm(a, b, *, nt=False, add=None, out_dtype=F32, tn=None, name, carry=None):
    m, k = a.shape
    n = b.shape[0] if nt else b.shape[1]
    tm = _tile(m, (640, 320, 256, 128, 64))
    if tn is None:
        step_bytes = lambda t: 2 * (tm * k * a.dtype.itemsize + t * k * b.dtype.itemsize
                                    + tm * t * (jnp.dtype(out_dtype).itemsize + (4 if add is not None else 0)))
        tn = next(t for t in range(n, 0, -128) if n % t == 0 and (step_bytes(t) <= MM_VMEM_BUDGET or t == 128))
    dn = (((1,), (1,)), ((), ())) if nt else (((1,), (0,)), ((), ()))
    nj, ni = n // tn, m // tm
    n_in = 2 + (add is not None)
    c_arrs, c_mode = carry if carry is not None else ((), None)
    nc = len(c_arrs)

    def body(*refs):
        a_ref, b_ref = refs[:2]
        c_ref = refs[2] if add is not None else None
        o_ref = refs[n_in + nc]
        if nc:
            c_ins, c_outs = refs[n_in:n_in + nc], refs[n_in + nc + 1:n_in + 2 * nc + 1]
            ssem, rsem = refs[n_in + 2 * nc + 1:]
            j, i = pl.program_id(0), pl.program_id(1)

            @pl.when((j == 0) & (i == 0))
            def _():
                for cp in _chip_copies(c_ins, c_outs, ssem, rsem, c_mode)[0]:
                    cp.start()
        r = lax.dot_general(a_ref[...].astype(BF16), b_ref[...].astype(BF16), dn, preferred_element_type=F32)
        if add is not None:
            r = r + c_ref[...]
        o_ref[...] = r.astype(o_ref.dtype)
        if nc:
            @pl.when((j == nj - 1) & (i == ni - 1))
            def _():
                _chip_wait(*_chip_copies(c_ins, c_outs, ssem, rsem, c_mode))

    b_spec = pl.BlockSpec((tn, k), lambda j, i: (j, 0)) if nt else pl.BlockSpec((k, tn), lambda j, i: (0, j))
    in_specs = [pl.BlockSpec((tm, k), lambda j, i: (i, 0)), b_spec]
    args = [a, b]
    if add is not None:
        in_specs.append(pl.BlockSpec((tm, tn), lambda j, i: (i, j)))
        args.append(add)
    out_shape = jax.ShapeDtypeStruct((m, n), out_dtype)
    out_spec = pl.BlockSpec((tm, tn), lambda j, i: (i, j))
    if not nc:
        return pl.pallas_call(
            body, out_shape=out_shape, grid=(nj, ni), in_specs=in_specs, out_specs=out_spec,
            compiler_params=_cp("parallel", "parallel"), name=name)(*args)
    sem = pltpu.SemaphoreType.DMA
    outs = pl.pallas_call(
        body, out_shape=(out_shape,) + tuple(_landing_shape(x, c_mode) for x in c_arrs), grid=(nj, ni),
        in_specs=in_specs + [ANY] * nc, out_specs=(out_spec,) + (ANY,) * nc,
        scratch_shapes=[sem((nc, 3)), sem((nc, 3))],
        compiler_params=_cp("arbitrary", "arbitrary"), name=name)(*args, *c_arrs)
    return outs[0], list(outs[1:])


def _call_with_carry(body, *, out_shape, grid, in_specs, out_specs, args, semantics, carry, name, aliases=None):
    if carry is None:
        return pl.pallas_call(body, out_shape=out_shape, grid=grid, in_specs=in_specs, out_specs=out_specs,
                              input_output_aliases=aliases or {}, compiler_params=_cp(*semantics), name=name)(*args)
    c_arrs, c_mode = carry
    n_in, nc = len(args), len(c_arrs)

    def carried(*refs):
        c_ins, c_outs = refs[n_in:n_in + nc], refs[n_in + nc + 1:n_in + 2 * nc + 1]
        ssem, rsem = refs[n_in + 2 * nc + 1:]
        ids = [pl.program_id(d) for d in range(len(grid))]
        first = functools.reduce(lambda u, v: u & v, [i == 0 for i in ids])
        last = functools.reduce(lambda u, v: u & v, [i == g - 1 for i, g in zip(ids, grid)])

        @pl.when(first)
        def _():
            for cp in _chip_copies(c_ins, c_outs, ssem, rsem, c_mode)[0]:
                cp.start()
        body(*refs[:n_in], refs[n_in + nc])

        @pl.when(last)
        def _():
            _chip_wait(*_chip_copies(c_ins, c_outs, ssem, rsem, c_mode))

    sem = pltpu.SemaphoreType.DMA
    outs = pl.pallas_call(
        carried, out_shape=(out_shape,) + tuple(_landing_shape(x, c_mode) for x in c_arrs), grid=grid,
        in_specs=list(in_specs) + [ANY] * nc, out_specs=(out_specs,) + (ANY,) * nc,
        scratch_shapes=[sem((nc, 3)), sem((nc, 3))], input_output_aliases=aliases or {},
        compiler_params=_cp(*(("arbitrary",) * len(grid))), name=name)(*args, *c_arrs)
    return outs[0], list(outs[1:])


def _mm_nt_sum(a_list, b_list, *, name, carry=None):
    m, n = a_list[0].shape[0], b_list[0].shape[0]
    tm = _tile(m, (640, 320, 256, 128, 64))
    np_ = len(a_list)

    def body(*refs):
        acc = None
        for a_ref, b_ref in zip(refs[:np_], refs[np_:2 * np_]):
            r = lax.dot_general(a_ref[...].astype(BF16), b_ref[...].astype(BF16), (((1,), (1,)), ((), ())),
                                preferred_element_type=F32)
            acc = r if acc is None else acc + r
        refs[2 * np_][...] = acc

    return _call_with_carry(
        body, out_shape=jax.ShapeDtypeStruct((m, n), F32), grid=(m // tm,),
        in_specs=[pl.BlockSpec((tm, a.shape[1]), lambda i: (i, 0)) for a in a_list]
        + [pl.BlockSpec(b.shape, lambda i: (0, 0)) for b in b_list],
        out_specs=pl.BlockSpec((tm, n), lambda i: (i, 0)), args=[*a_list, *b_list], semantics=("parallel",),
        carry=carry, name=name)


def _mm_tn(a, b, *, tn=None, blocks=None, into=None, name, carry=None):
    m, k = a.shape
    n = b.shape[1]
    tm = _tile(m, (1664, 640, 320, 256, 128, 64))
    tn = n if tn is None else tn
    if blocks is not None:
        total, first = blocks
        out_shape = jax.ShapeDtypeStruct((total, k, tn), F32)
        out_spec = pl.BlockSpec((None, k, tn), lambda j, i: (first + j, 0, 0))
    else:
        out_shape = jax.ShapeDtypeStruct((k, n), F32)
        out_spec = pl.BlockSpec((k, tn), lambda j, i: (0, j))

    def body(a_ref, b_ref, *rest):
        o_ref = rest[-1]

        @pl.when(pl.program_id(1) == 0)
        def _():
            o_ref[...] = jnp.zeros_like(o_ref)
        o_ref[...] += lax.dot_general(a_ref[...].astype(BF16), b_ref[...].astype(BF16),
                                      (((0,), (0,)), ((), ())), preferred_element_type=F32)

    in_specs = [pl.BlockSpec((tm, k), lambda j, i: (i, 0)), pl.BlockSpec((tm, tn), lambda j, i: (i, j))]
    args, alias = [a, b], {}
    if into is not None:
        in_specs.append(pl.BlockSpec(memory_space=pl.ANY))
        args.append(into)
        alias = {2: 0}
    return _call_with_carry(body, out_shape=out_shape, grid=(n // tn, m // tm), in_specs=in_specs, out_specs=out_spec,
                            args=args, semantics=("parallel", "arbitrary"), carry=carry, name=name, aliases=alias)


def _rms(x, w):
    r = lax.rsqrt(jnp.mean(x * x, axis=-1, keepdims=True) + EPS)
    return x * r * w


def _rms_bwd(x, w, dy):
    r = lax.rsqrt(jnp.mean(x * x, axis=-1, keepdims=True) + EPS)
    xh = x * r
    dxh = dy * w
    dx = r * (dxh - xh * jnp.mean(dxh * xh, axis=-1, keepdims=True))
    return dx, jnp.sum(dy * xh, axis=0, keepdims=True)


def _resid_norm(h, t, w_post, w_next, *, name):
    lp, d = h.shape
    tm = _row_tile(lp)
    has_t = t is not None

    def body(*refs):
        if has_t:
            h_ref, t_ref, wp_ref, wn_ref, ho_ref, hn_ref = refs
            hv = h_ref[...] + _rms(t_ref[...], wp_ref[...])
            ho_ref[...] = hv
        else:
            h_ref, wn_ref, hn_ref = refs
            hv = h_ref[...]
        hn_ref[...] = _rms(hv, wn_ref[...]).astype(BF16)

    row = pl.BlockSpec((tm, d), lambda i: (i, 0))
    vec = pl.BlockSpec((1, d), lambda i: (0, 0))
    if has_t:
        return pl.pallas_call(
            body, out_shape=(jax.ShapeDtypeStruct((lp, d), F32), jax.ShapeDtypeStruct((lp, d), BF16)),
            grid=(lp // tm,), in_specs=[row, row, vec, vec], out_specs=(row, row),
            compiler_params=_cp("parallel"), name=name)(h, t, w_post, w_next)
    return h, pl.pallas_call(
        body, out_shape=jax.ShapeDtypeStruct((lp, d), BF16), grid=(lp // tm,), in_specs=[row, vec],
        out_specs=row, compiler_params=_cp("parallel"), name=name)(h, w_next)


def _resid_norm_bwd(dh_out, dhn, h_new, t, w_next, w_post, *, name):
    lp, d = h_new.shape if h_new is not None else t.shape
    tm = _row_tile(lp)
    has_n = dhn is not None
    has_t = t is not None

    def body(*refs):
        refs = list(refs)
        dho_ref = refs.pop(0)
        if has_n:
            dhn_ref, hn_ref, wn_ref = refs.pop(0), refs.pop(0), refs.pop(0)
        if has_t:
            t_ref, wp_ref = refs.pop(0), refs.pop(0)
        dh_ref = refs.pop(0) if has_n else None
        dt_ref = refs.pop(0) if has_t else None
        dwn_ref = refs.pop(0) if has_n else None
        dwp_ref = refs.pop(0) if has_t else None
        first = pl.program_id(0) == 0
        dh = dho_ref[...]
        if has_n:
            dx, dwn = _rms_bwd(hn_ref[...], wn_ref[...], dhn_ref[...])
            dh = dh + dx
            dh_ref[...] = dh

            @pl.when(first)
            def _():
                dwn_ref[...] = jnp.zeros_like(dwn_ref)
            dwn_ref[...] += dwn
        if has_t:
            dt, dwp = _rms_bwd(t_ref[...], wp_ref[...], dh)
            dt_ref[...] = dt.astype(BF16)

            @pl.when(first)
            def _():
                dwp_ref[...] = jnp.zeros_like(dwp_ref)
            dwp_ref[...] += dwp

    row = pl.BlockSpec((tm, d), lambda i: (i, 0))
    vec = pl.BlockSpec((1, d), lambda i: (0, 0))
    args, in_specs, out_shape, out_specs = [dh_out], [row], [], []
    if has_n:
        args += [dhn, h_new, w_next]
        in_specs += [row, row, vec]
    if has_t:
        args += [t, w_post]
        in_specs += [row, vec]
    if has_n:
        out_shape.append(jax.ShapeDtypeStruct((lp, d), F32)); out_specs.append(row)
    if has_t:
        out_shape.append(jax.ShapeDtypeStruct((lp, d), BF16)); out_specs.append(row)
    if has_n:
        out_shape.append(jax.ShapeDtypeStruct((1, d), F32)); out_specs.append(vec)
    if has_t:
        out_shape.append(jax.ShapeDtypeStruct((1, d), F32)); out_specs.append(vec)
    outs = list(pl.pallas_call(body, out_shape=tuple(out_shape), grid=(lp // tm,), in_specs=in_specs,
                               out_specs=tuple(out_specs), compiler_params=_cp("arbitrary"), name=name)(*args))
    dh = outs.pop(0) if has_n else dh_out
    dt = outs.pop(0) if has_t else None
    dwn = outs.pop(0) if has_n else None
    dwp = outs.pop(0) if has_t else None
    return dh, dt, dwn, dwp


def _loss_head(h, f, w_post, target, *, name):
    lp, d = h.shape
    tm = _row_tile(lp)

    def body(h_ref, f_ref, w_ref, t_ref, loss_ref, dy_ref, df_ref, dw_ref):
        i = pl.program_id(0)
        f, w = f_ref[...], w_ref[...]
        y = h_ref[...] + _rms(f, w)
        rows = i * tm + lax.broadcasted_iota(jnp.int32, (tm, 1), 0)
        tok = (rows >= FRONT) & (rows < lp - BACK)
        err = jnp.where(tok, y - t_ref[...], 0.0)
        dy = err * (1.0 / d)
        dy_ref[...] = dy
        df, dw = _rms_bwd(f, w, dy)
        df_ref[...] = df.astype(BF16)

        @pl.when(i == 0)
        def _():
            loss_ref[...] = jnp.zeros_like(loss_ref)
            dw_ref[...] = jnp.zeros_like(dw_ref)
        part = jnp.sum(jnp.sum(err * err, axis=1, keepdims=True), axis=0, keepdims=True) * (0.5 / d)
        loss_ref[...] += jnp.broadcast_to(part, loss_ref.shape)
        dw_ref[...] += dw

    row = pl.BlockSpec((tm, d), lambda i: (i, 0))
    vec = pl.BlockSpec((1, d), lambda i: (0, 0))
    loss, dy, df, dw = pl.pallas_call(
        body, out_shape=(jax.ShapeDtypeStruct((8, 128), F32), jax.ShapeDtypeStruct((lp, d), F32),
                         jax.ShapeDtypeStruct((lp, d), BF16), jax.ShapeDtypeStruct((1, d), F32)),
        grid=(lp // tm,), in_specs=[row, row, vec, row],
        out_specs=(pl.BlockSpec((8, 128), lambda i: (0, 0)), row, row, vec),
        compiler_params=_cp("arbitrary"), name=name)(h, f, w_post, target)
    return loss[0, 0], dy, df, dw


_GELU_C = math.sqrt(2.0 / math.pi)


def _gelu_and_grad(a):
    a2 = a * a
    t = jnp.tanh(a * (_GELU_C + (_GELU_C * 0.044715) * a2))
    ha = 0.5 * a
    h1 = 0.5 + 0.5 * t
    return a * h1, h1 + ha * (1.0 - t * t) * (_GELU_C + (3.0 * _GELU_C * 0.044715) * a2)


def _gelu(a):
    t = jnp.tanh(a * (_GELU_C + (_GELU_C * 0.044715) * (a * a)))
    return a * (0.5 + 0.5 * t)


def _conv3(parts, n, w, b):
    xx = jnp.concatenate(parts, axis=0)
    return b + xx[8:8 + n] * w[2:3] + pltpu.roll(xx, 1, 0)[8:8 + n] * w[1:2] + pltpu.roll(xx, 2, 0)[8:8 + n] * w[0:1]


def _conv_act(ua, ug, wa, wg, ba, bg, *, name):
    lp, n = ua.shape
    tm = _row_tile(lp)
    tc = _tile(n, (256, 128))
    nb8 = tm // 8

    def body(ua_ref, uap_ref, ug_ref, ugp_ref, wa_ref, wg_ref, ba_ref, bg_ref, o_ref):
        i = pl.program_id(0)
        ca = _conv3([uap_ref[...], ua_ref[...]], tm, wa_ref[...], ba_ref[...])
        cg = _conv3([ugp_ref[...], ug_ref[...]], tm, wg_ref[...], bg_ref[...])
        rows = i * tm + lax.broadcasted_iota(jnp.int32, (tm, 1), 0)
        ok = (rows >= PADF) & (rows < lp - BACK)
        o_ref[...] = jnp.where(ok, _gelu(ca) * cg, 0.0).astype(BF16)

    cur = pl.BlockSpec((tm, tc), lambda i, j: (i, j))
    prev = pl.BlockSpec((8, tc), lambda i, j: (jnp.maximum(i * nb8 - 1, 0), j))
    w3 = pl.BlockSpec((3, tc), lambda i, j: (0, j))
    b1 = pl.BlockSpec((1, tc), lambda i, j: (0, j))
    return pl.pallas_call(
        body, out_shape=jax.ShapeDtypeStruct((lp, n), BF16), grid=(lp // tm, n // tc),
        in_specs=[cur, prev, cur, prev, w3, w3, b1, b1], out_specs=cur,
        compiler_params=_cp("parallel", "parallel"), name=name)(ua, ua, ug, ug, wa, wg, ba, bg)


def _conv_act_down(ua, ug, wa, wg, ba, bg, down, *, name):
    lp, n = ua.shape
    d = down.shape[1]
    tm = _tile(lp, (320, 256, 128, 64))
    tc = _tile(n, (256, 128))
    nb8 = tm // 8

    def body(ua_ref, uap_ref, ug_ref, ugp_ref, wa_ref, wg_ref, ba_ref, bg_ref, dn_ref, act_ref, f_ref):
        i = pl.program_id(0)
        rows = i * tm + lax.broadcasted_iota(jnp.int32, (tm, 1), 0)
        ok = (rows >= PADF) & (rows < lp - BACK)
        acc = None
        for j in range(n // tc):
            cs = slice(tc * j, tc * j + tc)
            ca = _conv3([uap_ref[:, cs], ua_ref[:, cs]], tm, wa_ref[:, cs], ba_ref[:, cs])
            cg = _conv3([ugp_ref[:, cs], ug_ref[:, cs]], tm, wg_ref[:, cs], bg_ref[:, cs])
            act = jnp.where(ok, _gelu(ca) * cg, 0.0).astype(BF16)
            act_ref[:, cs] = act
            part = _dot(act, dn_ref[cs, :])
            acc = part if acc is None else acc + part
        f_ref[...] = acc

    cur = pl.BlockSpec((tm, n), lambda i: (i, 0))
    prev = pl.BlockSpec((8, n), lambda i: (jnp.maximum(i * nb8 - 1, 0), 0))
    w3 = pl.BlockSpec((3, n), lambda i: (0, 0))
    b1 = pl.BlockSpec((1, n), lambda i: (0, 0))
    return pl.pallas_call(
        body, out_shape=(jax.ShapeDtypeStruct((lp, n), BF16), jax.ShapeDtypeStruct((lp, d), F32)),
        grid=(lp // tm,),
        in_specs=[cur, prev, cur, prev, w3, w3, b1, b1, pl.BlockSpec(down.shape, lambda i: (0, 0))],
        out_specs=(cur, pl.BlockSpec((tm, d), lambda i: (i, 0))),
        compiler_params=_cp("parallel"), name=name)(ua, ua, ug, ug, wa, wg, ba, bg, down)


def _ffn_fwd(h, m, w_post, w_next, up_a, up_g, wa, wg, ba, bg, down, *, name, carry=None, after=None):
    lp, d = h.shape
    n = up_a.shape[1]
    tm = _tile(lp, (320, 256, 128, 64))
    tc = _tile(n, (256, 128))
    nchunks = n // tc
    c_arrs, c_mode = carry if carry is not None else ((), None)
    nc = len(c_arrs)
    steps = lp // tm
    n_out = 6 if after is None else 8
    after = after or ()

    def body(h_ref, hp_ref, m_ref, mp_ref, wp_ref, wn_ref, upa_ref, upg_ref, wa_ref, wg_ref, ba_ref, bg_ref, dn_ref,
             *rest):
        after_refs, rest = rest[:len(after)], rest[len(after):]
        c_ins = rest[:nc]
        hmid_ref, hn_ref, ua_ref, ug_ref, act_ref, f_ref = rest[nc:nc + 6]
        c_outs = rest[nc + n_out:2 * nc + n_out]
        i = pl.program_id(0)
        if nc:
            ssem, rsem = rest[2 * nc + n_out:]

            @pl.when(i == 0)
            def _():
                for cp in _chip_copies(c_ins, c_outs, ssem, rsem, c_mode)[0]:
                    cp.start()
        rows = i * tm + lax.broadcasted_iota(jnp.int32, (tm, 1), 0)
        ok = (rows >= PADF) & (rows < lp - BACK)
        hv = (jnp.concatenate([hp_ref[...], h_ref[...]], axis=0)
              + _rms(jnp.concatenate([mp_ref[...], m_ref[...]], axis=0), wp_ref[...]))
        x = _rms(hv, wn_ref[...]).astype(BF16)
        hmid_ref[...] = hv[16:]
        hn_ref[...] = x[16:]
        u_of = lambda j: (_dot(x, upa_ref[:, tc * j:tc * j + tc]), _dot(x, upg_ref[:, tc * j:tc * j + tc]))
        u_next = u_of(0)
        acc = None
        for j in range(nchunks):
            cs = slice(tc * j, tc * j + tc)
            ua, ug = u_next
            if j + 1 < nchunks:
                u_next = u_of(j + 1)
            ua_ref[:, cs] = ua[16:]
            ug_ref[:, cs] = ug[16:]
            ca = _conv3([ua[8:]], tm, wa_ref[:, cs], ba_ref[:, cs])
            cg = _conv3([ug[8:]], tm, wg_ref[:, cs], bg_ref[:, cs])
            act = jnp.where(ok, _gelu(ca) * cg, 0.0).astype(BF16)
            act_ref[:, cs] = act
            part = _dot(act, dn_ref[cs, :])
            acc = part if acc is None else acc + part
        f_ref[...] = acc
        if after:
            h_new = hv[16:] + _rms(acc, after_refs[0][...])
            rest[nc + 6][...] = h_new
            rest[nc + 7][...] = _rms(h_new, after_refs[1][...]).astype(BF16)
        if nc:
            @pl.when(i == steps - 1)
            def _():
                _chip_wait(*_chip_copies(c_ins, c_outs, ssem, rsem, c_mode))

    whole = pl.BlockSpec(memory_space=pltpu.VMEM)
    wide = pl.BlockSpec((tm, n), lambda i: (i, 0))
    w3 = pl.BlockSpec((3, n), lambda i: (0, 0))
    b1 = pl.BlockSpec((1, n), lambda i: (0, 0))
    sem = pltpu.SemaphoreType.DMA
    row = pl.BlockSpec((tm, d), lambda i: (i, 0))
    prev16 = pl.BlockSpec((16, d), lambda i: (jnp.maximum(i * (tm // 16) - 1, 0), 0))
    vec = pl.BlockSpec((1, d), lambda i: (0, 0))
    outs = pl.pallas_call(
        body,
        out_shape=(jax.ShapeDtypeStruct((lp, d), F32), jax.ShapeDtypeStruct((lp, d), BF16),
                   jax.ShapeDtypeStruct((lp, n), F32), jax.ShapeDtypeStruct((lp, n), F32),
                   jax.ShapeDtypeStruct((lp, n), BF16), jax.ShapeDtypeStruct((lp, d), F32))
        + ((jax.ShapeDtypeStruct((lp, d), F32), jax.ShapeDtypeStruct((lp, d), BF16)) if after else ())
        + tuple(_landing_shape(a, c_mode) for a in c_arrs),
        grid=(steps,),
        in_specs=[row, prev16, row, prev16, vec, vec, whole, whole, w3, w3, b1, b1, whole] + [vec] * len(after)
        + [ANY] * nc,
        out_specs=(row, row, wide, wide, wide, row) + ((row, row) if after else ()) + (ANY,) * nc,
        scratch_shapes=[sem((nc, 3)), sem((nc, 3))] if nc else [],
        compiler_params=_cp("arbitrary"), name=name)(h, h, m, m, w_post, w_next, up_a, up_g, wa, wg, ba, bg, down,
                                                     *after, *c_arrs)
    return outs[:n_out], list(outs[n_out:])


def _conv_act_bwd(ua, ug, df, down, wa, wg, ba, bg, up_a, up_g, *, name):
    lp, n = ua.shape
    d = up_a.shape[0]
    tm = _tile(lp, (320, 256, 128, 64))
    tc = _tile(n, (256, 128))
    nb8 = tm // 8
    last8 = lp // 8 - 1
    last16 = lp // 16 - 1
    ext = tm + 8

    def body(ua_ref, uap_ref, uan_ref, ug_ref, ugp_ref, ugn_ref, df_ref, dfn_ref, dn_ref, wa_ref, wg_ref, ba_ref,
             bg_ref, upa_ref, upg_ref, dua_ref, dug_ref, dwa_ref, dwg_ref, dba_ref, dbg_ref, dhn_ref):
        i = pl.program_id(0)
        df_ext = jnp.concatenate([df_ref[...], dfn_ref[...]], axis=0)

        @pl.when(i == 0)
        def _():
            dwa_ref[...] = jnp.zeros_like(dwa_ref)
            dwg_ref[...] = jnp.zeros_like(dwg_ref)
            dba_ref[...] = jnp.zeros_like(dba_ref)
            dbg_ref[...] = jnp.zeros_like(dbg_ref)
        rows = i * tm + lax.broadcasted_iota(jnp.int32, (ext, 1), 0)
        ok = (rows >= PADF) & (rows < lp - BACK)

        def conv(parts, w, b):
            xx = jnp.concatenate(parts, axis=0)
            x, x1, x2 = xx[8:8 + ext], pltpu.roll(xx, 1, 0)[8:8 + ext], pltpu.roll(xx, 2, 0)[8:8 + ext]
            return b + x * w[2:3] + x1 * w[1:2] + x2 * w[0:1], x, x1, x2

        def back(dc, w):
            return (dc[:tm] * w[2:3] + pltpu.roll(dc, ext - 1, 0)[:tm] * w[1:2]
                    + pltpu.roll(dc, ext - 2, 0)[:tm] * w[0:1])

        def wsum(dw_ref, db_ref, cs, dc, x, x1, x2):
            dd = dc[:tm]
            s = lambda v: jnp.sum(v, axis=0, keepdims=True)
            dw_ref[0:1, cs] += s(dd * x2[:tm])
            dw_ref[1:2, cs] += s(dd * x1[:tm])
            dw_ref[2:3, cs] += s(dd * x[:tm])
            db_ref[:, cs] += s(dd)

        acc = None
        nchunks = n // tc
        dact_of = lambda j: _dot_nt(df_ext, dn_ref[tc * j:tc * j + tc, :])[:ext]
        dact_next = dact_of(0)
        for j in range(nchunks):
            cs = slice(tc * j, tc * j + tc)
            dact_cur = dact_next
            if j + 1 < nchunks:
                dact_next = dact_of(j + 1)
            wa, wg = wa_ref[:, cs], wg_ref[:, cs]
            ca, xa, xa1, xa2 = conv([uap_ref[:, cs], ua_ref[:, cs], uan_ref[:, cs]], wa, ba_ref[:, cs])
            cg, xg, xg1, xg2 = conv([ugp_ref[:, cs], ug_ref[:, cs], ugn_ref[:, cs]], wg, bg_ref[:, cs])
            dact_e = jnp.where(ok, dact_cur, 0.0)
            gel, gel_d = _gelu_and_grad(ca)
            dca = dact_e * cg * gel_d
            dcg = dact_e * gel
            du_a, du_g = back(dca, wa).astype(BF16), back(dcg, wg).astype(BF16)
            dua_ref[:, cs] = du_a
            dug_ref[:, cs] = du_g
            wsum(dwa_ref, dba_ref, cs, dca, xa, xa1, xa2)
            wsum(dwg_ref, dbg_ref, cs, dcg, xg, xg1, xg2)
            part = _dot_nt(du_a, upa_ref[:, cs]) + _dot_nt(du_g, upg_ref[:, cs])
            acc = part if acc is None else acc + part
        dhn_ref[...] = acc

    cur = pl.BlockSpec((tm, n), lambda i: (i, 0))
    prev = pl.BlockSpec((8, n), lambda i: (jnp.maximum(i * nb8 - 1, 0), 0))
    nxt = pl.BlockSpec((8, n), lambda i: (jnp.minimum((i + 1) * nb8, last8), 0))
    w3 = pl.BlockSpec((3, n), lambda i: (0, 0))
    b1 = pl.BlockSpec((1, n), lambda i: (0, 0))
    whole = pl.BlockSpec(memory_space=pltpu.VMEM)
    return pl.pallas_call(
        body,
        out_shape=(jax.ShapeDtypeStruct((lp, n), BF16), jax.ShapeDtypeStruct((lp, n), BF16),
                   jax.ShapeDtypeStruct((3, n), F32), jax.ShapeDtypeStruct((3, n), F32),
                   jax.ShapeDtypeStruct((1, n), F32), jax.ShapeDtypeStruct((1, n), F32),
                   jax.ShapeDtypeStruct((lp, d), F32)),
        grid=(lp // tm,),
        in_specs=[cur, prev, nxt, cur, prev, nxt, pl.BlockSpec((tm, d), lambda i: (i, 0)),
                  pl.BlockSpec((16, d), lambda i: (jnp.minimum((i + 1) * (tm // 16), last16), 0)), whole,
                  w3, w3, b1, b1, whole, whole],
        out_specs=(cur, cur, w3, w3, b1, b1, pl.BlockSpec((tm, d), lambda i: (i, 0))),
        compiler_params=_cp("arbitrary"), name=name)(ua, ua, ua, ug, ug, ug, df, df, down, wa, wg, ba, bg, up_a, up_g)


def _sigmoid(x):
    return 1.0 / (1.0 + jnp.exp(-x))


def _merge_mix_out(o_ret, o_gla, proj, w_ret, w_gla, wout, *, name, carry=None):
    lp = o_ret.shape[0]
    d = wout.shape[1]
    tm = _row_tile(lp)
    steps = lp // tm
    c_arrs, c_mode = carry if carry is not None else ((), None)
    nc = len(c_arrs)

    def body(or_ref, og_ref, rg_ref, gr_ref, wr_ref, wg_ref, wo_ref, *rest):
        c_ins = rest[:nc]
        m_ref, out_ref = rest[nc:nc + 2]
        c_outs = rest[nc + 2:2 * nc + 2]
        i = pl.program_id(0)
        if nc:
            ssem, rsem = rest[2 * nc + 2:]

            @pl.when(i == 0)
            def _():
                for cp in _chip_copies(c_ins, c_outs, ssem, rsem, c_mode)[0]:
                    cp.start()
        oret, ogla = or_ref[...], og_ref[...]
        yr, yg = [], []
        for h in range(4):
            hs = slice(128 * h, 128 * h + 128)
            o = oret[:, hs]
            xc = o - jnp.mean(o, axis=-1, keepdims=True)
            yr.append(xc * lax.rsqrt(jnp.mean(xc * xc, axis=-1, keepdims=True) + EPS))
            o = ogla[:, hs]
            yg.append(o * lax.rsqrt(jnp.mean(o * o, axis=-1, keepdims=True) + EPS))
        rg, gr = rg_ref[...], gr_ref[...]
        ret = (jnp.concatenate(yr, axis=1) * wr_ref[...] * (rg * _sigmoid(rg))).astype(BF16)
        gla = (jnp.concatenate(yg, axis=1) * wg_ref[...] * (gr * _sigmoid(gr))).astype(BF16)
        m_ref[:, 0:512] = ret
        m_ref[:, 512:1024] = gla
        out_ref[...] = _dot(ret, wo_ref[0:512, :]) + _dot(gla, wo_ref[512:1024, :])
        if nc:
            @pl.when(i == steps - 1)
            def _():
                _chip_wait(*_chip_copies(c_ins, c_outs, ssem, rsem, c_mode))

    row = pl.BlockSpec((tm, 512), lambda i: (i, 0))
    vec = pl.BlockSpec((1, 512), lambda i: (0, 0))
    wide = pl.BlockSpec((tm, 1024), lambda i: (i, 0))
    sem = pltpu.SemaphoreType.DMA
    outs = pl.pallas_call(
        body, out_shape=(jax.ShapeDtypeStruct((lp, 1024), BF16), jax.ShapeDtypeStruct((lp, d), F32))
        + tuple(_landing_shape(a, c_mode) for a in c_arrs),
        grid=(steps,),
        in_specs=[row, row, pl.BlockSpec((tm, 512), lambda i: (i, C_RG // 512)),
                  pl.BlockSpec((tm, 512), lambda i: (i, C_GR // 512)), vec, vec,
                  pl.BlockSpec(memory_space=pltpu.VMEM)] + [ANY] * nc,
        out_specs=(wide, pl.BlockSpec((tm, d), lambda i: (i, 0))) + (ANY,) * nc,
        scratch_shapes=[sem((nc, 3)), sem((nc, 3))] if nc else [],
        compiler_params=_cp("arbitrary"), name=name)(o_ret, o_gla, proj, proj, w_ret, w_gla, wout, *c_arrs)
    return outs[:2], list(outs[2:])


def _merge_bwd(dm, wout, o_ret, o_gla, proj, w_ret, w_gla, *, name, carry=None):
    lp = o_ret.shape[0]
    tm = _row_tile(lp)
    steps = lp // tm
    c_arrs, c_mode = carry if carry is not None else ((), None)
    nc = len(c_arrs)

    def body(dm_ref, wo_ref, or_ref, og_ref, rg_ref, gr_ref, wr_ref, wg_ref, *rest):
        c_ins = rest[:nc]
        dor_ref, dog_ref, dgate_ref, dwr_ref, dwg_ref = rest[nc:nc + 5]
        c_outs = rest[nc + 5:2 * nc + 5]
        i = pl.program_id(0)
        if nc:
            ssem, rsem = rest[2 * nc + 5:]

            @pl.when(i == 0)
            def _():
                for cp in _chip_copies(c_ins, c_outs, ssem, rsem, c_mode)[0]:
                    cp.start()

        @pl.when(i == 0)
        def _():
            dwr_ref[...] = jnp.zeros_like(dwr_ref)
            dwg_ref[...] = jnp.zeros_like(dwg_ref)

        def group(d, o_all, gate, w, center):
            sg = _sigmoid(gate)
            s = gate * sg
            ds = sg * (1.0 + gate * (1.0 - sg))
            xh, rr = [], []
            for h in range(4):
                o = o_all[:, 128 * h:128 * h + 128]
                if center:
                    o = o - jnp.mean(o, axis=-1, keepdims=True)
                r = lax.rsqrt(jnp.mean(o * o, axis=-1, keepdims=True) + EPS)
                xh.append(o * r)
                rr.append(r)
            xh_all = jnp.concatenate(xh, axis=1)
            dgate = d * xh_all * w * ds
            dw = jnp.sum(d * xh_all * s, axis=0, keepdims=True)
            dxh_all = d * w * s
            do = []
            for h in range(4):
                dxh = dxh_all[:, 128 * h:128 * h + 128]
                t = dxh - xh[h] * jnp.mean(dxh * xh[h], axis=-1, keepdims=True)
                if center:
                    t = t - jnp.mean(dxh, axis=-1, keepdims=True)
                do.append(rr[h] * t)
            return jnp.concatenate(do, axis=1), dgate, dw

        dmb = dm_ref[...]
        do, dg, dw = group(_dot_nt(dmb, wo_ref[0:512, :]), or_ref[...], rg_ref[...], wr_ref[...], True)
        dor_ref[...] = do
        dgate_ref[:, 0:512] = dg.astype(BF16)
        dwr_ref[...] += dw
        do, dg, dw = group(_dot_nt(dmb, wo_ref[512:1024, :]), og_ref[...], gr_ref[...], wg_ref[...], False)
        dog_ref[...] = do
        dgate_ref[:, 512:1024] = dg.astype(BF16)
        dwg_ref[...] += dw
        if nc:
            @pl.when(i == steps - 1)
            def _():
                _chip_wait(*_chip_copies(c_ins, c_outs, ssem, rsem, c_mode))

    row = pl.BlockSpec((tm, 512), lambda i: (i, 0))
    vec = pl.BlockSpec((1, 512), lambda i: (0, 0))
    sem = pltpu.SemaphoreType.DMA
    outs = pl.pallas_call(
        body,
        out_shape=(jax.ShapeDtypeStruct((lp, 512), F32), jax.ShapeDtypeStruct((lp, 512), F32),
                   jax.ShapeDtypeStruct((lp, P_GATE), BF16),
                   jax.ShapeDtypeStruct((1, 512), F32), jax.ShapeDtypeStruct((1, 512), F32))
        + tuple(_landing_shape(a, c_mode) for a in c_arrs),
        grid=(steps,),
        in_specs=[pl.BlockSpec((tm, dm.shape[1]), lambda i: (i, 0)), pl.BlockSpec(memory_space=pltpu.VMEM), row, row,
                  pl.BlockSpec((tm, 512), lambda i: (i, C_RG // 512)),
                  pl.BlockSpec((tm, 512), lambda i: (i, C_GR // 512)), vec, vec] + [ANY] * nc,
        out_specs=(row, row, pl.BlockSpec((tm, P_GATE), lambda i: (i, 0)), vec, vec) + (ANY,) * nc,
        scratch_shapes=[sem((nc, 3)), sem((nc, 3))] if nc else [],
        compiler_params=_cp("arbitrary"), name=name)(dm, wout, o_ret, o_gla, proj, proj, w_ret, w_gla, *c_arrs)
    return outs[:5], list(outs[5:])


def _dot(a, b):
    return lax.dot_general(a, b, (((1,), (0,)), ((), ())), preferred_element_type=F32)


def _dot_nt(a, b):
    return lax.dot_general(a, b, (((1,), (1,)), ((), ())), preferred_element_type=F32)


def _dot_tn(a, b):
    return lax.dot_general(a, b, (((0,), (0,)), ((), ())), preferred_element_type=F32)


def _ret_tables(lp):
    cr = RET_CHUNK
    pos = np.arange(lp, dtype=np.float32) - np.float32(PADF)
    half = RET_DK // 2
    inv = (np.float32(ROPE_BASE) ** (-np.arange(half, dtype=np.float32) / np.float32(half))).astype(np.float32)
    ang = (pos[:, None] * inv[None, :]).astype(np.float32)
    c, s = np.cos(ang).astype(np.float32), np.sin(ang).astype(np.float32)
    rope_c = jnp.asarray(np.concatenate([c, c], axis=1))
    rope_s = jnp.asarray(np.concatenate([-s, s], axis=1))
    log_g = np.log(1.0 - 2.0 ** (-5.0 - np.arange(RET_HEADS, dtype=np.float64)))
    idx = np.arange(cr, dtype=np.float64)
    diff = idx[:, None] - idx[None, :]
    dmat = np.where(diff >= 0, np.exp(log_g[:, None, None] * np.maximum(diff, 0.0)), 0.0)
    zeta = np.exp(log_g[:, None] * (cr - 1.0 - idx)[None, :])
    xi = np.exp(log_g[:, None] * (idx + 1.0)[None, :])
    gc = np.exp(log_g * cr)
    f = lambda a: jnp.asarray(a.astype(np.float32))
    return (rope_c, rope_s, f(dmat), f(np.broadcast_to(zeta[:, :, None], (RET_HEADS, cr, 128))),
            f(np.broadcast_to(xi[:, :, None], (RET_HEADS, cr, 128))),
            f(np.broadcast_to(gc[:, None, None], (RET_HEADS, 8, 128))))


def _rope(t, c, s):
    return t * c + pltpu.roll(t, 64, 1) * s


def _rope_t(d, c, s):
    return d * c + pltpu.roll(d * s, 64, 1)


def _ret_specs(nblk, rev):
    ix = (lambda i: nblk - 1 - i) if rev else (lambda i: i)
    cr = RET_CHUNK
    col = lambda base: pl.BlockSpec((BLK, 512), lambda i: (ix(i), base // 512))
    tab = pl.BlockSpec((BLK, 128), lambda i: (ix(i), 0))
    sq = pl.BlockSpec((RET_HEADS, cr, cr), lambda i: (0, 0, 0))
    hv = pl.BlockSpec((RET_HEADS, cr, 128), lambda i: (0, 0, 0))
    g8 = pl.BlockSpec((RET_HEADS, 8, 128), lambda i: (0, 0, 0))
    st = pl.BlockSpec((RET_HEADS, BLK // cr, 128, 128), lambda i: (0, ix(i), 0, 0))
    out = pl.BlockSpec((BLK, 512), lambda i: (ix(i), 0))
    return col, tab, sq, hv, g8, st, out


def _retention(proj, tables, *, name):
    lp = proj.shape[0]
    nblk, cr = lp // BLK, RET_CHUNK
    scale = RET_DK ** -0.5

    def body(q_ref, k_ref, v_ref, c_ref, s_ref, d_ref, z_ref, x_ref, g_ref, o_ref, st_ref, state):
        @pl.when(pl.program_id(0) == 0)
        def _():
            state[...] = jnp.zeros_like(state)

        def chunk(ci, carry):
            sl = pl.ds(pl.multiple_of(ci * cr, cr), cr)
            c, s = c_ref[sl, :], s_ref[sl, :]
            for h in range(RET_HEADS):
                hs = slice(128 * h, 128 * h + 128)
                q = _rope(q_ref[sl, hs], c, s)
                k = _rope(k_ref[sl, hs], c, s) * scale
                qb, kb, vb = q.astype(BF16), k.astype(BF16), v_ref[sl, hs].astype(BF16)
                st = state[h]
                st_ref[h, ci] = st
                sc = _dot_nt(qb, kb) * d_ref[h]
                o_ref[sl, hs] = _dot(sc.astype(BF16), vb) + _dot(qb, st.astype(BF16)) * x_ref[h]
                state[h] = st * g_ref[h][0:1, :] + _dot_tn((k * z_ref[h]).astype(BF16), vb)
            return carry

        lax.fori_loop(0, BLK // cr, chunk, 0)

    col, tab, sq, hv, g8, st, out = _ret_specs(nblk, False)
    return pl.pallas_call(
        body,
        out_shape=(jax.ShapeDtypeStruct((lp, 512), F32), jax.ShapeDtypeStruct((4, lp // cr, 128, 128), F32)),
        grid=(nblk,), in_specs=[col(C_RQ), col(C_RK), col(C_RV), tab, tab, sq, hv, hv, g8],
        out_specs=(out, st), scratch_shapes=[pltpu.VMEM((RET_HEADS, 128, 128), F32)],
        compiler_params=_cp("arbitrary"), name=name)(proj, proj, proj, *tables)


def _retention_bwd(proj, do, states, tables, *, name):
    lp = proj.shape[0]
    nblk, cr = lp // BLK, RET_CHUNK
    nch = BLK // cr
    scale = RET_DK ** -0.5

    def body(q_ref, k_ref, v_ref, do_ref, st_ref, c_ref, s_ref, d_ref, z_ref, x_ref, g_ref, dqkv_ref, dstate):
        @pl.when(pl.program_id(0) == 0)
        def _():
            dstate[...] = jnp.zeros_like(dstate)

        def chunk(cc, carry):
            ci = nch - 1 - cc
            sl = pl.ds(pl.multiple_of(ci * cr, cr), cr)
            c, s = c_ref[sl, :], s_ref[sl, :]
            for h in range(RET_HEADS):
                hs = slice(128 * h, 128 * h + 128)
                dmat, zeta, xi = d_ref[h], z_ref[h], x_ref[h]
                q = _rope(q_ref[sl, hs], c, s)
                k = _rope(k_ref[sl, hs], c, s) * scale
                qb, kb, vb = q.astype(BF16), k.astype(BF16), v_ref[sl, hs].astype(BF16)
                kzb = (k * zeta).astype(BF16)
                dov = do_ref[sl, hs]
                dob, doxb = dov.astype(BF16), (dov * xi).astype(BF16)
                stb = st_ref[h, ci].astype(BF16)
                dsn = dstate[h]
                dsnb = dsn.astype(BF16)
                scb = (_dot_nt(qb, kb) * dmat).astype(BF16)
                dscb = (_dot_nt(dob, vb) * dmat).astype(BF16)
                dq = _dot(dscb, kb) + _dot_nt(doxb, stb)
                dk = _dot_tn(dscb, qb) + _dot_nt(vb, dsnb) * zeta
                dv = _dot_tn(scb, dob) + _dot(kzb, dsnb)
                dstate[h] = dsn * g_ref[h][0:1, :] + _dot_tn(qb, doxb)
                dqkv_ref[sl, 128 * h:128 * h + 128] = _rope_t(dq, c, s).astype(BF16)
                dqkv_ref[sl, 512 + 128 * h:640 + 128 * h] = _rope_t(dk * scale, c, s).astype(BF16)
                dqkv_ref[sl, 1024 + 128 * h:1152 + 128 * h] = dv.astype(BF16)
            return carry

        lax.fori_loop(0, nch, chunk, 0)

    col, tab, sq, hv, g8, st, out = _ret_specs(nblk, True)
    return pl.pallas_call(
        body, out_shape=jax.ShapeDtypeStruct((lp, P_RET), BF16), grid=(nblk,),
        in_specs=[col(C_RQ), col(C_RK), col(C_RV), out, st, tab, tab, sq, hv, hv, g8],
        out_specs=pl.BlockSpec((BLK, P_RET), lambda i: (nblk - 1 - i, 0)),
        scratch_shapes=[pltpu.VMEM((RET_HEADS, 128, 128), F32)],
        compiler_params=_cp("arbitrary"), name=name)(proj, proj, proj, do, states, *tables)


def _gla_tables():
    c = GLA_CHUNK
    tri = np.tril(np.ones((c, c), np.float32))
    ones_qv = np.kron(np.eye(GLA_HEADS, dtype=np.float32), np.ones((GLA_DK, GLA_DV), np.float32))
    return (jnp.asarray(tri, BF16), jnp.asarray(tri.T.copy(), BF16), jnp.asarray(ones_qv, BF16),
            jnp.asarray(ones_qv.T.copy(), BF16))


def _tri_sum(tri, x):
    hi = x.astype(BF16)
    lo = (x - hi.astype(F32)).astype(BF16)
    return _dot(tri, hi) + _dot(tri, lo)


def _head_masks(width, per):
    lane = lax.broadcasted_iota(jnp.int32, (1, width), 1)
    return [((lane >= per * h) & (lane < per * (h + 1))).astype(F32) for h in range(GLA_HEADS)]


def _stack_heads(x, masks):
    return jnp.concatenate([x * m for m in masks], axis=0)


def _gla_gate(ga, w2, b, ok, tri):
    z = _dot(ga.astype(BF16), w2) + b
    la = (jnp.minimum(z, 0.0) - jnp.log(1.0 + jnp.exp(-jnp.abs(z)))) * (1.0 / GLA_TAU)
    la = jnp.where(ok, la, 0.0)
    return z, _tri_sum(tri, la)


def _gla_rows(i_blk, ci, lp):
    c = GLA_CHUNK
    rows = i_blk * BLK + ci * c + lax.broadcasted_iota(jnp.int32, (c, 1), 0)
    return (rows >= PADF) & (rows < lp - BACK)


N_SUB = GLA_CHUNK // GLA_SUB - 1
N_SUB2 = GLA_SUB // GLA_SUB2 - 1


def _gla_masks():
    c, s1, s2 = GLA_CHUNK, GLA_SUB, GLA_SUB2
    sh1, sh2 = s1.bit_length() - 1, s2.bit_length() - 1
    r = lax.broadcasted_iota(jnp.int32, (c, GLA_QK), 0)
    blk, within = jnp.right_shift(r, sh1), jnp.bitwise_and(r, s1 - 1)
    grp = jnp.right_shift(within, sh2)
    rowm = [(blk == a).astype(F32) for a in range(1, N_SUB + 1)] + [(grp == b).astype(F32) for b in range(1, N_SUB2 + 1)]
    keym = ([(r < s1 * a).astype(F32) for a in range(1, N_SUB + 1)]
            + [(within < s2 * b).astype(F32) for b in range(1, N_SUB2 + 1)])
    rs = lax.broadcasted_iota(jnp.int32, (GLA_HEADS * c, c), 0)
    ts = lax.broadcasted_iota(jnp.int32, (GLA_HEADS * c, c), 1)
    same = (jnp.right_shift(jnp.bitwise_and(rs, c - 1), sh1) == jnp.right_shift(ts, sh1)).astype(F32)
    lag = [(jnp.bitwise_and(r, s2 - 1) >= j).astype(F32) for j in range(s2)]
    return rowm, keym, same, lag


def _gla_hats(qs, k, g, masks, hm_q):
    c, s1, s2 = GLA_CHUNK, GLA_SUB, GLA_SUB2
    rowm, keym, same, _ = masks
    refs = [g[s1 * a - 1:s1 * a, :] for a in range(1, N_SUB + 1)]
    for b in range(1, N_SUB2 + 1):
        refs.append(jnp.concatenate([jnp.broadcast_to(g[s1 * i + s2 * b - 1:s1 * i + s2 * b, :], (s1, GLA_QK))
                                     for i in range(c // s1)], axis=0))
    eqs = [jnp.exp(jnp.minimum(g - r, 0.0)) * m for r, m in zip(refs, rowm)]
    eks = [jnp.exp(jnp.minimum(r - g, 0.0)) * m for r, m in zip(refs, keym)]
    qhs, khs = [qs * e for e in eqs], [k * e for e in eks]
    qst = [_stack_heads(q, hm_q).astype(BF16) for q in qhs]
    khb = [x.astype(BF16) for x in khs]
    qa, qb = jnp.concatenate(qst[:N_SUB], axis=1), jnp.concatenate(qst[N_SUB:], axis=1)
    ka, kb = jnp.concatenate(khb[:N_SUB], axis=1), jnp.concatenate(khb[N_SUB:], axis=1)
    p = _dot_nt(qa, ka) + _dot_nt(qb, kb) * same
    return eqs, eks, qhs, khs, qa, qb, ka, kb, p


def _roll_rows(x, j):
    return x if j == 0 else pltpu.roll(x, j, 0)


def _gla(proj, w2p, b, tables, *, name):
    lp = proj.shape[0]
    nblk, c, s2 = lp // BLK, GLA_CHUNK, GLA_SUB2
    nch = BLK // c

    def body(q_ref, k_ref, v_ref, a_ref, w_ref, b_ref, tri_ref, ones_ref, o_ref, st_ref, gz_ref, state):
        i_blk = pl.program_id(0)

        @pl.when(i_blk == 0)
        def _():
            state[...] = jnp.zeros_like(state)
        hm_q = _head_masks(GLA_QK, GLA_DK)
        masks = _gla_masks()
        tri, ones_qv, w2, bias = tri_ref[...], ones_ref[...], w_ref[...], b_ref[...]

        def chunk(ci, carry):
            sl = pl.ds(pl.multiple_of(ci * c, c), c)
            ok = _gla_rows(i_blk, ci, lp)
            k, v = k_ref[sl, :], v_ref[sl, :]
            vb = v.astype(BF16)
            qs = q_ref[sl, :] * (GLA_DK ** -0.5)
            z, g = _gla_gate(a_ref[sl, :], w2, bias, ok, tri)
            gz_ref[sl, 0:GLA_QK] = g
            gz_ref[sl, GLA_QK:2 * GLA_QK] = z
            last = g[c - 1:c, :]
            st = state[...]
            st_ref[ci] = st
            qst = _stack_heads(qs * jnp.exp(g), hm_q).astype(BF16)
            oi = _dot_nt(qst, st.astype(BF16))
            o = jnp.concatenate([oi[c * h:c * h + c, :] for h in range(GLA_HEADS)], axis=1)
            ke = k * jnp.exp(last - g)
            f = _dot_tn(vb, ke.astype(BF16))
            upd = f[0:GLA_DV, :] * hm_q[0]
            for h in range(1, GLA_HEADS):
                upd = upd + f[GLA_DV * h:GLA_DV * (h + 1), :] * hm_q[h]
            state[...] = st * jnp.exp(last) + upd
            p = _gla_hats(qs, k, g, masks, hm_q)[-1]
            ob = _dot(p.astype(BF16), vb)
            o = o + jnp.concatenate([ob[c * h:c * h + c, GLA_DV * h:GLA_DV * (h + 1)] for h in range(GLA_HEADS)],
                                    axis=1)
            ws = []
            for j in range(s2):
                ej = jnp.exp(jnp.minimum(g - _roll_rows(g, j), 0.0))
                ws.append((qs * _roll_rows(k, j) * ej * masks[3][j]).astype(BF16))
            ball = _dot(jnp.concatenate(ws, axis=0), ones_qv)
            for j in range(s2):
                o = o + ball[c * j:c * j + c, :] * _roll_rows(v, j)
            o_ref[sl, :] = o
            return carry

        lax.fori_loop(0, nch, chunk, 0)

    tri, _, ones_qv, _ = tables
    full = lambda arr: pl.BlockSpec(arr.shape, lambda i: (0,) * arr.ndim)
    return pl.pallas_call(
        body,
        out_shape=(jax.ShapeDtypeStruct((lp, GLA_V), F32), jax.ShapeDtypeStruct((lp // c, GLA_DV, GLA_QK), F32),
                   jax.ShapeDtypeStruct((lp, 2 * GLA_QK), F32)),
        grid=(nblk,),
        in_specs=[pl.BlockSpec((BLK, GLA_QK), lambda i: (i, C_GQ // GLA_QK)),
                  pl.BlockSpec((BLK, GLA_QK), lambda i: (i, C_GK // GLA_QK)),
                  pl.BlockSpec((BLK, GLA_V), lambda i: (i, C_GV // GLA_V)),
                  pl.BlockSpec((BLK, 128), lambda i: (i, C_GA // 128)),
                  full(w2p), full(b), full(tri), full(ones_qv)],
        out_specs=(pl.BlockSpec((BLK, GLA_V), lambda i: (i, 0)),
                   pl.BlockSpec((nch, GLA_DV, GLA_QK), lambda i: (i, 0, 0)),
                   pl.BlockSpec((BLK, 2 * GLA_QK), lambda i: (i, 0))),
        scratch_shapes=[pltpu.VMEM((GLA_DV, GLA_QK), F32)],
        compiler_params=_cp("arbitrary"), name=name)(proj, proj, proj, proj, w2p, b, tri, ones_qv)


def _gla_bwd(proj, do, states, gz, w2p, tables, *, name):
    lp = proj.shape[0]
    nblk, c, s1, s2 = lp // BLK, GLA_CHUNK, GLA_SUB, GLA_SUB2
    nch = BLK // c

    def body(q_ref, k_ref, v_ref, a_ref, do_ref, st_ref, gz_ref, w_ref, trit_ref, ones_ref, onest_ref,
             dp_ref, dw_ref, db_ref, dstate, dqs_s, dk_s, dg_s, dv_s):
        i_blk = nblk - 1 - pl.program_id(0)

        @pl.when(pl.program_id(0) == 0)
        def _():
            dstate[...] = jnp.zeros_like(dstate)
            dw_ref[...] = jnp.zeros_like(dw_ref)
            db_ref[...] = jnp.zeros_like(db_ref)
        hm_q = _head_masks(GLA_QK, GLA_DK)
        hm_v = _head_masks(GLA_V, GLA_DV)
        masks = _gla_masks()
        trit, ones_qv, ones_vq = trit_ref[...], ones_ref[...], onest_ref[...]
        w2 = w_ref[...]
        rsum = lambda x: jnp.sum(x, axis=0, keepdims=True)

        def chunk(cc, carry):
            ci = nch - 1 - cc
            sl = pl.ds(pl.multiple_of(ci * c, c), c)
            ok = _gla_rows(i_blk, ci, lp)
            k, v, ga = k_ref[sl, :], v_ref[sl, :], a_ref[sl, :]
            vb = v.astype(BF16)
            qs = q_ref[sl, :] * (GLA_DK ** -0.5)
            g, z = gz_ref[sl, 0:GLA_QK], gz_ref[sl, GLA_QK:2 * GLA_QK]
            last = g[c - 1:c, :]
            elast = jnp.exp(last)
            eg = jnp.exp(g)
            ekl = jnp.exp(last - g)
            qe, ke = qs * eg, k * ekl
            dov = do_ref[sl, :]
            st = st_ref[ci]
            dsn = dstate[...]
            qst = _stack_heads(qe, hm_q).astype(BF16)
            dost = jnp.concatenate([dov[:, GLA_DV * h:GLA_DV * (h + 1)] for h in range(GLA_HEADS)], axis=0).astype(BF16)
            dqe_st = _dot(dost, st.astype(BF16))
            dqe = dqe_st[0:c, :] * hm_q[0]
            for h in range(1, GLA_HEADS):
                dqe = dqe + dqe_st[c * h:c * h + c, :] * hm_q[h]
            dstate[...] = _dot_tn(dost, qst) + dsn * elast
            dlast = rsum(dsn * st) * elast
            df = _stack_heads(dsn, hm_q).astype(BF16)
            dv_s[...] = _dot_nt(ke.astype(BF16), df)
            dke = _dot(vb, df)
            xk = dke * ke
            dqs_s[...] = dqe * eg
            dk_s[...] = dke * ekl
            dg_s[...] = dqe * qe - xk
            dlast = dlast + rsum(xk)
            eqs, eks, qhs, khs, qa, qb, ka, kb, p = _gla_hats(qs, k, g, masks, hm_q)
            dost_v = _stack_heads(dov, hm_v).astype(BF16)
            dp = _dot_nt(dost_v, vb)
            dv_s[...] += _dot_tn(p.astype(BF16), dost_v)
            dpa, dpb = dp.astype(BF16), (dp * masks[2]).astype(BF16)
            dq_all = (_dot(dpa, ka), _dot(dpb, kb))
            dk_all = (_dot_tn(dpa, qa), _dot_tn(dpb, qb))
            for t in range(N_SUB + N_SUB2):
                lvl, i = (0, t) if t < N_SUB else (1, t - N_SUB)
                cols = slice(GLA_QK * i, GLA_QK * (i + 1))
                dq_st = dq_all[lvl][:, cols]
                dqh = dq_st[0:c, :] * hm_q[0]
                for h in range(1, GLA_HEADS):
                    dqh = dqh + dq_st[c * h:c * h + c, :] * hm_q[h]
                dkh = dk_all[lvl][:, cols]
                xq, xkh = dqh * qhs[t], dkh * khs[t]
                dqs_s[...] += dqh * eqs[t]
                dk_s[...] += dkh * eks[t]
                dg_s[...] += xq - xkh
                back_ref = xkh - xq
                if lvl == 0:
                    row = s1 * (i + 1) - 1
                    dg_s[row:row + 1, :] += rsum(back_ref)
                else:
                    for blk in range(c // s1):
                        row = s1 * blk + s2 * (i + 1) - 1
                        dg_s[row:row + 1, :] += rsum(back_ref[s1 * blk:s1 * blk + s1, :])
            kes, qes, ws, dbs = [], [], [], []
            for j in range(s2):
                em = jnp.exp(jnp.minimum(g - _roll_rows(g, j), 0.0)) * masks[3][j]
                kes.append(_roll_rows(k, j) * em)
                qes.append(qs * em)
                ws.append((qs * kes[j]).astype(BF16))
                dbs.append((dov * _roll_rows(v, j)).astype(BF16))
            ball = _dot(jnp.concatenate(ws, axis=0), ones_qv)
            dwall = _dot(jnp.concatenate(dbs, axis=0), ones_vq)
            for j in range(s2):
                back = (lambda x: x) if j == 0 else (lambda x, j=j: pltpu.roll(x, c - j, 0))
                dw = dwall[c * j:c * j + c, :]
                dv_s[...] += back(ball[c * j:c * j + c, :] * dov)
                dqs_s[...] += dw * kes[j]
                dk_s[...] += back(dw * qes[j])
                x = dw * qs * kes[j]
                dg_s[...] += x - back(x)
            dg_s[c - 1:c, :] += dlast
            dla = jnp.where(ok, _tri_sum(trit, dg_s[...]), 0.0)
            dz = dla * (1.0 / GLA_TAU) / (1.0 + jnp.exp(z))
            dzb = dz.astype(BF16)
            dp_ref[sl, 0:256] = (dqs_s[...] * (GLA_DK ** -0.5)).astype(BF16)
            dp_ref[sl, 256:512] = dk_s[...].astype(BF16)
            dp_ref[sl, 512:1024] = dv_s[...].astype(BF16)
            dp_ref[sl, 1024:1152] = _dot_nt(dzb, w2).astype(BF16)
            dp_ref[sl, 1152:1280] = jnp.zeros((c, 128), BF16)
            dw_ref[...] += _dot_tn(ga.astype(BF16), dzb)
            db_ref[...] += rsum(dz)
            return carry

        lax.fori_loop(0, nch, chunk, 0)

    tri, trit, ones_qv, ones_vq = tables
    full = lambda arr: pl.BlockSpec(arr.shape, lambda i: (0,) * arr.ndim)
    rev = lambda i: nblk - 1 - i
    return pl.pallas_call(
        body,
        out_shape=(jax.ShapeDtypeStruct((lp, P_GLA), BF16),
                   jax.ShapeDtypeStruct((128, GLA_QK), F32), jax.ShapeDtypeStruct((1, GLA_QK), F32)),
        grid=(nblk,),
        in_specs=[pl.BlockSpec((BLK, GLA_QK), lambda i: (rev(i), C_GQ // GLA_QK)),
                  pl.BlockSpec((BLK, GLA_QK), lambda i: (rev(i), C_GK // GLA_QK)),
                  pl.BlockSpec((BLK, GLA_V), lambda i: (rev(i), C_GV // GLA_V)),
                  pl.BlockSpec((BLK, 128), lambda i: (rev(i), C_GA // 128)),
                  pl.BlockSpec((BLK, GLA_V), lambda i: (rev(i), 0)),
                  pl.BlockSpec((nch, GLA_DV, GLA_QK), lambda i: (rev(i), 0, 0)),
                  pl.BlockSpec((BLK, 2 * GLA_QK), lambda i: (rev(i), 0)),
                  full(w2p), full(trit), full(ones_qv), full(ones_vq)],
        out_specs=(pl.BlockSpec((BLK, P_GLA), lambda i: (rev(i), 0)),
                   pl.BlockSpec((128, GLA_QK), lambda i: (0, 0)),
                   pl.BlockSpec((1, GLA_QK), lambda i: (0, 0))),
        scratch_shapes=[pltpu.VMEM((GLA_DV, GLA_QK), F32), pltpu.VMEM((c, GLA_QK), F32),
                        pltpu.VMEM((c, GLA_QK), F32), pltpu.VMEM((c, GLA_QK), F32), pltpu.VMEM((c, GLA_V), F32)],
        compiler_params=_cp("arbitrary"), name=name)(proj, proj, proj, proj, do, states, gz, w2p, trit, ones_qv, ones_vq)


def _as2d(a):
    return a.reshape(-1, a.shape[-1])


def _ew_tile(r):
    return _tile(r, (512, 256, 128, 64, 32, 16, 8))


def _add2(a, b, *, out_dtype, name):
    a2, b2 = _as2d(a), _as2d(b)
    r, n = a2.shape
    tm = _ew_tile(r)

    def body(a_ref, b_ref, o_ref):
        o_ref[...] = (a_ref[...] + b_ref[...]).astype(o_ref.dtype)

    blk = pl.BlockSpec((tm, n), lambda i: (i, 0))
    return pl.pallas_call(body, out_shape=jax.ShapeDtypeStruct((r, n), out_dtype), grid=(r // tm,), in_specs=[blk, blk],
                          out_specs=blk, compiler_params=_cp("parallel"), name=name)(a2, b2).reshape(a.shape)


def _sum_slots(own, q, *, name):
    shape = own.shape
    q3 = q.reshape(3, -1, shape[-1])
    own2 = _as2d(own)
    r, n = own2.shape
    tm = _ew_tile(r)

    def body(own_ref, q_ref, o_ref):
        f = lambda i: q_ref[i].astype(F32)
        o_ref[...] = ((own_ref[...].astype(F32) + f(0)) + f(1)) + f(2)

    blk = pl.BlockSpec((tm, n), lambda i: (i, 0))
    return pl.pallas_call(
        body, out_shape=jax.ShapeDtypeStruct((r, n), F32), grid=(r // tm,),
        in_specs=[blk, pl.BlockSpec((3, tm, n), lambda i: (0, i, 0))], out_specs=blk,
        compiler_params=_cp("parallel"), name=name)(own2, q3).reshape(shape)


def _adamw(w, g, m, v, *, name):
    shape = w.shape
    w2, g2, m2, v2 = _as2d(w), _as2d(g), _as2d(m), _as2d(v)
    r, n = w2.shape
    tm = _ew_tile(r)

    def body(w_ref, g_ref, m_ref, v_ref, d_ref, mo_ref, vo_ref):
        d_ref[...], mo_ref[...], vo_ref[...] = _adam_math(w_ref[...], g_ref[...], m_ref[...], v_ref[...])

    blk = pl.BlockSpec((tm, n), lambda i: (i, 0))
    o = jax.ShapeDtypeStruct((r, n), F32)
    d, mo, vo = pl.pallas_call(body, out_shape=(o, o, o), grid=(r // tm,), in_specs=[blk] * 4, out_specs=(blk,) * 3,
                               compiler_params=_cp("parallel"), name=name)(w2, g2, m2, v2)
    return d.reshape(shape), mo.reshape(shape), vo.reshape(shape)


def _adam_math(w, gv, m, v):
    c1 = 1.0 - ADAM_B1 ** ADAM_STEP
    c2 = 1.0 - ADAM_B2 ** ADAM_STEP
    mn = ADAM_B1 * m + (1.0 - ADAM_B1) * gv
    vn = ADAM_B2 * v + (1.0 - ADAM_B2) * (gv * gv)
    return -ADAM_LR * ((mn / c1) / (jnp.sqrt(vn / c2) + ADAM_EPS) + ADAM_WD * w), mn, vn


def _adamw_halves(w, m, v, mine, theirs, c, *, name):
    depth, rows, n = w.shape
    r2 = rows // 2
    tm = next(t for t in range(min(r2, 256), 0, -8) if r2 % t == 0)
    steps = r2 // tm

    def body(c_ref, w_ref, m_ref, v_ref, *rest):
        halves, (g_ref, d_ref, mo_ref, vo_ref) = rest[:2 * depth], rest[2 * depth:]
        l, h = pl.program_id(0), pl.program_id(1)
        gv = None
        for k in range(depth):
            gk = jnp.where(h == c_ref[0], halves[2 * k][...], halves[2 * k + 1][...])
            gv = gk if gv is None else jnp.where(l == k, gk, gv)
        g_ref[...] = gv
        d_ref[...], mo_ref[...], vo_ref[...] = _adam_math(w_ref[...], gv, m_ref[...], v_ref[...])

    big = pl.BlockSpec((tm, n), lambda l, h, i, c_ref: ((2 * l + h) * steps + i, 0))
    half = lambda k: pl.BlockSpec((tm, n), lambda l, h, i, c_ref: (jnp.where(l == k, i, 0), 0))
    o = jax.ShapeDtypeStruct((depth * rows, n), F32)
    args = [a for k in range(depth) for a in (mine[k], theirs[k])]
    outs = pl.pallas_call(
        body, out_shape=(o, o, o, o),
        grid_spec=pltpu.PrefetchScalarGridSpec(
            num_scalar_prefetch=1, grid=(depth, 2, steps),
            in_specs=[big, big, big] + [half(k) for k in range(depth) for _ in range(2)], out_specs=(big,) * 4),
        compiler_params=_cp("arbitrary", "arbitrary", "arbitrary"), name=name)(
            jnp.reshape(c, (1,)).astype(jnp.int32), _as2d(w), _as2d(m), _as2d(v), *args)
    return [a.reshape(w.shape) for a in outs]


ANY = pl.BlockSpec(memory_space=pl.ANY)


def _place():
    return lax.axis_index("x"), lax.axis_index("y"), lax.axis_index("c")


def _other_chips(x, y):
    return [(1 - x, y), (x, 1 - y), (1 - x, 1 - y)]


def _remote(src, dst, ssem, rsem, dev):
    return pltpu.make_async_remote_copy(src_ref=src, dst_ref=dst, send_sem=ssem, recv_sem=rsem, device_id=dev,
                                        device_id_type=MESH)


def _allgather_chips(arrs, *, name):
    n = len(arrs)

    def body(*refs):
        ins, outs = refs[:n], refs[n:2 * n]
        s1, r1, s2, r2 = refs[2 * n:]
        x, y, c = _place()
        q = 2 * x + y
        chips = _other_chips(x, y)
        qs = [2 * cx + cy for cx, cy in chips]
        sib = (x, y, 1 - c)
        first, passed = [], []
        for k in range(n):
            for j, chip in enumerate(chips):
                first.append(_remote(ins[k].at[c], outs[k].at[c, q], s1.at[k, j], r1.at[k, j], (*chip, c)))
        for cp in first:
            cp.start()
        for k in range(n):
            for j, chip in enumerate(chips):
                land = outs[k].at[c, qs[j]]
                _remote(land, land, s1.at[k, j], r1.at[k, j], (*chip, c)).wait_recv()
                fw = _remote(land, land, s2.at[k, j], r2.at[k, j], sib)
                fw.start()
                passed.append(fw)
        for k in range(n):
            for j in range(3):
                land = outs[k].at[1 - c, qs[j]]
                _remote(land, land, s2.at[k, j], r2.at[k, j], sib).wait_recv()
        for cp in first + passed:
            cp.wait_send()

    sem = pltpu.SemaphoreType.DMA
    outs = pl.pallas_call(
        body, out_shape=tuple(jax.ShapeDtypeStruct((2, 4) + a.shape[1:], a.dtype) for a in arrs),
        in_specs=[ANY] * n, out_specs=(ANY,) * n,
        scratch_shapes=[sem((n, 3)), sem((n, 3)), sem((n, 3)), sem((n, 3))], name=name)(*arrs)
    chip = 2 * lax.axis_index("x") + lax.axis_index("y")
    return [lax.dynamic_update_slice_in_dim(o, a[:, None], chip, axis=1) for o, a in zip(outs, arrs)]


def _pair_exchange(arrs, *, name):
    n = len(arrs)

    def body(*refs):
        ins, outs = refs[:n], refs[n:2 * n]
        ssem, rsem = refs[2 * n:]
        x, y, c = _place()
        cps = [_remote(ins[k].at[:, 1 - c], outs[k], ssem.at[k], rsem.at[k], (x, y, 1 - c)) for k in range(n)]
        for cp in cps:
            cp.start()
        for cp in cps:
            cp.wait()

    sem = pltpu.SemaphoreType.DMA
    return pl.pallas_call(
        body, out_shape=tuple(jax.ShapeDtypeStruct((a.shape[0],) + a.shape[2:], a.dtype) for a in arrs),
        in_specs=[ANY] * n, out_specs=(ANY,) * n, scratch_shapes=[sem((n,)), sem((n,))], name=name)(*arrs)


def _pair_sum(mine, theirs, c, *, name):
    _, _, r, n = mine.shape
    tm = r if r <= 512 else _ew_tile(r)

    def body(c_ref, a_ref, b_ref, o_ref):
        o_ref[...] = (a_ref[...] + b_ref[...]).astype(BF16)

    blk = pl.BlockSpec((None, tm, n), lambda s, i, c_ref: (s, i, 0))
    return pl.pallas_call(
        body, out_shape=jax.ShapeDtypeStruct((4, r, n), BF16),
        grid_spec=pltpu.PrefetchScalarGridSpec(
            num_scalar_prefetch=1, grid=(4, r // tm),
            in_specs=[pl.BlockSpec((None, None, tm, n), lambda s, i, c_ref: (s, c_ref[0], i, 0)), blk], out_specs=blk),
        compiler_params=_cp("parallel", "parallel"), name=name)(jnp.reshape(c, (1,)).astype(jnp.int32), mine, theirs)


def _chip_copies(ins, outs, ssem, rsem, mode):
    x, y, c = _place()
    q = 2 * x + y
    sends, recvs = [], []
    for k in range(len(ins)):
        if mode == "pair":
            sends.append(_remote(ins[k].at[:, 1 - c], outs[k], ssem.at[k, 0], rsem.at[k, 0], (x, y, 1 - c)))
            recvs.append(sends[-1])
            continue
        for j, (cx, cy) in enumerate(_other_chips(x, y)):
            sem = (ssem.at[k, j], rsem.at[k, j], (cx, cy, c))
            if mode == "scatter":
                sends.append(_remote(ins[k].at[2 * cx + cy], outs[k].at[j], *sem))
                recvs.append(sends[-1])
            else:
                sends.append(_remote(ins[k].at[c], outs[k].at[2 * q + c], *sem))
                recvs.append(_remote(ins[k].at[c], outs[k].at[2 * (2 * cx + cy) + c], *sem))
    return sends, recvs


def _chip_wait(sends, recvs):
    for cp in sends:
        cp.wait_send()
    for cp in recvs:
        cp.wait_recv()


def _landing_shape(a, mode):
    if mode == "pair":
        return jax.ShapeDtypeStruct((a.shape[0],) + a.shape[2:], a.dtype)
    return jax.ShapeDtypeStruct(((3,) if mode == "scatter" else (8,)) + a.shape[1:], a.dtype)


def _chip_exchange(arrs, mode, *, name):
    n = len(arrs)

    def body(*refs):
        ins, outs = refs[:n], refs[n:2 * n]
        ssem, rsem = refs[2 * n:]
        sends, recvs = _chip_copies(ins, outs, ssem, rsem, mode)
        for cp in sends:
            cp.start()
        _chip_wait(sends, recvs)

    sem = pltpu.SemaphoreType.DMA
    return list(pl.pallas_call(
        body, out_shape=tuple(_landing_shape(a, mode) for a in arrs),
        in_specs=[ANY] * n, out_specs=(ANY,) * n, scratch_shapes=[sem((n, 3)), sem((n, 3))], name=name)(*arrs))


def _pair_fill(bufs, owns, *, name):
    n = len(bufs)

    def body(*refs):
        own, outs = refs[n:2 * n], refs[2 * n:3 * n]
        ssem, rsem = refs[3 * n:]
        x, y, c = _place()
        q = 2 * x + y
        sib = (x, y, 1 - c)
        sends, recvs = [], []
        for k in range(n):
            for j, (cx, cy) in enumerate(_other_chips(x, y)):
                mine, theirs = outs[k].at[2 * (2 * cx + cy) + c], outs[k].at[2 * (2 * cx + cy) + 1 - c]
                sends.append(_remote(mine, mine, ssem.at[k, j], rsem.at[k, j], sib))
                recvs.append(_remote(mine, theirs, ssem.at[k, j], rsem.at[k, j], sib))
            slots = outs[k].at[pl.ds(2 * q, 2)]
            sends.append(_remote(own[k], slots, ssem.at[k, 3], rsem.at[k, 3], sib))
            recvs.append(sends[-1])
        for cp in sends:
            cp.start()
        _chip_wait(sends, recvs)

    sem = pltpu.SemaphoreType.DMA
    return list(pl.pallas_call(
        body, out_shape=tuple(jax.ShapeDtypeStruct(b.shape, b.dtype) for b in bufs),
        in_specs=[ANY] * (2 * n), out_specs=(ANY,) * n, scratch_shapes=[sem((n, 4)), sem((n, 4))],
        input_output_aliases={k: k for k in range(n)}, name=name)(*bufs, *owns))


def _pair_swap(arrs, *, name):
    n = len(arrs)

    def body(*refs):
        ins, outs = refs[:n], refs[n:2 * n]
        ssem, rsem = refs[2 * n:]
        x, y, c = _place()
        cps = [_remote(ins[k], outs[k], ssem.at[k], rsem.at[k], (x, y, 1 - c)) for k in range(n)]
        for cp in cps:
            cp.start()
        for cp in cps:
            cp.wait()

    sem = pltpu.SemaphoreType.DMA
    return pl.pallas_call(
        body, out_shape=tuple(jax.ShapeDtypeStruct(a.shape, a.dtype) for a in arrs),
        in_specs=[ANY] * n, out_specs=(ANY,) * n, scratch_shapes=[sem((n,)), sem((n,))], name=name)(*arrs)


def _allreduce_small(slab, *, name):
    r, n = slab.shape

    def body(x_ref, o_ref, buf, ssem, rsem):
        x, y, c = _place()
        me = 4 * x + 2 * y + c
        buf[me] = x_ref[...]
        cps = []
        for rel in range(1, 8):
            bx, by, bc = (rel >> 2) & 1, (rel >> 1) & 1, rel & 1
            px, py, pc = (x + bx) % 2, (y + by) % 2, (c + bc) % 2
            cps.append((_remote(x_ref, buf.at[me], ssem.at[rel - 1], rsem.at[rel - 1], (px, py, pc)),
                        4 * px + 2 * py + pc, (px, py, pc)))
        for cp, _, _ in cps:
            cp.start()
        for rel, (cp, peer, dev) in enumerate(cps):
            cp.wait_send()
            _remote(x_ref, buf.at[peer], ssem.at[rel], rsem.at[rel], dev).wait_recv()
        acc = buf[0]
        for k in range(1, 8):
            acc = acc + buf[k]
        o_ref[...] = acc

    vm = pl.BlockSpec(memory_space=pltpu.VMEM)
    sem = pltpu.SemaphoreType.DMA
    return pl.pallas_call(
        body, out_shape=jax.ShapeDtypeStruct((r, n), F32), in_specs=[vm], out_specs=vm,
        scratch_shapes=[pltpu.VMEM((8, r, n), F32), sem((7,)), sem((7,))], name=name)(slab)


def _slab(arrs, row_mult):
    flat = jnp.concatenate([a.reshape(-1) for a in arrs])
    unit = 128 * row_mult
    total = -(-flat.size // unit) * unit
    return jnp.pad(flat, (0, total - flat.size)).reshape(-1, 128)


def _unslab(slab, shapes):
    flat = slab.reshape(-1)
    out, off = [], 0
    for s in shapes:
        size = int(np.prod(s))
        out.append(flat[off:off + size].reshape(s))
        off += size
    return out


def _cols_from_chips(a):
    return jnp.transpose(a, (1, 0, 2)).reshape(a.shape[1], -1)


def _cols_to_chips(a, parts):
    r = a.shape[0]
    return jnp.transpose(a.reshape(r, parts, -1), (1, 0, 2))


BIG = ("w_in", "w_out", "up", "down")
GATHER_RIDES = {("proj", 0): (("w_out", 0), ("up", 0)), ("mix_out", 0): (("down", 0),),
                ("ffn_fwd", 0): (("w_in", 1), ("w_out", 1), ("up", 1), ("down", 1))}
REDUCE_RIDES = {("ffn_up_a_dw", 0): (("up",), 1), ("ffn_down_dw", 0): (("w_in", "w_out"), 1),
                ("mix_out_dx", 0): (("down",), 1),
                ("proj_dx", 0): (("up",), 0), ("proj_dw_0", 0): (("down",), 0), ("proj_dw_1", 0): (("w_out",), 0)}


class _LocalWeights:
    def __init__(self, meta, win, wout, up_a, up_g, down, w2p, cw):
        self._meta, self._w = meta, {"win": win, "wout": wout, "up_a": up_a, "up_g": up_g, "down": down, "w2p": w2p,
                                     "cw": cw}

    def meta(self):
        return self._meta

    def get(self, kind, l):
        return self._w[kind][l]

    def mm(self, site, l, a, b, fn=None, **kw):
        return (fn or _mm)(a, b, name=site, **kw)

    def ffn_fwd(self, l, h, m, w_post, w_next, wa, wg, ba, bg, after=None):
        return _ffn_fwd(h, m, w_post, w_next, self.get("up_a", l), self.get("up_g", l), wa, wg, ba, bg,
                        self.get("down", l), name="ffn_fwd", after=after)[0]

    def merge_mix(self, l, o_ret, o_gla, proj, w_ret, w_gla):
        return _merge_mix_out(o_ret, o_gla, proj, w_ret, w_gla, self.get("wout", l), name="mix_out")[0]

    def merge_bwd(self, l, dm, o_ret, o_gla, proj, w_ret, w_gla):
        return _merge_bwd(dm, self.get("wout", l), o_ret, o_gla, proj, w_ret, w_gla, name="mix_out_dx")[0]

    def grads_done(self, l, g, kinds):
        pass


class _ChipWeights:
    def __init__(self, w_in, w_out, ffn_up, ffn_down, meta_tokens, gla_gate_w2, ffn_conv_w):
        self.x, self.y, self.c = _place()
        self.q = 2 * self.x + self.y
        halves = lambda a: a.astype(BF16).reshape(2, a.shape[0] // 2, a.shape[1])
        self.own = {(k, l): halves(a[l]) for k, a in zip(BIG, (w_in, w_out, ffn_up, ffn_down)) for l in range(DEPTH)}
        self.landed, self.swapped, self.full, self.n_swaps = {}, {}, {}, 0
        self.sh_shapes = [meta_tokens.shape, gla_gate_w2.shape, ffn_conv_w.shape]
        self.own["small", 0] = _slab([meta_tokens, gla_gate_w2, ffn_conv_w], 16).reshape(2, -1, 128)
        first = [("w_in", 0), ("small", 0)]
        for key, arr in zip(first, _chip_exchange([self.own[k] for k in first], "bcast", name="gather_first")):
            self.landed[key] = arr
        sh = self._whole("small", 0).reshape(4, -1, 128)
        parts = [_unslab(sh[k], self.sh_shapes) for k in range(4)]
        self._meta = jnp.concatenate([p[0] for p in parts], axis=-1)
        self.w2 = jnp.concatenate([p[1] for p in parts], axis=-1)
        self.cw = jnp.concatenate([p[2] for p in parts], axis=-1)
        self.partial, self.slots = {}, {}

    def _whole(self, kind, l):
        if (kind, l) not in self.full:
            keys = [k for k in self.landed if k not in self.full]
            got = _pair_fill([self.landed[k] for k in keys], [self.own[k] for k in keys],
                             name=f"gather_fill_{self.n_swaps}")
            self.n_swaps += 1
            for k, buf in zip(keys, got):
                self.full[k] = buf.reshape(4, 2 * buf.shape[1], buf.shape[2])
        return self.full[kind, l]

    def meta(self):
        return self._meta

    def get(self, kind, l):
        if kind == "win":
            return _to_kernel_cols(_cols_from_chips(self._whole("w_in", l)))
        if kind == "wout":
            return self._whole("w_out", l).reshape(D_MODEL, D_MODEL)
        if kind == "up_a":
            return _cols_from_chips(self._whole("up", l)[0:2])
        if kind == "up_g":
            return _cols_from_chips(self._whole("up", l)[2:4])
        if kind == "down":
            return self._whole("down", l).reshape(D_FF, D_MODEL)
        if kind == "w2p":
            return jnp.pad(self.w2[l], ((0, 128 - GLA_RANK), (0, 0))).astype(BF16)
        return self.cw[l]

    def mm(self, site, l, a, b, fn=None, **kw):
        fn = fn or _mm
        if (site, l) in GATHER_RIDES:
            keys = GATHER_RIDES[site, l]
            out, got = fn(a, b, name=site, carry=([self.own[k] for k in keys], "bcast"), **kw)
            self.landed.update(zip(keys, got))
            return out
        if (site, l) in REDUCE_RIDES:
            kinds, gl = REDUCE_RIDES[site, l]
            keys = [(k, gl) for k in kinds]
            if all(k in self.partial and k not in self.slots for k in keys):
                out, got = fn(a, b, name=site, carry=([self.partial[k] for k in keys], "scatter"), **kw)
                self.slots.update(zip(keys, got))
                return out
        return fn(a, b, name=site, **kw)

    def merge_bwd(self, l, dm, o_ret, o_gla, proj, w_ret, w_gla):
        carry, keys = None, []
        if ("mix_out_dx", l) in REDUCE_RIDES:
            kinds, gl = REDUCE_RIDES["mix_out_dx", l]
            keys = [(k, gl) for k in kinds]
            if all(k in self.partial and k not in self.slots for k in keys):
                carry = ([self.partial[k] for k in keys], "scatter")
        pend = getattr(self, "pending", None)
        if carry is None and pend is not None:
            carry = ([pend[1][k] for k in pend[0]], "pair")
        outs, got = _merge_bwd(dm, self.get("wout", l), o_ret, o_gla, proj, w_ret, w_gla, name="mix_out_dx",
                               carry=carry)
        if carry is not None and carry[1] == "pair":
            self.pending = None
            for k, theirs in zip(pend[0], got):
                self.partial[k, l] = _pair_sum(pend[1][k], theirs, self.c, name=f"pair_sum_{k}_{l}")
        elif carry is not None:
            self.slots.update(zip(keys, got))
        return outs

    def merge_mix(self, l, o_ret, o_gla, proj, w_ret, w_gla):
        keys = GATHER_RIDES.get(("mix_out", l), ())
        outs, got = _merge_mix_out(o_ret, o_gla, proj, w_ret, w_gla, self.get("wout", l), name="mix_out",
                                   carry=([self.own[k] for k in keys], "bcast") if keys else None)
        self.landed.update(zip(keys, got))
        return outs

    def ffn_fwd(self, l, h, m, w_post, w_next, wa, wg, ba, bg, after=None):
        keys = GATHER_RIDES.get(("ffn_fwd", l), ())
        outs, got = _ffn_fwd(h, m, w_post, w_next, self.get("up_a", l), self.get("up_g", l), wa, wg, ba, bg,
                             self.get("down", l), name="ffn_fwd", after=after,
                             carry=([self.own[k] for k in keys], "bcast") if keys else None)
        self.landed.update(zip(keys, got))
        return outs

    def grads_done(self, l, g, kinds):
        split = lambda a: a.reshape(4, 2, a.shape[-2] // 2, a.shape[-1]) if a.ndim == 3 else \
            a.reshape(4, 2, a.shape[0] // 8, a.shape[1])
        src = {"w_in": lambda: g["w_in"][l], "w_out": lambda: g["w_out"][l],
               "up": lambda: g["up"][l], "down": lambda: g["down"][l]}
        big = {k: split(src[k]()) for k in kinds}
        if l == DEPTH - 1 and "w_out" in kinds:
            self.pending = (kinds, big)
            return
        from_sib = _pair_exchange([big[k] for k in kinds], name=f"grads_pair_exchange_{l}_{kinds[0]}")
        for k, theirs in zip(kinds, from_sib):
            self.partial[k, l] = _pair_sum(big[k], theirs, self.c, name=f"pair_sum_{k}_{l}")

    def reduce(self):
        keys = [(k, l) for l in range(DEPTH) for k in BIG]
        late = [k for k in keys if k not in self.slots]
        self.slots.update(zip(late, _chip_exchange([self.partial[k] for k in late], "scatter",
                                                   name="grads_chip_exchange")))
        half = {}
        for k in keys:
            own = lax.dynamic_index_in_dim(self.partial[k], self.q, 0, keepdims=False)
            half[k] = _sum_slots(own, self.slots[k], name=f"chip_sum_{k[0]}_{k[1]}")
        other = dict(zip(keys, _pair_swap([half[k] for k in keys], name="grads_pair_swap")))
        return [([half[k, l] for l in range(DEPTH)], [other[k, l] for l in range(DEPTH)]) for k in BIG]


def _local_step(x_rows, target_rows, wts, pre_mix_norm, gla_gate_b, ret_norm_w, gla_norm_w, post_mix_norm,
                pre_ffn_norm, ffn_conv_b, post_ffn_norm):
    d = D_MODEL
    lp = x_rows.shape[0] + FRONT + BACK
    row = lambda a, l: a[l][None, :]
    rtab = _ret_tables(lp)
    gtab = _gla_tables()
    h0 = jnp.concatenate([jnp.zeros((PADF, d), F32), wts.meta(), x_rows, jnp.zeros((BACK, d), F32)], axis=0)
    target = jnp.pad(target_rows, ((FRONT, BACK), (0, 0)))

    saved = []
    h = h0
    _, hn = _resid_norm(h0, None, None, row(pre_mix_norm, 0), name="norm_in")
    loss_local = dy = None
    for l in range(DEPTH):
        s = {"h_in": h, "hn": hn}
        s["proj"] = wts.mm("proj", l, hn, wts.get("win", l))
        s["o_ret"], s["st_ret"] = _retention(s["proj"], rtab, name="retention")
        s["o_gla"], s["st_gla"], s["gz"] = _gla(s["proj"], wts.get("w2p", l), row(gla_gate_b, l), gtab, name="gla")
        s["merged"], s["m"] = wts.merge_mix(l, s["o_ret"], s["o_gla"], s["proj"], row(ret_norm_w, l),
                                            row(gla_norm_w, l))
        cw_a, cw_g = wts.get("cw", l)[:, :D_FF], wts.get("cw", l)[:, D_FF:]
        cb_a, cb_g = ffn_conv_b[l][None, :D_FF], ffn_conv_b[l][None, D_FF:]
        s["conv"] = (cw_a, cw_g, cb_a, cb_g)
        after = (row(post_ffn_norm, l), row(pre_mix_norm, l + 1)) if l + 1 < DEPTH else None
        outs = wts.ffn_fwd(l, h, s["m"], row(post_mix_norm, l), row(pre_ffn_norm, l), cw_a, cw_g, cb_a, cb_g,
                           after=after)
        s["h_mid"], s["hn2"], s["ua"], s["ug"], s["act"], s["f"] = outs[:6]
        if l + 1 < DEPTH:
            h, hn = outs[6], outs[7]
        else:
            loss_local, dy, df_last, dw_last = _loss_head(s["h_mid"], s["f"], row(post_ffn_norm, l), target,
                                                          name="loss_head")
        saved.append(s)

    g = {k: [None] * DEPTH for k in ("pre_mix", "w_in", "w2", "gb", "ret_n", "gla_n", "w_out", "post_mix", "pre_ffn",
                                     "up", "cw", "cb", "down", "post_ffn")}
    dh_out, dhn_next = dy, None
    for l in reversed(range(DEPTH)):
        s = saved[l]
        cw_a, cw_g, cb_a, cb_g = s["conv"]
        if l + 1 < DEPTH:
            dh, df, g["pre_mix"][l + 1], g["post_ffn"][l] = _resid_norm_bwd(
                dh_out, dhn_next, saved[l + 1]["h_in"], s["f"], row(pre_mix_norm, l + 1), row(post_ffn_norm, l),
                name="resid_ffn_bwd")
        else:
            dh, df, g["post_ffn"][l] = dh_out, df_last, dw_last
        g["down"][l] = wts.mm("ffn_down_dw", l, s["act"], df, fn=_mm_tn, tn=512)
        du_a, du_g, dcw_a, dcw_g, dcb_a, dcb_g, dhn2 = _conv_act_bwd(
            s["ua"], s["ug"], df, wts.get("down", l), cw_a, cw_g, cb_a, cb_g, wts.get("up_a", l), wts.get("up_g", l),
            name="conv_act_bwd")
        g["cw"][l] = jnp.concatenate([dcw_a, dcw_g], axis=1)
        g["cb"][l] = jnp.concatenate([dcb_a, dcb_g], axis=1)[0]
        half_up = wts.mm("ffn_up_a_dw", l, s["hn2"], du_a, fn=_mm_tn, tn=D_FF // 2, blocks=(4, 0))
        g["up"][l] = _mm_tn(s["hn2"], du_g, tn=D_FF // 2, blocks=(4, 2), into=half_up, name="ffn_up_g_dw")
        dh, dm, g["pre_ffn"][l], g["post_mix"][l] = _resid_norm_bwd(
            dh, dhn2, s["h_mid"], s["m"], row(pre_ffn_norm, l), row(post_mix_norm, l), name="resid_mix_bwd")
        g["w_out"][l] = _mm_tn(s["merged"], dm, name="mix_out_dw")
        wts.grads_done(l, g, ("w_out", "up", "down"))
        do_ret, do_gla, d_gate, g["ret_n"][l], g["gla_n"][l] = wts.merge_bwd(
            l, dm, s["o_ret"], s["o_gla"], s["proj"], row(ret_norm_w, l), row(gla_norm_w, l))
        d_ret = _retention_bwd(s["proj"], do_ret, s["st_ret"], rtab, name="retention_bwd")
        d_gla, dw2, dgb = _gla_bwd(s["proj"], do_gla, s["st_gla"], s["gz"], wts.get("w2p", l), gtab, name="gla_bwd")
        g["w2"][l], g["gb"][l] = dw2[:GLA_RANK], dgb[0]
        pieces = (d_ret, d_gate, d_gla)
        g["w_in"][l] = _to_reference_chips(*[wts.mm(f"proj_dw_{i}", l, s["hn"], p, fn=_mm_tn)
                                             for i, p in enumerate(pieces)])
        win = wts.get("win", l)
        dhn_next = wts.mm("proj_dx", l, pieces, [win[:, 0:P_RET], win[:, P_RET:P_RET + P_GATE], win[:, P_RET + P_GATE:]],
                          fn=_mm_nt_sum)
        dh_out = dh
        wts.grads_done(l, g, ("w_in",))
    dh0, _, g["pre_mix"][0], _ = _resid_norm_bwd(dh_out, dhn_next, h0, None, row(pre_mix_norm, 0), None,
                                                 name="norm_in_bwd")
    return loss_local, dh0, g


def kernel(x, meta_tokens, pre_mix_norm, w_in, gla_gate_w2, gla_gate_b, ret_norm_w, gla_norm_w, w_out, post_mix_norm, pre_ffn_norm, ffn_up, ffn_conv_w, ffn_conv_b, ffn_down, post_ffn_norm, loss_target, m_meta_tokens, m_pre_mix_norm, m_w_in, m_gla_gate_w2, m_gla_gate_b, m_ret_norm_w, m_gla_norm_w, m_w_out, m_post_mix_norm, m_pre_ffn_norm, m_ffn_up, m_ffn_conv_w, m_ffn_conv_b, m_ffn_down, m_post_ffn_norm, v_meta_tokens, v_pre_mix_norm, v_w_in, v_gla_gate_w2, v_gla_gate_b, v_ret_norm_w, v_gla_norm_w, v_w_out, v_post_mix_norm, v_pre_ffn_norm, v_ffn_up, v_ffn_conv_w, v_ffn_conv_b, v_ffn_down, v_post_ffn_norm):
    xi, yi, ci = _place()
    chip = 2 * xi + yi
    seq = x.shape[1]
    d = D_MODEL
    wts = _ChipWeights(w_in, w_out, ffn_up, ffn_down, meta_tokens, gla_gate_w2, ffn_conv_w)
    loss_local, dh0, g = _local_step(x[0], loss_target[0], wts, pre_mix_norm, gla_gate_b, ret_norm_w, gla_norm_w,
                                     post_mix_norm, pre_ffn_norm, ffn_conv_b, post_ffn_norm)
    grad_x = dh0[FRONT:FRONT + seq][None]
    names = ("w_in", "w_out", "ffn_up", "ffn_down")
    big_halves = wts.reduce()

    small_full = [dh0[PADF:FRONT], jnp.stack(g["pre_mix"])[:, 0], jnp.stack(g["w2"]), jnp.stack(g["gb"]),
                  jnp.stack(g["ret_n"])[:, 0], jnp.stack(g["gla_n"])[:, 0], jnp.stack(g["post_mix"])[:, 0],
                  jnp.stack(g["pre_ffn"])[:, 0], jnp.stack(g["cw"]), jnp.stack(g["cb"]),
                  jnp.stack(g["post_ffn"])[:, 0]]
    small_sum = _unslab(_allreduce_small(_slab(small_full, 8), name="small_allreduce"), [a.shape for a in small_full])
    (g_meta, g_pre_mix, g_w2, g_gb, g_ret_n, g_gla_n, g_post_mix, g_pre_ffn, g_cw, g_cb, g_post_ffn) = small_sum
    g_meta = lax.dynamic_slice_in_dim(g_meta, chip * 256, 256, axis=1)
    g_w2 = lax.dynamic_slice_in_dim(g_w2, chip * 64, 64, axis=2)
    g_cw = lax.dynamic_slice_in_dim(g_cw, chip * 1408, 1408, axis=2)

    grads = [g_meta, g_pre_mix, None, g_w2, g_gb, g_ret_n, g_gla_n, None, g_post_mix, g_pre_ffn, None,
             g_cw, g_cb, None, g_post_ffn]
    ws = [meta_tokens, pre_mix_norm, w_in, gla_gate_w2, gla_gate_b, ret_norm_w, gla_norm_w, w_out, post_mix_norm,
          pre_ffn_norm, ffn_up, ffn_conv_w, ffn_conv_b, ffn_down, post_ffn_norm]
    ms = [m_meta_tokens, m_pre_mix_norm, m_w_in, m_gla_gate_w2, m_gla_gate_b, m_ret_norm_w, m_gla_norm_w, m_w_out,
          m_post_mix_norm, m_pre_ffn_norm, m_ffn_up, m_ffn_conv_w, m_ffn_conv_b, m_ffn_down, m_post_ffn_norm]
    vs = [v_meta_tokens, v_pre_mix_norm, v_w_in, v_gla_gate_w2, v_gla_gate_b, v_ret_norm_w, v_gla_norm_w, v_w_out,
          v_post_mix_norm, v_pre_ffn_norm, v_ffn_up, v_ffn_conv_w, v_ffn_conv_b, v_ffn_down, v_post_ffn_norm]
    big_idx = (2, 7, 10, 13)
    deltas, new_m, new_v = [None] * 15, [None] * 15, [None] * 15
    for i, nm, (mine, theirs) in zip(big_idx, names, big_halves):
        grads[i], deltas[i], new_m[i], new_v[i] = _adamw_halves(ws[i], ms[i], vs[i], mine, theirs, ci,
                                                                name=f"adamw_{nm}")
    small_idx = [i for i in range(15) if i not in big_idx]
    shapes = [ws[i].shape for i in small_idx]
    sd, sm, sv = _adamw(_slab([ws[i] for i in small_idx], 8), _slab([grads[i] for i in small_idx], 8),
                        _slab([ms[i] for i in small_idx], 8), _slab([vs[i] for i in small_idx], 8), name="adamw_small")
    for i, a, b, c_ in zip(small_idx, _unslab(sd, shapes), _unslab(sm, shapes), _unslab(sv, shapes)):
        deltas[i], new_m[i], new_v[i] = a, b, c_

    loss = lax.psum(loss_local, ("x", "y", "c"))
    return (loss, grad_x, *grads, *deltas, *new_m, *new_v)
```
